```python
import jax, jax.numpy as jnp
from jax import lax
import numpy as np

D_MODEL = 1024
BATCH = 8
SEQ = 2048
DEPTH = 2

N_MIXERS = 2
RMS_EPS = 1e-6
LN_EPS = 1e-5
CHUNK = 128
A_WIDTH = 2 * D_MODEL
A_GROUPS = 8
A_GROUP_DIM = A_WIDTH // A_GROUPS
B_WIDTH = 3 * D_MODEL // 2
B_HEADS = 12
B_HEAD_DIM = B_WIDTH // B_HEADS
CONV_WIDTH = 4
RG_C = 8.0

N_A_LAYERS = (DEPTH + 1) // 2
N_B_LAYERS = DEPTH // 2

kernel_name = "hybrid_sgu_rglru_trunk"


def rms_norm(x, w):
    xf = x.astype(jnp.float32)
    y = xf * lax.rsqrt(jnp.mean(xf * xf, axis=-1, keepdims=True) + RMS_EPS)
    return (y * w.astype(jnp.float32)).astype(x.dtype)


def layer_norm(x, w, b):
    xf = x.astype(jnp.float32)
    mu = jnp.mean(xf, axis=-1, keepdims=True)
    var = jnp.mean(jnp.square(xf - mu), axis=-1, keepdims=True)
    y = (xf - mu) * lax.rsqrt(var + LN_EPS)
    return (y * w.astype(jnp.float32) + b.astype(jnp.float32)).astype(x.dtype)


def chunked_sgu_mixer(h, w_in, ln_w, ln_b, w_s, b_s, w_out):
    B, S, _ = h.shape
    z = h @ w_in
    u, v, g = jnp.split(z, 3, axis=-1)
    u = jax.nn.gelu(u)
    v = layer_norm(jax.nn.gelu(v), ln_w, ln_b)
    v = v.reshape(B, S // CHUNK, CHUNK, A_GROUPS, A_GROUP_DIM)
    causal = jnp.tril(jnp.ones((CHUNK, CHUNK), dtype=w_s.dtype))
    w_causal = w_s * causal[None]
    s = jnp.einsum('gts,bnsgc->bntgc', w_causal, v) + b_s.T[None, None, :, :, None]
    y = u * s.reshape(B, S, A_WIDTH) * jax.nn.silu(g)
    return y @ w_out


def _linear_combine(left, right):
    a_l, b_l = left
    a_r, b_r = right
    return a_l * a_r, a_r * b_l + b_r


def rglru_mixer(h, w_in, conv_w, conv_b, gate_a_w, gate_a_b, gate_x_w, gate_x_b, lam, w_out):
    B, S, _ = h.shape
    xb, g = jnp.split(h @ w_in, 2, axis=-1)
    xp = jnp.pad(xb, ((0, 0), (CONV_WIDTH - 1, 0), (0, 0)))
    xc = conv_b + conv_w[CONV_WIDTH - 1] * xp[:, CONV_WIDTH - 1:CONV_WIDTH - 1 + S]
    for k in range(CONV_WIDTH - 1):
        xc = xc + conv_w[k] * xp[:, k:k + S]
    xh = xc.reshape(B, S, B_HEADS, B_HEAD_DIM)
    r = jax.nn.sigmoid(jnp.einsum('bshi,hij->bshj', xh, gate_a_w).reshape(B, S, B_WIDTH) + gate_a_b)
    i = jax.nn.sigmoid(jnp.einsum('bshi,hij->bshj', xh, gate_x_w).reshape(B, S, B_WIDTH) + gate_x_b)
    log_a = -RG_C * r.astype(jnp.float32) * jax.nn.softplus(-lam.astype(jnp.float32))
    a = jnp.exp(log_a)
    mult = jnp.sqrt(-jnp.expm1(2.0 * log_a))
    bterm = mult * (i * xc).astype(jnp.float32)
    _, hseq = lax.associative_scan(_linear_combine, (a, bterm), axis=1)
    y = hseq.astype(h.dtype) * jax.nn.silu(g)
    return y @ w_out


def _fwd_setup_inputs(seed: int = 0) -> dict:
    key = jax.random.key(seed)
    ks = jax.random.split(key, 20)
    f32 = jnp.float32
    x = jax.random.normal(ks[0], (BATCH, SEQ, D_MODEL), f32)
    norm_w = 1.0 + 0.05 * jax.random.normal(ks[1], (DEPTH, D_MODEL), f32)
    a_w_in = jax.random.normal(ks[2], (N_A_LAYERS, D_MODEL, 3 * A_WIDTH), f32) * D_MODEL ** -0.5
    a_ln_w = 1.0 + 0.05 * jax.random.normal(ks[3], (N_A_LAYERS, A_WIDTH), f32)
    a_ln_b = 0.02 * jax.random.normal(ks[4], (N_A_LAYERS, A_WIDTH), f32)
    a_w_s = jax.random.normal(ks[5], (N_A_LAYERS, A_GROUPS, CHUNK, CHUNK), f32) * CHUNK ** -0.5
    a_b_s = 1.0 + 0.05 * jax.random.normal(ks[6], (N_A_LAYERS, A_GROUPS, CHUNK), f32)
    a_w_out = jax.random.normal(ks[7], (N_A_LAYERS, A_WIDTH, D_MODEL), f32) * A_WIDTH ** -0.5
    b_w_in = jax.random.normal(ks[8], (N_B_LAYERS, D_MODEL, 2 * B_WIDTH), f32) * D_MODEL ** -0.5
    b_conv_w = jax.random.normal(ks[9], (N_B_LAYERS, CONV_WIDTH, B_WIDTH), f32) * CONV_WIDTH ** -0.5
    b_conv_b = 0.02 * jax.random.normal(ks[10], (N_B_LAYERS, B_WIDTH), f32)
    b_gate_a_w = jax.random.normal(ks[11], (N_B_LAYERS, B_HEADS, B_HEAD_DIM, B_HEAD_DIM), f32) * B_HEAD_DIM ** -0.5
    b_gate_a_b = 0.02 * jax.random.normal(ks[12], (N_B_LAYERS, B_WIDTH), f32)
    b_gate_x_w = jax.random.normal(ks[13], (N_B_LAYERS, B_HEADS, B_HEAD_DIM, B_HEAD_DIM), f32) * B_HEAD_DIM ** -0.5
    b_gate_x_b = 0.02 * jax.random.normal(ks[14], (N_B_LAYERS, B_WIDTH), f32)
    a_c = jax.random.uniform(ks[15], (N_B_LAYERS, B_WIDTH), f32, minval=0.9, maxval=0.999)
    a0 = a_c ** (1.0 / RG_C)
    b_lambda = jnp.log(a0) - jnp.log1p(-a0)
    b_w_out = jax.random.normal(ks[16], (N_B_LAYERS, B_WIDTH, D_MODEL), f32) * B_WIDTH ** -0.5
    norm_f_w = 1.0 + 0.05 * jax.random.normal(ks[17], (D_MODEL,), f32)
    return {
        "x": x, "norm_w": norm_w,
        "a_w_in": a_w_in, "a_ln_w": a_ln_w, "a_ln_b": a_ln_b,
        "a_w_s": a_w_s, "a_b_s": a_b_s, "a_w_out": a_w_out,
        "b_w_in": b_w_in, "b_conv_w": b_conv_w, "b_conv_b": b_conv_b,
        "b_gate_a_w": b_gate_a_w, "b_gate_a_b": b_gate_a_b,
        "b_gate_x_w": b_gate_x_w, "b_gate_x_b": b_gate_x_b,
        "b_lambda": b_lambda, "b_w_out": b_w_out,
        "norm_f_w": norm_f_w,
    }


def _fwd_reference(x, norm_w, a_w_in, a_ln_w, a_ln_b, a_w_s, a_b_s, a_w_out,
              b_w_in, b_conv_w, b_conv_b, b_gate_a_w, b_gate_a_b,
              b_gate_x_w, b_gate_x_b, b_lambda, b_w_out, norm_f_w):
    for layer in range(DEPTH):
        h = rms_norm(x, norm_w[layer])
        j = layer // N_MIXERS
        if layer % N_MIXERS == 0:
            y = chunked_sgu_mixer(h, a_w_in[j], a_ln_w[j], a_ln_b[j], a_w_s[j], a_b_s[j], a_w_out[j])
        else:
            y = rglru_mixer(h, b_w_in[j], b_conv_w[j], b_conv_b[j], b_gate_a_w[j], b_gate_a_b[j],
                            b_gate_x_w[j], b_gate_x_b[j], b_lambda[j], b_w_out[j])
        x = x + y
    return rms_norm(x, norm_f_w)


import jax as _jax
import jax.numpy as _jnp

TWIN_FORMAT = 'train_step'
FWD_PARAMS = ['x', 'norm_w', 'a_w_in', 'a_ln_w', 'a_ln_b', 'a_w_s', 'a_b_s', 'a_w_out', 'b_w_in', 'b_conv_w', 'b_conv_b', 'b_gate_a_w', 'b_gate_a_b', 'b_gate_x_w', 'b_gate_x_b', 'b_lambda', 'b_w_out', 'norm_f_w']
TWIN_WEIGHTS = ['norm_w', 'a_w_in', 'a_ln_w', 'a_ln_b', 'a_w_s', 'a_b_s', 'a_w_out', 'b_w_in', 'b_conv_w', 'b_conv_b', 'b_gate_a_w', 'b_gate_a_b', 'b_gate_x_w', 'b_gate_x_b', 'b_lambda', 'b_w_out', 'norm_f_w']
TWIN_DIFF_INPUT = 'x'
TWIN_INPUTS = ['x', 'norm_w', 'a_w_in', 'a_ln_w', 'a_ln_b', 'a_w_s', 'a_b_s', 'a_w_out', 'b_w_in', 'b_conv_w', 'b_conv_b', 'b_gate_a_w', 'b_gate_a_b', 'b_gate_x_w', 'b_gate_x_b', 'b_lambda', 'b_w_out', 'norm_f_w', 'loss_target', 'm_norm_w', 'm_a_w_in', 'm_a_ln_w', 'm_a_ln_b', 'm_a_w_s', 'm_a_b_s', 'm_a_w_out', 'm_b_w_in', 'm_b_conv_w', 'm_b_conv_b', 'm_b_gate_a_w', 'm_b_gate_a_b', 'm_b_gate_x_w', 'm_b_gate_x_b', 'm_b_lambda', 'm_b_w_out', 'm_norm_f_w', 'v_norm_w', 'v_a_w_in', 'v_a_ln_w', 'v_a_ln_b', 'v_a_w_s', 'v_a_b_s', 'v_a_w_out', 'v_b_w_in', 'v_b_conv_w', 'v_b_conv_b', 'v_b_gate_a_w', 'v_b_gate_a_b', 'v_b_gate_x_w', 'v_b_gate_x_b', 'v_b_lambda', 'v_b_w_out', 'v_norm_f_w']
TWIN_OUTPUTS = ['loss', 'grad_x', 'grad_norm_w', 'grad_a_w_in', 'grad_a_ln_w', 'grad_a_ln_b', 'grad_a_w_s', 'grad_a_b_s', 'grad_a_w_out', 'grad_b_w_in', 'grad_b_conv_w', 'grad_b_conv_b', 'grad_b_gate_a_w', 'grad_b_gate_a_b', 'grad_b_gate_x_w', 'grad_b_gate_x_b', 'grad_b_lambda', 'grad_b_w_out', 'grad_norm_f_w', 'delta_norm_w', 'delta_a_w_in', 'delta_a_ln_w', 'delta_a_ln_b', 'delta_a_w_s', 'delta_a_b_s', 'delta_a_w_out', 'delta_b_w_in', 'delta_b_conv_w', 'delta_b_conv_b', 'delta_b_gate_a_w', 'delta_b_gate_a_b', 'delta_b_gate_x_w', 'delta_b_gate_x_b', 'delta_b_lambda', 'delta_b_w_out', 'delta_norm_f_w', 'new_m_norm_w', 'new_m_a_w_in', 'new_m_a_ln_w', 'new_m_a_ln_b', 'new_m_a_w_s', 'new_m_a_b_s', 'new_m_a_w_out', 'new_m_b_w_in', 'new_m_b_conv_w', 'new_m_b_conv_b', 'new_m_b_gate_a_w', 'new_m_b_gate_a_b', 'new_m_b_gate_x_w', 'new_m_b_gate_x_b', 'new_m_b_lambda', 'new_m_b_w_out', 'new_m_norm_f_w', 'new_v_norm_w', 'new_v_a_w_in', 'new_v_a_ln_w', 'new_v_a_ln_b', 'new_v_a_w_s', 'new_v_a_b_s', 'new_v_a_w_out', 'new_v_b_w_in', 'new_v_b_conv_w', 'new_v_b_conv_b', 'new_v_b_gate_a_w', 'new_v_b_gate_a_b', 'new_v_b_gate_x_w', 'new_v_b_gate_x_b', 'new_v_b_lambda', 'new_v_b_w_out', 'new_v_norm_f_w']
TWIN_LEAF_KINDS = {'loss': 'loss', 'grad_x': 'grad_x', 'grad_norm_w': 'grad_w', 'grad_a_w_in': 'grad_w', 'grad_a_ln_w': 'grad_w', 'grad_a_ln_b': 'grad_w', 'grad_a_w_s': 'grad_w', 'grad_a_b_s': 'grad_w', 'grad_a_w_out': 'grad_w', 'grad_b_w_in': 'grad_w', 'grad_b_conv_w': 'grad_w', 'grad_b_conv_b': 'grad_w', 'grad_b_gate_a_w': 'grad_w', 'grad_b_gate_a_b': 'grad_w', 'grad_b_gate_x_w': 'grad_w', 'grad_b_gate_x_b': 'grad_w', 'grad_b_lambda': 'grad_w', 'grad_b_w_out': 'grad_w', 'grad_norm_f_w': 'grad_w', 'delta_norm_w': 'delta_w', 'delta_a_w_in': 'delta_w', 'delta_a_ln_w': 'delta_w', 'delta_a_ln_b': 'delta_w', 'delta_a_w_s': 'delta_w', 'delta_a_b_s': 'delta_w', 'delta_a_w_out': 'delta_w', 'delta_b_w_in': 'delta_w', 'delta_b_conv_w': 'delta_w', 'delta_b_conv_b': 'delta_w', 'delta_b_gate_a_w': 'delta_w', 'delta_b_gate_a_b': 'delta_w', 'delta_b_gate_x_w': 'delta_w', 'delta_b_gate_x_b': 'delta_w', 'delta_b_lambda': 'delta_w', 'delta_b_w_out': 'delta_w', 'delta_norm_f_w': 'delta_w', 'new_m_norm_w': 'new_m', 'new_m_a_w_in': 'new_m', 'new_m_a_ln_w': 'new_m', 'new_m_a_ln_b': 'new_m', 'new_m_a_w_s': 'new_m', 'new_m_a_b_s': 'new_m', 'new_m_a_w_out': 'new_m', 'new_m_b_w_in': 'new_m', 'new_m_b_conv_w': 'new_m', 'new_m_b_conv_b': 'new_m', 'new_m_b_gate_a_w': 'new_m', 'new_m_b_gate_a_b': 'new_m', 'new_m_b_gate_x_w': 'new_m', 'new_m_b_gate_x_b': 'new_m', 'new_m_b_lambda': 'new_m', 'new_m_b_w_out': 'new_m', 'new_m_norm_f_w': 'new_m', 'new_v_norm_w': 'new_v', 'new_v_a_w_in': 'new_v', 'new_v_a_ln_w': 'new_v', 'new_v_a_ln_b': 'new_v', 'new_v_a_w_s': 'new_v', 'new_v_a_b_s': 'new_v', 'new_v_a_w_out': 'new_v', 'new_v_b_w_in': 'new_v', 'new_v_b_conv_w': 'new_v', 'new_v_b_conv_b': 'new_v', 'new_v_b_gate_a_w': 'new_v', 'new_v_b_gate_a_b': 'new_v', 'new_v_b_gate_x_w': 'new_v', 'new_v_b_gate_x_b': 'new_v', 'new_v_b_lambda': 'new_v', 'new_v_b_w_out': 'new_v', 'new_v_norm_f_w': 'new_v'}


def _forward(args):
    return _fwd_reference(*[args[k] for k in FWD_PARAMS])


def _output_shape():
    out = _jax.eval_shape(lambda: _forward(_fwd_setup_inputs(0)))
    return out.shape, out.dtype

N_MICROBATCH = 1
ADAM_LR = 0.001
ADAM_B1 = 0.9
ADAM_B2 = 0.999
ADAM_EPS = 1e-08
ADAM_WD = 0.01
ADAM_STEP = 10
PER_EXAMPLE_BATCH_AXIS = {'x': 0, 'loss_target': 0}
SHARED_INPUTS = []
_WEIGHT_DTYPES = {'norm_w': _jnp.float32, 'a_w_in': _jnp.float32, 'a_ln_w': _jnp.float32, 'a_ln_b': _jnp.float32, 'a_w_s': _jnp.float32, 'a_b_s': _jnp.float32, 'a_w_out': _jnp.float32, 'b_w_in': _jnp.float32, 'b_conv_w': _jnp.float32, 'b_conv_b': _jnp.float32, 'b_gate_a_w': _jnp.float32, 'b_gate_a_b': _jnp.float32, 'b_gate_x_w': _jnp.float32, 'b_gate_x_b': _jnp.float32, 'b_lambda': _jnp.float32, 'b_w_out': _jnp.float32, 'norm_f_w': _jnp.float32}
MOMENT_SCALE = {'norm_w': 8.007285e-02, 'a_w_in': 3.726471e-02, 'a_ln_w': 2.270847e-02, 'a_ln_b': 2.219385e-02, 'a_w_s': 3.279518e-02, 'a_b_s': 4.635404e-02, 'a_w_out': 6.300752e-02, 'b_w_in': 4.087703e-02, 'b_conv_w': 4.056703e-02, 'b_conv_b': 3.953935e-01, 'b_gate_a_w': 1.151109e-02, 'b_gate_a_b': 1.030059e-02, 'b_gate_x_w': 2.065852e-02, 'b_gate_x_b': 1.516612e-02, 'b_lambda': 1.953651e-02, 'b_w_out': 5.129807e-02, 'norm_f_w': 1.604908e+01}


def _to_microbatches(a, axis):
    t = _jnp.moveaxis(a, axis, 0)
    t = t.reshape((N_MICROBATCH, t.shape[0] // N_MICROBATCH) + t.shape[1:])
    return _jnp.moveaxis(t, 1, axis + 1)


def setup_inputs(seed: int = 0) -> dict:
    inp = _fwd_setup_inputs(seed)
    key = _jax.random.fold_in(_jax.random.key(seed), 7919)
    shape, _ = _output_shape()
    out = dict(inp)
    out["loss_target"] = _jax.random.normal(_jax.random.fold_in(key, 0), shape, _jnp.float32)
    for i, name in enumerate(TWIN_WEIGHTS):
        w = inp[name].astype(_jnp.float32)
        if MOMENT_SCALE is None:
            s = _jnp.sqrt(_jnp.mean(_jnp.square(w)) + 1e-30)
        else:
            s = MOMENT_SCALE[name]
        km, kv = _jax.random.split(_jax.random.fold_in(key, i + 1))
        out[name] = w
        out["m_" + name] = s * _jax.random.normal(km, w.shape, _jnp.float32)
        out["v_" + name] = (s * s) * _jax.random.uniform(kv, w.shape, _jnp.float32, 0.5, 1.5)
    if N_MICROBATCH > 1:
        for name, axis in PER_EXAMPLE_BATCH_AXIS.items():
            out[name] = _to_microbatches(out[name], axis)
    return {'x': out['x'], 'norm_w': out['norm_w'], 'a_w_in': out['a_w_in'], 'a_ln_w': out['a_ln_w'], 'a_ln_b': out['a_ln_b'], 'a_w_s': out['a_w_s'], 'a_b_s': out['a_b_s'], 'a_w_out': out['a_w_out'], 'b_w_in': out['b_w_in'], 'b_conv_w': out['b_conv_w'], 'b_conv_b': out['b_conv_b'], 'b_gate_a_w': out['b_gate_a_w'], 'b_gate_a_b': out['b_gate_a_b'], 'b_gate_x_w': out['b_gate_x_w'], 'b_gate_x_b': out['b_gate_x_b'], 'b_lambda': out['b_lambda'], 'b_w_out': out['b_w_out'], 'norm_f_w': out['norm_f_w'], 'loss_target': out['loss_target'], 'm_norm_w': out['m_norm_w'], 'm_a_w_in': out['m_a_w_in'], 'm_a_ln_w': out['m_a_ln_w'], 'm_a_ln_b': out['m_a_ln_b'], 'm_a_w_s': out['m_a_w_s'], 'm_a_b_s': out['m_a_b_s'], 'm_a_w_out': out['m_a_w_out'], 'm_b_w_in': out['m_b_w_in'], 'm_b_conv_w': out['m_b_conv_w'], 'm_b_conv_b': out['m_b_conv_b'], 'm_b_gate_a_w': out['m_b_gate_a_w'], 'm_b_gate_a_b': out['m_b_gate_a_b'], 'm_b_gate_x_w': out['m_b_gate_x_w'], 'm_b_gate_x_b': out['m_b_gate_x_b'], 'm_b_lambda': out['m_b_lambda'], 'm_b_w_out': out['m_b_w_out'], 'm_norm_f_w': out['m_norm_f_w'], 'v_norm_w': out['v_norm_w'], 'v_a_w_in': out['v_a_w_in'], 'v_a_ln_w': out['v_a_ln_w'], 'v_a_ln_b': out['v_a_ln_b'], 'v_a_w_s': out['v_a_w_s'], 'v_a_b_s': out['v_a_b_s'], 'v_a_w_out': out['v_a_w_out'], 'v_b_w_in': out['v_b_w_in'], 'v_b_conv_w': out['v_b_conv_w'], 'v_b_conv_b': out['v_b_conv_b'], 'v_b_gate_a_w': out['v_b_gate_a_w'], 'v_b_gate_a_b': out['v_b_gate_a_b'], 'v_b_gate_x_w': out['v_b_gate_x_w'], 'v_b_gate_x_b': out['v_b_gate_x_b'], 'v_b_lambda': out['v_b_lambda'], 'v_b_w_out': out['v_b_w_out'], 'v_norm_f_w': out['v_norm_f_w']}


def _loss(weights, diff, rest, loss_target):
    with _jax.named_scope("forward"):
        args = {**rest, TWIN_DIFF_INPUT: diff, **{k: w.astype(_WEIGHT_DTYPES[k]) for k, w in weights.items()}}
        y = _forward(args)
    with _jax.named_scope("loss_head"):
        err = _jnp.square(y.astype(_jnp.float32) - loss_target)
        return 0.5 * _jnp.sum(_jnp.mean(err, axis=-1)) if err.ndim else 0.5 * err


def _adamw(w, g, m, v):
    m = ADAM_B1 * m + (1.0 - ADAM_B1) * g
    v = ADAM_B2 * v + (1.0 - ADAM_B2) * _jnp.square(g)
    m_hat = m / (1.0 - ADAM_B1 ** ADAM_STEP)
    v_hat = v / (1.0 - ADAM_B2 ** ADAM_STEP)
    delta = -ADAM_LR * (m_hat / (_jnp.sqrt(v_hat) + ADAM_EPS) + ADAM_WD * w)
    return delta, m, v


def reference(x, norm_w, a_w_in, a_ln_w, a_ln_b, a_w_s, a_b_s, a_w_out, b_w_in, b_conv_w, b_conv_b, b_gate_a_w, b_gate_a_b, b_gate_x_w, b_gate_x_b, b_lambda, b_w_out, norm_f_w, loss_target, m_norm_w, m_a_w_in, m_a_ln_w, m_a_ln_b, m_a_w_s, m_a_b_s, m_a_w_out, m_b_w_in, m_b_conv_w, m_b_conv_b, m_b_gate_a_w, m_b_gate_a_b, m_b_gate_x_w, m_b_gate_x_b, m_b_lambda, m_b_w_out, m_norm_f_w, v_norm_w, v_a_w_in, v_a_ln_w, v_a_ln_b, v_a_w_s, v_a_b_s, v_a_w_out, v_b_w_in, v_b_conv_w, v_b_conv_b, v_b_gate_a_w, v_b_gate_a_b, v_b_gate_x_w, v_b_gate_x_b, v_b_lambda, v_b_w_out, v_norm_f_w):
    given = dict(x=x, norm_w=norm_w, a_w_in=a_w_in, a_ln_w=a_ln_w, a_ln_b=a_ln_b, a_w_s=a_w_s, a_b_s=a_b_s, a_w_out=a_w_out, b_w_in=b_w_in, b_conv_w=b_conv_w, b_conv_b=b_conv_b, b_gate_a_w=b_gate_a_w, b_gate_a_b=b_gate_a_b, b_gate_x_w=b_gate_x_w, b_gate_x_b=b_gate_x_b, b_lambda=b_lambda, b_w_out=b_w_out, norm_f_w=norm_f_w, loss_target=loss_target, m_norm_w=m_norm_w, m_a_w_in=m_a_w_in, m_a_ln_w=m_a_ln_w, m_a_ln_b=m_a_ln_b, m_a_w_s=m_a_w_s, m_a_b_s=m_a_b_s, m_a_w_out=m_a_w_out, m_b_w_in=m_b_w_in, m_b_conv_w=m_b_conv_w, m_b_conv_b=m_b_conv_b, m_b_gate_a_w=m_b_gate_a_w, m_b_gate_a_b=m_b_gate_a_b, m_b_gate_x_w=m_b_gate_x_w, m_b_gate_x_b=m_b_gate_x_b, m_b_lambda=m_b_lambda, m_b_w_out=m_b_w_out, m_norm_f_w=m_norm_f_w, v_norm_w=v_norm_w, v_a_w_in=v_a_w_in, v_a_ln_w=v_a_ln_w, v_a_ln_b=v_a_ln_b, v_a_w_s=v_a_w_s, v_a_b_s=v_a_b_s, v_a_w_out=v_a_w_out, v_b_w_in=v_b_w_in, v_b_conv_w=v_b_conv_w, v_b_conv_b=v_b_conv_b, v_b_gate_a_w=v_b_gate_a_w, v_b_gate_a_b=v_b_gate_a_b, v_b_gate_x_w=v_b_gate_x_w, v_b_gate_x_b=v_b_gate_x_b, v_b_lambda=v_b_lambda, v_b_w_out=v_b_w_out, v_norm_f_w=v_norm_f_w)
    weights = {n: given[n] for n in TWIN_WEIGHTS}
    shared = {n: given[n] for n in SHARED_INPUTS}
    per_example = {n: given[n] for n in ['x']}
    grad_fn = _jax.value_and_grad(_loss, argnums=(0, 1))

    def one_microbatch(ex, loss_target):
        ex = dict(ex)
        diff = ex.pop(TWIN_DIFF_INPUT)
        return grad_fn(weights, diff, {**shared, **ex}, loss_target)

    if N_MICROBATCH == 1:
        loss, (grad_w, grad_x) = one_microbatch(per_example, given["loss_target"])
    else:
        def body(carry, xs):
            loss_sum, grad_sum = carry
            l_k, (gw_k, gx_k) = one_microbatch(xs[0], xs[1])
            with _jax.named_scope("update"):
                return (loss_sum + l_k, _jax.tree.map(_jnp.add, grad_sum, gw_k)), gx_k

        init = (_jnp.zeros((), _jnp.float32), _jax.tree.map(_jnp.zeros_like, weights))
        (loss, grad_w), grad_x = _jax.lax.scan(body, init, (per_example, given["loss_target"]))
    with _jax.named_scope("update"):
        delta_w, new_m, new_v = {}, {}, {}
        for n in TWIN_WEIGHTS:
            delta_w[n], new_m[n], new_v[n] = _adamw(weights[n], grad_w[n], given["m_" + n], given["v_" + n])
    return (loss, grad_x, *[grad_w[n] for n in TWIN_WEIGHTS], *[delta_w[n] for n in TWIN_WEIGHTS],
            *[new_m[n] for n in TWIN_WEIGHTS], *[new_v[n] for n in TWIN_WEIGHTS])
```

```python
import functools

import jax
import jax.numpy as jnp
from jax import lax
from jax.experimental import pallas as pl
from jax.experimental.pallas import tpu as pltpu

F32 = jnp.float32
BF16 = jnp.bfloat16
MESH = pl.DeviceIdType.MESH

NDEV = 8
D = 1024
AW = 2048
G = 8
GD = AW // G
CH = 128
BW = 1536
BH = 12
HD = BW // BH
CA = 3 * AW // NDEV
CB = 2 * BW // NDEV
RMS_EPS = 1e-6
LN_EPS = 1e-5
RG_C = 8.0
LR, B1, B2, ADAM_EPS, WD, STEP = 0.001, 0.9, 0.999, 1e-08, 0.01, 10
V7X_VMEM_BYTES = 64 * 1024 * 1024
VMEM_LIMIT = V7X_VMEM_BYTES - 8 * 1024 * 1024
SUBLANES = 8
GELU_C = 0.7978845608028654
GELU_K = 0.044715

_VMEM = pl.BlockSpec(memory_space=pltpu.VMEM)
_HBM = pl.BlockSpec(memory_space=pltpu.HBM)


def _params(**kw):
    return pltpu.CompilerParams(vmem_limit_bytes=VMEM_LIMIT, **kw)


def _gelu_t(z):
    t = jnp.tanh(GELU_C * (z + GELU_K * (z * z * z)))
    return 0.5 * z * (1.0 + t), t


def _dgelu(z, t):
    return 0.5 * (1.0 + t) + 0.5 * z * (1.0 - t * t) * (GELU_C * (1.0 + 3.0 * GELU_K * z * z))


def _sigmoid(v):
    return 1.0 / (1.0 + jnp.exp(-v))


def _softplus_neg(lam):
    return jnp.maximum(-lam, 0.0) + jnp.log1p(jnp.exp(-jnp.abs(lam)))


def _dot(a, b):
    return jnp.dot(a, b, preferred_element_type=F32)


def _dot_nt(a, b):
    return lax.dot_general(a, b, (((1,), (1,)), ((), ())), preferred_element_type=F32)


def _rowsum(v):
    return jnp.sum(v, axis=0, keepdims=True)


def _causal_mask():
    r = lax.broadcasted_iota(jnp.int32, (CH, CH), 0)
    c = lax.broadcasted_iota(jnp.int32, (CH, CH), 1)
    return r >= c


def _fwd_a(x, nw, win8, lnw, lnb, ws, bst, wout, *, tm):
    s_len = x.shape[0]
    nch = tm // CH

    def body(x_ref, nw_ref, win_ref, lnw_ref, lnb_ref, ws_ref, bst_ref, wout_ref,
             x1_ref, z_ref, h_ref, wc_scr, gv_scr, y_scr):
        @pl.when(pl.program_id(0) == 0)
        def _():
            m = _causal_mask()
            for g in range(G):
                wc_scr[g] = jnp.where(m, ws_ref[g], 0.0).astype(BF16)

        x = x_ref[...]
        r = lax.rsqrt(jnp.mean(x * x, axis=-1, keepdims=True) + RMS_EPS)
        h = (x * r * nw_ref[...]).astype(BF16)
        h_ref[...] = h
        for k in range(NDEV):
            z_ref[:, k * CA:(k + 1) * CA] = _dot(h, win_ref[k])

        ssum = jnp.zeros((tm, 1), F32)
        for g in range(G):
            gv = _gelu_t(z_ref[:, AW + g * GD:AW + (g + 1) * GD])[0]
            gv_scr[:, g * GD:(g + 1) * GD] = gv
            ssum = ssum + jnp.sum(gv, axis=-1, keepdims=True)
        mu = ssum * (1.0 / AW)
        vsum = jnp.zeros((tm, 1), F32)
        for g in range(G):
            dlt = gv_scr[:, g * GD:(g + 1) * GD] - mu
            vsum = vsum + jnp.sum(dlt * dlt, axis=-1, keepdims=True)
        rstd = lax.rsqrt(vsum * (1.0 / AW) + LN_EPS)

        for g in range(G):
            cs = slice(g * GD, (g + 1) * GD)
            v = (gv_scr[:, cs] - mu) * rstd * lnw_ref[:, cs] + lnb_ref[:, cs]
            vb = v.astype(BF16)
            u = _gelu_t(z_ref[:, cs])[0]
            zg = z_ref[:, 2 * AW + g * GD:2 * AW + (g + 1) * GD]
            sg = zg * _sigmoid(zg)
            for n in range(nch):
                rs = slice(n * CH, (n + 1) * CH)
                s = _dot(wc_scr[g], vb[rs, :]) + bst_ref[:, g:g + 1]
                y_scr[rs, cs] = (u[rs, :] * s * sg[rs, :]).astype(BF16)
        x1_ref[...] = x + _dot(y_scr[...], wout_ref[...])

    tile = lambda w: pl.BlockSpec((tm, w), lambda i: (i, 0))
    return pl.pallas_call(
        body, name="fwd_a", grid=(s_len // tm,),
        in_specs=[tile(D), _VMEM, _VMEM, _VMEM, _VMEM, _VMEM, _VMEM, _VMEM],
        out_specs=[tile(D), tile(3 * AW), tile(D)],
        out_shape=[jax.ShapeDtypeStruct((s_len, D), F32), jax.ShapeDtypeStruct((s_len, 3 * AW), F32),
                   jax.ShapeDtypeStruct((s_len, D), BF16)],
        scratch_shapes=[pltpu.VMEM((G, CH, CH), BF16), pltpu.VMEM((tm, AW), F32), pltpu.VMEM((tm, AW), BF16)],
        compiler_params=_params(dimension_semantics=("arbitrary",)),
    )(x, nw, win8, lnw, lnb, ws, bst, wout)


def _bwd_a(dx1, z, x, nw, win8, lnw, lnb, ws, bst, wout, *, tm):
    s_len = x.shape[0]
    nt = s_len // tm
    nch = tm // CH

    def body(dx1_ref, z_ref, x_ref, nw_ref, win_ref, lnw_ref, lnb_ref, ws_ref, bst_ref, wout_ref,
             gx_ref, dz_ref, y_ref, glnw_ref, glnb_ref, gws_ref, gbst_ref, gnw_ref,
             wc_scr, wct_scr, vh_scr, vb_scr, dgv_scr, dy_scr, ds_scr, dv_scr, gbs_acc, gwc_acc):
        i = pl.program_id(0)

        @pl.when(i == 0)
        def _():
            m = _causal_mask()
            for g in range(G):
                wm = jnp.where(m, ws_ref[g], 0.0)
                wc_scr[g] = wm.astype(BF16)
                wct_scr[g] = wm.T.astype(BF16)
            glnw_ref[...] = jnp.zeros_like(glnw_ref)
            glnb_ref[...] = jnp.zeros_like(glnb_ref)
            gnw_ref[...] = jnp.zeros_like(gnw_ref)
            gbs_acc[...] = jnp.zeros_like(gbs_acc)
            gwc_acc[...] = jnp.zeros_like(gwc_acc)

        dx1 = dx1_ref[...]
        dy_scr[...] = _dot_nt(dx1.astype(BF16), wout_ref[...])

        ssum = jnp.zeros((tm, 1), F32)
        for g in range(G):
            cs = slice(g * GD, (g + 1) * GD)
            zv = z_ref[:, AW + g * GD:AW + (g + 1) * GD]
            gv, t = _gelu_t(zv)
            vh_scr[:, cs] = gv
            dgv_scr[:, cs] = _dgelu(zv, t)
            ssum = ssum + jnp.sum(gv, axis=-1, keepdims=True)
        mu = ssum * (1.0 / AW)
        vsum = jnp.zeros((tm, 1), F32)
        for g in range(G):
            dlt = vh_scr[:, g * GD:(g + 1) * GD] - mu
            vsum = vsum + jnp.sum(dlt * dlt, axis=-1, keepdims=True)
        rstd = lax.rsqrt(vsum * (1.0 / AW) + LN_EPS)

        m1 = jnp.zeros((tm, 1), F32)
        m2 = jnp.zeros((tm, 1), F32)
        for g in range(G):
            cs = slice(g * GD, (g + 1) * GD)
            gs = slice(2 * AW + g * GD, 2 * AW + (g + 1) * GD)
            vhat = (vh_scr[:, cs] - mu) * rstd
            vh_scr[:, cs] = vhat
            vb = (vhat * lnw_ref[:, cs] + lnb_ref[:, cs]).astype(BF16)
            vb_scr[:, cs] = vb
            zu = z_ref[:, cs]
            u, tu = _gelu_t(zu)
            zg = z_ref[:, gs]
            sig = _sigmoid(zg)
            sg = zg * sig
            dy = dy_scr[:, cs]
            dsf = dy * u * sg
            dsb = dsf.astype(BF16)
            ds_scr[:, cs] = dsb
            dvs = []
            for n in range(nch):
                rs = slice(n * CH, (n + 1) * CH)
                s = _dot(wc_scr[g], vb[rs, :]) + bst_ref[:, g:g + 1]
                y_ref[rs, cs] = (u[rs, :] * s * sg[rs, :]).astype(BF16)
                dys = dy[rs, :] * s
                dz_ref[rs, cs] = (dys * sg[rs, :] * _dgelu(zu[rs, :], tu[rs, :])).astype(BF16)
                dz_ref[rs, gs] = (dys * u[rs, :] * (sig[rs, :] * (1.0 + zg[rs, :] * (1.0 - sig[rs, :])))).astype(BF16)
                gbs_acc[g] += dsf[rs, :]
                gwc_acc[g] += _dot_nt(dsb[rs, :], vb[rs, :])
                dvs.append(_dot(wct_scr[g], dsb[rs, :]))
            dv = jnp.concatenate(dvs, axis=0) if nch > 1 else dvs[0]
            glnw_ref[:, cs] += _rowsum(dv * vhat)
            glnb_ref[:, cs] += _rowsum(dv)
            dvh = dv * lnw_ref[:, cs]
            dv_scr[:, cs] = dvh
            m1 = m1 + jnp.sum(dvh, axis=-1, keepdims=True)
            m2 = m2 + jnp.sum(dvh * vhat, axis=-1, keepdims=True)
        m1 = m1 * (1.0 / AW)
        m2 = m2 * (1.0 / AW)
        for g in range(G):
            cs = slice(g * GD, (g + 1) * GD)
            dgv = rstd * (dv_scr[:, cs] - m1 - vh_scr[:, cs] * m2)
            dz_ref[:, AW + g * GD:AW + (g + 1) * GD] = (dgv * dgv_scr[:, cs]).astype(BF16)

        dh = jnp.zeros((tm, D), F32)
        for k in range(NDEV):
            dh = dh + _dot_nt(dz_ref[:, k * CA:(k + 1) * CA], win_ref[k])
        x = x_ref[...]
        r = lax.rsqrt(jnp.mean(x * x, axis=-1, keepdims=True) + RMS_EPS)
        gy = dh * nw_ref[...]
        gx_ref[...] = dx1 + r * gy - x * (r * r * r) * jnp.mean(gy * x, axis=-1, keepdims=True)
        gnw_ref[...] += _rowsum(dh * x * r)

        @pl.when(i == nt - 1)
        def _():
            m = _causal_mask()
            for g in range(G):
                gws_ref[g] = jnp.where(m, gwc_acc[g], 0.0)
                gbst_ref[:, g:g + 1] = jnp.sum(gbs_acc[g], axis=-1, keepdims=True)

    tile = lambda w: pl.BlockSpec((tm, w), lambda i: (i, 0))
    whole = lambda *s: pl.BlockSpec(s, lambda i: (0,) * len(s))
    return pl.pallas_call(
        body, name="bwd_a", grid=(nt,),
        in_specs=[tile(D), tile(3 * AW), tile(D), _VMEM, _VMEM, _VMEM, _VMEM, _VMEM, _VMEM, _VMEM],
        out_specs=[tile(D), tile(3 * AW), tile(AW), whole(1, AW), whole(1, AW), whole(G, CH, CH), whole(CH, G),
                   whole(1, D)],
        out_shape=[jax.ShapeDtypeStruct((s_len, D), F32), jax.ShapeDtypeStruct((s_len, 3 * AW), BF16),
                   jax.ShapeDtypeStruct((s_len, AW), BF16), jax.ShapeDtypeStruct((1, AW), F32),
                   jax.ShapeDtypeStruct((1, AW), F32), jax.ShapeDtypeStruct((G, CH, CH), F32),
                   jax.ShapeDtypeStruct((CH, G), F32), jax.ShapeDtypeStruct((1, D), F32)],
        scratch_shapes=[pltpu.VMEM((G, CH, CH), BF16), pltpu.VMEM((G, CH, CH), BF16),
                        pltpu.VMEM((tm, AW), F32), pltpu.VMEM((tm, AW), BF16), pltpu.VMEM((tm, AW), F32),
                        pltpu.VMEM((tm, AW), F32), pltpu.VMEM((tm, AW), BF16), pltpu.VMEM((tm, AW), F32),
                        pltpu.VMEM((G, CH, GD), F32), pltpu.VMEM((G, CH, CH), F32)],
        compiler_params=_params(dimension_semantics=("arbitrary",)),
    )(dx1, z, x, nw, win8, lnw, lnb, ws, bst, wout)


def _conv(p8_ref, cs, xb, xm1, xm2, xm3):
    xc = p8_ref[4:5, cs] + p8_ref[3:4, cs] * xb
    xc = xc + p8_ref[0:1, cs] * xm3
    xc = xc + p8_ref[1:2, cs] * xm2
    return xc + p8_ref[2:3, cs] * xm1


def _gates(p8_ref, gcat_ref, hh, xc):
    cs = slice(hh * HD, (hh + 1) * HD)
    pre = _dot(xc.astype(BF16), gcat_ref[hh])
    r = _sigmoid(pre[:, :HD] + p8_ref[5:6, cs])
    ig = _sigmoid(pre[:, HD:] + p8_ref[6:7, cs])
    sp = _softplus_neg(p8_ref[7:8, cs])
    la = (-RG_C) * r * sp
    a = jnp.exp(la)
    mult = jnp.sqrt(jnp.tanh(-la) * (1.0 + a * a))
    return r, ig, sp, a, mult


def _scan_rows(a_ref, b_ref, out_ref, carry, tm, reverse):
    row = lax.broadcasted_iota(jnp.int32, (SUBLANES, BW), 0)
    ngrp = tm // SUBLANES

    def step(j, cr):
        jj = (ngrp - 1 - j) if reverse else j
        off = pl.multiple_of(jj * SUBLANES, SUBLANES)
        a = a_ref[pl.ds(off, SUBLANES), :]
        b = b_ref[pl.ds(off, SUBLANES), :]
        for sh in (1, 2, 4):
            if reverse:
                a_s = pltpu.roll(a, SUBLANES - sh, 0)
                b_s = pltpu.roll(b, SUBLANES - sh, 0)
                m = row < SUBLANES - sh
            else:
                a_s = pltpu.roll(a, sh, 0)
                b_s = pltpu.roll(b, sh, 0)
                m = row >= sh
            b = jnp.where(m, a * b_s + b, b)
            a = jnp.where(m, a * a_s, a)
        o = b + a * cr
        out_ref[pl.ds(off, SUBLANES), :] = o
        return o[0:1, :] if reverse else o[SUBLANES - 1:SUBLANES, :]

    return lax.fori_loop(0, ngrp, step, carry)


def _fwd_b(x1, nw, win8, p8, gcat, wout, nfw, tgt, *, tm):
    s_len = x1.shape[0]

    def body(x1_ref, nw_ref, win_ref, p8_ref, gcat_ref, wout_ref, nfw_ref, t_ref,
             zb_ref, hs_ref, dx2_ref, dx2b_ref, h1_ref, yb_ref, loss_ref, gnfw_ref,
             xbe_scr, a_scr, b_scr, carry_scr):
        @pl.when(pl.program_id(0) == 0)
        def _():
            xbe_scr[0:SUBLANES, :] = jnp.zeros((SUBLANES, BW), F32)
            carry_scr[...] = jnp.zeros_like(carry_scr)
            loss_ref[...] = jnp.zeros_like(loss_ref)
            gnfw_ref[...] = jnp.zeros_like(gnfw_ref)

        x1 = x1_ref[...]
        r1 = lax.rsqrt(jnp.mean(x1 * x1, axis=-1, keepdims=True) + RMS_EPS)
        h = (x1 * r1 * nw_ref[...]).astype(BF16)
        h1_ref[...] = h
        for k in range(NDEV):
            zb_ref[:, k * CB:(k + 1) * CB] = _dot(h, win_ref[k])
        xbe_scr[SUBLANES:SUBLANES + tm, :] = zb_ref[:, :BW]
        for hh in range(BH):
            cs = slice(hh * HD, (hh + 1) * HD)
            xc = _conv(p8_ref, cs, xbe_scr[SUBLANES:SUBLANES + tm, cs], xbe_scr[7:7 + tm, cs],
                       xbe_scr[6:6 + tm, cs], xbe_scr[5:5 + tm, cs])
            _, ig, _, a, mult = _gates(p8_ref, gcat_ref, hh, xc)
            a_scr[:, cs] = a
            b_scr[:, cs] = mult * (ig * xc)
        xbe_scr[0:SUBLANES, :] = xbe_scr[tm:tm + SUBLANES, :]
        carry_scr[...] = _scan_rows(a_scr, b_scr, hs_ref, carry_scr[...], tm, False)
        for hh in range(BH):
            cs = slice(hh * HD, (hh + 1) * HD)
            gt = zb_ref[:, BW + hh * HD:BW + (hh + 1) * HD]
            yb_ref[:, cs] = (hs_ref[:, cs] * (gt * _sigmoid(gt))).astype(BF16)
        x2 = x1 + _dot(yb_ref[...], wout_ref[...])
        rf = lax.rsqrt(jnp.mean(x2 * x2, axis=-1, keepdims=True) + RMS_EPS)
        xn = x2 * rf
        e = xn * nfw_ref[...] - t_ref[...]
        loss_ref[...] += (0.5 / D) * jnp.sum(jnp.sum(e * e, axis=-1, keepdims=True), axis=0, keepdims=True)
        dyf = e * (1.0 / D)
        gnfw_ref[...] += _rowsum(dyf * xn)
        gy = dyf * nfw_ref[...]
        dx2 = rf * gy - x2 * (rf * rf * rf) * jnp.mean(gy * x2, axis=-1, keepdims=True)
        dx2_ref[...] = dx2
        dx2b_ref[...] = dx2.astype(BF16)

    tile = lambda w: pl.BlockSpec((tm, w), lambda i: (i, 0))
    whole = lambda *s: pl.BlockSpec(s, lambda i: (0,) * len(s))
    return pl.pallas_call(
        body, name="fwd_b", grid=(s_len // tm,),
        in_specs=[tile(D), _VMEM, _VMEM, _VMEM, _VMEM, _VMEM, _VMEM, tile(D)],
        out_specs=[tile(2 * BW), tile(BW), tile(D), tile(D), tile(D), tile(BW), whole(1, 1), whole(1, D)],
        out_shape=[jax.ShapeDtypeStruct((s_len, 2 * BW), F32), jax.ShapeDtypeStruct((s_len, BW), F32),
                   jax.ShapeDtypeStruct((s_len, D), F32), jax.ShapeDtypeStruct((s_len, D), BF16),
                   jax.ShapeDtypeStruct((s_len, D), BF16), jax.ShapeDtypeStruct((s_len, BW), BF16),
                   jax.ShapeDtypeStruct((1, 1), F32), jax.ShapeDtypeStruct((1, D), F32)],
        scratch_shapes=[pltpu.VMEM((tm + SUBLANES, BW), F32), pltpu.VMEM((tm, BW), F32),
                        pltpu.VMEM((tm, BW), F32), pltpu.VMEM((1, BW), F32)],
        compiler_params=_params(dimension_semantics=("arbitrary",)),
    )(x1, nw, win8, p8, gcat, wout, nfw, tgt)


def _bwd_b(dx2, zb, hs, x1, nw, win8, p8, gcat, wout, *, tm):
    s_len = x1.shape[0]
    nt = s_len // tm
    per = tm // SUBLANES

    def body(dx2_ref, zb_ref, zbp_ref, hs_ref, hsp_ref, x1_ref, nw_ref, win_ref, p8_ref, gcat_ref, wout_ref,
             dx1_ref, dx1b_ref, dzb_ref, gp8_ref, gg_ref, gnw_ref,
             xbe_scr, hse_scr, ae_scr, an_scr, r_scr, i_scr, m_scr, xc_scr, dhd_scr, dh_scr, dy_scr, dxce_scr,
             carry_scr, afirst_scr):
        i = pl.program_id(0)
        ti = nt - 1 - i

        @pl.when(i == 0)
        def _():
            gp8_ref[...] = jnp.zeros_like(gp8_ref)
            gg_ref[...] = jnp.zeros_like(gg_ref)
            gnw_ref[...] = jnp.zeros_like(gnw_ref)
            dxce_scr[tm:tm + SUBLANES, :] = jnp.zeros((SUBLANES, BW), F32)
            carry_scr[...] = jnp.zeros_like(carry_scr)
            afirst_scr[...] = jnp.zeros_like(afirst_scr)

        has_prev = (ti > 0).astype(F32)
        xbe_scr[0:SUBLANES, :] = zbp_ref[:, :BW] * has_prev
        xbe_scr[SUBLANES:SUBLANES + tm, :] = zb_ref[:, :BW]
        hse_scr[0:SUBLANES, :] = hsp_ref[...] * has_prev
        hse_scr[SUBLANES:SUBLANES + tm, :] = hs_ref[...]

        dx2 = dx2_ref[...]
        dy_scr[...] = _dot_nt(dx2.astype(BF16), wout_ref[...])

        for hh in range(BH):
            cs = slice(hh * HD, (hh + 1) * HD)
            xc = _conv(p8_ref, cs, xbe_scr[SUBLANES:SUBLANES + tm, cs], xbe_scr[7:7 + tm, cs],
                       xbe_scr[6:6 + tm, cs], xbe_scr[5:5 + tm, cs])
            r, ig, _, a, mult = _gates(p8_ref, gcat_ref, hh, xc)
            xc_scr[:, cs] = xc
            r_scr[:, cs] = r
            i_scr[:, cs] = ig
            m_scr[:, cs] = mult
            ae_scr[0:tm, cs] = a
            gt = zb_ref[:, BW + hh * HD:BW + (hh + 1) * HD]
            sig = _sigmoid(gt)
            dy = dy_scr[:, cs]
            dhd_scr[:, cs] = dy * (gt * sig)
            dzb_ref[:, BW + hh * HD:BW + (hh + 1) * HD] = (
                dy * hs_ref[:, cs] * (sig * (1.0 + gt * (1.0 - sig)))).astype(BF16)
        ae_scr[tm:tm + SUBLANES, :] = jnp.broadcast_to(afirst_scr[...], (SUBLANES, BW))
        an_scr[...] = ae_scr[1:1 + tm, :]
        afirst_scr[...] = ae_scr[0:1, :]
        carry_scr[...] = _scan_rows(an_scr, dhd_scr, dh_scr, carry_scr[...], tm, True)

        for hh in range(BH):
            cs = slice(hh * HD, (hh + 1) * HD)
            dh = dh_scr[:, cs]
            a = ae_scr[0:tm, cs]
            mult = m_scr[:, cs]
            ig = i_scr[:, cs]
            r = r_scr[:, cs]
            xc = xc_scr[:, cs]
            lam = p8_ref[7:8, cs]
            sp = _softplus_neg(lam)
            da = dh * hse_scr[7:7 + tm, cs]
            dmult = dh * (ig * xc)
            dla = da * a - dmult * (a * a) / mult
            gp8_ref[7:8, cs] += _rowsum(dla * ((-RG_C) * r)) * (-_sigmoid(-lam))
            dpr = dla * ((-RG_C) * sp) * (r * (1.0 - r))
            dpi = dh * mult * xc * (ig * (1.0 - ig))
            gp8_ref[5:6, cs] += _rowsum(dpr)
            gp8_ref[6:7, cs] += _rowsum(dpi)
            dcat = jnp.concatenate([dpr, dpi], axis=1).astype(BF16)
            dxc = dh * mult * ig + _dot_nt(dcat, gcat_ref[hh])
            gg_ref[hh] += _dot(xc.T.astype(BF16), dcat)
            dxce_scr[0:tm, cs] = dxc
            gp8_ref[4:5, cs] += _rowsum(dxc)
            gp8_ref[3:4, cs] += _rowsum(dxc * xbe_scr[SUBLANES:SUBLANES + tm, cs])
            gp8_ref[2:3, cs] += _rowsum(dxc * xbe_scr[7:7 + tm, cs])
            gp8_ref[1:2, cs] += _rowsum(dxc * xbe_scr[6:6 + tm, cs])
            gp8_ref[0:1, cs] += _rowsum(dxc * xbe_scr[5:5 + tm, cs])
        for hh in range(BH):
            cs = slice(hh * HD, (hh + 1) * HD)
            dxb = p8_ref[3:4, cs] * dxce_scr[0:tm, cs]
            dxb = dxb + p8_ref[2:3, cs] * dxce_scr[1:1 + tm, cs]
            dxb = dxb + p8_ref[1:2, cs] * dxce_scr[2:2 + tm, cs]
            dxb = dxb + p8_ref[0:1, cs] * dxce_scr[3:3 + tm, cs]
            dzb_ref[:, cs] = dxb.astype(BF16)
        dxce_scr[tm:tm + SUBLANES, :] = dxce_scr[0:SUBLANES, :]

        dh1 = jnp.zeros((tm, D), F32)
        for k in range(NDEV):
            dh1 = dh1 + _dot_nt(dzb_ref[:, k * CB:(k + 1) * CB], win_ref[k])
        x1 = x1_ref[...]
        r1 = lax.rsqrt(jnp.mean(x1 * x1, axis=-1, keepdims=True) + RMS_EPS)
        gy = dh1 * nw_ref[...]
        dx1 = dx2 + r1 * gy - x1 * (r1 * r1 * r1) * jnp.mean(gy * x1, axis=-1, keepdims=True)
        dx1_ref[...] = dx1
        dx1b_ref[...] = dx1.astype(BF16)
        gnw_ref[...] += _rowsum(dh1 * x1 * r1)

    tile = lambda w: pl.BlockSpec((tm, w), lambda i: (nt - 1 - i, 0))
    prev = lambda w: pl.BlockSpec((SUBLANES, w), lambda i: (jnp.maximum((nt - 1 - i) * per - 1, 0), 0))
    whole = lambda *s: pl.BlockSpec(s, lambda i: (0,) * len(s))
    full = lambda: pltpu.VMEM((tm, BW), F32)
    ext = lambda: pltpu.VMEM((tm + SUBLANES, BW), F32)
    return pl.pallas_call(
        body, name="bwd_b", grid=(nt,),
        in_specs=[tile(D), tile(2 * BW), prev(2 * BW), tile(BW), prev(BW), tile(D),
                  _VMEM, _VMEM, _VMEM, _VMEM, _VMEM],
        out_specs=[tile(D), tile(D), tile(2 * BW), whole(SUBLANES, BW), whole(BH, HD, 2 * HD), whole(1, D)],
        out_shape=[jax.ShapeDtypeStruct((s_len, D), F32), jax.ShapeDtypeStruct((s_len, D), BF16),
                   jax.ShapeDtypeStruct((s_len, 2 * BW), BF16), jax.ShapeDtypeStruct((SUBLANES, BW), F32),
                   jax.ShapeDtypeStruct((BH, HD, 2 * HD), F32), jax.ShapeDtypeStruct((1, D), F32)],
        scratch_shapes=[ext(), ext(), ext(), full(), full(), full(), full(), full(), full(), full(), full(), ext(),
                        pltpu.VMEM((1, BW), F32), pltpu.VMEM((1, BW), F32)],
        compiler_params=_params(dimension_semantics=("arbitrary",)),
    )(dx2, zb, zb, hs, hs, x1, nw, win8, p8, gcat, wout)


def _transpose_into(dst_ref, src_ref, rows):
    s_len = src_ref.shape[0]
    for r0 in range(0, s_len, rows):
        dst_ref[:, r0:r0 + rows] = src_ref[r0:r0 + rows, :].astype(F32).T.astype(BF16)


def _wgrad_cols(a, b, *, nblk, name):
    s_len, m = a.shape
    bn = b.shape[1] // nblk

    def body(a_ref, b_ref, o_ref, at_scr):
        @pl.when(pl.program_id(0) == 0)
        def _():
            _transpose_into(at_scr, a_ref, 256)

        o_ref[0] = _dot(at_scr[...], b_ref[...]).astype(BF16)

    return pl.pallas_call(
        body, name=name, grid=(nblk,),
        in_specs=[_VMEM, pl.BlockSpec((s_len, bn), lambda j: (0, j))],
        out_specs=pl.BlockSpec((1, m, bn), lambda j: (j, 0, 0)),
        out_shape=jax.ShapeDtypeStruct((nblk, m, bn), BF16),
        scratch_shapes=[pltpu.VMEM((m, s_len), BF16)],
        compiler_params=_params(dimension_semantics=("arbitrary",)),
    )(a, b)


def _wgrad_rows(a, b, *, nblk, per, name):
    s_len, m = a.shape
    n = b.shape[1]
    rb = m // nblk
    bm = per * rb

    def body(a_ref, b_ref, o_ref, at_scr):
        _transpose_into(at_scr, a_ref, 256)
        res = _dot(at_scr[...], b_ref[...]).astype(BF16)
        for q in range(per):
            o_ref[q] = res[q * rb:(q + 1) * rb, :]

    return pl.pallas_call(
        body, name=name, grid=(nblk // per,),
        in_specs=[pl.BlockSpec((s_len, bm), lambda j: (0, j)), _VMEM],
        out_specs=pl.BlockSpec((per, rb, n), lambda j: (j, 0, 0)),
        out_shape=jax.ShapeDtypeStruct((nblk, rb, n), BF16),
        scratch_shapes=[pltpu.VMEM((bm, s_len), BF16)],
        compiler_params=_params(dimension_semantics=("arbitrary",)),
    )(a, b)


def _place():
    return lax.axis_index("x"), lax.axis_index("y"), lax.axis_index("c")


def _all_gather(shards, name):
    n = len(shards)

    def body(*refs):
        ins, outs = refs[:n], refs[n:2 * n]
        send_sems, recv_sems, local_sems = refs[2 * n:]
        x, y, c = _place()
        me = 4 * x + 2 * y + c
        sibling = (x, y, 1 - c)
        chips = [(1 - x, y), (x, 1 - y), (1 - x, 1 - y)]

        def copy(i, k, block, to, src=None):
            dst = outs[i].at[block]
            return pltpu.make_async_remote_copy(
                src_ref=dst if src is None else src, dst_ref=dst,
                send_sem=send_sems.at[k, i], recv_sem=recv_sems.at[k, i], device_id=to, device_id_type=MESH)

        mine = [pltpu.make_async_copy(ins[i], outs[i].at[me], local_sems.at[i]) for i in range(n)]
        for cp in mine:
            cp.start()
        first = []
        for i in range(n):
            first.append(copy(i, 0, me, sibling, src=ins[i]))
            for j, chip in enumerate(chips):
                first.append(copy(i, 1 + j, me, (*chip, c), src=ins[i]))
        for cp in first:
            cp.start()
        passed = []
        for j, (px, py) in enumerate(chips):
            for i in range(n):
                blk = 4 * px + 2 * py + c
                copy(i, 1 + j, blk, sibling).wait_recv()
                fwd = copy(i, 4 + j, blk, sibling)
                fwd.start()
                passed.append(fwd)
        for i in range(n):
            copy(i, 0, 4 * x + 2 * y + (1 - c), sibling).wait_recv()
        for j, (px, py) in enumerate(chips):
            for i in range(n):
                copy(i, 4 + j, 4 * px + 2 * py + (1 - c), sibling).wait_recv()
        for cp in first + passed:
            cp.wait_send()
        for cp in mine:
            cp.wait()

    return pl.pallas_call(
        body, name=name,
        in_specs=[_HBM] * n, out_specs=[_HBM] * n,
        out_shape=[jax.ShapeDtypeStruct((NDEV,) + s.shape, s.dtype) for s in shards],
        scratch_shapes=[pltpu.SemaphoreType.DMA((7, n)), pltpu.SemaphoreType.DMA((7, n)),
                        pltpu.SemaphoreType.DMA((n,))],
        compiler_params=_params(),
    )(*shards)


def _reduce_scatter(p, name):
    _, r, cd = p.shape
    dt = p.dtype

    def body(p_ref, o_ref, mine, land1, q, land2, send_sems, recv_sems, local_sems):
        x, y, c = _place()
        sibling = (x, y, 1 - c)

        def remote(src, dst, k, to):
            return pltpu.make_async_remote_copy(src_ref=src, dst_ref=dst, send_sem=send_sems.at[k],
                                                recv_sem=recv_sems.at[k], device_id=to, device_id_type=MESH)

        stage1 = []
        for px in range(2):
            for py in range(2):
                pi = 2 * px + py
                loc = pltpu.make_async_copy(p_ref.at[4 * px + 2 * py + c], mine.at[pi], local_sems.at[pi])
                loc.start()
                cp = remote(p_ref.at[4 * px + 2 * py + (1 - c)], land1.at[pi], pi, sibling)
                cp.start()
                stage1.append((loc, cp))
        for loc, cp in stage1:
            loc.wait()
            cp.wait_recv()

        chips = [(1 - x, y), (x, 1 - y), (1 - x, 1 - y)]
        stage2 = []
        for j, (qx, qy) in enumerate(chips):
            qi = 2 * qx + qy
            q[j] = (mine[qi].astype(F32) + land1[qi].astype(F32)).astype(dt)
            cp = remote(q.at[j], land2.at[j], 4 + j, (qx, qy, c))
            cp.start()
            stage2.append(cp)
        mi = 2 * x + y
        acc = mine[mi].astype(F32) + land1[mi].astype(F32)
        for j, cp in enumerate(stage2):
            cp.wait_recv()
            acc = acc + land2[j].astype(F32)
        o_ref[...] = acc
        for _, cp in stage1:
            cp.wait_send()
        for cp in stage2:
            cp.wait_send()

    return pl.pallas_call(
        body, name=name,
        in_specs=[_HBM], out_specs=_VMEM,
        out_shape=jax.ShapeDtypeStruct((r, cd), F32),
        scratch_shapes=[pltpu.VMEM((4, r, cd), dt), pltpu.VMEM((4, r, cd), dt), pltpu.VMEM((3, r, cd), dt),
                        pltpu.VMEM((3, r, cd), dt), pltpu.SemaphoreType.DMA((7,)), pltpu.SemaphoreType.DMA((7,)),
                        pltpu.SemaphoreType.DMA((4,))],
        compiler_params=_params(),
    )(p)


def _adam_math(w, g, m, v):
    m = B1 * m + (1.0 - B1) * g
    v = B2 * v + (1.0 - B2) * (g * g)
    m_hat = m / (1.0 - B1 ** STEP)
    v_hat = v / (1.0 - B2 ** STEP)
    delta = (-LR) * (m_hat / (jnp.sqrt(v_hat) + ADAM_EPS) + WD * w)
    return delta, m, v


def _adam_big(w, g, m, v, name):
    r, cd = w.shape
    rb = 256 if r % 256 == 0 else r

    def body(w_ref, g_ref, m_ref, v_ref, d_ref, mo_ref, vo_ref):
        d_ref[...], mo_ref[...], vo_ref[...] = _adam_math(w_ref[...], g_ref[...], m_ref[...], v_ref[...])

    blk = pl.BlockSpec((rb, cd), lambda i: (i, 0))
    return pl.pallas_call(
        body, name=name, grid=(r // rb,), in_specs=[blk] * 4, out_specs=[blk] * 3,
        out_shape=[jax.ShapeDtypeStruct((r, cd), F32)] * 3,
        compiler_params=_params(dimension_semantics=("arbitrary",)),
    )(w, g, m, v)


def _adam_small(groups):
    n = len(groups)

    def body(*refs):
        ins, outs = refs[:4 * n], refs[4 * n:]
        for k in range(n):
            w_ref, g_ref, m_ref, v_ref = ins[4 * k:4 * k + 4]
            d, mo, vo = _adam_math(w_ref[...], g_ref[...], m_ref[...], v_ref[...])
            outs[3 * k][...] = d
            outs[3 * k + 1][...] = mo
            outs[3 * k + 2][...] = vo

    flat = [a for grp in groups for a in grp]
    shapes = [jax.ShapeDtypeStruct(grp[0].shape, F32) for grp in groups for _ in range(3)]
    res = pl.pallas_call(
        body, name="adam_small", in_specs=[_VMEM] * (4 * n), out_specs=[_VMEM] * (3 * n), out_shape=shapes,
        compiler_params=_params(),
    )(*flat)
    return [tuple(res[3 * k:3 * k + 3]) for k in range(n)]


TM_FWD_A = 256
TM_BWD_A = 128
TM_FWD_B = 256
TM_BWD_B = 256
PACK_ROWS = 536


def _local_step(x, tgt, norm_w, win_a8, ln_w, ln_b, w_s, bst, wout_a, win_b8, p8, gcat, wout_b, norm_f_w):
    nw0, nw1 = norm_w[0:1], norm_w[1:2]
    x1, z, h0 = _fwd_a(x, nw0, win_a8, ln_w, ln_b, w_s, bst, wout_a, tm=TM_FWD_A)
    zb, hs, dx2, dx2b, h1, yb, loss, g_nfw = _fwd_b(x1, nw1, win_b8, p8, gcat, wout_b, norm_f_w, tgt, tm=TM_FWD_B)
    dx1, dx1b, dzb, g_p8, g_gcat, g_nw1 = _bwd_b(dx2, zb, hs, x1, nw1, win_b8, p8, gcat, wout_b, tm=TM_BWD_B)
    gx, dz, ya, g_lnw, g_lnb, g_ws, g_bst, g_nw0 = _bwd_a(dx1, z, x, nw0, win_a8, ln_w, ln_b, w_s, bst, wout_a,
                                                          tm=TM_BWD_A)
    p_wout_b = _wgrad_rows(yb, dx2b, nblk=NDEV, per=2, name="wgrad_b_out")
    p_win_b = _wgrad_cols(h1, dzb, nblk=NDEV, name="wgrad_b_in")
    p_wout_a = _wgrad_rows(ya, dx1b, nblk=NDEV, per=1, name="wgrad_a_out")
    p_win_a = _wgrad_cols(h0, dz, nblk=NDEV, name="wgrad_a_in")
    small = dict(norm_w=jnp.concatenate([g_nw0, g_nw1], axis=0), ln_w=g_lnw, ln_b=g_lnb, w_s=g_ws,
                 b_s=g_bst.T, gate_a=g_gcat[:, :, :HD], gate_x=g_gcat[:, :, HD:], norm_f=g_nfw, p8=g_p8)
    return loss[0, 0], gx, (p_win_a, p_wout_a, p_win_b, p_wout_b), small


_SMALL_ORDER = ("norm_w", "ln_w", "ln_b", "w_s", "b_s", "gate_a", "gate_x", "norm_f", "p8")


def kernel(x, norm_w, a_w_in, a_ln_w, a_ln_b, a_w_s, a_b_s, a_w_out, b_w_in, b_conv_w, b_conv_b, b_gate_a_w, b_gate_a_b, b_gate_x_w, b_gate_x_b, b_lambda, b_w_out, norm_f_w, loss_target, m_norm_w, m_a_w_in, m_a_ln_w, m_a_ln_b, m_a_w_s, m_a_b_s, m_a_w_out, m_b_w_in, m_b_conv_w, m_b_conv_b, m_b_gate_a_w, m_b_gate_a_b, m_b_gate_x_w, m_b_gate_x_b, m_b_lambda, m_b_w_out, m_norm_f_w, v_norm_w, v_a_w_in, v_a_ln_w, v_a_ln_b, v_a_w_s, v_a_b_s, v_a_w_out, v_b_w_in, v_b_conv_w, v_b_conv_b, v_b_gate_a_w, v_b_gate_a_b, v_b_gate_x_w, v_b_gate_x_b, v_b_lambda, v_b_w_out, v_norm_f_w):
    me = 4 * lax.axis_index("x") + 2 * lax.axis_index("y") + lax.axis_index("c")

    p8_shard = jnp.concatenate([b_conv_w[0], b_conv_b, b_gate_a_b, b_gate_x_b, b_lambda], axis=0)
    win_a8, wout_a8, win_b8, wout_b8, p8_all = _all_gather(
        [a_w_in[0].astype(BF16), a_w_out[0].astype(BF16), b_w_in[0].astype(BF16), b_w_out[0].astype(BF16), p8_shard],
        "gather_weights")
    p8 = jnp.transpose(p8_all, (1, 0, 2)).reshape(SUBLANES, BW)
    gcat = jnp.concatenate([b_gate_a_w[0], b_gate_x_w[0]], axis=-1).astype(BF16)

    loss, gx, big, small = _local_step(
        x[0], loss_target[0], norm_w, win_a8, a_ln_w, a_ln_b, a_w_s[0], a_b_s[0].T,
        wout_a8.reshape(AW, D), win_b8, p8, gcat, wout_b8.reshape(BW, D), norm_f_w.reshape(1, D))

    g_win_a = _reduce_scatter(big[0], "reduce_a_in")
    g_wout_a = _reduce_scatter(big[1], "reduce_a_out")
    g_win_b = _reduce_scatter(big[2], "reduce_b_in")
    g_wout_b = _reduce_scatter(big[3], "reduce_b_out")

    sizes = {k: small[k].size for k in _SMALL_ORDER}
    flat = jnp.concatenate([small[k].reshape(-1) for k in _SMALL_ORDER])
    flat = jnp.pad(flat, (0, NDEV * PACK_ROWS * 128 - flat.shape[0]))
    mine = _reduce_scatter(flat.reshape(NDEV, PACK_ROWS, 128), "reduce_small")
    (packed,) = _all_gather([mine], "gather_small")
    packed = packed.reshape(-1)
    red, off = {}, 0
    for k in _SMALL_ORDER:
        red[k] = packed[off:off + sizes[k]].reshape(small[k].shape)
        off += sizes[k]
    g_p8 = lax.dynamic_slice_in_dim(red["p8"], me * (BW // NDEV), BW // NDEV, axis=1)

    loss = lax.psum(loss, ("x", "y", "c"))

    grads = dict(
        norm_w=red["norm_w"], a_w_in=g_win_a[None], a_ln_w=red["ln_w"], a_ln_b=red["ln_b"], a_w_s=red["w_s"][None],
        a_b_s=red["b_s"][None], a_w_out=g_wout_a[None], b_w_in=g_win_b[None], b_conv_w=g_p8[None, 0:4],
        b_conv_b=g_p8[4:5], b_gate_a_w=red["gate_a"][None], b_gate_a_b=g_p8[5:6], b_gate_x_w=red["gate_x"][None],
        b_gate_x_b=g_p8[6:7], b_lambda=g_p8[7:8], b_w_out=g_wout_b[None], norm_f_w=red["norm_f"].reshape(D))
    weights = dict(norm_w=norm_w, a_w_in=a_w_in, a_ln_w=a_ln_w, a_ln_b=a_ln_b, a_w_s=a_w_s, a_b_s=a_b_s, a_w_out=a_w_out,
                   b_w_in=b_w_in, b_conv_w=b_conv_w, b_conv_b=b_conv_b, b_gate_a_w=b_gate_a_w, b_gate_a_b=b_gate_a_b,
                   b_gate_x_w=b_gate_x_w, b_gate_x_b=b_gate_x_b, b_lambda=b_lambda, b_w_out=b_w_out, norm_f_w=norm_f_w)
    mom1 = dict(norm_w=m_norm_w, a_w_in=m_a_w_in, a_ln_w=m_a_ln_w, a_ln_b=m_a_ln_b, a_w_s=m_a_w_s, a_b_s=m_a_b_s,
                a_w_out=m_a_w_out, b_w_in=m_b_w_in, b_conv_w=m_b_conv_w, b_conv_b=m_b_conv_b, b_gate_a_w=m_b_gate_a_w,
                b_gate_a_b=m_b_gate_a_b, b_gate_x_w=m_b_gate_x_w, b_gate_x_b=m_b_gate_x_b, b_lambda=m_b_lambda,
                b_w_out=m_b_w_out, norm_f_w=m_norm_f_w)
    mom2 = dict(norm_w=v_norm_w, a_w_in=v_a_w_in, a_ln_w=v_a_ln_w, a_ln_b=v_a_ln_b, a_w_s=v_a_w_s, a_b_s=v_a_b_s,
                a_w_out=v_a_w_out, b_w_in=v_b_w_in, b_conv_w=v_b_conv_w, b_conv_b=v_b_conv_b, b_gate_a_w=v_b_gate_a_w,
                b_gate_a_b=v_b_gate_a_b, b_gate_x_w=v_b_gate_x_w, b_gate_x_b=v_b_gate_x_b, b_lambda=v_b_lambda,
                b_w_out=v_b_w_out, norm_f_w=v_norm_f_w)
    names = list(weights)
    big_names = ("a_w_in", "a_w_out", "b_w_in", "b_w_out")

    def as2d(a):
        return a.reshape(-1, a.shape[-1])

    upd = {}
    for k in big_names:
        upd[k] = _adam_big(as2d(weights[k]), as2d(grads[k]), as2d(mom1[k]), as2d(mom2[k]), "adam_" + k)
    small_names = [k for k in names if k not in big_names]
    res = _adam_small([(as2d(weights[k]), as2d(grads[k]), as2d(mom1[k]), as2d(mom2[k])) for k in small_names])
    for k, r3 in zip(small_names, res):
        upd[k] = r3
    deltas = [upd[k][0].reshape(weights[k].shape) for k in names]
    new_m = [upd[k][1].reshape(weights[k].shape) for k in names]
    new_v = [upd[k][2].reshape(weights[k].shape) for k in names]
    return (loss, gx[None], *[grads[k] for k in names], *deltas, *new_m, *new_v)
```

```python
import jax
import jax.numpy as jnp
from jax import lax
from jax.experimental import pallas as pl
from jax.experimental.pallas import tpu as pltpu

F32 = jnp.float32
BF16 = jnp.bfloat16
MESH = pl.DeviceIdType.MESH

NDEV = 8
NCHIP_OTHER = 3
D = 1024
AW = 2048
G = 8
GD = AW // G
CH = 128
BW = 1536
BH = 12
HD = BW // BH
CA = 3 * AW // NDEV
CB = 2 * BW // NDEV
RMS_EPS = 1e-6
LN_EPS = 1e-5
RG_C = 8.0
LR, B1, B2, ADAM_EPS, WD, STEP = 0.001, 0.9, 0.999, 1e-08, 0.01, 10
V7X_VMEM_BYTES = 64 * 1024 * 1024
VMEM_LIMIT = V7X_VMEM_BYTES - 8 * 1024 * 1024
SUBLANES = 8
LANES = 128
GELU_C = 0.7978845608028654
GELU_K = 0.044715

_VMEM = pl.BlockSpec(memory_space=pltpu.VMEM)
_HBM = pl.BlockSpec(memory_space=pltpu.HBM)


def _params(**kw):
    return pltpu.CompilerParams(vmem_limit_bytes=VMEM_LIMIT, **kw)


def _gelu_t(z):
    t = jnp.tanh(GELU_C * (z + GELU_K * (z * z * z)))
    return 0.5 * z * (1.0 + t), t


def _dgelu(z, t):
    return 0.5 * (1.0 + t) + 0.5 * z * (1.0 - t * t) * (GELU_C * (1.0 + 3.0 * GELU_K * z * z))


def _sigmoid(v):
    return 1.0 / (1.0 + jnp.exp(-v))


def _softplus_neg(lam):
    return jnp.maximum(-lam, 0.0) + jnp.log1p(jnp.exp(-jnp.abs(lam)))


def _dot(a, b):
    return jnp.dot(a, b, preferred_element_type=F32)


def _dot_nt(a, b):
    return lax.dot_general(a, b, (((1,), (1,)), ((), ())), preferred_element_type=F32)


def _rowsum(v):
    return jnp.sum(v, axis=0, keepdims=True)


def _causal_mask():
    r = lax.broadcasted_iota(jnp.int32, (CH, CH), 0)
    c = lax.broadcasted_iota(jnp.int32, (CH, CH), 1)
    return r >= c


def _rms(x):
    return lax.rsqrt(jnp.mean(x * x, axis=-1, keepdims=True) + RMS_EPS)


def _rms_bwd(dh, x, r, nw):
    gy = dh * nw
    return r * gy - x * (r * r * r) * jnp.mean(gy * x, axis=-1, keepdims=True)


def _place():
    return lax.axis_index("x"), lax.axis_index("y"), lax.axis_index("c")


def _other_chips(x, y):
    return [(1 - x, y), (x, 1 - y), (1 - x, 1 - y)]


def _gather_ops(ins, outs, send_sems, recv_sems, local_sems):
    n = len(ins)
    x, y, c = _place()
    me = 4 * x + 2 * y + c
    sibling = (x, y, 1 - c)
    chips = _other_chips(x, y)

    def copy(i, k, block, to, src=None):
        dst = outs[i].at[block]
        return pltpu.make_async_remote_copy(
            src_ref=dst if src is None else src, dst_ref=dst,
            send_sem=send_sems.at[k, i], recv_sem=recv_sems.at[k, i], device_id=to, device_id_type=MESH)

    def first_copies():
        mine = [pltpu.make_async_copy(ins[i], outs[i].at[me], local_sems.at[i]) for i in range(n)]
        first = []
        for i in range(n):
            first.append(copy(i, 0, me, sibling, src=ins[i]))
            for j, chip in enumerate(chips):
                first.append(copy(i, 1 + j, me, (*chip, c), src=ins[i]))
        return mine, first

    def start():
        mine, first = first_copies()
        for cp in mine + first:
            cp.start()

    def finish():
        mine, first = first_copies()
        passed = []
        for i in range(n):
            for j, (px, py) in enumerate(chips):
                blk = 4 * px + 2 * py + c
                copy(i, 1 + j, blk, sibling).wait_recv()
                fwd = copy(i, 4 + j, blk, sibling)
                fwd.start()
                passed.append(fwd)
        for i in range(n):
            copy(i, 0, 4 * x + 2 * y + (1 - c), sibling).wait_recv()
            for j, (px, py) in enumerate(chips):
                copy(i, 4 + j, 4 * px + 2 * py + (1 - c), sibling).wait_recv()
        for cp in first + passed:
            cp.wait_send()
        for cp in mine:
            cp.wait()

    return start, finish


def _gather_sems(n):
    return [pltpu.SemaphoreType.DMA((7, n)), pltpu.SemaphoreType.DMA((7, n)), pltpu.SemaphoreType.DMA((n,))]


def _gathered_shapes(shards):
    return [jax.ShapeDtypeStruct((NDEV,) + s.shape, s.dtype) for s in shards]


def _exchange_ops(srcs, dsts, send_sems, recv_sems):
    n = len(srcs)
    x, y, c = _place()
    chips = _other_chips(x, y)

    def copies():
        return [pltpu.make_async_remote_copy(
            src_ref=srcs[i].at[j], dst_ref=dsts[i].at[j], send_sem=send_sems.at[j, i], recv_sem=recv_sems.at[j, i],
            device_id=(*chips[j], c), device_id_type=MESH) for i in range(n) for j in range(NCHIP_OTHER)]

    def start():
        for cp in copies():
            cp.start()

    def finish():
        cps = copies()
        for cp in cps:
            cp.wait_recv()
        for cp in cps:
            cp.wait_send()

    return start, finish


def _exchange_sems(n):
    return [pltpu.SemaphoreType.DMA((NCHIP_OTHER, n)), pltpu.SemaphoreType.DMA((NCHIP_OTHER, n))]


def _all_gather(shards, name):
    n = len(shards)

    def body(*refs):
        start, finish = _gather_ops(refs[:n], refs[n:2 * n], *refs[2 * n:])
        start()
        finish()

    return pl.pallas_call(
        body, name=name, in_specs=[_HBM] * n, out_specs=[_HBM] * n, out_shape=_gathered_shapes(shards),
        scratch_shapes=_gather_sems(n), compiler_params=_params(),
    )(*shards)


def _reduce_in_chip(ps, name):
    n = len(ps)

    def body(*refs):
        p_refs, q_refs, acc_refs = refs[:n], refs[n:2 * n], refs[2 * n:3 * n]
        rest = refs[3 * n:]
        mines, lands = rest[:n], rest[n:2 * n]
        send_sems, recv_sems, local_sems = rest[2 * n:]
        x, y, c = _place()
        sibling = (x, y, 1 - c)
        pairs = []
        for i in range(n):
            for px in range(2):
                for py in range(2):
                    pi = 2 * px + py
                    loc = pltpu.make_async_copy(p_refs[i].at[4 * px + 2 * py + c], mines[i].at[pi],
                                                local_sems.at[pi, i])
                    cp = pltpu.make_async_remote_copy(
                        src_ref=p_refs[i].at[4 * px + 2 * py + (1 - c)], dst_ref=lands[i].at[pi],
                        send_sem=send_sems.at[pi, i], recv_sem=recv_sems.at[pi, i],
                        device_id=sibling, device_id_type=MESH)
                    loc.start()
                    cp.start()
                    pairs.append((loc, cp))
        for loc, cp in pairs:
            loc.wait()
            cp.wait_recv()
        for i in range(n):
            for j, (qx, qy) in enumerate(_other_chips(x, y)):
                qi = 2 * qx + qy
                q_refs[i][j] = (mines[i][qi].astype(F32) + lands[i][qi].astype(F32)).astype(q_refs[i].dtype)
            mi = 2 * x + y
            acc_refs[i][...] = mines[i][mi].astype(F32) + lands[i][mi].astype(F32)
        for _, cp in pairs:
            cp.wait_send()

    blk = [p.shape[1:] for p in ps]
    return pl.pallas_call(
        body, name=name, in_specs=[_HBM] * n, out_specs=[_VMEM] * (2 * n),
        out_shape=[jax.ShapeDtypeStruct((NCHIP_OTHER,) + b, p.dtype) for b, p in zip(blk, ps)]
        + [jax.ShapeDtypeStruct(b, F32) for b in blk],
        scratch_shapes=[pltpu.VMEM((4,) + b, p.dtype) for b, p in zip(blk, ps)]
        + [pltpu.VMEM((4,) + b, p.dtype) for b, p in zip(blk, ps)]
        + [pltpu.SemaphoreType.DMA((4, n)), pltpu.SemaphoreType.DMA((4, n)), pltpu.SemaphoreType.DMA((4, n))],
        compiler_params=_params(),
    )(*ps)


def _exchange(qs, name):
    n = len(qs)

    def body(*refs):
        start, finish = _exchange_ops(refs[:n], refs[n:2 * n], *refs[2 * n:])
        start()
        finish()

    return pl.pallas_call(
        body, name=name, in_specs=[_HBM] * n, out_specs=[_HBM] * n,
        out_shape=[jax.ShapeDtypeStruct(q.shape, q.dtype) for q in qs],
        scratch_shapes=_exchange_sems(n), compiler_params=_params(),
    )(*qs)


def _sum_and_gather(acc, land, name):
    def body(acc_ref, land_ref, out_ref, mine_scr, *sems):
        mine_scr[...] = acc_ref[...] + land_ref[0] + land_ref[1] + land_ref[2]
        start, finish = _gather_ops([mine_scr], [out_ref], *sems)
        start()
        finish()

    return pl.pallas_call(
        body, name=name, in_specs=[_VMEM, _VMEM], out_specs=_HBM,
        out_shape=jax.ShapeDtypeStruct((NDEV,) + acc.shape, acc.dtype),
        scratch_shapes=[pltpu.VMEM(acc.shape, acc.dtype)] + _gather_sems(1), compiler_params=_params(),
    )(acc, land)


def _fwd_a(x, nw, win8, lnw, lnb, ws, bst, shards, *, tm):
    s_len = x.shape[0]
    nt = s_len // tm
    nch = tm // CH
    ng = len(shards)

    def body(*refs):
        x_ref, nw_ref, win_ref, lnw_ref, lnb_ref, ws_ref, bst_ref = refs[:7]
        sh_refs = refs[7:7 + ng]
        z_ref, h_ref, y_ref = refs[7 + ng:10 + ng]
        ga_refs = refs[10 + ng:10 + 2 * ng]
        wc_scr, gv_scr = refs[10 + 2 * ng:12 + 2 * ng]
        sems = refs[12 + 2 * ng:]
        i = pl.program_id(0)
        start, finish = _gather_ops(sh_refs, ga_refs, *sems)

        @pl.when(i == 0)
        def _():
            start()
            m = _causal_mask()
            for g in range(G):
                wc_scr[g] = jnp.where(m, ws_ref[g], 0.0).astype(BF16)

        x = x_ref[...]
        h = (x * _rms(x) * nw_ref[...]).astype(BF16)
        h_ref[...] = h
        for k in range(NDEV):
            z_ref[:, k * CA:(k + 1) * CA] = _dot(h, win_ref[k])

        ssum = jnp.zeros((tm, 1), F32)
        for g in range(G):
            gv = _gelu_t(z_ref[:, AW + g * GD:AW + (g + 1) * GD])[0]
            gv_scr[:, g * GD:(g + 1) * GD] = gv
            ssum = ssum + jnp.sum(gv, axis=-1, keepdims=True)
        mu = ssum * (1.0 / AW)
        vsum = jnp.zeros((tm, 1), F32)
        for g in range(G):
            dlt = gv_scr[:, g * GD:(g + 1) * GD] - mu
            vsum = vsum + jnp.sum(dlt * dlt, axis=-1, keepdims=True)
        rstd = lax.rsqrt(vsum * (1.0 / AW) + LN_EPS)

        for g in range(G):
            cs = slice(g * GD, (g + 1) * GD)
            v = (gv_scr[:, cs] - mu) * rstd * lnw_ref[:, cs] + lnb_ref[:, cs]
            vb = v.astype(BF16)
            u = _gelu_t(z_ref[:, cs])[0]
            zg = z_ref[:, 2 * AW + g * GD:2 * AW + (g + 1) * GD]
            sg = zg * _sigmoid(zg)
            for n in range(nch):
                rs = slice(n * CH, (n + 1) * CH)
                s = _dot(wc_scr[g], vb[rs, :]) + bst_ref[:, g:g + 1]
                y_ref[rs, cs] = (u[rs, :] * s * sg[rs, :]).astype(BF16)

        @pl.when(i == nt - 1)
        def _():
            finish()

    tile = lambda w: pl.BlockSpec((tm, w), lambda i: (i, 0))
    res = pl.pallas_call(
        body, name="fwd_a", grid=(nt,),
        in_specs=[tile(D), _VMEM, _VMEM, _VMEM, _VMEM, _VMEM, _VMEM] + [_HBM] * ng,
        out_specs=[tile(3 * AW), tile(D), tile(AW)] + [_HBM] * ng,
        out_shape=[jax.ShapeDtypeStruct((s_len, 3 * AW), F32), jax.ShapeDtypeStruct((s_len, D), BF16),
                   jax.ShapeDtypeStruct((s_len, AW), BF16)] + _gathered_shapes(shards),
        scratch_shapes=[pltpu.VMEM((G, CH, CH), BF16), pltpu.VMEM((tm, AW), F32)] + _gather_sems(ng),
        compiler_params=_params(dimension_semantics=("arbitrary",)),
    )(x, nw, win8, lnw, lnb, ws, bst, *shards)
    return res[0], res[1], res[2], res[3:]


def _bwd_a(dx1, z, lnw, lnb, ws, bst, wout, qs, *, tm):
    s_len = dx1.shape[0]
    nt = s_len // tm
    nch = tm // CH
    nq = len(qs)

    def body(*refs):
        dx1_ref, z_ref, lnw_ref, lnb_ref, ws_ref, bst_ref, wout_ref = refs[:7]
        q_refs = refs[7:7 + nq]
        dz_ref, glnw_ref, glnb_ref, gws_ref, gbst_ref = refs[7 + nq:12 + nq]
        land_refs = refs[12 + nq:12 + 2 * nq]
        (wc_scr, wct_scr, vh_scr, dgv_scr, dy_scr, dv_scr, gbs_acc, gwc_acc) = refs[12 + 2 * nq:20 + 2 * nq]
        sems = refs[20 + 2 * nq:]
        i = pl.program_id(0)
        start, finish = _exchange_ops(q_refs, land_refs, *sems)

        @pl.when(i == 0)
        def _():
            start()
            m = _causal_mask()
            for g in range(G):
                wm = jnp.where(m, ws_ref[g], 0.0)
                wc_scr[g] = wm.astype(BF16)
                wct_scr[g] = wm.T.astype(BF16)
            glnw_ref[...] = jnp.zeros_like(glnw_ref)
            glnb_ref[...] = jnp.zeros_like(glnb_ref)
            gbs_acc[...] = jnp.zeros_like(gbs_acc)
            gwc_acc[...] = jnp.zeros_like(gwc_acc)

        dy_scr[...] = _dot_nt(dx1_ref[...], wout_ref[...])

        ssum = jnp.zeros((tm, 1), F32)
        for g in range(G):
            cs = slice(g * GD, (g + 1) * GD)
            zv = z_ref[:, AW + g * GD:AW + (g + 1) * GD]
            gv, t = _gelu_t(zv)
            vh_scr[:, cs] = gv
            dgv_scr[:, cs] = _dgelu(zv, t)
            ssum = ssum + jnp.sum(gv, axis=-1, keepdims=True)
        mu = ssum * (1.0 / AW)
        vsum = jnp.zeros((tm, 1), F32)
        for g in range(G):
            dlt = vh_scr[:, g * GD:(g + 1) * GD] - mu
            vsum = vsum + jnp.sum(dlt * dlt, axis=-1, keepdims=True)
        rstd = lax.rsqrt(vsum * (1.0 / AW) + LN_EPS)

        m1 = jnp.zeros((tm, 1), F32)
        m2 = jnp.zeros((tm, 1), F32)
        for g in range(G):
            cs = slice(g * GD, (g + 1) * GD)
            gs = slice(2 * AW + g * GD, 2 * AW + (g + 1) * GD)
            vhat = (vh_scr[:, cs] - mu) * rstd
            vh_scr[:, cs] = vhat
            vb = (vhat * lnw_ref[:, cs] + lnb_ref[:, cs]).astype(BF16)
            zu = z_ref[:, cs]
            u, tu = _gelu_t(zu)
            zg = z_ref[:, gs]
            sig = _sigmoid(zg)
            sg = zg * sig
            dy = dy_scr[:, cs]
            dsf = dy * u * sg
            dsb = dsf.astype(BF16)
            dvs = []
            for n in range(nch):
                rs = slice(n * CH, (n + 1) * CH)
                s = _dot(wc_scr[g], vb[rs, :]) + bst_ref[:, g:g + 1]
                dys = dy[rs, :] * s
                dz_ref[rs, cs] = (dys * sg[rs, :] * _dgelu(zu[rs, :], tu[rs, :])).astype(BF16)
                dz_ref[rs, gs] = (dys * u[rs, :] * (sig[rs, :] * (1.0 + zg[rs, :] * (1.0 - sig[rs, :])))).astype(BF16)
                gbs_acc[g] += dsf[rs, :]
                gwc_acc[g] += _dot_nt(dsb[rs, :], vb[rs, :])
                dvs.append(_dot(wct_scr[g], dsb[rs, :]))
            dv = jnp.concatenate(dvs, axis=0) if nch > 1 else dvs[0]
            glnw_ref[:, cs] += _rowsum(dv * vhat)
            glnb_ref[:, cs] += _rowsum(dv)
            dvh = dv * lnw_ref[:, cs]
            dv_scr[:, cs] = dvh
            m1 = m1 + jnp.sum(dvh, axis=-1, keepdims=True)
            m2 = m2 + jnp.sum(dvh * vhat, axis=-1, keepdims=True)
        m1 = m1 * (1.0 / AW)
        m2 = m2 * (1.0 / AW)
        for g in range(G):
            cs = slice(g * GD, (g + 1) * GD)
            dgv = rstd * (dv_scr[:, cs] - m1 - vh_scr[:, cs] * m2)
            dz_ref[:, AW + g * GD:AW + (g + 1) * GD] = (dgv * dgv_scr[:, cs]).astype(BF16)

        @pl.when(i == nt - 1)
        def _():
            m = _causal_mask()
            for g in range(G):
                gws_ref[g] = jnp.where(m, gwc_acc[g], 0.0)
                gbst_ref[:, g:g + 1] = jnp.sum(gbs_acc[g], axis=-1, keepdims=True)
            finish()

    tile = lambda w: pl.BlockSpec((tm, w), lambda i: (i, 0))
    whole = lambda *s: pl.BlockSpec(s, lambda i: (0,) * len(s))
    res = pl.pallas_call(
        body, name="bwd_a", grid=(nt,),
        in_specs=[tile(D), tile(3 * AW), _VMEM, _VMEM, _VMEM, _VMEM, _VMEM] + [_HBM] * nq,
        out_specs=[tile(3 * AW), whole(1, AW), whole(1, AW), whole(G, CH, CH), whole(CH, G)] + [_HBM] * nq,
        out_shape=[jax.ShapeDtypeStruct((s_len, 3 * AW), BF16), jax.ShapeDtypeStruct((1, AW), F32),
                   jax.ShapeDtypeStruct((1, AW), F32), jax.ShapeDtypeStruct((G, CH, CH), F32),
                   jax.ShapeDtypeStruct((CH, G), F32)] + [jax.ShapeDtypeStruct(q.shape, q.dtype) for q in qs],
        scratch_shapes=[pltpu.VMEM((G, CH, CH), BF16), pltpu.VMEM((G, CH, CH), BF16),
                        pltpu.VMEM((tm, AW), F32), pltpu.VMEM((tm, AW), F32),
                        pltpu.VMEM((tm, AW), F32), pltpu.VMEM((tm, AW), F32),
                        pltpu.VMEM((G, CH, GD), F32), pltpu.VMEM((G, CH, CH), F32)] + _exchange_sems(nq),
        compiler_params=_params(dimension_semantics=("arbitrary",)),
    )(dx1, z, lnw, lnb, ws, bst, wout, *qs)
    return res[0], res[1], res[2], res[3], res[4], res[5:]


def _bwd_a_in(dz, dx1, x, nw, win8, qs, *, tm):
    s_len = x.shape[0]
    nt = s_len // tm
    nq = len(qs)

    def body(*refs):
        dz_ref, dx1_ref, x_ref, nw_ref, win_ref = refs[:5]
        q_refs = refs[5:5 + nq]
        gx_ref, gnw_ref = refs[5 + nq:7 + nq]
        land_refs = refs[7 + nq:7 + 2 * nq]
        sems = refs[7 + 2 * nq:]
        i = pl.program_id(0)
        start, finish = _exchange_ops(q_refs, land_refs, *sems)

        @pl.when(i == 0)
        def _():
            start()
            gnw_ref[...] = jnp.zeros_like(gnw_ref)

        dh = jnp.zeros((tm, D), F32)
        for k in range(NDEV):
            dh = dh + _dot_nt(dz_ref[:, k * CA:(k + 1) * CA], win_ref[k])
        x = x_ref[...]
        r = _rms(x)
        gx_ref[...] = dx1_ref[...] + _rms_bwd(dh, x, r, nw_ref[...])
        gnw_ref[...] += _rowsum(dh * x * r)

        @pl.when(i == nt - 1)
        def _():
            finish()

    tile = lambda w: pl.BlockSpec((tm, w), lambda i: (i, 0))
    res = pl.pallas_call(
        body, name="bwd_a_in", grid=(nt,),
        in_specs=[tile(3 * AW), tile(D), tile(D), _VMEM, _VMEM] + [_HBM] * nq,
        out_specs=[tile(D), pl.BlockSpec((1, D), lambda i: (0, 0))] + [_HBM] * nq,
        out_shape=[jax.ShapeDtypeStruct((s_len, D), F32), jax.ShapeDtypeStruct((1, D), F32)]
        + [jax.ShapeDtypeStruct(q.shape, q.dtype) for q in qs],
        scratch_shapes=_exchange_sems(nq),
        compiler_params=_params(dimension_semantics=("arbitrary",)),
    )(dz, dx1, x, nw, win8, *qs)
    return res[0], res[1], res[2:]


def _conv(p8_ref, cs, xb, xm1, xm2, xm3):
    xc = p8_ref[4:5, cs] + p8_ref[3:4, cs] * xb
    xc = xc + p8_ref[0:1, cs] * xm3
    xc = xc + p8_ref[1:2, cs] * xm2
    return xc + p8_ref[2:3, cs] * xm1


def _gates(p8_ref, gcat_ref, hh, xc):
    cs = slice(hh * HD, (hh + 1) * HD)
    pre = _dot(xc.astype(BF16), gcat_ref[hh])
    r = _sigmoid(pre[:, :HD] + p8_ref[5:6, cs])
    ig = _sigmoid(pre[:, HD:] + p8_ref[6:7, cs])
    sp = _softplus_neg(p8_ref[7:8, cs])
    la = (-RG_C) * r * sp
    a = jnp.exp(la)
    mult = jnp.sqrt(jnp.tanh(-la) * (1.0 + a * a))
    return r, ig, sp, a, mult


def _scan_rows(a_ref, b_ref, out_ref, carry, tm, reverse):
    row = lax.broadcasted_iota(jnp.int32, (SUBLANES, BW), 0)
    ngrp = tm // SUBLANES

    def step(j, cr):
        jj = (ngrp - 1 - j) if reverse else j
        off = pl.multiple_of(jj * SUBLANES, SUBLANES)
        a = a_ref[pl.ds(off, SUBLANES), :]
        b = b_ref[pl.ds(off, SUBLANES), :]
        for sh in (1, 2, 4):
            if reverse:
                a_s = pltpu.roll(a, SUBLANES - sh, 0)
                b_s = pltpu.roll(b, SUBLANES - sh, 0)
                m = row < SUBLANES - sh
            else:
                a_s = pltpu.roll(a, sh, 0)
                b_s = pltpu.roll(b, sh, 0)
                m = row >= sh
            b = jnp.where(m, a * b_s + b, b)
            a = jnp.where(m, a * a_s, a)
        o = b + a * cr
        out_ref[pl.ds(off, SUBLANES), :] = o
        return o[0:1, :] if reverse else o[SUBLANES - 1:SUBLANES, :]

    return lax.fori_loop(0, ngrp, step, carry)


def _fwd_b(x, ya, wout_a, nw, win8, p8, gcat, shards, *, tm):
    s_len = x.shape[0]
    nt = s_len // tm
    ng = len(shards)

    def body(*refs):
        x_ref, ya_ref, wouta_ref, nw_ref, win_ref, p8_ref, gcat_ref = refs[:7]
        sh_refs = refs[7:7 + ng]
        x1_ref, zb_ref, hs_ref, h1_ref, yb_ref = refs[7 + ng:12 + ng]
        ga_refs = refs[12 + ng:12 + 2 * ng]
        xbe_scr, a_scr, b_scr, carry_scr = refs[12 + 2 * ng:16 + 2 * ng]
        sems = refs[16 + 2 * ng:]
        i = pl.program_id(0)
        start, finish = _gather_ops(sh_refs, ga_refs, *sems)

        @pl.when(i == 0)
        def _():
            start()
            xbe_scr[0:SUBLANES, :] = jnp.zeros((SUBLANES, BW), F32)
            carry_scr[...] = jnp.zeros_like(carry_scr)

        x1 = x_ref[...] + _dot(ya_ref[...], wouta_ref[...])
        x1_ref[...] = x1
        h = (x1 * _rms(x1) * nw_ref[...]).astype(BF16)
        h1_ref[...] = h
        for k in range(NDEV):
            zb_ref[:, k * CB:(k + 1) * CB] = _dot(h, win_ref[k])
        xbe_scr[SUBLANES:SUBLANES + tm, :] = zb_ref[:, :BW]
        for hh in range(BH):
            cs = slice(hh * HD, (hh + 1) * HD)
            xc = _conv(p8_ref, cs, xbe_scr[SUBLANES:SUBLANES + tm, cs], xbe_scr[7:7 + tm, cs],
                       xbe_scr[6:6 + tm, cs], xbe_scr[5:5 + tm, cs])
            _, ig, _, a, mult = _gates(p8_ref, gcat_ref, hh, xc)
            a_scr[:, cs] = a
            b_scr[:, cs] = mult * (ig * xc)
        xbe_scr[0:SUBLANES, :] = xbe_scr[tm:tm + SUBLANES, :]
        carry_scr[...] = _scan_rows(a_scr, b_scr, hs_ref, carry_scr[...], tm, False)
        for hh in range(BH):
            cs = slice(hh * HD, (hh + 1) * HD)
            gt = zb_ref[:, BW + hh * HD:BW + (hh + 1) * HD]
            yb_ref[:, cs] = (hs_ref[:, cs] * (gt * _sigmoid(gt))).astype(BF16)

        @pl.when(i == nt - 1)
        def _():
            finish()

    tile = lambda w: pl.BlockSpec((tm, w), lambda i: (i, 0))
    res = pl.pallas_call(
        body, name="fwd_b", grid=(nt,),
        in_specs=[tile(D), tile(AW), _VMEM, _VMEM, _VMEM, _VMEM, _VMEM] + [_HBM] * ng,
        out_specs=[tile(D), tile(2 * BW), tile(BW), tile(D), tile(BW)] + [_HBM] * ng,
        out_shape=[jax.ShapeDtypeStruct((s_len, D), F32), jax.ShapeDtypeStruct((s_len, 2 * BW), F32),
                   jax.ShapeDtypeStruct((s_len, BW), F32), jax.ShapeDtypeStruct((s_len, D), BF16),
                   jax.ShapeDtypeStruct((s_len, BW), BF16)] + _gathered_shapes(shards),
        scratch_shapes=[pltpu.VMEM((tm + SUBLANES, BW), F32), pltpu.VMEM((tm, BW), F32),
                        pltpu.VMEM((tm, BW), F32), pltpu.VMEM((1, BW), F32)] + _gather_sems(ng),
        compiler_params=_params(dimension_semantics=("arbitrary",)),
    )(x, ya, wout_a, nw, win8, p8, gcat, *shards)
    return res[0], res[1], res[2], res[3], res[4], res[5:]


def _head(x1, yb, wout, nfw, tgt, *, tm):
    s_len = x1.shape[0]

    def body(x1_ref, yb_ref, wout_ref, nfw_ref, t_ref, dx2_ref, dx2b_ref, loss_ref, gnfw_ref):
        @pl.when(pl.program_id(0) == 0)
        def _():
            loss_ref[...] = jnp.zeros_like(loss_ref)
            gnfw_ref[...] = jnp.zeros_like(gnfw_ref)

        x2 = x1_ref[...] + _dot(yb_ref[...], wout_ref[...])
        rf = _rms(x2)
        xn = x2 * rf
        e = xn * nfw_ref[...] - t_ref[...]
        loss_ref[...] += (0.5 / D) * jnp.sum(jnp.sum(e * e, axis=-1, keepdims=True), axis=0, keepdims=True)
        dyf = e * (1.0 / D)
        gnfw_ref[...] += _rowsum(dyf * xn)
        dx2 = _rms_bwd(dyf, x2, rf, nfw_ref[...])
        dx2_ref[...] = dx2
        dx2b_ref[...] = dx2.astype(BF16)

    tile = lambda w: pl.BlockSpec((tm, w), lambda i: (i, 0))
    whole = lambda *s: pl.BlockSpec(s, lambda i: (0,) * len(s))
    return pl.pallas_call(
        body, name="head", grid=(s_len // tm,),
        in_specs=[tile(D), tile(BW), _VMEM, _VMEM, tile(D)],
        out_specs=[tile(D), tile(D), whole(1, 1), whole(1, D)],
        out_shape=[jax.ShapeDtypeStruct((s_len, D), F32), jax.ShapeDtypeStruct((s_len, D), BF16),
                   jax.ShapeDtypeStruct((1, 1), F32), jax.ShapeDtypeStruct((1, D), F32)],
        compiler_params=_params(dimension_semantics=("arbitrary",)),
    )(x1, yb, wout, nfw, tgt)


def _bwd_b(dx2, zb, hs, x1, nw, win8, p8, gcat, wout, *, tm):
    s_len = x1.shape[0]
    nt = s_len // tm
    per = tm // SUBLANES

    def body(dx2_ref, zb_ref, zbp_ref, hs_ref, hsp_ref, x1_ref, nw_ref, win_ref, p8_ref, gcat_ref, wout_ref,
             dx1_ref, dx1b_ref, dzb_ref, gp8_ref, gg_ref, gnw_ref,
             xbe_scr, hse_scr, ae_scr, an_scr, r_scr, i_scr, m_scr, xc_scr, dhd_scr, dh_scr, dy_scr, dxce_scr,
             carry_scr, afirst_scr):
        i = pl.program_id(0)
        ti = nt - 1 - i

        @pl.when(i == 0)
        def _():
            gp8_ref[...] = jnp.zeros_like(gp8_ref)
            gg_ref[...] = jnp.zeros_like(gg_ref)
            gnw_ref[...] = jnp.zeros_like(gnw_ref)
            dxce_scr[tm:tm + SUBLANES, :] = jnp.zeros((SUBLANES, BW), F32)
            carry_scr[...] = jnp.zeros_like(carry_scr)
            afirst_scr[...] = jnp.zeros_like(afirst_scr)

        has_prev = (ti > 0).astype(F32)
        xbe_scr[0:SUBLANES, :] = zbp_ref[:, :BW] * has_prev
        xbe_scr[SUBLANES:SUBLANES + tm, :] = zb_ref[:, :BW]
        hse_scr[0:SUBLANES, :] = hsp_ref[...] * has_prev
        hse_scr[SUBLANES:SUBLANES + tm, :] = hs_ref[...]

        dx2 = dx2_ref[...]
        dy_scr[...] = _dot_nt(dx2.astype(BF16), wout_ref[...])

        for hh in range(BH):
            cs = slice(hh * HD, (hh + 1) * HD)
            xc = _conv(p8_ref, cs, xbe_scr[SUBLANES:SUBLANES + tm, cs], xbe_scr[7:7 + tm, cs],
                       xbe_scr[6:6 + tm, cs], xbe_scr[5:5 + tm, cs])
            r, ig, _, a, mult = _gates(p8_ref, gcat_ref, hh, xc)
            xc_scr[:, cs] = xc
            r_scr[:, cs] = r
            i_scr[:, cs] = ig
            m_scr[:, cs] = mult
            ae_scr[0:tm, cs] = a
            gt = zb_ref[:, BW + hh * HD:BW + (hh + 1) * HD]
            sig = _sigmoid(gt)
            dy = dy_scr[:, cs]
            dhd_scr[:, cs] = dy * (gt * sig)
            dzb_ref[:, BW + hh * HD:BW + (hh + 1) * HD] = (
                dy * hs_ref[:, cs] * (sig * (1.0 + gt * (1.0 - sig)))).astype(BF16)
        ae_scr[tm:tm + SUBLANES, :] = jnp.broadcast_to(afirst_scr[...], (SUBLANES, BW))
        an_scr[...] = ae_scr[1:1 + tm, :]
        afirst_scr[...] = ae_scr[0:1, :]
        carry_scr[...] = _scan_rows(an_scr, dhd_scr, dh_scr, carry_scr[...], tm, True)

        for hh in range(BH):
            cs = slice(hh * HD, (hh + 1) * HD)
            dh = dh_scr[:, cs]
            a = ae_scr[0:tm, cs]
            mult = m_scr[:, cs]
            ig = i_scr[:, cs]
            r = r_scr[:, cs]
            xc = xc_scr[:, cs]
            lam = p8_ref[7:8, cs]
            sp = _softplus_neg(lam)
            da = dh * hse_scr[7:7 + tm, cs]
            dmult = dh * (ig * xc)
            dla = da * a - dmult * (a * a) / mult
            gp8_ref[7:8, cs] += _rowsum(dla * ((-RG_C) * r)) * (-_sigmoid(-lam))
            dpr = dla * ((-RG_C) * sp) * (r * (1.0 - r))
            dpi = dh * mult * xc * (ig * (1.0 - ig))
            gp8_ref[5:6, cs] += _rowsum(dpr)
            gp8_ref[6:7, cs] += _rowsum(dpi)
            dcat = jnp.concatenate([dpr, dpi], axis=1).astype(BF16)
            dxc = dh * mult * ig + _dot_nt(dcat, gcat_ref[hh])
            gg_ref[hh] += _dot(xc.T.astype(BF16), dcat)
            dxce_scr[0:tm, cs] = dxc
            gp8_ref[4:5, cs] += _rowsum(dxc)
            gp8_ref[3:4, cs] += _rowsum(dxc * xbe_scr[SUBLANES:SUBLANES + tm, cs])
            gp8_ref[2:3, cs] += _rowsum(dxc * xbe_scr[7:7 + tm, cs])
            gp8_ref[1:2, cs] += _rowsum(dxc * xbe_scr[6:6 + tm, cs])
            gp8_ref[0:1, cs] += _rowsum(dxc * xbe_scr[5:5 + tm, cs])
        for hh in range(BH):
            cs = slice(hh * HD, (hh + 1) * HD)
            dxb = p8_ref[3:4, cs] * dxce_scr[0:tm, cs]
            dxb = dxb + p8_ref[2:3, cs] * dxce_scr[1:1 + tm, cs]
            dxb = dxb + p8_ref[1:2, cs] * dxce_scr[2:2 + tm, cs]
            dxb = dxb + p8_ref[0:1, cs] * dxce_scr[3:3 + tm, cs]
            dzb_ref[:, cs] = dxb.astype(BF16)
        dxce_scr[tm:tm + SUBLANES, :] = dxce_scr[0:SUBLANES, :]

        dh1 = jnp.zeros((tm, D), F32)
        for k in range(NDEV):
            dh1 = dh1 + _dot_nt(dzb_ref[:, k * CB:(k + 1) * CB], win_ref[k])
        x1 = x1_ref[...]
        r1 = _rms(x1)
        dx1 = dx2 + _rms_bwd(dh1, x1, r1, nw_ref[...])
        dx1_ref[...] = dx1
        dx1b_ref[...] = dx1.astype(BF16)
        gnw_ref[...] += _rowsum(dh1 * x1 * r1)

    tile = lambda w: pl.BlockSpec((tm, w), lambda i: (nt - 1 - i, 0))
    prev = lambda w: pl.BlockSpec((SUBLANES, w), lambda i: (jnp.maximum((nt - 1 - i) * per - 1, 0), 0))
    whole = lambda *s: pl.BlockSpec(s, lambda i: (0,) * len(s))
    full = lambda: pltpu.VMEM((tm, BW), F32)
    ext = lambda: pltpu.VMEM((tm + SUBLANES, BW), F32)
    return pl.pallas_call(
        body, name="bwd_b", grid=(nt,),
        in_specs=[tile(D), tile(2 * BW), prev(2 * BW), tile(BW), prev(BW), tile(D),
                  _VMEM, _VMEM, _VMEM, _VMEM, _VMEM],
        out_specs=[tile(D), tile(D), tile(2 * BW), whole(SUBLANES, BW), whole(BH, HD, 2 * HD), whole(1, D)],
        out_shape=[jax.ShapeDtypeStruct((s_len, D), F32), jax.ShapeDtypeStruct((s_len, D), BF16),
                   jax.ShapeDtypeStruct((s_len, 2 * BW), BF16), jax.ShapeDtypeStruct((SUBLANES, BW), F32),
                   jax.ShapeDtypeStruct((BH, HD, 2 * HD), F32), jax.ShapeDtypeStruct((1, D), F32)],
        scratch_shapes=[ext(), ext(), ext(), full(), full(), full(), full(), full(), full(), full(), full(), ext(),
                        pltpu.VMEM((1, BW), F32), pltpu.VMEM((1, BW), F32)],
        compiler_params=_params(dimension_semantics=("arbitrary",)),
    )(dx2, zb, zb, hs, hs, x1, nw, win8, p8, gcat, wout)


def _transpose_into(dst_ref, src_ref, rows):
    s_len = src_ref.shape[0]
    for r0 in range(0, s_len, rows):
        dst_ref[:, r0:r0 + rows] = src_ref[r0:r0 + rows, :].astype(F32).T.astype(BF16)


def _wgrad_cols(a, b, qs, *, nblk, name):
    s_len, m = a.shape
    bn = b.shape[1] // nblk
    nq = len(qs)

    def body(*refs):
        a_ref, b_ref = refs[:2]
        q_refs = refs[2:2 + nq]
        o_ref = refs[2 + nq]
        land_refs = refs[3 + nq:3 + 2 * nq]
        at_scr = refs[3 + 2 * nq]
        sems = refs[4 + 2 * nq:]
        i = pl.program_id(0)
        if nq:
            start, finish = _exchange_ops(q_refs, land_refs, *sems)

        @pl.when(i == 0)
        def _():
            if nq:
                start()
            _transpose_into(at_scr, a_ref, 256)

        o_ref[0] = _dot(at_scr[...], b_ref[...]).astype(BF16)

        if nq:
            @pl.when(i == nblk - 1)
            def _():
                finish()

    res = pl.pallas_call(
        body, name=name, grid=(nblk,),
        in_specs=[_VMEM, pl.BlockSpec((s_len, bn), lambda j: (0, j))] + [_HBM] * nq,
        out_specs=[pl.BlockSpec((1, m, bn), lambda j: (j, 0, 0))] + [_HBM] * nq,
        out_shape=[jax.ShapeDtypeStruct((nblk, m, bn), BF16)] + [jax.ShapeDtypeStruct(q.shape, q.dtype) for q in qs],
        scratch_shapes=[pltpu.VMEM((m, s_len), BF16)] + (_exchange_sems(nq) if nq else []),
        compiler_params=_params(dimension_semantics=("arbitrary",)),
    )(a, b, *qs)
    return res[0], res[1:]


def _wgrad_rows(a, b, *, nblk, per, name):
    s_len, m = a.shape
    n = b.shape[1]
    rb = m // nblk
    bm = per * rb

    def body(a_ref, b_ref, o_ref, at_scr):
        _transpose_into(at_scr, a_ref, 256)
        res = _dot(at_scr[...], b_ref[...]).astype(BF16)
        for q in range(per):
            o_ref[q] = res[q * rb:(q + 1) * rb, :]

    return pl.pallas_call(
        body, name=name, grid=(nblk // per,),
        in_specs=[pl.BlockSpec((s_len, bm), lambda j: (0, j)), _VMEM],
        out_specs=pl.BlockSpec((per, rb, n), lambda j: (j, 0, 0)),
        out_shape=jax.ShapeDtypeStruct((nblk, rb, n), BF16),
        scratch_shapes=[pltpu.VMEM((bm, s_len), BF16)],
        compiler_params=_params(dimension_semantics=("arbitrary",)),
    )(a, b)


def _adam_math(w, g, m, v):
    m = B1 * m + (1.0 - B1) * g
    v = B2 * v + (1.0 - B2) * (g * g)
    m_hat = m / (1.0 - B1 ** STEP)
    v_hat = v / (1.0 - B2 ** STEP)
    delta = (-LR) * (m_hat / (jnp.sqrt(v_hat) + ADAM_EPS) + WD * w)
    return delta, m, v


def _adam_big(w, acc, land, m, v, name):
    r, cd = w.shape
    rb = 256 if r % 256 == 0 else r

    def body(w_ref, acc_ref, land_ref, m_ref, v_ref, g_ref, d_ref, mo_ref, vo_ref):
        g = acc_ref[...]
        for j in range(NCHIP_OTHER):
            g = g + land_ref[j].astype(F32)
        g_ref[...] = g
        d_ref[...], mo_ref[...], vo_ref[...] = _adam_math(w_ref[...], g, m_ref[...], v_ref[...])

    blk = pl.BlockSpec((rb, cd), lambda i: (i, 0))
    blk3 = pl.BlockSpec((NCHIP_OTHER, rb, cd), lambda i: (0, i, 0))
    return pl.pallas_call(
        body, name=name, grid=(r // rb,), in_specs=[blk, blk, blk3, blk, blk], out_specs=[blk] * 4,
        out_shape=[jax.ShapeDtypeStruct((r, cd), F32)] * 4,
        compiler_params=_params(dimension_semantics=("arbitrary",)),
    )(w, acc, land, m, v)


def _adam_small(groups):
    n = len(groups)

    def body(*refs):
        ins, outs = refs[:4 * n], refs[4 * n:]
        for k in range(n):
            w_ref, g_ref, m_ref, v_ref = ins[4 * k:4 * k + 4]
            d, mo, vo = _adam_math(w_ref[...], g_ref[...], m_ref[...], v_ref[...])
            outs[3 * k][...] = d
            outs[3 * k + 1][...] = mo
            outs[3 * k + 2][...] = vo

    flat = [a for grp in groups for a in grp]
    shapes = [jax.ShapeDtypeStruct(grp[0].shape, F32) for grp in groups for _ in range(3)]
    res = pl.pallas_call(
        body, name="adam_small", in_specs=[_VMEM] * (4 * n), out_specs=[_VMEM] * (3 * n), out_shape=shapes,
        compiler_params=_params(),
    )(*flat)
    return [tuple(res[3 * k:3 * k + 3]) for k in range(n)]


TM_FWD_A = 256
TM_BWD_A = 256
TM_BWD_A_IN = 256
TM_FWD_B = 256
TM_HEAD = 512
TM_BWD_B = 256
PACK_ROWS = 536
_SMALL_ORDER = ("norm_w", "ln_w", "ln_b", "w_s", "b_s", "gate_a", "gate_x", "norm_f", "p8")


def kernel(x, norm_w, a_w_in, a_ln_w, a_ln_b, a_w_s, a_b_s, a_w_out, b_w_in, b_conv_w, b_conv_b, b_gate_a_w, b_gate_a_b, b_gate_x_w, b_gate_x_b, b_lambda, b_w_out, norm_f_w, loss_target, m_norm_w, m_a_w_in, m_a_ln_w, m_a_ln_b, m_a_w_s, m_a_b_s, m_a_w_out, m_b_w_in, m_b_conv_w, m_b_conv_b, m_b_gate_a_w, m_b_gate_a_b, m_b_gate_x_w, m_b_gate_x_b, m_b_lambda, m_b_w_out, m_norm_f_w, v_norm_w, v_a_w_in, v_a_ln_w, v_a_ln_b, v_a_w_s, v_a_b_s, v_a_w_out, v_b_w_in, v_b_conv_w, v_b_conv_b, v_b_gate_a_w, v_b_gate_a_b, v_b_gate_x_w, v_b_gate_x_b, v_b_lambda, v_b_w_out, v_norm_f_w):
    me = 4 * lax.axis_index("x") + 2 * lax.axis_index("y") + lax.axis_index("c")
    xs, tgt = x[0], loss_target[0]
    nw0, nw1, nfw = norm_w[0:1], norm_w[1:2], norm_f_w.reshape(1, D)
    w_s, bst = a_w_s[0], a_b_s[0].T
    gcat = jnp.concatenate([b_gate_a_w[0], b_gate_x_w[0]], axis=-1).astype(BF16)

    p8_shard = jnp.concatenate([b_conv_w[0], b_conv_b, b_gate_a_b, b_gate_x_b, b_lambda], axis=0)
    win_a8, p8_all = _all_gather([a_w_in[0].astype(BF16), p8_shard], "gather_first")
    p8 = jnp.transpose(p8_all, (1, 0, 2)).reshape(SUBLANES, BW)

    z, h0, ya, (wout_a8, win_b8) = _fwd_a(xs, nw0, win_a8, a_ln_w, a_ln_b, w_s, bst,
                                          [a_w_out[0].astype(BF16), b_w_in[0].astype(BF16)], tm=TM_FWD_A)
    wout_a = wout_a8.reshape(AW, D)
    x1, zb, hs, h1, yb, (wout_b8,) = _fwd_b(xs, ya, wout_a, nw1, win_b8, p8, gcat, [b_w_out[0].astype(BF16)],
                                            tm=TM_FWD_B)
    wout_b = wout_b8.reshape(BW, D)
    dx2, dx2b, loss, g_nfw = _head(x1, yb, wout_b, nfw, tgt, tm=TM_HEAD)

    dx1, dx1b, dzb, g_p8, g_gcat, g_nw1 = _bwd_b(dx2, zb, hs, x1, nw1, win_b8, p8, gcat, wout_b, tm=TM_BWD_B)
    p_wout_b = _wgrad_rows(yb, dx2b, nblk=NDEV, per=2, name="wgrad_b_out")
    p_win_b, _ = _wgrad_cols(h1, dzb, [], nblk=NDEV, name="wgrad_b_in")
    q_win_b, q_wout_b, acc_win_b, acc_wout_b = _reduce_in_chip([p_win_b, p_wout_b], "reduce_b_in_chip")

    dz, g_lnw, g_lnb, g_ws, g_bst, (l_win_b, l_wout_b) = _bwd_a(
        dx1b, z, a_ln_w, a_ln_b, w_s, bst, wout_a, [q_win_b, q_wout_b], tm=TM_BWD_A)
    p_wout_a = _wgrad_rows(ya, dx1b, nblk=NDEV, per=1, name="wgrad_a_out")
    q_wout_a, acc_wout_a = _reduce_in_chip([p_wout_a], "reduce_a_out_in_chip")
    p_win_a, (l_wout_a,) = _wgrad_cols(h0, dz, [q_wout_a], nblk=NDEV, name="wgrad_a_in")
    q_win_a, acc_win_a = _reduce_in_chip([p_win_a], "reduce_a_in_in_chip")
    gx, g_nw0, (l_win_a,) = _bwd_a_in(dz, dx1, xs, nw0, win_a8, [q_win_a], tm=TM_BWD_A_IN)

    small = dict(norm_w=jnp.concatenate([g_nw0, g_nw1], axis=0), ln_w=g_lnw, ln_b=g_lnb, w_s=g_ws,
                 b_s=g_bst.T, gate_a=g_gcat[:, :, :HD], gate_x=g_gcat[:, :, HD:], norm_f=g_nfw, p8=g_p8)
    sizes = {k: small[k].size for k in _SMALL_ORDER}
    flat = jnp.concatenate([small[k].reshape(-1) for k in _SMALL_ORDER])
    flat = jnp.pad(flat, (0, NDEV * PACK_ROWS * LANES - flat.shape[0]))
    q_small, acc_small = _reduce_in_chip([flat.reshape(NDEV, PACK_ROWS, LANES)], "reduce_small_in_chip")
    (l_small,) = _exchange([q_small], "reduce_small_across")
    packed = _sum_and_gather(acc_small, l_small, "gather_small")
    packed = packed.reshape(-1)
    red, off = {}, 0
    for k in _SMALL_ORDER:
        red[k] = packed[off:off + sizes[k]].reshape(small[k].shape)
        off += sizes[k]
    g_p8 = lax.dynamic_slice_in_dim(red["p8"], me * (BW // NDEV), BW // NDEV, axis=1)

    loss = lax.psum(loss[0, 0], ("x", "y", "c"))

    weights = dict(norm_w=norm_w, a_w_in=a_w_in, a_ln_w=a_ln_w, a_ln_b=a_ln_b, a_w_s=a_w_s, a_b_s=a_b_s, a_w_out=a_w_out,
                   b_w_in=b_w_in, b_conv_w=b_conv_w, b_conv_b=b_conv_b, b_gate_a_w=b_gate_a_w, b_gate_a_b=b_gate_a_b,
                   b_gate_x_w=b_gate_x_w, b_gate_x_b=b_gate_x_b, b_lambda=b_lambda, b_w_out=b_w_out, norm_f_w=norm_f_w)
    mom1 = dict(norm_w=m_norm_w, a_w_in=m_a_w_in, a_ln_w=m_a_ln_w, a_ln_b=m_a_ln_b, a_w_s=m_a_w_s, a_b_s=m_a_b_s,
                a_w_out=m_a_w_out, b_w_in=m_b_w_in, b_conv_w=m_b_conv_w, b_conv_b=m_b_conv_b, b_gate_a_w=m_b_gate_a_w,
                b_gate_a_b=m_b_gate_a_b, b_gate_x_w=m_b_gate_x_w, b_gate_x_b=m_b_gate_x_b, b_lambda=m_b_lambda,
                b_w_out=m_b_w_out, norm_f_w=m_norm_f_w)
    mom2 = dict(norm_w=v_norm_w, a_w_in=v_a_w_in, a_ln_w=v_a_ln_w, a_ln_b=v_a_ln_b, a_w_s=v_a_w_s, a_b_s=v_a_b_s,
                a_w_out=v_a_w_out, b_w_in=v_b_w_in, b_conv_w=v_b_conv_w, b_conv_b=v_b_conv_b, b_gate_a_w=v_b_gate_a_w,
                b_gate_a_b=v_b_gate_a_b, b_gate_x_w=v_b_gate_x_w, b_gate_x_b=v_b_gate_x_b, b_lambda=v_b_lambda,
                b_w_out=v_b_w_out, norm_f_w=v_norm_f_w)
    names = list(weights)

    def as2d(a):
        return a.reshape(-1, a.shape[-1])

    upd, grads = {}, {}
    for k, acc, land in (("a_w_in", acc_win_a, l_win_a), ("a_w_out", acc_wout_a, l_wout_a),
                         ("b_w_in", acc_win_b, l_win_b), ("b_w_out", acc_wout_b, l_wout_b)):
        g, d, mo, vo = _adam_big(as2d(weights[k]), acc, land, as2d(mom1[k]), as2d(mom2[k]), "adam_" + k)
        grads[k] = g[None]
        upd[k] = (d, mo, vo)
    grads.update(
        norm_w=red["norm_w"], a_ln_w=red["ln_w"], a_ln_b=red["ln_b"], a_w_s=red["w_s"][None], a_b_s=red["b_s"][None],
        b_conv_w=g_p8[None, 0:4], b_conv_b=g_p8[4:5], b_gate_a_w=red["gate_a"][None], b_gate_a_b=g_p8[5:6],
        b_gate_x_w=red["gate_x"][None], b_gate_x_b=g_p8[6:7], b_lambda=g_p8[7:8], norm_f_w=red["norm_f"].reshape(D))
    small_names = [k for k in names if k not in upd]
    res = _adam_small([(as2d(weights[k]), as2d(grads[k]), as2d(mom1[k]), as2d(mom2[k])) for k in small_names])
    for k, r3 in zip(small_names, res):
        upd[k] = r3
    deltas = [upd[k][0].reshape(weights[k].shape) for k in names]
    new_m = [upd[k][1].reshape(weights[k].shape) for k in names]
    new_v = [upd[k][2].reshape(weights[k].shape) for k in names]
    return (loss, gx[None], *[grads[k] for k in names], *deltas, *new_m, *new_v)
```

```python
import jax
import jax.numpy as jnp
from jax import lax
from jax.experimental import pallas as pl
from jax.experimental.pallas import tpu as pltpu

F32 = jnp.float32
BF16 = jnp.bfloat16
MESH = pl.DeviceIdType.MESH

NDEV = 8
NCHIP_OTHER = 3
D = 1024
AW = 2048
G = 8
GD = AW // G
CH = 128
BW = 1536
BH = 12
HD = BW // BH
CA = 3 * AW // NDEV
CB = 2 * BW // NDEV
RMS_EPS = 1e-6
LN_EPS = 1e-5
RG_C = 8.0
LR, B1, B2, ADAM_EPS, WD, STEP = 0.001, 0.9, 0.999, 1e-08, 0.01, 10
V7X_VMEM_BYTES = 64 * 1024 * 1024
VMEM_LIMIT = V7X_VMEM_BYTES - 8 * 1024 * 1024
SUBLANES = 8
LANES = 128
GELU_C = 0.7978845608028654
GELU_K = 0.044715

_VMEM = pl.BlockSpec(memory_space=pltpu.VMEM)
_HBM = pl.BlockSpec(memory_space=pltpu.HBM)


def _params(**kw):
    return pltpu.CompilerParams(vmem_limit_bytes=VMEM_LIMIT, **kw)


def _gelu_t(z):
    t = jnp.tanh(GELU_C * (z + GELU_K * (z * z * z)))
    return 0.5 * z * (1.0 + t), t


def _dgelu(z, t):
    return 0.5 * (1.0 + t) + 0.5 * z * (1.0 - t * t) * (GELU_C * (1.0 + 3.0 * GELU_K * z * z))


def _sigmoid(v):
    return 0.5 * jnp.tanh(0.5 * v) + 0.5


def _softplus_neg(lam):
    return jnp.maximum(-lam, 0.0) + jnp.log1p(jnp.exp(-jnp.abs(lam)))


def _dot(a, b):
    return jnp.dot(a, b, preferred_element_type=F32)


def _dot_nt(a, b):
    return lax.dot_general(a, b, (((1,), (1,)), ((), ())), preferred_element_type=F32)


def _rowsum(v):
    return jnp.sum(v, axis=0, keepdims=True)


def _causal_mask():
    r = lax.broadcasted_iota(jnp.int32, (CH, CH), 0)
    c = lax.broadcasted_iota(jnp.int32, (CH, CH), 1)
    return r >= c


def _rms(x):
    return lax.rsqrt(jnp.mean(x * x, axis=-1, keepdims=True) + RMS_EPS)


def _rms_bwd(dh, x, r, nw):
    gy = dh * nw
    return r * gy - x * (r * r * r) * jnp.mean(gy * x, axis=-1, keepdims=True)


def _place():
    return lax.axis_index("x"), lax.axis_index("y"), lax.axis_index("c")


def _other_chips(x, y):
    return [(1 - x, y), (x, 1 - y), (1 - x, 1 - y)]


def _gather_ops(ins, outs, send_sems, recv_sems, local_sems):
    n = len(ins)
    x, y, c = _place()
    me = 4 * x + 2 * y + c
    sibling = (x, y, 1 - c)
    chips = _other_chips(x, y)

    def copy(i, k, block, to, src=None):
        dst = outs[i].at[block]
        return pltpu.make_async_remote_copy(
            src_ref=dst if src is None else src, dst_ref=dst,
            send_sem=send_sems.at[k, i], recv_sem=recv_sems.at[k, i], device_id=to, device_id_type=MESH)

    def first_copies():
        mine = [pltpu.make_async_copy(ins[i], outs[i].at[me], local_sems.at[i]) for i in range(n)]
        first = []
        for i in range(n):
            first.append(copy(i, 0, me, sibling, src=ins[i]))
            for j, chip in enumerate(chips):
                first.append(copy(i, 1 + j, me, (*chip, c), src=ins[i]))
        return mine, first

    def start():
        mine, first = first_copies()
        for cp in mine + first:
            cp.start()

    def finish():
        mine, first = first_copies()
        passed = []
        for i in range(n):
            for j, (px, py) in enumerate(chips):
                blk = 4 * px + 2 * py + c
                copy(i, 1 + j, blk, sibling).wait_recv()
                fwd = copy(i, 4 + j, blk, sibling)
                fwd.start()
                passed.append(fwd)
        for i in range(n):
            copy(i, 0, 4 * x + 2 * y + (1 - c), sibling).wait_recv()
            for j, (px, py) in enumerate(chips):
                copy(i, 4 + j, 4 * px + 2 * py + (1 - c), sibling).wait_recv()
        for cp in first + passed:
            cp.wait_send()
        for cp in mine:
            cp.wait()

    return start, finish


def _gather_sems(n):
    return [pltpu.SemaphoreType.DMA((7, n)), pltpu.SemaphoreType.DMA((7, n)), pltpu.SemaphoreType.DMA((n,))]


def _gathered_shapes(shards):
    return [jax.ShapeDtypeStruct((NDEV,) + s.shape, s.dtype) for s in shards]


def _exchange_ops(srcs, dsts, send_sems, recv_sems):
    n = len(srcs)
    x, y, c = _place()
    chips = _other_chips(x, y)

    def copies():
        return [pltpu.make_async_remote_copy(
            src_ref=srcs[i].at[j], dst_ref=dsts[i].at[j], send_sem=send_sems.at[j, i], recv_sem=recv_sems.at[j, i],
            device_id=(*chips[j], c), device_id_type=MESH) for i in range(n) for j in range(NCHIP_OTHER)]

    def start():
        for cp in copies():
            cp.start()

    def finish():
        cps = copies()
        for cp in cps:
            cp.wait_recv()
        for cp in cps:
            cp.wait_send()

    return start, finish


def _exchange_sems(n):
    return [pltpu.SemaphoreType.DMA((NCHIP_OTHER, n)), pltpu.SemaphoreType.DMA((NCHIP_OTHER, n))]


def _all_gather(shards, name):
    n = len(shards)

    def body(*refs):
        start, finish = _gather_ops(refs[:n], refs[n:2 * n], *refs[2 * n:])
        start()
        finish()

    return pl.pallas_call(
        body, name=name, in_specs=[_HBM] * n, out_specs=[_HBM] * n, out_shape=_gathered_shapes(shards),
        scratch_shapes=_gather_sems(n), compiler_params=_params(),
    )(*shards)


def _reduce_in_chip(ps, name):
    n = len(ps)

    def body(*refs):
        p_refs, q_refs, acc_refs = refs[:n], refs[n:2 * n], refs[2 * n:3 * n]
        rest = refs[3 * n:]
        mines, lands = rest[:n], rest[n:2 * n]
        send_sems, recv_sems, local_sems = rest[2 * n:]
        x, y, c = _place()
        sibling = (x, y, 1 - c)
        pairs = []
        for i in range(n):
            for px in range(2):
                for py in range(2):
                    pi = 2 * px + py
                    loc = pltpu.make_async_copy(p_refs[i].at[4 * px + 2 * py + c], mines[i].at[pi],
                                                local_sems.at[pi, i])
                    cp = pltpu.make_async_remote_copy(
                        src_ref=p_refs[i].at[4 * px + 2 * py + (1 - c)], dst_ref=lands[i].at[pi],
                        send_sem=send_sems.at[pi, i], recv_sem=recv_sems.at[pi, i],
                        device_id=sibling, device_id_type=MESH)
                    loc.start()
                    cp.start()
                    pairs.append((loc, cp))
        for loc, cp in pairs:
            loc.wait()
            cp.wait_recv()
        for i in range(n):
            for j, (qx, qy) in enumerate(_other_chips(x, y)):
                qi = 2 * qx + qy
                q_refs[i][j] = (mines[i][qi].astype(F32) + lands[i][qi].astype(F32)).astype(q_refs[i].dtype)
            mi = 2 * x + y
            acc_refs[i][...] = mines[i][mi].astype(F32) + lands[i][mi].astype(F32)
        for _, cp in pairs:
            cp.wait_send()

    blk = [p.shape[1:] for p in ps]
    return pl.pallas_call(
        body, name=name, in_specs=[_HBM] * n, out_specs=[_VMEM] * (2 * n),
        out_shape=[jax.ShapeDtypeStruct((NCHIP_OTHER,) + b, p.dtype) for b, p in zip(blk, ps)]
        + [jax.ShapeDtypeStruct(b, F32) for b in blk],
        scratch_shapes=[pltpu.VMEM((4,) + b, p.dtype) for b, p in zip(blk, ps)]
        + [pltpu.VMEM((4,) + b, p.dtype) for b, p in zip(blk, ps)]
        + [pltpu.SemaphoreType.DMA((4, n)), pltpu.SemaphoreType.DMA((4, n)), pltpu.SemaphoreType.DMA((4, n))],
        compiler_params=_params(),
    )(*ps)


def _exchange(qs, name):
    n = len(qs)

    def body(*refs):
        start, finish = _exchange_ops(refs[:n], refs[n:2 * n], *refs[2 * n:])
        start()
        finish()

    return pl.pallas_call(
        body, name=name, in_specs=[_HBM] * n, out_specs=[_HBM] * n,
        out_shape=[jax.ShapeDtypeStruct(q.shape, q.dtype) for q in qs],
        scratch_shapes=_exchange_sems(n), compiler_params=_params(),
    )(*qs)


def _sum_and_gather(acc, land, name):
    def body(acc_ref, land_ref, out_ref, mine_scr, *sems):
        mine_scr[...] = acc_ref[...] + land_ref[0] + land_ref[1] + land_ref[2]
        start, finish = _gather_ops([mine_scr], [out_ref], *sems)
        start()
        finish()

    return pl.pallas_call(
        body, name=name, in_specs=[_VMEM, _VMEM], out_specs=_HBM,
        out_shape=jax.ShapeDtypeStruct((NDEV,) + acc.shape, acc.dtype),
        scratch_shapes=[pltpu.VMEM(acc.shape, acc.dtype)] + _gather_sems(1), compiler_params=_params(),
    )(acc, land)


def _fwd_a(x, nw, win8, lnw, lnb, ws, bst, shards, *, tm):
    s_len = x.shape[0]
    nt = s_len // tm
    nch = tm // CH
    ng = len(shards)

    def body(*refs):
        x_ref, nw_ref, win_ref, lnw_ref, lnb_ref, ws_ref, bst_ref = refs[:7]
        sh_refs = refs[7:7 + ng]
        z_ref, h_ref, y_ref = refs[7 + ng:10 + ng]
        ga_refs = refs[10 + ng:10 + 2 * ng]
        wc_scr, gv_scr = refs[10 + 2 * ng:12 + 2 * ng]
        sems = refs[12 + 2 * ng:]
        i = pl.program_id(0)
        start, finish = _gather_ops(sh_refs, ga_refs, *sems)

        @pl.when(i == 0)
        def _():
            start()
            m = _causal_mask()
            for g in range(G):
                wc_scr[g] = jnp.where(m, ws_ref[g], 0.0).astype(BF16)

        x = x_ref[...]
        h = (x * _rms(x) * nw_ref[...]).astype(BF16)
        h_ref[...] = h
        for k in range(NDEV):
            z_ref[:, k * CA:(k + 1) * CA] = _dot(h, win_ref[k])

        ssum = jnp.zeros((tm, 1), F32)
        for g in range(G):
            gv = _gelu_t(z_ref[:, AW + g * GD:AW + (g + 1) * GD])[0]
            gv_scr[:, g * GD:(g + 1) * GD] = gv
            ssum = ssum + jnp.sum(gv, axis=-1, keepdims=True)
        mu = ssum * (1.0 / AW)
        vsum = jnp.zeros((tm, 1), F32)
        for g in range(G):
            dlt = gv_scr[:, g * GD:(g + 1) * GD] - mu
            vsum = vsum + jnp.sum(dlt * dlt, axis=-1, keepdims=True)
        rstd = lax.rsqrt(vsum * (1.0 / AW) + LN_EPS)

        for g in range(G):
            cs = slice(g * GD, (g + 1) * GD)
            v = (gv_scr[:, cs] - mu) * rstd * lnw_ref[:, cs] + lnb_ref[:, cs]
            vb = v.astype(BF16)
            u = _gelu_t(z_ref[:, cs])[0]
            zg = z_ref[:, 2 * AW + g * GD:2 * AW + (g + 1) * GD]
            sg = zg * _sigmoid(zg)
            for n in range(nch):
                rs = slice(n * CH, (n + 1) * CH)
                s = _dot(wc_scr[g], vb[rs, :]) + bst_ref[:, g:g + 1]
                y_ref[rs, cs] = (u[rs, :] * s * sg[rs, :]).astype(BF16)

        @pl.when(i == nt - 1)
        def _():
            finish()

    tile = lambda w: pl.BlockSpec((tm, w), lambda i: (i, 0))
    res = pl.pallas_call(
        body, name="fwd_a", grid=(nt,),
        in_specs=[tile(D), _VMEM, _VMEM, _VMEM, _VMEM, _VMEM, _VMEM] + [_HBM] * ng,
        out_specs=[tile(3 * AW), tile(D), tile(AW)] + [_HBM] * ng,
        out_shape=[jax.ShapeDtypeStruct((s_len, 3 * AW), F32), jax.ShapeDtypeStruct((s_len, D), BF16),
                   jax.ShapeDtypeStruct((s_len, AW), BF16)] + _gathered_shapes(shards),
        scratch_shapes=[pltpu.VMEM((G, CH, CH), BF16), pltpu.VMEM((tm, AW), F32)] + _gather_sems(ng),
        compiler_params=_params(dimension_semantics=("arbitrary",)),
    )(x, nw, win8, lnw, lnb, ws, bst, *shards)
    return res[0], res[1], res[2], res[3:]


def _bwd_a(dx1, z, lnw, lnb, ws, bst, wout, qs, *, tm):
    s_len = dx1.shape[0]
    nt = s_len // tm
    nch = tm // CH
    nq = len(qs)

    def body(*refs):
        dx1_ref, z_ref, lnw_ref, lnb_ref, ws_ref, bst_ref, wout_ref = refs[:7]
        q_refs = refs[7:7 + nq]
        dz_ref, glnw_ref, glnb_ref, gws_ref, gbst_ref = refs[7 + nq:12 + nq]
        land_refs = refs[12 + nq:12 + 2 * nq]
        (wc_scr, wct_scr, vh_scr, dgv_scr, dy_scr, dv_scr, gbs_acc, gwc_acc) = refs[12 + 2 * nq:20 + 2 * nq]
        sems = refs[20 + 2 * nq:]
        i = pl.program_id(0)
        start, finish = _exchange_ops(q_refs, land_refs, *sems)

        @pl.when(i == 0)
        def _():
            start()
            m = _causal_mask()
            for g in range(G):
                wm = jnp.where(m, ws_ref[g], 0.0)
                wc_scr[g] = wm.astype(BF16)
                wct_scr[g] = wm.T.astype(BF16)
            glnw_ref[...] = jnp.zeros_like(glnw_ref)
            glnb_ref[...] = jnp.zeros_like(glnb_ref)
            gbs_acc[...] = jnp.zeros_like(gbs_acc)
            gwc_acc[...] = jnp.zeros_like(gwc_acc)

        dy_scr[...] = _dot_nt(dx1_ref[...], wout_ref[...])

        ssum = jnp.zeros((tm, 1), F32)
        for g in range(G):
            cs = slice(g * GD, (g + 1) * GD)
            zv = z_ref[:, AW + g * GD:AW + (g + 1) * GD]
            gv, t = _gelu_t(zv)
            vh_scr[:, cs] = gv
            dgv_scr[:, cs] = _dgelu(zv, t)
            ssum = ssum + jnp.sum(gv, axis=-1, keepdims=True)
        mu = ssum * (1.0 / AW)
        vsum = jnp.zeros((tm, 1), F32)
        for g in range(G):
            dlt = vh_scr[:, g * GD:(g + 1) * GD] - mu
            vsum = vsum + jnp.sum(dlt * dlt, axis=-1, keepdims=True)
        rstd = lax.rsqrt(vsum * (1.0 / AW) + LN_EPS)

        m1 = jnp.zeros((tm, 1), F32)
        m2 = jnp.zeros((tm, 1), F32)
        for g in range(G):
            cs = slice(g * GD, (g + 1) * GD)
            gs = slice(2 * AW + g * GD, 2 * AW + (g + 1) * GD)
            vhat = (vh_scr[:, cs] - mu) * rstd
            vh_scr[:, cs] = vhat
            vb = (vhat * lnw_ref[:, cs] + lnb_ref[:, cs]).astype(BF16)
            zu = z_ref[:, cs]
            u, tu = _gelu_t(zu)
            zg = z_ref[:, gs]
            sig = _sigmoid(zg)
            sg = zg * sig
            dy = dy_scr[:, cs]
            dsf = dy * u * sg
            dsb = dsf.astype(BF16)
            dvs = []
            for n in range(nch):
                rs = slice(n * CH, (n + 1) * CH)
                s = _dot(wc_scr[g], vb[rs, :]) + bst_ref[:, g:g + 1]
                dys = dy[rs, :] * s
                dz_ref[rs, cs] = (dys * sg[rs, :] * _dgelu(zu[rs, :], tu[rs, :])).astype(BF16)
                dz_ref[rs, gs] = (dys * u[rs, :] * (sig[rs, :] * (1.0 + zg[rs, :] * (1.0 - sig[rs, :])))).astype(BF16)
                gbs_acc[g] += dsf[rs, :]
                gwc_acc[g] += _dot_nt(dsb[rs, :], vb[rs, :])
                dvs.append(_dot(wct_scr[g], dsb[rs, :]))
            dv = jnp.concatenate(dvs, axis=0) if nch > 1 else dvs[0]
            glnw_ref[:, cs] += _rowsum(dv * vhat)
            glnb_ref[:, cs] += _rowsum(dv)
            dvh = dv * lnw_ref[:, cs]
            dv_scr[:, cs] = dvh
            m1 = m1 + jnp.sum(dvh, axis=-1, keepdims=True)
            m2 = m2 + jnp.sum(dvh * vhat, axis=-1, keepdims=True)
        m1 = m1 * (1.0 / AW)
        m2 = m2 * (1.0 / AW)
        for g in range(G):
            cs = slice(g * GD, (g + 1) * GD)
            dgv = rstd * (dv_scr[:, cs] - m1 - vh_scr[:, cs] * m2)
            dz_ref[:, AW + g * GD:AW + (g + 1) * GD] = (dgv * dgv_scr[:, cs]).astype(BF16)

        @pl.when(i == nt - 1)
        def _():
            m = _causal_mask()
            for g in range(G):
                gws_ref[g] = jnp.where(m, gwc_acc[g], 0.0)
                gbst_ref[:, g:g + 1] = jnp.sum(gbs_acc[g], axis=-1, keepdims=True)
            finish()

    tile = lambda w: pl.BlockSpec((tm, w), lambda i: (i, 0))
    whole = lambda *s: pl.BlockSpec(s, lambda i: (0,) * len(s))
    res = pl.pallas_call(
        body, name="bwd_a", grid=(nt,),
        in_specs=[tile(D), tile(3 * AW), _VMEM, _VMEM, _VMEM, _VMEM, _VMEM] + [_HBM] * nq,
        out_specs=[tile(3 * AW), whole(1, AW), whole(1, AW), whole(G, CH, CH), whole(CH, G)] + [_HBM] * nq,
        out_shape=[jax.ShapeDtypeStruct((s_len, 3 * AW), BF16), jax.ShapeDtypeStruct((1, AW), F32),
                   jax.ShapeDtypeStruct((1, AW), F32), jax.ShapeDtypeStruct((G, CH, CH), F32),
                   jax.ShapeDtypeStruct((CH, G), F32)] + [jax.ShapeDtypeStruct(q.shape, q.dtype) for q in qs],
        scratch_shapes=[pltpu.VMEM((G, CH, CH), BF16), pltpu.VMEM((G, CH, CH), BF16),
                        pltpu.VMEM((tm, AW), F32), pltpu.VMEM((tm, AW), F32),
                        pltpu.VMEM((tm, AW), F32), pltpu.VMEM((tm, AW), F32),
                        pltpu.VMEM((G, CH, GD), F32), pltpu.VMEM((G, CH, CH), F32)] + _exchange_sems(nq),
        compiler_params=_params(dimension_semantics=("arbitrary",)),
    )(dx1, z, lnw, lnb, ws, bst, wout, *qs)
    return res[0], res[1], res[2], res[3], res[4], res[5:]


def _bwd_a_in(dz, dx1, x, nw, win8, qs, *, tm):
    s_len = x.shape[0]
    nt = s_len // tm
    nq = len(qs)

    def body(*refs):
        dz_ref, dx1_ref, x_ref, nw_ref, win_ref = refs[:5]
        q_refs = refs[5:5 + nq]
        gx_ref, gnw_ref = refs[5 + nq:7 + nq]
        land_refs = refs[7 + nq:7 + 2 * nq]
        sems = refs[7 + 2 * nq:]
        i = pl.program_id(0)
        start, finish = _exchange_ops(q_refs, land_refs, *sems)

        @pl.when(i == 0)
        def _():
            start()
            gnw_ref[...] = jnp.zeros_like(gnw_ref)

        dh = jnp.zeros((tm, D), F32)
        for k in range(NDEV):
            dh = dh + _dot_nt(dz_ref[:, k * CA:(k + 1) * CA], win_ref[k])
        x = x_ref[...]
        r = _rms(x)
        gx_ref[...] = dx1_ref[...] + _rms_bwd(dh, x, r, nw_ref[...])
        gnw_ref[...] += _rowsum(dh * x * r)

        @pl.when(i == nt - 1)
        def _():
            finish()

    tile = lambda w: pl.BlockSpec((tm, w), lambda i: (i, 0))
    res = pl.pallas_call(
        body, name="bwd_a_in", grid=(nt,),
        in_specs=[tile(3 * AW), tile(D), tile(D), _VMEM, _VMEM] + [_HBM] * nq,
        out_specs=[tile(D), pl.BlockSpec((1, D), lambda i: (0, 0))] + [_HBM] * nq,
        out_shape=[jax.ShapeDtypeStruct((s_len, D), F32), jax.ShapeDtypeStruct((1, D), F32)]
        + [jax.ShapeDtypeStruct(q.shape, q.dtype) for q in qs],
        scratch_shapes=_exchange_sems(nq),
        compiler_params=_params(dimension_semantics=("arbitrary",)),
    )(dz, dx1, x, nw, win8, *qs)
    return res[0], res[1], res[2:]


def _conv(p8_ref, cs, xb, xm1, xm2, xm3):
    xc = p8_ref[4:5, cs] + p8_ref[3:4, cs] * xb
    xc = xc + p8_ref[0:1, cs] * xm3
    xc = xc + p8_ref[1:2, cs] * xm2
    return xc + p8_ref[2:3, cs] * xm1


def _gates(p8_ref, gcat_ref, hh, xc):
    cs = slice(hh * HD, (hh + 1) * HD)
    pre = _dot(xc.astype(BF16), gcat_ref[hh])
    r = _sigmoid(pre[:, :HD] + p8_ref[5:6, cs])
    ig = _sigmoid(pre[:, HD:] + p8_ref[6:7, cs])
    sp = _softplus_neg(p8_ref[7:8, cs])
    la = (-RG_C) * r * sp
    a = jnp.exp(la)
    m2 = jnp.tanh(-la) * (1.0 + a * a)
    rm = lax.rsqrt(m2)
    mult = jnp.where(m2 > 0.0, m2 * rm, 0.0)
    return r, ig, sp, a, mult, rm


def _scan_rows(a_ref, b_ref, out_ref, carry, tm, reverse):
    row = lax.broadcasted_iota(jnp.int32, (SUBLANES, BW), 0)
    ngrp = tm // SUBLANES

    def step(j, cr):
        jj = (ngrp - 1 - j) if reverse else j
        off = pl.multiple_of(jj * SUBLANES, SUBLANES)
        a = a_ref[pl.ds(off, SUBLANES), :]
        b = b_ref[pl.ds(off, SUBLANES), :]
        for sh in (1, 2, 4):
            if reverse:
                a_s = pltpu.roll(a, SUBLANES - sh, 0)
                b_s = pltpu.roll(b, SUBLANES - sh, 0)
                m = row < SUBLANES - sh
            else:
                a_s = pltpu.roll(a, sh, 0)
                b_s = pltpu.roll(b, sh, 0)
                m = row >= sh
            b = jnp.where(m, a * b_s + b, b)
            a = jnp.where(m, a * a_s, a)
        o = b + a * cr
        out_ref[pl.ds(off, SUBLANES), :] = o
        return o[0:1, :] if reverse else o[SUBLANES - 1:SUBLANES, :]

    return lax.fori_loop(0, ngrp, step, carry)


def _fwd_b(x, ya, wout_a, nw, win8, p8, gcat, shards, *, tm):
    s_len = x.shape[0]
    nt = s_len // tm
    ng = len(shards)

    def body(*refs):
        x_ref, ya_ref, wouta_ref, nw_ref, win_ref, p8_ref, gcat_ref = refs[:7]
        sh_refs = refs[7:7 + ng]
        x1_ref, zb_ref, hs_ref, h1_ref, yb_ref = refs[7 + ng:12 + ng]
        ga_refs = refs[12 + ng:12 + 2 * ng]
        xbe_scr, a_scr, b_scr, carry_scr = refs[12 + 2 * ng:16 + 2 * ng]
        sems = refs[16 + 2 * ng:]
        i = pl.program_id(0)
        start, finish = _gather_ops(sh_refs, ga_refs, *sems)

        @pl.when(i == 0)
        def _():
            start()
            xbe_scr[0:SUBLANES, :] = jnp.zeros((SUBLANES, BW), F32)
            carry_scr[...] = jnp.zeros_like(carry_scr)

        x1 = x_ref[...] + _dot(ya_ref[...], wouta_ref[...])
        x1_ref[...] = x1
        h = (x1 * _rms(x1) * nw_ref[...]).astype(BF16)
        h1_ref[...] = h
        for k in range(NDEV):
            zb_ref[:, k * CB:(k + 1) * CB] = _dot(h, win_ref[k])
        xbe_scr[SUBLANES:SUBLANES + tm, :] = zb_ref[:, :BW]
        for hh in range(BH):
            cs = slice(hh * HD, (hh + 1) * HD)
            xc = _conv(p8_ref, cs, xbe_scr[SUBLANES:SUBLANES + tm, cs], xbe_scr[7:7 + tm, cs],
                       xbe_scr[6:6 + tm, cs], xbe_scr[5:5 + tm, cs])
            _, ig, _, a, mult, _ = _gates(p8_ref, gcat_ref, hh, xc)
            a_scr[:, cs] = a
            b_scr[:, cs] = mult * (ig * xc)
        xbe_scr[0:SUBLANES, :] = xbe_scr[tm:tm + SUBLANES, :]
        carry_scr[...] = _scan_rows(a_scr, b_scr, hs_ref, carry_scr[...], tm, False)
        for hh in range(BH):
            cs = slice(hh * HD, (hh + 1) * HD)
            gt = zb_ref[:, BW + hh * HD:BW + (hh + 1) * HD]
            yb_ref[:, cs] = (hs_ref[:, cs] * (gt * _sigmoid(gt))).astype(BF16)

        @pl.when(i == nt - 1)
        def _():
            finish()

    tile = lambda w: pl.BlockSpec((tm, w), lambda i: (i, 0))
    res = pl.pallas_call(
        body, name="fwd_b", grid=(nt,),
        in_specs=[tile(D), tile(AW), _VMEM, _VMEM, _VMEM, _VMEM, _VMEM] + [_HBM] * ng,
        out_specs=[tile(D), tile(2 * BW), tile(BW), tile(D), tile(BW)] + [_HBM] * ng,
        out_shape=[jax.ShapeDtypeStruct((s_len, D), F32), jax.ShapeDtypeStruct((s_len, 2 * BW), F32),
                   jax.ShapeDtypeStruct((s_len, BW), F32), jax.ShapeDtypeStruct((s_len, D), BF16),
                   jax.ShapeDtypeStruct((s_len, BW), BF16)] + _gathered_shapes(shards),
        scratch_shapes=[pltpu.VMEM((tm + SUBLANES, BW), F32), pltpu.VMEM((tm, BW), F32),
                        pltpu.VMEM((tm, BW), F32), pltpu.VMEM((1, BW), F32)] + _gather_sems(ng),
        compiler_params=_params(dimension_semantics=("arbitrary",)),
    )(x, ya, wout_a, nw, win8, p8, gcat, *shards)
    return res[0], res[1], res[2], res[3], res[4], res[5:]


def _head(x1, yb, wout, nfw, tgt, *, tm):
    s_len = x1.shape[0]

    def body(x1_ref, yb_ref, wout_ref, nfw_ref, t_ref, dx2_ref, dx2b_ref, loss_ref, gnfw_ref):
        @pl.when(pl.program_id(0) == 0)
        def _():
            loss_ref[...] = jnp.zeros_like(loss_ref)
            gnfw_ref[...] = jnp.zeros_like(gnfw_ref)

        x2 = x1_ref[...] + _dot(yb_ref[...], wout_ref[...])
        rf = _rms(x2)
        xn = x2 * rf
        e = xn * nfw_ref[...] - t_ref[...]
        loss_ref[...] += (0.5 / D) * jnp.sum(jnp.sum(e * e, axis=-1, keepdims=True), axis=0, keepdims=True)
        dyf = e * (1.0 / D)
        gnfw_ref[...] += _rowsum(dyf * xn)
        dx2 = _rms_bwd(dyf, x2, rf, nfw_ref[...])
        dx2_ref[...] = dx2
        dx2b_ref[...] = dx2.astype(BF16)

    tile = lambda w: pl.BlockSpec((tm, w), lambda i: (i, 0))
    whole = lambda *s: pl.BlockSpec(s, lambda i: (0,) * len(s))
    return pl.pallas_call(
        body, name="head", grid=(s_len // tm,),
        in_specs=[tile(D), tile(BW), _VMEM, _VMEM, tile(D)],
        out_specs=[tile(D), tile(D), whole(1, 1), whole(1, D)],
        out_shape=[jax.ShapeDtypeStruct((s_len, D), F32), jax.ShapeDtypeStruct((s_len, D), BF16),
                   jax.ShapeDtypeStruct((1, 1), F32), jax.ShapeDtypeStruct((1, D), F32)],
        compiler_params=_params(dimension_semantics=("arbitrary",)),
    )(x1, yb, wout, nfw, tgt)


def _bwd_b(dx2, zb, hs, x1, nw, win8, p8, gcat, wout, *, tm):
    s_len = x1.shape[0]
    nt = s_len // tm
    per = tm // SUBLANES

    def body(dx2_ref, zb_ref, zbp_ref, hs_ref, hsp_ref, x1_ref, nw_ref, win_ref, p8_ref, gcat_ref, wout_ref,
             dx1_ref, dx1b_ref, dzb_ref, gp8_ref, gg_ref, gnw_ref,
             xbe_scr, hse_scr, ae_scr, an_scr, r_scr, i_scr, m_scr, xc_scr, cc_scr, dhd_scr, dh_scr, dy_scr, dxce_scr,
             carry_scr, afirst_scr):
        i = pl.program_id(0)
        ti = nt - 1 - i

        @pl.when(i == 0)
        def _():
            gp8_ref[...] = jnp.zeros_like(gp8_ref)
            gg_ref[...] = jnp.zeros_like(gg_ref)
            gnw_ref[...] = jnp.zeros_like(gnw_ref)
            dxce_scr[tm:tm + SUBLANES, :] = jnp.zeros((SUBLANES, BW), F32)
            carry_scr[...] = jnp.zeros_like(carry_scr)
            afirst_scr[...] = jnp.zeros_like(afirst_scr)

        has_prev = (ti > 0).astype(F32)
        xbe_scr[0:SUBLANES, :] = zbp_ref[:, :BW] * has_prev
        xbe_scr[SUBLANES:SUBLANES + tm, :] = zb_ref[:, :BW]
        hse_scr[0:SUBLANES, :] = hsp_ref[...] * has_prev
        hse_scr[SUBLANES:SUBLANES + tm, :] = hs_ref[...]

        dx2 = dx2_ref[...]
        dy_scr[...] = _dot_nt(dx2.astype(BF16), wout_ref[...])

        for hh in range(BH):
            cs = slice(hh * HD, (hh + 1) * HD)
            xc = _conv(p8_ref, cs, xbe_scr[SUBLANES:SUBLANES + tm, cs], xbe_scr[7:7 + tm, cs],
                       xbe_scr[6:6 + tm, cs], xbe_scr[5:5 + tm, cs])
            r, ig, _, a, mult, rm = _gates(p8_ref, gcat_ref, hh, xc)
            cc_scr[:, cs] = a * hse_scr[7:7 + tm, cs] - (ig * xc) * (a * a * rm)
            xc_scr[:, cs] = xc
            r_scr[:, cs] = r
            i_scr[:, cs] = ig
            m_scr[:, cs] = mult
            ae_scr[0:tm, cs] = a
            gt = zb_ref[:, BW + hh * HD:BW + (hh + 1) * HD]
            sig = _sigmoid(gt)
            dy = dy_scr[:, cs]
            dhd_scr[:, cs] = dy * (gt * sig)
            dzb_ref[:, BW + hh * HD:BW + (hh + 1) * HD] = (
                dy * hs_ref[:, cs] * (sig * (1.0 + gt * (1.0 - sig)))).astype(BF16)
        ae_scr[tm:tm + SUBLANES, :] = jnp.broadcast_to(afirst_scr[...], (SUBLANES, BW))
        an_scr[...] = ae_scr[1:1 + tm, :]
        afirst_scr[...] = ae_scr[0:1, :]
        carry_scr[...] = _scan_rows(an_scr, dhd_scr, dh_scr, carry_scr[...], tm, True)

        for hh in range(BH):
            cs = slice(hh * HD, (hh + 1) * HD)
            dh = dh_scr[:, cs]
            mult = m_scr[:, cs]
            ig = i_scr[:, cs]
            r = r_scr[:, cs]
            xc = xc_scr[:, cs]
            lam = p8_ref[7:8, cs]
            sp = _softplus_neg(lam)
            dla = dh * cc_scr[:, cs]
            gp8_ref[7:8, cs] += _rowsum(dla * ((-RG_C) * r)) * (-_sigmoid(-lam))
            dpr = dla * ((-RG_C) * sp) * (r * (1.0 - r))
            dpi = dh * mult * xc * (ig * (1.0 - ig))
            gp8_ref[5:6, cs] += _rowsum(dpr)
            gp8_ref[6:7, cs] += _rowsum(dpi)
            dcat = jnp.concatenate([dpr, dpi], axis=1).astype(BF16)
            dxc = dh * mult * ig + _dot_nt(dcat, gcat_ref[hh])
            gg_ref[hh] += _dot(xc.T.astype(BF16), dcat)
            dxce_scr[0:tm, cs] = dxc
            gp8_ref[4:5, cs] += _rowsum(dxc)
            gp8_ref[3:4, cs] += _rowsum(dxc * xbe_scr[SUBLANES:SUBLANES + tm, cs])
            gp8_ref[2:3, cs] += _rowsum(dxc * xbe_scr[7:7 + tm, cs])
            gp8_ref[1:2, cs] += _rowsum(dxc * xbe_scr[6:6 + tm, cs])
            gp8_ref[0:1, cs] += _rowsum(dxc * xbe_scr[5:5 + tm, cs])
        for hh in range(BH):
            cs = slice(hh * HD, (hh + 1) * HD)
            dxb = p8_ref[3:4, cs] * dxce_scr[0:tm, cs]
            dxb = dxb + p8_ref[2:3, cs] * dxce_scr[1:1 + tm, cs]
            dxb = dxb + p8_ref[1:2, cs] * dxce_scr[2:2 + tm, cs]
            dxb = dxb + p8_ref[0:1, cs] * dxce_scr[3:3 + tm, cs]
            dzb_ref[:, cs] = dxb.astype(BF16)
        dxce_scr[tm:tm + SUBLANES, :] = dxce_scr[0:SUBLANES, :]

        dh1 = jnp.zeros((tm, D), F32)
        for k in range(NDEV):
            dh1 = dh1 + _dot_nt(dzb_ref[:, k * CB:(k + 1) * CB], win_ref[k])
        x1 = x1_ref[...]
        r1 = _rms(x1)
        dx1 = dx2 + _rms_bwd(dh1, x1, r1, nw_ref[...])
        dx1_ref[...] = dx1
        dx1b_ref[...] = dx1.astype(BF16)
        gnw_ref[...] += _rowsum(dh1 * x1 * r1)

    tile = lambda w: pl.BlockSpec((tm, w), lambda i: (nt - 1 - i, 0))
    prev = lambda w: pl.BlockSpec((SUBLANES, w), lambda i: (jnp.maximum((nt - 1 - i) * per - 1, 0), 0))
    whole = lambda *s: pl.BlockSpec(s, lambda i: (0,) * len(s))
    full = lambda: pltpu.VMEM((tm, BW), F32)
    ext = lambda: pltpu.VMEM((tm + SUBLANES, BW), F32)
    return pl.pallas_call(
        body, name="bwd_b", grid=(nt,),
        in_specs=[tile(D), tile(2 * BW), prev(2 * BW), tile(BW), prev(BW), tile(D),
                  _VMEM, _VMEM, _VMEM, _VMEM, _VMEM],
        out_specs=[tile(D), tile(D), tile(2 * BW), whole(SUBLANES, BW), whole(BH, HD, 2 * HD), whole(1, D)],
        out_shape=[jax.ShapeDtypeStruct((s_len, D), F32), jax.ShapeDtypeStruct((s_len, D), BF16),
                   jax.ShapeDtypeStruct((s_len, 2 * BW), BF16), jax.ShapeDtypeStruct((SUBLANES, BW), F32),
                   jax.ShapeDtypeStruct((BH, HD, 2 * HD), F32), jax.ShapeDtypeStruct((1, D), F32)],
        scratch_shapes=[ext(), ext(), ext(), full(), full(), full(), full(), full(), full(), full(), full(), full(),
                        ext(), pltpu.VMEM((1, BW), F32), pltpu.VMEM((1, BW), F32)],
        compiler_params=_params(dimension_semantics=("arbitrary",)),
    )(dx2, zb, zb, hs, hs, x1, nw, win8, p8, gcat, wout)


def _transpose_into(dst_ref, src_ref, rows):
    s_len = src_ref.shape[0]
    for r0 in range(0, s_len, rows):
        dst_ref[:, r0:r0 + rows] = src_ref[r0:r0 + rows, :].astype(F32).T.astype(BF16)


def _wgrad_cols(a, b, qs, *, nblk, name):
    s_len, m = a.shape
    bn = b.shape[1] // nblk
    nq = len(qs)

    def body(*refs):
        a_ref, b_ref = refs[:2]
        q_refs = refs[2:2 + nq]
        o_ref = refs[2 + nq]
        land_refs = refs[3 + nq:3 + 2 * nq]
        at_scr = refs[3 + 2 * nq]
        sems = refs[4 + 2 * nq:]
        i = pl.program_id(0)
        if nq:
            start, finish = _exchange_ops(q_refs, land_refs, *sems)

        @pl.when(i == 0)
        def _():
            if nq:
                start()
            _transpose_into(at_scr, a_ref, 256)

        o_ref[0] = _dot(at_scr[...], b_ref[...]).astype(BF16)

        if nq:
            @pl.when(i == nblk - 1)
            def _():
                finish()

    res = pl.pallas_call(
        body, name=name, grid=(nblk,),
        in_specs=[_VMEM, pl.BlockSpec((s_len, bn), lambda j: (0, j))] + [_HBM] * nq,
        out_specs=[pl.BlockSpec((1, m, bn), lambda j: (j, 0, 0))] + [_HBM] * nq,
        out_shape=[jax.ShapeDtypeStruct((nblk, m, bn), BF16)] + [jax.ShapeDtypeStruct(q.shape, q.dtype) for q in qs],
        scratch_shapes=[pltpu.VMEM((m, s_len), BF16)] + (_exchange_sems(nq) if nq else []),
        compiler_params=_params(dimension_semantics=("arbitrary",)),
    )(a, b, *qs)
    return res[0], res[1:]


def _wgrad_rows(a, b, *, nblk, per, name):
    s_len, m = a.shape
    n = b.shape[1]
    rb = m // nblk
    bm = per * rb

    def body(a_ref, b_ref, o_ref, at_scr):
        _transpose_into(at_scr, a_ref, 256)
        res = _dot(at_scr[...], b_ref[...]).astype(BF16)
        for q in range(per):
            o_ref[q] = res[q * rb:(q + 1) * rb, :]

    return pl.pallas_call(
        body, name=name, grid=(nblk // per,),
        in_specs=[pl.BlockSpec((s_len, bm), lambda j: (0, j)), _VMEM],
        out_specs=pl.BlockSpec((per, rb, n), lambda j: (j, 0, 0)),
        out_shape=jax.ShapeDtypeStruct((nblk, rb, n), BF16),
        scratch_shapes=[pltpu.VMEM((bm, s_len), BF16)],
        compiler_params=_params(dimension_semantics=("arbitrary",)),
    )(a, b)


def _adam_math(w, g, m, v):
    m = B1 * m + (1.0 - B1) * g
    v = B2 * v + (1.0 - B2) * (g * g)
    m_hat = m / (1.0 - B1 ** STEP)
    v_hat = v / (1.0 - B2 ** STEP)
    delta = (-LR) * (m_hat / (jnp.sqrt(v_hat) + ADAM_EPS) + WD * w)
    return delta, m, v


def _adam_big(w, acc, land, m, v, name):
    r, cd = w.shape
    rb = 256 if r % 256 == 0 else r

    def body(w_ref, acc_ref, land_ref, m_ref, v_ref, g_ref, d_ref, mo_ref, vo_ref):
        g = acc_ref[...]
        for j in range(NCHIP_OTHER):
            g = g + land_ref[j].astype(F32)
        g_ref[...] = g
        d_ref[...], mo_ref[...], vo_ref[...] = _adam_math(w_ref[...], g, m_ref[...], v_ref[...])

    blk = pl.BlockSpec((rb, cd), lambda i: (i, 0))
    blk3 = pl.BlockSpec((NCHIP_OTHER, rb, cd), lambda i: (0, i, 0))
    return pl.pallas_call(
        body, name=name, grid=(r // rb,), in_specs=[blk, blk, blk3, blk, blk], out_specs=[blk] * 4,
        out_shape=[jax.ShapeDtypeStruct((r, cd), F32)] * 4,
        compiler_params=_params(dimension_semantics=("arbitrary",)),
    )(w, acc, land, m, v)


def _adam_small(groups):
    n = len(groups)

    def body(*refs):
        ins, outs = refs[:4 * n], refs[4 * n:]
        for k in range(n):
            w_ref, g_ref, m_ref, v_ref = ins[4 * k:4 * k + 4]
            d, mo, vo = _adam_math(w_ref[...], g_ref[...], m_ref[...], v_ref[...])
            outs[3 * k][...] = d
            outs[3 * k + 1][...] = mo
            outs[3 * k + 2][...] = vo

    flat = [a for grp in groups for a in grp]
    shapes = [jax.ShapeDtypeStruct(grp[0].shape, F32) for grp in groups for _ in range(3)]
    res = pl.pallas_call(
        body, name="adam_small", in_specs=[_VMEM] * (4 * n), out_specs=[_VMEM] * (3 * n), out_shape=shapes,
        compiler_params=_params(),
    )(*flat)
    return [tuple(res[3 * k:3 * k + 3]) for k in range(n)]


TM_FWD_A = 256
TM_BWD_A = 256
TM_BWD_A_IN = 256
TM_FWD_B = 256
TM_HEAD = 512
TM_BWD_B = 256
PACK_ROWS = 536
_SMALL_ORDER = ("norm_w", "ln_w", "ln_b", "w_s", "b_s", "gate_a", "gate_x", "norm_f", "p8")


def kernel(x, norm_w, a_w_in, a_ln_w, a_ln_b, a_w_s, a_b_s, a_w_out, b_w_in, b_conv_w, b_conv_b, b_gate_a_w, b_gate_a_b, b_gate_x_w, b_gate_x_b, b_lambda, b_w_out, norm_f_w, loss_target, m_norm_w, m_a_w_in, m_a_ln_w, m_a_ln_b, m_a_w_s, m_a_b_s, m_a_w_out, m_b_w_in, m_b_conv_w, m_b_conv_b, m_b_gate_a_w, m_b_gate_a_b, m_b_gate_x_w, m_b_gate_x_b, m_b_lambda, m_b_w_out, m_norm_f_w, v_norm_w, v_a_w_in, v_a_ln_w, v_a_ln_b, v_a_w_s, v_a_b_s, v_a_w_out, v_b_w_in, v_b_conv_w, v_b_conv_b, v_b_gate_a_w, v_b_gate_a_b, v_b_gate_x_w, v_b_gate_x_b, v_b_lambda, v_b_w_out, v_norm_f_w):
    me = 4 * lax.axis_index("x") + 2 * lax.axis_index("y") + lax.axis_index("c")
    xs, tgt = x[0], loss_target[0]
    nw0, nw1, nfw = norm_w[0:1], norm_w[1:2], norm_f_w.reshape(1, D)
    w_s, bst = a_w_s[0], a_b_s[0].T
    gcat = jnp.concatenate([b_gate_a_w[0], b_gate_x_w[0]], axis=-1).astype(BF16)

    p8_shard = jnp.concatenate([b_conv_w[0], b_conv_b, b_gate_a_b, b_gate_x_b, b_lambda], axis=0)
    win_a8, p8_all = _all_gather([a_w_in[0].astype(BF16), p8_shard], "gather_first")
    p8 = jnp.transpose(p8_all, (1, 0, 2)).reshape(SUBLANES, BW)

    z, h0, ya, (wout_a8, win_b8) = _fwd_a(xs, nw0, win_a8, a_ln_w, a_ln_b, w_s, bst,
                                          [a_w_out[0].astype(BF16), b_w_in[0].astype(BF16)], tm=TM_FWD_A)
    wout_a = wout_a8.reshape(AW, D)
    x1, zb, hs, h1, yb, (wout_b8,) = _fwd_b(xs, ya, wout_a, nw1, win_b8, p8, gcat, [b_w_out[0].astype(BF16)],
                                            tm=TM_FWD_B)
    wout_b = wout_b8.reshape(BW, D)
    dx2, dx2b, loss, g_nfw = _head(x1, yb, wout_b, nfw, tgt, tm=TM_HEAD)

    dx1, dx1b, dzb, g_p8, g_gcat, g_nw1 = _bwd_b(dx2, zb, hs, x1, nw1, win_b8, p8, gcat, wout_b, tm=TM_BWD_B)
    p_wout_b = _wgrad_rows(yb, dx2b, nblk=NDEV, per=2, name="wgrad_b_out")
    p_win_b, _ = _wgrad_cols(h1, dzb, [], nblk=NDEV, name="wgrad_b_in")
    q_win_b, q_wout_b, acc_win_b, acc_wout_b = _reduce_in_chip([p_win_b, p_wout_b], "reduce_b_in_chip")

    dz, g_lnw, g_lnb, g_ws, g_bst, (l_win_b, l_wout_b) = _bwd_a(
        dx1b, z, a_ln_w, a_ln_b, w_s, bst, wout_a, [q_win_b, q_wout_b], tm=TM_BWD_A)
    p_wout_a = _wgrad_rows(ya, dx1b, nblk=NDEV, per=1, name="wgrad_a_out")
    q_wout_a, acc_wout_a = _reduce_in_chip([p_wout_a], "reduce_a_out_in_chip")
    p_win_a, (l_wout_a,) = _wgrad_cols(h0, dz, [q_wout_a], nblk=NDEV, name="wgrad_a_in")
    q_win_a, acc_win_a = _reduce_in_chip([p_win_a], "reduce_a_in_in_chip")
    gx, g_nw0, (l_win_a,) = _bwd_a_in(dz, dx1, xs, nw0, win_a8, [q_win_a], tm=TM_BWD_A_IN)

    small = dict(norm_w=jnp.concatenate([g_nw0, g_nw1], axis=0), ln_w=g_lnw, ln_b=g_lnb, w_s=g_ws,
                 b_s=g_bst.T, gate_a=g_gcat[:, :, :HD], gate_x=g_gcat[:, :, HD:], norm_f=g_nfw, p8=g_p8)
    sizes = {k: small[k].size for k in _SMALL_ORDER}
    flat = jnp.concatenate([small[k].reshape(-1) for k in _SMALL_ORDER])
    flat = jnp.pad(flat, (0, NDEV * PACK_ROWS * LANES - flat.shape[0]))
    q_small, acc_small = _reduce_in_chip([flat.reshape(NDEV, PACK_ROWS, LANES)], "reduce_small_in_chip")
    (l_small,) = _exchange([q_small], "reduce_small_across")
    packed = _sum_and_gather(acc_small, l_small, "gather_small")
    packed = packed.reshape(-1)
    red, off = {}, 0
    for k in _SMALL_ORDER:
        red[k] = packed[off:off + sizes[k]].reshape(small[k].shape)
        off += sizes[k]
    g_p8 = lax.dynamic_slice_in_dim(red["p8"], me * (BW // NDEV), BW // NDEV, axis=1)

    loss = lax.psum(loss[0, 0], ("x", "y", "c"))

    weights = dict(norm_w=norm_w, a_w_in=a_w_in, a_ln_w=a_ln_w, a_ln_b=a_ln_b, a_w_s=a_w_s, a_b_s=a_b_s, a_w_out=a_w_out,
                   b_w_in=b_w_in, b_conv_w=b_conv_w, b_conv_b=b_conv_b, b_gate_a_w=b_gate_a_w, b_gate_a_b=b_gate_a_b,
                   b_gate_x_w=b_gate_x_w, b_gate_x_b=b_gate_x_b, b_lambda=b_lambda, b_w_out=b_w_out, norm_f_w=norm_f_w)
    mom1 = dict(norm_w=m_norm_w, a_w_in=m_a_w_in, a_ln_w=m_a_ln_w, a_ln_b=m_a_ln_b, a_w_s=m_a_w_s, a_b_s=m_a_b_s,
                a_w_out=m_a_w_out, b_w_in=m_b_w_in, b_conv_w=m_b_conv_w, b_conv_b=m_b_conv_b, b_gate_a_w=m_b_gate_a_w,
                b_gate_a_b=m_b_gate_a_b, b_gate_x_w=m_b_gate_x_w, b_gate_x_b=m_b_gate_x_b, b_lambda=m_b_lambda,
                b_w_out=m_b_w_out, norm_f_w=m_norm_f_w)
    mom2 = dict(norm_w=v_norm_w, a_w_in=v_a_w_in, a_ln_w=v_a_ln_w, a_ln_b=v_a_ln_b, a_w_s=v_a_w_s, a_b_s=v_a_b_s,
                a_w_out=v_a_w_out, b_w_in=v_b_w_in, b_conv_w=v_b_conv_w, b_conv_b=v_b_conv_b, b_gate_a_w=v_b_gate_a_w,
                b_gate_a_b=v_b_gate_a_b, b_gate_x_w=v_b_gate_x_w, b_gate_x_b=v_b_gate_x_b, b_lambda=v_b_lambda,
                b_w_out=v_b_w_out, norm_f_w=v_norm_f_w)
    names = list(weights)

    def as2d(a):
        return a.reshape(-1, a.shape[-1])

    upd, grads = {}, {}
    for k, acc, land in (("a_w_in", acc_win_a, l_win_a), ("a_w_out", acc_wout_a, l_wout_a),
                         ("b_w_in", acc_win_b, l_win_b), ("b_w_out", acc_wout_b, l_wout_b)):
        g, d, mo, vo = _adam_big(as2d(weights[k]), acc, land, as2d(mom1[k]), as2d(mom2[k]), "adam_" + k)
        grads[k] = g[None]
        upd[k] = (d, mo, vo)
    grads.update(
        norm_w=red["norm_w"], a_ln_w=red["ln_w"], a_ln_b=red["ln_b"], a_w_s=red["w_s"][None], a_b_s=red["b_s"][None],
        b_conv_w=g_p8[None, 0:4], b_conv_b=g_p8[4:5], b_gate_a_w=red["gate_a"][None], b_gate_a_b=g_p8[5:6],
        b_gate_x_w=red["gate_x"][None], b_gate_x_b=g_p8[6:7], b_lambda=g_p8[7:8], norm_f_w=red["norm_f"].reshape(D))
    small_names = [k for k in names if k not in upd]
    res = _adam_small([(as2d(weights[k]), as2d(grads[k]), as2d(mom1[k]), as2d(mom2[k])) for k in small_names])
    for k, r3 in zip(small_names, res):
        upd[k] = r3
    deltas = [upd[k][0].reshape(weights[k].shape) for k in names]
    new_m = [upd[k][1].reshape(weights[k].shape) for k in names]
    new_v = [upd[k][2].reshape(weights[k].shape) for k in names]
    return (loss, gx[None], *[grads[k] for k in names], *deltas, *new_m, *new_v)
```

```python
import jax
import jax.numpy as jnp
from jax import lax
from jax.experimental import pallas as pl
from jax.experimental.pallas import tpu as pltpu

F32 = jnp.float32
BF16 = jnp.bfloat16
MESH = pl.DeviceIdType.MESH

NDEV = 8
NCHIP_OTHER = 3
D = 1024
AW = 2048
G = 8
GD = AW // G
CH = 128
BW = 1536
BH = 12
HD = BW // BH
CA = 3 * AW // NDEV
CB = 2 * BW // NDEV
RMS_EPS = 1e-6
LN_EPS = 1e-5
RG_C = 8.0
LR, B1, B2, ADAM_EPS, WD, STEP = 0.001, 0.9, 0.999, 1e-08, 0.01, 10
V7X_VMEM_BYTES = 64 * 1024 * 1024
VMEM_LIMIT = V7X_VMEM_BYTES - 8 * 1024 * 1024
SUBLANES = 8
LANES = 128
BF16_ROWS = 16
GELU_C = 0.7978845608028654
GELU_K = 0.044715

_VMEM = pl.BlockSpec(memory_space=pltpu.VMEM)
_HBM = pl.BlockSpec(memory_space=pltpu.HBM)


def _params(**kw):
    return pltpu.CompilerParams(vmem_limit_bytes=VMEM_LIMIT, **kw)


def _gelu_t(z):
    t = jnp.tanh(GELU_C * (z + GELU_K * (z * z * z)))
    return 0.5 * z * (1.0 + t), t


def _dgelu(z, t):
    return 0.5 * (1.0 + t) + 0.5 * z * (1.0 - t * t) * (GELU_C * (1.0 + 3.0 * GELU_K * z * z))


def _sigmoid(v):
    return 0.5 * jnp.tanh(0.5 * v) + 0.5


def _softplus_neg(lam):
    return jnp.maximum(-lam, 0.0) + jnp.log1p(jnp.exp(-jnp.abs(lam)))


def _dot(a, b):
    return jnp.dot(a, b, preferred_element_type=F32)


def _dot_nt(a, b):
    return lax.dot_general(a, b, (((1,), (1,)), ((), ())), preferred_element_type=F32)


def _rowsum(v):
    return jnp.sum(v, axis=0, keepdims=True)


def _causal_mask():
    r = lax.broadcasted_iota(jnp.int32, (CH, CH), 0)
    c = lax.broadcasted_iota(jnp.int32, (CH, CH), 1)
    return r >= c


def _rms(x):
    return lax.rsqrt(jnp.mean(x * x, axis=-1, keepdims=True) + RMS_EPS)


def _rms_bwd(dh, x, r, nw):
    gy = dh * nw
    return r * gy - x * (r * r * r) * jnp.mean(gy * x, axis=-1, keepdims=True)


def _place():
    return lax.axis_index("x"), lax.axis_index("y"), lax.axis_index("c")


def _other_chips(x, y):
    return [(1 - x, y), (x, 1 - y), (1 - x, 1 - y)]


def _gather_ops(ins, outs, send_sems, recv_sems, local_sems):
    n = len(ins)
    x, y, c = _place()
    sibling = (x, y, 1 - c)
    xn, yn, dg = _other_chips(x, y)
    split = [ins[i].shape[0] % (2 * BF16_ROWS) == 0 for i in range(n)]

    def blk(chip, core):
        return 4 * chip[0] + 2 * chip[1] + core

    me = blk((x, y), c)

    def part(ref, i, half):
        if half is None:
            return ref
        h = ins[i].shape[0] // 2
        return ref.at[pl.ds(half * h, h)]

    def copy(i, k, block, to, half=None, src=None):
        dst = part(outs[i].at[block], i, half)
        return pltpu.make_async_remote_copy(
            src_ref=dst if src is None else part(src, i, half), dst_ref=dst,
            send_sem=send_sems.at[k, i], recv_sem=recv_sems.at[k, i], device_id=to, device_id_type=MESH)

    def first_copies():
        mine = [pltpu.make_async_copy(ins[i], outs[i].at[me], local_sems.at[i]) for i in range(n)]
        first = []
        for i in range(n):
            first.append(copy(i, 0, me, sibling, src=ins[i]))
            if split[i]:
                first.append(copy(i, 1, me, (*xn, c), 0, ins[i]))
                first.append(copy(i, 3, me, (*yn, c), 1, ins[i]))
                first.append(copy(i, 2, me, (*xn, c), 1, ins[i]))
                first.append(copy(i, 4, me, (*yn, c), 0, ins[i]))
            else:
                first.append(copy(i, 1, me, (*xn, c), None, ins[i]))
                first.append(copy(i, 3, me, (*yn, c), None, ins[i]))
                first.append(copy(i, 5, me, (*dg, c), None, ins[i]))
        return mine, first

    def onward():
        out = []
        for i in range(n):
            if split[i]:
                out.append(copy(i, 5, blk(xn, c), (*yn, c), 0))
                out.append(copy(i, 6, blk(yn, c), (*xn, c), 1))
        return out

    def start():
        mine, first = first_copies()
        for cp in mine + first:
            cp.start()

    def relay():
        sends = onward()
        for i in range(n):
            if split[i]:
                copy(i, 1, blk(xn, c), sibling, 0).wait_recv()
                sends.pop(0).start()
                copy(i, 3, blk(yn, c), sibling, 1).wait_recv()
                sends.pop(0).start()

    def finish():
        mine, first = first_copies()
        passed = []
        for i in range(n):
            if split[i]:
                copy(i, 2, blk(xn, c), sibling, 1).wait_recv()
                copy(i, 4, blk(yn, c), sibling, 0).wait_recv()
                copy(i, 5, blk(dg, c), sibling, 0).wait_recv()
                copy(i, 6, blk(dg, c), sibling, 1).wait_recv()
            else:
                copy(i, 1, blk(xn, c), sibling).wait_recv()
                copy(i, 3, blk(yn, c), sibling).wait_recv()
                copy(i, 5, blk(dg, c), sibling).wait_recv()
            for j, chip in enumerate((xn, yn, dg)):
                fwd = copy(i, 7 + j, blk(chip, c), sibling)
                fwd.start()
                passed.append(fwd)
        for i in range(n):
            copy(i, 0, blk((x, y), 1 - c), sibling).wait_recv()
            for j, chip in enumerate((xn, yn, dg)):
                copy(i, 7 + j, blk(chip, 1 - c), sibling).wait_recv()
        for cp in first + passed + onward():
            cp.wait_send()
        for cp in mine:
            cp.wait()

    return start, relay, finish


GATHER_SLOTS = 10


def _gather_sems(n):
    return [pltpu.SemaphoreType.DMA((GATHER_SLOTS, n)), pltpu.SemaphoreType.DMA((GATHER_SLOTS, n)),
            pltpu.SemaphoreType.DMA((n,))]


def _gathered_shapes(shards):
    return [jax.ShapeDtypeStruct((NDEV,) + s.shape, s.dtype) for s in shards]


def _exchange_ops(srcs, dsts, send_sems, recv_sems):
    n = len(srcs)
    x, y, c = _place()
    chips = _other_chips(x, y)

    def copies():
        return [pltpu.make_async_remote_copy(
            src_ref=srcs[i].at[j], dst_ref=dsts[i].at[j], send_sem=send_sems.at[j, i], recv_sem=recv_sems.at[j, i],
            device_id=(*chips[j], c), device_id_type=MESH) for i in range(n) for j in range(NCHIP_OTHER)]

    def start():
        for cp in copies():
            cp.start()

    def finish():
        cps = copies()
        for cp in cps:
            cp.wait_recv()
        for cp in cps:
            cp.wait_send()

    return start, finish


def _exchange_sems(n):
    return [pltpu.SemaphoreType.DMA((NCHIP_OTHER, n)), pltpu.SemaphoreType.DMA((NCHIP_OTHER, n))]


def _all_gather(shards, name):
    n = len(shards)

    def body(*refs):
        start, relay, finish = _gather_ops(refs[:n], refs[n:2 * n], *refs[2 * n:])
        start()
        relay()
        finish()

    return pl.pallas_call(
        body, name=name, in_specs=[_HBM] * n, out_specs=[_HBM] * n, out_shape=_gathered_shapes(shards),
        scratch_shapes=_gather_sems(n), compiler_params=_params(),
    )(*shards)


def _reduce_in_chip(ps, name):
    n = len(ps)

    def body(*refs):
        p_refs, q_refs, acc_refs = refs[:n], refs[n:2 * n], refs[2 * n:3 * n]
        rest = refs[3 * n:]
        mines, lands = rest[:n], rest[n:2 * n]
        send_sems, recv_sems, local_sems = rest[2 * n:]
        x, y, c = _place()
        sibling = (x, y, 1 - c)
        pairs = []
        for i in range(n):
            for px in range(2):
                for py in range(2):
                    pi = 2 * px + py
                    loc = pltpu.make_async_copy(p_refs[i].at[4 * px + 2 * py + c], mines[i].at[pi],
                                                local_sems.at[pi, i])
                    cp = pltpu.make_async_remote_copy(
                        src_ref=p_refs[i].at[4 * px + 2 * py + (1 - c)], dst_ref=lands[i].at[pi],
                        send_sem=send_sems.at[pi, i], recv_sem=recv_sems.at[pi, i],
                        device_id=sibling, device_id_type=MESH)
                    loc.start()
                    cp.start()
                    pairs.append((loc, cp))
        for loc, cp in pairs:
            loc.wait()
            cp.wait_recv()
        for i in range(n):
            for j, (qx, qy) in enumerate(_other_chips(x, y)):
                qi = 2 * qx + qy
                q_refs[i][j] = (mines[i][qi].astype(F32) + lands[i][qi].astype(F32)).astype(q_refs[i].dtype)
            mi = 2 * x + y
            acc_refs[i][...] = mines[i][mi].astype(F32) + lands[i][mi].astype(F32)
        for _, cp in pairs:
            cp.wait_send()

    blk = [p.shape[1:] for p in ps]
    return pl.pallas_call(
        body, name=name, in_specs=[_HBM] * n, out_specs=[_VMEM] * (2 * n),
        out_shape=[jax.ShapeDtypeStruct((NCHIP_OTHER,) + b, p.dtype) for b, p in zip(blk, ps)]
        + [jax.ShapeDtypeStruct(b, F32) for b in blk],
        scratch_shapes=[pltpu.VMEM((4,) + b, p.dtype) for b, p in zip(blk, ps)]
        + [pltpu.VMEM((4,) + b, p.dtype) for b, p in zip(blk, ps)]
        + [pltpu.SemaphoreType.DMA((4, n)), pltpu.SemaphoreType.DMA((4, n)), pltpu.SemaphoreType.DMA((4, n))],
        compiler_params=_params(),
    )(*ps)


def _exchange(qs, name):
    n = len(qs)

    def body(*refs):
        start, finish = _exchange_ops(refs[:n], refs[n:2 * n], *refs[2 * n:])
        start()
        finish()

    return pl.pallas_call(
        body, name=name, in_specs=[_HBM] * n, out_specs=[_HBM] * n,
        out_shape=[jax.ShapeDtypeStruct(q.shape, q.dtype) for q in qs],
        scratch_shapes=_exchange_sems(n), compiler_params=_params(),
    )(*qs)


def _sum_and_gather(acc, land, name):
    def body(acc_ref, land_ref, out_ref, mine_scr, *sems):
        mine_scr[...] = acc_ref[...] + land_ref[0] + land_ref[1] + land_ref[2]
        start, relay, finish = _gather_ops([mine_scr], [out_ref], *sems)
        start()
        relay()
        finish()

    return pl.pallas_call(
        body, name=name, in_specs=[_VMEM, _VMEM], out_specs=_HBM,
        out_shape=jax.ShapeDtypeStruct((NDEV,) + acc.shape, acc.dtype),
        scratch_shapes=[pltpu.VMEM(acc.shape, acc.dtype)] + _gather_sems(1), compiler_params=_params(),
    )(acc, land)


def _fwd_a(x, nw, win8, lnw, lnb, ws, bst, shards, *, tm, relay_step):
    s_len = x.shape[0]
    nt = s_len // tm
    nch = tm // CH
    ng = len(shards)

    def body(*refs):
        x_ref, nw_ref, win_ref, lnw_ref, lnb_ref, ws_ref, bst_ref = refs[:7]
        sh_refs = refs[7:7 + ng]
        z_ref, h_ref, y_ref = refs[7 + ng:10 + ng]
        ga_refs = refs[10 + ng:10 + 2 * ng]
        wc_scr, gv_scr = refs[10 + 2 * ng:12 + 2 * ng]
        sems = refs[12 + 2 * ng:]
        i = pl.program_id(0)
        start, relay, finish = _gather_ops(sh_refs, ga_refs, *sems)

        @pl.when(i == 0)
        def _():
            start()
            m = _causal_mask()
            for g in range(G):
                wc_scr[g] = jnp.where(m, ws_ref[g], 0.0).astype(BF16)

        x = x_ref[...]
        h = (x * _rms(x) * nw_ref[...]).astype(BF16)
        h_ref[...] = h
        for k in range(NDEV):
            z_ref[:, k * CA:(k + 1) * CA] = _dot(h, win_ref[k])

        ssum = jnp.zeros((tm, 1), F32)
        for g in range(G):
            gv = _gelu_t(z_ref[:, AW + g * GD:AW + (g + 1) * GD])[0]
            gv_scr[:, g * GD:(g + 1) * GD] = gv
            ssum = ssum + jnp.sum(gv, axis=-1, keepdims=True)
        mu = ssum * (1.0 / AW)
        vsum = jnp.zeros((tm, 1), F32)
        for g in range(G):
            dlt = gv_scr[:, g * GD:(g + 1) * GD] - mu
            vsum = vsum + jnp.sum(dlt * dlt, axis=-1, keepdims=True)
        rstd = lax.rsqrt(vsum * (1.0 / AW) + LN_EPS)

        for g in range(G):
            cs = slice(g * GD, (g + 1) * GD)
            v = (gv_scr[:, cs] - mu) * rstd * lnw_ref[:, cs] + lnb_ref[:, cs]
            vb = v.astype(BF16)
            u = _gelu_t(z_ref[:, cs])[0]
            zg = z_ref[:, 2 * AW + g * GD:2 * AW + (g + 1) * GD]
            sg = zg * _sigmoid(zg)
            for n in range(nch):
                rs = slice(n * CH, (n + 1) * CH)
                s = _dot(wc_scr[g], vb[rs, :]) + bst_ref[:, g:g + 1]
                y_ref[rs, cs] = (u[rs, :] * s * sg[rs, :]).astype(BF16)

        @pl.when(i == min(relay_step, nt - 1))
        def _():
            relay()

        @pl.when(i == nt - 1)
        def _():
            finish()

    tile = lambda w: pl.BlockSpec((tm, w), lambda i: (i, 0))
    res = pl.pallas_call(
        body, name="fwd_a", grid=(nt,),
        in_specs=[tile(D), _VMEM, _VMEM, _VMEM, _VMEM, _VMEM, _VMEM] + [_HBM] * ng,
        out_specs=[tile(3 * AW), tile(D), tile(AW)] + [_HBM] * ng,
        out_shape=[jax.ShapeDtypeStruct((s_len, 3 * AW), F32), jax.ShapeDtypeStruct((s_len, D), BF16),
                   jax.ShapeDtypeStruct((s_len, AW), BF16)] + _gathered_shapes(shards),
        scratch_shapes=[pltpu.VMEM((G, CH, CH), BF16), pltpu.VMEM((tm, AW), F32)] + _gather_sems(ng),
        compiler_params=_params(dimension_semantics=("arbitrary",)),
    )(x, nw, win8, lnw, lnb, ws, bst, *shards)
    return res[0], res[1], res[2], res[3:]


def _bwd_a(dx1, z, lnw, lnb, ws, bst, wout, qs, *, tm):
    s_len = dx1.shape[0]
    nt = s_len // tm
    nch = tm // CH
    nq = len(qs)

    def body(*refs):
        dx1_ref, z_ref, lnw_ref, lnb_ref, ws_ref, bst_ref, wout_ref = refs[:7]
        q_refs = refs[7:7 + nq]
        dz_ref, glnw_ref, glnb_ref, gws_ref, gbst_ref = refs[7 + nq:12 + nq]
        land_refs = refs[12 + nq:12 + 2 * nq]
        (wc_scr, wct_scr, vh_scr, dgv_scr, dy_scr, dv_scr, gbs_acc, gwc_acc) = refs[12 + 2 * nq:20 + 2 * nq]
        sems = refs[20 + 2 * nq:]
        i = pl.program_id(0)
        start, finish = _exchange_ops(q_refs, land_refs, *sems)

        @pl.when(i == 0)
        def _():
            start()
            m = _causal_mask()
            for g in range(G):
                wm = jnp.where(m, ws_ref[g], 0.0)
                wc_scr[g] = wm.astype(BF16)
                wct_scr[g] = wm.T.astype(BF16)
            glnw_ref[...] = jnp.zeros_like(glnw_ref)
            glnb_ref[...] = jnp.zeros_like(glnb_ref)
            gbs_acc[...] = jnp.zeros_like(gbs_acc)
            gwc_acc[...] = jnp.zeros_like(gwc_acc)

        dy_scr[...] = _dot_nt(dx1_ref[...], wout_ref[...])

        ssum = jnp.zeros((tm, 1), F32)
        for g in range(G):
            cs = slice(g * GD, (g + 1) * GD)
            zv = z_ref[:, AW + g * GD:AW + (g + 1) * GD]
            gv, t = _gelu_t(zv)
            vh_scr[:, cs] = gv
            dgv_scr[:, cs] = _dgelu(zv, t)
            ssum = ssum + jnp.sum(gv, axis=-1, keepdims=True)
        mu = ssum * (1.0 / AW)
        vsum = jnp.zeros((tm, 1), F32)
        for g in range(G):
            dlt = vh_scr[:, g * GD:(g + 1) * GD] - mu
            vsum = vsum + jnp.sum(dlt * dlt, axis=-1, keepdims=True)
        rstd = lax.rsqrt(vsum * (1.0 / AW) + LN_EPS)

        m1 = jnp.zeros((tm, 1), F32)
        m2 = jnp.zeros((tm, 1), F32)
        for g in range(G):
            cs = slice(g * GD, (g + 1) * GD)
            gs = slice(2 * AW + g * GD, 2 * AW + (g + 1) * GD)
            vhat = (vh_scr[:, cs] - mu) * rstd
            vh_scr[:, cs] = vhat
            vb = (vhat * lnw_ref[:, cs] + lnb_ref[:, cs]).astype(BF16)
            zu = z_ref[:, cs]
            u, tu = _gelu_t(zu)
            zg = z_ref[:, gs]
            sig = _sigmoid(zg)
            sg = zg * sig
            dy = dy_scr[:, cs]
            dsf = dy * u * sg
            dsb = dsf.astype(BF16)
            dvs = []
            for n in range(nch):
                rs = slice(n * CH, (n + 1) * CH)
                s = _dot(wc_scr[g], vb[rs, :]) + bst_ref[:, g:g + 1]
                dys = dy[rs, :] * s
                dz_ref[rs, cs] = (dys * sg[rs, :] * _dgelu(zu[rs, :], tu[rs, :])).astype(BF16)
                dz_ref[rs, gs] = (dys * u[rs, :] * (sig[rs, :] * (1.0 + zg[rs, :] * (1.0 - sig[rs, :])))).astype(BF16)
                gbs_acc[g] += dsf[rs, :]
                gwc_acc[g] += _dot_nt(dsb[rs, :], vb[rs, :])
                dvs.append(_dot(wct_scr[g], dsb[rs, :]))
            dv = jnp.concatenate(dvs, axis=0) if nch > 1 else dvs[0]
            glnw_ref[:, cs] += _rowsum(dv * vhat)
            glnb_ref[:, cs] += _rowsum(dv)
            dvh = dv * lnw_ref[:, cs]
            dv_scr[:, cs] = dvh
            m1 = m1 + jnp.sum(dvh, axis=-1, keepdims=True)
            m2 = m2 + jnp.sum(dvh * vhat, axis=-1, keepdims=True)
        m1 = m1 * (1.0 / AW)
        m2 = m2 * (1.0 / AW)
        for g in range(G):
            cs = slice(g * GD, (g + 1) * GD)
            dgv = rstd * (dv_scr[:, cs] - m1 - vh_scr[:, cs] * m2)
            dz_ref[:, AW + g * GD:AW + (g + 1) * GD] = (dgv * dgv_scr[:, cs]).astype(BF16)

        @pl.when(i == nt - 1)
        def _():
            m = _causal_mask()
            for g in range(G):
                gws_ref[g] = jnp.where(m, gwc_acc[g], 0.0)
                gbst_ref[:, g:g + 1] = jnp.sum(gbs_acc[g], axis=-1, keepdims=True)
            finish()

    tile = lambda w: pl.BlockSpec((tm, w), lambda i: (i, 0))
    whole = lambda *s: pl.BlockSpec(s, lambda i: (0,) * len(s))
    res = pl.pallas_call(
        body, name="bwd_a", grid=(nt,),
        in_specs=[tile(D), tile(3 * AW), _VMEM, _VMEM, _VMEM, _VMEM, _VMEM] + [_HBM] * nq,
        out_specs=[tile(3 * AW), whole(1, AW), whole(1, AW), whole(G, CH, CH), whole(CH, G)] + [_HBM] * nq,
        out_shape=[jax.ShapeDtypeStruct((s_len, 3 * AW), BF16), jax.ShapeDtypeStruct((1, AW), F32),
                   jax.ShapeDtypeStruct((1, AW), F32), jax.ShapeDtypeStruct((G, CH, CH), F32),
                   jax.ShapeDtypeStruct((CH, G), F32)] + [jax.ShapeDtypeStruct(q.shape, q.dtype) for q in qs],
        scratch_shapes=[pltpu.VMEM((G, CH, CH), BF16), pltpu.VMEM((G, CH, CH), BF16),
                        pltpu.VMEM((tm, AW), F32), pltpu.VMEM((tm, AW), F32),
                        pltpu.VMEM((tm, AW), F32), pltpu.VMEM((tm, AW), F32),
                        pltpu.VMEM((G, CH, GD), F32), pltpu.VMEM((G, CH, CH), F32)] + _exchange_sems(nq),
        compiler_params=_params(dimension_semantics=("arbitrary",)),
    )(dx1, z, lnw, lnb, ws, bst, wout, *qs)
    return res[0], res[1], res[2], res[3], res[4], res[5:]


def _bwd_a_in(dz, dx1, x, nw, win8, qs, *, tm):
    s_len = x.shape[0]
    nt = s_len // tm
    nq = len(qs)

    def body(*refs):
        dz_ref, dx1_ref, x_ref, nw_ref, win_ref = refs[:5]
        q_refs = refs[5:5 + nq]
        gx_ref, gnw_ref = refs[5 + nq:7 + nq]
        land_refs = refs[7 + nq:7 + 2 * nq]
        sems = refs[7 + 2 * nq:]
        i = pl.program_id(0)
        start, finish = _exchange_ops(q_refs, land_refs, *sems)

        @pl.when(i == 0)
        def _():
            start()
            gnw_ref[...] = jnp.zeros_like(gnw_ref)

        dh = jnp.zeros((tm, D), F32)
        for k in range(NDEV):
            dh = dh + _dot_nt(dz_ref[:, k * CA:(k + 1) * CA], win_ref[k])
        x = x_ref[...]
        r = _rms(x)
        gx_ref[...] = dx1_ref[...] + _rms_bwd(dh, x, r, nw_ref[...])
        gnw_ref[...] += _rowsum(dh * x * r)

        @pl.when(i == nt - 1)
        def _():
            finish()

    tile = lambda w: pl.BlockSpec((tm, w), lambda i: (i, 0))
    res = pl.pallas_call(
        body, name="bwd_a_in", grid=(nt,),
        in_specs=[tile(3 * AW), tile(D), tile(D), _VMEM, _VMEM] + [_HBM] * nq,
        out_specs=[tile(D), pl.BlockSpec((1, D), lambda i: (0, 0))] + [_HBM] * nq,
        out_shape=[jax.ShapeDtypeStruct((s_len, D), F32), jax.ShapeDtypeStruct((1, D), F32)]
        + [jax.ShapeDtypeStruct(q.shape, q.dtype) for q in qs],
        scratch_shapes=_exchange_sems(nq),
        compiler_params=_params(dimension_semantics=("arbitrary",)),
    )(dz, dx1, x, nw, win8, *qs)
    return res[0], res[1], res[2:]


def _conv(p8_ref, cs, xb, xm1, xm2, xm3):
    xc = p8_ref[4:5, cs] + p8_ref[3:4, cs] * xb
    xc = xc + p8_ref[0:1, cs] * xm3
    xc = xc + p8_ref[1:2, cs] * xm2
    return xc + p8_ref[2:3, cs] * xm1


def _gates(p8_ref, gcat_ref, hh, xc):
    cs = slice(hh * HD, (hh + 1) * HD)
    pre = _dot(xc.astype(BF16), gcat_ref[hh])
    r = _sigmoid(pre[:, :HD] + p8_ref[5:6, cs])
    ig = _sigmoid(pre[:, HD:] + p8_ref[6:7, cs])
    sp = _softplus_neg(p8_ref[7:8, cs])
    la = (-RG_C) * r * sp
    a = jnp.exp(la)
    m2 = jnp.tanh(-la) * (1.0 + a * a)
    rm = lax.rsqrt(m2)
    mult = jnp.where(m2 > 0.0, m2 * rm, 0.0)
    return r, ig, sp, a, mult, rm


def _scan_rows(a_ref, b_ref, out_ref, carry, tm, reverse):
    row = lax.broadcasted_iota(jnp.int32, (SUBLANES, BW), 0)
    ngrp = tm // SUBLANES

    def step(j, cr):
        jj = (ngrp - 1 - j) if reverse else j
        off = pl.multiple_of(jj * SUBLANES, SUBLANES)
        a = a_ref[pl.ds(off, SUBLANES), :]
        b = b_ref[pl.ds(off, SUBLANES), :]
        for sh in (1, 2, 4):
            if reverse:
                a_s = pltpu.roll(a, SUBLANES - sh, 0)
                b_s = pltpu.roll(b, SUBLANES - sh, 0)
                m = row < SUBLANES - sh
            else:
                a_s = pltpu.roll(a, sh, 0)
                b_s = pltpu.roll(b, sh, 0)
                m = row >= sh
            b = jnp.where(m, a * b_s + b, b)
            a = jnp.where(m, a * a_s, a)
        o = b + a * cr
        out_ref[pl.ds(off, SUBLANES), :] = o
        return o[0:1, :] if reverse else o[SUBLANES - 1:SUBLANES, :]

    return lax.fori_loop(0, ngrp, step, carry)


def _fwd_b(x, ya, wout_a, nw, win8, p8, gcat, shards, *, tm, relay_step):
    s_len = x.shape[0]
    nt = s_len // tm
    ng = len(shards)

    def body(*refs):
        x_ref, ya_ref, wouta_ref, nw_ref, win_ref, p8_ref, gcat_ref = refs[:7]
        sh_refs = refs[7:7 + ng]
        x1_ref, zb_ref, hs_ref, h1_ref, yb_ref = refs[7 + ng:12 + ng]
        ga_refs = refs[12 + ng:12 + 2 * ng]
        xbe_scr, a_scr, b_scr, carry_scr = refs[12 + 2 * ng:16 + 2 * ng]
        sems = refs[16 + 2 * ng:]
        i = pl.program_id(0)
        start, relay, finish = _gather_ops(sh_refs, ga_refs, *sems)

        @pl.when(i == 0)
        def _():
            start()
            xbe_scr[0:SUBLANES, :] = jnp.zeros((SUBLANES, BW), F32)
            carry_scr[...] = jnp.zeros_like(carry_scr)

        x1 = x_ref[...] + _dot(ya_ref[...], wouta_ref[...])
        x1_ref[...] = x1
        h = (x1 * _rms(x1) * nw_ref[...]).astype(BF16)
        h1_ref[...] = h
        for k in range(NDEV):
            zb_ref[:, k * CB:(k + 1) * CB] = _dot(h, win_ref[k])
        xbe_scr[SUBLANES:SUBLANES + tm, :] = zb_ref[:, :BW]
        for hh in range(BH):
            cs = slice(hh * HD, (hh + 1) * HD)
            xc = _conv(p8_ref, cs, xbe_scr[SUBLANES:SUBLANES + tm, cs], xbe_scr[7:7 + tm, cs],
                       xbe_scr[6:6 + tm, cs], xbe_scr[5:5 + tm, cs])
            _, ig, _, a, mult, _ = _gates(p8_ref, gcat_ref, hh, xc)
            a_scr[:, cs] = a
            b_scr[:, cs] = mult * (ig * xc)
        xbe_scr[0:SUBLANES, :] = xbe_scr[tm:tm + SUBLANES, :]
        carry_scr[...] = _scan_rows(a_scr, b_scr, hs_ref, carry_scr[...], tm, False)
        for hh in range(BH):
            cs = slice(hh * HD, (hh + 1) * HD)
            gt = zb_ref[:, BW + hh * HD:BW + (hh + 1) * HD]
            yb_ref[:, cs] = (hs_ref[:, cs] * (gt * _sigmoid(gt))).astype(BF16)

        @pl.when(i == min(relay_step, nt - 1))
        def _():
            relay()

        @pl.when(i == nt - 1)
        def _():
            finish()

    tile = lambda w: pl.BlockSpec((tm, w), lambda i: (i, 0))
    res = pl.pallas_call(
        body, name="fwd_b", grid=(nt,),
        in_specs=[tile(D), tile(AW), _VMEM, _VMEM, _VMEM, _VMEM, _VMEM] + [_HBM] * ng,
        out_specs=[tile(D), tile(2 * BW), tile(BW), tile(D), tile(BW)] + [_HBM] * ng,
        out_shape=[jax.ShapeDtypeStruct((s_len, D), F32), jax.ShapeDtypeStruct((s_len, 2 * BW), F32),
                   jax.ShapeDtypeStruct((s_len, BW), F32), jax.ShapeDtypeStruct((s_len, D), BF16),
                   jax.ShapeDtypeStruct((s_len, BW), BF16)] + _gathered_shapes(shards),
        scratch_shapes=[pltpu.VMEM((tm + SUBLANES, BW), F32), pltpu.VMEM((tm, BW), F32),
                        pltpu.VMEM((tm, BW), F32), pltpu.VMEM((1, BW), F32)] + _gather_sems(ng),
        compiler_params=_params(dimension_semantics=("arbitrary",)),
    )(x, ya, wout_a, nw, win8, p8, gcat, *shards)
    return res[0], res[1], res[2], res[3], res[4], res[5:]


def _head(x1, yb, wout, nfw, tgt, *, tm):
    s_len = x1.shape[0]

    def body(x1_ref, yb_ref, wout_ref, nfw_ref, t_ref, dx2_ref, dx2b_ref, loss_ref, gnfw_ref):
        @pl.when(pl.program_id(0) == 0)
        def _():
            loss_ref[...] = jnp.zeros_like(loss_ref)
            gnfw_ref[...] = jnp.zeros_like(gnfw_ref)

        x2 = x1_ref[...] + _dot(yb_ref[...], wout_ref[...])
        rf = _rms(x2)
        xn = x2 * rf
        e = xn * nfw_ref[...] - t_ref[...]
        loss_ref[...] += (0.5 / D) * jnp.sum(jnp.sum(e * e, axis=-1, keepdims=True), axis=0, keepdims=True)
        dyf = e * (1.0 / D)
        gnfw_ref[...] += _rowsum(dyf * xn)
        dx2 = _rms_bwd(dyf, x2, rf, nfw_ref[...])
        dx2_ref[...] = dx2
        dx2b_ref[...] = dx2.astype(BF16)

    tile = lambda w: pl.BlockSpec((tm, w), lambda i: (i, 0))
    whole = lambda *s: pl.BlockSpec(s, lambda i: (0,) * len(s))
    return pl.pallas_call(
        body, name="head", grid=(s_len // tm,),
        in_specs=[tile(D), tile(BW), _VMEM, _VMEM, tile(D)],
        out_specs=[tile(D), tile(D), whole(1, 1), whole(1, D)],
        out_shape=[jax.ShapeDtypeStruct((s_len, D), F32), jax.ShapeDtypeStruct((s_len, D), BF16),
                   jax.ShapeDtypeStruct((1, 1), F32), jax.ShapeDtypeStruct((1, D), F32)],
        compiler_params=_params(dimension_semantics=("arbitrary",)),
    )(x1, yb, wout, nfw, tgt)


def _bwd_b(dx2, zb, hs, x1, nw, win8, p8, gcat, wout, *, tm):
    s_len = x1.shape[0]
    nt = s_len // tm
    per = tm // SUBLANES

    def body(dx2_ref, zb_ref, zbp_ref, hs_ref, hsp_ref, x1_ref, nw_ref, win_ref, p8_ref, gcat_ref, wout_ref,
             dx1_ref, dx1b_ref, dzb_ref, gp8_ref, gg_ref, gnw_ref,
             xbe_scr, hse_scr, ae_scr, an_scr, r_scr, i_scr, m_scr, xc_scr, cc_scr, dhd_scr, dh_scr, dy_scr, dxce_scr,
             carry_scr, afirst_scr):
        i = pl.program_id(0)
        ti = nt - 1 - i

        @pl.when(i == 0)
        def _():
            gp8_ref[...] = jnp.zeros_like(gp8_ref)
            gg_ref[...] = jnp.zeros_like(gg_ref)
            gnw_ref[...] = jnp.zeros_like(gnw_ref)
            dxce_scr[tm:tm + SUBLANES, :] = jnp.zeros((SUBLANES, BW), F32)
            carry_scr[...] = jnp.zeros_like(carry_scr)
            afirst_scr[...] = jnp.zeros_like(afirst_scr)

        has_prev = (ti > 0).astype(F32)
        xbe_scr[0:SUBLANES, :] = zbp_ref[:, :BW] * has_prev
        xbe_scr[SUBLANES:SUBLANES + tm, :] = zb_ref[:, :BW]
        hse_scr[0:SUBLANES, :] = hsp_ref[...] * has_prev
        hse_scr[SUBLANES:SUBLANES + tm, :] = hs_ref[...]

        dx2 = dx2_ref[...]
        dy_scr[...] = _dot_nt(dx2.astype(BF16), wout_ref[...])

        for hh in range(BH):
            cs = slice(hh * HD, (hh + 1) * HD)
            xc = _conv(p8_ref, cs, xbe_scr[SUBLANES:SUBLANES + tm, cs], xbe_scr[7:7 + tm, cs],
                       xbe_scr[6:6 + tm, cs], xbe_scr[5:5 + tm, cs])
            r, ig, _, a, mult, rm = _gates(p8_ref, gcat_ref, hh, xc)
            cc_scr[:, cs] = a * hse_scr[7:7 + tm, cs] - (ig * xc) * (a * a * rm)
            xc_scr[:, cs] = xc
            r_scr[:, cs] = r
            i_scr[:, cs] = ig
            m_scr[:, cs] = mult
            ae_scr[0:tm, cs] = a
            gt = zb_ref[:, BW + hh * HD:BW + (hh + 1) * HD]
            sig = _sigmoid(gt)
            dy = dy_scr[:, cs]
            dhd_scr[:, cs] = dy * (gt * sig)
            dzb_ref[:, BW + hh * HD:BW + (hh + 1) * HD] = (
                dy * hs_ref[:, cs] * (sig * (1.0 + gt * (1.0 - sig)))).astype(BF16)
        ae_scr[tm:tm + SUBLANES, :] = jnp.broadcast_to(afirst_scr[...], (SUBLANES, BW))
        an_scr[...] = ae_scr[1:1 + tm, :]
        afirst_scr[...] = ae_scr[0:1, :]
        carry_scr[...] = _scan_rows(an_scr, dhd_scr, dh_scr, carry_scr[...], tm, True)

        for hh in range(BH):
            cs = slice(hh * HD, (hh + 1) * HD)
            dh = dh_scr[:, cs]
            mult = m_scr[:, cs]
            ig = i_scr[:, cs]
            r = r_scr[:, cs]
            xc = xc_scr[:, cs]
            lam = p8_ref[7:8, cs]
            sp = _softplus_neg(lam)
            dla = dh * cc_scr[:, cs]
            gp8_ref[7:8, cs] += _rowsum(dla * ((-RG_C) * r)) * (-_sigmoid(-lam))
            dpr = dla * ((-RG_C) * sp) * (r * (1.0 - r))
            dpi = dh * mult * xc * (ig * (1.0 - ig))
            gp8_ref[5:6, cs] += _rowsum(dpr)
            gp8_ref[6:7, cs] += _rowsum(dpi)
            dcat = jnp.concatenate([dpr, dpi], axis=1).astype(BF16)
            dxc = dh * mult * ig + _dot_nt(dcat, gcat_ref[hh])
            gg_ref[hh] += _dot(xc.T.astype(BF16), dcat)
            dxce_scr[0:tm, cs] = dxc
            gp8_ref[4:5, cs] += _rowsum(dxc)
            gp8_ref[3:4, cs] += _rowsum(dxc * xbe_scr[SUBLANES:SUBLANES + tm, cs])
            gp8_ref[2:3, cs] += _rowsum(dxc * xbe_scr[7:7 + tm, cs])
            gp8_ref[1:2, cs] += _rowsum(dxc * xbe_scr[6:6 + tm, cs])
            gp8_ref[0:1, cs] += _rowsum(dxc * xbe_scr[5:5 + tm, cs])
        for hh in range(BH):
            cs = slice(hh * HD, (hh + 1) * HD)
            dxb = p8_ref[3:4, cs] * dxce_scr[0:tm, cs]
            dxb = dxb + p8_ref[2:3, cs] * dxce_scr[1:1 + tm, cs]
            dxb = dxb + p8_ref[1:2, cs] * dxce_scr[2:2 + tm, cs]
            dxb = dxb + p8_ref[0:1, cs] * dxce_scr[3:3 + tm, cs]
            dzb_ref[:, cs] = dxb.astype(BF16)
        dxce_scr[tm:tm + SUBLANES, :] = dxce_scr[0:SUBLANES, :]

        dh1 = jnp.zeros((tm, D), F32)
        for k in range(NDEV):
            dh1 = dh1 + _dot_nt(dzb_ref[:, k * CB:(k + 1) * CB], win_ref[k])
        x1 = x1_ref[...]
        r1 = _rms(x1)
        dx1 = dx2 + _rms_bwd(dh1, x1, r1, nw_ref[...])
        dx1_ref[...] = dx1
        dx1b_ref[...] = dx1.astype(BF16)
        gnw_ref[...] += _rowsum(dh1 * x1 * r1)

    tile = lambda w: pl.BlockSpec((tm, w), lambda i: (nt - 1 - i, 0))
    prev = lambda w: pl.BlockSpec((SUBLANES, w), lambda i: (jnp.maximum((nt - 1 - i) * per - 1, 0), 0))
    whole = lambda *s: pl.BlockSpec(s, lambda i: (0,) * len(s))
    full = lambda: pltpu.VMEM((tm, BW), F32)
    ext = lambda: pltpu.VMEM((tm + SUBLANES, BW), F32)
    return pl.pallas_call(
        body, name="bwd_b", grid=(nt,),
        in_specs=[tile(D), tile(2 * BW), prev(2 * BW), tile(BW), prev(BW), tile(D),
                  _VMEM, _VMEM, _VMEM, _VMEM, _VMEM],
        out_specs=[tile(D), tile(D), tile(2 * BW), whole(SUBLANES, BW), whole(BH, HD, 2 * HD), whole(1, D)],
        out_shape=[jax.ShapeDtypeStruct((s_len, D), F32), jax.ShapeDtypeStruct((s_len, D), BF16),
                   jax.ShapeDtypeStruct((s_len, 2 * BW), BF16), jax.ShapeDtypeStruct((SUBLANES, BW), F32),
                   jax.ShapeDtypeStruct((BH, HD, 2 * HD), F32), jax.ShapeDtypeStruct((1, D), F32)],
        scratch_shapes=[ext(), ext(), ext(), full(), full(), full(), full(), full(), full(), full(), full(), full(),
                        ext(), pltpu.VMEM((1, BW), F32), pltpu.VMEM((1, BW), F32)],
        compiler_params=_params(dimension_semantics=("arbitrary",)),
    )(dx2, zb, zb, hs, hs, x1, nw, win8, p8, gcat, wout)


def _transpose_into(dst_ref, src_ref, rows):
    s_len = src_ref.shape[0]
    for r0 in range(0, s_len, rows):
        dst_ref[:, r0:r0 + rows] = src_ref[r0:r0 + rows, :].astype(F32).T.astype(BF16)


def _wgrad_cols(a, b, qs, *, nblk, name):
    s_len, m = a.shape
    bn = b.shape[1] // nblk
    nq = len(qs)

    def body(*refs):
        a_ref, b_ref = refs[:2]
        q_refs = refs[2:2 + nq]
        o_ref = refs[2 + nq]
        land_refs = refs[3 + nq:3 + 2 * nq]
        at_scr = refs[3 + 2 * nq]
        sems = refs[4 + 2 * nq:]
        i = pl.program_id(0)
        if nq:
            start, finish = _exchange_ops(q_refs, land_refs, *sems)

        @pl.when(i == 0)
        def _():
            if nq:
                start()
            _transpose_into(at_scr, a_ref, 256)

        o_ref[0] = _dot(at_scr[...], b_ref[...]).astype(BF16)

        if nq:
            @pl.when(i == nblk - 1)
            def _():
                finish()

    res = pl.pallas_call(
        body, name=name, grid=(nblk,),
        in_specs=[_VMEM, pl.BlockSpec((s_len, bn), lambda j: (0, j))] + [_HBM] * nq,
        out_specs=[pl.BlockSpec((1, m, bn), lambda j: (j, 0, 0))] + [_HBM] * nq,
        out_shape=[jax.ShapeDtypeStruct((nblk, m, bn), BF16)] + [jax.ShapeDtypeStruct(q.shape, q.dtype) for q in qs],
        scratch_shapes=[pltpu.VMEM((m, s_len), BF16)] + (_exchange_sems(nq) if nq else []),
        compiler_params=_params(dimension_semantics=("arbitrary",)),
    )(a, b, *qs)
    return res[0], res[1:]


def _wgrad_rows(a, b, *, nblk, per, name):
    s_len, m = a.shape
    n = b.shape[1]
    rb = m // nblk
    bm = per * rb

    def body(a_ref, b_ref, o_ref, at_scr):
        _transpose_into(at_scr, a_ref, 256)
        res = _dot(at_scr[...], b_ref[...]).astype(BF16)
        for q in range(per):
            o_ref[q] = res[q * rb:(q + 1) * rb, :]

    return pl.pallas_call(
        body, name=name, grid=(nblk // per,),
        in_specs=[pl.BlockSpec((s_len, bm), lambda j: (0, j)), _VMEM],
        out_specs=pl.BlockSpec((per, rb, n), lambda j: (j, 0, 0)),
        out_shape=jax.ShapeDtypeStruct((nblk, rb, n), BF16),
        scratch_shapes=[pltpu.VMEM((bm, s_len), BF16)],
        compiler_params=_params(dimension_semantics=("arbitrary",)),
    )(a, b)


def _adam_math(w, g, m, v):
    m = B1 * m + (1.0 - B1) * g
    v = B2 * v + (1.0 - B2) * (g * g)
    m_hat = m / (1.0 - B1 ** STEP)
    v_hat = v / (1.0 - B2 ** STEP)
    delta = (-LR) * (m_hat / (jnp.sqrt(v_hat) + ADAM_EPS) + WD * w)
    return delta, m, v


def _adam_big(w, acc, land, m, v, name):
    r, cd = w.shape
    rb = 256 if r % 256 == 0 else r

    def body(w_ref, acc_ref, land_ref, m_ref, v_ref, g_ref, d_ref, mo_ref, vo_ref):
        g = acc_ref[...]
        for j in range(NCHIP_OTHER):
            g = g + land_ref[j].astype(F32)
        g_ref[...] = g
        d_ref[...], mo_ref[...], vo_ref[...] = _adam_math(w_ref[...], g, m_ref[...], v_ref[...])

    blk = pl.BlockSpec((rb, cd), lambda i: (i, 0))
    blk3 = pl.BlockSpec((NCHIP_OTHER, rb, cd), lambda i: (0, i, 0))
    return pl.pallas_call(
        body, name=name, grid=(r // rb,), in_specs=[blk, blk, blk3, blk, blk], out_specs=[blk] * 4,
        out_shape=[jax.ShapeDtypeStruct((r, cd), F32)] * 4,
        compiler_params=_params(dimension_semantics=("arbitrary",)),
    )(w, acc, land, m, v)


def _adam_small(groups):
    n = len(groups)

    def body(*refs):
        ins, outs = refs[:4 * n], refs[4 * n:]
        for k in range(n):
            w_ref, g_ref, m_ref, v_ref = ins[4 * k:4 * k + 4]
            d, mo, vo = _adam_math(w_ref[...], g_ref[...], m_ref[...], v_ref[...])
            outs[3 * k][...] = d
            outs[3 * k + 1][...] = mo
            outs[3 * k + 2][...] = vo

    flat = [a for grp in groups for a in grp]
    shapes = [jax.ShapeDtypeStruct(grp[0].shape, F32) for grp in groups for _ in range(3)]
    res = pl.pallas_call(
        body, name="adam_small", in_specs=[_VMEM] * (4 * n), out_specs=[_VMEM] * (3 * n), out_shape=shapes,
        compiler_params=_params(),
    )(*flat)
    return [tuple(res[3 * k:3 * k + 3]) for k in range(n)]


TM_FWD_A = 256
RELAY_STEP_FWD_A = 4
RELAY_STEP_FWD_B = 2
TM_BWD_A = 256
TM_BWD_A_IN = 256
TM_FWD_B = 256
TM_HEAD = 512
TM_BWD_B = 256
PACK_ROWS = 536
_SMALL_ORDER = ("norm_w", "ln_w", "ln_b", "w_s", "b_s", "gate_a", "gate_x", "norm_f", "p8")


def kernel(x, norm_w, a_w_in, a_ln_w, a_ln_b, a_w_s, a_b_s, a_w_out, b_w_in, b_conv_w, b_conv_b, b_gate_a_w, b_gate_a_b, b_gate_x_w, b_gate_x_b, b_lambda, b_w_out, norm_f_w, loss_target, m_norm_w, m_a_w_in, m_a_ln_w, m_a_ln_b, m_a_w_s, m_a_b_s, m_a_w_out, m_b_w_in, m_b_conv_w, m_b_conv_b, m_b_gate_a_w, m_b_gate_a_b, m_b_gate_x_w, m_b_gate_x_b, m_b_lambda, m_b_w_out, m_norm_f_w, v_norm_w, v_a_w_in, v_a_ln_w, v_a_ln_b, v_a_w_s, v_a_b_s, v_a_w_out, v_b_w_in, v_b_conv_w, v_b_conv_b, v_b_gate_a_w, v_b_gate_a_b, v_b_gate_x_w, v_b_gate_x_b, v_b_lambda, v_b_w_out, v_norm_f_w):
    me = 4 * lax.axis_index("x") + 2 * lax.axis_index("y") + lax.axis_index("c")
    xs, tgt = x[0], loss_target[0]
    nw0, nw1, nfw = norm_w[0:1], norm_w[1:2], norm_f_w.reshape(1, D)
    w_s, bst = a_w_s[0], a_b_s[0].T
    gcat = jnp.concatenate([b_gate_a_w[0], b_gate_x_w[0]], axis=-1).astype(BF16)

    p8_shard = jnp.concatenate([b_conv_w[0], b_conv_b, b_gate_a_b, b_gate_x_b, b_lambda], axis=0)
    win_a8, p8_all = _all_gather([a_w_in[0].astype(BF16), p8_shard], "gather_first")
    p8 = jnp.transpose(p8_all, (1, 0, 2)).reshape(SUBLANES, BW)

    z, h0, ya, (wout_a8, win_b8) = _fwd_a(xs, nw0, win_a8, a_ln_w, a_ln_b, w_s, bst,
                                          [a_w_out[0].astype(BF16), b_w_in[0].astype(BF16)], tm=TM_FWD_A,
                                          relay_step=RELAY_STEP_FWD_A)
    wout_a = wout_a8.reshape(AW, D)
    x1, zb, hs, h1, yb, (wout_b8,) = _fwd_b(xs, ya, wout_a, nw1, win_b8, p8, gcat, [b_w_out[0].astype(BF16)],
                                            tm=TM_FWD_B, relay_step=RELAY_STEP_FWD_B)
    wout_b = wout_b8.reshape(BW, D)
    dx2, dx2b, loss, g_nfw = _head(x1, yb, wout_b, nfw, tgt, tm=TM_HEAD)

    dx1, dx1b, dzb, g_p8, g_gcat, g_nw1 = _bwd_b(dx2, zb, hs, x1, nw1, win_b8, p8, gcat, wout_b, tm=TM_BWD_B)
    p_wout_b = _wgrad_rows(yb, dx2b, nblk=NDEV, per=2, name="wgrad_b_out")
    p_win_b, _ = _wgrad_cols(h1, dzb, [], nblk=NDEV, name="wgrad_b_in")
    q_win_b, q_wout_b, acc_win_b, acc_wout_b = _reduce_in_chip([p_win_b, p_wout_b], "reduce_b_in_chip")

    dz, g_lnw, g_lnb, g_ws, g_bst, (l_win_b, l_wout_b) = _bwd_a(
        dx1b, z, a_ln_w, a_ln_b, w_s, bst, wout_a, [q_win_b, q_wout_b], tm=TM_BWD_A)
    p_wout_a = _wgrad_rows(ya, dx1b, nblk=NDEV, per=1, name="wgrad_a_out")
    q_wout_a, acc_wout_a = _reduce_in_chip([p_wout_a], "reduce_a_out_in_chip")
    p_win_a, (l_wout_a,) = _wgrad_cols(h0, dz, [q_wout_a], nblk=NDEV, name="wgrad_a_in")
    q_win_a, acc_win_a = _reduce_in_chip([p_win_a], "reduce_a_in_in_chip")
    gx, g_nw0, (l_win_a,) = _bwd_a_in(dz, dx1, xs, nw0, win_a8, [q_win_a], tm=TM_BWD_A_IN)

    small = dict(norm_w=jnp.concatenate([g_nw0, g_nw1], axis=0), ln_w=g_lnw, ln_b=g_lnb, w_s=g_ws,
                 b_s=g_bst.T, gate_a=g_gcat[:, :, :HD], gate_x=g_gcat[:, :, HD:], norm_f=g_nfw, p8=g_p8)
    sizes = {k: small[k].size for k in _SMALL_ORDER}
    flat = jnp.concatenate([small[k].reshape(-1) for k in _SMALL_ORDER])
    flat = jnp.pad(flat, (0, NDEV * PACK_ROWS * LANES - flat.shape[0]))
    q_small, acc_small = _reduce_in_chip([flat.reshape(NDEV, PACK_ROWS, LANES)], "reduce_small_in_chip")
    (l_small,) = _exchange([q_small], "reduce_small_across")
    packed = _sum_and_gather(acc_small, l_small, "gather_small")
    packed = packed.reshape(-1)
    red, off = {}, 0
    for k in _SMALL_ORDER:
        red[k] = packed[off:off + sizes[k]].reshape(small[k].shape)
        off += sizes[k]
    g_p8 = lax.dynamic_slice_in_dim(red["p8"], me * (BW // NDEV), BW // NDEV, axis=1)

    loss = lax.psum(loss[0, 0], ("x", "y", "c"))

    weights = dict(norm_w=norm_w, a_w_in=a_w_in, a_ln_w=a_ln_w, a_ln_b=a_ln_b, a_w_s=a_w_s, a_b_s=a_b_s, a_w_out=a_w_out,
                   b_w_in=b_w_in, b_conv_w=b_conv_w, b_conv_b=b_conv_b, b_gate_a_w=b_gate_a_w, b_gate_a_b=b_gate_a_b,
                   b_gate_x_w=b_gate_x_w, b_gate_x_b=b_gate_x_b, b_lambda=b_lambda, b_w_out=b_w_out, norm_f_w=norm_f_w)
    mom1 = dict(norm_w=m_norm_w, a_w_in=m_a_w_in, a_ln_w=m_a_ln_w, a_ln_b=m_a_ln_b, a_w_s=m_a_w_s, a_b_s=m_a_b_s,
                a_w_out=m_a_w_out, b_w_in=m_b_w_in, b_conv_w=m_b_conv_w, b_conv_b=m_b_conv_b, b_gate_a_w=m_b_gate_a_w,
                b_gate_a_b=m_b_gate_a_b, b_gate_x_w=m_b_gate_x_w, b_gate_x_b=m_b_gate_x_b, b_lambda=m_b_lambda,
                b_w_out=m_b_w_out, norm_f_w=m_norm_f_w)
    mom2 = dict(norm_w=v_norm_w, a_w_in=v_a_w_in, a_ln_w=v_a_ln_w, a_ln_b=v_a_ln_b, a_w_s=v_a_w_s, a_b_s=v_a_b_s,
                a_w_out=v_a_w_out, b_w_in=v_b_w_in, b_conv_w=v_b_conv_w, b_conv_b=v_b_conv_b, b_gate_a_w=v_b_gate_a_w,
                b_gate_a_b=v_b_gate_a_b, b_gate_x_w=v_b_gate_x_w, b_gate_x_b=v_b_gate_x_b, b_lambda=v_b_lambda,
                b_w_out=v_b_w_out, norm_f_w=v_norm_f_w)
    names = list(weights)

    def as2d(a):
        return a.reshape(-1, a.shape[-1])

    upd, grads = {}, {}
    for k, acc, land in (("a_w_in", acc_win_a, l_win_a), ("a_w_out", acc_wout_a, l_wout_a),
                         ("b_w_in", acc_win_b, l_win_b), ("b_w_out", acc_wout_b, l_wout_b)):
        g, d, mo, vo = _adam_big(as2d(weights[k]), acc, land, as2d(mom1[k]), as2d(mom2[k]), "adam_" + k)
        grads[k] = g[None]
        upd[k] = (d, mo, vo)
    grads.update(
        norm_w=red["norm_w"], a_ln_w=red["ln_w"], a_ln_b=red["ln_b"], a_w_s=red["w_s"][None], a_b_s=red["b_s"][None],
        b_conv_w=g_p8[None, 0:4], b_conv_b=g_p8[4:5], b_gate_a_w=red["gate_a"][None], b_gate_a_b=g_p8[5:6],
        b_gate_x_w=red["gate_x"][None], b_gate_x_b=g_p8[6:7], b_lambda=g_p8[7:8], norm_f_w=red["norm_f"].reshape(D))
    small_names = [k for k in names if k not in upd]
    res = _adam_small([(as2d(weights[k]), as2d(grads[k]), as2d(mom1[k]), as2d(mom2[k])) for k in small_names])
    for k, r3 in zip(small_names, res):
        upd[k] = r3
    deltas = [upd[k][0].reshape(weights[k].shape) for k in names]
    new_m = [upd[k][1].reshape(weights[k].shape) for k in names]
    new_v = [upd[k][2].reshape(weights[k].shape) for k in names]
    return (loss, gx[None], *[grads[k] for k in names], *deltas, *new_m, *new_v)
```

```python
import jax
import jax.numpy as jnp
from jax import lax
from jax.experimental import pallas as pl
from jax.experimental.pallas import tpu as pltpu

F32 = jnp.float32
BF16 = jnp.bfloat16
MESH = pl.DeviceIdType.MESH

NDEV = 8
NCHIP_OTHER = 3
D = 1024
AW = 2048
G = 8
GD = AW // G
CH = 128
BW = 1536
BH = 12
HD = BW // BH
CA = 3 * AW // NDEV
CB = 2 * BW // NDEV
RMS_EPS = 1e-6
LN_EPS = 1e-5
RG_C = 8.0
LR, B1, B2, ADAM_EPS, WD, STEP = 0.001, 0.9, 0.999, 1e-08, 0.01, 10
V7X_VMEM_BYTES = 64 * 1024 * 1024
VMEM_LIMIT = V7X_VMEM_BYTES - 8 * 1024 * 1024
SUBLANES = 8
LANES = 128
BF16_ROWS = 16
GELU_C = 0.7978845608028654
GELU_K = 0.044715

_VMEM = pl.BlockSpec(memory_space=pltpu.VMEM)
_HBM = pl.BlockSpec(memory_space=pltpu.HBM)


def _sds(shape, dtype):
    return jax.ShapeDtypeStruct(tuple(shape), dtype)


def _params(**kw):
    return pltpu.CompilerParams(vmem_limit_bytes=VMEM_LIMIT, **kw)


def _gelu_t(z):
    t = jnp.tanh(GELU_C * (z + GELU_K * (z * z * z)))
    return 0.5 * z * (1.0 + t), t


def _dgelu(z, t):
    return 0.5 * (1.0 + t) + 0.5 * z * (1.0 - t * t) * (GELU_C * (1.0 + 3.0 * GELU_K * z * z))


def _sigmoid(v):
    return 0.5 * jnp.tanh(0.5 * v) + 0.5


def _softplus_neg(lam):
    return jnp.maximum(-lam, 0.0) + jnp.log1p(jnp.exp(-jnp.abs(lam)))


def _dot(a, b):
    return jnp.dot(a, b, preferred_element_type=F32)


def _dot_nt(a, b):
    return lax.dot_general(a, b, (((1,), (1,)), ((), ())), preferred_element_type=F32)


def _rowsum(v):
    return jnp.sum(v, axis=0, keepdims=True)


def _causal_mask():
    r = lax.broadcasted_iota(jnp.int32, (CH, CH), 0)
    c = lax.broadcasted_iota(jnp.int32, (CH, CH), 1)
    return r >= c


def _rms(x):
    return lax.rsqrt(jnp.mean(x * x, axis=-1, keepdims=True) + RMS_EPS)


def _rms_bwd(dh, x, r, nw):
    gy = dh * nw
    return r * gy - x * (r * r * r) * jnp.mean(gy * x, axis=-1, keepdims=True)


def _place():
    return lax.axis_index("x"), lax.axis_index("y"), lax.axis_index("c")


def _other_chips(x, y):
    return [(1 - x, y), (x, 1 - y), (1 - x, 1 - y)]


GATHER_SLOTS = 10


def _gather_ops(ins, outs, send_sems, recv_sems, local_sems):
    n = len(ins)
    x, y, c = _place()
    sibling = (x, y, 1 - c)
    xn, yn, dg = _other_chips(x, y)
    split = [ins[i].shape[0] % (2 * BF16_ROWS) == 0 for i in range(n)]

    def blk(chip, core):
        return 4 * chip[0] + 2 * chip[1] + core

    me = blk((x, y), c)

    def part(ref, i, half):
        if half is None:
            return ref
        h = ins[i].shape[0] // 2
        return ref.at[pl.ds(half * h, h)]

    def copy(i, k, block, to, half=None, src=None):
        dst = part(outs[i].at[block], i, half)
        return pltpu.make_async_remote_copy(
            src_ref=dst if src is None else part(src, i, half), dst_ref=dst,
            send_sem=send_sems.at[k, i], recv_sem=recv_sems.at[k, i], device_id=to, device_id_type=MESH)

    def first_copies():
        mine = [pltpu.make_async_copy(ins[i], outs[i].at[me], local_sems.at[i]) for i in range(n)]
        first = []
        for i in range(n):
            first.append(copy(i, 0, me, sibling, src=ins[i]))
            if split[i]:
                first.append(copy(i, 1, me, (*xn, c), 0, ins[i]))
                first.append(copy(i, 3, me, (*yn, c), 1, ins[i]))
                first.append(copy(i, 2, me, (*xn, c), 1, ins[i]))
                first.append(copy(i, 4, me, (*yn, c), 0, ins[i]))
            else:
                first.append(copy(i, 1, me, (*xn, c), None, ins[i]))
                first.append(copy(i, 3, me, (*yn, c), None, ins[i]))
                first.append(copy(i, 5, me, (*dg, c), None, ins[i]))
        return mine, first

    def onward():
        out = []
        for i in range(n):
            if split[i]:
                out.append(copy(i, 5, blk(xn, c), (*yn, c), 0))
                out.append(copy(i, 6, blk(yn, c), (*xn, c), 1))
        return out

    def start():
        mine, first = first_copies()
        for cp in mine + first:
            cp.start()

    def relay():
        sends = onward()
        for i in range(n):
            if split[i]:
                copy(i, 1, blk(xn, c), sibling, 0).wait_recv()
                sends.pop(0).start()
                copy(i, 3, blk(yn, c), sibling, 1).wait_recv()
                sends.pop(0).start()

    def finish():
        mine, first = first_copies()
        passed = []

        def pass_on(i, j, chip):
            fwd = copy(i, 7 + j, blk(chip, c), sibling)
            fwd.start()
            passed.append(fwd)

        for i in range(n):
            if split[i]:
                copy(i, 2, blk(xn, c), sibling, 1).wait_recv()
                pass_on(i, 0, xn)
                copy(i, 4, blk(yn, c), sibling, 0).wait_recv()
                pass_on(i, 1, yn)
                copy(i, 5, blk(dg, c), sibling, 0).wait_recv()
                copy(i, 6, blk(dg, c), sibling, 1).wait_recv()
                pass_on(i, 2, dg)
            else:
                copy(i, 1, blk(xn, c), sibling).wait_recv()
                pass_on(i, 0, xn)
                copy(i, 3, blk(yn, c), sibling).wait_recv()
                pass_on(i, 1, yn)
                copy(i, 5, blk(dg, c), sibling).wait_recv()
                pass_on(i, 2, dg)
        for i in range(n):
            copy(i, 0, blk((x, y), 1 - c), sibling).wait_recv()
            for j, chip in enumerate((xn, yn, dg)):
                copy(i, 7 + j, blk(chip, 1 - c), sibling).wait_recv()
        for cp in first + passed + onward():
            cp.wait_send()
        for cp in mine:
            cp.wait()

    return start, relay, finish


def _gather_sems(n):
    return [pltpu.SemaphoreType.DMA((GATHER_SLOTS, n)), pltpu.SemaphoreType.DMA((GATHER_SLOTS, n)),
            pltpu.SemaphoreType.DMA((n,))]


class _Gather:
    def __init__(self, shards):
        n = len(shards)
        self.ins, self.in_specs = list(shards), [_HBM] * n
        self.out_shape = [_sds((NDEV,) + s.shape, s.dtype) for s in shards]
        self.out_specs = [_HBM] * n
        self.scratch = _gather_sems(n)

    def ops(self, ins, outs, scr):
        return _gather_ops(ins, outs, *scr)


class _Exchange:
    def __init__(self, qs):
        n = len(qs)
        self.ins, self.in_specs = list(qs), [_HBM] * n
        self.out_shape = [_sds(q.shape, q.dtype) for q in qs]
        self.out_specs = [_HBM] * n
        self.scratch = [pltpu.SemaphoreType.DMA((NCHIP_OTHER, n)), pltpu.SemaphoreType.DMA((NCHIP_OTHER, n))]

    def ops(self, ins, outs, scr):
        send_sems, recv_sems = scr
        n = len(ins)
        x, y, c = _place()
        chips = _other_chips(x, y)

        def copies():
            return [pltpu.make_async_remote_copy(
                src_ref=ins[i].at[j], dst_ref=outs[i].at[j], send_sem=send_sems.at[j, i],
                recv_sem=recv_sems.at[j, i], device_id=(*chips[j], c), device_id_type=MESH)
                for i in range(n) for j in range(NCHIP_OTHER)]

        def start():
            for cp in copies():
                cp.start()

        def finish():
            cps = copies()
            for cp in cps:
                cp.wait_recv()
            for cp in cps:
                cp.wait_send()

        return start, lambda: None, finish


class _SumGather:
    def __init__(self, accs, lands):
        n = len(accs)
        self.n = n
        self.ins, self.in_specs = list(accs) + list(lands), [_VMEM] * (2 * n)
        self.out_shape = [_sds((NDEV,) + a.shape, a.dtype) for a in accs]
        self.out_specs = [_HBM] * n
        self.scratch = [pltpu.VMEM(a.shape, a.dtype) for a in accs] + _gather_sems(n)

    def ops(self, ins, outs, scr):
        n = self.n
        accs, lands, mine = ins[:n], ins[n:], scr[:n]
        g_start, relay, finish = _gather_ops(mine, outs, *scr[n:])

        def start():
            for i in range(n):
                mine[i][...] = accs[i][...] + lands[i][0] + lands[i][1] + lands[i][2]
            g_start()

        return start, relay, finish


def _call(main, jobs, *, name, grid, ins, in_specs, out_shape, out_specs, scratch, relay_step=0):
    nsteps = grid[0] if grid else 1
    n_in, n_out, n_scr = len(ins), len(out_shape), len(scratch)

    def body(*refs):
        pos = [0]

        def take(k):
            r = refs[pos[0]:pos[0] + k]
            pos[0] += k
            return r

        m_in = take(n_in)
        j_in = [take(len(j.ins)) for j in jobs]
        m_out = take(n_out)
        j_out = [take(len(j.out_shape)) for j in jobs]
        m_scr = take(n_scr)
        j_scr = [take(len(j.scratch)) for j in jobs]
        ops = [j.ops(a, b, s) for j, a, b, s in zip(jobs, j_in, j_out, j_scr)]
        i = pl.program_id(0) if grid else 0
        if not grid:
            for o in ops:
                o[0]()
            main(i, m_in, m_out, m_scr)
            for o in ops:
                o[1]()
            for o in ops:
                o[2]()
            return

        if ops:
            @pl.when(i == 0)
            def _():
                for o in ops:
                    o[0]()

        main(i, m_in, m_out, m_scr)

        if ops:
            @pl.when(i == min(relay_step, nsteps - 1))
            def _():
                for o in ops:
                    o[1]()

            @pl.when(i == nsteps - 1)
            def _():
                for o in ops:
                    o[2]()

    extra = dict(dimension_semantics=("arbitrary",)) if grid else {}
    res = pl.pallas_call(
        body, name=name, grid=grid,
        in_specs=list(in_specs) + [s for j in jobs for s in j.in_specs],
        out_specs=list(out_specs) + [s for j in jobs for s in j.out_specs],
        out_shape=list(out_shape) + [s for j in jobs for s in j.out_shape],
        scratch_shapes=list(scratch) + [s for j in jobs for s in j.scratch],
        compiler_params=_params(**extra),
    )(*ins, *[a for j in jobs for a in j.ins])
    main_out, rest, job_out = res[:n_out], res[n_out:], []
    for j in jobs:
        k = len(j.out_shape)
        job_out.append(rest[:k])
        rest = rest[k:]
    return main_out, job_out


def _comm_only(jobs, name):
    _, job_out = _call(lambda i, a, b, s: None, jobs, name=name, grid=(), ins=[], in_specs=[], out_shape=[],
                       out_specs=[], scratch=[])
    return job_out


def _reduce_in_chip(ps, name):
    n = len(ps)

    def body(*refs):
        p_refs, q_refs, acc_refs = refs[:n], refs[n:2 * n], refs[2 * n:3 * n]
        rest = refs[3 * n:]
        mines, lands = rest[:n], rest[n:2 * n]
        send_sems, recv_sems, local_sems = rest[2 * n:]
        x, y, c = _place()
        sibling = (x, y, 1 - c)
        pairs = []
        for i in range(n):
            for px in range(2):
                for py in range(2):
                    pi = 2 * px + py
                    loc = pltpu.make_async_copy(p_refs[i].at[4 * px + 2 * py + c], mines[i].at[pi],
                                                local_sems.at[pi, i])
                    cp = pltpu.make_async_remote_copy(
                        src_ref=p_refs[i].at[4 * px + 2 * py + (1 - c)], dst_ref=lands[i].at[pi],
                        send_sem=send_sems.at[pi, i], recv_sem=recv_sems.at[pi, i],
                        device_id=sibling, device_id_type=MESH)
                    loc.start()
                    cp.start()
                    pairs.append((loc, cp))
        for loc, cp in pairs:
            loc.wait()
            cp.wait_recv()
        for i in range(n):
            for j, (qx, qy) in enumerate(_other_chips(x, y)):
                qi = 2 * qx + qy
                q_refs[i][j] = (mines[i][qi].astype(F32) + lands[i][qi].astype(F32)).astype(q_refs[i].dtype)
            mi = 2 * x + y
            acc_refs[i][...] = mines[i][mi].astype(F32) + lands[i][mi].astype(F32)
        for _, cp in pairs:
            cp.wait_send()

    blk = [p.shape[1:] for p in ps]
    res = pl.pallas_call(
        body, name=name, in_specs=[_HBM] * n, out_specs=[_VMEM] * (2 * n),
        out_shape=[_sds((NCHIP_OTHER,) + b, p.dtype) for b, p in zip(blk, ps)] + [_sds(b, F32) for b in blk],
        scratch_shapes=[pltpu.VMEM((4,) + b, p.dtype) for b, p in zip(blk, ps)]
        + [pltpu.VMEM((4,) + b, p.dtype) for b, p in zip(blk, ps)]
        + [pltpu.SemaphoreType.DMA((4, n)), pltpu.SemaphoreType.DMA((4, n)), pltpu.SemaphoreType.DMA((4, n))],
        compiler_params=_params(),
    )(*ps)
    return res[:n], res[n:]


def _allreduce_direct(v, name):
    def body(v_ref, o_ref, buf, send_sems, recv_sems):
        x, y, c = _place()
        me = 4 * x + 2 * y + c
        buf[me] = v_ref[...]
        cps = []
        for k in range(1, NDEV):
            fx, fy, fc = (k >> 2) & 1, (k >> 1) & 1, k & 1
            peer = ((1 - x) if fx else x, (1 - y) if fy else y, (1 - c) if fc else c)
            cps.append((peer, pltpu.make_async_remote_copy(
                src_ref=buf.at[me], dst_ref=buf.at[me], send_sem=send_sems.at[k - 1], recv_sem=recv_sems.at[k - 1],
                device_id=peer, device_id_type=MESH)))
        for _, cp in cps:
            cp.start()
        for k, (peer, _) in enumerate(cps):
            theirs = 4 * peer[0] + 2 * peer[1] + peer[2]
            pltpu.make_async_remote_copy(
                src_ref=buf.at[theirs], dst_ref=buf.at[theirs], send_sem=send_sems.at[k], recv_sem=recv_sems.at[k],
                device_id=peer, device_id_type=MESH).wait_recv()
        acc = buf[0]
        for j in range(1, NDEV):
            acc = acc + buf[j]
        o_ref[...] = acc
        for _, cp in cps:
            cp.wait_send()

    return pl.pallas_call(
        body, name=name, in_specs=[_VMEM], out_specs=_VMEM, out_shape=_sds(v.shape, v.dtype),
        scratch_shapes=[pltpu.VMEM((NDEV,) + v.shape, v.dtype), pltpu.SemaphoreType.DMA((NDEV - 1,)),
                        pltpu.SemaphoreType.DMA((NDEV - 1,))],
        compiler_params=_params(),
    )(v)


def _fwd_a(x, nw, win8, lnw, lnb, ws, bst, jobs, *, tm, relay_step):
    s_len = x.shape[0]
    nt = s_len // tm
    nch = tm // CH

    def main(i, ins, outs, scr):
        x_ref, nw_ref, win_ref, lnw_ref, lnb_ref, ws_ref, bst_ref = ins
        z_ref, h_ref, y_ref = outs
        wc_scr, gv_scr = scr

        @pl.when(i == 0)
        def _():
            m = _causal_mask()
            for g in range(G):
                wc_scr[g] = jnp.where(m, ws_ref[g], 0.0).astype(BF16)

        x = x_ref[...]
        h = (x * _rms(x) * nw_ref[...]).astype(BF16)
        h_ref[...] = h
        for k in range(NDEV):
            z_ref[:, k * CA:(k + 1) * CA] = _dot(h, win_ref[k])

        ssum = jnp.zeros((tm, 1), F32)
        for g in range(G):
            gv = _gelu_t(z_ref[:, AW + g * GD:AW + (g + 1) * GD])[0]
            gv_scr[:, g * GD:(g + 1) * GD] = gv
            ssum = ssum + jnp.sum(gv, axis=-1, keepdims=True)
        mu = ssum * (1.0 / AW)
        vsum = jnp.zeros((tm, 1), F32)
        for g in range(G):
            dlt = gv_scr[:, g * GD:(g + 1) * GD] - mu
            vsum = vsum + jnp.sum(dlt * dlt, axis=-1, keepdims=True)
        rstd = lax.rsqrt(vsum * (1.0 / AW) + LN_EPS)

        for g in range(G):
            cs = slice(g * GD, (g + 1) * GD)
            v = (gv_scr[:, cs] - mu) * rstd * lnw_ref[:, cs] + lnb_ref[:, cs]
            vb = v.astype(BF16)
            u = _gelu_t(z_ref[:, cs])[0]
            zg = z_ref[:, 2 * AW + g * GD:2 * AW + (g + 1) * GD]
            sg = zg * _sigmoid(zg)
            for n in range(nch):
                rs = slice(n * CH, (n + 1) * CH)
                s = _dot(wc_scr[g], vb[rs, :]) + bst_ref[:, g:g + 1]
                y_ref[rs, cs] = (u[rs, :] * s * sg[rs, :]).astype(BF16)

    tile = lambda w: pl.BlockSpec((tm, w), lambda i: (i, 0))
    return _call(
        main, jobs, name="fwd_a", grid=(nt,), relay_step=relay_step,
        ins=[x, nw, win8, lnw, lnb, ws, bst], in_specs=[tile(D), _VMEM, _VMEM, _VMEM, _VMEM, _VMEM, _VMEM],
        out_shape=[_sds((s_len, 3 * AW), F32), _sds((s_len, D), BF16), _sds((s_len, AW), BF16)],
        out_specs=[tile(3 * AW), tile(D), tile(AW)],
        scratch=[pltpu.VMEM((G, CH, CH), BF16), pltpu.VMEM((tm, AW), F32)])


def _bwd_a(dx1, z, lnw, lnb, ws, bst, wout, jobs, *, tm):
    s_len = dx1.shape[0]
    nt = s_len // tm
    nch = tm // CH

    def main(i, ins, outs, scr):
        dx1_ref, z_ref, lnw_ref, lnb_ref, ws_ref, bst_ref, wout_ref = ins
        dz_ref, glnw_ref, glnb_ref, gws_ref, gbst_ref = outs
        wc_scr, wct_scr, vh_scr, dgv_scr, dy_scr, dv_scr, gbs_acc, gwc_acc = scr

        @pl.when(i == 0)
        def _():
            m = _causal_mask()
            for g in range(G):
                wm = jnp.where(m, ws_ref[g], 0.0)
                wc_scr[g] = wm.astype(BF16)
                wct_scr[g] = wm.T.astype(BF16)
            glnw_ref[...] = jnp.zeros_like(glnw_ref)
            glnb_ref[...] = jnp.zeros_like(glnb_ref)
            gbs_acc[...] = jnp.zeros_like(gbs_acc)
            gwc_acc[...] = jnp.zeros_like(gwc_acc)

        dy_scr[...] = _dot_nt(dx1_ref[...], wout_ref[...])

        ssum = jnp.zeros((tm, 1), F32)
        for g in range(G):
            cs = slice(g * GD, (g + 1) * GD)
            zv = z_ref[:, AW + g * GD:AW + (g + 1) * GD]
            gv, t = _gelu_t(zv)
            vh_scr[:, cs] = gv
            dgv_scr[:, cs] = _dgelu(zv, t)
            ssum = ssum + jnp.sum(gv, axis=-1, keepdims=True)
        mu = ssum * (1.0 / AW)
        vsum = jnp.zeros((tm, 1), F32)
        for g in range(G):
            dlt = vh_scr[:, g * GD:(g + 1) * GD] - mu
            vsum = vsum + jnp.sum(dlt * dlt, axis=-1, keepdims=True)
        rstd = lax.rsqrt(vsum * (1.0 / AW) + LN_EPS)

        m1 = jnp.zeros((tm, 1), F32)
        m2 = jnp.zeros((tm, 1), F32)
        for g in range(G):
            cs = slice(g * GD, (g + 1) * GD)
            gs = slice(2 * AW + g * GD, 2 * AW + (g + 1) * GD)
            vhat = (vh_scr[:, cs] - mu) * rstd
            vh_scr[:, cs] = vhat
            vb = (vhat * lnw_ref[:, cs] + lnb_ref[:, cs]).astype(BF16)
            zu = z_ref[:, cs]
            u, tu = _gelu_t(zu)
            zg = z_ref[:, gs]
            sig = _sigmoid(zg)
            sg = zg * sig
            dy = dy_scr[:, cs]
            dsf = dy * u * sg
            dsb = dsf.astype(BF16)
            dvs = []
            for n in range(nch):
                rs = slice(n * CH, (n + 1) * CH)
                s = _dot(wc_scr[g], vb[rs, :]) + bst_ref[:, g:g + 1]
                dys = dy[rs, :] * s
                dz_ref[rs, cs] = (dys * sg[rs, :] * _dgelu(zu[rs, :], tu[rs, :])).astype(BF16)
                dz_ref[rs, gs] = (dys * u[rs, :] * (sig[rs, :] * (1.0 + zg[rs, :] * (1.0 - sig[rs, :])))).astype(BF16)
                gbs_acc[g] += dsf[rs, :]
                gwc_acc[g] += _dot_nt(dsb[rs, :], vb[rs, :])
                dvs.append(_dot(wct_scr[g], dsb[rs, :]))
            dv = jnp.concatenate(dvs, axis=0) if nch > 1 else dvs[0]
            glnw_ref[:, cs] += _rowsum(dv * vhat)
            glnb_ref[:, cs] += _rowsum(dv)
            dvh = dv * lnw_ref[:, cs]
            dv_scr[:, cs] = dvh
            m1 = m1 + jnp.sum(dvh, axis=-1, keepdims=True)
            m2 = m2 + jnp.sum(dvh * vhat, axis=-1, keepdims=True)
        m1 = m1 * (1.0 / AW)
        m2 = m2 * (1.0 / AW)
        for g in range(G):
            cs = slice(g * GD, (g + 1) * GD)
            dgv = rstd * (dv_scr[:, cs] - m1 - vh_scr[:, cs] * m2)
            dz_ref[:, AW + g * GD:AW + (g + 1) * GD] = (dgv * dgv_scr[:, cs]).astype(BF16)

        @pl.when(i == nt - 1)
        def _():
            m = _causal_mask()
            for g in range(G):
                gws_ref[g] = jnp.where(m, gwc_acc[g], 0.0)
                gbst_ref[:, g:g + 1] = jnp.sum(gbs_acc[g], axis=-1, keepdims=True)

    tile = lambda w: pl.BlockSpec((tm, w), lambda i: (i, 0))
    whole = lambda *s: pl.BlockSpec(s, lambda i: (0,) * len(s))
    big = lambda dt: pltpu.VMEM((tm, AW), dt)
    return _call(
        main, jobs, name="bwd_a", grid=(nt,),
        ins=[dx1, z, lnw, lnb, ws, bst, wout], in_specs=[tile(D), tile(3 * AW), _VMEM, _VMEM, _VMEM, _VMEM, _VMEM],
        out_shape=[_sds((s_len, 3 * AW), BF16), _sds((1, AW), F32), _sds((1, AW), F32), _sds((G, CH, CH), F32),
                   _sds((CH, G), F32)],
        out_specs=[tile(3 * AW), whole(1, AW), whole(1, AW), whole(G, CH, CH), whole(CH, G)],
        scratch=[pltpu.VMEM((G, CH, CH), BF16), pltpu.VMEM((G, CH, CH), BF16), big(F32), big(F32), big(F32), big(F32),
                 pltpu.VMEM((G, CH, GD), F32), pltpu.VMEM((G, CH, CH), F32)])


def _bwd_a_in(dz, dx1, x, nw, win8, jobs, *, tm, relay_step):
    s_len = x.shape[0]
    nt = s_len // tm

    def main(i, ins, outs, scr):
        dz_ref, dx1_ref, x_ref, nw_ref, win_ref = ins
        gx_ref, gnw_ref = outs

        @pl.when(i == 0)
        def _():
            gnw_ref[...] = jnp.zeros_like(gnw_ref)

        dh = jnp.zeros((tm, D), F32)
        for k in range(NDEV):
            dh = dh + _dot_nt(dz_ref[:, k * CA:(k + 1) * CA], win_ref[k])
        x = x_ref[...]
        r = _rms(x)
        gx_ref[...] = dx1_ref[...] + _rms_bwd(dh, x, r, nw_ref[...])
        gnw_ref[...] += _rowsum(dh * x * r)

    tile = lambda w: pl.BlockSpec((tm, w), lambda i: (i, 0))
    return _call(
        main, jobs, name="bwd_a_in", grid=(nt,), relay_step=relay_step,
        ins=[dz, dx1, x, nw, win8], in_specs=[tile(3 * AW), tile(D), tile(D), _VMEM, _VMEM],
        out_shape=[_sds((s_len, D), F32), _sds((1, D), F32)],
        out_specs=[tile(D), pl.BlockSpec((1, D), lambda i: (0, 0))], scratch=[])


def _conv(p8_ref, cs, xb, xm1, xm2, xm3):
    xc = p8_ref[4:5, cs] + p8_ref[3:4, cs] * xb
    xc = xc + p8_ref[0:1, cs] * xm3
    xc = xc + p8_ref[1:2, cs] * xm2
    return xc + p8_ref[2:3, cs] * xm1


def _gates(p8_ref, gcat_ref, hh, xc):
    cs = slice(hh * HD, (hh + 1) * HD)
    pre = _dot(xc.astype(BF16), gcat_ref[hh])
    r = _sigmoid(pre[:, :HD] + p8_ref[5:6, cs])
    ig = _sigmoid(pre[:, HD:] + p8_ref[6:7, cs])
    sp = _softplus_neg(p8_ref[7:8, cs])
    la = (-RG_C) * r * sp
    a = jnp.exp(la)
    m2 = jnp.tanh(-la) * (1.0 + a * a)
    rm = lax.rsqrt(m2)
    mult = jnp.where(m2 > 0.0, m2 * rm, 0.0)
    return r, ig, sp, a, mult, rm


def _scan_rows(a_ref, b_ref, out_ref, carry, tm, reverse):
    row = lax.broadcasted_iota(jnp.int32, (SUBLANES, BW), 0)
    ngrp = tm // SUBLANES

    def step(j, cr):
        jj = (ngrp - 1 - j) if reverse else j
        off = pl.multiple_of(jj * SUBLANES, SUBLANES)
        a = a_ref[pl.ds(off, SUBLANES), :]
        b = b_ref[pl.ds(off, SUBLANES), :]
        for sh in (1, 2, 4):
            if reverse:
                a_s = pltpu.roll(a, SUBLANES - sh, 0)
                b_s = pltpu.roll(b, SUBLANES - sh, 0)
                m = row < SUBLANES - sh
            else:
                a_s = pltpu.roll(a, sh, 0)
                b_s = pltpu.roll(b, sh, 0)
                m = row >= sh
            b = jnp.where(m, a * b_s + b, b)
            a = jnp.where(m, a * a_s, a)
        o = b + a * cr
        out_ref[pl.ds(off, SUBLANES), :] = o
        return o[0:1, :] if reverse else o[SUBLANES - 1:SUBLANES, :]

    return lax.fori_loop(0, ngrp, step, carry)


def _fwd_b(x, ya, wout_a, nw, win8, p8, gcat, jobs, *, tm, relay_step):
    s_len = x.shape[0]
    nt = s_len // tm

    def main(i, ins, outs, scr):
        x_ref, ya_ref, wouta_ref, nw_ref, win_ref, p8_ref, gcat_ref = ins
        x1_ref, zb_ref, hs_ref, h1_ref, yb_ref = outs
        xbe_scr, a_scr, b_scr, carry_scr = scr

        @pl.when(i == 0)
        def _():
            xbe_scr[0:SUBLANES, :] = jnp.zeros((SUBLANES, BW), F32)
            carry_scr[...] = jnp.zeros_like(carry_scr)

        x1 = x_ref[...] + _dot(ya_ref[...], wouta_ref[...])
        x1_ref[...] = x1
        h = (x1 * _rms(x1) * nw_ref[...]).astype(BF16)
        h1_ref[...] = h
        for k in range(NDEV):
            zb_ref[:, k * CB:(k + 1) * CB] = _dot(h, win_ref[k])
        xbe_scr[SUBLANES:SUBLANES + tm, :] = zb_ref[:, :BW]
        for hh in range(BH):
            cs = slice(hh * HD, (hh + 1) * HD)
            xc = _conv(p8_ref, cs, xbe_scr[SUBLANES:SUBLANES + tm, cs], xbe_scr[7:7 + tm, cs],
                       xbe_scr[6:6 + tm, cs], xbe_scr[5:5 + tm, cs])
            _, ig, _, a, mult, _ = _gates(p8_ref, gcat_ref, hh, xc)
            a_scr[:, cs] = a
            b_scr[:, cs] = mult * (ig * xc)
        xbe_scr[0:SUBLANES, :] = xbe_scr[tm:tm + SUBLANES, :]
        carry_scr[...] = _scan_rows(a_scr, b_scr, hs_ref, carry_scr[...], tm, False)
        for hh in range(BH):
            cs = slice(hh * HD, (hh + 1) * HD)
            gt = zb_ref[:, BW + hh * HD:BW + (hh + 1) * HD]
            yb_ref[:, cs] = (hs_ref[:, cs] * (gt * _sigmoid(gt))).astype(BF16)

    tile = lambda w: pl.BlockSpec((tm, w), lambda i: (i, 0))
    return _call(
        main, jobs, name="fwd_b", grid=(nt,), relay_step=relay_step,
        ins=[x, ya, wout_a, nw, win8, p8, gcat], in_specs=[tile(D), tile(AW), _VMEM, _VMEM, _VMEM, _VMEM, _VMEM],
        out_shape=[_sds((s_len, D), F32), _sds((s_len, 2 * BW), F32), _sds((s_len, BW), F32), _sds((s_len, D), BF16),
                   _sds((s_len, BW), BF16)],
        out_specs=[tile(D), tile(2 * BW), tile(BW), tile(D), tile(BW)],
        scratch=[pltpu.VMEM((tm + SUBLANES, BW), F32), pltpu.VMEM((tm, BW), F32), pltpu.VMEM((tm, BW), F32),
                 pltpu.VMEM((1, BW), F32)])


def _head(x1, yb, wout, nfw, tgt, *, tm):
    s_len = x1.shape[0]

    def main(i, ins, outs, scr):
        x1_ref, yb_ref, wout_ref, nfw_ref, t_ref = ins
        dx2_ref, dx2b_ref, loss_ref, gnfw_ref = outs

        @pl.when(i == 0)
        def _():
            loss_ref[...] = jnp.zeros_like(loss_ref)
            gnfw_ref[...] = jnp.zeros_like(gnfw_ref)

        x2 = x1_ref[...] + _dot(yb_ref[...], wout_ref[...])
        rf = _rms(x2)
        xn = x2 * rf
        e = xn * nfw_ref[...] - t_ref[...]
        loss_ref[...] += (0.5 / D) * jnp.sum(jnp.sum(e * e, axis=-1, keepdims=True), axis=0, keepdims=True)
        dyf = e * (1.0 / D)
        gnfw_ref[...] += _rowsum(dyf * xn)
        dx2 = _rms_bwd(dyf, x2, rf, nfw_ref[...])
        dx2_ref[...] = dx2
        dx2b_ref[...] = dx2.astype(BF16)

    tile = lambda w: pl.BlockSpec((tm, w), lambda i: (i, 0))
    whole = lambda *s: pl.BlockSpec(s, lambda i: (0,) * len(s))
    (dx2, dx2b, loss, gnfw), _ = _call(
        main, [], name="head", grid=(s_len // tm,),
        ins=[x1, yb, wout, nfw, tgt], in_specs=[tile(D), tile(BW), _VMEM, _VMEM, tile(D)],
        out_shape=[_sds((s_len, D), F32), _sds((s_len, D), BF16), _sds((1, 1), F32), _sds((1, D), F32)],
        out_specs=[tile(D), tile(D), whole(1, 1), whole(1, D)], scratch=[])
    return dx2, dx2b, loss, gnfw


def _bwd_b(dx2, zb, hs, x1, nw, win8, p8, gcat, wout, *, tm):
    s_len = x1.shape[0]
    nt = s_len // tm
    per = tm // SUBLANES

    def main(i, ins, outs, scr):
        dx2_ref, zb_ref, zbp_ref, hs_ref, hsp_ref, x1_ref, nw_ref, win_ref, p8_ref, gcat_ref, wout_ref = ins
        dx1_ref, dx1b_ref, dzb_ref, gp8_ref, gga_ref, ggx_ref, gnw_ref = outs
        (xbe_scr, hse_scr, ae_scr, an_scr, r_scr, i_scr, m_scr, xc_scr, cc_scr, dhd_scr, dh_scr, dy_scr, dxce_scr,
         carry_scr, afirst_scr) = scr
        ti = nt - 1 - i

        @pl.when(i == 0)
        def _():
            gp8_ref[...] = jnp.zeros_like(gp8_ref)
            gga_ref[...] = jnp.zeros_like(gga_ref)
            ggx_ref[...] = jnp.zeros_like(ggx_ref)
            gnw_ref[...] = jnp.zeros_like(gnw_ref)
            dxce_scr[tm:tm + SUBLANES, :] = jnp.zeros((SUBLANES, BW), F32)
            carry_scr[...] = jnp.zeros_like(carry_scr)
            afirst_scr[...] = jnp.zeros_like(afirst_scr)

        has_prev = (ti > 0).astype(F32)
        xbe_scr[0:SUBLANES, :] = zbp_ref[:, :BW] * has_prev
        xbe_scr[SUBLANES:SUBLANES + tm, :] = zb_ref[:, :BW]
        hse_scr[0:SUBLANES, :] = hsp_ref[...] * has_prev
        hse_scr[SUBLANES:SUBLANES + tm, :] = hs_ref[...]

        dx2 = dx2_ref[...]
        dy_scr[...] = _dot_nt(dx2.astype(BF16), wout_ref[...])

        for hh in range(BH):
            cs = slice(hh * HD, (hh + 1) * HD)
            xc = _conv(p8_ref, cs, xbe_scr[SUBLANES:SUBLANES + tm, cs], xbe_scr[7:7 + tm, cs],
                       xbe_scr[6:6 + tm, cs], xbe_scr[5:5 + tm, cs])
            r, ig, _, a, mult, rm = _gates(p8_ref, gcat_ref, hh, xc)
            cc_scr[:, cs] = a * hse_scr[7:7 + tm, cs] - (ig * xc) * (a * a * rm)
            xc_scr[:, cs] = xc
            r_scr[:, cs] = r
            i_scr[:, cs] = ig
            m_scr[:, cs] = mult
            ae_scr[0:tm, cs] = a
            gt = zb_ref[:, BW + hh * HD:BW + (hh + 1) * HD]
            sig = _sigmoid(gt)
            dy = dy_scr[:, cs]
            dhd_scr[:, cs] = dy * (gt * sig)
            dzb_ref[:, BW + hh * HD:BW + (hh + 1) * HD] = (
                dy * hs_ref[:, cs] * (sig * (1.0 + gt * (1.0 - sig)))).astype(BF16)
        ae_scr[tm:tm + SUBLANES, :] = jnp.broadcast_to(afirst_scr[...], (SUBLANES, BW))
        an_scr[...] = ae_scr[1:1 + tm, :]
        afirst_scr[...] = ae_scr[0:1, :]
        carry_scr[...] = _scan_rows(an_scr, dhd_scr, dh_scr, carry_scr[...], tm, True)

        for hh in range(BH):
            cs = slice(hh * HD, (hh + 1) * HD)
            dh = dh_scr[:, cs]
            mult = m_scr[:, cs]
            ig = i_scr[:, cs]
            r = r_scr[:, cs]
            xc = xc_scr[:, cs]
            lam = p8_ref[7:8, cs]
            sp = _softplus_neg(lam)
            dla = dh * cc_scr[:, cs]
            gp8_ref[7:8, cs] += _rowsum(dla * ((-RG_C) * r)) * (-_sigmoid(-lam))
            dpr = dla * ((-RG_C) * sp) * (r * (1.0 - r))
            dpi = dh * mult * xc * (ig * (1.0 - ig))
            gp8_ref[5:6, cs] += _rowsum(dpr)
            gp8_ref[6:7, cs] += _rowsum(dpi)
            dcat = jnp.concatenate([dpr, dpi], axis=1).astype(BF16)
            dxc = dh * mult * ig + _dot_nt(dcat, gcat_ref[hh])
            gg = _dot(xc.T.astype(BF16), dcat)
            gga_ref[hh] += gg[:, :HD]
            ggx_ref[hh] += gg[:, HD:]
            dxce_scr[0:tm, cs] = dxc
            gp8_ref[4:5, cs] += _rowsum(dxc)
            gp8_ref[3:4, cs] += _rowsum(dxc * xbe_scr[SUBLANES:SUBLANES + tm, cs])
            gp8_ref[2:3, cs] += _rowsum(dxc * xbe_scr[7:7 + tm, cs])
            gp8_ref[1:2, cs] += _rowsum(dxc * xbe_scr[6:6 + tm, cs])
            gp8_ref[0:1, cs] += _rowsum(dxc * xbe_scr[5:5 + tm, cs])
        for hh in range(BH):
            cs = slice(hh * HD, (hh + 1) * HD)
            dxb = p8_ref[3:4, cs] * dxce_scr[0:tm, cs]
            dxb = dxb + p8_ref[2:3, cs] * dxce_scr[1:1 + tm, cs]
            dxb = dxb + p8_ref[1:2, cs] * dxce_scr[2:2 + tm, cs]
            dxb = dxb + p8_ref[0:1, cs] * dxce_scr[3:3 + tm, cs]
            dzb_ref[:, cs] = dxb.astype(BF16)
        dxce_scr[tm:tm + SUBLANES, :] = dxce_scr[0:SUBLANES, :]

        dh1 = jnp.zeros((tm, D), F32)
        for k in range(NDEV):
            dh1 = dh1 + _dot_nt(dzb_ref[:, k * CB:(k + 1) * CB], win_ref[k])
        x1 = x1_ref[...]
        r1 = _rms(x1)
        dx1 = dx2 + _rms_bwd(dh1, x1, r1, nw_ref[...])
        dx1_ref[...] = dx1
        dx1b_ref[...] = dx1.astype(BF16)
        gnw_ref[...] += _rowsum(dh1 * x1 * r1)

    tile = lambda w: pl.BlockSpec((tm, w), lambda i: (nt - 1 - i, 0))
    prev = lambda w: pl.BlockSpec((SUBLANES, w), lambda i: (jnp.maximum((nt - 1 - i) * per - 1, 0), 0))
    whole = lambda *s: pl.BlockSpec(s, lambda i: (0,) * len(s))
    full = lambda: pltpu.VMEM((tm, BW), F32)
    ext = lambda: pltpu.VMEM((tm + SUBLANES, BW), F32)
    out, _ = _call(
        main, [], name="bwd_b", grid=(nt,),
        ins=[dx2, zb, zb, hs, hs, x1, nw, win8, p8, gcat, wout],
        in_specs=[tile(D), tile(2 * BW), prev(2 * BW), tile(BW), prev(BW), tile(D), _VMEM, _VMEM, _VMEM, _VMEM, _VMEM],
        out_shape=[_sds((s_len, D), F32), _sds((s_len, D), BF16), _sds((s_len, 2 * BW), BF16), _sds((SUBLANES, BW), F32),
                   _sds((BH, HD, HD), F32), _sds((BH, HD, HD), F32), _sds((1, D), F32)],
        out_specs=[tile(D), tile(D), tile(2 * BW), whole(SUBLANES, BW), whole(BH, HD, HD), whole(BH, HD, HD),
                   whole(1, D)],
        scratch=[ext(), ext(), ext(), full(), full(), full(), full(), full(), full(), full(), full(), full(), ext(),
                 pltpu.VMEM((1, BW), F32), pltpu.VMEM((1, BW), F32)])
    return out


def _transpose_into(dst_ref, src_ref, rows):
    s_len = src_ref.shape[0]
    for r0 in range(0, s_len, rows):
        dst_ref[:, r0:r0 + rows] = src_ref[r0:r0 + rows, :].astype(F32).T.astype(BF16)


def _wgrad_cols(a, b, jobs, *, nblk, name, relay_step=0):
    s_len, m = a.shape
    bn = b.shape[1] // nblk

    def main(i, ins, outs, scr):
        a_ref, b_ref = ins
        (o_ref,), (at_scr,) = outs, scr

        @pl.when(i == 0)
        def _():
            _transpose_into(at_scr, a_ref, 256)

        o_ref[0] = _dot(at_scr[...], b_ref[...]).astype(BF16)

    (out,), job_out = _call(
        main, jobs, name=name, grid=(nblk,), relay_step=relay_step,
        ins=[a, b], in_specs=[_VMEM, pl.BlockSpec((s_len, bn), lambda j: (0, j))],
        out_shape=[_sds((nblk, m, bn), BF16)], out_specs=[pl.BlockSpec((1, m, bn), lambda j: (j, 0, 0))],
        scratch=[pltpu.VMEM((m, s_len), BF16)])
    return out, job_out


def _wgrad_rows(a, b, jobs, *, nblk, per, name, relay_step=0):
    s_len, m = a.shape
    n = b.shape[1]
    rb = m // nblk
    bm = per * rb

    def main(i, ins, outs, scr):
        a_ref, b_ref = ins
        (o_ref,), (at_scr,) = outs, scr
        _transpose_into(at_scr, a_ref, 256)
        res = _dot(at_scr[...], b_ref[...]).astype(BF16)
        for q in range(per):
            o_ref[q] = res[q * rb:(q + 1) * rb, :]

    (out,), job_out = _call(
        main, jobs, name=name, grid=(nblk // per,), relay_step=relay_step,
        ins=[a, b], in_specs=[pl.BlockSpec((s_len, bm), lambda j: (0, j)), _VMEM],
        out_shape=[_sds((nblk, rb, n), BF16)], out_specs=[pl.BlockSpec((per, rb, n), lambda j: (j, 0, 0))],
        scratch=[pltpu.VMEM((bm, s_len), BF16)])
    return out, job_out


def _adam_math(w, g, m, v):
    m = B1 * m + (1.0 - B1) * g
    v = B2 * v + (1.0 - B2) * (g * g)
    m_hat = m / (1.0 - B1 ** STEP)
    v_hat = v / (1.0 - B2 ** STEP)
    delta = (-LR) * (m_hat / (jnp.sqrt(v_hat) + ADAM_EPS) + WD * w)
    return delta, m, v


def _adam_big(w, acc, land, m, v, name):
    r, cd = w.shape
    rb = 256 if r % 256 == 0 else r

    def body(w_ref, acc_ref, land_ref, m_ref, v_ref, g_ref, d_ref, mo_ref, vo_ref):
        g = acc_ref[...]
        for j in range(NCHIP_OTHER):
            g = g + land_ref[j].astype(F32)
        g_ref[...] = g
        d_ref[...], mo_ref[...], vo_ref[...] = _adam_math(w_ref[...], g, m_ref[...], v_ref[...])

    blk = pl.BlockSpec((rb, cd), lambda i: (i, 0))
    blk3 = pl.BlockSpec((NCHIP_OTHER, rb, cd), lambda i: (0, i, 0))
    return pl.pallas_call(
        body, name=name, grid=(r // rb,), in_specs=[blk, blk, blk3, blk, blk], out_specs=[blk] * 4,
        out_shape=[_sds((r, cd), F32)] * 4,
        compiler_params=_params(dimension_semantics=("arbitrary",)),
    )(w, acc, land, m, v)


def _adam_small(groups):
    n = len(groups)

    def body(*refs):
        ins, outs = refs[:4 * n], refs[4 * n:]
        for k in range(n):
            w_ref, g_ref, m_ref, v_ref = ins[4 * k:4 * k + 4]
            d, mo, vo = _adam_math(w_ref[...], g_ref[...], m_ref[...], v_ref[...])
            outs[3 * k][...] = d
            outs[3 * k + 1][...] = mo
            outs[3 * k + 2][...] = vo

    flat = [a for grp in groups for a in grp]
    shapes = [_sds(grp[0].shape, F32) for grp in groups for _ in range(3)]
    res = pl.pallas_call(
        body, name="adam_small", in_specs=[_VMEM] * (4 * n), out_specs=[_VMEM] * (3 * n), out_shape=shapes,
        compiler_params=_params(),
    )(*flat)
    return [tuple(res[3 * k:3 * k + 3]) for k in range(n)]


TM_FWD_A = 256
RELAY_STEP_FWD_A = 4
RELAY_STEP_FWD_B = 2
TM_BWD_A = 256
TM_BWD_A_IN = 256
TM_FWD_B = 256
TM_HEAD = 512
TM_BWD_B = 256


def _pack(parts, rows):
    flat = jnp.concatenate([p.reshape(-1) for p in parts])
    return jnp.pad(flat, (0, NDEV * rows * LANES - flat.shape[0])).reshape(NDEV, rows, LANES)


def _unpack(packed, shapes):
    flat, out, off = packed.reshape(-1), [], 0
    for s in shapes:
        size = 1
        for d in s:
            size *= d
        out.append(flat[off:off + size].reshape(s))
        off += size
    return out


def kernel(x, norm_w, a_w_in, a_ln_w, a_ln_b, a_w_s, a_b_s, a_w_out, b_w_in, b_conv_w, b_conv_b, b_gate_a_w, b_gate_a_b, b_gate_x_w, b_gate_x_b, b_lambda, b_w_out, norm_f_w, loss_target, m_norm_w, m_a_w_in, m_a_ln_w, m_a_ln_b, m_a_w_s, m_a_b_s, m_a_w_out, m_b_w_in, m_b_conv_w, m_b_conv_b, m_b_gate_a_w, m_b_gate_a_b, m_b_gate_x_w, m_b_gate_x_b, m_b_lambda, m_b_w_out, m_norm_f_w, v_norm_w, v_a_w_in, v_a_ln_w, v_a_ln_b, v_a_w_s, v_a_b_s, v_a_w_out, v_b_w_in, v_b_conv_w, v_b_conv_b, v_b_gate_a_w, v_b_gate_a_b, v_b_gate_x_w, v_b_gate_x_b, v_b_lambda, v_b_w_out, v_norm_f_w):
    me = 4 * lax.axis_index("x") + 2 * lax.axis_index("y") + lax.axis_index("c")
    xs, tgt = x[0], loss_target[0]
    nw0, nw1, nfw = norm_w[0:1], norm_w[1:2], norm_f_w.reshape(1, D)
    w_s, bst = a_w_s[0], a_b_s[0].T
    gcat = jnp.concatenate([b_gate_a_w[0], b_gate_x_w[0]], axis=-1).astype(BF16)

    p8_shard = jnp.concatenate([b_conv_w[0], b_conv_b, b_gate_a_b, b_gate_x_b, b_lambda], axis=0)
    ((win_a8, p8_all),) = _comm_only([_Gather([a_w_in[0].astype(BF16), p8_shard])], "gather_first")
    p8 = jnp.transpose(p8_all, (1, 0, 2)).reshape(SUBLANES, BW)

    (z, h0, ya), ((wout_a8, win_b8),) = _fwd_a(
        xs, nw0, win_a8, a_ln_w, a_ln_b, w_s, bst, [_Gather([a_w_out[0].astype(BF16), b_w_in[0].astype(BF16)])],
        tm=TM_FWD_A, relay_step=RELAY_STEP_FWD_A)
    wout_a = wout_a8.reshape(AW, D)
    (x1, zb, hs, h1, yb), ((wout_b8,),) = _fwd_b(
        xs, ya, wout_a, nw1, win_b8, p8, gcat, [_Gather([b_w_out[0].astype(BF16)])],
        tm=TM_FWD_B, relay_step=RELAY_STEP_FWD_B)
    wout_b = wout_b8.reshape(BW, D)
    dx2, dx2b, loss, g_nfw = _head(x1, yb, wout_b, nfw, tgt, tm=TM_HEAD)

    dx1, dx1b, dzb, g_p8, g_ga, g_gx, g_nw1 = _bwd_b(dx2, zb, hs, x1, nw1, win_b8, p8, gcat, wout_b, tm=TM_BWD_B)
    p_wout_b, _ = _wgrad_rows(yb, dx2b, [], nblk=NDEV, per=2, name="wgrad_b_out")
    p_win_b, _ = _wgrad_cols(h1, dzb, [], nblk=NDEV, name="wgrad_b_in")
    shapes_b = [(1, D), (1, D), (SUBLANES, BW), (1, 1)]
    pack_b = _pack([g_nfw, g_nw1, g_p8, loss], 16)
    qs_b, accs_b = _reduce_in_chip(
        [p_win_b, p_wout_b, g_ga.reshape(NDEV, -1, HD), g_gx.reshape(NDEV, -1, HD), pack_b], "reduce_b_in_chip")

    (dz, g_lnw, g_lnb, g_ws, g_bst), (lands_b,) = _bwd_a(
        dx1b, z, a_ln_w, a_ln_b, w_s, bst, wout_a, [_Exchange(qs_b)], tm=TM_BWD_A)
    p_wout_a, (red_b,) = _wgrad_rows(ya, dx1b, [_SumGather(accs_b[2:], lands_b[2:])], nblk=NDEV, per=1,
                                     name="wgrad_a_out", relay_step=2)
    shapes_a = [(1, AW), (1, AW), (CH, G)]
    pack_a = _pack([g_lnw, g_lnb, g_bst], 8)
    qs_a, accs_a = _reduce_in_chip([p_wout_a, g_ws, pack_a], "reduce_a_out_in_chip")
    p_win_a, (lands_a,) = _wgrad_cols(h0, dz, [_Exchange(qs_a)], nblk=NDEV, name="wgrad_a_in")
    (q_win_a,), (acc_win_a,) = _reduce_in_chip([p_win_a], "reduce_a_in_in_chip")
    (gx, g_nw0), ((l_win_a,), red_a) = _bwd_a_in(
        dz, dx1, xs, nw0, win_a8, [_Exchange([q_win_a]), _SumGather(accs_a[1:], lands_a[1:])],
        tm=TM_BWD_A_IN, relay_step=1)
    g_nw0 = _allreduce_direct(g_nw0, "allreduce_norm_w0")

    r_ga, r_gx, r_pack_b = red_b
    r_nfw, r_nw1, r_p8, loss = _unpack(r_pack_b, shapes_b)
    r_ws, r_pack_a = red_a
    r_lnw, r_lnb, r_bst = _unpack(r_pack_a, shapes_a)
    g_p8 = lax.dynamic_slice_in_dim(r_p8, me * (BW // NDEV), BW // NDEV, axis=1)
    loss = loss[0, 0]

    weights = dict(norm_w=norm_w, a_w_in=a_w_in, a_ln_w=a_ln_w, a_ln_b=a_ln_b, a_w_s=a_w_s, a_b_s=a_b_s, a_w_out=a_w_out,
                   b_w_in=b_w_in, b_conv_w=b_conv_w, b_conv_b=b_conv_b, b_gate_a_w=b_gate_a_w, b_gate_a_b=b_gate_a_b,
                   b_gate_x_w=b_gate_x_w, b_gate_x_b=b_gate_x_b, b_lambda=b_lambda, b_w_out=b_w_out, norm_f_w=norm_f_w)
    mom1 = dict(norm_w=m_norm_w, a_w_in=m_a_w_in, a_ln_w=m_a_ln_w, a_ln_b=m_a_ln_b, a_w_s=m_a_w_s, a_b_s=m_a_b_s,
                a_w_out=m_a_w_out, b_w_in=m_b_w_in, b_conv_w=m_b_conv_w, b_conv_b=m_b_conv_b, b_gate_a_w=m_b_gate_a_w,
                b_gate_a_b=m_b_gate_a_b, b_gate_x_w=m_b_gate_x_w, b_gate_x_b=m_b_gate_x_b, b_lambda=m_b_lambda,
                b_w_out=m_b_w_out, norm_f_w=m_norm_f_w)
    mom2 = dict(norm_w=v_norm_w, a_w_in=v_a_w_in, a_ln_w=v_a_ln_w, a_ln_b=v_a_ln_b, a_w_s=v_a_w_s, a_b_s=v_a_b_s,
                a_w_out=v_a_w_out, b_w_in=v_b_w_in, b_conv_w=v_b_conv_w, b_conv_b=v_b_conv_b, b_gate_a_w=v_b_gate_a_w,
                b_gate_a_b=v_b_gate_a_b, b_gate_x_w=v_b_gate_x_w, b_gate_x_b=v_b_gate_x_b, b_lambda=v_b_lambda,
                b_w_out=v_b_w_out, norm_f_w=v_norm_f_w)
    names = list(weights)

    def as2d(a):
        return a.reshape(-1, a.shape[-1])

    upd, grads = {}, {}
    for k, acc, land in (("a_w_in", acc_win_a, l_win_a), ("a_w_out", accs_a[0], lands_a[0]),
                         ("b_w_in", accs_b[0], lands_b[0]), ("b_w_out", accs_b[1], lands_b[1])):
        g, d, mo, vo = _adam_big(as2d(weights[k]), acc, land, as2d(mom1[k]), as2d(mom2[k]), "adam_" + k)
        grads[k] = g[None]
        upd[k] = (d, mo, vo)
    grads.update(
        norm_w=jnp.concatenate([g_nw0, r_nw1], axis=0), a_ln_w=r_lnw, a_ln_b=r_lnb,
        a_w_s=r_ws.reshape(1, G, CH, CH), a_b_s=r_bst.T[None],
        b_conv_w=g_p8[None, 0:4], b_conv_b=g_p8[4:5], b_gate_a_w=r_ga.reshape(1, BH, HD, HD), b_gate_a_b=g_p8[5:6],
        b_gate_x_w=r_gx.reshape(1, BH, HD, HD), b_gate_x_b=g_p8[6:7], b_lambda=g_p8[7:8], norm_f_w=r_nfw.reshape(D))
    small_names = [k for k in names if k not in upd]
    res = _adam_small([(as2d(weights[k]), as2d(grads[k]), as2d(mom1[k]), as2d(mom2[k])) for k in small_names])
    for k, r3 in zip(small_names, res):
        upd[k] = r3
    deltas = [upd[k][0].reshape(weights[k].shape) for k in names]
    new_m = [upd[k][1].reshape(weights[k].shape) for k in names]
    new_v = [upd[k][2].reshape(weights[k].shape) for k in names]
    return (loss, gx[None], *[grads[k] for k in names], *deltas, *new_m, *new_v)
```

```python
import jax
import jax.numpy as jnp
from jax import lax
from jax.experimental import pallas as pl
from jax.experimental.pallas import tpu as pltpu

F32 = jnp.float32
BF16 = jnp.bfloat16
MESH = pl.DeviceIdType.MESH

NDEV = 8
NCHIP_OTHER = 3
D = 1024
AW = 2048
G = 8
GD = AW // G
CH = 128
BW = 1536
BH = 12
HD = BW // BH
CA = 3 * AW // NDEV
CB = 2 * BW // NDEV
RMS_EPS = 1e-6
LN_EPS = 1e-5
RG_C = 8.0
LR, B1, B2, ADAM_EPS, WD, STEP = 0.001, 0.9, 0.999, 1e-08, 0.01, 10
V7X_VMEM_BYTES = 64 * 1024 * 1024
VMEM_LIMIT = V7X_VMEM_BYTES - 8 * 1024 * 1024
SUBLANES = 8
LANES = 128
BF16_ROWS = 16
GELU_C = 0.7978845608028654
GELU_K = 0.044715

_VMEM = pl.BlockSpec(memory_space=pltpu.VMEM)
_HBM = pl.BlockSpec(memory_space=pltpu.HBM)


def _sds(shape, dtype):
    return jax.ShapeDtypeStruct(tuple(shape), dtype)


def _params(**kw):
    return pltpu.CompilerParams(vmem_limit_bytes=VMEM_LIMIT, **kw)


def _gelu_t(z):
    t = jnp.tanh(GELU_C * (z + GELU_K * (z * z * z)))
    return 0.5 * z * (1.0 + t), t


def _dgelu(z, t):
    return 0.5 * (1.0 + t) + 0.5 * z * (1.0 - t * t) * (GELU_C * (1.0 + 3.0 * GELU_K * z * z))


def _sigmoid(v):
    return 0.5 * jnp.tanh(0.5 * v) + 0.5


def _softplus_neg(lam):
    return jnp.maximum(-lam, 0.0) + jnp.log1p(jnp.exp(-jnp.abs(lam)))


def _dot(a, b):
    return jnp.dot(a, b, preferred_element_type=F32)


def _dot_nt(a, b):
    return lax.dot_general(a, b, (((1,), (1,)), ((), ())), preferred_element_type=F32)


def _rowsum(v):
    return jnp.sum(v, axis=0, keepdims=True)


def _causal_mask():
    r = lax.broadcasted_iota(jnp.int32, (CH, CH), 0)
    c = lax.broadcasted_iota(jnp.int32, (CH, CH), 1)
    return r >= c


def _rms(x):
    return lax.rsqrt(jnp.mean(x * x, axis=-1, keepdims=True) + RMS_EPS)


def _rms_bwd(dh, x, r, nw):
    gy = dh * nw
    return r * gy - x * (r * r * r) * jnp.mean(gy * x, axis=-1, keepdims=True)


def _place():
    return lax.axis_index("x"), lax.axis_index("y"), lax.axis_index("c")


def _other_chips(x, y):
    return [(1 - x, y), (x, 1 - y), (1 - x, 1 - y)]


GATHER_SLOTS = 10


def _gather_ops(ins, outs, send_sems, recv_sems, local_sems):
    n = len(ins)
    x, y, c = _place()
    sibling = (x, y, 1 - c)
    xn, yn, dg = _other_chips(x, y)
    split = [ins[i].shape[0] % (2 * BF16_ROWS) == 0 for i in range(n)]

    def blk(chip, core):
        return 4 * chip[0] + 2 * chip[1] + core

    me = blk((x, y), c)

    def part(ref, i, half):
        if half is None:
            return ref
        h = ins[i].shape[0] // 2
        return ref.at[pl.ds(half * h, h)]

    def copy(i, k, block, to, half=None, src=None):
        dst = part(outs[i].at[block], i, half)
        return pltpu.make_async_remote_copy(
            src_ref=dst if src is None else part(src, i, half), dst_ref=dst,
            send_sem=send_sems.at[k, i], recv_sem=recv_sems.at[k, i], device_id=to, device_id_type=MESH)

    def first_copies():
        mine = [pltpu.make_async_copy(ins[i], outs[i].at[me], local_sems.at[i]) for i in range(n)]
        first = []
        for i in range(n):
            first.append(copy(i, 0, me, sibling, src=ins[i]))
            if split[i]:
                first.append(copy(i, 1, me, (*xn, c), 0, ins[i]))
                first.append(copy(i, 3, me, (*yn, c), 1, ins[i]))
                first.append(copy(i, 2, me, (*xn, c), 1, ins[i]))
                first.append(copy(i, 4, me, (*yn, c), 0, ins[i]))
            else:
                first.append(copy(i, 1, me, (*xn, c), None, ins[i]))
                first.append(copy(i, 3, me, (*yn, c), None, ins[i]))
                first.append(copy(i, 5, me, (*dg, c), None, ins[i]))
        return mine, first

    def onward():
        out = []
        for i in range(n):
            if split[i]:
                out.append(copy(i, 5, blk(xn, c), (*yn, c), 0))
                out.append(copy(i, 6, blk(yn, c), (*xn, c), 1))
        return out

    def start():
        mine, first = first_copies()
        for cp in mine + first:
            cp.start()

    def relay():
        sends = onward()
        for i in range(n):
            if split[i]:
                copy(i, 1, blk(xn, c), sibling, 0).wait_recv()
                sends.pop(0).start()
                copy(i, 3, blk(yn, c), sibling, 1).wait_recv()
                sends.pop(0).start()

    def finish():
        mine, first = first_copies()
        passed = []

        def pass_on(i, j, chip):
            fwd = copy(i, 7 + j, blk(chip, c), sibling)
            fwd.start()
            passed.append(fwd)

        for i in range(n):
            if split[i]:
                copy(i, 2, blk(xn, c), sibling, 1).wait_recv()
                pass_on(i, 0, xn)
                copy(i, 4, blk(yn, c), sibling, 0).wait_recv()
                pass_on(i, 1, yn)
                copy(i, 5, blk(dg, c), sibling, 0).wait_recv()
                copy(i, 6, blk(dg, c), sibling, 1).wait_recv()
                pass_on(i, 2, dg)
            else:
                copy(i, 1, blk(xn, c), sibling).wait_recv()
                pass_on(i, 0, xn)
                copy(i, 3, blk(yn, c), sibling).wait_recv()
                pass_on(i, 1, yn)
                copy(i, 5, blk(dg, c), sibling).wait_recv()
                pass_on(i, 2, dg)
        for i in range(n):
            copy(i, 0, blk((x, y), 1 - c), sibling).wait_recv()
            for j, chip in enumerate((xn, yn, dg)):
                copy(i, 7 + j, blk(chip, 1 - c), sibling).wait_recv()
        for cp in first + passed + onward():
            cp.wait_send()
        for cp in mine:
            cp.wait()

    return start, relay, finish


def _gather_sems(n):
    return [pltpu.SemaphoreType.DMA((GATHER_SLOTS, n)), pltpu.SemaphoreType.DMA((GATHER_SLOTS, n)),
            pltpu.SemaphoreType.DMA((n,))]


class _Gather:
    def __init__(self, shards):
        n = len(shards)
        self.ins, self.in_specs = list(shards), [_HBM] * n
        self.out_shape = [_sds((NDEV,) + s.shape, s.dtype) for s in shards]
        self.out_specs = [_HBM] * n
        self.scratch = _gather_sems(n)

    def ops(self, ins, outs, scr):
        return _gather_ops(ins, outs, *scr)


class _Exchange:
    def __init__(self, qs):
        n = len(qs)
        self.ins, self.in_specs = list(qs), [_HBM] * n
        self.out_shape = [_sds(q.shape, q.dtype) for q in qs]
        self.out_specs = [_HBM] * n
        self.scratch = [pltpu.SemaphoreType.DMA((NCHIP_OTHER, n)), pltpu.SemaphoreType.DMA((NCHIP_OTHER, n))]

    def ops(self, ins, outs, scr):
        send_sems, recv_sems = scr
        n = len(ins)
        x, y, c = _place()
        chips = _other_chips(x, y)

        def copies():
            return [pltpu.make_async_remote_copy(
                src_ref=ins[i].at[j], dst_ref=outs[i].at[j], send_sem=send_sems.at[j, i],
                recv_sem=recv_sems.at[j, i], device_id=(*chips[j], c), device_id_type=MESH)
                for i in range(n) for j in range(NCHIP_OTHER)]

        def start():
            for cp in copies():
                cp.start()

        def finish():
            cps = copies()
            for cp in cps:
                cp.wait_recv()
            for cp in cps:
                cp.wait_send()

        return start, lambda: None, finish


class _SumGather:
    def __init__(self, accs, lands):
        n = len(accs)
        self.n = n
        self.ins, self.in_specs = list(accs) + list(lands), [_VMEM] * (2 * n)
        self.out_shape = [_sds((NDEV,) + a.shape, a.dtype) for a in accs]
        self.out_specs = [_HBM] * n
        self.scratch = [pltpu.VMEM(a.shape, a.dtype) for a in accs] + _gather_sems(n)

    def ops(self, ins, outs, scr):
        n = self.n
        accs, lands, mine = ins[:n], ins[n:], scr[:n]
        g_start, relay, finish = _gather_ops(mine, outs, *scr[n:])

        def start():
            for i in range(n):
                mine[i][...] = accs[i][...] + lands[i][0] + lands[i][1] + lands[i][2]
            g_start()

        return start, relay, finish


def _call(main, jobs, *, name, grid, ins, in_specs, out_shape, out_specs, scratch, relay_step=0):
    nsteps = grid[0] if grid else 1
    n_in, n_out, n_scr = len(ins), len(out_shape), len(scratch)

    def body(*refs):
        pos = [0]

        def take(k):
            r = refs[pos[0]:pos[0] + k]
            pos[0] += k
            return r

        m_in = take(n_in)
        j_in = [take(len(j.ins)) for j in jobs]
        m_out = take(n_out)
        j_out = [take(len(j.out_shape)) for j in jobs]
        m_scr = take(n_scr)
        j_scr = [take(len(j.scratch)) for j in jobs]
        ops = [j.ops(a, b, s) for j, a, b, s in zip(jobs, j_in, j_out, j_scr)]
        i = pl.program_id(0) if grid else 0
        if not grid:
            for o in ops:
                o[0]()
            main(i, m_in, m_out, m_scr)
            for o in ops:
                o[1]()
            for o in ops:
                o[2]()
            return

        if ops:
            @pl.when(i == 0)
            def _():
                for o in ops:
                    o[0]()

        main(i, m_in, m_out, m_scr)

        if ops:
            @pl.when(i == min(relay_step, nsteps - 1))
            def _():
                for o in ops:
                    o[1]()

            @pl.when(i == nsteps - 1)
            def _():
                for o in ops:
                    o[2]()

    extra = dict(dimension_semantics=("arbitrary",)) if grid else {}
    res = pl.pallas_call(
        body, name=name, grid=grid,
        in_specs=list(in_specs) + [s for j in jobs for s in j.in_specs],
        out_specs=list(out_specs) + [s for j in jobs for s in j.out_specs],
        out_shape=list(out_shape) + [s for j in jobs for s in j.out_shape],
        scratch_shapes=list(scratch) + [s for j in jobs for s in j.scratch],
        compiler_params=_params(**extra),
    )(*ins, *[a for j in jobs for a in j.ins])
    main_out, rest, job_out = res[:n_out], res[n_out:], []
    for j in jobs:
        k = len(j.out_shape)
        job_out.append(rest[:k])
        rest = rest[k:]
    return main_out, job_out


def _comm_only(jobs, name):
    _, job_out = _call(lambda i, a, b, s: None, jobs, name=name, grid=(), ins=[], in_specs=[], out_shape=[],
                       out_specs=[], scratch=[])
    return job_out


def _reduce_in_chip(ps, name):
    n = len(ps)

    def body(*refs):
        p_refs, q_refs, acc_refs = refs[:n], refs[n:2 * n], refs[2 * n:3 * n]
        rest = refs[3 * n:]
        mines, lands = rest[:n], rest[n:2 * n]
        send_sems, recv_sems, local_sems = rest[2 * n:]
        x, y, c = _place()
        sibling = (x, y, 1 - c)
        pairs = []
        for i in range(n):
            for px in range(2):
                for py in range(2):
                    pi = 2 * px + py
                    loc = pltpu.make_async_copy(p_refs[i].at[4 * px + 2 * py + c], mines[i].at[pi],
                                                local_sems.at[pi, i])
                    cp = pltpu.make_async_remote_copy(
                        src_ref=p_refs[i].at[4 * px + 2 * py + (1 - c)], dst_ref=lands[i].at[pi],
                        send_sem=send_sems.at[pi, i], recv_sem=recv_sems.at[pi, i],
                        device_id=sibling, device_id_type=MESH)
                    loc.start()
                    cp.start()
                    pairs.append((loc, cp))
        for loc, cp in pairs:
            loc.wait()
            cp.wait_recv()
        for i in range(n):
            for j, (qx, qy) in enumerate(_other_chips(x, y)):
                qi = 2 * qx + qy
                q_refs[i][j] = (mines[i][qi].astype(F32) + lands[i][qi].astype(F32)).astype(q_refs[i].dtype)
            mi = 2 * x + y
            acc_refs[i][...] = mines[i][mi].astype(F32) + lands[i][mi].astype(F32)
        for _, cp in pairs:
            cp.wait_send()

    blk = [p.shape[1:] for p in ps]
    res = pl.pallas_call(
        body, name=name, in_specs=[_HBM] * n, out_specs=[_VMEM] * (2 * n),
        out_shape=[_sds((NCHIP_OTHER,) + b, p.dtype) for b, p in zip(blk, ps)] + [_sds(b, F32) for b in blk],
        scratch_shapes=[pltpu.VMEM((4,) + b, p.dtype) for b, p in zip(blk, ps)]
        + [pltpu.VMEM((4,) + b, p.dtype) for b, p in zip(blk, ps)]
        + [pltpu.SemaphoreType.DMA((4, n)), pltpu.SemaphoreType.DMA((4, n)), pltpu.SemaphoreType.DMA((4, n))],
        compiler_params=_params(),
    )(*ps)
    return res[:n], res[n:]


def _allreduce_direct(v, name):
    def body(v_ref, o_ref, buf, send_sems, recv_sems):
        x, y, c = _place()
        me = 4 * x + 2 * y + c
        buf[me] = v_ref[...]
        cps = []
        for k in range(1, NDEV):
            fx, fy, fc = (k >> 2) & 1, (k >> 1) & 1, k & 1
            peer = ((1 - x) if fx else x, (1 - y) if fy else y, (1 - c) if fc else c)
            cps.append((peer, pltpu.make_async_remote_copy(
                src_ref=buf.at[me], dst_ref=buf.at[me], send_sem=send_sems.at[k - 1], recv_sem=recv_sems.at[k - 1],
                device_id=peer, device_id_type=MESH)))
        for _, cp in cps:
            cp.start()
        for k, (peer, _) in enumerate(cps):
            theirs = 4 * peer[0] + 2 * peer[1] + peer[2]
            pltpu.make_async_remote_copy(
                src_ref=buf.at[theirs], dst_ref=buf.at[theirs], send_sem=send_sems.at[k], recv_sem=recv_sems.at[k],
                device_id=peer, device_id_type=MESH).wait_recv()
        acc = buf[0]
        for j in range(1, NDEV):
            acc = acc + buf[j]
        o_ref[...] = acc
        for _, cp in cps:
            cp.wait_send()

    return pl.pallas_call(
        body, name=name, in_specs=[_VMEM], out_specs=_VMEM, out_shape=_sds(v.shape, v.dtype),
        scratch_shapes=[pltpu.VMEM((NDEV,) + v.shape, v.dtype), pltpu.SemaphoreType.DMA((NDEV - 1,)),
                        pltpu.SemaphoreType.DMA((NDEV - 1,))],
        compiler_params=_params(),
    )(v)


def _fwd_a(x, nw, win8, lnw, lnb, ws, bst, jobs, *, tm, relay_step):
    s_len = x.shape[0]
    nt = s_len // tm
    nch = tm // CH

    def main(i, ins, outs, scr):
        x_ref, nw_ref, win_ref, lnw_ref, lnb_ref, ws_ref, bst_ref = ins
        z_ref, h_ref, y_ref = outs
        wc_scr, gv_scr = scr

        @pl.when(i == 0)
        def _():
            m = _causal_mask()
            for g in range(G):
                wc_scr[g] = jnp.where(m, ws_ref[g], 0.0).astype(BF16)

        x = x_ref[...]
        h = (x * _rms(x) * nw_ref[...]).astype(BF16)
        h_ref[...] = h
        for k in range(NDEV):
            z_ref[:, k * CA:(k + 1) * CA] = _dot(h, win_ref[k])

        ssum = jnp.zeros((tm, 1), F32)
        for g in range(G):
            gv = _gelu_t(z_ref[:, AW + g * GD:AW + (g + 1) * GD])[0]
            gv_scr[:, g * GD:(g + 1) * GD] = gv
            ssum = ssum + jnp.sum(gv, axis=-1, keepdims=True)
        mu = ssum * (1.0 / AW)
        vsum = jnp.zeros((tm, 1), F32)
        for g in range(G):
            dlt = gv_scr[:, g * GD:(g + 1) * GD] - mu
            vsum = vsum + jnp.sum(dlt * dlt, axis=-1, keepdims=True)
        rstd = lax.rsqrt(vsum * (1.0 / AW) + LN_EPS)

        for g in range(G):
            cs = slice(g * GD, (g + 1) * GD)
            v = (gv_scr[:, cs] - mu) * rstd * lnw_ref[:, cs] + lnb_ref[:, cs]
            vb = v.astype(BF16)
            u = _gelu_t(z_ref[:, cs])[0]
            zg = z_ref[:, 2 * AW + g * GD:2 * AW + (g + 1) * GD]
            sg = zg * _sigmoid(zg)
            for n in range(nch):
                rs = slice(n * CH, (n + 1) * CH)
                s = _dot(wc_scr[g], vb[rs, :]) + bst_ref[:, g:g + 1]
                y_ref[rs, cs] = (u[rs, :] * s * sg[rs, :]).astype(BF16)

    tile = lambda w: pl.BlockSpec((tm, w), lambda i: (i, 0))
    return _call(
        main, jobs, name="fwd_a", grid=(nt,), relay_step=relay_step,
        ins=[x, nw, win8, lnw, lnb, ws, bst], in_specs=[tile(D), _VMEM, _VMEM, _VMEM, _VMEM, _VMEM, _VMEM],
        out_shape=[_sds((s_len, 3 * AW), F32), _sds((s_len, D), BF16), _sds((s_len, AW), BF16)],
        out_specs=[tile(3 * AW), tile(D), tile(AW)],
        scratch=[pltpu.VMEM((G, CH, CH), BF16), pltpu.VMEM((tm, AW), F32)])


def _bwd_a(dx1, z, lnw, lnb, ws, bst, wout, jobs, *, tm):
    s_len = dx1.shape[0]
    nt = s_len // tm
    nch = tm // CH

    def main(i, ins, outs, scr):
        dx1_ref, z_ref, lnw_ref, lnb_ref, ws_ref, bst_ref, wout_ref = ins
        dz_ref, glnw_ref, glnb_ref, gws_ref, gbst_ref = outs
        wc_scr, wct_scr, vh_scr, dgv_scr, dy_scr, dv_scr, gbs_acc, gwc_acc = scr

        @pl.when(i == 0)
        def _():
            m = _causal_mask()
            for g in range(G):
                wm = jnp.where(m, ws_ref[g], 0.0)
                wc_scr[g] = wm.astype(BF16)
                wct_scr[g] = wm.T.astype(BF16)
            glnw_ref[...] = jnp.zeros_like(glnw_ref)
            glnb_ref[...] = jnp.zeros_like(glnb_ref)
            gbs_acc[...] = jnp.zeros_like(gbs_acc)
            gwc_acc[...] = jnp.zeros_like(gwc_acc)

        dy_scr[...] = _dot_nt(dx1_ref[...], wout_ref[...])

        ssum = jnp.zeros((tm, 1), F32)
        for g in range(G):
            cs = slice(g * GD, (g + 1) * GD)
            zv = z_ref[:, AW + g * GD:AW + (g + 1) * GD]
            gv, t = _gelu_t(zv)
            vh_scr[:, cs] = gv
            dgv_scr[:, cs] = _dgelu(zv, t)
            ssum = ssum + jnp.sum(gv, axis=-1, keepdims=True)
        mu = ssum * (1.0 / AW)
        vsum = jnp.zeros((tm, 1), F32)
        for g in range(G):
            dlt = vh_scr[:, g * GD:(g + 1) * GD] - mu
            vsum = vsum + jnp.sum(dlt * dlt, axis=-1, keepdims=True)
        rstd = lax.rsqrt(vsum * (1.0 / AW) + LN_EPS)

        m1 = jnp.zeros((tm, 1), F32)
        m2 = jnp.zeros((tm, 1), F32)
        for g in range(G):
            cs = slice(g * GD, (g + 1) * GD)
            gs = slice(2 * AW + g * GD, 2 * AW + (g + 1) * GD)
            vhat = (vh_scr[:, cs] - mu) * rstd
            vh_scr[:, cs] = vhat
            vb = (vhat * lnw_ref[:, cs] + lnb_ref[:, cs]).astype(BF16)
            zu = z_ref[:, cs]
            u, tu = _gelu_t(zu)
            zg = z_ref[:, gs]
            sig = _sigmoid(zg)
            sg = zg * sig
            dy = dy_scr[:, cs]
            dsf = dy * u * sg
            dsb = dsf.astype(BF16)
            dvs = []
            for n in range(nch):
                rs = slice(n * CH, (n + 1) * CH)
                s = _dot(wc_scr[g], vb[rs, :]) + bst_ref[:, g:g + 1]
                dys = dy[rs, :] * s
                dz_ref[rs, cs] = (dys * sg[rs, :] * _dgelu(zu[rs, :], tu[rs, :])).astype(BF16)
                dz_ref[rs, gs] = (dys * u[rs, :] * (sig[rs, :] * (1.0 + zg[rs, :] * (1.0 - sig[rs, :])))).astype(BF16)
                gbs_acc[g] += dsf[rs, :]
                gwc_acc[g] += _dot_nt(dsb[rs, :], vb[rs, :])
                dvs.append(_dot(wct_scr[g], dsb[rs, :]))
            dv = jnp.concatenate(dvs, axis=0) if nch > 1 else dvs[0]
            glnw_ref[:, cs] += _rowsum(dv * vhat)
            glnb_ref[:, cs] += _rowsum(dv)
            dvh = dv * lnw_ref[:, cs]
            dv_scr[:, cs] = dvh
            m1 = m1 + jnp.sum(dvh, axis=-1, keepdims=True)
            m2 = m2 + jnp.sum(dvh * vhat, axis=-1, keepdims=True)
        m1 = m1 * (1.0 / AW)
        m2 = m2 * (1.0 / AW)
        for g in range(G):
            cs = slice(g * GD, (g + 1) * GD)
            dgv = rstd * (dv_scr[:, cs] - m1 - vh_scr[:, cs] * m2)
            dz_ref[:, AW + g * GD:AW + (g + 1) * GD] = (dgv * dgv_scr[:, cs]).astype(BF16)

        @pl.when(i == nt - 1)
        def _():
            m = _causal_mask()
            for g in range(G):
                gws_ref[g] = jnp.where(m, gwc_acc[g], 0.0)
                gbst_ref[:, g:g + 1] = jnp.sum(gbs_acc[g], axis=-1, keepdims=True)

    tile = lambda w: pl.BlockSpec((tm, w), lambda i: (i, 0))
    whole = lambda *s: pl.BlockSpec(s, lambda i: (0,) * len(s))
    big = lambda dt: pltpu.VMEM((tm, AW), dt)
    return _call(
        main, jobs, name="bwd_a", grid=(nt,),
        ins=[dx1, z, lnw, lnb, ws, bst, wout], in_specs=[tile(D), tile(3 * AW), _VMEM, _VMEM, _VMEM, _VMEM, _VMEM],
        out_shape=[_sds((s_len, 3 * AW), BF16), _sds((1, AW), F32), _sds((1, AW), F32), _sds((G, CH, CH), F32),
                   _sds((CH, G), F32)],
        out_specs=[tile(3 * AW), whole(1, AW), whole(1, AW), whole(G, CH, CH), whole(CH, G)],
        scratch=[pltpu.VMEM((G, CH, CH), BF16), pltpu.VMEM((G, CH, CH), BF16), big(F32), big(F32), big(F32), big(F32),
                 pltpu.VMEM((G, CH, GD), F32), pltpu.VMEM((G, CH, CH), F32)])


def _bwd_a_in(dz, dx1, x, nw, win8, jobs, *, tm, relay_step):
    s_len = x.shape[0]
    nt = s_len // tm

    def main(i, ins, outs, scr):
        dz_ref, dx1_ref, x_ref, nw_ref, win_ref = ins
        gx_ref, gnw_ref = outs

        @pl.when(i == 0)
        def _():
            gnw_ref[...] = jnp.zeros_like(gnw_ref)

        dh = jnp.zeros((tm, D), F32)
        for k in range(NDEV):
            dh = dh + _dot_nt(dz_ref[:, k * CA:(k + 1) * CA], win_ref[k])
        x = x_ref[...]
        r = _rms(x)
        gx_ref[...] = dx1_ref[...] + _rms_bwd(dh, x, r, nw_ref[...])
        gnw_ref[...] += _rowsum(dh * x * r)

    tile = lambda w: pl.BlockSpec((tm, w), lambda i: (i, 0))
    return _call(
        main, jobs, name="bwd_a_in", grid=(nt,), relay_step=relay_step,
        ins=[dz, dx1, x, nw, win8], in_specs=[tile(3 * AW), tile(D), tile(D), _VMEM, _VMEM],
        out_shape=[_sds((s_len, D), F32), _sds((1, D), F32)],
        out_specs=[tile(D), pl.BlockSpec((1, D), lambda i: (0, 0))], scratch=[])


def _conv(p8_ref, cs, xb, xm1, xm2, xm3):
    xc = p8_ref[4:5, cs] + p8_ref[3:4, cs] * xb
    xc = xc + p8_ref[0:1, cs] * xm3
    xc = xc + p8_ref[1:2, cs] * xm2
    return xc + p8_ref[2:3, cs] * xm1


def _gates(p8_ref, gcat_ref, hh, xc):
    cs = slice(hh * HD, (hh + 1) * HD)
    pre = _dot(xc.astype(BF16), gcat_ref[hh])
    r = _sigmoid(pre[:, :HD] + p8_ref[5:6, cs])
    ig = _sigmoid(pre[:, HD:] + p8_ref[6:7, cs])
    sp = _softplus_neg(p8_ref[7:8, cs])
    la = (-RG_C) * r * sp
    a = jnp.exp(la)
    m2 = jnp.tanh(-la) * (1.0 + a * a)
    rm = lax.rsqrt(m2)
    mult = jnp.where(m2 > 0.0, m2 * rm, 0.0)
    return r, ig, sp, a, mult, rm


def _scan_rows(a_ref, b_ref, out_ref, carry, tm, reverse):
    row = lax.broadcasted_iota(jnp.int32, (SUBLANES, BW), 0)
    ngrp = tm // SUBLANES

    def step(j, cr):
        jj = (ngrp - 1 - j) if reverse else j
        off = pl.multiple_of(jj * SUBLANES, SUBLANES)
        a = a_ref[pl.ds(off, SUBLANES), :]
        b = b_ref[pl.ds(off, SUBLANES), :]
        for sh in (1, 2, 4):
            if reverse:
                a_s = pltpu.roll(a, SUBLANES - sh, 0)
                b_s = pltpu.roll(b, SUBLANES - sh, 0)
                m = row < SUBLANES - sh
            else:
                a_s = pltpu.roll(a, sh, 0)
                b_s = pltpu.roll(b, sh, 0)
                m = row >= sh
            b = jnp.where(m, a * b_s + b, b)
            a = jnp.where(m, a * a_s, a)
        o = b + a * cr
        out_ref[pl.ds(off, SUBLANES), :] = o
        return o[0:1, :] if reverse else o[SUBLANES - 1:SUBLANES, :]

    return lax.fori_loop(0, ngrp, step, carry)


def _fwd_b(x, ya, wout_a, nw, win8, p8, gcat, jobs, *, tm, relay_step):
    s_len = x.shape[0]
    nt = s_len // tm

    def main(i, ins, outs, scr):
        x_ref, ya_ref, wouta_ref, nw_ref, win_ref, p8_ref, gcat_ref = ins
        x1_ref, zb_ref, hs_ref, h1_ref, yb_ref = outs
        xbe_scr, a_scr, b_scr, carry_scr = scr

        @pl.when(i == 0)
        def _():
            xbe_scr[0:SUBLANES, :] = jnp.zeros((SUBLANES, BW), F32)
            carry_scr[...] = jnp.zeros_like(carry_scr)

        x1 = x_ref[...] + _dot(ya_ref[...], wouta_ref[...])
        x1_ref[...] = x1
        h = (x1 * _rms(x1) * nw_ref[...]).astype(BF16)
        h1_ref[...] = h
        for k in range(NDEV):
            zb_ref[:, k * CB:(k + 1) * CB] = _dot(h, win_ref[k])
        xbe_scr[SUBLANES:SUBLANES + tm, :] = zb_ref[:, :BW]
        for hh in range(BH):
            cs = slice(hh * HD, (hh + 1) * HD)
            xc = _conv(p8_ref, cs, xbe_scr[SUBLANES:SUBLANES + tm, cs], xbe_scr[7:7 + tm, cs],
                       xbe_scr[6:6 + tm, cs], xbe_scr[5:5 + tm, cs])
            _, ig, _, a, mult, _ = _gates(p8_ref, gcat_ref, hh, xc)
            a_scr[:, cs] = a
            b_scr[:, cs] = mult * (ig * xc)
        xbe_scr[0:SUBLANES, :] = xbe_scr[tm:tm + SUBLANES, :]
        carry_scr[...] = _scan_rows(a_scr, b_scr, hs_ref, carry_scr[...], tm, False)
        for hh in range(BH):
            cs = slice(hh * HD, (hh + 1) * HD)
            gt = zb_ref[:, BW + hh * HD:BW + (hh + 1) * HD]
            yb_ref[:, cs] = (hs_ref[:, cs] * (gt * _sigmoid(gt))).astype(BF16)

    tile = lambda w: pl.BlockSpec((tm, w), lambda i: (i, 0))
    return _call(
        main, jobs, name="fwd_b", grid=(nt,), relay_step=relay_step,
        ins=[x, ya, wout_a, nw, win8, p8, gcat], in_specs=[tile(D), tile(AW), _VMEM, _VMEM, _VMEM, _VMEM, _VMEM],
        out_shape=[_sds((s_len, D), F32), _sds((s_len, 2 * BW), F32), _sds((s_len, BW), F32), _sds((s_len, D), BF16),
                   _sds((s_len, BW), BF16)],
        out_specs=[tile(D), tile(2 * BW), tile(BW), tile(D), tile(BW)],
        scratch=[pltpu.VMEM((tm + SUBLANES, BW), F32), pltpu.VMEM((tm, BW), F32), pltpu.VMEM((tm, BW), F32),
                 pltpu.VMEM((1, BW), F32)])


def _head(x1, yb, wout, nfw, tgt, *, tm):
    s_len = x1.shape[0]

    def main(i, ins, outs, scr):
        x1_ref, yb_ref, wout_ref, nfw_ref, t_ref = ins
        dx2_ref, dx2b_ref, loss_ref, gnfw_ref = outs

        @pl.when(i == 0)
        def _():
            loss_ref[...] = jnp.zeros_like(loss_ref)
            gnfw_ref[...] = jnp.zeros_like(gnfw_ref)

        x2 = x1_ref[...] + _dot(yb_ref[...], wout_ref[...])
        rf = _rms(x2)
        xn = x2 * rf
        e = xn * nfw_ref[...] - t_ref[...]
        loss_ref[...] += (0.5 / D) * jnp.sum(jnp.sum(e * e, axis=-1, keepdims=True), axis=0, keepdims=True)
        dyf = e * (1.0 / D)
        gnfw_ref[...] += _rowsum(dyf * xn)
        dx2 = _rms_bwd(dyf, x2, rf, nfw_ref[...])
        dx2_ref[...] = dx2
        dx2b_ref[...] = dx2.astype(BF16)

    tile = lambda w: pl.BlockSpec((tm, w), lambda i: (i, 0))
    whole = lambda *s: pl.BlockSpec(s, lambda i: (0,) * len(s))
    (dx2, dx2b, loss, gnfw), _ = _call(
        main, [], name="head", grid=(s_len // tm,),
        ins=[x1, yb, wout, nfw, tgt], in_specs=[tile(D), tile(BW), _VMEM, _VMEM, tile(D)],
        out_shape=[_sds((s_len, D), F32), _sds((s_len, D), BF16), _sds((1, 1), F32), _sds((1, D), F32)],
        out_specs=[tile(D), tile(D), whole(1, 1), whole(1, D)], scratch=[])
    return dx2, dx2b, loss, gnfw


def _bwd_b(dx2, zb, hs, x1, nw, win8, p8, gcat, wout, *, tm):
    s_len = x1.shape[0]
    nt = s_len // tm
    per = tm // SUBLANES

    def main(i, ins, outs, scr):
        dx2_ref, zb_ref, zbp_ref, hs_ref, hsp_ref, x1_ref, nw_ref, win_ref, p8_ref, gcat_ref, wout_ref = ins
        dx1_ref, dx1b_ref, dzb_ref, gp8_ref, gga_ref, ggx_ref, gnw_ref = outs
        (xbe_scr, hse_scr, ae_scr, an_scr, r_scr, i_scr, m_scr, xc_scr, cc_scr, dhd_scr, dh_scr, dy_scr, dxce_scr,
         carry_scr, afirst_scr) = scr
        ti = nt - 1 - i

        @pl.when(i == 0)
        def _():
            gp8_ref[...] = jnp.zeros_like(gp8_ref)
            gga_ref[...] = jnp.zeros_like(gga_ref)
            ggx_ref[...] = jnp.zeros_like(ggx_ref)
            gnw_ref[...] = jnp.zeros_like(gnw_ref)
            dxce_scr[tm:tm + SUBLANES, :] = jnp.zeros((SUBLANES, BW), F32)
            carry_scr[...] = jnp.zeros_like(carry_scr)
            afirst_scr[...] = jnp.zeros_like(afirst_scr)

        has_prev = (ti > 0).astype(F32)
        xbe_scr[0:SUBLANES, :] = zbp_ref[:, :BW] * has_prev
        xbe_scr[SUBLANES:SUBLANES + tm, :] = zb_ref[:, :BW]
        hse_scr[0:SUBLANES, :] = hsp_ref[...] * has_prev
        hse_scr[SUBLANES:SUBLANES + tm, :] = hs_ref[...]

        dx2 = dx2_ref[...]
        dy_scr[...] = _dot_nt(dx2.astype(BF16), wout_ref[...])

        for hh in range(BH):
            cs = slice(hh * HD, (hh + 1) * HD)
            xc = _conv(p8_ref, cs, xbe_scr[SUBLANES:SUBLANES + tm, cs], xbe_scr[7:7 + tm, cs],
                       xbe_scr[6:6 + tm, cs], xbe_scr[5:5 + tm, cs])
            r, ig, _, a, mult, rm = _gates(p8_ref, gcat_ref, hh, xc)
            cc_scr[:, cs] = a * hse_scr[7:7 + tm, cs] - (ig * xc) * (a * a * rm)
            xc_scr[:, cs] = xc
            r_scr[:, cs] = r
            i_scr[:, cs] = ig
            m_scr[:, cs] = mult
            ae_scr[0:tm, cs] = a
            gt = zb_ref[:, BW + hh * HD:BW + (hh + 1) * HD]
            sig = _sigmoid(gt)
            dy = dy_scr[:, cs]
            dhd_scr[:, cs] = dy * (gt * sig)
            dzb_ref[:, BW + hh * HD:BW + (hh + 1) * HD] = (
                dy * hs_ref[:, cs] * (sig * (1.0 + gt * (1.0 - sig)))).astype(BF16)
        ae_scr[tm:tm + SUBLANES, :] = jnp.broadcast_to(afirst_scr[...], (SUBLANES, BW))
        an_scr[...] = ae_scr[1:1 + tm, :]
        afirst_scr[...] = ae_scr[0:1, :]
        carry_scr[...] = _scan_rows(an_scr, dhd_scr, dh_scr, carry_scr[...], tm, True)

        for hh in range(BH):
            cs = slice(hh * HD, (hh + 1) * HD)
            dh = dh_scr[:, cs]
            mult = m_scr[:, cs]
            ig = i_scr[:, cs]
            r = r_scr[:, cs]
            xc = xc_scr[:, cs]
            lam = p8_ref[7:8, cs]
            sp = _softplus_neg(lam)
            dla = dh * cc_scr[:, cs]
            gp8_ref[7:8, cs] += _rowsum(dla * ((-RG_C) * r)) * (-_sigmoid(-lam))
            dpr = dla * ((-RG_C) * sp) * (r * (1.0 - r))
            dpi = dh * mult * xc * (ig * (1.0 - ig))
            gp8_ref[5:6, cs] += _rowsum(dpr)
            gp8_ref[6:7, cs] += _rowsum(dpi)
            dcat = jnp.concatenate([dpr, dpi], axis=1).astype(BF16)
            dxc = dh * mult * ig + _dot_nt(dcat, gcat_ref[hh])
            gg = _dot(xc.T.astype(BF16), dcat)
            gga_ref[hh] += gg[:, :HD]
            ggx_ref[hh] += gg[:, HD:]
            dxce_scr[0:tm, cs] = dxc
            gp8_ref[4:5, cs] += _rowsum(dxc)
            gp8_ref[3:4, cs] += _rowsum(dxc * xbe_scr[SUBLANES:SUBLANES + tm, cs])
            gp8_ref[2:3, cs] += _rowsum(dxc * xbe_scr[7:7 + tm, cs])
            gp8_ref[1:2, cs] += _rowsum(dxc * xbe_scr[6:6 + tm, cs])
            gp8_ref[0:1, cs] += _rowsum(dxc * xbe_scr[5:5 + tm, cs])
        for hh in range(BH):
            cs = slice(hh * HD, (hh + 1) * HD)
            dxb = p8_ref[3:4, cs] * dxce_scr[0:tm, cs]
            dxb = dxb + p8_ref[2:3, cs] * dxce_scr[1:1 + tm, cs]
            dxb = dxb + p8_ref[1:2, cs] * dxce_scr[2:2 + tm, cs]
            dxb = dxb + p8_ref[0:1, cs] * dxce_scr[3:3 + tm, cs]
            dzb_ref[:, cs] = dxb.astype(BF16)
        dxce_scr[tm:tm + SUBLANES, :] = dxce_scr[0:SUBLANES, :]

        dh1 = jnp.zeros((tm, D), F32)
        for k in range(NDEV):
            dh1 = dh1 + _dot_nt(dzb_ref[:, k * CB:(k + 1) * CB], win_ref[k])
        x1 = x1_ref[...]
        r1 = _rms(x1)
        dx1 = dx2 + _rms_bwd(dh1, x1, r1, nw_ref[...])
        dx1_ref[...] = dx1
        dx1b_ref[...] = dx1.astype(BF16)
        gnw_ref[...] += _rowsum(dh1 * x1 * r1)

    tile = lambda w: pl.BlockSpec((tm, w), lambda i: (nt - 1 - i, 0))
    prev = lambda w: pl.BlockSpec((SUBLANES, w), lambda i: (jnp.maximum((nt - 1 - i) * per - 1, 0), 0))
    whole = lambda *s: pl.BlockSpec(s, lambda i: (0,) * len(s))
    full = lambda: pltpu.VMEM((tm, BW), F32)
    ext = lambda: pltpu.VMEM((tm + SUBLANES, BW), F32)
    out, _ = _call(
        main, [], name="bwd_b", grid=(nt,),
        ins=[dx2, zb, zb, hs, hs, x1, nw, win8, p8, gcat, wout],
        in_specs=[tile(D), tile(2 * BW), prev(2 * BW), tile(BW), prev(BW), tile(D), _VMEM, _VMEM, _VMEM, _VMEM, _VMEM],
        out_shape=[_sds((s_len, D), F32), _sds((s_len, D), BF16), _sds((s_len, 2 * BW), BF16), _sds((SUBLANES, BW), F32),
                   _sds((BH, HD, HD), F32), _sds((BH, HD, HD), F32), _sds((1, D), F32)],
        out_specs=[tile(D), tile(D), tile(2 * BW), whole(SUBLANES, BW), whole(BH, HD, HD), whole(BH, HD, HD),
                   whole(1, D)],
        scratch=[ext(), ext(), ext(), full(), full(), full(), full(), full(), full(), full(), full(), full(), ext(),
                 pltpu.VMEM((1, BW), F32), pltpu.VMEM((1, BW), F32)])
    return out


def _transpose_into(dst_ref, src_ref, rows):
    s_len = src_ref.shape[0]
    for r0 in range(0, s_len, rows):
        dst_ref[:, r0:r0 + rows] = src_ref[r0:r0 + rows, :].astype(F32).T.astype(BF16)


def _wgrad_cols(a, b, jobs, *, nblk, name, relay_step=0):
    s_len, m = a.shape
    bn = b.shape[1] // nblk

    def main(i, ins, outs, scr):
        a_ref, b_ref = ins
        (o_ref,), (at_scr,) = outs, scr

        @pl.when(i == 0)
        def _():
            _transpose_into(at_scr, a_ref, 256)

        o_ref[0] = _dot(at_scr[...], b_ref[...]).astype(BF16)

    (out,), job_out = _call(
        main, jobs, name=name, grid=(nblk,), relay_step=relay_step,
        ins=[a, b], in_specs=[_VMEM, pl.BlockSpec((s_len, bn), lambda j: (0, j))],
        out_shape=[_sds((nblk, m, bn), BF16)], out_specs=[pl.BlockSpec((1, m, bn), lambda j: (j, 0, 0))],
        scratch=[pltpu.VMEM((m, s_len), BF16)])
    return out, job_out


def _wgrad_rows(a, b, jobs, *, nblk, per, name, relay_step=0):
    s_len, m = a.shape
    n = b.shape[1]
    rb = m // nblk
    bm = per * rb

    def main(i, ins, outs, scr):
        a_ref, b_ref = ins
        (o_ref,), (at_scr,) = outs, scr
        _transpose_into(at_scr, a_ref, 256)
        res = _dot(at_scr[...], b_ref[...]).astype(BF16)
        for q in range(per):
            o_ref[q] = res[q * rb:(q + 1) * rb, :]

    (out,), job_out = _call(
        main, jobs, name=name, grid=(nblk // per,), relay_step=relay_step,
        ins=[a, b], in_specs=[pl.BlockSpec((s_len, bm), lambda j: (0, j)), _VMEM],
        out_shape=[_sds((nblk, rb, n), BF16)], out_specs=[pl.BlockSpec((per, rb, n), lambda j: (j, 0, 0))],
        scratch=[pltpu.VMEM((bm, s_len), BF16)])
    return out, job_out


def _adam_math(w, g, m, v):
    m = B1 * m + (1.0 - B1) * g
    v = B2 * v + (1.0 - B2) * (g * g)
    m_hat = m / (1.0 - B1 ** STEP)
    v_hat = v / (1.0 - B2 ** STEP)
    delta = (-LR) * (m_hat / (jnp.sqrt(v_hat) + ADAM_EPS) + WD * w)
    return delta, m, v


def _adam_big(w, acc, land, m, v, name):
    r, cd = w.shape
    rb = 256 if r % 256 == 0 else r

    def body(w_ref, acc_ref, land_ref, m_ref, v_ref, g_ref, d_ref, mo_ref, vo_ref):
        g = acc_ref[...]
        for j in range(NCHIP_OTHER):
            g = g + land_ref[j].astype(F32)
        g_ref[...] = g
        d_ref[...], mo_ref[...], vo_ref[...] = _adam_math(w_ref[...], g, m_ref[...], v_ref[...])

    blk = pl.BlockSpec((rb, cd), lambda i: (i, 0))
    blk3 = pl.BlockSpec((NCHIP_OTHER, rb, cd), lambda i: (0, i, 0))
    return pl.pallas_call(
        body, name=name, grid=(r // rb,), in_specs=[blk, blk, blk3, blk, blk], out_specs=[blk] * 4,
        out_shape=[_sds((r, cd), F32)] * 4,
        compiler_params=_params(dimension_semantics=("arbitrary",)),
    )(w, acc, land, m, v)


def _adam_small(groups):
    n = len(groups)

    def body(*refs):
        ins, outs = refs[:4 * n], refs[4 * n:]
        for k in range(n):
            w_ref, g_ref, m_ref, v_ref = ins[4 * k:4 * k + 4]
            d, mo, vo = _adam_math(w_ref[...], g_ref[...], m_ref[...], v_ref[...])
            outs[3 * k][...] = d
            outs[3 * k + 1][...] = mo
            outs[3 * k + 2][...] = vo

    flat = [a for grp in groups for a in grp]
    shapes = [_sds(grp[0].shape, F32) for grp in groups for _ in range(3)]
    res = pl.pallas_call(
        body, name="adam_small", in_specs=[_VMEM] * (4 * n), out_specs=[_VMEM] * (3 * n), out_shape=shapes,
        compiler_params=_params(),
    )(*flat)
    return [tuple(res[3 * k:3 * k + 3]) for k in range(n)]


TM_FWD_A = 256
RELAY_STEP_FWD_A = 4
RELAY_STEP_FWD_B = 2
TM_BWD_A = 256
TM_BWD_A_IN = 256
TM_FWD_B = 256
TM_HEAD = 512
TM_BWD_B = 256


def _pack(parts, rows):
    flat = jnp.concatenate([p.reshape(-1) for p in parts])
    return jnp.pad(flat, (0, NDEV * rows * LANES - flat.shape[0])).reshape(NDEV, rows, LANES)


def _unpack(packed, shapes):
    flat, out, off = packed.reshape(-1), [], 0
    for s in shapes:
        size = 1
        for d in s:
            size *= d
        out.append(flat[off:off + size].reshape(s))
        off += size
    return out


def kernel(x, norm_w, a_w_in, a_ln_w, a_ln_b, a_w_s, a_b_s, a_w_out, b_w_in, b_conv_w, b_conv_b, b_gate_a_w, b_gate_a_b, b_gate_x_w, b_gate_x_b, b_lambda, b_w_out, norm_f_w, loss_target, m_norm_w, m_a_w_in, m_a_ln_w, m_a_ln_b, m_a_w_s, m_a_b_s, m_a_w_out, m_b_w_in, m_b_conv_w, m_b_conv_b, m_b_gate_a_w, m_b_gate_a_b, m_b_gate_x_w, m_b_gate_x_b, m_b_lambda, m_b_w_out, m_norm_f_w, v_norm_w, v_a_w_in, v_a_ln_w, v_a_ln_b, v_a_w_s, v_a_b_s, v_a_w_out, v_b_w_in, v_b_conv_w, v_b_conv_b, v_b_gate_a_w, v_b_gate_a_b, v_b_gate_x_w, v_b_gate_x_b, v_b_lambda, v_b_w_out, v_norm_f_w):
    me = 4 * lax.axis_index("x") + 2 * lax.axis_index("y") + lax.axis_index("c")
    xs, tgt = x[0], loss_target[0]
    nw0, nw1, nfw = norm_w[0:1], norm_w[1:2], norm_f_w.reshape(1, D)
    w_s, bst = a_w_s[0], a_b_s[0].T
    gcat = jnp.concatenate([b_gate_a_w[0], b_gate_x_w[0]], axis=-1).astype(BF16)

    p8_shard = jnp.concatenate([b_conv_w[0], b_conv_b, b_gate_a_b, b_gate_x_b, b_lambda], axis=0)
    ((win_a8, p8_all),) = _comm_only([_Gather([a_w_in[0].astype(BF16), p8_shard])], "gather_first")
    p8 = jnp.transpose(p8_all, (1, 0, 2)).reshape(SUBLANES, BW)

    (z, h0, ya), ((wout_a8, win_b8),) = _fwd_a(
        xs, nw0, win_a8, a_ln_w, a_ln_b, w_s, bst, [_Gather([a_w_out[0].astype(BF16), b_w_in[0].astype(BF16)])],
        tm=TM_FWD_A, relay_step=RELAY_STEP_FWD_A)
    wout_a = wout_a8.reshape(AW, D)
    (x1, zb, hs, h1, yb), ((wout_b8,),) = _fwd_b(
        xs, ya, wout_a, nw1, win_b8, p8, gcat, [_Gather([b_w_out[0].astype(BF16)])],
        tm=TM_FWD_B, relay_step=RELAY_STEP_FWD_B)
    wout_b = wout_b8.reshape(BW, D)
    dx2, dx2b, loss, g_nfw = _head(x1, yb, wout_b, nfw, tgt, tm=TM_HEAD)

    dx1, dx1b, dzb, g_p8, g_ga, g_gx, g_nw1 = _bwd_b(dx2, zb, hs, x1, nw1, win_b8, p8, gcat, wout_b, tm=TM_BWD_B)
    p_wout_b, _ = _wgrad_rows(yb, dx2b, [], nblk=NDEV, per=2, name="wgrad_b_out")
    p_win_b, _ = _wgrad_cols(h1, dzb, [], nblk=NDEV, name="wgrad_b_in")
    shapes_b = [(1, D), (1, D), (SUBLANES, BW), (1, 1)]
    pack_b = _pack([g_nfw, g_nw1, g_p8, loss], 16)
    qs_b, accs_b = _reduce_in_chip(
        [p_win_b, p_wout_b, g_ga.reshape(NDEV, -1, HD), g_gx.reshape(NDEV, -1, HD), pack_b], "reduce_b_in_chip")

    (dz, g_lnw, g_lnb, g_ws, g_bst), (lands_b,) = _bwd_a(
        dx1b, z, a_ln_w, a_ln_b, w_s, bst, wout_a, [_Exchange(qs_b)], tm=TM_BWD_A)
    p_wout_a, (red_b,) = _wgrad_rows(ya, dx1b, [_SumGather(accs_b[2:], lands_b[2:])], nblk=NDEV, per=1,
                                     name="wgrad_a_out", relay_step=2)
    shapes_a = [(1, AW), (1, AW), (CH, G)]
    pack_a = _pack([g_lnw, g_lnb, g_bst], 8)
    qs_a, accs_a = _reduce_in_chip([p_wout_a, g_ws, pack_a], "reduce_a_out_in_chip")
    p_win_a, (lands_a,) = _wgrad_cols(h0, dz, [_Exchange(qs_a)], nblk=NDEV, name="wgrad_a_in")
    (q_win_a,), (acc_win_a,) = _reduce_in_chip([p_win_a], "reduce_a_in_in_chip")
    (gx, g_nw0), (red_a, (l_win_a,)) = _bwd_a_in(
        dz, dx1, xs, nw0, win_a8, [_SumGather(accs_a[1:], lands_a[1:]), _Exchange([q_win_a])],
        tm=TM_BWD_A_IN, relay_step=2)
    g_nw0 = _allreduce_direct(g_nw0, "allreduce_norm_w0")

    r_ga, r_gx, r_pack_b = red_b
    r_nfw, r_nw1, r_p8, loss = _unpack(r_pack_b, shapes_b)
    r_ws, r_pack_a = red_a
    r_lnw, r_lnb, r_bst = _unpack(r_pack_a, shapes_a)
    g_p8 = lax.dynamic_slice_in_dim(r_p8, me * (BW // NDEV), BW // NDEV, axis=1)
    loss = loss[0, 0]

    weights = dict(norm_w=norm_w, a_w_in=a_w_in, a_ln_w=a_ln_w, a_ln_b=a_ln_b, a_w_s=a_w_s, a_b_s=a_b_s, a_w_out=a_w_out,
                   b_w_in=b_w_in, b_conv_w=b_conv_w, b_conv_b=b_conv_b, b_gate_a_w=b_gate_a_w, b_gate_a_b=b_gate_a_b,
                   b_gate_x_w=b_gate_x_w, b_gate_x_b=b_gate_x_b, b_lambda=b_lambda, b_w_out=b_w_out, norm_f_w=norm_f_w)
    mom1 = dict(norm_w=m_norm_w, a_w_in=m_a_w_in, a_ln_w=m_a_ln_w, a_ln_b=m_a_ln_b, a_w_s=m_a_w_s, a_b_s=m_a_b_s,
                a_w_out=m_a_w_out, b_w_in=m_b_w_in, b_conv_w=m_b_conv_w, b_conv_b=m_b_conv_b, b_gate_a_w=m_b_gate_a_w,
                b_gate_a_b=m_b_gate_a_b, b_gate_x_w=m_b_gate_x_w, b_gate_x_b=m_b_gate_x_b, b_lambda=m_b_lambda,
                b_w_out=m_b_w_out, norm_f_w=m_norm_f_w)
    mom2 = dict(norm_w=v_norm_w, a_w_in=v_a_w_in, a_ln_w=v_a_ln_w, a_ln_b=v_a_ln_b, a_w_s=v_a_w_s, a_b_s=v_a_b_s,
                a_w_out=v_a_w_out, b_w_in=v_b_w_in, b_conv_w=v_b_conv_w, b_conv_b=v_b_conv_b, b_gate_a_w=v_b_gate_a_w,
                b_gate_a_b=v_b_gate_a_b, b_gate_x_w=v_b_gate_x_w, b_gate_x_b=v_b_gate_x_b, b_lambda=v_b_lambda,
                b_w_out=v_b_w_out, norm_f_w=v_norm_f_w)
    names = list(weights)

    def as2d(a):
        return a.reshape(-1, a.shape[-1])

    upd, grads = {}, {}
    for k, acc, land in (("a_w_in", acc_win_a, l_win_a), ("a_w_out", accs_a[0], lands_a[0]),
                         ("b_w_in", accs_b[0], lands_b[0]), ("b_w_out", accs_b[1], lands_b[1])):
        g, d, mo, vo = _adam_big(as2d(weights[k]), acc, land, as2d(mom1[k]), as2d(mom2[k]), "adam_" + k)
        grads[k] = g[None]
        upd[k] = (d, mo, vo)
    grads.update(
        norm_w=jnp.concatenate([g_nw0, r_nw1], axis=0), a_ln_w=r_lnw, a_ln_b=r_lnb,
        a_w_s=r_ws.reshape(1, G, CH, CH), a_b_s=r_bst.T[None],
        b_conv_w=g_p8[None, 0:4], b_conv_b=g_p8[4:5], b_gate_a_w=r_ga.reshape(1, BH, HD, HD), b_gate_a_b=g_p8[5:6],
        b_gate_x_w=r_gx.reshape(1, BH, HD, HD), b_gate_x_b=g_p8[6:7], b_lambda=g_p8[7:8], norm_f_w=r_nfw.reshape(D))
    small_names = [k for k in names if k not in upd]
    res = _adam_small([(as2d(weights[k]), as2d(grads[k]), as2d(mom1[k]), as2d(mom2[k])) for k in small_names])
    for k, r3 in zip(small_names, res):
        upd[k] = r3
    deltas = [upd[k][0].reshape(weights[k].shape) for k in names]
    new_m = [upd[k][1].reshape(weights[k].shape) for k in names]
    new_v = [upd[k][2].reshape(weights[k].shape) for k in names]
    return (loss, gx[None], *[grads[k] for k in names], *deltas, *new_m, *new_v)
```

```python
import jax
import jax.numpy as jnp
from jax import lax
from jax.experimental import pallas as pl
from jax.experimental.pallas import tpu as pltpu

F32 = jnp.float32
BF16 = jnp.bfloat16
MESH = pl.DeviceIdType.MESH

NDEV = 8
NCHIP_OTHER = 3
D = 1024
AW = 2048
G = 8
GD = AW // G
CH = 128
BW = 1536
BH = 12
HD = BW // BH
CA = 3 * AW // NDEV
CB = 2 * BW // NDEV
RMS_EPS = 1e-6
LN_EPS = 1e-5
RG_C = 8.0
LR, B1, B2, ADAM_EPS, WD, STEP = 0.001, 0.9, 0.999, 1e-08, 0.01, 10
V7X_VMEM_BYTES = 64 * 1024 * 1024
VMEM_LIMIT = V7X_VMEM_BYTES - 8 * 1024 * 1024
SUBLANES = 8
LANES = 128
BF16_ROWS = 16
GELU_C = 0.7978845608028654
GELU_K = 0.044715

_VMEM = pl.BlockSpec(memory_space=pltpu.VMEM)
_HBM = pl.BlockSpec(memory_space=pltpu.HBM)


def _sds(shape, dtype):
    return jax.ShapeDtypeStruct(tuple(shape), dtype)


def _params(**kw):
    return pltpu.CompilerParams(vmem_limit_bytes=VMEM_LIMIT, **kw)


def _gelu_t(z):
    t = jnp.tanh(GELU_C * (z + GELU_K * (z * z * z)))
    return 0.5 * z * (1.0 + t), t


def _dgelu(z, t):
    return 0.5 * (1.0 + t) + 0.5 * z * (1.0 - t * t) * (GELU_C * (1.0 + 3.0 * GELU_K * z * z))


def _sigmoid(v):
    return 0.5 * jnp.tanh(0.5 * v) + 0.5


def _softplus_neg(lam):
    return jnp.maximum(-lam, 0.0) + jnp.log1p(jnp.exp(-jnp.abs(lam)))


def _dot(a, b):
    return jnp.dot(a, b, preferred_element_type=F32)


def _dot_nt(a, b):
    return lax.dot_general(a, b, (((1,), (1,)), ((), ())), preferred_element_type=F32)


def _rowsum(v):
    return jnp.sum(v, axis=0, keepdims=True)


def _causal_mask():
    r = lax.broadcasted_iota(jnp.int32, (CH, CH), 0)
    c = lax.broadcasted_iota(jnp.int32, (CH, CH), 1)
    return r >= c


def _rms(x):
    return lax.rsqrt(jnp.mean(x * x, axis=-1, keepdims=True) + RMS_EPS)


def _rms_bwd(dh, x, r, nw):
    gy = dh * nw
    return r * gy - x * (r * r * r) * jnp.mean(gy * x, axis=-1, keepdims=True)


def _place():
    return lax.axis_index("x"), lax.axis_index("y"), lax.axis_index("c")


def _other_chips(x, y):
    return [(1 - x, y), (x, 1 - y), (1 - x, 1 - y)]


GATHER_SLOTS = 10


def _gather_ops(ins, outs, send_sems, recv_sems, local_sems):
    n = len(ins)
    x, y, c = _place()
    sibling = (x, y, 1 - c)
    xn, yn, dg = _other_chips(x, y)
    split = [ins[i].shape[0] % (2 * BF16_ROWS) == 0 for i in range(n)]

    def blk(chip, core):
        return 4 * chip[0] + 2 * chip[1] + core

    me = blk((x, y), c)

    def part(ref, i, half):
        if half is None:
            return ref
        h = ins[i].shape[0] // 2
        return ref.at[pl.ds(half * h, h)]

    def copy(i, k, block, to, half=None, src=None):
        dst = part(outs[i].at[block], i, half)
        return pltpu.make_async_remote_copy(
            src_ref=dst if src is None else part(src, i, half), dst_ref=dst,
            send_sem=send_sems.at[k, i], recv_sem=recv_sems.at[k, i], device_id=to, device_id_type=MESH)

    def first_copies():
        mine = [pltpu.make_async_copy(ins[i], outs[i].at[me], local_sems.at[i]) for i in range(n)]
        first = []
        for i in range(n):
            first.append(copy(i, 0, me, sibling, src=ins[i]))
            if split[i]:
                first.append(copy(i, 1, me, (*xn, c), 0, ins[i]))
                first.append(copy(i, 3, me, (*yn, c), 1, ins[i]))
                first.append(copy(i, 2, me, (*xn, c), 1, ins[i]))
                first.append(copy(i, 4, me, (*yn, c), 0, ins[i]))
            else:
                first.append(copy(i, 1, me, (*xn, c), None, ins[i]))
                first.append(copy(i, 3, me, (*yn, c), None, ins[i]))
                first.append(copy(i, 5, me, (*dg, c), None, ins[i]))
        return mine, first

    def onward():
        out = []
        for i in range(n):
            if split[i]:
                out.append(copy(i, 5, blk(xn, c), (*yn, c), 0))
                out.append(copy(i, 6, blk(yn, c), (*xn, c), 1))
        return out

    def start():
        mine, first = first_copies()
        for cp in mine + first:
            cp.start()

    def relay():
        sends = onward()
        for i in range(n):
            if split[i]:
                copy(i, 1, blk(xn, c), sibling, 0).wait_recv()
                sends.pop(0).start()
                copy(i, 3, blk(yn, c), sibling, 1).wait_recv()
                sends.pop(0).start()

    def finish():
        mine, first = first_copies()
        passed = []

        def pass_on(i, j, chip):
            fwd = copy(i, 7 + j, blk(chip, c), sibling)
            fwd.start()
            passed.append(fwd)

        for i in range(n):
            if split[i]:
                copy(i, 2, blk(xn, c), sibling, 1).wait_recv()
                pass_on(i, 0, xn)
                copy(i, 4, blk(yn, c), sibling, 0).wait_recv()
                pass_on(i, 1, yn)
                copy(i, 5, blk(dg, c), sibling, 0).wait_recv()
                copy(i, 6, blk(dg, c), sibling, 1).wait_recv()
                pass_on(i, 2, dg)
            else:
                copy(i, 1, blk(xn, c), sibling).wait_recv()
                pass_on(i, 0, xn)
                copy(i, 3, blk(yn, c), sibling).wait_recv()
                pass_on(i, 1, yn)
                copy(i, 5, blk(dg, c), sibling).wait_recv()
                pass_on(i, 2, dg)
        for i in range(n):
            copy(i, 0, blk((x, y), 1 - c), sibling).wait_recv()
            for j, chip in enumerate((xn, yn, dg)):
                copy(i, 7 + j, blk(chip, 1 - c), sibling).wait_recv()
        for cp in first + passed + onward():
            cp.wait_send()
        for cp in mine:
            cp.wait()

    return start, relay, finish


def _gather_sems(n):
    return [pltpu.SemaphoreType.DMA((GATHER_SLOTS, n)), pltpu.SemaphoreType.DMA((GATHER_SLOTS, n)),
            pltpu.SemaphoreType.DMA((n,))]


class _Gather:
    def __init__(self, shards):
        n = len(shards)
        self.ins, self.in_specs = list(shards), [_HBM] * n
        self.out_shape = [_sds((NDEV,) + s.shape, s.dtype) for s in shards]
        self.out_specs = [_HBM] * n
        self.scratch = _gather_sems(n)

    def ops(self, ins, outs, scr):
        return _gather_ops(ins, outs, *scr)


class _Exchange:
    def __init__(self, qs):
        n = len(qs)
        self.ins, self.in_specs = list(qs), [_HBM] * n
        self.out_shape = [_sds(q.shape, q.dtype) for q in qs]
        self.out_specs = [_HBM] * n
        self.scratch = [pltpu.SemaphoreType.DMA((NCHIP_OTHER, n)), pltpu.SemaphoreType.DMA((NCHIP_OTHER, n))]

    def ops(self, ins, outs, scr):
        send_sems, recv_sems = scr
        n = len(ins)
        x, y, c = _place()
        chips = _other_chips(x, y)

        def copies():
            return [pltpu.make_async_remote_copy(
                src_ref=ins[i].at[j], dst_ref=outs[i].at[j], send_sem=send_sems.at[j, i],
                recv_sem=recv_sems.at[j, i], device_id=(*chips[j], c), device_id_type=MESH)
                for i in range(n) for j in range(NCHIP_OTHER)]

        def start():
            for cp in copies():
                cp.start()

        def finish():
            cps = copies()
            for cp in cps:
                cp.wait_recv()
            for cp in cps:
                cp.wait_send()

        return start, lambda: None, finish


class _SumGather:
    def __init__(self, accs, lands):
        n = len(accs)
        self.n = n
        self.ins, self.in_specs = list(accs) + list(lands), [_VMEM] * (2 * n)
        self.out_shape = [_sds((NDEV,) + a.shape, a.dtype) for a in accs]
        self.out_specs = [_HBM] * n
        self.scratch = [pltpu.VMEM(a.shape, a.dtype) for a in accs] + _gather_sems(n)

    def ops(self, ins, outs, scr):
        n = self.n
        accs, lands, mine = ins[:n], ins[n:], scr[:n]
        g_start, relay, finish = _gather_ops(mine, outs, *scr[n:])

        def start():
            for i in range(n):
                mine[i][...] = accs[i][...] + lands[i][0] + lands[i][1] + lands[i][2]
            g_start()

        return start, relay, finish


def _call(main, jobs, *, name, grid, ins, in_specs, out_shape, out_specs, scratch, relay_step=0):
    nsteps = grid[0] if grid else 1
    n_in, n_out, n_scr = len(ins), len(out_shape), len(scratch)

    def body(*refs):
        pos = [0]

        def take(k):
            r = refs[pos[0]:pos[0] + k]
            pos[0] += k
            return r

        m_in = take(n_in)
        j_in = [take(len(j.ins)) for j in jobs]
        m_out = take(n_out)
        j_out = [take(len(j.out_shape)) for j in jobs]
        m_scr = take(n_scr)
        j_scr = [take(len(j.scratch)) for j in jobs]
        ops = [j.ops(a, b, s) for j, a, b, s in zip(jobs, j_in, j_out, j_scr)]
        i = pl.program_id(0) if grid else 0
        if not grid:
            for o in ops:
                o[0]()
            main(i, m_in, m_out, m_scr)
            for o in ops:
                o[1]()
            for o in ops:
                o[2]()
            return

        if ops:
            @pl.when(i == 0)
            def _():
                for o in ops:
                    o[0]()

        main(i, m_in, m_out, m_scr)

        if ops:
            @pl.when(i == min(relay_step, nsteps - 1))
            def _():
                for o in ops:
                    o[1]()

            @pl.when(i == nsteps - 1)
            def _():
                for o in ops:
                    o[2]()

    extra = dict(dimension_semantics=("arbitrary",)) if grid else {}
    res = pl.pallas_call(
        body, name=name, grid=grid,
        in_specs=list(in_specs) + [s for j in jobs for s in j.in_specs],
        out_specs=list(out_specs) + [s for j in jobs for s in j.out_specs],
        out_shape=list(out_shape) + [s for j in jobs for s in j.out_shape],
        scratch_shapes=list(scratch) + [s for j in jobs for s in j.scratch],
        compiler_params=_params(**extra),
    )(*ins, *[a for j in jobs for a in j.ins])
    main_out, rest, job_out = res[:n_out], res[n_out:], []
    for j in jobs:
        k = len(j.out_shape)
        job_out.append(rest[:k])
        rest = rest[k:]
    return main_out, job_out


def _comm_only(jobs, name):
    _, job_out = _call(lambda i, a, b, s: None, jobs, name=name, grid=(), ins=[], in_specs=[], out_shape=[],
                       out_specs=[], scratch=[])
    return job_out


class _InChip:
    def __init__(self, ps):
        n = len(ps)
        self.n = n
        blk = [p.shape[1:] for p in ps]
        self.ins, self.in_specs = list(ps), [_HBM] * n
        self.out_shape = [_sds((NCHIP_OTHER,) + b, p.dtype) for b, p in zip(blk, ps)] + [_sds(b, F32) for b in blk]
        self.out_specs = [_VMEM] * (2 * n)
        self.scratch = ([pltpu.VMEM((4,) + b, p.dtype) for b, p in zip(blk, ps)] * 2
                        + [pltpu.SemaphoreType.DMA((4, n))] * 3)

    def ops(self, ins, outs, scr):
        n = self.n
        q_refs, acc_refs = outs[:n], outs[n:]
        mines, lands = scr[:n], scr[n:2 * n]
        send_sems, recv_sems, local_sems = scr[2 * n:]
        x, y, c = _place()
        sibling = (x, y, 1 - c)

        def copies():
            out = []
            for i in range(n):
                for pi in range(4):
                    loc = pltpu.make_async_copy(ins[i].at[2 * pi + c], mines[i].at[pi], local_sems.at[pi, i])
                    cp = pltpu.make_async_remote_copy(
                        src_ref=ins[i].at[2 * pi + (1 - c)], dst_ref=lands[i].at[pi],
                        send_sem=send_sems.at[pi, i], recv_sem=recv_sems.at[pi, i],
                        device_id=sibling, device_id_type=MESH)
                    out.append((loc, cp))
            return out

        def start():
            for loc, cp in copies():
                loc.start()
                cp.start()

        def finish():
            pairs = copies()
            for loc, cp in pairs:
                loc.wait()
                cp.wait_recv()
            for i in range(n):
                _chip_sums(mines[i], lands[i], q_refs[i], acc_refs[i], x, y)
            for _, cp in pairs:
                cp.wait_send()

        return start, lambda: None, finish


def _chip_sums(mine, land, q_ref, acc_ref, x, y):
    for j, (qx, qy) in enumerate(_other_chips(x, y)):
        qi = 2 * qx + qy
        q_ref[j] = (mine[qi].astype(F32) + land[qi].astype(F32)).astype(q_ref.dtype)
    mi = 2 * x + y
    acc_ref[...] = mine[mi].astype(F32) + land[mi].astype(F32)


def _allreduce_direct(v, name):
    def body(v_ref, o_ref, buf, send_sems, recv_sems):
        x, y, c = _place()
        me = 4 * x + 2 * y + c
        buf[me] = v_ref[...]
        cps = []
        for k in range(1, NDEV):
            fx, fy, fc = (k >> 2) & 1, (k >> 1) & 1, k & 1
            peer = ((1 - x) if fx else x, (1 - y) if fy else y, (1 - c) if fc else c)
            cps.append((peer, pltpu.make_async_remote_copy(
                src_ref=buf.at[me], dst_ref=buf.at[me], send_sem=send_sems.at[k - 1], recv_sem=recv_sems.at[k - 1],
                device_id=peer, device_id_type=MESH)))
        for _, cp in cps:
            cp.start()
        for k, (peer, _) in enumerate(cps):
            theirs = 4 * peer[0] + 2 * peer[1] + peer[2]
            pltpu.make_async_remote_copy(
                src_ref=buf.at[theirs], dst_ref=buf.at[theirs], send_sem=send_sems.at[k], recv_sem=recv_sems.at[k],
                device_id=peer, device_id_type=MESH).wait_recv()
        acc = buf[0]
        for j in range(1, NDEV):
            acc = acc + buf[j]
        o_ref[...] = acc
        for _, cp in cps:
            cp.wait_send()

    return pl.pallas_call(
        body, name=name, in_specs=[_VMEM], out_specs=_VMEM, out_shape=_sds(v.shape, v.dtype),
        scratch_shapes=[pltpu.VMEM((NDEV,) + v.shape, v.dtype), pltpu.SemaphoreType.DMA((NDEV - 1,)),
                        pltpu.SemaphoreType.DMA((NDEV - 1,))],
        compiler_params=_params(),
    )(v)


def _fwd_a(x, nw, win8, lnw, lnb, ws, bst, jobs, *, tm, relay_step):
    s_len = x.shape[0]
    nt = s_len // tm
    nch = tm // CH

    def main(i, ins, outs, scr):
        x_ref, nw_ref, win_ref, lnw_ref, lnb_ref, ws_ref, bst_ref = ins
        z_ref, h_ref, y_ref = outs
        wc_scr, gv_scr = scr

        @pl.when(i == 0)
        def _():
            m = _causal_mask()
            for g in range(G):
                wc_scr[g] = jnp.where(m, ws_ref[g], 0.0).astype(BF16)

        x = x_ref[...]
        h = (x * _rms(x) * nw_ref[...]).astype(BF16)
        h_ref[...] = h
        for k in range(NDEV):
            z_ref[:, k * CA:(k + 1) * CA] = _dot(h, win_ref[k])

        ssum = jnp.zeros((tm, 1), F32)
        for g in range(G):
            gv = _gelu_t(z_ref[:, AW + g * GD:AW + (g + 1) * GD])[0]
            gv_scr[:, g * GD:(g + 1) * GD] = gv
            ssum = ssum + jnp.sum(gv, axis=-1, keepdims=True)
        mu = ssum * (1.0 / AW)
        vsum = jnp.zeros((tm, 1), F32)
        for g in range(G):
            dlt = gv_scr[:, g * GD:(g + 1) * GD] - mu
            vsum = vsum + jnp.sum(dlt * dlt, axis=-1, keepdims=True)
        rstd = lax.rsqrt(vsum * (1.0 / AW) + LN_EPS)

        for g in range(G):
            cs = slice(g * GD, (g + 1) * GD)
            v = (gv_scr[:, cs] - mu) * rstd * lnw_ref[:, cs] + lnb_ref[:, cs]
            vb = v.astype(BF16)
            u = _gelu_t(z_ref[:, cs])[0]
            zg = z_ref[:, 2 * AW + g * GD:2 * AW + (g + 1) * GD]
            sg = zg * _sigmoid(zg)
            for n in range(nch):
                rs = slice(n * CH, (n + 1) * CH)
                s = _dot(wc_scr[g], vb[rs, :]) + bst_ref[:, g:g + 1]
                y_ref[rs, cs] = (u[rs, :] * s * sg[rs, :]).astype(BF16)

    tile = lambda w: pl.BlockSpec((tm, w), lambda i: (i, 0))
    return _call(
        main, jobs, name="fwd_a", grid=(nt,), relay_step=relay_step,
        ins=[x, nw, win8, lnw, lnb, ws, bst], in_specs=[tile(D), _VMEM, _VMEM, _VMEM, _VMEM, _VMEM, _VMEM],
        out_shape=[_sds((s_len, 3 * AW), F32), _sds((s_len, D), BF16), _sds((s_len, AW), BF16)],
        out_specs=[tile(3 * AW), tile(D), tile(AW)],
        scratch=[pltpu.VMEM((G, CH, CH), BF16), pltpu.VMEM((tm, AW), F32)])


def _bwd_a(dx1, z, lnw, lnb, ws, bst, wout, jobs, *, tm):
    s_len = dx1.shape[0]
    nt = s_len // tm
    nch = tm // CH

    def main(i, ins, outs, scr):
        dx1_ref, z_ref, lnw_ref, lnb_ref, ws_ref, bst_ref, wout_ref = ins
        dz_ref, glnw_ref, glnb_ref, gws_ref, gbst_ref = outs
        wc_scr, wct_scr, vh_scr, dgv_scr, dy_scr, dv_scr, gbs_acc, gwc_acc = scr

        @pl.when(i == 0)
        def _():
            m = _causal_mask()
            for g in range(G):
                wm = jnp.where(m, ws_ref[g], 0.0)
                wc_scr[g] = wm.astype(BF16)
                wct_scr[g] = wm.T.astype(BF16)
            glnw_ref[...] = jnp.zeros_like(glnw_ref)
            glnb_ref[...] = jnp.zeros_like(glnb_ref)
            gbs_acc[...] = jnp.zeros_like(gbs_acc)
            gwc_acc[...] = jnp.zeros_like(gwc_acc)

        dy_scr[...] = _dot_nt(dx1_ref[...], wout_ref[...])

        ssum = jnp.zeros((tm, 1), F32)
        for g in range(G):
            cs = slice(g * GD, (g + 1) * GD)
            zv = z_ref[:, AW + g * GD:AW + (g + 1) * GD]
            gv, t = _gelu_t(zv)
            vh_scr[:, cs] = gv
            dgv_scr[:, cs] = _dgelu(zv, t)
            ssum = ssum + jnp.sum(gv, axis=-1, keepdims=True)
        mu = ssum * (1.0 / AW)
        vsum = jnp.zeros((tm, 1), F32)
        for g in range(G):
            dlt = vh_scr[:, g * GD:(g + 1) * GD] - mu
            vsum = vsum + jnp.sum(dlt * dlt, axis=-1, keepdims=True)
        rstd = lax.rsqrt(vsum * (1.0 / AW) + LN_EPS)

        m1 = jnp.zeros((tm, 1), F32)
        m2 = jnp.zeros((tm, 1), F32)
        for g in range(G):
            cs = slice(g * GD, (g + 1) * GD)
            gs = slice(2 * AW + g * GD, 2 * AW + (g + 1) * GD)
            vhat = (vh_scr[:, cs] - mu) * rstd
            vh_scr[:, cs] = vhat
            vb = (vhat * lnw_ref[:, cs] + lnb_ref[:, cs]).astype(BF16)
            zu = z_ref[:, cs]
            u, tu = _gelu_t(zu)
            zg = z_ref[:, gs]
            sig = _sigmoid(zg)
            sg = zg * sig
            dy = dy_scr[:, cs]
            dsf = dy * u * sg
            dsb = dsf.astype(BF16)
            dvs = []
            for n in range(nch):
                rs = slice(n * CH, (n + 1) * CH)
                s = _dot(wc_scr[g], vb[rs, :]) + bst_ref[:, g:g + 1]
                dys = dy[rs, :] * s
                dz_ref[rs, cs] = (dys * sg[rs, :] * _dgelu(zu[rs, :], tu[rs, :])).astype(BF16)
                dz_ref[rs, gs] = (dys * u[rs, :] * (sig[rs, :] * (1.0 + zg[rs, :] * (1.0 - sig[rs, :])))).astype(BF16)
                gbs_acc[g] += dsf[rs, :]
                gwc_acc[g] += _dot_nt(dsb[rs, :], vb[rs, :])
                dvs.append(_dot(wct_scr[g], dsb[rs, :]))
            dv = jnp.concatenate(dvs, axis=0) if nch > 1 else dvs[0]
            glnw_ref[:, cs] += _rowsum(dv * vhat)
            glnb_ref[:, cs] += _rowsum(dv)
            dvh = dv * lnw_ref[:, cs]
            dv_scr[:, cs] = dvh
            m1 = m1 + jnp.sum(dvh, axis=-1, keepdims=True)
            m2 = m2 + jnp.sum(dvh * vhat, axis=-1, keepdims=True)
        m1 = m1 * (1.0 / AW)
        m2 = m2 * (1.0 / AW)
        for g in range(G):
            cs = slice(g * GD, (g + 1) * GD)
            dgv = rstd * (dv_scr[:, cs] - m1 - vh_scr[:, cs] * m2)
            dz_ref[:, AW + g * GD:AW + (g + 1) * GD] = (dgv * dgv_scr[:, cs]).astype(BF16)

        @pl.when(i == nt - 1)
        def _():
            m = _causal_mask()
            for g in range(G):
                gws_ref[g] = jnp.where(m, gwc_acc[g], 0.0)
                gbst_ref[:, g:g + 1] = jnp.sum(gbs_acc[g], axis=-1, keepdims=True)

    tile = lambda w: pl.BlockSpec((tm, w), lambda i: (i, 0))
    whole = lambda *s: pl.BlockSpec(s, lambda i: (0,) * len(s))
    big = lambda dt: pltpu.VMEM((tm, AW), dt)
    return _call(
        main, jobs, name="bwd_a", grid=(nt,),
        ins=[dx1, z, lnw, lnb, ws, bst, wout], in_specs=[tile(D), tile(3 * AW), _VMEM, _VMEM, _VMEM, _VMEM, _VMEM],
        out_shape=[_sds((s_len, 3 * AW), BF16), _sds((1, AW), F32), _sds((1, AW), F32), _sds((G, CH, CH), F32),
                   _sds((CH, G), F32)],
        out_specs=[tile(3 * AW), whole(1, AW), whole(1, AW), whole(G, CH, CH), whole(CH, G)],
        scratch=[pltpu.VMEM((G, CH, CH), BF16), pltpu.VMEM((G, CH, CH), BF16), big(F32), big(F32), big(F32), big(F32),
                 pltpu.VMEM((G, CH, GD), F32), pltpu.VMEM((G, CH, CH), F32)])


def _bwd_a_in(dz, dx1, x, nw, win8, jobs, *, tm, relay_step):
    s_len = x.shape[0]
    nt = s_len // tm

    def main(i, ins, outs, scr):
        dz_ref, dx1_ref, x_ref, nw_ref, win_ref = ins
        gx_ref, gnw_ref = outs

        @pl.when(i == 0)
        def _():
            gnw_ref[...] = jnp.zeros_like(gnw_ref)

        dh = jnp.zeros((tm, D), F32)
        for k in range(NDEV):
            dh = dh + _dot_nt(dz_ref[:, k * CA:(k + 1) * CA], win_ref[k])
        x = x_ref[...]
        r = _rms(x)
        gx_ref[...] = dx1_ref[...] + _rms_bwd(dh, x, r, nw_ref[...])
        gnw_ref[...] += _rowsum(dh * x * r)

    tile = lambda w: pl.BlockSpec((tm, w), lambda i: (i, 0))
    return _call(
        main, jobs, name="bwd_a_in", grid=(nt,), relay_step=relay_step,
        ins=[dz, dx1, x, nw, win8], in_specs=[tile(3 * AW), tile(D), tile(D), _VMEM, _VMEM],
        out_shape=[_sds((s_len, D), F32), _sds((1, D), F32)],
        out_specs=[tile(D), pl.BlockSpec((1, D), lambda i: (0, 0))], scratch=[])


def _conv(p8_ref, cs, xb, xm1, xm2, xm3):
    xc = p8_ref[4:5, cs] + p8_ref[3:4, cs] * xb
    xc = xc + p8_ref[0:1, cs] * xm3
    xc = xc + p8_ref[1:2, cs] * xm2
    return xc + p8_ref[2:3, cs] * xm1


def _gates(p8_ref, gcat_ref, hh, xc):
    cs = slice(hh * HD, (hh + 1) * HD)
    pre = _dot(xc.astype(BF16), gcat_ref[hh])
    r = _sigmoid(pre[:, :HD] + p8_ref[5:6, cs])
    ig = _sigmoid(pre[:, HD:] + p8_ref[6:7, cs])
    sp = _softplus_neg(p8_ref[7:8, cs])
    la = (-RG_C) * r * sp
    a = jnp.exp(la)
    m2 = jnp.tanh(-la) * (1.0 + a * a)
    rm = lax.rsqrt(m2)
    mult = jnp.where(m2 > 0.0, m2 * rm, 0.0)
    return r, ig, sp, a, mult, rm


def _scan_rows(a_ref, b_ref, out_ref, carry, tm, reverse):
    row = lax.broadcasted_iota(jnp.int32, (SUBLANES, BW), 0)
    ngrp = tm // SUBLANES

    def step(j, cr):
        jj = (ngrp - 1 - j) if reverse else j
        off = pl.multiple_of(jj * SUBLANES, SUBLANES)
        a = a_ref[pl.ds(off, SUBLANES), :]
        b = b_ref[pl.ds(off, SUBLANES), :]
        for sh in (1, 2, 4):
            if reverse:
                a_s = pltpu.roll(a, SUBLANES - sh, 0)
                b_s = pltpu.roll(b, SUBLANES - sh, 0)
                m = row < SUBLANES - sh
            else:
                a_s = pltpu.roll(a, sh, 0)
                b_s = pltpu.roll(b, sh, 0)
                m = row >= sh
            b = jnp.where(m, a * b_s + b, b)
            a = jnp.where(m, a * a_s, a)
        o = b + a * cr
        out_ref[pl.ds(off, SUBLANES), :] = o
        return o[0:1, :] if reverse else o[SUBLANES - 1:SUBLANES, :]

    return lax.fori_loop(0, ngrp, step, carry)


def _fwd_b(x, ya, wout_a, nw, win8, p8, gcat, jobs, *, tm, relay_step):
    s_len = x.shape[0]
    nt = s_len // tm

    def main(i, ins, outs, scr):
        x_ref, ya_ref, wouta_ref, nw_ref, win_ref, p8_ref, gcat_ref = ins
        x1_ref, zb_ref, hs_ref, h1_ref, yb_ref = outs
        xbe_scr, a_scr, b_scr, carry_scr = scr

        @pl.when(i == 0)
        def _():
            xbe_scr[0:SUBLANES, :] = jnp.zeros((SUBLANES, BW), F32)
            carry_scr[...] = jnp.zeros_like(carry_scr)

        x1 = x_ref[...] + _dot(ya_ref[...], wouta_ref[...])
        x1_ref[...] = x1
        h = (x1 * _rms(x1) * nw_ref[...]).astype(BF16)
        h1_ref[...] = h
        for k in range(NDEV):
            zb_ref[:, k * CB:(k + 1) * CB] = _dot(h, win_ref[k])
        xbe_scr[SUBLANES:SUBLANES + tm, :] = zb_ref[:, :BW]
        for hh in range(BH):
            cs = slice(hh * HD, (hh + 1) * HD)
            xc = _conv(p8_ref, cs, xbe_scr[SUBLANES:SUBLANES + tm, cs], xbe_scr[7:7 + tm, cs],
                       xbe_scr[6:6 + tm, cs], xbe_scr[5:5 + tm, cs])
            _, ig, _, a, mult, _ = _gates(p8_ref, gcat_ref, hh, xc)
            a_scr[:, cs] = a
            b_scr[:, cs] = mult * (ig * xc)
        xbe_scr[0:SUBLANES, :] = xbe_scr[tm:tm + SUBLANES, :]
        carry_scr[...] = _scan_rows(a_scr, b_scr, hs_ref, carry_scr[...], tm, False)
        for hh in range(BH):
            cs = slice(hh * HD, (hh + 1) * HD)
            gt = zb_ref[:, BW + hh * HD:BW + (hh + 1) * HD]
            yb_ref[:, cs] = (hs_ref[:, cs] * (gt * _sigmoid(gt))).astype(BF16)

    tile = lambda w: pl.BlockSpec((tm, w), lambda i: (i, 0))
    return _call(
        main, jobs, name="fwd_b", grid=(nt,), relay_step=relay_step,
        ins=[x, ya, wout_a, nw, win8, p8, gcat], in_specs=[tile(D), tile(AW), _VMEM, _VMEM, _VMEM, _VMEM, _VMEM],
        out_shape=[_sds((s_len, D), F32), _sds((s_len, 2 * BW), F32), _sds((s_len, BW), F32), _sds((s_len, D), BF16),
                   _sds((s_len, BW), BF16)],
        out_specs=[tile(D), tile(2 * BW), tile(BW), tile(D), tile(BW)],
        scratch=[pltpu.VMEM((tm + SUBLANES, BW), F32), pltpu.VMEM((tm, BW), F32), pltpu.VMEM((tm, BW), F32),
                 pltpu.VMEM((1, BW), F32)])


def _head(x1, yb, wout, nfw, tgt, *, tm):
    s_len = x1.shape[0]

    def main(i, ins, outs, scr):
        x1_ref, yb_ref, wout_ref, nfw_ref, t_ref = ins
        dx2_ref, dx2b_ref, loss_ref, gnfw_ref = outs

        @pl.when(i == 0)
        def _():
            loss_ref[...] = jnp.zeros_like(loss_ref)
            gnfw_ref[...] = jnp.zeros_like(gnfw_ref)

        x2 = x1_ref[...] + _dot(yb_ref[...], wout_ref[...])
        rf = _rms(x2)
        xn = x2 * rf
        e = xn * nfw_ref[...] - t_ref[...]
        loss_ref[...] += (0.5 / D) * jnp.sum(jnp.sum(e * e, axis=-1, keepdims=True), axis=0, keepdims=True)
        dyf = e * (1.0 / D)
        gnfw_ref[...] += _rowsum(dyf * xn)
        dx2 = _rms_bwd(dyf, x2, rf, nfw_ref[...])
        dx2_ref[...] = dx2
        dx2b_ref[...] = dx2.astype(BF16)

    tile = lambda w: pl.BlockSpec((tm, w), lambda i: (i, 0))
    whole = lambda *s: pl.BlockSpec(s, lambda i: (0,) * len(s))
    (dx2, dx2b, loss, gnfw), _ = _call(
        main, [], name="head", grid=(s_len // tm,),
        ins=[x1, yb, wout, nfw, tgt], in_specs=[tile(D), tile(BW), _VMEM, _VMEM, tile(D)],
        out_shape=[_sds((s_len, D), F32), _sds((s_len, D), BF16), _sds((1, 1), F32), _sds((1, D), F32)],
        out_specs=[tile(D), tile(D), whole(1, 1), whole(1, D)], scratch=[])
    return dx2, dx2b, loss, gnfw


def _bwd_b(dx2, zb, hs, x1, nw, win8, p8, gcat, wout, *, tm):
    s_len = x1.shape[0]
    nt = s_len // tm
    per = tm // SUBLANES

    def main(i, ins, outs, scr):
        dx2_ref, zb_ref, zbp_ref, hs_ref, hsp_ref, x1_ref, nw_ref, win_ref, p8_ref, gcat_ref, wout_ref = ins
        dx1_ref, dx1b_ref, dzb_ref, gp8_ref, gga_ref, ggx_ref, gnw_ref = outs
        (xbe_scr, hse_scr, ae_scr, an_scr, r_scr, i_scr, m_scr, xc_scr, cc_scr, dhd_scr, dh_scr, dy_scr, dxce_scr,
         carry_scr, afirst_scr) = scr
        ti = nt - 1 - i

        @pl.when(i == 0)
        def _():
            gp8_ref[...] = jnp.zeros_like(gp8_ref)
            gga_ref[...] = jnp.zeros_like(gga_ref)
            ggx_ref[...] = jnp.zeros_like(ggx_ref)
            gnw_ref[...] = jnp.zeros_like(gnw_ref)
            dxce_scr[tm:tm + SUBLANES, :] = jnp.zeros((SUBLANES, BW), F32)
            carry_scr[...] = jnp.zeros_like(carry_scr)
            afirst_scr[...] = jnp.zeros_like(afirst_scr)

        has_prev = (ti > 0).astype(F32)
        xbe_scr[0:SUBLANES, :] = zbp_ref[:, :BW] * has_prev
        xbe_scr[SUBLANES:SUBLANES + tm, :] = zb_ref[:, :BW]
        hse_scr[0:SUBLANES, :] = hsp_ref[...] * has_prev
        hse_scr[SUBLANES:SUBLANES + tm, :] = hs_ref[...]

        dx2 = dx2_ref[...]
        dy_scr[...] = _dot_nt(dx2.astype(BF16), wout_ref[...])

        for hh in range(BH):
            cs = slice(hh * HD, (hh + 1) * HD)
            xc = _conv(p8_ref, cs, xbe_scr[SUBLANES:SUBLANES + tm, cs], xbe_scr[7:7 + tm, cs],
                       xbe_scr[6:6 + tm, cs], xbe_scr[5:5 + tm, cs])
            r, ig, _, a, mult, rm = _gates(p8_ref, gcat_ref, hh, xc)
            cc_scr[:, cs] = a * hse_scr[7:7 + tm, cs] - (ig * xc) * (a * a * rm)
            xc_scr[:, cs] = xc
            r_scr[:, cs] = r
            i_scr[:, cs] = ig
            m_scr[:, cs] = mult
            ae_scr[0:tm, cs] = a
            gt = zb_ref[:, BW + hh * HD:BW + (hh + 1) * HD]
            sig = _sigmoid(gt)
            dy = dy_scr[:, cs]
            dhd_scr[:, cs] = dy * (gt * sig)
            dzb_ref[:, BW + hh * HD:BW + (hh + 1) * HD] = (
                dy * hs_ref[:, cs] * (sig * (1.0 + gt * (1.0 - sig)))).astype(BF16)
        ae_scr[tm:tm + SUBLANES, :] = jnp.broadcast_to(afirst_scr[...], (SUBLANES, BW))
        an_scr[...] = ae_scr[1:1 + tm, :]
        afirst_scr[...] = ae_scr[0:1, :]
        carry_scr[...] = _scan_rows(an_scr, dhd_scr, dh_scr, carry_scr[...], tm, True)

        for hh in range(BH):
            cs = slice(hh * HD, (hh + 1) * HD)
            dh = dh_scr[:, cs]
            mult = m_scr[:, cs]
            ig = i_scr[:, cs]
            r = r_scr[:, cs]
            xc = xc_scr[:, cs]
            lam = p8_ref[7:8, cs]
            sp = _softplus_neg(lam)
            dla = dh * cc_scr[:, cs]
            gp8_ref[7:8, cs] += _rowsum(dla * ((-RG_C) * r)) * (-_sigmoid(-lam))
            dpr = dla * ((-RG_C) * sp) * (r * (1.0 - r))
            dpi = dh * mult * xc * (ig * (1.0 - ig))
            gp8_ref[5:6, cs] += _rowsum(dpr)
            gp8_ref[6:7, cs] += _rowsum(dpi)
            dcat = jnp.concatenate([dpr, dpi], axis=1).astype(BF16)
            dxc = dh * mult * ig + _dot_nt(dcat, gcat_ref[hh])
            gg = _dot(xc.T.astype(BF16), dcat)
            gga_ref[hh] += gg[:, :HD]
            ggx_ref[hh] += gg[:, HD:]
            dxce_scr[0:tm, cs] = dxc
            gp8_ref[4:5, cs] += _rowsum(dxc)
            gp8_ref[3:4, cs] += _rowsum(dxc * xbe_scr[SUBLANES:SUBLANES + tm, cs])
            gp8_ref[2:3, cs] += _rowsum(dxc * xbe_scr[7:7 + tm, cs])
            gp8_ref[1:2, cs] += _rowsum(dxc * xbe_scr[6:6 + tm, cs])
            gp8_ref[0:1, cs] += _rowsum(dxc * xbe_scr[5:5 + tm, cs])
        for hh in range(BH):
            cs = slice(hh * HD, (hh + 1) * HD)
            dxb = p8_ref[3:4, cs] * dxce_scr[0:tm, cs]
            dxb = dxb + p8_ref[2:3, cs] * dxce_scr[1:1 + tm, cs]
            dxb = dxb + p8_ref[1:2, cs] * dxce_scr[2:2 + tm, cs]
            dxb = dxb + p8_ref[0:1, cs] * dxce_scr[3:3 + tm, cs]
            dzb_ref[:, cs] = dxb.astype(BF16)
        dxce_scr[tm:tm + SUBLANES, :] = dxce_scr[0:SUBLANES, :]

        dh1 = jnp.zeros((tm, D), F32)
        for k in range(NDEV):
            dh1 = dh1 + _dot_nt(dzb_ref[:, k * CB:(k + 1) * CB], win_ref[k])
        x1 = x1_ref[...]
        r1 = _rms(x1)
        dx1 = dx2 + _rms_bwd(dh1, x1, r1, nw_ref[...])
        dx1_ref[...] = dx1
        dx1b_ref[...] = dx1.astype(BF16)
        gnw_ref[...] += _rowsum(dh1 * x1 * r1)

    tile = lambda w: pl.BlockSpec((tm, w), lambda i: (nt - 1 - i, 0))
    prev = lambda w: pl.BlockSpec((SUBLANES, w), lambda i: (jnp.maximum((nt - 1 - i) * per - 1, 0), 0))
    whole = lambda *s: pl.BlockSpec(s, lambda i: (0,) * len(s))
    full = lambda: pltpu.VMEM((tm, BW), F32)
    ext = lambda: pltpu.VMEM((tm + SUBLANES, BW), F32)
    out, _ = _call(
        main, [], name="bwd_b", grid=(nt,),
        ins=[dx2, zb, zb, hs, hs, x1, nw, win8, p8, gcat, wout],
        in_specs=[tile(D), tile(2 * BW), prev(2 * BW), tile(BW), prev(BW), tile(D), _VMEM, _VMEM, _VMEM, _VMEM, _VMEM],
        out_shape=[_sds((s_len, D), F32), _sds((s_len, D), BF16), _sds((s_len, 2 * BW), BF16), _sds((SUBLANES, BW), F32),
                   _sds((BH, HD, HD), F32), _sds((BH, HD, HD), F32), _sds((1, D), F32)],
        out_specs=[tile(D), tile(D), tile(2 * BW), whole(SUBLANES, BW), whole(BH, HD, HD), whole(BH, HD, HD),
                   whole(1, D)],
        scratch=[ext(), ext(), ext(), full(), full(), full(), full(), full(), full(), full(), full(), full(), ext(),
                 pltpu.VMEM((1, BW), F32), pltpu.VMEM((1, BW), F32)])
    return out


def _transpose_into(dst_ref, src_ref, rows):
    s_len = src_ref.shape[0]
    for r0 in range(0, s_len, rows):
        dst_ref[:, r0:r0 + rows] = src_ref[r0:r0 + rows, :].astype(F32).T.astype(BF16)


def _wgrad(a, b, jobs, *, by_rows, per, name, relay_step=0):
    s_len, m = a.shape
    n = b.shape[1]
    r, cd = (m // NDEV, n) if by_rows else (m, n // NDEV)
    nsteps = NDEV // per
    at_rows = per * r if by_rows else m

    def main(i, ins, outs, scr):
        a_ref, b_ref = ins
        q_ref, acc_ref = outs
        at_scr, stage, mine, land, send_sems, recv_sems = scr
        x, y, c = _place()

        def to_sibling(pi):
            return pltpu.make_async_remote_copy(
                src_ref=stage.at[pi & 1], dst_ref=land.at[pi], send_sem=send_sems.at[pi], recv_sem=recv_sems.at[pi],
                device_id=(x, y, 1 - c), device_id_type=MESH)

        if by_rows:
            _transpose_into(at_scr, a_ref, 256)
        else:
            @pl.when(i == 0)
            def _():
                _transpose_into(at_scr, a_ref, 256)

        res = _dot(at_scr[...], b_ref[...]).astype(BF16)
        for k in range(per):
            blk = per * i + k
            pi, pc = blk >> 1, blk & 1
            val = res[k * r:(k + 1) * r, :] if by_rows else res

            @pl.when(pc != c)
            def _():
                @pl.when(pi >= 2)
                def _():
                    to_sibling(pi - 2).wait_send()

                stage[pi & 1] = val
                to_sibling(pi).start()

            @pl.when(pc == c)
            def _():
                mine[pi] = val

        @pl.when(i == nsteps - 1)
        def _():
            for p in range(4):
                to_sibling(p).wait_recv()
            to_sibling(2).wait_send()
            to_sibling(3).wait_send()
            _chip_sums(mine, land, q_ref, acc_ref, x, y)

    if by_rows:
        in_specs = [pl.BlockSpec((s_len, at_rows), lambda j: (0, j)), _VMEM]
    else:
        in_specs = [_VMEM, pl.BlockSpec((s_len, cd), lambda j: (0, j))]
    blk_vmem = lambda k: pltpu.VMEM((k, r, cd), BF16)
    (q, acc), job_out = _call(
        main, jobs, name=name, grid=(nsteps,), relay_step=relay_step, ins=[a, b], in_specs=in_specs,
        out_shape=[_sds((NCHIP_OTHER, r, cd), BF16), _sds((r, cd), F32)],
        out_specs=[pl.BlockSpec((NCHIP_OTHER, r, cd), lambda j: (0, 0, 0)), pl.BlockSpec((r, cd), lambda j: (0, 0))],
        scratch=[pltpu.VMEM((at_rows, s_len), BF16), blk_vmem(2), blk_vmem(4), blk_vmem(4),
                 pltpu.SemaphoreType.DMA((4,)), pltpu.SemaphoreType.DMA((4,))])
    return q, acc, job_out


def _adam_math(w, g, m, v):
    m = B1 * m + (1.0 - B1) * g
    v = B2 * v + (1.0 - B2) * (g * g)
    m_hat = m / (1.0 - B1 ** STEP)
    v_hat = v / (1.0 - B2 ** STEP)
    delta = (-LR) * (m_hat / (jnp.sqrt(v_hat) + ADAM_EPS) + WD * w)
    return delta, m, v


def _adam_big(w, acc, land, m, v, name):
    r, cd = w.shape
    rb = 256 if r % 256 == 0 else r

    def body(w_ref, acc_ref, land_ref, m_ref, v_ref, g_ref, d_ref, mo_ref, vo_ref):
        g = acc_ref[...]
        for j in range(NCHIP_OTHER):
            g = g + land_ref[j].astype(F32)
        g_ref[...] = g
        d_ref[...], mo_ref[...], vo_ref[...] = _adam_math(w_ref[...], g, m_ref[...], v_ref[...])

    blk = pl.BlockSpec((rb, cd), lambda i: (i, 0))
    blk3 = pl.BlockSpec((NCHIP_OTHER, rb, cd), lambda i: (0, i, 0))
    return pl.pallas_call(
        body, name=name, grid=(r // rb,), in_specs=[blk, blk, blk3, blk, blk], out_specs=[blk] * 4,
        out_shape=[_sds((r, cd), F32)] * 4,
        compiler_params=_params(dimension_semantics=("arbitrary",)),
    )(w, acc, land, m, v)


def _adam_small(groups):
    n = len(groups)

    def body(*refs):
        ins, outs = refs[:4 * n], refs[4 * n:]
        for k in range(n):
            w_ref, g_ref, m_ref, v_ref = ins[4 * k:4 * k + 4]
            d, mo, vo = _adam_math(w_ref[...], g_ref[...], m_ref[...], v_ref[...])
            outs[3 * k][...] = d
            outs[3 * k + 1][...] = mo
            outs[3 * k + 2][...] = vo

    flat = [a for grp in groups for a in grp]
    shapes = [_sds(grp[0].shape, F32) for grp in groups for _ in range(3)]
    res = pl.pallas_call(
        body, name="adam_small", in_specs=[_VMEM] * (4 * n), out_specs=[_VMEM] * (3 * n), out_shape=shapes,
        compiler_params=_params(),
    )(*flat)
    return [tuple(res[3 * k:3 * k + 3]) for k in range(n)]


TM_FWD_A = 256
RELAY_STEP_FWD_A = 4
RELAY_STEP_FWD_B = 2
TM_BWD_A = 256
TM_BWD_A_IN = 256
TM_FWD_B = 256
TM_HEAD = 512
TM_BWD_B = 256


def _pack(parts, rows):
    flat = jnp.concatenate([p.reshape(-1) for p in parts])
    return jnp.pad(flat, (0, NDEV * rows * LANES - flat.shape[0])).reshape(NDEV, rows, LANES)


def _unpack(packed, shapes):
    flat, out, off = packed.reshape(-1), [], 0
    for s in shapes:
        size = 1
        for d in s:
            size *= d
        out.append(flat[off:off + size].reshape(s))
        off += size
    return out


def kernel(x, norm_w, a_w_in, a_ln_w, a_ln_b, a_w_s, a_b_s, a_w_out, b_w_in, b_conv_w, b_conv_b, b_gate_a_w, b_gate_a_b, b_gate_x_w, b_gate_x_b, b_lambda, b_w_out, norm_f_w, loss_target, m_norm_w, m_a_w_in, m_a_ln_w, m_a_ln_b, m_a_w_s, m_a_b_s, m_a_w_out, m_b_w_in, m_b_conv_w, m_b_conv_b, m_b_gate_a_w, m_b_gate_a_b, m_b_gate_x_w, m_b_gate_x_b, m_b_lambda, m_b_w_out, m_norm_f_w, v_norm_w, v_a_w_in, v_a_ln_w, v_a_ln_b, v_a_w_s, v_a_b_s, v_a_w_out, v_b_w_in, v_b_conv_w, v_b_conv_b, v_b_gate_a_w, v_b_gate_a_b, v_b_gate_x_w, v_b_gate_x_b, v_b_lambda, v_b_w_out, v_norm_f_w):
    me = 4 * lax.axis_index("x") + 2 * lax.axis_index("y") + lax.axis_index("c")
    xs, tgt = x[0], loss_target[0]
    nw0, nw1, nfw = norm_w[0:1], norm_w[1:2], norm_f_w.reshape(1, D)
    w_s, bst = a_w_s[0], a_b_s[0].T
    gcat = jnp.concatenate([b_gate_a_w[0], b_gate_x_w[0]], axis=-1).astype(BF16)

    p8_shard = jnp.concatenate([b_conv_w[0], b_conv_b, b_gate_a_b, b_gate_x_b, b_lambda], axis=0)
    ((win_a8, p8_all),) = _comm_only([_Gather([a_w_in[0].astype(BF16), p8_shard])], "gather_first")
    p8 = jnp.transpose(p8_all, (1, 0, 2)).reshape(SUBLANES, BW)

    (z, h0, ya), ((wout_a8, win_b8),) = _fwd_a(
        xs, nw0, win_a8, a_ln_w, a_ln_b, w_s, bst, [_Gather([a_w_out[0].astype(BF16), b_w_in[0].astype(BF16)])],
        tm=TM_FWD_A, relay_step=RELAY_STEP_FWD_A)
    wout_a = wout_a8.reshape(AW, D)
    (x1, zb, hs, h1, yb), ((wout_b8,),) = _fwd_b(
        xs, ya, wout_a, nw1, win_b8, p8, gcat, [_Gather([b_w_out[0].astype(BF16)])],
        tm=TM_FWD_B, relay_step=RELAY_STEP_FWD_B)
    wout_b = wout_b8.reshape(BW, D)
    dx2, dx2b, loss, g_nfw = _head(x1, yb, wout_b, nfw, tgt, tm=TM_HEAD)

    dx1, dx1b, dzb, g_p8, g_ga, g_gx, g_nw1 = _bwd_b(dx2, zb, hs, x1, nw1, win_b8, p8, gcat, wout_b, tm=TM_BWD_B)
    q_wout_b, acc_wout_b, _ = _wgrad(yb, dx2b, [], by_rows=True, per=2, name="wgrad_b_out")
    shapes_b = [(1, D), (1, D), (SUBLANES, BW), (1, 1)]
    pack_b = _pack([g_nfw, g_nw1, g_p8, loss], 16)
    small_b = _InChip([g_ga.reshape(NDEV, -1, HD), g_gx.reshape(NDEV, -1, HD), pack_b])
    q_win_b, acc_win_b, (sm_b,) = _wgrad(h1, dzb, [small_b], by_rows=False, per=1, name="wgrad_b_in")
    qs_b, accs_b = [q_win_b, q_wout_b, *sm_b[:3]], [acc_win_b, acc_wout_b, *sm_b[3:]]

    (dz, g_lnw, g_lnb, g_ws, g_bst), (lands_b,) = _bwd_a(
        dx1b, z, a_ln_w, a_ln_b, w_s, bst, wout_a, [_Exchange(qs_b)], tm=TM_BWD_A)
    shapes_a = [(1, AW), (1, AW), (CH, G)]
    pack_a = _pack([g_lnw, g_lnb, g_bst], 8)
    q_wout_a, acc_wout_a, (red_b, sm_a) = _wgrad(
        ya, dx1b, [_SumGather(accs_b[2:], lands_b[2:]), _InChip([g_ws, pack_a])], by_rows=True, per=1,
        name="wgrad_a_out", relay_step=2)
    qs_a, accs_a = [q_wout_a, *sm_a[:2]], [acc_wout_a, *sm_a[2:]]
    q_win_a, acc_win_a, (lands_a,) = _wgrad(h0, dz, [_Exchange(qs_a)], by_rows=False, per=1, name="wgrad_a_in")
    (gx, g_nw0), (red_a, (l_win_a,)) = _bwd_a_in(
        dz, dx1, xs, nw0, win_a8, [_SumGather(accs_a[1:], lands_a[1:]), _Exchange([q_win_a])],
        tm=TM_BWD_A_IN, relay_step=2)
    g_nw0 = _allreduce_direct(g_nw0, "allreduce_norm_w0")

    r_ga, r_gx, r_pack_b = red_b
    r_nfw, r_nw1, r_p8, loss = _unpack(r_pack_b, shapes_b)
    r_ws, r_pack_a = red_a
    r_lnw, r_lnb, r_bst = _unpack(r_pack_a, shapes_a)
    g_p8 = lax.dynamic_slice_in_dim(r_p8, me * (BW // NDEV), BW // NDEV, axis=1)
    loss = loss[0, 0]

    weights = dict(norm_w=norm_w, a_w_in=a_w_in, a_ln_w=a_ln_w, a_ln_b=a_ln_b, a_w_s=a_w_s, a_b_s=a_b_s, a_w_out=a_w_out,
                   b_w_in=b_w_in, b_conv_w=b_conv_w, b_conv_b=b_conv_b, b_gate_a_w=b_gate_a_w, b_gate_a_b=b_gate_a_b,
                   b_gate_x_w=b_gate_x_w, b_gate_x_b=b_gate_x_b, b_lambda=b_lambda, b_w_out=b_w_out, norm_f_w=norm_f_w)
    mom1 = dict(norm_w=m_norm_w, a_w_in=m_a_w_in, a_ln_w=m_a_ln_w, a_ln_b=m_a_ln_b, a_w_s=m_a_w_s, a_b_s=m_a_b_s,
                a_w_out=m_a_w_out, b_w_in=m_b_w_in, b_conv_w=m_b_conv_w, b_conv_b=m_b_conv_b, b_gate_a_w=m_b_gate_a_w,
                b_gate_a_b=m_b_gate_a_b, b_gate_x_w=m_b_gate_x_w, b_gate_x_b=m_b_gate_x_b, b_lambda=m_b_lambda,
                b_w_out=m_b_w_out, norm_f_w=m_norm_f_w)
    mom2 = dict(norm_w=v_norm_w, a_w_in=v_a_w_in, a_ln_w=v_a_ln_w, a_ln_b=v_a_ln_b, a_w_s=v_a_w_s, a_b_s=v_a_b_s,
                a_w_out=v_a_w_out, b_w_in=v_b_w_in, b_conv_w=v_b_conv_w, b_conv_b=v_b_conv_b, b_gate_a_w=v_b_gate_a_w,
                b_gate_a_b=v_b_gate_a_b, b_gate_x_w=v_b_gate_x_w, b_gate_x_b=v_b_gate_x_b, b_lambda=v_b_lambda,
                b_w_out=v_b_w_out, norm_f_w=v_norm_f_w)
    names = list(weights)

    def as2d(a):
        return a.reshape(-1, a.shape[-1])

    upd, grads = {}, {}
    for k, acc, land in (("a_w_in", acc_win_a, l_win_a), ("a_w_out", accs_a[0], lands_a[0]),
                         ("b_w_in", accs_b[0], lands_b[0]), ("b_w_out", accs_b[1], lands_b[1])):
        g, d, mo, vo = _adam_big(as2d(weights[k]), acc, land, as2d(mom1[k]), as2d(mom2[k]), "adam_" + k)
        grads[k] = g[None]
        upd[k] = (d, mo, vo)
    grads.update(
        norm_w=jnp.concatenate([g_nw0, r_nw1], axis=0), a_ln_w=r_lnw, a_ln_b=r_lnb,
        a_w_s=r_ws.reshape(1, G, CH, CH), a_b_s=r_bst.T[None],
        b_conv_w=g_p8[None, 0:4], b_conv_b=g_p8[4:5], b_gate_a_w=r_ga.reshape(1, BH, HD, HD), b_gate_a_b=g_p8[5:6],
        b_gate_x_w=r_gx.reshape(1, BH, HD, HD), b_gate_x_b=g_p8[6:7], b_lambda=g_p8[7:8], norm_f_w=r_nfw.reshape(D))
    small_names = [k for k in names if k not in upd]
    res = _adam_small([(as2d(weights[k]), as2d(grads[k]), as2d(mom1[k]), as2d(mom2[k])) for k in small_names])
    for k, r3 in zip(small_names, res):
        upd[k] = r3
    deltas = [upd[k][0].reshape(weights[k].shape) for k in names]
    new_m = [upd[k][1].reshape(weights[k].shape) for k in names]
    new_v = [upd[k][2].reshape(weights[k].shape) for k in names]
    return (loss, gx[None], *[grads[k] for k in names], *deltas, *new_m, *new_v)
```

```python
import jax
import jax.numpy as jnp
from jax import lax
from jax.experimental import pallas as pl
from jax.experimental.pallas import tpu as pltpu

F32 = jnp.float32
BF16 = jnp.bfloat16
MESH = pl.DeviceIdType.MESH

NDEV = 8
NCHIP_OTHER = 3
D = 1024
AW = 2048
G = 8
GD = AW // G
CH = 128
BW = 1536
BH = 12
HD = BW // BH
CA = 3 * AW // NDEV
CB = 2 * BW // NDEV
RMS_EPS = 1e-6
LN_EPS = 1e-5
RG_C = 8.0
LR, B1, B2, ADAM_EPS, WD, STEP = 0.001, 0.9, 0.999, 1e-08, 0.01, 10
V7X_VMEM_BYTES = 64 * 1024 * 1024
VMEM_LIMIT = V7X_VMEM_BYTES - 8 * 1024 * 1024
SUBLANES = 8
LANES = 128
BF16_ROWS = 16
GELU_C = 0.7978845608028654
GELU_K = 0.044715

_VMEM = pl.BlockSpec(memory_space=pltpu.VMEM)
_HBM = pl.BlockSpec(memory_space=pltpu.HBM)


def _sds(shape, dtype):
    return jax.ShapeDtypeStruct(tuple(shape), dtype)


def _params(**kw):
    return pltpu.CompilerParams(vmem_limit_bytes=VMEM_LIMIT, **kw)


def _gelu_t(z):
    t = jnp.tanh(GELU_C * (z + GELU_K * (z * z * z)))
    return 0.5 * z * (1.0 + t), t


def _dgelu(z, t):
    return 0.5 * (1.0 + t) + 0.5 * z * (1.0 - t * t) * (GELU_C * (1.0 + 3.0 * GELU_K * z * z))


def _sigmoid(v):
    return 0.5 * jnp.tanh(0.5 * v) + 0.5


def _softplus_neg(lam):
    return jnp.maximum(-lam, 0.0) + jnp.log1p(jnp.exp(-jnp.abs(lam)))


def _dot(a, b):
    return jnp.dot(a, b, preferred_element_type=F32)


def _dot_nt(a, b):
    return lax.dot_general(a, b, (((1,), (1,)), ((), ())), preferred_element_type=F32)


def _rowsum(v):
    return jnp.sum(v, axis=0, keepdims=True)


def _causal_mask():
    r = lax.broadcasted_iota(jnp.int32, (CH, CH), 0)
    c = lax.broadcasted_iota(jnp.int32, (CH, CH), 1)
    return r >= c


def _rms(x):
    return lax.rsqrt(jnp.mean(x * x, axis=-1, keepdims=True) + RMS_EPS)


def _rms_bwd(dh, x, r, nw):
    gy = dh * nw
    return r * gy - x * (r * r * r) * jnp.mean(gy * x, axis=-1, keepdims=True)


def _place():
    return lax.axis_index("x"), lax.axis_index("y"), lax.axis_index("c")


def _other_chips(x, y):
    return [(1 - x, y), (x, 1 - y), (1 - x, 1 - y)]


GATHER_SLOTS = 10


def _gather_ops(ins, outs, send_sems, recv_sems, local_sems):
    n = len(ins)
    x, y, c = _place()
    sibling = (x, y, 1 - c)
    xn, yn, dg = _other_chips(x, y)
    split = [ins[i].shape[0] % (2 * BF16_ROWS) == 0 for i in range(n)]

    def blk(chip, core):
        return 4 * chip[0] + 2 * chip[1] + core

    me = blk((x, y), c)

    def part(ref, i, half):
        if half is None:
            return ref
        h = ins[i].shape[0] // 2
        return ref.at[pl.ds(half * h, h)]

    def copy(i, k, block, to, half=None, src=None):
        dst = part(outs[i].at[block], i, half)
        return pltpu.make_async_remote_copy(
            src_ref=dst if src is None else part(src, i, half), dst_ref=dst,
            send_sem=send_sems.at[k, i], recv_sem=recv_sems.at[k, i], device_id=to, device_id_type=MESH)

    def first_copies():
        mine = [pltpu.make_async_copy(ins[i], outs[i].at[me], local_sems.at[i]) for i in range(n)]
        first = []
        for i in range(n):
            first.append(copy(i, 0, me, sibling, src=ins[i]))
            if split[i]:
                first.append(copy(i, 1, me, (*xn, c), 0, ins[i]))
                first.append(copy(i, 3, me, (*yn, c), 1, ins[i]))
                first.append(copy(i, 2, me, (*xn, c), 1, ins[i]))
                first.append(copy(i, 4, me, (*yn, c), 0, ins[i]))
            else:
                first.append(copy(i, 1, me, (*xn, c), None, ins[i]))
                first.append(copy(i, 3, me, (*yn, c), None, ins[i]))
                first.append(copy(i, 5, me, (*dg, c), None, ins[i]))
        return mine, first

    def onward():
        out = []
        for i in range(n):
            if split[i]:
                out.append(copy(i, 5, blk(xn, c), (*yn, c), 0))
                out.append(copy(i, 6, blk(yn, c), (*xn, c), 1))
        return out

    def start():
        mine, first = first_copies()
        for cp in mine + first:
            cp.start()

    def relay():
        sends = onward()
        for i in range(n):
            if split[i]:
                copy(i, 1, blk(xn, c), sibling, 0).wait_recv()
                sends.pop(0).start()
                copy(i, 3, blk(yn, c), sibling, 1).wait_recv()
                sends.pop(0).start()

    def finish():
        mine, first = first_copies()
        passed = []

        def pass_on(i, j, chip):
            fwd = copy(i, 7 + j, blk(chip, c), sibling)
            fwd.start()
            passed.append(fwd)

        for i in range(n):
            if split[i]:
                copy(i, 2, blk(xn, c), sibling, 1).wait_recv()
                pass_on(i, 0, xn)
                copy(i, 4, blk(yn, c), sibling, 0).wait_recv()
                pass_on(i, 1, yn)
                copy(i, 5, blk(dg, c), sibling, 0).wait_recv()
                copy(i, 6, blk(dg, c), sibling, 1).wait_recv()
                pass_on(i, 2, dg)
            else:
                copy(i, 1, blk(xn, c), sibling).wait_recv()
                pass_on(i, 0, xn)
                copy(i, 3, blk(yn, c), sibling).wait_recv()
                pass_on(i, 1, yn)
                copy(i, 5, blk(dg, c), sibling).wait_recv()
                pass_on(i, 2, dg)
        for i in range(n):
            copy(i, 0, blk((x, y), 1 - c), sibling).wait_recv()
            for j, chip in enumerate((xn, yn, dg)):
                copy(i, 7 + j, blk(chip, 1 - c), sibling).wait_recv()
        for cp in first + passed + onward():
            cp.wait_send()
        for cp in mine:
            cp.wait()

    return start, relay, finish


def _gather_sems(n):
    return [pltpu.SemaphoreType.DMA((GATHER_SLOTS, n)), pltpu.SemaphoreType.DMA((GATHER_SLOTS, n)),
            pltpu.SemaphoreType.DMA((n,))]


class _Gather:
    def __init__(self, shards):
        n = len(shards)
        self.ins, self.in_specs = list(shards), [_HBM] * n
        self.out_shape = [_sds((NDEV,) + s.shape, s.dtype) for s in shards]
        self.out_specs = [_HBM] * n
        self.scratch = _gather_sems(n)

    def ops(self, ins, outs, scr):
        return _gather_ops(ins, outs, *scr)


class _Exchange:
    def __init__(self, qs):
        n = len(qs)
        self.ins, self.in_specs = list(qs), [_HBM] * n
        self.out_shape = [_sds(q.shape, q.dtype) for q in qs]
        self.out_specs = [_HBM] * n
        self.scratch = [pltpu.SemaphoreType.DMA((NCHIP_OTHER, n)), pltpu.SemaphoreType.DMA((NCHIP_OTHER, n))]

    def ops(self, ins, outs, scr):
        send_sems, recv_sems = scr
        n = len(ins)
        x, y, c = _place()
        chips = _other_chips(x, y)

        def copies():
            return [pltpu.make_async_remote_copy(
                src_ref=ins[i].at[j], dst_ref=outs[i].at[j], send_sem=send_sems.at[j, i],
                recv_sem=recv_sems.at[j, i], device_id=(*chips[j], c), device_id_type=MESH)
                for i in range(n) for j in range(NCHIP_OTHER)]

        def start():
            for cp in copies():
                cp.start()

        def finish():
            cps = copies()
            for cp in cps:
                cp.wait_recv()
            for cp in cps:
                cp.wait_send()

        return start, lambda: None, finish


class _ExchangeVia:
    def __init__(self, q):
        _, r, cd = q.shape
        half = (2, r // 2, cd)
        self.ins, self.in_specs = [q], [_HBM]
        self.out_shape, self.out_specs = [_sds((2, r, cd), q.dtype)], [_HBM]
        self.scratch = [pltpu.VMEM(half, q.dtype), pltpu.VMEM(half, q.dtype), pltpu.VMEM(half, q.dtype),
                        pltpu.SemaphoreType.DMA((6,)), pltpu.SemaphoreType.DMA((6,)), pltpu.SemaphoreType.DMA((2,))]

    def ops(self, ins, outs, scr):
        (q,), (land,) = ins, outs
        relayed, own, comb, send_sems, recv_sems, local_sems = scr
        h = q.shape[1] // 2
        x, y, c = _place()
        xn, yn, _ = _other_chips(x, y)
        h0, h1 = pl.ds(0, h), pl.ds(h, h)

        def remote(k, src, dst, chip):
            return pltpu.make_async_remote_copy(src_ref=src, dst_ref=dst, send_sem=send_sems.at[k],
                                                recv_sem=recv_sems.at[k], device_id=(*chip, c), device_id_type=MESH)

        def via():
            return [remote(2, q.at[2, h0], relayed.at[0], xn), remote(3, q.at[2, h1], relayed.at[1], yn)]

        def direct():
            return [remote(0, q.at[0, h0], land.at[0, h0], xn), remote(1, q.at[1, h1], land.at[1, h1], yn)]

        def second():
            return [remote(4, comb.at[0], land.at[1, h0], yn), remote(5, comb.at[1], land.at[0, h1], xn)]

        def mine():
            return [pltpu.make_async_copy(q.at[1, h0], own.at[0], local_sems.at[0]),
                    pltpu.make_async_copy(q.at[0, h1], own.at[1], local_sems.at[1])]

        def start():
            for cp in via() + direct() + mine():
                cp.start()

        def relay():
            arrived, loaded, onward = via(), mine(), second()
            for k in range(2):
                arrived[k].wait_recv()
                loaded[k].wait()
                comb[k] = (own[k].astype(F32) + relayed[k].astype(F32)).astype(comb.dtype)
                onward[k].start()

        def finish():
            landing = direct() + second()
            for cp in landing:
                cp.wait_recv()
            for cp in via() + landing:
                cp.wait_send()

        return start, relay, finish


class _SumGather:
    def __init__(self, accs, lands):
        n = len(accs)
        self.n = n
        self.ins, self.in_specs = list(accs) + list(lands), [_VMEM] * (2 * n)
        self.out_shape = [_sds((NDEV,) + a.shape, a.dtype) for a in accs]
        self.out_specs = [_HBM] * n
        self.scratch = [pltpu.VMEM(a.shape, a.dtype) for a in accs] + _gather_sems(n)

    def ops(self, ins, outs, scr):
        n = self.n
        accs, lands, mine = ins[:n], ins[n:], scr[:n]
        g_start, relay, finish = _gather_ops(mine, outs, *scr[n:])

        def start():
            for i in range(n):
                mine[i][...] = accs[i][...] + lands[i][0] + lands[i][1] + lands[i][2]
            g_start()

        return start, relay, finish


def _call(main, jobs, *, name, grid, ins, in_specs, out_shape, out_specs, scratch, relay_step=0):
    nsteps = grid[0] if grid else 1
    n_in, n_out, n_scr = len(ins), len(out_shape), len(scratch)

    def body(*refs):
        pos = [0]

        def take(k):
            r = refs[pos[0]:pos[0] + k]
            pos[0] += k
            return r

        m_in = take(n_in)
        j_in = [take(len(j.ins)) for j in jobs]
        m_out = take(n_out)
        j_out = [take(len(j.out_shape)) for j in jobs]
        m_scr = take(n_scr)
        j_scr = [take(len(j.scratch)) for j in jobs]
        ops = [j.ops(a, b, s) for j, a, b, s in zip(jobs, j_in, j_out, j_scr)]
        i = pl.program_id(0) if grid else 0
        if not grid:
            for o in ops:
                o[0]()
            main(i, m_in, m_out, m_scr)
            for o in ops:
                o[1]()
            for o in ops:
                o[2]()
            return

        if ops:
            @pl.when(i == 0)
            def _():
                for o in ops:
                    o[0]()

        main(i, m_in, m_out, m_scr)

        if ops:
            @pl.when(i == min(relay_step, nsteps - 1))
            def _():
                for o in ops:
                    o[1]()

            @pl.when(i == nsteps - 1)
            def _():
                for o in ops:
                    o[2]()

    extra = dict(dimension_semantics=("arbitrary",)) if grid else {}
    res = pl.pallas_call(
        body, name=name, grid=grid,
        in_specs=list(in_specs) + [s for j in jobs for s in j.in_specs],
        out_specs=list(out_specs) + [s for j in jobs for s in j.out_specs],
        out_shape=list(out_shape) + [s for j in jobs for s in j.out_shape],
        scratch_shapes=list(scratch) + [s for j in jobs for s in j.scratch],
        compiler_params=_params(**extra),
    )(*ins, *[a for j in jobs for a in j.ins])
    main_out, rest, job_out = res[:n_out], res[n_out:], []
    for j in jobs:
        k = len(j.out_shape)
        job_out.append(rest[:k])
        rest = rest[k:]
    return main_out, job_out


def _comm_only(jobs, name):
    _, job_out = _call(lambda i, a, b, s: None, jobs, name=name, grid=(), ins=[], in_specs=[], out_shape=[],
                       out_specs=[], scratch=[])
    return job_out


class _InChip:
    def __init__(self, ps):
        n = len(ps)
        self.n = n
        blk = [p.shape[1:] for p in ps]
        self.ins, self.in_specs = list(ps), [_HBM] * n
        self.out_shape = [_sds((NCHIP_OTHER,) + b, p.dtype) for b, p in zip(blk, ps)] + [_sds(b, F32) for b in blk]
        self.out_specs = [_VMEM] * (2 * n)
        self.scratch = ([pltpu.VMEM((4,) + b, p.dtype) for b, p in zip(blk, ps)] * 2
                        + [pltpu.SemaphoreType.DMA((4, n))] * 3)

    def ops(self, ins, outs, scr):
        n = self.n
        q_refs, acc_refs = outs[:n], outs[n:]
        mines, lands = scr[:n], scr[n:2 * n]
        send_sems, recv_sems, local_sems = scr[2 * n:]
        x, y, c = _place()
        sibling = (x, y, 1 - c)

        def copies():
            out = []
            for i in range(n):
                for pi in range(4):
                    loc = pltpu.make_async_copy(ins[i].at[2 * pi + c], mines[i].at[pi], local_sems.at[pi, i])
                    cp = pltpu.make_async_remote_copy(
                        src_ref=ins[i].at[2 * pi + (1 - c)], dst_ref=lands[i].at[pi],
                        send_sem=send_sems.at[pi, i], recv_sem=recv_sems.at[pi, i],
                        device_id=sibling, device_id_type=MESH)
                    out.append((loc, cp))
            return out

        def start():
            for loc, cp in copies():
                loc.start()
                cp.start()

        def finish():
            pairs = copies()
            for loc, cp in pairs:
                loc.wait()
                cp.wait_recv()
            for i in range(n):
                _chip_sums(mines[i], lands[i], q_refs[i], acc_refs[i], x, y)
            for _, cp in pairs:
                cp.wait_send()

        return start, lambda: None, finish


def _chip_sums(mine, land, q_ref, acc_ref, x, y):
    for j, (qx, qy) in enumerate(_other_chips(x, y)):
        qi = 2 * qx + qy
        q_ref[j] = (mine[qi].astype(F32) + land[qi].astype(F32)).astype(q_ref.dtype)
    mi = 2 * x + y
    acc_ref[...] = mine[mi].astype(F32) + land[mi].astype(F32)


def _allreduce_direct(v, name):
    def body(v_ref, o_ref, buf, send_sems, recv_sems):
        x, y, c = _place()
        me = 4 * x + 2 * y + c
        buf[me] = v_ref[...]
        cps = []
        for k in range(1, NDEV):
            fx, fy, fc = (k >> 2) & 1, (k >> 1) & 1, k & 1
            peer = ((1 - x) if fx else x, (1 - y) if fy else y, (1 - c) if fc else c)
            cps.append((peer, pltpu.make_async_remote_copy(
                src_ref=buf.at[me], dst_ref=buf.at[me], send_sem=send_sems.at[k - 1], recv_sem=recv_sems.at[k - 1],
                device_id=peer, device_id_type=MESH)))
        for _, cp in cps:
            cp.start()
        for k, (peer, _) in enumerate(cps):
            theirs = 4 * peer[0] + 2 * peer[1] + peer[2]
            pltpu.make_async_remote_copy(
                src_ref=buf.at[theirs], dst_ref=buf.at[theirs], send_sem=send_sems.at[k], recv_sem=recv_sems.at[k],
                device_id=peer, device_id_type=MESH).wait_recv()
        acc = buf[0]
        for j in range(1, NDEV):
            acc = acc + buf[j]
        o_ref[...] = acc
        for _, cp in cps:
            cp.wait_send()

    return pl.pallas_call(
        body, name=name, in_specs=[_VMEM], out_specs=_VMEM, out_shape=_sds(v.shape, v.dtype),
        scratch_shapes=[pltpu.VMEM((NDEV,) + v.shape, v.dtype), pltpu.SemaphoreType.DMA((NDEV - 1,)),
                        pltpu.SemaphoreType.DMA((NDEV - 1,))],
        compiler_params=_params(),
    )(v)


def _fwd_a(x, nw, win8, lnw, lnb, ws, bst, jobs, *, tm, relay_step):
    s_len = x.shape[0]
    nt = s_len // tm
    nch = tm // CH

    def main(i, ins, outs, scr):
        x_ref, nw_ref, win_ref, lnw_ref, lnb_ref, ws_ref, bst_ref = ins
        z_ref, h_ref, y_ref = outs
        wc_scr, gv_scr = scr

        @pl.when(i == 0)
        def _():
            m = _causal_mask()
            for g in range(G):
                wc_scr[g] = jnp.where(m, ws_ref[g], 0.0).astype(BF16)

        x = x_ref[...]
        h = (x * _rms(x) * nw_ref[...]).astype(BF16)
        h_ref[...] = h
        for k in range(NDEV):
            z_ref[:, k * CA:(k + 1) * CA] = _dot(h, win_ref[k])

        ssum = jnp.zeros((tm, 1), F32)
        for g in range(G):
            gv = _gelu_t(z_ref[:, AW + g * GD:AW + (g + 1) * GD])[0]
            gv_scr[:, g * GD:(g + 1) * GD] = gv
            ssum = ssum + jnp.sum(gv, axis=-1, keepdims=True)
        mu = ssum * (1.0 / AW)
        vsum = jnp.zeros((tm, 1), F32)
        for g in range(G):
            dlt = gv_scr[:, g * GD:(g + 1) * GD] - mu
            vsum = vsum + jnp.sum(dlt * dlt, axis=-1, keepdims=True)
        rstd = lax.rsqrt(vsum * (1.0 / AW) + LN_EPS)

        for g in range(G):
            cs = slice(g * GD, (g + 1) * GD)
            v = (gv_scr[:, cs] - mu) * rstd * lnw_ref[:, cs] + lnb_ref[:, cs]
            vb = v.astype(BF16)
            u = _gelu_t(z_ref[:, cs])[0]
            zg = z_ref[:, 2 * AW + g * GD:2 * AW + (g + 1) * GD]
            sg = zg * _sigmoid(zg)
            for n in range(nch):
                rs = slice(n * CH, (n + 1) * CH)
                s = _dot(wc_scr[g], vb[rs, :]) + bst_ref[:, g:g + 1]
                y_ref[rs, cs] = (u[rs, :] * s * sg[rs, :]).astype(BF16)

    tile = lambda w: pl.BlockSpec((tm, w), lambda i: (i, 0))
    return _call(
        main, jobs, name="fwd_a", grid=(nt,), relay_step=relay_step,
        ins=[x, nw, win8, lnw, lnb, ws, bst], in_specs=[tile(D), _VMEM, _VMEM, _VMEM, _VMEM, _VMEM, _VMEM],
        out_shape=[_sds((s_len, 3 * AW), F32), _sds((s_len, D), BF16), _sds((s_len, AW), BF16)],
        out_specs=[tile(3 * AW), tile(D), tile(AW)],
        scratch=[pltpu.VMEM((G, CH, CH), BF16), pltpu.VMEM((tm, AW), F32)])


def _bwd_a(dx1, z, lnw, lnb, ws, bst, wout, jobs, *, tm):
    s_len = dx1.shape[0]
    nt = s_len // tm
    nch = tm // CH

    def main(i, ins, outs, scr):
        dx1_ref, z_ref, lnw_ref, lnb_ref, ws_ref, bst_ref, wout_ref = ins
        dz_ref, glnw_ref, glnb_ref, gws_ref, gbst_ref = outs
        wc_scr, wct_scr, vh_scr, dgv_scr, dy_scr, dv_scr, gbs_acc, gwc_acc = scr

        @pl.when(i == 0)
        def _():
            m = _causal_mask()
            for g in range(G):
                wm = jnp.where(m, ws_ref[g], 0.0)
                wc_scr[g] = wm.astype(BF16)
                wct_scr[g] = wm.T.astype(BF16)
            glnw_ref[...] = jnp.zeros_like(glnw_ref)
            glnb_ref[...] = jnp.zeros_like(glnb_ref)
            gbs_acc[...] = jnp.zeros_like(gbs_acc)
            gwc_acc[...] = jnp.zeros_like(gwc_acc)

        dy_scr[...] = _dot_nt(dx1_ref[...], wout_ref[...])

        ssum = jnp.zeros((tm, 1), F32)
        for g in range(G):
            cs = slice(g * GD, (g + 1) * GD)
            zv = z_ref[:, AW + g * GD:AW + (g + 1) * GD]
            gv, t = _gelu_t(zv)
            vh_scr[:, cs] = gv
            dgv_scr[:, cs] = _dgelu(zv, t)
            ssum = ssum + jnp.sum(gv, axis=-1, keepdims=True)
        mu = ssum * (1.0 / AW)
        vsum = jnp.zeros((tm, 1), F32)
        for g in range(G):
            dlt = vh_scr[:, g * GD:(g + 1) * GD] - mu
            vsum = vsum + jnp.sum(dlt * dlt, axis=-1, keepdims=True)
        rstd = lax.rsqrt(vsum * (1.0 / AW) + LN_EPS)

        m1 = jnp.zeros((tm, 1), F32)
        m2 = jnp.zeros((tm, 1), F32)
        for g in range(G):
            cs = slice(g * GD, (g + 1) * GD)
            gs = slice(2 * AW + g * GD, 2 * AW + (g + 1) * GD)
            vhat = (vh_scr[:, cs] - mu) * rstd
            vh_scr[:, cs] = vhat
            vb = (vhat * lnw_ref[:, cs] + lnb_ref[:, cs]).astype(BF16)
            zu = z_ref[:, cs]
            u, tu = _gelu_t(zu)
            zg = z_ref[:, gs]
            sig = _sigmoid(zg)
            sg = zg * sig
            dy = dy_scr[:, cs]
            dsf = dy * u * sg
            dsb = dsf.astype(BF16)
            dvs = []
            for n in range(nch):
                rs = slice(n * CH, (n + 1) * CH)
                s = _dot(wc_scr[g], vb[rs, :]) + bst_ref[:, g:g + 1]
                dys = dy[rs, :] * s
                dz_ref[rs, cs] = (dys * sg[rs, :] * _dgelu(zu[rs, :], tu[rs, :])).astype(BF16)
                dz_ref[rs, gs] = (dys * u[rs, :] * (sig[rs, :] * (1.0 + zg[rs, :] * (1.0 - sig[rs, :])))).astype(BF16)
                gbs_acc[g] += dsf[rs, :]
                gwc_acc[g] += _dot_nt(dsb[rs, :], vb[rs, :])
                dvs.append(_dot(wct_scr[g], dsb[rs, :]))
            dv = jnp.concatenate(dvs, axis=0) if nch > 1 else dvs[0]
            glnw_ref[:, cs] += _rowsum(dv * vhat)
            glnb_ref[:, cs] += _rowsum(dv)
            dvh = dv * lnw_ref[:, cs]
            dv_scr[:, cs] = dvh
            m1 = m1 + jnp.sum(dvh, axis=-1, keepdims=True)
            m2 = m2 + jnp.sum(dvh * vhat, axis=-1, keepdims=True)
        m1 = m1 * (1.0 / AW)
        m2 = m2 * (1.0 / AW)
        for g in range(G):
            cs = slice(g * GD, (g + 1) * GD)
            dgv = rstd * (dv_scr[:, cs] - m1 - vh_scr[:, cs] * m2)
            dz_ref[:, AW + g * GD:AW + (g + 1) * GD] = (dgv * dgv_scr[:, cs]).astype(BF16)

        @pl.when(i == nt - 1)
        def _():
            m = _causal_mask()
            for g in range(G):
                gws_ref[g] = jnp.where(m, gwc_acc[g], 0.0)
                gbst_ref[:, g:g + 1] = jnp.sum(gbs_acc[g], axis=-1, keepdims=True)

    tile = lambda w: pl.BlockSpec((tm, w), lambda i: (i, 0))
    whole = lambda *s: pl.BlockSpec(s, lambda i: (0,) * len(s))
    big = lambda dt: pltpu.VMEM((tm, AW), dt)
    return _call(
        main, jobs, name="bwd_a", grid=(nt,),
        ins=[dx1, z, lnw, lnb, ws, bst, wout], in_specs=[tile(D), tile(3 * AW), _VMEM, _VMEM, _VMEM, _VMEM, _VMEM],
        out_shape=[_sds((s_len, 3 * AW), BF16), _sds((1, AW), F32), _sds((1, AW), F32), _sds((G, CH, CH), F32),
                   _sds((CH, G), F32)],
        out_specs=[tile(3 * AW), whole(1, AW), whole(1, AW), whole(G, CH, CH), whole(CH, G)],
        scratch=[pltpu.VMEM((G, CH, CH), BF16), pltpu.VMEM((G, CH, CH), BF16), big(F32), big(F32), big(F32), big(F32),
                 pltpu.VMEM((G, CH, GD), F32), pltpu.VMEM((G, CH, CH), F32)])


def _bwd_a_in(dz, dx1, x, nw, win8, jobs, *, tm, relay_step):
    s_len = x.shape[0]
    nt = s_len // tm

    def main(i, ins, outs, scr):
        dz_ref, dx1_ref, x_ref, nw_ref, win_ref = ins
        gx_ref, gnw_ref = outs

        @pl.when(i == 0)
        def _():
            gnw_ref[...] = jnp.zeros_like(gnw_ref)

        dh = jnp.zeros((tm, D), F32)
        for k in range(NDEV):
            dh = dh + _dot_nt(dz_ref[:, k * CA:(k + 1) * CA], win_ref[k])
        x = x_ref[...]
        r = _rms(x)
        gx_ref[...] = dx1_ref[...] + _rms_bwd(dh, x, r, nw_ref[...])
        gnw_ref[...] += _rowsum(dh * x * r)

    tile = lambda w: pl.BlockSpec((tm, w), lambda i: (i, 0))
    return _call(
        main, jobs, name="bwd_a_in", grid=(nt,), relay_step=relay_step,
        ins=[dz, dx1, x, nw, win8], in_specs=[tile(3 * AW), tile(D), tile(D), _VMEM, _VMEM],
        out_shape=[_sds((s_len, D), F32), _sds((1, D), F32)],
        out_specs=[tile(D), pl.BlockSpec((1, D), lambda i: (0, 0))], scratch=[])


def _conv(p8_ref, cs, xb, xm1, xm2, xm3):
    xc = p8_ref[4:5, cs] + p8_ref[3:4, cs] * xb
    xc = xc + p8_ref[0:1, cs] * xm3
    xc = xc + p8_ref[1:2, cs] * xm2
    return xc + p8_ref[2:3, cs] * xm1


def _gates(p8_ref, gcat_ref, hh, xc):
    cs = slice(hh * HD, (hh + 1) * HD)
    pre = _dot(xc.astype(BF16), gcat_ref[hh])
    r = _sigmoid(pre[:, :HD] + p8_ref[5:6, cs])
    ig = _sigmoid(pre[:, HD:] + p8_ref[6:7, cs])
    sp = _softplus_neg(p8_ref[7:8, cs])
    la = (-RG_C) * r * sp
    a = jnp.exp(la)
    half_log = 0.5 * jnp.log(jnp.tanh(-la) * (1.0 + a * a))
    return r, ig, sp, a, jnp.exp(half_log), jnp.exp(-half_log)


def _scan_rows(a_ref, b_ref, out_ref, carry, tm, reverse):
    row = lax.broadcasted_iota(jnp.int32, (SUBLANES, BW), 0)
    ngrp = tm // SUBLANES

    def step(j, cr):
        jj = (ngrp - 1 - j) if reverse else j
        off = pl.multiple_of(jj * SUBLANES, SUBLANES)
        a = a_ref[pl.ds(off, SUBLANES), :]
        b = b_ref[pl.ds(off, SUBLANES), :]
        for sh in (1, 2, 4):
            if reverse:
                a_s = pltpu.roll(a, SUBLANES - sh, 0)
                b_s = pltpu.roll(b, SUBLANES - sh, 0)
                m = row < SUBLANES - sh
            else:
                a_s = pltpu.roll(a, sh, 0)
                b_s = pltpu.roll(b, sh, 0)
                m = row >= sh
            b = jnp.where(m, a * b_s + b, b)
            a = jnp.where(m, a * a_s, a)
        o = b + a * cr
        out_ref[pl.ds(off, SUBLANES), :] = o
        return o[0:1, :] if reverse else o[SUBLANES - 1:SUBLANES, :]

    return lax.fori_loop(0, ngrp, step, carry)


def _fwd_b(x, ya, wout_a, nw, win8, p8, gcat, jobs, *, tm, relay_step):
    s_len = x.shape[0]
    nt = s_len // tm

    def main(i, ins, outs, scr):
        x_ref, ya_ref, wouta_ref, nw_ref, win_ref, p8_ref, gcat_ref = ins
        x1_ref, zb_ref, hs_ref, h1_ref, yb_ref = outs
        xbe_scr, a_scr, b_scr, carry_scr = scr

        @pl.when(i == 0)
        def _():
            xbe_scr[0:SUBLANES, :] = jnp.zeros((SUBLANES, BW), F32)
            carry_scr[...] = jnp.zeros_like(carry_scr)

        x1 = x_ref[...] + _dot(ya_ref[...], wouta_ref[...])
        x1_ref[...] = x1
        h = (x1 * _rms(x1) * nw_ref[...]).astype(BF16)
        h1_ref[...] = h
        for k in range(NDEV):
            zb_ref[:, k * CB:(k + 1) * CB] = _dot(h, win_ref[k])
        xbe_scr[SUBLANES:SUBLANES + tm, :] = zb_ref[:, :BW]
        for hh in range(BH):
            cs = slice(hh * HD, (hh + 1) * HD)
            xc = _conv(p8_ref, cs, xbe_scr[SUBLANES:SUBLANES + tm, cs], xbe_scr[7:7 + tm, cs],
                       xbe_scr[6:6 + tm, cs], xbe_scr[5:5 + tm, cs])
            _, ig, _, a, mult, _ = _gates(p8_ref, gcat_ref, hh, xc)
            a_scr[:, cs] = a
            b_scr[:, cs] = mult * (ig * xc)
        xbe_scr[0:SUBLANES, :] = xbe_scr[tm:tm + SUBLANES, :]
        carry_scr[...] = _scan_rows(a_scr, b_scr, hs_ref, carry_scr[...], tm, False)
        for hh in range(BH):
            cs = slice(hh * HD, (hh + 1) * HD)
            gt = zb_ref[:, BW + hh * HD:BW + (hh + 1) * HD]
            yb_ref[:, cs] = (hs_ref[:, cs] * (gt * _sigmoid(gt))).astype(BF16)

    tile = lambda w: pl.BlockSpec((tm, w), lambda i: (i, 0))
    return _call(
        main, jobs, name="fwd_b", grid=(nt,), relay_step=relay_step,
        ins=[x, ya, wout_a, nw, win8, p8, gcat], in_specs=[tile(D), tile(AW), _VMEM, _VMEM, _VMEM, _VMEM, _VMEM],
        out_shape=[_sds((s_len, D), F32), _sds((s_len, 2 * BW), F32), _sds((s_len, BW), F32), _sds((s_len, D), BF16),
                   _sds((s_len, BW), BF16)],
        out_specs=[tile(D), tile(2 * BW), tile(BW), tile(D), tile(BW)],
        scratch=[pltpu.VMEM((tm + SUBLANES, BW), F32), pltpu.VMEM((tm, BW), F32), pltpu.VMEM((tm, BW), F32),
                 pltpu.VMEM((1, BW), F32)])


def _head(x1, yb, wout, nfw, tgt, *, tm):
    s_len = x1.shape[0]

    def main(i, ins, outs, scr):
        x1_ref, yb_ref, wout_ref, nfw_ref, t_ref = ins
        dx2_ref, dx2b_ref, loss_ref, gnfw_ref = outs

        @pl.when(i == 0)
        def _():
            loss_ref[...] = jnp.zeros_like(loss_ref)
            gnfw_ref[...] = jnp.zeros_like(gnfw_ref)

        x2 = x1_ref[...] + _dot(yb_ref[...], wout_ref[...])
        rf = _rms(x2)
        xn = x2 * rf
        e = xn * nfw_ref[...] - t_ref[...]
        loss_ref[...] += (0.5 / D) * jnp.sum(jnp.sum(e * e, axis=-1, keepdims=True), axis=0, keepdims=True)
        dyf = e * (1.0 / D)
        gnfw_ref[...] += _rowsum(dyf * xn)
        dx2 = _rms_bwd(dyf, x2, rf, nfw_ref[...])
        dx2_ref[...] = dx2
        dx2b_ref[...] = dx2.astype(BF16)

    tile = lambda w: pl.BlockSpec((tm, w), lambda i: (i, 0))
    whole = lambda *s: pl.BlockSpec(s, lambda i: (0,) * len(s))
    (dx2, dx2b, loss, gnfw), _ = _call(
        main, [], name="head", grid=(s_len // tm,),
        ins=[x1, yb, wout, nfw, tgt], in_specs=[tile(D), tile(BW), _VMEM, _VMEM, tile(D)],
        out_shape=[_sds((s_len, D), F32), _sds((s_len, D), BF16), _sds((1, 1), F32), _sds((1, D), F32)],
        out_specs=[tile(D), tile(D), whole(1, 1), whole(1, D)], scratch=[])
    return dx2, dx2b, loss, gnfw


def _bwd_b(dx2, zb, hs, x1, nw, win8, p8, gcat, wout, *, tm):
    s_len = x1.shape[0]
    nt = s_len // tm
    per = tm // SUBLANES

    def main(i, ins, outs, scr):
        dx2_ref, zb_ref, zbp_ref, hs_ref, hsp_ref, x1_ref, nw_ref, win_ref, p8_ref, gcat_ref, wout_ref = ins
        dx1_ref, dx1b_ref, dzb_ref, gp8_ref, gga_ref, ggx_ref, gnw_ref = outs
        (xbe_scr, hse_scr, ae_scr, an_scr, r_scr, i_scr, m_scr, xc_scr, cc_scr, dhd_scr, dh_scr, dy_scr, dxce_scr,
         carry_scr, afirst_scr) = scr
        ti = nt - 1 - i

        @pl.when(i == 0)
        def _():
            gp8_ref[...] = jnp.zeros_like(gp8_ref)
            gga_ref[...] = jnp.zeros_like(gga_ref)
            ggx_ref[...] = jnp.zeros_like(ggx_ref)
            gnw_ref[...] = jnp.zeros_like(gnw_ref)
            dxce_scr[tm:tm + SUBLANES, :] = jnp.zeros((SUBLANES, BW), F32)
            carry_scr[...] = jnp.zeros_like(carry_scr)
            afirst_scr[...] = jnp.zeros_like(afirst_scr)

        has_prev = (ti > 0).astype(F32)
        xbe_scr[0:SUBLANES, :] = zbp_ref[:, :BW] * has_prev
        xbe_scr[SUBLANES:SUBLANES + tm, :] = zb_ref[:, :BW]
        hse_scr[0:SUBLANES, :] = hsp_ref[...] * has_prev
        hse_scr[SUBLANES:SUBLANES + tm, :] = hs_ref[...]

        dx2 = dx2_ref[...]
        dy_scr[...] = _dot_nt(dx2.astype(BF16), wout_ref[...])

        for hh in range(BH):
            cs = slice(hh * HD, (hh + 1) * HD)
            xc = _conv(p8_ref, cs, xbe_scr[SUBLANES:SUBLANES + tm, cs], xbe_scr[7:7 + tm, cs],
                       xbe_scr[6:6 + tm, cs], xbe_scr[5:5 + tm, cs])
            r, ig, _, a, mult, rm = _gates(p8_ref, gcat_ref, hh, xc)
            cc_scr[:, cs] = a * hse_scr[7:7 + tm, cs] - (ig * xc) * (a * a * rm)
            xc_scr[:, cs] = xc
            r_scr[:, cs] = r
            i_scr[:, cs] = ig
            m_scr[:, cs] = mult
            ae_scr[0:tm, cs] = a
            gt = zb_ref[:, BW + hh * HD:BW + (hh + 1) * HD]
            sig = _sigmoid(gt)
            dy = dy_scr[:, cs]
            dhd_scr[:, cs] = dy * (gt * sig)
            dzb_ref[:, BW + hh * HD:BW + (hh + 1) * HD] = (
                dy * hs_ref[:, cs] * (sig * (1.0 + gt * (1.0 - sig)))).astype(BF16)
        ae_scr[tm:tm + SUBLANES, :] = jnp.broadcast_to(afirst_scr[...], (SUBLANES, BW))
        an_scr[...] = ae_scr[1:1 + tm, :]
        afirst_scr[...] = ae_scr[0:1, :]
        carry_scr[...] = _scan_rows(an_scr, dhd_scr, dh_scr, carry_scr[...], tm, True)

        for hh in range(BH):
            cs = slice(hh * HD, (hh + 1) * HD)
            dh = dh_scr[:, cs]
            mult = m_scr[:, cs]
            ig = i_scr[:, cs]
            r = r_scr[:, cs]
            xc = xc_scr[:, cs]
            lam = p8_ref[7:8, cs]
            sp = _softplus_neg(lam)
            dla = dh * cc_scr[:, cs]
            gp8_ref[7:8, cs] += _rowsum(dla * ((-RG_C) * r)) * (-_sigmoid(-lam))
            dpr = dla * ((-RG_C) * sp) * (r * (1.0 - r))
            dpi = dh * mult * xc * (ig * (1.0 - ig))
            gp8_ref[5:6, cs] += _rowsum(dpr)
            gp8_ref[6:7, cs] += _rowsum(dpi)
            dcat = jnp.concatenate([dpr, dpi], axis=1).astype(BF16)
            dxc = dh * mult * ig + _dot_nt(dcat, gcat_ref[hh])
            gg = _dot(xc.T.astype(BF16), dcat)
            gga_ref[hh] += gg[:, :HD]
            ggx_ref[hh] += gg[:, HD:]
            dxce_scr[0:tm, cs] = dxc
            gp8_ref[4:5, cs] += _rowsum(dxc)
            gp8_ref[3:4, cs] += _rowsum(dxc * xbe_scr[SUBLANES:SUBLANES + tm, cs])
            gp8_ref[2:3, cs] += _rowsum(dxc * xbe_scr[7:7 + tm, cs])
            gp8_ref[1:2, cs] += _rowsum(dxc * xbe_scr[6:6 + tm, cs])
            gp8_ref[0:1, cs] += _rowsum(dxc * xbe_scr[5:5 + tm, cs])
        for hh in range(BH):
            cs = slice(hh * HD, (hh + 1) * HD)
            dxb = p8_ref[3:4, cs] * dxce_scr[0:tm, cs]
            dxb = dxb + p8_ref[2:3, cs] * dxce_scr[1:1 + tm, cs]
            dxb = dxb + p8_ref[1:2, cs] * dxce_scr[2:2 + tm, cs]
            dxb = dxb + p8_ref[0:1, cs] * dxce_scr[3:3 + tm, cs]
            dzb_ref[:, cs] = dxb.astype(BF16)
        dxce_scr[tm:tm + SUBLANES, :] = dxce_scr[0:SUBLANES, :]

        dh1 = jnp.zeros((tm, D), F32)
        for k in range(NDEV):
            dh1 = dh1 + _dot_nt(dzb_ref[:, k * CB:(k + 1) * CB], win_ref[k])
        x1 = x1_ref[...]
        r1 = _rms(x1)
        dx1 = dx2 + _rms_bwd(dh1, x1, r1, nw_ref[...])
        dx1_ref[...] = dx1
        dx1b_ref[...] = dx1.astype(BF16)
        gnw_ref[...] += _rowsum(dh1 * x1 * r1)

    tile = lambda w: pl.BlockSpec((tm, w), lambda i: (nt - 1 - i, 0))
    prev = lambda w: pl.BlockSpec((SUBLANES, w), lambda i: (jnp.maximum((nt - 1 - i) * per - 1, 0), 0))
    whole = lambda *s: pl.BlockSpec(s, lambda i: (0,) * len(s))
    full = lambda: pltpu.VMEM((tm, BW), F32)
    ext = lambda: pltpu.VMEM((tm + SUBLANES, BW), F32)
    out, _ = _call(
        main, [], name="bwd_b", grid=(nt,),
        ins=[dx2, zb, zb, hs, hs, x1, nw, win8, p8, gcat, wout],
        in_specs=[tile(D), tile(2 * BW), prev(2 * BW), tile(BW), prev(BW), tile(D), _VMEM, _VMEM, _VMEM, _VMEM, _VMEM],
        out_shape=[_sds((s_len, D), F32), _sds((s_len, D), BF16), _sds((s_len, 2 * BW), BF16), _sds((SUBLANES, BW), F32),
                   _sds((BH, HD, HD), F32), _sds((BH, HD, HD), F32), _sds((1, D), F32)],
        out_specs=[tile(D), tile(D), tile(2 * BW), whole(SUBLANES, BW), whole(BH, HD, HD), whole(BH, HD, HD),
                   whole(1, D)],
        scratch=[ext(), ext(), ext(), full(), full(), full(), full(), full(), full(), full(), full(), full(), ext(),
                 pltpu.VMEM((1, BW), F32), pltpu.VMEM((1, BW), F32)])
    return out


def _transpose_into(dst_ref, src_ref, rows):
    s_len = src_ref.shape[0]
    for r0 in range(0, s_len, rows):
        dst_ref[:, r0:r0 + rows] = src_ref[r0:r0 + rows, :].astype(F32).T.astype(BF16)


def _wgrad(a, b, jobs, *, by_rows, per, name, relay_step=0):
    s_len, m = a.shape
    n = b.shape[1]
    r, cd = (m // NDEV, n) if by_rows else (m, n // NDEV)
    nsteps = NDEV // per
    at_rows = per * r if by_rows else m

    def main(i, ins, outs, scr):
        a_ref, b_ref = ins
        q_ref, acc_ref = outs
        at_scr, stage, mine, land, send_sems, recv_sems = scr
        x, y, c = _place()

        def to_sibling(pi):
            return pltpu.make_async_remote_copy(
                src_ref=stage.at[pi & 1], dst_ref=land.at[pi], send_sem=send_sems.at[pi], recv_sem=recv_sems.at[pi],
                device_id=(x, y, 1 - c), device_id_type=MESH)

        if by_rows:
            _transpose_into(at_scr, a_ref, 256)
        else:
            @pl.when(i == 0)
            def _():
                _transpose_into(at_scr, a_ref, 256)

        res = _dot(at_scr[...], b_ref[...]).astype(BF16)
        for k in range(per):
            blk = per * i + k
            pi, pc = blk >> 1, blk & 1
            val = res[k * r:(k + 1) * r, :] if by_rows else res

            @pl.when(pc != c)
            def _():
                @pl.when(pi >= 2)
                def _():
                    to_sibling(pi - 2).wait_send()

                stage[pi & 1] = val
                to_sibling(pi).start()

            @pl.when(pc == c)
            def _():
                mine[pi] = val

        @pl.when(i == nsteps - 1)
        def _():
            for p in range(4):
                to_sibling(p).wait_recv()
            to_sibling(2).wait_send()
            to_sibling(3).wait_send()
            _chip_sums(mine, land, q_ref, acc_ref, x, y)

    if by_rows:
        in_specs = [pl.BlockSpec((s_len, at_rows), lambda j: (0, j)), _VMEM]
    else:
        in_specs = [_VMEM, pl.BlockSpec((s_len, cd), lambda j: (0, j))]
    blk_vmem = lambda k: pltpu.VMEM((k, r, cd), BF16)
    (q, acc), job_out = _call(
        main, jobs, name=name, grid=(nsteps,), relay_step=relay_step, ins=[a, b], in_specs=in_specs,
        out_shape=[_sds((NCHIP_OTHER, r, cd), BF16), _sds((r, cd), F32)],
        out_specs=[pl.BlockSpec((NCHIP_OTHER, r, cd), lambda j: (0, 0, 0)), pl.BlockSpec((r, cd), lambda j: (0, 0))],
        scratch=[pltpu.VMEM((at_rows, s_len), BF16), blk_vmem(2), blk_vmem(4), blk_vmem(4),
                 pltpu.SemaphoreType.DMA((4,)), pltpu.SemaphoreType.DMA((4,))])
    return q, acc, job_out


def _adam_math(w, g, m, v):
    m = B1 * m + (1.0 - B1) * g
    v = B2 * v + (1.0 - B2) * (g * g)
    m_hat = m / (1.0 - B1 ** STEP)
    v_hat = v / (1.0 - B2 ** STEP)
    delta = (-LR) * (m_hat / (jnp.sqrt(v_hat) + ADAM_EPS) + WD * w)
    return delta, m, v


def _adam_big(w, acc, land, m, v, name):
    r, cd = w.shape
    rb = 256 if r % 256 == 0 else r
    nland = land.shape[0]

    def body(w_ref, acc_ref, land_ref, m_ref, v_ref, g_ref, d_ref, mo_ref, vo_ref):
        g = acc_ref[...]
        for j in range(nland):
            g = g + land_ref[j].astype(F32)
        g_ref[...] = g
        d_ref[...], mo_ref[...], vo_ref[...] = _adam_math(w_ref[...], g, m_ref[...], v_ref[...])

    blk = pl.BlockSpec((rb, cd), lambda i: (i, 0))
    blk3 = pl.BlockSpec((nland, rb, cd), lambda i: (0, i, 0))
    return pl.pallas_call(
        body, name=name, grid=(r // rb,), in_specs=[blk, blk, blk3, blk, blk], out_specs=[blk] * 4,
        out_shape=[_sds((r, cd), F32)] * 4,
        compiler_params=_params(dimension_semantics=("arbitrary",)),
    )(w, acc, land, m, v)


def _adam_small(groups):
    n = len(groups)

    def body(*refs):
        ins, outs = refs[:4 * n], refs[4 * n:]
        for k in range(n):
            w_ref, g_ref, m_ref, v_ref = ins[4 * k:4 * k + 4]
            d, mo, vo = _adam_math(w_ref[...], g_ref[...], m_ref[...], v_ref[...])
            outs[3 * k][...] = d
            outs[3 * k + 1][...] = mo
            outs[3 * k + 2][...] = vo

    flat = [a for grp in groups for a in grp]
    shapes = [_sds(grp[0].shape, F32) for grp in groups for _ in range(3)]
    res = pl.pallas_call(
        body, name="adam_small", in_specs=[_VMEM] * (4 * n), out_specs=[_VMEM] * (3 * n), out_shape=shapes,
        compiler_params=_params(),
    )(*flat)
    return [tuple(res[3 * k:3 * k + 3]) for k in range(n)]


TM_FWD_A = 256
RELAY_STEP_FWD_A = 4
RELAY_STEP_FWD_B = 2
TM_BWD_A = 256
TM_BWD_A_IN = 256
RELAY_STEP_BWD_A_IN = 4
TM_FWD_B = 256
TM_HEAD = 512
TM_BWD_B = 256


def _pack(parts, rows):
    flat = jnp.concatenate([p.reshape(-1) for p in parts])
    return jnp.pad(flat, (0, NDEV * rows * LANES - flat.shape[0])).reshape(NDEV, rows, LANES)


def _unpack(packed, shapes):
    flat, out, off = packed.reshape(-1), [], 0
    for s in shapes:
        size = 1
        for d in s:
            size *= d
        out.append(flat[off:off + size].reshape(s))
        off += size
    return out


def kernel(x, norm_w, a_w_in, a_ln_w, a_ln_b, a_w_s, a_b_s, a_w_out, b_w_in, b_conv_w, b_conv_b, b_gate_a_w, b_gate_a_b, b_gate_x_w, b_gate_x_b, b_lambda, b_w_out, norm_f_w, loss_target, m_norm_w, m_a_w_in, m_a_ln_w, m_a_ln_b, m_a_w_s, m_a_b_s, m_a_w_out, m_b_w_in, m_b_conv_w, m_b_conv_b, m_b_gate_a_w, m_b_gate_a_b, m_b_gate_x_w, m_b_gate_x_b, m_b_lambda, m_b_w_out, m_norm_f_w, v_norm_w, v_a_w_in, v_a_ln_w, v_a_ln_b, v_a_w_s, v_a_b_s, v_a_w_out, v_b_w_in, v_b_conv_w, v_b_conv_b, v_b_gate_a_w, v_b_gate_a_b, v_b_gate_x_w, v_b_gate_x_b, v_b_lambda, v_b_w_out, v_norm_f_w):
    me = 4 * lax.axis_index("x") + 2 * lax.axis_index("y") + lax.axis_index("c")
    xs, tgt = x[0], loss_target[0]
    nw0, nw1, nfw = norm_w[0:1], norm_w[1:2], norm_f_w.reshape(1, D)
    w_s, bst = a_w_s[0], a_b_s[0].T
    gcat = jnp.concatenate([b_gate_a_w[0], b_gate_x_w[0]], axis=-1).astype(BF16)

    p8_shard = jnp.concatenate([b_conv_w[0], b_conv_b, b_gate_a_b, b_gate_x_b, b_lambda], axis=0)
    ((win_a8, p8_all),) = _comm_only([_Gather([a_w_in[0].astype(BF16), p8_shard])], "gather_first")
    p8 = jnp.transpose(p8_all, (1, 0, 2)).reshape(SUBLANES, BW)

    (z, h0, ya), ((wout_a8, win_b8),) = _fwd_a(
        xs, nw0, win_a8, a_ln_w, a_ln_b, w_s, bst, [_Gather([a_w_out[0].astype(BF16), b_w_in[0].astype(BF16)])],
        tm=TM_FWD_A, relay_step=RELAY_STEP_FWD_A)
    wout_a = wout_a8.reshape(AW, D)
    (x1, zb, hs, h1, yb), ((wout_b8,),) = _fwd_b(
        xs, ya, wout_a, nw1, win_b8, p8, gcat, [_Gather([b_w_out[0].astype(BF16)])],
        tm=TM_FWD_B, relay_step=RELAY_STEP_FWD_B)
    wout_b = wout_b8.reshape(BW, D)
    dx2, dx2b, loss, g_nfw = _head(x1, yb, wout_b, nfw, tgt, tm=TM_HEAD)

    dx1, dx1b, dzb, g_p8, g_ga, g_gx, g_nw1 = _bwd_b(dx2, zb, hs, x1, nw1, win_b8, p8, gcat, wout_b, tm=TM_BWD_B)
    q_wout_b, acc_wout_b, _ = _wgrad(yb, dx2b, [], by_rows=True, per=2, name="wgrad_b_out")
    shapes_b = [(1, D), (1, D), (SUBLANES, BW), (1, 1)]
    pack_b = _pack([g_nfw, g_nw1, g_p8, loss], 16)
    small_b = _InChip([g_ga.reshape(NDEV, -1, HD), g_gx.reshape(NDEV, -1, HD), pack_b])
    q_win_b, acc_win_b, (sm_b,) = _wgrad(h1, dzb, [small_b], by_rows=False, per=1, name="wgrad_b_in")
    qs_b, accs_b = [q_win_b, q_wout_b, *sm_b[:3]], [acc_win_b, acc_wout_b, *sm_b[3:]]

    (dz, g_lnw, g_lnb, g_ws, g_bst), (lands_b,) = _bwd_a(
        dx1b, z, a_ln_w, a_ln_b, w_s, bst, wout_a, [_Exchange(qs_b)], tm=TM_BWD_A)
    shapes_a = [(1, AW), (1, AW), (CH, G)]
    pack_a = _pack([g_lnw, g_lnb, g_bst], 8)
    q_wout_a, acc_wout_a, (red_b, sm_a) = _wgrad(
        ya, dx1b, [_SumGather(accs_b[2:], lands_b[2:]), _InChip([g_ws, pack_a])], by_rows=True, per=1,
        name="wgrad_a_out", relay_step=2)
    qs_a, accs_a = [q_wout_a, *sm_a[:2]], [acc_wout_a, *sm_a[2:]]
    q_win_a, acc_win_a, (lands_a,) = _wgrad(h0, dz, [_Exchange(qs_a)], by_rows=False, per=1, name="wgrad_a_in")
    (gx, g_nw0), (red_a, (l_win_a,)) = _bwd_a_in(
        dz, dx1, xs, nw0, win_a8, [_SumGather(accs_a[1:], lands_a[1:]), _ExchangeVia(q_win_a)],
        tm=TM_BWD_A_IN, relay_step=RELAY_STEP_BWD_A_IN)
    g_nw0 = _allreduce_direct(g_nw0, "allreduce_norm_w0")

    r_ga, r_gx, r_pack_b = red_b
    r_nfw, r_nw1, r_p8, loss = _unpack(r_pack_b, shapes_b)
    r_ws, r_pack_a = red_a
    r_lnw, r_lnb, r_bst = _unpack(r_pack_a, shapes_a)
    g_p8 = lax.dynamic_slice_in_dim(r_p8, me * (BW // NDEV), BW // NDEV, axis=1)
    loss = loss[0, 0]

    weights = dict(norm_w=norm_w, a_w_in=a_w_in, a_ln_w=a_ln_w, a_ln_b=a_ln_b, a_w_s=a_w_s, a_b_s=a_b_s, a_w_out=a_w_out,
                   b_w_in=b_w_in, b_conv_w=b_conv_w, b_conv_b=b_conv_b, b_gate_a_w=b_gate_a_w, b_gate_a_b=b_gate_a_b,
                   b_gate_x_w=b_gate_x_w, b_gate_x_b=b_gate_x_b, b_lambda=b_lambda, b_w_out=b_w_out, norm_f_w=norm_f_w)
    mom1 = dict(norm_w=m_norm_w, a_w_in=m_a_w_in, a_ln_w=m_a_ln_w, a_ln_b=m_a_ln_b, a_w_s=m_a_w_s, a_b_s=m_a_b_s,
                a_w_out=m_a_w_out, b_w_in=m_b_w_in, b_conv_w=m_b_conv_w, b_conv_b=m_b_conv_b, b_gate_a_w=m_b_gate_a_w,
                b_gate_a_b=m_b_gate_a_b, b_gate_x_w=m_b_gate_x_w, b_gate_x_b=m_b_gate_x_b, b_lambda=m_b_lambda,
                b_w_out=m_b_w_out, norm_f_w=m_norm_f_w)
    mom2 = dict(norm_w=v_norm_w, a_w_in=v_a_w_in, a_ln_w=v_a_ln_w, a_ln_b=v_a_ln_b, a_w_s=v_a_w_s, a_b_s=v_a_b_s,
                a_w_out=v_a_w_out, b_w_in=v_b_w_in, b_conv_w=v_b_conv_w, b_conv_b=v_b_conv_b, b_gate_a_w=v_b_gate_a_w,
                b_gate_a_b=v_b_gate_a_b, b_gate_x_w=v_b_gate_x_w, b_gate_x_b=v_b_gate_x_b, b_lambda=v_b_lambda,
                b_w_out=v_b_w_out, norm_f_w=v_norm_f_w)
    names = list(weights)

    def as2d(a):
        return a.reshape(-1, a.shape[-1])

    upd, grads = {}, {}
    for k, acc, land in (("a_w_in", acc_win_a, l_win_a), ("a_w_out", accs_a[0], lands_a[0]),
                         ("b_w_in", accs_b[0], lands_b[0]), ("b_w_out", accs_b[1], lands_b[1])):
        g, d, mo, vo = _adam_big(as2d(weights[k]), acc, land, as2d(mom1[k]), as2d(mom2[k]), "adam_" + k)
        grads[k] = g[None]
        upd[k] = (d, mo, vo)
    grads.update(
        norm_w=jnp.concatenate([g_nw0, r_nw1], axis=0), a_ln_w=r_lnw, a_ln_b=r_lnb,
        a_w_s=r_ws.reshape(1, G, CH, CH), a_b_s=r_bst.T[None],
        b_conv_w=g_p8[None, 0:4], b_conv_b=g_p8[4:5], b_gate_a_w=r_ga.reshape(1, BH, HD, HD), b_gate_a_b=g_p8[5:6],
        b_gate_x_w=r_gx.reshape(1, BH, HD, HD), b_gate_x_b=g_p8[6:7], b_lambda=g_p8[7:8], norm_f_w=r_nfw.reshape(D))
    small_names = [k for k in names if k not in upd]
    res = _adam_small([(as2d(weights[k]), as2d(grads[k]), as2d(mom1[k]), as2d(mom2[k])) for k in small_names])
    for k, r3 in zip(small_names, res):
        upd[k] = r3
    deltas = [upd[k][0].reshape(weights[k].shape) for k in names]
    new_m = [upd[k][1].reshape(weights[k].shape) for k in names]
    new_v = [upd[k][2].reshape(weights[k].shape) for k in names]
    return (loss, gx[None], *[grads[k] for k in names], *deltas, *new_m, *new_v)
```

```python
import jax
import jax.numpy as jnp
from jax import lax
from jax.experimental import pallas as pl
from jax.experimental.pallas import tpu as pltpu

F32 = jnp.float32
BF16 = jnp.bfloat16
MESH = pl.DeviceIdType.MESH

NDEV = 8
NCHIP_OTHER = 3
D = 1024
AW = 2048
G = 8
GD = AW // G
CH = 128
BW = 1536
BH = 12
HD = BW // BH
CA = 3 * AW // NDEV
CB = 2 * BW // NDEV
RMS_EPS = 1e-6
LN_EPS = 1e-5
RG_C = 8.0
LR, B1, B2, ADAM_EPS, WD, STEP = 0.001, 0.9, 0.999, 1e-08, 0.01, 10
V7X_VMEM_BYTES = 64 * 1024 * 1024
VMEM_LIMIT = V7X_VMEM_BYTES - 8 * 1024 * 1024
SUBLANES = 8
LANES = 128
BF16_ROWS = 16
GELU_C = 0.7978845608028654
GELU_K = 0.044715

_VMEM = pl.BlockSpec(memory_space=pltpu.VMEM)
_HBM = pl.BlockSpec(memory_space=pltpu.HBM)


def _sds(shape, dtype):
    return jax.ShapeDtypeStruct(tuple(shape), dtype)


def _params(**kw):
    return pltpu.CompilerParams(vmem_limit_bytes=VMEM_LIMIT, **kw)


def _gelu_t(z):
    t = jnp.tanh(GELU_C * (z + GELU_K * (z * z * z)))
    return 0.5 * z * (1.0 + t), t


def _dgelu(z, t):
    return 0.5 * (1.0 + t) + 0.5 * z * (1.0 - t * t) * (GELU_C * (1.0 + 3.0 * GELU_K * z * z))


def _sigmoid(v):
    return 0.5 * jnp.tanh(0.5 * v) + 0.5


def _softplus_neg(lam):
    return jnp.maximum(-lam, 0.0) + jnp.log1p(jnp.exp(-jnp.abs(lam)))


def _dot(a, b):
    return jnp.dot(a, b, preferred_element_type=F32)


def _dot_nt(a, b):
    return lax.dot_general(a, b, (((1,), (1,)), ((), ())), preferred_element_type=F32)


def _rowsum(v):
    return jnp.sum(v, axis=0, keepdims=True)


def _causal_mask():
    r = lax.broadcasted_iota(jnp.int32, (CH, CH), 0)
    c = lax.broadcasted_iota(jnp.int32, (CH, CH), 1)
    return r >= c


def _rms(x):
    return lax.rsqrt(jnp.mean(x * x, axis=-1, keepdims=True) + RMS_EPS)


def _rms_bwd(dh, x, r, nw):
    gy = dh * nw
    return r * gy - x * (r * r * r) * jnp.mean(gy * x, axis=-1, keepdims=True)


def _place():
    return lax.axis_index("x"), lax.axis_index("y"), lax.axis_index("c")


def _other_chips(x, y):
    return [(1 - x, y), (x, 1 - y), (1 - x, 1 - y)]


GATHER_SLOTS = 10


def _gather_ops(ins, outs, send_sems, recv_sems, local_sems):
    n = len(ins)
    x, y, c = _place()
    sibling = (x, y, 1 - c)
    xn, yn, dg = _other_chips(x, y)
    split = [ins[i].shape[0] % (2 * BF16_ROWS) == 0 for i in range(n)]

    def blk(chip, core):
        return 4 * chip[0] + 2 * chip[1] + core

    me = blk((x, y), c)

    def part(ref, i, half):
        if half is None:
            return ref
        h = ins[i].shape[0] // 2
        return ref.at[pl.ds(half * h, h)]

    def copy(i, k, block, to, half=None, src=None):
        dst = part(outs[i].at[block], i, half)
        return pltpu.make_async_remote_copy(
            src_ref=dst if src is None else part(src, i, half), dst_ref=dst,
            send_sem=send_sems.at[k, i], recv_sem=recv_sems.at[k, i], device_id=to, device_id_type=MESH)

    def first_copies():
        mine = [pltpu.make_async_copy(ins[i], outs[i].at[me], local_sems.at[i]) for i in range(n)]
        first = []
        for i in range(n):
            first.append(copy(i, 0, me, sibling, src=ins[i]))
            if split[i]:
                first.append(copy(i, 1, me, (*xn, c), 0, ins[i]))
                first.append(copy(i, 3, me, (*yn, c), 1, ins[i]))
                first.append(copy(i, 2, me, (*xn, c), 1, ins[i]))
                first.append(copy(i, 4, me, (*yn, c), 0, ins[i]))
            else:
                first.append(copy(i, 1, me, (*xn, c), None, ins[i]))
                first.append(copy(i, 3, me, (*yn, c), None, ins[i]))
                first.append(copy(i, 5, me, (*dg, c), None, ins[i]))
        return mine, first

    def onward():
        out = []
        for i in range(n):
            if split[i]:
                out.append(copy(i, 5, blk(xn, c), (*yn, c), 0))
                out.append(copy(i, 6, blk(yn, c), (*xn, c), 1))
        return out

    def start():
        mine, first = first_copies()
        for cp in mine + first:
            cp.start()

    def relay():
        sends = onward()
        for i in range(n):
            if split[i]:
                copy(i, 1, blk(xn, c), sibling, 0).wait_recv()
                sends.pop(0).start()
                copy(i, 3, blk(yn, c), sibling, 1).wait_recv()
                sends.pop(0).start()

    def finish():
        mine, first = first_copies()
        passed = []

        def pass_on(i, j, chip):
            fwd = copy(i, 7 + j, blk(chip, c), sibling)
            fwd.start()
            passed.append(fwd)

        for i in range(n):
            if split[i]:
                copy(i, 2, blk(xn, c), sibling, 1).wait_recv()
                pass_on(i, 0, xn)
                copy(i, 4, blk(yn, c), sibling, 0).wait_recv()
                pass_on(i, 1, yn)
                copy(i, 5, blk(dg, c), sibling, 0).wait_recv()
                copy(i, 6, blk(dg, c), sibling, 1).wait_recv()
                pass_on(i, 2, dg)
            else:
                copy(i, 1, blk(xn, c), sibling).wait_recv()
                pass_on(i, 0, xn)
                copy(i, 3, blk(yn, c), sibling).wait_recv()
                pass_on(i, 1, yn)
                copy(i, 5, blk(dg, c), sibling).wait_recv()
                pass_on(i, 2, dg)
        for i in range(n):
            copy(i, 0, blk((x, y), 1 - c), sibling).wait_recv()
            for j, chip in enumerate((xn, yn, dg)):
                copy(i, 7 + j, blk(chip, 1 - c), sibling).wait_recv()
        for cp in first + passed + onward():
            cp.wait_send()
        for cp in mine:
            cp.wait()

    return start, relay, finish


def _gather_sems(n):
    return [pltpu.SemaphoreType.DMA((GATHER_SLOTS, n)), pltpu.SemaphoreType.DMA((GATHER_SLOTS, n)),
            pltpu.SemaphoreType.DMA((n,))]


class _Gather:
    def __init__(self, shards):
        n = len(shards)
        self.ins, self.in_specs = list(shards), [_HBM] * n
        self.out_shape = [_sds((NDEV,) + s.shape, s.dtype) for s in shards]
        self.out_specs = [_HBM] * n
        self.scratch = _gather_sems(n)

    def ops(self, ins, outs, scr):
        return _gather_ops(ins, outs, *scr)


class _Exchange:
    def __init__(self, qs):
        n = len(qs)
        self.ins, self.in_specs = list(qs), [_HBM] * n
        self.out_shape = [_sds(q.shape, q.dtype) for q in qs]
        self.out_specs = [_HBM] * n
        self.scratch = [pltpu.SemaphoreType.DMA((NCHIP_OTHER, n)), pltpu.SemaphoreType.DMA((NCHIP_OTHER, n))]

    def ops(self, ins, outs, scr):
        send_sems, recv_sems = scr
        n = len(ins)
        x, y, c = _place()
        chips = _other_chips(x, y)

        def copies():
            return [pltpu.make_async_remote_copy(
                src_ref=ins[i].at[j], dst_ref=outs[i].at[j], send_sem=send_sems.at[j, i],
                recv_sem=recv_sems.at[j, i], device_id=(*chips[j], c), device_id_type=MESH)
                for i in range(n) for j in range(NCHIP_OTHER)]

        def start():
            for cp in copies():
                cp.start()

        def finish():
            cps = copies()
            for cp in cps:
                cp.wait_recv()
            for cp in cps:
                cp.wait_send()

        return start, lambda: None, finish


class _ExchangeVia:
    def __init__(self, q):
        _, r, cd = q.shape
        half = (2, r // 2, cd)
        self.ins, self.in_specs = [q], [_HBM]
        self.out_shape, self.out_specs = [_sds((2, r, cd), q.dtype)], [_HBM]
        self.scratch = [pltpu.VMEM(half, q.dtype), pltpu.VMEM(half, q.dtype), pltpu.VMEM(half, q.dtype),
                        pltpu.SemaphoreType.DMA((6,)), pltpu.SemaphoreType.DMA((6,)), pltpu.SemaphoreType.DMA((2,))]

    def ops(self, ins, outs, scr):
        (q,), (land,) = ins, outs
        relayed, own, comb, send_sems, recv_sems, local_sems = scr
        h = q.shape[1] // 2
        x, y, c = _place()
        xn, yn, _ = _other_chips(x, y)
        h0, h1 = pl.ds(0, h), pl.ds(h, h)

        def remote(k, src, dst, chip):
            return pltpu.make_async_remote_copy(src_ref=src, dst_ref=dst, send_sem=send_sems.at[k],
                                                recv_sem=recv_sems.at[k], device_id=(*chip, c), device_id_type=MESH)

        def via():
            return [remote(2, q.at[2, h0], relayed.at[0], xn), remote(3, q.at[2, h1], relayed.at[1], yn)]

        def direct():
            return [remote(0, q.at[0, h0], land.at[0, h0], xn), remote(1, q.at[1, h1], land.at[1, h1], yn)]

        def second():
            return [remote(4, comb.at[0], land.at[1, h0], yn), remote(5, comb.at[1], land.at[0, h1], xn)]

        def mine():
            return [pltpu.make_async_copy(q.at[1, h0], own.at[0], local_sems.at[0]),
                    pltpu.make_async_copy(q.at[0, h1], own.at[1], local_sems.at[1])]

        def start():
            for cp in via() + direct() + mine():
                cp.start()

        def relay():
            arrived, loaded, onward = via(), mine(), second()
            for k in range(2):
                arrived[k].wait_recv()
                loaded[k].wait()
                comb[k] = (own[k].astype(F32) + relayed[k].astype(F32)).astype(comb.dtype)
                onward[k].start()

        def finish():
            landing = direct() + second()
            for cp in landing:
                cp.wait_recv()
            for cp in via() + landing:
                cp.wait_send()

        return start, relay, finish


class _SumGather:
    def __init__(self, accs, lands):
        n = len(accs)
        self.n = n
        self.ins, self.in_specs = list(accs) + list(lands), [_VMEM] * (2 * n)
        self.out_shape = [_sds((NDEV,) + a.shape, a.dtype) for a in accs]
        self.out_specs = [_HBM] * n
        self.scratch = [pltpu.VMEM(a.shape, a.dtype) for a in accs] + _gather_sems(n)

    def ops(self, ins, outs, scr):
        n = self.n
        accs, lands, mine = ins[:n], ins[n:], scr[:n]
        g_start, relay, finish = _gather_ops(mine, outs, *scr[n:])

        def start():
            for i in range(n):
                mine[i][...] = accs[i][...] + lands[i][0] + lands[i][1] + lands[i][2]
            g_start()

        return start, relay, finish


def _call(main, jobs, *, name, grid, ins, in_specs, out_shape, out_specs, scratch, relay_step=0):
    nsteps = grid[0] if grid else 1
    n_in, n_out, n_scr = len(ins), len(out_shape), len(scratch)

    def body(*refs):
        pos = [0]

        def take(k):
            r = refs[pos[0]:pos[0] + k]
            pos[0] += k
            return r

        m_in = take(n_in)
        j_in = [take(len(j.ins)) for j in jobs]
        m_out = take(n_out)
        j_out = [take(len(j.out_shape)) for j in jobs]
        m_scr = take(n_scr)
        j_scr = [take(len(j.scratch)) for j in jobs]
        ops = [j.ops(a, b, s) for j, a, b, s in zip(jobs, j_in, j_out, j_scr)]
        i = pl.program_id(0) if grid else 0
        if not grid:
            for o in ops:
                o[0]()
            main(i, m_in, m_out, m_scr)
            for o in ops:
                o[1]()
            for o in ops:
                o[2]()
            return

        if ops:
            @pl.when(i == 0)
            def _():
                for o in ops:
                    o[0]()

        main(i, m_in, m_out, m_scr)

        if ops:
            @pl.when(i == min(relay_step, nsteps - 1))
            def _():
                for o in ops:
                    o[1]()

            @pl.when(i == nsteps - 1)
            def _():
                for o in ops:
                    o[2]()

    extra = dict(dimension_semantics=("arbitrary",)) if grid else {}
    res = pl.pallas_call(
        body, name=name, grid=grid,
        in_specs=list(in_specs) + [s for j in jobs for s in j.in_specs],
        out_specs=list(out_specs) + [s for j in jobs for s in j.out_specs],
        out_shape=list(out_shape) + [s for j in jobs for s in j.out_shape],
        scratch_shapes=list(scratch) + [s for j in jobs for s in j.scratch],
        compiler_params=_params(**extra),
    )(*ins, *[a for j in jobs for a in j.ins])
    main_out, rest, job_out = res[:n_out], res[n_out:], []
    for j in jobs:
        k = len(j.out_shape)
        job_out.append(rest[:k])
        rest = rest[k:]
    return main_out, job_out


def _comm_only(jobs, name):
    _, job_out = _call(lambda i, a, b, s: None, jobs, name=name, grid=(), ins=[], in_specs=[], out_shape=[],
                       out_specs=[], scratch=[])
    return job_out


class _InChip:
    def __init__(self, ps):
        n = len(ps)
        self.n = n
        blk = [p.shape[1:] for p in ps]
        self.ins, self.in_specs = list(ps), [_HBM] * n
        self.out_shape = [_sds((NCHIP_OTHER,) + b, p.dtype) for b, p in zip(blk, ps)] + [_sds(b, F32) for b in blk]
        self.out_specs = [_VMEM] * (2 * n)
        self.scratch = ([pltpu.VMEM((4,) + b, p.dtype) for b, p in zip(blk, ps)] * 2
                        + [pltpu.SemaphoreType.DMA((4, n))] * 3)

    def ops(self, ins, outs, scr):
        n = self.n
        q_refs, acc_refs = outs[:n], outs[n:]
        mines, lands = scr[:n], scr[n:2 * n]
        send_sems, recv_sems, local_sems = scr[2 * n:]
        x, y, c = _place()
        sibling = (x, y, 1 - c)

        def copies():
            out = []
            for i in range(n):
                for pi in range(4):
                    loc = pltpu.make_async_copy(ins[i].at[2 * pi + c], mines[i].at[pi], local_sems.at[pi, i])
                    cp = pltpu.make_async_remote_copy(
                        src_ref=ins[i].at[2 * pi + (1 - c)], dst_ref=lands[i].at[pi],
                        send_sem=send_sems.at[pi, i], recv_sem=recv_sems.at[pi, i],
                        device_id=sibling, device_id_type=MESH)
                    out.append((loc, cp))
            return out

        def start():
            for loc, cp in copies():
                loc.start()
                cp.start()

        def finish():
            pairs = copies()
            for loc, cp in pairs:
                loc.wait()
                cp.wait_recv()
            for i in range(n):
                _chip_sums(mines[i], lands[i], q_refs[i], acc_refs[i], x, y)
            for _, cp in pairs:
                cp.wait_send()

        return start, lambda: None, finish


def _chip_sums(mine, land, q_ref, acc_ref, x, y):
    for j, (qx, qy) in enumerate(_other_chips(x, y)):
        qi = 2 * qx + qy
        q_ref[j] = (mine[qi].astype(F32) + land[qi].astype(F32)).astype(q_ref.dtype)
    mi = 2 * x + y
    acc_ref[...] = mine[mi].astype(F32) + land[mi].astype(F32)


def _allreduce_direct(v, name):
    def body(v_ref, o_ref, buf, send_sems, recv_sems):
        x, y, c = _place()
        me = 4 * x + 2 * y + c
        buf[me] = v_ref[...]
        cps = []
        for k in range(1, NDEV):
            fx, fy, fc = (k >> 2) & 1, (k >> 1) & 1, k & 1
            peer = ((1 - x) if fx else x, (1 - y) if fy else y, (1 - c) if fc else c)
            cps.append((peer, pltpu.make_async_remote_copy(
                src_ref=buf.at[me], dst_ref=buf.at[me], send_sem=send_sems.at[k - 1], recv_sem=recv_sems.at[k - 1],
                device_id=peer, device_id_type=MESH)))
        for _, cp in cps:
            cp.start()
        for k, (peer, _) in enumerate(cps):
            theirs = 4 * peer[0] + 2 * peer[1] + peer[2]
            pltpu.make_async_remote_copy(
                src_ref=buf.at[theirs], dst_ref=buf.at[theirs], send_sem=send_sems.at[k], recv_sem=recv_sems.at[k],
                device_id=peer, device_id_type=MESH).wait_recv()
        acc = buf[0]
        for j in range(1, NDEV):
            acc = acc + buf[j]
        o_ref[...] = acc
        for _, cp in cps:
            cp.wait_send()

    return pl.pallas_call(
        body, name=name, in_specs=[_VMEM], out_specs=_VMEM, out_shape=_sds(v.shape, v.dtype),
        scratch_shapes=[pltpu.VMEM((NDEV,) + v.shape, v.dtype), pltpu.SemaphoreType.DMA((NDEV - 1,)),
                        pltpu.SemaphoreType.DMA((NDEV - 1,))],
        compiler_params=_params(),
    )(v)


def _fwd_a(x, nw, win8, lnw, lnb, ws, bst, jobs, *, tm, relay_step):
    s_len = x.shape[0]
    nt = s_len // tm
    nch = tm // CH

    def main(i, ins, outs, scr):
        x_ref, nw_ref, win_ref, lnw_ref, lnb_ref, ws_ref, bst_ref = ins
        h_ref, y_ref, act_ref, dact_ref = outs
        wc_scr, gv_scr, z_ref = scr

        @pl.when(i == 0)
        def _():
            m = _causal_mask()
            for g in range(G):
                wc_scr[g] = jnp.where(m, ws_ref[g], 0.0).astype(BF16)

        x = x_ref[...]
        h = (x * _rms(x) * nw_ref[...]).astype(BF16)
        h_ref[...] = h
        for k in range(NDEV):
            z_ref[:, k * CA:(k + 1) * CA] = _dot(h, win_ref[k])

        ssum = jnp.zeros((tm, 1), F32)
        for g in range(G):
            cs = slice(g * GD, (g + 1) * GD)
            zv = z_ref[:, AW + g * GD:AW + (g + 1) * GD]
            gv, t = _gelu_t(zv)
            gv_scr[:, cs] = gv
            act_ref[:, AW + g * GD:AW + (g + 1) * GD] = _dgelu(zv, t)
            ssum = ssum + jnp.sum(gv, axis=-1, keepdims=True)
        mu = ssum * (1.0 / AW)
        vsum = jnp.zeros((tm, 1), F32)
        for g in range(G):
            dlt = gv_scr[:, g * GD:(g + 1) * GD] - mu
            vsum = vsum + jnp.sum(dlt * dlt, axis=-1, keepdims=True)
        rstd = lax.rsqrt(vsum * (1.0 / AW) + LN_EPS)

        for g in range(G):
            cs = slice(g * GD, (g + 1) * GD)
            vs = slice(AW + g * GD, AW + (g + 1) * GD)
            gs = slice(2 * AW + g * GD, 2 * AW + (g + 1) * GD)
            vhat = (gv_scr[:, cs] - mu) * rstd
            dact_ref[:, vs] = (act_ref[:, vs] * rstd).astype(BF16)
            act_ref[:, vs] = vhat
            vb = (vhat * lnw_ref[:, cs] + lnb_ref[:, cs]).astype(BF16)
            zu = z_ref[:, cs]
            u, tu = _gelu_t(zu)
            act_ref[:, cs] = u
            dact_ref[:, cs] = _dgelu(zu, tu).astype(BF16)
            zg = z_ref[:, gs]
            sig = _sigmoid(zg)
            sg = zg * sig
            act_ref[:, gs] = sg
            dact_ref[:, gs] = (sig * (1.0 + zg * (1.0 - sig))).astype(BF16)
            for n in range(nch):
                rs = slice(n * CH, (n + 1) * CH)
                s = _dot(wc_scr[g], vb[rs, :]) + bst_ref[:, g:g + 1]
                y_ref[rs, cs] = (u[rs, :] * s * sg[rs, :]).astype(BF16)

    tile = lambda w: pl.BlockSpec((tm, w), lambda i: (i, 0))
    return _call(
        main, jobs, name="fwd_a", grid=(nt,), relay_step=relay_step,
        ins=[x, nw, win8, lnw, lnb, ws, bst], in_specs=[tile(D), _VMEM, _VMEM, _VMEM, _VMEM, _VMEM, _VMEM],
        out_shape=[_sds((s_len, D), BF16), _sds((s_len, AW), BF16), _sds((s_len, 3 * AW), F32),
                   _sds((s_len, 3 * AW), BF16)],
        out_specs=[tile(D), tile(AW), tile(3 * AW), tile(3 * AW)],
        scratch=[pltpu.VMEM((G, CH, CH), BF16), pltpu.VMEM((tm, AW), F32), pltpu.VMEM((tm, 3 * AW), F32)])


def _bwd_a(dx1, act, dact, lnw, lnb, ws, bst, wout, jobs, *, tm):
    s_len = dx1.shape[0]
    nt = s_len // tm
    nch = tm // CH

    def main(i, ins, outs, scr):
        dx1_ref, act_ref, dact_ref, lnw_ref, lnb_ref, ws_ref, bst_ref, wout_ref = ins
        dz_ref, glnw_ref, glnb_ref, gws_ref, gbst_ref = outs
        wc_scr, wct_scr, dy_scr, dv_scr, gbs_acc, gwc_acc = scr

        @pl.when(i == 0)
        def _():
            m = _causal_mask()
            for g in range(G):
                wm = jnp.where(m, ws_ref[g], 0.0)
                wc_scr[g] = wm.astype(BF16)
                wct_scr[g] = wm.T.astype(BF16)
            glnw_ref[...] = jnp.zeros_like(glnw_ref)
            glnb_ref[...] = jnp.zeros_like(glnb_ref)
            gbs_acc[...] = jnp.zeros_like(gbs_acc)
            gwc_acc[...] = jnp.zeros_like(gwc_acc)

        dy_scr[...] = _dot_nt(dx1_ref[...], wout_ref[...])

        m1 = jnp.zeros((tm, 1), F32)
        m2 = jnp.zeros((tm, 1), F32)
        for g in range(G):
            cs = slice(g * GD, (g + 1) * GD)
            vs = slice(AW + g * GD, AW + (g + 1) * GD)
            gs = slice(2 * AW + g * GD, 2 * AW + (g + 1) * GD)
            vhat = act_ref[:, vs]
            vb = (vhat * lnw_ref[:, cs] + lnb_ref[:, cs]).astype(BF16)
            u = act_ref[:, cs]
            sg = act_ref[:, gs]
            dy = dy_scr[:, cs]
            dsf = dy * u * sg
            dsb = dsf.astype(BF16)
            dvs = []
            for n in range(nch):
                rs = slice(n * CH, (n + 1) * CH)
                s = _dot(wc_scr[g], vb[rs, :]) + bst_ref[:, g:g + 1]
                dys = dy[rs, :] * s
                dz_ref[rs, cs] = (dys * sg[rs, :] * dact_ref[rs, cs].astype(F32)).astype(BF16)
                dz_ref[rs, gs] = (dys * u[rs, :] * dact_ref[rs, gs].astype(F32)).astype(BF16)
                gbs_acc[g] += dsf[rs, :]
                gwc_acc[g] += _dot_nt(dsb[rs, :], vb[rs, :])
                dvs.append(_dot(wct_scr[g], dsb[rs, :]))
            dv = jnp.concatenate(dvs, axis=0) if nch > 1 else dvs[0]
            glnw_ref[:, cs] += _rowsum(dv * vhat)
            glnb_ref[:, cs] += _rowsum(dv)
            dvh = dv * lnw_ref[:, cs]
            dv_scr[:, cs] = dvh
            m1 = m1 + jnp.sum(dvh, axis=-1, keepdims=True)
            m2 = m2 + jnp.sum(dvh * vhat, axis=-1, keepdims=True)
        m1 = m1 * (1.0 / AW)
        m2 = m2 * (1.0 / AW)
        for g in range(G):
            cs = slice(g * GD, (g + 1) * GD)
            vs = slice(AW + g * GD, AW + (g + 1) * GD)
            dz_ref[:, vs] = ((dv_scr[:, cs] - m1 - act_ref[:, vs] * m2) * dact_ref[:, vs].astype(F32)).astype(BF16)

        @pl.when(i == nt - 1)
        def _():
            m = _causal_mask()
            for g in range(G):
                gws_ref[g] = jnp.where(m, gwc_acc[g], 0.0)
                gbst_ref[:, g:g + 1] = jnp.sum(gbs_acc[g], axis=-1, keepdims=True)

    tile = lambda w: pl.BlockSpec((tm, w), lambda i: (i, 0))
    whole = lambda *s: pl.BlockSpec(s, lambda i: (0,) * len(s))
    big = lambda dt: pltpu.VMEM((tm, AW), dt)
    return _call(
        main, jobs, name="bwd_a", grid=(nt,),
        ins=[dx1, act, dact, lnw, lnb, ws, bst, wout],
        in_specs=[tile(D), tile(3 * AW), tile(3 * AW), _VMEM, _VMEM, _VMEM, _VMEM, _VMEM],
        out_shape=[_sds((s_len, 3 * AW), BF16), _sds((1, AW), F32), _sds((1, AW), F32), _sds((G, CH, CH), F32),
                   _sds((CH, G), F32)],
        out_specs=[tile(3 * AW), whole(1, AW), whole(1, AW), whole(G, CH, CH), whole(CH, G)],
        scratch=[pltpu.VMEM((G, CH, CH), BF16), pltpu.VMEM((G, CH, CH), BF16), big(F32), big(F32),
                 pltpu.VMEM((G, CH, GD), F32), pltpu.VMEM((G, CH, CH), F32)])


def _bwd_a_in(dz, dx1, x, nw, win8, jobs, *, tm, relay_step):
    s_len = x.shape[0]
    nt = s_len // tm

    def main(i, ins, outs, scr):
        dz_ref, dx1_ref, x_ref, nw_ref, win_ref = ins
        gx_ref, gnw_ref = outs

        @pl.when(i == 0)
        def _():
            gnw_ref[...] = jnp.zeros_like(gnw_ref)

        dh = jnp.zeros((tm, D), F32)
        for k in range(NDEV):
            dh = dh + _dot_nt(dz_ref[:, k * CA:(k + 1) * CA], win_ref[k])
        x = x_ref[...]
        r = _rms(x)
        gx_ref[...] = dx1_ref[...] + _rms_bwd(dh, x, r, nw_ref[...])
        gnw_ref[...] += _rowsum(dh * x * r)

    tile = lambda w: pl.BlockSpec((tm, w), lambda i: (i, 0))
    return _call(
        main, jobs, name="bwd_a_in", grid=(nt,), relay_step=relay_step,
        ins=[dz, dx1, x, nw, win8], in_specs=[tile(3 * AW), tile(D), tile(D), _VMEM, _VMEM],
        out_shape=[_sds((s_len, D), F32), _sds((1, D), F32)],
        out_specs=[tile(D), pl.BlockSpec((1, D), lambda i: (0, 0))], scratch=[])


def _conv(p8_ref, cs, xb, xm1, xm2, xm3):
    xc = p8_ref[4:5, cs] + p8_ref[3:4, cs] * xb
    xc = xc + p8_ref[0:1, cs] * xm3
    xc = xc + p8_ref[1:2, cs] * xm2
    return xc + p8_ref[2:3, cs] * xm1


def _gates(p8_ref, gcat_ref, hh, xc):
    cs = slice(hh * HD, (hh + 1) * HD)
    pre = _dot(xc.astype(BF16), gcat_ref[hh])
    r = _sigmoid(pre[:, :HD] + p8_ref[5:6, cs])
    ig = _sigmoid(pre[:, HD:] + p8_ref[6:7, cs])
    sp = _softplus_neg(p8_ref[7:8, cs])
    la = (-RG_C) * r * sp
    a = jnp.exp(la)
    half_log = 0.5 * jnp.log(jnp.tanh(-la) * (1.0 + a * a))
    return r, ig, sp, a, jnp.exp(half_log), jnp.exp(-half_log)


def _scan_rows(a_ref, b_ref, out_ref, carry, tm, reverse):
    row = lax.broadcasted_iota(jnp.int32, (SUBLANES, BW), 0)
    ngrp = tm // SUBLANES

    def step(j, cr):
        jj = (ngrp - 1 - j) if reverse else j
        off = pl.multiple_of(jj * SUBLANES, SUBLANES)
        a = a_ref[pl.ds(off, SUBLANES), :]
        b = b_ref[pl.ds(off, SUBLANES), :]
        for sh in (1, 2, 4):
            if reverse:
                a_s = pltpu.roll(a, SUBLANES - sh, 0)
                b_s = pltpu.roll(b, SUBLANES - sh, 0)
                m = row < SUBLANES - sh
            else:
                a_s = pltpu.roll(a, sh, 0)
                b_s = pltpu.roll(b, sh, 0)
                m = row >= sh
            b = jnp.where(m, a * b_s + b, b)
            a = jnp.where(m, a * a_s, a)
        o = b + a * cr
        out_ref[pl.ds(off, SUBLANES), :] = o
        return o[0:1, :] if reverse else o[SUBLANES - 1:SUBLANES, :]

    return lax.fori_loop(0, ngrp, step, carry)


def _fwd_b(x, ya, wout_a, nw, win8, p8, gcat, jobs, *, tm, relay_step):
    s_len = x.shape[0]
    nt = s_len // tm

    def main(i, ins, outs, scr):
        x_ref, ya_ref, wouta_ref, nw_ref, win_ref, p8_ref, gcat_ref = ins
        x1_ref, zb_ref, hs_ref, h1_ref, yb_ref = outs
        xbe_scr, a_scr, b_scr, carry_scr = scr

        @pl.when(i == 0)
        def _():
            xbe_scr[0:SUBLANES, :] = jnp.zeros((SUBLANES, BW), F32)
            carry_scr[...] = jnp.zeros_like(carry_scr)

        x1 = x_ref[...] + _dot(ya_ref[...], wouta_ref[...])
        x1_ref[...] = x1
        h = (x1 * _rms(x1) * nw_ref[...]).astype(BF16)
        h1_ref[...] = h
        for k in range(NDEV):
            zb_ref[:, k * CB:(k + 1) * CB] = _dot(h, win_ref[k])
        xbe_scr[SUBLANES:SUBLANES + tm, :] = zb_ref[:, :BW]
        for hh in range(BH):
            cs = slice(hh * HD, (hh + 1) * HD)
            xc = _conv(p8_ref, cs, xbe_scr[SUBLANES:SUBLANES + tm, cs], xbe_scr[7:7 + tm, cs],
                       xbe_scr[6:6 + tm, cs], xbe_scr[5:5 + tm, cs])
            _, ig, _, a, mult, _ = _gates(p8_ref, gcat_ref, hh, xc)
            a_scr[:, cs] = a
            b_scr[:, cs] = mult * (ig * xc)
        xbe_scr[0:SUBLANES, :] = xbe_scr[tm:tm + SUBLANES, :]
        carry_scr[...] = _scan_rows(a_scr, b_scr, hs_ref, carry_scr[...], tm, False)
        for hh in range(BH):
            cs = slice(hh * HD, (hh + 1) * HD)
            gt = zb_ref[:, BW + hh * HD:BW + (hh + 1) * HD]
            yb_ref[:, cs] = (hs_ref[:, cs] * (gt * _sigmoid(gt))).astype(BF16)

    tile = lambda w: pl.BlockSpec((tm, w), lambda i: (i, 0))
    return _call(
        main, jobs, name="fwd_b", grid=(nt,), relay_step=relay_step,
        ins=[x, ya, wout_a, nw, win8, p8, gcat], in_specs=[tile(D), tile(AW), _VMEM, _VMEM, _VMEM, _VMEM, _VMEM],
        out_shape=[_sds((s_len, D), F32), _sds((s_len, 2 * BW), F32), _sds((s_len, BW), F32), _sds((s_len, D), BF16),
                   _sds((s_len, BW), BF16)],
        out_specs=[tile(D), tile(2 * BW), tile(BW), tile(D), tile(BW)],
        scratch=[pltpu.VMEM((tm + SUBLANES, BW), F32), pltpu.VMEM((tm, BW), F32), pltpu.VMEM((tm, BW), F32),
                 pltpu.VMEM((1, BW), F32)])


def _head(x1, yb, wout, nfw, tgt, *, tm):
    s_len = x1.shape[0]

    def main(i, ins, outs, scr):
        x1_ref, yb_ref, wout_ref, nfw_ref, t_ref = ins
        dx2_ref, dx2b_ref, loss_ref, gnfw_ref = outs

        @pl.when(i == 0)
        def _():
            loss_ref[...] = jnp.zeros_like(loss_ref)
            gnfw_ref[...] = jnp.zeros_like(gnfw_ref)

        x2 = x1_ref[...] + _dot(yb_ref[...], wout_ref[...])
        rf = _rms(x2)
        xn = x2 * rf
        e = xn * nfw_ref[...] - t_ref[...]
        loss_ref[...] += (0.5 / D) * jnp.sum(jnp.sum(e * e, axis=-1, keepdims=True), axis=0, keepdims=True)
        dyf = e * (1.0 / D)
        gnfw_ref[...] += _rowsum(dyf * xn)
        dx2 = _rms_bwd(dyf, x2, rf, nfw_ref[...])
        dx2_ref[...] = dx2
        dx2b_ref[...] = dx2.astype(BF16)

    tile = lambda w: pl.BlockSpec((tm, w), lambda i: (i, 0))
    whole = lambda *s: pl.BlockSpec(s, lambda i: (0,) * len(s))
    (dx2, dx2b, loss, gnfw), _ = _call(
        main, [], name="head", grid=(s_len // tm,),
        ins=[x1, yb, wout, nfw, tgt], in_specs=[tile(D), tile(BW), _VMEM, _VMEM, tile(D)],
        out_shape=[_sds((s_len, D), F32), _sds((s_len, D), BF16), _sds((1, 1), F32), _sds((1, D), F32)],
        out_specs=[tile(D), tile(D), whole(1, 1), whole(1, D)], scratch=[])
    return dx2, dx2b, loss, gnfw


def _bwd_b(dx2, zb, hs, x1, nw, win8, p8, gcat, wout, *, tm):
    s_len = x1.shape[0]
    nt = s_len // tm
    per = tm // SUBLANES

    def main(i, ins, outs, scr):
        dx2_ref, zb_ref, zbp_ref, hs_ref, hsp_ref, x1_ref, nw_ref, win_ref, p8_ref, gcat_ref, wout_ref = ins
        dx1_ref, dx1b_ref, dzb_ref, gp8_ref, gga_ref, ggx_ref, gnw_ref = outs
        (xbe_scr, hse_scr, ae_scr, an_scr, r_scr, i_scr, m_scr, xc_scr, cc_scr, dhd_scr, dh_scr, dy_scr, dxce_scr,
         carry_scr, afirst_scr) = scr
        ti = nt - 1 - i

        @pl.when(i == 0)
        def _():
            gp8_ref[...] = jnp.zeros_like(gp8_ref)
            gga_ref[...] = jnp.zeros_like(gga_ref)
            ggx_ref[...] = jnp.zeros_like(ggx_ref)
            gnw_ref[...] = jnp.zeros_like(gnw_ref)
            dxce_scr[tm:tm + SUBLANES, :] = jnp.zeros((SUBLANES, BW), F32)
            carry_scr[...] = jnp.zeros_like(carry_scr)
            afirst_scr[...] = jnp.zeros_like(afirst_scr)

        has_prev = (ti > 0).astype(F32)
        xbe_scr[0:SUBLANES, :] = zbp_ref[:, :BW] * has_prev
        xbe_scr[SUBLANES:SUBLANES + tm, :] = zb_ref[:, :BW]
        hse_scr[0:SUBLANES, :] = hsp_ref[...] * has_prev
        hse_scr[SUBLANES:SUBLANES + tm, :] = hs_ref[...]

        dx2 = dx2_ref[...]
        dy_scr[...] = _dot_nt(dx2.astype(BF16), wout_ref[...])

        for hh in range(BH):
            cs = slice(hh * HD, (hh + 1) * HD)
            xc = _conv(p8_ref, cs, xbe_scr[SUBLANES:SUBLANES + tm, cs], xbe_scr[7:7 + tm, cs],
                       xbe_scr[6:6 + tm, cs], xbe_scr[5:5 + tm, cs])
            r, ig, _, a, mult, rm = _gates(p8_ref, gcat_ref, hh, xc)
            cc_scr[:, cs] = a * hse_scr[7:7 + tm, cs] - (ig * xc) * (a * a * rm)
            xc_scr[:, cs] = xc
            r_scr[:, cs] = r
            i_scr[:, cs] = ig
            m_scr[:, cs] = mult
            ae_scr[0:tm, cs] = a
            gt = zb_ref[:, BW + hh * HD:BW + (hh + 1) * HD]
            sig = _sigmoid(gt)
            dy = dy_scr[:, cs]
            dhd_scr[:, cs] = dy * (gt * sig)
            dzb_ref[:, BW + hh * HD:BW + (hh + 1) * HD] = (
                dy * hs_ref[:, cs] * (sig * (1.0 + gt * (1.0 - sig)))).astype(BF16)
        ae_scr[tm:tm + SUBLANES, :] = jnp.broadcast_to(afirst_scr[...], (SUBLANES, BW))
        an_scr[...] = ae_scr[1:1 + tm, :]
        afirst_scr[...] = ae_scr[0:1, :]
        carry_scr[...] = _scan_rows(an_scr, dhd_scr, dh_scr, carry_scr[...], tm, True)

        for hh in range(BH):
            cs = slice(hh * HD, (hh + 1) * HD)
            dh = dh_scr[:, cs]
            mult = m_scr[:, cs]
            ig = i_scr[:, cs]
            r = r_scr[:, cs]
            xc = xc_scr[:, cs]
            lam = p8_ref[7:8, cs]
            sp = _softplus_neg(lam)
            dla = dh * cc_scr[:, cs]
            gp8_ref[7:8, cs] += _rowsum(dla * ((-RG_C) * r)) * (-_sigmoid(-lam))
            dpr = dla * ((-RG_C) * sp) * (r * (1.0 - r))
            dpi = dh * mult * xc * (ig * (1.0 - ig))
            gp8_ref[5:6, cs] += _rowsum(dpr)
            gp8_ref[6:7, cs] += _rowsum(dpi)
            dcat = jnp.concatenate([dpr, dpi], axis=1).astype(BF16)
            dxc = dh * mult * ig + _dot_nt(dcat, gcat_ref[hh])
            gg = _dot(xc.T.astype(BF16), dcat)
            gga_ref[hh] += gg[:, :HD]
            ggx_ref[hh] += gg[:, HD:]
            dxce_scr[0:tm, cs] = dxc
            gp8_ref[4:5, cs] += _rowsum(dxc)
            gp8_ref[3:4, cs] += _rowsum(dxc * xbe_scr[SUBLANES:SUBLANES + tm, cs])
            gp8_ref[2:3, cs] += _rowsum(dxc * xbe_scr[7:7 + tm, cs])
            gp8_ref[1:2, cs] += _rowsum(dxc * xbe_scr[6:6 + tm, cs])
            gp8_ref[0:1, cs] += _rowsum(dxc * xbe_scr[5:5 + tm, cs])
        for hh in range(BH):
            cs = slice(hh * HD, (hh + 1) * HD)
            dxb = p8_ref[3:4, cs] * dxce_scr[0:tm, cs]
            dxb = dxb + p8_ref[2:3, cs] * dxce_scr[1:1 + tm, cs]
            dxb = dxb + p8_ref[1:2, cs] * dxce_scr[2:2 + tm, cs]
            dxb = dxb + p8_ref[0:1, cs] * dxce_scr[3:3 + tm, cs]
            dzb_ref[:, cs] = dxb.astype(BF16)
        dxce_scr[tm:tm + SUBLANES, :] = dxce_scr[0:SUBLANES, :]

        dh1 = jnp.zeros((tm, D), F32)
        for k in range(NDEV):
            dh1 = dh1 + _dot_nt(dzb_ref[:, k * CB:(k + 1) * CB], win_ref[k])
        x1 = x1_ref[...]
        r1 = _rms(x1)
        dx1 = dx2 + _rms_bwd(dh1, x1, r1, nw_ref[...])
        dx1_ref[...] = dx1
        dx1b_ref[...] = dx1.astype(BF16)
        gnw_ref[...] += _rowsum(dh1 * x1 * r1)

    tile = lambda w: pl.BlockSpec((tm, w), lambda i: (nt - 1 - i, 0))
    prev = lambda w: pl.BlockSpec((SUBLANES, w), lambda i: (jnp.maximum((nt - 1 - i) * per - 1, 0), 0))
    whole = lambda *s: pl.BlockSpec(s, lambda i: (0,) * len(s))
    full = lambda: pltpu.VMEM((tm, BW), F32)
    ext = lambda: pltpu.VMEM((tm + SUBLANES, BW), F32)
    out, _ = _call(
        main, [], name="bwd_b", grid=(nt,),
        ins=[dx2, zb, zb, hs, hs, x1, nw, win8, p8, gcat, wout],
        in_specs=[tile(D), tile(2 * BW), prev(2 * BW), tile(BW), prev(BW), tile(D), _VMEM, _VMEM, _VMEM, _VMEM, _VMEM],
        out_shape=[_sds((s_len, D), F32), _sds((s_len, D), BF16), _sds((s_len, 2 * BW), BF16), _sds((SUBLANES, BW), F32),
                   _sds((BH, HD, HD), F32), _sds((BH, HD, HD), F32), _sds((1, D), F32)],
        out_specs=[tile(D), tile(D), tile(2 * BW), whole(SUBLANES, BW), whole(BH, HD, HD), whole(BH, HD, HD),
                   whole(1, D)],
        scratch=[ext(), ext(), ext(), full(), full(), full(), full(), full(), full(), full(), full(), full(), ext(),
                 pltpu.VMEM((1, BW), F32), pltpu.VMEM((1, BW), F32)])
    return out


def _transpose_into(dst_ref, src_ref, rows):
    s_len = src_ref.shape[0]
    for r0 in range(0, s_len, rows):
        dst_ref[:, r0:r0 + rows] = src_ref[r0:r0 + rows, :].astype(F32).T.astype(BF16)


def _wgrad(a, b, jobs, *, by_rows, per, name, relay_step=0):
    s_len, m = a.shape
    n = b.shape[1]
    r, cd = (m // NDEV, n) if by_rows else (m, n // NDEV)
    nsteps = NDEV // per
    at_rows = per * r if by_rows else m

    def main(i, ins, outs, scr):
        a_ref, b_ref = ins
        q_ref, acc_ref = outs
        at_scr, stage, mine, land, send_sems, recv_sems = scr
        x, y, c = _place()

        def to_sibling(pi):
            return pltpu.make_async_remote_copy(
                src_ref=stage.at[pi & 1], dst_ref=land.at[pi], send_sem=send_sems.at[pi], recv_sem=recv_sems.at[pi],
                device_id=(x, y, 1 - c), device_id_type=MESH)

        if by_rows:
            _transpose_into(at_scr, a_ref, 256)
        else:
            @pl.when(i == 0)
            def _():
                _transpose_into(at_scr, a_ref, 256)

        res = _dot(at_scr[...], b_ref[...]).astype(BF16)
        for k in range(per):
            blk = per * i + k
            pi, pc = blk >> 1, blk & 1
            val = res[k * r:(k + 1) * r, :] if by_rows else res

            @pl.when(pc != c)
            def _():
                @pl.when(pi >= 2)
                def _():
                    to_sibling(pi - 2).wait_send()

                stage[pi & 1] = val
                to_sibling(pi).start()

            @pl.when(pc == c)
            def _():
                mine[pi] = val

        @pl.when(i == nsteps - 1)
        def _():
            for p in range(4):
                to_sibling(p).wait_recv()
            to_sibling(2).wait_send()
            to_sibling(3).wait_send()
            _chip_sums(mine, land, q_ref, acc_ref, x, y)

    if by_rows:
        in_specs = [pl.BlockSpec((s_len, at_rows), lambda j: (0, j)), _VMEM]
    else:
        in_specs = [_VMEM, pl.BlockSpec((s_len, cd), lambda j: (0, j))]
    blk_vmem = lambda k: pltpu.VMEM((k, r, cd), BF16)
    (q, acc), job_out = _call(
        main, jobs, name=name, grid=(nsteps,), relay_step=relay_step, ins=[a, b], in_specs=in_specs,
        out_shape=[_sds((NCHIP_OTHER, r, cd), BF16), _sds((r, cd), F32)],
        out_specs=[pl.BlockSpec((NCHIP_OTHER, r, cd), lambda j: (0, 0, 0)), pl.BlockSpec((r, cd), lambda j: (0, 0))],
        scratch=[pltpu.VMEM((at_rows, s_len), BF16), blk_vmem(2), blk_vmem(4), blk_vmem(4),
                 pltpu.SemaphoreType.DMA((4,)), pltpu.SemaphoreType.DMA((4,))])
    return q, acc, job_out


def _adam_math(w, g, m, v):
    m = B1 * m + (1.0 - B1) * g
    v = B2 * v + (1.0 - B2) * (g * g)
    m_hat = m / (1.0 - B1 ** STEP)
    v_hat = v / (1.0 - B2 ** STEP)
    delta = (-LR) * (m_hat / (jnp.sqrt(v_hat) + ADAM_EPS) + WD * w)
    return delta, m, v


def _adam_big(w, acc, land, m, v, name):
    r, cd = w.shape
    rb = 256 if r % 256 == 0 else r
    nland = land.shape[0]

    def body(w_ref, acc_ref, land_ref, m_ref, v_ref, g_ref, d_ref, mo_ref, vo_ref):
        g = acc_ref[...]
        for j in range(nland):
            g = g + land_ref[j].astype(F32)
        g_ref[...] = g
        d_ref[...], mo_ref[...], vo_ref[...] = _adam_math(w_ref[...], g, m_ref[...], v_ref[...])

    blk = pl.BlockSpec((rb, cd), lambda i: (i, 0))
    blk3 = pl.BlockSpec((nland, rb, cd), lambda i: (0, i, 0))
    return pl.pallas_call(
        body, name=name, grid=(r // rb,), in_specs=[blk, blk, blk3, blk, blk], out_specs=[blk] * 4,
        out_shape=[_sds((r, cd), F32)] * 4,
        compiler_params=_params(dimension_semantics=("arbitrary",)),
    )(w, acc, land, m, v)


def _adam_small(groups):
    n = len(groups)

    def body(*refs):
        ins, outs = refs[:4 * n], refs[4 * n:]
        for k in range(n):
            w_ref, g_ref, m_ref, v_ref = ins[4 * k:4 * k + 4]
            d, mo, vo = _adam_math(w_ref[...], g_ref[...], m_ref[...], v_ref[...])
            outs[3 * k][...] = d
            outs[3 * k + 1][...] = mo
            outs[3 * k + 2][...] = vo

    flat = [a for grp in groups for a in grp]
    shapes = [_sds(grp[0].shape, F32) for grp in groups for _ in range(3)]
    res = pl.pallas_call(
        body, name="adam_small", in_specs=[_VMEM] * (4 * n), out_specs=[_VMEM] * (3 * n), out_shape=shapes,
        compiler_params=_params(),
    )(*flat)
    return [tuple(res[3 * k:3 * k + 3]) for k in range(n)]


TM_FWD_A = 256
RELAY_STEP_FWD_A = 4
RELAY_STEP_FWD_B = 2
TM_BWD_A = 256
TM_BWD_A_IN = 256
RELAY_STEP_BWD_A_IN = 4
TM_FWD_B = 256
TM_HEAD = 512
TM_BWD_B = 256


def _pack(parts, rows):
    flat = jnp.concatenate([p.reshape(-1) for p in parts])
    return jnp.pad(flat, (0, NDEV * rows * LANES - flat.shape[0])).reshape(NDEV, rows, LANES)


def _unpack(packed, shapes):
    flat, out, off = packed.reshape(-1), [], 0
    for s in shapes:
        size = 1
        for d in s:
            size *= d
        out.append(flat[off:off + size].reshape(s))
        off += size
    return out


def kernel(x, norm_w, a_w_in, a_ln_w, a_ln_b, a_w_s, a_b_s, a_w_out, b_w_in, b_conv_w, b_conv_b, b_gate_a_w, b_gate_a_b, b_gate_x_w, b_gate_x_b, b_lambda, b_w_out, norm_f_w, loss_target, m_norm_w, m_a_w_in, m_a_ln_w, m_a_ln_b, m_a_w_s, m_a_b_s, m_a_w_out, m_b_w_in, m_b_conv_w, m_b_conv_b, m_b_gate_a_w, m_b_gate_a_b, m_b_gate_x_w, m_b_gate_x_b, m_b_lambda, m_b_w_out, m_norm_f_w, v_norm_w, v_a_w_in, v_a_ln_w, v_a_ln_b, v_a_w_s, v_a_b_s, v_a_w_out, v_b_w_in, v_b_conv_w, v_b_conv_b, v_b_gate_a_w, v_b_gate_a_b, v_b_gate_x_w, v_b_gate_x_b, v_b_lambda, v_b_w_out, v_norm_f_w):
    me = 4 * lax.axis_index("x") + 2 * lax.axis_index("y") + lax.axis_index("c")
    xs, tgt = x[0], loss_target[0]
    nw0, nw1, nfw = norm_w[0:1], norm_w[1:2], norm_f_w.reshape(1, D)
    w_s, bst = a_w_s[0], a_b_s[0].T
    gcat = jnp.concatenate([b_gate_a_w[0], b_gate_x_w[0]], axis=-1).astype(BF16)

    p8_shard = jnp.concatenate([b_conv_w[0], b_conv_b, b_gate_a_b, b_gate_x_b, b_lambda], axis=0)
    ((win_a8, p8_all),) = _comm_only([_Gather([a_w_in[0].astype(BF16), p8_shard])], "gather_first")
    p8 = jnp.transpose(p8_all, (1, 0, 2)).reshape(SUBLANES, BW)

    (h0, ya, act, dact), ((wout_a8, win_b8),) = _fwd_a(
        xs, nw0, win_a8, a_ln_w, a_ln_b, w_s, bst, [_Gather([a_w_out[0].astype(BF16), b_w_in[0].astype(BF16)])],
        tm=TM_FWD_A, relay_step=RELAY_STEP_FWD_A)
    wout_a = wout_a8.reshape(AW, D)
    (x1, zb, hs, h1, yb), ((wout_b8,),) = _fwd_b(
        xs, ya, wout_a, nw1, win_b8, p8, gcat, [_Gather([b_w_out[0].astype(BF16)])],
        tm=TM_FWD_B, relay_step=RELAY_STEP_FWD_B)
    wout_b = wout_b8.reshape(BW, D)
    dx2, dx2b, loss, g_nfw = _head(x1, yb, wout_b, nfw, tgt, tm=TM_HEAD)

    dx1, dx1b, dzb, g_p8, g_ga, g_gx, g_nw1 = _bwd_b(dx2, zb, hs, x1, nw1, win_b8, p8, gcat, wout_b, tm=TM_BWD_B)
    q_wout_b, acc_wout_b, _ = _wgrad(yb, dx2b, [], by_rows=True, per=2, name="wgrad_b_out")
    shapes_b = [(1, D), (1, D), (SUBLANES, BW), (1, 1)]
    pack_b = _pack([g_nfw, g_nw1, g_p8, loss], 16)
    small_b = _InChip([g_ga.reshape(NDEV, -1, HD), g_gx.reshape(NDEV, -1, HD), pack_b])
    q_win_b, acc_win_b, (sm_b,) = _wgrad(h1, dzb, [small_b], by_rows=False, per=1, name="wgrad_b_in")
    qs_b, accs_b = [q_win_b, q_wout_b, *sm_b[:3]], [acc_win_b, acc_wout_b, *sm_b[3:]]

    (dz, g_lnw, g_lnb, g_ws, g_bst), (lands_b,) = _bwd_a(
        dx1b, act, dact, a_ln_w, a_ln_b, w_s, bst, wout_a, [_Exchange(qs_b)], tm=TM_BWD_A)
    shapes_a = [(1, AW), (1, AW), (CH, G)]
    pack_a = _pack([g_lnw, g_lnb, g_bst], 8)
    q_wout_a, acc_wout_a, (red_b, sm_a) = _wgrad(
        ya, dx1b, [_SumGather(accs_b[2:], lands_b[2:]), _InChip([g_ws, pack_a])], by_rows=True, per=1,
        name="wgrad_a_out", relay_step=2)
    qs_a, accs_a = [q_wout_a, *sm_a[:2]], [acc_wout_a, *sm_a[2:]]
    q_win_a, acc_win_a, (lands_a,) = _wgrad(h0, dz, [_Exchange(qs_a)], by_rows=False, per=1, name="wgrad_a_in")
    (gx, g_nw0), (red_a, (l_win_a,)) = _bwd_a_in(
        dz, dx1, xs, nw0, win_a8, [_SumGather(accs_a[1:], lands_a[1:]), _ExchangeVia(q_win_a)],
        tm=TM_BWD_A_IN, relay_step=RELAY_STEP_BWD_A_IN)
    g_nw0 = _allreduce_direct(g_nw0, "allreduce_norm_w0")

    r_ga, r_gx, r_pack_b = red_b
    r_nfw, r_nw1, r_p8, loss = _unpack(r_pack_b, shapes_b)
    r_ws, r_pack_a = red_a
    r_lnw, r_lnb, r_bst = _unpack(r_pack_a, shapes_a)
    g_p8 = lax.dynamic_slice_in_dim(r_p8, me * (BW // NDEV), BW // NDEV, axis=1)
    loss = loss[0, 0]

    weights = dict(norm_w=norm_w, a_w_in=a_w_in, a_ln_w=a_ln_w, a_ln_b=a_ln_b, a_w_s=a_w_s, a_b_s=a_b_s, a_w_out=a_w_out,
                   b_w_in=b_w_in, b_conv_w=b_conv_w, b_conv_b=b_conv_b, b_gate_a_w=b_gate_a_w, b_gate_a_b=b_gate_a_b,
                   b_gate_x_w=b_gate_x_w, b_gate_x_b=b_gate_x_b, b_lambda=b_lambda, b_w_out=b_w_out, norm_f_w=norm_f_w)
    mom1 = dict(norm_w=m_norm_w, a_w_in=m_a_w_in, a_ln_w=m_a_ln_w, a_ln_b=m_a_ln_b, a_w_s=m_a_w_s, a_b_s=m_a_b_s,
                a_w_out=m_a_w_out, b_w_in=m_b_w_in, b_conv_w=m_b_conv_w, b_conv_b=m_b_conv_b, b_gate_a_w=m_b_gate_a_w,
                b_gate_a_b=m_b_gate_a_b, b_gate_x_w=m_b_gate_x_w, b_gate_x_b=m_b_gate_x_b, b_lambda=m_b_lambda,
                b_w_out=m_b_w_out, norm_f_w=m_norm_f_w)
    mom2 = dict(norm_w=v_norm_w, a_w_in=v_a_w_in, a_ln_w=v_a_ln_w, a_ln_b=v_a_ln_b, a_w_s=v_a_w_s, a_b_s=v_a_b_s,
                a_w_out=v_a_w_out, b_w_in=v_b_w_in, b_conv_w=v_b_conv_w, b_conv_b=v_b_conv_b, b_gate_a_w=v_b_gate_a_w,
                b_gate_a_b=v_b_gate_a_b, b_gate_x_w=v_b_gate_x_w, b_gate_x_b=v_b_gate_x_b, b_lambda=v_b_lambda,
                b_w_out=v_b_w_out, norm_f_w=v_norm_f_w)
    names = list(weights)

    def as2d(a):
        return a.reshape(-1, a.shape[-1])

    upd, grads = {}, {}
    for k, acc, land in (("a_w_in", acc_win_a, l_win_a), ("a_w_out", accs_a[0], lands_a[0]),
                         ("b_w_in", accs_b[0], lands_b[0]), ("b_w_out", accs_b[1], lands_b[1])):
        g, d, mo, vo = _adam_big(as2d(weights[k]), acc, land, as2d(mom1[k]), as2d(mom2[k]), "adam_" + k)
        grads[k] = g[None]
        upd[k] = (d, mo, vo)
    grads.update(
        norm_w=jnp.concatenate([g_nw0, r_nw1], axis=0), a_ln_w=r_lnw, a_ln_b=r_lnb,
        a_w_s=r_ws.reshape(1, G, CH, CH), a_b_s=r_bst.T[None],
        b_conv_w=g_p8[None, 0:4], b_conv_b=g_p8[4:5], b_gate_a_w=r_ga.reshape(1, BH, HD, HD), b_gate_a_b=g_p8[5:6],
        b_gate_x_w=r_gx.reshape(1, BH, HD, HD), b_gate_x_b=g_p8[6:7], b_lambda=g_p8[7:8], norm_f_w=r_nfw.reshape(D))
    small_names = [k for k in names if k not in upd]
    res = _adam_small([(as2d(weights[k]), as2d(grads[k]), as2d(mom1[k]), as2d(mom2[k])) for k in small_names])
    for k, r3 in zip(small_names, res):
        upd[k] = r3
    deltas = [upd[k][0].reshape(weights[k].shape) for k in names]
    new_m = [upd[k][1].reshape(weights[k].shape) for k in names]
    new_v = [upd[k][2].reshape(weights[k].shape) for k in names]
    return (loss, gx[None], *[grads[k] for k in names], *deltas, *new_m, *new_v)
```

```python
import jax
import jax.numpy as jnp
from jax import lax
from jax.experimental import pallas as pl
from jax.experimental.pallas import tpu as pltpu

F32 = jnp.float32
BF16 = jnp.bfloat16
MESH = pl.DeviceIdType.MESH

NDEV = 8
NCHIP_OTHER = 3
D = 1024
AW = 2048
G = 8
GD = AW // G
CH = 128
BW = 1536
BH = 12
HD = BW // BH
CA = 3 * AW // NDEV
CB = 2 * BW // NDEV
RMS_EPS = 1e-6
LN_EPS = 1e-5
RG_C = 8.0
LR, B1, B2, ADAM_EPS, WD, STEP = 0.001, 0.9, 0.999, 1e-08, 0.01, 10
V7X_VMEM_BYTES = 64 * 1024 * 1024
VMEM_LIMIT = V7X_VMEM_BYTES - 8 * 1024 * 1024
SUBLANES = 8
LANES = 128
BF16_ROWS = 16
GELU_C = 0.7978845608028654
GELU_K = 0.044715

_VMEM = pl.BlockSpec(memory_space=pltpu.VMEM)
_HBM = pl.BlockSpec(memory_space=pltpu.HBM)


def _sds(shape, dtype):
    return jax.ShapeDtypeStruct(tuple(shape), dtype)


def _params(**kw):
    return pltpu.CompilerParams(vmem_limit_bytes=VMEM_LIMIT, **kw)


def _gelu_t(z):
    t = jnp.tanh(GELU_C * (z + GELU_K * (z * z * z)))
    return 0.5 * z * (1.0 + t), t


def _dgelu(z, t):
    return 0.5 * (1.0 + t) + 0.5 * z * (1.0 - t * t) * (GELU_C * (1.0 + 3.0 * GELU_K * z * z))


def _sigmoid(v):
    return 0.5 * jnp.tanh(0.5 * v) + 0.5


def _softplus_neg(lam):
    return jnp.maximum(-lam, 0.0) + jnp.log1p(jnp.exp(-jnp.abs(lam)))


def _dot(a, b):
    return jnp.dot(a, b, preferred_element_type=F32)


def _dot_nt(a, b):
    return lax.dot_general(a, b, (((1,), (1,)), ((), ())), preferred_element_type=F32)


def _rowsum(v):
    return jnp.sum(v, axis=0, keepdims=True)


def _causal_mask():
    r = lax.broadcasted_iota(jnp.int32, (CH, CH), 0)
    c = lax.broadcasted_iota(jnp.int32, (CH, CH), 1)
    return r >= c


def _rms(x):
    return lax.rsqrt(jnp.mean(x * x, axis=-1, keepdims=True) + RMS_EPS)


def _rms_bwd(dh, x, r, nw):
    gy = dh * nw
    return r * gy - x * (r * r * r) * jnp.mean(gy * x, axis=-1, keepdims=True)


def _place():
    return lax.axis_index("x"), lax.axis_index("y"), lax.axis_index("c")


def _other_chips(x, y):
    return [(1 - x, y), (x, 1 - y), (1 - x, 1 - y)]


GATHER_SLOTS = 10


def _gather_ops(ins, outs, send_sems, recv_sems, local_sems):
    n = len(ins)
    x, y, c = _place()
    sibling = (x, y, 1 - c)
    xn, yn, dg = _other_chips(x, y)
    split = [ins[i].shape[0] % (2 * BF16_ROWS) == 0 for i in range(n)]

    def blk(chip, core):
        return 4 * chip[0] + 2 * chip[1] + core

    me = blk((x, y), c)

    def part(ref, i, half):
        if half is None:
            return ref
        h = ins[i].shape[0] // 2
        return ref.at[pl.ds(half * h, h)]

    def copy(i, k, block, to, half=None, src=None):
        dst = part(outs[i].at[block], i, half)
        return pltpu.make_async_remote_copy(
            src_ref=dst if src is None else part(src, i, half), dst_ref=dst,
            send_sem=send_sems.at[k, i], recv_sem=recv_sems.at[k, i], device_id=to, device_id_type=MESH)

    def first_copies():
        mine = [pltpu.make_async_copy(ins[i], outs[i].at[me], local_sems.at[i]) for i in range(n)]
        first = []
        for i in range(n):
            first.append(copy(i, 0, me, sibling, src=ins[i]))
            if split[i]:
                first.append(copy(i, 1, me, (*xn, c), 0, ins[i]))
                first.append(copy(i, 3, me, (*yn, c), 1, ins[i]))
                first.append(copy(i, 2, me, (*xn, c), 1, ins[i]))
                first.append(copy(i, 4, me, (*yn, c), 0, ins[i]))
            else:
                first.append(copy(i, 1, me, (*xn, c), None, ins[i]))
                first.append(copy(i, 3, me, (*yn, c), None, ins[i]))
                first.append(copy(i, 5, me, (*dg, c), None, ins[i]))
        return mine, first

    def onward():
        out = []
        for i in range(n):
            if split[i]:
                out.append(copy(i, 5, blk(xn, c), (*yn, c), 0))
                out.append(copy(i, 6, blk(yn, c), (*xn, c), 1))
        return out

    def start():
        mine, first = first_copies()
        for cp in mine + first:
            cp.start()

    def relay():
        sends = onward()
        for i in range(n):
            if split[i]:
                copy(i, 1, blk(xn, c), sibling, 0).wait_recv()
                sends.pop(0).start()
                copy(i, 3, blk(yn, c), sibling, 1).wait_recv()
                sends.pop(0).start()

    def finish():
        mine, first = first_copies()
        passed = []

        def pass_on(i, j, chip):
            fwd = copy(i, 7 + j, blk(chip, c), sibling)
            fwd.start()
            passed.append(fwd)

        for i in range(n):
            if split[i]:
                copy(i, 2, blk(xn, c), sibling, 1).wait_recv()
                pass_on(i, 0, xn)
                copy(i, 4, blk(yn, c), sibling, 0).wait_recv()
                pass_on(i, 1, yn)
                copy(i, 5, blk(dg, c), sibling, 0).wait_recv()
                copy(i, 6, blk(dg, c), sibling, 1).wait_recv()
                pass_on(i, 2, dg)
            else:
                copy(i, 1, blk(xn, c), sibling).wait_recv()
                pass_on(i, 0, xn)
                copy(i, 3, blk(yn, c), sibling).wait_recv()
                pass_on(i, 1, yn)
                copy(i, 5, blk(dg, c), sibling).wait_recv()
                pass_on(i, 2, dg)
        for i in range(n):
            copy(i, 0, blk((x, y), 1 - c), sibling).wait_recv()
            for j, chip in enumerate((xn, yn, dg)):
                copy(i, 7 + j, blk(chip, 1 - c), sibling).wait_recv()
        for cp in first + passed + onward():
            cp.wait_send()
        for cp in mine:
            cp.wait()

    return start, relay, finish


def _gather_sems(n):
    return [pltpu.SemaphoreType.DMA((GATHER_SLOTS, n)), pltpu.SemaphoreType.DMA((GATHER_SLOTS, n)),
            pltpu.SemaphoreType.DMA((n,))]


class _Gather:
    def __init__(self, shards):
        n = len(shards)
        self.ins, self.in_specs = list(shards), [_HBM] * n
        self.out_shape = [_sds((NDEV,) + s.shape, s.dtype) for s in shards]
        self.out_specs = [_HBM] * n
        self.scratch = _gather_sems(n)

    def ops(self, ins, outs, scr):
        return _gather_ops(ins, outs, *scr)


class _Exchange:
    def __init__(self, qs):
        n = len(qs)
        self.ins, self.in_specs = list(qs), [_HBM] * n
        self.out_shape = [_sds(q.shape, q.dtype) for q in qs]
        self.out_specs = [_HBM] * n
        self.scratch = [pltpu.SemaphoreType.DMA((NCHIP_OTHER, n)), pltpu.SemaphoreType.DMA((NCHIP_OTHER, n))]

    def ops(self, ins, outs, scr):
        send_sems, recv_sems = scr
        n = len(ins)
        x, y, c = _place()
        chips = _other_chips(x, y)

        def copies():
            return [pltpu.make_async_remote_copy(
                src_ref=ins[i].at[j], dst_ref=outs[i].at[j], send_sem=send_sems.at[j, i],
                recv_sem=recv_sems.at[j, i], device_id=(*chips[j], c), device_id_type=MESH)
                for i in range(n) for j in range(NCHIP_OTHER)]

        def start():
            for cp in copies():
                cp.start()

        def finish():
            cps = copies()
            for cp in cps:
                cp.wait_recv()
            for cp in cps:
                cp.wait_send()

        return start, lambda: None, finish


class _ExchangeVia:
    def __init__(self, q):
        _, r, cd = q.shape
        half = (2, r // 2, cd)
        self.ins, self.in_specs = [q], [_HBM]
        self.out_shape, self.out_specs = [_sds((2, r, cd), q.dtype)], [_HBM]
        self.scratch = [pltpu.VMEM(half, q.dtype), pltpu.VMEM(half, q.dtype), pltpu.VMEM(half, q.dtype),
                        pltpu.SemaphoreType.DMA((6,)), pltpu.SemaphoreType.DMA((6,)), pltpu.SemaphoreType.DMA((2,))]

    def ops(self, ins, outs, scr):
        (q,), (land,) = ins, outs
        relayed, own, comb, send_sems, recv_sems, local_sems = scr
        h = q.shape[1] // 2
        x, y, c = _place()
        xn, yn, _ = _other_chips(x, y)
        h0, h1 = pl.ds(0, h), pl.ds(h, h)

        def remote(k, src, dst, chip):
            return pltpu.make_async_remote_copy(src_ref=src, dst_ref=dst, send_sem=send_sems.at[k],
                                                recv_sem=recv_sems.at[k], device_id=(*chip, c), device_id_type=MESH)

        def via():
            return [remote(2, q.at[2, h0], relayed.at[0], xn), remote(3, q.at[2, h1], relayed.at[1], yn)]

        def direct():
            return [remote(0, q.at[0, h0], land.at[0, h0], xn), remote(1, q.at[1, h1], land.at[1, h1], yn)]

        def second():
            return [remote(4, comb.at[0], land.at[1, h0], yn), remote(5, comb.at[1], land.at[0, h1], xn)]

        def mine():
            return [pltpu.make_async_copy(q.at[1, h0], own.at[0], local_sems.at[0]),
                    pltpu.make_async_copy(q.at[0, h1], own.at[1], local_sems.at[1])]

        def start():
            for cp in via() + direct() + mine():
                cp.start()

        def relay():
            arrived, loaded, onward = via(), mine(), second()
            for k in range(2):
                arrived[k].wait_recv()
                loaded[k].wait()
                comb[k] = (own[k].astype(F32) + relayed[k].astype(F32)).astype(comb.dtype)
                onward[k].start()

        def finish():
            landing = direct() + second()
            for cp in landing:
                cp.wait_recv()
            for cp in via() + landing:
                cp.wait_send()

        return start, relay, finish


class _SumGather:
    def __init__(self, accs, lands):
        n = len(accs)
        self.n = n
        self.ins, self.in_specs = list(accs) + list(lands), [_VMEM] * (2 * n)
        self.out_shape = [_sds((NDEV,) + a.shape, a.dtype) for a in accs]
        self.out_specs = [_HBM] * n
        self.scratch = [pltpu.VMEM(a.shape, a.dtype) for a in accs] + _gather_sems(n)

    def ops(self, ins, outs, scr):
        n = self.n
        accs, lands, mine = ins[:n], ins[n:], scr[:n]
        g_start, relay, finish = _gather_ops(mine, outs, *scr[n:])

        def start():
            for i in range(n):
                mine[i][...] = accs[i][...] + lands[i][0] + lands[i][1] + lands[i][2]
            g_start()

        return start, relay, finish


def _call(main, jobs, *, name, grid, ins, in_specs, out_shape, out_specs, scratch, relay_step=0):
    nsteps = grid[0] if grid else 1
    n_in, n_out, n_scr = len(ins), len(out_shape), len(scratch)

    def body(*refs):
        pos = [0]

        def take(k):
            r = refs[pos[0]:pos[0] + k]
            pos[0] += k
            return r

        m_in = take(n_in)
        j_in = [take(len(j.ins)) for j in jobs]
        m_out = take(n_out)
        j_out = [take(len(j.out_shape)) for j in jobs]
        m_scr = take(n_scr)
        j_scr = [take(len(j.scratch)) for j in jobs]
        ops = [j.ops(a, b, s) for j, a, b, s in zip(jobs, j_in, j_out, j_scr)]
        i = pl.program_id(0) if grid else 0
        if not grid:
            for o in ops:
                o[0]()
            main(i, m_in, m_out, m_scr)
            for o in ops:
                o[1]()
            for o in ops:
                o[2]()
            return

        if ops:
            @pl.when(i == 0)
            def _():
                for o in ops:
                    o[0]()

        main(i, m_in, m_out, m_scr)

        if ops:
            @pl.when(i == min(relay_step, nsteps - 1))
            def _():
                for o in ops:
                    o[1]()

            @pl.when(i == nsteps - 1)
            def _():
                for o in ops:
                    o[2]()

    extra = dict(dimension_semantics=("arbitrary",)) if grid else {}
    res = pl.pallas_call(
        body, name=name, grid=grid,
        in_specs=list(in_specs) + [s for j in jobs for s in j.in_specs],
        out_specs=list(out_specs) + [s for j in jobs for s in j.out_specs],
        out_shape=list(out_shape) + [s for j in jobs for s in j.out_shape],
        scratch_shapes=list(scratch) + [s for j in jobs for s in j.scratch],
        compiler_params=_params(**extra),
    )(*ins, *[a for j in jobs for a in j.ins])
    main_out, rest, job_out = res[:n_out], res[n_out:], []
    for j in jobs:
        k = len(j.out_shape)
        job_out.append(rest[:k])
        rest = rest[k:]
    return main_out, job_out


def _comm_only(jobs, name):
    _, job_out = _call(lambda i, a, b, s: None, jobs, name=name, grid=(), ins=[], in_specs=[], out_shape=[],
                       out_specs=[], scratch=[])
    return job_out


class _InChip:
    def __init__(self, ps):
        n = len(ps)
        self.n = n
        blk = [p.shape[1:] for p in ps]
        self.ins, self.in_specs = list(ps), [_HBM] * n
        self.out_shape = [_sds((NCHIP_OTHER,) + b, p.dtype) for b, p in zip(blk, ps)] + [_sds(b, F32) for b in blk]
        self.out_specs = [_VMEM] * (2 * n)
        self.scratch = ([pltpu.VMEM((4,) + b, p.dtype) for b, p in zip(blk, ps)] * 2
                        + [pltpu.SemaphoreType.DMA((4, n))] * 3)

    def ops(self, ins, outs, scr):
        n = self.n
        q_refs, acc_refs = outs[:n], outs[n:]
        mines, lands = scr[:n], scr[n:2 * n]
        send_sems, recv_sems, local_sems = scr[2 * n:]
        x, y, c = _place()
        sibling = (x, y, 1 - c)

        def copies():
            out = []
            for i in range(n):
                for pi in range(4):
                    loc = pltpu.make_async_copy(ins[i].at[2 * pi + c], mines[i].at[pi], local_sems.at[pi, i])
                    cp = pltpu.make_async_remote_copy(
                        src_ref=ins[i].at[2 * pi + (1 - c)], dst_ref=lands[i].at[pi],
                        send_sem=send_sems.at[pi, i], recv_sem=recv_sems.at[pi, i],
                        device_id=sibling, device_id_type=MESH)
                    out.append((loc, cp))
            return out

        def start():
            for loc, cp in copies():
                loc.start()
                cp.start()

        def finish():
            pairs = copies()
            for loc, cp in pairs:
                loc.wait()
                cp.wait_recv()
            for i in range(n):
                _chip_sums(mines[i], lands[i], q_refs[i], acc_refs[i], x, y)
            for _, cp in pairs:
                cp.wait_send()

        return start, lambda: None, finish


def _chip_sums(mine, land, q_ref, acc_ref, x, y):
    for j, (qx, qy) in enumerate(_other_chips(x, y)):
        qi = 2 * qx + qy
        q_ref[j] = (mine[qi].astype(F32) + land[qi].astype(F32)).astype(q_ref.dtype)
    mi = 2 * x + y
    acc_ref[...] = mine[mi].astype(F32) + land[mi].astype(F32)


def _allreduce_direct(v, name):
    def body(v_ref, o_ref, buf, send_sems, recv_sems):
        x, y, c = _place()
        me = 4 * x + 2 * y + c
        buf[me] = v_ref[...]
        cps = []
        for k in range(1, NDEV):
            fx, fy, fc = (k >> 2) & 1, (k >> 1) & 1, k & 1
            peer = ((1 - x) if fx else x, (1 - y) if fy else y, (1 - c) if fc else c)
            cps.append((peer, pltpu.make_async_remote_copy(
                src_ref=buf.at[me], dst_ref=buf.at[me], send_sem=send_sems.at[k - 1], recv_sem=recv_sems.at[k - 1],
                device_id=peer, device_id_type=MESH)))
        for _, cp in cps:
            cp.start()
        for k, (peer, _) in enumerate(cps):
            theirs = 4 * peer[0] + 2 * peer[1] + peer[2]
            pltpu.make_async_remote_copy(
                src_ref=buf.at[theirs], dst_ref=buf.at[theirs], send_sem=send_sems.at[k], recv_sem=recv_sems.at[k],
                device_id=peer, device_id_type=MESH).wait_recv()
        acc = buf[0]
        for j in range(1, NDEV):
            acc = acc + buf[j]
        o_ref[...] = acc
        for _, cp in cps:
            cp.wait_send()

    return pl.pallas_call(
        body, name=name, in_specs=[_VMEM], out_specs=_VMEM, out_shape=_sds(v.shape, v.dtype),
        scratch_shapes=[pltpu.VMEM((NDEV,) + v.shape, v.dtype), pltpu.SemaphoreType.DMA((NDEV - 1,)),
                        pltpu.SemaphoreType.DMA((NDEV - 1,))],
        compiler_params=_params(),
    )(v)


def _fwd_a(x, nw, win8, lnw, lnb, ws, bst, jobs, *, tm, relay_step):
    s_len = x.shape[0]
    nt = s_len // tm
    nch = tm // CH

    def main(i, ins, outs, scr):
        x_ref, nw_ref, win_ref, lnw_ref, lnb_ref, ws_ref, bst_ref = ins
        h_ref, y_ref, act_ref, dact_ref = outs
        wc_scr, gv_scr, z_ref = scr

        @pl.when(i == 0)
        def _():
            m = _causal_mask()
            for g in range(G):
                wc_scr[g] = jnp.where(m, ws_ref[g], 0.0).astype(BF16)

        x = x_ref[...]
        h = (x * _rms(x) * nw_ref[...]).astype(BF16)
        h_ref[...] = h
        for k in range(NDEV):
            z_ref[:, k * CA:(k + 1) * CA] = _dot(h, win_ref[k])

        ssum = jnp.zeros((tm, 1), F32)
        for g in range(G):
            cs = slice(g * GD, (g + 1) * GD)
            zv = z_ref[:, AW + g * GD:AW + (g + 1) * GD]
            gv, t = _gelu_t(zv)
            gv_scr[:, cs] = gv
            act_ref[:, AW + g * GD:AW + (g + 1) * GD] = _dgelu(zv, t)
            ssum = ssum + jnp.sum(gv, axis=-1, keepdims=True)
        mu = ssum * (1.0 / AW)
        vsum = jnp.zeros((tm, 1), F32)
        for g in range(G):
            dlt = gv_scr[:, g * GD:(g + 1) * GD] - mu
            vsum = vsum + jnp.sum(dlt * dlt, axis=-1, keepdims=True)
        rstd = lax.rsqrt(vsum * (1.0 / AW) + LN_EPS)

        for g in range(G):
            cs = slice(g * GD, (g + 1) * GD)
            vs = slice(AW + g * GD, AW + (g + 1) * GD)
            gs = slice(2 * AW + g * GD, 2 * AW + (g + 1) * GD)
            vhat = (gv_scr[:, cs] - mu) * rstd
            dact_ref[:, vs] = (act_ref[:, vs] * rstd).astype(BF16)
            act_ref[:, vs] = vhat
            vb = (vhat * lnw_ref[:, cs] + lnb_ref[:, cs]).astype(BF16)
            zu = z_ref[:, cs]
            u, tu = _gelu_t(zu)
            act_ref[:, cs] = u
            dact_ref[:, cs] = _dgelu(zu, tu).astype(BF16)
            zg = z_ref[:, gs]
            sig = _sigmoid(zg)
            sg = zg * sig
            act_ref[:, gs] = sg
            dact_ref[:, gs] = (sig * (1.0 + zg * (1.0 - sig))).astype(BF16)
            for n in range(nch):
                rs = slice(n * CH, (n + 1) * CH)
                s = _dot(wc_scr[g], vb[rs, :]) + bst_ref[:, g:g + 1]
                y_ref[rs, cs] = (u[rs, :] * s * sg[rs, :]).astype(BF16)

    tile = lambda w: pl.BlockSpec((tm, w), lambda i: (i, 0))
    return _call(
        main, jobs, name="fwd_a", grid=(nt,), relay_step=relay_step,
        ins=[x, nw, win8, lnw, lnb, ws, bst], in_specs=[tile(D), _VMEM, _VMEM, _VMEM, _VMEM, _VMEM, _VMEM],
        out_shape=[_sds((s_len, D), BF16), _sds((s_len, AW), BF16), _sds((s_len, 3 * AW), F32),
                   _sds((s_len, 3 * AW), BF16)],
        out_specs=[tile(D), tile(AW), tile(3 * AW), tile(3 * AW)],
        scratch=[pltpu.VMEM((G, CH, CH), BF16), pltpu.VMEM((tm, AW), F32), pltpu.VMEM((tm, 3 * AW), F32)])


def _bwd_a(dx1, act, dact, lnw, lnb, ws, bst, wout, jobs, *, tm, relay_step):
    s_len = dx1.shape[0]
    nt = s_len // tm
    nch = tm // CH

    def main(i, ins, outs, scr):
        dx1_ref, act_ref, dact_ref, lnw_ref, lnb_ref, ws_ref, bst_ref, wout_ref = ins
        dz_ref, glnw_ref, glnb_ref, gws_ref, gbst_ref = outs
        wc_scr, wct_scr, dy_scr, dv_scr, gbs_acc, gwc_acc = scr

        @pl.when(i == 0)
        def _():
            m = _causal_mask()
            for g in range(G):
                wm = jnp.where(m, ws_ref[g], 0.0)
                wc_scr[g] = wm.astype(BF16)
                wct_scr[g] = wm.T.astype(BF16)
            glnw_ref[...] = jnp.zeros_like(glnw_ref)
            glnb_ref[...] = jnp.zeros_like(glnb_ref)
            gbs_acc[...] = jnp.zeros_like(gbs_acc)
            gwc_acc[...] = jnp.zeros_like(gwc_acc)

        dy_scr[...] = _dot_nt(dx1_ref[...], wout_ref[...])

        m1 = jnp.zeros((tm, 1), F32)
        m2 = jnp.zeros((tm, 1), F32)
        for g in range(G):
            cs = slice(g * GD, (g + 1) * GD)
            vs = slice(AW + g * GD, AW + (g + 1) * GD)
            gs = slice(2 * AW + g * GD, 2 * AW + (g + 1) * GD)
            vhat = act_ref[:, vs]
            vb = (vhat * lnw_ref[:, cs] + lnb_ref[:, cs]).astype(BF16)
            u = act_ref[:, cs]
            sg = act_ref[:, gs]
            dy = dy_scr[:, cs]
            dsf = dy * u * sg
            dsb = dsf.astype(BF16)
            dvs = []
            for n in range(nch):
                rs = slice(n * CH, (n + 1) * CH)
                s = _dot(wc_scr[g], vb[rs, :]) + bst_ref[:, g:g + 1]
                dys = dy[rs, :] * s
                dz_ref[rs, cs] = (dys * sg[rs, :] * dact_ref[rs, cs].astype(F32)).astype(BF16)
                dz_ref[rs, gs] = (dys * u[rs, :] * dact_ref[rs, gs].astype(F32)).astype(BF16)
                gbs_acc[g] += dsf[rs, :]
                gwc_acc[g] += _dot_nt(dsb[rs, :], vb[rs, :])
                dvs.append(_dot(wct_scr[g], dsb[rs, :]))
            dv = jnp.concatenate(dvs, axis=0) if nch > 1 else dvs[0]
            glnw_ref[:, cs] += _rowsum(dv * vhat)
            glnb_ref[:, cs] += _rowsum(dv)
            dvh = dv * lnw_ref[:, cs]
            dv_scr[:, cs] = dvh
            m1 = m1 + jnp.sum(dvh, axis=-1, keepdims=True)
            m2 = m2 + jnp.sum(dvh * vhat, axis=-1, keepdims=True)
        m1 = m1 * (1.0 / AW)
        m2 = m2 * (1.0 / AW)
        for g in range(G):
            cs = slice(g * GD, (g + 1) * GD)
            vs = slice(AW + g * GD, AW + (g + 1) * GD)
            dz_ref[:, vs] = ((dv_scr[:, cs] - m1 - act_ref[:, vs] * m2) * dact_ref[:, vs].astype(F32)).astype(BF16)

        @pl.when(i == nt - 1)
        def _():
            m = _causal_mask()
            for g in range(G):
                gws_ref[g] = jnp.where(m, gwc_acc[g], 0.0)
                gbst_ref[:, g:g + 1] = jnp.sum(gbs_acc[g], axis=-1, keepdims=True)

    tile = lambda w: pl.BlockSpec((tm, w), lambda i: (i, 0))
    whole = lambda *s: pl.BlockSpec(s, lambda i: (0,) * len(s))
    big = lambda dt: pltpu.VMEM((tm, AW), dt)
    return _call(
        main, jobs, name="bwd_a", grid=(nt,), relay_step=relay_step,
        ins=[dx1, act, dact, lnw, lnb, ws, bst, wout],
        in_specs=[tile(D), tile(3 * AW), tile(3 * AW), _VMEM, _VMEM, _VMEM, _VMEM, _VMEM],
        out_shape=[_sds((s_len, 3 * AW), BF16), _sds((1, AW), F32), _sds((1, AW), F32), _sds((G, CH, CH), F32),
                   _sds((CH, G), F32)],
        out_specs=[tile(3 * AW), whole(1, AW), whole(1, AW), whole(G, CH, CH), whole(CH, G)],
        scratch=[pltpu.VMEM((G, CH, CH), BF16), pltpu.VMEM((G, CH, CH), BF16), big(F32), big(F32),
                 pltpu.VMEM((G, CH, GD), F32), pltpu.VMEM((G, CH, CH), F32)])


def _bwd_a_in(dz, dx1, x, nw, win8, jobs, *, tm, relay_step):
    s_len = x.shape[0]
    nt = s_len // tm

    def main(i, ins, outs, scr):
        dz_ref, dx1_ref, x_ref, nw_ref, win_ref = ins
        gx_ref, gnw_ref = outs

        @pl.when(i == 0)
        def _():
            gnw_ref[...] = jnp.zeros_like(gnw_ref)

        dh = jnp.zeros((tm, D), F32)
        for k in range(NDEV):
            dh = dh + _dot_nt(dz_ref[:, k * CA:(k + 1) * CA], win_ref[k])
        x = x_ref[...]
        r = _rms(x)
        gx_ref[...] = dx1_ref[...] + _rms_bwd(dh, x, r, nw_ref[...])
        gnw_ref[...] += _rowsum(dh * x * r)

    tile = lambda w: pl.BlockSpec((tm, w), lambda i: (i, 0))
    return _call(
        main, jobs, name="bwd_a_in", grid=(nt,), relay_step=relay_step,
        ins=[dz, dx1, x, nw, win8], in_specs=[tile(3 * AW), tile(D), tile(D), _VMEM, _VMEM],
        out_shape=[_sds((s_len, D), F32), _sds((1, D), F32)],
        out_specs=[tile(D), pl.BlockSpec((1, D), lambda i: (0, 0))], scratch=[])


def _conv(p8_ref, cs, xb, xm1, xm2, xm3):
    xc = p8_ref[4:5, cs] + p8_ref[3:4, cs] * xb
    xc = xc + p8_ref[0:1, cs] * xm3
    xc = xc + p8_ref[1:2, cs] * xm2
    return xc + p8_ref[2:3, cs] * xm1


def _gates(p8_ref, gcat_ref, hh, xc):
    cs = slice(hh * HD, (hh + 1) * HD)
    pre = _dot(xc.astype(BF16), gcat_ref[hh])
    r = _sigmoid(pre[:, :HD] + p8_ref[5:6, cs])
    ig = _sigmoid(pre[:, HD:] + p8_ref[6:7, cs])
    sp = _softplus_neg(p8_ref[7:8, cs])
    la = (-RG_C) * r * sp
    a = jnp.exp(la)
    half_log = 0.5 * jnp.log(jnp.tanh(-la) * (1.0 + a * a))
    return r, ig, sp, a, jnp.exp(half_log), jnp.exp(-half_log)


def _scan_rows(a_ref, b_ref, out_ref, carry, tm, reverse):
    row = lax.broadcasted_iota(jnp.int32, (SUBLANES, BW), 0)
    ngrp = tm // SUBLANES

    def step(j, cr):
        jj = (ngrp - 1 - j) if reverse else j
        off = pl.multiple_of(jj * SUBLANES, SUBLANES)
        a = a_ref[pl.ds(off, SUBLANES), :]
        b = b_ref[pl.ds(off, SUBLANES), :]
        for sh in (1, 2, 4):
            if reverse:
                a_s = pltpu.roll(a, SUBLANES - sh, 0)
                b_s = pltpu.roll(b, SUBLANES - sh, 0)
                m = row < SUBLANES - sh
            else:
                a_s = pltpu.roll(a, sh, 0)
                b_s = pltpu.roll(b, sh, 0)
                m = row >= sh
            b = jnp.where(m, a * b_s + b, b)
            a = jnp.where(m, a * a_s, a)
        o = b + a * cr
        out_ref[pl.ds(off, SUBLANES), :] = o
        return o[0:1, :] if reverse else o[SUBLANES - 1:SUBLANES, :]

    return lax.fori_loop(0, ngrp, step, carry)


def _fwd_b(x, ya, wout_a, nw, win8, p8, gcat, jobs, *, tm, relay_step):
    s_len = x.shape[0]
    nt = s_len // tm

    def main(i, ins, outs, scr):
        x_ref, ya_ref, wouta_ref, nw_ref, win_ref, p8_ref, gcat_ref = ins
        x1_ref, zb_ref, hs_ref, h1_ref, yb_ref = outs
        xbe_scr, a_scr, b_scr, carry_scr = scr

        @pl.when(i == 0)
        def _():
            xbe_scr[0:SUBLANES, :] = jnp.zeros((SUBLANES, BW), F32)
            carry_scr[...] = jnp.zeros_like(carry_scr)

        x1 = x_ref[...] + _dot(ya_ref[...], wouta_ref[...])
        x1_ref[...] = x1
        h = (x1 * _rms(x1) * nw_ref[...]).astype(BF16)
        h1_ref[...] = h
        for k in range(NDEV):
            zb_ref[:, k * CB:(k + 1) * CB] = _dot(h, win_ref[k])
        xbe_scr[SUBLANES:SUBLANES + tm, :] = zb_ref[:, :BW]
        for hh in range(BH):
            cs = slice(hh * HD, (hh + 1) * HD)
            xc = _conv(p8_ref, cs, xbe_scr[SUBLANES:SUBLANES + tm, cs], xbe_scr[7:7 + tm, cs],
                       xbe_scr[6:6 + tm, cs], xbe_scr[5:5 + tm, cs])
            _, ig, _, a, mult, _ = _gates(p8_ref, gcat_ref, hh, xc)
            a_scr[:, cs] = a
            b_scr[:, cs] = mult * (ig * xc)
        xbe_scr[0:SUBLANES, :] = xbe_scr[tm:tm + SUBLANES, :]
        carry_scr[...] = _scan_rows(a_scr, b_scr, hs_ref, carry_scr[...], tm, False)
        for hh in range(BH):
            cs = slice(hh * HD, (hh + 1) * HD)
            gt = zb_ref[:, BW + hh * HD:BW + (hh + 1) * HD]
            yb_ref[:, cs] = (hs_ref[:, cs] * (gt * _sigmoid(gt))).astype(BF16)

    tile = lambda w: pl.BlockSpec((tm, w), lambda i: (i, 0))
    return _call(
        main, jobs, name="fwd_b", grid=(nt,), relay_step=relay_step,
        ins=[x, ya, wout_a, nw, win8, p8, gcat], in_specs=[tile(D), tile(AW), _VMEM, _VMEM, _VMEM, _VMEM, _VMEM],
        out_shape=[_sds((s_len, D), F32), _sds((s_len, 2 * BW), F32), _sds((s_len, BW), F32), _sds((s_len, D), BF16),
                   _sds((s_len, BW), BF16)],
        out_specs=[tile(D), tile(2 * BW), tile(BW), tile(D), tile(BW)],
        scratch=[pltpu.VMEM((tm + SUBLANES, BW), F32), pltpu.VMEM((tm, BW), F32), pltpu.VMEM((tm, BW), F32),
                 pltpu.VMEM((1, BW), F32)])


def _head(x1, yb, wout, nfw, tgt, *, tm):
    s_len = x1.shape[0]

    def main(i, ins, outs, scr):
        x1_ref, yb_ref, wout_ref, nfw_ref, t_ref = ins
        dx2_ref, dx2b_ref, loss_ref, gnfw_ref = outs

        @pl.when(i == 0)
        def _():
            loss_ref[...] = jnp.zeros_like(loss_ref)
            gnfw_ref[...] = jnp.zeros_like(gnfw_ref)

        x2 = x1_ref[...] + _dot(yb_ref[...], wout_ref[...])
        rf = _rms(x2)
        xn = x2 * rf
        e = xn * nfw_ref[...] - t_ref[...]
        loss_ref[...] += (0.5 / D) * jnp.sum(jnp.sum(e * e, axis=-1, keepdims=True), axis=0, keepdims=True)
        dyf = e * (1.0 / D)
        gnfw_ref[...] += _rowsum(dyf * xn)
        dx2 = _rms_bwd(dyf, x2, rf, nfw_ref[...])
        dx2_ref[...] = dx2
        dx2b_ref[...] = dx2.astype(BF16)

    tile = lambda w: pl.BlockSpec((tm, w), lambda i: (i, 0))
    whole = lambda *s: pl.BlockSpec(s, lambda i: (0,) * len(s))
    (dx2, dx2b, loss, gnfw), _ = _call(
        main, [], name="head", grid=(s_len // tm,),
        ins=[x1, yb, wout, nfw, tgt], in_specs=[tile(D), tile(BW), _VMEM, _VMEM, tile(D)],
        out_shape=[_sds((s_len, D), F32), _sds((s_len, D), BF16), _sds((1, 1), F32), _sds((1, D), F32)],
        out_specs=[tile(D), tile(D), whole(1, 1), whole(1, D)], scratch=[])
    return dx2, dx2b, loss, gnfw


def _bwd_b(dx2, zb, hs, x1, nw, win8, p8, gcat, wout, *, tm):
    s_len = x1.shape[0]
    nt = s_len // tm
    per = tm // SUBLANES

    def main(i, ins, outs, scr):
        dx2_ref, zb_ref, zbp_ref, hs_ref, hsp_ref, x1_ref, nw_ref, win_ref, p8_ref, gcat_ref, wout_ref = ins
        dx1_ref, dx1b_ref, dzb_ref, gp8_ref, gga_ref, ggx_ref, gnw_ref = outs
        (xbe_scr, hse_scr, ae_scr, an_scr, r_scr, i_scr, m_scr, xc_scr, cc_scr, dhd_scr, dh_scr, dy_scr, dxce_scr,
         carry_scr, afirst_scr) = scr
        ti = nt - 1 - i

        @pl.when(i == 0)
        def _():
            gp8_ref[...] = jnp.zeros_like(gp8_ref)
            gga_ref[...] = jnp.zeros_like(gga_ref)
            ggx_ref[...] = jnp.zeros_like(ggx_ref)
            gnw_ref[...] = jnp.zeros_like(gnw_ref)
            dxce_scr[tm:tm + SUBLANES, :] = jnp.zeros((SUBLANES, BW), F32)
            carry_scr[...] = jnp.zeros_like(carry_scr)
            afirst_scr[...] = jnp.zeros_like(afirst_scr)

        has_prev = (ti > 0).astype(F32)
        xbe_scr[0:SUBLANES, :] = zbp_ref[:, :BW] * has_prev
        xbe_scr[SUBLANES:SUBLANES + tm, :] = zb_ref[:, :BW]
        hse_scr[0:SUBLANES, :] = hsp_ref[...] * has_prev
        hse_scr[SUBLANES:SUBLANES + tm, :] = hs_ref[...]

        dx2 = dx2_ref[...]
        dy_scr[...] = _dot_nt(dx2.astype(BF16), wout_ref[...])

        for hh in range(BH):
            cs = slice(hh * HD, (hh + 1) * HD)
            xc = _conv(p8_ref, cs, xbe_scr[SUBLANES:SUBLANES + tm, cs], xbe_scr[7:7 + tm, cs],
                       xbe_scr[6:6 + tm, cs], xbe_scr[5:5 + tm, cs])
            r, ig, _, a, mult, rm = _gates(p8_ref, gcat_ref, hh, xc)
            cc_scr[:, cs] = a * hse_scr[7:7 + tm, cs] - (ig * xc) * (a * a * rm)
            xc_scr[:, cs] = xc
            r_scr[:, cs] = r
            i_scr[:, cs] = ig
            m_scr[:, cs] = mult
            ae_scr[0:tm, cs] = a
            gt = zb_ref[:, BW + hh * HD:BW + (hh + 1) * HD]
            sig = _sigmoid(gt)
            dy = dy_scr[:, cs]
            dhd_scr[:, cs] = dy * (gt * sig)
            dzb_ref[:, BW + hh * HD:BW + (hh + 1) * HD] = (
                dy * hs_ref[:, cs] * (sig * (1.0 + gt * (1.0 - sig)))).astype(BF16)
        ae_scr[tm:tm + SUBLANES, :] = jnp.broadcast_to(afirst_scr[...], (SUBLANES, BW))
        an_scr[...] = ae_scr[1:1 + tm, :]
        afirst_scr[...] = ae_scr[0:1, :]
        carry_scr[...] = _scan_rows(an_scr, dhd_scr, dh_scr, carry_scr[...], tm, True)

        for hh in range(BH):
            cs = slice(hh * HD, (hh + 1) * HD)
            dh = dh_scr[:, cs]
            mult = m_scr[:, cs]
            ig = i_scr[:, cs]
            r = r_scr[:, cs]
            xc = xc_scr[:, cs]
            lam = p8_ref[7:8, cs]
            sp = _softplus_neg(lam)
            dla = dh * cc_scr[:, cs]
            gp8_ref[7:8, cs] += _rowsum(dla * ((-RG_C) * r)) * (-_sigmoid(-lam))
            dpr = dla * ((-RG_C) * sp) * (r * (1.0 - r))
            dpi = dh * mult * xc * (ig * (1.0 - ig))
            gp8_ref[5:6, cs] += _rowsum(dpr)
            gp8_ref[6:7, cs] += _rowsum(dpi)
            dcat = jnp.concatenate([dpr, dpi], axis=1).astype(BF16)
            dxc = dh * mult * ig + _dot_nt(dcat, gcat_ref[hh])
            gg = _dot(xc.T.astype(BF16), dcat)
            gga_ref[hh] += gg[:, :HD]
            ggx_ref[hh] += gg[:, HD:]
            dxce_scr[0:tm, cs] = dxc
            gp8_ref[4:5, cs] += _rowsum(dxc)
            gp8_ref[3:4, cs] += _rowsum(dxc * xbe_scr[SUBLANES:SUBLANES + tm, cs])
            gp8_ref[2:3, cs] += _rowsum(dxc * xbe_scr[7:7 + tm, cs])
            gp8_ref[1:2, cs] += _rowsum(dxc * xbe_scr[6:6 + tm, cs])
            gp8_ref[0:1, cs] += _rowsum(dxc * xbe_scr[5:5 + tm, cs])
        for hh in range(BH):
            cs = slice(hh * HD, (hh + 1) * HD)
            dxb = p8_ref[3:4, cs] * dxce_scr[0:tm, cs]
            dxb = dxb + p8_ref[2:3, cs] * dxce_scr[1:1 + tm, cs]
            dxb = dxb + p8_ref[1:2, cs] * dxce_scr[2:2 + tm, cs]
            dxb = dxb + p8_ref[0:1, cs] * dxce_scr[3:3 + tm, cs]
            dzb_ref[:, cs] = dxb.astype(BF16)
        dxce_scr[tm:tm + SUBLANES, :] = dxce_scr[0:SUBLANES, :]

        dh1 = jnp.zeros((tm, D), F32)
        for k in range(NDEV):
            dh1 = dh1 + _dot_nt(dzb_ref[:, k * CB:(k + 1) * CB], win_ref[k])
        x1 = x1_ref[...]
        r1 = _rms(x1)
        dx1 = dx2 + _rms_bwd(dh1, x1, r1, nw_ref[...])
        dx1_ref[...] = dx1
        dx1b_ref[...] = dx1.astype(BF16)
        gnw_ref[...] += _rowsum(dh1 * x1 * r1)

    tile = lambda w: pl.BlockSpec((tm, w), lambda i: (nt - 1 - i, 0))
    prev = lambda w: pl.BlockSpec((SUBLANES, w), lambda i: (jnp.maximum((nt - 1 - i) * per - 1, 0), 0))
    whole = lambda *s: pl.BlockSpec(s, lambda i: (0,) * len(s))
    full = lambda: pltpu.VMEM((tm, BW), F32)
    ext = lambda: pltpu.VMEM((tm + SUBLANES, BW), F32)
    out, _ = _call(
        main, [], name="bwd_b", grid=(nt,),
        ins=[dx2, zb, zb, hs, hs, x1, nw, win8, p8, gcat, wout],
        in_specs=[tile(D), tile(2 * BW), prev(2 * BW), tile(BW), prev(BW), tile(D), _VMEM, _VMEM, _VMEM, _VMEM, _VMEM],
        out_shape=[_sds((s_len, D), F32), _sds((s_len, D), BF16), _sds((s_len, 2 * BW), BF16), _sds((SUBLANES, BW), F32),
                   _sds((BH, HD, HD), F32), _sds((BH, HD, HD), F32), _sds((1, D), F32)],
        out_specs=[tile(D), tile(D), tile(2 * BW), whole(SUBLANES, BW), whole(BH, HD, HD), whole(BH, HD, HD),
                   whole(1, D)],
        scratch=[ext(), ext(), ext(), full(), full(), full(), full(), full(), full(), full(), full(), full(), ext(),
                 pltpu.VMEM((1, BW), F32), pltpu.VMEM((1, BW), F32)])
    return out


def _transpose_into(dst_ref, src_ref, rows):
    s_len = src_ref.shape[0]
    for r0 in range(0, s_len, rows):
        dst_ref[:, r0:r0 + rows] = src_ref[r0:r0 + rows, :].astype(F32).T.astype(BF16)


def _wgrad(a, b, jobs, *, by_rows, per, name, relay_step=0):
    s_len, m = a.shape
    n = b.shape[1]
    r, cd = (m // NDEV, n) if by_rows else (m, n // NDEV)
    nsteps = NDEV // per
    at_rows = per * r if by_rows else m

    def main(i, ins, outs, scr):
        a_ref, b_ref = ins
        q_ref, acc_ref = outs
        at_scr, stage, mine, land, send_sems, recv_sems = scr
        x, y, c = _place()

        def to_sibling(pi):
            return pltpu.make_async_remote_copy(
                src_ref=stage.at[pi & 1], dst_ref=land.at[pi], send_sem=send_sems.at[pi], recv_sem=recv_sems.at[pi],
                device_id=(x, y, 1 - c), device_id_type=MESH)

        if by_rows:
            _transpose_into(at_scr, a_ref, 256)
        else:
            @pl.when(i == 0)
            def _():
                _transpose_into(at_scr, a_ref, 256)

        res = _dot(at_scr[...], b_ref[...]).astype(BF16)
        for k in range(per):
            blk = per * i + k
            pi, pc = blk >> 1, blk & 1
            val = res[k * r:(k + 1) * r, :] if by_rows else res

            @pl.when(pc != c)
            def _():
                @pl.when(pi >= 2)
                def _():
                    to_sibling(pi - 2).wait_send()

                stage[pi & 1] = val
                to_sibling(pi).start()

            @pl.when(pc == c)
            def _():
                mine[pi] = val

        @pl.when(i == nsteps - 1)
        def _():
            for p in range(4):
                to_sibling(p).wait_recv()
            to_sibling(2).wait_send()
            to_sibling(3).wait_send()
            _chip_sums(mine, land, q_ref, acc_ref, x, y)

    if by_rows:
        in_specs = [pl.BlockSpec((s_len, at_rows), lambda j: (0, j)), _VMEM]
    else:
        in_specs = [_VMEM, pl.BlockSpec((s_len, cd), lambda j: (0, j))]
    blk_vmem = lambda k: pltpu.VMEM((k, r, cd), BF16)
    (q, acc), job_out = _call(
        main, jobs, name=name, grid=(nsteps,), relay_step=relay_step, ins=[a, b], in_specs=in_specs,
        out_shape=[_sds((NCHIP_OTHER, r, cd), BF16), _sds((r, cd), F32)],
        out_specs=[pl.BlockSpec((NCHIP_OTHER, r, cd), lambda j: (0, 0, 0)), pl.BlockSpec((r, cd), lambda j: (0, 0))],
        scratch=[pltpu.VMEM((at_rows, s_len), BF16), blk_vmem(2), blk_vmem(4), blk_vmem(4),
                 pltpu.SemaphoreType.DMA((4,)), pltpu.SemaphoreType.DMA((4,))])
    return q, acc, job_out


def _adam_math(w, g, m, v):
    m = B1 * m + (1.0 - B1) * g
    v = B2 * v + (1.0 - B2) * (g * g)
    m_hat = m / (1.0 - B1 ** STEP)
    v_hat = v / (1.0 - B2 ** STEP)
    delta = (-LR) * (m_hat / (jnp.sqrt(v_hat) + ADAM_EPS) + WD * w)
    return delta, m, v


def _adam_big(w, acc, land, m, v, name):
    r, cd = w.shape
    rb = 256 if r % 256 == 0 else r
    nland = land.shape[0]

    def body(w_ref, acc_ref, land_ref, m_ref, v_ref, g_ref, d_ref, mo_ref, vo_ref):
        g = acc_ref[...]
        for j in range(nland):
            g = g + land_ref[j].astype(F32)
        g_ref[...] = g
        d_ref[...], mo_ref[...], vo_ref[...] = _adam_math(w_ref[...], g, m_ref[...], v_ref[...])

    blk = pl.BlockSpec((rb, cd), lambda i: (i, 0))
    blk3 = pl.BlockSpec((nland, rb, cd), lambda i: (0, i, 0))
    return pl.pallas_call(
        body, name=name, grid=(r // rb,), in_specs=[blk, blk, blk3, blk, blk], out_specs=[blk] * 4,
        out_shape=[_sds((r, cd), F32)] * 4,
        compiler_params=_params(dimension_semantics=("arbitrary",)),
    )(w, acc, land, m, v)


def _adam_small(groups):
    n = len(groups)

    def body(*refs):
        ins, outs = refs[:4 * n], refs[4 * n:]
        for k in range(n):
            w_ref, g_ref, m_ref, v_ref = ins[4 * k:4 * k + 4]
            d, mo, vo = _adam_math(w_ref[...], g_ref[...], m_ref[...], v_ref[...])
            outs[3 * k][...] = d
            outs[3 * k + 1][...] = mo
            outs[3 * k + 2][...] = vo

    flat = [a for grp in groups for a in grp]
    shapes = [_sds(grp[0].shape, F32) for grp in groups for _ in range(3)]
    res = pl.pallas_call(
        body, name="adam_small", in_specs=[_VMEM] * (4 * n), out_specs=[_VMEM] * (3 * n), out_shape=shapes,
        compiler_params=_params(),
    )(*flat)
    return [tuple(res[3 * k:3 * k + 3]) for k in range(n)]


TM_FWD_A = 256
RELAY_STEP_FWD_A = 4
RELAY_STEP_FWD_B = 2
TM_BWD_A = 256
RELAY_STEP_BWD_A = 3
TM_BWD_A_IN = 256
RELAY_STEP_BWD_A_IN = 4
TM_FWD_B = 256
TM_HEAD = 512
TM_BWD_B = 256


def _pack(parts, rows):
    flat = jnp.concatenate([p.reshape(-1) for p in parts])
    return jnp.pad(flat, (0, NDEV * rows * LANES - flat.shape[0])).reshape(NDEV, rows, LANES)


def _unpack(packed, shapes):
    flat, out, off = packed.reshape(-1), [], 0
    for s in shapes:
        size = 1
        for d in s:
            size *= d
        out.append(flat[off:off + size].reshape(s))
        off += size
    return out


def kernel(x, norm_w, a_w_in, a_ln_w, a_ln_b, a_w_s, a_b_s, a_w_out, b_w_in, b_conv_w, b_conv_b, b_gate_a_w, b_gate_a_b, b_gate_x_w, b_gate_x_b, b_lambda, b_w_out, norm_f_w, loss_target, m_norm_w, m_a_w_in, m_a_ln_w, m_a_ln_b, m_a_w_s, m_a_b_s, m_a_w_out, m_b_w_in, m_b_conv_w, m_b_conv_b, m_b_gate_a_w, m_b_gate_a_b, m_b_gate_x_w, m_b_gate_x_b, m_b_lambda, m_b_w_out, m_norm_f_w, v_norm_w, v_a_w_in, v_a_ln_w, v_a_ln_b, v_a_w_s, v_a_b_s, v_a_w_out, v_b_w_in, v_b_conv_w, v_b_conv_b, v_b_gate_a_w, v_b_gate_a_b, v_b_gate_x_w, v_b_gate_x_b, v_b_lambda, v_b_w_out, v_norm_f_w):
    me = 4 * lax.axis_index("x") + 2 * lax.axis_index("y") + lax.axis_index("c")
    xs, tgt = x[0], loss_target[0]
    nw0, nw1, nfw = norm_w[0:1], norm_w[1:2], norm_f_w.reshape(1, D)
    w_s, bst = a_w_s[0], a_b_s[0].T
    gcat = jnp.concatenate([b_gate_a_w[0], b_gate_x_w[0]], axis=-1).astype(BF16)

    p8_shard = jnp.concatenate([b_conv_w[0], b_conv_b, b_gate_a_b, b_gate_x_b, b_lambda], axis=0)
    ((win_a8, p8_all),) = _comm_only([_Gather([a_w_in[0].astype(BF16), p8_shard])], "gather_first")
    p8 = jnp.transpose(p8_all, (1, 0, 2)).reshape(SUBLANES, BW)

    (h0, ya, act, dact), ((wout_a8, win_b8),) = _fwd_a(
        xs, nw0, win_a8, a_ln_w, a_ln_b, w_s, bst, [_Gather([a_w_out[0].astype(BF16), b_w_in[0].astype(BF16)])],
        tm=TM_FWD_A, relay_step=RELAY_STEP_FWD_A)
    wout_a = wout_a8.reshape(AW, D)
    (x1, zb, hs, h1, yb), ((wout_b8,),) = _fwd_b(
        xs, ya, wout_a, nw1, win_b8, p8, gcat, [_Gather([b_w_out[0].astype(BF16)])],
        tm=TM_FWD_B, relay_step=RELAY_STEP_FWD_B)
    wout_b = wout_b8.reshape(BW, D)
    dx2, dx2b, loss, g_nfw = _head(x1, yb, wout_b, nfw, tgt, tm=TM_HEAD)

    dx1, dx1b, dzb, g_p8, g_ga, g_gx, g_nw1 = _bwd_b(dx2, zb, hs, x1, nw1, win_b8, p8, gcat, wout_b, tm=TM_BWD_B)
    q_wout_b, acc_wout_b, _ = _wgrad(yb, dx2b, [], by_rows=True, per=2, name="wgrad_b_out")
    shapes_b = [(1, D), (1, D), (SUBLANES, BW), (1, 1)]
    pack_b = _pack([g_nfw, g_nw1, g_p8, loss], 16)
    small_b = _InChip([g_ga.reshape(NDEV, -1, HD), g_gx.reshape(NDEV, -1, HD), pack_b])
    q_win_b, acc_win_b, (sm_b, (l_wout_b,)) = _wgrad(h1, dzb, [small_b, _Exchange([q_wout_b])], by_rows=False, per=1,
                                                      name="wgrad_b_in")
    qs_b, accs_b = sm_b[:3], sm_b[3:]

    (dz, g_lnw, g_lnb, g_ws, g_bst), (lands_b, (l_win_b,)) = _bwd_a(
        dx1b, act, dact, a_ln_w, a_ln_b, w_s, bst, wout_a, [_Exchange(qs_b), _ExchangeVia(q_win_b)],
        tm=TM_BWD_A, relay_step=RELAY_STEP_BWD_A)
    shapes_a = [(1, AW), (1, AW), (CH, G)]
    pack_a = _pack([g_lnw, g_lnb, g_bst], 8)
    q_wout_a, acc_wout_a, (red_b, sm_a) = _wgrad(
        ya, dx1b, [_SumGather(accs_b, lands_b), _InChip([g_ws, pack_a])], by_rows=True, per=1,
        name="wgrad_a_out", relay_step=2)
    qs_a, accs_a = [q_wout_a, *sm_a[:2]], [acc_wout_a, *sm_a[2:]]
    q_win_a, acc_win_a, (lands_a,) = _wgrad(h0, dz, [_Exchange(qs_a)], by_rows=False, per=1, name="wgrad_a_in")
    (gx, g_nw0), (red_a, (l_win_a,)) = _bwd_a_in(
        dz, dx1, xs, nw0, win_a8, [_SumGather(accs_a[1:], lands_a[1:]), _ExchangeVia(q_win_a)],
        tm=TM_BWD_A_IN, relay_step=RELAY_STEP_BWD_A_IN)
    g_nw0 = _allreduce_direct(g_nw0, "allreduce_norm_w0")

    r_ga, r_gx, r_pack_b = red_b
    r_nfw, r_nw1, r_p8, loss = _unpack(r_pack_b, shapes_b)
    r_ws, r_pack_a = red_a
    r_lnw, r_lnb, r_bst = _unpack(r_pack_a, shapes_a)
    g_p8 = lax.dynamic_slice_in_dim(r_p8, me * (BW // NDEV), BW // NDEV, axis=1)
    loss = loss[0, 0]

    weights = dict(norm_w=norm_w, a_w_in=a_w_in, a_ln_w=a_ln_w, a_ln_b=a_ln_b, a_w_s=a_w_s, a_b_s=a_b_s, a_w_out=a_w_out,
                   b_w_in=b_w_in, b_conv_w=b_conv_w, b_conv_b=b_conv_b, b_gate_a_w=b_gate_a_w, b_gate_a_b=b_gate_a_b,
                   b_gate_x_w=b_gate_x_w, b_gate_x_b=b_gate_x_b, b_lambda=b_lambda, b_w_out=b_w_out, norm_f_w=norm_f_w)
    mom1 = dict(norm_w=m_norm_w, a_w_in=m_a_w_in, a_ln_w=m_a_ln_w, a_ln_b=m_a_ln_b, a_w_s=m_a_w_s, a_b_s=m_a_b_s,
                a_w_out=m_a_w_out, b_w_in=m_b_w_in, b_conv_w=m_b_conv_w, b_conv_b=m_b_conv_b, b_gate_a_w=m_b_gate_a_w,
                b_gate_a_b=m_b_gate_a_b, b_gate_x_w=m_b_gate_x_w, b_gate_x_b=m_b_gate_x_b, b_lambda=m_b_lambda,
                b_w_out=m_b_w_out, norm_f_w=m_norm_f_w)
    mom2 = dict(norm_w=v_norm_w, a_w_in=v_a_w_in, a_ln_w=v_a_ln_w, a_ln_b=v_a_ln_b, a_w_s=v_a_w_s, a_b_s=v_a_b_s,
                a_w_out=v_a_w_out, b_w_in=v_b_w_in, b_conv_w=v_b_conv_w, b_conv_b=v_b_conv_b, b_gate_a_w=v_b_gate_a_w,
                b_gate_a_b=v_b_gate_a_b, b_gate_x_w=v_b_gate_x_w, b_gate_x_b=v_b_gate_x_b, b_lambda=v_b_lambda,
                b_w_out=v_b_w_out, norm_f_w=v_norm_f_w)
    names = list(weights)

    def as2d(a):
        return a.reshape(-1, a.shape[-1])

    upd, grads = {}, {}
    for k, acc, land in (("a_w_in", acc_win_a, l_win_a), ("a_w_out", accs_a[0], lands_a[0]),
                         ("b_w_in", acc_win_b, l_win_b), ("b_w_out", acc_wout_b, l_wout_b)):
        g, d, mo, vo = _adam_big(as2d(weights[k]), acc, land, as2d(mom1[k]), as2d(mom2[k]), "adam_" + k)
        grads[k] = g[None]
        upd[k] = (d, mo, vo)
    grads.update(
        norm_w=jnp.concatenate([g_nw0, r_nw1], axis=0), a_ln_w=r_lnw, a_ln_b=r_lnb,
        a_w_s=r_ws.reshape(1, G, CH, CH), a_b_s=r_bst.T[None],
        b_conv_w=g_p8[None, 0:4], b_conv_b=g_p8[4:5], b_gate_a_w=r_ga.reshape(1, BH, HD, HD), b_gate_a_b=g_p8[5:6],
        b_gate_x_w=r_gx.reshape(1, BH, HD, HD), b_gate_x_b=g_p8[6:7], b_lambda=g_p8[7:8], norm_f_w=r_nfw.reshape(D))
    small_names = [k for k in names if k not in upd]
    res = _adam_small([(as2d(weights[k]), as2d(grads[k]), as2d(mom1[k]), as2d(mom2[k])) for k in small_names])
    for k, r3 in zip(small_names, res):
        upd[k] = r3
    deltas = [upd[k][0].reshape(weights[k].shape) for k in names]
    new_m = [upd[k][1].reshape(weights[k].shape) for k in names]
    new_v = [upd[k][2].reshape(weights[k].shape) for k in names]
    return (loss, gx[None], *[grads[k] for k in names], *deltas, *new_m, *new_v)
```

```python
import jax
import jax.numpy as jnp
from jax import lax
from jax.experimental import pallas as pl
from jax.experimental.pallas import tpu as pltpu

F32 = jnp.float32
BF16 = jnp.bfloat16
MESH = pl.DeviceIdType.MESH

NDEV = 8
NCHIP_OTHER = 3
D = 1024
AW = 2048
G = 8
GD = AW // G
CH = 128
BW = 1536
BH = 12
HD = BW // BH
CA = 3 * AW // NDEV
CB = 2 * BW // NDEV
RMS_EPS = 1e-6
LN_EPS = 1e-5
RG_C = 8.0
LR, B1, B2, ADAM_EPS, WD, STEP = 0.001, 0.9, 0.999, 1e-08, 0.01, 10
V7X_VMEM_BYTES = 64 * 1024 * 1024
VMEM_LIMIT = V7X_VMEM_BYTES - 8 * 1024 * 1024
SUBLANES = 8
LANES = 128
BF16_ROWS = 16
GELU_C = 0.7978845608028654
GELU_K = 0.044715

_VMEM = pl.BlockSpec(memory_space=pltpu.VMEM)
_HBM = pl.BlockSpec(memory_space=pltpu.HBM)


def _sds(shape, dtype):
    return jax.ShapeDtypeStruct(tuple(shape), dtype)


def _params(**kw):
    return pltpu.CompilerParams(vmem_limit_bytes=VMEM_LIMIT, **kw)


def _gelu_t(z):
    t = jnp.tanh(GELU_C * (z + GELU_K * (z * z * z)))
    return 0.5 * z * (1.0 + t), t


def _dgelu(z, t):
    return 0.5 * (1.0 + t) + 0.5 * z * (1.0 - t * t) * (GELU_C * (1.0 + 3.0 * GELU_K * z * z))


def _sigmoid(v):
    return 0.5 * jnp.tanh(0.5 * v) + 0.5


def _softplus_neg(lam):
    return jnp.maximum(-lam, 0.0) + jnp.log1p(jnp.exp(-jnp.abs(lam)))


def _dot(a, b):
    return jnp.dot(a, b, preferred_element_type=F32)


def _dot_nt(a, b):
    return lax.dot_general(a, b, (((1,), (1,)), ((), ())), preferred_element_type=F32)


def _rowsum(v):
    return jnp.sum(v, axis=0, keepdims=True)


def _causal_mask():
    r = lax.broadcasted_iota(jnp.int32, (CH, CH), 0)
    c = lax.broadcasted_iota(jnp.int32, (CH, CH), 1)
    return r >= c


def _rms(x):
    return lax.rsqrt(jnp.mean(x * x, axis=-1, keepdims=True) + RMS_EPS)


def _rms_bwd(dh, x, r, nw):
    gy = dh * nw
    return r * gy - x * (r * r * r) * jnp.mean(gy * x, axis=-1, keepdims=True)


def _place():
    return lax.axis_index("x"), lax.axis_index("y"), lax.axis_index("c")


def _other_chips(x, y):
    return [(1 - x, y), (x, 1 - y), (1 - x, 1 - y)]


GATHER_SLOTS = 10


def _gather_ops(ins, outs, send_sems, recv_sems, local_sems):
    n = len(ins)
    x, y, c = _place()
    sibling = (x, y, 1 - c)
    xn, yn, dg = _other_chips(x, y)
    split = [ins[i].shape[0] % (2 * BF16_ROWS) == 0 for i in range(n)]

    def blk(chip, core):
        return 4 * chip[0] + 2 * chip[1] + core

    me = blk((x, y), c)

    def part(ref, i, half):
        if half is None:
            return ref
        h = ins[i].shape[0] // 2
        return ref.at[pl.ds(half * h, h)]

    def copy(i, k, block, to, half=None, src=None):
        dst = part(outs[i].at[block], i, half)
        return pltpu.make_async_remote_copy(
            src_ref=dst if src is None else part(src, i, half), dst_ref=dst,
            send_sem=send_sems.at[k, i], recv_sem=recv_sems.at[k, i], device_id=to, device_id_type=MESH)

    def first_copies():
        mine = [pltpu.make_async_copy(ins[i], outs[i].at[me], local_sems.at[i]) for i in range(n)]
        first = []
        for i in range(n):
            first.append(copy(i, 0, me, sibling, src=ins[i]))
            if split[i]:
                first.append(copy(i, 1, me, (*xn, c), 0, ins[i]))
                first.append(copy(i, 3, me, (*yn, c), 1, ins[i]))
                first.append(copy(i, 2, me, (*xn, c), 1, ins[i]))
                first.append(copy(i, 4, me, (*yn, c), 0, ins[i]))
            else:
                first.append(copy(i, 1, me, (*xn, c), None, ins[i]))
                first.append(copy(i, 3, me, (*yn, c), None, ins[i]))
                first.append(copy(i, 5, me, (*dg, c), None, ins[i]))
        return mine, first

    def onward():
        out = []
        for i in range(n):
            if split[i]:
                out.append(copy(i, 5, blk(xn, c), (*yn, c), 0))
                out.append(copy(i, 6, blk(yn, c), (*xn, c), 1))
        return out

    def start():
        mine, first = first_copies()
        for cp in mine + first:
            cp.start()

    def relay():
        sends = onward()
        for i in range(n):
            if split[i]:
                copy(i, 1, blk(xn, c), sibling, 0).wait_recv()
                sends.pop(0).start()
                copy(i, 3, blk(yn, c), sibling, 1).wait_recv()
                sends.pop(0).start()

    def finish():
        mine, first = first_copies()
        passed = []

        def pass_on(i, j, chip):
            fwd = copy(i, 7 + j, blk(chip, c), sibling)
            fwd.start()
            passed.append(fwd)

        for i in range(n):
            if split[i]:
                copy(i, 2, blk(xn, c), sibling, 1).wait_recv()
                pass_on(i, 0, xn)
                copy(i, 4, blk(yn, c), sibling, 0).wait_recv()
                pass_on(i, 1, yn)
                copy(i, 5, blk(dg, c), sibling, 0).wait_recv()
                copy(i, 6, blk(dg, c), sibling, 1).wait_recv()
                pass_on(i, 2, dg)
            else:
                copy(i, 1, blk(xn, c), sibling).wait_recv()
                pass_on(i, 0, xn)
                copy(i, 3, blk(yn, c), sibling).wait_recv()
                pass_on(i, 1, yn)
                copy(i, 5, blk(dg, c), sibling).wait_recv()
                pass_on(i, 2, dg)
        for i in range(n):
            copy(i, 0, blk((x, y), 1 - c), sibling).wait_recv()
            for j, chip in enumerate((xn, yn, dg)):
                copy(i, 7 + j, blk(chip, 1 - c), sibling).wait_recv()
        for cp in first + passed + onward():
            cp.wait_send()
        for cp in mine:
            cp.wait()

    return start, relay, finish


def _gather_sems(n):
    return [pltpu.SemaphoreType.DMA((GATHER_SLOTS, n)), pltpu.SemaphoreType.DMA((GATHER_SLOTS, n)),
            pltpu.SemaphoreType.DMA((n,))]


class _Gather:
    def __init__(self, shards):
        n = len(shards)
        self.ins, self.in_specs = list(shards), [_HBM] * n
        self.out_shape = [_sds((NDEV,) + s.shape, s.dtype) for s in shards]
        self.out_specs = [_HBM] * n
        self.scratch = _gather_sems(n)

    def ops(self, ins, outs, scr):
        return _gather_ops(ins, outs, *scr)


class _Exchange:
    def __init__(self, qs):
        n = len(qs)
        self.ins, self.in_specs = list(qs), [_HBM] * n
        self.out_shape = [_sds(q.shape, q.dtype) for q in qs]
        self.out_specs = [_HBM] * n
        self.scratch = [pltpu.SemaphoreType.DMA((NCHIP_OTHER, n)), pltpu.SemaphoreType.DMA((NCHIP_OTHER, n))]

    def ops(self, ins, outs, scr):
        send_sems, recv_sems = scr
        n = len(ins)
        x, y, c = _place()
        chips = _other_chips(x, y)

        def copies():
            return [pltpu.make_async_remote_copy(
                src_ref=ins[i].at[j], dst_ref=outs[i].at[j], send_sem=send_sems.at[j, i],
                recv_sem=recv_sems.at[j, i], device_id=(*chips[j], c), device_id_type=MESH)
                for i in range(n) for j in range(NCHIP_OTHER)]

        def start():
            for cp in copies():
                cp.start()

        def finish():
            cps = copies()
            for cp in cps:
                cp.wait_recv()
            for cp in cps:
                cp.wait_send()

        return start, lambda: None, finish


class _ExchangeVia:
    def __init__(self, q):
        _, r, cd = q.shape
        half = (2, r // 2, cd)
        self.ins, self.in_specs = [q], [_HBM]
        self.out_shape, self.out_specs = [_sds((2, r, cd), q.dtype)], [_HBM]
        self.scratch = [pltpu.VMEM(half, q.dtype), pltpu.VMEM(half, q.dtype), pltpu.VMEM(half, q.dtype),
                        pltpu.SemaphoreType.DMA((6,)), pltpu.SemaphoreType.DMA((6,)), pltpu.SemaphoreType.DMA((2,))]

    def ops(self, ins, outs, scr):
        (q,), (land,) = ins, outs
        relayed, own, comb, send_sems, recv_sems, local_sems = scr
        h = q.shape[1] // 2
        x, y, c = _place()
        xn, yn, _ = _other_chips(x, y)
        h0, h1 = pl.ds(0, h), pl.ds(h, h)

        def remote(k, src, dst, chip):
            return pltpu.make_async_remote_copy(src_ref=src, dst_ref=dst, send_sem=send_sems.at[k],
                                                recv_sem=recv_sems.at[k], device_id=(*chip, c), device_id_type=MESH)

        def via():
            return [remote(2, q.at[2, h0], relayed.at[0], xn), remote(3, q.at[2, h1], relayed.at[1], yn)]

        def direct():
            return [remote(0, q.at[0, h0], land.at[0, h0], xn), remote(1, q.at[1, h1], land.at[1, h1], yn)]

        def second():
            return [remote(4, comb.at[0], land.at[1, h0], yn), remote(5, comb.at[1], land.at[0, h1], xn)]

        def mine():
            return [pltpu.make_async_copy(q.at[1, h0], own.at[0], local_sems.at[0]),
                    pltpu.make_async_copy(q.at[0, h1], own.at[1], local_sems.at[1])]

        def start():
            for cp in via() + direct() + mine():
                cp.start()

        def relay():
            arrived, loaded, onward = via(), mine(), second()
            for k in range(2):
                arrived[k].wait_recv()
                loaded[k].wait()
                comb[k] = (own[k].astype(F32) + relayed[k].astype(F32)).astype(comb.dtype)
                onward[k].start()

        def finish():
            landing = direct() + second()
            for cp in landing:
                cp.wait_recv()
            for cp in via() + landing:
                cp.wait_send()

        return start, relay, finish


class _SumGather:
    def __init__(self, accs, lands):
        n = len(accs)
        self.n = n
        self.ins, self.in_specs = list(accs) + list(lands), [_VMEM] * (2 * n)
        self.out_shape = [_sds((NDEV,) + a.shape, a.dtype) for a in accs]
        self.out_specs = [_HBM] * n
        self.scratch = [pltpu.VMEM(a.shape, a.dtype) for a in accs] + _gather_sems(n)

    def ops(self, ins, outs, scr):
        n = self.n
        accs, lands, mine = ins[:n], ins[n:], scr[:n]
        g_start, relay, finish = _gather_ops(mine, outs, *scr[n:])

        def start():
            for i in range(n):
                mine[i][...] = accs[i][...] + lands[i][0] + lands[i][1] + lands[i][2]
            g_start()

        return start, relay, finish


def _call(main, jobs, *, name, grid, ins, in_specs, out_shape, out_specs, scratch, relay_step=0):
    nsteps = grid[0] if grid else 1
    n_in, n_out, n_scr = len(ins), len(out_shape), len(scratch)

    def body(*refs):
        pos = [0]

        def take(k):
            r = refs[pos[0]:pos[0] + k]
            pos[0] += k
            return r

        m_in = take(n_in)
        j_in = [take(len(j.ins)) for j in jobs]
        m_out = take(n_out)
        j_out = [take(len(j.out_shape)) for j in jobs]
        m_scr = take(n_scr)
        j_scr = [take(len(j.scratch)) for j in jobs]
        ops = [j.ops(a, b, s) for j, a, b, s in zip(jobs, j_in, j_out, j_scr)]
        i = pl.program_id(0) if grid else 0
        if not grid:
            for o in ops:
                o[0]()
            main(i, m_in, m_out, m_scr)
            for o in ops:
                o[1]()
            for o in ops:
                o[2]()
            return

        if ops:
            @pl.when(i == 0)
            def _():
                for o in ops:
                    o[0]()

        main(i, m_in, m_out, m_scr)

        if ops:
            @pl.when(i == min(relay_step, nsteps - 1))
            def _():
                for o in ops:
                    o[1]()

            @pl.when(i == nsteps - 1)
            def _():
                for o in ops:
                    o[2]()

    extra = dict(dimension_semantics=("arbitrary",)) if grid else {}
    res = pl.pallas_call(
        body, name=name, grid=grid,
        in_specs=list(in_specs) + [s for j in jobs for s in j.in_specs],
        out_specs=list(out_specs) + [s for j in jobs for s in j.out_specs],
        out_shape=list(out_shape) + [s for j in jobs for s in j.out_shape],
        scratch_shapes=list(scratch) + [s for j in jobs for s in j.scratch],
        compiler_params=_params(**extra),
    )(*ins, *[a for j in jobs for a in j.ins])
    main_out, rest, job_out = res[:n_out], res[n_out:], []
    for j in jobs:
        k = len(j.out_shape)
        job_out.append(rest[:k])
        rest = rest[k:]
    return main_out, job_out


def _comm_only(jobs, name):
    _, job_out = _call(lambda i, a, b, s: None, jobs, name=name, grid=(), ins=[], in_specs=[], out_shape=[],
                       out_specs=[], scratch=[])
    return job_out


class _InChip:
    def __init__(self, ps):
        n = len(ps)
        self.n = n
        blk = [p.shape[1:] for p in ps]
        self.ins, self.in_specs = list(ps), [_HBM] * n
        self.out_shape = [_sds((NCHIP_OTHER,) + b, p.dtype) for b, p in zip(blk, ps)] + [_sds(b, F32) for b in blk]
        self.out_specs = [_VMEM] * (2 * n)
        self.scratch = ([pltpu.VMEM((4,) + b, p.dtype) for b, p in zip(blk, ps)] * 2
                        + [pltpu.SemaphoreType.DMA((4, n))] * 3)

    def ops(self, ins, outs, scr):
        n = self.n
        q_refs, acc_refs = outs[:n], outs[n:]
        mines, lands = scr[:n], scr[n:2 * n]
        send_sems, recv_sems, local_sems = scr[2 * n:]
        x, y, c = _place()
        sibling = (x, y, 1 - c)

        def copies():
            out = []
            for i in range(n):
                for pi in range(4):
                    loc = pltpu.make_async_copy(ins[i].at[2 * pi + c], mines[i].at[pi], local_sems.at[pi, i])
                    cp = pltpu.make_async_remote_copy(
                        src_ref=ins[i].at[2 * pi + (1 - c)], dst_ref=lands[i].at[pi],
                        send_sem=send_sems.at[pi, i], recv_sem=recv_sems.at[pi, i],
                        device_id=sibling, device_id_type=MESH)
                    out.append((loc, cp))
            return out

        def start():
            for loc, cp in copies():
                loc.start()
                cp.start()

        def finish():
            pairs = copies()
            for loc, cp in pairs:
                loc.wait()
                cp.wait_recv()
            for i in range(n):
                _chip_sums(mines[i], lands[i], q_refs[i], acc_refs[i], x, y)
            for _, cp in pairs:
                cp.wait_send()

        return start, lambda: None, finish


def _chip_sums(mine, land, q_ref, acc_ref, x, y):
    for j, (qx, qy) in enumerate(_other_chips(x, y)):
        qi = 2 * qx + qy
        q_ref[j] = (mine[qi].astype(F32) + land[qi].astype(F32)).astype(q_ref.dtype)
    mi = 2 * x + y
    acc_ref[...] = mine[mi].astype(F32) + land[mi].astype(F32)


def _allreduce_direct(v, name):
    def body(v_ref, o_ref, buf, send_sems, recv_sems):
        x, y, c = _place()
        me = 4 * x + 2 * y + c
        buf[me] = v_ref[...]
        cps = []
        for k in range(1, NDEV):
            fx, fy, fc = (k >> 2) & 1, (k >> 1) & 1, k & 1
            peer = ((1 - x) if fx else x, (1 - y) if fy else y, (1 - c) if fc else c)
            cps.append((peer, pltpu.make_async_remote_copy(
                src_ref=buf.at[me], dst_ref=buf.at[me], send_sem=send_sems.at[k - 1], recv_sem=recv_sems.at[k - 1],
                device_id=peer, device_id_type=MESH)))
        for _, cp in cps:
            cp.start()
        for k, (peer, _) in enumerate(cps):
            theirs = 4 * peer[0] + 2 * peer[1] + peer[2]
            pltpu.make_async_remote_copy(
                src_ref=buf.at[theirs], dst_ref=buf.at[theirs], send_sem=send_sems.at[k], recv_sem=recv_sems.at[k],
                device_id=peer, device_id_type=MESH).wait_recv()
        acc = buf[0]
        for j in range(1, NDEV):
            acc = acc + buf[j]
        o_ref[...] = acc
        for _, cp in cps:
            cp.wait_send()

    return pl.pallas_call(
        body, name=name, in_specs=[_VMEM], out_specs=_VMEM, out_shape=_sds(v.shape, v.dtype),
        scratch_shapes=[pltpu.VMEM((NDEV,) + v.shape, v.dtype), pltpu.SemaphoreType.DMA((NDEV - 1,)),
                        pltpu.SemaphoreType.DMA((NDEV - 1,))],
        compiler_params=_params(),
    )(v)


def _fwd_a(x, nw, win8, lnw, lnb, ws, bst, jobs, *, tm, relay_step):
    s_len = x.shape[0]
    nt = s_len // tm
    nch = tm // CH

    def main(i, ins, outs, scr):
        x_ref, nw_ref, win_ref, lnw_ref, lnb_ref, ws_ref, bst_ref = ins
        z_ref, h_ref, y_ref = outs
        wc_scr, gv_scr = scr

        @pl.when(i == 0)
        def _():
            m = _causal_mask()
            for g in range(G):
                wc_scr[g] = jnp.where(m, ws_ref[g], 0.0).astype(BF16)

        x = x_ref[...]
        h = (x * _rms(x) * nw_ref[...]).astype(BF16)
        h_ref[...] = h
        for k in range(NDEV):
            z_ref[:, k * CA:(k + 1) * CA] = _dot(h, win_ref[k])

        ssum = jnp.zeros((tm, 1), F32)
        for g in range(G):
            gv = _gelu_t(z_ref[:, AW + g * GD:AW + (g + 1) * GD])[0]
            gv_scr[:, g * GD:(g + 1) * GD] = gv
            ssum = ssum + jnp.sum(gv, axis=-1, keepdims=True)
        mu = ssum * (1.0 / AW)
        vsum = jnp.zeros((tm, 1), F32)
        for g in range(G):
            dlt = gv_scr[:, g * GD:(g + 1) * GD] - mu
            vsum = vsum + jnp.sum(dlt * dlt, axis=-1, keepdims=True)
        rstd = lax.rsqrt(vsum * (1.0 / AW) + LN_EPS)

        for g in range(G):
            cs = slice(g * GD, (g + 1) * GD)
            v = (gv_scr[:, cs] - mu) * rstd * lnw_ref[:, cs] + lnb_ref[:, cs]
            vb = v.astype(BF16)
            u = _gelu_t(z_ref[:, cs])[0]
            zg = z_ref[:, 2 * AW + g * GD:2 * AW + (g + 1) * GD]
            sg = zg * _sigmoid(zg)
            for n in range(nch):
                rs = slice(n * CH, (n + 1) * CH)
                s = _dot(wc_scr[g], vb[rs, :]) + bst_ref[:, g:g + 1]
                y_ref[rs, cs] = (u[rs, :] * s * sg[rs, :]).astype(BF16)

    tile = lambda w: pl.BlockSpec((tm, w), lambda i: (i, 0))
    return _call(
        main, jobs, name="fwd_a", grid=(nt,), relay_step=relay_step,
        ins=[x, nw, win8, lnw, lnb, ws, bst], in_specs=[tile(D), _VMEM, _VMEM, _VMEM, _VMEM, _VMEM, _VMEM],
        out_shape=[_sds((s_len, 3 * AW), F32), _sds((s_len, D), BF16), _sds((s_len, AW), BF16)],
        out_specs=[tile(3 * AW), tile(D), tile(AW)],
        scratch=[pltpu.VMEM((G, CH, CH), BF16), pltpu.VMEM((tm, AW), F32)])


def _bwd_a(dx1, z, lnw, lnb, ws, bst, wout, jobs, *, tm, relay_step):
    s_len = dx1.shape[0]
    nt = s_len // tm
    nch = tm // CH

    def main(i, ins, outs, scr):
        dx1_ref, z_ref, lnw_ref, lnb_ref, ws_ref, bst_ref, wout_ref = ins
        dz_ref, glnw_ref, glnb_ref, gws_ref, gbst_ref = outs
        wc_scr, wct_scr, vh_scr, dgv_scr, dy_scr, dv_scr, gbs_acc, gwc_acc = scr

        @pl.when(i == 0)
        def _():
            m = _causal_mask()
            for g in range(G):
                wm = jnp.where(m, ws_ref[g], 0.0)
                wc_scr[g] = wm.astype(BF16)
                wct_scr[g] = wm.T.astype(BF16)
            glnw_ref[...] = jnp.zeros_like(glnw_ref)
            glnb_ref[...] = jnp.zeros_like(glnb_ref)
            gbs_acc[...] = jnp.zeros_like(gbs_acc)
            gwc_acc[...] = jnp.zeros_like(gwc_acc)

        dy_scr[...] = _dot_nt(dx1_ref[...], wout_ref[...])

        ssum = jnp.zeros((tm, 1), F32)
        for g in range(G):
            cs = slice(g * GD, (g + 1) * GD)
            zv = z_ref[:, AW + g * GD:AW + (g + 1) * GD]
            gv, t = _gelu_t(zv)
            vh_scr[:, cs] = gv
            dgv_scr[:, cs] = _dgelu(zv, t)
            ssum = ssum + jnp.sum(gv, axis=-1, keepdims=True)
        mu = ssum * (1.0 / AW)
        vsum = jnp.zeros((tm, 1), F32)
        for g in range(G):
            dlt = vh_scr[:, g * GD:(g + 1) * GD] - mu
            vsum = vsum + jnp.sum(dlt * dlt, axis=-1, keepdims=True)
        rstd = lax.rsqrt(vsum * (1.0 / AW) + LN_EPS)

        m1 = jnp.zeros((tm, 1), F32)
        m2 = jnp.zeros((tm, 1), F32)
        for g in range(G):
            cs = slice(g * GD, (g + 1) * GD)
            gs = slice(2 * AW + g * GD, 2 * AW + (g + 1) * GD)
            vhat = (vh_scr[:, cs] - mu) * rstd
            vh_scr[:, cs] = vhat
            vb = (vhat * lnw_ref[:, cs] + lnb_ref[:, cs]).astype(BF16)
            zu = z_ref[:, cs]
            u, tu = _gelu_t(zu)
            zg = z_ref[:, gs]
            sig = _sigmoid(zg)
            sg = zg * sig
            dy = dy_scr[:, cs]
            dsf = dy * u * sg
            dsb = dsf.astype(BF16)
            dvs = []
            for n in range(nch):
                rs = slice(n * CH, (n + 1) * CH)
                s = _dot(wc_scr[g], vb[rs, :]) + bst_ref[:, g:g + 1]
                dys = dy[rs, :] * s
                dz_ref[rs, cs] = (dys * sg[rs, :] * _dgelu(zu[rs, :], tu[rs, :])).astype(BF16)
                dz_ref[rs, gs] = (dys * u[rs, :] * (sig[rs, :] * (1.0 + zg[rs, :] * (1.0 - sig[rs, :])))).astype(BF16)
                gbs_acc[g] += dsf[rs, :]
                gwc_acc[g] += _dot_nt(dsb[rs, :], vb[rs, :])
                dvs.append(_dot(wct_scr[g], dsb[rs, :]))
            dv = jnp.concatenate(dvs, axis=0) if nch > 1 else dvs[0]
            glnw_ref[:, cs] += _rowsum(dv * vhat)
            glnb_ref[:, cs] += _rowsum(dv)
            dvh = dv * lnw_ref[:, cs]
            dv_scr[:, cs] = dvh
            m1 = m1 + jnp.sum(dvh, axis=-1, keepdims=True)
            m2 = m2 + jnp.sum(dvh * vhat, axis=-1, keepdims=True)
        m1 = m1 * (1.0 / AW)
        m2 = m2 * (1.0 / AW)
        for g in range(G):
            cs = slice(g * GD, (g + 1) * GD)
            dgv = rstd * (dv_scr[:, cs] - m1 - vh_scr[:, cs] * m2)
            dz_ref[:, AW + g * GD:AW + (g + 1) * GD] = (dgv * dgv_scr[:, cs]).astype(BF16)

        @pl.when(i == nt - 1)
        def _():
            m = _causal_mask()
            for g in range(G):
                gws_ref[g] = jnp.where(m, gwc_acc[g], 0.0)
                gbst_ref[:, g:g + 1] = jnp.sum(gbs_acc[g], axis=-1, keepdims=True)

    tile = lambda w: pl.BlockSpec((tm, w), lambda i: (i, 0))
    whole = lambda *s: pl.BlockSpec(s, lambda i: (0,) * len(s))
    big = lambda dt: pltpu.VMEM((tm, AW), dt)
    return _call(
        main, jobs, name="bwd_a", grid=(nt,), relay_step=relay_step,
        ins=[dx1, z, lnw, lnb, ws, bst, wout], in_specs=[tile(D), tile(3 * AW), _VMEM, _VMEM, _VMEM, _VMEM, _VMEM],
        out_shape=[_sds((s_len, 3 * AW), BF16), _sds((1, AW), F32), _sds((1, AW), F32), _sds((G, CH, CH), F32),
                   _sds((CH, G), F32)],
        out_specs=[tile(3 * AW), whole(1, AW), whole(1, AW), whole(G, CH, CH), whole(CH, G)],
        scratch=[pltpu.VMEM((G, CH, CH), BF16), pltpu.VMEM((G, CH, CH), BF16), big(F32), big(F32), big(F32), big(F32),
                 pltpu.VMEM((G, CH, GD), F32), pltpu.VMEM((G, CH, CH), F32)])


def _bwd_a_in(dz, dx1, x, nw, win8, jobs, *, tm, relay_step):
    s_len = x.shape[0]
    nt = s_len // tm

    def main(i, ins, outs, scr):
        dz_ref, dx1_ref, x_ref, nw_ref, win_ref = ins
        gx_ref, gnw_ref = outs

        @pl.when(i == 0)
        def _():
            gnw_ref[...] = jnp.zeros_like(gnw_ref)

        dh = jnp.zeros((tm, D), F32)
        for k in range(NDEV):
            dh = dh + _dot_nt(dz_ref[:, k * CA:(k + 1) * CA], win_ref[k])
        x = x_ref[...]
        r = _rms(x)
        gx_ref[...] = dx1_ref[...] + _rms_bwd(dh, x, r, nw_ref[...])
        gnw_ref[...] += _rowsum(dh * x * r)

    tile = lambda w: pl.BlockSpec((tm, w), lambda i: (i, 0))
    return _call(
        main, jobs, name="bwd_a_in", grid=(nt,), relay_step=relay_step,
        ins=[dz, dx1, x, nw, win8], in_specs=[tile(3 * AW), tile(D), tile(D), _VMEM, _VMEM],
        out_shape=[_sds((s_len, D), F32), _sds((1, D), F32)],
        out_specs=[tile(D), pl.BlockSpec((1, D), lambda i: (0, 0))], scratch=[])


def _conv(p8_ref, cs, xb, xm1, xm2, xm3):
    xc = p8_ref[4:5, cs] + p8_ref[3:4, cs] * xb
    xc = xc + p8_ref[0:1, cs] * xm3
    xc = xc + p8_ref[1:2, cs] * xm2
    return xc + p8_ref[2:3, cs] * xm1


def _gates(p8_ref, gcat_ref, hh, xc):
    cs = slice(hh * HD, (hh + 1) * HD)
    pre = _dot(xc.astype(BF16), gcat_ref[hh])
    r = _sigmoid(pre[:, :HD] + p8_ref[5:6, cs])
    ig = _sigmoid(pre[:, HD:] + p8_ref[6:7, cs])
    sp = _softplus_neg(p8_ref[7:8, cs])
    la = (-RG_C) * r * sp
    a = jnp.exp(la)
    half_log = 0.5 * jnp.log(jnp.tanh(-la) * (1.0 + a * a))
    return r, ig, sp, a, jnp.exp(half_log), jnp.exp(-half_log)


def _scan_rows(a_ref, b_ref, out_ref, carry, tm, reverse):
    row = lax.broadcasted_iota(jnp.int32, (SUBLANES, BW), 0)
    ngrp = tm // SUBLANES

    def step(j, cr):
        jj = (ngrp - 1 - j) if reverse else j
        off = pl.multiple_of(jj * SUBLANES, SUBLANES)
        a = a_ref[pl.ds(off, SUBLANES), :]
        b = b_ref[pl.ds(off, SUBLANES), :]
        for sh in (1, 2, 4):
            if reverse:
                a_s = pltpu.roll(a, SUBLANES - sh, 0)
                b_s = pltpu.roll(b, SUBLANES - sh, 0)
                m = row < SUBLANES - sh
            else:
                a_s = pltpu.roll(a, sh, 0)
                b_s = pltpu.roll(b, sh, 0)
                m = row >= sh
            b = jnp.where(m, a * b_s + b, b)
            a = jnp.where(m, a * a_s, a)
        o = b + a * cr
        out_ref[pl.ds(off, SUBLANES), :] = o
        return o[0:1, :] if reverse else o[SUBLANES - 1:SUBLANES, :]

    return lax.fori_loop(0, ngrp, step, carry)


def _fwd_b(x, ya, wout_a, nw, win8, p8, gcat, jobs, *, tm, relay_step):
    s_len = x.shape[0]
    nt = s_len // tm

    def main(i, ins, outs, scr):
        x_ref, ya_ref, wouta_ref, nw_ref, win_ref, p8_ref, gcat_ref = ins
        x1_ref, zb_ref, hs_ref, h1_ref, yb_ref = outs
        xbe_scr, a_scr, b_scr, carry_scr = scr

        @pl.when(i == 0)
        def _():
            xbe_scr[0:SUBLANES, :] = jnp.zeros((SUBLANES, BW), F32)
            carry_scr[...] = jnp.zeros_like(carry_scr)

        x1 = x_ref[...] + _dot(ya_ref[...], wouta_ref[...])
        x1_ref[...] = x1
        h = (x1 * _rms(x1) * nw_ref[...]).astype(BF16)
        h1_ref[...] = h
        for k in range(NDEV):
            zb_ref[:, k * CB:(k + 1) * CB] = _dot(h, win_ref[k])
        xbe_scr[SUBLANES:SUBLANES + tm, :] = zb_ref[:, :BW]
        for hh in range(BH):
            cs = slice(hh * HD, (hh + 1) * HD)
            xc = _conv(p8_ref, cs, xbe_scr[SUBLANES:SUBLANES + tm, cs], xbe_scr[7:7 + tm, cs],
                       xbe_scr[6:6 + tm, cs], xbe_scr[5:5 + tm, cs])
            _, ig, _, a, mult, _ = _gates(p8_ref, gcat_ref, hh, xc)
            a_scr[:, cs] = a
            b_scr[:, cs] = mult * (ig * xc)
        xbe_scr[0:SUBLANES, :] = xbe_scr[tm:tm + SUBLANES, :]
        carry_scr[...] = _scan_rows(a_scr, b_scr, hs_ref, carry_scr[...], tm, False)
        for hh in range(BH):
            cs = slice(hh * HD, (hh + 1) * HD)
            gt = zb_ref[:, BW + hh * HD:BW + (hh + 1) * HD]
            yb_ref[:, cs] = (hs_ref[:, cs] * (gt * _sigmoid(gt))).astype(BF16)

    tile = lambda w: pl.BlockSpec((tm, w), lambda i: (i, 0))
    return _call(
        main, jobs, name="fwd_b", grid=(nt,), relay_step=relay_step,
        ins=[x, ya, wout_a, nw, win8, p8, gcat], in_specs=[tile(D), tile(AW), _VMEM, _VMEM, _VMEM, _VMEM, _VMEM],
        out_shape=[_sds((s_len, D), F32), _sds((s_len, 2 * BW), F32), _sds((s_len, BW), F32), _sds((s_len, D), BF16),
                   _sds((s_len, BW), BF16)],
        out_specs=[tile(D), tile(2 * BW), tile(BW), tile(D), tile(BW)],
        scratch=[pltpu.VMEM((tm + SUBLANES, BW), F32), pltpu.VMEM((tm, BW), F32), pltpu.VMEM((tm, BW), F32),
                 pltpu.VMEM((1, BW), F32)])


def _head(x1, yb, wout, nfw, tgt, *, tm):
    s_len = x1.shape[0]

    def main(i, ins, outs, scr):
        x1_ref, yb_ref, wout_ref, nfw_ref, t_ref = ins
        dx2_ref, dx2b_ref, loss_ref, gnfw_ref = outs

        @pl.when(i == 0)
        def _():
            loss_ref[...] = jnp.zeros_like(loss_ref)
            gnfw_ref[...] = jnp.zeros_like(gnfw_ref)

        x2 = x1_ref[...] + _dot(yb_ref[...], wout_ref[...])
        rf = _rms(x2)
        xn = x2 * rf
        e = xn * nfw_ref[...] - t_ref[...]
        loss_ref[...] += (0.5 / D) * jnp.sum(jnp.sum(e * e, axis=-1, keepdims=True), axis=0, keepdims=True)
        dyf = e * (1.0 / D)
        gnfw_ref[...] += _rowsum(dyf * xn)
        dx2 = _rms_bwd(dyf, x2, rf, nfw_ref[...])
        dx2_ref[...] = dx2
        dx2b_ref[...] = dx2.astype(BF16)

    tile = lambda w: pl.BlockSpec((tm, w), lambda i: (i, 0))
    whole = lambda *s: pl.BlockSpec(s, lambda i: (0,) * len(s))
    (dx2, dx2b, loss, gnfw), _ = _call(
        main, [], name="head", grid=(s_len // tm,),
        ins=[x1, yb, wout, nfw, tgt], in_specs=[tile(D), tile(BW), _VMEM, _VMEM, tile(D)],
        out_shape=[_sds((s_len, D), F32), _sds((s_len, D), BF16), _sds((1, 1), F32), _sds((1, D), F32)],
        out_specs=[tile(D), tile(D), whole(1, 1), whole(1, D)], scratch=[])
    return dx2, dx2b, loss, gnfw


def _bwd_b(dx2, zb, hs, x1, nw, win8, p8, gcat, wout, *, tm):
    s_len = x1.shape[0]
    nt = s_len // tm
    per = tm // SUBLANES

    def main(i, ins, outs, scr):
        dx2_ref, zb_ref, zbp_ref, hs_ref, hsp_ref, x1_ref, nw_ref, win_ref, p8_ref, gcat_ref, wout_ref = ins
        dx1_ref, dx1b_ref, dzb_ref, gp8_ref, gga_ref, ggx_ref, gnw_ref = outs
        (xbe_scr, hse_scr, ae_scr, an_scr, r_scr, i_scr, m_scr, xc_scr, cc_scr, dhd_scr, dh_scr, dy_scr, dxce_scr,
         carry_scr, afirst_scr) = scr
        ti = nt - 1 - i

        @pl.when(i == 0)
        def _():
            gp8_ref[...] = jnp.zeros_like(gp8_ref)
            gga_ref[...] = jnp.zeros_like(gga_ref)
            ggx_ref[...] = jnp.zeros_like(ggx_ref)
            gnw_ref[...] = jnp.zeros_like(gnw_ref)
            dxce_scr[tm:tm + SUBLANES, :] = jnp.zeros((SUBLANES, BW), F32)
            carry_scr[...] = jnp.zeros_like(carry_scr)
            afirst_scr[...] = jnp.zeros_like(afirst_scr)

        has_prev = (ti > 0).astype(F32)
        xbe_scr[0:SUBLANES, :] = zbp_ref[:, :BW] * has_prev
        xbe_scr[SUBLANES:SUBLANES + tm, :] = zb_ref[:, :BW]
        hse_scr[0:SUBLANES, :] = hsp_ref[...] * has_prev
        hse_scr[SUBLANES:SUBLANES + tm, :] = hs_ref[...]

        dx2 = dx2_ref[...]
        dy_scr[...] = _dot_nt(dx2.astype(BF16), wout_ref[...])

        for hh in range(BH):
            cs = slice(hh * HD, (hh + 1) * HD)
            xc = _conv(p8_ref, cs, xbe_scr[SUBLANES:SUBLANES + tm, cs], xbe_scr[7:7 + tm, cs],
                       xbe_scr[6:6 + tm, cs], xbe_scr[5:5 + tm, cs])
            r, ig, _, a, mult, rm = _gates(p8_ref, gcat_ref, hh, xc)
            cc_scr[:, cs] = a * hse_scr[7:7 + tm, cs] - (ig * xc) * (a * a * rm)
            xc_scr[:, cs] = xc
            r_scr[:, cs] = r
            i_scr[:, cs] = ig
            m_scr[:, cs] = mult
            ae_scr[0:tm, cs] = a
            gt = zb_ref[:, BW + hh * HD:BW + (hh + 1) * HD]
            sig = _sigmoid(gt)
            dy = dy_scr[:, cs]
            dhd_scr[:, cs] = dy * (gt * sig)
            dzb_ref[:, BW + hh * HD:BW + (hh + 1) * HD] = (
                dy * hs_ref[:, cs] * (sig * (1.0 + gt * (1.0 - sig)))).astype(BF16)
        ae_scr[tm:tm + SUBLANES, :] = jnp.broadcast_to(afirst_scr[...], (SUBLANES, BW))
        an_scr[...] = ae_scr[1:1 + tm, :]
        afirst_scr[...] = ae_scr[0:1, :]
        carry_scr[...] = _scan_rows(an_scr, dhd_scr, dh_scr, carry_scr[...], tm, True)

        for hh in range(BH):
            cs = slice(hh * HD, (hh + 1) * HD)
            dh = dh_scr[:, cs]
            mult = m_scr[:, cs]
            ig = i_scr[:, cs]
            r = r_scr[:, cs]
            xc = xc_scr[:, cs]
            lam = p8_ref[7:8, cs]
            sp = _softplus_neg(lam)
            dla = dh * cc_scr[:, cs]
            gp8_ref[7:8, cs] += _rowsum(dla * ((-RG_C) * r)) * (-_sigmoid(-lam))
            dpr = dla * ((-RG_C) * sp) * (r * (1.0 - r))
            dpi = dh * mult * xc * (ig * (1.0 - ig))
            gp8_ref[5:6, cs] += _rowsum(dpr)
            gp8_ref[6:7, cs] += _rowsum(dpi)
            dcat = jnp.concatenate([dpr, dpi], axis=1).astype(BF16)
            dxc = dh * mult * ig + _dot_nt(dcat, gcat_ref[hh])
            gg = _dot(xc.T.astype(BF16), dcat)
            gga_ref[hh] += gg[:, :HD]
            ggx_ref[hh] += gg[:, HD:]
            dxce_scr[0:tm, cs] = dxc
            gp8_ref[4:5, cs] += _rowsum(dxc)
            gp8_ref[3:4, cs] += _rowsum(dxc * xbe_scr[SUBLANES:SUBLANES + tm, cs])
            gp8_ref[2:3, cs] += _rowsum(dxc * xbe_scr[7:7 + tm, cs])
            gp8_ref[1:2, cs] += _rowsum(dxc * xbe_scr[6:6 + tm, cs])
            gp8_ref[0:1, cs] += _rowsum(dxc * xbe_scr[5:5 + tm, cs])
        for hh in range(BH):
            cs = slice(hh * HD, (hh + 1) * HD)
            dxb = p8_ref[3:4, cs] * dxce_scr[0:tm, cs]
            dxb = dxb + p8_ref[2:3, cs] * dxce_scr[1:1 + tm, cs]
            dxb = dxb + p8_ref[1:2, cs] * dxce_scr[2:2 + tm, cs]
            dxb = dxb + p8_ref[0:1, cs] * dxce_scr[3:3 + tm, cs]
            dzb_ref[:, cs] = dxb.astype(BF16)
        dxce_scr[tm:tm + SUBLANES, :] = dxce_scr[0:SUBLANES, :]

        dh1 = jnp.zeros((tm, D), F32)
        for k in range(NDEV):
            dh1 = dh1 + _dot_nt(dzb_ref[:, k * CB:(k + 1) * CB], win_ref[k])
        x1 = x1_ref[...]
        r1 = _rms(x1)
        dx1 = dx2 + _rms_bwd(dh1, x1, r1, nw_ref[...])
        dx1_ref[...] = dx1
        dx1b_ref[...] = dx1.astype(BF16)
        gnw_ref[...] += _rowsum(dh1 * x1 * r1)

    tile = lambda w: pl.BlockSpec((tm, w), lambda i: (nt - 1 - i, 0))
    prev = lambda w: pl.BlockSpec((SUBLANES, w), lambda i: (jnp.maximum((nt - 1 - i) * per - 1, 0), 0))
    whole = lambda *s: pl.BlockSpec(s, lambda i: (0,) * len(s))
    full = lambda: pltpu.VMEM((tm, BW), F32)
    ext = lambda: pltpu.VMEM((tm + SUBLANES, BW), F32)
    out, _ = _call(
        main, [], name="bwd_b", grid=(nt,),
        ins=[dx2, zb, zb, hs, hs, x1, nw, win8, p8, gcat, wout],
        in_specs=[tile(D), tile(2 * BW), prev(2 * BW), tile(BW), prev(BW), tile(D), _VMEM, _VMEM, _VMEM, _VMEM, _VMEM],
        out_shape=[_sds((s_len, D), F32), _sds((s_len, D), BF16), _sds((s_len, 2 * BW), BF16), _sds((SUBLANES, BW), F32),
                   _sds((BH, HD, HD), F32), _sds((BH, HD, HD), F32), _sds((1, D), F32)],
        out_specs=[tile(D), tile(D), tile(2 * BW), whole(SUBLANES, BW), whole(BH, HD, HD), whole(BH, HD, HD),
                   whole(1, D)],
        scratch=[ext(), ext(), ext(), full(), full(), full(), full(), full(), full(), full(), full(), full(), ext(),
                 pltpu.VMEM((1, BW), F32), pltpu.VMEM((1, BW), F32)])
    return out


def _transpose_into(dst_ref, src_ref, rows):
    s_len = src_ref.shape[0]
    for r0 in range(0, s_len, rows):
        dst_ref[:, r0:r0 + rows] = src_ref[r0:r0 + rows, :].astype(F32).T.astype(BF16)


def _wgrad(a, b, jobs, *, by_rows, per, name, relay_step=0):
    s_len, m = a.shape
    n = b.shape[1]
    r, cd = (m // NDEV, n) if by_rows else (m, n // NDEV)
    nsteps = NDEV // per
    at_rows = per * r if by_rows else m

    def main(i, ins, outs, scr):
        a_ref, b_ref = ins
        q_ref, acc_ref = outs
        at_scr, stage, mine, land, send_sems, recv_sems = scr
        x, y, c = _place()

        def to_sibling(pi):
            return pltpu.make_async_remote_copy(
                src_ref=stage.at[pi & 1], dst_ref=land.at[pi], send_sem=send_sems.at[pi], recv_sem=recv_sems.at[pi],
                device_id=(x, y, 1 - c), device_id_type=MESH)

        if by_rows:
            _transpose_into(at_scr, a_ref, 256)
        else:
            @pl.when(i == 0)
            def _():
                _transpose_into(at_scr, a_ref, 256)

        res = _dot(at_scr[...], b_ref[...]).astype(BF16)
        for k in range(per):
            blk = per * i + k
            pi, pc = blk >> 1, blk & 1
            val = res[k * r:(k + 1) * r, :] if by_rows else res

            @pl.when(pc != c)
            def _():
                @pl.when(pi >= 2)
                def _():
                    to_sibling(pi - 2).wait_send()

                stage[pi & 1] = val
                to_sibling(pi).start()

            @pl.when(pc == c)
            def _():
                mine[pi] = val

        @pl.when(i == nsteps - 1)
        def _():
            for p in range(4):
                to_sibling(p).wait_recv()
            to_sibling(2).wait_send()
            to_sibling(3).wait_send()
            _chip_sums(mine, land, q_ref, acc_ref, x, y)

    if by_rows:
        in_specs = [pl.BlockSpec((s_len, at_rows), lambda j: (0, j)), _VMEM]
    else:
        in_specs = [_VMEM, pl.BlockSpec((s_len, cd), lambda j: (0, j))]
    blk_vmem = lambda k: pltpu.VMEM((k, r, cd), BF16)
    (q, acc), job_out = _call(
        main, jobs, name=name, grid=(nsteps,), relay_step=relay_step, ins=[a, b], in_specs=in_specs,
        out_shape=[_sds((NCHIP_OTHER, r, cd), BF16), _sds((r, cd), F32)],
        out_specs=[pl.BlockSpec((NCHIP_OTHER, r, cd), lambda j: (0, 0, 0)), pl.BlockSpec((r, cd), lambda j: (0, 0))],
        scratch=[pltpu.VMEM((at_rows, s_len), BF16), blk_vmem(2), blk_vmem(4), blk_vmem(4),
                 pltpu.SemaphoreType.DMA((4,)), pltpu.SemaphoreType.DMA((4,))])
    return q, acc, job_out


def _adam_math(w, g, m, v):
    m = B1 * m + (1.0 - B1) * g
    v = B2 * v + (1.0 - B2) * (g * g)
    m_hat = m / (1.0 - B1 ** STEP)
    v_hat = v / (1.0 - B2 ** STEP)
    delta = (-LR) * (m_hat / (jnp.sqrt(v_hat) + ADAM_EPS) + WD * w)
    return delta, m, v


def _adam_big(w, acc, land, m, v, name):
    r, cd = w.shape
    rb = 256 if r % 256 == 0 else r
    nland = land.shape[0]

    def body(w_ref, acc_ref, land_ref, m_ref, v_ref, g_ref, d_ref, mo_ref, vo_ref):
        g = acc_ref[...]
        for j in range(nland):
            g = g + land_ref[j].astype(F32)
        g_ref[...] = g
        d_ref[...], mo_ref[...], vo_ref[...] = _adam_math(w_ref[...], g, m_ref[...], v_ref[...])

    blk = pl.BlockSpec((rb, cd), lambda i: (i, 0))
    blk3 = pl.BlockSpec((nland, rb, cd), lambda i: (0, i, 0))
    return pl.pallas_call(
        body, name=name, grid=(r // rb,), in_specs=[blk, blk, blk3, blk, blk], out_specs=[blk] * 4,
        out_shape=[_sds((r, cd), F32)] * 4,
        compiler_params=_params(dimension_semantics=("arbitrary",)),
    )(w, acc, land, m, v)


def _adam_small(groups):
    n = len(groups)

    def body(*refs):
        ins, outs = refs[:4 * n], refs[4 * n:]
        for k in range(n):
            w_ref, g_ref, m_ref, v_ref = ins[4 * k:4 * k + 4]
            d, mo, vo = _adam_math(w_ref[...], g_ref[...], m_ref[...], v_ref[...])
            outs[3 * k][...] = d
            outs[3 * k + 1][...] = mo
            outs[3 * k + 2][...] = vo

    flat = [a for grp in groups for a in grp]
    shapes = [_sds(grp[0].shape, F32) for grp in groups for _ in range(3)]
    res = pl.pallas_call(
        body, name="adam_small", in_specs=[_VMEM] * (4 * n), out_specs=[_VMEM] * (3 * n), out_shape=shapes,
        compiler_params=_params(),
    )(*flat)
    return [tuple(res[3 * k:3 * k + 3]) for k in range(n)]


TM_FWD_A = 256
RELAY_STEP_FWD_A = 4
RELAY_STEP_FWD_B = 2
TM_BWD_A = 256
RELAY_STEP_BWD_A = 3
TM_BWD_A_IN = 256
RELAY_STEP_BWD_A_IN = 4
TM_FWD_B = 256
TM_HEAD = 512
TM_BWD_B = 256


def _pack(parts, rows):
    flat = jnp.concatenate([p.reshape(-1) for p in parts])
    return jnp.pad(flat, (0, NDEV * rows * LANES - flat.shape[0])).reshape(NDEV, rows, LANES)


def _unpack(packed, shapes):
    flat, out, off = packed.reshape(-1), [], 0
    for s in shapes:
        size = 1
        for d in s:
            size *= d
        out.append(flat[off:off + size].reshape(s))
        off += size
    return out


def kernel(x, norm_w, a_w_in, a_ln_w, a_ln_b, a_w_s, a_b_s, a_w_out, b_w_in, b_conv_w, b_conv_b, b_gate_a_w, b_gate_a_b, b_gate_x_w, b_gate_x_b, b_lambda, b_w_out, norm_f_w, loss_target, m_norm_w, m_a_w_in, m_a_ln_w, m_a_ln_b, m_a_w_s, m_a_b_s, m_a_w_out, m_b_w_in, m_b_conv_w, m_b_conv_b, m_b_gate_a_w, m_b_gate_a_b, m_b_gate_x_w, m_b_gate_x_b, m_b_lambda, m_b_w_out, m_norm_f_w, v_norm_w, v_a_w_in, v_a_ln_w, v_a_ln_b, v_a_w_s, v_a_b_s, v_a_w_out, v_b_w_in, v_b_conv_w, v_b_conv_b, v_b_gate_a_w, v_b_gate_a_b, v_b_gate_x_w, v_b_gate_x_b, v_b_lambda, v_b_w_out, v_norm_f_w):
    me = 4 * lax.axis_index("x") + 2 * lax.axis_index("y") + lax.axis_index("c")
    xs, tgt = x[0], loss_target[0]
    nw0, nw1, nfw = norm_w[0:1], norm_w[1:2], norm_f_w.reshape(1, D)
    w_s, bst = a_w_s[0], a_b_s[0].T
    gcat = jnp.concatenate([b_gate_a_w[0], b_gate_x_w[0]], axis=-1).astype(BF16)

    p8_shard = jnp.concatenate([b_conv_w[0], b_conv_b, b_gate_a_b, b_gate_x_b, b_lambda], axis=0)
    ((win_a8, p8_all),) = _comm_only([_Gather([a_w_in[0].astype(BF16), p8_shard])], "gather_first")
    p8 = jnp.transpose(p8_all, (1, 0, 2)).reshape(SUBLANES, BW)

    (z, h0, ya), ((wout_a8, win_b8),) = _fwd_a(
        xs, nw0, win_a8, a_ln_w, a_ln_b, w_s, bst, [_Gather([a_w_out[0].astype(BF16), b_w_in[0].astype(BF16)])],
        tm=TM_FWD_A, relay_step=RELAY_STEP_FWD_A)
    wout_a = wout_a8.reshape(AW, D)
    (x1, zb, hs, h1, yb), ((wout_b8,),) = _fwd_b(
        xs, ya, wout_a, nw1, win_b8, p8, gcat, [_Gather([b_w_out[0].astype(BF16)])],
        tm=TM_FWD_B, relay_step=RELAY_STEP_FWD_B)
    wout_b = wout_b8.reshape(BW, D)
    dx2, dx2b, loss, g_nfw = _head(x1, yb, wout_b, nfw, tgt, tm=TM_HEAD)

    dx1, dx1b, dzb, g_p8, g_ga, g_gx, g_nw1 = _bwd_b(dx2, zb, hs, x1, nw1, win_b8, p8, gcat, wout_b, tm=TM_BWD_B)
    q_wout_b, acc_wout_b, _ = _wgrad(yb, dx2b, [], by_rows=True, per=2, name="wgrad_b_out")
    shapes_b = [(1, D), (1, D), (SUBLANES, BW), (1, 1)]
    pack_b = _pack([g_nfw, g_nw1, g_p8, loss], 16)
    small_b = _InChip([g_ga.reshape(NDEV, -1, HD), g_gx.reshape(NDEV, -1, HD), pack_b])
    q_win_b, acc_win_b, (sm_b, (l_wout_b,)) = _wgrad(h1, dzb, [small_b, _Exchange([q_wout_b])], by_rows=False, per=1,
                                                      name="wgrad_b_in")
    qs_b, accs_b = sm_b[:3], sm_b[3:]

    (dz, g_lnw, g_lnb, g_ws, g_bst), (lands_b, (l_win_b,)) = _bwd_a(
        dx1b, z, a_ln_w, a_ln_b, w_s, bst, wout_a, [_Exchange(qs_b), _ExchangeVia(q_win_b)],
        tm=TM_BWD_A, relay_step=RELAY_STEP_BWD_A)
    shapes_a = [(1, AW), (1, AW), (CH, G)]
    pack_a = _pack([g_lnw, g_lnb, g_bst], 8)
    q_wout_a, acc_wout_a, (red_b, sm_a) = _wgrad(
        ya, dx1b, [_SumGather(accs_b, lands_b), _InChip([g_ws, pack_a])], by_rows=True, per=1,
        name="wgrad_a_out", relay_step=2)
    qs_a, accs_a = [q_wout_a, *sm_a[:2]], [acc_wout_a, *sm_a[2:]]
    q_win_a, acc_win_a, (lands_a,) = _wgrad(h0, dz, [_Exchange(qs_a)], by_rows=False, per=1, name="wgrad_a_in")
    (gx, g_nw0), (red_a, (l_win_a,)) = _bwd_a_in(
        dz, dx1, xs, nw0, win_a8, [_SumGather(accs_a[1:], lands_a[1:]), _ExchangeVia(q_win_a)],
        tm=TM_BWD_A_IN, relay_step=RELAY_STEP_BWD_A_IN)
    g_nw0 = _allreduce_direct(g_nw0, "allreduce_norm_w0")

    r_ga, r_gx, r_pack_b = red_b
    r_nfw, r_nw1, r_p8, loss = _unpack(r_pack_b, shapes_b)
    r_ws, r_pack_a = red_a
    r_lnw, r_lnb, r_bst = _unpack(r_pack_a, shapes_a)
    g_p8 = lax.dynamic_slice_in_dim(r_p8, me * (BW // NDEV), BW // NDEV, axis=1)
    loss = loss[0, 0]

    weights = dict(norm_w=norm_w, a_w_in=a_w_in, a_ln_w=a_ln_w, a_ln_b=a_ln_b, a_w_s=a_w_s, a_b_s=a_b_s, a_w_out=a_w_out,
                   b_w_in=b_w_in, b_conv_w=b_conv_w, b_conv_b=b_conv_b, b_gate_a_w=b_gate_a_w, b_gate_a_b=b_gate_a_b,
                   b_gate_x_w=b_gate_x_w, b_gate_x_b=b_gate_x_b, b_lambda=b_lambda, b_w_out=b_w_out, norm_f_w=norm_f_w)
    mom1 = dict(norm_w=m_norm_w, a_w_in=m_a_w_in, a_ln_w=m_a_ln_w, a_ln_b=m_a_ln_b, a_w_s=m_a_w_s, a_b_s=m_a_b_s,
                a_w_out=m_a_w_out, b_w_in=m_b_w_in, b_conv_w=m_b_conv_w, b_conv_b=m_b_conv_b, b_gate_a_w=m_b_gate_a_w,
                b_gate_a_b=m_b_gate_a_b, b_gate_x_w=m_b_gate_x_w, b_gate_x_b=m_b_gate_x_b, b_lambda=m_b_lambda,
                b_w_out=m_b_w_out, norm_f_w=m_norm_f_w)
    mom2 = dict(norm_w=v_norm_w, a_w_in=v_a_w_in, a_ln_w=v_a_ln_w, a_ln_b=v_a_ln_b, a_w_s=v_a_w_s, a_b_s=v_a_b_s,
                a_w_out=v_a_w_out, b_w_in=v_b_w_in, b_conv_w=v_b_conv_w, b_conv_b=v_b_conv_b, b_gate_a_w=v_b_gate_a_w,
                b_gate_a_b=v_b_gate_a_b, b_gate_x_w=v_b_gate_x_w, b_gate_x_b=v_b_gate_x_b, b_lambda=v_b_lambda,
                b_w_out=v_b_w_out, norm_f_w=v_norm_f_w)
    names = list(weights)

    def as2d(a):
        return a.reshape(-1, a.shape[-1])

    upd, grads = {}, {}
    for k, acc, land in (("a_w_in", acc_win_a, l_win_a), ("a_w_out", accs_a[0], lands_a[0]),
                         ("b_w_in", acc_win_b, l_win_b), ("b_w_out", acc_wout_b, l_wout_b)):
        g, d, mo, vo = _adam_big(as2d(weights[k]), acc, land, as2d(mom1[k]), as2d(mom2[k]), "adam_" + k)
        grads[k] = g[None]
        upd[k] = (d, mo, vo)
    grads.update(
        norm_w=jnp.concatenate([g_nw0, r_nw1], axis=0), a_ln_w=r_lnw, a_ln_b=r_lnb,
        a_w_s=r_ws.reshape(1, G, CH, CH), a_b_s=r_bst.T[None],
        b_conv_w=g_p8[None, 0:4], b_conv_b=g_p8[4:5], b_gate_a_w=r_ga.reshape(1, BH, HD, HD), b_gate_a_b=g_p8[5:6],
        b_gate_x_w=r_gx.reshape(1, BH, HD, HD), b_gate_x_b=g_p8[6:7], b_lambda=g_p8[7:8], norm_f_w=r_nfw.reshape(D))
    small_names = [k for k in names if k not in upd]
    res = _adam_small([(as2d(weights[k]), as2d(grads[k]), as2d(mom1[k]), as2d(mom2[k])) for k in small_names])
    for k, r3 in zip(small_names, res):
        upd[k] = r3
    deltas = [upd[k][0].reshape(weights[k].shape) for k in names]
    new_m = [upd[k][1].reshape(weights[k].shape) for k in names]
    new_v = [upd[k][2].reshape(weights[k].shape) for k in names]
    return (loss, gx[None], *[grads[k] for k in names], *deltas, *new_m, *new_v)
```

```python
import jax
import jax.numpy as jnp
from jax import lax
from jax.experimental import pallas as pl
from jax.experimental.pallas import tpu as pltpu

F32 = jnp.float32
BF16 = jnp.bfloat16
MESH = pl.DeviceIdType.MESH

NDEV = 8
NCHIP_OTHER = 3
D = 1024
AW = 2048
G = 8
GD = AW // G
CH = 128
BW = 1536
BH = 12
HD = BW // BH
CA = 3 * AW // NDEV
CB = 2 * BW // NDEV
RMS_EPS = 1e-6
LN_EPS = 1e-5
RG_C = 8.0
LR, B1, B2, ADAM_EPS, WD, STEP = 0.001, 0.9, 0.999, 1e-08, 0.01, 10
V7X_VMEM_BYTES = 64 * 1024 * 1024
VMEM_LIMIT = V7X_VMEM_BYTES - 8 * 1024 * 1024
SUBLANES = 8
LANES = 128
BF16_ROWS = 16
GELU_C = 0.7978845608028654
GELU_K = 0.044715

_VMEM = pl.BlockSpec(memory_space=pltpu.VMEM)
_HBM = pl.BlockSpec(memory_space=pltpu.HBM)


def _sds(shape, dtype):
    return jax.ShapeDtypeStruct(tuple(shape), dtype)


def _params(**kw):
    return pltpu.CompilerParams(vmem_limit_bytes=VMEM_LIMIT, **kw)


def _gelu_t(z):
    t = jnp.tanh(GELU_C * (z + GELU_K * (z * z * z)))
    return 0.5 * z * (1.0 + t), t


def _dgelu(z, t):
    return 0.5 * (1.0 + t) + 0.5 * z * (1.0 - t * t) * (GELU_C * (1.0 + 3.0 * GELU_K * z * z))


def _sigmoid(v):
    return 0.5 * jnp.tanh(0.5 * v) + 0.5


def _softplus_neg(lam):
    return jnp.maximum(-lam, 0.0) + jnp.log1p(jnp.exp(-jnp.abs(lam)))


def _dot(a, b):
    return jnp.dot(a, b, preferred_element_type=F32)


def _dot_nt(a, b):
    return lax.dot_general(a, b, (((1,), (1,)), ((), ())), preferred_element_type=F32)


def _rowsum(v):
    return jnp.sum(v, axis=0, keepdims=True)


def _causal_mask():
    r = lax.broadcasted_iota(jnp.int32, (CH, CH), 0)
    c = lax.broadcasted_iota(jnp.int32, (CH, CH), 1)
    return r >= c


def _rms(x):
    return lax.rsqrt(jnp.mean(x * x, axis=-1, keepdims=True) + RMS_EPS)


def _rms_bwd(dh, x, r, nw):
    gy = dh * nw
    return r * gy - x * (r * r * r) * jnp.mean(gy * x, axis=-1, keepdims=True)


def _place():
    return lax.axis_index("x"), lax.axis_index("y"), lax.axis_index("c")


def _other_chips(x, y):
    return [(1 - x, y), (x, 1 - y), (1 - x, 1 - y)]


GATHER_SLOTS = 10


def _gather_ops(ins, outs, send_sems, recv_sems, local_sems):
    n = len(ins)
    x, y, c = _place()
    sibling = (x, y, 1 - c)
    xn, yn, dg = _other_chips(x, y)
    split = [ins[i].shape[0] % (2 * BF16_ROWS) == 0 for i in range(n)]

    def blk(chip, core):
        return 4 * chip[0] + 2 * chip[1] + core

    me = blk((x, y), c)

    def part(ref, i, half):
        if half is None:
            return ref
        h = ins[i].shape[0] // 2
        return ref.at[pl.ds(half * h, h)]

    def copy(i, k, block, to, half=None, src=None):
        dst = part(outs[i].at[block], i, half)
        return pltpu.make_async_remote_copy(
            src_ref=dst if src is None else part(src, i, half), dst_ref=dst,
            send_sem=send_sems.at[k, i], recv_sem=recv_sems.at[k, i], device_id=to, device_id_type=MESH)

    def first_copies():
        mine = [pltpu.make_async_copy(ins[i], outs[i].at[me], local_sems.at[i]) for i in range(n)]
        first = []
        for i in range(n):
            first.append(copy(i, 0, me, sibling, src=ins[i]))
            if split[i]:
                first.append(copy(i, 1, me, (*xn, c), 0, ins[i]))
                first.append(copy(i, 3, me, (*yn, c), 1, ins[i]))
                first.append(copy(i, 2, me, (*xn, c), 1, ins[i]))
                first.append(copy(i, 4, me, (*yn, c), 0, ins[i]))
            else:
                first.append(copy(i, 1, me, (*xn, c), None, ins[i]))
                first.append(copy(i, 3, me, (*yn, c), None, ins[i]))
                first.append(copy(i, 5, me, (*dg, c), None, ins[i]))
        return mine, first

    def onward():
        out = []
        for i in range(n):
            if split[i]:
                out.append(copy(i, 5, blk(xn, c), (*yn, c), 0))
                out.append(copy(i, 6, blk(yn, c), (*xn, c), 1))
        return out

    def start():
        mine, first = first_copies()
        for cp in mine + first:
            cp.start()

    def relay():
        sends = onward()
        for i in range(n):
            if split[i]:
                copy(i, 1, blk(xn, c), sibling, 0).wait_recv()
                sends.pop(0).start()
                copy(i, 3, blk(yn, c), sibling, 1).wait_recv()
                sends.pop(0).start()

    def finish():
        mine, first = first_copies()
        passed = []

        def pass_on(i, j, chip):
            fwd = copy(i, 7 + j, blk(chip, c), sibling)
            fwd.start()
            passed.append(fwd)

        for i in range(n):
            if split[i]:
                copy(i, 2, blk(xn, c), sibling, 1).wait_recv()
                pass_on(i, 0, xn)
                copy(i, 4, blk(yn, c), sibling, 0).wait_recv()
                pass_on(i, 1, yn)
                copy(i, 5, blk(dg, c), sibling, 0).wait_recv()
                copy(i, 6, blk(dg, c), sibling, 1).wait_recv()
                pass_on(i, 2, dg)
            else:
                copy(i, 1, blk(xn, c), sibling).wait_recv()
                pass_on(i, 0, xn)
                copy(i, 3, blk(yn, c), sibling).wait_recv()
                pass_on(i, 1, yn)
                copy(i, 5, blk(dg, c), sibling).wait_recv()
                pass_on(i, 2, dg)
        for i in range(n):
            copy(i, 0, blk((x, y), 1 - c), sibling).wait_recv()
            for j, chip in enumerate((xn, yn, dg)):
                copy(i, 7 + j, blk(chip, 1 - c), sibling).wait_recv()
        for cp in first + passed + onward():
            cp.wait_send()
        for cp in mine:
            cp.wait()

    return start, relay, finish


def _gather_sems(n):
    return [pltpu.SemaphoreType.DMA((GATHER_SLOTS, n)), pltpu.SemaphoreType.DMA((GATHER_SLOTS, n)),
            pltpu.SemaphoreType.DMA((n,))]


class _Gather:
    def __init__(self, shards):
        n = len(shards)
        self.ins, self.in_specs = list(shards), [_HBM] * n
        self.out_shape = [_sds((NDEV,) + s.shape, s.dtype) for s in shards]
        self.out_specs = [_HBM] * n
        self.scratch = _gather_sems(n)

    def ops(self, ins, outs, scr):
        return _gather_ops(ins, outs, *scr)


class _Exchange:
    def __init__(self, qs):
        n = len(qs)
        self.ins, self.in_specs = list(qs), [_HBM] * n
        self.out_shape = [_sds(q.shape, q.dtype) for q in qs]
        self.out_specs = [_HBM] * n
        self.scratch = [pltpu.SemaphoreType.DMA((NCHIP_OTHER, n)), pltpu.SemaphoreType.DMA((NCHIP_OTHER, n))]

    def ops(self, ins, outs, scr):
        send_sems, recv_sems = scr
        n = len(ins)
        x, y, c = _place()
        chips = _other_chips(x, y)

        def copies():
            return [pltpu.make_async_remote_copy(
                src_ref=ins[i].at[j], dst_ref=outs[i].at[j], send_sem=send_sems.at[j, i],
                recv_sem=recv_sems.at[j, i], device_id=(*chips[j], c), device_id_type=MESH)
                for i in range(n) for j in range(NCHIP_OTHER)]

        def start():
            for cp in copies():
                cp.start()

        def finish():
            cps = copies()
            for cp in cps:
                cp.wait_recv()
            for cp in cps:
                cp.wait_send()

        return start, lambda: None, finish


class _ExchangeVia:
    def __init__(self, q):
        _, r, cd = q.shape
        half = (2, r // 2, cd)
        self.ins, self.in_specs = [q], [_HBM]
        self.out_shape, self.out_specs = [_sds((2, r, cd), q.dtype)], [_HBM]
        self.scratch = [pltpu.VMEM(half, q.dtype), pltpu.VMEM(half, q.dtype), pltpu.VMEM(half, q.dtype),
                        pltpu.SemaphoreType.DMA((6,)), pltpu.SemaphoreType.DMA((6,)), pltpu.SemaphoreType.DMA((2,))]

    def ops(self, ins, outs, scr):
        (q,), (land,) = ins, outs
        relayed, own, comb, send_sems, recv_sems, local_sems = scr
        h = q.shape[1] // 2
        x, y, c = _place()
        xn, yn, _ = _other_chips(x, y)
        h0, h1 = pl.ds(0, h), pl.ds(h, h)

        def remote(k, src, dst, chip):
            return pltpu.make_async_remote_copy(src_ref=src, dst_ref=dst, send_sem=send_sems.at[k],
                                                recv_sem=recv_sems.at[k], device_id=(*chip, c), device_id_type=MESH)

        def via():
            return [remote(2, q.at[2, h0], relayed.at[0], xn), remote(3, q.at[2, h1], relayed.at[1], yn)]

        def direct():
            return [remote(0, q.at[0, h0], land.at[0, h0], xn), remote(1, q.at[1, h1], land.at[1, h1], yn)]

        def second():
            return [remote(4, comb.at[0], land.at[1, h0], yn), remote(5, comb.at[1], land.at[0, h1], xn)]

        def mine():
            return [pltpu.make_async_copy(q.at[1, h0], own.at[0], local_sems.at[0]),
                    pltpu.make_async_copy(q.at[0, h1], own.at[1], local_sems.at[1])]

        def start():
            for cp in via() + direct() + mine():
                cp.start()

        def relay():
            arrived, loaded, onward = via(), mine(), second()
            for k in range(2):
                arrived[k].wait_recv()
                loaded[k].wait()
                comb[k] = (own[k].astype(F32) + relayed[k].astype(F32)).astype(comb.dtype)
                onward[k].start()

        def finish():
            landing = direct() + second()
            for cp in landing:
                cp.wait_recv()
            for cp in via() + landing:
                cp.wait_send()

        return start, relay, finish


class _SumGather:
    def __init__(self, accs, lands):
        n = len(accs)
        self.n = n
        self.ins, self.in_specs = list(accs) + list(lands), [_VMEM] * (2 * n)
        self.out_shape = [_sds((NDEV,) + a.shape, a.dtype) for a in accs]
        self.out_specs = [_HBM] * n
        self.scratch = [pltpu.VMEM(a.shape, a.dtype) for a in accs] + _gather_sems(n)

    def ops(self, ins, outs, scr):
        n = self.n
        accs, lands, mine = ins[:n], ins[n:], scr[:n]
        g_start, relay, finish = _gather_ops(mine, outs, *scr[n:])

        def start():
            for i in range(n):
                mine[i][...] = accs[i][...] + lands[i][0] + lands[i][1] + lands[i][2]
            g_start()

        return start, relay, finish


def _call(main, jobs, *, name, grid, ins, in_specs, out_shape, out_specs, scratch, relay_step=0):
    nsteps = grid[0] if grid else 1
    n_in, n_out, n_scr = len(ins), len(out_shape), len(scratch)

    def body(*refs):
        pos = [0]

        def take(k):
            r = refs[pos[0]:pos[0] + k]
            pos[0] += k
            return r

        m_in = take(n_in)
        j_in = [take(len(j.ins)) for j in jobs]
        m_out = take(n_out)
        j_out = [take(len(j.out_shape)) for j in jobs]
        m_scr = take(n_scr)
        j_scr = [take(len(j.scratch)) for j in jobs]
        ops = [j.ops(a, b, s) for j, a, b, s in zip(jobs, j_in, j_out, j_scr)]
        i = pl.program_id(0) if grid else 0
        if not grid:
            for o in ops:
                o[0]()
            main(i, m_in, m_out, m_scr)
            for o in ops:
                o[1]()
            for o in ops:
                o[2]()
            return

        if ops:
            @pl.when(i == 0)
            def _():
                for o in ops:
                    o[0]()

        main(i, m_in, m_out, m_scr)

        if ops:
            @pl.when(i == min(relay_step, nsteps - 1))
            def _():
                for o in ops:
                    o[1]()

            @pl.when(i == nsteps - 1)
            def _():
                for o in ops:
                    o[2]()

    extra = dict(dimension_semantics=("arbitrary",)) if grid else {}
    res = pl.pallas_call(
        body, name=name, grid=grid,
        in_specs=list(in_specs) + [s for j in jobs for s in j.in_specs],
        out_specs=list(out_specs) + [s for j in jobs for s in j.out_specs],
        out_shape=list(out_shape) + [s for j in jobs for s in j.out_shape],
        scratch_shapes=list(scratch) + [s for j in jobs for s in j.scratch],
        compiler_params=_params(**extra),
    )(*ins, *[a for j in jobs for a in j.ins])
    main_out, rest, job_out = res[:n_out], res[n_out:], []
    for j in jobs:
        k = len(j.out_shape)
        job_out.append(rest[:k])
        rest = rest[k:]
    return main_out, job_out


def _comm_only(jobs, name):
    _, job_out = _call(lambda i, a, b, s: None, jobs, name=name, grid=(), ins=[], in_specs=[], out_shape=[],
                       out_specs=[], scratch=[])
    return job_out


class _InChip:
    def __init__(self, ps):
        n = len(ps)
        self.n = n
        blk = [p.shape[1:] for p in ps]
        self.ins, self.in_specs = list(ps), [_HBM] * n
        self.out_shape = [_sds((NCHIP_OTHER,) + b, p.dtype) for b, p in zip(blk, ps)] + [_sds(b, F32) for b in blk]
        self.out_specs = [_VMEM] * (2 * n)
        self.scratch = ([pltpu.VMEM((4,) + b, p.dtype) for b, p in zip(blk, ps)] * 2
                        + [pltpu.SemaphoreType.DMA((4, n))] * 3)

    def ops(self, ins, outs, scr):
        n = self.n
        q_refs, acc_refs = outs[:n], outs[n:]
        mines, lands = scr[:n], scr[n:2 * n]
        send_sems, recv_sems, local_sems = scr[2 * n:]
        x, y, c = _place()
        sibling = (x, y, 1 - c)

        def copies():
            out = []
            for i in range(n):
                for pi in range(4):
                    loc = pltpu.make_async_copy(ins[i].at[2 * pi + c], mines[i].at[pi], local_sems.at[pi, i])
                    cp = pltpu.make_async_remote_copy(
                        src_ref=ins[i].at[2 * pi + (1 - c)], dst_ref=lands[i].at[pi],
                        send_sem=send_sems.at[pi, i], recv_sem=recv_sems.at[pi, i],
                        device_id=sibling, device_id_type=MESH)
                    out.append((loc, cp))
            return out

        def start():
            for loc, cp in copies():
                loc.start()
                cp.start()

        def finish():
            pairs = copies()
            for loc, cp in pairs:
                loc.wait()
                cp.wait_recv()
            for i in range(n):
                _chip_sums(mines[i], lands[i], q_refs[i], acc_refs[i], x, y)
            for _, cp in pairs:
                cp.wait_send()

        return start, lambda: None, finish


def _chip_sums(mine, land, q_ref, acc_ref, x, y):
    for j, (qx, qy) in enumerate(_other_chips(x, y)):
        qi = 2 * qx + qy
        q_ref[j] = (mine[qi].astype(F32) + land[qi].astype(F32)).astype(q_ref.dtype)
    mi = 2 * x + y
    acc_ref[...] = mine[mi].astype(F32) + land[mi].astype(F32)


def _allreduce_direct(v, name):
    def body(v_ref, o_ref, buf, send_sems, recv_sems):
        x, y, c = _place()
        me = 4 * x + 2 * y + c
        buf[me] = v_ref[...]
        cps = []
        for k in range(1, NDEV):
            fx, fy, fc = (k >> 2) & 1, (k >> 1) & 1, k & 1
            peer = ((1 - x) if fx else x, (1 - y) if fy else y, (1 - c) if fc else c)
            cps.append((peer, pltpu.make_async_remote_copy(
                src_ref=buf.at[me], dst_ref=buf.at[me], send_sem=send_sems.at[k - 1], recv_sem=recv_sems.at[k - 1],
                device_id=peer, device_id_type=MESH)))
        for _, cp in cps:
            cp.start()
        for k, (peer, _) in enumerate(cps):
            theirs = 4 * peer[0] + 2 * peer[1] + peer[2]
            pltpu.make_async_remote_copy(
                src_ref=buf.at[theirs], dst_ref=buf.at[theirs], send_sem=send_sems.at[k], recv_sem=recv_sems.at[k],
                device_id=peer, device_id_type=MESH).wait_recv()
        acc = buf[0]
        for j in range(1, NDEV):
            acc = acc + buf[j]
        o_ref[...] = acc
        for _, cp in cps:
            cp.wait_send()

    return pl.pallas_call(
        body, name=name, in_specs=[_VMEM], out_specs=_VMEM, out_shape=_sds(v.shape, v.dtype),
        scratch_shapes=[pltpu.VMEM((NDEV,) + v.shape, v.dtype), pltpu.SemaphoreType.DMA((NDEV - 1,)),
                        pltpu.SemaphoreType.DMA((NDEV - 1,))],
        compiler_params=_params(),
    )(v)


def _fwd_a(x, nw, win8, lnw, lnb, ws, bst, jobs, *, tm, relay_step):
    s_len = x.shape[0]
    nt = s_len // tm
    nch = tm // CH

    def main(i, ins, outs, scr):
        x_ref, nw_ref, win_ref, lnw_ref, lnb_ref, ws_ref, bst_ref = ins
        z_ref, h_ref, y_ref = outs
        wc_scr, gv_scr = scr

        @pl.when(i == 0)
        def _():
            m = _causal_mask()
            for g in range(G):
                wc_scr[g] = jnp.where(m, ws_ref[g], 0.0).astype(BF16)

        x = x_ref[...]
        h = (x * _rms(x) * nw_ref[...]).astype(BF16)
        h_ref[...] = h
        for k in range(NDEV):
            z_ref[:, k * CA:(k + 1) * CA] = _dot(h, win_ref[k])

        ssum = jnp.zeros((tm, 1), F32)
        for g in range(G):
            gv = _gelu_t(z_ref[:, AW + g * GD:AW + (g + 1) * GD])[0]
            gv_scr[:, g * GD:(g + 1) * GD] = gv
            ssum = ssum + jnp.sum(gv, axis=-1, keepdims=True)
        mu = ssum * (1.0 / AW)
        vsum = jnp.zeros((tm, 1), F32)
        for g in range(G):
            dlt = gv_scr[:, g * GD:(g + 1) * GD] - mu
            vsum = vsum + jnp.sum(dlt * dlt, axis=-1, keepdims=True)
        rstd = lax.rsqrt(vsum * (1.0 / AW) + LN_EPS)

        for g in range(G):
            cs = slice(g * GD, (g + 1) * GD)
            v = (gv_scr[:, cs] - mu) * rstd * lnw_ref[:, cs] + lnb_ref[:, cs]
            vb = v.astype(BF16)
            u = _gelu_t(z_ref[:, cs])[0]
            zg = z_ref[:, 2 * AW + g * GD:2 * AW + (g + 1) * GD]
            sg = zg * _sigmoid(zg)
            for n in range(nch):
                rs = slice(n * CH, (n + 1) * CH)
                s = _dot(wc_scr[g], vb[rs, :]) + bst_ref[:, g:g + 1]
                y_ref[rs, cs] = (u[rs, :] * s * sg[rs, :]).astype(BF16)

    tile = lambda w: pl.BlockSpec((tm, w), lambda i: (i, 0))
    return _call(
        main, jobs, name="fwd_a", grid=(nt,), relay_step=relay_step,
        ins=[x, nw, win8, lnw, lnb, ws, bst], in_specs=[tile(D), _VMEM, _VMEM, _VMEM, _VMEM, _VMEM, _VMEM],
        out_shape=[_sds((s_len, 3 * AW), F32), _sds((s_len, D), BF16), _sds((s_len, AW), BF16)],
        out_specs=[tile(3 * AW), tile(D), tile(AW)],
        scratch=[pltpu.VMEM((G, CH, CH), BF16), pltpu.VMEM((tm, AW), F32)])


def _bwd_a(dx1, z, lnw, lnb, ws, bst, wout, jobs, *, tm, relay_step):
    s_len = dx1.shape[0]
    nt = s_len // tm
    nch = tm // CH

    def main(i, ins, outs, scr):
        dx1_ref, z_ref, lnw_ref, lnb_ref, ws_ref, bst_ref, wout_ref = ins
        dz_ref, glnw_ref, glnb_ref, gws_ref, gbst_ref = outs
        wc_scr, wct_scr, vh_scr, dgv_scr, dy_scr, dv_scr, gbs_acc, gwc_acc = scr

        @pl.when(i == 0)
        def _():
            m = _causal_mask()
            for g in range(G):
                wm = jnp.where(m, ws_ref[g], 0.0)
                wc_scr[g] = wm.astype(BF16)
                wct_scr[g] = wm.T.astype(BF16)
            glnw_ref[...] = jnp.zeros_like(glnw_ref)
            glnb_ref[...] = jnp.zeros_like(glnb_ref)
            gbs_acc[...] = jnp.zeros_like(gbs_acc)
            gwc_acc[...] = jnp.zeros_like(gwc_acc)

        dy_scr[...] = _dot_nt(dx1_ref[...], wout_ref[...])

        ssum = jnp.zeros((tm, 1), F32)
        for g in range(G):
            cs = slice(g * GD, (g + 1) * GD)
            zv = z_ref[:, AW + g * GD:AW + (g + 1) * GD]
            gv, t = _gelu_t(zv)
            vh_scr[:, cs] = gv
            dgv_scr[:, cs] = _dgelu(zv, t)
            ssum = ssum + jnp.sum(gv, axis=-1, keepdims=True)
        mu = ssum * (1.0 / AW)
        vsum = jnp.zeros((tm, 1), F32)
        for g in range(G):
            dlt = vh_scr[:, g * GD:(g + 1) * GD] - mu
            vsum = vsum + jnp.sum(dlt * dlt, axis=-1, keepdims=True)
        rstd = lax.rsqrt(vsum * (1.0 / AW) + LN_EPS)

        m1 = jnp.zeros((tm, 1), F32)
        m2 = jnp.zeros((tm, 1), F32)
        for g in range(G):
            cs = slice(g * GD, (g + 1) * GD)
            gs = slice(2 * AW + g * GD, 2 * AW + (g + 1) * GD)
            vhat = (vh_scr[:, cs] - mu) * rstd
            vh_scr[:, cs] = vhat
            vb = (vhat * lnw_ref[:, cs] + lnb_ref[:, cs]).astype(BF16)
            zu = z_ref[:, cs]
            u, tu = _gelu_t(zu)
            zg = z_ref[:, gs]
            sig = _sigmoid(zg)
            sg = zg * sig
            dy = dy_scr[:, cs]
            dsf = dy * u * sg
            dsb = dsf.astype(BF16)
            dvs = []
            for n in range(nch):
                rs = slice(n * CH, (n + 1) * CH)
                s = _dot(wc_scr[g], vb[rs, :]) + bst_ref[:, g:g + 1]
                dys = dy[rs, :] * s
                dz_ref[rs, cs] = (dys * sg[rs, :] * _dgelu(zu[rs, :], tu[rs, :])).astype(BF16)
                dz_ref[rs, gs] = (dys * u[rs, :] * (sig[rs, :] * (1.0 + zg[rs, :] * (1.0 - sig[rs, :])))).astype(BF16)
                gbs_acc[g] += dsf[rs, :]
                gwc_acc[g] += _dot_nt(dsb[rs, :], vb[rs, :])
                dvs.append(_dot(wct_scr[g], dsb[rs, :]))
            dv = jnp.concatenate(dvs, axis=0) if nch > 1 else dvs[0]
            glnw_ref[:, cs] += _rowsum(dv * vhat)
            glnb_ref[:, cs] += _rowsum(dv)
            dvh = dv * lnw_ref[:, cs]
            dv_scr[:, cs] = dvh
            m1 = m1 + jnp.sum(dvh, axis=-1, keepdims=True)
            m2 = m2 + jnp.sum(dvh * vhat, axis=-1, keepdims=True)
        m1 = m1 * (1.0 / AW)
        m2 = m2 * (1.0 / AW)
        for g in range(G):
            cs = slice(g * GD, (g + 1) * GD)
            dgv = rstd * (dv_scr[:, cs] - m1 - vh_scr[:, cs] * m2)
            dz_ref[:, AW + g * GD:AW + (g + 1) * GD] = (dgv * dgv_scr[:, cs]).astype(BF16)

        @pl.when(i == nt - 1)
        def _():
            m = _causal_mask()
            for g in range(G):
                gws_ref[g] = jnp.where(m, gwc_acc[g], 0.0)
                gbst_ref[:, g:g + 1] = jnp.sum(gbs_acc[g], axis=-1, keepdims=True)

    tile = lambda w: pl.BlockSpec((tm, w), lambda i: (i, 0))
    whole = lambda *s: pl.BlockSpec(s, lambda i: (0,) * len(s))
    big = lambda dt: pltpu.VMEM((tm, AW), dt)
    return _call(
        main, jobs, name="bwd_a", grid=(nt,), relay_step=relay_step,
        ins=[dx1, z, lnw, lnb, ws, bst, wout], in_specs=[tile(D), tile(3 * AW), _VMEM, _VMEM, _VMEM, _VMEM, _VMEM],
        out_shape=[_sds((s_len, 3 * AW), BF16), _sds((1, AW), F32), _sds((1, AW), F32), _sds((G, CH, CH), F32),
                   _sds((CH, G), F32)],
        out_specs=[tile(3 * AW), whole(1, AW), whole(1, AW), whole(G, CH, CH), whole(CH, G)],
        scratch=[pltpu.VMEM((G, CH, CH), BF16), pltpu.VMEM((G, CH, CH), BF16), big(F32), big(F32), big(F32), big(F32),
                 pltpu.VMEM((G, CH, GD), F32), pltpu.VMEM((G, CH, CH), F32)])


def _bwd_a_in(dz, dx1, x, nw, win8, jobs, *, tm, relay_step):
    s_len = x.shape[0]
    nt = s_len // tm

    def main(i, ins, outs, scr):
        dz_ref, dx1_ref, x_ref, nw_ref, win_ref = ins
        gx_ref, gnw_ref = outs

        @pl.when(i == 0)
        def _():
            gnw_ref[...] = jnp.zeros_like(gnw_ref)

        dh = jnp.zeros((tm, D), F32)
        for k in range(NDEV):
            dh = dh + _dot_nt(dz_ref[:, k * CA:(k + 1) * CA], win_ref[k])
        x = x_ref[...]
        r = _rms(x)
        gx_ref[...] = dx1_ref[...] + _rms_bwd(dh, x, r, nw_ref[...])
        gnw_ref[...] += _rowsum(dh * x * r)

    tile = lambda w: pl.BlockSpec((tm, w), lambda i: (i, 0))
    return _call(
        main, jobs, name="bwd_a_in", grid=(nt,), relay_step=relay_step,
        ins=[dz, dx1, x, nw, win8], in_specs=[tile(3 * AW), tile(D), tile(D), _VMEM, _VMEM],
        out_shape=[_sds((s_len, D), F32), _sds((1, D), F32)],
        out_specs=[tile(D), pl.BlockSpec((1, D), lambda i: (0, 0))], scratch=[])


def _conv(p8_ref, cs, xb, xm1, xm2, xm3):
    xc = p8_ref[4:5, cs] + p8_ref[3:4, cs] * xb
    xc = xc + p8_ref[0:1, cs] * xm3
    xc = xc + p8_ref[1:2, cs] * xm2
    return xc + p8_ref[2:3, cs] * xm1


def _gates(p8_ref, gcat_ref, hh, xc):
    cs = slice(hh * HD, (hh + 1) * HD)
    pre = _dot(xc.astype(BF16), gcat_ref[hh])
    r = _sigmoid(pre[:, :HD] + p8_ref[5:6, cs])
    ig = _sigmoid(pre[:, HD:] + p8_ref[6:7, cs])
    sp = _softplus_neg(p8_ref[7:8, cs])
    la = (-RG_C) * r * sp
    a = jnp.exp(la)
    half_log = 0.5 * jnp.log(jnp.tanh(-la) * (1.0 + a * a))
    return r, ig, sp, a, jnp.exp(half_log), jnp.exp(-half_log)


def _scan_rows(a_ref, b_ref, out_ref, carry, tm, reverse):
    row = lax.broadcasted_iota(jnp.int32, (SUBLANES, BW), 0)
    ngrp = tm // SUBLANES

    def step(j, cr):
        jj = (ngrp - 1 - j) if reverse else j
        off = pl.multiple_of(jj * SUBLANES, SUBLANES)
        a = a_ref[pl.ds(off, SUBLANES), :]
        b = b_ref[pl.ds(off, SUBLANES), :]
        for sh in (1, 2, 4):
            if reverse:
                a_s = pltpu.roll(a, SUBLANES - sh, 0)
                b_s = pltpu.roll(b, SUBLANES - sh, 0)
                m = row < SUBLANES - sh
            else:
                a_s = pltpu.roll(a, sh, 0)
                b_s = pltpu.roll(b, sh, 0)
                m = row >= sh
            b = jnp.where(m, a * b_s + b, b)
            a = jnp.where(m, a * a_s, a)
        o = b + a * cr
        out_ref[pl.ds(off, SUBLANES), :] = o
        return o[0:1, :] if reverse else o[SUBLANES - 1:SUBLANES, :]

    return lax.fori_loop(0, ngrp, step, carry)


def _fwd_b(x, ya, wout_a, nw, win8, p8, gcat, jobs, *, tm, relay_step):
    s_len = x.shape[0]
    nt = s_len // tm

    def main(i, ins, outs, scr):
        x_ref, ya_ref, wouta_ref, nw_ref, win_ref, p8_ref, gcat_ref = ins
        x1_ref, zb_ref, hs_ref, h1_ref, yb_ref, xc_ref, a_ref, cc_ref, r_ref, ig_ref, m_ref = outs
        xbe_scr, b_scr, k_scr, carry_scr = scr

        @pl.when(i == 0)
        def _():
            xbe_scr[0:SUBLANES, :] = jnp.zeros((SUBLANES, BW), F32)
            carry_scr[...] = jnp.zeros_like(carry_scr)

        x1 = x_ref[...] + _dot(ya_ref[...], wouta_ref[...])
        x1_ref[...] = x1
        h = (x1 * _rms(x1) * nw_ref[...]).astype(BF16)
        h1_ref[...] = h
        for k in range(NDEV):
            zb_ref[:, k * CB:(k + 1) * CB] = _dot(h, win_ref[k])
        xbe_scr[SUBLANES:SUBLANES + tm, :] = zb_ref[:, :BW]
        for hh in range(BH):
            cs = slice(hh * HD, (hh + 1) * HD)
            xc = _conv(p8_ref, cs, xbe_scr[SUBLANES:SUBLANES + tm, cs], xbe_scr[7:7 + tm, cs],
                       xbe_scr[6:6 + tm, cs], xbe_scr[5:5 + tm, cs])
            r, ig, _, a, mult, rm = _gates(p8_ref, gcat_ref, hh, xc)
            ixc = ig * xc
            xc_ref[:, cs] = xc
            a_ref[:, cs] = a
            r_ref[:, cs] = r.astype(BF16)
            ig_ref[:, cs] = ig.astype(BF16)
            m_ref[:, cs] = mult.astype(BF16)
            b_scr[:, cs] = mult * ixc
            k_scr[:, cs] = ixc * (a * a * rm)
        xbe_scr[0:SUBLANES, :] = xbe_scr[tm:tm + SUBLANES, :]
        carry_scr[...] = _scan_rows(a_ref, b_scr, hs_ref, carry_scr[...], tm, False)
        for hh in range(BH):
            cs = slice(hh * HD, (hh + 1) * HD)
            gt = zb_ref[:, BW + hh * HD:BW + (hh + 1) * HD]
            hsv = hs_ref[:, cs]
            yb_ref[:, cs] = (hsv * (gt * _sigmoid(gt))).astype(BF16)
            cc_ref[:, cs] = (hsv - b_scr[:, cs]) - k_scr[:, cs]

    tile = lambda w: pl.BlockSpec((tm, w), lambda i: (i, 0))
    wide = lambda dt: _sds((s_len, BW), dt)
    return _call(
        main, jobs, name="fwd_b", grid=(nt,), relay_step=relay_step,
        ins=[x, ya, wout_a, nw, win8, p8, gcat], in_specs=[tile(D), tile(AW), _VMEM, _VMEM, _VMEM, _VMEM, _VMEM],
        out_shape=[_sds((s_len, D), F32), _sds((s_len, 2 * BW), F32), wide(F32), _sds((s_len, D), BF16), wide(BF16),
                   wide(F32), wide(F32), wide(F32), wide(BF16), wide(BF16), wide(BF16)],
        out_specs=[tile(D), tile(2 * BW), tile(BW), tile(D)] + [tile(BW)] * 7,
        scratch=[pltpu.VMEM((tm + SUBLANES, BW), F32), pltpu.VMEM((tm, BW), F32), pltpu.VMEM((tm, BW), F32),
                 pltpu.VMEM((1, BW), F32)])


def _head(x1, yb, wout, nfw, tgt, *, tm):
    s_len = x1.shape[0]

    def main(i, ins, outs, scr):
        x1_ref, yb_ref, wout_ref, nfw_ref, t_ref = ins
        dx2_ref, dx2b_ref, loss_ref, gnfw_ref = outs

        @pl.when(i == 0)
        def _():
            loss_ref[...] = jnp.zeros_like(loss_ref)
            gnfw_ref[...] = jnp.zeros_like(gnfw_ref)

        x2 = x1_ref[...] + _dot(yb_ref[...], wout_ref[...])
        rf = _rms(x2)
        xn = x2 * rf
        e = xn * nfw_ref[...] - t_ref[...]
        loss_ref[...] += (0.5 / D) * jnp.sum(jnp.sum(e * e, axis=-1, keepdims=True), axis=0, keepdims=True)
        dyf = e * (1.0 / D)
        gnfw_ref[...] += _rowsum(dyf * xn)
        dx2 = _rms_bwd(dyf, x2, rf, nfw_ref[...])
        dx2_ref[...] = dx2
        dx2b_ref[...] = dx2.astype(BF16)

    tile = lambda w: pl.BlockSpec((tm, w), lambda i: (i, 0))
    whole = lambda *s: pl.BlockSpec(s, lambda i: (0,) * len(s))
    (dx2, dx2b, loss, gnfw), _ = _call(
        main, [], name="head", grid=(s_len // tm,),
        ins=[x1, yb, wout, nfw, tgt], in_specs=[tile(D), tile(BW), _VMEM, _VMEM, tile(D)],
        out_shape=[_sds((s_len, D), F32), _sds((s_len, D), BF16), _sds((1, 1), F32), _sds((1, D), F32)],
        out_specs=[tile(D), tile(D), whole(1, 1), whole(1, D)], scratch=[])
    return dx2, dx2b, loss, gnfw


def _bwd_b(dx2, zb, hs, x1, saved, nw, win8, p8, gcat, wout, *, tm):
    s_len = x1.shape[0]
    nt = s_len // tm

    def main(i, ins, outs, scr):
        (dx2_ref, zb_ref, hs_ref, x1_ref, xc_ref, a_ref, cc_ref, r_ref, ig_ref, m_ref,
         nw_ref, win_ref, p8_ref, gcat_ref, wout_ref) = ins
        dx1_ref, dx1b_ref, dzb_ref, gp8_ref, gga_ref, ggx_ref, gnw_ref = outs
        ae_scr, an_scr, dhd_scr, dh_scr, dy_scr, dxce_scr, carry_scr, afirst_scr = scr

        @pl.when(i == 0)
        def _():
            gp8_ref[...] = jnp.zeros_like(gp8_ref)
            gga_ref[...] = jnp.zeros_like(gga_ref)
            ggx_ref[...] = jnp.zeros_like(ggx_ref)
            gnw_ref[...] = jnp.zeros_like(gnw_ref)
            dxce_scr[tm:tm + SUBLANES, :] = jnp.zeros((SUBLANES, BW), F32)
            carry_scr[...] = jnp.zeros_like(carry_scr)
            afirst_scr[...] = jnp.zeros_like(afirst_scr)

        dx2 = dx2_ref[...]
        dy_scr[...] = _dot_nt(dx2.astype(BF16), wout_ref[...])
        for hh in range(BH):
            cs = slice(hh * HD, (hh + 1) * HD)
            gs = slice(BW + hh * HD, BW + (hh + 1) * HD)
            gt = zb_ref[:, gs]
            sig = _sigmoid(gt)
            dy = dy_scr[:, cs]
            dhd_scr[:, cs] = dy * (gt * sig)
            dzb_ref[:, gs] = (dy * hs_ref[:, cs] * (sig * (1.0 + gt * (1.0 - sig)))).astype(BF16)

        ae_scr[0:tm, :] = a_ref[...]
        ae_scr[tm:tm + SUBLANES, :] = jnp.broadcast_to(afirst_scr[...], (SUBLANES, BW))
        an_scr[...] = ae_scr[1:1 + tm, :]
        afirst_scr[...] = ae_scr[0:1, :]
        carry_scr[...] = _scan_rows(an_scr, dhd_scr, dh_scr, carry_scr[...], tm, True)

        for hh in range(BH):
            cs = slice(hh * HD, (hh + 1) * HD)
            dh = dh_scr[:, cs]
            mult = m_ref[:, cs].astype(F32)
            ig = ig_ref[:, cs].astype(F32)
            r = r_ref[:, cs].astype(F32)
            xc = xc_ref[:, cs]
            lam = p8_ref[7:8, cs]
            sp = _softplus_neg(lam)
            dla = dh * cc_ref[:, cs]
            gp8_ref[7:8, cs] += _rowsum(dla * ((-RG_C) * r)) * (-_sigmoid(-lam))
            dpr = dla * ((-RG_C) * sp) * (r * (1.0 - r))
            dpi = dh * mult * xc * (ig * (1.0 - ig))
            gp8_ref[5:6, cs] += _rowsum(dpr)
            gp8_ref[6:7, cs] += _rowsum(dpi)
            dcat = jnp.concatenate([dpr, dpi], axis=1).astype(BF16)
            dxc = dh * mult * ig + _dot_nt(dcat, gcat_ref[hh])
            gg = _dot(xc.T.astype(BF16), dcat)
            gga_ref[hh] += gg[:, :HD]
            ggx_ref[hh] += gg[:, HD:]
            dxce_scr[0:tm, cs] = dxc
            gp8_ref[4:5, cs] += _rowsum(dxc)
        for hh in range(BH):
            cs = slice(hh * HD, (hh + 1) * HD)
            xb = zb_ref[:, cs]
            d0, d1 = dxce_scr[0:tm, cs], dxce_scr[1:1 + tm, cs]
            d2, d3 = dxce_scr[2:2 + tm, cs], dxce_scr[3:3 + tm, cs]
            dzb_ref[:, cs] = (p8_ref[3:4, cs] * d0 + p8_ref[2:3, cs] * d1 + p8_ref[1:2, cs] * d2
                              + p8_ref[0:1, cs] * d3).astype(BF16)
            gp8_ref[3:4, cs] += _rowsum(d0 * xb)
            gp8_ref[2:3, cs] += _rowsum(d1 * xb)
            gp8_ref[1:2, cs] += _rowsum(d2 * xb)
            gp8_ref[0:1, cs] += _rowsum(d3 * xb)
        dxce_scr[tm:tm + SUBLANES, :] = dxce_scr[0:SUBLANES, :]

        dh1 = jnp.zeros((tm, D), F32)
        for k in range(NDEV):
            dh1 = dh1 + _dot_nt(dzb_ref[:, k * CB:(k + 1) * CB], win_ref[k])
        x1 = x1_ref[...]
        r1 = _rms(x1)
        dx1 = dx2 + _rms_bwd(dh1, x1, r1, nw_ref[...])
        dx1_ref[...] = dx1
        dx1b_ref[...] = dx1.astype(BF16)
        gnw_ref[...] += _rowsum(dh1 * x1 * r1)

    tile = lambda w: pl.BlockSpec((tm, w), lambda i: (nt - 1 - i, 0))
    whole = lambda *s: pl.BlockSpec(s, lambda i: (0,) * len(s))
    full = lambda: pltpu.VMEM((tm, BW), F32)
    ext = lambda: pltpu.VMEM((tm + SUBLANES, BW), F32)
    out, _ = _call(
        main, [], name="bwd_b", grid=(nt,),
        ins=[dx2, zb, hs, x1, *saved, nw, win8, p8, gcat, wout],
        in_specs=[tile(D), tile(2 * BW), tile(BW), tile(D)] + [tile(BW)] * 6 + [_VMEM] * 5,
        out_shape=[_sds((s_len, D), F32), _sds((s_len, D), BF16), _sds((s_len, 2 * BW), BF16), _sds((SUBLANES, BW), F32),
                   _sds((BH, HD, HD), F32), _sds((BH, HD, HD), F32), _sds((1, D), F32)],
        out_specs=[tile(D), tile(D), tile(2 * BW), whole(SUBLANES, BW), whole(BH, HD, HD), whole(BH, HD, HD),
                   whole(1, D)],
        scratch=[ext(), full(), full(), full(), full(), ext(), pltpu.VMEM((1, BW), F32), pltpu.VMEM((1, BW), F32)])
    return out


def _transpose_into(dst_ref, src_ref, rows):
    s_len = src_ref.shape[0]
    for r0 in range(0, s_len, rows):
        dst_ref[:, r0:r0 + rows] = src_ref[r0:r0 + rows, :].astype(F32).T.astype(BF16)


def _wgrad(a, b, jobs, *, by_rows, per, name, relay_step=0):
    s_len, m = a.shape
    n = b.shape[1]
    r, cd = (m // NDEV, n) if by_rows else (m, n // NDEV)
    nsteps = NDEV // per
    at_rows = per * r if by_rows else m

    def main(i, ins, outs, scr):
        a_ref, b_ref = ins
        q_ref, acc_ref = outs
        at_scr, stage, mine, land, send_sems, recv_sems = scr
        x, y, c = _place()

        def to_sibling(pi):
            return pltpu.make_async_remote_copy(
                src_ref=stage.at[pi & 1], dst_ref=land.at[pi], send_sem=send_sems.at[pi], recv_sem=recv_sems.at[pi],
                device_id=(x, y, 1 - c), device_id_type=MESH)

        if by_rows:
            _transpose_into(at_scr, a_ref, 256)
        else:
            @pl.when(i == 0)
            def _():
                _transpose_into(at_scr, a_ref, 256)

        res = _dot(at_scr[...], b_ref[...]).astype(BF16)
        for k in range(per):
            blk = per * i + k
            pi, pc = blk >> 1, blk & 1
            val = res[k * r:(k + 1) * r, :] if by_rows else res

            @pl.when(pc != c)
            def _():
                @pl.when(pi >= 2)
                def _():
                    to_sibling(pi - 2).wait_send()

                stage[pi & 1] = val
                to_sibling(pi).start()

            @pl.when(pc == c)
            def _():
                mine[pi] = val

        @pl.when(i == nsteps - 1)
        def _():
            for p in range(4):
                to_sibling(p).wait_recv()
            to_sibling(2).wait_send()
            to_sibling(3).wait_send()
            _chip_sums(mine, land, q_ref, acc_ref, x, y)

    if by_rows:
        in_specs = [pl.BlockSpec((s_len, at_rows), lambda j: (0, j)), _VMEM]
    else:
        in_specs = [_VMEM, pl.BlockSpec((s_len, cd), lambda j: (0, j))]
    blk_vmem = lambda k: pltpu.VMEM((k, r, cd), BF16)
    (q, acc), job_out = _call(
        main, jobs, name=name, grid=(nsteps,), relay_step=relay_step, ins=[a, b], in_specs=in_specs,
        out_shape=[_sds((NCHIP_OTHER, r, cd), BF16), _sds((r, cd), F32)],
        out_specs=[pl.BlockSpec((NCHIP_OTHER, r, cd), lambda j: (0, 0, 0)), pl.BlockSpec((r, cd), lambda j: (0, 0))],
        scratch=[pltpu.VMEM((at_rows, s_len), BF16), blk_vmem(2), blk_vmem(4), blk_vmem(4),
                 pltpu.SemaphoreType.DMA((4,)), pltpu.SemaphoreType.DMA((4,))])
    return q, acc, job_out


def _adam_math(w, g, m, v):
    m = B1 * m + (1.0 - B1) * g
    v = B2 * v + (1.0 - B2) * (g * g)
    m_hat = m / (1.0 - B1 ** STEP)
    v_hat = v / (1.0 - B2 ** STEP)
    delta = (-LR) * (m_hat / (jnp.sqrt(v_hat) + ADAM_EPS) + WD * w)
    return delta, m, v


def _adam_big(w, acc, land, m, v, name):
    r, cd = w.shape
    rb = 256 if r % 256 == 0 else r
    nland = land.shape[0]

    def body(w_ref, acc_ref, land_ref, m_ref, v_ref, g_ref, d_ref, mo_ref, vo_ref):
        g = acc_ref[...]
        for j in range(nland):
            g = g + land_ref[j].astype(F32)
        g_ref[...] = g
        d_ref[...], mo_ref[...], vo_ref[...] = _adam_math(w_ref[...], g, m_ref[...], v_ref[...])

    blk = pl.BlockSpec((rb, cd), lambda i: (i, 0))
    blk3 = pl.BlockSpec((nland, rb, cd), lambda i: (0, i, 0))
    return pl.pallas_call(
        body, name=name, grid=(r // rb,), in_specs=[blk, blk, blk3, blk, blk], out_specs=[blk] * 4,
        out_shape=[_sds((r, cd), F32)] * 4,
        compiler_params=_params(dimension_semantics=("arbitrary",)),
    )(w, acc, land, m, v)


def _adam_small(groups):
    n = len(groups)

    def body(*refs):
        ins, outs = refs[:4 * n], refs[4 * n:]
        for k in range(n):
            w_ref, g_ref, m_ref, v_ref = ins[4 * k:4 * k + 4]
            d, mo, vo = _adam_math(w_ref[...], g_ref[...], m_ref[...], v_ref[...])
            outs[3 * k][...] = d
            outs[3 * k + 1][...] = mo
            outs[3 * k + 2][...] = vo

    flat = [a for grp in groups for a in grp]
    shapes = [_sds(grp[0].shape, F32) for grp in groups for _ in range(3)]
    res = pl.pallas_call(
        body, name="adam_small", in_specs=[_VMEM] * (4 * n), out_specs=[_VMEM] * (3 * n), out_shape=shapes,
        compiler_params=_params(),
    )(*flat)
    return [tuple(res[3 * k:3 * k + 3]) for k in range(n)]


TM_FWD_A = 256
RELAY_STEP_FWD_A = 4
RELAY_STEP_FWD_B = 2
TM_BWD_A = 256
RELAY_STEP_BWD_A = 3
TM_BWD_A_IN = 256
RELAY_STEP_BWD_A_IN = 4
TM_FWD_B = 256
TM_HEAD = 512
TM_BWD_B = 256


def _pack(parts, rows):
    flat = jnp.concatenate([p.reshape(-1) for p in parts])
    return jnp.pad(flat, (0, NDEV * rows * LANES - flat.shape[0])).reshape(NDEV, rows, LANES)


def _unpack(packed, shapes):
    flat, out, off = packed.reshape(-1), [], 0
    for s in shapes:
        size = 1
        for d in s:
            size *= d
        out.append(flat[off:off + size].reshape(s))
        off += size
    return out


def kernel(x, norm_w, a_w_in, a_ln_w, a_ln_b, a_w_s, a_b_s, a_w_out, b_w_in, b_conv_w, b_conv_b, b_gate_a_w, b_gate_a_b, b_gate_x_w, b_gate_x_b, b_lambda, b_w_out, norm_f_w, loss_target, m_norm_w, m_a_w_in, m_a_ln_w, m_a_ln_b, m_a_w_s, m_a_b_s, m_a_w_out, m_b_w_in, m_b_conv_w, m_b_conv_b, m_b_gate_a_w, m_b_gate_a_b, m_b_gate_x_w, m_b_gate_x_b, m_b_lambda, m_b_w_out, m_norm_f_w, v_norm_w, v_a_w_in, v_a_ln_w, v_a_ln_b, v_a_w_s, v_a_b_s, v_a_w_out, v_b_w_in, v_b_conv_w, v_b_conv_b, v_b_gate_a_w, v_b_gate_a_b, v_b_gate_x_w, v_b_gate_x_b, v_b_lambda, v_b_w_out, v_norm_f_w):
    me = 4 * lax.axis_index("x") + 2 * lax.axis_index("y") + lax.axis_index("c")
    xs, tgt = x[0], loss_target[0]
    nw0, nw1, nfw = norm_w[0:1], norm_w[1:2], norm_f_w.reshape(1, D)
    w_s, bst = a_w_s[0], a_b_s[0].T
    gcat = jnp.concatenate([b_gate_a_w[0], b_gate_x_w[0]], axis=-1).astype(BF16)

    p8_shard = jnp.concatenate([b_conv_w[0], b_conv_b, b_gate_a_b, b_gate_x_b, b_lambda], axis=0)
    ((win_a8, p8_all),) = _comm_only([_Gather([a_w_in[0].astype(BF16), p8_shard])], "gather_first")
    p8 = jnp.transpose(p8_all, (1, 0, 2)).reshape(SUBLANES, BW)

    (z, h0, ya), ((wout_a8, win_b8),) = _fwd_a(
        xs, nw0, win_a8, a_ln_w, a_ln_b, w_s, bst, [_Gather([a_w_out[0].astype(BF16), b_w_in[0].astype(BF16)])],
        tm=TM_FWD_A, relay_step=RELAY_STEP_FWD_A)
    wout_a = wout_a8.reshape(AW, D)
    (x1, zb, hs, h1, yb, *saved_b), ((wout_b8,),) = _fwd_b(
        xs, ya, wout_a, nw1, win_b8, p8, gcat, [_Gather([b_w_out[0].astype(BF16)])],
        tm=TM_FWD_B, relay_step=RELAY_STEP_FWD_B)
    wout_b = wout_b8.reshape(BW, D)
    dx2, dx2b, loss, g_nfw = _head(x1, yb, wout_b, nfw, tgt, tm=TM_HEAD)

    dx1, dx1b, dzb, g_p8, g_ga, g_gx, g_nw1 = _bwd_b(dx2, zb, hs, x1, saved_b, nw1, win_b8, p8, gcat, wout_b,
                                                     tm=TM_BWD_B)
    q_wout_b, acc_wout_b, _ = _wgrad(yb, dx2b, [], by_rows=True, per=2, name="wgrad_b_out")
    shapes_b = [(1, D), (1, D), (SUBLANES, BW), (1, 1)]
    pack_b = _pack([g_nfw, g_nw1, g_p8, loss], 16)
    small_b = _InChip([g_ga.reshape(NDEV, -1, HD), g_gx.reshape(NDEV, -1, HD), pack_b])
    q_win_b, acc_win_b, (sm_b, (l_wout_b,)) = _wgrad(h1, dzb, [small_b, _Exchange([q_wout_b])], by_rows=False, per=1,
                                                      name="wgrad_b_in")
    qs_b, accs_b = sm_b[:3], sm_b[3:]

    (dz, g_lnw, g_lnb, g_ws, g_bst), (lands_b, (l_win_b,)) = _bwd_a(
        dx1b, z, a_ln_w, a_ln_b, w_s, bst, wout_a, [_Exchange(qs_b), _ExchangeVia(q_win_b)],
        tm=TM_BWD_A, relay_step=RELAY_STEP_BWD_A)
    shapes_a = [(1, AW), (1, AW), (CH, G)]
    pack_a = _pack([g_lnw, g_lnb, g_bst], 8)
    q_wout_a, acc_wout_a, (red_b, sm_a) = _wgrad(
        ya, dx1b, [_SumGather(accs_b, lands_b), _InChip([g_ws, pack_a])], by_rows=True, per=1,
        name="wgrad_a_out", relay_step=2)
    qs_a, accs_a = [q_wout_a, *sm_a[:2]], [acc_wout_a, *sm_a[2:]]
    q_win_a, acc_win_a, (lands_a,) = _wgrad(h0, dz, [_Exchange(qs_a)], by_rows=False, per=1, name="wgrad_a_in")
    (gx, g_nw0), (red_a, (l_win_a,)) = _bwd_a_in(
        dz, dx1, xs, nw0, win_a8, [_SumGather(accs_a[1:], lands_a[1:]), _ExchangeVia(q_win_a)],
        tm=TM_BWD_A_IN, relay_step=RELAY_STEP_BWD_A_IN)
    g_nw0 = _allreduce_direct(g_nw0, "allreduce_norm_w0")

    r_ga, r_gx, r_pack_b = red_b
    r_nfw, r_nw1, r_p8, loss = _unpack(r_pack_b, shapes_b)
    r_ws, r_pack_a = red_a
    r_lnw, r_lnb, r_bst = _unpack(r_pack_a, shapes_a)
    g_p8 = lax.dynamic_slice_in_dim(r_p8, me * (BW // NDEV), BW // NDEV, axis=1)
    loss = loss[0, 0]

    weights = dict(norm_w=norm_w, a_w_in=a_w_in, a_ln_w=a_ln_w, a_ln_b=a_ln_b, a_w_s=a_w_s, a_b_s=a_b_s, a_w_out=a_w_out,
                   b_w_in=b_w_in, b_conv_w=b_conv_w, b_conv_b=b_conv_b, b_gate_a_w=b_gate_a_w, b_gate_a_b=b_gate_a_b,
                   b_gate_x_w=b_gate_x_w, b_gate_x_b=b_gate_x_b, b_lambda=b_lambda, b_w_out=b_w_out, norm_f_w=norm_f_w)
    mom1 = dict(norm_w=m_norm_w, a_w_in=m_a_w_in, a_ln_w=m_a_ln_w, a_ln_b=m_a_ln_b, a_w_s=m_a_w_s, a_b_s=m_a_b_s,
                a_w_out=m_a_w_out, b_w_in=m_b_w_in, b_conv_w=m_b_conv_w, b_conv_b=m_b_conv_b, b_gate_a_w=m_b_gate_a_w,
                b_gate_a_b=m_b_gate_a_b, b_gate_x_w=m_b_gate_x_w, b_gate_x_b=m_b_gate_x_b, b_lambda=m_b_lambda,
                b_w_out=m_b_w_out, norm_f_w=m_norm_f_w)
    mom2 = dict(norm_w=v_norm_w, a_w_in=v_a_w_in, a_ln_w=v_a_ln_w, a_ln_b=v_a_ln_b, a_w_s=v_a_w_s, a_b_s=v_a_b_s,
                a_w_out=v_a_w_out, b_w_in=v_b_w_in, b_conv_w=v_b_conv_w, b_conv_b=v_b_conv_b, b_gate_a_w=v_b_gate_a_w,
                b_gate_a_b=v_b_gate_a_b, b_gate_x_w=v_b_gate_x_w, b_gate_x_b=v_b_gate_x_b, b_lambda=v_b_lambda,
                b_w_out=v_b_w_out, norm_f_w=v_norm_f_w)
    names = list(weights)

    def as2d(a):
        return a.reshape(-1, a.shape[-1])

    upd, grads = {}, {}
    for k, acc, land in (("a_w_in", acc_win_a, l_win_a), ("a_w_out", accs_a[0], lands_a[0]),
                         ("b_w_in", acc_win_b, l_win_b), ("b_w_out", acc_wout_b, l_wout_b)):
        g, d, mo, vo = _adam_big(as2d(weights[k]), acc, land, as2d(mom1[k]), as2d(mom2[k]), "adam_" + k)
        grads[k] = g[None]
        upd[k] = (d, mo, vo)
    grads.update(
        norm_w=jnp.concatenate([g_nw0, r_nw1], axis=0), a_ln_w=r_lnw, a_ln_b=r_lnb,
        a_w_s=r_ws.reshape(1, G, CH, CH), a_b_s=r_bst.T[None],
        b_conv_w=g_p8[None, 0:4], b_conv_b=g_p8[4:5], b_gate_a_w=r_ga.reshape(1, BH, HD, HD), b_gate_a_b=g_p8[5:6],
        b_gate_x_w=r_gx.reshape(1, BH, HD, HD), b_gate_x_b=g_p8[6:7], b_lambda=g_p8[7:8], norm_f_w=r_nfw.reshape(D))
    small_names = [k for k in names if k not in upd]
    res = _adam_small([(as2d(weights[k]), as2d(grads[k]), as2d(mom1[k]), as2d(mom2[k])) for k in small_names])
    for k, r3 in zip(small_names, res):
        upd[k] = r3
    deltas = [upd[k][0].reshape(weights[k].shape) for k in names]
    new_m = [upd[k][1].reshape(weights[k].shape) for k in names]
    new_v = [upd[k][2].reshape(weights[k].shape) for k in names]
    return (loss, gx[None], *[grads[k] for k in names], *deltas, *new_m, *new_v)
```

```python
import jax
import jax.numpy as jnp
from jax import lax
from jax.experimental import pallas as pl
from jax.experimental.pallas import tpu as pltpu

F32 = jnp.float32
BF16 = jnp.bfloat16
MESH = pl.DeviceIdType.MESH

NDEV = 8
NCHIP_OTHER = 3
D = 1024
AW = 2048
G = 8
GD = AW // G
CH = 128
BW = 1536
BH = 12
HD = BW // BH
CA = 3 * AW // NDEV
CB = 2 * BW // NDEV
RMS_EPS = 1e-6
LN_EPS = 1e-5
RG_C = 8.0
LR, B1, B2, ADAM_EPS, WD, STEP = 0.001, 0.9, 0.999, 1e-08, 0.01, 10
V7X_VMEM_BYTES = 64 * 1024 * 1024
VMEM_LIMIT = V7X_VMEM_BYTES - 8 * 1024 * 1024
SUBLANES = 8
LANES = 128
BF16_ROWS = 16
GELU_C = 0.7978845608028654
GELU_K = 0.044715

_VMEM = pl.BlockSpec(memory_space=pltpu.VMEM)
_HBM = pl.BlockSpec(memory_space=pltpu.HBM)


def _sds(shape, dtype):
    return jax.ShapeDtypeStruct(tuple(shape), dtype)


def _params(**kw):
    return pltpu.CompilerParams(vmem_limit_bytes=VMEM_LIMIT, **kw)


def _gelu_t(z):
    t = jnp.tanh(GELU_C * (z + GELU_K * (z * z * z)))
    return 0.5 * z * (1.0 + t), t


def _dgelu(z, t):
    return 0.5 * (1.0 + t) + 0.5 * z * (1.0 - t * t) * (GELU_C * (1.0 + 3.0 * GELU_K * z * z))


def _sigmoid(v):
    return 0.5 * jnp.tanh(0.5 * v) + 0.5


def _softplus_neg(lam):
    return jnp.maximum(-lam, 0.0) + jnp.log1p(jnp.exp(-jnp.abs(lam)))


def _dot(a, b):
    return jnp.dot(a, b, preferred_element_type=F32)


def _dot_nt(a, b):
    return lax.dot_general(a, b, (((1,), (1,)), ((), ())), preferred_element_type=F32)


def _rowsum(v):
    return jnp.sum(v, axis=0, keepdims=True)


def _causal_mask():
    r = lax.broadcasted_iota(jnp.int32, (CH, CH), 0)
    c = lax.broadcasted_iota(jnp.int32, (CH, CH), 1)
    return r >= c


def _rms(x):
    return lax.rsqrt(jnp.mean(x * x, axis=-1, keepdims=True) + RMS_EPS)


def _rms_bwd(dh, x, r, nw):
    gy = dh * nw
    return r * gy - x * (r * r * r) * jnp.mean(gy * x, axis=-1, keepdims=True)


def _place():
    return lax.axis_index("x"), lax.axis_index("y"), lax.axis_index("c")


def _other_chips(x, y):
    return [(1 - x, y), (x, 1 - y), (1 - x, 1 - y)]


GATHER_SLOTS = 10


def _gather_ops(ins, outs, send_sems, recv_sems, local_sems):
    n = len(ins)
    x, y, c = _place()
    sibling = (x, y, 1 - c)
    xn, yn, dg = _other_chips(x, y)
    split = [ins[i].shape[0] % (2 * BF16_ROWS) == 0 for i in range(n)]

    def blk(chip, core):
        return 4 * chip[0] + 2 * chip[1] + core

    me = blk((x, y), c)

    def part(ref, i, half):
        if half is None:
            return ref
        h = ins[i].shape[0] // 2
        return ref.at[pl.ds(half * h, h)]

    def copy(i, k, block, to, half=None, src=None):
        dst = part(outs[i].at[block], i, half)
        return pltpu.make_async_remote_copy(
            src_ref=dst if src is None else part(src, i, half), dst_ref=dst,
            send_sem=send_sems.at[k, i], recv_sem=recv_sems.at[k, i], device_id=to, device_id_type=MESH)

    def first_copies():
        mine = [pltpu.make_async_copy(ins[i], outs[i].at[me], local_sems.at[i]) for i in range(n)]
        first = []
        for i in range(n):
            first.append(copy(i, 0, me, sibling, src=ins[i]))
            if split[i]:
                first.append(copy(i, 1, me, (*xn, c), 0, ins[i]))
                first.append(copy(i, 3, me, (*yn, c), 1, ins[i]))
                first.append(copy(i, 2, me, (*xn, c), 1, ins[i]))
                first.append(copy(i, 4, me, (*yn, c), 0, ins[i]))
            else:
                first.append(copy(i, 1, me, (*xn, c), None, ins[i]))
                first.append(copy(i, 3, me, (*yn, c), None, ins[i]))
                first.append(copy(i, 5, me, (*dg, c), None, ins[i]))
        return mine, first

    def onward():
        out = []
        for i in range(n):
            if split[i]:
                out.append(copy(i, 5, blk(xn, c), (*yn, c), 0))
                out.append(copy(i, 6, blk(yn, c), (*xn, c), 1))
        return out

    def start():
        mine, first = first_copies()
        for cp in mine + first:
            cp.start()

    def relay():
        sends = onward()
        for i in range(n):
            if split[i]:
                copy(i, 1, blk(xn, c), sibling, 0).wait_recv()
                sends.pop(0).start()
                copy(i, 3, blk(yn, c), sibling, 1).wait_recv()
                sends.pop(0).start()

    def finish():
        mine, first = first_copies()
        passed = []

        def pass_on(i, j, chip):
            fwd = copy(i, 7 + j, blk(chip, c), sibling)
            fwd.start()
            passed.append(fwd)

        for i in range(n):
            if split[i]:
                copy(i, 2, blk(xn, c), sibling, 1).wait_recv()
                pass_on(i, 0, xn)
                copy(i, 4, blk(yn, c), sibling, 0).wait_recv()
                pass_on(i, 1, yn)
                copy(i, 5, blk(dg, c), sibling, 0).wait_recv()
                copy(i, 6, blk(dg, c), sibling, 1).wait_recv()
                pass_on(i, 2, dg)
            else:
                copy(i, 1, blk(xn, c), sibling).wait_recv()
                pass_on(i, 0, xn)
                copy(i, 3, blk(yn, c), sibling).wait_recv()
                pass_on(i, 1, yn)
                copy(i, 5, blk(dg, c), sibling).wait_recv()
                pass_on(i, 2, dg)
        for i in range(n):
            copy(i, 0, blk((x, y), 1 - c), sibling).wait_recv()
            for j, chip in enumerate((xn, yn, dg)):
                copy(i, 7 + j, blk(chip, 1 - c), sibling).wait_recv()
        for cp in first + passed + onward():
            cp.wait_send()
        for cp in mine:
            cp.wait()

    return start, relay, finish


def _gather_sems(n):
    return [pltpu.SemaphoreType.DMA((GATHER_SLOTS, n)), pltpu.SemaphoreType.DMA((GATHER_SLOTS, n)),
            pltpu.SemaphoreType.DMA((n,))]


class _Gather:
    def __init__(self, shards):
        n = len(shards)
        self.ins, self.in_specs = list(shards), [_HBM] * n
        self.out_shape = [_sds((NDEV,) + s.shape, s.dtype) for s in shards]
        self.out_specs = [_HBM] * n
        self.scratch = _gather_sems(n)

    def ops(self, ins, outs, scr):
        return _gather_ops(ins, outs, *scr)


class _Exchange:
    def __init__(self, qs):
        n = len(qs)
        self.ins, self.in_specs = list(qs), [_HBM] * n
        self.out_shape = [_sds(q.shape, q.dtype) for q in qs]
        self.out_specs = [_HBM] * n
        self.scratch = [pltpu.SemaphoreType.DMA((NCHIP_OTHER, n)), pltpu.SemaphoreType.DMA((NCHIP_OTHER, n))]

    def ops(self, ins, outs, scr):
        send_sems, recv_sems = scr
        n = len(ins)
        x, y, c = _place()
        chips = _other_chips(x, y)

        def copies():
            return [pltpu.make_async_remote_copy(
                src_ref=ins[i].at[j], dst_ref=outs[i].at[j], send_sem=send_sems.at[j, i],
                recv_sem=recv_sems.at[j, i], device_id=(*chips[j], c), device_id_type=MESH)
                for i in range(n) for j in range(NCHIP_OTHER)]

        def start():
            for cp in copies():
                cp.start()

        def finish():
            cps = copies()
            for cp in cps:
                cp.wait_recv()
            for cp in cps:
                cp.wait_send()

        return start, lambda: None, finish


class _ExchangeVia:
    def __init__(self, q):
        _, r, cd = q.shape
        half = (2, r // 2, cd)
        self.ins, self.in_specs = [q], [_HBM]
        self.out_shape, self.out_specs = [_sds((2, r, cd), q.dtype)], [_HBM]
        self.scratch = [pltpu.VMEM(half, q.dtype), pltpu.VMEM(half, q.dtype), pltpu.VMEM(half, q.dtype),
                        pltpu.SemaphoreType.DMA((6,)), pltpu.SemaphoreType.DMA((6,)), pltpu.SemaphoreType.DMA((2,))]

    def ops(self, ins, outs, scr):
        (q,), (land,) = ins, outs
        relayed, own, comb, send_sems, recv_sems, local_sems = scr
        h = q.shape[1] // 2
        x, y, c = _place()
        xn, yn, _ = _other_chips(x, y)
        h0, h1 = pl.ds(0, h), pl.ds(h, h)

        def remote(k, src, dst, chip):
            return pltpu.make_async_remote_copy(src_ref=src, dst_ref=dst, send_sem=send_sems.at[k],
                                                recv_sem=recv_sems.at[k], device_id=(*chip, c), device_id_type=MESH)

        def via():
            return [remote(2, q.at[2, h0], relayed.at[0], xn), remote(3, q.at[2, h1], relayed.at[1], yn)]

        def direct():
            return [remote(0, q.at[0, h0], land.at[0, h0], xn), remote(1, q.at[1, h1], land.at[1, h1], yn)]

        def second():
            return [remote(4, comb.at[0], land.at[1, h0], yn), remote(5, comb.at[1], land.at[0, h1], xn)]

        def mine():
            return [pltpu.make_async_copy(q.at[1, h0], own.at[0], local_sems.at[0]),
                    pltpu.make_async_copy(q.at[0, h1], own.at[1], local_sems.at[1])]

        def start():
            for cp in via() + direct() + mine():
                cp.start()

        def relay():
            arrived, loaded, onward = via(), mine(), second()
            for k in range(2):
                arrived[k].wait_recv()
                loaded[k].wait()
                comb[k] = (own[k].astype(F32) + relayed[k].astype(F32)).astype(comb.dtype)
                onward[k].start()

        def finish():
            landing = direct() + second()
            for cp in landing:
                cp.wait_recv()
            for cp in via() + landing:
                cp.wait_send()

        return start, relay, finish


class _SumGather:
    def __init__(self, accs, lands):
        n = len(accs)
        self.n = n
        self.ins, self.in_specs = list(accs) + list(lands), [_VMEM] * (2 * n)
        self.out_shape = [_sds((NDEV,) + a.shape, a.dtype) for a in accs]
        self.out_specs = [_HBM] * n
        self.scratch = [pltpu.VMEM(a.shape, a.dtype) for a in accs] + _gather_sems(n)

    def ops(self, ins, outs, scr):
        n = self.n
        accs, lands, mine = ins[:n], ins[n:], scr[:n]
        g_start, relay, finish = _gather_ops(mine, outs, *scr[n:])

        def start():
            for i in range(n):
                mine[i][...] = accs[i][...] + lands[i][0] + lands[i][1] + lands[i][2]
            g_start()

        return start, relay, finish


def _call(main, jobs, *, name, grid, ins, in_specs, out_shape, out_specs, scratch, relay_step=0):
    nsteps = grid[0] if grid else 1
    n_in, n_out, n_scr = len(ins), len(out_shape), len(scratch)

    def body(*refs):
        pos = [0]

        def take(k):
            r = refs[pos[0]:pos[0] + k]
            pos[0] += k
            return r

        m_in = take(n_in)
        j_in = [take(len(j.ins)) for j in jobs]
        m_out = take(n_out)
        j_out = [take(len(j.out_shape)) for j in jobs]
        m_scr = take(n_scr)
        j_scr = [take(len(j.scratch)) for j in jobs]
        ops = [j.ops(a, b, s) for j, a, b, s in zip(jobs, j_in, j_out, j_scr)]
        i = pl.program_id(0) if grid else 0
        if not grid:
            for o in ops:
                o[0]()
            main(i, m_in, m_out, m_scr)
            for o in ops:
                o[1]()
            for o in ops:
                o[2]()
            return

        if ops:
            @pl.when(i == 0)
            def _():
                for o in ops:
                    o[0]()

        main(i, m_in, m_out, m_scr)

        if ops:
            @pl.when(i == min(relay_step, nsteps - 1))
            def _():
                for o in ops:
                    o[1]()

            @pl.when(i == nsteps - 1)
            def _():
                for o in ops:
                    o[2]()

    extra = dict(dimension_semantics=("arbitrary",)) if grid else {}
    res = pl.pallas_call(
        body, name=name, grid=grid,
        in_specs=list(in_specs) + [s for j in jobs for s in j.in_specs],
        out_specs=list(out_specs) + [s for j in jobs for s in j.out_specs],
        out_shape=list(out_shape) + [s for j in jobs for s in j.out_shape],
        scratch_shapes=list(scratch) + [s for j in jobs for s in j.scratch],
        compiler_params=_params(**extra),
    )(*ins, *[a for j in jobs for a in j.ins])
    main_out, rest, job_out = res[:n_out], res[n_out:], []
    for j in jobs:
        k = len(j.out_shape)
        job_out.append(rest[:k])
        rest = rest[k:]
    return main_out, job_out


def _comm_only(jobs, name):
    _, job_out = _call(lambda i, a, b, s: None, jobs, name=name, grid=(), ins=[], in_specs=[], out_shape=[],
                       out_specs=[], scratch=[])
    return job_out


class _InChip:
    def __init__(self, ps):
        n = len(ps)
        self.n = n
        blk = [p.shape[1:] for p in ps]
        self.ins, self.in_specs = list(ps), [_HBM] * n
        self.out_shape = [_sds((NCHIP_OTHER,) + b, p.dtype) for b, p in zip(blk, ps)] + [_sds(b, F32) for b in blk]
        self.out_specs = [_VMEM] * (2 * n)
        self.scratch = ([pltpu.VMEM((4,) + b, p.dtype) for b, p in zip(blk, ps)] * 2
                        + [pltpu.SemaphoreType.DMA((4, n))] * 3)

    def ops(self, ins, outs, scr):
        n = self.n
        q_refs, acc_refs = outs[:n], outs[n:]
        mines, lands = scr[:n], scr[n:2 * n]
        send_sems, recv_sems, local_sems = scr[2 * n:]
        x, y, c = _place()
        sibling = (x, y, 1 - c)

        def copies():
            out = []
            for i in range(n):
                for pi in range(4):
                    loc = pltpu.make_async_copy(ins[i].at[2 * pi + c], mines[i].at[pi], local_sems.at[pi, i])
                    cp = pltpu.make_async_remote_copy(
                        src_ref=ins[i].at[2 * pi + (1 - c)], dst_ref=lands[i].at[pi],
                        send_sem=send_sems.at[pi, i], recv_sem=recv_sems.at[pi, i],
                        device_id=sibling, device_id_type=MESH)
                    out.append((loc, cp))
            return out

        def start():
            for loc, cp in copies():
                loc.start()
                cp.start()

        def finish():
            pairs = copies()
            for loc, cp in pairs:
                loc.wait()
                cp.wait_recv()
            for i in range(n):
                _chip_sums(mines[i], lands[i], q_refs[i], acc_refs[i], x, y)
            for _, cp in pairs:
                cp.wait_send()

        return start, lambda: None, finish


def _chip_sums(mine, land, q_ref, acc_ref, x, y):
    for j, (qx, qy) in enumerate(_other_chips(x, y)):
        qi = 2 * qx + qy
        q_ref[j] = (mine[qi].astype(F32) + land[qi].astype(F32)).astype(q_ref.dtype)
    mi = 2 * x + y
    acc_ref[...] = mine[mi].astype(F32) + land[mi].astype(F32)


def _allreduce_direct(v, name):
    def body(v_ref, o_ref, buf, send_sems, recv_sems):
        x, y, c = _place()
        me = 4 * x + 2 * y + c
        buf[me] = v_ref[...]
        cps = []
        for k in range(1, NDEV):
            fx, fy, fc = (k >> 2) & 1, (k >> 1) & 1, k & 1
            peer = ((1 - x) if fx else x, (1 - y) if fy else y, (1 - c) if fc else c)
            cps.append((peer, pltpu.make_async_remote_copy(
                src_ref=buf.at[me], dst_ref=buf.at[me], send_sem=send_sems.at[k - 1], recv_sem=recv_sems.at[k - 1],
                device_id=peer, device_id_type=MESH)))
        for _, cp in cps:
            cp.start()
        for k, (peer, _) in enumerate(cps):
            theirs = 4 * peer[0] + 2 * peer[1] + peer[2]
            pltpu.make_async_remote_copy(
                src_ref=buf.at[theirs], dst_ref=buf.at[theirs], send_sem=send_sems.at[k], recv_sem=recv_sems.at[k],
                device_id=peer, device_id_type=MESH).wait_recv()
        acc = buf[0]
        for j in range(1, NDEV):
            acc = acc + buf[j]
        o_ref[...] = acc
        for _, cp in cps:
            cp.wait_send()

    return pl.pallas_call(
        body, name=name, in_specs=[_VMEM], out_specs=_VMEM, out_shape=_sds(v.shape, v.dtype),
        scratch_shapes=[pltpu.VMEM((NDEV,) + v.shape, v.dtype), pltpu.SemaphoreType.DMA((NDEV - 1,)),
                        pltpu.SemaphoreType.DMA((NDEV - 1,))],
        compiler_params=_params(),
    )(v)


def _fwd_a(x, nw, win8, lnw, lnb, ws, bst, jobs, *, tm, relay_step):
    s_len = x.shape[0]
    nt = s_len // tm
    nch = tm // CH

    def main(i, ins, outs, scr):
        x_ref, nw_ref, win_ref, lnw_ref, lnb_ref, ws_ref, bst_ref = ins
        z_ref, h_ref, y_ref = outs
        wc_scr, gv_scr = scr

        @pl.when(i == 0)
        def _():
            m = _causal_mask()
            for g in range(G):
                wc_scr[g] = jnp.where(m, ws_ref[g], 0.0).astype(BF16)

        x = x_ref[...]
        h = (x * _rms(x) * nw_ref[...]).astype(BF16)
        h_ref[...] = h
        for k in range(NDEV):
            z_ref[:, k * CA:(k + 1) * CA] = _dot(h, win_ref[k])

        ssum = jnp.zeros((tm, 1), F32)
        for g in range(G):
            gv = _gelu_t(z_ref[:, AW + g * GD:AW + (g + 1) * GD])[0]
            gv_scr[:, g * GD:(g + 1) * GD] = gv
            ssum = ssum + jnp.sum(gv, axis=-1, keepdims=True)
        mu = ssum * (1.0 / AW)
        vsum = jnp.zeros((tm, 1), F32)
        for g in range(G):
            dlt = gv_scr[:, g * GD:(g + 1) * GD] - mu
            vsum = vsum + jnp.sum(dlt * dlt, axis=-1, keepdims=True)
        rstd = lax.rsqrt(vsum * (1.0 / AW) + LN_EPS)

        for g in range(G):
            cs = slice(g * GD, (g + 1) * GD)
            v = (gv_scr[:, cs] - mu) * rstd * lnw_ref[:, cs] + lnb_ref[:, cs]
            vb = v.astype(BF16)
            u = _gelu_t(z_ref[:, cs])[0]
            zg = z_ref[:, 2 * AW + g * GD:2 * AW + (g + 1) * GD]
            sg = zg * _sigmoid(zg)
            for n in range(nch):
                rs = slice(n * CH, (n + 1) * CH)
                s = _dot(wc_scr[g], vb[rs, :]) + bst_ref[:, g:g + 1]
                y_ref[rs, cs] = (u[rs, :] * s * sg[rs, :]).astype(BF16)

    tile = lambda w: pl.BlockSpec((tm, w), lambda i: (i, 0))
    return _call(
        main, jobs, name="fwd_a", grid=(nt,), relay_step=relay_step,
        ins=[x, nw, win8, lnw, lnb, ws, bst], in_specs=[tile(D), _VMEM, _VMEM, _VMEM, _VMEM, _VMEM, _VMEM],
        out_shape=[_sds((s_len, 3 * AW), F32), _sds((s_len, D), BF16), _sds((s_len, AW), BF16)],
        out_specs=[tile(3 * AW), tile(D), tile(AW)],
        scratch=[pltpu.VMEM((G, CH, CH), BF16), pltpu.VMEM((tm, AW), F32)])


def _bwd_a(dx1, z, lnw, lnb, ws, bst, wout, jobs, *, tm, relay_step):
    s_len = dx1.shape[0]
    nt = s_len // tm
    nch = tm // CH

    def main(i, ins, outs, scr):
        dx1_ref, z_ref, lnw_ref, lnb_ref, ws_ref, bst_ref, wout_ref = ins
        dz_ref, glnw_ref, glnb_ref, gws_ref, gbst_ref = outs
        wc_scr, wct_scr, vh_scr, dgv_scr, dy_scr, dv_scr, gbs_acc, gwc_acc = scr

        @pl.when(i == 0)
        def _():
            m = _causal_mask()
            for g in range(G):
                wm = jnp.where(m, ws_ref[g], 0.0)
                wc_scr[g] = wm.astype(BF16)
                wct_scr[g] = wm.T.astype(BF16)
            glnw_ref[...] = jnp.zeros_like(glnw_ref)
            glnb_ref[...] = jnp.zeros_like(glnb_ref)
            gbs_acc[...] = jnp.zeros_like(gbs_acc)
            gwc_acc[...] = jnp.zeros_like(gwc_acc)

        dy_scr[...] = _dot_nt(dx1_ref[...], wout_ref[...])

        ssum = jnp.zeros((tm, 1), F32)
        for g in range(G):
            cs = slice(g * GD, (g + 1) * GD)
            zv = z_ref[:, AW + g * GD:AW + (g + 1) * GD]
            gv, t = _gelu_t(zv)
            vh_scr[:, cs] = gv
            dgv_scr[:, cs] = _dgelu(zv, t)
            ssum = ssum + jnp.sum(gv, axis=-1, keepdims=True)
        mu = ssum * (1.0 / AW)
        vsum = jnp.zeros((tm, 1), F32)
        for g in range(G):
            dlt = vh_scr[:, g * GD:(g + 1) * GD] - mu
            vsum = vsum + jnp.sum(dlt * dlt, axis=-1, keepdims=True)
        rstd = lax.rsqrt(vsum * (1.0 / AW) + LN_EPS)

        m1 = jnp.zeros((tm, 1), F32)
        m2 = jnp.zeros((tm, 1), F32)
        for g in range(G):
            cs = slice(g * GD, (g + 1) * GD)
            gs = slice(2 * AW + g * GD, 2 * AW + (g + 1) * GD)
            vhat = (vh_scr[:, cs] - mu) * rstd
            vh_scr[:, cs] = vhat
            vb = (vhat * lnw_ref[:, cs] + lnb_ref[:, cs]).astype(BF16)
            zu = z_ref[:, cs]
            u, tu = _gelu_t(zu)
            zg = z_ref[:, gs]
            sig = _sigmoid(zg)
            sg = zg * sig
            dy = dy_scr[:, cs]
            dsf = dy * u * sg
            dsb = dsf.astype(BF16)
            dvs = []
            for n in range(nch):
                rs = slice(n * CH, (n + 1) * CH)
                s = _dot(wc_scr[g], vb[rs, :]) + bst_ref[:, g:g + 1]
                dys = dy[rs, :] * s
                dz_ref[rs, cs] = (dys * sg[rs, :] * _dgelu(zu[rs, :], tu[rs, :])).astype(BF16)
                dz_ref[rs, gs] = (dys * u[rs, :] * (sig[rs, :] * (1.0 + zg[rs, :] * (1.0 - sig[rs, :])))).astype(BF16)
                gbs_acc[g] += dsf[rs, :]
                gwc_acc[g] += _dot_nt(dsb[rs, :], vb[rs, :])
                dvs.append(_dot(wct_scr[g], dsb[rs, :]))
            dv = jnp.concatenate(dvs, axis=0) if nch > 1 else dvs[0]
            glnw_ref[:, cs] += _rowsum(dv * vhat)
            glnb_ref[:, cs] += _rowsum(dv)
            dvh = dv * lnw_ref[:, cs]
            dv_scr[:, cs] = dvh
            m1 = m1 + jnp.sum(dvh, axis=-1, keepdims=True)
            m2 = m2 + jnp.sum(dvh * vhat, axis=-1, keepdims=True)
        m1 = m1 * (1.0 / AW)
        m2 = m2 * (1.0 / AW)
        for g in range(G):
            cs = slice(g * GD, (g + 1) * GD)
            dgv = rstd * (dv_scr[:, cs] - m1 - vh_scr[:, cs] * m2)
            dz_ref[:, AW + g * GD:AW + (g + 1) * GD] = (dgv * dgv_scr[:, cs]).astype(BF16)

        @pl.when(i == nt - 1)
        def _():
            m = _causal_mask()
            for g in range(G):
                gws_ref[g] = jnp.where(m, gwc_acc[g], 0.0)
                gbst_ref[:, g:g + 1] = jnp.sum(gbs_acc[g], axis=-1, keepdims=True)

    tile = lambda w: pl.BlockSpec((tm, w), lambda i: (i, 0))
    whole = lambda *s: pl.BlockSpec(s, lambda i: (0,) * len(s))
    big = lambda dt: pltpu.VMEM((tm, AW), dt)
    return _call(
        main, jobs, name="bwd_a", grid=(nt,), relay_step=relay_step,
        ins=[dx1, z, lnw, lnb, ws, bst, wout], in_specs=[tile(D), tile(3 * AW), _VMEM, _VMEM, _VMEM, _VMEM, _VMEM],
        out_shape=[_sds((s_len, 3 * AW), BF16), _sds((1, AW), F32), _sds((1, AW), F32), _sds((G, CH, CH), F32),
                   _sds((CH, G), F32)],
        out_specs=[tile(3 * AW), whole(1, AW), whole(1, AW), whole(G, CH, CH), whole(CH, G)],
        scratch=[pltpu.VMEM((G, CH, CH), BF16), pltpu.VMEM((G, CH, CH), BF16), big(F32), big(F32), big(F32), big(F32),
                 pltpu.VMEM((G, CH, GD), F32), pltpu.VMEM((G, CH, CH), F32)])


def _bwd_a_in(dz, dx1, x, nw, win8, jobs, *, tm, relay_step):
    s_len = x.shape[0]
    nt = s_len // tm

    def main(i, ins, outs, scr):
        dz_ref, dx1_ref, x_ref, nw_ref, win_ref = ins
        gx_ref, gnw_ref = outs

        @pl.when(i == 0)
        def _():
            gnw_ref[...] = jnp.zeros_like(gnw_ref)

        dh = jnp.zeros((tm, D), F32)
        for k in range(NDEV):
            dh = dh + _dot_nt(dz_ref[:, k * CA:(k + 1) * CA], win_ref[k])
        x = x_ref[...]
        r = _rms(x)
        gx_ref[...] = dx1_ref[...] + _rms_bwd(dh, x, r, nw_ref[...])
        gnw_ref[...] += _rowsum(dh * x * r)

    tile = lambda w: pl.BlockSpec((tm, w), lambda i: (i, 0))
    return _call(
        main, jobs, name="bwd_a_in", grid=(nt,), relay_step=relay_step,
        ins=[dz, dx1, x, nw, win8], in_specs=[tile(3 * AW), tile(D), tile(D), _VMEM, _VMEM],
        out_shape=[_sds((s_len, D), F32), _sds((1, D), F32)],
        out_specs=[tile(D), pl.BlockSpec((1, D), lambda i: (0, 0))], scratch=[])


def _conv(p8_ref, cs, xb, xm1, xm2, xm3):
    xc = p8_ref[4:5, cs] + p8_ref[3:4, cs] * xb
    xc = xc + p8_ref[0:1, cs] * xm3
    xc = xc + p8_ref[1:2, cs] * xm2
    return xc + p8_ref[2:3, cs] * xm1


def _gates(p8_ref, gcat_ref, hh, xc):
    cs = slice(hh * HD, (hh + 1) * HD)
    pre = _dot(xc.astype(BF16), gcat_ref[hh])
    r = _sigmoid(pre[:, :HD] + p8_ref[5:6, cs])
    ig = _sigmoid(pre[:, HD:] + p8_ref[6:7, cs])
    sp = _softplus_neg(p8_ref[7:8, cs])
    la = (-RG_C) * r * sp
    a = jnp.exp(la)
    half_log = 0.5 * jnp.log(jnp.tanh(-la) * (1.0 + a * a))
    return r, ig, sp, a, jnp.exp(half_log), jnp.exp(-half_log)


def _scan_rows(a_ref, b_ref, out_ref, carry, tm, reverse):
    row = lax.broadcasted_iota(jnp.int32, (SUBLANES, BW), 0)
    ngrp = tm // SUBLANES

    def step(j, cr):
        jj = (ngrp - 1 - j) if reverse else j
        off = pl.multiple_of(jj * SUBLANES, SUBLANES)
        a = a_ref[pl.ds(off, SUBLANES), :]
        b = b_ref[pl.ds(off, SUBLANES), :]
        for sh in (1, 2, 4):
            if reverse:
                a_s = pltpu.roll(a, SUBLANES - sh, 0)
                b_s = pltpu.roll(b, SUBLANES - sh, 0)
                m = row < SUBLANES - sh
            else:
                a_s = pltpu.roll(a, sh, 0)
                b_s = pltpu.roll(b, sh, 0)
                m = row >= sh
            b = jnp.where(m, a * b_s + b, b)
            a = jnp.where(m, a * a_s, a)
        o = b + a * cr
        out_ref[pl.ds(off, SUBLANES), :] = o
        return o[0:1, :] if reverse else o[SUBLANES - 1:SUBLANES, :]

    return lax.fori_loop(0, ngrp, step, carry)


def _fwd_b(x, ya, wout_a, nw, win8, p8, gcat, jobs, *, tm, relay_step):
    s_len = x.shape[0]
    nt = s_len // tm

    def main(i, ins, outs, scr):
        x_ref, ya_ref, wouta_ref, nw_ref, win_ref, p8_ref, gcat_ref = ins
        x1_ref, zb_ref, hs_ref, h1_ref, yb_ref, xc_ref, a_ref, cc_ref, r_ref, ig_ref, m_ref = outs
        xbe_scr, b_scr, k_scr, carry_scr = scr

        @pl.when(i == 0)
        def _():
            xbe_scr[0:SUBLANES, :] = jnp.zeros((SUBLANES, BW), F32)
            carry_scr[...] = jnp.zeros_like(carry_scr)

        x1 = x_ref[...] + _dot(ya_ref[...], wouta_ref[...])
        x1_ref[...] = x1
        h = (x1 * _rms(x1) * nw_ref[...]).astype(BF16)
        h1_ref[...] = h
        for k in range(NDEV):
            zb_ref[:, k * CB:(k + 1) * CB] = _dot(h, win_ref[k])
        xbe_scr[SUBLANES:SUBLANES + tm, :] = zb_ref[:, :BW]
        for hh in range(BH):
            cs = slice(hh * HD, (hh + 1) * HD)
            xc = _conv(p8_ref, cs, xbe_scr[SUBLANES:SUBLANES + tm, cs], xbe_scr[7:7 + tm, cs],
                       xbe_scr[6:6 + tm, cs], xbe_scr[5:5 + tm, cs])
            r, ig, _, a, mult, rm = _gates(p8_ref, gcat_ref, hh, xc)
            ixc = ig * xc
            xc_ref[:, cs] = xc
            a_ref[:, cs] = a
            r_ref[:, cs] = r.astype(BF16)
            ig_ref[:, cs] = ig.astype(BF16)
            m_ref[:, cs] = mult.astype(BF16)
            b_scr[:, cs] = mult * ixc
            k_scr[:, cs] = ixc * (a * a * rm)
        xbe_scr[0:SUBLANES, :] = xbe_scr[tm:tm + SUBLANES, :]
        carry_scr[...] = _scan_rows(a_ref, b_scr, hs_ref, carry_scr[...], tm, False)
        for hh in range(BH):
            cs = slice(hh * HD, (hh + 1) * HD)
            gt = zb_ref[:, BW + hh * HD:BW + (hh + 1) * HD]
            hsv = hs_ref[:, cs]
            yb_ref[:, cs] = (hsv * (gt * _sigmoid(gt))).astype(BF16)
            cc_ref[:, cs] = (hsv - b_scr[:, cs]) - k_scr[:, cs]

    tile = lambda w: pl.BlockSpec((tm, w), lambda i: (i, 0))
    wide = lambda dt: _sds((s_len, BW), dt)
    return _call(
        main, jobs, name="fwd_b", grid=(nt,), relay_step=relay_step,
        ins=[x, ya, wout_a, nw, win8, p8, gcat], in_specs=[tile(D), tile(AW), _VMEM, _VMEM, _VMEM, _VMEM, _VMEM],
        out_shape=[_sds((s_len, D), F32), _sds((s_len, 2 * BW), F32), wide(F32), _sds((s_len, D), BF16), wide(BF16),
                   wide(F32), wide(F32), wide(F32), wide(BF16), wide(BF16), wide(BF16)],
        out_specs=[tile(D), tile(2 * BW), tile(BW), tile(D)] + [tile(BW)] * 7,
        scratch=[pltpu.VMEM((tm + SUBLANES, BW), F32), pltpu.VMEM((tm, BW), F32), pltpu.VMEM((tm, BW), F32),
                 pltpu.VMEM((1, BW), F32)])


def _head(x1, yb, wout, nfw, tgt, *, tm):
    s_len = x1.shape[0]

    def main(i, ins, outs, scr):
        x1_ref, yb_ref, wout_ref, nfw_ref, t_ref = ins
        dx2_ref, dx2b_ref, loss_ref, gnfw_ref = outs

        @pl.when(i == 0)
        def _():
            loss_ref[...] = jnp.zeros_like(loss_ref)
            gnfw_ref[...] = jnp.zeros_like(gnfw_ref)

        x2 = x1_ref[...] + _dot(yb_ref[...], wout_ref[...])
        rf = _rms(x2)
        xn = x2 * rf
        e = xn * nfw_ref[...] - t_ref[...]
        loss_ref[...] += (0.5 / D) * jnp.sum(jnp.sum(e * e, axis=-1, keepdims=True), axis=0, keepdims=True)
        dyf = e * (1.0 / D)
        gnfw_ref[...] += _rowsum(dyf * xn)
        dx2 = _rms_bwd(dyf, x2, rf, nfw_ref[...])
        dx2_ref[...] = dx2
        dx2b_ref[...] = dx2.astype(BF16)

    tile = lambda w: pl.BlockSpec((tm, w), lambda i: (i, 0))
    whole = lambda *s: pl.BlockSpec(s, lambda i: (0,) * len(s))
    (dx2, dx2b, loss, gnfw), _ = _call(
        main, [], name="head", grid=(s_len // tm,),
        ins=[x1, yb, wout, nfw, tgt], in_specs=[tile(D), tile(BW), _VMEM, _VMEM, tile(D)],
        out_shape=[_sds((s_len, D), F32), _sds((s_len, D), BF16), _sds((1, 1), F32), _sds((1, D), F32)],
        out_specs=[tile(D), tile(D), whole(1, 1), whole(1, D)], scratch=[])
    return dx2, dx2b, loss, gnfw


def _bwd_b(dx2, zb, hs, x1, saved, nw, win8, p8, gcat, wout, *, tm):
    s_len = x1.shape[0]
    nt = s_len // tm

    def main(i, ins, outs, scr):
        (dx2_ref, zb_ref, hs_ref, x1_ref, xc_ref, a_ref, cc_ref, r_ref, ig_ref, m_ref,
         nw_ref, win_ref, p8_ref, gcat_ref, wout_ref) = ins
        dx1_ref, dx1b_ref, dzb_ref, gp8_ref, gga_ref, ggx_ref, gnw_ref = outs
        ae_scr, an_scr, dhd_scr, dh_scr, dy_scr, dxce_scr, carry_scr, afirst_scr = scr

        @pl.when(i == 0)
        def _():
            gp8_ref[...] = jnp.zeros_like(gp8_ref)
            gga_ref[...] = jnp.zeros_like(gga_ref)
            ggx_ref[...] = jnp.zeros_like(ggx_ref)
            gnw_ref[...] = jnp.zeros_like(gnw_ref)
            dxce_scr[tm:tm + SUBLANES, :] = jnp.zeros((SUBLANES, BW), F32)
            carry_scr[...] = jnp.zeros_like(carry_scr)
            afirst_scr[...] = jnp.zeros_like(afirst_scr)

        dx2 = dx2_ref[...]
        dy_scr[...] = _dot_nt(dx2.astype(BF16), wout_ref[...])
        for hh in range(BH):
            cs = slice(hh * HD, (hh + 1) * HD)
            gs = slice(BW + hh * HD, BW + (hh + 1) * HD)
            gt = zb_ref[:, gs]
            sig = _sigmoid(gt)
            dy = dy_scr[:, cs]
            dhd_scr[:, cs] = dy * (gt * sig)
            dzb_ref[:, gs] = (dy * hs_ref[:, cs] * (sig * (1.0 + gt * (1.0 - sig)))).astype(BF16)

        ae_scr[0:tm, :] = a_ref[...]
        ae_scr[tm:tm + SUBLANES, :] = jnp.broadcast_to(afirst_scr[...], (SUBLANES, BW))
        an_scr[...] = ae_scr[1:1 + tm, :]
        afirst_scr[...] = ae_scr[0:1, :]
        carry_scr[...] = _scan_rows(an_scr, dhd_scr, dh_scr, carry_scr[...], tm, True)

        for hh in range(BH):
            cs = slice(hh * HD, (hh + 1) * HD)
            dh = dh_scr[:, cs]
            mult = m_ref[:, cs].astype(F32)
            ig = ig_ref[:, cs].astype(F32)
            r = r_ref[:, cs].astype(F32)
            xc = xc_ref[:, cs]
            lam = p8_ref[7:8, cs]
            sp = _softplus_neg(lam)
            dla = dh * cc_ref[:, cs]
            gp8_ref[7:8, cs] += _rowsum(dla * ((-RG_C) * r)) * (-_sigmoid(-lam))
            dpr = dla * ((-RG_C) * sp) * (r * (1.0 - r))
            dpi = dh * mult * xc * (ig * (1.0 - ig))
            gp8_ref[5:6, cs] += _rowsum(dpr)
            gp8_ref[6:7, cs] += _rowsum(dpi)
            dcat = jnp.concatenate([dpr, dpi], axis=1).astype(BF16)
            dxc = dh * mult * ig + _dot_nt(dcat, gcat_ref[hh])
            gg = _dot(xc.T.astype(BF16), dcat)
            gga_ref[hh] += gg[:, :HD]
            ggx_ref[hh] += gg[:, HD:]
            dxce_scr[0:tm, cs] = dxc
            gp8_ref[4:5, cs] += _rowsum(dxc)
        for hh in range(BH):
            cs = slice(hh * HD, (hh + 1) * HD)
            xb = zb_ref[:, cs]
            d0, d1 = dxce_scr[0:tm, cs], dxce_scr[1:1 + tm, cs]
            d2, d3 = dxce_scr[2:2 + tm, cs], dxce_scr[3:3 + tm, cs]
            dzb_ref[:, cs] = (p8_ref[3:4, cs] * d0 + p8_ref[2:3, cs] * d1 + p8_ref[1:2, cs] * d2
                              + p8_ref[0:1, cs] * d3).astype(BF16)
            gp8_ref[3:4, cs] += _rowsum(d0 * xb)
            gp8_ref[2:3, cs] += _rowsum(d1 * xb)
            gp8_ref[1:2, cs] += _rowsum(d2 * xb)
            gp8_ref[0:1, cs] += _rowsum(d3 * xb)
        dxce_scr[tm:tm + SUBLANES, :] = dxce_scr[0:SUBLANES, :]

        dh1 = jnp.zeros((tm, D), F32)
        for k in range(NDEV):
            dh1 = dh1 + _dot_nt(dzb_ref[:, k * CB:(k + 1) * CB], win_ref[k])
        x1 = x1_ref[...]
        r1 = _rms(x1)
        dx1 = dx2 + _rms_bwd(dh1, x1, r1, nw_ref[...])
        dx1_ref[...] = dx1
        dx1b_ref[...] = dx1.astype(BF16)
        gnw_ref[...] += _rowsum(dh1 * x1 * r1)

    tile = lambda w: pl.BlockSpec((tm, w), lambda i: (nt - 1 - i, 0))
    whole = lambda *s: pl.BlockSpec(s, lambda i: (0,) * len(s))
    full = lambda: pltpu.VMEM((tm, BW), F32)
    ext = lambda: pltpu.VMEM((tm + SUBLANES, BW), F32)
    out, _ = _call(
        main, [], name="bwd_b", grid=(nt,),
        ins=[dx2, zb, hs, x1, *saved, nw, win8, p8, gcat, wout],
        in_specs=[tile(D), tile(2 * BW), tile(BW), tile(D)] + [tile(BW)] * 6 + [_VMEM] * 5,
        out_shape=[_sds((s_len, D), F32), _sds((s_len, D), BF16), _sds((s_len, 2 * BW), BF16), _sds((SUBLANES, BW), F32),
                   _sds((BH, HD, HD), F32), _sds((BH, HD, HD), F32), _sds((1, D), F32)],
        out_specs=[tile(D), tile(D), tile(2 * BW), whole(SUBLANES, BW), whole(BH, HD, HD), whole(BH, HD, HD),
                   whole(1, D)],
        scratch=[ext(), full(), full(), full(), full(), ext(), pltpu.VMEM((1, BW), F32), pltpu.VMEM((1, BW), F32)])
    return out


def _transpose_into(dst_ref, src_ref, rows):
    s_len = src_ref.shape[0]
    for r0 in range(0, s_len, rows):
        dst_ref[:, r0:r0 + rows] = src_ref[r0:r0 + rows, :].astype(F32).T.astype(BF16)


def _wgrad(a, b, jobs, *, by_rows, per, name, relay_step=0):
    s_len, m = a.shape
    n = b.shape[1]
    r, cd = (m // NDEV, n) if by_rows else (m, n // NDEV)
    nsteps = NDEV // per
    at_rows = per * r if by_rows else m

    def main(i, ins, outs, scr):
        a_ref, b_ref = ins
        q_ref, acc_ref = outs
        at_scr, stage, mine, land, send_sems, recv_sems = scr
        x, y, c = _place()

        def to_sibling(pi):
            return pltpu.make_async_remote_copy(
                src_ref=stage.at[pi & 1], dst_ref=land.at[pi], send_sem=send_sems.at[pi], recv_sem=recv_sems.at[pi],
                device_id=(x, y, 1 - c), device_id_type=MESH)

        if by_rows:
            _transpose_into(at_scr, a_ref, 256)
        else:
            @pl.when(i == 0)
            def _():
                _transpose_into(at_scr, a_ref, 256)

        res = _dot(at_scr[...], b_ref[...]).astype(BF16)
        for k in range(per):
            blk = per * i + k
            pi, pc = blk >> 1, blk & 1
            val = res[k * r:(k + 1) * r, :] if by_rows else res

            @pl.when(pc != c)
            def _():
                @pl.when(pi >= 2)
                def _():
                    to_sibling(pi - 2).wait_send()

                stage[pi & 1] = val
                to_sibling(pi).start()

            @pl.when(pc == c)
            def _():
                mine[pi] = val

        @pl.when(i == nsteps - 1)
        def _():
            for p in range(4):
                to_sibling(p).wait_recv()
            to_sibling(2).wait_send()
            to_sibling(3).wait_send()
            _chip_sums(mine, land, q_ref, acc_ref, x, y)

    if by_rows:
        in_specs = [pl.BlockSpec((s_len, at_rows), lambda j: (0, j)), _VMEM]
    else:
        in_specs = [_VMEM, pl.BlockSpec((s_len, cd), lambda j: (0, j))]
    blk_vmem = lambda k: pltpu.VMEM((k, r, cd), BF16)
    (q, acc), job_out = _call(
        main, jobs, name=name, grid=(nsteps,), relay_step=relay_step, ins=[a, b], in_specs=in_specs,
        out_shape=[_sds((NCHIP_OTHER, r, cd), BF16), _sds((r, cd), F32)],
        out_specs=[pl.BlockSpec((NCHIP_OTHER, r, cd), lambda j: (0, 0, 0)), pl.BlockSpec((r, cd), lambda j: (0, 0))],
        scratch=[pltpu.VMEM((at_rows, s_len), BF16), blk_vmem(2), blk_vmem(4), blk_vmem(4),
                 pltpu.SemaphoreType.DMA((4,)), pltpu.SemaphoreType.DMA((4,))])
    return q, acc, job_out


def _adam_math(w, g, m, v):
    m = B1 * m + (1.0 - B1) * g
    v = B2 * v + (1.0 - B2) * (g * g)
    m_hat = m / (1.0 - B1 ** STEP)
    v_hat = v / (1.0 - B2 ** STEP)
    delta = (-LR) * (m_hat / (jnp.sqrt(v_hat) + ADAM_EPS) + WD * w)
    return delta, m, v


def _adam_big(w, acc, land, m, v, name):
    r, cd = w.shape
    rb = 256 if r % 256 == 0 else r
    nland = land.shape[0]

    def body(w_ref, acc_ref, land_ref, m_ref, v_ref, g_ref, d_ref, mo_ref, vo_ref):
        g = acc_ref[...]
        for j in range(nland):
            g = g + land_ref[j].astype(F32)
        g_ref[...] = g
        d_ref[...], mo_ref[...], vo_ref[...] = _adam_math(w_ref[...], g, m_ref[...], v_ref[...])

    blk = pl.BlockSpec((rb, cd), lambda i: (i, 0))
    blk3 = pl.BlockSpec((nland, rb, cd), lambda i: (0, i, 0))
    return pl.pallas_call(
        body, name=name, grid=(r // rb,), in_specs=[blk, blk, blk3, blk, blk], out_specs=[blk] * 4,
        out_shape=[_sds((r, cd), F32)] * 4,
        compiler_params=_params(dimension_semantics=("arbitrary",)),
    )(w, acc, land, m, v)


def _adam_small(groups):
    n = len(groups)

    def body(*refs):
        ins, outs = refs[:4 * n], refs[4 * n:]
        for k in range(n):
            w_ref, g_ref, m_ref, v_ref = ins[4 * k:4 * k + 4]
            d, mo, vo = _adam_math(w_ref[...], g_ref[...], m_ref[...], v_ref[...])
            outs[3 * k][...] = d
            outs[3 * k + 1][...] = mo
            outs[3 * k + 2][...] = vo

    flat = [a for grp in groups for a in grp]
    shapes = [_sds(grp[0].shape, F32) for grp in groups for _ in range(3)]
    res = pl.pallas_call(
        body, name="adam_small", in_specs=[_VMEM] * (4 * n), out_specs=[_VMEM] * (3 * n), out_shape=shapes,
        compiler_params=_params(),
    )(*flat)
    return [tuple(res[3 * k:3 * k + 3]) for k in range(n)]


TM_FWD_A = 256
RELAY_STEP_FWD_A = 4
RELAY_STEP_FWD_B = 2
TM_BWD_A = 256
RELAY_STEP_BWD_A = 3
TM_BWD_A_IN = 256
RELAY_STEP_BWD_A_IN = 4
TM_FWD_B = 256
TM_HEAD = 512
TM_BWD_B = 256


def _pack(parts, rows):
    flat = jnp.concatenate([p.reshape(-1) for p in parts])
    return jnp.pad(flat, (0, NDEV * rows * LANES - flat.shape[0])).reshape(NDEV, rows, LANES)


def _unpack(packed, shapes):
    flat, out, off = packed.reshape(-1), [], 0
    for s in shapes:
        size = 1
        for d in s:
            size *= d
        out.append(flat[off:off + size].reshape(s))
        off += size
    return out


def kernel(x, norm_w, a_w_in, a_ln_w, a_ln_b, a_w_s, a_b_s, a_w_out, b_w_in, b_conv_w, b_conv_b, b_gate_a_w, b_gate_a_b, b_gate_x_w, b_gate_x_b, b_lambda, b_w_out, norm_f_w, loss_target, m_norm_w, m_a_w_in, m_a_ln_w, m_a_ln_b, m_a_w_s, m_a_b_s, m_a_w_out, m_b_w_in, m_b_conv_w, m_b_conv_b, m_b_gate_a_w, m_b_gate_a_b, m_b_gate_x_w, m_b_gate_x_b, m_b_lambda, m_b_w_out, m_norm_f_w, v_norm_w, v_a_w_in, v_a_ln_w, v_a_ln_b, v_a_w_s, v_a_b_s, v_a_w_out, v_b_w_in, v_b_conv_w, v_b_conv_b, v_b_gate_a_w, v_b_gate_a_b, v_b_gate_x_w, v_b_gate_x_b, v_b_lambda, v_b_w_out, v_norm_f_w):
    me = 4 * lax.axis_index("x") + 2 * lax.axis_index("y") + lax.axis_index("c")
    xs, tgt = x[0], loss_target[0]
    nw0, nw1, nfw = norm_w[0:1], norm_w[1:2], norm_f_w.reshape(1, D)
    w_s, bst = a_w_s[0], a_b_s[0].T
    gcat = jnp.concatenate([b_gate_a_w[0], b_gate_x_w[0]], axis=-1).astype(BF16)

    p8_shard = jnp.concatenate([b_conv_w[0], b_conv_b, b_gate_a_b, b_gate_x_b, b_lambda], axis=0)
    ((win_a8, p8_all),) = _comm_only([_Gather([a_w_in[0].astype(BF16), p8_shard])], "gather_first")
    p8 = jnp.transpose(p8_all, (1, 0, 2)).reshape(SUBLANES, BW)

    (z, h0, ya), ((wout_a8, win_b8),) = _fwd_a(
        xs, nw0, win_a8, a_ln_w, a_ln_b, w_s, bst, [_Gather([a_w_out[0].astype(BF16), b_w_in[0].astype(BF16)])],
        tm=TM_FWD_A, relay_step=RELAY_STEP_FWD_A)
    wout_a = wout_a8.reshape(AW, D)
    (x1, zb, hs, h1, yb, *saved_b), ((wout_b8,),) = _fwd_b(
        xs, ya, wout_a, nw1, win_b8, p8, gcat, [_Gather([b_w_out[0].astype(BF16)])],
        tm=TM_FWD_B, relay_step=RELAY_STEP_FWD_B)
    wout_b = wout_b8.reshape(BW, D)
    dx2, dx2b, loss, g_nfw = _head(x1, yb, wout_b, nfw, tgt, tm=TM_HEAD)

    dx1, dx1b, dzb, g_p8, g_ga, g_gx, g_nw1 = _bwd_b(dx2, zb, hs, x1, saved_b, nw1, win_b8, p8, gcat, wout_b,
                                                     tm=TM_BWD_B)
    q_wout_b, acc_wout_b, _ = _wgrad(yb, dx2b, [], by_rows=True, per=2, name="wgrad_b_out")
    shapes_b = [(1, D), (1, D), (SUBLANES, BW), (1, 1)]
    pack_b = _pack([g_nfw, g_nw1, g_p8, loss], 16)
    small_b = _InChip([g_ga.reshape(NDEV, -1, HD), g_gx.reshape(NDEV, -1, HD), pack_b])
    q_win_b, acc_win_b, (sm_b, (l_wout_b,)) = _wgrad(h1, dzb, [small_b, _Exchange([q_wout_b])], by_rows=False, per=1,
                                                      name="wgrad_b_in")
    qs_b, accs_b = sm_b[:3], sm_b[3:]

    (dz, g_lnw, g_lnb, g_ws, g_bst), (lands_b, (l_win_b,)) = _bwd_a(
        dx1b, z, a_ln_w, a_ln_b, w_s, bst, wout_a, [_Exchange(qs_b), _ExchangeVia(q_win_b)],
        tm=TM_BWD_A, relay_step=RELAY_STEP_BWD_A)
    shapes_a = [(1, AW), (1, AW), (CH, G)]
    pack_a = _pack([g_lnw, g_lnb, g_bst], 8)
    q_wout_a, acc_wout_a, (red_b, sm_a) = _wgrad(
        ya, dx1b, [_SumGather(accs_b, lands_b), _InChip([g_ws, pack_a])], by_rows=True, per=2,
        name="wgrad_a_out", relay_step=1)
    qs_a, accs_a = [q_wout_a, *sm_a[:2]], [acc_wout_a, *sm_a[2:]]
    q_win_a, acc_win_a, (lands_a,) = _wgrad(h0, dz, [_Exchange(qs_a)], by_rows=False, per=1, name="wgrad_a_in")
    (gx, g_nw0), (red_a, (l_win_a,)) = _bwd_a_in(
        dz, dx1, xs, nw0, win_a8, [_SumGather(accs_a[1:], lands_a[1:]), _ExchangeVia(q_win_a)],
        tm=TM_BWD_A_IN, relay_step=RELAY_STEP_BWD_A_IN)
    g_nw0 = _allreduce_direct(g_nw0, "allreduce_norm_w0")

    r_ga, r_gx, r_pack_b = red_b
    r_nfw, r_nw1, r_p8, loss = _unpack(r_pack_b, shapes_b)
    r_ws, r_pack_a = red_a
    r_lnw, r_lnb, r_bst = _unpack(r_pack_a, shapes_a)
    g_p8 = lax.dynamic_slice_in_dim(r_p8, me * (BW // NDEV), BW // NDEV, axis=1)
    loss = loss[0, 0]

    weights = dict(norm_w=norm_w, a_w_in=a_w_in, a_ln_w=a_ln_w, a_ln_b=a_ln_b, a_w_s=a_w_s, a_b_s=a_b_s, a_w_out=a_w_out,
                   b_w_in=b_w_in, b_conv_w=b_conv_w, b_conv_b=b_conv_b, b_gate_a_w=b_gate_a_w, b_gate_a_b=b_gate_a_b,
                   b_gate_x_w=b_gate_x_w, b_gate_x_b=b_gate_x_b, b_lambda=b_lambda, b_w_out=b_w_out, norm_f_w=norm_f_w)
    mom1 = dict(norm_w=m_norm_w, a_w_in=m_a_w_in, a_ln_w=m_a_ln_w, a_ln_b=m_a_ln_b, a_w_s=m_a_w_s, a_b_s=m_a_b_s,
                a_w_out=m_a_w_out, b_w_in=m_b_w_in, b_conv_w=m_b_conv_w, b_conv_b=m_b_conv_b, b_gate_a_w=m_b_gate_a_w,
                b_gate_a_b=m_b_gate_a_b, b_gate_x_w=m_b_gate_x_w, b_gate_x_b=m_b_gate_x_b, b_lambda=m_b_lambda,
                b_w_out=m_b_w_out, norm_f_w=m_norm_f_w)
    mom2 = dict(norm_w=v_norm_w, a_w_in=v_a_w_in, a_ln_w=v_a_ln_w, a_ln_b=v_a_ln_b, a_w_s=v_a_w_s, a_b_s=v_a_b_s,
                a_w_out=v_a_w_out, b_w_in=v_b_w_in, b_conv_w=v_b_conv_w, b_conv_b=v_b_conv_b, b_gate_a_w=v_b_gate_a_w,
                b_gate_a_b=v_b_gate_a_b, b_gate_x_w=v_b_gate_x_w, b_gate_x_b=v_b_gate_x_b, b_lambda=v_b_lambda,
                b_w_out=v_b_w_out, norm_f_w=v_norm_f_w)
    names = list(weights)

    def as2d(a):
        return a.reshape(-1, a.shape[-1])

    upd, grads = {}, {}
    for k, acc, land in (("a_w_in", acc_win_a, l_win_a), ("a_w_out", accs_a[0], lands_a[0]),
                         ("b_w_in", acc_win_b, l_win_b), ("b_w_out", acc_wout_b, l_wout_b)):
        g, d, mo, vo = _adam_big(as2d(weights[k]), acc, land, as2d(mom1[k]), as2d(mom2[k]), "adam_" + k)
        grads[k] = g[None]
        upd[k] = (d, mo, vo)
    grads.update(
        norm_w=jnp.concatenate([g_nw0, r_nw1], axis=0), a_ln_w=r_lnw, a_ln_b=r_lnb,
        a_w_s=r_ws.reshape(1, G, CH, CH), a_b_s=r_bst.T[None],
        b_conv_w=g_p8[None, 0:4], b_conv_b=g_p8[4:5], b_gate_a_w=r_ga.reshape(1, BH, HD, HD), b_gate_a_b=g_p8[5:6],
        b_gate_x_w=r_gx.reshape(1, BH, HD, HD), b_gate_x_b=g_p8[6:7], b_lambda=g_p8[7:8], norm_f_w=r_nfw.reshape(D))
    small_names = [k for k in names if k not in upd]
    res = _adam_small([(as2d(weights[k]), as2d(grads[k]), as2d(mom1[k]), as2d(mom2[k])) for k in small_names])
    for k, r3 in zip(small_names, res):
        upd[k] = r3
    deltas = [upd[k][0].reshape(weights[k].shape) for k in names]
    new_m = [upd[k][1].reshape(weights[k].shape) for k in names]
    new_v = [upd[k][2].reshape(weights[k].shape) for k in names]
    return (loss, gx[None], *[grads[k] for k in names], *deltas, *new_m, *new_v)
```

```python
import jax
import jax.numpy as jnp
from jax import lax
from jax.experimental import pallas as pl
from jax.experimental.pallas import tpu as pltpu

F32 = jnp.float32
BF16 = jnp.bfloat16
MESH = pl.DeviceIdType.MESH

NDEV = 8
NCHIP_OTHER = 3
D = 1024
AW = 2048
G = 8
GD = AW // G
CH = 128
BW = 1536
BH = 12
HD = BW // BH
CA = 3 * AW // NDEV
CB = 2 * BW // NDEV
RMS_EPS = 1e-6
LN_EPS = 1e-5
RG_C = 8.0
LR, B1, B2, ADAM_EPS, WD, STEP = 0.001, 0.9, 0.999, 1e-08, 0.01, 10
V7X_VMEM_BYTES = 64 * 1024 * 1024
VMEM_LIMIT = V7X_VMEM_BYTES - 8 * 1024 * 1024
SUBLANES = 8
LANES = 128
BF16_ROWS = 16
GELU_C = 0.7978845608028654
GELU_K = 0.044715

_VMEM = pl.BlockSpec(memory_space=pltpu.VMEM)
_HBM = pl.BlockSpec(memory_space=pltpu.HBM)


def _sds(shape, dtype):
    return jax.ShapeDtypeStruct(tuple(shape), dtype)


def _params(**kw):
    return pltpu.CompilerParams(vmem_limit_bytes=VMEM_LIMIT, **kw)


def _gelu_t(z):
    t = jnp.tanh(GELU_C * (z + GELU_K * (z * z * z)))
    return 0.5 * z * (1.0 + t), t


def _dgelu(z, t):
    return 0.5 * (1.0 + t) + 0.5 * z * (1.0 - t * t) * (GELU_C * (1.0 + 3.0 * GELU_K * z * z))


def _sigmoid(v):
    return 0.5 * jnp.tanh(0.5 * v) + 0.5


def _softplus_neg(lam):
    return jnp.maximum(-lam, 0.0) + jnp.log1p(jnp.exp(-jnp.abs(lam)))


def _dot(a, b):
    return jnp.dot(a, b, preferred_element_type=F32)


def _dot_nt(a, b):
    return lax.dot_general(a, b, (((1,), (1,)), ((), ())), preferred_element_type=F32)


def _rowsum(v):
    return jnp.sum(v, axis=0, keepdims=True)


def _causal_mask():
    r = lax.broadcasted_iota(jnp.int32, (CH, CH), 0)
    c = lax.broadcasted_iota(jnp.int32, (CH, CH), 1)
    return r >= c


def _rms(x):
    return lax.rsqrt(jnp.mean(x * x, axis=-1, keepdims=True) + RMS_EPS)


def _rms_bwd(dh, x, r, nw):
    gy = dh * nw
    return r * gy - x * (r * r * r) * jnp.mean(gy * x, axis=-1, keepdims=True)


def _place():
    return lax.axis_index("x"), lax.axis_index("y"), lax.axis_index("c")


def _other_chips(x, y):
    return [(1 - x, y), (x, 1 - y), (1 - x, 1 - y)]


GATHER_SLOTS = 10


def _gather_ops(ins, outs, send_sems, recv_sems, local_sems):
    n = len(ins)
    x, y, c = _place()
    sibling = (x, y, 1 - c)
    xn, yn, dg = _other_chips(x, y)
    split = [ins[i].shape[0] % (2 * BF16_ROWS) == 0 for i in range(n)]

    def blk(chip, core):
        return 4 * chip[0] + 2 * chip[1] + core

    me = blk((x, y), c)

    def part(ref, i, half):
        if half is None:
            return ref
        h = ins[i].shape[0] // 2
        return ref.at[pl.ds(half * h, h)]

    def copy(i, k, block, to, half=None, src=None):
        dst = part(outs[i].at[block], i, half)
        return pltpu.make_async_remote_copy(
            src_ref=dst if src is None else part(src, i, half), dst_ref=dst,
            send_sem=send_sems.at[k, i], recv_sem=recv_sems.at[k, i], device_id=to, device_id_type=MESH)

    def first_copies():
        mine = [pltpu.make_async_copy(ins[i], outs[i].at[me], local_sems.at[i]) for i in range(n)]
        first = []
        for i in range(n):
            first.append(copy(i, 0, me, sibling, src=ins[i]))
            if split[i]:
                first.append(copy(i, 1, me, (*xn, c), 0, ins[i]))
                first.append(copy(i, 3, me, (*yn, c), 1, ins[i]))
                first.append(copy(i, 2, me, (*xn, c), 1, ins[i]))
                first.append(copy(i, 4, me, (*yn, c), 0, ins[i]))
            else:
                first.append(copy(i, 1, me, (*xn, c), None, ins[i]))
                first.append(copy(i, 3, me, (*yn, c), None, ins[i]))
                first.append(copy(i, 5, me, (*dg, c), None, ins[i]))
        return mine, first

    def onward():
        out = []
        for i in range(n):
            if split[i]:
                out.append(copy(i, 5, blk(xn, c), (*yn, c), 0))
                out.append(copy(i, 6, blk(yn, c), (*xn, c), 1))
        return out

    def start():
        mine, first = first_copies()
        for cp in mine + first:
            cp.start()

    def relay():
        sends = onward()
        for i in range(n):
            if split[i]:
                copy(i, 1, blk(xn, c), sibling, 0).wait_recv()
                sends.pop(0).start()
                copy(i, 3, blk(yn, c), sibling, 1).wait_recv()
                sends.pop(0).start()

    def finish():
        mine, first = first_copies()
        passed = []

        def pass_on(i, j, chip):
            fwd = copy(i, 7 + j, blk(chip, c), sibling)
            fwd.start()
            passed.append(fwd)

        for i in range(n):
            if split[i]:
                copy(i, 2, blk(xn, c), sibling, 1).wait_recv()
                pass_on(i, 0, xn)
                copy(i, 4, blk(yn, c), sibling, 0).wait_recv()
                pass_on(i, 1, yn)
                copy(i, 5, blk(dg, c), sibling, 0).wait_recv()
                copy(i, 6, blk(dg, c), sibling, 1).wait_recv()
                pass_on(i, 2, dg)
            else:
                copy(i, 1, blk(xn, c), sibling).wait_recv()
                pass_on(i, 0, xn)
                copy(i, 3, blk(yn, c), sibling).wait_recv()
                pass_on(i, 1, yn)
                copy(i, 5, blk(dg, c), sibling).wait_recv()
                pass_on(i, 2, dg)
        for i in range(n):
            copy(i, 0, blk((x, y), 1 - c), sibling).wait_recv()
            for j, chip in enumerate((xn, yn, dg)):
                copy(i, 7 + j, blk(chip, 1 - c), sibling).wait_recv()
        for cp in first + passed + onward():
            cp.wait_send()
        for cp in mine:
            cp.wait()

    return start, relay, finish


def _gather_sems(n):
    return [pltpu.SemaphoreType.DMA((GATHER_SLOTS, n)), pltpu.SemaphoreType.DMA((GATHER_SLOTS, n)),
            pltpu.SemaphoreType.DMA((n,))]


class _Gather:
    def __init__(self, shards):
        n = len(shards)
        self.ins, self.in_specs = list(shards), [_HBM] * n
        self.out_shape = [_sds((NDEV,) + s.shape, s.dtype) for s in shards]
        self.out_specs = [_HBM] * n
        self.scratch = _gather_sems(n)

    def ops(self, ins, outs, scr):
        return _gather_ops(ins, outs, *scr)


class _Exchange:
    def __init__(self, qs):
        n = len(qs)
        self.ins, self.in_specs = list(qs), [_HBM] * n
        self.out_shape = [_sds(q.shape, q.dtype) for q in qs]
        self.out_specs = [_HBM] * n
        self.scratch = [pltpu.SemaphoreType.DMA((NCHIP_OTHER, n)), pltpu.SemaphoreType.DMA((NCHIP_OTHER, n))]

    def ops(self, ins, outs, scr):
        send_sems, recv_sems = scr
        n = len(ins)
        x, y, c = _place()
        chips = _other_chips(x, y)

        def copies():
            return [pltpu.make_async_remote_copy(
                src_ref=ins[i].at[j], dst_ref=outs[i].at[j], send_sem=send_sems.at[j, i],
                recv_sem=recv_sems.at[j, i], device_id=(*chips[j], c), device_id_type=MESH)
                for i in range(n) for j in range(NCHIP_OTHER)]

        def start():
            for cp in copies():
                cp.start()

        def finish():
            cps = copies()
            for cp in cps:
                cp.wait_recv()
            for cp in cps:
                cp.wait_send()

        return start, lambda: None, finish


class _ExchangeVia:
    def __init__(self, q):
        _, r, cd = q.shape
        half = (2, r // 2, cd)
        self.ins, self.in_specs = [q], [_HBM]
        self.out_shape, self.out_specs = [_sds((2, r, cd), q.dtype)], [_HBM]
        self.scratch = [pltpu.VMEM(half, q.dtype), pltpu.VMEM(half, q.dtype), pltpu.VMEM(half, q.dtype),
                        pltpu.SemaphoreType.DMA((6,)), pltpu.SemaphoreType.DMA((6,)), pltpu.SemaphoreType.DMA((2,))]

    def ops(self, ins, outs, scr):
        (q,), (land,) = ins, outs
        relayed, own, comb, send_sems, recv_sems, local_sems = scr
        h = q.shape[1] // 2
        x, y, c = _place()
        xn, yn, _ = _other_chips(x, y)
        h0, h1 = pl.ds(0, h), pl.ds(h, h)

        def remote(k, src, dst, chip):
            return pltpu.make_async_remote_copy(src_ref=src, dst_ref=dst, send_sem=send_sems.at[k],
                                                recv_sem=recv_sems.at[k], device_id=(*chip, c), device_id_type=MESH)

        def via():
            return [remote(2, q.at[2, h0], relayed.at[0], xn), remote(3, q.at[2, h1], relayed.at[1], yn)]

        def direct():
            return [remote(0, q.at[0, h0], land.at[0, h0], xn), remote(1, q.at[1, h1], land.at[1, h1], yn)]

        def second():
            return [remote(4, comb.at[0], land.at[1, h0], yn), remote(5, comb.at[1], land.at[0, h1], xn)]

        def mine():
            return [pltpu.make_async_copy(q.at[1, h0], own.at[0], local_sems.at[0]),
                    pltpu.make_async_copy(q.at[0, h1], own.at[1], local_sems.at[1])]

        def start():
            for cp in via() + direct() + mine():
                cp.start()

        def relay():
            arrived, loaded, onward = via(), mine(), second()
            for k in range(2):
                arrived[k].wait_recv()
                loaded[k].wait()
                comb[k] = (own[k].astype(F32) + relayed[k].astype(F32)).astype(comb.dtype)
                onward[k].start()

        def finish():
            landing = direct() + second()
            for cp in landing:
                cp.wait_recv()
            for cp in via() + landing:
                cp.wait_send()

        return start, relay, finish


class _SumGather:
    def __init__(self, accs, lands):
        n = len(accs)
        self.n = n
        self.ins, self.in_specs = list(accs) + list(lands), [_VMEM] * (2 * n)
        self.out_shape = [_sds((NDEV,) + a.shape, a.dtype) for a in accs]
        self.out_specs = [_HBM] * n
        self.scratch = [pltpu.VMEM(a.shape, a.dtype) for a in accs] + _gather_sems(n)

    def ops(self, ins, outs, scr):
        n = self.n
        accs, lands, mine = ins[:n], ins[n:], scr[:n]
        g_start, relay, finish = _gather_ops(mine, outs, *scr[n:])

        def start():
            for i in range(n):
                mine[i][...] = accs[i][...] + lands[i][0] + lands[i][1] + lands[i][2]
            g_start()

        return start, relay, finish


def _call(main, jobs, *, name, grid, ins, in_specs, out_shape, out_specs, scratch, relay_step=0):
    nsteps = grid[0] if grid else 1
    n_in, n_out, n_scr = len(ins), len(out_shape), len(scratch)

    def body(*refs):
        pos = [0]

        def take(k):
            r = refs[pos[0]:pos[0] + k]
            pos[0] += k
            return r

        m_in = take(n_in)
        j_in = [take(len(j.ins)) for j in jobs]
        m_out = take(n_out)
        j_out = [take(len(j.out_shape)) for j in jobs]
        m_scr = take(n_scr)
        j_scr = [take(len(j.scratch)) for j in jobs]
        ops = [j.ops(a, b, s) for j, a, b, s in zip(jobs, j_in, j_out, j_scr)]
        i = pl.program_id(0) if grid else 0
        if not grid:
            for o in ops:
                o[0]()
            main(i, m_in, m_out, m_scr)
            for o in ops:
                o[1]()
            for o in ops:
                o[2]()
            return

        if ops:
            @pl.when(i == 0)
            def _():
                for o in ops:
                    o[0]()

        main(i, m_in, m_out, m_scr)

        if ops:
            @pl.when(i == min(relay_step, nsteps - 1))
            def _():
                for o in ops:
                    o[1]()

            @pl.when(i == nsteps - 1)
            def _():
                for o in ops:
                    o[2]()

    extra = dict(dimension_semantics=("arbitrary",)) if grid else {}
    res = pl.pallas_call(
        body, name=name, grid=grid,
        in_specs=list(in_specs) + [s for j in jobs for s in j.in_specs],
        out_specs=list(out_specs) + [s for j in jobs for s in j.out_specs],
        out_shape=list(out_shape) + [s for j in jobs for s in j.out_shape],
        scratch_shapes=list(scratch) + [s for j in jobs for s in j.scratch],
        compiler_params=_params(**extra),
    )(*ins, *[a for j in jobs for a in j.ins])
    main_out, rest, job_out = res[:n_out], res[n_out:], []
    for j in jobs:
        k = len(j.out_shape)
        job_out.append(rest[:k])
        rest = rest[k:]
    return main_out, job_out


def _comm_only(jobs, name):
    _, job_out = _call(lambda i, a, b, s: None, jobs, name=name, grid=(), ins=[], in_specs=[], out_shape=[],
                       out_specs=[], scratch=[])
    return job_out


class _InChip:
    def __init__(self, ps):
        n = len(ps)
        self.n = n
        blk = [p.shape[1:] for p in ps]
        self.ins, self.in_specs = list(ps), [_HBM] * n
        self.out_shape = [_sds((NCHIP_OTHER,) + b, p.dtype) for b, p in zip(blk, ps)] + [_sds(b, F32) for b in blk]
        self.out_specs = [_VMEM] * (2 * n)
        self.scratch = ([pltpu.VMEM((4,) + b, p.dtype) for b, p in zip(blk, ps)] * 2
                        + [pltpu.SemaphoreType.DMA((4, n))] * 3)

    def ops(self, ins, outs, scr):
        n = self.n
        q_refs, acc_refs = outs[:n], outs[n:]
        mines, lands = scr[:n], scr[n:2 * n]
        send_sems, recv_sems, local_sems = scr[2 * n:]
        x, y, c = _place()
        sibling = (x, y, 1 - c)

        def copies():
            out = []
            for i in range(n):
                for pi in range(4):
                    loc = pltpu.make_async_copy(ins[i].at[2 * pi + c], mines[i].at[pi], local_sems.at[pi, i])
                    cp = pltpu.make_async_remote_copy(
                        src_ref=ins[i].at[2 * pi + (1 - c)], dst_ref=lands[i].at[pi],
                        send_sem=send_sems.at[pi, i], recv_sem=recv_sems.at[pi, i],
                        device_id=sibling, device_id_type=MESH)
                    out.append((loc, cp))
            return out

        def start():
            for loc, cp in copies():
                loc.start()
                cp.start()

        def finish():
            pairs = copies()
            for loc, cp in pairs:
                loc.wait()
                cp.wait_recv()
            for i in range(n):
                _chip_sums(mines[i], lands[i], q_refs[i], acc_refs[i], x, y)
            for _, cp in pairs:
                cp.wait_send()

        return start, lambda: None, finish


def _chip_sums(mine, land, q_ref, acc_ref, x, y):
    for j, (qx, qy) in enumerate(_other_chips(x, y)):
        qi = 2 * qx + qy
        q_ref[j] = (mine[qi].astype(F32) + land[qi].astype(F32)).astype(q_ref.dtype)
    mi = 2 * x + y
    acc_ref[...] = mine[mi].astype(F32) + land[mi].astype(F32)


def _allreduce_direct(v, name):
    def body(v_ref, o_ref, buf, send_sems, recv_sems):
        x, y, c = _place()
        me = 4 * x + 2 * y + c
        buf[me] = v_ref[...]
        cps = []
        for k in range(1, NDEV):
            fx, fy, fc = (k >> 2) & 1, (k >> 1) & 1, k & 1
            peer = ((1 - x) if fx else x, (1 - y) if fy else y, (1 - c) if fc else c)
            cps.append((peer, pltpu.make_async_remote_copy(
                src_ref=buf.at[me], dst_ref=buf.at[me], send_sem=send_sems.at[k - 1], recv_sem=recv_sems.at[k - 1],
                device_id=peer, device_id_type=MESH)))
        for _, cp in cps:
            cp.start()
        for k, (peer, _) in enumerate(cps):
            theirs = 4 * peer[0] + 2 * peer[1] + peer[2]
            pltpu.make_async_remote_copy(
                src_ref=buf.at[theirs], dst_ref=buf.at[theirs], send_sem=send_sems.at[k], recv_sem=recv_sems.at[k],
                device_id=peer, device_id_type=MESH).wait_recv()
        acc = buf[0]
        for j in range(1, NDEV):
            acc = acc + buf[j]
        o_ref[...] = acc
        for _, cp in cps:
            cp.wait_send()

    return pl.pallas_call(
        body, name=name, in_specs=[_VMEM], out_specs=_VMEM, out_shape=_sds(v.shape, v.dtype),
        scratch_shapes=[pltpu.VMEM((NDEV,) + v.shape, v.dtype), pltpu.SemaphoreType.DMA((NDEV - 1,)),
                        pltpu.SemaphoreType.DMA((NDEV - 1,))],
        compiler_params=_params(),
    )(v)


def _fwd_a_top(x, nw, wtop8, jobs, *, tm, relay_step):
    s_len = x.shape[0]
    nt = s_len // tm

    def main(i, ins, outs, scr):
        x_ref, nw_ref, win_ref = ins
        h_ref, z_ref = outs
        x = x_ref[...]
        h = (x * _rms(x) * nw_ref[...]).astype(BF16)
        h_ref[...] = h
        for k in range(NDEV):
            z_ref[:, k * CA:(k + 1) * CA] = _dot(h[:, :D // 2], win_ref[k])

    tile = lambda w: pl.BlockSpec((tm, w), lambda i: (i, 0))
    return _call(
        main, jobs, name="fwd_a_top", grid=(nt,), relay_step=relay_step,
        ins=[x, nw, wtop8], in_specs=[tile(D), _VMEM, _VMEM],
        out_shape=[_sds((s_len, D), BF16), _sds((s_len, 3 * AW), F32)], out_specs=[tile(D), tile(3 * AW)], scratch=[])


def _fwd_a(h, zp, wbot8, lnw, lnb, ws, bst, jobs, *, tm, relay_step):
    s_len = h.shape[0]
    nt = s_len // tm
    nch = tm // CH

    def main(i, ins, outs, scr):
        h_ref, zp_ref, win_ref, lnw_ref, lnb_ref, ws_ref, bst_ref = ins
        z_ref, y_ref = outs
        wc_scr, gv_scr = scr

        @pl.when(i == 0)
        def _():
            m = _causal_mask()
            for g in range(G):
                wc_scr[g] = jnp.where(m, ws_ref[g], 0.0).astype(BF16)

        h = h_ref[:, D // 2:]
        for k in range(NDEV):
            ks = slice(k * CA, (k + 1) * CA)
            z_ref[:, ks] = zp_ref[:, ks] + _dot(h, win_ref[k])

        ssum = jnp.zeros((tm, 1), F32)
        for g in range(G):
            gv = _gelu_t(z_ref[:, AW + g * GD:AW + (g + 1) * GD])[0]
            gv_scr[:, g * GD:(g + 1) * GD] = gv
            ssum = ssum + jnp.sum(gv, axis=-1, keepdims=True)
        mu = ssum * (1.0 / AW)
        vsum = jnp.zeros((tm, 1), F32)
        for g in range(G):
            dlt = gv_scr[:, g * GD:(g + 1) * GD] - mu
            vsum = vsum + jnp.sum(dlt * dlt, axis=-1, keepdims=True)
        rstd = lax.rsqrt(vsum * (1.0 / AW) + LN_EPS)

        for g in range(G):
            cs = slice(g * GD, (g + 1) * GD)
            v = (gv_scr[:, cs] - mu) * rstd * lnw_ref[:, cs] + lnb_ref[:, cs]
            vb = v.astype(BF16)
            u = _gelu_t(z_ref[:, cs])[0]
            zg = z_ref[:, 2 * AW + g * GD:2 * AW + (g + 1) * GD]
            sg = zg * _sigmoid(zg)
            for n in range(nch):
                rs = slice(n * CH, (n + 1) * CH)
                s = _dot(wc_scr[g], vb[rs, :]) + bst_ref[:, g:g + 1]
                y_ref[rs, cs] = (u[rs, :] * s * sg[rs, :]).astype(BF16)

    tile = lambda w: pl.BlockSpec((tm, w), lambda i: (i, 0))
    return _call(
        main, jobs, name="fwd_a", grid=(nt,), relay_step=relay_step,
        ins=[h, zp, wbot8, lnw, lnb, ws, bst], in_specs=[tile(D), tile(3 * AW), _VMEM, _VMEM, _VMEM, _VMEM, _VMEM],
        out_shape=[_sds((s_len, 3 * AW), F32), _sds((s_len, AW), BF16)],
        out_specs=[tile(3 * AW), tile(AW)],
        scratch=[pltpu.VMEM((G, CH, CH), BF16), pltpu.VMEM((tm, AW), F32)])


def _bwd_a(dx1, z, lnw, lnb, ws, bst, wout, jobs, *, tm, relay_step):
    s_len = dx1.shape[0]
    nt = s_len // tm
    nch = tm // CH

    def main(i, ins, outs, scr):
        dx1_ref, z_ref, lnw_ref, lnb_ref, ws_ref, bst_ref, wout_ref = ins
        dz_ref, glnw_ref, glnb_ref, gws_ref, gbst_ref = outs
        wc_scr, wct_scr, vh_scr, dgv_scr, dy_scr, dv_scr, gbs_acc, gwc_acc = scr

        @pl.when(i == 0)
        def _():
            m = _causal_mask()
            for g in range(G):
                wm = jnp.where(m, ws_ref[g], 0.0)
                wc_scr[g] = wm.astype(BF16)
                wct_scr[g] = wm.T.astype(BF16)
            glnw_ref[...] = jnp.zeros_like(glnw_ref)
            glnb_ref[...] = jnp.zeros_like(glnb_ref)
            gbs_acc[...] = jnp.zeros_like(gbs_acc)
            gwc_acc[...] = jnp.zeros_like(gwc_acc)

        dy_scr[...] = _dot_nt(dx1_ref[...], wout_ref[...])

        ssum = jnp.zeros((tm, 1), F32)
        for g in range(G):
            cs = slice(g * GD, (g + 1) * GD)
            zv = z_ref[:, AW + g * GD:AW + (g + 1) * GD]
            gv, t = _gelu_t(zv)
            vh_scr[:, cs] = gv
            dgv_scr[:, cs] = _dgelu(zv, t)
            ssum = ssum + jnp.sum(gv, axis=-1, keepdims=True)
        mu = ssum * (1.0 / AW)
        vsum = jnp.zeros((tm, 1), F32)
        for g in range(G):
            dlt = vh_scr[:, g * GD:(g + 1) * GD] - mu
            vsum = vsum + jnp.sum(dlt * dlt, axis=-1, keepdims=True)
        rstd = lax.rsqrt(vsum * (1.0 / AW) + LN_EPS)

        m1 = jnp.zeros((tm, 1), F32)
        m2 = jnp.zeros((tm, 1), F32)
        for g in range(G):
            cs = slice(g * GD, (g + 1) * GD)
            gs = slice(2 * AW + g * GD, 2 * AW + (g + 1) * GD)
            vhat = (vh_scr[:, cs] - mu) * rstd
            vh_scr[:, cs] = vhat
            vb = (vhat * lnw_ref[:, cs] + lnb_ref[:, cs]).astype(BF16)
            zu = z_ref[:, cs]
            u, tu = _gelu_t(zu)
            zg = z_ref[:, gs]
            sig = _sigmoid(zg)
            sg = zg * sig
            dy = dy_scr[:, cs]
            dsf = dy * u * sg
            dsb = dsf.astype(BF16)
            dvs = []
            for n in range(nch):
                rs = slice(n * CH, (n + 1) * CH)
                s = _dot(wc_scr[g], vb[rs, :]) + bst_ref[:, g:g + 1]
                dys = dy[rs, :] * s
                dz_ref[rs, cs] = (dys * sg[rs, :] * _dgelu(zu[rs, :], tu[rs, :])).astype(BF16)
                dz_ref[rs, gs] = (dys * u[rs, :] * (sig[rs, :] * (1.0 + zg[rs, :] * (1.0 - sig[rs, :])))).astype(BF16)
                gbs_acc[g] += dsf[rs, :]
                gwc_acc[g] += _dot_nt(dsb[rs, :], vb[rs, :])
                dvs.append(_dot(wct_scr[g], dsb[rs, :]))
            dv = jnp.concatenate(dvs, axis=0) if nch > 1 else dvs[0]
            glnw_ref[:, cs] += _rowsum(dv * vhat)
            glnb_ref[:, cs] += _rowsum(dv)
            dvh = dv * lnw_ref[:, cs]
            dv_scr[:, cs] = dvh
            m1 = m1 + jnp.sum(dvh, axis=-1, keepdims=True)
            m2 = m2 + jnp.sum(dvh * vhat, axis=-1, keepdims=True)
        m1 = m1 * (1.0 / AW)
        m2 = m2 * (1.0 / AW)
        for g in range(G):
            cs = slice(g * GD, (g + 1) * GD)
            dgv = rstd * (dv_scr[:, cs] - m1 - vh_scr[:, cs] * m2)
            dz_ref[:, AW + g * GD:AW + (g + 1) * GD] = (dgv * dgv_scr[:, cs]).astype(BF16)

        @pl.when(i == nt - 1)
        def _():
            m = _causal_mask()
            for g in range(G):
                gws_ref[g] = jnp.where(m, gwc_acc[g], 0.0)
                gbst_ref[:, g:g + 1] = jnp.sum(gbs_acc[g], axis=-1, keepdims=True)

    tile = lambda w: pl.BlockSpec((tm, w), lambda i: (i, 0))
    whole = lambda *s: pl.BlockSpec(s, lambda i: (0,) * len(s))
    big = lambda dt: pltpu.VMEM((tm, AW), dt)
    return _call(
        main, jobs, name="bwd_a", grid=(nt,), relay_step=relay_step,
        ins=[dx1, z, lnw, lnb, ws, bst, wout], in_specs=[tile(D), tile(3 * AW), _VMEM, _VMEM, _VMEM, _VMEM, _VMEM],
        out_shape=[_sds((s_len, 3 * AW), BF16), _sds((1, AW), F32), _sds((1, AW), F32), _sds((G, CH, CH), F32),
                   _sds((CH, G), F32)],
        out_specs=[tile(3 * AW), whole(1, AW), whole(1, AW), whole(G, CH, CH), whole(CH, G)],
        scratch=[pltpu.VMEM((G, CH, CH), BF16), pltpu.VMEM((G, CH, CH), BF16), big(F32), big(F32), big(F32), big(F32),
                 pltpu.VMEM((G, CH, GD), F32), pltpu.VMEM((G, CH, CH), F32)])


def _bwd_a_in(dz, dx1, x, nw, wtop8, wbot8, jobs, *, tm, relay_step):
    s_len = x.shape[0]
    nt = s_len // tm

    def main(i, ins, outs, scr):
        dz_ref, dx1_ref, x_ref, nw_ref, wtop_ref, wbot_ref = ins
        gx_ref, gnw_ref = outs

        @pl.when(i == 0)
        def _():
            gnw_ref[...] = jnp.zeros_like(gnw_ref)

        lo = jnp.zeros((tm, D // 2), F32)
        hi = jnp.zeros((tm, D // 2), F32)
        for k in range(NDEV):
            dzk = dz_ref[:, k * CA:(k + 1) * CA]
            lo = lo + _dot_nt(dzk, wtop_ref[k])
            hi = hi + _dot_nt(dzk, wbot_ref[k])
        dh = jnp.concatenate([lo, hi], axis=1)
        x = x_ref[...]
        r = _rms(x)
        gx_ref[...] = dx1_ref[...] + _rms_bwd(dh, x, r, nw_ref[...])
        gnw_ref[...] += _rowsum(dh * x * r)

    tile = lambda w: pl.BlockSpec((tm, w), lambda i: (i, 0))
    return _call(
        main, jobs, name="bwd_a_in", grid=(nt,), relay_step=relay_step,
        ins=[dz, dx1, x, nw, wtop8, wbot8], in_specs=[tile(3 * AW), tile(D), tile(D), _VMEM, _VMEM, _VMEM],
        out_shape=[_sds((s_len, D), F32), _sds((1, D), F32)],
        out_specs=[tile(D), pl.BlockSpec((1, D), lambda i: (0, 0))], scratch=[])


def _conv(p8_ref, cs, xb, xm1, xm2, xm3):
    xc = p8_ref[4:5, cs] + p8_ref[3:4, cs] * xb
    xc = xc + p8_ref[0:1, cs] * xm3
    xc = xc + p8_ref[1:2, cs] * xm2
    return xc + p8_ref[2:3, cs] * xm1


def _gates(p8_ref, gcat_ref, hh, xc):
    cs = slice(hh * HD, (hh + 1) * HD)
    pre = _dot(xc.astype(BF16), gcat_ref[hh])
    r = _sigmoid(pre[:, :HD] + p8_ref[5:6, cs])
    ig = _sigmoid(pre[:, HD:] + p8_ref[6:7, cs])
    sp = _softplus_neg(p8_ref[7:8, cs])
    la = (-RG_C) * r * sp
    a = jnp.exp(la)
    half_log = 0.5 * jnp.log(jnp.tanh(-la) * (1.0 + a * a))
    return r, ig, sp, a, jnp.exp(half_log), jnp.exp(-half_log)


def _scan_rows(a_ref, b_ref, out_ref, carry, tm, reverse):
    row = lax.broadcasted_iota(jnp.int32, (SUBLANES, BW), 0)
    ngrp = tm // SUBLANES

    def step(j, cr):
        jj = (ngrp - 1 - j) if reverse else j
        off = pl.multiple_of(jj * SUBLANES, SUBLANES)
        a = a_ref[pl.ds(off, SUBLANES), :]
        b = b_ref[pl.ds(off, SUBLANES), :]
        for sh in (1, 2, 4):
            if reverse:
                a_s = pltpu.roll(a, SUBLANES - sh, 0)
                b_s = pltpu.roll(b, SUBLANES - sh, 0)
                m = row < SUBLANES - sh
            else:
                a_s = pltpu.roll(a, sh, 0)
                b_s = pltpu.roll(b, sh, 0)
                m = row >= sh
            b = jnp.where(m, a * b_s + b, b)
            a = jnp.where(m, a * a_s, a)
        o = b + a * cr
        out_ref[pl.ds(off, SUBLANES), :] = o
        return o[0:1, :] if reverse else o[SUBLANES - 1:SUBLANES, :]

    return lax.fori_loop(0, ngrp, step, carry)


def _fwd_b(x, ya, wout_a, nw, win8, p8, gcat, jobs, *, tm, relay_step):
    s_len = x.shape[0]
    nt = s_len // tm

    def main(i, ins, outs, scr):
        x_ref, ya_ref, wouta_ref, nw_ref, win_ref, p8_ref, gcat_ref = ins
        x1_ref, zb_ref, hs_ref, h1_ref, yb_ref, xc_ref, a_ref, cc_ref, r_ref, ig_ref, m_ref = outs
        xbe_scr, b_scr, k_scr, carry_scr = scr

        @pl.when(i == 0)
        def _():
            xbe_scr[0:SUBLANES, :] = jnp.zeros((SUBLANES, BW), F32)
            carry_scr[...] = jnp.zeros_like(carry_scr)

        x1 = x_ref[...] + _dot(ya_ref[...], wouta_ref[...])
        x1_ref[...] = x1
        h = (x1 * _rms(x1) * nw_ref[...]).astype(BF16)
        h1_ref[...] = h
        for k in range(NDEV):
            zb_ref[:, k * CB:(k + 1) * CB] = _dot(h, win_ref[k])
        xbe_scr[SUBLANES:SUBLANES + tm, :] = zb_ref[:, :BW]
        for hh in range(BH):
            cs = slice(hh * HD, (hh + 1) * HD)
            xc = _conv(p8_ref, cs, xbe_scr[SUBLANES:SUBLANES + tm, cs], xbe_scr[7:7 + tm, cs],
                       xbe_scr[6:6 + tm, cs], xbe_scr[5:5 + tm, cs])
            r, ig, _, a, mult, rm = _gates(p8_ref, gcat_ref, hh, xc)
            ixc = ig * xc
            xc_ref[:, cs] = xc
            a_ref[:, cs] = a
            r_ref[:, cs] = r.astype(BF16)
            ig_ref[:, cs] = ig.astype(BF16)
            m_ref[:, cs] = mult.astype(BF16)
            b_scr[:, cs] = mult * ixc
            k_scr[:, cs] = ixc * (a * a * rm)
        xbe_scr[0:SUBLANES, :] = xbe_scr[tm:tm + SUBLANES, :]
        carry_scr[...] = _scan_rows(a_ref, b_scr, hs_ref, carry_scr[...], tm, False)
        for hh in range(BH):
            cs = slice(hh * HD, (hh + 1) * HD)
            gt = zb_ref[:, BW + hh * HD:BW + (hh + 1) * HD]
            hsv = hs_ref[:, cs]
            yb_ref[:, cs] = (hsv * (gt * _sigmoid(gt))).astype(BF16)
            cc_ref[:, cs] = (hsv - b_scr[:, cs]) - k_scr[:, cs]

    tile = lambda w: pl.BlockSpec((tm, w), lambda i: (i, 0))
    wide = lambda dt: _sds((s_len, BW), dt)
    return _call(
        main, jobs, name="fwd_b", grid=(nt,), relay_step=relay_step,
        ins=[x, ya, wout_a, nw, win8, p8, gcat], in_specs=[tile(D), tile(AW), _VMEM, _VMEM, _VMEM, _VMEM, _VMEM],
        out_shape=[_sds((s_len, D), F32), _sds((s_len, 2 * BW), F32), wide(F32), _sds((s_len, D), BF16), wide(BF16),
                   wide(F32), wide(F32), wide(F32), wide(BF16), wide(BF16), wide(BF16)],
        out_specs=[tile(D), tile(2 * BW), tile(BW), tile(D)] + [tile(BW)] * 7,
        scratch=[pltpu.VMEM((tm + SUBLANES, BW), F32), pltpu.VMEM((tm, BW), F32), pltpu.VMEM((tm, BW), F32),
                 pltpu.VMEM((1, BW), F32)])


def _head(x1, yb, wout, nfw, tgt, *, tm):
    s_len = x1.shape[0]

    def main(i, ins, outs, scr):
        x1_ref, yb_ref, wout_ref, nfw_ref, t_ref = ins
        dx2_ref, dx2b_ref, loss_ref, gnfw_ref = outs

        @pl.when(i == 0)
        def _():
            loss_ref[...] = jnp.zeros_like(loss_ref)
            gnfw_ref[...] = jnp.zeros_like(gnfw_ref)

        x2 = x1_ref[...] + _dot(yb_ref[...], wout_ref[...])
        rf = _rms(x2)
        xn = x2 * rf
        e = xn * nfw_ref[...] - t_ref[...]
        loss_ref[...] += (0.5 / D) * jnp.sum(jnp.sum(e * e, axis=-1, keepdims=True), axis=0, keepdims=True)
        dyf = e * (1.0 / D)
        gnfw_ref[...] += _rowsum(dyf * xn)
        dx2 = _rms_bwd(dyf, x2, rf, nfw_ref[...])
        dx2_ref[...] = dx2
        dx2b_ref[...] = dx2.astype(BF16)

    tile = lambda w: pl.BlockSpec((tm, w), lambda i: (i, 0))
    whole = lambda *s: pl.BlockSpec(s, lambda i: (0,) * len(s))
    (dx2, dx2b, loss, gnfw), _ = _call(
        main, [], name="head", grid=(s_len // tm,),
        ins=[x1, yb, wout, nfw, tgt], in_specs=[tile(D), tile(BW), _VMEM, _VMEM, tile(D)],
        out_shape=[_sds((s_len, D), F32), _sds((s_len, D), BF16), _sds((1, 1), F32), _sds((1, D), F32)],
        out_specs=[tile(D), tile(D), whole(1, 1), whole(1, D)], scratch=[])
    return dx2, dx2b, loss, gnfw


def _bwd_b(dx2, zb, hs, x1, saved, nw, win8, p8, gcat, wout, *, tm):
    s_len = x1.shape[0]
    nt = s_len // tm

    def main(i, ins, outs, scr):
        (dx2_ref, zb_ref, hs_ref, x1_ref, xc_ref, a_ref, cc_ref, r_ref, ig_ref, m_ref,
         nw_ref, win_ref, p8_ref, gcat_ref, wout_ref) = ins
        dx1_ref, dx1b_ref, dzb_ref, gp8_ref, gga_ref, ggx_ref, gnw_ref = outs
        ae_scr, an_scr, dhd_scr, dh_scr, dy_scr, dxce_scr, carry_scr, afirst_scr = scr

        @pl.when(i == 0)
        def _():
            gp8_ref[...] = jnp.zeros_like(gp8_ref)
            gga_ref[...] = jnp.zeros_like(gga_ref)
            ggx_ref[...] = jnp.zeros_like(ggx_ref)
            gnw_ref[...] = jnp.zeros_like(gnw_ref)
            dxce_scr[tm:tm + SUBLANES, :] = jnp.zeros((SUBLANES, BW), F32)
            carry_scr[...] = jnp.zeros_like(carry_scr)
            afirst_scr[...] = jnp.zeros_like(afirst_scr)

        dx2 = dx2_ref[...]
        dy_scr[...] = _dot_nt(dx2.astype(BF16), wout_ref[...])
        for hh in range(BH):
            cs = slice(hh * HD, (hh + 1) * HD)
            gs = slice(BW + hh * HD, BW + (hh + 1) * HD)
            gt = zb_ref[:, gs]
            sig = _sigmoid(gt)
            dy = dy_scr[:, cs]
            dhd_scr[:, cs] = dy * (gt * sig)
            dzb_ref[:, gs] = (dy * hs_ref[:, cs] * (sig * (1.0 + gt * (1.0 - sig)))).astype(BF16)

        ae_scr[0:tm, :] = a_ref[...]
        ae_scr[tm:tm + SUBLANES, :] = jnp.broadcast_to(afirst_scr[...], (SUBLANES, BW))
        an_scr[...] = ae_scr[1:1 + tm, :]
        afirst_scr[...] = ae_scr[0:1, :]
        carry_scr[...] = _scan_rows(an_scr, dhd_scr, dh_scr, carry_scr[...], tm, True)

        for hh in range(BH):
            cs = slice(hh * HD, (hh + 1) * HD)
            dh = dh_scr[:, cs]
            mult = m_ref[:, cs].astype(F32)
            ig = ig_ref[:, cs].astype(F32)
            r = r_ref[:, cs].astype(F32)
            xc = xc_ref[:, cs]
            lam = p8_ref[7:8, cs]
            sp = _softplus_neg(lam)
            dla = dh * cc_ref[:, cs]
            gp8_ref[7:8, cs] += _rowsum(dla * ((-RG_C) * r)) * (-_sigmoid(-lam))
            dpr = dla * ((-RG_C) * sp) * (r * (1.0 - r))
            dpi = dh * mult * xc * (ig * (1.0 - ig))
            gp8_ref[5:6, cs] += _rowsum(dpr)
            gp8_ref[6:7, cs] += _rowsum(dpi)
            dcat = jnp.concatenate([dpr, dpi], axis=1).astype(BF16)
            dxc = dh * mult * ig + _dot_nt(dcat, gcat_ref[hh])
            gg = _dot(xc.T.astype(BF16), dcat)
            gga_ref[hh] += gg[:, :HD]
            ggx_ref[hh] += gg[:, HD:]
            dxce_scr[0:tm, cs] = dxc
            gp8_ref[4:5, cs] += _rowsum(dxc)
        for hh in range(BH):
            cs = slice(hh * HD, (hh + 1) * HD)
            xb = zb_ref[:, cs]
            d0, d1 = dxce_scr[0:tm, cs], dxce_scr[1:1 + tm, cs]
            d2, d3 = dxce_scr[2:2 + tm, cs], dxce_scr[3:3 + tm, cs]
            dzb_ref[:, cs] = (p8_ref[3:4, cs] * d0 + p8_ref[2:3, cs] * d1 + p8_ref[1:2, cs] * d2
                              + p8_ref[0:1, cs] * d3).astype(BF16)
            gp8_ref[3:4, cs] += _rowsum(d0 * xb)
            gp8_ref[2:3, cs] += _rowsum(d1 * xb)
            gp8_ref[1:2, cs] += _rowsum(d2 * xb)
            gp8_ref[0:1, cs] += _rowsum(d3 * xb)
        dxce_scr[tm:tm + SUBLANES, :] = dxce_scr[0:SUBLANES, :]

        dh1 = jnp.zeros((tm, D), F32)
        for k in range(NDEV):
            dh1 = dh1 + _dot_nt(dzb_ref[:, k * CB:(k + 1) * CB], win_ref[k])
        x1 = x1_ref[...]
        r1 = _rms(x1)
        dx1 = dx2 + _rms_bwd(dh1, x1, r1, nw_ref[...])
        dx1_ref[...] = dx1
        dx1b_ref[...] = dx1.astype(BF16)
        gnw_ref[...] += _rowsum(dh1 * x1 * r1)

    tile = lambda w: pl.BlockSpec((tm, w), lambda i: (nt - 1 - i, 0))
    whole = lambda *s: pl.BlockSpec(s, lambda i: (0,) * len(s))
    full = lambda: pltpu.VMEM((tm, BW), F32)
    ext = lambda: pltpu.VMEM((tm + SUBLANES, BW), F32)
    out, _ = _call(
        main, [], name="bwd_b", grid=(nt,),
        ins=[dx2, zb, hs, x1, *saved, nw, win8, p8, gcat, wout],
        in_specs=[tile(D), tile(2 * BW), tile(BW), tile(D)] + [tile(BW)] * 6 + [_VMEM] * 5,
        out_shape=[_sds((s_len, D), F32), _sds((s_len, D), BF16), _sds((s_len, 2 * BW), BF16), _sds((SUBLANES, BW), F32),
                   _sds((BH, HD, HD), F32), _sds((BH, HD, HD), F32), _sds((1, D), F32)],
        out_specs=[tile(D), tile(D), tile(2 * BW), whole(SUBLANES, BW), whole(BH, HD, HD), whole(BH, HD, HD),
                   whole(1, D)],
        scratch=[ext(), full(), full(), full(), full(), ext(), pltpu.VMEM((1, BW), F32), pltpu.VMEM((1, BW), F32)])
    return out


def _transpose_into(dst_ref, src_ref, rows):
    s_len = src_ref.shape[0]
    for r0 in range(0, s_len, rows):
        dst_ref[:, r0:r0 + rows] = src_ref[r0:r0 + rows, :].astype(F32).T.astype(BF16)


def _wgrad(a, b, jobs, *, by_rows, per, name, relay_step=0):
    s_len, m = a.shape
    n = b.shape[1]
    r, cd = (m // NDEV, n) if by_rows else (m, n // NDEV)
    nsteps = NDEV // per
    at_rows = per * r if by_rows else m

    def main(i, ins, outs, scr):
        a_ref, b_ref = ins
        q_ref, acc_ref = outs
        at_scr, stage, mine, land, send_sems, recv_sems = scr
        x, y, c = _place()

        def to_sibling(pi):
            return pltpu.make_async_remote_copy(
                src_ref=stage.at[pi & 1], dst_ref=land.at[pi], send_sem=send_sems.at[pi], recv_sem=recv_sems.at[pi],
                device_id=(x, y, 1 - c), device_id_type=MESH)

        if by_rows:
            _transpose_into(at_scr, a_ref, 256)
        else:
            @pl.when(i == 0)
            def _():
                _transpose_into(at_scr, a_ref, 256)

        res = _dot(at_scr[...], b_ref[...]).astype(BF16)
        for k in range(per):
            blk = per * i + k
            pi, pc = blk >> 1, blk & 1
            val = res[k * r:(k + 1) * r, :] if by_rows else res

            @pl.when(pc != c)
            def _():
                @pl.when(pi >= 2)
                def _():
                    to_sibling(pi - 2).wait_send()

                stage[pi & 1] = val
                to_sibling(pi).start()

            @pl.when(pc == c)
            def _():
                mine[pi] = val

        @pl.when(i == nsteps - 1)
        def _():
            for p in range(4):
                to_sibling(p).wait_recv()
            to_sibling(2).wait_send()
            to_sibling(3).wait_send()
            _chip_sums(mine, land, q_ref, acc_ref, x, y)

    if by_rows:
        in_specs = [pl.BlockSpec((s_len, at_rows), lambda j: (0, j)), _VMEM]
    else:
        in_specs = [_VMEM, pl.BlockSpec((s_len, cd), lambda j: (0, j))]
    blk_vmem = lambda k: pltpu.VMEM((k, r, cd), BF16)
    (q, acc), job_out = _call(
        main, jobs, name=name, grid=(nsteps,), relay_step=relay_step, ins=[a, b], in_specs=in_specs,
        out_shape=[_sds((NCHIP_OTHER, r, cd), BF16), _sds((r, cd), F32)],
        out_specs=[pl.BlockSpec((NCHIP_OTHER, r, cd), lambda j: (0, 0, 0)), pl.BlockSpec((r, cd), lambda j: (0, 0))],
        scratch=[pltpu.VMEM((at_rows, s_len), BF16), blk_vmem(2), blk_vmem(4), blk_vmem(4),
                 pltpu.SemaphoreType.DMA((4,)), pltpu.SemaphoreType.DMA((4,))])
    return q, acc, job_out


def _adam_math(w, g, m, v):
    m = B1 * m + (1.0 - B1) * g
    v = B2 * v + (1.0 - B2) * (g * g)
    m_hat = m / (1.0 - B1 ** STEP)
    v_hat = v / (1.0 - B2 ** STEP)
    delta = (-LR) * (m_hat / (jnp.sqrt(v_hat) + ADAM_EPS) + WD * w)
    return delta, m, v


def _adam_big(w, acc, land, m, v, name):
    r, cd = w.shape
    rb = 256 if r % 256 == 0 else r
    nland = land.shape[0]

    def body(w_ref, acc_ref, land_ref, m_ref, v_ref, g_ref, d_ref, mo_ref, vo_ref):
        g = acc_ref[...]
        for j in range(nland):
            g = g + land_ref[j].astype(F32)
        g_ref[...] = g
        d_ref[...], mo_ref[...], vo_ref[...] = _adam_math(w_ref[...], g, m_ref[...], v_ref[...])

    blk = pl.BlockSpec((rb, cd), lambda i: (i, 0))
    blk3 = pl.BlockSpec((nland, rb, cd), lambda i: (0, i, 0))
    return pl.pallas_call(
        body, name=name, grid=(r // rb,), in_specs=[blk, blk, blk3, blk, blk], out_specs=[blk] * 4,
        out_shape=[_sds((r, cd), F32)] * 4,
        compiler_params=_params(dimension_semantics=("arbitrary",)),
    )(w, acc, land, m, v)


def _adam_small(groups):
    n = len(groups)

    def body(*refs):
        ins, outs = refs[:4 * n], refs[4 * n:]
        for k in range(n):
            w_ref, g_ref, m_ref, v_ref = ins[4 * k:4 * k + 4]
            d, mo, vo = _adam_math(w_ref[...], g_ref[...], m_ref[...], v_ref[...])
            outs[3 * k][...] = d
            outs[3 * k + 1][...] = mo
            outs[3 * k + 2][...] = vo

    flat = [a for grp in groups for a in grp]
    shapes = [_sds(grp[0].shape, F32) for grp in groups for _ in range(3)]
    res = pl.pallas_call(
        body, name="adam_small", in_specs=[_VMEM] * (4 * n), out_specs=[_VMEM] * (3 * n), out_shape=shapes,
        compiler_params=_params(),
    )(*flat)
    return [tuple(res[3 * k:3 * k + 3]) for k in range(n)]


TM_FWD_A_TOP = 256
RELAY_STEP_FWD_A_TOP = 3
TM_FWD_A = 256
RELAY_STEP_FWD_A = 4
RELAY_STEP_FWD_B = 2
TM_BWD_A = 256
RELAY_STEP_BWD_A = 3
TM_BWD_A_IN = 256
RELAY_STEP_BWD_A_IN = 4
TM_FWD_B = 256
TM_HEAD = 512
TM_BWD_B = 256


def _pack(parts, rows):
    flat = jnp.concatenate([p.reshape(-1) for p in parts])
    return jnp.pad(flat, (0, NDEV * rows * LANES - flat.shape[0])).reshape(NDEV, rows, LANES)


def _unpack(packed, shapes):
    flat, out, off = packed.reshape(-1), [], 0
    for s in shapes:
        size = 1
        for d in s:
            size *= d
        out.append(flat[off:off + size].reshape(s))
        off += size
    return out


def kernel(x, norm_w, a_w_in, a_ln_w, a_ln_b, a_w_s, a_b_s, a_w_out, b_w_in, b_conv_w, b_conv_b, b_gate_a_w, b_gate_a_b, b_gate_x_w, b_gate_x_b, b_lambda, b_w_out, norm_f_w, loss_target, m_norm_w, m_a_w_in, m_a_ln_w, m_a_ln_b, m_a_w_s, m_a_b_s, m_a_w_out, m_b_w_in, m_b_conv_w, m_b_conv_b, m_b_gate_a_w, m_b_gate_a_b, m_b_gate_x_w, m_b_gate_x_b, m_b_lambda, m_b_w_out, m_norm_f_w, v_norm_w, v_a_w_in, v_a_ln_w, v_a_ln_b, v_a_w_s, v_a_b_s, v_a_w_out, v_b_w_in, v_b_conv_w, v_b_conv_b, v_b_gate_a_w, v_b_gate_a_b, v_b_gate_x_w, v_b_gate_x_b, v_b_lambda, v_b_w_out, v_norm_f_w):
    me = 4 * lax.axis_index("x") + 2 * lax.axis_index("y") + lax.axis_index("c")
    xs, tgt = x[0], loss_target[0]
    nw0, nw1, nfw = norm_w[0:1], norm_w[1:2], norm_f_w.reshape(1, D)
    w_s, bst = a_w_s[0], a_b_s[0].T
    gcat = jnp.concatenate([b_gate_a_w[0], b_gate_x_w[0]], axis=-1).astype(BF16)

    p8_shard = jnp.concatenate([b_conv_w[0], b_conv_b, b_gate_a_b, b_gate_x_b, b_lambda], axis=0)
    win_a = a_w_in[0].astype(BF16)
    ((wtop_a8, p8_all),) = _comm_only([_Gather([win_a[:D // 2], p8_shard])], "gather_first")
    p8 = jnp.transpose(p8_all, (1, 0, 2)).reshape(SUBLANES, BW)

    (h0, zp), ((wbot_a8,),) = _fwd_a_top(xs, nw0, wtop_a8, [_Gather([win_a[D // 2:]])], tm=TM_FWD_A_TOP,
                                       relay_step=RELAY_STEP_FWD_A_TOP)
    (z, ya), ((wout_a8, win_b8),) = _fwd_a(
        h0, zp, wbot_a8, a_ln_w, a_ln_b, w_s, bst, [_Gather([a_w_out[0].astype(BF16), b_w_in[0].astype(BF16)])],
        tm=TM_FWD_A, relay_step=RELAY_STEP_FWD_A)
    wout_a = wout_a8.reshape(AW, D)
    (x1, zb, hs, h1, yb, *saved_b), ((wout_b8,),) = _fwd_b(
        xs, ya, wout_a, nw1, win_b8, p8, gcat, [_Gather([b_w_out[0].astype(BF16)])],
        tm=TM_FWD_B, relay_step=RELAY_STEP_FWD_B)
    wout_b = wout_b8.reshape(BW, D)
    dx2, dx2b, loss, g_nfw = _head(x1, yb, wout_b, nfw, tgt, tm=TM_HEAD)

    dx1, dx1b, dzb, g_p8, g_ga, g_gx, g_nw1 = _bwd_b(dx2, zb, hs, x1, saved_b, nw1, win_b8, p8, gcat, wout_b,
                                                     tm=TM_BWD_B)
    q_wout_b, acc_wout_b, _ = _wgrad(yb, dx2b, [], by_rows=True, per=2, name="wgrad_b_out")
    shapes_b = [(1, D), (1, D), (SUBLANES, BW), (1, 1)]
    pack_b = _pack([g_nfw, g_nw1, g_p8, loss], 16)
    small_b = _InChip([g_ga.reshape(NDEV, -1, HD), g_gx.reshape(NDEV, -1, HD), pack_b])
    q_win_b, acc_win_b, (sm_b, (l_wout_b,)) = _wgrad(h1, dzb, [small_b, _Exchange([q_wout_b])], by_rows=False, per=1,
                                                      name="wgrad_b_in")
    qs_b, accs_b = sm_b[:3], sm_b[3:]

    (dz, g_lnw, g_lnb, g_ws, g_bst), (lands_b, (l_win_b,)) = _bwd_a(
        dx1b, z, a_ln_w, a_ln_b, w_s, bst, wout_a, [_Exchange(qs_b), _ExchangeVia(q_win_b)],
        tm=TM_BWD_A, relay_step=RELAY_STEP_BWD_A)
    shapes_a = [(1, AW), (1, AW), (CH, G)]
    pack_a = _pack([g_lnw, g_lnb, g_bst], 8)
    q_wout_a, acc_wout_a, (red_b, sm_a) = _wgrad(
        ya, dx1b, [_SumGather(accs_b, lands_b), _InChip([g_ws, pack_a])], by_rows=True, per=2,
        name="wgrad_a_out", relay_step=1)
    qs_a, accs_a = [q_wout_a, *sm_a[:2]], [acc_wout_a, *sm_a[2:]]
    q_win_a, acc_win_a, (lands_a,) = _wgrad(h0, dz, [_Exchange(qs_a)], by_rows=False, per=1, name="wgrad_a_in")
    (gx, g_nw0), (red_a, (l_win_a,)) = _bwd_a_in(
        dz, dx1, xs, nw0, wtop_a8, wbot_a8, [_SumGather(accs_a[1:], lands_a[1:]), _ExchangeVia(q_win_a)],
        tm=TM_BWD_A_IN, relay_step=RELAY_STEP_BWD_A_IN)
    g_nw0 = _allreduce_direct(g_nw0, "allreduce_norm_w0")

    r_ga, r_gx, r_pack_b = red_b
    r_nfw, r_nw1, r_p8, loss = _unpack(r_pack_b, shapes_b)
    r_ws, r_pack_a = red_a
    r_lnw, r_lnb, r_bst = _unpack(r_pack_a, shapes_a)
    g_p8 = lax.dynamic_slice_in_dim(r_p8, me * (BW // NDEV), BW // NDEV, axis=1)
    loss = loss[0, 0]

    weights = dict(norm_w=norm_w, a_w_in=a_w_in, a_ln_w=a_ln_w, a_ln_b=a_ln_b, a_w_s=a_w_s, a_b_s=a_b_s, a_w_out=a_w_out,
                   b_w_in=b_w_in, b_conv_w=b_conv_w, b_conv_b=b_conv_b, b_gate_a_w=b_gate_a_w, b_gate_a_b=b_gate_a_b,
                   b_gate_x_w=b_gate_x_w, b_gate_x_b=b_gate_x_b, b_lambda=b_lambda, b_w_out=b_w_out, norm_f_w=norm_f_w)
    mom1 = dict(norm_w=m_norm_w, a_w_in=m_a_w_in, a_ln_w=m_a_ln_w, a_ln_b=m_a_ln_b, a_w_s=m_a_w_s, a_b_s=m_a_b_s,
                a_w_out=m_a_w_out, b_w_in=m_b_w_in, b_conv_w=m_b_conv_w, b_conv_b=m_b_conv_b, b_gate_a_w=m_b_gate_a_w,
                b_gate_a_b=m_b_gate_a_b, b_gate_x_w=m_b_gate_x_w, b_gate_x_b=m_b_gate_x_b, b_lambda=m_b_lambda,
                b_w_out=m_b_w_out, norm_f_w=m_norm_f_w)
    mom2 = dict(norm_w=v_norm_w, a_w_in=v_a_w_in, a_ln_w=v_a_ln_w, a_ln_b=v_a_ln_b, a_w_s=v_a_w_s, a_b_s=v_a_b_s,
                a_w_out=v_a_w_out, b_w_in=v_b_w_in, b_conv_w=v_b_conv_w, b_conv_b=v_b_conv_b, b_gate_a_w=v_b_gate_a_w,
                b_gate_a_b=v_b_gate_a_b, b_gate_x_w=v_b_gate_x_w, b_gate_x_b=v_b_gate_x_b, b_lambda=v_b_lambda,
                b_w_out=v_b_w_out, norm_f_w=v_norm_f_w)
    names = list(weights)

    def as2d(a):
        return a.reshape(-1, a.shape[-1])

    upd, grads = {}, {}
    for k, acc, land in (("a_w_in", acc_win_a, l_win_a), ("a_w_out", accs_a[0], lands_a[0]),
                         ("b_w_in", acc_win_b, l_win_b), ("b_w_out", acc_wout_b, l_wout_b)):
        g, d, mo, vo = _adam_big(as2d(weights[k]), acc, land, as2d(mom1[k]), as2d(mom2[k]), "adam_" + k)
        grads[k] = g[None]
        upd[k] = (d, mo, vo)
    grads.update(
        norm_w=jnp.concatenate([g_nw0, r_nw1], axis=0), a_ln_w=r_lnw, a_ln_b=r_lnb,
        a_w_s=r_ws.reshape(1, G, CH, CH), a_b_s=r_bst.T[None],
        b_conv_w=g_p8[None, 0:4], b_conv_b=g_p8[4:5], b_gate_a_w=r_ga.reshape(1, BH, HD, HD), b_gate_a_b=g_p8[5:6],
        b_gate_x_w=r_gx.reshape(1, BH, HD, HD), b_gate_x_b=g_p8[6:7], b_lambda=g_p8[7:8], norm_f_w=r_nfw.reshape(D))
    small_names = [k for k in names if k not in upd]
    res = _adam_small([(as2d(weights[k]), as2d(grads[k]), as2d(mom1[k]), as2d(mom2[k])) for k in small_names])
    for k, r3 in zip(small_names, res):
        upd[k] = r3
    deltas = [upd[k][0].reshape(weights[k].shape) for k in names]
    new_m = [upd[k][1].reshape(weights[k].shape) for k in names]
    new_v = [upd[k][2].reshape(weights[k].shape) for k in names]
    return (loss, gx[None], *[grads[k] for k in names], *deltas, *new_m, *new_v)
```

```python
import jax
import jax.numpy as jnp
from jax import lax
from jax.experimental import pallas as pl
from jax.experimental.pallas import tpu as pltpu

F32 = jnp.float32
BF16 = jnp.bfloat16
MESH = pl.DeviceIdType.MESH

NDEV = 8
NCHIP_OTHER = 3
D = 1024
AW = 2048
G = 8
GD = AW // G
CH = 128
BW = 1536
BH = 12
HD = BW // BH
CA = 3 * AW // NDEV
CB = 2 * BW // NDEV
RMS_EPS = 1e-6
LN_EPS = 1e-5
RG_C = 8.0
LR, B1, B2, ADAM_EPS, WD, STEP = 0.001, 0.9, 0.999, 1e-08, 0.01, 10
V7X_VMEM_BYTES = 64 * 1024 * 1024
VMEM_LIMIT = V7X_VMEM_BYTES - 8 * 1024 * 1024
SUBLANES = 8
LANES = 128
BF16_ROWS = 16
GELU_C = 0.7978845608028654
GELU_K = 0.044715

_VMEM = pl.BlockSpec(memory_space=pltpu.VMEM)
_HBM = pl.BlockSpec(memory_space=pltpu.HBM)


def _sds(shape, dtype):
    return jax.ShapeDtypeStruct(tuple(shape), dtype)


def _params(**kw):
    return pltpu.CompilerParams(vmem_limit_bytes=VMEM_LIMIT, **kw)


def _gelu_t(z):
    t = jnp.tanh(GELU_C * (z + GELU_K * (z * z * z)))
    return 0.5 * z * (1.0 + t), t


def _dgelu(z, t):
    return 0.5 * (1.0 + t) + 0.5 * z * (1.0 - t * t) * (GELU_C * (1.0 + 3.0 * GELU_K * z * z))


def _sigmoid(v):
    return 0.5 * jnp.tanh(0.5 * v) + 0.5


def _softplus_neg(lam):
    return jnp.maximum(-lam, 0.0) + jnp.log1p(jnp.exp(-jnp.abs(lam)))


def _dot(a, b):
    return jnp.dot(a, b, preferred_element_type=F32)


def _dot_nt(a, b):
    return lax.dot_general(a, b, (((1,), (1,)), ((), ())), preferred_element_type=F32)


def _rowsum(v):
    return jnp.sum(v, axis=0, keepdims=True)


def _causal_mask():
    r = lax.broadcasted_iota(jnp.int32, (CH, CH), 0)
    c = lax.broadcasted_iota(jnp.int32, (CH, CH), 1)
    return r >= c


def _rms(x):
    return lax.rsqrt(jnp.mean(x * x, axis=-1, keepdims=True) + RMS_EPS)


def _rms_bwd(dh, x, r, nw):
    gy = dh * nw
    return r * gy - x * (r * r * r) * jnp.mean(gy * x, axis=-1, keepdims=True)


def _place():
    return lax.axis_index("x"), lax.axis_index("y"), lax.axis_index("c")


def _other_chips(x, y):
    return [(1 - x, y), (x, 1 - y), (1 - x, 1 - y)]


GATHER_SLOTS = 10


def _gather_ops(ins, outs, send_sems, recv_sems, local_sems):
    n = len(ins)
    x, y, c = _place()
    sibling = (x, y, 1 - c)
    xn, yn, dg = _other_chips(x, y)
    split = [ins[i].shape[0] % (2 * BF16_ROWS) == 0 for i in range(n)]

    def blk(chip, core):
        return 4 * chip[0] + 2 * chip[1] + core

    me = blk((x, y), c)

    def part(ref, i, half):
        if half is None:
            return ref
        h = ins[i].shape[0] // 2
        return ref.at[pl.ds(half * h, h)]

    def copy(i, k, block, to, half=None, src=None):
        dst = part(outs[i].at[block], i, half)
        return pltpu.make_async_remote_copy(
            src_ref=dst if src is None else part(src, i, half), dst_ref=dst,
            send_sem=send_sems.at[k, i], recv_sem=recv_sems.at[k, i], device_id=to, device_id_type=MESH)

    def first_copies():
        mine = [pltpu.make_async_copy(ins[i], outs[i].at[me], local_sems.at[i]) for i in range(n)]
        first = []
        for i in range(n):
            first.append(copy(i, 0, me, sibling, src=ins[i]))
            if split[i]:
                first.append(copy(i, 1, me, (*xn, c), 0, ins[i]))
                first.append(copy(i, 3, me, (*yn, c), 1, ins[i]))
                first.append(copy(i, 2, me, (*xn, c), 1, ins[i]))
                first.append(copy(i, 4, me, (*yn, c), 0, ins[i]))
            else:
                first.append(copy(i, 1, me, (*xn, c), None, ins[i]))
                first.append(copy(i, 3, me, (*yn, c), None, ins[i]))
                first.append(copy(i, 5, me, (*dg, c), None, ins[i]))
        return mine, first

    def onward():
        out = []
        for i in range(n):
            if split[i]:
                out.append(copy(i, 5, blk(xn, c), (*yn, c), 0))
                out.append(copy(i, 6, blk(yn, c), (*xn, c), 1))
        return out

    def start():
        mine, first = first_copies()
        for cp in mine + first:
            cp.start()

    def relay():
        sends = onward()
        for i in range(n):
            if split[i]:
                copy(i, 1, blk(xn, c), sibling, 0).wait_recv()
                sends.pop(0).start()
                copy(i, 3, blk(yn, c), sibling, 1).wait_recv()
                sends.pop(0).start()

    def finish():
        mine, first = first_copies()
        passed = []

        def pass_on(i, j, chip):
            fwd = copy(i, 7 + j, blk(chip, c), sibling)
            fwd.start()
            passed.append(fwd)

        for i in range(n):
            if split[i]:
                copy(i, 2, blk(xn, c), sibling, 1).wait_recv()
                pass_on(i, 0, xn)
                copy(i, 4, blk(yn, c), sibling, 0).wait_recv()
                pass_on(i, 1, yn)
                copy(i, 5, blk(dg, c), sibling, 0).wait_recv()
                copy(i, 6, blk(dg, c), sibling, 1).wait_recv()
                pass_on(i, 2, dg)
            else:
                copy(i, 1, blk(xn, c), sibling).wait_recv()
                pass_on(i, 0, xn)
                copy(i, 3, blk(yn, c), sibling).wait_recv()
                pass_on(i, 1, yn)
                copy(i, 5, blk(dg, c), sibling).wait_recv()
                pass_on(i, 2, dg)
        for i in range(n):
            copy(i, 0, blk((x, y), 1 - c), sibling).wait_recv()
            for j, chip in enumerate((xn, yn, dg)):
                copy(i, 7 + j, blk(chip, 1 - c), sibling).wait_recv()
        for cp in first + passed + onward():
            cp.wait_send()
        for cp in mine:
            cp.wait()

    return start, relay, finish


def _gather_sems(n):
    return [pltpu.SemaphoreType.DMA((GATHER_SLOTS, n)), pltpu.SemaphoreType.DMA((GATHER_SLOTS, n)),
            pltpu.SemaphoreType.DMA((n,))]


class _Gather:
    def __init__(self, shards):
        n = len(shards)
        self.ins, self.in_specs = list(shards), [_HBM] * n
        self.out_shape = [_sds((NDEV,) + s.shape, s.dtype) for s in shards]
        self.out_specs = [_HBM] * n
        self.scratch = _gather_sems(n)

    def ops(self, ins, outs, scr):
        return _gather_ops(ins, outs, *scr)


class _Exchange:
    def __init__(self, qs):
        n = len(qs)
        self.ins, self.in_specs = list(qs), [_HBM] * n
        self.out_shape = [_sds(q.shape, q.dtype) for q in qs]
        self.out_specs = [_HBM] * n
        self.scratch = [pltpu.SemaphoreType.DMA((NCHIP_OTHER, n)), pltpu.SemaphoreType.DMA((NCHIP_OTHER, n))]

    def ops(self, ins, outs, scr):
        send_sems, recv_sems = scr
        n = len(ins)
        x, y, c = _place()
        chips = _other_chips(x, y)

        def copies():
            return [pltpu.make_async_remote_copy(
                src_ref=ins[i].at[j], dst_ref=outs[i].at[j], send_sem=send_sems.at[j, i],
                recv_sem=recv_sems.at[j, i], device_id=(*chips[j], c), device_id_type=MESH)
                for i in range(n) for j in range(NCHIP_OTHER)]

        def start():
            for cp in copies():
                cp.start()

        def finish():
            cps = copies()
            for cp in cps:
                cp.wait_recv()
            for cp in cps:
                cp.wait_send()

        return start, lambda: None, finish


class _ExchangeVia:
    def __init__(self, q):
        _, r, cd = q.shape
        half = (2, r // 2, cd)
        self.ins, self.in_specs = [q], [_HBM]
        self.out_shape, self.out_specs = [_sds((2, r, cd), q.dtype)], [_HBM]
        self.scratch = [pltpu.VMEM(half, q.dtype), pltpu.VMEM(half, q.dtype), pltpu.VMEM(half, q.dtype),
                        pltpu.SemaphoreType.DMA((6,)), pltpu.SemaphoreType.DMA((6,)), pltpu.SemaphoreType.DMA((2,))]

    def ops(self, ins, outs, scr):
        (q,), (land,) = ins, outs
        relayed, own, comb, send_sems, recv_sems, local_sems = scr
        h = q.shape[1] // 2
        x, y, c = _place()
        xn, yn, _ = _other_chips(x, y)
        h0, h1 = pl.ds(0, h), pl.ds(h, h)

        def remote(k, src, dst, chip):
            return pltpu.make_async_remote_copy(src_ref=src, dst_ref=dst, send_sem=send_sems.at[k],
                                                recv_sem=recv_sems.at[k], device_id=(*chip, c), device_id_type=MESH)

        def via():
            return [remote(2, q.at[2, h0], relayed.at[0], xn), remote(3, q.at[2, h1], relayed.at[1], yn)]

        def direct():
            return [remote(0, q.at[0, h0], land.at[0, h0], xn), remote(1, q.at[1, h1], land.at[1, h1], yn)]

        def second():
            return [remote(4, comb.at[0], land.at[1, h0], yn), remote(5, comb.at[1], land.at[0, h1], xn)]

        def mine():
            return [pltpu.make_async_copy(q.at[1, h0], own.at[0], local_sems.at[0]),
                    pltpu.make_async_copy(q.at[0, h1], own.at[1], local_sems.at[1])]

        def start():
            for cp in via() + direct() + mine():
                cp.start()

        def relay():
            arrived, loaded, onward = via(), mine(), second()
            for k in range(2):
                arrived[k].wait_recv()
                loaded[k].wait()
                comb[k] = (own[k].astype(F32) + relayed[k].astype(F32)).astype(comb.dtype)
                onward[k].start()

        def finish():
            landing = direct() + second()
            for cp in landing:
                cp.wait_recv()
            for cp in via() + landing:
                cp.wait_send()

        return start, relay, finish


class _SumGather:
    def __init__(self, accs, lands):
        n = len(accs)
        self.n = n
        self.ins, self.in_specs = list(accs) + list(lands), [_VMEM] * (2 * n)
        self.out_shape = [_sds((NDEV,) + a.shape, a.dtype) for a in accs]
        self.out_specs = [_HBM] * n
        self.scratch = [pltpu.VMEM(a.shape, a.dtype) for a in accs] + _gather_sems(n)

    def ops(self, ins, outs, scr):
        n = self.n
        accs, lands, mine = ins[:n], ins[n:], scr[:n]
        g_start, relay, finish = _gather_ops(mine, outs, *scr[n:])

        def start():
            for i in range(n):
                mine[i][...] = accs[i][...] + lands[i][0] + lands[i][1] + lands[i][2]
            g_start()

        return start, relay, finish


def _call(main, jobs, *, name, grid, ins, in_specs, out_shape, out_specs, scratch, relay_step=0):
    nsteps = grid[0] if grid else 1
    n_in, n_out, n_scr = len(ins), len(out_shape), len(scratch)

    def body(*refs):
        pos = [0]

        def take(k):
            r = refs[pos[0]:pos[0] + k]
            pos[0] += k
            return r

        m_in = take(n_in)
        j_in = [take(len(j.ins)) for j in jobs]
        m_out = take(n_out)
        j_out = [take(len(j.out_shape)) for j in jobs]
        m_scr = take(n_scr)
        j_scr = [take(len(j.scratch)) for j in jobs]
        ops = [j.ops(a, b, s) for j, a, b, s in zip(jobs, j_in, j_out, j_scr)]
        i = pl.program_id(0) if grid else 0
        if not grid:
            for o in ops:
                o[0]()
            main(i, m_in, m_out, m_scr)
            for o in ops:
                o[1]()
            for o in ops:
                o[2]()
            return

        if ops:
            @pl.when(i == 0)
            def _():
                for o in ops:
                    o[0]()

            @pl.when(i == min(max(relay_step, 1), nsteps - 1))
            def _():
                for o in ops:
                    o[1]()

        main(i, m_in, m_out, m_scr)

        if ops:
            @pl.when(i == nsteps - 1)
            def _():
                for o in ops:
                    o[2]()

    extra = dict(dimension_semantics=("arbitrary",)) if grid else {}
    res = pl.pallas_call(
        body, name=name, grid=grid,
        in_specs=list(in_specs) + [s for j in jobs for s in j.in_specs],
        out_specs=list(out_specs) + [s for j in jobs for s in j.out_specs],
        out_shape=list(out_shape) + [s for j in jobs for s in j.out_shape],
        scratch_shapes=list(scratch) + [s for j in jobs for s in j.scratch],
        compiler_params=_params(**extra),
    )(*ins, *[a for j in jobs for a in j.ins])
    main_out, rest, job_out = res[:n_out], res[n_out:], []
    for j in jobs:
        k = len(j.out_shape)
        job_out.append(rest[:k])
        rest = rest[k:]
    return main_out, job_out


def _comm_only(jobs, name):
    _, job_out = _call(lambda i, a, b, s: None, jobs, name=name, grid=(), ins=[], in_specs=[], out_shape=[],
                       out_specs=[], scratch=[])
    return job_out


class _InChip:
    def __init__(self, ps):
        n = len(ps)
        self.n = n
        blk = [p.shape[1:] for p in ps]
        self.ins, self.in_specs = list(ps), [_HBM] * n
        self.out_shape = [_sds((NCHIP_OTHER,) + b, p.dtype) for b, p in zip(blk, ps)] + [_sds(b, F32) for b in blk]
        self.out_specs = [_VMEM] * (2 * n)
        self.scratch = ([pltpu.VMEM((4,) + b, p.dtype) for b, p in zip(blk, ps)] * 2
                        + [pltpu.SemaphoreType.DMA((4, n))] * 3)

    def ops(self, ins, outs, scr):
        n = self.n
        q_refs, acc_refs = outs[:n], outs[n:]
        mines, lands = scr[:n], scr[n:2 * n]
        send_sems, recv_sems, local_sems = scr[2 * n:]
        x, y, c = _place()
        sibling = (x, y, 1 - c)

        def copies():
            out = []
            for i in range(n):
                for pi in range(4):
                    loc = pltpu.make_async_copy(ins[i].at[2 * pi + c], mines[i].at[pi], local_sems.at[pi, i])
                    cp = pltpu.make_async_remote_copy(
                        src_ref=ins[i].at[2 * pi + (1 - c)], dst_ref=lands[i].at[pi],
                        send_sem=send_sems.at[pi, i], recv_sem=recv_sems.at[pi, i],
                        device_id=sibling, device_id_type=MESH)
                    out.append((loc, cp))
            return out

        def start():
            for loc, cp in copies():
                loc.start()
                cp.start()

        def finish():
            pairs = copies()
            for loc, cp in pairs:
                loc.wait()
                cp.wait_recv()
            for i in range(n):
                _chip_sums(mines[i], lands[i], q_refs[i], acc_refs[i], x, y)
            for _, cp in pairs:
                cp.wait_send()

        return start, lambda: None, finish


def _chip_sums(mine, land, q_ref, acc_ref, x, y):
    for j, (qx, qy) in enumerate(_other_chips(x, y)):
        qi = 2 * qx + qy
        q_ref[j] = (mine[qi].astype(F32) + land[qi].astype(F32)).astype(q_ref.dtype)
    mi = 2 * x + y
    acc_ref[...] = mine[mi].astype(F32) + land[mi].astype(F32)


def _allreduce_direct(v, name):
    def body(v_ref, o_ref, buf, send_sems, recv_sems):
        x, y, c = _place()
        me = 4 * x + 2 * y + c
        buf[me] = v_ref[...]
        cps = []
        for k in range(1, NDEV):
            fx, fy, fc = (k >> 2) & 1, (k >> 1) & 1, k & 1
            peer = ((1 - x) if fx else x, (1 - y) if fy else y, (1 - c) if fc else c)
            cps.append((peer, pltpu.make_async_remote_copy(
                src_ref=buf.at[me], dst_ref=buf.at[me], send_sem=send_sems.at[k - 1], recv_sem=recv_sems.at[k - 1],
                device_id=peer, device_id_type=MESH)))
        for _, cp in cps:
            cp.start()
        for k, (peer, _) in enumerate(cps):
            theirs = 4 * peer[0] + 2 * peer[1] + peer[2]
            pltpu.make_async_remote_copy(
                src_ref=buf.at[theirs], dst_ref=buf.at[theirs], send_sem=send_sems.at[k], recv_sem=recv_sems.at[k],
                device_id=peer, device_id_type=MESH).wait_recv()
        acc = buf[0]
        for j in range(1, NDEV):
            acc = acc + buf[j]
        o_ref[...] = acc
        for _, cp in cps:
            cp.wait_send()

    return pl.pallas_call(
        body, name=name, in_specs=[_VMEM], out_specs=_VMEM, out_shape=_sds(v.shape, v.dtype),
        scratch_shapes=[pltpu.VMEM((NDEV,) + v.shape, v.dtype), pltpu.SemaphoreType.DMA((NDEV - 1,)),
                        pltpu.SemaphoreType.DMA((NDEV - 1,))],
        compiler_params=_params(),
    )(v)


def _fwd_a(x, nw, win8, lnw, lnb, ws, bst, jobs, *, tm, relay_step):
    s_len = x.shape[0]
    nt = s_len // tm
    nch = tm // CH

    def main(i, ins, outs, scr):
        x_ref, nw_ref, win_ref, lnw_ref, lnb_ref, ws_ref, bst_ref = ins
        z_ref, h_ref, y_ref = outs
        wc_scr, gv_scr = scr

        @pl.when(i == 0)
        def _():
            m = _causal_mask()
            for g in range(G):
                wc_scr[g] = jnp.where(m, ws_ref[g], 0.0).astype(BF16)

        x = x_ref[...]
        h = (x * _rms(x) * nw_ref[...]).astype(BF16)
        h_ref[...] = h
        for k in range(NDEV):
            z_ref[:, k * CA:(k + 1) * CA] = _dot(h, win_ref[k])

        ssum = jnp.zeros((tm, 1), F32)
        for g in range(G):
            gv = _gelu_t(z_ref[:, AW + g * GD:AW + (g + 1) * GD])[0]
            gv_scr[:, g * GD:(g + 1) * GD] = gv
            ssum = ssum + jnp.sum(gv, axis=-1, keepdims=True)
        mu = ssum * (1.0 / AW)
        vsum = jnp.zeros((tm, 1), F32)
        for g in range(G):
            dlt = gv_scr[:, g * GD:(g + 1) * GD] - mu
            vsum = vsum + jnp.sum(dlt * dlt, axis=-1, keepdims=True)
        rstd = lax.rsqrt(vsum * (1.0 / AW) + LN_EPS)

        for g in range(G):
            cs = slice(g * GD, (g + 1) * GD)
            v = (gv_scr[:, cs] - mu) * rstd * lnw_ref[:, cs] + lnb_ref[:, cs]
            vb = v.astype(BF16)
            u = _gelu_t(z_ref[:, cs])[0]
            zg = z_ref[:, 2 * AW + g * GD:2 * AW + (g + 1) * GD]
            sg = zg * _sigmoid(zg)
            for n in range(nch):
                rs = slice(n * CH, (n + 1) * CH)
                s = _dot(wc_scr[g], vb[rs, :]) + bst_ref[:, g:g + 1]
                y_ref[rs, cs] = (u[rs, :] * s * sg[rs, :]).astype(BF16)

    tile = lambda w: pl.BlockSpec((tm, w), lambda i: (i, 0))
    return _call(
        main, jobs, name="fwd_a", grid=(nt,), relay_step=relay_step,
        ins=[x, nw, win8, lnw, lnb, ws, bst], in_specs=[tile(D), _VMEM, _VMEM, _VMEM, _VMEM, _VMEM, _VMEM],
        out_shape=[_sds((s_len, 3 * AW), F32), _sds((s_len, D), BF16), _sds((s_len, AW), BF16)],
        out_specs=[tile(3 * AW), tile(D), tile(AW)],
        scratch=[pltpu.VMEM((G, CH, CH), BF16), pltpu.VMEM((tm, AW), F32)])


def _bwd_a(dx1, z, lnw, lnb, ws, bst, wout, jobs, *, tm, relay_step):
    s_len = dx1.shape[0]
    nt = s_len // tm
    nch = tm // CH

    def main(i, ins, outs, scr):
        dx1_ref, z_ref, lnw_ref, lnb_ref, ws_ref, bst_ref, wout_ref = ins
        dz_ref, glnw_ref, glnb_ref, gws_ref, gbst_ref = outs
        wc_scr, wct_scr, vh_scr, dgv_scr, dy_scr, dv_scr, gbs_acc, gwc_acc = scr

        @pl.when(i == 0)
        def _():
            m = _causal_mask()
            for g in range(G):
                wm = jnp.where(m, ws_ref[g], 0.0)
                wc_scr[g] = wm.astype(BF16)
                wct_scr[g] = wm.T.astype(BF16)
            glnw_ref[...] = jnp.zeros_like(glnw_ref)
            glnb_ref[...] = jnp.zeros_like(glnb_ref)
            gbs_acc[...] = jnp.zeros_like(gbs_acc)
            gwc_acc[...] = jnp.zeros_like(gwc_acc)

        dy_scr[...] = _dot_nt(dx1_ref[...], wout_ref[...])

        ssum = jnp.zeros((tm, 1), F32)
        for g in range(G):
            cs = slice(g * GD, (g + 1) * GD)
            zv = z_ref[:, AW + g * GD:AW + (g + 1) * GD]
            gv, t = _gelu_t(zv)
            vh_scr[:, cs] = gv
            dgv_scr[:, cs] = _dgelu(zv, t)
            ssum = ssum + jnp.sum(gv, axis=-1, keepdims=True)
        mu = ssum * (1.0 / AW)
        vsum = jnp.zeros((tm, 1), F32)
        for g in range(G):
            dlt = vh_scr[:, g * GD:(g + 1) * GD] - mu
            vsum = vsum + jnp.sum(dlt * dlt, axis=-1, keepdims=True)
        rstd = lax.rsqrt(vsum * (1.0 / AW) + LN_EPS)

        m1 = jnp.zeros((tm, 1), F32)
        m2 = jnp.zeros((tm, 1), F32)
        for g in range(G):
            cs = slice(g * GD, (g + 1) * GD)
            gs = slice(2 * AW + g * GD, 2 * AW + (g + 1) * GD)
            vhat = (vh_scr[:, cs] - mu) * rstd
            vh_scr[:, cs] = vhat
            vb = (vhat * lnw_ref[:, cs] + lnb_ref[:, cs]).astype(BF16)
            zu = z_ref[:, cs]
            u, tu = _gelu_t(zu)
            zg = z_ref[:, gs]
            sig = _sigmoid(zg)
            sg = zg * sig
            dy = dy_scr[:, cs]
            dsf = dy * u * sg
            dsb = dsf.astype(BF16)
            dvs = []
            for n in range(nch):
                rs = slice(n * CH, (n + 1) * CH)
                s = _dot(wc_scr[g], vb[rs, :]) + bst_ref[:, g:g + 1]
                dys = dy[rs, :] * s
                dz_ref[rs, cs] = (dys * sg[rs, :] * _dgelu(zu[rs, :], tu[rs, :])).astype(BF16)
                dz_ref[rs, gs] = (dys * u[rs, :] * (sig[rs, :] * (1.0 + zg[rs, :] * (1.0 - sig[rs, :])))).astype(BF16)
                gbs_acc[g] += dsf[rs, :]
                gwc_acc[g] += _dot_nt(dsb[rs, :], vb[rs, :])
                dvs.append(_dot(wct_scr[g], dsb[rs, :]))
            dv = jnp.concatenate(dvs, axis=0) if nch > 1 else dvs[0]
            glnw_ref[:, cs] += _rowsum(dv * vhat)
            glnb_ref[:, cs] += _rowsum(dv)
            dvh = dv * lnw_ref[:, cs]
            dv_scr[:, cs] = dvh
            m1 = m1 + jnp.sum(dvh, axis=-1, keepdims=True)
            m2 = m2 + jnp.sum(dvh * vhat, axis=-1, keepdims=True)
        m1 = m1 * (1.0 / AW)
        m2 = m2 * (1.0 / AW)
        for g in range(G):
            cs = slice(g * GD, (g + 1) * GD)
            dgv = rstd * (dv_scr[:, cs] - m1 - vh_scr[:, cs] * m2)
            dz_ref[:, AW + g * GD:AW + (g + 1) * GD] = (dgv * dgv_scr[:, cs]).astype(BF16)

        @pl.when(i == nt - 1)
        def _():
            m = _causal_mask()
            for g in range(G):
                gws_ref[g] = jnp.where(m, gwc_acc[g], 0.0)
                gbst_ref[:, g:g + 1] = jnp.sum(gbs_acc[g], axis=-1, keepdims=True)

    tile = lambda w: pl.BlockSpec((tm, w), lambda i: (i, 0))
    whole = lambda *s: pl.BlockSpec(s, lambda i: (0,) * len(s))
    big = lambda dt: pltpu.VMEM((tm, AW), dt)
    return _call(
        main, jobs, name="bwd_a", grid=(nt,), relay_step=relay_step,
        ins=[dx1, z, lnw, lnb, ws, bst, wout], in_specs=[tile(D), tile(3 * AW), _VMEM, _VMEM, _VMEM, _VMEM, _VMEM],
        out_shape=[_sds((s_len, 3 * AW), BF16), _sds((1, AW), F32), _sds((1, AW), F32), _sds((G, CH, CH), F32),
                   _sds((CH, G), F32)],
        out_specs=[tile(3 * AW), whole(1, AW), whole(1, AW), whole(G, CH, CH), whole(CH, G)],
        scratch=[pltpu.VMEM((G, CH, CH), BF16), pltpu.VMEM((G, CH, CH), BF16), big(F32), big(F32), big(F32), big(F32),
                 pltpu.VMEM((G, CH, GD), F32), pltpu.VMEM((G, CH, CH), F32)])


def _bwd_a_in(dz, dx1, x, nw, win8, jobs, *, tm, relay_step):
    s_len = x.shape[0]
    nt = s_len // tm

    def main(i, ins, outs, scr):
        dz_ref, dx1_ref, x_ref, nw_ref, win_ref = ins
        gx_ref, gnw_ref = outs

        @pl.when(i == 0)
        def _():
            gnw_ref[...] = jnp.zeros_like(gnw_ref)

        dh = jnp.zeros((tm, D), F32)
        for k in range(NDEV):
            dh = dh + _dot_nt(dz_ref[:, k * CA:(k + 1) * CA], win_ref[k])
        x = x_ref[...]
        r = _rms(x)
        gx_ref[...] = dx1_ref[...] + _rms_bwd(dh, x, r, nw_ref[...])
        gnw_ref[...] += _rowsum(dh * x * r)

    tile = lambda w: pl.BlockSpec((tm, w), lambda i: (i, 0))
    return _call(
        main, jobs, name="bwd_a_in", grid=(nt,), relay_step=relay_step,
        ins=[dz, dx1, x, nw, win8], in_specs=[tile(3 * AW), tile(D), tile(D), _VMEM, _VMEM],
        out_shape=[_sds((s_len, D), F32), _sds((1, D), F32)],
        out_specs=[tile(D), pl.BlockSpec((1, D), lambda i: (0, 0))], scratch=[])


def _conv(p8_ref, cs, xb, xm1, xm2, xm3):
    xc = p8_ref[4:5, cs] + p8_ref[3:4, cs] * xb
    xc = xc + p8_ref[0:1, cs] * xm3
    xc = xc + p8_ref[1:2, cs] * xm2
    return xc + p8_ref[2:3, cs] * xm1


def _gates(p8_ref, gcat_ref, hh, xc):
    cs = slice(hh * HD, (hh + 1) * HD)
    pre = _dot(xc.astype(BF16), gcat_ref[hh])
    r = _sigmoid(pre[:, :HD] + p8_ref[5:6, cs])
    ig = _sigmoid(pre[:, HD:] + p8_ref[6:7, cs])
    sp = _softplus_neg(p8_ref[7:8, cs])
    la = (-RG_C) * r * sp
    a = jnp.exp(la)
    half_log = 0.5 * jnp.log(jnp.tanh(-la) * (1.0 + a * a))
    return r, ig, sp, a, jnp.exp(half_log), jnp.exp(-half_log)


def _scan_rows(a_ref, b_ref, out_ref, carry, tm, reverse):
    row = lax.broadcasted_iota(jnp.int32, (SUBLANES, BW), 0)
    ngrp = tm // SUBLANES

    def step(j, cr):
        jj = (ngrp - 1 - j) if reverse else j
        off = pl.multiple_of(jj * SUBLANES, SUBLANES)
        a = a_ref[pl.ds(off, SUBLANES), :]
        b = b_ref[pl.ds(off, SUBLANES), :]
        for sh in (1, 2, 4):
            if reverse:
                a_s = pltpu.roll(a, SUBLANES - sh, 0)
                b_s = pltpu.roll(b, SUBLANES - sh, 0)
                m = row < SUBLANES - sh
            else:
                a_s = pltpu.roll(a, sh, 0)
                b_s = pltpu.roll(b, sh, 0)
                m = row >= sh
            b = jnp.where(m, a * b_s + b, b)
            a = jnp.where(m, a * a_s, a)
        o = b + a * cr
        out_ref[pl.ds(off, SUBLANES), :] = o
        return o[0:1, :] if reverse else o[SUBLANES - 1:SUBLANES, :]

    return lax.fori_loop(0, ngrp, step, carry)


def _fwd_b(x, ya, wout_a, nw, win8, p8, gcat, jobs, *, tm, relay_step):
    s_len = x.shape[0]
    nt = s_len // tm

    def main(i, ins, outs, scr):
        x_ref, ya_ref, wouta_ref, nw_ref, win_ref, p8_ref, gcat_ref = ins
        x1_ref, zb_ref, hs_ref, h1_ref, yb_ref, xc_ref, a_ref, cc_ref, r_ref, ig_ref, m_ref = outs
        xbe_scr, b_scr, k_scr, carry_scr = scr

        @pl.when(i == 0)
        def _():
            xbe_scr[0:SUBLANES, :] = jnp.zeros((SUBLANES, BW), F32)
            carry_scr[...] = jnp.zeros_like(carry_scr)

        x1 = x_ref[...] + _dot(ya_ref[...], wouta_ref[...])
        x1_ref[...] = x1
        h = (x1 * _rms(x1) * nw_ref[...]).astype(BF16)
        h1_ref[...] = h
        for k in range(NDEV):
            zb_ref[:, k * CB:(k + 1) * CB] = _dot(h, win_ref[k])
        xbe_scr[SUBLANES:SUBLANES + tm, :] = zb_ref[:, :BW]
        for hh in range(BH):
            cs = slice(hh * HD, (hh + 1) * HD)
            xc = _conv(p8_ref, cs, xbe_scr[SUBLANES:SUBLANES + tm, cs], xbe_scr[7:7 + tm, cs],
                       xbe_scr[6:6 + tm, cs], xbe_scr[5:5 + tm, cs])
            r, ig, _, a, mult, rm = _gates(p8_ref, gcat_ref, hh, xc)
            ixc = ig * xc
            xc_ref[:, cs] = xc
            a_ref[:, cs] = a
            r_ref[:, cs] = r.astype(BF16)
            ig_ref[:, cs] = ig.astype(BF16)
            m_ref[:, cs] = mult.astype(BF16)
            b_scr[:, cs] = mult * ixc
            k_scr[:, cs] = ixc * (a * a * rm)
        xbe_scr[0:SUBLANES, :] = xbe_scr[tm:tm + SUBLANES, :]
        carry_scr[...] = _scan_rows(a_ref, b_scr, hs_ref, carry_scr[...], tm, False)
        for hh in range(BH):
            cs = slice(hh * HD, (hh + 1) * HD)
            gt = zb_ref[:, BW + hh * HD:BW + (hh + 1) * HD]
            hsv = hs_ref[:, cs]
            yb_ref[:, cs] = (hsv * (gt * _sigmoid(gt))).astype(BF16)
            cc_ref[:, cs] = (hsv - b_scr[:, cs]) - k_scr[:, cs]

    tile = lambda w: pl.BlockSpec((tm, w), lambda i: (i, 0))
    wide = lambda dt: _sds((s_len, BW), dt)
    return _call(
        main, jobs, name="fwd_b", grid=(nt,), relay_step=relay_step,
        ins=[x, ya, wout_a, nw, win8, p8, gcat], in_specs=[tile(D), tile(AW), _VMEM, _VMEM, _VMEM, _VMEM, _VMEM],
        out_shape=[_sds((s_len, D), F32), _sds((s_len, 2 * BW), F32), wide(F32), _sds((s_len, D), BF16), wide(BF16),
                   wide(F32), wide(F32), wide(F32), wide(BF16), wide(BF16), wide(BF16)],
        out_specs=[tile(D), tile(2 * BW), tile(BW), tile(D)] + [tile(BW)] * 7,
        scratch=[pltpu.VMEM((tm + SUBLANES, BW), F32), pltpu.VMEM((tm, BW), F32), pltpu.VMEM((tm, BW), F32),
                 pltpu.VMEM((1, BW), F32)])


def _head(x1, yb, wout, nfw, tgt, *, tm):
    s_len = x1.shape[0]

    def main(i, ins, outs, scr):
        x1_ref, yb_ref, wout_ref, nfw_ref, t_ref = ins
        dx2_ref, dx2b_ref, loss_ref, gnfw_ref = outs

        @pl.when(i == 0)
        def _():
            loss_ref[...] = jnp.zeros_like(loss_ref)
            gnfw_ref[...] = jnp.zeros_like(gnfw_ref)

        x2 = x1_ref[...] + _dot(yb_ref[...], wout_ref[...])
        rf = _rms(x2)
        xn = x2 * rf
        e = xn * nfw_ref[...] - t_ref[...]
        loss_ref[...] += (0.5 / D) * jnp.sum(jnp.sum(e * e, axis=-1, keepdims=True), axis=0, keepdims=True)
        dyf = e * (1.0 / D)
        gnfw_ref[...] += _rowsum(dyf * xn)
        dx2 = _rms_bwd(dyf, x2, rf, nfw_ref[...])
        dx2_ref[...] = dx2
        dx2b_ref[...] = dx2.astype(BF16)

    tile = lambda w: pl.BlockSpec((tm, w), lambda i: (i, 0))
    whole = lambda *s: pl.BlockSpec(s, lambda i: (0,) * len(s))
    (dx2, dx2b, loss, gnfw), _ = _call(
        main, [], name="head", grid=(s_len // tm,),
        ins=[x1, yb, wout, nfw, tgt], in_specs=[tile(D), tile(BW), _VMEM, _VMEM, tile(D)],
        out_shape=[_sds((s_len, D), F32), _sds((s_len, D), BF16), _sds((1, 1), F32), _sds((1, D), F32)],
        out_specs=[tile(D), tile(D), whole(1, 1), whole(1, D)], scratch=[])
    return dx2, dx2b, loss, gnfw


def _bwd_b(dx2, zb, hs, x1, saved, nw, win8, p8, gcat, wout, *, tm):
    s_len = x1.shape[0]
    nt = s_len // tm

    def main(i, ins, outs, scr):
        (dx2_ref, zb_ref, hs_ref, x1_ref, xc_ref, a_ref, cc_ref, r_ref, ig_ref, m_ref,
         nw_ref, win_ref, p8_ref, gcat_ref, wout_ref) = ins
        dx1_ref, dx1b_ref, dzb_ref, gp8_ref, gga_ref, ggx_ref, gnw_ref = outs
        ae_scr, an_scr, dhd_scr, dh_scr, dy_scr, dxce_scr, carry_scr, afirst_scr = scr

        @pl.when(i == 0)
        def _():
            gp8_ref[...] = jnp.zeros_like(gp8_ref)
            gga_ref[...] = jnp.zeros_like(gga_ref)
            ggx_ref[...] = jnp.zeros_like(ggx_ref)
            gnw_ref[...] = jnp.zeros_like(gnw_ref)
            dxce_scr[tm:tm + SUBLANES, :] = jnp.zeros((SUBLANES, BW), F32)
            carry_scr[...] = jnp.zeros_like(carry_scr)
            afirst_scr[...] = jnp.zeros_like(afirst_scr)

        dx2 = dx2_ref[...]
        dy_scr[...] = _dot_nt(dx2.astype(BF16), wout_ref[...])
        for hh in range(BH):
            cs = slice(hh * HD, (hh + 1) * HD)
            gs = slice(BW + hh * HD, BW + (hh + 1) * HD)
            gt = zb_ref[:, gs]
            sig = _sigmoid(gt)
            dy = dy_scr[:, cs]
            dhd_scr[:, cs] = dy * (gt * sig)
            dzb_ref[:, gs] = (dy * hs_ref[:, cs] * (sig * (1.0 + gt * (1.0 - sig)))).astype(BF16)

        ae_scr[0:tm, :] = a_ref[...]
        ae_scr[tm:tm + SUBLANES, :] = jnp.broadcast_to(afirst_scr[...], (SUBLANES, BW))
        an_scr[...] = ae_scr[1:1 + tm, :]
        afirst_scr[...] = ae_scr[0:1, :]
        carry_scr[...] = _scan_rows(an_scr, dhd_scr, dh_scr, carry_scr[...], tm, True)

        for hh in range(BH):
            cs = slice(hh * HD, (hh + 1) * HD)
            dh = dh_scr[:, cs]
            mult = m_ref[:, cs].astype(F32)
            ig = ig_ref[:, cs].astype(F32)
            r = r_ref[:, cs].astype(F32)
            xc = xc_ref[:, cs]
            lam = p8_ref[7:8, cs]
            sp = _softplus_neg(lam)
            dla = dh * cc_ref[:, cs]
            gp8_ref[7:8, cs] += _rowsum(dla * ((-RG_C) * r)) * (-_sigmoid(-lam))
            dpr = dla * ((-RG_C) * sp) * (r * (1.0 - r))
            dpi = dh * mult * xc * (ig * (1.0 - ig))
            gp8_ref[5:6, cs] += _rowsum(dpr)
            gp8_ref[6:7, cs] += _rowsum(dpi)
            dcat = jnp.concatenate([dpr, dpi], axis=1).astype(BF16)
            dxc = dh * mult * ig + _dot_nt(dcat, gcat_ref[hh])
            gg = _dot(xc.T.astype(BF16), dcat)
            gga_ref[hh] += gg[:, :HD]
            ggx_ref[hh] += gg[:, HD:]
            dxce_scr[0:tm, cs] = dxc
            gp8_ref[4:5, cs] += _rowsum(dxc)
        for hh in range(BH):
            cs = slice(hh * HD, (hh + 1) * HD)
            xb = zb_ref[:, cs]
            d0, d1 = dxce_scr[0:tm, cs], dxce_scr[1:1 + tm, cs]
            d2, d3 = dxce_scr[2:2 + tm, cs], dxce_scr[3:3 + tm, cs]
            dzb_ref[:, cs] = (p8_ref[3:4, cs] * d0 + p8_ref[2:3, cs] * d1 + p8_ref[1:2, cs] * d2
                              + p8_ref[0:1, cs] * d3).astype(BF16)
            gp8_ref[3:4, cs] += _rowsum(d0 * xb)
            gp8_ref[2:3, cs] += _rowsum(d1 * xb)
            gp8_ref[1:2, cs] += _rowsum(d2 * xb)
            gp8_ref[0:1, cs] += _rowsum(d3 * xb)
        dxce_scr[tm:tm + SUBLANES, :] = dxce_scr[0:SUBLANES, :]

        dh1 = jnp.zeros((tm, D), F32)
        for k in range(NDEV):
            dh1 = dh1 + _dot_nt(dzb_ref[:, k * CB:(k + 1) * CB], win_ref[k])
        x1 = x1_ref[...]
        r1 = _rms(x1)
        dx1 = dx2 + _rms_bwd(dh1, x1, r1, nw_ref[...])
        dx1_ref[...] = dx1
        dx1b_ref[...] = dx1.astype(BF16)
        gnw_ref[...] += _rowsum(dh1 * x1 * r1)

    tile = lambda w: pl.BlockSpec((tm, w), lambda i: (nt - 1 - i, 0))
    whole = lambda *s: pl.BlockSpec(s, lambda i: (0,) * len(s))
    full = lambda: pltpu.VMEM((tm, BW), F32)
    ext = lambda: pltpu.VMEM((tm + SUBLANES, BW), F32)
    out, _ = _call(
        main, [], name="bwd_b", grid=(nt,),
        ins=[dx2, zb, hs, x1, *saved, nw, win8, p8, gcat, wout],
        in_specs=[tile(D), tile(2 * BW), tile(BW), tile(D)] + [tile(BW)] * 6 + [_VMEM] * 5,
        out_shape=[_sds((s_len, D), F32), _sds((s_len, D), BF16), _sds((s_len, 2 * BW), BF16), _sds((SUBLANES, BW), F32),
                   _sds((BH, HD, HD), F32), _sds((BH, HD, HD), F32), _sds((1, D), F32)],
        out_specs=[tile(D), tile(D), tile(2 * BW), whole(SUBLANES, BW), whole(BH, HD, HD), whole(BH, HD, HD),
                   whole(1, D)],
        scratch=[ext(), full(), full(), full(), full(), ext(), pltpu.VMEM((1, BW), F32), pltpu.VMEM((1, BW), F32)])
    return out


def _transpose_into(dst_ref, src_ref, rows):
    s_len = src_ref.shape[0]
    for r0 in range(0, s_len, rows):
        dst_ref[:, r0:r0 + rows] = src_ref[r0:r0 + rows, :].astype(F32).T.astype(BF16)


def _wgrad(a, b, jobs, *, by_rows, per, name, relay_step=0):
    s_len, m = a.shape
    n = b.shape[1]
    r, cd = (m // NDEV, n) if by_rows else (m, n // NDEV)
    nsteps = NDEV // per
    at_rows = per * r if by_rows else m

    def main(i, ins, outs, scr):
        a_ref, b_ref = ins
        q_ref, acc_ref = outs
        at_scr, stage, mine, land, send_sems, recv_sems = scr
        x, y, c = _place()

        def to_sibling(pi):
            return pltpu.make_async_remote_copy(
                src_ref=stage.at[pi & 1], dst_ref=land.at[pi], send_sem=send_sems.at[pi], recv_sem=recv_sems.at[pi],
                device_id=(x, y, 1 - c), device_id_type=MESH)

        if by_rows:
            _transpose_into(at_scr, a_ref, 256)
        else:
            @pl.when(i == 0)
            def _():
                _transpose_into(at_scr, a_ref, 256)

        res = _dot(at_scr[...], b_ref[...]).astype(BF16)
        for k in range(per):
            blk = per * i + k
            pi, pc = blk >> 1, blk & 1
            val = res[k * r:(k + 1) * r, :] if by_rows else res

            @pl.when(pc != c)
            def _():
                @pl.when(pi >= 2)
                def _():
                    to_sibling(pi - 2).wait_send()

                stage[pi & 1] = val
                to_sibling(pi).start()

            @pl.when(pc == c)
            def _():
                mine[pi] = val

        @pl.when(i == nsteps - 1)
        def _():
            for p in range(4):
                to_sibling(p).wait_recv()
            to_sibling(2).wait_send()
            to_sibling(3).wait_send()
            _chip_sums(mine, land, q_ref, acc_ref, x, y)

    if by_rows:
        in_specs = [pl.BlockSpec((s_len, at_rows), lambda j: (0, j)), _VMEM]
    else:
        in_specs = [_VMEM, pl.BlockSpec((s_len, cd), lambda j: (0, j))]
    blk_vmem = lambda k: pltpu.VMEM((k, r, cd), BF16)
    (q, acc), job_out = _call(
        main, jobs, name=name, grid=(nsteps,), relay_step=relay_step, ins=[a, b], in_specs=in_specs,
        out_shape=[_sds((NCHIP_OTHER, r, cd), BF16), _sds((r, cd), F32)],
        out_specs=[pl.BlockSpec((NCHIP_OTHER, r, cd), lambda j: (0, 0, 0)), pl.BlockSpec((r, cd), lambda j: (0, 0))],
        scratch=[pltpu.VMEM((at_rows, s_len), BF16), blk_vmem(2), blk_vmem(4), blk_vmem(4),
                 pltpu.SemaphoreType.DMA((4,)), pltpu.SemaphoreType.DMA((4,))])
    return q, acc, job_out


def _adam_math(w, g, m, v):
    m = B1 * m + (1.0 - B1) * g
    v = B2 * v + (1.0 - B2) * (g * g)
    m_hat = m / (1.0 - B1 ** STEP)
    v_hat = v / (1.0 - B2 ** STEP)
    delta = (-LR) * (m_hat / (jnp.sqrt(v_hat) + ADAM_EPS) + WD * w)
    return delta, m, v


def _adam_big(w, acc, land, m, v, name):
    r, cd = w.shape
    rb = 256 if r % 256 == 0 else r
    nland = land.shape[0]

    def body(w_ref, acc_ref, land_ref, m_ref, v_ref, g_ref, d_ref, mo_ref, vo_ref):
        g = acc_ref[...]
        for j in range(nland):
            g = g + land_ref[j].astype(F32)
        g_ref[...] = g
        d_ref[...], mo_ref[...], vo_ref[...] = _adam_math(w_ref[...], g, m_ref[...], v_ref[...])

    blk = pl.BlockSpec((rb, cd), lambda i: (i, 0))
    blk3 = pl.BlockSpec((nland, rb, cd), lambda i: (0, i, 0))
    return pl.pallas_call(
        body, name=name, grid=(r // rb,), in_specs=[blk, blk, blk3, blk, blk], out_specs=[blk] * 4,
        out_shape=[_sds((r, cd), F32)] * 4,
        compiler_params=_params(dimension_semantics=("arbitrary",)),
    )(w, acc, land, m, v)


def _adam_small(groups):
    n = len(groups)

    def body(*refs):
        ins, outs = refs[:4 * n], refs[4 * n:]
        for k in range(n):
            w_ref, g_ref, m_ref, v_ref = ins[4 * k:4 * k + 4]
            d, mo, vo = _adam_math(w_ref[...], g_ref[...], m_ref[...], v_ref[...])
            outs[3 * k][...] = d
            outs[3 * k + 1][...] = mo
            outs[3 * k + 2][...] = vo

    flat = [a for grp in groups for a in grp]
    shapes = [_sds(grp[0].shape, F32) for grp in groups for _ in range(3)]
    res = pl.pallas_call(
        body, name="adam_small", in_specs=[_VMEM] * (4 * n), out_specs=[_VMEM] * (3 * n), out_shape=shapes,
        compiler_params=_params(),
    )(*flat)
    return [tuple(res[3 * k:3 * k + 3]) for k in range(n)]


TM_FWD_A = 256
RELAY_STEP_FWD_A = 4
RELAY_STEP_FWD_B = 1
TM_BWD_A = 256
RELAY_STEP_BWD_A = 2
TM_BWD_A_IN = 256
RELAY_STEP_BWD_A_IN = 4
TM_FWD_B = 256
TM_HEAD = 512
TM_BWD_B = 256


def _pack(parts, rows):
    flat = jnp.concatenate([p.reshape(-1) for p in parts])
    return jnp.pad(flat, (0, NDEV * rows * LANES - flat.shape[0])).reshape(NDEV, rows, LANES)


def _unpack(packed, shapes):
    flat, out, off = packed.reshape(-1), [], 0
    for s in shapes:
        size = 1
        for d in s:
            size *= d
        out.append(flat[off:off + size].reshape(s))
        off += size
    return out


def kernel(x, norm_w, a_w_in, a_ln_w, a_ln_b, a_w_s, a_b_s, a_w_out, b_w_in, b_conv_w, b_conv_b, b_gate_a_w, b_gate_a_b, b_gate_x_w, b_gate_x_b, b_lambda, b_w_out, norm_f_w, loss_target, m_norm_w, m_a_w_in, m_a_ln_w, m_a_ln_b, m_a_w_s, m_a_b_s, m_a_w_out, m_b_w_in, m_b_conv_w, m_b_conv_b, m_b_gate_a_w, m_b_gate_a_b, m_b_gate_x_w, m_b_gate_x_b, m_b_lambda, m_b_w_out, m_norm_f_w, v_norm_w, v_a_w_in, v_a_ln_w, v_a_ln_b, v_a_w_s, v_a_b_s, v_a_w_out, v_b_w_in, v_b_conv_w, v_b_conv_b, v_b_gate_a_w, v_b_gate_a_b, v_b_gate_x_w, v_b_gate_x_b, v_b_lambda, v_b_w_out, v_norm_f_w):
    me = 4 * lax.axis_index("x") + 2 * lax.axis_index("y") + lax.axis_index("c")
    xs, tgt = x[0], loss_target[0]
    nw0, nw1, nfw = norm_w[0:1], norm_w[1:2], norm_f_w.reshape(1, D)
    w_s, bst = a_w_s[0], a_b_s[0].T
    gcat = jnp.concatenate([b_gate_a_w[0], b_gate_x_w[0]], axis=-1).astype(BF16)

    p8_shard = jnp.concatenate([b_conv_w[0], b_conv_b, b_gate_a_b, b_gate_x_b, b_lambda], axis=0)
    ((win_a8, p8_all),) = _comm_only([_Gather([a_w_in[0].astype(BF16), p8_shard])], "gather_first")
    p8 = jnp.transpose(p8_all, (1, 0, 2)).reshape(SUBLANES, BW)

    (z, h0, ya), ((wout_a8, win_b8),) = _fwd_a(
        xs, nw0, win_a8, a_ln_w, a_ln_b, w_s, bst, [_Gather([a_w_out[0].astype(BF16), b_w_in[0].astype(BF16)])],
        tm=TM_FWD_A, relay_step=RELAY_STEP_FWD_A)
    wout_a = wout_a8.reshape(AW, D)
    (x1, zb, hs, h1, yb, *saved_b), ((wout_b8,),) = _fwd_b(
        xs, ya, wout_a, nw1, win_b8, p8, gcat, [_Gather([b_w_out[0].astype(BF16)])],
        tm=TM_FWD_B, relay_step=RELAY_STEP_FWD_B)
    wout_b = wout_b8.reshape(BW, D)
    dx2, dx2b, loss, g_nfw = _head(x1, yb, wout_b, nfw, tgt, tm=TM_HEAD)

    dx1, dx1b, dzb, g_p8, g_ga, g_gx, g_nw1 = _bwd_b(dx2, zb, hs, x1, saved_b, nw1, win_b8, p8, gcat, wout_b,
                                                     tm=TM_BWD_B)
    q_wout_b, acc_wout_b, _ = _wgrad(yb, dx2b, [], by_rows=True, per=2, name="wgrad_b_out")
    shapes_b = [(1, D), (1, D), (SUBLANES, BW), (1, 1)]
    pack_b = _pack([g_nfw, g_nw1, g_p8, loss], 16)
    small_b = _InChip([g_ga.reshape(NDEV, -1, HD), g_gx.reshape(NDEV, -1, HD), pack_b])
    q_win_b, acc_win_b, (sm_b, (l_wout_b,)) = _wgrad(h1, dzb, [small_b, _Exchange([q_wout_b])], by_rows=False, per=1,
                                                      name="wgrad_b_in")
    qs_b, accs_b = sm_b[:3], sm_b[3:]

    (dz, g_lnw, g_lnb, g_ws, g_bst), (lands_b, (l_win_b,)) = _bwd_a(
        dx1b, z, a_ln_w, a_ln_b, w_s, bst, wout_a, [_Exchange(qs_b), _ExchangeVia(q_win_b)],
        tm=TM_BWD_A, relay_step=RELAY_STEP_BWD_A)
    shapes_a = [(1, AW), (1, AW), (CH, G)]
    pack_a = _pack([g_lnw, g_lnb, g_bst], 8)
    q_wout_a, acc_wout_a, (red_b, sm_a) = _wgrad(
        ya, dx1b, [_SumGather(accs_b, lands_b), _InChip([g_ws, pack_a])], by_rows=True, per=2,
        name="wgrad_a_out", relay_step=1)
    qs_a, accs_a = [q_wout_a, *sm_a[:2]], [acc_wout_a, *sm_a[2:]]
    q_win_a, acc_win_a, (lands_a,) = _wgrad(h0, dz, [_Exchange(qs_a)], by_rows=False, per=1, name="wgrad_a_in")
    (gx, g_nw0), (red_a, (l_win_a,)) = _bwd_a_in(
        dz, dx1, xs, nw0, win_a8, [_SumGather(accs_a[1:], lands_a[1:]), _ExchangeVia(q_win_a)],
        tm=TM_BWD_A_IN, relay_step=RELAY_STEP_BWD_A_IN)
    g_nw0 = _allreduce_direct(g_nw0, "allreduce_norm_w0")

    r_ga, r_gx, r_pack_b = red_b
    r_nfw, r_nw1, r_p8, loss = _unpack(r_pack_b, shapes_b)
    r_ws, r_pack_a = red_a
    r_lnw, r_lnb, r_bst = _unpack(r_pack_a, shapes_a)
    g_p8 = lax.dynamic_slice_in_dim(r_p8, me * (BW // NDEV), BW // NDEV, axis=1)
    loss = loss[0, 0]

    weights = dict(norm_w=norm_w, a_w_in=a_w_in, a_ln_w=a_ln_w, a_ln_b=a_ln_b, a_w_s=a_w_s, a_b_s=a_b_s, a_w_out=a_w_out,
                   b_w_in=b_w_in, b_conv_w=b_conv_w, b_conv_b=b_conv_b, b_gate_a_w=b_gate_a_w, b_gate_a_b=b_gate_a_b,
                   b_gate_x_w=b_gate_x_w, b_gate_x_b=b_gate_x_b, b_lambda=b_lambda, b_w_out=b_w_out, norm_f_w=norm_f_w)
    mom1 = dict(norm_w=m_norm_w, a_w_in=m_a_w_in, a_ln_w=m_a_ln_w, a_ln_b=m_a_ln_b, a_w_s=m_a_w_s, a_b_s=m_a_b_s,
                a_w_out=m_a_w_out, b_w_in=m_b_w_in, b_conv_w=m_b_conv_w, b_conv_b=m_b_conv_b, b_gate_a_w=m_b_gate_a_w,
                b_gate_a_b=m_b_gate_a_b, b_gate_x_w=m_b_gate_x_w, b_gate_x_b=m_b_gate_x_b, b_lambda=m_b_lambda,
                b_w_out=m_b_w_out, norm_f_w=m_norm_f_w)
    mom2 = dict(norm_w=v_norm_w, a_w_in=v_a_w_in, a_ln_w=v_a_ln_w, a_ln_b=v_a_ln_b, a_w_s=v_a_w_s, a_b_s=v_a_b_s,
                a_w_out=v_a_w_out, b_w_in=v_b_w_in, b_conv_w=v_b_conv_w, b_conv_b=v_b_conv_b, b_gate_a_w=v_b_gate_a_w,
                b_gate_a_b=v_b_gate_a_b, b_gate_x_w=v_b_gate_x_w, b_gate_x_b=v_b_gate_x_b, b_lambda=v_b_lambda,
                b_w_out=v_b_w_out, norm_f_w=v_norm_f_w)
    names = list(weights)

    def as2d(a):
        return a.reshape(-1, a.shape[-1])

    upd, grads = {}, {}
    for k, acc, land in (("a_w_in", acc_win_a, l_win_a), ("a_w_out", accs_a[0], lands_a[0]),
                         ("b_w_in", acc_win_b, l_win_b), ("b_w_out", acc_wout_b, l_wout_b)):
        g, d, mo, vo = _adam_big(as2d(weights[k]), acc, land, as2d(mom1[k]), as2d(mom2[k]), "adam_" + k)
        grads[k] = g[None]
        upd[k] = (d, mo, vo)
    grads.update(
        norm_w=jnp.concatenate([g_nw0, r_nw1], axis=0), a_ln_w=r_lnw, a_ln_b=r_lnb,
        a_w_s=r_ws.reshape(1, G, CH, CH), a_b_s=r_bst.T[None],
        b_conv_w=g_p8[None, 0:4], b_conv_b=g_p8[4:5], b_gate_a_w=r_ga.reshape(1, BH, HD, HD), b_gate_a_b=g_p8[5:6],
        b_gate_x_w=r_gx.reshape(1, BH, HD, HD), b_gate_x_b=g_p8[6:7], b_lambda=g_p8[7:8], norm_f_w=r_nfw.reshape(D))
    small_names = [k for k in names if k not in upd]
    res = _adam_small([(as2d(weights[k]), as2d(grads[k]), as2d(mom1[k]), as2d(mom2[k])) for k in small_names])
    for k, r3 in zip(small_names, res):
        upd[k] = r3
    deltas = [upd[k][0].reshape(weights[k].shape) for k in names]
    new_m = [upd[k][1].reshape(weights[k].shape) for k in names]
    new_v = [upd[k][2].reshape(weights[k].shape) for k in names]
    return (loss, gx[None], *[grads[k] for k in names], *deltas, *new_m, *new_v)
```

```python
import jax
import jax.numpy as jnp
from jax import lax
from jax.experimental import pallas as pl
from jax.experimental.pallas import tpu as pltpu

F32 = jnp.float32
BF16 = jnp.bfloat16
MESH = pl.DeviceIdType.MESH

NDEV = 8
NCHIP_OTHER = 3
D = 1024
AW = 2048
G = 8
GD = AW // G
CH = 128
BW = 1536
BH = 12
HD = BW // BH
CA = 3 * AW // NDEV
CB = 2 * BW // NDEV
RMS_EPS = 1e-6
LN_EPS = 1e-5
RG_C = 8.0
LR, B1, B2, ADAM_EPS, WD, STEP = 0.001, 0.9, 0.999, 1e-08, 0.01, 10
V7X_VMEM_BYTES = 64 * 1024 * 1024
VMEM_LIMIT = V7X_VMEM_BYTES - 8 * 1024 * 1024
SUBLANES = 8
LANES = 128
BF16_ROWS = 16
GELU_C = 0.7978845608028654
GELU_K = 0.044715

_VMEM = pl.BlockSpec(memory_space=pltpu.VMEM)
_HBM = pl.BlockSpec(memory_space=pltpu.HBM)


def _sds(shape, dtype):
    return jax.ShapeDtypeStruct(tuple(shape), dtype)


def _params(**kw):
    return pltpu.CompilerParams(vmem_limit_bytes=VMEM_LIMIT, **kw)


def _gelu_t(z):
    t = jnp.tanh(GELU_C * (z + GELU_K * (z * z * z)))
    return 0.5 * z * (1.0 + t), t


def _dgelu(z, t):
    return 0.5 * (1.0 + t) + 0.5 * z * (1.0 - t * t) * (GELU_C * (1.0 + 3.0 * GELU_K * z * z))


def _sigmoid(v):
    return 0.5 * jnp.tanh(0.5 * v) + 0.5


def _softplus_neg(lam):
    return jnp.maximum(-lam, 0.0) + jnp.log1p(jnp.exp(-jnp.abs(lam)))


def _dot(a, b):
    return jnp.dot(a, b, preferred_element_type=F32)


def _dot_nt(a, b):
    return lax.dot_general(a, b, (((1,), (1,)), ((), ())), preferred_element_type=F32)


def _rowsum(v):
    return jnp.sum(v, axis=0, keepdims=True)


def _causal_mask():
    r = lax.broadcasted_iota(jnp.int32, (CH, CH), 0)
    c = lax.broadcasted_iota(jnp.int32, (CH, CH), 1)
    return r >= c


def _rms(x):
    return lax.rsqrt(jnp.mean(x * x, axis=-1, keepdims=True) + RMS_EPS)


def _rms_bwd(dh, x, r, nw):
    gy = dh * nw
    return r * gy - x * (r * r * r) * jnp.mean(gy * x, axis=-1, keepdims=True)


def _place():
    return lax.axis_index("x"), lax.axis_index("y"), lax.axis_index("c")


def _other_chips(x, y):
    return [(1 - x, y), (x, 1 - y), (1 - x, 1 - y)]


GATHER_SLOTS = 10


def _gather_ops(ins, outs, send_sems, recv_sems, local_sems):
    n = len(ins)
    x, y, c = _place()
    sibling = (x, y, 1 - c)
    xn, yn, dg = _other_chips(x, y)
    split = [ins[i].shape[0] % (2 * BF16_ROWS) == 0 for i in range(n)]

    def blk(chip, core):
        return 4 * chip[0] + 2 * chip[1] + core

    me = blk((x, y), c)

    def part(ref, i, half):
        if half is None:
            return ref
        h = ins[i].shape[0] // 2
        return ref.at[pl.ds(half * h, h)]

    def copy(i, k, block, to, half=None, src=None):
        dst = part(outs[i].at[block], i, half)
        return pltpu.make_async_remote_copy(
            src_ref=dst if src is None else part(src, i, half), dst_ref=dst,
            send_sem=send_sems.at[k, i], recv_sem=recv_sems.at[k, i], device_id=to, device_id_type=MESH)

    def first_copies():
        mine = [pltpu.make_async_copy(ins[i], outs[i].at[me], local_sems.at[i]) for i in range(n)]
        first = []
        for i in range(n):
            first.append(copy(i, 0, me, sibling, src=ins[i]))
            if split[i]:
                first.append(copy(i, 1, me, (*xn, c), 0, ins[i]))
                first.append(copy(i, 3, me, (*yn, c), 1, ins[i]))
                first.append(copy(i, 2, me, (*xn, c), 1, ins[i]))
                first.append(copy(i, 4, me, (*yn, c), 0, ins[i]))
            else:
                first.append(copy(i, 1, me, (*xn, c), None, ins[i]))
                first.append(copy(i, 3, me, (*yn, c), None, ins[i]))
                first.append(copy(i, 5, me, (*dg, c), None, ins[i]))
        return mine, first

    def onward():
        out = []
        for i in range(n):
            if split[i]:
                out.append(copy(i, 5, blk(xn, c), (*yn, c), 0))
                out.append(copy(i, 6, blk(yn, c), (*xn, c), 1))
        return out

    def start():
        mine, first = first_copies()
        for cp in mine + first:
            cp.start()

    def relay():
        sends = onward()
        for i in range(n):
            if split[i]:
                copy(i, 1, blk(xn, c), sibling, 0).wait_recv()
                sends.pop(0).start()
                copy(i, 3, blk(yn, c), sibling, 1).wait_recv()
                sends.pop(0).start()

    def finish():
        mine, first = first_copies()
        passed = []

        def pass_on(i, j, chip):
            fwd = copy(i, 7 + j, blk(chip, c), sibling)
            fwd.start()
            passed.append(fwd)

        for i in range(n):
            if split[i]:
                copy(i, 2, blk(xn, c), sibling, 1).wait_recv()
                pass_on(i, 0, xn)
                copy(i, 4, blk(yn, c), sibling, 0).wait_recv()
                pass_on(i, 1, yn)
                copy(i, 5, blk(dg, c), sibling, 0).wait_recv()
                copy(i, 6, blk(dg, c), sibling, 1).wait_recv()
                pass_on(i, 2, dg)
            else:
                copy(i, 1, blk(xn, c), sibling).wait_recv()
                pass_on(i, 0, xn)
                copy(i, 3, blk(yn, c), sibling).wait_recv()
                pass_on(i, 1, yn)
                copy(i, 5, blk(dg, c), sibling).wait_recv()
                pass_on(i, 2, dg)
        for i in range(n):
            copy(i, 0, blk((x, y), 1 - c), sibling).wait_recv()
            for j, chip in enumerate((xn, yn, dg)):
                copy(i, 7 + j, blk(chip, 1 - c), sibling).wait_recv()
        for cp in first + passed + onward():
            cp.wait_send()
        for cp in mine:
            cp.wait()

    return start, relay, finish


def _gather_sems(n):
    return [pltpu.SemaphoreType.DMA((GATHER_SLOTS, n)), pltpu.SemaphoreType.DMA((GATHER_SLOTS, n)),
            pltpu.SemaphoreType.DMA((n,))]


class _Gather:
    def __init__(self, shards, as_dtypes=None):
        n = len(shards)
        dts = [s.dtype for s in shards] if as_dtypes is None else list(as_dtypes)
        self.cast = [jnp.dtype(d) != s.dtype for d, s in zip(dts, shards)]
        self.ins = list(shards)
        self.in_specs = [_VMEM if c else _HBM for c in self.cast]
        self.out_shape = [_sds((NDEV,) + s.shape, d) for s, d in zip(shards, dts)]
        self.out_specs = [_HBM] * n
        self.scratch = [pltpu.VMEM(s.shape, d) for s, d, c in zip(shards, dts, self.cast) if c] + _gather_sems(n)

    def ops(self, ins, outs, scr):
        ncast = sum(self.cast)
        staged = iter(scr[:ncast])
        srcs = [next(staged) if c else ref for c, ref in zip(self.cast, ins)]
        start, relay, finish = _gather_ops(srcs, outs, *scr[ncast:])

        def cast_and_start():
            for c, ref, src in zip(self.cast, ins, srcs):
                if c:
                    src[...] = ref[...].astype(src.dtype)
            start()

        return cast_and_start, relay, finish


class _Exchange:
    def __init__(self, qs):
        n = len(qs)
        self.ins, self.in_specs = list(qs), [_HBM] * n
        self.out_shape = [_sds(q.shape, q.dtype) for q in qs]
        self.out_specs = [_HBM] * n
        self.scratch = [pltpu.SemaphoreType.DMA((NCHIP_OTHER, n)), pltpu.SemaphoreType.DMA((NCHIP_OTHER, n))]

    def ops(self, ins, outs, scr):
        send_sems, recv_sems = scr
        n = len(ins)
        x, y, c = _place()
        chips = _other_chips(x, y)

        def copies():
            return [pltpu.make_async_remote_copy(
                src_ref=ins[i].at[j], dst_ref=outs[i].at[j], send_sem=send_sems.at[j, i],
                recv_sem=recv_sems.at[j, i], device_id=(*chips[j], c), device_id_type=MESH)
                for i in range(n) for j in range(NCHIP_OTHER)]

        def start():
            for cp in copies():
                cp.start()

        def finish():
            cps = copies()
            for cp in cps:
                cp.wait_recv()
            for cp in cps:
                cp.wait_send()

        return start, lambda: None, finish


class _ExchangeVia:
    def __init__(self, q):
        _, r, cd = q.shape
        half = (2, r // 2, cd)
        self.ins, self.in_specs = [q], [_HBM]
        self.out_shape, self.out_specs = [_sds((2, r, cd), q.dtype)], [_HBM]
        self.scratch = [pltpu.VMEM(half, q.dtype), pltpu.VMEM(half, q.dtype), pltpu.VMEM(half, q.dtype),
                        pltpu.SemaphoreType.DMA((6,)), pltpu.SemaphoreType.DMA((6,)), pltpu.SemaphoreType.DMA((2,))]

    def ops(self, ins, outs, scr):
        (q,), (land,) = ins, outs
        relayed, own, comb, send_sems, recv_sems, local_sems = scr
        h = q.shape[1] // 2
        x, y, c = _place()
        xn, yn, _ = _other_chips(x, y)
        h0, h1 = pl.ds(0, h), pl.ds(h, h)

        def remote(k, src, dst, chip):
            return pltpu.make_async_remote_copy(src_ref=src, dst_ref=dst, send_sem=send_sems.at[k],
                                                recv_sem=recv_sems.at[k], device_id=(*chip, c), device_id_type=MESH)

        def via():
            return [remote(2, q.at[2, h0], relayed.at[0], xn), remote(3, q.at[2, h1], relayed.at[1], yn)]

        def direct():
            return [remote(0, q.at[0, h0], land.at[0, h0], xn), remote(1, q.at[1, h1], land.at[1, h1], yn)]

        def second():
            return [remote(4, comb.at[0], land.at[1, h0], yn), remote(5, comb.at[1], land.at[0, h1], xn)]

        def mine():
            return [pltpu.make_async_copy(q.at[1, h0], own.at[0], local_sems.at[0]),
                    pltpu.make_async_copy(q.at[0, h1], own.at[1], local_sems.at[1])]

        def start():
            for cp in via() + direct() + mine():
                cp.start()

        def relay():
            arrived, loaded, onward = via(), mine(), second()
            for k in range(2):
                arrived[k].wait_recv()
                loaded[k].wait()
                comb[k] = (own[k].astype(F32) + relayed[k].astype(F32)).astype(comb.dtype)
                onward[k].start()

        def finish():
            landing = direct() + second()
            for cp in landing:
                cp.wait_recv()
            for cp in via() + landing:
                cp.wait_send()

        return start, relay, finish


class _SumGather:
    def __init__(self, accs, lands):
        n = len(accs)
        self.n = n
        self.ins, self.in_specs = list(accs) + list(lands), [_VMEM] * (2 * n)
        self.out_shape = [_sds((NDEV,) + a.shape, a.dtype) for a in accs]
        self.out_specs = [_HBM] * n
        self.scratch = [pltpu.VMEM(a.shape, a.dtype) for a in accs] + _gather_sems(n)

    def ops(self, ins, outs, scr):
        n = self.n
        accs, lands, mine = ins[:n], ins[n:], scr[:n]
        g_start, relay, finish = _gather_ops(mine, outs, *scr[n:])

        def start():
            for i in range(n):
                mine[i][...] = accs[i][...] + lands[i][0] + lands[i][1] + lands[i][2]
            g_start()

        return start, relay, finish


def _call(main, jobs, *, name, grid, ins, in_specs, out_shape, out_specs, scratch, relay_step=0):
    nsteps = grid[0] if grid else 1
    n_in, n_out, n_scr = len(ins), len(out_shape), len(scratch)

    def body(*refs):
        pos = [0]

        def take(k):
            r = refs[pos[0]:pos[0] + k]
            pos[0] += k
            return r

        m_in = take(n_in)
        j_in = [take(len(j.ins)) for j in jobs]
        m_out = take(n_out)
        j_out = [take(len(j.out_shape)) for j in jobs]
        m_scr = take(n_scr)
        j_scr = [take(len(j.scratch)) for j in jobs]
        ops = [j.ops(a, b, s) for j, a, b, s in zip(jobs, j_in, j_out, j_scr)]
        i = pl.program_id(0) if grid else 0
        if not grid:
            for o in ops:
                o[0]()
            main(i, m_in, m_out, m_scr)
            for o in ops:
                o[1]()
            for o in ops:
                o[2]()
            return

        if ops:
            @pl.when(i == 0)
            def _():
                for o in ops:
                    o[0]()

        main(i, m_in, m_out, m_scr)

        if ops:
            @pl.when(i == min(relay_step, nsteps - 1))
            def _():
                for o in ops:
                    o[1]()

            @pl.when(i == nsteps - 1)
            def _():
                for o in ops:
                    o[2]()

    extra = dict(dimension_semantics=("arbitrary",)) if grid else {}
    res = pl.pallas_call(
        body, name=name, grid=grid,
        in_specs=list(in_specs) + [s for j in jobs for s in j.in_specs],
        out_specs=list(out_specs) + [s for j in jobs for s in j.out_specs],
        out_shape=list(out_shape) + [s for j in jobs for s in j.out_shape],
        scratch_shapes=list(scratch) + [s for j in jobs for s in j.scratch],
        compiler_params=_params(**extra),
    )(*ins, *[a for j in jobs for a in j.ins])
    main_out, rest, job_out = res[:n_out], res[n_out:], []
    for j in jobs:
        k = len(j.out_shape)
        job_out.append(rest[:k])
        rest = rest[k:]
    return main_out, job_out


def _comm_only(jobs, name):
    _, job_out = _call(lambda i, a, b, s: None, jobs, name=name, grid=(), ins=[], in_specs=[], out_shape=[],
                       out_specs=[], scratch=[])
    return job_out


class _InChip:
    def __init__(self, ps):
        n = len(ps)
        self.n = n
        blk = [p.shape[1:] for p in ps]
        self.ins, self.in_specs = list(ps), [_HBM] * n
        self.out_shape = [_sds((NCHIP_OTHER,) + b, p.dtype) for b, p in zip(blk, ps)] + [_sds(b, F32) for b in blk]
        self.out_specs = [_VMEM] * (2 * n)
        self.scratch = ([pltpu.VMEM((4,) + b, p.dtype) for b, p in zip(blk, ps)] * 2
                        + [pltpu.SemaphoreType.DMA((4, n))] * 3)

    def ops(self, ins, outs, scr):
        n = self.n
        q_refs, acc_refs = outs[:n], outs[n:]
        mines, lands = scr[:n], scr[n:2 * n]
        send_sems, recv_sems, local_sems = scr[2 * n:]
        x, y, c = _place()
        sibling = (x, y, 1 - c)

        def copies():
            out = []
            for i in range(n):
                for pi in range(4):
                    loc = pltpu.make_async_copy(ins[i].at[2 * pi + c], mines[i].at[pi], local_sems.at[pi, i])
                    cp = pltpu.make_async_remote_copy(
                        src_ref=ins[i].at[2 * pi + (1 - c)], dst_ref=lands[i].at[pi],
                        send_sem=send_sems.at[pi, i], recv_sem=recv_sems.at[pi, i],
                        device_id=sibling, device_id_type=MESH)
                    out.append((loc, cp))
            return out

        def start():
            for loc, cp in copies():
                loc.start()
                cp.start()

        def finish():
            pairs = copies()
            for loc, cp in pairs:
                loc.wait()
                cp.wait_recv()
            for i in range(n):
                _chip_sums(mines[i], lands[i], q_refs[i], acc_refs[i], x, y)
            for _, cp in pairs:
                cp.wait_send()

        return start, lambda: None, finish


def _chip_sums(mine, land, q_ref, acc_ref, x, y):
    for j, (qx, qy) in enumerate(_other_chips(x, y)):
        qi = 2 * qx + qy
        q_ref[j] = (mine[qi].astype(F32) + land[qi].astype(F32)).astype(q_ref.dtype)
    mi = 2 * x + y
    acc_ref[...] = mine[mi].astype(F32) + land[mi].astype(F32)


def _direct_sum(v, buf, send_sems, recv_sems):
    x, y, c = _place()
    me = 4 * x + 2 * y + c
    buf[me] = v
    cps = []
    for k in range(1, NDEV):
        fx, fy, fc = (k >> 2) & 1, (k >> 1) & 1, k & 1
        peer = ((1 - x) if fx else x, (1 - y) if fy else y, (1 - c) if fc else c)
        cps.append((peer, pltpu.make_async_remote_copy(
            src_ref=buf.at[me], dst_ref=buf.at[me], send_sem=send_sems.at[k - 1], recv_sem=recv_sems.at[k - 1],
            device_id=peer, device_id_type=MESH)))
    for _, cp in cps:
        cp.start()
    for k, (peer, _) in enumerate(cps):
        theirs = 4 * peer[0] + 2 * peer[1] + peer[2]
        pltpu.make_async_remote_copy(
            src_ref=buf.at[theirs], dst_ref=buf.at[theirs], send_sem=send_sems.at[k], recv_sem=recv_sems.at[k],
            device_id=peer, device_id_type=MESH).wait_recv()
    acc = buf[0]
    for j in range(1, NDEV):
        acc = acc + buf[j]
    for _, cp in cps:
        cp.wait_send()
    return acc


def _direct_sum_scratch(shape, dtype):
    return [pltpu.VMEM((NDEV,) + tuple(shape), dtype), pltpu.SemaphoreType.DMA((NDEV - 1,)),
            pltpu.SemaphoreType.DMA((NDEV - 1,))]


def _fwd_a(x, nw, win8, lnw, lnb, ws, bst, jobs, *, tm, relay_step):
    s_len = x.shape[0]
    nt = s_len // tm
    nch = tm // CH

    def main(i, ins, outs, scr):
        x_ref, nw_ref, win_ref, lnw_ref, lnb_ref, ws_ref, bst_ref = ins
        z_ref, h_ref, y_ref = outs
        wc_scr, gv_scr = scr

        @pl.when(i == 0)
        def _():
            m = _causal_mask()
            for g in range(G):
                wc_scr[g] = jnp.where(m, ws_ref[g], 0.0).astype(BF16)

        x = x_ref[...]
        h = (x * _rms(x) * nw_ref[...]).astype(BF16)
        h_ref[...] = h
        for k in range(NDEV):
            z_ref[:, k * CA:(k + 1) * CA] = _dot(h, win_ref[k])

        ssum = jnp.zeros((tm, 1), F32)
        for g in range(G):
            gv = _gelu_t(z_ref[:, AW + g * GD:AW + (g + 1) * GD])[0]
            gv_scr[:, g * GD:(g + 1) * GD] = gv
            ssum = ssum + jnp.sum(gv, axis=-1, keepdims=True)
        mu = ssum * (1.0 / AW)
        vsum = jnp.zeros((tm, 1), F32)
        for g in range(G):
            dlt = gv_scr[:, g * GD:(g + 1) * GD] - mu
            vsum = vsum + jnp.sum(dlt * dlt, axis=-1, keepdims=True)
        rstd = lax.rsqrt(vsum * (1.0 / AW) + LN_EPS)

        for g in range(G):
            cs = slice(g * GD, (g + 1) * GD)
            v = (gv_scr[:, cs] - mu) * rstd * lnw_ref[:, cs] + lnb_ref[:, cs]
            vb = v.astype(BF16)
            u = _gelu_t(z_ref[:, cs])[0]
            zg = z_ref[:, 2 * AW + g * GD:2 * AW + (g + 1) * GD]
            sg = zg * _sigmoid(zg)
            for n in range(nch):
                rs = slice(n * CH, (n + 1) * CH)
                s = _dot(wc_scr[g], vb[rs, :]) + bst_ref[:, g:g + 1]
                y_ref[rs, cs] = (u[rs, :] * s * sg[rs, :]).astype(BF16)

    tile = lambda w: pl.BlockSpec((tm, w), lambda i: (i, 0))
    return _call(
        main, jobs, name="fwd_a", grid=(nt,), relay_step=relay_step,
        ins=[x, nw, win8, lnw, lnb, ws, bst], in_specs=[tile(D), _VMEM, _VMEM, _VMEM, _VMEM, _VMEM, _VMEM],
        out_shape=[_sds((s_len, 3 * AW), F32), _sds((s_len, D), BF16), _sds((s_len, AW), BF16)],
        out_specs=[tile(3 * AW), tile(D), tile(AW)],
        scratch=[pltpu.VMEM((G, CH, CH), BF16), pltpu.VMEM((tm, AW), F32)])


def _bwd_a(dx1, z, lnw, lnb, ws, bst, wout, jobs, *, tm, relay_step):
    s_len = dx1.shape[0]
    nt = s_len // tm
    nch = tm // CH

    def main(i, ins, outs, scr):
        dx1_ref, z_ref, lnw_ref, lnb_ref, ws_ref, bst_ref, wout_ref = ins
        dz_ref, glnw_ref, glnb_ref, gws_ref, gbst_ref = outs
        wc_scr, wct_scr, vh_scr, dgv_scr, dy_scr, dv_scr, gbs_acc, gwc_acc = scr

        @pl.when(i == 0)
        def _():
            m = _causal_mask()
            for g in range(G):
                wm = jnp.where(m, ws_ref[g], 0.0)
                wc_scr[g] = wm.astype(BF16)
                wct_scr[g] = wm.T.astype(BF16)
            glnw_ref[...] = jnp.zeros_like(glnw_ref)
            glnb_ref[...] = jnp.zeros_like(glnb_ref)
            gbs_acc[...] = jnp.zeros_like(gbs_acc)
            gwc_acc[...] = jnp.zeros_like(gwc_acc)

        dy_scr[...] = _dot_nt(dx1_ref[...], wout_ref[...])

        ssum = jnp.zeros((tm, 1), F32)
        for g in range(G):
            cs = slice(g * GD, (g + 1) * GD)
            zv = z_ref[:, AW + g * GD:AW + (g + 1) * GD]
            gv, t = _gelu_t(zv)
            vh_scr[:, cs] = gv
            dgv_scr[:, cs] = _dgelu(zv, t)
            ssum = ssum + jnp.sum(gv, axis=-1, keepdims=True)
        mu = ssum * (1.0 / AW)
        vsum = jnp.zeros((tm, 1), F32)
        for g in range(G):
            dlt = vh_scr[:, g * GD:(g + 1) * GD] - mu
            vsum = vsum + jnp.sum(dlt * dlt, axis=-1, keepdims=True)
        rstd = lax.rsqrt(vsum * (1.0 / AW) + LN_EPS)

        m1 = jnp.zeros((tm, 1), F32)
        m2 = jnp.zeros((tm, 1), F32)
        for g in range(G):
            cs = slice(g * GD, (g + 1) * GD)
            gs = slice(2 * AW + g * GD, 2 * AW + (g + 1) * GD)
            vhat = (vh_scr[:, cs] - mu) * rstd
            vh_scr[:, cs] = vhat
            vb = (vhat * lnw_ref[:, cs] + lnb_ref[:, cs]).astype(BF16)
            zu = z_ref[:, cs]
            u, tu = _gelu_t(zu)
            zg = z_ref[:, gs]
            sig = _sigmoid(zg)
            sg = zg * sig
            dy = dy_scr[:, cs]
            dsf = dy * u * sg
            dsb = dsf.astype(BF16)
            dvs = []
            for n in range(nch):
                rs = slice(n * CH, (n + 1) * CH)
                s = _dot(wc_scr[g], vb[rs, :]) + bst_ref[:, g:g + 1]
                dys = dy[rs, :] * s
                dz_ref[rs, cs] = (dys * sg[rs, :] * _dgelu(zu[rs, :], tu[rs, :])).astype(BF16)
                dz_ref[rs, gs] = (dys * u[rs, :] * (sig[rs, :] * (1.0 + zg[rs, :] * (1.0 - sig[rs, :])))).astype(BF16)
                gbs_acc[g] += dsf[rs, :]
                gwc_acc[g] += _dot_nt(dsb[rs, :], vb[rs, :])
                dvs.append(_dot(wct_scr[g], dsb[rs, :]))
            dv = jnp.concatenate(dvs, axis=0) if nch > 1 else dvs[0]
            glnw_ref[:, cs] += _rowsum(dv * vhat)
            glnb_ref[:, cs] += _rowsum(dv)
            dvh = dv * lnw_ref[:, cs]
            dv_scr[:, cs] = dvh
            m1 = m1 + jnp.sum(dvh, axis=-1, keepdims=True)
            m2 = m2 + jnp.sum(dvh * vhat, axis=-1, keepdims=True)
        m1 = m1 * (1.0 / AW)
        m2 = m2 * (1.0 / AW)
        for g in range(G):
            cs = slice(g * GD, (g + 1) * GD)
            dgv = rstd * (dv_scr[:, cs] - m1 - vh_scr[:, cs] * m2)
            dz_ref[:, AW + g * GD:AW + (g + 1) * GD] = (dgv * dgv_scr[:, cs]).astype(BF16)

        @pl.when(i == nt - 1)
        def _():
            m = _causal_mask()
            for g in range(G):
                gws_ref[g] = jnp.where(m, gwc_acc[g], 0.0)
                gbst_ref[:, g:g + 1] = jnp.sum(gbs_acc[g], axis=-1, keepdims=True)

    tile = lambda w: pl.BlockSpec((tm, w), lambda i: (i, 0))
    whole = lambda *s: pl.BlockSpec(s, lambda i: (0,) * len(s))
    big = lambda dt: pltpu.VMEM((tm, AW), dt)
    return _call(
        main, jobs, name="bwd_a", grid=(nt,), relay_step=relay_step,
        ins=[dx1, z, lnw, lnb, ws, bst, wout], in_specs=[tile(D), tile(3 * AW), _VMEM, _VMEM, _VMEM, _VMEM, _VMEM],
        out_shape=[_sds((s_len, 3 * AW), BF16), _sds((1, AW), F32), _sds((1, AW), F32), _sds((G, CH, CH), F32),
                   _sds((CH, G), F32)],
        out_specs=[tile(3 * AW), whole(1, AW), whole(1, AW), whole(G, CH, CH), whole(CH, G)],
        scratch=[pltpu.VMEM((G, CH, CH), BF16), pltpu.VMEM((G, CH, CH), BF16), big(F32), big(F32), big(F32), big(F32),
                 pltpu.VMEM((G, CH, GD), F32), pltpu.VMEM((G, CH, CH), F32)])


def _bwd_a_in(dz, dx1, x, nw, win8, jobs, *, tm, relay_step):
    s_len = x.shape[0]
    nt = s_len // tm

    def main(i, ins, outs, scr):
        dz_ref, dx1_ref, x_ref, nw_ref, win_ref = ins
        gx_ref, gnw_ref = outs

        @pl.when(i == 0)
        def _():
            gnw_ref[...] = jnp.zeros_like(gnw_ref)

        dh = jnp.zeros((tm, D), F32)
        for k in range(NDEV):
            dh = dh + _dot_nt(dz_ref[:, k * CA:(k + 1) * CA], win_ref[k])
        x = x_ref[...]
        r = _rms(x)
        gx_ref[...] = dx1_ref[...] + _rms_bwd(dh, x, r, nw_ref[...])
        gnw_ref[...] += _rowsum(dh * x * r)

        @pl.when(i == nt - 1)
        def _():
            gnw_ref[...] = _direct_sum(gnw_ref[...], *scr)

    tile = lambda w: pl.BlockSpec((tm, w), lambda i: (i, 0))
    return _call(
        main, jobs, name="bwd_a_in", grid=(nt,), relay_step=relay_step,
        ins=[dz, dx1, x, nw, win8], in_specs=[tile(3 * AW), tile(D), tile(D), _VMEM, _VMEM],
        out_shape=[_sds((s_len, D), F32), _sds((1, D), F32)],
        out_specs=[tile(D), pl.BlockSpec((1, D), lambda i: (0, 0))], scratch=_direct_sum_scratch((1, D), F32))


def _conv(p8_ref, cs, xb, xm1, xm2, xm3):
    xc = p8_ref[4:5, cs] + p8_ref[3:4, cs] * xb
    xc = xc + p8_ref[0:1, cs] * xm3
    xc = xc + p8_ref[1:2, cs] * xm2
    return xc + p8_ref[2:3, cs] * xm1


def _gates(p8_ref, gcat_ref, hh, xc):
    cs = slice(hh * HD, (hh + 1) * HD)
    pre = _dot(xc.astype(BF16), gcat_ref[hh])
    r = _sigmoid(pre[:, :HD] + p8_ref[5:6, cs])
    ig = _sigmoid(pre[:, HD:] + p8_ref[6:7, cs])
    sp = _softplus_neg(p8_ref[7:8, cs])
    la = (-RG_C) * r * sp
    a = jnp.exp(la)
    half_log = 0.5 * jnp.log(jnp.tanh(-la) * (1.0 + a * a))
    return r, ig, sp, a, jnp.exp(half_log), jnp.exp(-half_log)


def _scan_rows(a_ref, b_ref, out_ref, carry, tm, reverse):
    row = lax.broadcasted_iota(jnp.int32, (SUBLANES, BW), 0)
    ngrp = tm // SUBLANES

    def step(j, cr):
        jj = (ngrp - 1 - j) if reverse else j
        off = pl.multiple_of(jj * SUBLANES, SUBLANES)
        a = a_ref[pl.ds(off, SUBLANES), :]
        b = b_ref[pl.ds(off, SUBLANES), :]
        for sh in (1, 2, 4):
            if reverse:
                a_s = pltpu.roll(a, SUBLANES - sh, 0)
                b_s = pltpu.roll(b, SUBLANES - sh, 0)
                m = row < SUBLANES - sh
            else:
                a_s = pltpu.roll(a, sh, 0)
                b_s = pltpu.roll(b, sh, 0)
                m = row >= sh
            b = jnp.where(m, a * b_s + b, b)
            a = jnp.where(m, a * a_s, a)
        o = b + a * cr
        out_ref[pl.ds(off, SUBLANES), :] = o
        return o[0:1, :] if reverse else o[SUBLANES - 1:SUBLANES, :]

    return lax.fori_loop(0, ngrp, step, carry)


def _fwd_b(x, ya, wout_a, nw, win8, p8, gcat, jobs, *, tm, relay_step):
    s_len = x.shape[0]
    nt = s_len // tm

    def main(i, ins, outs, scr):
        x_ref, ya_ref, wouta_ref, nw_ref, win_ref, p8_ref, gcat_ref = ins
        x1_ref, zb_ref, hs_ref, h1_ref, yb_ref, xc_ref, a_ref, cc_ref, r_ref, ig_ref, m_ref = outs
        xbe_scr, b_scr, k_scr, carry_scr = scr

        @pl.when(i == 0)
        def _():
            xbe_scr[0:SUBLANES, :] = jnp.zeros((SUBLANES, BW), F32)
            carry_scr[...] = jnp.zeros_like(carry_scr)

        x1 = x_ref[...] + _dot(ya_ref[...], wouta_ref[...])
        x1_ref[...] = x1
        h = (x1 * _rms(x1) * nw_ref[...]).astype(BF16)
        h1_ref[...] = h
        for k in range(NDEV):
            zb_ref[:, k * CB:(k + 1) * CB] = _dot(h, win_ref[k])
        xbe_scr[SUBLANES:SUBLANES + tm, :] = zb_ref[:, :BW]
        for hh in range(BH):
            cs = slice(hh * HD, (hh + 1) * HD)
            xc = _conv(p8_ref, cs, xbe_scr[SUBLANES:SUBLANES + tm, cs], xbe_scr[7:7 + tm, cs],
                       xbe_scr[6:6 + tm, cs], xbe_scr[5:5 + tm, cs])
            r, ig, _, a, mult, rm = _gates(p8_ref, gcat_ref, hh, xc)
            ixc = ig * xc
            xc_ref[:, cs] = xc
            a_ref[:, cs] = a
            r_ref[:, cs] = r.astype(BF16)
            ig_ref[:, cs] = ig.astype(BF16)
            m_ref[:, cs] = mult.astype(BF16)
            b_scr[:, cs] = mult * ixc
            k_scr[:, cs] = ixc * (a * a * rm)
        xbe_scr[0:SUBLANES, :] = xbe_scr[tm:tm + SUBLANES, :]
        carry_scr[...] = _scan_rows(a_ref, b_scr, hs_ref, carry_scr[...], tm, False)
        for hh in range(BH):
            cs = slice(hh * HD, (hh + 1) * HD)
            gt = zb_ref[:, BW + hh * HD:BW + (hh + 1) * HD]
            hsv = hs_ref[:, cs]
            yb_ref[:, cs] = (hsv * (gt * _sigmoid(gt))).astype(BF16)
            cc_ref[:, cs] = (hsv - b_scr[:, cs]) - k_scr[:, cs]

    tile = lambda w: pl.BlockSpec((tm, w), lambda i: (i, 0))
    wide = lambda dt: _sds((s_len, BW), dt)
    return _call(
        main, jobs, name="fwd_b", grid=(nt,), relay_step=relay_step,
        ins=[x, ya, wout_a, nw, win8, p8, gcat], in_specs=[tile(D), tile(AW), _VMEM, _VMEM, _VMEM, _VMEM, _VMEM],
        out_shape=[_sds((s_len, D), F32), _sds((s_len, 2 * BW), F32), wide(F32), _sds((s_len, D), BF16), wide(BF16),
                   wide(F32), wide(F32), wide(F32), wide(BF16), wide(BF16), wide(BF16)],
        out_specs=[tile(D), tile(2 * BW), tile(BW), tile(D)] + [tile(BW)] * 7,
        scratch=[pltpu.VMEM((tm + SUBLANES, BW), F32), pltpu.VMEM((tm, BW), F32), pltpu.VMEM((tm, BW), F32),
                 pltpu.VMEM((1, BW), F32)])


def _head(x1, yb, wout, nfw, tgt, *, tm):
    s_len = x1.shape[0]

    def main(i, ins, outs, scr):
        x1_ref, yb_ref, wout_ref, nfw_ref, t_ref = ins
        dx2_ref, dx2b_ref, loss_ref, gnfw_ref = outs

        @pl.when(i == 0)
        def _():
            loss_ref[...] = jnp.zeros_like(loss_ref)
            gnfw_ref[...] = jnp.zeros_like(gnfw_ref)

        x2 = x1_ref[...] + _dot(yb_ref[...], wout_ref[...])
        rf = _rms(x2)
        xn = x2 * rf
        e = xn * nfw_ref[...] - t_ref[...]
        loss_ref[...] += (0.5 / D) * jnp.sum(jnp.sum(e * e, axis=-1, keepdims=True), axis=0, keepdims=True)
        dyf = e * (1.0 / D)
        gnfw_ref[...] += _rowsum(dyf * xn)
        dx2 = _rms_bwd(dyf, x2, rf, nfw_ref[...])
        dx2_ref[...] = dx2
        dx2b_ref[...] = dx2.astype(BF16)

    tile = lambda w: pl.BlockSpec((tm, w), lambda i: (i, 0))
    whole = lambda *s: pl.BlockSpec(s, lambda i: (0,) * len(s))
    (dx2, dx2b, loss, gnfw), _ = _call(
        main, [], name="head", grid=(s_len // tm,),
        ins=[x1, yb, wout, nfw, tgt], in_specs=[tile(D), tile(BW), _VMEM, _VMEM, tile(D)],
        out_shape=[_sds((s_len, D), F32), _sds((s_len, D), BF16), _sds((1, 1), F32), _sds((1, D), F32)],
        out_specs=[tile(D), tile(D), whole(1, 1), whole(1, D)], scratch=[])
    return dx2, dx2b, loss, gnfw


def _bwd_b(dx2, zb, hs, x1, saved, nw, win8, p8, gcat, wout, *, tm):
    s_len = x1.shape[0]
    nt = s_len // tm

    def main(i, ins, outs, scr):
        (dx2_ref, zb_ref, hs_ref, x1_ref, xc_ref, a_ref, cc_ref, r_ref, ig_ref, m_ref,
         nw_ref, win_ref, p8_ref, gcat_ref, wout_ref) = ins
        dx1_ref, dx1b_ref, dzb_ref, gp8_ref, gga_ref, ggx_ref, gnw_ref = outs
        ae_scr, an_scr, dhd_scr, dh_scr, dy_scr, dxce_scr, carry_scr, afirst_scr = scr

        @pl.when(i == 0)
        def _():
            gp8_ref[...] = jnp.zeros_like(gp8_ref)
            gga_ref[...] = jnp.zeros_like(gga_ref)
            ggx_ref[...] = jnp.zeros_like(ggx_ref)
            gnw_ref[...] = jnp.zeros_like(gnw_ref)
            dxce_scr[tm:tm + SUBLANES, :] = jnp.zeros((SUBLANES, BW), F32)
            carry_scr[...] = jnp.zeros_like(carry_scr)
            afirst_scr[...] = jnp.zeros_like(afirst_scr)

        dx2 = dx2_ref[...]
        dy_scr[...] = _dot_nt(dx2.astype(BF16), wout_ref[...])
        for hh in range(BH):
            cs = slice(hh * HD, (hh + 1) * HD)
            gs = slice(BW + hh * HD, BW + (hh + 1) * HD)
            gt = zb_ref[:, gs]
            sig = _sigmoid(gt)
            dy = dy_scr[:, cs]
            dhd_scr[:, cs] = dy * (gt * sig)
            dzb_ref[:, gs] = (dy * hs_ref[:, cs] * (sig * (1.0 + gt * (1.0 - sig)))).astype(BF16)

        ae_scr[0:tm, :] = a_ref[...]
        ae_scr[tm:tm + SUBLANES, :] = jnp.broadcast_to(afirst_scr[...], (SUBLANES, BW))
        an_scr[...] = ae_scr[1:1 + tm, :]
        afirst_scr[...] = ae_scr[0:1, :]
        carry_scr[...] = _scan_rows(an_scr, dhd_scr, dh_scr, carry_scr[...], tm, True)

        for hh in range(BH):
            cs = slice(hh * HD, (hh + 1) * HD)
            dh = dh_scr[:, cs]
            mult = m_ref[:, cs].astype(F32)
            ig = ig_ref[:, cs].astype(F32)
            r = r_ref[:, cs].astype(F32)
            xc = xc_ref[:, cs]
            lam = p8_ref[7:8, cs]
            sp = _softplus_neg(lam)
            dla = dh * cc_ref[:, cs]
            gp8_ref[7:8, cs] += _rowsum(dla * ((-RG_C) * r)) * (-_sigmoid(-lam))
            dpr = dla * ((-RG_C) * sp) * (r * (1.0 - r))
            dpi = dh * mult * xc * (ig * (1.0 - ig))
            gp8_ref[5:6, cs] += _rowsum(dpr)
            gp8_ref[6:7, cs] += _rowsum(dpi)
            dcat = jnp.concatenate([dpr, dpi], axis=1).astype(BF16)
            dxc = dh * mult * ig + _dot_nt(dcat, gcat_ref[hh])
            gg = _dot(xc.T.astype(BF16), dcat)
            gga_ref[hh] += gg[:, :HD]
            ggx_ref[hh] += gg[:, HD:]
            dxce_scr[0:tm, cs] = dxc
            gp8_ref[4:5, cs] += _rowsum(dxc)
        for hh in range(BH):
            cs = slice(hh * HD, (hh + 1) * HD)
            xb = zb_ref[:, cs]
            d0, d1 = dxce_scr[0:tm, cs], dxce_scr[1:1 + tm, cs]
            d2, d3 = dxce_scr[2:2 + tm, cs], dxce_scr[3:3 + tm, cs]
            dzb_ref[:, cs] = (p8_ref[3:4, cs] * d0 + p8_ref[2:3, cs] * d1 + p8_ref[1:2, cs] * d2
                              + p8_ref[0:1, cs] * d3).astype(BF16)
            gp8_ref[3:4, cs] += _rowsum(d0 * xb)
            gp8_ref[2:3, cs] += _rowsum(d1 * xb)
            gp8_ref[1:2, cs] += _rowsum(d2 * xb)
            gp8_ref[0:1, cs] += _rowsum(d3 * xb)
        dxce_scr[tm:tm + SUBLANES, :] = dxce_scr[0:SUBLANES, :]

        dh1 = jnp.zeros((tm, D), F32)
        for k in range(NDEV):
            dh1 = dh1 + _dot_nt(dzb_ref[:, k * CB:(k + 1) * CB], win_ref[k])
        x1 = x1_ref[...]
        r1 = _rms(x1)
        dx1 = dx2 + _rms_bwd(dh1, x1, r1, nw_ref[...])
        dx1_ref[...] = dx1
        dx1b_ref[...] = dx1.astype(BF16)
        gnw_ref[...] += _rowsum(dh1 * x1 * r1)

    tile = lambda w: pl.BlockSpec((tm, w), lambda i: (nt - 1 - i, 0))
    whole = lambda *s: pl.BlockSpec(s, lambda i: (0,) * len(s))
    full = lambda: pltpu.VMEM((tm, BW), F32)
    ext = lambda: pltpu.VMEM((tm + SUBLANES, BW), F32)
    out, _ = _call(
        main, [], name="bwd_b", grid=(nt,),
        ins=[dx2, zb, hs, x1, *saved, nw, win8, p8, gcat, wout],
        in_specs=[tile(D), tile(2 * BW), tile(BW), tile(D)] + [tile(BW)] * 6 + [_VMEM] * 5,
        out_shape=[_sds((s_len, D), F32), _sds((s_len, D), BF16), _sds((s_len, 2 * BW), BF16), _sds((SUBLANES, BW), F32),
                   _sds((BH, HD, HD), F32), _sds((BH, HD, HD), F32), _sds((1, D), F32)],
        out_specs=[tile(D), tile(D), tile(2 * BW), whole(SUBLANES, BW), whole(BH, HD, HD), whole(BH, HD, HD),
                   whole(1, D)],
        scratch=[ext(), full(), full(), full(), full(), ext(), pltpu.VMEM((1, BW), F32), pltpu.VMEM((1, BW), F32)])
    return out


def _transpose_into(dst_ref, src_ref, rows):
    s_len = src_ref.shape[0]
    for r0 in range(0, s_len, rows):
        dst_ref[:, r0:r0 + rows] = src_ref[r0:r0 + rows, :].astype(F32).T.astype(BF16)


def _wgrad(a, b, jobs, *, by_rows, per, name, relay_step=0):
    s_len, m = a.shape
    n = b.shape[1]
    r, cd = (m // NDEV, n) if by_rows else (m, n // NDEV)
    nsteps = NDEV // per
    at_rows = per * r if by_rows else m

    def main(i, ins, outs, scr):
        a_ref, b_ref = ins
        q_ref, acc_ref = outs
        at_scr, stage, mine, land, send_sems, recv_sems = scr
        x, y, c = _place()

        def to_sibling(pi):
            return pltpu.make_async_remote_copy(
                src_ref=stage.at[pi & 1], dst_ref=land.at[pi], send_sem=send_sems.at[pi], recv_sem=recv_sems.at[pi],
                device_id=(x, y, 1 - c), device_id_type=MESH)

        if by_rows:
            _transpose_into(at_scr, a_ref, 256)
        else:
            @pl.when(i == 0)
            def _():
                _transpose_into(at_scr, a_ref, 256)

        res = _dot(at_scr[...], b_ref[...]).astype(BF16)
        for k in range(per):
            blk = per * i + k
            pi, pc = blk >> 1, blk & 1
            val = res[k * r:(k + 1) * r, :] if by_rows else res

            @pl.when(pc != c)
            def _():
                @pl.when(pi >= 2)
                def _():
                    to_sibling(pi - 2).wait_send()

                stage[pi & 1] = val
                to_sibling(pi).start()

            @pl.when(pc == c)
            def _():
                mine[pi] = val

        @pl.when(i == nsteps - 1)
        def _():
            for p in range(4):
                to_sibling(p).wait_recv()
            to_sibling(2).wait_send()
            to_sibling(3).wait_send()
            _chip_sums(mine, land, q_ref, acc_ref, x, y)

    if by_rows:
        in_specs = [pl.BlockSpec((s_len, at_rows), lambda j: (0, j)), _VMEM]
    else:
        in_specs = [_VMEM, pl.BlockSpec((s_len, cd), lambda j: (0, j))]
    blk_vmem = lambda k: pltpu.VMEM((k, r, cd), BF16)
    (q, acc), job_out = _call(
        main, jobs, name=name, grid=(nsteps,), relay_step=relay_step, ins=[a, b], in_specs=in_specs,
        out_shape=[_sds((NCHIP_OTHER, r, cd), BF16), _sds((r, cd), F32)],
        out_specs=[pl.BlockSpec((NCHIP_OTHER, r, cd), lambda j: (0, 0, 0)), pl.BlockSpec((r, cd), lambda j: (0, 0))],
        scratch=[pltpu.VMEM((at_rows, s_len), BF16), blk_vmem(2), blk_vmem(4), blk_vmem(4),
                 pltpu.SemaphoreType.DMA((4,)), pltpu.SemaphoreType.DMA((4,))])
    return q, acc, job_out


def _adam_math(w, g, m, v):
    m = B1 * m + (1.0 - B1) * g
    v = B2 * v + (1.0 - B2) * (g * g)
    m_hat = m / (1.0 - B1 ** STEP)
    v_hat = v / (1.0 - B2 ** STEP)
    delta = (-LR) * (m_hat / (jnp.sqrt(v_hat) + ADAM_EPS) + WD * w)
    return delta, m, v


def _adam_big(w, acc, land, m, v, name):
    r, cd = w.shape
    rb = 256 if r % 256 == 0 else r
    nland = land.shape[0]

    def body(w_ref, acc_ref, land_ref, m_ref, v_ref, g_ref, d_ref, mo_ref, vo_ref):
        g = acc_ref[...]
        for j in range(nland):
            g = g + land_ref[j].astype(F32)
        g_ref[...] = g
        d_ref[...], mo_ref[...], vo_ref[...] = _adam_math(w_ref[...], g, m_ref[...], v_ref[...])

    blk = pl.BlockSpec((rb, cd), lambda i: (i, 0))
    blk3 = pl.BlockSpec((nland, rb, cd), lambda i: (0, i, 0))
    return pl.pallas_call(
        body, name=name, grid=(r // rb,), in_specs=[blk, blk, blk3, blk, blk], out_specs=[blk] * 4,
        out_shape=[_sds((r, cd), F32)] * 4,
        compiler_params=_params(dimension_semantics=("arbitrary",)),
    )(w, acc, land, m, v)


def _adam_small(groups):
    n = len(groups)

    def body(*refs):
        ins, outs = refs[:4 * n], refs[4 * n:]
        for k in range(n):
            w_ref, g_ref, m_ref, v_ref = ins[4 * k:4 * k + 4]
            d, mo, vo = _adam_math(w_ref[...], g_ref[...], m_ref[...], v_ref[...])
            outs[3 * k][...] = d
            outs[3 * k + 1][...] = mo
            outs[3 * k + 2][...] = vo

    flat = [a for grp in groups for a in grp]
    shapes = [_sds(grp[0].shape, F32) for grp in groups for _ in range(3)]
    res = pl.pallas_call(
        body, name="adam_small", in_specs=[_VMEM] * (4 * n), out_specs=[_VMEM] * (3 * n), out_shape=shapes,
        compiler_params=_params(),
    )(*flat)
    return [tuple(res[3 * k:3 * k + 3]) for k in range(n)]


TM_FWD_A = 256
RELAY_STEP_FWD_A = 4
RELAY_STEP_FWD_B = 2
TM_BWD_A = 256
RELAY_STEP_BWD_A = 3
TM_BWD_A_IN = 256
RELAY_STEP_BWD_A_IN = 4
TM_FWD_B = 256
TM_HEAD = 512
TM_BWD_B = 256


def _pack(parts, rows):
    flat = jnp.concatenate([p.reshape(-1) for p in parts])
    return jnp.pad(flat, (0, NDEV * rows * LANES - flat.shape[0])).reshape(NDEV, rows, LANES)


def _unpack(packed, shapes):
    flat, out, off = packed.reshape(-1), [], 0
    for s in shapes:
        size = 1
        for d in s:
            size *= d
        out.append(flat[off:off + size].reshape(s))
        off += size
    return out


def kernel(x, norm_w, a_w_in, a_ln_w, a_ln_b, a_w_s, a_b_s, a_w_out, b_w_in, b_conv_w, b_conv_b, b_gate_a_w, b_gate_a_b, b_gate_x_w, b_gate_x_b, b_lambda, b_w_out, norm_f_w, loss_target, m_norm_w, m_a_w_in, m_a_ln_w, m_a_ln_b, m_a_w_s, m_a_b_s, m_a_w_out, m_b_w_in, m_b_conv_w, m_b_conv_b, m_b_gate_a_w, m_b_gate_a_b, m_b_gate_x_w, m_b_gate_x_b, m_b_lambda, m_b_w_out, m_norm_f_w, v_norm_w, v_a_w_in, v_a_ln_w, v_a_ln_b, v_a_w_s, v_a_b_s, v_a_w_out, v_b_w_in, v_b_conv_w, v_b_conv_b, v_b_gate_a_w, v_b_gate_a_b, v_b_gate_x_w, v_b_gate_x_b, v_b_lambda, v_b_w_out, v_norm_f_w):
    me = 4 * lax.axis_index("x") + 2 * lax.axis_index("y") + lax.axis_index("c")
    xs, tgt = x[0], loss_target[0]
    nw0, nw1, nfw = norm_w[0:1], norm_w[1:2], norm_f_w.reshape(1, D)
    w_s, bst = a_w_s[0], a_b_s[0].T
    gcat = jnp.concatenate([b_gate_a_w[0], b_gate_x_w[0]], axis=-1).astype(BF16)

    p8_shard = jnp.concatenate([b_conv_w[0], b_conv_b, b_gate_a_b, b_gate_x_b, b_lambda], axis=0)
    ((win_a8, p8_all),) = _comm_only([_Gather([a_w_in[0], p8_shard], [BF16, F32])], "gather_first")
    p8 = jnp.transpose(p8_all, (1, 0, 2)).reshape(SUBLANES, BW)

    (z, h0, ya), ((wout_a8, win_b8),) = _fwd_a(
        xs, nw0, win_a8, a_ln_w, a_ln_b, w_s, bst, [_Gather([a_w_out[0], b_w_in[0]], [BF16, BF16])],
        tm=TM_FWD_A, relay_step=RELAY_STEP_FWD_A)
    wout_a = wout_a8.reshape(AW, D)
    (x1, zb, hs, h1, yb, *saved_b), ((wout_b8,),) = _fwd_b(
        xs, ya, wout_a, nw1, win_b8, p8, gcat, [_Gather([b_w_out[0]], [BF16])],
        tm=TM_FWD_B, relay_step=RELAY_STEP_FWD_B)
    wout_b = wout_b8.reshape(BW, D)
    dx2, dx2b, loss, g_nfw = _head(x1, yb, wout_b, nfw, tgt, tm=TM_HEAD)

    dx1, dx1b, dzb, g_p8, g_ga, g_gx, g_nw1 = _bwd_b(dx2, zb, hs, x1, saved_b, nw1, win_b8, p8, gcat, wout_b,
                                                     tm=TM_BWD_B)
    q_wout_b, acc_wout_b, _ = _wgrad(yb, dx2b, [], by_rows=True, per=4, name="wgrad_b_out")
    shapes_b = [(1, D), (1, D), (SUBLANES, BW), (1, 1)]
    pack_b = _pack([g_nfw, g_nw1, g_p8, loss], 16)
    small_b = _InChip([g_ga.reshape(NDEV, -1, HD), g_gx.reshape(NDEV, -1, HD), pack_b])
    q_win_b, acc_win_b, (sm_b, (l_wout_b,)) = _wgrad(h1, dzb, [small_b, _Exchange([q_wout_b])], by_rows=False, per=1,
                                                      name="wgrad_b_in")
    qs_b, accs_b = sm_b[:3], sm_b[3:]

    (dz, g_lnw, g_lnb, g_ws, g_bst), (lands_b, (l_win_b,)) = _bwd_a(
        dx1b, z, a_ln_w, a_ln_b, w_s, bst, wout_a, [_Exchange(qs_b), _ExchangeVia(q_win_b)],
        tm=TM_BWD_A, relay_step=RELAY_STEP_BWD_A)
    shapes_a = [(1, AW), (1, AW), (CH, G)]
    pack_a = _pack([g_lnw, g_lnb, g_bst], 8)
    q_wout_a, acc_wout_a, (red_b, sm_a) = _wgrad(
        ya, dx1b, [_SumGather(accs_b, lands_b), _InChip([g_ws, pack_a])], by_rows=True, per=4,
        name="wgrad_a_out", relay_step=1)
    qs_a, accs_a = [q_wout_a, *sm_a[:2]], [acc_wout_a, *sm_a[2:]]
    q_win_a, acc_win_a, (lands_a,) = _wgrad(h0, dz, [_Exchange(qs_a)], by_rows=False, per=1, name="wgrad_a_in")
    (gx, g_nw0), (red_a, (l_win_a,)) = _bwd_a_in(
        dz, dx1, xs, nw0, win_a8, [_SumGather(accs_a[1:], lands_a[1:]), _ExchangeVia(q_win_a)],
        tm=TM_BWD_A_IN, relay_step=RELAY_STEP_BWD_A_IN)

    r_ga, r_gx, r_pack_b = red_b
    r_nfw, r_nw1, r_p8, loss = _unpack(r_pack_b, shapes_b)
    r_ws, r_pack_a = red_a
    r_lnw, r_lnb, r_bst = _unpack(r_pack_a, shapes_a)
    g_p8 = lax.dynamic_slice_in_dim(r_p8, me * (BW // NDEV), BW // NDEV, axis=1)
    loss = loss[0, 0]

    weights = dict(norm_w=norm_w, a_w_in=a_w_in, a_ln_w=a_ln_w, a_ln_b=a_ln_b, a_w_s=a_w_s, a_b_s=a_b_s, a_w_out=a_w_out,
                   b_w_in=b_w_in, b_conv_w=b_conv_w, b_conv_b=b_conv_b, b_gate_a_w=b_gate_a_w, b_gate_a_b=b_gate_a_b,
                   b_gate_x_w=b_gate_x_w, b_gate_x_b=b_gate_x_b, b_lambda=b_lambda, b_w_out=b_w_out, norm_f_w=norm_f_w)
    mom1 = dict(norm_w=m_norm_w, a_w_in=m_a_w_in, a_ln_w=m_a_ln_w, a_ln_b=m_a_ln_b, a_w_s=m_a_w_s, a_b_s=m_a_b_s,
                a_w_out=m_a_w_out, b_w_in=m_b_w_in, b_conv_w=m_b_conv_w, b_conv_b=m_b_conv_b, b_gate_a_w=m_b_gate_a_w,
                b_gate_a_b=m_b_gate_a_b, b_gate_x_w=m_b_gate_x_w, b_gate_x_b=m_b_gate_x_b, b_lambda=m_b_lambda,
                b_w_out=m_b_w_out, norm_f_w=m_norm_f_w)
    mom2 = dict(norm_w=v_norm_w, a_w_in=v_a_w_in, a_ln_w=v_a_ln_w, a_ln_b=v_a_ln_b, a_w_s=v_a_w_s, a_b_s=v_a_b_s,
                a_w_out=v_a_w_out, b_w_in=v_b_w_in, b_conv_w=v_b_conv_w, b_conv_b=v_b_conv_b, b_gate_a_w=v_b_gate_a_w,
                b_gate_a_b=v_b_gate_a_b, b_gate_x_w=v_b_gate_x_w, b_gate_x_b=v_b_gate_x_b, b_lambda=v_b_lambda,
                b_w_out=v_b_w_out, norm_f_w=v_norm_f_w)
    names = list(weights)

    def as2d(a):
        return a.reshape(-1, a.shape[-1])

    upd, grads = {}, {}
    for k, acc, land in (("a_w_in", acc_win_a, l_win_a), ("a_w_out", accs_a[0], lands_a[0]),
                         ("b_w_in", acc_win_b, l_win_b), ("b_w_out", acc_wout_b, l_wout_b)):
        g, d, mo, vo = _adam_big(as2d(weights[k]), acc, land, as2d(mom1[k]), as2d(mom2[k]), "adam_" + k)
        grads[k] = g[None]
        upd[k] = (d, mo, vo)
    grads.update(
        norm_w=jnp.concatenate([g_nw0, r_nw1], axis=0), a_ln_w=r_lnw, a_ln_b=r_lnb,
        a_w_s=r_ws.reshape(1, G, CH, CH), a_b_s=r_bst.T[None],
        b_conv_w=g_p8[None, 0:4], b_conv_b=g_p8[4:5], b_gate_a_w=r_ga.reshape(1, BH, HD, HD), b_gate_a_b=g_p8[5:6],
        b_gate_x_w=r_gx.reshape(1, BH, HD, HD), b_gate_x_b=g_p8[6:7], b_lambda=g_p8[7:8], norm_f_w=r_nfw.reshape(D))
    small_names = [k for k in names if k not in upd]
    res = _adam_small([(as2d(weights[k]), as2d(grads[k]), as2d(mom1[k]), as2d(mom2[k])) for k in small_names])
    for k, r3 in zip(small_names, res):
        upd[k] = r3
    deltas = [upd[k][0].reshape(weights[k].shape) for k in names]
    new_m = [upd[k][1].reshape(weights[k].shape) for k in names]
    new_v = [upd[k][2].reshape(weights[k].shape) for k in names]
    return (loss, gx[None], *[grads[k] for k in names], *deltas, *new_m, *new_v)
```

```python
import jax
import jax.numpy as jnp
from jax import lax
from jax.experimental import pallas as pl
from jax.experimental.pallas import tpu as pltpu

F32 = jnp.float32
BF16 = jnp.bfloat16
MESH = pl.DeviceIdType.MESH

NDEV = 8
NCHIP_OTHER = 3
D = 1024
AW = 2048
G = 8
GD = AW // G
CH = 128
BW = 1536
BH = 12
HD = BW // BH
CA = 3 * AW // NDEV
CB = 2 * BW // NDEV
RMS_EPS = 1e-6
LN_EPS = 1e-5
RG_C = 8.0
LR, B1, B2, ADAM_EPS, WD, STEP = 0.001, 0.9, 0.999, 1e-08, 0.01, 10
V7X_VMEM_BYTES = 64 * 1024 * 1024
VMEM_LIMIT = V7X_VMEM_BYTES - 8 * 1024 * 1024
SUBLANES = 8
LANES = 128
BF16_ROWS = 16
GELU_C = 0.7978845608028654
GELU_K = 0.044715

_VMEM = pl.BlockSpec(memory_space=pltpu.VMEM)
_HBM = pl.BlockSpec(memory_space=pltpu.HBM)


def _sds(shape, dtype):
    return jax.ShapeDtypeStruct(tuple(shape), dtype)


def _params(**kw):
    return pltpu.CompilerParams(vmem_limit_bytes=VMEM_LIMIT, **kw)


def _gelu_t(z):
    t = jnp.tanh(GELU_C * (z + GELU_K * (z * z * z)))
    return 0.5 * z * (1.0 + t), t


def _dgelu(z, t):
    return 0.5 * (1.0 + t) + 0.5 * z * (1.0 - t * t) * (GELU_C * (1.0 + 3.0 * GELU_K * z * z))


def _sigmoid(v):
    return 0.5 * jnp.tanh(0.5 * v) + 0.5


def _softplus_neg(lam):
    return jnp.maximum(-lam, 0.0) + jnp.log1p(jnp.exp(-jnp.abs(lam)))


def _dot(a, b):
    return jnp.dot(a, b, preferred_element_type=F32)


def _dot_nt(a, b):
    return lax.dot_general(a, b, (((1,), (1,)), ((), ())), preferred_element_type=F32)


def _rowsum(v):
    return jnp.sum(v, axis=0, keepdims=True)


def _causal_mask():
    r = lax.broadcasted_iota(jnp.int32, (CH, CH), 0)
    c = lax.broadcasted_iota(jnp.int32, (CH, CH), 1)
    return r >= c


def _rms(x):
    return lax.rsqrt(jnp.mean(x * x, axis=-1, keepdims=True) + RMS_EPS)


def _rms_bwd(dh, x, r, nw):
    gy = dh * nw
    return r * gy - x * (r * r * r) * jnp.mean(gy * x, axis=-1, keepdims=True)


def _place():
    return lax.axis_index("x"), lax.axis_index("y"), lax.axis_index("c")


def _other_chips(x, y):
    return [(1 - x, y), (x, 1 - y), (1 - x, 1 - y)]


GATHER_SLOTS = 10


def _gather_ops(ins, outs, send_sems, recv_sems, local_sems):
    n = len(ins)
    x, y, c = _place()
    sibling = (x, y, 1 - c)
    xn, yn, dg = _other_chips(x, y)
    split = [ins[i].shape[0] % (2 * BF16_ROWS) == 0 for i in range(n)]

    def blk(chip, core):
        return 4 * chip[0] + 2 * chip[1] + core

    me = blk((x, y), c)

    def part(ref, i, half):
        if half is None:
            return ref
        h = ins[i].shape[0] // 2
        return ref.at[pl.ds(half * h, h)]

    def copy(i, k, block, to, half=None, src=None):
        dst = part(outs[i].at[block], i, half)
        return pltpu.make_async_remote_copy(
            src_ref=dst if src is None else part(src, i, half), dst_ref=dst,
            send_sem=send_sems.at[k, i], recv_sem=recv_sems.at[k, i], device_id=to, device_id_type=MESH)

    def first_copies():
        mine = [pltpu.make_async_copy(ins[i], outs[i].at[me], local_sems.at[i]) for i in range(n)]
        first = []
        for i in range(n):
            first.append(copy(i, 0, me, sibling, src=ins[i]))
            if split[i]:
                first.append(copy(i, 1, me, (*xn, c), 0, ins[i]))
                first.append(copy(i, 3, me, (*yn, c), 1, ins[i]))
                first.append(copy(i, 2, me, (*xn, c), 1, ins[i]))
                first.append(copy(i, 4, me, (*yn, c), 0, ins[i]))
            else:
                first.append(copy(i, 1, me, (*xn, c), None, ins[i]))
                first.append(copy(i, 3, me, (*yn, c), None, ins[i]))
                first.append(copy(i, 5, me, (*dg, c), None, ins[i]))
        return mine, first

    def onward():
        out = []
        for i in range(n):
            if split[i]:
                out.append(copy(i, 5, blk(xn, c), (*yn, c), 0))
                out.append(copy(i, 6, blk(yn, c), (*xn, c), 1))
        return out

    def start():
        mine, first = first_copies()
        for cp in mine + first:
            cp.start()

    def relay():
        sends = onward()
        for i in range(n):
            if split[i]:
                copy(i, 1, blk(xn, c), sibling, 0).wait_recv()
                sends.pop(0).start()
                copy(i, 3, blk(yn, c), sibling, 1).wait_recv()
                sends.pop(0).start()

    def finish():
        mine, first = first_copies()
        passed = []

        def pass_on(i, j, chip):
            fwd = copy(i, 7 + j, blk(chip, c), sibling)
            fwd.start()
            passed.append(fwd)

        for i in range(n):
            if split[i]:
                copy(i, 2, blk(xn, c), sibling, 1).wait_recv()
                pass_on(i, 0, xn)
                copy(i, 4, blk(yn, c), sibling, 0).wait_recv()
                pass_on(i, 1, yn)
                copy(i, 5, blk(dg, c), sibling, 0).wait_recv()
                copy(i, 6, blk(dg, c), sibling, 1).wait_recv()
                pass_on(i, 2, dg)
            else:
                copy(i, 1, blk(xn, c), sibling).wait_recv()
                pass_on(i, 0, xn)
                copy(i, 3, blk(yn, c), sibling).wait_recv()
                pass_on(i, 1, yn)
                copy(i, 5, blk(dg, c), sibling).wait_recv()
                pass_on(i, 2, dg)
        for i in range(n):
            copy(i, 0, blk((x, y), 1 - c), sibling).wait_recv()
            for j, chip in enumerate((xn, yn, dg)):
                copy(i, 7 + j, blk(chip, 1 - c), sibling).wait_recv()
        for cp in first + passed + onward():
            cp.wait_send()
        for cp in mine:
            cp.wait()

    return start, relay, finish


def _gather_sems(n):
    return [pltpu.SemaphoreType.DMA((GATHER_SLOTS, n)), pltpu.SemaphoreType.DMA((GATHER_SLOTS, n)),
            pltpu.SemaphoreType.DMA((n,))]


class _Gather:
    def __init__(self, shards, as_dtypes=None):
        n = len(shards)
        dts = [s.dtype for s in shards] if as_dtypes is None else list(as_dtypes)
        self.cast = [jnp.dtype(d) != s.dtype for d, s in zip(dts, shards)]
        self.ins = list(shards)
        self.in_specs = [_VMEM if c else _HBM for c in self.cast]
        self.out_shape = [_sds((NDEV,) + s.shape, d) for s, d in zip(shards, dts)]
        self.out_specs = [_HBM] * n
        self.scratch = [pltpu.VMEM(s.shape, d) for s, d, c in zip(shards, dts, self.cast) if c] + _gather_sems(n)

    def ops(self, ins, outs, scr):
        ncast = sum(self.cast)
        staged = iter(scr[:ncast])
        srcs = [next(staged) if c else ref for c, ref in zip(self.cast, ins)]
        start, relay, finish = _gather_ops(srcs, outs, *scr[ncast:])

        def cast_and_start():
            for c, ref, src in zip(self.cast, ins, srcs):
                if c:
                    src[...] = ref[...].astype(src.dtype)
            start()

        return cast_and_start, relay, finish


class _Exchange:
    def __init__(self, qs):
        n = len(qs)
        self.ins, self.in_specs = list(qs), [_HBM] * n
        self.out_shape = [_sds(q.shape, q.dtype) for q in qs]
        self.out_specs = [_HBM] * n
        self.scratch = [pltpu.SemaphoreType.DMA((NCHIP_OTHER, n)), pltpu.SemaphoreType.DMA((NCHIP_OTHER, n))]

    def ops(self, ins, outs, scr):
        send_sems, recv_sems = scr
        n = len(ins)
        x, y, c = _place()
        chips = _other_chips(x, y)

        def copies():
            return [pltpu.make_async_remote_copy(
                src_ref=ins[i].at[j], dst_ref=outs[i].at[j], send_sem=send_sems.at[j, i],
                recv_sem=recv_sems.at[j, i], device_id=(*chips[j], c), device_id_type=MESH)
                for i in range(n) for j in range(NCHIP_OTHER)]

        def start():
            for cp in copies():
                cp.start()

        def finish():
            cps = copies()
            for cp in cps:
                cp.wait_recv()
            for cp in cps:
                cp.wait_send()

        return start, lambda: None, finish


class _ExchangeVia:
    def __init__(self, q):
        _, r, cd = q.shape
        half = (2, r // 2, cd)
        self.ins, self.in_specs = [q], [_HBM]
        self.out_shape, self.out_specs = [_sds((2, r, cd), q.dtype)], [_HBM]
        self.scratch = [pltpu.VMEM(half, q.dtype), pltpu.VMEM(half, q.dtype), pltpu.VMEM(half, q.dtype),
                        pltpu.SemaphoreType.DMA((6,)), pltpu.SemaphoreType.DMA((6,)), pltpu.SemaphoreType.DMA((2,))]

    def ops(self, ins, outs, scr):
        (q,), (land,) = ins, outs
        relayed, own, comb, send_sems, recv_sems, local_sems = scr
        h = q.shape[1] // 2
        x, y, c = _place()
        xn, yn, _ = _other_chips(x, y)
        h0, h1 = pl.ds(0, h), pl.ds(h, h)

        def remote(k, src, dst, chip):
            return pltpu.make_async_remote_copy(src_ref=src, dst_ref=dst, send_sem=send_sems.at[k],
                                                recv_sem=recv_sems.at[k], device_id=(*chip, c), device_id_type=MESH)

        def via():
            return [remote(2, q.at[2, h0], relayed.at[0], xn), remote(3, q.at[2, h1], relayed.at[1], yn)]

        def direct():
            return [remote(0, q.at[0, h0], land.at[0, h0], xn), remote(1, q.at[1, h1], land.at[1, h1], yn)]

        def second():
            return [remote(4, comb.at[0], land.at[1, h0], yn), remote(5, comb.at[1], land.at[0, h1], xn)]

        def mine():
            return [pltpu.make_async_copy(q.at[1, h0], own.at[0], local_sems.at[0]),
                    pltpu.make_async_copy(q.at[0, h1], own.at[1], local_sems.at[1])]

        def start():
            for cp in via() + direct() + mine():
                cp.start()

        def relay():
            arrived, loaded, onward = via(), mine(), second()
            for k in range(2):
                arrived[k].wait_recv()
                loaded[k].wait()
                comb[k] = (own[k].astype(F32) + relayed[k].astype(F32)).astype(comb.dtype)
                onward[k].start()

        def finish():
            landing = direct() + second()
            for cp in landing:
                cp.wait_recv()
            for cp in via() + landing:
                cp.wait_send()

        return start, relay, finish


class _SumGather:
    def __init__(self, accs, lands):
        n = len(accs)
        self.n = n
        self.ins, self.in_specs = list(accs) + list(lands), [_VMEM] * (2 * n)
        self.out_shape = [_sds((NDEV,) + a.shape, a.dtype) for a in accs]
        self.out_specs = [_HBM] * n
        self.scratch = [pltpu.VMEM(a.shape, a.dtype) for a in accs] + _gather_sems(n)

    def ops(self, ins, outs, scr):
        n = self.n
        accs, lands, mine = ins[:n], ins[n:], scr[:n]
        g_start, relay, finish = _gather_ops(mine, outs, *scr[n:])

        def start():
            for i in range(n):
                mine[i][...] = accs[i][...] + lands[i][0] + lands[i][1] + lands[i][2]
            g_start()

        return start, relay, finish


def _call(main, jobs, *, name, grid, ins, in_specs, out_shape, out_specs, scratch, relay_step=0):
    nsteps = grid[0] if grid else 1
    n_in, n_out, n_scr = len(ins), len(out_shape), len(scratch)

    def body(*refs):
        pos = [0]

        def take(k):
            r = refs[pos[0]:pos[0] + k]
            pos[0] += k
            return r

        m_in = take(n_in)
        j_in = [take(len(j.ins)) for j in jobs]
        m_out = take(n_out)
        j_out = [take(len(j.out_shape)) for j in jobs]
        m_scr = take(n_scr)
        j_scr = [take(len(j.scratch)) for j in jobs]
        ops = [j.ops(a, b, s) for j, a, b, s in zip(jobs, j_in, j_out, j_scr)]
        i = pl.program_id(0) if grid else 0
        if not grid:
            for o in ops:
                o[0]()
            main(i, m_in, m_out, m_scr)
            for o in ops:
                o[1]()
            for o in ops:
                o[2]()
            return

        if ops:
            @pl.when(i == 0)
            def _():
                for o in ops:
                    o[0]()

        main(i, m_in, m_out, m_scr)

        if ops:
            @pl.when(i == min(relay_step, nsteps - 1))
            def _():
                for o in ops:
                    o[1]()

            @pl.when(i == nsteps - 1)
            def _():
                for o in ops:
                    o[2]()

    extra = dict(dimension_semantics=("arbitrary",)) if grid else {}
    res = pl.pallas_call(
        body, name=name, grid=grid,
        in_specs=list(in_specs) + [s for j in jobs for s in j.in_specs],
        out_specs=list(out_specs) + [s for j in jobs for s in j.out_specs],
        out_shape=list(out_shape) + [s for j in jobs for s in j.out_shape],
        scratch_shapes=list(scratch) + [s for j in jobs for s in j.scratch],
        compiler_params=_params(**extra),
    )(*ins, *[a for j in jobs for a in j.ins])
    main_out, rest, job_out = res[:n_out], res[n_out:], []
    for j in jobs:
        k = len(j.out_shape)
        job_out.append(rest[:k])
        rest = rest[k:]
    return main_out, job_out


def _comm_only(jobs, name):
    _, job_out = _call(lambda i, a, b, s: None, jobs, name=name, grid=(), ins=[], in_specs=[], out_shape=[],
                       out_specs=[], scratch=[])
    return job_out


class _InChip:
    def __init__(self, ps):
        n = len(ps)
        self.n = n
        blk = [p.shape[1:] for p in ps]
        self.ins, self.in_specs = list(ps), [_HBM] * n
        self.out_shape = [_sds((NCHIP_OTHER,) + b, p.dtype) for b, p in zip(blk, ps)] + [_sds(b, F32) for b in blk]
        self.out_specs = [_VMEM] * (2 * n)
        self.scratch = ([pltpu.VMEM((4,) + b, p.dtype) for b, p in zip(blk, ps)] * 2
                        + [pltpu.SemaphoreType.DMA((4, n))] * 3)

    def ops(self, ins, outs, scr):
        n = self.n
        q_refs, acc_refs = outs[:n], outs[n:]
        mines, lands = scr[:n], scr[n:2 * n]
        send_sems, recv_sems, local_sems = scr[2 * n:]
        x, y, c = _place()
        sibling = (x, y, 1 - c)

        def copies():
            out = []
            for i in range(n):
                for pi in range(4):
                    loc = pltpu.make_async_copy(ins[i].at[2 * pi + c], mines[i].at[pi], local_sems.at[pi, i])
                    cp = pltpu.make_async_remote_copy(
                        src_ref=ins[i].at[2 * pi + (1 - c)], dst_ref=lands[i].at[pi],
                        send_sem=send_sems.at[pi, i], recv_sem=recv_sems.at[pi, i],
                        device_id=sibling, device_id_type=MESH)
                    out.append((loc, cp))
            return out

        def start():
            for loc, cp in copies():
                loc.start()
                cp.start()

        def finish():
            pairs = copies()
            for loc, cp in pairs:
                loc.wait()
                cp.wait_recv()
            for i in range(n):
                _chip_sums(mines[i], lands[i], q_refs[i], acc_refs[i], x, y)
            for _, cp in pairs:
                cp.wait_send()

        return start, lambda: None, finish


def _chip_sums(mine, land, q_ref, acc_ref, x, y):
    for j, (qx, qy) in enumerate(_other_chips(x, y)):
        qi = 2 * qx + qy
        q_ref[j] = (mine[qi].astype(F32) + land[qi].astype(F32)).astype(q_ref.dtype)
    mi = 2 * x + y
    acc_ref[...] = mine[mi].astype(F32) + land[mi].astype(F32)


def _direct_sum(v, buf, send_sems, recv_sems):
    x, y, c = _place()
    me = 4 * x + 2 * y + c
    buf[me] = v
    cps = []
    for k in range(1, NDEV):
        fx, fy, fc = (k >> 2) & 1, (k >> 1) & 1, k & 1
        peer = ((1 - x) if fx else x, (1 - y) if fy else y, (1 - c) if fc else c)
        cps.append((peer, pltpu.make_async_remote_copy(
            src_ref=buf.at[me], dst_ref=buf.at[me], send_sem=send_sems.at[k - 1], recv_sem=recv_sems.at[k - 1],
            device_id=peer, device_id_type=MESH)))
    for _, cp in cps:
        cp.start()
    for k, (peer, _) in enumerate(cps):
        theirs = 4 * peer[0] + 2 * peer[1] + peer[2]
        pltpu.make_async_remote_copy(
            src_ref=buf.at[theirs], dst_ref=buf.at[theirs], send_sem=send_sems.at[k], recv_sem=recv_sems.at[k],
            device_id=peer, device_id_type=MESH).wait_recv()
    acc = buf[0]
    for j in range(1, NDEV):
        acc = acc + buf[j]
    for _, cp in cps:
        cp.wait_send()
    return acc


def _direct_sum_scratch(shape, dtype):
    return [pltpu.VMEM((NDEV,) + tuple(shape), dtype), pltpu.SemaphoreType.DMA((NDEV - 1,)),
            pltpu.SemaphoreType.DMA((NDEV - 1,))]


def _fwd_a(x, nw, win8, lnw, lnb, ws, bst, jobs, *, tm, relay_step):
    s_len = x.shape[0]
    nt = s_len // tm
    nch = tm // CH

    def main(i, ins, outs, scr):
        x_ref, nw_ref, win_ref, lnw_ref, lnb_ref, ws_ref, bst_ref = ins
        z_ref, h_ref, y_ref = outs
        wc_scr, gv_scr = scr

        @pl.when(i == 0)
        def _():
            m = _causal_mask()
            for g in range(G):
                wc_scr[g] = jnp.where(m, ws_ref[g], 0.0).astype(BF16)

        x = x_ref[...]
        h = (x * _rms(x) * nw_ref[...]).astype(BF16)
        h_ref[...] = h
        for k in range(NDEV):
            z_ref[:, k * CA:(k + 1) * CA] = _dot(h, win_ref[k])

        ssum = jnp.zeros((tm, 1), F32)
        for g in range(G):
            gv = _gelu_t(z_ref[:, AW + g * GD:AW + (g + 1) * GD])[0]
            gv_scr[:, g * GD:(g + 1) * GD] = gv
            ssum = ssum + jnp.sum(gv, axis=-1, keepdims=True)
        mu = ssum * (1.0 / AW)
        vsum = jnp.zeros((tm, 1), F32)
        for g in range(G):
            dlt = gv_scr[:, g * GD:(g + 1) * GD] - mu
            vsum = vsum + jnp.sum(dlt * dlt, axis=-1, keepdims=True)
        rstd = lax.rsqrt(vsum * (1.0 / AW) + LN_EPS)

        for g in range(G):
            cs = slice(g * GD, (g + 1) * GD)
            v = (gv_scr[:, cs] - mu) * rstd * lnw_ref[:, cs] + lnb_ref[:, cs]
            vb = v.astype(BF16)
            u = _gelu_t(z_ref[:, cs])[0]
            zg = z_ref[:, 2 * AW + g * GD:2 * AW + (g + 1) * GD]
            sg = zg * _sigmoid(zg)
            for n in range(nch):
                rs = slice(n * CH, (n + 1) * CH)
                s = _dot(wc_scr[g], vb[rs, :]) + bst_ref[:, g:g + 1]
                y_ref[rs, cs] = (u[rs, :] * s * sg[rs, :]).astype(BF16)

    tile = lambda w: pl.BlockSpec((tm, w), lambda i: (i, 0))
    return _call(
        main, jobs, name="fwd_a", grid=(nt,), relay_step=relay_step,
        ins=[x, nw, win8, lnw, lnb, ws, bst], in_specs=[tile(D), _VMEM, _VMEM, _VMEM, _VMEM, _VMEM, _VMEM],
        out_shape=[_sds((s_len, 3 * AW), F32), _sds((s_len, D), BF16), _sds((s_len, AW), BF16)],
        out_specs=[tile(3 * AW), tile(D), tile(AW)],
        scratch=[pltpu.VMEM((G, CH, CH), BF16), pltpu.VMEM((tm, AW), F32)])


def _bwd_a(dx1, z, lnw, lnb, ws, bst, wout, jobs, *, tm, relay_step):
    s_len = dx1.shape[0]
    nt = s_len // tm
    nch = tm // CH

    def main(i, ins, outs, scr):
        dx1_ref, z_ref, lnw_ref, lnb_ref, ws_ref, bst_ref, wout_ref = ins
        dz_ref, glnw_ref, glnb_ref, gws_ref, gbst_ref = outs
        wc_scr, wct_scr, vh_scr, dgv_scr, dy_scr, dv_scr, gbs_acc, gwc_acc = scr

        @pl.when(i == 0)
        def _():
            m = _causal_mask()
            for g in range(G):
                wm = jnp.where(m, ws_ref[g], 0.0)
                wc_scr[g] = wm.astype(BF16)
                wct_scr[g] = wm.T.astype(BF16)
            glnw_ref[...] = jnp.zeros_like(glnw_ref)
            glnb_ref[...] = jnp.zeros_like(glnb_ref)
            gbs_acc[...] = jnp.zeros_like(gbs_acc)
            gwc_acc[...] = jnp.zeros_like(gwc_acc)

        dy_scr[...] = _dot_nt(dx1_ref[...], wout_ref[...])

        ssum = jnp.zeros((tm, 1), F32)
        for g in range(G):
            cs = slice(g * GD, (g + 1) * GD)
            zv = z_ref[:, AW + g * GD:AW + (g + 1) * GD]
            gv, t = _gelu_t(zv)
            vh_scr[:, cs] = gv
            dgv_scr[:, cs] = _dgelu(zv, t)
            ssum = ssum + jnp.sum(gv, axis=-1, keepdims=True)
        mu = ssum * (1.0 / AW)
        vsum = jnp.zeros((tm, 1), F32)
        for g in range(G):
            dlt = vh_scr[:, g * GD:(g + 1) * GD] - mu
            vsum = vsum + jnp.sum(dlt * dlt, axis=-1, keepdims=True)
        rstd = lax.rsqrt(vsum * (1.0 / AW) + LN_EPS)

        m1 = jnp.zeros((tm, 1), F32)
        m2 = jnp.zeros((tm, 1), F32)
        for g in range(G):
            cs = slice(g * GD, (g + 1) * GD)
            gs = slice(2 * AW + g * GD, 2 * AW + (g + 1) * GD)
            vhat = (vh_scr[:, cs] - mu) * rstd
            vh_scr[:, cs] = vhat
            vb = (vhat * lnw_ref[:, cs] + lnb_ref[:, cs]).astype(BF16)
            zu = z_ref[:, cs]
            u, tu = _gelu_t(zu)
            zg = z_ref[:, gs]
            sig = _sigmoid(zg)
            sg = zg * sig
            dy = dy_scr[:, cs]
            dsf = dy * u * sg
            dsb = dsf.astype(BF16)
            dvs = []
            for n in range(nch):
                rs = slice(n * CH, (n + 1) * CH)
                s = _dot(wc_scr[g], vb[rs, :]) + bst_ref[:, g:g + 1]
                dys = dy[rs, :] * s
                dz_ref[rs, cs] = (dys * sg[rs, :] * _dgelu(zu[rs, :], tu[rs, :])).astype(BF16)
                dz_ref[rs, gs] = (dys * u[rs, :] * (sig[rs, :] * (1.0 + zg[rs, :] * (1.0 - sig[rs, :])))).astype(BF16)
                gbs_acc[g] += dsf[rs, :]
                gwc_acc[g] += _dot_nt(dsb[rs, :], vb[rs, :])
                dvs.append(_dot(wct_scr[g], dsb[rs, :]))
            dv = jnp.concatenate(dvs, axis=0) if nch > 1 else dvs[0]
            glnw_ref[:, cs] += _rowsum(dv * vhat)
            glnb_ref[:, cs] += _rowsum(dv)
            dvh = dv * lnw_ref[:, cs]
            dv_scr[:, cs] = dvh
            m1 = m1 + jnp.sum(dvh, axis=-1, keepdims=True)
            m2 = m2 + jnp.sum(dvh * vhat, axis=-1, keepdims=True)
        m1 = m1 * (1.0 / AW)
        m2 = m2 * (1.0 / AW)
        for g in range(G):
            cs = slice(g * GD, (g + 1) * GD)
            dgv = rstd * (dv_scr[:, cs] - m1 - vh_scr[:, cs] * m2)
            dz_ref[:, AW + g * GD:AW + (g + 1) * GD] = (dgv * dgv_scr[:, cs]).astype(BF16)

        @pl.when(i == nt - 1)
        def _():
            m = _causal_mask()
            for g in range(G):
                gws_ref[g] = jnp.where(m, gwc_acc[g], 0.0)
                gbst_ref[:, g:g + 1] = jnp.sum(gbs_acc[g], axis=-1, keepdims=True)

    tile = lambda w: pl.BlockSpec((tm, w), lambda i: (i, 0))
    whole = lambda *s: pl.BlockSpec(s, lambda i: (0,) * len(s))
    big = lambda dt: pltpu.VMEM((tm, AW), dt)
    return _call(
        main, jobs, name="bwd_a", grid=(nt,), relay_step=relay_step,
        ins=[dx1, z, lnw, lnb, ws, bst, wout], in_specs=[tile(D), tile(3 * AW), _VMEM, _VMEM, _VMEM, _VMEM, _VMEM],
        out_shape=[_sds((s_len, 3 * AW), BF16), _sds((1, AW), F32), _sds((1, AW), F32), _sds((G, CH, CH), F32),
                   _sds((CH, G), F32)],
        out_specs=[tile(3 * AW), whole(1, AW), whole(1, AW), whole(G, CH, CH), whole(CH, G)],
        scratch=[pltpu.VMEM((G, CH, CH), BF16), pltpu.VMEM((G, CH, CH), BF16), big(F32), big(F32), big(F32), big(F32),
                 pltpu.VMEM((G, CH, GD), F32), pltpu.VMEM((G, CH, CH), F32)])


def _bwd_a_in(dz, dx1, x, nw, win8, jobs, *, tm, relay_step):
    s_len = x.shape[0]
    nt = s_len // tm

    def main(i, ins, outs, scr):
        dz_ref, dx1_ref, x_ref, nw_ref, win_ref = ins
        gx_ref, gnw_ref = outs

        @pl.when(i == 0)
        def _():
            gnw_ref[...] = jnp.zeros_like(gnw_ref)

        dh = jnp.zeros((tm, D), F32)
        for k in range(NDEV):
            dh = dh + _dot_nt(dz_ref[:, k * CA:(k + 1) * CA], win_ref[k])
        x = x_ref[...]
        r = _rms(x)
        gx_ref[...] = dx1_ref[...] + _rms_bwd(dh, x, r, nw_ref[...])
        gnw_ref[...] += _rowsum(dh * x * r)

        @pl.when(i == nt - 1)
        def _():
            gnw_ref[...] = _direct_sum(gnw_ref[...], *scr)

    tile = lambda w: pl.BlockSpec((tm, w), lambda i: (i, 0))
    return _call(
        main, jobs, name="bwd_a_in", grid=(nt,), relay_step=relay_step,
        ins=[dz, dx1, x, nw, win8], in_specs=[tile(3 * AW), tile(D), tile(D), _VMEM, _VMEM],
        out_shape=[_sds((s_len, D), F32), _sds((1, D), F32)],
        out_specs=[tile(D), pl.BlockSpec((1, D), lambda i: (0, 0))], scratch=_direct_sum_scratch((1, D), F32))


def _conv(p8_ref, cs, xb, xm1, xm2, xm3):
    xc = p8_ref[4:5, cs] + p8_ref[3:4, cs] * xb
    xc = xc + p8_ref[0:1, cs] * xm3
    xc = xc + p8_ref[1:2, cs] * xm2
    return xc + p8_ref[2:3, cs] * xm1


def _gates(p8_ref, gcat_ref, hh, xc):
    cs = slice(hh * HD, (hh + 1) * HD)
    pre = _dot(xc.astype(BF16), gcat_ref[hh])
    r = _sigmoid(pre[:, :HD] + p8_ref[5:6, cs])
    ig = _sigmoid(pre[:, HD:] + p8_ref[6:7, cs])
    sp = _softplus_neg(p8_ref[7:8, cs])
    la = (-RG_C) * r * sp
    a = jnp.exp(la)
    half_log = 0.5 * jnp.log(jnp.tanh(-la) * (1.0 + a * a))
    return r, ig, sp, a, jnp.exp(half_log), jnp.exp(-half_log)


def _scan_rows(a_ref, b_ref, out_ref, carry, tm, reverse):
    row = lax.broadcasted_iota(jnp.int32, (SUBLANES, BW), 0)
    ngrp = tm // SUBLANES

    def step(j, cr):
        jj = (ngrp - 1 - j) if reverse else j
        off = pl.multiple_of(jj * SUBLANES, SUBLANES)
        a = a_ref[pl.ds(off, SUBLANES), :]
        b = b_ref[pl.ds(off, SUBLANES), :]
        for sh in (1, 2, 4):
            if reverse:
                a_s = pltpu.roll(a, SUBLANES - sh, 0)
                b_s = pltpu.roll(b, SUBLANES - sh, 0)
                m = row < SUBLANES - sh
            else:
                a_s = pltpu.roll(a, sh, 0)
                b_s = pltpu.roll(b, sh, 0)
                m = row >= sh
            b = jnp.where(m, a * b_s + b, b)
            a = jnp.where(m, a * a_s, a)
        o = b + a * cr
        out_ref[pl.ds(off, SUBLANES), :] = o
        return o[0:1, :] if reverse else o[SUBLANES - 1:SUBLANES, :]

    return lax.fori_loop(0, ngrp, step, carry)


def _fwd_b(x, ya, wout_a, nw, win8, p8, gcat, jobs, *, tm, relay_step):
    s_len = x.shape[0]
    nt = s_len // tm

    def main(i, ins, outs, scr):
        x_ref, ya_ref, wouta_ref, nw_ref, win_ref, p8_ref, gcat_ref = ins
        x1_ref, zb_ref, hs_ref, h1_ref, yb_ref, xc_ref, a_ref, cc_ref, r_ref, ig_ref, m_ref = outs
        xbe_scr, b_scr, k_scr, carry_scr = scr

        @pl.when(i == 0)
        def _():
            xbe_scr[0:SUBLANES, :] = jnp.zeros((SUBLANES, BW), F32)
            carry_scr[...] = jnp.zeros_like(carry_scr)

        x1 = x_ref[...] + _dot(ya_ref[...], wouta_ref[...])
        x1_ref[...] = x1
        h = (x1 * _rms(x1) * nw_ref[...]).astype(BF16)
        h1_ref[...] = h
        for k in range(NDEV):
            zb_ref[:, k * CB:(k + 1) * CB] = _dot(h, win_ref[k])
        xbe_scr[SUBLANES:SUBLANES + tm, :] = zb_ref[:, :BW]
        for hh in range(BH):
            cs = slice(hh * HD, (hh + 1) * HD)
            xc = _conv(p8_ref, cs, xbe_scr[SUBLANES:SUBLANES + tm, cs], xbe_scr[7:7 + tm, cs],
                       xbe_scr[6:6 + tm, cs], xbe_scr[5:5 + tm, cs])
            r, ig, _, a, mult, rm = _gates(p8_ref, gcat_ref, hh, xc)
            ixc = ig * xc
            xc_ref[:, cs] = xc
            a_ref[:, cs] = a
            r_ref[:, cs] = r.astype(BF16)
            ig_ref[:, cs] = ig.astype(BF16)
            m_ref[:, cs] = mult.astype(BF16)
            b_scr[:, cs] = mult * ixc
            k_scr[:, cs] = ixc * (a * a * rm)
        xbe_scr[0:SUBLANES, :] = xbe_scr[tm:tm + SUBLANES, :]
        carry_scr[...] = _scan_rows(a_ref, b_scr, hs_ref, carry_scr[...], tm, False)
        for hh in range(BH):
            cs = slice(hh * HD, (hh + 1) * HD)
            gt = zb_ref[:, BW + hh * HD:BW + (hh + 1) * HD]
            hsv = hs_ref[:, cs]
            yb_ref[:, cs] = (hsv * (gt * _sigmoid(gt))).astype(BF16)
            cc_ref[:, cs] = (hsv - b_scr[:, cs]) - k_scr[:, cs]

    tile = lambda w: pl.BlockSpec((tm, w), lambda i: (i, 0))
    wide = lambda dt: _sds((s_len, BW), dt)
    return _call(
        main, jobs, name="fwd_b", grid=(nt,), relay_step=relay_step,
        ins=[x, ya, wout_a, nw, win8, p8, gcat], in_specs=[tile(D), tile(AW), _VMEM, _VMEM, _VMEM, _VMEM, _VMEM],
        out_shape=[_sds((s_len, D), F32), _sds((s_len, 2 * BW), F32), wide(F32), _sds((s_len, D), BF16), wide(BF16),
                   wide(F32), wide(F32), wide(F32), wide(BF16), wide(BF16), wide(BF16)],
        out_specs=[tile(D), tile(2 * BW), tile(BW), tile(D)] + [tile(BW)] * 7,
        scratch=[pltpu.VMEM((tm + SUBLANES, BW), F32), pltpu.VMEM((tm, BW), F32), pltpu.VMEM((tm, BW), F32),
                 pltpu.VMEM((1, BW), F32)])


def _head(x1, yb, wout, nfw, tgt, *, tm):
    s_len = x1.shape[0]

    def main(i, ins, outs, scr):
        x1_ref, yb_ref, wout_ref, nfw_ref, t_ref = ins
        dx2_ref, dx2b_ref, loss_ref, gnfw_ref = outs

        @pl.when(i == 0)
        def _():
            loss_ref[...] = jnp.zeros_like(loss_ref)
            gnfw_ref[...] = jnp.zeros_like(gnfw_ref)

        x2 = x1_ref[...] + _dot(yb_ref[...], wout_ref[...])
        rf = _rms(x2)
        xn = x2 * rf
        e = xn * nfw_ref[...] - t_ref[...]
        loss_ref[...] += (0.5 / D) * jnp.sum(jnp.sum(e * e, axis=-1, keepdims=True), axis=0, keepdims=True)
        dyf = e * (1.0 / D)
        gnfw_ref[...] += _rowsum(dyf * xn)
        dx2 = _rms_bwd(dyf, x2, rf, nfw_ref[...])
        dx2_ref[...] = dx2
        dx2b_ref[...] = dx2.astype(BF16)

    tile = lambda w: pl.BlockSpec((tm, w), lambda i: (i, 0))
    whole = lambda *s: pl.BlockSpec(s, lambda i: (0,) * len(s))
    (dx2, dx2b, loss, gnfw), _ = _call(
        main, [], name="head", grid=(s_len // tm,),
        ins=[x1, yb, wout, nfw, tgt], in_specs=[tile(D), tile(BW), _VMEM, _VMEM, tile(D)],
        out_shape=[_sds((s_len, D), F32), _sds((s_len, D), BF16), _sds((1, 1), F32), _sds((1, D), F32)],
        out_specs=[tile(D), tile(D), whole(1, 1), whole(1, D)], scratch=[])
    return dx2, dx2b, loss, gnfw


def _bwd_b(dx2, zb, hs, x1, saved, nw, win8, p8, gcat, wout, *, tm):
    s_len = x1.shape[0]
    nt = s_len // tm

    def main(i, ins, outs, scr):
        (dx2_ref, zb_ref, hs_ref, x1_ref, xc_ref, a_ref, cc_ref, r_ref, ig_ref, m_ref,
         nw_ref, win_ref, p8_ref, gcat_ref, wout_ref) = ins
        dx1_ref, dx1b_ref, dzb_ref, gp8_ref, gga_ref, ggx_ref, gnw_ref = outs
        ae_scr, an_scr, dhd_scr, dh_scr, dy_scr, dxce_scr, carry_scr, afirst_scr = scr

        @pl.when(i == 0)
        def _():
            gp8_ref[...] = jnp.zeros_like(gp8_ref)
            gga_ref[...] = jnp.zeros_like(gga_ref)
            ggx_ref[...] = jnp.zeros_like(ggx_ref)
            gnw_ref[...] = jnp.zeros_like(gnw_ref)
            dxce_scr[tm:tm + SUBLANES, :] = jnp.zeros((SUBLANES, BW), F32)
            carry_scr[...] = jnp.zeros_like(carry_scr)
            afirst_scr[...] = jnp.zeros_like(afirst_scr)

        dx2 = dx2_ref[...]
        dy_scr[...] = _dot_nt(dx2.astype(BF16), wout_ref[...])
        for hh in range(BH):
            cs = slice(hh * HD, (hh + 1) * HD)
            gs = slice(BW + hh * HD, BW + (hh + 1) * HD)
            gt = zb_ref[:, gs]
            sig = _sigmoid(gt)
            dy = dy_scr[:, cs]
            dhd_scr[:, cs] = dy * (gt * sig)
            dzb_ref[:, gs] = (dy * hs_ref[:, cs] * (sig * (1.0 + gt * (1.0 - sig)))).astype(BF16)

        ae_scr[0:tm, :] = a_ref[...]
        ae_scr[tm:tm + SUBLANES, :] = jnp.broadcast_to(afirst_scr[...], (SUBLANES, BW))
        an_scr[...] = ae_scr[1:1 + tm, :]
        afirst_scr[...] = ae_scr[0:1, :]
        carry_scr[...] = _scan_rows(an_scr, dhd_scr, dh_scr, carry_scr[...], tm, True)

        for hh in range(BH):
            cs = slice(hh * HD, (hh + 1) * HD)
            dh = dh_scr[:, cs]
            mult = m_ref[:, cs].astype(F32)
            ig = ig_ref[:, cs].astype(F32)
            r = r_ref[:, cs].astype(F32)
            xc = xc_ref[:, cs]
            lam = p8_ref[7:8, cs]
            sp = _softplus_neg(lam)
            dla = dh * cc_ref[:, cs]
            gp8_ref[7:8, cs] += _rowsum(dla * ((-RG_C) * r)) * (-_sigmoid(-lam))
            dpr = dla * ((-RG_C) * sp) * (r * (1.0 - r))
            dpi = dh * mult * xc * (ig * (1.0 - ig))
            gp8_ref[5:6, cs] += _rowsum(dpr)
            gp8_ref[6:7, cs] += _rowsum(dpi)
            dcat = jnp.concatenate([dpr, dpi], axis=1).astype(BF16)
            dxc = dh * mult * ig + _dot_nt(dcat, gcat_ref[hh])
            gg = _dot(xc.T.astype(BF16), dcat)
            gga_ref[hh] += gg[:, :HD]
            ggx_ref[hh] += gg[:, HD:]
            dxce_scr[0:tm, cs] = dxc
            gp8_ref[4:5, cs] += _rowsum(dxc)
        for hh in range(BH):
            cs = slice(hh * HD, (hh + 1) * HD)
            xb = zb_ref[:, cs]
            d0, d1 = dxce_scr[0:tm, cs], dxce_scr[1:1 + tm, cs]
            d2, d3 = dxce_scr[2:2 + tm, cs], dxce_scr[3:3 + tm, cs]
            dzb_ref[:, cs] = (p8_ref[3:4, cs] * d0 + p8_ref[2:3, cs] * d1 + p8_ref[1:2, cs] * d2
                              + p8_ref[0:1, cs] * d3).astype(BF16)
            gp8_ref[3:4, cs] += _rowsum(d0 * xb)
            gp8_ref[2:3, cs] += _rowsum(d1 * xb)
            gp8_ref[1:2, cs] += _rowsum(d2 * xb)
            gp8_ref[0:1, cs] += _rowsum(d3 * xb)
        dxce_scr[tm:tm + SUBLANES, :] = dxce_scr[0:SUBLANES, :]

        dh1 = jnp.zeros((tm, D), F32)
        for k in range(NDEV):
            dh1 = dh1 + _dot_nt(dzb_ref[:, k * CB:(k + 1) * CB], win_ref[k])
        x1 = x1_ref[...]
        r1 = _rms(x1)
        dx1 = dx2 + _rms_bwd(dh1, x1, r1, nw_ref[...])
        dx1_ref[...] = dx1
        dx1b_ref[...] = dx1.astype(BF16)
        gnw_ref[...] += _rowsum(dh1 * x1 * r1)

    tile = lambda w: pl.BlockSpec((tm, w), lambda i: (nt - 1 - i, 0))
    whole = lambda *s: pl.BlockSpec(s, lambda i: (0,) * len(s))
    full = lambda: pltpu.VMEM((tm, BW), F32)
    ext = lambda: pltpu.VMEM((tm + SUBLANES, BW), F32)
    out, _ = _call(
        main, [], name="bwd_b", grid=(nt,),
        ins=[dx2, zb, hs, x1, *saved, nw, win8, p8, gcat, wout],
        in_specs=[tile(D), tile(2 * BW), tile(BW), tile(D)] + [tile(BW)] * 6 + [_VMEM] * 5,
        out_shape=[_sds((s_len, D), F32), _sds((s_len, D), BF16), _sds((s_len, 2 * BW), BF16), _sds((SUBLANES, BW), F32),
                   _sds((BH, HD, HD), F32), _sds((BH, HD, HD), F32), _sds((1, D), F32)],
        out_specs=[tile(D), tile(D), tile(2 * BW), whole(SUBLANES, BW), whole(BH, HD, HD), whole(BH, HD, HD),
                   whole(1, D)],
        scratch=[ext(), full(), full(), full(), full(), ext(), pltpu.VMEM((1, BW), F32), pltpu.VMEM((1, BW), F32)])
    return out


def _transpose_into(dst_ref, src_ref, rows):
    s_len = src_ref.shape[0]
    for r0 in range(0, s_len, rows):
        dst_ref[:, r0:r0 + rows] = src_ref[r0:r0 + rows, :].astype(F32).T.astype(BF16)


def _wgrad(a, b, jobs, *, by_rows, per, name, relay_step=0):
    s_len, m = a.shape
    n = b.shape[1]
    r, cd = (m // NDEV, n) if by_rows else (m, n // NDEV)
    nsteps = NDEV // per
    at_rows = per * r if by_rows else m

    def main(i, ins, outs, scr):
        a_ref, b_ref = ins
        q_ref, acc_ref = outs
        at_scr, stage, mine, land, send_sems, recv_sems = scr
        x, y, c = _place()

        def to_sibling(pi):
            return pltpu.make_async_remote_copy(
                src_ref=stage.at[pi & 1], dst_ref=land.at[pi], send_sem=send_sems.at[pi], recv_sem=recv_sems.at[pi],
                device_id=(x, y, 1 - c), device_id_type=MESH)

        if by_rows:
            _transpose_into(at_scr, a_ref, 256)
        else:
            @pl.when(i == 0)
            def _():
                _transpose_into(at_scr, a_ref, 256)

        res = _dot(at_scr[...], b_ref[...]).astype(BF16)
        for k in range(per):
            blk = per * i + k
            pi, pc = blk >> 1, blk & 1
            val = res[k * r:(k + 1) * r, :] if by_rows else res

            @pl.when(pc != c)
            def _():
                @pl.when(pi >= 2)
                def _():
                    to_sibling(pi - 2).wait_send()

                stage[pi & 1] = val
                to_sibling(pi).start()

            @pl.when(pc == c)
            def _():
                mine[pi] = val

        @pl.when(i == nsteps - 1)
        def _():
            for p in range(4):
                to_sibling(p).wait_recv()
            to_sibling(2).wait_send()
            to_sibling(3).wait_send()
            _chip_sums(mine, land, q_ref, acc_ref, x, y)

    if by_rows:
        in_specs = [pl.BlockSpec((s_len, at_rows), lambda j: (0, j)), _VMEM]
    else:
        in_specs = [_VMEM, pl.BlockSpec((s_len, cd), lambda j: (0, j))]
    blk_vmem = lambda k: pltpu.VMEM((k, r, cd), BF16)
    (q, acc), job_out = _call(
        main, jobs, name=name, grid=(nsteps,), relay_step=relay_step, ins=[a, b], in_specs=in_specs,
        out_shape=[_sds((NCHIP_OTHER, r, cd), BF16), _sds((r, cd), F32)],
        out_specs=[pl.BlockSpec((NCHIP_OTHER, r, cd), lambda j: (0, 0, 0)), pl.BlockSpec((r, cd), lambda j: (0, 0))],
        scratch=[pltpu.VMEM((at_rows, s_len), BF16), blk_vmem(2), blk_vmem(4), blk_vmem(4),
                 pltpu.SemaphoreType.DMA((4,)), pltpu.SemaphoreType.DMA((4,))])
    return q, acc, job_out


def _adam_math(w, g, m, v):
    m = B1 * m + (1.0 - B1) * g
    v = B2 * v + (1.0 - B2) * (g * g)
    m_hat = m / (1.0 - B1 ** STEP)
    v_hat = v / (1.0 - B2 ** STEP)
    delta = (-LR) * (m_hat / (jnp.sqrt(v_hat) + ADAM_EPS) + WD * w)
    return delta, m, v


def _adam_big(w, acc, land, m, v, name):
    r, cd = w.shape
    rb = 256 if r % 256 == 0 else r
    nland = land.shape[0]

    def body(w_ref, acc_ref, land_ref, m_ref, v_ref, g_ref, d_ref, mo_ref, vo_ref):
        g = acc_ref[...]
        for j in range(nland):
            g = g + land_ref[j].astype(F32)
        g_ref[...] = g
        d_ref[...], mo_ref[...], vo_ref[...] = _adam_math(w_ref[...], g, m_ref[...], v_ref[...])

    blk = pl.BlockSpec((rb, cd), lambda i: (i, 0))
    blk3 = pl.BlockSpec((nland, rb, cd), lambda i: (0, i, 0))
    return pl.pallas_call(
        body, name=name, grid=(r // rb,), in_specs=[blk, blk, blk3, blk, blk], out_specs=[blk] * 4,
        out_shape=[_sds((r, cd), F32)] * 4,
        compiler_params=_params(dimension_semantics=("arbitrary",)),
    )(w, acc, land, m, v)


def _adam_small(groups):
    n = len(groups)

    def body(*refs):
        ins, outs = refs[:4 * n], refs[4 * n:]
        for k in range(n):
            w_ref, g_ref, m_ref, v_ref = ins[4 * k:4 * k + 4]
            d, mo, vo = _adam_math(w_ref[...], g_ref[...], m_ref[...], v_ref[...])
            outs[3 * k][...] = d
            outs[3 * k + 1][...] = mo
            outs[3 * k + 2][...] = vo

    flat = [a for grp in groups for a in grp]
    shapes = [_sds(grp[0].shape, F32) for grp in groups for _ in range(3)]
    res = pl.pallas_call(
        body, name="adam_small", in_specs=[_VMEM] * (4 * n), out_specs=[_VMEM] * (3 * n), out_shape=shapes,
        compiler_params=_params(),
    )(*flat)
    return [tuple(res[3 * k:3 * k + 3]) for k in range(n)]


TM_FWD_A = 256
RELAY_STEP_FWD_A = 4
RELAY_STEP_FWD_B = 2
TM_BWD_A = 256
RELAY_STEP_BWD_A = 3
TM_BWD_A_IN = 256
RELAY_STEP_BWD_A_IN = 4
TM_FWD_B = 256
TM_HEAD = 512
TM_BWD_B = 256


def _pack(parts, rows):
    flat = jnp.concatenate([p.reshape(-1) for p in parts])
    return jnp.pad(flat, (0, NDEV * rows * LANES - flat.shape[0])).reshape(NDEV, rows, LANES)


def _unpack(packed, shapes):
    flat, out, off = packed.reshape(-1), [], 0
    for s in shapes:
        size = 1
        for d in s:
            size *= d
        out.append(flat[off:off + size].reshape(s))
        off += size
    return out


def kernel(x, norm_w, a_w_in, a_ln_w, a_ln_b, a_w_s, a_b_s, a_w_out, b_w_in, b_conv_w, b_conv_b, b_gate_a_w, b_gate_a_b, b_gate_x_w, b_gate_x_b, b_lambda, b_w_out, norm_f_w, loss_target, m_norm_w, m_a_w_in, m_a_ln_w, m_a_ln_b, m_a_w_s, m_a_b_s, m_a_w_out, m_b_w_in, m_b_conv_w, m_b_conv_b, m_b_gate_a_w, m_b_gate_a_b, m_b_gate_x_w, m_b_gate_x_b, m_b_lambda, m_b_w_out, m_norm_f_w, v_norm_w, v_a_w_in, v_a_ln_w, v_a_ln_b, v_a_w_s, v_a_b_s, v_a_w_out, v_b_w_in, v_b_conv_w, v_b_conv_b, v_b_gate_a_w, v_b_gate_a_b, v_b_gate_x_w, v_b_gate_x_b, v_b_lambda, v_b_w_out, v_norm_f_w):
    me = 4 * lax.axis_index("x") + 2 * lax.axis_index("y") + lax.axis_index("c")
    xs, tgt = x[0], loss_target[0]
    nw0, nw1, nfw = norm_w[0:1], norm_w[1:2], norm_f_w.reshape(1, D)
    w_s, bst = a_w_s[0], a_b_s[0].T
    gcat = jnp.concatenate([b_gate_a_w[0], b_gate_x_w[0]], axis=-1).astype(BF16)

    p8_shard = jnp.concatenate([b_conv_w[0], b_conv_b, b_gate_a_b, b_gate_x_b, b_lambda], axis=0)
    ((win_a8, p8_all),) = _comm_only([_Gather([a_w_in[0], p8_shard], [BF16, F32])], "gather_first")
    p8 = jnp.transpose(p8_all, (1, 0, 2)).reshape(SUBLANES, BW)

    (z, h0, ya), ((wout_a8, win_b8),) = _fwd_a(
        xs, nw0, win_a8, a_ln_w, a_ln_b, w_s, bst, [_Gather([a_w_out[0], b_w_in[0]], [BF16, BF16])],
        tm=TM_FWD_A, relay_step=RELAY_STEP_FWD_A)
    wout_a = wout_a8.reshape(AW, D)
    (x1, zb, hs, h1, yb, *saved_b), ((wout_b8,),) = _fwd_b(
        xs, ya, wout_a, nw1, win_b8, p8, gcat, [_Gather([b_w_out[0]], [BF16])],
        tm=TM_FWD_B, relay_step=RELAY_STEP_FWD_B)
    wout_b = wout_b8.reshape(BW, D)
    dx2, dx2b, loss, g_nfw = _head(x1, yb, wout_b, nfw, tgt, tm=TM_HEAD)

    dx1, dx1b, dzb, g_p8, g_ga, g_gx, g_nw1 = _bwd_b(dx2, zb, hs, x1, saved_b, nw1, win_b8, p8, gcat, wout_b,
                                                     tm=TM_BWD_B)
    q_wout_b, acc_wout_b, _ = _wgrad(yb, dx2b, [], by_rows=True, per=2, name="wgrad_b_out")
    shapes_b = [(1, D), (1, D), (SUBLANES, BW), (1, 1)]
    pack_b = _pack([g_nfw, g_nw1, g_p8, loss], 16)
    small_b = _InChip([g_ga.reshape(NDEV, -1, HD), g_gx.reshape(NDEV, -1, HD), pack_b])
    q_win_b, acc_win_b, (sm_b, (l_wout_b,)) = _wgrad(h1, dzb, [small_b, _Exchange([q_wout_b])], by_rows=False, per=1,
                                                      name="wgrad_b_in")
    qs_b, accs_b = sm_b[:3], sm_b[3:]

    (dz, g_lnw, g_lnb, g_ws, g_bst), (lands_b, (l_win_b,)) = _bwd_a(
        dx1b, z, a_ln_w, a_ln_b, w_s, bst, wout_a, [_Exchange(qs_b), _ExchangeVia(q_win_b)],
        tm=TM_BWD_A, relay_step=RELAY_STEP_BWD_A)
    shapes_a = [(1, AW), (1, AW), (CH, G)]
    pack_a = _pack([g_lnw, g_lnb, g_bst], 8)
    q_wout_a, acc_wout_a, (red_b, sm_a) = _wgrad(
        ya, dx1b, [_SumGather(accs_b, lands_b), _InChip([g_ws, pack_a])], by_rows=True, per=2,
        name="wgrad_a_out", relay_step=1)
    qs_a, accs_a = [q_wout_a, *sm_a[:2]], [acc_wout_a, *sm_a[2:]]
    q_win_a, acc_win_a, (lands_a,) = _wgrad(h0, dz, [_Exchange(qs_a)], by_rows=False, per=1, name="wgrad_a_in")
    (gx, g_nw0), (red_a, (l_win_a,)) = _bwd_a_in(
        dz, dx1, xs, nw0, win_a8, [_SumGather(accs_a[1:], lands_a[1:]), _ExchangeVia(q_win_a)],
        tm=TM_BWD_A_IN, relay_step=RELAY_STEP_BWD_A_IN)

    r_ga, r_gx, r_pack_b = red_b
    r_nfw, r_nw1, r_p8, loss = _unpack(r_pack_b, shapes_b)
    r_ws, r_pack_a = red_a
    r_lnw, r_lnb, r_bst = _unpack(r_pack_a, shapes_a)
    g_p8 = lax.dynamic_slice_in_dim(r_p8, me * (BW // NDEV), BW // NDEV, axis=1)
    loss = loss[0, 0]

    weights = dict(norm_w=norm_w, a_w_in=a_w_in, a_ln_w=a_ln_w, a_ln_b=a_ln_b, a_w_s=a_w_s, a_b_s=a_b_s, a_w_out=a_w_out,
                   b_w_in=b_w_in, b_conv_w=b_conv_w, b_conv_b=b_conv_b, b_gate_a_w=b_gate_a_w, b_gate_a_b=b_gate_a_b,
                   b_gate_x_w=b_gate_x_w, b_gate_x_b=b_gate_x_b, b_lambda=b_lambda, b_w_out=b_w_out, norm_f_w=norm_f_w)
    mom1 = dict(norm_w=m_norm_w, a_w_in=m_a_w_in, a_ln_w=m_a_ln_w, a_ln_b=m_a_ln_b, a_w_s=m_a_w_s, a_b_s=m_a_b_s,
                a_w_out=m_a_w_out, b_w_in=m_b_w_in, b_conv_w=m_b_conv_w, b_conv_b=m_b_conv_b, b_gate_a_w=m_b_gate_a_w,
                b_gate_a_b=m_b_gate_a_b, b_gate_x_w=m_b_gate_x_w, b_gate_x_b=m_b_gate_x_b, b_lambda=m_b_lambda,
                b_w_out=m_b_w_out, norm_f_w=m_norm_f_w)
    mom2 = dict(norm_w=v_norm_w, a_w_in=v_a_w_in, a_ln_w=v_a_ln_w, a_ln_b=v_a_ln_b, a_w_s=v_a_w_s, a_b_s=v_a_b_s,
                a_w_out=v_a_w_out, b_w_in=v_b_w_in, b_conv_w=v_b_conv_w, b_conv_b=v_b_conv_b, b_gate_a_w=v_b_gate_a_w,
                b_gate_a_b=v_b_gate_a_b, b_gate_x_w=v_b_gate_x_w, b_gate_x_b=v_b_gate_x_b, b_lambda=v_b_lambda,
                b_w_out=v_b_w_out, norm_f_w=v_norm_f_w)
    names = list(weights)

    def as2d(a):
        return a.reshape(-1, a.shape[-1])

    upd, grads = {}, {}
    for k, acc, land in (("a_w_in", acc_win_a, l_win_a), ("a_w_out", accs_a[0], lands_a[0]),
                         ("b_w_in", acc_win_b, l_win_b), ("b_w_out", acc_wout_b, l_wout_b)):
        g, d, mo, vo = _adam_big(as2d(weights[k]), acc, land, as2d(mom1[k]), as2d(mom2[k]), "adam_" + k)
        grads[k] = g[None]
        upd[k] = (d, mo, vo)
    grads.update(
        norm_w=jnp.concatenate([g_nw0, r_nw1], axis=0), a_ln_w=r_lnw, a_ln_b=r_lnb,
        a_w_s=r_ws.reshape(1, G, CH, CH), a_b_s=r_bst.T[None],
        b_conv_w=g_p8[None, 0:4], b_conv_b=g_p8[4:5], b_gate_a_w=r_ga.reshape(1, BH, HD, HD), b_gate_a_b=g_p8[5:6],
        b_gate_x_w=r_gx.reshape(1, BH, HD, HD), b_gate_x_b=g_p8[6:7], b_lambda=g_p8[7:8], norm_f_w=r_nfw.reshape(D))
    small_names = [k for k in names if k not in upd]
    res = _adam_small([(as2d(weights[k]), as2d(grads[k]), as2d(mom1[k]), as2d(mom2[k])) for k in small_names])
    for k, r3 in zip(small_names, res):
        upd[k] = r3
    deltas = [upd[k][0].reshape(weights[k].shape) for k in names]
    new_m = [upd[k][1].reshape(weights[k].shape) for k in names]
    new_v = [upd[k][2].reshape(weights[k].shape) for k in names]
    return (loss, gx[None], *[grads[k] for k in names], *deltas, *new_m, *new_v)
```

```python
import jax
import jax.numpy as jnp
from jax import lax
from jax.experimental import pallas as pl
from jax.experimental.pallas import tpu as pltpu

F32 = jnp.float32
BF16 = jnp.bfloat16
MESH = pl.DeviceIdType.MESH

NDEV = 8
NCHIP_OTHER = 3
D = 1024
AW = 2048
G = 8
GD = AW // G
CH = 128
BW = 1536
BH = 12
HD = BW // BH
CA = 3 * AW // NDEV
CB = 2 * BW // NDEV
RMS_EPS = 1e-6
LN_EPS = 1e-5
RG_C = 8.0
LR, B1, B2, ADAM_EPS, WD, STEP = 0.001, 0.9, 0.999, 1e-08, 0.01, 10
V7X_VMEM_BYTES = 64 * 1024 * 1024
VMEM_LIMIT = V7X_VMEM_BYTES - 8 * 1024 * 1024
SUBLANES = 8
LANES = 128
BF16_ROWS = 16
GELU_C = 0.7978845608028654
GELU_K = 0.044715

_VMEM = pl.BlockSpec(memory_space=pltpu.VMEM)
_HBM = pl.BlockSpec(memory_space=pltpu.HBM)


def _sds(shape, dtype):
    return jax.ShapeDtypeStruct(tuple(shape), dtype)


def _params(**kw):
    return pltpu.CompilerParams(vmem_limit_bytes=VMEM_LIMIT, **kw)


def _gelu_t(z):
    t = jnp.tanh(GELU_C * (z + GELU_K * (z * z * z)))
    return 0.5 * z * (1.0 + t), t


def _dgelu(z, t):
    return 0.5 * (1.0 + t) + 0.5 * z * (1.0 - t * t) * (GELU_C * (1.0 + 3.0 * GELU_K * z * z))


def _sigmoid(v):
    return 0.5 * jnp.tanh(0.5 * v) + 0.5


def _softplus_neg(lam):
    return jnp.maximum(-lam, 0.0) + jnp.log1p(jnp.exp(-jnp.abs(lam)))


def _dot(a, b):
    return jnp.dot(a, b, preferred_element_type=F32)


def _dot_nt(a, b):
    return lax.dot_general(a, b, (((1,), (1,)), ((), ())), preferred_element_type=F32)


def _rowsum(v):
    return jnp.sum(v, axis=0, keepdims=True)


def _causal_mask():
    r = lax.broadcasted_iota(jnp.int32, (CH, CH), 0)
    c = lax.broadcasted_iota(jnp.int32, (CH, CH), 1)
    return r >= c


def _rms(x):
    return lax.rsqrt(jnp.mean(x * x, axis=-1, keepdims=True) + RMS_EPS)


def _rms_bwd(dh, x, r, nw):
    gy = dh * nw
    return r * gy - x * (r * r * r) * jnp.mean(gy * x, axis=-1, keepdims=True)


def _place():
    return lax.axis_index("x"), lax.axis_index("y"), lax.axis_index("c")


def _other_chips(x, y):
    return [(1 - x, y), (x, 1 - y), (1 - x, 1 - y)]


GATHER_SLOTS = 10


def _gather_ops(ins, outs, send_sems, recv_sems, local_sems):
    n = len(ins)
    x, y, c = _place()
    sibling = (x, y, 1 - c)
    xn, yn, dg = _other_chips(x, y)
    split = [ins[i].shape[0] % (2 * BF16_ROWS) == 0 for i in range(n)]

    def blk(chip, core):
        return 4 * chip[0] + 2 * chip[1] + core

    me = blk((x, y), c)

    def part(ref, i, half):
        if half is None:
            return ref
        h = ins[i].shape[0] // 2
        return ref.at[pl.ds(half * h, h)]

    def copy(i, k, block, to, half=None, src=None):
        dst = part(outs[i].at[block], i, half)
        return pltpu.make_async_remote_copy(
            src_ref=dst if src is None else part(src, i, half), dst_ref=dst,
            send_sem=send_sems.at[k, i], recv_sem=recv_sems.at[k, i], device_id=to, device_id_type=MESH)

    def first_copies():
        mine = [pltpu.make_async_copy(ins[i], outs[i].at[me], local_sems.at[i]) for i in range(n)]
        first = []
        for i in range(n):
            first.append(copy(i, 0, me, sibling, src=ins[i]))
            if split[i]:
                first.append(copy(i, 1, me, (*xn, c), 0, ins[i]))
                first.append(copy(i, 3, me, (*yn, c), 1, ins[i]))
                first.append(copy(i, 2, me, (*xn, c), 1, ins[i]))
                first.append(copy(i, 4, me, (*yn, c), 0, ins[i]))
            else:
                first.append(copy(i, 1, me, (*xn, c), None, ins[i]))
                first.append(copy(i, 3, me, (*yn, c), None, ins[i]))
                first.append(copy(i, 5, me, (*dg, c), None, ins[i]))
        return mine, first

    def onward():
        out = []
        for i in range(n):
            if split[i]:
                out.append(copy(i, 5, blk(xn, c), (*yn, c), 0))
                out.append(copy(i, 6, blk(yn, c), (*xn, c), 1))
        return out

    def start():
        mine, first = first_copies()
        for cp in mine + first:
            cp.start()

    def relay():
        sends = onward()
        for i in range(n):
            if split[i]:
                copy(i, 1, blk(xn, c), sibling, 0).wait_recv()
                sends.pop(0).start()
                copy(i, 3, blk(yn, c), sibling, 1).wait_recv()
                sends.pop(0).start()

    def finish():
        mine, first = first_copies()
        passed = []

        def pass_on(i, j, chip):
            fwd = copy(i, 7 + j, blk(chip, c), sibling)
            fwd.start()
            passed.append(fwd)

        for i in range(n):
            if split[i]:
                copy(i, 2, blk(xn, c), sibling, 1).wait_recv()
                pass_on(i, 0, xn)
                copy(i, 4, blk(yn, c), sibling, 0).wait_recv()
                pass_on(i, 1, yn)
                copy(i, 5, blk(dg, c), sibling, 0).wait_recv()
                copy(i, 6, blk(dg, c), sibling, 1).wait_recv()
                pass_on(i, 2, dg)
            else:
                copy(i, 1, blk(xn, c), sibling).wait_recv()
                pass_on(i, 0, xn)
                copy(i, 3, blk(yn, c), sibling).wait_recv()
                pass_on(i, 1, yn)
                copy(i, 5, blk(dg, c), sibling).wait_recv()
                pass_on(i, 2, dg)
        for i in range(n):
            copy(i, 0, blk((x, y), 1 - c), sibling).wait_recv()
            for j, chip in enumerate((xn, yn, dg)):
                copy(i, 7 + j, blk(chip, 1 - c), sibling).wait_recv()
        for cp in first + passed + onward():
            cp.wait_send()
        for cp in mine:
            cp.wait()

    return start, relay, finish


def _gather_sems(n):
    return [pltpu.SemaphoreType.DMA((GATHER_SLOTS, n)), pltpu.SemaphoreType.DMA((GATHER_SLOTS, n)),
            pltpu.SemaphoreType.DMA((n,))]


class _Gather:
    def __init__(self, shards, as_dtypes=None):
        n = len(shards)
        dts = [s.dtype for s in shards] if as_dtypes is None else list(as_dtypes)
        self.cast = [jnp.dtype(d) != s.dtype for d, s in zip(dts, shards)]
        self.ins = list(shards)
        self.in_specs = [_VMEM if c else _HBM for c in self.cast]
        self.out_shape = [_sds((NDEV,) + s.shape, d) for s, d in zip(shards, dts)]
        self.out_specs = [_HBM] * n
        self.scratch = [pltpu.VMEM(s.shape, d) for s, d, c in zip(shards, dts, self.cast) if c] + _gather_sems(n)

    def ops(self, ins, outs, scr):
        ncast = sum(self.cast)
        staged = iter(scr[:ncast])
        srcs = [next(staged) if c else ref for c, ref in zip(self.cast, ins)]
        start, relay, finish = _gather_ops(srcs, outs, *scr[ncast:])

        def cast_and_start():
            for c, ref, src in zip(self.cast, ins, srcs):
                if c:
                    src[...] = ref[...].astype(src.dtype)
            start()

        return cast_and_start, relay, finish


class _Exchange:
    def __init__(self, qs):
        n = len(qs)
        self.ins, self.in_specs = list(qs), [_HBM] * n
        self.out_shape = [_sds(q.shape, q.dtype) for q in qs]
        self.out_specs = [_HBM] * n
        self.scratch = [pltpu.SemaphoreType.DMA((NCHIP_OTHER, n)), pltpu.SemaphoreType.DMA((NCHIP_OTHER, n))]

    def ops(self, ins, outs, scr):
        send_sems, recv_sems = scr
        n = len(ins)
        x, y, c = _place()
        chips = _other_chips(x, y)

        def copies():
            return [pltpu.make_async_remote_copy(
                src_ref=ins[i].at[j], dst_ref=outs[i].at[j], send_sem=send_sems.at[j, i],
                recv_sem=recv_sems.at[j, i], device_id=(*chips[j], c), device_id_type=MESH)
                for i in range(n) for j in range(NCHIP_OTHER)]

        def start():
            for cp in copies():
                cp.start()

        def finish():
            cps = copies()
            for cp in cps:
                cp.wait_recv()
            for cp in cps:
                cp.wait_send()

        return start, lambda: None, finish


class _ExchangeVia:
    def __init__(self, q):
        _, r, cd = q.shape
        half = (2, r // 2, cd)
        self.ins, self.in_specs = [q], [_HBM]
        self.out_shape, self.out_specs = [_sds((2, r, cd), q.dtype)], [_HBM]
        self.scratch = [pltpu.VMEM(half, q.dtype), pltpu.VMEM(half, q.dtype), pltpu.VMEM(half, q.dtype),
                        pltpu.SemaphoreType.DMA((6,)), pltpu.SemaphoreType.DMA((6,)), pltpu.SemaphoreType.DMA((2,))]

    def ops(self, ins, outs, scr):
        (q,), (land,) = ins, outs
        relayed, own, comb, send_sems, recv_sems, local_sems = scr
        h = q.shape[1] // 2
        x, y, c = _place()
        xn, yn, _ = _other_chips(x, y)
        h0, h1 = pl.ds(0, h), pl.ds(h, h)

        def remote(k, src, dst, chip):
            return pltpu.make_async_remote_copy(src_ref=src, dst_ref=dst, send_sem=send_sems.at[k],
                                                recv_sem=recv_sems.at[k], device_id=(*chip, c), device_id_type=MESH)

        def via():
            return [remote(2, q.at[2, h0], relayed.at[0], xn), remote(3, q.at[2, h1], relayed.at[1], yn)]

        def direct():
            return [remote(0, q.at[0, h0], land.at[0, h0], xn), remote(1, q.at[1, h1], land.at[1, h1], yn)]

        def second():
            return [remote(4, comb.at[0], land.at[1, h0], yn), remote(5, comb.at[1], land.at[0, h1], xn)]

        def mine():
            return [pltpu.make_async_copy(q.at[1, h0], own.at[0], local_sems.at[0]),
                    pltpu.make_async_copy(q.at[0, h1], own.at[1], local_sems.at[1])]

        def start():
            for cp in via() + direct() + mine():
                cp.start()

        def relay():
            arrived, loaded, onward = via(), mine(), second()
            for k in range(2):
                arrived[k].wait_recv()
                loaded[k].wait()
                comb[k] = (own[k].astype(F32) + relayed[k].astype(F32)).astype(comb.dtype)
                onward[k].start()

        def finish():
            landing = direct() + second()
            for cp in landing:
                cp.wait_recv()
            for cp in via() + landing:
                cp.wait_send()

        return start, relay, finish


class _SumGather:
    def __init__(self, accs, lands):
        n = len(accs)
        self.n = n
        self.ins, self.in_specs = list(accs) + list(lands), [_VMEM] * (2 * n)
        self.out_shape = [_sds((NDEV,) + a.shape, a.dtype) for a in accs]
        self.out_specs = [_HBM] * n
        self.scratch = [pltpu.VMEM(a.shape, a.dtype) for a in accs] + _gather_sems(n)

    def ops(self, ins, outs, scr):
        n = self.n
        accs, lands, mine = ins[:n], ins[n:], scr[:n]
        g_start, relay, finish = _gather_ops(mine, outs, *scr[n:])

        def start():
            for i in range(n):
                mine[i][...] = accs[i][...] + lands[i][0] + lands[i][1] + lands[i][2]
            g_start()

        return start, relay, finish


def _call(main, jobs, *, name, grid, ins, in_specs, out_shape, out_specs, scratch, relay_step=0):
    nsteps = grid[0] if grid else 1
    n_in, n_out, n_scr = len(ins), len(out_shape), len(scratch)

    def body(*refs):
        pos = [0]

        def take(k):
            r = refs[pos[0]:pos[0] + k]
            pos[0] += k
            return r

        m_in = take(n_in)
        j_in = [take(len(j.ins)) for j in jobs]
        m_out = take(n_out)
        j_out = [take(len(j.out_shape)) for j in jobs]
        m_scr = take(n_scr)
        j_scr = [take(len(j.scratch)) for j in jobs]
        ops = [j.ops(a, b, s) for j, a, b, s in zip(jobs, j_in, j_out, j_scr)]
        i = pl.program_id(0) if grid else 0
        if not grid:
            for o in ops:
                o[0]()
            main(i, m_in, m_out, m_scr)
            for o in ops:
                o[1]()
            for o in ops:
                o[2]()
            return

        if ops:
            @pl.when(i == 0)
            def _():
                for o in ops:
                    o[0]()

        main(i, m_in, m_out, m_scr)

        if ops:
            @pl.when(i == min(relay_step, nsteps - 1))
            def _():
                for o in ops:
                    o[1]()

            @pl.when(i == nsteps - 1)
            def _():
                for o in ops:
                    o[2]()

    extra = dict(dimension_semantics=("arbitrary",)) if grid else {}
    res = pl.pallas_call(
        body, name=name, grid=grid,
        in_specs=list(in_specs) + [s for j in jobs for s in j.in_specs],
        out_specs=list(out_specs) + [s for j in jobs for s in j.out_specs],
        out_shape=list(out_shape) + [s for j in jobs for s in j.out_shape],
        scratch_shapes=list(scratch) + [s for j in jobs for s in j.scratch],
        compiler_params=_params(**extra),
    )(*ins, *[a for j in jobs for a in j.ins])
    main_out, rest, job_out = res[:n_out], res[n_out:], []
    for j in jobs:
        k = len(j.out_shape)
        job_out.append(rest[:k])
        rest = rest[k:]
    return main_out, job_out


def _comm_only(jobs, name):
    _, job_out = _call(lambda i, a, b, s: None, jobs, name=name, grid=(), ins=[], in_specs=[], out_shape=[],
                       out_specs=[], scratch=[])
    return job_out


class _InChip:
    def __init__(self, ps):
        n = len(ps)
        self.n = n
        blk = [p.shape[1:] for p in ps]
        self.ins, self.in_specs = list(ps), [_HBM] * n
        self.out_shape = [_sds((NCHIP_OTHER,) + b, p.dtype) for b, p in zip(blk, ps)] + [_sds(b, F32) for b in blk]
        self.out_specs = [_VMEM] * (2 * n)
        self.scratch = ([pltpu.VMEM((4,) + b, p.dtype) for b, p in zip(blk, ps)] * 2
                        + [pltpu.SemaphoreType.DMA((4, n))] * 3)

    def ops(self, ins, outs, scr):
        n = self.n
        q_refs, acc_refs = outs[:n], outs[n:]
        mines, lands = scr[:n], scr[n:2 * n]
        send_sems, recv_sems, local_sems = scr[2 * n:]
        x, y, c = _place()
        sibling = (x, y, 1 - c)

        def copies():
            out = []
            for i in range(n):
                for pi in range(4):
                    loc = pltpu.make_async_copy(ins[i].at[2 * pi + c], mines[i].at[pi], local_sems.at[pi, i])
                    cp = pltpu.make_async_remote_copy(
                        src_ref=ins[i].at[2 * pi + (1 - c)], dst_ref=lands[i].at[pi],
                        send_sem=send_sems.at[pi, i], recv_sem=recv_sems.at[pi, i],
                        device_id=sibling, device_id_type=MESH)
                    out.append((loc, cp))
            return out

        def start():
            for loc, cp in copies():
                loc.start()
                cp.start()

        def finish():
            pairs = copies()
            for loc, cp in pairs:
                loc.wait()
                cp.wait_recv()
            for i in range(n):
                _chip_sums(mines[i], lands[i], q_refs[i], acc_refs[i], x, y)
            for _, cp in pairs:
                cp.wait_send()

        return start, lambda: None, finish


def _chip_sums(mine, land, q_ref, acc_ref, x, y):
    for j, (qx, qy) in enumerate(_other_chips(x, y)):
        qi = 2 * qx + qy
        q_ref[j] = (mine[qi].astype(F32) + land[qi].astype(F32)).astype(q_ref.dtype)
    mi = 2 * x + y
    acc_ref[...] = mine[mi].astype(F32) + land[mi].astype(F32)


def _direct_sum(v, buf, send_sems, recv_sems):
    x, y, c = _place()
    me = 4 * x + 2 * y + c
    buf[me] = v
    cps = []
    for k in range(1, NDEV):
        fx, fy, fc = (k >> 2) & 1, (k >> 1) & 1, k & 1
        peer = ((1 - x) if fx else x, (1 - y) if fy else y, (1 - c) if fc else c)
        cps.append((peer, pltpu.make_async_remote_copy(
            src_ref=buf.at[me], dst_ref=buf.at[me], send_sem=send_sems.at[k - 1], recv_sem=recv_sems.at[k - 1],
            device_id=peer, device_id_type=MESH)))
    for _, cp in cps:
        cp.start()
    for k, (peer, _) in enumerate(cps):
        theirs = 4 * peer[0] + 2 * peer[1] + peer[2]
        pltpu.make_async_remote_copy(
            src_ref=buf.at[theirs], dst_ref=buf.at[theirs], send_sem=send_sems.at[k], recv_sem=recv_sems.at[k],
            device_id=peer, device_id_type=MESH).wait_recv()
    acc = buf[0]
    for j in range(1, NDEV):
        acc = acc + buf[j]
    for _, cp in cps:
        cp.wait_send()
    return acc


def _direct_sum_scratch(shape, dtype):
    return [pltpu.VMEM((NDEV,) + tuple(shape), dtype), pltpu.SemaphoreType.DMA((NDEV - 1,)),
            pltpu.SemaphoreType.DMA((NDEV - 1,))]


def _fwd_a(x, nw, win8, lnw, lnb, ws, bst, jobs, *, tm, relay_step):
    s_len = x.shape[0]
    nt = s_len // tm
    nch = tm // CH

    def main(i, ins, outs, scr):
        x_ref, nw_ref, win_ref, lnw_ref, lnb_ref, ws_ref, bst_ref = ins
        z_ref, h_ref, y_ref = outs
        wc_scr, gv_scr = scr

        @pl.when(i == 0)
        def _():
            m = _causal_mask()
            for g in range(G):
                wc_scr[g] = jnp.where(m, ws_ref[g], 0.0).astype(BF16)

        x = x_ref[...]
        h = (x * _rms(x) * nw_ref[...]).astype(BF16)
        h_ref[...] = h
        for k in range(NDEV):
            z_ref[:, k * CA:(k + 1) * CA] = _dot(h, win_ref[k])

        ssum = jnp.zeros((tm, 1), F32)
        for g in range(G):
            gv = _gelu_t(z_ref[:, AW + g * GD:AW + (g + 1) * GD])[0]
            gv_scr[:, g * GD:(g + 1) * GD] = gv
            ssum = ssum + jnp.sum(gv, axis=-1, keepdims=True)
        mu = ssum * (1.0 / AW)
        vsum = jnp.zeros((tm, 1), F32)
        for g in range(G):
            dlt = gv_scr[:, g * GD:(g + 1) * GD] - mu
            vsum = vsum + jnp.sum(dlt * dlt, axis=-1, keepdims=True)
        rstd = lax.rsqrt(vsum * (1.0 / AW) + LN_EPS)

        for g in range(G):
            cs = slice(g * GD, (g + 1) * GD)
            v = (gv_scr[:, cs] - mu) * rstd * lnw_ref[:, cs] + lnb_ref[:, cs]
            vb = v.astype(BF16)
            u = _gelu_t(z_ref[:, cs])[0]
            zg = z_ref[:, 2 * AW + g * GD:2 * AW + (g + 1) * GD]
            sg = zg * _sigmoid(zg)
            for n in range(nch):
                rs = slice(n * CH, (n + 1) * CH)
                s = _dot(wc_scr[g], vb[rs, :]) + bst_ref[:, g:g + 1]
                y_ref[rs, cs] = (u[rs, :] * s * sg[rs, :]).astype(BF16)

    tile = lambda w: pl.BlockSpec((tm, w), lambda i: (i, 0))
    return _call(
        main, jobs, name="fwd_a", grid=(nt,), relay_step=relay_step,
        ins=[x, nw, win8, lnw, lnb, ws, bst], in_specs=[tile(D), _VMEM, _VMEM, _VMEM, _VMEM, _VMEM, _VMEM],
        out_shape=[_sds((s_len, 3 * AW), F32), _sds((s_len, D), BF16), _sds((s_len, AW), BF16)],
        out_specs=[tile(3 * AW), tile(D), tile(AW)],
        scratch=[pltpu.VMEM((G, CH, CH), BF16), pltpu.VMEM((tm, AW), F32)])


def _bwd_a(dx1, z, lnw, lnb, ws, bst, wout, jobs, *, tm, relay_step):
    s_len = dx1.shape[0]
    nt = s_len // tm
    nch = tm // CH

    def main(i, ins, outs, scr):
        dx1_ref, z_ref, lnw_ref, lnb_ref, ws_ref, bst_ref, wout_ref = ins
        dz_ref, glnw_ref, glnb_ref, gws_ref, gbst_ref = outs
        wc_scr, wct_scr, vh_scr, dgv_scr, dy_scr, dv_scr, gbs_acc, gwc_acc = scr

        @pl.when(i == 0)
        def _():
            m = _causal_mask()
            for g in range(G):
                wm = jnp.where(m, ws_ref[g], 0.0)
                wc_scr[g] = wm.astype(BF16)
                wct_scr[g] = wm.T.astype(BF16)
            glnw_ref[...] = jnp.zeros_like(glnw_ref)
            glnb_ref[...] = jnp.zeros_like(glnb_ref)
            gbs_acc[...] = jnp.zeros_like(gbs_acc)
            gwc_acc[...] = jnp.zeros_like(gwc_acc)

        dy_scr[...] = _dot_nt(dx1_ref[...], wout_ref[...])

        ssum = jnp.zeros((tm, 1), F32)
        for g in range(G):
            cs = slice(g * GD, (g + 1) * GD)
            zv = z_ref[:, AW + g * GD:AW + (g + 1) * GD]
            gv, t = _gelu_t(zv)
            vh_scr[:, cs] = gv
            dgv_scr[:, cs] = _dgelu(zv, t)
            ssum = ssum + jnp.sum(gv, axis=-1, keepdims=True)
        mu = ssum * (1.0 / AW)
        vsum = jnp.zeros((tm, 1), F32)
        for g in range(G):
            dlt = vh_scr[:, g * GD:(g + 1) * GD] - mu
            vsum = vsum + jnp.sum(dlt * dlt, axis=-1, keepdims=True)
        rstd = lax.rsqrt(vsum * (1.0 / AW) + LN_EPS)

        m1 = jnp.zeros((tm, 1), F32)
        m2 = jnp.zeros((tm, 1), F32)
        for g in range(G):
            cs = slice(g * GD, (g + 1) * GD)
            gs = slice(2 * AW + g * GD, 2 * AW + (g + 1) * GD)
            vhat = (vh_scr[:, cs] - mu) * rstd
            vh_scr[:, cs] = vhat
            vb = (vhat * lnw_ref[:, cs] + lnb_ref[:, cs]).astype(BF16)
            zu = z_ref[:, cs]
            u, tu = _gelu_t(zu)
            zg = z_ref[:, gs]
            sig = _sigmoid(zg)
            sg = zg * sig
            dy = dy_scr[:, cs]
            dsf = dy * u * sg
            dsb = dsf.astype(BF16)
            dvs = []
            for n in range(nch):
                rs = slice(n * CH, (n + 1) * CH)
                s = _dot(wc_scr[g], vb[rs, :]) + bst_ref[:, g:g + 1]
                dys = dy[rs, :] * s
                dz_ref[rs, cs] = (dys * sg[rs, :] * _dgelu(zu[rs, :], tu[rs, :])).astype(BF16)
                dz_ref[rs, gs] = (dys * u[rs, :] * (sig[rs, :] * (1.0 + zg[rs, :] * (1.0 - sig[rs, :])))).astype(BF16)
                gbs_acc[g] += dsf[rs, :]
                gwc_acc[g] += _dot_nt(dsb[rs, :], vb[rs, :])
                dvs.append(_dot(wct_scr[g], dsb[rs, :]))
            dv = jnp.concatenate(dvs, axis=0) if nch > 1 else dvs[0]
            glnw_ref[:, cs] += _rowsum(dv * vhat)
            glnb_ref[:, cs] += _rowsum(dv)
            dvh = dv * lnw_ref[:, cs]
            dv_scr[:, cs] = dvh
            m1 = m1 + jnp.sum(dvh, axis=-1, keepdims=True)
            m2 = m2 + jnp.sum(dvh * vhat, axis=-1, keepdims=True)
        m1 = m1 * (1.0 / AW)
        m2 = m2 * (1.0 / AW)
        for g in range(G):
            cs = slice(g * GD, (g + 1) * GD)
            dgv = rstd * (dv_scr[:, cs] - m1 - vh_scr[:, cs] * m2)
            dz_ref[:, AW + g * GD:AW + (g + 1) * GD] = (dgv * dgv_scr[:, cs]).astype(BF16)

        @pl.when(i == nt - 1)
        def _():
            m = _causal_mask()
            for g in range(G):
                gws_ref[g] = jnp.where(m, gwc_acc[g], 0.0)
                gbst_ref[:, g:g + 1] = jnp.sum(gbs_acc[g], axis=-1, keepdims=True)

    tile = lambda w: pl.BlockSpec((tm, w), lambda i: (i, 0))
    whole = lambda *s: pl.BlockSpec(s, lambda i: (0,) * len(s))
    big = lambda dt: pltpu.VMEM((tm, AW), dt)
    return _call(
        main, jobs, name="bwd_a", grid=(nt,), relay_step=relay_step,
        ins=[dx1, z, lnw, lnb, ws, bst, wout], in_specs=[tile(D), tile(3 * AW), _VMEM, _VMEM, _VMEM, _VMEM, _VMEM],
        out_shape=[_sds((s_len, 3 * AW), BF16), _sds((1, AW), F32), _sds((1, AW), F32), _sds((G, CH, CH), F32),
                   _sds((CH, G), F32)],
        out_specs=[tile(3 * AW), whole(1, AW), whole(1, AW), whole(G, CH, CH), whole(CH, G)],
        scratch=[pltpu.VMEM((G, CH, CH), BF16), pltpu.VMEM((G, CH, CH), BF16), big(F32), big(F32), big(F32), big(F32),
                 pltpu.VMEM((G, CH, GD), F32), pltpu.VMEM((G, CH, CH), F32)])


def _bwd_a_in(dz, dx1, x, nw, win8, jobs, *, tm, relay_step):
    s_len = x.shape[0]
    nt = s_len // tm

    def main(i, ins, outs, scr):
        dz_ref, dx1_ref, x_ref, nw_ref, win_ref = ins
        gx_ref, gnw_ref = outs

        @pl.when(i == 0)
        def _():
            gnw_ref[...] = jnp.zeros_like(gnw_ref)

        dh = jnp.zeros((tm, D), F32)
        for k in range(NDEV):
            dh = dh + _dot_nt(dz_ref[:, k * CA:(k + 1) * CA], win_ref[k])
        x = x_ref[...]
        r = _rms(x)
        gx_ref[...] = dx1_ref[...] + _rms_bwd(dh, x, r, nw_ref[...])
        gnw_ref[...] += _rowsum(dh * x * r)

        @pl.when(i == nt - 1)
        def _():
            gnw_ref[...] = _direct_sum(gnw_ref[...], *scr)

    tile = lambda w: pl.BlockSpec((tm, w), lambda i: (i, 0))
    return _call(
        main, jobs, name="bwd_a_in", grid=(nt,), relay_step=relay_step,
        ins=[dz, dx1, x, nw, win8], in_specs=[tile(3 * AW), tile(D), tile(D), _VMEM, _VMEM],
        out_shape=[_sds((s_len, D), F32), _sds((1, D), F32)],
        out_specs=[tile(D), pl.BlockSpec((1, D), lambda i: (0, 0))], scratch=_direct_sum_scratch((1, D), F32))


def _conv(p8_ref, cs, xb, xm1, xm2, xm3):
    xc = p8_ref[4:5, cs] + p8_ref[3:4, cs] * xb
    xc = xc + p8_ref[0:1, cs] * xm3
    xc = xc + p8_ref[1:2, cs] * xm2
    return xc + p8_ref[2:3, cs] * xm1


def _gates(p8_ref, gcat_ref, hh, xc):
    cs = slice(hh * HD, (hh + 1) * HD)
    pre = _dot(xc.astype(BF16), gcat_ref[hh])
    r = _sigmoid(pre[:, :HD] + p8_ref[5:6, cs])
    ig = _sigmoid(pre[:, HD:] + p8_ref[6:7, cs])
    sp = _softplus_neg(p8_ref[7:8, cs])
    la = (-RG_C) * r * sp
    a = jnp.exp(la)
    half_log = 0.5 * jnp.log(jnp.tanh(-la) * (1.0 + a * a))
    return r, ig, sp, a, jnp.exp(half_log), jnp.exp(-half_log)


def _scan_rows(a_ref, b_ref, out_ref, carry, tm, reverse):
    row = lax.broadcasted_iota(jnp.int32, (SUBLANES, BW), 0)
    ngrp = tm // SUBLANES

    def step(j, cr):
        jj = (ngrp - 1 - j) if reverse else j
        off = pl.multiple_of(jj * SUBLANES, SUBLANES)
        a = a_ref[pl.ds(off, SUBLANES), :]
        b = b_ref[pl.ds(off, SUBLANES), :]
        for sh in (1, 2, 4):
            if reverse:
                a_s = pltpu.roll(a, SUBLANES - sh, 0)
                b_s = pltpu.roll(b, SUBLANES - sh, 0)
                m = row < SUBLANES - sh
            else:
                a_s = pltpu.roll(a, sh, 0)
                b_s = pltpu.roll(b, sh, 0)
                m = row >= sh
            b = jnp.where(m, a * b_s + b, b)
            a = jnp.where(m, a * a_s, a)
        o = b + a * cr
        out_ref[pl.ds(off, SUBLANES), :] = o
        return o[0:1, :] if reverse else o[SUBLANES - 1:SUBLANES, :]

    return lax.fori_loop(0, ngrp, step, carry)


def _fwd_b(x, ya, wout_a, nw, win8, p8, gcat, jobs, *, tm, relay_step):
    s_len = x.shape[0]
    nt = s_len // tm

    def main(i, ins, outs, scr):
        x_ref, ya_ref, wouta_ref, nw_ref, win_ref, p8_ref, gcat_ref = ins
        x1_ref, zb_ref, hs_ref, h1_ref, yb_ref, xc_ref, a_ref, cc_ref, r_ref, ig_ref, m_ref = outs
        xbe_scr, b_scr, k_scr, carry_scr = scr

        @pl.when(i == 0)
        def _():
            xbe_scr[0:SUBLANES, :] = jnp.zeros((SUBLANES, BW), F32)
            carry_scr[...] = jnp.zeros_like(carry_scr)

        x1 = x_ref[...] + _dot(ya_ref[...], wouta_ref[...])
        x1_ref[...] = x1
        h = (x1 * _rms(x1) * nw_ref[...]).astype(BF16)
        h1_ref[...] = h
        for k in range(NDEV):
            zb_ref[:, k * CB:(k + 1) * CB] = _dot(h, win_ref[k])
        xbe_scr[SUBLANES:SUBLANES + tm, :] = zb_ref[:, :BW]
        for hh in range(BH):
            cs = slice(hh * HD, (hh + 1) * HD)
            xc = _conv(p8_ref, cs, xbe_scr[SUBLANES:SUBLANES + tm, cs], xbe_scr[7:7 + tm, cs],
                       xbe_scr[6:6 + tm, cs], xbe_scr[5:5 + tm, cs])
            r, ig, _, a, mult, rm = _gates(p8_ref, gcat_ref, hh, xc)
            ixc = ig * xc
            xc_ref[:, cs] = xc
            a_ref[:, cs] = a
            r_ref[:, cs] = r.astype(BF16)
            ig_ref[:, cs] = ig.astype(BF16)
            m_ref[:, cs] = mult.astype(BF16)
            b_scr[:, cs] = mult * ixc
            k_scr[:, cs] = ixc * (a * a * rm)
        xbe_scr[0:SUBLANES, :] = xbe_scr[tm:tm + SUBLANES, :]
        carry_scr[...] = _scan_rows(a_ref, b_scr, hs_ref, carry_scr[...], tm, False)
        for hh in range(BH):
            cs = slice(hh * HD, (hh + 1) * HD)
            gt = zb_ref[:, BW + hh * HD:BW + (hh + 1) * HD]
            hsv = hs_ref[:, cs]
            yb_ref[:, cs] = (hsv * (gt * _sigmoid(gt))).astype(BF16)
            cc_ref[:, cs] = (hsv - b_scr[:, cs]) - k_scr[:, cs]

    tile = lambda w: pl.BlockSpec((tm, w), lambda i: (i, 0))
    wide = lambda dt: _sds((s_len, BW), dt)
    return _call(
        main, jobs, name="fwd_b", grid=(nt,), relay_step=relay_step,
        ins=[x, ya, wout_a, nw, win8, p8, gcat], in_specs=[tile(D), tile(AW), _VMEM, _VMEM, _VMEM, _VMEM, _VMEM],
        out_shape=[_sds((s_len, D), F32), _sds((s_len, 2 * BW), F32), wide(F32), _sds((s_len, D), BF16), wide(BF16),
                   wide(F32), wide(F32), wide(F32), wide(BF16), wide(BF16), wide(BF16)],
        out_specs=[tile(D), tile(2 * BW), tile(BW), tile(D)] + [tile(BW)] * 7,
        scratch=[pltpu.VMEM((tm + SUBLANES, BW), F32), pltpu.VMEM((tm, BW), F32), pltpu.VMEM((tm, BW), F32),
                 pltpu.VMEM((1, BW), F32)])


def _head(x1, yb, wout, nfw, tgt, *, tm):
    s_len = x1.shape[0]

    def main(i, ins, outs, scr):
        x1_ref, yb_ref, wout_ref, nfw_ref, t_ref = ins
        dx2_ref, dx2b_ref, loss_ref, gnfw_ref = outs

        @pl.when(i == 0)
        def _():
            loss_ref[...] = jnp.zeros_like(loss_ref)
            gnfw_ref[...] = jnp.zeros_like(gnfw_ref)

        x2 = x1_ref[...] + _dot(yb_ref[...], wout_ref[...])
        rf = _rms(x2)
        xn = x2 * rf
        e = xn * nfw_ref[...] - t_ref[...]
        loss_ref[...] += (0.5 / D) * jnp.sum(jnp.sum(e * e, axis=-1, keepdims=True), axis=0, keepdims=True)
        dyf = e * (1.0 / D)
        gnfw_ref[...] += _rowsum(dyf * xn)
        dx2 = _rms_bwd(dyf, x2, rf, nfw_ref[...])
        dx2_ref[...] = dx2
        dx2b_ref[...] = dx2.astype(BF16)

    tile = lambda w: pl.BlockSpec((tm, w), lambda i: (i, 0))
    whole = lambda *s: pl.BlockSpec(s, lambda i: (0,) * len(s))
    (dx2, dx2b, loss, gnfw), _ = _call(
        main, [], name="head", grid=(s_len // tm,),
        ins=[x1, yb, wout, nfw, tgt], in_specs=[tile(D), tile(BW), _VMEM, _VMEM, tile(D)],
        out_shape=[_sds((s_len, D), F32), _sds((s_len, D), BF16), _sds((1, 1), F32), _sds((1, D), F32)],
        out_specs=[tile(D), tile(D), whole(1, 1), whole(1, D)], scratch=[])
    return dx2, dx2b, loss, gnfw


def _bwd_b(dx2, zb, hs, x1, saved, nw, win8, p8, gcat, wout, *, tm):
    s_len = x1.shape[0]
    nt = s_len // tm

    def main(i, ins, outs, scr):
        (dx2_ref, zb_ref, hs_ref, x1_ref, xc_ref, a_ref, cc_ref, r_ref, ig_ref, m_ref,
         nw_ref, win_ref, p8_ref, gcat_ref, wout_ref) = ins
        dx1_ref, dx1b_ref, dzb_ref, gp8_ref, gga_ref, ggx_ref, gnw_ref = outs
        ae_scr, an_scr, dhd_scr, dh_scr, dy_scr, dxce_scr, carry_scr, afirst_scr = scr

        @pl.when(i == 0)
        def _():
            gp8_ref[...] = jnp.zeros_like(gp8_ref)
            gga_ref[...] = jnp.zeros_like(gga_ref)
            ggx_ref[...] = jnp.zeros_like(ggx_ref)
            gnw_ref[...] = jnp.zeros_like(gnw_ref)
            dxce_scr[tm:tm + SUBLANES, :] = jnp.zeros((SUBLANES, BW), F32)
            carry_scr[...] = jnp.zeros_like(carry_scr)
            afirst_scr[...] = jnp.zeros_like(afirst_scr)

        dx2 = dx2_ref[...]
        dy_scr[...] = _dot_nt(dx2.astype(BF16), wout_ref[...])
        for hh in range(BH):
            cs = slice(hh * HD, (hh + 1) * HD)
            gs = slice(BW + hh * HD, BW + (hh + 1) * HD)
            gt = zb_ref[:, gs]
            sig = _sigmoid(gt)
            dy = dy_scr[:, cs]
            dhd_scr[:, cs] = dy * (gt * sig)
            dzb_ref[:, gs] = (dy * hs_ref[:, cs] * (sig * (1.0 + gt * (1.0 - sig)))).astype(BF16)

        ae_scr[0:tm, :] = a_ref[...]
        ae_scr[tm:tm + SUBLANES, :] = jnp.broadcast_to(afirst_scr[...], (SUBLANES, BW))
        an_scr[...] = ae_scr[1:1 + tm, :]
        afirst_scr[...] = ae_scr[0:1, :]
        carry_scr[...] = _scan_rows(an_scr, dhd_scr, dh_scr, carry_scr[...], tm, True)

        for hh in range(BH):
            cs = slice(hh * HD, (hh + 1) * HD)
            dh = dh_scr[:, cs]
            mult = m_ref[:, cs].astype(F32)
            ig = ig_ref[:, cs].astype(F32)
            r = r_ref[:, cs].astype(F32)
            xc = xc_ref[:, cs]
            lam = p8_ref[7:8, cs]
            sp = _softplus_neg(lam)
            dla = dh * cc_ref[:, cs]
            gp8_ref[7:8, cs] += _rowsum(dla * ((-RG_C) * r)) * (-_sigmoid(-lam))
            dpr = dla * ((-RG_C) * sp) * (r * (1.0 - r))
            dpi = dh * mult * xc * (ig * (1.0 - ig))
            gp8_ref[5:6, cs] += _rowsum(dpr)
            gp8_ref[6:7, cs] += _rowsum(dpi)
            dcat = jnp.concatenate([dpr, dpi], axis=1).astype(BF16)
            dxc = dh * mult * ig + _dot_nt(dcat, gcat_ref[hh])
            gg = _dot(xc.T.astype(BF16), dcat)
            gga_ref[hh] += gg[:, :HD]
            ggx_ref[hh] += gg[:, HD:]
            dxce_scr[0:tm, cs] = dxc
            gp8_ref[4:5, cs] += _rowsum(dxc)
        for hh in range(BH):
            cs = slice(hh * HD, (hh + 1) * HD)
            xb = zb_ref[:, cs]
            d0, d1 = dxce_scr[0:tm, cs], dxce_scr[1:1 + tm, cs]
            d2, d3 = dxce_scr[2:2 + tm, cs], dxce_scr[3:3 + tm, cs]
            dzb_ref[:, cs] = (p8_ref[3:4, cs] * d0 + p8_ref[2:3, cs] * d1 + p8_ref[1:2, cs] * d2
                              + p8_ref[0:1, cs] * d3).astype(BF16)
            gp8_ref[3:4, cs] += _rowsum(d0 * xb)
            gp8_ref[2:3, cs] += _rowsum(d1 * xb)
            gp8_ref[1:2, cs] += _rowsum(d2 * xb)
            gp8_ref[0:1, cs] += _rowsum(d3 * xb)
        dxce_scr[tm:tm + SUBLANES, :] = dxce_scr[0:SUBLANES, :]

        dh1 = jnp.zeros((tm, D), F32)
        for k in range(NDEV):
            dh1 = dh1 + _dot_nt(dzb_ref[:, k * CB:(k + 1) * CB], win_ref[k])
        x1 = x1_ref[...]
        r1 = _rms(x1)
        dx1 = dx2 + _rms_bwd(dh1, x1, r1, nw_ref[...])
        dx1_ref[...] = dx1
        dx1b_ref[...] = dx1.astype(BF16)
        gnw_ref[...] += _rowsum(dh1 * x1 * r1)

    tile = lambda w: pl.BlockSpec((tm, w), lambda i: (nt - 1 - i, 0))
    whole = lambda *s: pl.BlockSpec(s, lambda i: (0,) * len(s))
    full = lambda: pltpu.VMEM((tm, BW), F32)
    ext = lambda: pltpu.VMEM((tm + SUBLANES, BW), F32)
    out, _ = _call(
        main, [], name="bwd_b", grid=(nt,),
        ins=[dx2, zb, hs, x1, *saved, nw, win8, p8, gcat, wout],
        in_specs=[tile(D), tile(2 * BW), tile(BW), tile(D)] + [tile(BW)] * 6 + [_VMEM] * 5,
        out_shape=[_sds((s_len, D), F32), _sds((s_len, D), BF16), _sds((s_len, 2 * BW), BF16), _sds((SUBLANES, BW), F32),
                   _sds((BH, HD, HD), F32), _sds((BH, HD, HD), F32), _sds((1, D), F32)],
        out_specs=[tile(D), tile(D), tile(2 * BW), whole(SUBLANES, BW), whole(BH, HD, HD), whole(BH, HD, HD),
                   whole(1, D)],
        scratch=[ext(), full(), full(), full(), full(), ext(), pltpu.VMEM((1, BW), F32), pltpu.VMEM((1, BW), F32)])
    return out


def _transpose_into(dst_ref, src_ref, rows):
    s_len = src_ref.shape[0]
    for r0 in range(0, s_len, rows):
        dst_ref[:, r0:r0 + rows] = src_ref[r0:r0 + rows, :].astype(F32).T.astype(BF16)


def _wgrad(a, b, jobs, *, by_rows, per, name, relay_step=0):
    s_len, m = a.shape
    n = b.shape[1]
    r, cd = (m // NDEV, n) if by_rows else (m, n // NDEV)
    nsteps = NDEV // per
    at_rows = per * r if by_rows else m

    def main(i, ins, outs, scr):
        a_ref, b_ref = ins
        q_ref, acc_ref = outs
        at_scr, stage, mine, land, send_sems, recv_sems = scr
        x, y, c = _place()

        def to_sibling(pi):
            return pltpu.make_async_remote_copy(
                src_ref=stage.at[pi & 1], dst_ref=land.at[pi], send_sem=send_sems.at[pi], recv_sem=recv_sems.at[pi],
                device_id=(x, y, 1 - c), device_id_type=MESH)

        if by_rows:
            _transpose_into(at_scr, a_ref, 256)
        else:
            @pl.when(i == 0)
            def _():
                _transpose_into(at_scr, a_ref, 256)

        res = _dot(at_scr[...], b_ref[...]).astype(BF16)
        for k in range(per):
            blk = per * i + k
            pi, pc = blk >> 1, blk & 1
            val = res[k * r:(k + 1) * r, :] if by_rows else res

            @pl.when(pc != c)
            def _():
                @pl.when(pi >= 2)
                def _():
                    to_sibling(pi - 2).wait_send()

                stage[pi & 1] = val
                to_sibling(pi).start()

            @pl.when(pc == c)
            def _():
                mine[pi] = val

        @pl.when(i == nsteps - 1)
        def _():
            for p in range(4):
                to_sibling(p).wait_recv()
            to_sibling(2).wait_send()
            to_sibling(3).wait_send()
            _chip_sums(mine, land, q_ref, acc_ref, x, y)

    if by_rows:
        in_specs = [pl.BlockSpec((s_len, at_rows), lambda j: (0, j)), _VMEM]
    else:
        in_specs = [_VMEM, pl.BlockSpec((s_len, cd), lambda j: (0, j))]
    blk_vmem = lambda k: pltpu.VMEM((k, r, cd), BF16)
    (q, acc), job_out = _call(
        main, jobs, name=name, grid=(nsteps,), relay_step=relay_step, ins=[a, b], in_specs=in_specs,
        out_shape=[_sds((NCHIP_OTHER, r, cd), BF16), _sds((r, cd), F32)],
        out_specs=[pl.BlockSpec((NCHIP_OTHER, r, cd), lambda j: (0, 0, 0)), pl.BlockSpec((r, cd), lambda j: (0, 0))],
        scratch=[pltpu.VMEM((at_rows, s_len), BF16), blk_vmem(2), blk_vmem(4), blk_vmem(4),
                 pltpu.SemaphoreType.DMA((4,)), pltpu.SemaphoreType.DMA((4,))])
    return q, acc, job_out


def _wgrad_cols_early(a, b, jobs, *, name, relay_step=0):
    s_len, m = a.shape
    r, cd = m, b.shape[1] // NDEV
    h = r // 2

    def chip_at(pos, base):
        return base ^ (3 - pos)

    def main(i, ins, outs, scr):
        a_ref, b_ref = ins
        q_ref, acc_ref, rel_ref = outs
        at_scr, stage, mine, land, q2_scr, send_sems, recv_sems, via_send, via_recv = scr
        x, y, c = _place()
        base = 2 * x + y
        xn, yn, _ = _other_chips(x, y)
        pos, pc = i >> 1, i & 1
        pi = chip_at(pos, base)

        def to_sibling(chip, slot):
            return pltpu.make_async_remote_copy(
                src_ref=stage.at[slot], dst_ref=land.at[chip], send_sem=send_sems.at[chip],
                recv_sem=recv_sems.at[chip], device_id=(x, y, 1 - c), device_id_type=MESH)

        def via(k):
            return pltpu.make_async_remote_copy(
                src_ref=q2_scr.at[pl.ds(k * h, h)], dst_ref=rel_ref.at[k], send_sem=via_send.at[k],
                recv_sem=via_recv.at[k], device_id=(*(xn, yn)[k], c), device_id_type=MESH)

        @pl.when(i == 0)
        def _():
            _transpose_into(at_scr, a_ref, 256)

        res = _dot(at_scr[...], b_ref[...]).astype(BF16)

        @pl.when(pc != c)
        def _():
            @pl.when(pos >= 2)
            def _():
                to_sibling(chip_at(pos - 2, base), pos & 1).wait_send()

            stage[pos & 1] = res
            to_sibling(pi, pos & 1).start()

        @pl.when(pc == c)
        def _():
            mine[pi] = res

        @pl.when(i == 1)
        def _():
            dg = chip_at(0, base)
            to_sibling(dg, 0).wait_recv()
            q2 = (mine[dg].astype(F32) + land[dg].astype(F32)).astype(BF16)
            q2_scr[...] = q2
            q_ref[2] = q2
            via(0).start()
            via(1).start()

        @pl.when(i == NDEV - 1)
        def _():
            for pos_ in (1, 2, 3):
                to_sibling(chip_at(pos_, base), 0).wait_recv()
            to_sibling(chip_at(2, base), 0).wait_send()
            to_sibling(chip_at(3, base), 1).wait_send()
            for k in range(2):
                via(k).wait_recv()
            for k in range(2):
                via(k).wait_send()
            for j, chip in enumerate((base ^ 2, base ^ 1)):
                q_ref[j] = (mine[chip].astype(F32) + land[chip].astype(F32)).astype(BF16)
            acc_ref[...] = mine[base].astype(F32) + land[base].astype(F32)

    def b_block(j):
        base = 2 * lax.axis_index("x") + lax.axis_index("y")
        return (0, 2 * chip_at(j >> 1, base) + (j & 1))

    blk_vmem = lambda k: pltpu.VMEM((k, r, cd), BF16)
    (q, acc, rel), job_out = _call(
        main, jobs, name=name, grid=(NDEV,), relay_step=relay_step, ins=[a, b],
        in_specs=[_VMEM, pl.BlockSpec((s_len, cd), b_block)],
        out_shape=[_sds((NCHIP_OTHER, r, cd), BF16), _sds((r, cd), F32), _sds((2, h, cd), BF16)],
        out_specs=[pl.BlockSpec((NCHIP_OTHER, r, cd), lambda j: (0, 0, 0)), pl.BlockSpec((r, cd), lambda j: (0, 0)), _HBM],
        scratch=[pltpu.VMEM((m, s_len), BF16), blk_vmem(2), blk_vmem(4), blk_vmem(4), pltpu.VMEM((r, cd), BF16),
                 pltpu.SemaphoreType.DMA((4,)), pltpu.SemaphoreType.DMA((4,)), pltpu.SemaphoreType.DMA((2,)),
                 pltpu.SemaphoreType.DMA((2,))])
    return q, acc, rel, job_out


class _ExchangeRest:
    def __init__(self, q, relayed):
        _, r, cd = q.shape
        half = (2, r // 2, cd)
        self.ins, self.in_specs = [q, relayed], [_HBM, _HBM]
        self.out_shape, self.out_specs = [_sds((2, r, cd), q.dtype)], [_HBM]
        self.scratch = [pltpu.VMEM(half, q.dtype), pltpu.VMEM(half, q.dtype), pltpu.VMEM(half, q.dtype),
                        pltpu.SemaphoreType.DMA((4,)), pltpu.SemaphoreType.DMA((4,)), pltpu.SemaphoreType.DMA((4,))]

    def ops(self, ins, outs, scr):
        (q, rel_in), (land,) = ins, outs
        own, rel, comb, send_sems, recv_sems, local_sems = scr
        h = q.shape[1] // 2
        x, y, c = _place()
        xn, yn, _ = _other_chips(x, y)
        h0, h1 = pl.ds(0, h), pl.ds(h, h)

        def remote(k, src, dst, chip):
            return pltpu.make_async_remote_copy(src_ref=src, dst_ref=dst, send_sem=send_sems.at[k],
                                                recv_sem=recv_sems.at[k], device_id=(*chip, c), device_id_type=MESH)

        def sends():
            return [remote(0, q.at[0, h0], land.at[0, h0], xn), remote(1, q.at[1, h1], land.at[1, h1], yn),
                    remote(2, comb.at[0], land.at[1, h0], yn), remote(3, comb.at[1], land.at[0, h1], xn)]

        def loads():
            return [pltpu.make_async_copy(q.at[1, h0], own.at[0], local_sems.at[0]),
                    pltpu.make_async_copy(q.at[0, h1], own.at[1], local_sems.at[1]),
                    pltpu.make_async_copy(rel_in.at[0], rel.at[0], local_sems.at[2]),
                    pltpu.make_async_copy(rel_in.at[1], rel.at[1], local_sems.at[3])]

        def start():
            cps, lds = sends(), loads()
            for ld in lds:
                ld.start()
            cps[0].start()
            cps[1].start()
            for ld in lds:
                ld.wait()
            for k in range(2):
                comb[k] = (own[k].astype(F32) + rel[k].astype(F32)).astype(comb.dtype)
            cps[2].start()
            cps[3].start()

        def finish():
            cps = sends()
            for cp in cps:
                cp.wait_recv()
            for cp in cps:
                cp.wait_send()

        return start, lambda: None, finish


def _adam_math(w, g, m, v):
    m = B1 * m + (1.0 - B1) * g
    v = B2 * v + (1.0 - B2) * (g * g)
    m_hat = m / (1.0 - B1 ** STEP)
    v_hat = v / (1.0 - B2 ** STEP)
    delta = (-LR) * (m_hat / (jnp.sqrt(v_hat) + ADAM_EPS) + WD * w)
    return delta, m, v


def _adam_big(w, acc, land, m, v, name):
    r, cd = w.shape
    rb = 256 if r % 256 == 0 else r
    nland = land.shape[0]

    def body(w_ref, acc_ref, land_ref, m_ref, v_ref, g_ref, d_ref, mo_ref, vo_ref):
        g = acc_ref[...]
        for j in range(nland):
            g = g + land_ref[j].astype(F32)
        g_ref[...] = g
        d_ref[...], mo_ref[...], vo_ref[...] = _adam_math(w_ref[...], g, m_ref[...], v_ref[...])

    blk = pl.BlockSpec((rb, cd), lambda i: (i, 0))
    blk3 = pl.BlockSpec((nland, rb, cd), lambda i: (0, i, 0))
    return pl.pallas_call(
        body, name=name, grid=(r // rb,), in_specs=[blk, blk, blk3, blk, blk], out_specs=[blk] * 4,
        out_shape=[_sds((r, cd), F32)] * 4,
        compiler_params=_params(dimension_semantics=("arbitrary",)),
    )(w, acc, land, m, v)


def _adam_small(groups):
    n = len(groups)

    def body(*refs):
        ins, outs = refs[:4 * n], refs[4 * n:]
        for k in range(n):
            w_ref, g_ref, m_ref, v_ref = ins[4 * k:4 * k + 4]
            d, mo, vo = _adam_math(w_ref[...], g_ref[...], m_ref[...], v_ref[...])
            outs[3 * k][...] = d
            outs[3 * k + 1][...] = mo
            outs[3 * k + 2][...] = vo

    flat = [a for grp in groups for a in grp]
    shapes = [_sds(grp[0].shape, F32) for grp in groups for _ in range(3)]
    res = pl.pallas_call(
        body, name="adam_small", in_specs=[_VMEM] * (4 * n), out_specs=[_VMEM] * (3 * n), out_shape=shapes,
        compiler_params=_params(),
    )(*flat)
    return [tuple(res[3 * k:3 * k + 3]) for k in range(n)]


TM_FWD_A = 256
RELAY_STEP_FWD_A = 4
RELAY_STEP_FWD_B = 2
TM_BWD_A = 256
RELAY_STEP_BWD_A = 3
TM_BWD_A_IN = 256
RELAY_STEP_BWD_A_IN = 4
TM_FWD_B = 256
TM_HEAD = 512
TM_BWD_B = 256


def _pack(parts, rows):
    flat = jnp.concatenate([p.reshape(-1) for p in parts])
    return jnp.pad(flat, (0, NDEV * rows * LANES - flat.shape[0])).reshape(NDEV, rows, LANES)


def _unpack(packed, shapes):
    flat, out, off = packed.reshape(-1), [], 0
    for s in shapes:
        size = 1
        for d in s:
            size *= d
        out.append(flat[off:off + size].reshape(s))
        off += size
    return out


def kernel(x, norm_w, a_w_in, a_ln_w, a_ln_b, a_w_s, a_b_s, a_w_out, b_w_in, b_conv_w, b_conv_b, b_gate_a_w, b_gate_a_b, b_gate_x_w, b_gate_x_b, b_lambda, b_w_out, norm_f_w, loss_target, m_norm_w, m_a_w_in, m_a_ln_w, m_a_ln_b, m_a_w_s, m_a_b_s, m_a_w_out, m_b_w_in, m_b_conv_w, m_b_conv_b, m_b_gate_a_w, m_b_gate_a_b, m_b_gate_x_w, m_b_gate_x_b, m_b_lambda, m_b_w_out, m_norm_f_w, v_norm_w, v_a_w_in, v_a_ln_w, v_a_ln_b, v_a_w_s, v_a_b_s, v_a_w_out, v_b_w_in, v_b_conv_w, v_b_conv_b, v_b_gate_a_w, v_b_gate_a_b, v_b_gate_x_w, v_b_gate_x_b, v_b_lambda, v_b_w_out, v_norm_f_w):
    me = 4 * lax.axis_index("x") + 2 * lax.axis_index("y") + lax.axis_index("c")
    xs, tgt = x[0], loss_target[0]
    nw0, nw1, nfw = norm_w[0:1], norm_w[1:2], norm_f_w.reshape(1, D)
    w_s, bst = a_w_s[0], a_b_s[0].T
    gcat = jnp.concatenate([b_gate_a_w[0], b_gate_x_w[0]], axis=-1).astype(BF16)

    p8_shard = jnp.concatenate([b_conv_w[0], b_conv_b, b_gate_a_b, b_gate_x_b, b_lambda], axis=0)
    ((win_a8, p8_all),) = _comm_only([_Gather([a_w_in[0], p8_shard], [BF16, F32])], "gather_first")
    p8 = jnp.transpose(p8_all, (1, 0, 2)).reshape(SUBLANES, BW)

    (z, h0, ya), ((wout_a8, win_b8),) = _fwd_a(
        xs, nw0, win_a8, a_ln_w, a_ln_b, w_s, bst, [_Gather([a_w_out[0], b_w_in[0]], [BF16, BF16])],
        tm=TM_FWD_A, relay_step=RELAY_STEP_FWD_A)
    wout_a = wout_a8.reshape(AW, D)
    (x1, zb, hs, h1, yb, *saved_b), ((wout_b8,),) = _fwd_b(
        xs, ya, wout_a, nw1, win_b8, p8, gcat, [_Gather([b_w_out[0]], [BF16])],
        tm=TM_FWD_B, relay_step=RELAY_STEP_FWD_B)
    wout_b = wout_b8.reshape(BW, D)
    dx2, dx2b, loss, g_nfw = _head(x1, yb, wout_b, nfw, tgt, tm=TM_HEAD)

    dx1, dx1b, dzb, g_p8, g_ga, g_gx, g_nw1 = _bwd_b(dx2, zb, hs, x1, saved_b, nw1, win_b8, p8, gcat, wout_b,
                                                     tm=TM_BWD_B)
    q_wout_b, acc_wout_b, _ = _wgrad(yb, dx2b, [], by_rows=True, per=2, name="wgrad_b_out")
    shapes_b = [(1, D), (1, D), (SUBLANES, BW), (1, 1)]
    pack_b = _pack([g_nfw, g_nw1, g_p8, loss], 16)
    small_b = _InChip([g_ga.reshape(NDEV, -1, HD), g_gx.reshape(NDEV, -1, HD), pack_b])
    q_win_b, acc_win_b, (sm_b, (l_wout_b,)) = _wgrad(h1, dzb, [small_b, _Exchange([q_wout_b])], by_rows=False, per=1,
                                                      name="wgrad_b_in")
    qs_b, accs_b = sm_b[:3], sm_b[3:]

    (dz, g_lnw, g_lnb, g_ws, g_bst), (lands_b, (l_win_b,)) = _bwd_a(
        dx1b, z, a_ln_w, a_ln_b, w_s, bst, wout_a, [_Exchange(qs_b), _ExchangeVia(q_win_b)],
        tm=TM_BWD_A, relay_step=RELAY_STEP_BWD_A)
    shapes_a = [(1, AW), (1, AW), (CH, G)]
    pack_a = _pack([g_lnw, g_lnb, g_bst], 8)
    q_wout_a, acc_wout_a, (red_b, sm_a) = _wgrad(
        ya, dx1b, [_SumGather(accs_b, lands_b), _InChip([g_ws, pack_a])], by_rows=True, per=2,
        name="wgrad_a_out", relay_step=1)
    qs_a, accs_a = [q_wout_a, *sm_a[:2]], [acc_wout_a, *sm_a[2:]]
    q_win_a, acc_win_a, rel_a, (lands_a,) = _wgrad_cols_early(h0, dz, [_Exchange(qs_a)], name="wgrad_a_in")
    (gx, g_nw0), (red_a, (l_win_a,)) = _bwd_a_in(
        dz, dx1, xs, nw0, win_a8, [_SumGather(accs_a[1:], lands_a[1:]), _ExchangeRest(q_win_a, rel_a)],
        tm=TM_BWD_A_IN, relay_step=RELAY_STEP_BWD_A_IN)

    r_ga, r_gx, r_pack_b = red_b
    r_nfw, r_nw1, r_p8, loss = _unpack(r_pack_b, shapes_b)
    r_ws, r_pack_a = red_a
    r_lnw, r_lnb, r_bst = _unpack(r_pack_a, shapes_a)
    g_p8 = lax.dynamic_slice_in_dim(r_p8, me * (BW // NDEV), BW // NDEV, axis=1)
    loss = loss[0, 0]

    weights = dict(norm_w=norm_w, a_w_in=a_w_in, a_ln_w=a_ln_w, a_ln_b=a_ln_b, a_w_s=a_w_s, a_b_s=a_b_s, a_w_out=a_w_out,
                   b_w_in=b_w_in, b_conv_w=b_conv_w, b_conv_b=b_conv_b, b_gate_a_w=b_gate_a_w, b_gate_a_b=b_gate_a_b,
                   b_gate_x_w=b_gate_x_w, b_gate_x_b=b_gate_x_b, b_lambda=b_lambda, b_w_out=b_w_out, norm_f_w=norm_f_w)
    mom1 = dict(norm_w=m_norm_w, a_w_in=m_a_w_in, a_ln_w=m_a_ln_w, a_ln_b=m_a_ln_b, a_w_s=m_a_w_s, a_b_s=m_a_b_s,
                a_w_out=m_a_w_out, b_w_in=m_b_w_in, b_conv_w=m_b_conv_w, b_conv_b=m_b_conv_b, b_gate_a_w=m_b_gate_a_w,
                b_gate_a_b=m_b_gate_a_b, b_gate_x_w=m_b_gate_x_w, b_gate_x_b=m_b_gate_x_b, b_lambda=m_b_lambda,
                b_w_out=m_b_w_out, norm_f_w=m_norm_f_w)
    mom2 = dict(norm_w=v_norm_w, a_w_in=v_a_w_in, a_ln_w=v_a_ln_w, a_ln_b=v_a_ln_b, a_w_s=v_a_w_s, a_b_s=v_a_b_s,
                a_w_out=v_a_w_out, b_w_in=v_b_w_in, b_conv_w=v_b_conv_w, b_conv_b=v_b_conv_b, b_gate_a_w=v_b_gate_a_w,
                b_gate_a_b=v_b_gate_a_b, b_gate_x_w=v_b_gate_x_w, b_gate_x_b=v_b_gate_x_b, b_lambda=v_b_lambda,
                b_w_out=v_b_w_out, norm_f_w=v_norm_f_w)
    names = list(weights)

    def as2d(a):
        return a.reshape(-1, a.shape[-1])

    upd, grads = {}, {}
    for k, acc, land in (("a_w_in", acc_win_a, l_win_a), ("a_w_out", accs_a[0], lands_a[0]),
                         ("b_w_in", acc_win_b, l_win_b), ("b_w_out", acc_wout_b, l_wout_b)):
        g, d, mo, vo = _adam_big(as2d(weights[k]), acc, land, as2d(mom1[k]), as2d(mom2[k]), "adam_" + k)
        grads[k] = g[None]
        upd[k] = (d, mo, vo)
    grads.update(
        norm_w=jnp.concatenate([g_nw0, r_nw1], axis=0), a_ln_w=r_lnw, a_ln_b=r_lnb,
        a_w_s=r_ws.reshape(1, G, CH, CH), a_b_s=r_bst.T[None],
        b_conv_w=g_p8[None, 0:4], b_conv_b=g_p8[4:5], b_gate_a_w=r_ga.reshape(1, BH, HD, HD), b_gate_a_b=g_p8[5:6],
        b_gate_x_w=r_gx.reshape(1, BH, HD, HD), b_gate_x_b=g_p8[6:7], b_lambda=g_p8[7:8], norm_f_w=r_nfw.reshape(D))
    small_names = [k for k in names if k not in upd]
    res = _adam_small([(as2d(weights[k]), as2d(grads[k]), as2d(mom1[k]), as2d(mom2[k])) for k in small_names])
    for k, r3 in zip(small_names, res):
        upd[k] = r3
    deltas = [upd[k][0].reshape(weights[k].shape) for k in names]
    new_m = [upd[k][1].reshape(weights[k].shape) for k in names]
    new_v = [upd[k][2].reshape(weights[k].shape) for k in names]
    return (loss, gx[None], *[grads[k] for k in names], *deltas, *new_m, *new_v)
```

```python
import jax
import jax.numpy as jnp
from jax import lax
from jax.experimental import pallas as pl
from jax.experimental.pallas import tpu as pltpu

F32 = jnp.float32
BF16 = jnp.bfloat16
MESH = pl.DeviceIdType.MESH

NDEV = 8
NCHIP_OTHER = 3
D = 1024
AW = 2048
G = 8
GD = AW // G
CH = 128
BW = 1536
BH = 12
HD = BW // BH
CA = 3 * AW // NDEV
CB = 2 * BW // NDEV
RMS_EPS = 1e-6
LN_EPS = 1e-5
RG_C = 8.0
LR, B1, B2, ADAM_EPS, WD, STEP = 0.001, 0.9, 0.999, 1e-08, 0.01, 10
V7X_VMEM_BYTES = 64 * 1024 * 1024
VMEM_LIMIT = V7X_VMEM_BYTES - 8 * 1024 * 1024
SUBLANES = 8
LANES = 128
BF16_ROWS = 16
GELU_C = 0.7978845608028654
GELU_K = 0.044715

_VMEM = pl.BlockSpec(memory_space=pltpu.VMEM)
_HBM = pl.BlockSpec(memory_space=pltpu.HBM)


def _sds(shape, dtype):
    return jax.ShapeDtypeStruct(tuple(shape), dtype)


def _params(**kw):
    return pltpu.CompilerParams(vmem_limit_bytes=VMEM_LIMIT, **kw)


def _gelu_t(z):
    t = jnp.tanh(GELU_C * (z + GELU_K * (z * z * z)))
    return 0.5 * z * (1.0 + t), t


def _dgelu(z, t):
    return 0.5 * (1.0 + t) + 0.5 * z * (1.0 - t * t) * (GELU_C * (1.0 + 3.0 * GELU_K * z * z))


def _sigmoid(v):
    return 0.5 * jnp.tanh(0.5 * v) + 0.5


def _softplus_neg(lam):
    return jnp.maximum(-lam, 0.0) + jnp.log1p(jnp.exp(-jnp.abs(lam)))


def _dot(a, b):
    return jnp.dot(a, b, preferred_element_type=F32)


def _dot_nt(a, b):
    return lax.dot_general(a, b, (((1,), (1,)), ((), ())), preferred_element_type=F32)


def _rowsum(v):
    return jnp.sum(v, axis=0, keepdims=True)


def _causal_mask():
    r = lax.broadcasted_iota(jnp.int32, (CH, CH), 0)
    c = lax.broadcasted_iota(jnp.int32, (CH, CH), 1)
    return r >= c


def _rms(x):
    return lax.rsqrt(jnp.mean(x * x, axis=-1, keepdims=True) + RMS_EPS)


def _rms_bwd(dh, x, r, nw):
    gy = dh * nw
    return r * gy - x * (r * r * r) * jnp.mean(gy * x, axis=-1, keepdims=True)


def _place():
    return lax.axis_index("x"), lax.axis_index("y"), lax.axis_index("c")


def _other_chips(x, y):
    return [(1 - x, y), (x, 1 - y), (1 - x, 1 - y)]


GATHER_SLOTS = 10


def _gather_ops(ins, outs, send_sems, recv_sems, local_sems):
    n = len(ins)
    x, y, c = _place()
    sibling = (x, y, 1 - c)
    xn, yn, dg = _other_chips(x, y)
    split = [ins[i].shape[0] % (2 * BF16_ROWS) == 0 for i in range(n)]

    def blk(chip, core):
        return 4 * chip[0] + 2 * chip[1] + core

    me = blk((x, y), c)

    def part(ref, i, half):
        if half is None:
            return ref
        h = ins[i].shape[0] // 2
        return ref.at[pl.ds(half * h, h)]

    def copy(i, k, block, to, half=None, src=None):
        dst = part(outs[i].at[block], i, half)
        return pltpu.make_async_remote_copy(
            src_ref=dst if src is None else part(src, i, half), dst_ref=dst,
            send_sem=send_sems.at[k, i], recv_sem=recv_sems.at[k, i], device_id=to, device_id_type=MESH)

    def first_copies():
        mine = [pltpu.make_async_copy(ins[i], outs[i].at[me], local_sems.at[i]) for i in range(n)]
        first = []
        for i in range(n):
            first.append(copy(i, 0, me, sibling, src=ins[i]))
            if split[i]:
                first.append(copy(i, 1, me, (*xn, c), 0, ins[i]))
                first.append(copy(i, 3, me, (*yn, c), 1, ins[i]))
                first.append(copy(i, 2, me, (*xn, c), 1, ins[i]))
                first.append(copy(i, 4, me, (*yn, c), 0, ins[i]))
            else:
                first.append(copy(i, 1, me, (*xn, c), None, ins[i]))
                first.append(copy(i, 3, me, (*yn, c), None, ins[i]))
                first.append(copy(i, 5, me, (*dg, c), None, ins[i]))
        return mine, first

    def onward():
        out = []
        for i in range(n):
            if split[i]:
                out.append(copy(i, 5, blk(xn, c), (*yn, c), 0))
                out.append(copy(i, 6, blk(yn, c), (*xn, c), 1))
        return out

    def start():
        mine, first = first_copies()
        for cp in mine + first:
            cp.start()

    def relay():
        sends = onward()
        for i in range(n):
            if split[i]:
                copy(i, 1, blk(xn, c), sibling, 0).wait_recv()
                sends.pop(0).start()
                copy(i, 3, blk(yn, c), sibling, 1).wait_recv()
                sends.pop(0).start()

    def finish():
        mine, first = first_copies()
        passed = []

        def pass_on(i, j, chip):
            fwd = copy(i, 7 + j, blk(chip, c), sibling)
            fwd.start()
            passed.append(fwd)

        for i in range(n):
            if split[i]:
                copy(i, 2, blk(xn, c), sibling, 1).wait_recv()
                pass_on(i, 0, xn)
                copy(i, 4, blk(yn, c), sibling, 0).wait_recv()
                pass_on(i, 1, yn)
                copy(i, 5, blk(dg, c), sibling, 0).wait_recv()
                copy(i, 6, blk(dg, c), sibling, 1).wait_recv()
                pass_on(i, 2, dg)
            else:
                copy(i, 1, blk(xn, c), sibling).wait_recv()
                pass_on(i, 0, xn)
                copy(i, 3, blk(yn, c), sibling).wait_recv()
                pass_on(i, 1, yn)
                copy(i, 5, blk(dg, c), sibling).wait_recv()
                pass_on(i, 2, dg)
        for i in range(n):
            copy(i, 0, blk((x, y), 1 - c), sibling).wait_recv()
            for j, chip in enumerate((xn, yn, dg)):
                copy(i, 7 + j, blk(chip, 1 - c), sibling).wait_recv()
        for cp in first + passed + onward():
            cp.wait_send()
        for cp in mine:
            cp.wait()

    return start, relay, finish


def _gather_sems(n):
    return [pltpu.SemaphoreType.DMA((GATHER_SLOTS, n)), pltpu.SemaphoreType.DMA((GATHER_SLOTS, n)),
            pltpu.SemaphoreType.DMA((n,))]


class _Gather:
    def __init__(self, shards, as_dtypes=None):
        n = len(shards)
        dts = [s.dtype for s in shards] if as_dtypes is None else list(as_dtypes)
        self.cast = [jnp.dtype(d) != s.dtype for d, s in zip(dts, shards)]
        self.ins = list(shards)
        self.in_specs = [_VMEM if c else _HBM for c in self.cast]
        self.out_shape = [_sds((NDEV,) + s.shape, d) for s, d in zip(shards, dts)]
        self.out_specs = [_HBM] * n
        self.scratch = [pltpu.VMEM(s.shape, d) for s, d, c in zip(shards, dts, self.cast) if c] + _gather_sems(n)

    def ops(self, ins, outs, scr):
        ncast = sum(self.cast)
        staged = iter(scr[:ncast])
        srcs = [next(staged) if c else ref for c, ref in zip(self.cast, ins)]
        start, relay, finish = _gather_ops(srcs, outs, *scr[ncast:])

        def cast_and_start():
            for c, ref, src in zip(self.cast, ins, srcs):
                if c:
                    src[...] = ref[...].astype(src.dtype)
            start()

        return cast_and_start, relay, finish


class _Exchange:
    def __init__(self, qs):
        n = len(qs)
        self.ins, self.in_specs = list(qs), [_HBM] * n
        self.out_shape = [_sds(q.shape, q.dtype) for q in qs]
        self.out_specs = [_HBM] * n
        self.scratch = [pltpu.SemaphoreType.DMA((NCHIP_OTHER, n)), pltpu.SemaphoreType.DMA((NCHIP_OTHER, n))]

    def ops(self, ins, outs, scr):
        send_sems, recv_sems = scr
        n = len(ins)
        x, y, c = _place()
        chips = _other_chips(x, y)

        def copies():
            return [pltpu.make_async_remote_copy(
                src_ref=ins[i].at[j], dst_ref=outs[i].at[j], send_sem=send_sems.at[j, i],
                recv_sem=recv_sems.at[j, i], device_id=(*chips[j], c), device_id_type=MESH)
                for i in range(n) for j in range(NCHIP_OTHER)]

        def start():
            for cp in copies():
                cp.start()

        def finish():
            cps = copies()
            for cp in cps:
                cp.wait_recv()
            for cp in cps:
                cp.wait_send()

        return start, lambda: None, finish


class _ExchangeVia:
    def __init__(self, q):
        _, r, cd = q.shape
        half = (2, r // 2, cd)
        self.ins, self.in_specs = [q], [_HBM]
        self.out_shape, self.out_specs = [_sds((2, r, cd), q.dtype)], [_HBM]
        self.scratch = [pltpu.VMEM(half, q.dtype), pltpu.VMEM(half, q.dtype), pltpu.VMEM(half, q.dtype),
                        pltpu.SemaphoreType.DMA((6,)), pltpu.SemaphoreType.DMA((6,)), pltpu.SemaphoreType.DMA((2,))]

    def ops(self, ins, outs, scr):
        (q,), (land,) = ins, outs
        relayed, own, comb, send_sems, recv_sems, local_sems = scr
        h = q.shape[1] // 2
        x, y, c = _place()
        xn, yn, _ = _other_chips(x, y)
        h0, h1 = pl.ds(0, h), pl.ds(h, h)

        def remote(k, src, dst, chip):
            return pltpu.make_async_remote_copy(src_ref=src, dst_ref=dst, send_sem=send_sems.at[k],
                                                recv_sem=recv_sems.at[k], device_id=(*chip, c), device_id_type=MESH)

        def via():
            return [remote(2, q.at[2, h0], relayed.at[0], xn), remote(3, q.at[2, h1], relayed.at[1], yn)]

        def direct():
            return [remote(0, q.at[0, h0], land.at[0, h0], xn), remote(1, q.at[1, h1], land.at[1, h1], yn)]

        def second():
            return [remote(4, comb.at[0], land.at[1, h0], yn), remote(5, comb.at[1], land.at[0, h1], xn)]

        def mine():
            return [pltpu.make_async_copy(q.at[1, h0], own.at[0], local_sems.at[0]),
                    pltpu.make_async_copy(q.at[0, h1], own.at[1], local_sems.at[1])]

        def start():
            for cp in via() + direct() + mine():
                cp.start()

        def relay():
            arrived, loaded, onward = via(), mine(), second()
            for k in range(2):
                arrived[k].wait_recv()
                loaded[k].wait()
                comb[k] = (own[k].astype(F32) + relayed[k].astype(F32)).astype(comb.dtype)
                onward[k].start()

        def finish():
            landing = direct() + second()
            for cp in landing:
                cp.wait_recv()
            for cp in via() + landing:
                cp.wait_send()

        return start, relay, finish


class _SumGather:
    def __init__(self, accs, lands):
        n = len(accs)
        self.n = n
        self.ins, self.in_specs = list(accs) + list(lands), [_VMEM] * (2 * n)
        self.out_shape = [_sds((NDEV,) + a.shape, a.dtype) for a in accs]
        self.out_specs = [_HBM] * n
        self.scratch = [pltpu.VMEM(a.shape, a.dtype) for a in accs] + _gather_sems(n)

    def ops(self, ins, outs, scr):
        n = self.n
        accs, lands, mine = ins[:n], ins[n:], scr[:n]
        g_start, relay, finish = _gather_ops(mine, outs, *scr[n:])

        def start():
            for i in range(n):
                mine[i][...] = accs[i][...] + lands[i][0] + lands[i][1] + lands[i][2]
            g_start()

        return start, relay, finish


def _call(main, jobs, *, name, grid, ins, in_specs, out_shape, out_specs, scratch, relay_step=0):
    nsteps = grid[0] if grid else 1
    n_in, n_out, n_scr = len(ins), len(out_shape), len(scratch)

    def body(*refs):
        pos = [0]

        def take(k):
            r = refs[pos[0]:pos[0] + k]
            pos[0] += k
            return r

        m_in = take(n_in)
        j_in = [take(len(j.ins)) for j in jobs]
        m_out = take(n_out)
        j_out = [take(len(j.out_shape)) for j in jobs]
        m_scr = take(n_scr)
        j_scr = [take(len(j.scratch)) for j in jobs]
        ops = [j.ops(a, b, s) for j, a, b, s in zip(jobs, j_in, j_out, j_scr)]
        i = pl.program_id(0) if grid else 0
        if not grid:
            for o in ops:
                o[0]()
            main(i, m_in, m_out, m_scr)
            for o in ops:
                o[1]()
            for o in ops:
                o[2]()
            return

        if ops:
            @pl.when(i == 0)
            def _():
                for o in ops:
                    o[0]()

        main(i, m_in, m_out, m_scr)

        if ops:
            @pl.when(i == min(relay_step, nsteps - 1))
            def _():
                for o in ops:
                    o[1]()

            @pl.when(i == nsteps - 1)
            def _():
                for o in ops:
                    o[2]()

    extra = dict(dimension_semantics=("arbitrary",)) if grid else {}
    res = pl.pallas_call(
        body, name=name, grid=grid,
        in_specs=list(in_specs) + [s for j in jobs for s in j.in_specs],
        out_specs=list(out_specs) + [s for j in jobs for s in j.out_specs],
        out_shape=list(out_shape) + [s for j in jobs for s in j.out_shape],
        scratch_shapes=list(scratch) + [s for j in jobs for s in j.scratch],
        compiler_params=_params(**extra),
    )(*ins, *[a for j in jobs for a in j.ins])
    main_out, rest, job_out = res[:n_out], res[n_out:], []
    for j in jobs:
        k = len(j.out_shape)
        job_out.append(rest[:k])
        rest = rest[k:]
    return main_out, job_out


def _comm_only(jobs, name):
    _, job_out = _call(lambda i, a, b, s: None, jobs, name=name, grid=(), ins=[], in_specs=[], out_shape=[],
                       out_specs=[], scratch=[])
    return job_out


class _InChip:
    def __init__(self, ps):
        n = len(ps)
        self.n = n
        blk = [p.shape[1:] for p in ps]
        self.ins, self.in_specs = list(ps), [_HBM] * n
        self.out_shape = [_sds((NCHIP_OTHER,) + b, p.dtype) for b, p in zip(blk, ps)] + [_sds(b, F32) for b in blk]
        self.out_specs = [_VMEM] * (2 * n)
        self.scratch = ([pltpu.VMEM((4,) + b, p.dtype) for b, p in zip(blk, ps)] * 2
                        + [pltpu.SemaphoreType.DMA((4, n))] * 3)

    def ops(self, ins, outs, scr):
        n = self.n
        q_refs, acc_refs = outs[:n], outs[n:]
        mines, lands = scr[:n], scr[n:2 * n]
        send_sems, recv_sems, local_sems = scr[2 * n:]
        x, y, c = _place()
        sibling = (x, y, 1 - c)

        def copies():
            out = []
            for i in range(n):
                for pi in range(4):
                    loc = pltpu.make_async_copy(ins[i].at[2 * pi + c], mines[i].at[pi], local_sems.at[pi, i])
                    cp = pltpu.make_async_remote_copy(
                        src_ref=ins[i].at[2 * pi + (1 - c)], dst_ref=lands[i].at[pi],
                        send_sem=send_sems.at[pi, i], recv_sem=recv_sems.at[pi, i],
                        device_id=sibling, device_id_type=MESH)
                    out.append((loc, cp))
            return out

        def start():
            for loc, cp in copies():
                loc.start()
                cp.start()

        def finish():
            pairs = copies()
            for loc, cp in pairs:
                loc.wait()
                cp.wait_recv()
            for i in range(n):
                _chip_sums(mines[i], lands[i], q_refs[i], acc_refs[i], x, y)
            for _, cp in pairs:
                cp.wait_send()

        return start, lambda: None, finish


def _chip_sums(mine, land, q_ref, acc_ref, x, y):
    for j, (qx, qy) in enumerate(_other_chips(x, y)):
        qi = 2 * qx + qy
        q_ref[j] = (mine[qi].astype(F32) + land[qi].astype(F32)).astype(q_ref.dtype)
    mi = 2 * x + y
    acc_ref[...] = mine[mi].astype(F32) + land[mi].astype(F32)


def _direct_sum(v, buf, send_sems, recv_sems):
    x, y, c = _place()
    me = 4 * x + 2 * y + c
    buf[me] = v
    cps = []
    for k in range(1, NDEV):
        fx, fy, fc = (k >> 2) & 1, (k >> 1) & 1, k & 1
        peer = ((1 - x) if fx else x, (1 - y) if fy else y, (1 - c) if fc else c)
        cps.append((peer, pltpu.make_async_remote_copy(
            src_ref=buf.at[me], dst_ref=buf.at[me], send_sem=send_sems.at[k - 1], recv_sem=recv_sems.at[k - 1],
            device_id=peer, device_id_type=MESH)))
    for _, cp in cps:
        cp.start()
    for k, (peer, _) in enumerate(cps):
        theirs = 4 * peer[0] + 2 * peer[1] + peer[2]
        pltpu.make_async_remote_copy(
            src_ref=buf.at[theirs], dst_ref=buf.at[theirs], send_sem=send_sems.at[k], recv_sem=recv_sems.at[k],
            device_id=peer, device_id_type=MESH).wait_recv()
    acc = buf[0]
    for j in range(1, NDEV):
        acc = acc + buf[j]
    for _, cp in cps:
        cp.wait_send()
    return acc


def _direct_sum_scratch(shape, dtype):
    return [pltpu.VMEM((NDEV,) + tuple(shape), dtype), pltpu.SemaphoreType.DMA((NDEV - 1,)),
            pltpu.SemaphoreType.DMA((NDEV - 1,))]


def _fwd_a(x, nw, win8, lnw, lnb, ws, bst, jobs, *, tm, relay_step):
    s_len = x.shape[0]
    nt = s_len // tm
    nch = tm // CH

    def main(i, ins, outs, scr):
        x_ref, nw_ref, win_ref, lnw_ref, lnb_ref, ws_ref, bst_ref = ins
        z_ref, h_ref, y_ref = outs
        wc_scr, gv_scr = scr

        @pl.when(i == 0)
        def _():
            m = _causal_mask()
            for g in range(G):
                wc_scr[g] = jnp.where(m, ws_ref[g], 0.0).astype(BF16)

        x = x_ref[...]
        h = (x * _rms(x) * nw_ref[...]).astype(BF16)
        h_ref[...] = h
        for k in range(NDEV):
            z_ref[:, k * CA:(k + 1) * CA] = _dot(h, win_ref[k])

        ssum = jnp.zeros((tm, 1), F32)
        for g in range(G):
            gv = _gelu_t(z_ref[:, AW + g * GD:AW + (g + 1) * GD])[0]
            gv_scr[:, g * GD:(g + 1) * GD] = gv
            ssum = ssum + jnp.sum(gv, axis=-1, keepdims=True)
        mu = ssum * (1.0 / AW)
        vsum = jnp.zeros((tm, 1), F32)
        for g in range(G):
            dlt = gv_scr[:, g * GD:(g + 1) * GD] - mu
            vsum = vsum + jnp.sum(dlt * dlt, axis=-1, keepdims=True)
        rstd = lax.rsqrt(vsum * (1.0 / AW) + LN_EPS)

        for g in range(G):
            cs = slice(g * GD, (g + 1) * GD)
            v = (gv_scr[:, cs] - mu) * rstd * lnw_ref[:, cs] + lnb_ref[:, cs]
            vb = v.astype(BF16)
            u = _gelu_t(z_ref[:, cs])[0]
            zg = z_ref[:, 2 * AW + g * GD:2 * AW + (g + 1) * GD]
            sg = zg * _sigmoid(zg)
            for n in range(nch):
                rs = slice(n * CH, (n + 1) * CH)
                s = _dot(wc_scr[g], vb[rs, :]) + bst_ref[:, g:g + 1]
                y_ref[rs, cs] = (u[rs, :] * s * sg[rs, :]).astype(BF16)

    tile = lambda w: pl.BlockSpec((tm, w), lambda i: (i, 0))
    return _call(
        main, jobs, name="fwd_a", grid=(nt,), relay_step=relay_step,
        ins=[x, nw, win8, lnw, lnb, ws, bst], in_specs=[tile(D), _VMEM, _VMEM, _VMEM, _VMEM, _VMEM, _VMEM],
        out_shape=[_sds((s_len, 3 * AW), F32), _sds((s_len, D), BF16), _sds((s_len, AW), BF16)],
        out_specs=[tile(3 * AW), tile(D), tile(AW)],
        scratch=[pltpu.VMEM((G, CH, CH), BF16), pltpu.VMEM((tm, AW), F32)])


def _bwd_a(dx1, z, lnw, lnb, ws, bst, wout, jobs, *, tm, relay_step):
    s_len = dx1.shape[0]
    nt = s_len // tm
    nch = tm // CH

    def main(i, ins, outs, scr):
        dx1_ref, z_ref, lnw_ref, lnb_ref, ws_ref, bst_ref, wout_ref = ins
        dz_ref, glnw_ref, glnb_ref, gws_ref, gbst_ref = outs
        wc_scr, wct_scr, vh_scr, dgv_scr, dy_scr, dv_scr, gbs_acc, gwc_acc = scr

        @pl.when(i == 0)
        def _():
            m = _causal_mask()
            for g in range(G):
                wm = jnp.where(m, ws_ref[g], 0.0)
                wc_scr[g] = wm.astype(BF16)
                wct_scr[g] = wm.T.astype(BF16)
            glnw_ref[...] = jnp.zeros_like(glnw_ref)
            glnb_ref[...] = jnp.zeros_like(glnb_ref)
            gbs_acc[...] = jnp.zeros_like(gbs_acc)
            gwc_acc[...] = jnp.zeros_like(gwc_acc)

        dy_scr[...] = _dot_nt(dx1_ref[...], wout_ref[...])

        ssum = jnp.zeros((tm, 1), F32)
        for g in range(G):
            cs = slice(g * GD, (g + 1) * GD)
            zv = z_ref[:, AW + g * GD:AW + (g + 1) * GD]
            gv, t = _gelu_t(zv)
            vh_scr[:, cs] = gv
            dgv_scr[:, cs] = _dgelu(zv, t)
            ssum = ssum + jnp.sum(gv, axis=-1, keepdims=True)
        mu = ssum * (1.0 / AW)
        vsum = jnp.zeros((tm, 1), F32)
        for g in range(G):
            dlt = vh_scr[:, g * GD:(g + 1) * GD] - mu
            vsum = vsum + jnp.sum(dlt * dlt, axis=-1, keepdims=True)
        rstd = lax.rsqrt(vsum * (1.0 / AW) + LN_EPS)

        m1 = jnp.zeros((tm, 1), F32)
        m2 = jnp.zeros((tm, 1), F32)
        for g in range(G):
            cs = slice(g * GD, (g + 1) * GD)
            gs = slice(2 * AW + g * GD, 2 * AW + (g + 1) * GD)
            vhat = (vh_scr[:, cs] - mu) * rstd
            vh_scr[:, cs] = vhat
            vb = (vhat * lnw_ref[:, cs] + lnb_ref[:, cs]).astype(BF16)
            zu = z_ref[:, cs]
            u, tu = _gelu_t(zu)
            zg = z_ref[:, gs]
            sig = _sigmoid(zg)
            sg = zg * sig
            dy = dy_scr[:, cs]
            dsf = dy * u * sg
            dsb = dsf.astype(BF16)
            dvs = []
            for n in range(nch):
                rs = slice(n * CH, (n + 1) * CH)
                s = _dot(wc_scr[g], vb[rs, :]) + bst_ref[:, g:g + 1]
                dys = dy[rs, :] * s
                dz_ref[rs, cs] = (dys * sg[rs, :] * _dgelu(zu[rs, :], tu[rs, :])).astype(BF16)
                dz_ref[rs, gs] = (dys * u[rs, :] * (sig[rs, :] * (1.0 + zg[rs, :] * (1.0 - sig[rs, :])))).astype(BF16)
                gbs_acc[g] += dsf[rs, :]
                gwc_acc[g] += _dot_nt(dsb[rs, :], vb[rs, :])
                dvs.append(_dot(wct_scr[g], dsb[rs, :]))
            dv = jnp.concatenate(dvs, axis=0) if nch > 1 else dvs[0]
            glnw_ref[:, cs] += _rowsum(dv * vhat)
            glnb_ref[:, cs] += _rowsum(dv)
            dvh = dv * lnw_ref[:, cs]
            dv_scr[:, cs] = dvh
            m1 = m1 + jnp.sum(dvh, axis=-1, keepdims=True)
            m2 = m2 + jnp.sum(dvh * vhat, axis=-1, keepdims=True)
        m1 = m1 * (1.0 / AW)
        m2 = m2 * (1.0 / AW)
        for g in range(G):
            cs = slice(g * GD, (g + 1) * GD)
            dgv = rstd * (dv_scr[:, cs] - m1 - vh_scr[:, cs] * m2)
            dz_ref[:, AW + g * GD:AW + (g + 1) * GD] = (dgv * dgv_scr[:, cs]).astype(BF16)

        @pl.when(i == nt - 1)
        def _():
            m = _causal_mask()
            for g in range(G):
                gws_ref[g] = jnp.where(m, gwc_acc[g], 0.0)
                gbst_ref[:, g:g + 1] = jnp.sum(gbs_acc[g], axis=-1, keepdims=True)

    tile = lambda w: pl.BlockSpec((tm, w), lambda i: (i, 0))
    whole = lambda *s: pl.BlockSpec(s, lambda i: (0,) * len(s))
    big = lambda dt: pltpu.VMEM((tm, AW), dt)
    return _call(
        main, jobs, name="bwd_a", grid=(nt,), relay_step=relay_step,
        ins=[dx1, z, lnw, lnb, ws, bst, wout], in_specs=[tile(D), tile(3 * AW), _VMEM, _VMEM, _VMEM, _VMEM, _VMEM],
        out_shape=[_sds((s_len, 3 * AW), BF16), _sds((1, AW), F32), _sds((1, AW), F32), _sds((G, CH, CH), F32),
                   _sds((CH, G), F32)],
        out_specs=[tile(3 * AW), whole(1, AW), whole(1, AW), whole(G, CH, CH), whole(CH, G)],
        scratch=[pltpu.VMEM((G, CH, CH), BF16), pltpu.VMEM((G, CH, CH), BF16), big(F32), big(F32), big(F32), big(F32),
                 pltpu.VMEM((G, CH, GD), F32), pltpu.VMEM((G, CH, CH), F32)])


def _bwd_a_in(dz, dx1, x, nw, win8, jobs, *, tm, relay_step):
    s_len = x.shape[0]
    nt = s_len // tm

    def main(i, ins, outs, scr):
        dz_ref, dx1_ref, x_ref, nw_ref, win_ref = ins
        gx_ref, gnw_ref = outs

        @pl.when(i == 0)
        def _():
            gnw_ref[...] = jnp.zeros_like(gnw_ref)

        dh = jnp.zeros((tm, D), F32)
        for k in range(NDEV):
            dh = dh + _dot_nt(dz_ref[:, k * CA:(k + 1) * CA], win_ref[k])
        x = x_ref[...]
        r = _rms(x)
        gx_ref[...] = dx1_ref[...] + _rms_bwd(dh, x, r, nw_ref[...])
        gnw_ref[...] += _rowsum(dh * x * r)

        @pl.when(i == nt - 1)
        def _():
            gnw_ref[...] = _direct_sum(gnw_ref[...], *scr)

    tile = lambda w: pl.BlockSpec((tm, w), lambda i: (i, 0))
    return _call(
        main, jobs, name="bwd_a_in", grid=(nt,), relay_step=relay_step,
        ins=[dz, dx1, x, nw, win8], in_specs=[tile(3 * AW), tile(D), tile(D), _VMEM, _VMEM],
        out_shape=[_sds((s_len, D), F32), _sds((1, D), F32)],
        out_specs=[tile(D), pl.BlockSpec((1, D), lambda i: (0, 0))], scratch=_direct_sum_scratch((1, D), F32))


def _conv(p8_ref, cs, xb, xm1, xm2, xm3):
    xc = p8_ref[4:5, cs] + p8_ref[3:4, cs] * xb
    xc = xc + p8_ref[0:1, cs] * xm3
    xc = xc + p8_ref[1:2, cs] * xm2
    return xc + p8_ref[2:3, cs] * xm1


def _gates(p8_ref, gcat_ref, hh, xc):
    cs = slice(hh * HD, (hh + 1) * HD)
    pre = _dot(xc.astype(BF16), gcat_ref[hh])
    r = _sigmoid(pre[:, :HD] + p8_ref[5:6, cs])
    ig = _sigmoid(pre[:, HD:] + p8_ref[6:7, cs])
    sp = _softplus_neg(p8_ref[7:8, cs])
    la = (-RG_C) * r * sp
    a = jnp.exp(la)
    half_log = 0.5 * jnp.log(jnp.tanh(-la) * (1.0 + a * a))
    return r, ig, sp, a, jnp.exp(half_log), jnp.exp(-half_log)


def _scan_rows(a_ref, b_ref, out_ref, carry, tm, reverse):
    row = lax.broadcasted_iota(jnp.int32, (SUBLANES, BW), 0)
    ngrp = tm // SUBLANES

    def step(j, cr):
        jj = (ngrp - 1 - j) if reverse else j
        off = pl.multiple_of(jj * SUBLANES, SUBLANES)
        a = a_ref[pl.ds(off, SUBLANES), :]
        b = b_ref[pl.ds(off, SUBLANES), :]
        for sh in (1, 2, 4):
            if reverse:
                a_s = pltpu.roll(a, SUBLANES - sh, 0)
                b_s = pltpu.roll(b, SUBLANES - sh, 0)
                m = row < SUBLANES - sh
            else:
                a_s = pltpu.roll(a, sh, 0)
                b_s = pltpu.roll(b, sh, 0)
                m = row >= sh
            b = jnp.where(m, a * b_s + b, b)
            a = jnp.where(m, a * a_s, a)
        o = b + a * cr
        out_ref[pl.ds(off, SUBLANES), :] = o
        return o[0:1, :] if reverse else o[SUBLANES - 1:SUBLANES, :]

    return lax.fori_loop(0, ngrp, step, carry)


def _fwd_b(x, ya, wout_a, nw, win8, p8, gcat, jobs, *, tm, relay_step):
    s_len = x.shape[0]
    nt = s_len // tm

    def main(i, ins, outs, scr):
        x_ref, ya_ref, wouta_ref, nw_ref, win_ref, p8_ref, gcat_ref = ins
        x1_ref, zb_ref, hs_ref, h1_ref, yb_ref, xc_ref, a_ref, cc_ref, r_ref, ig_ref, m_ref = outs
        xbe_scr, b_scr, k_scr, carry_scr = scr

        @pl.when(i == 0)
        def _():
            xbe_scr[0:SUBLANES, :] = jnp.zeros((SUBLANES, BW), F32)
            carry_scr[...] = jnp.zeros_like(carry_scr)

        x1 = x_ref[...] + _dot(ya_ref[...], wouta_ref[...])
        x1_ref[...] = x1
        h = (x1 * _rms(x1) * nw_ref[...]).astype(BF16)
        h1_ref[...] = h
        for k in range(NDEV):
            zb_ref[:, k * CB:(k + 1) * CB] = _dot(h, win_ref[k])
        xbe_scr[SUBLANES:SUBLANES + tm, :] = zb_ref[:, :BW]
        for hh in range(BH):
            cs = slice(hh * HD, (hh + 1) * HD)
            xc = _conv(p8_ref, cs, xbe_scr[SUBLANES:SUBLANES + tm, cs], xbe_scr[7:7 + tm, cs],
                       xbe_scr[6:6 + tm, cs], xbe_scr[5:5 + tm, cs])
            r, ig, _, a, mult, rm = _gates(p8_ref, gcat_ref, hh, xc)
            ixc = ig * xc
            xc_ref[:, cs] = xc
            a_ref[:, cs] = a
            r_ref[:, cs] = r.astype(BF16)
            ig_ref[:, cs] = ig.astype(BF16)
            m_ref[:, cs] = mult.astype(BF16)
            b_scr[:, cs] = mult * ixc
            k_scr[:, cs] = ixc * (a * a * rm)
        xbe_scr[0:SUBLANES, :] = xbe_scr[tm:tm + SUBLANES, :]
        carry_scr[...] = _scan_rows(a_ref, b_scr, hs_ref, carry_scr[...], tm, False)
        for hh in range(BH):
            cs = slice(hh * HD, (hh + 1) * HD)
            gt = zb_ref[:, BW + hh * HD:BW + (hh + 1) * HD]
            hsv = hs_ref[:, cs]
            yb_ref[:, cs] = (hsv * (gt * _sigmoid(gt))).astype(BF16)
            cc_ref[:, cs] = (hsv - b_scr[:, cs]) - k_scr[:, cs]

    tile = lambda w: pl.BlockSpec((tm, w), lambda i: (i, 0))
    wide = lambda dt: _sds((s_len, BW), dt)
    return _call(
        main, jobs, name="fwd_b", grid=(nt,), relay_step=relay_step,
        ins=[x, ya, wout_a, nw, win8, p8, gcat], in_specs=[tile(D), tile(AW), _VMEM, _VMEM, _VMEM, _VMEM, _VMEM],
        out_shape=[_sds((s_len, D), F32), _sds((s_len, 2 * BW), F32), wide(F32), _sds((s_len, D), BF16), wide(BF16),
                   wide(F32), wide(F32), wide(F32), wide(BF16), wide(BF16), wide(BF16)],
        out_specs=[tile(D), tile(2 * BW), tile(BW), tile(D)] + [tile(BW)] * 7,
        scratch=[pltpu.VMEM((tm + SUBLANES, BW), F32), pltpu.VMEM((tm, BW), F32), pltpu.VMEM((tm, BW), F32),
                 pltpu.VMEM((1, BW), F32)])


def _head(x1, yb, wout, nfw, tgt, *, tm):
    s_len = x1.shape[0]

    def main(i, ins, outs, scr):
        x1_ref, yb_ref, wout_ref, nfw_ref, t_ref = ins
        dx2_ref, dx2b_ref, loss_ref, gnfw_ref = outs

        @pl.when(i == 0)
        def _():
            loss_ref[...] = jnp.zeros_like(loss_ref)
            gnfw_ref[...] = jnp.zeros_like(gnfw_ref)

        x2 = x1_ref[...] + _dot(yb_ref[...], wout_ref[...])
        rf = _rms(x2)
        xn = x2 * rf
        e = xn * nfw_ref[...] - t_ref[...]
        loss_ref[...] += (0.5 / D) * jnp.sum(jnp.sum(e * e, axis=-1, keepdims=True), axis=0, keepdims=True)
        dyf = e * (1.0 / D)
        gnfw_ref[...] += _rowsum(dyf * xn)
        dx2 = _rms_bwd(dyf, x2, rf, nfw_ref[...])
        dx2_ref[...] = dx2
        dx2b_ref[...] = dx2.astype(BF16)

    tile = lambda w: pl.BlockSpec((tm, w), lambda i: (i, 0))
    whole = lambda *s: pl.BlockSpec(s, lambda i: (0,) * len(s))
    (dx2, dx2b, loss, gnfw), _ = _call(
        main, [], name="head", grid=(s_len // tm,),
        ins=[x1, yb, wout, nfw, tgt], in_specs=[tile(D), tile(BW), _VMEM, _VMEM, tile(D)],
        out_shape=[_sds((s_len, D), F32), _sds((s_len, D), BF16), _sds((1, 1), F32), _sds((1, D), F32)],
        out_specs=[tile(D), tile(D), whole(1, 1), whole(1, D)], scratch=[])
    return dx2, dx2b, loss, gnfw


def _bwd_b(dx2, zb, hs, x1, saved, nw, win8, p8, gcat, wout, *, tm):
    s_len = x1.shape[0]
    nt = s_len // tm

    def main(i, ins, outs, scr):
        (dx2_ref, zb_ref, hs_ref, x1_ref, xc_ref, a_ref, cc_ref, r_ref, ig_ref, m_ref,
         nw_ref, win_ref, p8_ref, gcat_ref, wout_ref) = ins
        dx1_ref, dx1b_ref, dzb_ref, gp8_ref, gga_ref, ggx_ref, gnw_ref = outs
        ae_scr, an_scr, dhd_scr, dh_scr, dy_scr, dxce_scr, carry_scr, afirst_scr = scr

        @pl.when(i == 0)
        def _():
            gp8_ref[...] = jnp.zeros_like(gp8_ref)
            gga_ref[...] = jnp.zeros_like(gga_ref)
            ggx_ref[...] = jnp.zeros_like(ggx_ref)
            gnw_ref[...] = jnp.zeros_like(gnw_ref)
            dxce_scr[tm:tm + SUBLANES, :] = jnp.zeros((SUBLANES, BW), F32)
            carry_scr[...] = jnp.zeros_like(carry_scr)
            afirst_scr[...] = jnp.zeros_like(afirst_scr)

        dx2 = dx2_ref[...]
        dy_scr[...] = _dot_nt(dx2.astype(BF16), wout_ref[...])
        for hh in range(BH):
            cs = slice(hh * HD, (hh + 1) * HD)
            gs = slice(BW + hh * HD, BW + (hh + 1) * HD)
            gt = zb_ref[:, gs]
            sig = _sigmoid(gt)
            dy = dy_scr[:, cs]
            dhd_scr[:, cs] = dy * (gt * sig)
            dzb_ref[:, gs] = (dy * hs_ref[:, cs] * (sig * (1.0 + gt * (1.0 - sig)))).astype(BF16)

        ae_scr[0:tm, :] = a_ref[...]
        ae_scr[tm:tm + SUBLANES, :] = jnp.broadcast_to(afirst_scr[...], (SUBLANES, BW))
        an_scr[...] = ae_scr[1:1 + tm, :]
        afirst_scr[...] = ae_scr[0:1, :]
        carry_scr[...] = _scan_rows(an_scr, dhd_scr, dh_scr, carry_scr[...], tm, True)

        for hh in range(BH):
            cs = slice(hh * HD, (hh + 1) * HD)
            dh = dh_scr[:, cs]
            mult = m_ref[:, cs].astype(F32)
            ig = ig_ref[:, cs].astype(F32)
            r = r_ref[:, cs].astype(F32)
            xc = xc_ref[:, cs]
            lam = p8_ref[7:8, cs]
            sp = _softplus_neg(lam)
            dla = dh * cc_ref[:, cs]
            gp8_ref[7:8, cs] += _rowsum(dla * ((-RG_C) * r)) * (-_sigmoid(-lam))
            dpr = dla * ((-RG_C) * sp) * (r * (1.0 - r))
            dpi = dh * mult * xc * (ig * (1.0 - ig))
            gp8_ref[5:6, cs] += _rowsum(dpr)
            gp8_ref[6:7, cs] += _rowsum(dpi)
            dcat = jnp.concatenate([dpr, dpi], axis=1).astype(BF16)
            dxc = dh * mult * ig + _dot_nt(dcat, gcat_ref[hh])
            gg = _dot(xc.T.astype(BF16), dcat)
            gga_ref[hh] += gg[:, :HD]
            ggx_ref[hh] += gg[:, HD:]
            dxce_scr[0:tm, cs] = dxc
            gp8_ref[4:5, cs] += _rowsum(dxc)
        for hh in range(BH):
            cs = slice(hh * HD, (hh + 1) * HD)
            xb = zb_ref[:, cs]
            d0, d1 = dxce_scr[0:tm, cs], dxce_scr[1:1 + tm, cs]
            d2, d3 = dxce_scr[2:2 + tm, cs], dxce_scr[3:3 + tm, cs]
            dzb_ref[:, cs] = (p8_ref[3:4, cs] * d0 + p8_ref[2:3, cs] * d1 + p8_ref[1:2, cs] * d2
                              + p8_ref[0:1, cs] * d3).astype(BF16)
            gp8_ref[3:4, cs] += _rowsum(d0 * xb)
            gp8_ref[2:3, cs] += _rowsum(d1 * xb)
            gp8_ref[1:2, cs] += _rowsum(d2 * xb)
            gp8_ref[0:1, cs] += _rowsum(d3 * xb)
        dxce_scr[tm:tm + SUBLANES, :] = dxce_scr[0:SUBLANES, :]

        dh1 = jnp.zeros((tm, D), F32)
        for k in range(NDEV):
            dh1 = dh1 + _dot_nt(dzb_ref[:, k * CB:(k + 1) * CB], win_ref[k])
        x1 = x1_ref[...]
        r1 = _rms(x1)
        dx1 = dx2 + _rms_bwd(dh1, x1, r1, nw_ref[...])
        dx1_ref[...] = dx1
        dx1b_ref[...] = dx1.astype(BF16)
        gnw_ref[...] += _rowsum(dh1 * x1 * r1)

    tile = lambda w: pl.BlockSpec((tm, w), lambda i: (nt - 1 - i, 0))
    whole = lambda *s: pl.BlockSpec(s, lambda i: (0,) * len(s))
    full = lambda: pltpu.VMEM((tm, BW), F32)
    ext = lambda: pltpu.VMEM((tm + SUBLANES, BW), F32)
    out, _ = _call(
        main, [], name="bwd_b", grid=(nt,),
        ins=[dx2, zb, hs, x1, *saved, nw, win8, p8, gcat, wout],
        in_specs=[tile(D), tile(2 * BW), tile(BW), tile(D)] + [tile(BW)] * 6 + [_VMEM] * 5,
        out_shape=[_sds((s_len, D), F32), _sds((s_len, D), BF16), _sds((s_len, 2 * BW), BF16), _sds((SUBLANES, BW), F32),
                   _sds((BH, HD, HD), F32), _sds((BH, HD, HD), F32), _sds((1, D), F32)],
        out_specs=[tile(D), tile(D), tile(2 * BW), whole(SUBLANES, BW), whole(BH, HD, HD), whole(BH, HD, HD),
                   whole(1, D)],
        scratch=[ext(), full(), full(), full(), full(), ext(), pltpu.VMEM((1, BW), F32), pltpu.VMEM((1, BW), F32)])
    return out


def _transpose_into(dst_ref, src_ref, rows):
    s_len = src_ref.shape[0]
    for r0 in range(0, s_len, rows):
        dst_ref[:, r0:r0 + rows] = src_ref[r0:r0 + rows, :].astype(F32).T.astype(BF16)


def _wgrad(a, b, jobs, *, by_rows, per, name, relay_step=0):
    s_len, m = a.shape
    n = b.shape[1]
    r, cd = (m // NDEV, n) if by_rows else (m, n // NDEV)
    nsteps = NDEV // per
    at_rows = per * r if by_rows else m

    def main(i, ins, outs, scr):
        a_ref, b_ref = ins
        q_ref, acc_ref = outs
        at_scr, stage, mine, land, send_sems, recv_sems = scr
        x, y, c = _place()

        def to_sibling(pi):
            return pltpu.make_async_remote_copy(
                src_ref=stage.at[pi & 1], dst_ref=land.at[pi], send_sem=send_sems.at[pi], recv_sem=recv_sems.at[pi],
                device_id=(x, y, 1 - c), device_id_type=MESH)

        if by_rows:
            _transpose_into(at_scr, a_ref, 256)
        else:
            @pl.when(i == 0)
            def _():
                _transpose_into(at_scr, a_ref, 256)

        res = _dot(at_scr[...], b_ref[...]).astype(BF16)
        for k in range(per):
            blk = per * i + k
            pi, pc = blk >> 1, blk & 1
            val = res[k * r:(k + 1) * r, :] if by_rows else res

            @pl.when(pc != c)
            def _():
                @pl.when(pi >= 2)
                def _():
                    to_sibling(pi - 2).wait_send()

                stage[pi & 1] = val
                to_sibling(pi).start()

            @pl.when(pc == c)
            def _():
                mine[pi] = val

        @pl.when(i == nsteps - 1)
        def _():
            for p in range(4):
                to_sibling(p).wait_recv()
            to_sibling(2).wait_send()
            to_sibling(3).wait_send()
            _chip_sums(mine, land, q_ref, acc_ref, x, y)

    if by_rows:
        in_specs = [pl.BlockSpec((s_len, at_rows), lambda j: (0, j)), _VMEM]
    else:
        in_specs = [_VMEM, pl.BlockSpec((s_len, cd), lambda j: (0, j))]
    blk_vmem = lambda k: pltpu.VMEM((k, r, cd), BF16)
    (q, acc), job_out = _call(
        main, jobs, name=name, grid=(nsteps,), relay_step=relay_step, ins=[a, b], in_specs=in_specs,
        out_shape=[_sds((NCHIP_OTHER, r, cd), BF16), _sds((r, cd), F32)],
        out_specs=[pl.BlockSpec((NCHIP_OTHER, r, cd), lambda j: (0, 0, 0)), pl.BlockSpec((r, cd), lambda j: (0, 0))],
        scratch=[pltpu.VMEM((at_rows, s_len), BF16), blk_vmem(2), blk_vmem(4), blk_vmem(4),
                 pltpu.SemaphoreType.DMA((4,)), pltpu.SemaphoreType.DMA((4,))])
    return q, acc, job_out


def _wgrad_cols_early(a, b, jobs, *, name, relay_step=0):
    s_len, m = a.shape
    r, cd = m, b.shape[1] // NDEV
    h = r // 2

    def chip_at(pos, base):
        return base ^ (3 - pos)

    def main(i, ins, outs, scr):
        a_ref, b_ref = ins
        q_ref, acc_ref, rel_ref = outs
        at_scr, stage, mine, land, q2_scr, send_sems, recv_sems, via_send, via_recv = scr
        x, y, c = _place()
        base = 2 * x + y
        xn, yn, _ = _other_chips(x, y)
        pos, pc = i >> 1, i & 1
        pi = chip_at(pos, base)

        def to_sibling(chip, slot):
            return pltpu.make_async_remote_copy(
                src_ref=stage.at[slot], dst_ref=land.at[chip], send_sem=send_sems.at[chip],
                recv_sem=recv_sems.at[chip], device_id=(x, y, 1 - c), device_id_type=MESH)

        def via(k):
            return pltpu.make_async_remote_copy(
                src_ref=q2_scr.at[pl.ds(k * h, h)], dst_ref=rel_ref.at[k], send_sem=via_send.at[k],
                recv_sem=via_recv.at[k], device_id=(*(xn, yn)[k], c), device_id_type=MESH)

        @pl.when(i == 0)
        def _():
            _transpose_into(at_scr, a_ref, 256)

        res = _dot(at_scr[...], b_ref[...]).astype(BF16)

        @pl.when(pc != c)
        def _():
            @pl.when(pos >= 2)
            def _():
                to_sibling(chip_at(pos - 2, base), pos & 1).wait_send()

            stage[pos & 1] = res
            to_sibling(pi, pos & 1).start()

        @pl.when(pc == c)
        def _():
            mine[pi] = res

        @pl.when(i == 1)
        def _():
            dg = chip_at(0, base)
            to_sibling(dg, 0).wait_recv()
            q2 = (mine[dg].astype(F32) + land[dg].astype(F32)).astype(BF16)
            q2_scr[...] = q2
            q_ref[2] = q2
            via(0).start()
            via(1).start()

        @pl.when(i == NDEV - 1)
        def _():
            for pos_ in (1, 2, 3):
                to_sibling(chip_at(pos_, base), 0).wait_recv()
            to_sibling(chip_at(2, base), 0).wait_send()
            to_sibling(chip_at(3, base), 1).wait_send()
            for k in range(2):
                via(k).wait_recv()
            for k in range(2):
                via(k).wait_send()
            for j, chip in enumerate((base ^ 2, base ^ 1)):
                q_ref[j] = (mine[chip].astype(F32) + land[chip].astype(F32)).astype(BF16)
            acc_ref[...] = mine[base].astype(F32) + land[base].astype(F32)

    def b_block(j):
        base = 2 * lax.axis_index("x") + lax.axis_index("y")
        return (0, 2 * chip_at(j >> 1, base) + (j & 1))

    blk_vmem = lambda k: pltpu.VMEM((k, r, cd), BF16)
    (q, acc, rel), job_out = _call(
        main, jobs, name=name, grid=(NDEV,), relay_step=relay_step, ins=[a, b],
        in_specs=[_VMEM, pl.BlockSpec((s_len, cd), b_block)],
        out_shape=[_sds((NCHIP_OTHER, r, cd), BF16), _sds((r, cd), F32), _sds((2, h, cd), BF16)],
        out_specs=[pl.BlockSpec((NCHIP_OTHER, r, cd), lambda j: (0, 0, 0)), pl.BlockSpec((r, cd), lambda j: (0, 0)), _HBM],
        scratch=[pltpu.VMEM((m, s_len), BF16), blk_vmem(2), blk_vmem(4), blk_vmem(4), pltpu.VMEM((r, cd), BF16),
                 pltpu.SemaphoreType.DMA((4,)), pltpu.SemaphoreType.DMA((4,)), pltpu.SemaphoreType.DMA((2,)),
                 pltpu.SemaphoreType.DMA((2,))])
    return q, acc, rel, job_out


class _ExchangeRest:
    def __init__(self, q, relayed):
        _, r, cd = q.shape
        half = (2, r // 2, cd)
        self.ins, self.in_specs = [q, relayed], [_HBM, _HBM]
        self.out_shape, self.out_specs = [_sds((2, r, cd), q.dtype)], [_HBM]
        self.scratch = [pltpu.VMEM(half, q.dtype), pltpu.VMEM(half, q.dtype), pltpu.VMEM(half, q.dtype),
                        pltpu.SemaphoreType.DMA((4,)), pltpu.SemaphoreType.DMA((4,)), pltpu.SemaphoreType.DMA((4,))]

    def ops(self, ins, outs, scr):
        (q, rel_in), (land,) = ins, outs
        own, rel, comb, send_sems, recv_sems, local_sems = scr
        h = q.shape[1] // 2
        x, y, c = _place()
        xn, yn, _ = _other_chips(x, y)
        h0, h1 = pl.ds(0, h), pl.ds(h, h)

        def remote(k, src, dst, chip):
            return pltpu.make_async_remote_copy(src_ref=src, dst_ref=dst, send_sem=send_sems.at[k],
                                                recv_sem=recv_sems.at[k], device_id=(*chip, c), device_id_type=MESH)

        def sends():
            return [remote(0, q.at[0, h0], land.at[0, h0], xn), remote(1, q.at[1, h1], land.at[1, h1], yn),
                    remote(2, comb.at[0], land.at[1, h0], yn), remote(3, comb.at[1], land.at[0, h1], xn)]

        def loads():
            return [pltpu.make_async_copy(q.at[1, h0], own.at[0], local_sems.at[0]),
                    pltpu.make_async_copy(q.at[0, h1], own.at[1], local_sems.at[1]),
                    pltpu.make_async_copy(rel_in.at[0], rel.at[0], local_sems.at[2]),
                    pltpu.make_async_copy(rel_in.at[1], rel.at[1], local_sems.at[3])]

        def start():
            cps, lds = sends(), loads()
            for ld in lds:
                ld.start()
            cps[0].start()
            cps[1].start()
            for ld in lds:
                ld.wait()
            for k in range(2):
                comb[k] = (own[k].astype(F32) + rel[k].astype(F32)).astype(comb.dtype)
            cps[2].start()
            cps[3].start()

        def finish():
            cps = sends()
            for cp in cps:
                cp.wait_recv()
            for cp in cps:
                cp.wait_send()

        return start, lambda: None, finish


def _adam_math(w, g, m, v):
    m = B1 * m + (1.0 - B1) * g
    v = B2 * v + (1.0 - B2) * (g * g)
    m_hat = m / (1.0 - B1 ** STEP)
    v_hat = v / (1.0 - B2 ** STEP)
    delta = (-LR) * (m_hat / (jnp.sqrt(v_hat) + ADAM_EPS) + WD * w)
    return delta, m, v


def _adam_big(w, acc, land, m, v, name):
    r, cd = w.shape
    rb = 256 if r % 256 == 0 else r
    nland = land.shape[0]

    def body(w_ref, acc_ref, land_ref, m_ref, v_ref, g_ref, d_ref, mo_ref, vo_ref):
        g = acc_ref[...]
        for j in range(nland):
            g = g + land_ref[j].astype(F32)
        g_ref[...] = g
        d_ref[...], mo_ref[...], vo_ref[...] = _adam_math(w_ref[...], g, m_ref[...], v_ref[...])

    blk = pl.BlockSpec((rb, cd), lambda i: (i, 0))
    blk3 = pl.BlockSpec((nland, rb, cd), lambda i: (0, i, 0))
    return pl.pallas_call(
        body, name=name, grid=(r // rb,), in_specs=[blk, blk, blk3, blk, blk], out_specs=[blk] * 4,
        out_shape=[_sds((r, cd), F32)] * 4,
        compiler_params=_params(dimension_semantics=("arbitrary",)),
    )(w, acc, land, m, v)


def _adam_small(groups):
    n = len(groups)

    def body(*refs):
        ins, outs = refs[:4 * n], refs[4 * n:]
        for k in range(n):
            w_ref, g_ref, m_ref, v_ref = ins[4 * k:4 * k + 4]
            d, mo, vo = _adam_math(w_ref[...], g_ref[...], m_ref[...], v_ref[...])
            outs[3 * k][...] = d
            outs[3 * k + 1][...] = mo
            outs[3 * k + 2][...] = vo

    flat = [a for grp in groups for a in grp]
    shapes = [_sds(grp[0].shape, F32) for grp in groups for _ in range(3)]
    res = pl.pallas_call(
        body, name="adam_small", in_specs=[_VMEM] * (4 * n), out_specs=[_VMEM] * (3 * n), out_shape=shapes,
        compiler_params=_params(),
    )(*flat)
    return [tuple(res[3 * k:3 * k + 3]) for k in range(n)]


TM_FWD_A = 256
RELAY_STEP_FWD_A = 4
RELAY_STEP_FWD_B = 2
TM_BWD_A = 256
RELAY_STEP_BWD_A = 3
TM_BWD_A_IN = 256
RELAY_STEP_BWD_A_IN = 4
RELAY_STEP_WGRAD_A_IN = 2
TM_FWD_B = 256
TM_HEAD = 512
TM_BWD_B = 256


def _pack(parts, rows):
    flat = jnp.concatenate([p.reshape(-1) for p in parts])
    return jnp.pad(flat, (0, NDEV * rows * LANES - flat.shape[0])).reshape(NDEV, rows, LANES)


def _unpack(packed, shapes):
    flat, out, off = packed.reshape(-1), [], 0
    for s in shapes:
        size = 1
        for d in s:
            size *= d
        out.append(flat[off:off + size].reshape(s))
        off += size
    return out


def kernel(x, norm_w, a_w_in, a_ln_w, a_ln_b, a_w_s, a_b_s, a_w_out, b_w_in, b_conv_w, b_conv_b, b_gate_a_w, b_gate_a_b, b_gate_x_w, b_gate_x_b, b_lambda, b_w_out, norm_f_w, loss_target, m_norm_w, m_a_w_in, m_a_ln_w, m_a_ln_b, m_a_w_s, m_a_b_s, m_a_w_out, m_b_w_in, m_b_conv_w, m_b_conv_b, m_b_gate_a_w, m_b_gate_a_b, m_b_gate_x_w, m_b_gate_x_b, m_b_lambda, m_b_w_out, m_norm_f_w, v_norm_w, v_a_w_in, v_a_ln_w, v_a_ln_b, v_a_w_s, v_a_b_s, v_a_w_out, v_b_w_in, v_b_conv_w, v_b_conv_b, v_b_gate_a_w, v_b_gate_a_b, v_b_gate_x_w, v_b_gate_x_b, v_b_lambda, v_b_w_out, v_norm_f_w):
    me = 4 * lax.axis_index("x") + 2 * lax.axis_index("y") + lax.axis_index("c")
    xs, tgt = x[0], loss_target[0]
    nw0, nw1, nfw = norm_w[0:1], norm_w[1:2], norm_f_w.reshape(1, D)
    w_s, bst = a_w_s[0], a_b_s[0].T
    gcat = jnp.concatenate([b_gate_a_w[0], b_gate_x_w[0]], axis=-1).astype(BF16)

    p8_shard = jnp.concatenate([b_conv_w[0], b_conv_b, b_gate_a_b, b_gate_x_b, b_lambda], axis=0)
    ((win_a8, p8_all),) = _comm_only([_Gather([a_w_in[0], p8_shard], [BF16, F32])], "gather_first")
    p8 = jnp.transpose(p8_all, (1, 0, 2)).reshape(SUBLANES, BW)

    (z, h0, ya), ((wout_a8, win_b8),) = _fwd_a(
        xs, nw0, win_a8, a_ln_w, a_ln_b, w_s, bst, [_Gather([a_w_out[0], b_w_in[0]], [BF16, BF16])],
        tm=TM_FWD_A, relay_step=RELAY_STEP_FWD_A)
    wout_a = wout_a8.reshape(AW, D)
    (x1, zb, hs, h1, yb, *saved_b), ((wout_b8,),) = _fwd_b(
        xs, ya, wout_a, nw1, win_b8, p8, gcat, [_Gather([b_w_out[0]], [BF16])],
        tm=TM_FWD_B, relay_step=RELAY_STEP_FWD_B)
    wout_b = wout_b8.reshape(BW, D)
    dx2, dx2b, loss, g_nfw = _head(x1, yb, wout_b, nfw, tgt, tm=TM_HEAD)

    dx1, dx1b, dzb, g_p8, g_ga, g_gx, g_nw1 = _bwd_b(dx2, zb, hs, x1, saved_b, nw1, win_b8, p8, gcat, wout_b,
                                                     tm=TM_BWD_B)
    q_wout_b, acc_wout_b, _ = _wgrad(yb, dx2b, [], by_rows=True, per=2, name="wgrad_b_out")
    shapes_b = [(1, D), (1, D), (SUBLANES, BW), (1, 1)]
    pack_b = _pack([g_nfw, g_nw1, g_p8, loss], 16)
    small_b = _InChip([g_ga.reshape(NDEV, -1, HD), g_gx.reshape(NDEV, -1, HD), pack_b])
    q_win_b, acc_win_b, (sm_b, (l_wout_b,)) = _wgrad(h1, dzb, [small_b, _Exchange([q_wout_b])], by_rows=False, per=1,
                                                      name="wgrad_b_in")
    qs_b, accs_b = sm_b[:3], sm_b[3:]

    (dz, g_lnw, g_lnb, g_ws, g_bst), (lands_b, (l_win_b,)) = _bwd_a(
        dx1b, z, a_ln_w, a_ln_b, w_s, bst, wout_a, [_Exchange(qs_b), _ExchangeVia(q_win_b)],
        tm=TM_BWD_A, relay_step=RELAY_STEP_BWD_A)
    shapes_a = [(1, AW), (1, AW), (CH, G)]
    pack_a = _pack([g_lnw, g_lnb, g_bst], 8)
    q_wout_a, acc_wout_a, (red_b, sm_a) = _wgrad(
        ya, dx1b, [_SumGather(accs_b, lands_b), _InChip([g_ws, pack_a])], by_rows=True, per=2,
        name="wgrad_a_out", relay_step=1)
    qs_a, accs_a = sm_a[:2], sm_a[2:]
    q_win_a, acc_win_a, rel_a, (lands_a, (l_wout_a,)) = _wgrad_cols_early(
        h0, dz, [_Exchange(qs_a), _ExchangeVia(q_wout_a)], name="wgrad_a_in", relay_step=RELAY_STEP_WGRAD_A_IN)
    (gx, g_nw0), (red_a, (l_win_a,)) = _bwd_a_in(
        dz, dx1, xs, nw0, win_a8, [_SumGather(accs_a, lands_a), _ExchangeRest(q_win_a, rel_a)],
        tm=TM_BWD_A_IN, relay_step=RELAY_STEP_BWD_A_IN)

    r_ga, r_gx, r_pack_b = red_b
    r_nfw, r_nw1, r_p8, loss = _unpack(r_pack_b, shapes_b)
    r_ws, r_pack_a = red_a
    r_lnw, r_lnb, r_bst = _unpack(r_pack_a, shapes_a)
    g_p8 = lax.dynamic_slice_in_dim(r_p8, me * (BW // NDEV), BW // NDEV, axis=1)
    loss = loss[0, 0]

    weights = dict(norm_w=norm_w, a_w_in=a_w_in, a_ln_w=a_ln_w, a_ln_b=a_ln_b, a_w_s=a_w_s, a_b_s=a_b_s, a_w_out=a_w_out,
                   b_w_in=b_w_in, b_conv_w=b_conv_w, b_conv_b=b_conv_b, b_gate_a_w=b_gate_a_w, b_gate_a_b=b_gate_a_b,
                   b_gate_x_w=b_gate_x_w, b_gate_x_b=b_gate_x_b, b_lambda=b_lambda, b_w_out=b_w_out, norm_f_w=norm_f_w)
    mom1 = dict(norm_w=m_norm_w, a_w_in=m_a_w_in, a_ln_w=m_a_ln_w, a_ln_b=m_a_ln_b, a_w_s=m_a_w_s, a_b_s=m_a_b_s,
                a_w_out=m_a_w_out, b_w_in=m_b_w_in, b_conv_w=m_b_conv_w, b_conv_b=m_b_conv_b, b_gate_a_w=m_b_gate_a_w,
                b_gate_a_b=m_b_gate_a_b, b_gate_x_w=m_b_gate_x_w, b_gate_x_b=m_b_gate_x_b, b_lambda=m_b_lambda,
                b_w_out=m_b_w_out, norm_f_w=m_norm_f_w)
    mom2 = dict(norm_w=v_norm_w, a_w_in=v_a_w_in, a_ln_w=v_a_ln_w, a_ln_b=v_a_ln_b, a_w_s=v_a_w_s, a_b_s=v_a_b_s,
                a_w_out=v_a_w_out, b_w_in=v_b_w_in, b_conv_w=v_b_conv_w, b_conv_b=v_b_conv_b, b_gate_a_w=v_b_gate_a_w,
                b_gate_a_b=v_b_gate_a_b, b_gate_x_w=v_b_gate_x_w, b_gate_x_b=v_b_gate_x_b, b_lambda=v_b_lambda,
                b_w_out=v_b_w_out, norm_f_w=v_norm_f_w)
    names = list(weights)

    def as2d(a):
        return a.reshape(-1, a.shape[-1])

    upd, grads = {}, {}
    for k, acc, land in (("a_w_in", acc_win_a, l_win_a), ("a_w_out", acc_wout_a, l_wout_a),
                         ("b_w_in", acc_win_b, l_win_b), ("b_w_out", acc_wout_b, l_wout_b)):
        g, d, mo, vo = _adam_big(as2d(weights[k]), acc, land, as2d(mom1[k]), as2d(mom2[k]), "adam_" + k)
        grads[k] = g[None]
        upd[k] = (d, mo, vo)
    grads.update(
        norm_w=jnp.concatenate([g_nw0, r_nw1], axis=0), a_ln_w=r_lnw, a_ln_b=r_lnb,
        a_w_s=r_ws.reshape(1, G, CH, CH), a_b_s=r_bst.T[None],
        b_conv_w=g_p8[None, 0:4], b_conv_b=g_p8[4:5], b_gate_a_w=r_ga.reshape(1, BH, HD, HD), b_gate_a_b=g_p8[5:6],
        b_gate_x_w=r_gx.reshape(1, BH, HD, HD), b_gate_x_b=g_p8[6:7], b_lambda=g_p8[7:8], norm_f_w=r_nfw.reshape(D))
    small_names = [k for k in names if k not in upd]
    res = _adam_small([(as2d(weights[k]), as2d(grads[k]), as2d(mom1[k]), as2d(mom2[k])) for k in small_names])
    for k, r3 in zip(small_names, res):
        upd[k] = r3
    deltas = [upd[k][0].reshape(weights[k].shape) for k in names]
    new_m = [upd[k][1].reshape(weights[k].shape) for k in names]
    new_v = [upd[k][2].reshape(weights[k].shape) for k in names]
    return (loss, gx[None], *[grads[k] for k in names], *deltas, *new_m, *new_v)
```

```python
import jax
import jax.numpy as jnp
from jax import lax
from jax.experimental import pallas as pl
from jax.experimental.pallas import tpu as pltpu

F32 = jnp.float32
BF16 = jnp.bfloat16
MESH = pl.DeviceIdType.MESH

NDEV = 8
NCHIP_OTHER = 3
D = 1024
AW = 2048
G = 8
GD = AW // G
CH = 128
BW = 1536
BH = 12
HD = BW // BH
CA = 3 * AW // NDEV
CB = 2 * BW // NDEV
RMS_EPS = 1e-6
LN_EPS = 1e-5
RG_C = 8.0
LR, B1, B2, ADAM_EPS, WD, STEP = 0.001, 0.9, 0.999, 1e-08, 0.01, 10
V7X_VMEM_BYTES = 64 * 1024 * 1024
VMEM_LIMIT = V7X_VMEM_BYTES - 8 * 1024 * 1024
SUBLANES = 8
LANES = 128
BF16_ROWS = 16
TRANSPOSE_ROWS = 256
ADAM_ROWS = 512
GELU_C = 0.7978845608028654
GELU_K = 0.044715

_VMEM = pl.BlockSpec(memory_space=pltpu.VMEM)
_HBM = pl.BlockSpec(memory_space=pltpu.HBM)


def _sds(shape, dtype):
    return jax.ShapeDtypeStruct(tuple(shape), dtype)


def _params(**kw):
    return pltpu.CompilerParams(vmem_limit_bytes=VMEM_LIMIT, **kw)


def _gelu_t(z):
    t = jnp.tanh(GELU_C * (z + GELU_K * (z * z * z)))
    return 0.5 * z * (1.0 + t), t


def _dgelu(z, t):
    return 0.5 * (1.0 + t) + 0.5 * z * (1.0 - t * t) * (GELU_C * (1.0 + 3.0 * GELU_K * z * z))


def _sigmoid(v):
    return 0.5 * jnp.tanh(0.5 * v) + 0.5


def _softplus_neg(lam):
    return jnp.maximum(-lam, 0.0) + jnp.log1p(jnp.exp(-jnp.abs(lam)))


def _dot(a, b):
    return jnp.dot(a, b, preferred_element_type=F32)


def _dot_nt(a, b):
    return lax.dot_general(a, b, (((1,), (1,)), ((), ())), preferred_element_type=F32)


def _rowsum(v):
    return jnp.sum(v, axis=0, keepdims=True)


def _causal_mask():
    r = lax.broadcasted_iota(jnp.int32, (CH, CH), 0)
    c = lax.broadcasted_iota(jnp.int32, (CH, CH), 1)
    return r >= c


def _rms(x):
    return lax.rsqrt(jnp.mean(x * x, axis=-1, keepdims=True) + RMS_EPS)


def _rms_bwd(dh, x, r, nw):
    gy = dh * nw
    return r * gy - x * (r * r * r) * jnp.mean(gy * x, axis=-1, keepdims=True)


def _place():
    return lax.axis_index("x"), lax.axis_index("y"), lax.axis_index("c")


def _other_chips(x, y):
    return [(1 - x, y), (x, 1 - y), (1 - x, 1 - y)]


GATHER_SLOTS = 10


def _gather_ops(ins, outs, send_sems, recv_sems, local_sems):
    n = len(ins)
    x, y, c = _place()
    sibling = (x, y, 1 - c)
    xn, yn, dg = _other_chips(x, y)
    split = [ins[i].shape[0] % (2 * BF16_ROWS) == 0 for i in range(n)]

    def blk(chip, core):
        return 4 * chip[0] + 2 * chip[1] + core

    me = blk((x, y), c)

    def part(ref, i, half):
        if half is None:
            return ref
        h = ins[i].shape[0] // 2
        return ref.at[pl.ds(half * h, h)]

    def copy(i, k, block, to, half=None, src=None):
        dst = part(outs[i].at[block], i, half)
        return pltpu.make_async_remote_copy(
            src_ref=dst if src is None else part(src, i, half), dst_ref=dst,
            send_sem=send_sems.at[k, i], recv_sem=recv_sems.at[k, i], device_id=to, device_id_type=MESH)

    def first_copies():
        mine = [pltpu.make_async_copy(ins[i], outs[i].at[me], local_sems.at[i]) for i in range(n)]
        first = []
        for i in range(n):
            first.append(copy(i, 0, me, sibling, src=ins[i]))
            if split[i]:
                first.append(copy(i, 1, me, (*xn, c), 0, ins[i]))
                first.append(copy(i, 3, me, (*yn, c), 1, ins[i]))
                first.append(copy(i, 2, me, (*xn, c), 1, ins[i]))
                first.append(copy(i, 4, me, (*yn, c), 0, ins[i]))
            else:
                first.append(copy(i, 1, me, (*xn, c), None, ins[i]))
                first.append(copy(i, 3, me, (*yn, c), None, ins[i]))
                first.append(copy(i, 5, me, (*dg, c), None, ins[i]))
        return mine, first

    def onward():
        out = []
        for i in range(n):
            if split[i]:
                out.append(copy(i, 5, blk(xn, c), (*yn, c), 0))
                out.append(copy(i, 6, blk(yn, c), (*xn, c), 1))
        return out

    def start():
        mine, first = first_copies()
        for cp in mine + first:
            cp.start()

    def relay():
        sends = onward()
        for i in range(n):
            if split[i]:
                copy(i, 1, blk(xn, c), sibling, 0).wait_recv()
                sends.pop(0).start()
                copy(i, 3, blk(yn, c), sibling, 1).wait_recv()
                sends.pop(0).start()

    def finish():
        mine, first = first_copies()
        passed = []

        def pass_on(i, j, chip):
            fwd = copy(i, 7 + j, blk(chip, c), sibling)
            fwd.start()
            passed.append(fwd)

        for i in range(n):
            if split[i]:
                copy(i, 2, blk(xn, c), sibling, 1).wait_recv()
                pass_on(i, 0, xn)
                copy(i, 4, blk(yn, c), sibling, 0).wait_recv()
                pass_on(i, 1, yn)
                copy(i, 5, blk(dg, c), sibling, 0).wait_recv()
                copy(i, 6, blk(dg, c), sibling, 1).wait_recv()
                pass_on(i, 2, dg)
            else:
                copy(i, 1, blk(xn, c), sibling).wait_recv()
                pass_on(i, 0, xn)
                copy(i, 3, blk(yn, c), sibling).wait_recv()
                pass_on(i, 1, yn)
                copy(i, 5, blk(dg, c), sibling).wait_recv()
                pass_on(i, 2, dg)
        for i in range(n):
            copy(i, 0, blk((x, y), 1 - c), sibling).wait_recv()
            for j, chip in enumerate((xn, yn, dg)):
                copy(i, 7 + j, blk(chip, 1 - c), sibling).wait_recv()
        for cp in first + passed + onward():
            cp.wait_send()
        for cp in mine:
            cp.wait()

    return start, relay, finish


def _gather_sems(n):
    return [pltpu.SemaphoreType.DMA((GATHER_SLOTS, n)), pltpu.SemaphoreType.DMA((GATHER_SLOTS, n)),
            pltpu.SemaphoreType.DMA((n,))]


class _Gather:
    def __init__(self, shards, as_dtypes=None):
        n = len(shards)
        dts = [s.dtype for s in shards] if as_dtypes is None else list(as_dtypes)
        self.cast = [jnp.dtype(d) != s.dtype for d, s in zip(dts, shards)]
        self.ins = list(shards)
        self.in_specs = [_VMEM if c else _HBM for c in self.cast]
        self.out_shape = [_sds((NDEV,) + s.shape, d) for s, d in zip(shards, dts)]
        self.out_specs = [_HBM] * n
        self.scratch = [pltpu.VMEM(s.shape, d) for s, d, c in zip(shards, dts, self.cast) if c] + _gather_sems(n)

    def ops(self, ins, outs, scr):
        ncast = sum(self.cast)
        staged = iter(scr[:ncast])
        srcs = [next(staged) if c else ref for c, ref in zip(self.cast, ins)]
        start, relay, finish = _gather_ops(srcs, outs, *scr[ncast:])

        def cast_and_start():
            for c, ref, src in zip(self.cast, ins, srcs):
                if c:
                    src[...] = ref[...].astype(src.dtype)
            start()

        return cast_and_start, relay, finish


class _Exchange:
    def __init__(self, qs):
        n = len(qs)
        self.ins, self.in_specs = list(qs), [_HBM] * n
        self.out_shape = [_sds(q.shape, q.dtype) for q in qs]
        self.out_specs = [_HBM] * n
        self.scratch = [pltpu.SemaphoreType.DMA((NCHIP_OTHER, n)), pltpu.SemaphoreType.DMA((NCHIP_OTHER, n))]

    def ops(self, ins, outs, scr):
        send_sems, recv_sems = scr
        n = len(ins)
        x, y, c = _place()
        chips = _other_chips(x, y)

        def copies():
            return [pltpu.make_async_remote_copy(
                src_ref=ins[i].at[j], dst_ref=outs[i].at[j], send_sem=send_sems.at[j, i],
                recv_sem=recv_sems.at[j, i], device_id=(*chips[j], c), device_id_type=MESH)
                for i in range(n) for j in range(NCHIP_OTHER)]

        def start():
            for cp in copies():
                cp.start()

        def finish():
            cps = copies()
            for cp in cps:
                cp.wait_recv()
            for cp in cps:
                cp.wait_send()

        return start, lambda: None, finish


class _ExchangeVia:
    def __init__(self, q):
        _, r, cd = q.shape
        half = (2, r // 2, cd)
        self.ins, self.in_specs = [q], [_HBM]
        self.out_shape, self.out_specs = [_sds((2, r, cd), q.dtype)], [_HBM]
        self.scratch = [pltpu.VMEM(half, q.dtype), pltpu.VMEM(half, q.dtype), pltpu.VMEM(half, q.dtype),
                        pltpu.SemaphoreType.DMA((6,)), pltpu.SemaphoreType.DMA((6,)), pltpu.SemaphoreType.DMA((2,))]

    def ops(self, ins, outs, scr):
        (q,), (land,) = ins, outs
        relayed, own, comb, send_sems, recv_sems, local_sems = scr
        h = q.shape[1] // 2
        x, y, c = _place()
        xn, yn, _ = _other_chips(x, y)
        h0, h1 = pl.ds(0, h), pl.ds(h, h)

        def remote(k, src, dst, chip):
            return pltpu.make_async_remote_copy(src_ref=src, dst_ref=dst, send_sem=send_sems.at[k],
                                                recv_sem=recv_sems.at[k], device_id=(*chip, c), device_id_type=MESH)

        def via():
            return [remote(2, q.at[2, h0], relayed.at[0], xn), remote(3, q.at[2, h1], relayed.at[1], yn)]

        def direct():
            return [remote(0, q.at[0, h0], land.at[0, h0], xn), remote(1, q.at[1, h1], land.at[1, h1], yn)]

        def second():
            return [remote(4, comb.at[0], land.at[1, h0], yn), remote(5, comb.at[1], land.at[0, h1], xn)]

        def mine():
            return [pltpu.make_async_copy(q.at[1, h0], own.at[0], local_sems.at[0]),
                    pltpu.make_async_copy(q.at[0, h1], own.at[1], local_sems.at[1])]

        def start():
            for cp in via() + direct() + mine():
                cp.start()

        def relay():
            arrived, loaded, onward = via(), mine(), second()
            for k in range(2):
                arrived[k].wait_recv()
                loaded[k].wait()
                comb[k] = (own[k].astype(F32) + relayed[k].astype(F32)).astype(comb.dtype)
                onward[k].start()

        def finish():
            landing = direct() + second()
            for cp in landing:
                cp.wait_recv()
            for cp in via() + landing:
                cp.wait_send()

        return start, relay, finish


class _SumGather:
    def __init__(self, accs, lands):
        n = len(accs)
        self.n = n
        self.ins, self.in_specs = list(accs) + list(lands), [_VMEM] * (2 * n)
        self.out_shape = [_sds((NDEV,) + a.shape, a.dtype) for a in accs]
        self.out_specs = [_HBM] * n
        self.scratch = [pltpu.VMEM(a.shape, a.dtype) for a in accs] + _gather_sems(n)

    def ops(self, ins, outs, scr):
        n = self.n
        accs, lands, mine = ins[:n], ins[n:], scr[:n]
        g_start, relay, finish = _gather_ops(mine, outs, *scr[n:])

        def start():
            for i in range(n):
                mine[i][...] = accs[i][...] + lands[i][0] + lands[i][1] + lands[i][2]
            g_start()

        return start, relay, finish


def _call(main, jobs, *, name, grid, ins, in_specs, out_shape, out_specs, scratch, relay_step=0):
    nsteps = grid[0] if grid else 1
    n_in, n_out, n_scr = len(ins), len(out_shape), len(scratch)

    def body(*refs):
        pos = [0]

        def take(k):
            r = refs[pos[0]:pos[0] + k]
            pos[0] += k
            return r

        m_in = take(n_in)
        j_in = [take(len(j.ins)) for j in jobs]
        m_out = take(n_out)
        j_out = [take(len(j.out_shape)) for j in jobs]
        m_scr = take(n_scr)
        j_scr = [take(len(j.scratch)) for j in jobs]
        ops = [j.ops(a, b, s) for j, a, b, s in zip(jobs, j_in, j_out, j_scr)]
        i = pl.program_id(0) if grid else 0
        if not grid:
            for o in ops:
                o[0]()
            main(i, m_in, m_out, m_scr)
            for o in ops:
                o[1]()
            for o in ops:
                o[2]()
            return

        if ops:
            @pl.when(i == 0)
            def _():
                for o in ops:
                    o[0]()

        main(i, m_in, m_out, m_scr)

        if ops:
            @pl.when(i == min(relay_step, nsteps - 1))
            def _():
                for o in ops:
                    o[1]()

            @pl.when(i == nsteps - 1)
            def _():
                for o in ops:
                    o[2]()

    extra = dict(dimension_semantics=("arbitrary",)) if grid else {}
    res = pl.pallas_call(
        body, name=name, grid=grid,
        in_specs=list(in_specs) + [s for j in jobs for s in j.in_specs],
        out_specs=list(out_specs) + [s for j in jobs for s in j.out_specs],
        out_shape=list(out_shape) + [s for j in jobs for s in j.out_shape],
        scratch_shapes=list(scratch) + [s for j in jobs for s in j.scratch],
        compiler_params=_params(**extra),
    )(*ins, *[a for j in jobs for a in j.ins])
    main_out, rest, job_out = res[:n_out], res[n_out:], []
    for j in jobs:
        k = len(j.out_shape)
        job_out.append(rest[:k])
        rest = rest[k:]
    return main_out, job_out


def _comm_only(jobs, name):
    _, job_out = _call(lambda i, a, b, s: None, jobs, name=name, grid=(), ins=[], in_specs=[], out_shape=[],
                       out_specs=[], scratch=[])
    return job_out


class _InChip:
    def __init__(self, ps):
        n = len(ps)
        self.n = n
        blk = [p.shape[1:] for p in ps]
        self.ins, self.in_specs = list(ps), [_HBM] * n
        self.out_shape = [_sds((NCHIP_OTHER,) + b, p.dtype) for b, p in zip(blk, ps)] + [_sds(b, F32) for b in blk]
        self.out_specs = [_VMEM] * (2 * n)
        self.scratch = ([pltpu.VMEM((4,) + b, p.dtype) for b, p in zip(blk, ps)] * 2
                        + [pltpu.SemaphoreType.DMA((4, n))] * 3)

    def ops(self, ins, outs, scr):
        n = self.n
        q_refs, acc_refs = outs[:n], outs[n:]
        mines, lands = scr[:n], scr[n:2 * n]
        send_sems, recv_sems, local_sems = scr[2 * n:]
        x, y, c = _place()
        sibling = (x, y, 1 - c)

        def copies():
            out = []
            for i in range(n):
                for pi in range(4):
                    loc = pltpu.make_async_copy(ins[i].at[2 * pi + c], mines[i].at[pi], local_sems.at[pi, i])
                    cp = pltpu.make_async_remote_copy(
                        src_ref=ins[i].at[2 * pi + (1 - c)], dst_ref=lands[i].at[pi],
                        send_sem=send_sems.at[pi, i], recv_sem=recv_sems.at[pi, i],
                        device_id=sibling, device_id_type=MESH)
                    out.append((loc, cp))
            return out

        def start():
            for loc, cp in copies():
                loc.start()
                cp.start()

        def finish():
            pairs = copies()
            for loc, cp in pairs:
                loc.wait()
                cp.wait_recv()
            for i in range(n):
                _chip_sums(mines[i], lands[i], q_refs[i], acc_refs[i], x, y)
            for _, cp in pairs:
                cp.wait_send()

        return start, lambda: None, finish


def _chip_sums(mine, land, q_ref, acc_ref, x, y):
    for j, (qx, qy) in enumerate(_other_chips(x, y)):
        qi = 2 * qx + qy
        q_ref[j] = (mine[qi].astype(F32) + land[qi].astype(F32)).astype(q_ref.dtype)
    mi = 2 * x + y
    acc_ref[...] = mine[mi].astype(F32) + land[mi].astype(F32)


def _direct_sum(v, buf, send_sems, recv_sems):
    x, y, c = _place()
    me = 4 * x + 2 * y + c
    buf[me] = v
    cps = []
    for k in range(1, NDEV):
        fx, fy, fc = (k >> 2) & 1, (k >> 1) & 1, k & 1
        peer = ((1 - x) if fx else x, (1 - y) if fy else y, (1 - c) if fc else c)
        cps.append((peer, pltpu.make_async_remote_copy(
            src_ref=buf.at[me], dst_ref=buf.at[me], send_sem=send_sems.at[k - 1], recv_sem=recv_sems.at[k - 1],
            device_id=peer, device_id_type=MESH)))
    for _, cp in cps:
        cp.start()
    for k, (peer, _) in enumerate(cps):
        theirs = 4 * peer[0] + 2 * peer[1] + peer[2]
        pltpu.make_async_remote_copy(
            src_ref=buf.at[theirs], dst_ref=buf.at[theirs], send_sem=send_sems.at[k], recv_sem=recv_sems.at[k],
            device_id=peer, device_id_type=MESH).wait_recv()
    acc = buf[0]
    for j in range(1, NDEV):
        acc = acc + buf[j]
    for _, cp in cps:
        cp.wait_send()
    return acc


def _direct_sum_scratch(shape, dtype):
    return [pltpu.VMEM((NDEV,) + tuple(shape), dtype), pltpu.SemaphoreType.DMA((NDEV - 1,)),
            pltpu.SemaphoreType.DMA((NDEV - 1,))]


def _fwd_a(x, nw, win8, lnw, lnb, ws, bst, jobs, *, tm, relay_step):
    s_len = x.shape[0]
    nt = s_len // tm
    nch = tm // CH

    def main(i, ins, outs, scr):
        x_ref, nw_ref, win_ref, lnw_ref, lnb_ref, ws_ref, bst_ref = ins
        z_ref, h_ref, y_ref = outs
        wc_scr, gv_scr = scr

        @pl.when(i == 0)
        def _():
            m = _causal_mask()
            for g in range(G):
                wc_scr[g] = jnp.where(m, ws_ref[g], 0.0).astype(BF16)

        x = x_ref[...]
        h = (x * _rms(x) * nw_ref[...]).astype(BF16)
        h_ref[...] = h
        for k in range(NDEV):
            z_ref[:, k * CA:(k + 1) * CA] = _dot(h, win_ref[k])

        ssum = jnp.zeros((tm, 1), F32)
        for g in range(G):
            gv = _gelu_t(z_ref[:, AW + g * GD:AW + (g + 1) * GD])[0]
            gv_scr[:, g * GD:(g + 1) * GD] = gv
            ssum = ssum + jnp.sum(gv, axis=-1, keepdims=True)
        mu = ssum * (1.0 / AW)
        vsum = jnp.zeros((tm, 1), F32)
        for g in range(G):
            dlt = gv_scr[:, g * GD:(g + 1) * GD] - mu
            vsum = vsum + jnp.sum(dlt * dlt, axis=-1, keepdims=True)
        rstd = lax.rsqrt(vsum * (1.0 / AW) + LN_EPS)

        for g in range(G):
            cs = slice(g * GD, (g + 1) * GD)
            v = (gv_scr[:, cs] - mu) * rstd * lnw_ref[:, cs] + lnb_ref[:, cs]
            vb = v.astype(BF16)
            u = _gelu_t(z_ref[:, cs])[0]
            zg = z_ref[:, 2 * AW + g * GD:2 * AW + (g + 1) * GD]
            sg = zg * _sigmoid(zg)
            for n in range(nch):
                rs = slice(n * CH, (n + 1) * CH)
                s = _dot(wc_scr[g], vb[rs, :]) + bst_ref[:, g:g + 1]
                y_ref[rs, cs] = (u[rs, :] * s * sg[rs, :]).astype(BF16)

    tile = lambda w: pl.BlockSpec((tm, w), lambda i: (i, 0))
    return _call(
        main, jobs, name="fwd_a", grid=(nt,), relay_step=relay_step,
        ins=[x, nw, win8, lnw, lnb, ws, bst], in_specs=[tile(D), _VMEM, _VMEM, _VMEM, _VMEM, _VMEM, _VMEM],
        out_shape=[_sds((s_len, 3 * AW), F32), _sds((s_len, D), BF16), _sds((s_len, AW), BF16)],
        out_specs=[tile(3 * AW), tile(D), tile(AW)],
        scratch=[pltpu.VMEM((G, CH, CH), BF16), pltpu.VMEM((tm, AW), F32)])


def _bwd_a(dx1, z, lnw, lnb, ws, bst, wout, jobs, *, tm, relay_step):
    s_len = dx1.shape[0]
    nt = s_len // tm
    nch = tm // CH

    def main(i, ins, outs, scr):
        dx1_ref, z_ref, lnw_ref, lnb_ref, ws_ref, bst_ref, wout_ref = ins
        dz_ref, glnw_ref, glnb_ref, gws_ref, gbst_ref = outs
        wc_scr, wct_scr, vh_scr, dgv_scr, dy_scr, dv_scr, gbs_acc, gwc_acc = scr

        @pl.when(i == 0)
        def _():
            m = _causal_mask()
            for g in range(G):
                wm = jnp.where(m, ws_ref[g], 0.0)
                wc_scr[g] = wm.astype(BF16)
                wct_scr[g] = wm.T.astype(BF16)
            glnw_ref[...] = jnp.zeros_like(glnw_ref)
            glnb_ref[...] = jnp.zeros_like(glnb_ref)
            gbs_acc[...] = jnp.zeros_like(gbs_acc)
            gwc_acc[...] = jnp.zeros_like(gwc_acc)

        dy_scr[...] = _dot_nt(dx1_ref[...], wout_ref[...])

        ssum = jnp.zeros((tm, 1), F32)
        for g in range(G):
            cs = slice(g * GD, (g + 1) * GD)
            zv = z_ref[:, AW + g * GD:AW + (g + 1) * GD]
            gv, t = _gelu_t(zv)
            vh_scr[:, cs] = gv
            dgv_scr[:, cs] = _dgelu(zv, t)
            ssum = ssum + jnp.sum(gv, axis=-1, keepdims=True)
        mu = ssum * (1.0 / AW)
        vsum = jnp.zeros((tm, 1), F32)
        for g in range(G):
            dlt = vh_scr[:, g * GD:(g + 1) * GD] - mu
            vsum = vsum + jnp.sum(dlt * dlt, axis=-1, keepdims=True)
        rstd = lax.rsqrt(vsum * (1.0 / AW) + LN_EPS)

        m1 = jnp.zeros((tm, 1), F32)
        m2 = jnp.zeros((tm, 1), F32)
        for g in range(G):
            cs = slice(g * GD, (g + 1) * GD)
            gs = slice(2 * AW + g * GD, 2 * AW + (g + 1) * GD)
            vhat = (vh_scr[:, cs] - mu) * rstd
            vh_scr[:, cs] = vhat
            vb = (vhat * lnw_ref[:, cs] + lnb_ref[:, cs]).astype(BF16)
            zu = z_ref[:, cs]
            u, tu = _gelu_t(zu)
            zg = z_ref[:, gs]
            sig = _sigmoid(zg)
            sg = zg * sig
            dy = dy_scr[:, cs]
            dsf = dy * u * sg
            dsb = dsf.astype(BF16)
            dvs = []
            for n in range(nch):
                rs = slice(n * CH, (n + 1) * CH)
                s = _dot(wc_scr[g], vb[rs, :]) + bst_ref[:, g:g + 1]
                dys = dy[rs, :] * s
                dz_ref[rs, cs] = (dys * sg[rs, :] * _dgelu(zu[rs, :], tu[rs, :])).astype(BF16)
                dz_ref[rs, gs] = (dys * u[rs, :] * (sig[rs, :] * (1.0 + zg[rs, :] * (1.0 - sig[rs, :])))).astype(BF16)
                gbs_acc[g] += dsf[rs, :]
                gwc_acc[g] += _dot_nt(dsb[rs, :], vb[rs, :])
                dvs.append(_dot(wct_scr[g], dsb[rs, :]))
            dv = jnp.concatenate(dvs, axis=0) if nch > 1 else dvs[0]
            glnw_ref[:, cs] += _rowsum(dv * vhat)
            glnb_ref[:, cs] += _rowsum(dv)
            dvh = dv * lnw_ref[:, cs]
            dv_scr[:, cs] = dvh
            m1 = m1 + jnp.sum(dvh, axis=-1, keepdims=True)
            m2 = m2 + jnp.sum(dvh * vhat, axis=-1, keepdims=True)
        m1 = m1 * (1.0 / AW)
        m2 = m2 * (1.0 / AW)
        for g in range(G):
            cs = slice(g * GD, (g + 1) * GD)
            dgv = rstd * (dv_scr[:, cs] - m1 - vh_scr[:, cs] * m2)
            dz_ref[:, AW + g * GD:AW + (g + 1) * GD] = (dgv * dgv_scr[:, cs]).astype(BF16)

        @pl.when(i == nt - 1)
        def _():
            m = _causal_mask()
            for g in range(G):
                gws_ref[g] = jnp.where(m, gwc_acc[g], 0.0)
                gbst_ref[:, g:g + 1] = jnp.sum(gbs_acc[g], axis=-1, keepdims=True)

    tile = lambda w: pl.BlockSpec((tm, w), lambda i: (i, 0))
    whole = lambda *s: pl.BlockSpec(s, lambda i: (0,) * len(s))
    big = lambda dt: pltpu.VMEM((tm, AW), dt)
    return _call(
        main, jobs, name="bwd_a", grid=(nt,), relay_step=relay_step,
        ins=[dx1, z, lnw, lnb, ws, bst, wout], in_specs=[tile(D), tile(3 * AW), _VMEM, _VMEM, _VMEM, _VMEM, _VMEM],
        out_shape=[_sds((s_len, 3 * AW), BF16), _sds((1, AW), F32), _sds((1, AW), F32), _sds((G, CH, CH), F32),
                   _sds((CH, G), F32)],
        out_specs=[tile(3 * AW), whole(1, AW), whole(1, AW), whole(G, CH, CH), whole(CH, G)],
        scratch=[pltpu.VMEM((G, CH, CH), BF16), pltpu.VMEM((G, CH, CH), BF16), big(F32), big(F32), big(F32), big(F32),
                 pltpu.VMEM((G, CH, GD), F32), pltpu.VMEM((G, CH, CH), F32)])


def _bwd_a_in(dz, dx1, x, nw, win8, jobs, *, tm, relay_step):
    s_len = x.shape[0]
    nt = s_len // tm

    def main(i, ins, outs, scr):
        dz_ref, dx1_ref, x_ref, nw_ref, win_ref = ins
        gx_ref, gnw_ref = outs

        @pl.when(i == 0)
        def _():
            gnw_ref[...] = jnp.zeros_like(gnw_ref)

        dh = jnp.zeros((tm, D), F32)
        for k in range(NDEV):
            dh = dh + _dot_nt(dz_ref[:, k * CA:(k + 1) * CA], win_ref[k])
        x = x_ref[...]
        r = _rms(x)
        gx_ref[...] = dx1_ref[...] + _rms_bwd(dh, x, r, nw_ref[...])
        gnw_ref[...] += _rowsum(dh * x * r)

        @pl.when(i == nt - 1)
        def _():
            gnw_ref[...] = _direct_sum(gnw_ref[...], *scr)

    tile = lambda w: pl.BlockSpec((tm, w), lambda i: (i, 0))
    return _call(
        main, jobs, name="bwd_a_in", grid=(nt,), relay_step=relay_step,
        ins=[dz, dx1, x, nw, win8], in_specs=[tile(3 * AW), tile(D), tile(D), _VMEM, _VMEM],
        out_shape=[_sds((s_len, D), F32), _sds((1, D), F32)],
        out_specs=[tile(D), pl.BlockSpec((1, D), lambda i: (0, 0))], scratch=_direct_sum_scratch((1, D), F32))


def _conv(p8_ref, cs, xb, xm1, xm2, xm3):
    xc = p8_ref[4:5, cs] + p8_ref[3:4, cs] * xb
    xc = xc + p8_ref[0:1, cs] * xm3
    xc = xc + p8_ref[1:2, cs] * xm2
    return xc + p8_ref[2:3, cs] * xm1


def _gates(p8_ref, gcat_ref, hh, xc):
    cs = slice(hh * HD, (hh + 1) * HD)
    pre = _dot(xc.astype(BF16), gcat_ref[hh])
    r = _sigmoid(pre[:, :HD] + p8_ref[5:6, cs])
    ig = _sigmoid(pre[:, HD:] + p8_ref[6:7, cs])
    sp = _softplus_neg(p8_ref[7:8, cs])
    la = (-RG_C) * r * sp
    a = jnp.exp(la)
    half_log = 0.5 * jnp.log(jnp.tanh(-la) * (1.0 + a * a))
    return r, ig, sp, a, jnp.exp(half_log), jnp.exp(-half_log)


def _scan_rows(a_ref, b_ref, out_ref, carry, tm, reverse):
    row = lax.broadcasted_iota(jnp.int32, (SUBLANES, BW), 0)
    ngrp = tm // SUBLANES

    def step(j, cr):
        jj = (ngrp - 1 - j) if reverse else j
        off = pl.multiple_of(jj * SUBLANES, SUBLANES)
        a = a_ref[pl.ds(off, SUBLANES), :]
        b = b_ref[pl.ds(off, SUBLANES), :]
        for sh in (1, 2, 4):
            if reverse:
                a_s = pltpu.roll(a, SUBLANES - sh, 0)
                b_s = pltpu.roll(b, SUBLANES - sh, 0)
                m = row < SUBLANES - sh
            else:
                a_s = pltpu.roll(a, sh, 0)
                b_s = pltpu.roll(b, sh, 0)
                m = row >= sh
            b = jnp.where(m, a * b_s + b, b)
            a = jnp.where(m, a * a_s, a)
        o = b + a * cr
        out_ref[pl.ds(off, SUBLANES), :] = o
        return o[0:1, :] if reverse else o[SUBLANES - 1:SUBLANES, :]

    return lax.fori_loop(0, ngrp, step, carry)


def _fwd_b(x, ya, wout_a, nw, win8, p8, gcat, jobs, *, tm, relay_step):
    s_len = x.shape[0]
    nt = s_len // tm

    def main(i, ins, outs, scr):
        x_ref, ya_ref, wouta_ref, nw_ref, win_ref, p8_ref, gcat_ref = ins
        x1_ref, zb_ref, hs_ref, h1_ref, yb_ref, xc_ref, a_ref, cc_ref, r_ref, ig_ref, m_ref = outs
        xbe_scr, b_scr, k_scr, carry_scr = scr

        @pl.when(i == 0)
        def _():
            xbe_scr[0:SUBLANES, :] = jnp.zeros((SUBLANES, BW), F32)
            carry_scr[...] = jnp.zeros_like(carry_scr)

        x1 = x_ref[...] + _dot(ya_ref[...], wouta_ref[...])
        x1_ref[...] = x1
        h = (x1 * _rms(x1) * nw_ref[...]).astype(BF16)
        h1_ref[...] = h
        for k in range(NDEV):
            zb_ref[:, k * CB:(k + 1) * CB] = _dot(h, win_ref[k])
        xbe_scr[SUBLANES:SUBLANES + tm, :] = zb_ref[:, :BW]
        for hh in range(BH):
            cs = slice(hh * HD, (hh + 1) * HD)
            xc = _conv(p8_ref, cs, xbe_scr[SUBLANES:SUBLANES + tm, cs], xbe_scr[7:7 + tm, cs],
                       xbe_scr[6:6 + tm, cs], xbe_scr[5:5 + tm, cs])
            r, ig, _, a, mult, rm = _gates(p8_ref, gcat_ref, hh, xc)
            ixc = ig * xc
            xc_ref[:, cs] = xc
            a_ref[:, cs] = a
            r_ref[:, cs] = r.astype(BF16)
            ig_ref[:, cs] = ig.astype(BF16)
            m_ref[:, cs] = mult.astype(BF16)
            b_scr[:, cs] = mult * ixc
            k_scr[:, cs] = ixc * (a * a * rm)
        xbe_scr[0:SUBLANES, :] = xbe_scr[tm:tm + SUBLANES, :]
        carry_scr[...] = _scan_rows(a_ref, b_scr, hs_ref, carry_scr[...], tm, False)
        for hh in range(BH):
            cs = slice(hh * HD, (hh + 1) * HD)
            gt = zb_ref[:, BW + hh * HD:BW + (hh + 1) * HD]
            hsv = hs_ref[:, cs]
            yb_ref[:, cs] = (hsv * (gt * _sigmoid(gt))).astype(BF16)
            cc_ref[:, cs] = (hsv - b_scr[:, cs]) - k_scr[:, cs]

    tile = lambda w: pl.BlockSpec((tm, w), lambda i: (i, 0))
    wide = lambda dt: _sds((s_len, BW), dt)
    return _call(
        main, jobs, name="fwd_b", grid=(nt,), relay_step=relay_step,
        ins=[x, ya, wout_a, nw, win8, p8, gcat], in_specs=[tile(D), tile(AW), _VMEM, _VMEM, _VMEM, _VMEM, _VMEM],
        out_shape=[_sds((s_len, D), F32), _sds((s_len, 2 * BW), F32), wide(F32), _sds((s_len, D), BF16), wide(BF16),
                   wide(F32), wide(F32), wide(F32), wide(BF16), wide(BF16), wide(BF16)],
        out_specs=[tile(D), tile(2 * BW), tile(BW), tile(D)] + [tile(BW)] * 7,
        scratch=[pltpu.VMEM((tm + SUBLANES, BW), F32), pltpu.VMEM((tm, BW), F32), pltpu.VMEM((tm, BW), F32),
                 pltpu.VMEM((1, BW), F32)])


def _head(x1, yb, wout, nfw, tgt, *, tm):
    s_len = x1.shape[0]

    def main(i, ins, outs, scr):
        x1_ref, yb_ref, wout_ref, nfw_ref, t_ref = ins
        dx2_ref, dx2b_ref, loss_ref, gnfw_ref = outs

        @pl.when(i == 0)
        def _():
            loss_ref[...] = jnp.zeros_like(loss_ref)
            gnfw_ref[...] = jnp.zeros_like(gnfw_ref)

        x2 = x1_ref[...] + _dot(yb_ref[...], wout_ref[...])
        rf = _rms(x2)
        xn = x2 * rf
        e = xn * nfw_ref[...] - t_ref[...]
        loss_ref[...] += (0.5 / D) * jnp.sum(jnp.sum(e * e, axis=-1, keepdims=True), axis=0, keepdims=True)
        dyf = e * (1.0 / D)
        gnfw_ref[...] += _rowsum(dyf * xn)
        dx2 = _rms_bwd(dyf, x2, rf, nfw_ref[...])
        dx2_ref[...] = dx2
        dx2b_ref[...] = dx2.astype(BF16)

    tile = lambda w: pl.BlockSpec((tm, w), lambda i: (i, 0))
    whole = lambda *s: pl.BlockSpec(s, lambda i: (0,) * len(s))
    (dx2, dx2b, loss, gnfw), _ = _call(
        main, [], name="head", grid=(s_len // tm,),
        ins=[x1, yb, wout, nfw, tgt], in_specs=[tile(D), tile(BW), _VMEM, _VMEM, tile(D)],
        out_shape=[_sds((s_len, D), F32), _sds((s_len, D), BF16), _sds((1, 1), F32), _sds((1, D), F32)],
        out_specs=[tile(D), tile(D), whole(1, 1), whole(1, D)], scratch=[])
    return dx2, dx2b, loss, gnfw


def _bwd_b(dx2, zb, hs, x1, saved, nw, win8, p8, gcat, wout, *, tm):
    s_len = x1.shape[0]
    nt = s_len // tm

    def main(i, ins, outs, scr):
        (dx2_ref, zb_ref, hs_ref, x1_ref, xc_ref, a_ref, cc_ref, r_ref, ig_ref, m_ref,
         nw_ref, win_ref, p8_ref, gcat_ref, wout_ref) = ins
        dx1_ref, dx1b_ref, dzb_ref, gp8_ref, gga_ref, ggx_ref, gnw_ref = outs
        ae_scr, an_scr, dhd_scr, dh_scr, dy_scr, dxce_scr, carry_scr, afirst_scr = scr

        @pl.when(i == 0)
        def _():
            gp8_ref[...] = jnp.zeros_like(gp8_ref)
            gga_ref[...] = jnp.zeros_like(gga_ref)
            ggx_ref[...] = jnp.zeros_like(ggx_ref)
            gnw_ref[...] = jnp.zeros_like(gnw_ref)
            dxce_scr[tm:tm + SUBLANES, :] = jnp.zeros((SUBLANES, BW), F32)
            carry_scr[...] = jnp.zeros_like(carry_scr)
            afirst_scr[...] = jnp.zeros_like(afirst_scr)

        dx2 = dx2_ref[...]
        dy_scr[...] = _dot_nt(dx2.astype(BF16), wout_ref[...])
        for hh in range(BH):
            cs = slice(hh * HD, (hh + 1) * HD)
            gs = slice(BW + hh * HD, BW + (hh + 1) * HD)
            gt = zb_ref[:, gs]
            sig = _sigmoid(gt)
            dy = dy_scr[:, cs]
            dhd_scr[:, cs] = dy * (gt * sig)
            dzb_ref[:, gs] = (dy * hs_ref[:, cs] * (sig * (1.0 + gt * (1.0 - sig)))).astype(BF16)

        ae_scr[0:tm, :] = a_ref[...]
        ae_scr[tm:tm + SUBLANES, :] = jnp.broadcast_to(afirst_scr[...], (SUBLANES, BW))
        an_scr[...] = ae_scr[1:1 + tm, :]
        afirst_scr[...] = ae_scr[0:1, :]
        carry_scr[...] = _scan_rows(an_scr, dhd_scr, dh_scr, carry_scr[...], tm, True)

        for hh in range(BH):
            cs = slice(hh * HD, (hh + 1) * HD)
            dh = dh_scr[:, cs]
            mult = m_ref[:, cs].astype(F32)
            ig = ig_ref[:, cs].astype(F32)
            r = r_ref[:, cs].astype(F32)
            xc = xc_ref[:, cs]
            lam = p8_ref[7:8, cs]
            sp = _softplus_neg(lam)
            dla = dh * cc_ref[:, cs]
            gp8_ref[7:8, cs] += _rowsum(dla * ((-RG_C) * r)) * (-_sigmoid(-lam))
            dpr = dla * ((-RG_C) * sp) * (r * (1.0 - r))
            dpi = dh * mult * xc * (ig * (1.0 - ig))
            gp8_ref[5:6, cs] += _rowsum(dpr)
            gp8_ref[6:7, cs] += _rowsum(dpi)
            dcat = jnp.concatenate([dpr, dpi], axis=1).astype(BF16)
            dxc = dh * mult * ig + _dot_nt(dcat, gcat_ref[hh])
            gg = _dot(xc.T.astype(BF16), dcat)
            gga_ref[hh] += gg[:, :HD]
            ggx_ref[hh] += gg[:, HD:]
            dxce_scr[0:tm, cs] = dxc
            gp8_ref[4:5, cs] += _rowsum(dxc)
        for hh in range(BH):
            cs = slice(hh * HD, (hh + 1) * HD)
            xb = zb_ref[:, cs]
            d0, d1 = dxce_scr[0:tm, cs], dxce_scr[1:1 + tm, cs]
            d2, d3 = dxce_scr[2:2 + tm, cs], dxce_scr[3:3 + tm, cs]
            dzb_ref[:, cs] = (p8_ref[3:4, cs] * d0 + p8_ref[2:3, cs] * d1 + p8_ref[1:2, cs] * d2
                              + p8_ref[0:1, cs] * d3).astype(BF16)
            gp8_ref[3:4, cs] += _rowsum(d0 * xb)
            gp8_ref[2:3, cs] += _rowsum(d1 * xb)
            gp8_ref[1:2, cs] += _rowsum(d2 * xb)
            gp8_ref[0:1, cs] += _rowsum(d3 * xb)
        dxce_scr[tm:tm + SUBLANES, :] = dxce_scr[0:SUBLANES, :]

        dh1 = jnp.zeros((tm, D), F32)
        for k in range(NDEV):
            dh1 = dh1 + _dot_nt(dzb_ref[:, k * CB:(k + 1) * CB], win_ref[k])
        x1 = x1_ref[...]
        r1 = _rms(x1)
        dx1 = dx2 + _rms_bwd(dh1, x1, r1, nw_ref[...])
        dx1_ref[...] = dx1
        dx1b_ref[...] = dx1.astype(BF16)
        gnw_ref[...] += _rowsum(dh1 * x1 * r1)

    tile = lambda w: pl.BlockSpec((tm, w), lambda i: (nt - 1 - i, 0))
    whole = lambda *s: pl.BlockSpec(s, lambda i: (0,) * len(s))
    full = lambda: pltpu.VMEM((tm, BW), F32)
    ext = lambda: pltpu.VMEM((tm + SUBLANES, BW), F32)
    out, _ = _call(
        main, [], name="bwd_b", grid=(nt,),
        ins=[dx2, zb, hs, x1, *saved, nw, win8, p8, gcat, wout],
        in_specs=[tile(D), tile(2 * BW), tile(BW), tile(D)] + [tile(BW)] * 6 + [_VMEM] * 5,
        out_shape=[_sds((s_len, D), F32), _sds((s_len, D), BF16), _sds((s_len, 2 * BW), BF16), _sds((SUBLANES, BW), F32),
                   _sds((BH, HD, HD), F32), _sds((BH, HD, HD), F32), _sds((1, D), F32)],
        out_specs=[tile(D), tile(D), tile(2 * BW), whole(SUBLANES, BW), whole(BH, HD, HD), whole(BH, HD, HD),
                   whole(1, D)],
        scratch=[ext(), full(), full(), full(), full(), ext(), pltpu.VMEM((1, BW), F32), pltpu.VMEM((1, BW), F32)])
    return out


def _transpose_into(dst_ref, src_ref, rows):
    s_len = src_ref.shape[0]
    for r0 in range(0, s_len, rows):
        dst_ref[:, r0:r0 + rows] = src_ref[r0:r0 + rows, :].astype(F32).T.astype(BF16)


def _wgrad(a, b, jobs, *, by_rows, per, name, relay_step=0):
    s_len, m = a.shape
    n = b.shape[1]
    r, cd = (m // NDEV, n) if by_rows else (m, n // NDEV)
    nsteps = NDEV // per
    at_rows = per * r if by_rows else m

    def main(i, ins, outs, scr):
        a_ref, b_ref = ins
        q_ref, acc_ref = outs
        at_scr, stage, mine, land, send_sems, recv_sems = scr
        x, y, c = _place()

        def to_sibling(pi):
            return pltpu.make_async_remote_copy(
                src_ref=stage.at[pi & 1], dst_ref=land.at[pi], send_sem=send_sems.at[pi], recv_sem=recv_sems.at[pi],
                device_id=(x, y, 1 - c), device_id_type=MESH)

        if by_rows:
            _transpose_into(at_scr, a_ref, TRANSPOSE_ROWS)
        else:
            @pl.when(i == 0)
            def _():
                _transpose_into(at_scr, a_ref, TRANSPOSE_ROWS)

        res = _dot(at_scr[...], b_ref[...]).astype(BF16)
        for k in range(per):
            blk = per * i + k
            pi, pc = blk >> 1, blk & 1
            val = res[k * r:(k + 1) * r, :] if by_rows else res

            @pl.when(pc != c)
            def _():
                @pl.when(pi >= 2)
                def _():
                    to_sibling(pi - 2).wait_send()

                stage[pi & 1] = val
                to_sibling(pi).start()

            @pl.when(pc == c)
            def _():
                mine[pi] = val

        @pl.when(i == nsteps - 1)
        def _():
            for p in range(4):
                to_sibling(p).wait_recv()
            to_sibling(2).wait_send()
            to_sibling(3).wait_send()
            _chip_sums(mine, land, q_ref, acc_ref, x, y)

    if by_rows:
        in_specs = [pl.BlockSpec((s_len, at_rows), lambda j: (0, j)), _VMEM]
    else:
        in_specs = [_VMEM, pl.BlockSpec((s_len, cd), lambda j: (0, j))]
    blk_vmem = lambda k: pltpu.VMEM((k, r, cd), BF16)
    (q, acc), job_out = _call(
        main, jobs, name=name, grid=(nsteps,), relay_step=relay_step, ins=[a, b], in_specs=in_specs,
        out_shape=[_sds((NCHIP_OTHER, r, cd), BF16), _sds((r, cd), F32)],
        out_specs=[pl.BlockSpec((NCHIP_OTHER, r, cd), lambda j: (0, 0, 0)), pl.BlockSpec((r, cd), lambda j: (0, 0))],
        scratch=[pltpu.VMEM((at_rows, s_len), BF16), blk_vmem(2), blk_vmem(4), blk_vmem(4),
                 pltpu.SemaphoreType.DMA((4,)), pltpu.SemaphoreType.DMA((4,))])
    return q, acc, job_out


def _wgrad_cols_early(a, b, jobs, *, name, relay_step=0):
    s_len, m = a.shape
    r, cd = m, b.shape[1] // NDEV
    h = r // 2

    def chip_at(pos, base):
        return base ^ (3 - pos)

    def main(i, ins, outs, scr):
        a_ref, b_ref = ins
        q_ref, acc_ref, rel_ref = outs
        at_scr, stage, mine, land, q2_scr, send_sems, recv_sems, via_send, via_recv = scr
        x, y, c = _place()
        base = 2 * x + y
        xn, yn, _ = _other_chips(x, y)
        pos, pc = i >> 1, i & 1
        pi = chip_at(pos, base)

        def to_sibling(chip, slot):
            return pltpu.make_async_remote_copy(
                src_ref=stage.at[slot], dst_ref=land.at[chip], send_sem=send_sems.at[chip],
                recv_sem=recv_sems.at[chip], device_id=(x, y, 1 - c), device_id_type=MESH)

        def via(k):
            return pltpu.make_async_remote_copy(
                src_ref=q2_scr.at[pl.ds(k * h, h)], dst_ref=rel_ref.at[k], send_sem=via_send.at[k],
                recv_sem=via_recv.at[k], device_id=(*(xn, yn)[k], c), device_id_type=MESH)

        @pl.when(i == 0)
        def _():
            _transpose_into(at_scr, a_ref, TRANSPOSE_ROWS)

        res = _dot(at_scr[...], b_ref[...]).astype(BF16)

        @pl.when(pc != c)
        def _():
            @pl.when(pos >= 2)
            def _():
                to_sibling(chip_at(pos - 2, base), pos & 1).wait_send()

            stage[pos & 1] = res
            to_sibling(pi, pos & 1).start()

        @pl.when(pc == c)
        def _():
            mine[pi] = res

        @pl.when(i == 1)
        def _():
            dg = chip_at(0, base)
            to_sibling(dg, 0).wait_recv()
            q2 = (mine[dg].astype(F32) + land[dg].astype(F32)).astype(BF16)
            q2_scr[...] = q2
            q_ref[2] = q2
            via(0).start()
            via(1).start()

        @pl.when(i == NDEV - 1)
        def _():
            for pos_ in (1, 2, 3):
                to_sibling(chip_at(pos_, base), 0).wait_recv()
            to_sibling(chip_at(2, base), 0).wait_send()
            to_sibling(chip_at(3, base), 1).wait_send()
            for k in range(2):
                via(k).wait_recv()
            for k in range(2):
                via(k).wait_send()
            for j, chip in enumerate((base ^ 2, base ^ 1)):
                q_ref[j] = (mine[chip].astype(F32) + land[chip].astype(F32)).astype(BF16)
            acc_ref[...] = mine[base].astype(F32) + land[base].astype(F32)

    def b_block(j):
        base = 2 * lax.axis_index("x") + lax.axis_index("y")
        return (0, 2 * chip_at(j >> 1, base) + (j & 1))

    blk_vmem = lambda k: pltpu.VMEM((k, r, cd), BF16)
    (q, acc, rel), job_out = _call(
        main, jobs, name=name, grid=(NDEV,), relay_step=relay_step, ins=[a, b],
        in_specs=[_VMEM, pl.BlockSpec((s_len, cd), b_block)],
        out_shape=[_sds((NCHIP_OTHER, r, cd), BF16), _sds((r, cd), F32), _sds((2, h, cd), BF16)],
        out_specs=[pl.BlockSpec((NCHIP_OTHER, r, cd), lambda j: (0, 0, 0)), pl.BlockSpec((r, cd), lambda j: (0, 0)), _HBM],
        scratch=[pltpu.VMEM((m, s_len), BF16), blk_vmem(2), blk_vmem(4), blk_vmem(4), pltpu.VMEM((r, cd), BF16),
                 pltpu.SemaphoreType.DMA((4,)), pltpu.SemaphoreType.DMA((4,)), pltpu.SemaphoreType.DMA((2,)),
                 pltpu.SemaphoreType.DMA((2,))])
    return q, acc, rel, job_out


class _ExchangeRest:
    def __init__(self, q, relayed):
        _, r, cd = q.shape
        half = (2, r // 2, cd)
        self.ins, self.in_specs = [q, relayed], [_HBM, _HBM]
        self.out_shape, self.out_specs = [_sds((2, r, cd), q.dtype)], [_HBM]
        self.scratch = [pltpu.VMEM(half, q.dtype), pltpu.VMEM(half, q.dtype), pltpu.VMEM(half, q.dtype),
                        pltpu.SemaphoreType.DMA((4,)), pltpu.SemaphoreType.DMA((4,)), pltpu.SemaphoreType.DMA((4,))]

    def ops(self, ins, outs, scr):
        (q, rel_in), (land,) = ins, outs
        own, rel, comb, send_sems, recv_sems, local_sems = scr
        h = q.shape[1] // 2
        x, y, c = _place()
        xn, yn, _ = _other_chips(x, y)
        h0, h1 = pl.ds(0, h), pl.ds(h, h)

        def remote(k, src, dst, chip):
            return pltpu.make_async_remote_copy(src_ref=src, dst_ref=dst, send_sem=send_sems.at[k],
                                                recv_sem=recv_sems.at[k], device_id=(*chip, c), device_id_type=MESH)

        def sends():
            return [remote(0, q.at[0, h0], land.at[0, h0], xn), remote(1, q.at[1, h1], land.at[1, h1], yn),
                    remote(2, comb.at[0], land.at[1, h0], yn), remote(3, comb.at[1], land.at[0, h1], xn)]

        def loads():
            return [pltpu.make_async_copy(q.at[1, h0], own.at[0], local_sems.at[0]),
                    pltpu.make_async_copy(q.at[0, h1], own.at[1], local_sems.at[1]),
                    pltpu.make_async_copy(rel_in.at[0], rel.at[0], local_sems.at[2]),
                    pltpu.make_async_copy(rel_in.at[1], rel.at[1], local_sems.at[3])]

        def start():
            cps, lds = sends(), loads()
            for ld in lds:
                ld.start()
            cps[0].start()
            cps[1].start()
            for ld in lds:
                ld.wait()
            for k in range(2):
                comb[k] = (own[k].astype(F32) + rel[k].astype(F32)).astype(comb.dtype)
            cps[2].start()
            cps[3].start()

        def finish():
            cps = sends()
            for cp in cps:
                cp.wait_recv()
            for cp in cps:
                cp.wait_send()

        return start, lambda: None, finish


def _adam_math(w, g, m, v):
    m = B1 * m + (1.0 - B1) * g
    v = B2 * v + (1.0 - B2) * (g * g)
    m_hat = m / (1.0 - B1 ** STEP)
    v_hat = v / (1.0 - B2 ** STEP)
    delta = (-LR) * (m_hat / (jnp.sqrt(v_hat) + ADAM_EPS) + WD * w)
    return delta, m, v


def _adam_big(w, acc, land, m, v, name):
    r, cd = w.shape
    rb = ADAM_ROWS if r % ADAM_ROWS == 0 else r
    nland = land.shape[0]

    def body(w_ref, acc_ref, land_ref, m_ref, v_ref, g_ref, d_ref, mo_ref, vo_ref):
        g = acc_ref[...]
        for j in range(nland):
            g = g + land_ref[j].astype(F32)
        g_ref[...] = g
        d_ref[...], mo_ref[...], vo_ref[...] = _adam_math(w_ref[...], g, m_ref[...], v_ref[...])

    blk = pl.BlockSpec((rb, cd), lambda i: (i, 0))
    blk3 = pl.BlockSpec((nland, rb, cd), lambda i: (0, i, 0))
    return pl.pallas_call(
        body, name=name, grid=(r // rb,), in_specs=[blk, blk, blk3, blk, blk], out_specs=[blk] * 4,
        out_shape=[_sds((r, cd), F32)] * 4,
        compiler_params=_params(dimension_semantics=("arbitrary",)),
    )(w, acc, land, m, v)


def _adam_small(groups):
    n = len(groups)

    def body(*refs):
        ins, outs = refs[:4 * n], refs[4 * n:]
        for k in range(n):
            w_ref, g_ref, m_ref, v_ref = ins[4 * k:4 * k + 4]
            d, mo, vo = _adam_math(w_ref[...], g_ref[...], m_ref[...], v_ref[...])
            outs[3 * k][...] = d
            outs[3 * k + 1][...] = mo
            outs[3 * k + 2][...] = vo

    flat = [a for grp in groups for a in grp]
    shapes = [_sds(grp[0].shape, F32) for grp in groups for _ in range(3)]
    res = pl.pallas_call(
        body, name="adam_small", in_specs=[_VMEM] * (4 * n), out_specs=[_VMEM] * (3 * n), out_shape=shapes,
        compiler_params=_params(),
    )(*flat)
    return [tuple(res[3 * k:3 * k + 3]) for k in range(n)]


TM_FWD_A = 256
RELAY_STEP_FWD_A = 3
RELAY_STEP_FWD_B = 2
TM_BWD_A = 256
RELAY_STEP_BWD_A = 3
TM_BWD_A_IN = 256
RELAY_STEP_BWD_A_IN = 4
RELAY_STEP_WGRAD_A_IN = 2
TM_FWD_B = 256
TM_HEAD = 1024
TM_BWD_B = 256


def _pack(parts, rows):
    flat = jnp.concatenate([p.reshape(-1) for p in parts])
    return jnp.pad(flat, (0, NDEV * rows * LANES - flat.shape[0])).reshape(NDEV, rows, LANES)


def _unpack(packed, shapes):
    flat, out, off = packed.reshape(-1), [], 0
    for s in shapes:
        size = 1
        for d in s:
            size *= d
        out.append(flat[off:off + size].reshape(s))
        off += size
    return out


def kernel(x, norm_w, a_w_in, a_ln_w, a_ln_b, a_w_s, a_b_s, a_w_out, b_w_in, b_conv_w, b_conv_b, b_gate_a_w, b_gate_a_b, b_gate_x_w, b_gate_x_b, b_lambda, b_w_out, norm_f_w, loss_target, m_norm_w, m_a_w_in, m_a_ln_w, m_a_ln_b, m_a_w_s, m_a_b_s, m_a_w_out, m_b_w_in, m_b_conv_w, m_b_conv_b, m_b_gate_a_w, m_b_gate_a_b, m_b_gate_x_w, m_b_gate_x_b, m_b_lambda, m_b_w_out, m_norm_f_w, v_norm_w, v_a_w_in, v_a_ln_w, v_a_ln_b, v_a_w_s, v_a_b_s, v_a_w_out, v_b_w_in, v_b_conv_w, v_b_conv_b, v_b_gate_a_w, v_b_gate_a_b, v_b_gate_x_w, v_b_gate_x_b, v_b_lambda, v_b_w_out, v_norm_f_w):
    me = 4 * lax.axis_index("x") + 2 * lax.axis_index("y") + lax.axis_index("c")
    xs, tgt = x[0], loss_target[0]
    nw0, nw1, nfw = norm_w[0:1], norm_w[1:2], norm_f_w.reshape(1, D)
    w_s, bst = a_w_s[0], a_b_s[0].T
    gcat = jnp.concatenate([b_gate_a_w[0], b_gate_x_w[0]], axis=-1).astype(BF16)

    p8_shard = jnp.concatenate([b_conv_w[0], b_conv_b, b_gate_a_b, b_gate_x_b, b_lambda], axis=0)
    ((win_a8, p8_all),) = _comm_only([_Gather([a_w_in[0], p8_shard], [BF16, F32])], "gather_first")
    p8 = jnp.transpose(p8_all, (1, 0, 2)).reshape(SUBLANES, BW)

    (z, h0, ya), ((wout_a8, win_b8),) = _fwd_a(
        xs, nw0, win_a8, a_ln_w, a_ln_b, w_s, bst, [_Gather([a_w_out[0], b_w_in[0]], [BF16, BF16])],
        tm=TM_FWD_A, relay_step=RELAY_STEP_FWD_A)
    wout_a = wout_a8.reshape(AW, D)
    (x1, zb, hs, h1, yb, *saved_b), ((wout_b8,),) = _fwd_b(
        xs, ya, wout_a, nw1, win_b8, p8, gcat, [_Gather([b_w_out[0]], [BF16])],
        tm=TM_FWD_B, relay_step=RELAY_STEP_FWD_B)
    wout_b = wout_b8.reshape(BW, D)
    dx2, dx2b, loss, g_nfw = _head(x1, yb, wout_b, nfw, tgt, tm=TM_HEAD)

    dx1, dx1b, dzb, g_p8, g_ga, g_gx, g_nw1 = _bwd_b(dx2, zb, hs, x1, saved_b, nw1, win_b8, p8, gcat, wout_b,
                                                     tm=TM_BWD_B)
    q_wout_b, acc_wout_b, _ = _wgrad(yb, dx2b, [], by_rows=True, per=2, name="wgrad_b_out")
    shapes_b = [(1, D), (1, D), (SUBLANES, BW), (1, 1)]
    pack_b = _pack([g_nfw, g_nw1, g_p8, loss], 16)
    small_b = _InChip([g_ga.reshape(NDEV, -1, HD), g_gx.reshape(NDEV, -1, HD), pack_b])
    q_win_b, acc_win_b, (sm_b, (l_wout_b,)) = _wgrad(h1, dzb, [small_b, _Exchange([q_wout_b])], by_rows=False, per=1,
                                                      name="wgrad_b_in")
    qs_b, accs_b = sm_b[:3], sm_b[3:]

    (dz, g_lnw, g_lnb, g_ws, g_bst), (lands_b, (l_win_b,)) = _bwd_a(
        dx1b, z, a_ln_w, a_ln_b, w_s, bst, wout_a, [_Exchange(qs_b), _ExchangeVia(q_win_b)],
        tm=TM_BWD_A, relay_step=RELAY_STEP_BWD_A)
    shapes_a = [(1, AW), (1, AW), (CH, G)]
    pack_a = _pack([g_lnw, g_lnb, g_bst], 8)
    q_wout_a, acc_wout_a, (red_b, sm_a) = _wgrad(
        ya, dx1b, [_SumGather(accs_b, lands_b), _InChip([g_ws, pack_a])], by_rows=True, per=2,
        name="wgrad_a_out", relay_step=1)
    qs_a, accs_a = sm_a[:2], sm_a[2:]
    q_win_a, acc_win_a, rel_a, (lands_a, (l_wout_a,)) = _wgrad_cols_early(
        h0, dz, [_Exchange(qs_a), _ExchangeVia(q_wout_a)], name="wgrad_a_in", relay_step=RELAY_STEP_WGRAD_A_IN)
    (gx, g_nw0), (red_a, (l_win_a,)) = _bwd_a_in(
        dz, dx1, xs, nw0, win_a8, [_SumGather(accs_a, lands_a), _ExchangeRest(q_win_a, rel_a)],
        tm=TM_BWD_A_IN, relay_step=RELAY_STEP_BWD_A_IN)

    r_ga, r_gx, r_pack_b = red_b
    r_nfw, r_nw1, r_p8, loss = _unpack(r_pack_b, shapes_b)
    r_ws, r_pack_a = red_a
    r_lnw, r_lnb, r_bst = _unpack(r_pack_a, shapes_a)
    g_p8 = lax.dynamic_slice_in_dim(r_p8, me * (BW // NDEV), BW // NDEV, axis=1)
    loss = loss[0, 0]

    weights = dict(norm_w=norm_w, a_w_in=a_w_in, a_ln_w=a_ln_w, a_ln_b=a_ln_b, a_w_s=a_w_s, a_b_s=a_b_s, a_w_out=a_w_out,
                   b_w_in=b_w_in, b_conv_w=b_conv_w, b_conv_b=b_conv_b, b_gate_a_w=b_gate_a_w, b_gate_a_b=b_gate_a_b,
                   b_gate_x_w=b_gate_x_w, b_gate_x_b=b_gate_x_b, b_lambda=b_lambda, b_w_out=b_w_out, norm_f_w=norm_f_w)
    mom1 = dict(norm_w=m_norm_w, a_w_in=m_a_w_in, a_ln_w=m_a_ln_w, a_ln_b=m_a_ln_b, a_w_s=m_a_w_s, a_b_s=m_a_b_s,
                a_w_out=m_a_w_out, b_w_in=m_b_w_in, b_conv_w=m_b_conv_w, b_conv_b=m_b_conv_b, b_gate_a_w=m_b_gate_a_w,
                b_gate_a_b=m_b_gate_a_b, b_gate_x_w=m_b_gate_x_w, b_gate_x_b=m_b_gate_x_b, b_lambda=m_b_lambda,
                b_w_out=m_b_w_out, norm_f_w=m_norm_f_w)
    mom2 = dict(norm_w=v_norm_w, a_w_in=v_a_w_in, a_ln_w=v_a_ln_w, a_ln_b=v_a_ln_b, a_w_s=v_a_w_s, a_b_s=v_a_b_s,
                a_w_out=v_a_w_out, b_w_in=v_b_w_in, b_conv_w=v_b_conv_w, b_conv_b=v_b_conv_b, b_gate_a_w=v_b_gate_a_w,
                b_gate_a_b=v_b_gate_a_b, b_gate_x_w=v_b_gate_x_w, b_gate_x_b=v_b_gate_x_b, b_lambda=v_b_lambda,
                b_w_out=v_b_w_out, norm_f_w=v_norm_f_w)
    names = list(weights)

    def as2d(a):
        return a.reshape(-1, a.shape[-1])

    upd, grads = {}, {}
    for k, acc, land in (("a_w_in", acc_win_a, l_win_a), ("a_w_out", acc_wout_a, l_wout_a),
                         ("b_w_in", acc_win_b, l_win_b), ("b_w_out", acc_wout_b, l_wout_b)):
        g, d, mo, vo = _adam_big(as2d(weights[k]), acc, land, as2d(mom1[k]), as2d(mom2[k]), "adam_" + k)
        grads[k] = g[None]
        upd[k] = (d, mo, vo)
    grads.update(
        norm_w=jnp.concatenate([g_nw0, r_nw1], axis=0), a_ln_w=r_lnw, a_ln_b=r_lnb,
        a_w_s=r_ws.reshape(1, G, CH, CH), a_b_s=r_bst.T[None],
        b_conv_w=g_p8[None, 0:4], b_conv_b=g_p8[4:5], b_gate_a_w=r_ga.reshape(1, BH, HD, HD), b_gate_a_b=g_p8[5:6],
        b_gate_x_w=r_gx.reshape(1, BH, HD, HD), b_gate_x_b=g_p8[6:7], b_lambda=g_p8[7:8], norm_f_w=r_nfw.reshape(D))
    small_names = [k for k in names if k not in upd]
    res = _adam_small([(as2d(weights[k]), as2d(grads[k]), as2d(mom1[k]), as2d(mom2[k])) for k in small_names])
    for k, r3 in zip(small_names, res):
        upd[k] = r3
    deltas = [upd[k][0].reshape(weights[k].shape) for k in names]
    new_m = [upd[k][1].reshape(weights[k].shape) for k in names]
    new_v = [upd[k][2].reshape(weights[k].shape) for k in names]
    return (loss, gx[None], *[grads[k] for k in names], *deltas, *new_m, *new_v)
```

```python
import jax
import jax.numpy as jnp
from jax import lax
from jax.experimental import pallas as pl
from jax.experimental.pallas import tpu as pltpu

F32 = jnp.float32
BF16 = jnp.bfloat16
MESH = pl.DeviceIdType.MESH

NDEV = 8
NCHIP_OTHER = 3
D = 1024
AW = 2048
G = 8
GD = AW // G
CH = 128
BW = 1536
BH = 12
HD = BW // BH
CA = 3 * AW // NDEV
CB = 2 * BW // NDEV
RMS_EPS = 1e-6
LN_EPS = 1e-5
RG_C = 8.0
LR, B1, B2, ADAM_EPS, WD, STEP = 0.001, 0.9, 0.999, 1e-08, 0.01, 10
V7X_VMEM_BYTES = 64 * 1024 * 1024
VMEM_LIMIT = V7X_VMEM_BYTES - 8 * 1024 * 1024
SUBLANES = 8
LANES = 128
BF16_ROWS = 16
TRANSPOSE_ROWS = 256
ADAM_ROWS = 512
GELU_C = 0.7978845608028654
GELU_K = 0.044715

_VMEM = pl.BlockSpec(memory_space=pltpu.VMEM)
_HBM = pl.BlockSpec(memory_space=pltpu.HBM)


def _sds(shape, dtype):
    return jax.ShapeDtypeStruct(tuple(shape), dtype)


def _params(**kw):
    return pltpu.CompilerParams(vmem_limit_bytes=VMEM_LIMIT, **kw)


def _gelu_t(z):
    t = jnp.tanh(GELU_C * (z + GELU_K * (z * z * z)))
    return 0.5 * z * (1.0 + t), t


def _dgelu(z, t):
    return 0.5 * (1.0 + t) + 0.5 * z * (1.0 - t * t) * (GELU_C * (1.0 + 3.0 * GELU_K * z * z))


def _sigmoid(v):
    return 0.5 * jnp.tanh(0.5 * v) + 0.5


def _softplus_neg(lam):
    return jnp.maximum(-lam, 0.0) + jnp.log1p(jnp.exp(-jnp.abs(lam)))


def _dot(a, b):
    return jnp.dot(a, b, preferred_element_type=F32)


def _dot_nt(a, b):
    return lax.dot_general(a, b, (((1,), (1,)), ((), ())), preferred_element_type=F32)


def _rowsum(v):
    return jnp.sum(v, axis=0, keepdims=True)


def _causal_mask():
    r = lax.broadcasted_iota(jnp.int32, (CH, CH), 0)
    c = lax.broadcasted_iota(jnp.int32, (CH, CH), 1)
    return r >= c


def _rms(x):
    return lax.rsqrt(jnp.mean(x * x, axis=-1, keepdims=True) + RMS_EPS)


def _rms_bwd(dh, x, r, nw):
    gy = dh * nw
    return r * gy - x * (r * r * r) * jnp.mean(gy * x, axis=-1, keepdims=True)


def _place():
    return lax.axis_index("x"), lax.axis_index("y"), lax.axis_index("c")


def _other_chips(x, y):
    return [(1 - x, y), (x, 1 - y), (1 - x, 1 - y)]


GATHER_SLOTS = 10


def _gather_ops(ins, outs, send_sems, recv_sems, local_sems):
    n = len(ins)
    x, y, c = _place()
    sibling = (x, y, 1 - c)
    xn, yn, dg = _other_chips(x, y)
    split = [ins[i].shape[0] % (2 * BF16_ROWS) == 0 for i in range(n)]

    def blk(chip, core):
        return 4 * chip[0] + 2 * chip[1] + core

    me = blk((x, y), c)

    def part(ref, i, half):
        if half is None:
            return ref
        h = ins[i].shape[0] // 2
        return ref.at[pl.ds(half * h, h)]

    def copy(i, k, block, to, half=None, src=None):
        dst = part(outs[i].at[block], i, half)
        return pltpu.make_async_remote_copy(
            src_ref=dst if src is None else part(src, i, half), dst_ref=dst,
            send_sem=send_sems.at[k, i], recv_sem=recv_sems.at[k, i], device_id=to, device_id_type=MESH)

    def first_copies():
        mine = [pltpu.make_async_copy(ins[i], outs[i].at[me], local_sems.at[i]) for i in range(n)]
        first = []
        for i in range(n):
            first.append(copy(i, 0, me, sibling, src=ins[i]))
            if split[i]:
                first.append(copy(i, 1, me, (*xn, c), 0, ins[i]))
                first.append(copy(i, 3, me, (*yn, c), 1, ins[i]))
                first.append(copy(i, 2, me, (*xn, c), 1, ins[i]))
                first.append(copy(i, 4, me, (*yn, c), 0, ins[i]))
            else:
                first.append(copy(i, 1, me, (*xn, c), None, ins[i]))
                first.append(copy(i, 3, me, (*yn, c), None, ins[i]))
                first.append(copy(i, 5, me, (*dg, c), None, ins[i]))
        return mine, first

    def onward():
        out = []
        for i in range(n):
            if split[i]:
                out.append(copy(i, 5, blk(xn, c), (*yn, c), 0))
                out.append(copy(i, 6, blk(yn, c), (*xn, c), 1))
        return out

    def start():
        mine, first = first_copies()
        for cp in mine + first:
            cp.start()

    def relay():
        sends = onward()
        for i in range(n):
            if split[i]:
                copy(i, 1, blk(xn, c), sibling, 0).wait_recv()
                sends.pop(0).start()
                copy(i, 3, blk(yn, c), sibling, 1).wait_recv()
                sends.pop(0).start()

    def finish():
        mine, first = first_copies()
        passed = []

        def pass_on(i, j, chip):
            fwd = copy(i, 7 + j, blk(chip, c), sibling)
            fwd.start()
            passed.append(fwd)

        for i in range(n):
            if split[i]:
                copy(i, 2, blk(xn, c), sibling, 1).wait_recv()
                pass_on(i, 0, xn)
                copy(i, 4, blk(yn, c), sibling, 0).wait_recv()
                pass_on(i, 1, yn)
                copy(i, 5, blk(dg, c), sibling, 0).wait_recv()
                copy(i, 6, blk(dg, c), sibling, 1).wait_recv()
                pass_on(i, 2, dg)
            else:
                copy(i, 1, blk(xn, c), sibling).wait_recv()
                pass_on(i, 0, xn)
                copy(i, 3, blk(yn, c), sibling).wait_recv()
                pass_on(i, 1, yn)
                copy(i, 5, blk(dg, c), sibling).wait_recv()
                pass_on(i, 2, dg)
        for i in range(n):
            copy(i, 0, blk((x, y), 1 - c), sibling).wait_recv()
            for j, chip in enumerate((xn, yn, dg)):
                copy(i, 7 + j, blk(chip, 1 - c), sibling).wait_recv()
        for cp in first + passed + onward():
            cp.wait_send()
        for cp in mine:
            cp.wait()

    return start, relay, finish


def _gather_sems(n):
    return [pltpu.SemaphoreType.DMA((GATHER_SLOTS, n)), pltpu.SemaphoreType.DMA((GATHER_SLOTS, n)),
            pltpu.SemaphoreType.DMA((n,))]


class _Gather:
    def __init__(self, shards, as_dtypes=None):
        n = len(shards)
        dts = [s.dtype for s in shards] if as_dtypes is None else list(as_dtypes)
        self.cast = [jnp.dtype(d) != s.dtype for d, s in zip(dts, shards)]
        self.ins = list(shards)
        self.in_specs = [_VMEM if c else _HBM for c in self.cast]
        self.out_shape = [_sds((NDEV,) + s.shape, d) for s, d in zip(shards, dts)]
        self.out_specs = [_HBM] * n
        self.scratch = [pltpu.VMEM(s.shape, d) for s, d, c in zip(shards, dts, self.cast) if c] + _gather_sems(n)

    def ops(self, ins, outs, scr):
        ncast = sum(self.cast)
        staged = iter(scr[:ncast])
        srcs = [next(staged) if c else ref for c, ref in zip(self.cast, ins)]
        start, relay, finish = _gather_ops(srcs, outs, *scr[ncast:])

        def cast_and_start():
            for c, ref, src in zip(self.cast, ins, srcs):
                if c:
                    src[...] = ref[...].astype(src.dtype)
            start()

        return cast_and_start, relay, finish


class _Exchange:
    def __init__(self, qs):
        n = len(qs)
        self.ins, self.in_specs = list(qs), [_HBM] * n
        self.out_shape = [_sds(q.shape, q.dtype) for q in qs]
        self.out_specs = [_HBM] * n
        self.scratch = [pltpu.SemaphoreType.DMA((NCHIP_OTHER, n)), pltpu.SemaphoreType.DMA((NCHIP_OTHER, n))]

    def ops(self, ins, outs, scr):
        send_sems, recv_sems = scr
        n = len(ins)
        x, y, c = _place()
        chips = _other_chips(x, y)

        def copies():
            return [pltpu.make_async_remote_copy(
                src_ref=ins[i].at[j], dst_ref=outs[i].at[j], send_sem=send_sems.at[j, i],
                recv_sem=recv_sems.at[j, i], device_id=(*chips[j], c), device_id_type=MESH)
                for i in range(n) for j in range(NCHIP_OTHER)]

        def start():
            for cp in copies():
                cp.start()

        def finish():
            cps = copies()
            for cp in cps:
                cp.wait_recv()
            for cp in cps:
                cp.wait_send()

        return start, lambda: None, finish


class _ExchangeVia:
    def __init__(self, q):
        _, r, cd = q.shape
        half = (2, r // 2, cd)
        self.ins, self.in_specs = [q], [_HBM]
        self.out_shape, self.out_specs = [_sds((2, r, cd), q.dtype)], [_HBM]
        self.scratch = [pltpu.VMEM(half, q.dtype), pltpu.VMEM(half, q.dtype), pltpu.VMEM(half, q.dtype),
                        pltpu.SemaphoreType.DMA((6,)), pltpu.SemaphoreType.DMA((6,)), pltpu.SemaphoreType.DMA((2,))]

    def ops(self, ins, outs, scr):
        (q,), (land,) = ins, outs
        relayed, own, comb, send_sems, recv_sems, local_sems = scr
        h = q.shape[1] // 2
        x, y, c = _place()
        xn, yn, _ = _other_chips(x, y)
        h0, h1 = pl.ds(0, h), pl.ds(h, h)

        def remote(k, src, dst, chip):
            return pltpu.make_async_remote_copy(src_ref=src, dst_ref=dst, send_sem=send_sems.at[k],
                                                recv_sem=recv_sems.at[k], device_id=(*chip, c), device_id_type=MESH)

        def via():
            return [remote(2, q.at[2, h0], relayed.at[0], xn), remote(3, q.at[2, h1], relayed.at[1], yn)]

        def direct():
            return [remote(0, q.at[0, h0], land.at[0, h0], xn), remote(1, q.at[1, h1], land.at[1, h1], yn)]

        def second():
            return [remote(4, comb.at[0], land.at[1, h0], yn), remote(5, comb.at[1], land.at[0, h1], xn)]

        def mine():
            return [pltpu.make_async_copy(q.at[1, h0], own.at[0], local_sems.at[0]),
                    pltpu.make_async_copy(q.at[0, h1], own.at[1], local_sems.at[1])]

        def start():
            for cp in via() + direct() + mine():
                cp.start()

        def relay():
            arrived, loaded, onward = via(), mine(), second()
            for k in range(2):
                arrived[k].wait_recv()
                loaded[k].wait()
                comb[k] = (own[k].astype(F32) + relayed[k].astype(F32)).astype(comb.dtype)
                onward[k].start()

        def finish():
            landing = direct() + second()
            for cp in landing:
                cp.wait_recv()
            for cp in via() + landing:
                cp.wait_send()

        return start, relay, finish


class _SumGather:
    def __init__(self, accs, lands):
        n = len(accs)
        self.n = n
        self.ins, self.in_specs = list(accs) + list(lands), [_VMEM] * (2 * n)
        self.out_shape = [_sds((NDEV,) + a.shape, a.dtype) for a in accs]
        self.out_specs = [_HBM] * n
        self.scratch = [pltpu.VMEM(a.shape, a.dtype) for a in accs] + _gather_sems(n)

    def ops(self, ins, outs, scr):
        n = self.n
        accs, lands, mine = ins[:n], ins[n:], scr[:n]
        g_start, relay, finish = _gather_ops(mine, outs, *scr[n:])

        def start():
            for i in range(n):
                mine[i][...] = accs[i][...] + lands[i][0] + lands[i][1] + lands[i][2]
            g_start()

        return start, relay, finish


def _call(main, jobs, *, name, grid, ins, in_specs, out_shape, out_specs, scratch, relay_step=0):
    nsteps = grid[0] if grid else 1
    n_in, n_out, n_scr = len(ins), len(out_shape), len(scratch)

    def body(*refs):
        pos = [0]

        def take(k):
            r = refs[pos[0]:pos[0] + k]
            pos[0] += k
            return r

        m_in = take(n_in)
        j_in = [take(len(j.ins)) for j in jobs]
        m_out = take(n_out)
        j_out = [take(len(j.out_shape)) for j in jobs]
        m_scr = take(n_scr)
        j_scr = [take(len(j.scratch)) for j in jobs]
        ops = [j.ops(a, b, s) for j, a, b, s in zip(jobs, j_in, j_out, j_scr)]
        i = pl.program_id(0) if grid else 0
        if not grid:
            for o in ops:
                o[0]()
            main(i, m_in, m_out, m_scr)
            for o in ops:
                o[1]()
            for o in ops:
                o[2]()
            return

        if ops:
            @pl.when(i == 0)
            def _():
                for o in ops:
                    o[0]()

        main(i, m_in, m_out, m_scr)

        if ops:
            @pl.when(i == min(relay_step, nsteps - 1))
            def _():
                for o in ops:
                    o[1]()

            @pl.when(i == nsteps - 1)
            def _():
                for o in ops:
                    o[2]()

    extra = dict(dimension_semantics=("arbitrary",)) if grid else {}
    res = pl.pallas_call(
        body, name=name, grid=grid,
        in_specs=list(in_specs) + [s for j in jobs for s in j.in_specs],
        out_specs=list(out_specs) + [s for j in jobs for s in j.out_specs],
        out_shape=list(out_shape) + [s for j in jobs for s in j.out_shape],
        scratch_shapes=list(scratch) + [s for j in jobs for s in j.scratch],
        compiler_params=_params(**extra),
    )(*ins, *[a for j in jobs for a in j.ins])
    main_out, rest, job_out = res[:n_out], res[n_out:], []
    for j in jobs:
        k = len(j.out_shape)
        job_out.append(rest[:k])
        rest = rest[k:]
    return main_out, job_out


def _comm_only(jobs, name):
    _, job_out = _call(lambda i, a, b, s: None, jobs, name=name, grid=(), ins=[], in_specs=[], out_shape=[],
                       out_specs=[], scratch=[])
    return job_out


class _InChip:
    def __init__(self, ps):
        n = len(ps)
        self.n = n
        blk = [p.shape[1:] for p in ps]
        self.ins, self.in_specs = list(ps), [_HBM] * n
        self.out_shape = [_sds((NCHIP_OTHER,) + b, p.dtype) for b, p in zip(blk, ps)] + [_sds(b, F32) for b in blk]
        self.out_specs = [_VMEM] * (2 * n)
        self.scratch = ([pltpu.VMEM((4,) + b, p.dtype) for b, p in zip(blk, ps)] * 2
                        + [pltpu.SemaphoreType.DMA((4, n))] * 3)

    def ops(self, ins, outs, scr):
        n = self.n
        q_refs, acc_refs = outs[:n], outs[n:]
        mines, lands = scr[:n], scr[n:2 * n]
        send_sems, recv_sems, local_sems = scr[2 * n:]
        x, y, c = _place()
        sibling = (x, y, 1 - c)

        def copies():
            out = []
            for i in range(n):
                for pi in range(4):
                    loc = pltpu.make_async_copy(ins[i].at[2 * pi + c], mines[i].at[pi], local_sems.at[pi, i])
                    cp = pltpu.make_async_remote_copy(
                        src_ref=ins[i].at[2 * pi + (1 - c)], dst_ref=lands[i].at[pi],
                        send_sem=send_sems.at[pi, i], recv_sem=recv_sems.at[pi, i],
                        device_id=sibling, device_id_type=MESH)
                    out.append((loc, cp))
            return out

        def start():
            for loc, cp in copies():
                loc.start()
                cp.start()

        def finish():
            pairs = copies()
            for loc, cp in pairs:
                loc.wait()
                cp.wait_recv()
            for i in range(n):
                _chip_sums(mines[i], lands[i], q_refs[i], acc_refs[i], x, y)
            for _, cp in pairs:
                cp.wait_send()

        return start, lambda: None, finish


def _chip_sums(mine, land, q_ref, acc_ref, x, y):
    for j, (qx, qy) in enumerate(_other_chips(x, y)):
        qi = 2 * qx + qy
        q_ref[j] = (mine[qi].astype(F32) + land[qi].astype(F32)).astype(q_ref.dtype)
    mi = 2 * x + y
    acc_ref[...] = mine[mi].astype(F32) + land[mi].astype(F32)


def _direct_sum(v, buf, send_sems, recv_sems):
    x, y, c = _place()
    me = 4 * x + 2 * y + c
    buf[me] = v
    cps = []
    for k in range(1, NDEV):
        fx, fy, fc = (k >> 2) & 1, (k >> 1) & 1, k & 1
        peer = ((1 - x) if fx else x, (1 - y) if fy else y, (1 - c) if fc else c)
        cps.append((peer, pltpu.make_async_remote_copy(
            src_ref=buf.at[me], dst_ref=buf.at[me], send_sem=send_sems.at[k - 1], recv_sem=recv_sems.at[k - 1],
            device_id=peer, device_id_type=MESH)))
    for _, cp in cps:
        cp.start()
    for k, (peer, _) in enumerate(cps):
        theirs = 4 * peer[0] + 2 * peer[1] + peer[2]
        pltpu.make_async_remote_copy(
            src_ref=buf.at[theirs], dst_ref=buf.at[theirs], send_sem=send_sems.at[k], recv_sem=recv_sems.at[k],
            device_id=peer, device_id_type=MESH).wait_recv()
    acc = buf[0]
    for j in range(1, NDEV):
        acc = acc + buf[j]
    for _, cp in cps:
        cp.wait_send()
    return acc


def _direct_sum_scratch(shape, dtype):
    return [pltpu.VMEM((NDEV,) + tuple(shape), dtype), pltpu.SemaphoreType.DMA((NDEV - 1,)),
            pltpu.SemaphoreType.DMA((NDEV - 1,))]


def _fwd_a(x, nw, win8, lnw, lnb, ws, bst, jobs, *, tm, relay_step):
    s_len = x.shape[0]
    nt = s_len // tm
    nch = tm // CH

    def main(i, ins, outs, scr):
        x_ref, nw_ref, win_ref, lnw_ref, lnb_ref, ws_ref, bst_ref = ins
        z_ref, h_ref, y_ref = outs
        wc_scr, gv_scr = scr

        @pl.when(i == 0)
        def _():
            m = _causal_mask()
            for g in range(G):
                wc_scr[g] = jnp.where(m, ws_ref[g], 0.0).astype(BF16)

        x = x_ref[...]
        h = (x * _rms(x) * nw_ref[...]).astype(BF16)
        h_ref[...] = h
        for k in range(NDEV):
            z_ref[:, k * CA:(k + 1) * CA] = _dot(h, win_ref[k])

        ssum = jnp.zeros((tm, 1), F32)
        for g in range(G):
            gv = _gelu_t(z_ref[:, AW + g * GD:AW + (g + 1) * GD])[0]
            gv_scr[:, g * GD:(g + 1) * GD] = gv
            ssum = ssum + jnp.sum(gv, axis=-1, keepdims=True)
        mu = ssum * (1.0 / AW)
        vsum = jnp.zeros((tm, 1), F32)
        for g in range(G):
            dlt = gv_scr[:, g * GD:(g + 1) * GD] - mu
            vsum = vsum + jnp.sum(dlt * dlt, axis=-1, keepdims=True)
        rstd = lax.rsqrt(vsum * (1.0 / AW) + LN_EPS)

        for g in range(G):
            cs = slice(g * GD, (g + 1) * GD)
            v = (gv_scr[:, cs] - mu) * rstd * lnw_ref[:, cs] + lnb_ref[:, cs]
            vb = v.astype(BF16)
            u = _gelu_t(z_ref[:, cs])[0]
            zg = z_ref[:, 2 * AW + g * GD:2 * AW + (g + 1) * GD]
            sg = zg * _sigmoid(zg)
            for n in range(nch):
                rs = slice(n * CH, (n + 1) * CH)
                s = _dot(wc_scr[g], vb[rs, :]) + bst_ref[:, g:g + 1]
                y_ref[rs, cs] = (u[rs, :] * s * sg[rs, :]).astype(BF16)

    tile = lambda w: pl.BlockSpec((tm, w), lambda i: (i, 0))
    return _call(
        main, jobs, name="fwd_a", grid=(nt,), relay_step=relay_step,
        ins=[x, nw, win8, lnw, lnb, ws, bst], in_specs=[tile(D), _VMEM, _VMEM, _VMEM, _VMEM, _VMEM, _VMEM],
        out_shape=[_sds((s_len, 3 * AW), F32), _sds((s_len, D), BF16), _sds((s_len, AW), BF16)],
        out_specs=[tile(3 * AW), tile(D), tile(AW)],
        scratch=[pltpu.VMEM((G, CH, CH), BF16), pltpu.VMEM((tm, AW), F32)])


def _bwd_a(dx1, z, lnw, lnb, ws, bst, wout, jobs, *, tm, relay_step):
    s_len = dx1.shape[0]
    nt = s_len // tm
    nch = tm // CH

    def main(i, ins, outs, scr):
        dx1_ref, z_ref, lnw_ref, lnb_ref, ws_ref, bst_ref, wout_ref = ins
        dz_ref, glnw_ref, glnb_ref, gws_ref, gbst_ref = outs
        wc_scr, wct_scr, vh_scr, dgv_scr, dy_scr, dv_scr, gbs_acc, gwc_acc = scr

        @pl.when(i == 0)
        def _():
            m = _causal_mask()
            for g in range(G):
                wm = jnp.where(m, ws_ref[g], 0.0)
                wc_scr[g] = wm.astype(BF16)
                wct_scr[g] = wm.T.astype(BF16)
            glnw_ref[...] = jnp.zeros_like(glnw_ref)
            glnb_ref[...] = jnp.zeros_like(glnb_ref)
            gbs_acc[...] = jnp.zeros_like(gbs_acc)
            gwc_acc[...] = jnp.zeros_like(gwc_acc)

        dy_scr[...] = _dot_nt(dx1_ref[...], wout_ref[...])

        ssum = jnp.zeros((tm, 1), F32)
        for g in range(G):
            cs = slice(g * GD, (g + 1) * GD)
            zv = z_ref[:, AW + g * GD:AW + (g + 1) * GD]
            gv, t = _gelu_t(zv)
            vh_scr[:, cs] = gv
            dgv_scr[:, cs] = _dgelu(zv, t)
            ssum = ssum + jnp.sum(gv, axis=-1, keepdims=True)
        mu = ssum * (1.0 / AW)
        vsum = jnp.zeros((tm, 1), F32)
        for g in range(G):
            dlt = vh_scr[:, g * GD:(g + 1) * GD] - mu
            vsum = vsum + jnp.sum(dlt * dlt, axis=-1, keepdims=True)
        rstd = lax.rsqrt(vsum * (1.0 / AW) + LN_EPS)

        m1 = jnp.zeros((tm, 1), F32)
        m2 = jnp.zeros((tm, 1), F32)
        for g in range(G):
            cs = slice(g * GD, (g + 1) * GD)
            gs = slice(2 * AW + g * GD, 2 * AW + (g + 1) * GD)
            vhat = (vh_scr[:, cs] - mu) * rstd
            vh_scr[:, cs] = vhat
            vb = (vhat * lnw_ref[:, cs] + lnb_ref[:, cs]).astype(BF16)
            zu = z_ref[:, cs]
            u, tu = _gelu_t(zu)
            zg = z_ref[:, gs]
            sig = _sigmoid(zg)
            sg = zg * sig
            dy = dy_scr[:, cs]
            dsf = dy * u * sg
            dsb = dsf.astype(BF16)
            dvs = []
            for n in range(nch):
                rs = slice(n * CH, (n + 1) * CH)
                s = _dot(wc_scr[g], vb[rs, :]) + bst_ref[:, g:g + 1]
                dys = dy[rs, :] * s
                dz_ref[rs, cs] = (dys * sg[rs, :] * _dgelu(zu[rs, :], tu[rs, :])).astype(BF16)
                dz_ref[rs, gs] = (dys * u[rs, :] * (sig[rs, :] * (1.0 + zg[rs, :] * (1.0 - sig[rs, :])))).astype(BF16)
                gbs_acc[g] += dsf[rs, :]
                gwc_acc[g] += _dot_nt(dsb[rs, :], vb[rs, :])
                dvs.append(_dot(wct_scr[g], dsb[rs, :]))
            dv = jnp.concatenate(dvs, axis=0) if nch > 1 else dvs[0]
            glnw_ref[:, cs] += _rowsum(dv * vhat)
            glnb_ref[:, cs] += _rowsum(dv)
            dvh = dv * lnw_ref[:, cs]
            dv_scr[:, cs] = dvh
            m1 = m1 + jnp.sum(dvh, axis=-1, keepdims=True)
            m2 = m2 + jnp.sum(dvh * vhat, axis=-1, keepdims=True)
        m1 = m1 * (1.0 / AW)
        m2 = m2 * (1.0 / AW)
        for g in range(G):
            cs = slice(g * GD, (g + 1) * GD)
            dgv = rstd * (dv_scr[:, cs] - m1 - vh_scr[:, cs] * m2)
            dz_ref[:, AW + g * GD:AW + (g + 1) * GD] = (dgv * dgv_scr[:, cs]).astype(BF16)

        @pl.when(i == nt - 1)
        def _():
            m = _causal_mask()
            for g in range(G):
                gws_ref[g] = jnp.where(m, gwc_acc[g], 0.0)
                gbst_ref[:, g:g + 1] = jnp.sum(gbs_acc[g], axis=-1, keepdims=True)

    tile = lambda w: pl.BlockSpec((tm, w), lambda i: (i, 0))
    whole = lambda *s: pl.BlockSpec(s, lambda i: (0,) * len(s))
    big = lambda dt: pltpu.VMEM((tm, AW), dt)
    return _call(
        main, jobs, name="bwd_a", grid=(nt,), relay_step=relay_step,
        ins=[dx1, z, lnw, lnb, ws, bst, wout], in_specs=[tile(D), tile(3 * AW), _VMEM, _VMEM, _VMEM, _VMEM, _VMEM],
        out_shape=[_sds((s_len, 3 * AW), BF16), _sds((1, AW), F32), _sds((1, AW), F32), _sds((G, CH, CH), F32),
                   _sds((CH, G), F32)],
        out_specs=[tile(3 * AW), whole(1, AW), whole(1, AW), whole(G, CH, CH), whole(CH, G)],
        scratch=[pltpu.VMEM((G, CH, CH), BF16), pltpu.VMEM((G, CH, CH), BF16), big(F32), big(F32), big(F32), big(F32),
                 pltpu.VMEM((G, CH, GD), F32), pltpu.VMEM((G, CH, CH), F32)])


def _bwd_a_in(dz, dx1, x, nw, win8, jobs, *, tm, relay_step):
    s_len = x.shape[0]
    nt = s_len // tm

    def main(i, ins, outs, scr):
        dz_ref, dx1_ref, x_ref, nw_ref, win_ref = ins
        gx_ref, gnw_ref = outs

        @pl.when(i == 0)
        def _():
            gnw_ref[...] = jnp.zeros_like(gnw_ref)

        dh = jnp.zeros((tm, D), F32)
        for k in range(NDEV):
            dh = dh + _dot_nt(dz_ref[:, k * CA:(k + 1) * CA], win_ref[k])
        x = x_ref[...]
        r = _rms(x)
        gx_ref[...] = dx1_ref[...] + _rms_bwd(dh, x, r, nw_ref[...])
        gnw_ref[...] += _rowsum(dh * x * r)

        @pl.when(i == nt - 1)
        def _():
            gnw_ref[...] = _direct_sum(gnw_ref[...], *scr)

    tile = lambda w: pl.BlockSpec((tm, w), lambda i: (i, 0))
    return _call(
        main, jobs, name="bwd_a_in", grid=(nt,), relay_step=relay_step,
        ins=[dz, dx1, x, nw, win8], in_specs=[tile(3 * AW), tile(D), tile(D), _VMEM, _VMEM],
        out_shape=[_sds((s_len, D), F32), _sds((1, D), F32)],
        out_specs=[tile(D), pl.BlockSpec((1, D), lambda i: (0, 0))], scratch=_direct_sum_scratch((1, D), F32))


def _conv(p8_ref, cs, xb, xm1, xm2, xm3):
    xc = p8_ref[4:5, cs] + p8_ref[3:4, cs] * xb
    xc = xc + p8_ref[0:1, cs] * xm3
    xc = xc + p8_ref[1:2, cs] * xm2
    return xc + p8_ref[2:3, cs] * xm1


def _gates(p8_ref, gcat_ref, hh, xc):
    cs = slice(hh * HD, (hh + 1) * HD)
    pre = _dot(xc.astype(BF16), gcat_ref[hh])
    r = _sigmoid(pre[:, :HD] + p8_ref[5:6, cs])
    ig = _sigmoid(pre[:, HD:] + p8_ref[6:7, cs])
    sp = _softplus_neg(p8_ref[7:8, cs])
    la = (-RG_C) * r * sp
    a = jnp.exp(la)
    half_log = 0.5 * jnp.log(jnp.tanh(-la) * (1.0 + a * a))
    return r, ig, sp, a, jnp.exp(half_log), jnp.exp(-half_log)


def _scan_rows(a_ref, b_ref, out_ref, carry, tm, reverse):
    row = lax.broadcasted_iota(jnp.int32, (SUBLANES, BW), 0)
    ngrp = tm // SUBLANES

    def step(j, cr):
        jj = (ngrp - 1 - j) if reverse else j
        off = pl.multiple_of(jj * SUBLANES, SUBLANES)
        a = a_ref[pl.ds(off, SUBLANES), :]
        b = b_ref[pl.ds(off, SUBLANES), :]
        for sh in (1, 2, 4):
            if reverse:
                a_s = pltpu.roll(a, SUBLANES - sh, 0)
                b_s = pltpu.roll(b, SUBLANES - sh, 0)
                m = row < SUBLANES - sh
            else:
                a_s = pltpu.roll(a, sh, 0)
                b_s = pltpu.roll(b, sh, 0)
                m = row >= sh
            b = jnp.where(m, a * b_s + b, b)
            a = jnp.where(m, a * a_s, a)
        o = b + a * cr
        out_ref[pl.ds(off, SUBLANES), :] = o
        return o[0:1, :] if reverse else o[SUBLANES - 1:SUBLANES, :]

    return lax.fori_loop(0, ngrp, step, carry)


def _fwd_b(x, ya, wout_a, nw, win8, p8, gcat, jobs, *, tm, relay_step):
    s_len = x.shape[0]
    nt = s_len // tm

    def main(i, ins, outs, scr):
        x_ref, ya_ref, wouta_ref, nw_ref, win_ref, p8_ref, gcat_ref = ins
        x1_ref, zb_ref, hs_ref, h1_ref, yb_ref, xc_ref, a_ref, cc_ref, r_ref, ig_ref, m_ref = outs
        xbe_scr, b_scr, k_scr, carry_scr = scr

        @pl.when(i == 0)
        def _():
            xbe_scr[0:SUBLANES, :] = jnp.zeros((SUBLANES, BW), F32)
            carry_scr[...] = jnp.zeros_like(carry_scr)

        x1 = x_ref[...] + _dot(ya_ref[...], wouta_ref[...])
        x1_ref[...] = x1
        h = (x1 * _rms(x1) * nw_ref[...]).astype(BF16)
        h1_ref[...] = h
        for k in range(NDEV):
            zb_ref[:, k * CB:(k + 1) * CB] = _dot(h, win_ref[k])
        xbe_scr[SUBLANES:SUBLANES + tm, :] = zb_ref[:, :BW]
        for hh in range(BH):
            cs = slice(hh * HD, (hh + 1) * HD)
            xc = _conv(p8_ref, cs, xbe_scr[SUBLANES:SUBLANES + tm, cs], xbe_scr[7:7 + tm, cs],
                       xbe_scr[6:6 + tm, cs], xbe_scr[5:5 + tm, cs])
            r, ig, _, a, mult, rm = _gates(p8_ref, gcat_ref, hh, xc)
            ixc = ig * xc
            xc_ref[:, cs] = xc
            a_ref[:, cs] = a
            r_ref[:, cs] = r.astype(BF16)
            ig_ref[:, cs] = ig.astype(BF16)
            m_ref[:, cs] = mult.astype(BF16)
            b_scr[:, cs] = mult * ixc
            k_scr[:, cs] = ixc * (a * a * rm)
        xbe_scr[0:SUBLANES, :] = xbe_scr[tm:tm + SUBLANES, :]
        carry_scr[...] = _scan_rows(a_ref, b_scr, hs_ref, carry_scr[...], tm, False)
        for hh in range(BH):
            cs = slice(hh * HD, (hh + 1) * HD)
            gt = zb_ref[:, BW + hh * HD:BW + (hh + 1) * HD]
            hsv = hs_ref[:, cs]
            yb_ref[:, cs] = (hsv * (gt * _sigmoid(gt))).astype(BF16)
            cc_ref[:, cs] = (hsv - b_scr[:, cs]) - k_scr[:, cs]

    tile = lambda w: pl.BlockSpec((tm, w), lambda i: (i, 0))
    wide = lambda dt: _sds((s_len, BW), dt)
    return _call(
        main, jobs, name="fwd_b", grid=(nt,), relay_step=relay_step,
        ins=[x, ya, wout_a, nw, win8, p8, gcat], in_specs=[tile(D), tile(AW), _VMEM, _VMEM, _VMEM, _VMEM, _VMEM],
        out_shape=[_sds((s_len, D), F32), _sds((s_len, 2 * BW), F32), wide(F32), _sds((s_len, D), BF16), wide(BF16),
                   wide(F32), wide(F32), wide(F32), wide(BF16), wide(BF16), wide(BF16)],
        out_specs=[tile(D), tile(2 * BW), tile(BW), tile(D)] + [tile(BW)] * 7,
        scratch=[pltpu.VMEM((tm + SUBLANES, BW), F32), pltpu.VMEM((tm, BW), F32), pltpu.VMEM((tm, BW), F32),
                 pltpu.VMEM((1, BW), F32)])


def _head(x1, yb, wout, nfw, tgt, *, tm):
    s_len = x1.shape[0]

    def main(i, ins, outs, scr):
        x1_ref, yb_ref, wout_ref, nfw_ref, t_ref = ins
        dx2_ref, dx2b_ref, loss_ref, gnfw_ref = outs

        @pl.when(i == 0)
        def _():
            loss_ref[...] = jnp.zeros_like(loss_ref)
            gnfw_ref[...] = jnp.zeros_like(gnfw_ref)

        x2 = x1_ref[...] + _dot(yb_ref[...], wout_ref[...])
        rf = _rms(x2)
        xn = x2 * rf
        e = xn * nfw_ref[...] - t_ref[...]
        loss_ref[...] += (0.5 / D) * jnp.sum(jnp.sum(e * e, axis=-1, keepdims=True), axis=0, keepdims=True)
        dyf = e * (1.0 / D)
        gnfw_ref[...] += _rowsum(dyf * xn)
        dx2 = _rms_bwd(dyf, x2, rf, nfw_ref[...])
        dx2_ref[...] = dx2
        dx2b_ref[...] = dx2.astype(BF16)

    tile = lambda w: pl.BlockSpec((tm, w), lambda i: (i, 0))
    whole = lambda *s: pl.BlockSpec(s, lambda i: (0,) * len(s))
    (dx2, dx2b, loss, gnfw), _ = _call(
        main, [], name="head", grid=(s_len // tm,),
        ins=[x1, yb, wout, nfw, tgt], in_specs=[tile(D), tile(BW), _VMEM, _VMEM, tile(D)],
        out_shape=[_sds((s_len, D), F32), _sds((s_len, D), BF16), _sds((1, 1), F32), _sds((1, D), F32)],
        out_specs=[tile(D), tile(D), whole(1, 1), whole(1, D)], scratch=[])
    return dx2, dx2b, loss, gnfw


def _bwd_b(dx2, zb, hs, x1, saved, nw, win8, p8, gcat, wout, *, tm):
    s_len = x1.shape[0]
    nt = s_len // tm

    def main(i, ins, outs, scr):
        (dx2_ref, zb_ref, hs_ref, x1_ref, xc_ref, a_ref, cc_ref, r_ref, ig_ref, m_ref,
         nw_ref, win_ref, p8_ref, gcat_ref, wout_ref) = ins
        dx1_ref, dx1b_ref, dzb_ref, gp8_ref, gga_ref, ggx_ref, gnw_ref = outs
        ae_scr, an_scr, dhd_scr, dh_scr, dy_scr, dxce_scr, carry_scr, afirst_scr = scr

        @pl.when(i == 0)
        def _():
            gp8_ref[...] = jnp.zeros_like(gp8_ref)
            gga_ref[...] = jnp.zeros_like(gga_ref)
            ggx_ref[...] = jnp.zeros_like(ggx_ref)
            gnw_ref[...] = jnp.zeros_like(gnw_ref)
            dxce_scr[tm:tm + SUBLANES, :] = jnp.zeros((SUBLANES, BW), F32)
            carry_scr[...] = jnp.zeros_like(carry_scr)
            afirst_scr[...] = jnp.zeros_like(afirst_scr)

        dx2 = dx2_ref[...]
        dy_scr[...] = _dot_nt(dx2.astype(BF16), wout_ref[...])
        for hh in range(BH):
            cs = slice(hh * HD, (hh + 1) * HD)
            gs = slice(BW + hh * HD, BW + (hh + 1) * HD)
            gt = zb_ref[:, gs]
            sig = _sigmoid(gt)
            dy = dy_scr[:, cs]
            dhd_scr[:, cs] = dy * (gt * sig)
            dzb_ref[:, gs] = (dy * hs_ref[:, cs] * (sig * (1.0 + gt * (1.0 - sig)))).astype(BF16)

        ae_scr[0:tm, :] = a_ref[...]
        ae_scr[tm:tm + SUBLANES, :] = jnp.broadcast_to(afirst_scr[...], (SUBLANES, BW))
        an_scr[...] = ae_scr[1:1 + tm, :]
        afirst_scr[...] = ae_scr[0:1, :]
        carry_scr[...] = _scan_rows(an_scr, dhd_scr, dh_scr, carry_scr[...], tm, True)

        for hh in range(BH):
            cs = slice(hh * HD, (hh + 1) * HD)
            dh = dh_scr[:, cs]
            mult = m_ref[:, cs].astype(F32)
            ig = ig_ref[:, cs].astype(F32)
            r = r_ref[:, cs].astype(F32)
            xc = xc_ref[:, cs]
            lam = p8_ref[7:8, cs]
            sp = _softplus_neg(lam)
            dla = dh * cc_ref[:, cs]
            gp8_ref[7:8, cs] += _rowsum(dla * ((-RG_C) * r)) * (-_sigmoid(-lam))
            dpr = dla * ((-RG_C) * sp) * (r * (1.0 - r))
            dpi = dh * mult * xc * (ig * (1.0 - ig))
            gp8_ref[5:6, cs] += _rowsum(dpr)
            gp8_ref[6:7, cs] += _rowsum(dpi)
            dcat = jnp.concatenate([dpr, dpi], axis=1).astype(BF16)
            dxc = dh * mult * ig + _dot_nt(dcat, gcat_ref[hh])
            gg = _dot(xc.T.astype(BF16), dcat)
            gga_ref[hh] += gg[:, :HD]
            ggx_ref[hh] += gg[:, HD:]
            dxce_scr[0:tm, cs] = dxc
            gp8_ref[4:5, cs] += _rowsum(dxc)
        for hh in range(BH):
            cs = slice(hh * HD, (hh + 1) * HD)
            xb = zb_ref[:, cs]
            d0, d1 = dxce_scr[0:tm, cs], dxce_scr[1:1 + tm, cs]
            d2, d3 = dxce_scr[2:2 + tm, cs], dxce_scr[3:3 + tm, cs]
            dzb_ref[:, cs] = (p8_ref[3:4, cs] * d0 + p8_ref[2:3, cs] * d1 + p8_ref[1:2, cs] * d2
                              + p8_ref[0:1, cs] * d3).astype(BF16)
            gp8_ref[3:4, cs] += _rowsum(d0 * xb)
            gp8_ref[2:3, cs] += _rowsum(d1 * xb)
            gp8_ref[1:2, cs] += _rowsum(d2 * xb)
            gp8_ref[0:1, cs] += _rowsum(d3 * xb)
        dxce_scr[tm:tm + SUBLANES, :] = dxce_scr[0:SUBLANES, :]

        dh1 = jnp.zeros((tm, D), F32)
        for k in range(NDEV):
            dh1 = dh1 + _dot_nt(dzb_ref[:, k * CB:(k + 1) * CB], win_ref[k])
        x1 = x1_ref[...]
        r1 = _rms(x1)
        dx1 = dx2 + _rms_bwd(dh1, x1, r1, nw_ref[...])
        dx1_ref[...] = dx1
        dx1b_ref[...] = dx1.astype(BF16)
        gnw_ref[...] += _rowsum(dh1 * x1 * r1)

    tile = lambda w: pl.BlockSpec((tm, w), lambda i: (nt - 1 - i, 0))
    whole = lambda *s: pl.BlockSpec(s, lambda i: (0,) * len(s))
    full = lambda: pltpu.VMEM((tm, BW), F32)
    ext = lambda: pltpu.VMEM((tm + SUBLANES, BW), F32)
    out, _ = _call(
        main, [], name="bwd_b", grid=(nt,),
        ins=[dx2, zb, hs, x1, *saved, nw, win8, p8, gcat, wout],
        in_specs=[tile(D), tile(2 * BW), tile(BW), tile(D)] + [tile(BW)] * 6 + [_VMEM] * 5,
        out_shape=[_sds((s_len, D), F32), _sds((s_len, D), BF16), _sds((s_len, 2 * BW), BF16), _sds((SUBLANES, BW), F32),
                   _sds((BH, HD, HD), F32), _sds((BH, HD, HD), F32), _sds((1, D), F32)],
        out_specs=[tile(D), tile(D), tile(2 * BW), whole(SUBLANES, BW), whole(BH, HD, HD), whole(BH, HD, HD),
                   whole(1, D)],
        scratch=[ext(), full(), full(), full(), full(), ext(), pltpu.VMEM((1, BW), F32), pltpu.VMEM((1, BW), F32)])
    return out


def _transpose_into(dst_ref, src_ref, rows):
    s_len = src_ref.shape[0]
    for r0 in range(0, s_len, rows):
        dst_ref[:, r0:r0 + rows] = src_ref[r0:r0 + rows, :].astype(F32).T.astype(BF16)


def _wgrad(a, b, jobs, *, by_rows, per, name, relay_step=0):
    s_len, m = a.shape
    n = b.shape[1]
    r, cd = (m // NDEV, n) if by_rows else (m, n // NDEV)
    nsteps = NDEV // per
    at_rows = per * r if by_rows else m

    def main(i, ins, outs, scr):
        a_ref, b_ref = ins
        q_ref, acc_ref = outs
        at_scr, stage, mine, land, send_sems, recv_sems = scr
        x, y, c = _place()

        def to_sibling(pi):
            return pltpu.make_async_remote_copy(
                src_ref=stage.at[pi & 1], dst_ref=land.at[pi], send_sem=send_sems.at[pi], recv_sem=recv_sems.at[pi],
                device_id=(x, y, 1 - c), device_id_type=MESH)

        if by_rows:
            _transpose_into(at_scr, a_ref, TRANSPOSE_ROWS)
        else:
            @pl.when(i == 0)
            def _():
                _transpose_into(at_scr, a_ref, TRANSPOSE_ROWS)

        res = _dot(at_scr[...], b_ref[...]).astype(BF16)
        for k in range(per):
            blk = per * i + k
            pi, pc = blk >> 1, blk & 1
            val = res[k * r:(k + 1) * r, :] if by_rows else res

            @pl.when(pc != c)
            def _():
                @pl.when(pi >= 2)
                def _():
                    to_sibling(pi - 2).wait_send()

                stage[pi & 1] = val
                to_sibling(pi).start()

            @pl.when(pc == c)
            def _():
                mine[pi] = val

        @pl.when(i == nsteps - 1)
        def _():
            for p in range(4):
                to_sibling(p).wait_recv()
            to_sibling(2).wait_send()
            to_sibling(3).wait_send()
            _chip_sums(mine, land, q_ref, acc_ref, x, y)

    if by_rows:
        in_specs = [pl.BlockSpec((s_len, at_rows), lambda j: (0, j)), _VMEM]
    else:
        in_specs = [_VMEM, pl.BlockSpec((s_len, cd), lambda j: (0, j))]
    blk_vmem = lambda k: pltpu.VMEM((k, r, cd), BF16)
    (q, acc), job_out = _call(
        main, jobs, name=name, grid=(nsteps,), relay_step=relay_step, ins=[a, b], in_specs=in_specs,
        out_shape=[_sds((NCHIP_OTHER, r, cd), BF16), _sds((r, cd), F32)],
        out_specs=[pl.BlockSpec((NCHIP_OTHER, r, cd), lambda j: (0, 0, 0)), pl.BlockSpec((r, cd), lambda j: (0, 0))],
        scratch=[pltpu.VMEM((at_rows, s_len), BF16), blk_vmem(2), blk_vmem(4), blk_vmem(4),
                 pltpu.SemaphoreType.DMA((4,)), pltpu.SemaphoreType.DMA((4,))])
    return q, acc, job_out


def _wgrad_cols_early(a, b, jobs, *, name, relay_step=0):
    s_len, m = a.shape
    r, cd = m, b.shape[1] // NDEV
    h = r // 2

    def chip_at(pos, base):
        return base ^ (3 - pos)

    def main(i, ins, outs, scr):
        a_ref, b_ref = ins
        q_ref, acc_ref, rel_ref = outs
        at_scr, stage, mine, land, q2_scr, send_sems, recv_sems, via_send, via_recv = scr
        x, y, c = _place()
        base = 2 * x + y
        xn, yn, _ = _other_chips(x, y)
        pos, pc = i >> 1, i & 1
        pi = chip_at(pos, base)

        def to_sibling(chip, slot):
            return pltpu.make_async_remote_copy(
                src_ref=stage.at[slot], dst_ref=land.at[chip], send_sem=send_sems.at[chip],
                recv_sem=recv_sems.at[chip], device_id=(x, y, 1 - c), device_id_type=MESH)

        def via(k):
            return pltpu.make_async_remote_copy(
                src_ref=q2_scr.at[pl.ds(k * h, h)], dst_ref=rel_ref.at[k], send_sem=via_send.at[k],
                recv_sem=via_recv.at[k], device_id=(*(xn, yn)[k], c), device_id_type=MESH)

        @pl.when(i == 0)
        def _():
            _transpose_into(at_scr, a_ref, TRANSPOSE_ROWS)

        res = _dot(at_scr[...], b_ref[...]).astype(BF16)

        @pl.when(pc != c)
        def _():
            @pl.when(pos >= 2)
            def _():
                to_sibling(chip_at(pos - 2, base), pos & 1).wait_send()

            stage[pos & 1] = res
            to_sibling(pi, pos & 1).start()

        @pl.when(pc == c)
        def _():
            mine[pi] = res

        @pl.when(i == 1)
        def _():
            dg = chip_at(0, base)
            to_sibling(dg, 0).wait_recv()
            q2 = (mine[dg].astype(F32) + land[dg].astype(F32)).astype(BF16)
            q2_scr[...] = q2
            q_ref[2] = q2
            via(0).start()
            via(1).start()

        @pl.when(i == NDEV - 1)
        def _():
            for pos_ in (1, 2, 3):
                to_sibling(chip_at(pos_, base), 0).wait_recv()
            to_sibling(chip_at(2, base), 0).wait_send()
            to_sibling(chip_at(3, base), 1).wait_send()
            for k in range(2):
                via(k).wait_recv()
            for k in range(2):
                via(k).wait_send()
            for j, chip in enumerate((base ^ 2, base ^ 1)):
                q_ref[j] = (mine[chip].astype(F32) + land[chip].astype(F32)).astype(BF16)
            acc_ref[...] = mine[base].astype(F32) + land[base].astype(F32)

    def b_block(j):
        base = 2 * lax.axis_index("x") + lax.axis_index("y")
        return (0, 2 * chip_at(j >> 1, base) + (j & 1))

    blk_vmem = lambda k: pltpu.VMEM((k, r, cd), BF16)
    (q, acc, rel), job_out = _call(
        main, jobs, name=name, grid=(NDEV,), relay_step=relay_step, ins=[a, b],
        in_specs=[_VMEM, pl.BlockSpec((s_len, cd), b_block)],
        out_shape=[_sds((NCHIP_OTHER, r, cd), BF16), _sds((r, cd), F32), _sds((2, h, cd), BF16)],
        out_specs=[pl.BlockSpec((NCHIP_OTHER, r, cd), lambda j: (0, 0, 0)), pl.BlockSpec((r, cd), lambda j: (0, 0)), _HBM],
        scratch=[pltpu.VMEM((m, s_len), BF16), blk_vmem(2), blk_vmem(4), blk_vmem(4), pltpu.VMEM((r, cd), BF16),
                 pltpu.SemaphoreType.DMA((4,)), pltpu.SemaphoreType.DMA((4,)), pltpu.SemaphoreType.DMA((2,)),
                 pltpu.SemaphoreType.DMA((2,))])
    return q, acc, rel, job_out


class _ExchangeRest:
    def __init__(self, q, relayed):
        _, r, cd = q.shape
        half = (2, r // 2, cd)
        self.ins, self.in_specs = [q, relayed], [_HBM, _HBM]
        self.out_shape, self.out_specs = [_sds((2, r, cd), q.dtype)], [_HBM]
        self.scratch = [pltpu.VMEM(half, q.dtype), pltpu.VMEM(half, q.dtype), pltpu.VMEM(half, q.dtype),
                        pltpu.SemaphoreType.DMA((4,)), pltpu.SemaphoreType.DMA((4,)), pltpu.SemaphoreType.DMA((4,))]

    def ops(self, ins, outs, scr):
        (q, rel_in), (land,) = ins, outs
        own, rel, comb, send_sems, recv_sems, local_sems = scr
        h = q.shape[1] // 2
        x, y, c = _place()
        xn, yn, _ = _other_chips(x, y)
        h0, h1 = pl.ds(0, h), pl.ds(h, h)

        def remote(k, src, dst, chip):
            return pltpu.make_async_remote_copy(src_ref=src, dst_ref=dst, send_sem=send_sems.at[k],
                                                recv_sem=recv_sems.at[k], device_id=(*chip, c), device_id_type=MESH)

        def sends():
            return [remote(0, q.at[0, h0], land.at[0, h0], xn), remote(1, q.at[1, h1], land.at[1, h1], yn),
                    remote(2, comb.at[0], land.at[1, h0], yn), remote(3, comb.at[1], land.at[0, h1], xn)]

        def loads():
            return [pltpu.make_async_copy(q.at[1, h0], own.at[0], local_sems.at[0]),
                    pltpu.make_async_copy(q.at[0, h1], own.at[1], local_sems.at[1]),
                    pltpu.make_async_copy(rel_in.at[0], rel.at[0], local_sems.at[2]),
                    pltpu.make_async_copy(rel_in.at[1], rel.at[1], local_sems.at[3])]

        def start():
            cps, lds = sends(), loads()
            for ld in lds:
                ld.start()
            cps[0].start()
            cps[1].start()
            for ld in lds:
                ld.wait()
            for k in range(2):
                comb[k] = (own[k].astype(F32) + rel[k].astype(F32)).astype(comb.dtype)
            cps[2].start()
            cps[3].start()

        def finish():
            cps = sends()
            for cp in cps:
                cp.wait_recv()
            for cp in cps:
                cp.wait_send()

        return start, lambda: None, finish


def _adam_math(w, g, m, v):
    m = B1 * m + (1.0 - B1) * g
    v = B2 * v + (1.0 - B2) * (g * g)
    m_hat = m / (1.0 - B1 ** STEP)
    v_hat = v / (1.0 - B2 ** STEP)
    delta = (-LR) * (m_hat / (jnp.sqrt(v_hat) + ADAM_EPS) + WD * w)
    return delta, m, v


def _adam_big(w, acc, land, m, v, name):
    r, cd = w.shape
    rb = ADAM_ROWS if r % ADAM_ROWS == 0 else r
    nland = land.shape[0]

    def body(w_ref, acc_ref, land_ref, m_ref, v_ref, g_ref, d_ref, mo_ref, vo_ref):
        g = acc_ref[...]
        for j in range(nland):
            g = g + land_ref[j].astype(F32)
        g_ref[...] = g
        d_ref[...], mo_ref[...], vo_ref[...] = _adam_math(w_ref[...], g, m_ref[...], v_ref[...])

    blk = pl.BlockSpec((rb, cd), lambda i: (i, 0))
    blk3 = pl.BlockSpec((nland, rb, cd), lambda i: (0, i, 0))
    return pl.pallas_call(
        body, name=name, grid=(r // rb,), in_specs=[blk, blk, blk3, blk, blk], out_specs=[blk] * 4,
        out_shape=[_sds((r, cd), F32)] * 4,
        compiler_params=_params(dimension_semantics=("arbitrary",)),
    )(w, acc, land, m, v)


def _adam_small(groups):
    n = len(groups)

    def body(*refs):
        ins, outs = refs[:4 * n], refs[4 * n:]
        for k in range(n):
            w_ref, g_ref, m_ref, v_ref = ins[4 * k:4 * k + 4]
            d, mo, vo = _adam_math(w_ref[...], g_ref[...], m_ref[...], v_ref[...])
            outs[3 * k][...] = d
            outs[3 * k + 1][...] = mo
            outs[3 * k + 2][...] = vo

    flat = [a for grp in groups for a in grp]
    shapes = [_sds(grp[0].shape, F32) for grp in groups for _ in range(3)]
    res = pl.pallas_call(
        body, name="adam_small", in_specs=[_VMEM] * (4 * n), out_specs=[_VMEM] * (3 * n), out_shape=shapes,
        compiler_params=_params(),
    )(*flat)
    return [tuple(res[3 * k:3 * k + 3]) for k in range(n)]


TM_FWD_A = 256
RELAY_STEP_FWD_A = 4
RELAY_STEP_FWD_B = 2
TM_BWD_A = 256
RELAY_STEP_BWD_A = 3
TM_BWD_A_IN = 256
RELAY_STEP_BWD_A_IN = 4
RELAY_STEP_WGRAD_A_IN = 2
TM_FWD_B = 256
TM_HEAD = 512
TM_BWD_B = 256


def _pack(parts, rows):
    flat = jnp.concatenate([p.reshape(-1) for p in parts])
    return jnp.pad(flat, (0, NDEV * rows * LANES - flat.shape[0])).reshape(NDEV, rows, LANES)


def _unpack(packed, shapes):
    flat, out, off = packed.reshape(-1), [], 0
    for s in shapes:
        size = 1
        for d in s:
            size *= d
        out.append(flat[off:off + size].reshape(s))
        off += size
    return out


def kernel(x, norm_w, a_w_in, a_ln_w, a_ln_b, a_w_s, a_b_s, a_w_out, b_w_in, b_conv_w, b_conv_b, b_gate_a_w, b_gate_a_b, b_gate_x_w, b_gate_x_b, b_lambda, b_w_out, norm_f_w, loss_target, m_norm_w, m_a_w_in, m_a_ln_w, m_a_ln_b, m_a_w_s, m_a_b_s, m_a_w_out, m_b_w_in, m_b_conv_w, m_b_conv_b, m_b_gate_a_w, m_b_gate_a_b, m_b_gate_x_w, m_b_gate_x_b, m_b_lambda, m_b_w_out, m_norm_f_w, v_norm_w, v_a_w_in, v_a_ln_w, v_a_ln_b, v_a_w_s, v_a_b_s, v_a_w_out, v_b_w_in, v_b_conv_w, v_b_conv_b, v_b_gate_a_w, v_b_gate_a_b, v_b_gate_x_w, v_b_gate_x_b, v_b_lambda, v_b_w_out, v_norm_f_w):
    me = 4 * lax.axis_index("x") + 2 * lax.axis_index("y") + lax.axis_index("c")
    xs, tgt = x[0], loss_target[0]
    nw0, nw1, nfw = norm_w[0:1], norm_w[1:2], norm_f_w.reshape(1, D)
    w_s, bst = a_w_s[0], a_b_s[0].T
    gcat = jnp.concatenate([b_gate_a_w[0], b_gate_x_w[0]], axis=-1).astype(BF16)

    p8_shard = jnp.concatenate([b_conv_w[0], b_conv_b, b_gate_a_b, b_gate_x_b, b_lambda], axis=0)
    ((win_a8, p8_all),) = _comm_only([_Gather([a_w_in[0], p8_shard], [BF16, F32])], "gather_first")
    p8 = jnp.transpose(p8_all, (1, 0, 2)).reshape(SUBLANES, BW)

    (z, h0, ya), ((wout_a8, win_b8),) = _fwd_a(
        xs, nw0, win_a8, a_ln_w, a_ln_b, w_s, bst, [_Gather([a_w_out[0], b_w_in[0]], [BF16, BF16])],
        tm=TM_FWD_A, relay_step=RELAY_STEP_FWD_A)
    wout_a = wout_a8.reshape(AW, D)
    (x1, zb, hs, h1, yb, *saved_b), ((wout_b8,),) = _fwd_b(
        xs, ya, wout_a, nw1, win_b8, p8, gcat, [_Gather([b_w_out[0]], [BF16])],
        tm=TM_FWD_B, relay_step=RELAY_STEP_FWD_B)
    wout_b = wout_b8.reshape(BW, D)
    dx2, dx2b, loss, g_nfw = _head(x1, yb, wout_b, nfw, tgt, tm=TM_HEAD)

    dx1, dx1b, dzb, g_p8, g_ga, g_gx, g_nw1 = _bwd_b(dx2, zb, hs, x1, saved_b, nw1, win_b8, p8, gcat, wout_b,
                                                     tm=TM_BWD_B)
    q_wout_b, acc_wout_b, _ = _wgrad(yb, dx2b, [], by_rows=True, per=2, name="wgrad_b_out")
    shapes_b = [(1, D), (1, D), (SUBLANES, BW), (1, 1)]
    pack_b = _pack([g_nfw, g_nw1, g_p8, loss], 16)
    small_b = _InChip([g_ga.reshape(NDEV, -1, HD), g_gx.reshape(NDEV, -1, HD), pack_b])
    q_win_b, acc_win_b, (sm_b, (l_wout_b,)) = _wgrad(h1, dzb, [small_b, _Exchange([q_wout_b])], by_rows=False, per=1,
                                                      name="wgrad_b_in")
    qs_b, accs_b = sm_b[:3], sm_b[3:]

    (dz, g_lnw, g_lnb, g_ws, g_bst), (lands_b, (l_win_b,)) = _bwd_a(
        dx1b, z, a_ln_w, a_ln_b, w_s, bst, wout_a, [_Exchange(qs_b), _ExchangeVia(q_win_b)],
        tm=TM_BWD_A, relay_step=RELAY_STEP_BWD_A)
    shapes_a = [(1, AW), (1, AW), (CH, G)]
    pack_a = _pack([g_lnw, g_lnb, g_bst], 8)
    q_wout_a, acc_wout_a, (red_b, sm_a) = _wgrad(
        ya, dx1b, [_SumGather(accs_b, lands_b), _InChip([g_ws, pack_a])], by_rows=True, per=2,
        name="wgrad_a_out", relay_step=1)
    qs_a, accs_a = sm_a[:2], sm_a[2:]
    q_win_a, acc_win_a, rel_a, (lands_a, (l_wout_a,)) = _wgrad_cols_early(
        h0, dz, [_Exchange(qs_a), _ExchangeVia(q_wout_a)], name="wgrad_a_in", relay_step=RELAY_STEP_WGRAD_A_IN)
    (gx, g_nw0), (red_a, (l_win_a,)) = _bwd_a_in(
        dz, dx1, xs, nw0, win_a8, [_SumGather(accs_a, lands_a), _ExchangeRest(q_win_a, rel_a)],
        tm=TM_BWD_A_IN, relay_step=RELAY_STEP_BWD_A_IN)

    r_ga, r_gx, r_pack_b = red_b
    r_nfw, r_nw1, r_p8, loss = _unpack(r_pack_b, shapes_b)
    r_ws, r_pack_a = red_a
    r_lnw, r_lnb, r_bst = _unpack(r_pack_a, shapes_a)
    g_p8 = lax.dynamic_slice_in_dim(r_p8, me * (BW // NDEV), BW // NDEV, axis=1)
    loss = loss[0, 0]

    weights = dict(norm_w=norm_w, a_w_in=a_w_in, a_ln_w=a_ln_w, a_ln_b=a_ln_b, a_w_s=a_w_s, a_b_s=a_b_s, a_w_out=a_w_out,
                   b_w_in=b_w_in, b_conv_w=b_conv_w, b_conv_b=b_conv_b, b_gate_a_w=b_gate_a_w, b_gate_a_b=b_gate_a_b,
                   b_gate_x_w=b_gate_x_w, b_gate_x_b=b_gate_x_b, b_lambda=b_lambda, b_w_out=b_w_out, norm_f_w=norm_f_w)
    mom1 = dict(norm_w=m_norm_w, a_w_in=m_a_w_in, a_ln_w=m_a_ln_w, a_ln_b=m_a_ln_b, a_w_s=m_a_w_s, a_b_s=m_a_b_s,
                a_w_out=m_a_w_out, b_w_in=m_b_w_in, b_conv_w=m_b_conv_w, b_conv_b=m_b_conv_b, b_gate_a_w=m_b_gate_a_w,
                b_gate_a_b=m_b_gate_a_b, b_gate_x_w=m_b_gate_x_w, b_gate_x_b=m_b_gate_x_b, b_lambda=m_b_lambda,
                b_w_out=m_b_w_out, norm_f_w=m_norm_f_w)
    mom2 = dict(norm_w=v_norm_w, a_w_in=v_a_w_in, a_ln_w=v_a_ln_w, a_ln_b=v_a_ln_b, a_w_s=v_a_w_s, a_b_s=v_a_b_s,
                a_w_out=v_a_w_out, b_w_in=v_b_w_in, b_conv_w=v_b_conv_w, b_conv_b=v_b_conv_b, b_gate_a_w=v_b_gate_a_w,
                b_gate_a_b=v_b_gate_a_b, b_gate_x_w=v_b_gate_x_w, b_gate_x_b=v_b_gate_x_b, b_lambda=v_b_lambda,
                b_w_out=v_b_w_out, norm_f_w=v_norm_f_w)
    names = list(weights)

    def as2d(a):
        return a.reshape(-1, a.shape[-1])

    upd, grads = {}, {}
    for k, acc, land in (("a_w_in", acc_win_a, l_win_a), ("a_w_out", acc_wout_a, l_wout_a),
                         ("b_w_in", acc_win_b, l_win_b), ("b_w_out", acc_wout_b, l_wout_b)):
        g, d, mo, vo = _adam_big(as2d(weights[k]), acc, land, as2d(mom1[k]), as2d(mom2[k]), "adam_" + k)
        grads[k] = g[None]
        upd[k] = (d, mo, vo)
    grads.update(
        norm_w=jnp.concatenate([g_nw0, r_nw1], axis=0), a_ln_w=r_lnw, a_ln_b=r_lnb,
        a_w_s=r_ws.reshape(1, G, CH, CH), a_b_s=r_bst.T[None],
        b_conv_w=g_p8[None, 0:4], b_conv_b=g_p8[4:5], b_gate_a_w=r_ga.reshape(1, BH, HD, HD), b_gate_a_b=g_p8[5:6],
        b_gate_x_w=r_gx.reshape(1, BH, HD, HD), b_gate_x_b=g_p8[6:7], b_lambda=g_p8[7:8], norm_f_w=r_nfw.reshape(D))
    small_names = [k for k in names if k not in upd]
    res = _adam_small([(as2d(weights[k]), as2d(grads[k]), as2d(mom1[k]), as2d(mom2[k])) for k in small_names])
    for k, r3 in zip(small_names, res):
        upd[k] = r3
    deltas = [upd[k][0].reshape(weights[k].shape) for k in names]
    new_m = [upd[k][1].reshape(weights[k].shape) for k in names]
    new_v = [upd[k][2].reshape(weights[k].shape) for k in names]
    return (loss, gx[None], *[grads[k] for k in names], *deltas, *new_m, *new_v)
```

```python
import jax
import jax.numpy as jnp
from jax import lax
from jax.experimental import pallas as pl
from jax.experimental.pallas import tpu as pltpu

F32 = jnp.float32
BF16 = jnp.bfloat16
MESH = pl.DeviceIdType.MESH

NDEV = 8
NCHIP_OTHER = 3
D = 1024
AW = 2048
G = 8
GD = AW // G
CH = 128
BW = 1536
BH = 12
HD = BW // BH
CA = 3 * AW // NDEV
CB = 2 * BW // NDEV
RMS_EPS = 1e-6
LN_EPS = 1e-5
RG_C = 8.0
LR, B1, B2, ADAM_EPS, WD, STEP = 0.001, 0.9, 0.999, 1e-08, 0.01, 10
V7X_VMEM_BYTES = 64 * 1024 * 1024
VMEM_LIMIT = V7X_VMEM_BYTES - 8 * 1024 * 1024
SUBLANES = 8
LANES = 128
BF16_ROWS = 16
TRANSPOSE_ROWS = 256
ADAM_ROWS = 512
GELU_C = 0.7978845608028654
GELU_K = 0.044715

_VMEM = pl.BlockSpec(memory_space=pltpu.VMEM)
_HBM = pl.BlockSpec(memory_space=pltpu.HBM)


def _sds(shape, dtype):
    return jax.ShapeDtypeStruct(tuple(shape), dtype)


def _params(**kw):
    return pltpu.CompilerParams(vmem_limit_bytes=VMEM_LIMIT, **kw)


def _gelu_t(z):
    t = jnp.tanh(GELU_C * (z + GELU_K * (z * z * z)))
    return 0.5 * z * (1.0 + t), t


def _dgelu(z, t):
    return 0.5 * (1.0 + t) + 0.5 * z * (1.0 - t * t) * (GELU_C * (1.0 + 3.0 * GELU_K * z * z))


def _sigmoid(v):
    return 0.5 * jnp.tanh(0.5 * v) + 0.5


def _softplus_neg(lam):
    return jnp.maximum(-lam, 0.0) + jnp.log1p(jnp.exp(-jnp.abs(lam)))


def _dot(a, b):
    return jnp.dot(a, b, preferred_element_type=F32)


def _dot_nt(a, b):
    return lax.dot_general(a, b, (((1,), (1,)), ((), ())), preferred_element_type=F32)


def _rowsum(v):
    return jnp.sum(v, axis=0, keepdims=True)


def _causal_mask():
    r = lax.broadcasted_iota(jnp.int32, (CH, CH), 0)
    c = lax.broadcasted_iota(jnp.int32, (CH, CH), 1)
    return r >= c


def _rms(x):
    return lax.rsqrt(jnp.mean(x * x, axis=-1, keepdims=True) + RMS_EPS)


def _rms_bwd(dh, x, r, nw):
    gy = dh * nw
    return r * gy - x * (r * r * r) * jnp.mean(gy * x, axis=-1, keepdims=True)


def _place():
    return lax.axis_index("x"), lax.axis_index("y"), lax.axis_index("c")


def _other_chips(x, y):
    return [(1 - x, y), (x, 1 - y), (1 - x, 1 - y)]


GATHER_SLOTS = 10


def _gather_ops(ins, outs, send_sems, recv_sems, local_sems):
    n = len(ins)
    x, y, c = _place()
    sibling = (x, y, 1 - c)
    xn, yn, dg = _other_chips(x, y)
    split = [ins[i].shape[0] % (2 * BF16_ROWS) == 0 for i in range(n)]

    def blk(chip, core):
        return 4 * chip[0] + 2 * chip[1] + core

    me = blk((x, y), c)

    def part(ref, i, half):
        if half is None:
            return ref
        h = ins[i].shape[0] // 2
        return ref.at[pl.ds(half * h, h)]

    def copy(i, k, block, to, half=None, src=None):
        dst = part(outs[i].at[block], i, half)
        return pltpu.make_async_remote_copy(
            src_ref=dst if src is None else part(src, i, half), dst_ref=dst,
            send_sem=send_sems.at[k, i], recv_sem=recv_sems.at[k, i], device_id=to, device_id_type=MESH)

    def first_copies():
        mine = [pltpu.make_async_copy(ins[i], outs[i].at[me], local_sems.at[i]) for i in range(n)]
        first = []
        for i in range(n):
            first.append(copy(i, 0, me, sibling, src=ins[i]))
            if split[i]:
                first.append(copy(i, 1, me, (*xn, c), 0, ins[i]))
                first.append(copy(i, 3, me, (*yn, c), 1, ins[i]))
                first.append(copy(i, 2, me, (*xn, c), 1, ins[i]))
                first.append(copy(i, 4, me, (*yn, c), 0, ins[i]))
            else:
                first.append(copy(i, 1, me, (*xn, c), None, ins[i]))
                first.append(copy(i, 3, me, (*yn, c), None, ins[i]))
                first.append(copy(i, 5, me, (*dg, c), None, ins[i]))
        return mine, first

    def onward():
        out = []
        for i in range(n):
            if split[i]:
                out.append(copy(i, 5, blk(xn, c), (*yn, c), 0))
                out.append(copy(i, 6, blk(yn, c), (*xn, c), 1))
        return out

    def start():
        mine, first = first_copies()
        for cp in mine + first:
            cp.start()

    def relay():
        sends = onward()
        for i in range(n):
            if split[i]:
                copy(i, 1, blk(xn, c), sibling, 0).wait_recv()
                sends.pop(0).start()
                copy(i, 3, blk(yn, c), sibling, 1).wait_recv()
                sends.pop(0).start()

    def finish():
        mine, first = first_copies()
        passed = []

        def pass_on(i, j, chip):
            fwd = copy(i, 7 + j, blk(chip, c), sibling)
            fwd.start()
            passed.append(fwd)

        for i in range(n):
            if split[i]:
                copy(i, 2, blk(xn, c), sibling, 1).wait_recv()
                pass_on(i, 0, xn)
                copy(i, 4, blk(yn, c), sibling, 0).wait_recv()
                pass_on(i, 1, yn)
                copy(i, 5, blk(dg, c), sibling, 0).wait_recv()
                copy(i, 6, blk(dg, c), sibling, 1).wait_recv()
                pass_on(i, 2, dg)
            else:
                copy(i, 1, blk(xn, c), sibling).wait_recv()
                pass_on(i, 0, xn)
                copy(i, 3, blk(yn, c), sibling).wait_recv()
                pass_on(i, 1, yn)
                copy(i, 5, blk(dg, c), sibling).wait_recv()
                pass_on(i, 2, dg)
        for i in range(n):
            copy(i, 0, blk((x, y), 1 - c), sibling).wait_recv()
            for j, chip in enumerate((xn, yn, dg)):
                copy(i, 7 + j, blk(chip, 1 - c), sibling).wait_recv()
        for cp in first + passed + onward():
            cp.wait_send()
        for cp in mine:
            cp.wait()

    return start, relay, finish


def _gather_sems(n):
    return [pltpu.SemaphoreType.DMA((GATHER_SLOTS, n)), pltpu.SemaphoreType.DMA((GATHER_SLOTS, n)),
            pltpu.SemaphoreType.DMA((n,))]


class _Gather:
    def __init__(self, shards, as_dtypes=None):
        n = len(shards)
        dts = [s.dtype for s in shards] if as_dtypes is None else list(as_dtypes)
        self.cast = [jnp.dtype(d) != s.dtype for d, s in zip(dts, shards)]
        self.ins = list(shards)
        self.in_specs = [_VMEM if c else _HBM for c in self.cast]
        self.out_shape = [_sds((NDEV,) + s.shape, d) for s, d in zip(shards, dts)]
        self.out_specs = [_HBM] * n
        self.scratch = [pltpu.VMEM(s.shape, d) for s, d, c in zip(shards, dts, self.cast) if c] + _gather_sems(n)

    def ops(self, ins, outs, scr):
        ncast = sum(self.cast)
        staged = iter(scr[:ncast])
        srcs = [next(staged) if c else ref for c, ref in zip(self.cast, ins)]
        start, relay, finish = _gather_ops(srcs, outs, *scr[ncast:])

        def cast_and_start():
            for c, ref, src in zip(self.cast, ins, srcs):
                if c:
                    src[...] = ref[...].astype(src.dtype)
            start()

        return cast_and_start, relay, finish


class _Exchange:
    def __init__(self, qs):
        n = len(qs)
        self.ins, self.in_specs = list(qs), [_HBM] * n
        self.out_shape = [_sds(q.shape, q.dtype) for q in qs]
        self.out_specs = [_HBM] * n
        self.scratch = [pltpu.SemaphoreType.DMA((NCHIP_OTHER, n)), pltpu.SemaphoreType.DMA((NCHIP_OTHER, n))]

    def ops(self, ins, outs, scr):
        send_sems, recv_sems = scr
        n = len(ins)
        x, y, c = _place()
        chips = _other_chips(x, y)

        def copies():
            return [pltpu.make_async_remote_copy(
                src_ref=ins[i].at[j], dst_ref=outs[i].at[j], send_sem=send_sems.at[j, i],
                recv_sem=recv_sems.at[j, i], device_id=(*chips[j], c), device_id_type=MESH)
                for i in range(n) for j in range(NCHIP_OTHER)]

        def start():
            for cp in copies():
                cp.start()

        def finish():
            cps = copies()
            for cp in cps:
                cp.wait_recv()
            for cp in cps:
                cp.wait_send()

        return start, lambda: None, finish


class _ExchangeVia:
    def __init__(self, q):
        _, r, cd = q.shape
        half = (2, r // 2, cd)
        self.ins, self.in_specs = [q], [_HBM]
        self.out_shape, self.out_specs = [_sds((2, r, cd), q.dtype)], [_HBM]
        self.scratch = [pltpu.VMEM(half, q.dtype), pltpu.VMEM(half, q.dtype), pltpu.VMEM(half, q.dtype),
                        pltpu.SemaphoreType.DMA((6,)), pltpu.SemaphoreType.DMA((6,)), pltpu.SemaphoreType.DMA((2,))]

    def ops(self, ins, outs, scr):
        (q,), (land,) = ins, outs
        relayed, own, comb, send_sems, recv_sems, local_sems = scr
        h = q.shape[1] // 2
        x, y, c = _place()
        xn, yn, _ = _other_chips(x, y)
        h0, h1 = pl.ds(0, h), pl.ds(h, h)

        def remote(k, src, dst, chip):
            return pltpu.make_async_remote_copy(src_ref=src, dst_ref=dst, send_sem=send_sems.at[k],
                                                recv_sem=recv_sems.at[k], device_id=(*chip, c), device_id_type=MESH)

        def via():
            return [remote(2, q.at[2, h0], relayed.at[0], xn), remote(3, q.at[2, h1], relayed.at[1], yn)]

        def direct():
            return [remote(0, q.at[0, h0], land.at[0, h0], xn), remote(1, q.at[1, h1], land.at[1, h1], yn)]

        def second():
            return [remote(4, comb.at[0], land.at[1, h0], yn), remote(5, comb.at[1], land.at[0, h1], xn)]

        def mine():
            return [pltpu.make_async_copy(q.at[1, h0], own.at[0], local_sems.at[0]),
                    pltpu.make_async_copy(q.at[0, h1], own.at[1], local_sems.at[1])]

        def start():
            for cp in via() + direct() + mine():
                cp.start()

        def relay():
            arrived, loaded, onward = via(), mine(), second()
            for k in range(2):
                arrived[k].wait_recv()
                loaded[k].wait()
                comb[k] = (own[k].astype(F32) + relayed[k].astype(F32)).astype(comb.dtype)
                onward[k].start()

        def finish():
            landing = direct() + second()
            for cp in landing:
                cp.wait_recv()
            for cp in via() + landing:
                cp.wait_send()

        return start, relay, finish


class _SumGather:
    def __init__(self, accs, lands):
        n = len(accs)
        self.n = n
        self.ins, self.in_specs = list(accs) + list(lands), [_VMEM] * (2 * n)
        self.out_shape = [_sds((NDEV,) + a.shape, a.dtype) for a in accs]
        self.out_specs = [_HBM] * n
        self.scratch = [pltpu.VMEM(a.shape, a.dtype) for a in accs] + _gather_sems(n)

    def ops(self, ins, outs, scr):
        n = self.n
        accs, lands, mine = ins[:n], ins[n:], scr[:n]
        g_start, relay, finish = _gather_ops(mine, outs, *scr[n:])

        def start():
            for i in range(n):
                mine[i][...] = accs[i][...] + lands[i][0] + lands[i][1] + lands[i][2]
            g_start()

        return start, relay, finish


def _call(main, jobs, *, name, grid, ins, in_specs, out_shape, out_specs, scratch, relay_step=0):
    nsteps = grid[0] if grid else 1
    n_in, n_out, n_scr = len(ins), len(out_shape), len(scratch)

    def body(*refs):
        pos = [0]

        def take(k):
            r = refs[pos[0]:pos[0] + k]
            pos[0] += k
            return r

        m_in = take(n_in)
        j_in = [take(len(j.ins)) for j in jobs]
        m_out = take(n_out)
        j_out = [take(len(j.out_shape)) for j in jobs]
        m_scr = take(n_scr)
        j_scr = [take(len(j.scratch)) for j in jobs]
        ops = [j.ops(a, b, s) for j, a, b, s in zip(jobs, j_in, j_out, j_scr)]
        i = pl.program_id(0) if grid else 0
        if not grid:
            for o in ops:
                o[0]()
            main(i, m_in, m_out, m_scr)
            for o in ops:
                o[1]()
            for o in ops:
                o[2]()
            return

        if ops:
            @pl.when(i == 0)
            def _():
                for o in ops:
                    o[0]()

        main(i, m_in, m_out, m_scr)

        if ops:
            @pl.when(i == min(relay_step, nsteps - 1))
            def _():
                for o in ops:
                    o[1]()

            @pl.when(i == nsteps - 1)
            def _():
                for o in ops:
                    o[2]()

    extra = dict(dimension_semantics=("arbitrary",)) if grid else {}
    res = pl.pallas_call(
        body, name=name, grid=grid,
        in_specs=list(in_specs) + [s for j in jobs for s in j.in_specs],
        out_specs=list(out_specs) + [s for j in jobs for s in j.out_specs],
        out_shape=list(out_shape) + [s for j in jobs for s in j.out_shape],
        scratch_shapes=list(scratch) + [s for j in jobs for s in j.scratch],
        compiler_params=_params(**extra),
    )(*ins, *[a for j in jobs for a in j.ins])
    main_out, rest, job_out = res[:n_out], res[n_out:], []
    for j in jobs:
        k = len(j.out_shape)
        job_out.append(rest[:k])
        rest = rest[k:]
    return main_out, job_out


def _comm_only(jobs, name):
    _, job_out = _call(lambda i, a, b, s: None, jobs, name=name, grid=(), ins=[], in_specs=[], out_shape=[],
                       out_specs=[], scratch=[])
    return job_out


class _InChip:
    def __init__(self, ps):
        n = len(ps)
        self.n = n
        blk = [p.shape[1:] for p in ps]
        self.ins, self.in_specs = list(ps), [_HBM] * n
        self.out_shape = [_sds((NCHIP_OTHER,) + b, p.dtype) for b, p in zip(blk, ps)] + [_sds(b, F32) for b in blk]
        self.out_specs = [_VMEM] * (2 * n)
        self.scratch = ([pltpu.VMEM((4,) + b, p.dtype) for b, p in zip(blk, ps)] * 2
                        + [pltpu.SemaphoreType.DMA((4, n))] * 3)

    def ops(self, ins, outs, scr):
        n = self.n
        q_refs, acc_refs = outs[:n], outs[n:]
        mines, lands = scr[:n], scr[n:2 * n]
        send_sems, recv_sems, local_sems = scr[2 * n:]
        x, y, c = _place()
        sibling = (x, y, 1 - c)

        def copies():
            out = []
            for i in range(n):
                for pi in range(4):
                    loc = pltpu.make_async_copy(ins[i].at[2 * pi + c], mines[i].at[pi], local_sems.at[pi, i])
                    cp = pltpu.make_async_remote_copy(
                        src_ref=ins[i].at[2 * pi + (1 - c)], dst_ref=lands[i].at[pi],
                        send_sem=send_sems.at[pi, i], recv_sem=recv_sems.at[pi, i],
                        device_id=sibling, device_id_type=MESH)
                    out.append((loc, cp))
            return out

        def start():
            for loc, cp in copies():
                loc.start()
                cp.start()

        def finish():
            pairs = copies()
            for loc, cp in pairs:
                loc.wait()
                cp.wait_recv()
            for i in range(n):
                _chip_sums(mines[i], lands[i], q_refs[i], acc_refs[i], x, y)
            for _, cp in pairs:
                cp.wait_send()

        return start, lambda: None, finish


def _chip_sums(mine, land, q_ref, acc_ref, x, y):
    for j, (qx, qy) in enumerate(_other_chips(x, y)):
        qi = 2 * qx + qy
        q_ref[j] = (mine[qi].astype(F32) + land[qi].astype(F32)).astype(q_ref.dtype)
    mi = 2 * x + y
    acc_ref[...] = mine[mi].astype(F32) + land[mi].astype(F32)


def _direct_sum(v, buf, send_sems, recv_sems):
    x, y, c = _place()
    me = 4 * x + 2 * y + c
    buf[me] = v
    cps = []
    for k in range(1, NDEV):
        fx, fy, fc = (k >> 2) & 1, (k >> 1) & 1, k & 1
        peer = ((1 - x) if fx else x, (1 - y) if fy else y, (1 - c) if fc else c)
        cps.append((peer, pltpu.make_async_remote_copy(
            src_ref=buf.at[me], dst_ref=buf.at[me], send_sem=send_sems.at[k - 1], recv_sem=recv_sems.at[k - 1],
            device_id=peer, device_id_type=MESH)))
    for _, cp in cps:
        cp.start()
    for k, (peer, _) in enumerate(cps):
        theirs = 4 * peer[0] + 2 * peer[1] + peer[2]
        pltpu.make_async_remote_copy(
            src_ref=buf.at[theirs], dst_ref=buf.at[theirs], send_sem=send_sems.at[k], recv_sem=recv_sems.at[k],
            device_id=peer, device_id_type=MESH).wait_recv()
    acc = buf[0]
    for j in range(1, NDEV):
        acc = acc + buf[j]
    for _, cp in cps:
        cp.wait_send()
    return acc


def _direct_sum_scratch(shape, dtype):
    return [pltpu.VMEM((NDEV,) + tuple(shape), dtype), pltpu.SemaphoreType.DMA((NDEV - 1,)),
            pltpu.SemaphoreType.DMA((NDEV - 1,))]


def _fwd_a(x, nw, win8, lnw, lnb, ws, bst, jobs, *, tm, relay_step):
    s_len = x.shape[0]
    nt = s_len // tm
    nch = tm // CH

    def main(i, ins, outs, scr):
        x_ref, nw_ref, win_ref, lnw_ref, lnb_ref, ws_ref, bst_ref = ins
        z_ref, h_ref, y_ref = outs
        wc_scr, gv_scr = scr

        @pl.when(i == 0)
        def _():
            m = _causal_mask()
            for g in range(G):
                wc_scr[g] = jnp.where(m, ws_ref[g], 0.0).astype(BF16)

        x = x_ref[...]
        h = (x * _rms(x) * nw_ref[...]).astype(BF16)
        h_ref[...] = h
        for k in range(NDEV):
            z_ref[:, k * CA:(k + 1) * CA] = _dot(h, win_ref[k])

        ssum = jnp.zeros((tm, 1), F32)
        for g in range(G):
            gv = _gelu_t(z_ref[:, AW + g * GD:AW + (g + 1) * GD])[0]
            gv_scr[:, g * GD:(g + 1) * GD] = gv
            ssum = ssum + jnp.sum(gv, axis=-1, keepdims=True)
        mu = ssum * (1.0 / AW)
        vsum = jnp.zeros((tm, 1), F32)
        for g in range(G):
            dlt = gv_scr[:, g * GD:(g + 1) * GD] - mu
            vsum = vsum + jnp.sum(dlt * dlt, axis=-1, keepdims=True)
        rstd = lax.rsqrt(vsum * (1.0 / AW) + LN_EPS)

        for g in range(G):
            cs = slice(g * GD, (g + 1) * GD)
            v = (gv_scr[:, cs] - mu) * rstd * lnw_ref[:, cs] + lnb_ref[:, cs]
            vb = v.astype(BF16)
            u = _gelu_t(z_ref[:, cs])[0]
            zg = z_ref[:, 2 * AW + g * GD:2 * AW + (g + 1) * GD]
            sg = zg * _sigmoid(zg)
            for n in range(nch):
                rs = slice(n * CH, (n + 1) * CH)
                s = _dot(wc_scr[g], vb[rs, :]) + bst_ref[:, g:g + 1]
                y_ref[rs, cs] = (u[rs, :] * s * sg[rs, :]).astype(BF16)

    tile = lambda w: pl.BlockSpec((tm, w), lambda i: (i, 0))
    return _call(
        main, jobs, name="fwd_a", grid=(nt,), relay_step=relay_step,
        ins=[x, nw, win8, lnw, lnb, ws, bst], in_specs=[tile(D), _VMEM, _VMEM, _VMEM, _VMEM, _VMEM, _VMEM],
        out_shape=[_sds((s_len, 3 * AW), F32), _sds((s_len, D), BF16), _sds((s_len, AW), BF16)],
        out_specs=[tile(3 * AW), tile(D), tile(AW)],
        scratch=[pltpu.VMEM((G, CH, CH), BF16), pltpu.VMEM((tm, AW), F32)])


def _bwd_a(dx1, z, lnw, lnb, ws, bst, wout, jobs, *, tm, relay_step):
    s_len = dx1.shape[0]
    nt = s_len // tm
    nch = tm // CH

    def main(i, ins, outs, scr):
        dx1_ref, z_ref, lnw_ref, lnb_ref, ws_ref, bst_ref, wout_ref = ins
        dz_ref, glnw_ref, glnb_ref, gws_ref, gbst_ref = outs
        wc_scr, wct_scr, vh_scr, dgv_scr, dy_scr, dv_scr, gbs_acc, gwc_acc = scr

        @pl.when(i == 0)
        def _():
            m = _causal_mask()
            for g in range(G):
                wm = jnp.where(m, ws_ref[g], 0.0)
                wc_scr[g] = wm.astype(BF16)
                wct_scr[g] = wm.T.astype(BF16)
            glnw_ref[...] = jnp.zeros_like(glnw_ref)
            glnb_ref[...] = jnp.zeros_like(glnb_ref)
            gbs_acc[...] = jnp.zeros_like(gbs_acc)
            gwc_acc[...] = jnp.zeros_like(gwc_acc)

        dy_scr[...] = _dot_nt(dx1_ref[...], wout_ref[...])

        ssum = jnp.zeros((tm, 1), F32)
        for g in range(G):
            cs = slice(g * GD, (g + 1) * GD)
            zv = z_ref[:, AW + g * GD:AW + (g + 1) * GD]
            gv, t = _gelu_t(zv)
            vh_scr[:, cs] = gv
            dgv_scr[:, cs] = _dgelu(zv, t)
            ssum = ssum + jnp.sum(gv, axis=-1, keepdims=True)
        mu = ssum * (1.0 / AW)
        vsum = jnp.zeros((tm, 1), F32)
        for g in range(G):
            dlt = vh_scr[:, g * GD:(g + 1) * GD] - mu
            vsum = vsum + jnp.sum(dlt * dlt, axis=-1, keepdims=True)
        rstd = lax.rsqrt(vsum * (1.0 / AW) + LN_EPS)

        m1 = jnp.zeros((tm, 1), F32)
        m2 = jnp.zeros((tm, 1), F32)
        for g in range(G):
            cs = slice(g * GD, (g + 1) * GD)
            gs = slice(2 * AW + g * GD, 2 * AW + (g + 1) * GD)
            vhat = (vh_scr[:, cs] - mu) * rstd
            vh_scr[:, cs] = vhat
            vb = (vhat * lnw_ref[:, cs] + lnb_ref[:, cs]).astype(BF16)
            zu = z_ref[:, cs]
            u, tu = _gelu_t(zu)
            zg = z_ref[:, gs]
            sig = _sigmoid(zg)
            sg = zg * sig
            dy = dy_scr[:, cs]
            dsf = dy * u * sg
            dsb = dsf.astype(BF16)
            dvs = []
            for n in range(nch):
                rs = slice(n * CH, (n + 1) * CH)
                s = _dot(wc_scr[g], vb[rs, :]) + bst_ref[:, g:g + 1]
                dys = dy[rs, :] * s
                dz_ref[rs, cs] = (dys * sg[rs, :] * _dgelu(zu[rs, :], tu[rs, :])).astype(BF16)
                dz_ref[rs, gs] = (dys * u[rs, :] * (sig[rs, :] * (1.0 + zg[rs, :] * (1.0 - sig[rs, :])))).astype(BF16)
                gbs_acc[g] += dsf[rs, :]
                gwc_acc[g] += _dot_nt(dsb[rs, :], vb[rs, :])
                dvs.append(_dot(wct_scr[g], dsb[rs, :]))
            dv = jnp.concatenate(dvs, axis=0) if nch > 1 else dvs[0]
            glnw_ref[:, cs] += _rowsum(dv * vhat)
            glnb_ref[:, cs] += _rowsum(dv)
            dvh = dv * lnw_ref[:, cs]
            dv_scr[:, cs] = dvh
            m1 = m1 + jnp.sum(dvh, axis=-1, keepdims=True)
            m2 = m2 + jnp.sum(dvh * vhat, axis=-1, keepdims=True)
        m1 = m1 * (1.0 / AW)
        m2 = m2 * (1.0 / AW)
        for g in range(G):
            cs = slice(g * GD, (g + 1) * GD)
            dgv = rstd * (dv_scr[:, cs] - m1 - vh_scr[:, cs] * m2)
            dz_ref[:, AW + g * GD:AW + (g + 1) * GD] = (dgv * dgv_scr[:, cs]).astype(BF16)

        @pl.when(i == nt - 1)
        def _():
            m = _causal_mask()
            for g in range(G):
                gws_ref[g] = jnp.where(m, gwc_acc[g], 0.0)
                gbst_ref[:, g:g + 1] = jnp.sum(gbs_acc[g], axis=-1, keepdims=True)

    tile = lambda w: pl.BlockSpec((tm, w), lambda i: (i, 0))
    whole = lambda *s: pl.BlockSpec(s, lambda i: (0,) * len(s))
    big = lambda dt: pltpu.VMEM((tm, AW), dt)
    return _call(
        main, jobs, name="bwd_a", grid=(nt,), relay_step=relay_step,
        ins=[dx1, z, lnw, lnb, ws, bst, wout], in_specs=[tile(D), tile(3 * AW), _VMEM, _VMEM, _VMEM, _VMEM, _VMEM],
        out_shape=[_sds((s_len, 3 * AW), BF16), _sds((1, AW), F32), _sds((1, AW), F32), _sds((G, CH, CH), F32),
                   _sds((CH, G), F32)],
        out_specs=[tile(3 * AW), whole(1, AW), whole(1, AW), whole(G, CH, CH), whole(CH, G)],
        scratch=[pltpu.VMEM((G, CH, CH), BF16), pltpu.VMEM((G, CH, CH), BF16), big(F32), big(F32), big(F32), big(F32),
                 pltpu.VMEM((G, CH, GD), F32), pltpu.VMEM((G, CH, CH), F32)])


def _bwd_a_in(dz, dx1, x, nw, win8, jobs, *, tm, relay_step):
    s_len = x.shape[0]
    nt = s_len // tm

    def main(i, ins, outs, scr):
        dz_ref, dx1_ref, x_ref, nw_ref, win_ref = ins
        gx_ref, gnw_ref = outs

        @pl.when(i == 0)
        def _():
            gnw_ref[...] = jnp.zeros_like(gnw_ref)

        dh = jnp.zeros((tm, D), F32)
        for k in range(NDEV):
            dh = dh + _dot_nt(dz_ref[:, k * CA:(k + 1) * CA], win_ref[k])
        x = x_ref[...]
        r = _rms(x)
        gx_ref[...] = dx1_ref[...] + _rms_bwd(dh, x, r, nw_ref[...])
        gnw_ref[...] += _rowsum(dh * x * r)

        @pl.when(i == nt - 1)
        def _():
            gnw_ref[...] = _direct_sum(gnw_ref[...], *scr)

    tile = lambda w: pl.BlockSpec((tm, w), lambda i: (i, 0))
    return _call(
        main, jobs, name="bwd_a_in", grid=(nt,), relay_step=relay_step,
        ins=[dz, dx1, x, nw, win8], in_specs=[tile(3 * AW), tile(D), tile(D), _VMEM, _VMEM],
        out_shape=[_sds((s_len, D), F32), _sds((1, D), F32)],
        out_specs=[tile(D), pl.BlockSpec((1, D), lambda i: (0, 0))], scratch=_direct_sum_scratch((1, D), F32))


def _conv(p8_ref, cs, xb, xm1, xm2, xm3):
    xc = p8_ref[4:5, cs] + p8_ref[3:4, cs] * xb
    xc = xc + p8_ref[0:1, cs] * xm3
    xc = xc + p8_ref[1:2, cs] * xm2
    return xc + p8_ref[2:3, cs] * xm1


def _gates(p8_ref, gcat_ref, hh, xc):
    cs = slice(hh * HD, (hh + 1) * HD)
    pre = _dot(xc.astype(BF16), gcat_ref[hh])
    r = _sigmoid(pre[:, :HD] + p8_ref[5:6, cs])
    ig = _sigmoid(pre[:, HD:] + p8_ref[6:7, cs])
    sp = _softplus_neg(p8_ref[7:8, cs])
    la = (-RG_C) * r * sp
    a = jnp.exp(la)
    half_log = 0.5 * jnp.log(jnp.tanh(-la) * (1.0 + a * a))
    return r, ig, sp, a, jnp.exp(half_log), jnp.exp(-half_log)


def _scan_rows(a_ref, b_ref, out_ref, carry, tm, reverse):
    row = lax.broadcasted_iota(jnp.int32, (SUBLANES, BW), 0)
    ngrp = tm // SUBLANES

    def step(j, cr):
        jj = (ngrp - 1 - j) if reverse else j
        off = pl.multiple_of(jj * SUBLANES, SUBLANES)
        a = a_ref[pl.ds(off, SUBLANES), :]
        b = b_ref[pl.ds(off, SUBLANES), :]
        for sh in (1, 2, 4):
            if reverse:
                a_s = pltpu.roll(a, SUBLANES - sh, 0)
                b_s = pltpu.roll(b, SUBLANES - sh, 0)
                m = row < SUBLANES - sh
            else:
                a_s = pltpu.roll(a, sh, 0)
                b_s = pltpu.roll(b, sh, 0)
                m = row >= sh
            b = jnp.where(m, a * b_s + b, b)
            a = jnp.where(m, a * a_s, a)
        o = b + a * cr
        out_ref[pl.ds(off, SUBLANES), :] = o
        return o[0:1, :] if reverse else o[SUBLANES - 1:SUBLANES, :]

    return lax.fori_loop(0, ngrp, step, carry)


def _fwd_b(x, ya, wout_a, nw, win8, p8, gcat, jobs, *, tm, relay_step):
    s_len = x.shape[0]
    nt = s_len // tm

    def main(i, ins, outs, scr):
        x_ref, ya_ref, wouta_ref, nw_ref, win_ref, p8_ref, gcat_ref = ins
        x1_ref, zb_ref, hs_ref, h1_ref, yb_ref, xc_ref, a_ref, cc_ref, r_ref, ig_ref, m_ref = outs
        xbe_scr, b_scr, k_scr, carry_scr = scr

        @pl.when(i == 0)
        def _():
            xbe_scr[0:SUBLANES, :] = jnp.zeros((SUBLANES, BW), F32)
            carry_scr[...] = jnp.zeros_like(carry_scr)

        x1 = x_ref[...] + _dot(ya_ref[...], wouta_ref[...])
        x1_ref[...] = x1
        h = (x1 * _rms(x1) * nw_ref[...]).astype(BF16)
        h1_ref[...] = h
        for k in range(NDEV):
            zb_ref[:, k * CB:(k + 1) * CB] = _dot(h, win_ref[k])
        xbe_scr[SUBLANES:SUBLANES + tm, :] = zb_ref[:, :BW]
        for hh in range(BH):
            cs = slice(hh * HD, (hh + 1) * HD)
            xc = _conv(p8_ref, cs, xbe_scr[SUBLANES:SUBLANES + tm, cs], xbe_scr[7:7 + tm, cs],
                       xbe_scr[6:6 + tm, cs], xbe_scr[5:5 + tm, cs])
            r, ig, _, a, mult, rm = _gates(p8_ref, gcat_ref, hh, xc)
            ixc = ig * xc
            xc_ref[:, cs] = xc
            a_ref[:, cs] = a
            r_ref[:, cs] = r.astype(BF16)
            ig_ref[:, cs] = ig.astype(BF16)
            m_ref[:, cs] = mult.astype(BF16)
            b_scr[:, cs] = mult * ixc
            k_scr[:, cs] = ixc * (a * a * rm)
        xbe_scr[0:SUBLANES, :] = xbe_scr[tm:tm + SUBLANES, :]
        carry_scr[...] = _scan_rows(a_ref, b_scr, hs_ref, carry_scr[...], tm, False)
        for hh in range(BH):
            cs = slice(hh * HD, (hh + 1) * HD)
            gt = zb_ref[:, BW + hh * HD:BW + (hh + 1) * HD]
            hsv = hs_ref[:, cs]
            yb_ref[:, cs] = (hsv * (gt * _sigmoid(gt))).astype(BF16)
            cc_ref[:, cs] = (hsv - b_scr[:, cs]) - k_scr[:, cs]

    tile = lambda w: pl.BlockSpec((tm, w), lambda i: (i, 0))
    wide = lambda dt: _sds((s_len, BW), dt)
    return _call(
        main, jobs, name="fwd_b", grid=(nt,), relay_step=relay_step,
        ins=[x, ya, wout_a, nw, win8, p8, gcat], in_specs=[tile(D), tile(AW), _VMEM, _VMEM, _VMEM, _VMEM, _VMEM],
        out_shape=[_sds((s_len, D), F32), _sds((s_len, 2 * BW), F32), wide(F32), _sds((s_len, D), BF16), wide(BF16),
                   wide(F32), wide(F32), wide(F32), wide(BF16), wide(BF16), wide(BF16)],
        out_specs=[tile(D), tile(2 * BW), tile(BW), tile(D)] + [tile(BW)] * 7,
        scratch=[pltpu.VMEM((tm + SUBLANES, BW), F32), pltpu.VMEM((tm, BW), F32), pltpu.VMEM((tm, BW), F32),
                 pltpu.VMEM((1, BW), F32)])


def _head(x1, yb, wout, nfw, tgt, *, tm):
    s_len = x1.shape[0]

    def main(i, ins, outs, scr):
        x1_ref, yb_ref, wout_ref, nfw_ref, t_ref = ins
        dx2_ref, dx2b_ref, loss_ref, gnfw_ref = outs

        @pl.when(i == 0)
        def _():
            loss_ref[...] = jnp.zeros_like(loss_ref)
            gnfw_ref[...] = jnp.zeros_like(gnfw_ref)

        x2 = x1_ref[...] + _dot(yb_ref[...], wout_ref[...])
        rf = _rms(x2)
        xn = x2 * rf
        e = xn * nfw_ref[...] - t_ref[...]
        loss_ref[...] += (0.5 / D) * jnp.sum(jnp.sum(e * e, axis=-1, keepdims=True), axis=0, keepdims=True)
        dyf = e * (1.0 / D)
        gnfw_ref[...] += _rowsum(dyf * xn)
        dx2 = _rms_bwd(dyf, x2, rf, nfw_ref[...])
        dx2_ref[...] = dx2
        dx2b_ref[...] = dx2.astype(BF16)

    tile = lambda w: pl.BlockSpec((tm, w), lambda i: (i, 0))
    whole = lambda *s: pl.BlockSpec(s, lambda i: (0,) * len(s))
    (dx2, dx2b, loss, gnfw), _ = _call(
        main, [], name="head", grid=(s_len // tm,),
        ins=[x1, yb, wout, nfw, tgt], in_specs=[tile(D), tile(BW), _VMEM, _VMEM, tile(D)],
        out_shape=[_sds((s_len, D), F32), _sds((s_len, D), BF16), _sds((1, 1), F32), _sds((1, D), F32)],
        out_specs=[tile(D), tile(D), whole(1, 1), whole(1, D)], scratch=[])
    return dx2, dx2b, loss, gnfw


def _bwd_b(dx2, zb, hs, x1, saved, nw, win8, p8, gcat, wout, *, tm):
    s_len = x1.shape[0]
    nt = s_len // tm

    def main(i, ins, outs, scr):
        (dx2_ref, zb_ref, hs_ref, x1_ref, xc_ref, a_ref, cc_ref, r_ref, ig_ref, m_ref,
         nw_ref, win_ref, p8_ref, gcat_ref, wout_ref) = ins
        dx1_ref, dx1b_ref, dzb_ref, gp8_ref, gga_ref, ggx_ref, gnw_ref = outs
        ae_scr, an_scr, dhd_scr, dh_scr, dy_scr, dxce_scr, carry_scr, afirst_scr = scr

        @pl.when(i == 0)
        def _():
            gp8_ref[...] = jnp.zeros_like(gp8_ref)
            gga_ref[...] = jnp.zeros_like(gga_ref)
            ggx_ref[...] = jnp.zeros_like(ggx_ref)
            gnw_ref[...] = jnp.zeros_like(gnw_ref)
            dxce_scr[tm:tm + SUBLANES, :] = jnp.zeros((SUBLANES, BW), F32)
            carry_scr[...] = jnp.zeros_like(carry_scr)
            afirst_scr[...] = jnp.zeros_like(afirst_scr)

        dx2 = dx2_ref[...]
        dy_scr[...] = _dot_nt(dx2.astype(BF16), wout_ref[...])
        for hh in range(BH):
            cs = slice(hh * HD, (hh + 1) * HD)
            gs = slice(BW + hh * HD, BW + (hh + 1) * HD)
            gt = zb_ref[:, gs]
            sig = _sigmoid(gt)
            dy = dy_scr[:, cs]
            dhd_scr[:, cs] = dy * (gt * sig)
            dzb_ref[:, gs] = (dy * hs_ref[:, cs] * (sig * (1.0 + gt * (1.0 - sig)))).astype(BF16)

        ae_scr[0:tm, :] = a_ref[...]
        ae_scr[tm:tm + SUBLANES, :] = jnp.broadcast_to(afirst_scr[...], (SUBLANES, BW))
        an_scr[...] = ae_scr[1:1 + tm, :]
        afirst_scr[...] = ae_scr[0:1, :]
        carry_scr[...] = _scan_rows(an_scr, dhd_scr, dh_scr, carry_scr[...], tm, True)

        for hh in range(BH):
            cs = slice(hh * HD, (hh + 1) * HD)
            dh = dh_scr[:, cs]
            mult = m_ref[:, cs].astype(F32)
            ig = ig_ref[:, cs].astype(F32)
            r = r_ref[:, cs].astype(F32)
            xc = xc_ref[:, cs]
            lam = p8_ref[7:8, cs]
            sp = _softplus_neg(lam)
            dla = dh * cc_ref[:, cs]
            gp8_ref[7:8, cs] += _rowsum(dla * ((-RG_C) * r)) * (-_sigmoid(-lam))
            dpr = dla * ((-RG_C) * sp) * (r * (1.0 - r))
            dpi = dh * mult * xc * (ig * (1.0 - ig))
            gp8_ref[5:6, cs] += _rowsum(dpr)
            gp8_ref[6:7, cs] += _rowsum(dpi)
            dcat = jnp.concatenate([dpr, dpi], axis=1).astype(BF16)
            dxc = dh * mult * ig + _dot_nt(dcat, gcat_ref[hh])
            gg = _dot(xc.T.astype(BF16), dcat)
            gga_ref[hh] += gg[:, :HD]
            ggx_ref[hh] += gg[:, HD:]
            dxce_scr[0:tm, cs] = dxc
            gp8_ref[4:5, cs] += _rowsum(dxc)
        for hh in range(BH):
            cs = slice(hh * HD, (hh + 1) * HD)
            xb = zb_ref[:, cs]
            d0, d1 = dxce_scr[0:tm, cs], dxce_scr[1:1 + tm, cs]
            d2, d3 = dxce_scr[2:2 + tm, cs], dxce_scr[3:3 + tm, cs]
            dzb_ref[:, cs] = (p8_ref[3:4, cs] * d0 + p8_ref[2:3, cs] * d1 + p8_ref[1:2, cs] * d2
                              + p8_ref[0:1, cs] * d3).astype(BF16)
            gp8_ref[3:4, cs] += _rowsum(d0 * xb)
            gp8_ref[2:3, cs] += _rowsum(d1 * xb)
            gp8_ref[1:2, cs] += _rowsum(d2 * xb)
            gp8_ref[0:1, cs] += _rowsum(d3 * xb)
        dxce_scr[tm:tm + SUBLANES, :] = dxce_scr[0:SUBLANES, :]

        dh1 = jnp.zeros((tm, D), F32)
        for k in range(NDEV):
            dh1 = dh1 + _dot_nt(dzb_ref[:, k * CB:(k + 1) * CB], win_ref[k])
        x1 = x1_ref[...]
        r1 = _rms(x1)
        dx1 = dx2 + _rms_bwd(dh1, x1, r1, nw_ref[...])
        dx1_ref[...] = dx1
        dx1b_ref[...] = dx1.astype(BF16)
        gnw_ref[...] += _rowsum(dh1 * x1 * r1)

    tile = lambda w: pl.BlockSpec((tm, w), lambda i: (nt - 1 - i, 0))
    whole = lambda *s: pl.BlockSpec(s, lambda i: (0,) * len(s))
    full = lambda: pltpu.VMEM((tm, BW), F32)
    ext = lambda: pltpu.VMEM((tm + SUBLANES, BW), F32)
    out, _ = _call(
        main, [], name="bwd_b", grid=(nt,),
        ins=[dx2, zb, hs, x1, *saved, nw, win8, p8, gcat, wout],
        in_specs=[tile(D), tile(2 * BW), tile(BW), tile(D)] + [tile(BW)] * 6 + [_VMEM] * 5,
        out_shape=[_sds((s_len, D), F32), _sds((s_len, D), BF16), _sds((s_len, 2 * BW), BF16), _sds((SUBLANES, BW), F32),
                   _sds((BH, HD, HD), F32), _sds((BH, HD, HD), F32), _sds((1, D), F32)],
        out_specs=[tile(D), tile(D), tile(2 * BW), whole(SUBLANES, BW), whole(BH, HD, HD), whole(BH, HD, HD),
                   whole(1, D)],
        scratch=[ext(), full(), full(), full(), full(), ext(), pltpu.VMEM((1, BW), F32), pltpu.VMEM((1, BW), F32)])
    return out


def _transpose_into(dst_ref, src_ref, rows):
    s_len = src_ref.shape[0]
    for r0 in range(0, s_len, rows):
        dst_ref[:, r0:r0 + rows] = src_ref[r0:r0 + rows, :].astype(F32).T.astype(BF16)


def _wgrad(a, b, jobs, *, by_rows, per, name, relay_step=0):
    s_len, m = a.shape
    n = b.shape[1]
    r, cd = (m // NDEV, n) if by_rows else (m, n // NDEV)
    nsteps = NDEV // per
    at_rows = per * r if by_rows else m

    def main(i, ins, outs, scr):
        a_ref, b_ref = ins
        q_ref, acc_ref = outs
        at_scr, stage, mine, land, send_sems, recv_sems = scr
        x, y, c = _place()

        def to_sibling(pi):
            return pltpu.make_async_remote_copy(
                src_ref=stage.at[pi & 1], dst_ref=land.at[pi], send_sem=send_sems.at[pi], recv_sem=recv_sems.at[pi],
                device_id=(x, y, 1 - c), device_id_type=MESH)

        if by_rows:
            _transpose_into(at_scr, a_ref, TRANSPOSE_ROWS)
        else:
            @pl.when(i == 0)
            def _():
                _transpose_into(at_scr, a_ref, TRANSPOSE_ROWS)

        res = _dot(at_scr[...], b_ref[...]).astype(BF16)
        for k in range(per):
            blk = per * i + k
            pi, pc = blk >> 1, blk & 1
            val = res[k * r:(k + 1) * r, :] if by_rows else res

            @pl.when(pc != c)
            def _():
                @pl.when(pi >= 2)
                def _():
                    to_sibling(pi - 2).wait_send()

                stage[pi & 1] = val
                to_sibling(pi).start()

            @pl.when(pc == c)
            def _():
                mine[pi] = val

        @pl.when(i == nsteps - 1)
        def _():
            for p in range(4):
                to_sibling(p).wait_recv()
            to_sibling(2).wait_send()
            to_sibling(3).wait_send()
            _chip_sums(mine, land, q_ref, acc_ref, x, y)

    if by_rows:
        in_specs = [pl.BlockSpec((s_len, at_rows), lambda j: (0, j)), _VMEM]
    else:
        in_specs = [_VMEM, pl.BlockSpec((s_len, cd), lambda j: (0, j))]
    blk_vmem = lambda k: pltpu.VMEM((k, r, cd), BF16)
    (q, acc), job_out = _call(
        main, jobs, name=name, grid=(nsteps,), relay_step=relay_step, ins=[a, b], in_specs=in_specs,
        out_shape=[_sds((NCHIP_OTHER, r, cd), BF16), _sds((r, cd), F32)],
        out_specs=[pl.BlockSpec((NCHIP_OTHER, r, cd), lambda j: (0, 0, 0)), pl.BlockSpec((r, cd), lambda j: (0, 0))],
        scratch=[pltpu.VMEM((at_rows, s_len), BF16), blk_vmem(2), blk_vmem(4), blk_vmem(4),
                 pltpu.SemaphoreType.DMA((4,)), pltpu.SemaphoreType.DMA((4,))])
    return q, acc, job_out


def _wgrad_cols_early(a, b, jobs, *, name, relay_step=0):
    s_len, m = a.shape
    r, cd = m, b.shape[1] // NDEV
    h = r // 2

    def chip_at(pos, base):
        return base ^ (3 - pos)

    def main(i, ins, outs, scr):
        a_ref, b_ref = ins
        q_ref, acc_ref, rel_ref = outs
        at_scr, stage, mine, land, q2_scr, send_sems, recv_sems, via_send, via_recv = scr
        x, y, c = _place()
        base = 2 * x + y
        xn, yn, _ = _other_chips(x, y)
        pos, pc = i >> 1, i & 1
        pi = chip_at(pos, base)

        def to_sibling(chip, slot):
            return pltpu.make_async_remote_copy(
                src_ref=stage.at[slot], dst_ref=land.at[chip], send_sem=send_sems.at[chip],
                recv_sem=recv_sems.at[chip], device_id=(x, y, 1 - c), device_id_type=MESH)

        def via(k):
            return pltpu.make_async_remote_copy(
                src_ref=q2_scr.at[pl.ds(k * h, h)], dst_ref=rel_ref.at[k], send_sem=via_send.at[k],
                recv_sem=via_recv.at[k], device_id=(*(xn, yn)[k], c), device_id_type=MESH)

        @pl.when(i == 0)
        def _():
            _transpose_into(at_scr, a_ref, TRANSPOSE_ROWS)

        res = _dot(at_scr[...], b_ref[...]).astype(BF16)

        @pl.when(pc != c)
        def _():
            @pl.when(pos >= 2)
            def _():
                to_sibling(chip_at(pos - 2, base), pos & 1).wait_send()

            stage[pos & 1] = res
            to_sibling(pi, pos & 1).start()

        @pl.when(pc == c)
        def _():
            mine[pi] = res

        @pl.when(i == 1)
        def _():
            dg = chip_at(0, base)
            to_sibling(dg, 0).wait_recv()
            q2 = (mine[dg].astype(F32) + land[dg].astype(F32)).astype(BF16)
            q2_scr[...] = q2
            q_ref[2] = q2
            via(0).start()
            via(1).start()

        @pl.when(i == NDEV - 1)
        def _():
            for pos_ in (1, 2, 3):
                to_sibling(chip_at(pos_, base), 0).wait_recv()
            to_sibling(chip_at(2, base), 0).wait_send()
            to_sibling(chip_at(3, base), 1).wait_send()
            for k in range(2):
                via(k).wait_recv()
            for k in range(2):
                via(k).wait_send()
            for j, chip in enumerate((base ^ 2, base ^ 1)):
                q_ref[j] = (mine[chip].astype(F32) + land[chip].astype(F32)).astype(BF16)
            acc_ref[...] = mine[base].astype(F32) + land[base].astype(F32)

    def b_block(j):
        base = 2 * lax.axis_index("x") + lax.axis_index("y")
        return (0, 2 * chip_at(j >> 1, base) + (j & 1))

    blk_vmem = lambda k: pltpu.VMEM((k, r, cd), BF16)
    (q, acc, rel), job_out = _call(
        main, jobs, name=name, grid=(NDEV,), relay_step=relay_step, ins=[a, b],
        in_specs=[_VMEM, pl.BlockSpec((s_len, cd), b_block)],
        out_shape=[_sds((NCHIP_OTHER, r, cd), BF16), _sds((r, cd), F32), _sds((2, h, cd), BF16)],
        out_specs=[pl.BlockSpec((NCHIP_OTHER, r, cd), lambda j: (0, 0, 0)), pl.BlockSpec((r, cd), lambda j: (0, 0)), _HBM],
        scratch=[pltpu.VMEM((m, s_len), BF16), blk_vmem(2), blk_vmem(4), blk_vmem(4), pltpu.VMEM((r, cd), BF16),
                 pltpu.SemaphoreType.DMA((4,)), pltpu.SemaphoreType.DMA((4,)), pltpu.SemaphoreType.DMA((2,)),
                 pltpu.SemaphoreType.DMA((2,))])
    return q, acc, rel, job_out


class _ExchangeRest:
    def __init__(self, q, relayed):
        _, r, cd = q.shape
        half = (2, r // 2, cd)
        self.ins, self.in_specs = [q, relayed], [_HBM, _HBM]
        self.out_shape, self.out_specs = [_sds((2, r, cd), q.dtype)], [_HBM]
        self.scratch = [pltpu.VMEM(half, q.dtype), pltpu.VMEM(half, q.dtype), pltpu.VMEM(half, q.dtype),
                        pltpu.SemaphoreType.DMA((4,)), pltpu.SemaphoreType.DMA((4,)), pltpu.SemaphoreType.DMA((4,))]

    def ops(self, ins, outs, scr):
        (q, rel_in), (land,) = ins, outs
        own, rel, comb, send_sems, recv_sems, local_sems = scr
        h = q.shape[1] // 2
        x, y, c = _place()
        xn, yn, _ = _other_chips(x, y)
        h0, h1 = pl.ds(0, h), pl.ds(h, h)

        def remote(k, src, dst, chip):
            return pltpu.make_async_remote_copy(src_ref=src, dst_ref=dst, send_sem=send_sems.at[k],
                                                recv_sem=recv_sems.at[k], device_id=(*chip, c), device_id_type=MESH)

        def sends():
            return [remote(0, q.at[0, h0], land.at[0, h0], xn), remote(1, q.at[1, h1], land.at[1, h1], yn),
                    remote(2, comb.at[0], land.at[1, h0], yn), remote(3, comb.at[1], land.at[0, h1], xn)]

        def loads():
            return [pltpu.make_async_copy(q.at[1, h0], own.at[0], local_sems.at[0]),
                    pltpu.make_async_copy(q.at[0, h1], own.at[1], local_sems.at[1]),
                    pltpu.make_async_copy(rel_in.at[0], rel.at[0], local_sems.at[2]),
                    pltpu.make_async_copy(rel_in.at[1], rel.at[1], local_sems.at[3])]

        def start():
            cps, lds = sends(), loads()
            for ld in lds:
                ld.start()
            cps[0].start()
            cps[1].start()
            for ld in lds:
                ld.wait()
            for k in range(2):
                comb[k] = (own[k].astype(F32) + rel[k].astype(F32)).astype(comb.dtype)
            cps[2].start()
            cps[3].start()

        def finish():
            cps = sends()
            for cp in cps:
                cp.wait_recv()
            for cp in cps:
                cp.wait_send()

        return start, lambda: None, finish


def _adam_math(w, g, m, v):
    m = B1 * m + (1.0 - B1) * g
    v = B2 * v + (1.0 - B2) * (g * g)
    m_hat = m / (1.0 - B1 ** STEP)
    v_hat = v / (1.0 - B2 ** STEP)
    delta = (-LR) * (m_hat / (jnp.sqrt(v_hat) + ADAM_EPS) + WD * w)
    return delta, m, v


def _adam_big(w, acc, land, m, v, name):
    r, cd = w.shape
    rb = ADAM_ROWS if r % ADAM_ROWS == 0 else r
    nland = land.shape[0]

    def body(w_ref, acc_ref, land_ref, m_ref, v_ref, g_ref, d_ref, mo_ref, vo_ref):
        g = acc_ref[...]
        for j in range(nland):
            g = g + land_ref[j].astype(F32)
        g_ref[...] = g
        d_ref[...], mo_ref[...], vo_ref[...] = _adam_math(w_ref[...], g, m_ref[...], v_ref[...])

    blk = pl.BlockSpec((rb, cd), lambda i: (i, 0))
    blk3 = pl.BlockSpec((nland, rb, cd), lambda i: (0, i, 0))
    return pl.pallas_call(
        body, name=name, grid=(r // rb,), in_specs=[blk, blk, blk3, blk, blk], out_specs=[blk] * 4,
        out_shape=[_sds((r, cd), F32)] * 4,
        compiler_params=_params(dimension_semantics=("arbitrary",)),
    )(w, acc, land, m, v)


def _adam_small(groups):
    n = len(groups)

    def body(*refs):
        ins, outs = refs[:4 * n], refs[4 * n:]
        for k in range(n):
            w_ref, g_ref, m_ref, v_ref = ins[4 * k:4 * k + 4]
            d, mo, vo = _adam_math(w_ref[...], g_ref[...], m_ref[...], v_ref[...])
            outs[3 * k][...] = d
            outs[3 * k + 1][...] = mo
            outs[3 * k + 2][...] = vo

    flat = [a for grp in groups for a in grp]
    shapes = [_sds(grp[0].shape, F32) for grp in groups for _ in range(3)]
    res = pl.pallas_call(
        body, name="adam_small", in_specs=[_VMEM] * (4 * n), out_specs=[_VMEM] * (3 * n), out_shape=shapes,
        compiler_params=_params(),
    )(*flat)
    return [tuple(res[3 * k:3 * k + 3]) for k in range(n)]


TM_FWD_A = 256
RELAY_STEP_FWD_A = 4
RELAY_STEP_FWD_B = 2
TM_BWD_A = 256
RELAY_STEP_BWD_A = 3
TM_BWD_A_IN = 256
RELAY_STEP_BWD_A_IN = 4
RELAY_STEP_WGRAD_A_IN = 2
TM_FWD_B = 256
TM_HEAD = 512
TM_BWD_B = 128


def _pack(parts, rows):
    flat = jnp.concatenate([p.reshape(-1) for p in parts])
    return jnp.pad(flat, (0, NDEV * rows * LANES - flat.shape[0])).reshape(NDEV, rows, LANES)


def _unpack(packed, shapes):
    flat, out, off = packed.reshape(-1), [], 0
    for s in shapes:
        size = 1
        for d in s:
            size *= d
        out.append(flat[off:off + size].reshape(s))
        off += size
    return out


def kernel(x, norm_w, a_w_in, a_ln_w, a_ln_b, a_w_s, a_b_s, a_w_out, b_w_in, b_conv_w, b_conv_b, b_gate_a_w, b_gate_a_b, b_gate_x_w, b_gate_x_b, b_lambda, b_w_out, norm_f_w, loss_target, m_norm_w, m_a_w_in, m_a_ln_w, m_a_ln_b, m_a_w_s, m_a_b_s, m_a_w_out, m_b_w_in, m_b_conv_w, m_b_conv_b, m_b_gate_a_w, m_b_gate_a_b, m_b_gate_x_w, m_b_gate_x_b, m_b_lambda, m_b_w_out, m_norm_f_w, v_norm_w, v_a_w_in, v_a_ln_w, v_a_ln_b, v_a_w_s, v_a_b_s, v_a_w_out, v_b_w_in, v_b_conv_w, v_b_conv_b, v_b_gate_a_w, v_b_gate_a_b, v_b_gate_x_w, v_b_gate_x_b, v_b_lambda, v_b_w_out, v_norm_f_w):
    me = 4 * lax.axis_index("x") + 2 * lax.axis_index("y") + lax.axis_index("c")
    xs, tgt = x[0], loss_target[0]
    nw0, nw1, nfw = norm_w[0:1], norm_w[1:2], norm_f_w.reshape(1, D)
    w_s, bst = a_w_s[0], a_b_s[0].T
    gcat = jnp.concatenate([b_gate_a_w[0], b_gate_x_w[0]], axis=-1).astype(BF16)

    p8_shard = jnp.concatenate([b_conv_w[0], b_conv_b, b_gate_a_b, b_gate_x_b, b_lambda], axis=0)
    ((win_a8, p8_all),) = _comm_only([_Gather([a_w_in[0], p8_shard], [BF16, F32])], "gather_first")
    p8 = jnp.transpose(p8_all, (1, 0, 2)).reshape(SUBLANES, BW)

    (z, h0, ya), ((wout_a8, win_b8),) = _fwd_a(
        xs, nw0, win_a8, a_ln_w, a_ln_b, w_s, bst, [_Gather([a_w_out[0], b_w_in[0]], [BF16, BF16])],
        tm=TM_FWD_A, relay_step=RELAY_STEP_FWD_A)
    wout_a = wout_a8.reshape(AW, D)
    (x1, zb, hs, h1, yb, *saved_b), ((wout_b8,),) = _fwd_b(
        xs, ya, wout_a, nw1, win_b8, p8, gcat, [_Gather([b_w_out[0]], [BF16])],
        tm=TM_FWD_B, relay_step=RELAY_STEP_FWD_B)
    wout_b = wout_b8.reshape(BW, D)
    dx2, dx2b, loss, g_nfw = _head(x1, yb, wout_b, nfw, tgt, tm=TM_HEAD)

    dx1, dx1b, dzb, g_p8, g_ga, g_gx, g_nw1 = _bwd_b(dx2, zb, hs, x1, saved_b, nw1, win_b8, p8, gcat, wout_b,
                                                     tm=TM_BWD_B)
    q_wout_b, acc_wout_b, _ = _wgrad(yb, dx2b, [], by_rows=True, per=2, name="wgrad_b_out")
    shapes_b = [(1, D), (1, D), (SUBLANES, BW), (1, 1)]
    pack_b = _pack([g_nfw, g_nw1, g_p8, loss], 16)
    small_b = _InChip([g_ga.reshape(NDEV, -1, HD), g_gx.reshape(NDEV, -1, HD), pack_b])
    q_win_b, acc_win_b, (sm_b, (l_wout_b,)) = _wgrad(h1, dzb, [small_b, _Exchange([q_wout_b])], by_rows=False, per=1,
                                                      name="wgrad_b_in")
    qs_b, accs_b = sm_b[:3], sm_b[3:]

    (dz, g_lnw, g_lnb, g_ws, g_bst), (lands_b, (l_win_b,)) = _bwd_a(
        dx1b, z, a_ln_w, a_ln_b, w_s, bst, wout_a, [_Exchange(qs_b), _ExchangeVia(q_win_b)],
        tm=TM_BWD_A, relay_step=RELAY_STEP_BWD_A)
    shapes_a = [(1, AW), (1, AW), (CH, G)]
    pack_a = _pack([g_lnw, g_lnb, g_bst], 8)
    q_wout_a, acc_wout_a, (red_b, sm_a) = _wgrad(
        ya, dx1b, [_SumGather(accs_b, lands_b), _InChip([g_ws, pack_a])], by_rows=True, per=2,
        name="wgrad_a_out", relay_step=1)
    qs_a, accs_a = sm_a[:2], sm_a[2:]
    q_win_a, acc_win_a, rel_a, (lands_a, (l_wout_a,)) = _wgrad_cols_early(
        h0, dz, [_Exchange(qs_a), _ExchangeVia(q_wout_a)], name="wgrad_a_in", relay_step=RELAY_STEP_WGRAD_A_IN)
    (gx, g_nw0), (red_a, (l_win_a,)) = _bwd_a_in(
        dz, dx1, xs, nw0, win_a8, [_SumGather(accs_a, lands_a), _ExchangeRest(q_win_a, rel_a)],
        tm=TM_BWD_A_IN, relay_step=RELAY_STEP_BWD_A_IN)

    r_ga, r_gx, r_pack_b = red_b
    r_nfw, r_nw1, r_p8, loss = _unpack(r_pack_b, shapes_b)
    r_ws, r_pack_a = red_a
    r_lnw, r_lnb, r_bst = _unpack(r_pack_a, shapes_a)
    g_p8 = lax.dynamic_slice_in_dim(r_p8, me * (BW // NDEV), BW // NDEV, axis=1)
    loss = loss[0, 0]

    weights = dict(norm_w=norm_w, a_w_in=a_w_in, a_ln_w=a_ln_w, a_ln_b=a_ln_b, a_w_s=a_w_s, a_b_s=a_b_s, a_w_out=a_w_out,
                   b_w_in=b_w_in, b_conv_w=b_conv_w, b_conv_b=b_conv_b, b_gate_a_w=b_gate_a_w, b_gate_a_b=b_gate_a_b,
                   b_gate_x_w=b_gate_x_w, b_gate_x_b=b_gate_x_b, b_lambda=b_lambda, b_w_out=b_w_out, norm_f_w=norm_f_w)
    mom1 = dict(norm_w=m_norm_w, a_w_in=m_a_w_in, a_ln_w=m_a_ln_w, a_ln_b=m_a_ln_b, a_w_s=m_a_w_s, a_b_s=m_a_b_s,
                a_w_out=m_a_w_out, b_w_in=m_b_w_in, b_conv_w=m_b_conv_w, b_conv_b=m_b_conv_b, b_gate_a_w=m_b_gate_a_w,
                b_gate_a_b=m_b_gate_a_b, b_gate_x_w=m_b_gate_x_w, b_gate_x_b=m_b_gate_x_b, b_lambda=m_b_lambda,
                b_w_out=m_b_w_out, norm_f_w=m_norm_f_w)
    mom2 = dict(norm_w=v_norm_w, a_w_in=v_a_w_in, a_ln_w=v_a_ln_w, a_ln_b=v_a_ln_b, a_w_s=v_a_w_s, a_b_s=v_a_b_s,
                a_w_out=v_a_w_out, b_w_in=v_b_w_in, b_conv_w=v_b_conv_w, b_conv_b=v_b_conv_b, b_gate_a_w=v_b_gate_a_w,
                b_gate_a_b=v_b_gate_a_b, b_gate_x_w=v_b_gate_x_w, b_gate_x_b=v_b_gate_x_b, b_lambda=v_b_lambda,
                b_w_out=v_b_w_out, norm_f_w=v_norm_f_w)
    names = list(weights)

    def as2d(a):
        return a.reshape(-1, a.shape[-1])

    upd, grads = {}, {}
    for k, acc, land in (("a_w_in", acc_win_a, l_win_a), ("a_w_out", acc_wout_a, l_wout_a),
                         ("b_w_in", acc_win_b, l_win_b), ("b_w_out", acc_wout_b, l_wout_b)):
        g, d, mo, vo = _adam_big(as2d(weights[k]), acc, land, as2d(mom1[k]), as2d(mom2[k]), "adam_" + k)
        grads[k] = g[None]
        upd[k] = (d, mo, vo)
    grads.update(
        norm_w=jnp.concatenate([g_nw0, r_nw1], axis=0), a_ln_w=r_lnw, a_ln_b=r_lnb,
        a_w_s=r_ws.reshape(1, G, CH, CH), a_b_s=r_bst.T[None],
        b_conv_w=g_p8[None, 0:4], b_conv_b=g_p8[4:5], b_gate_a_w=r_ga.reshape(1, BH, HD, HD), b_gate_a_b=g_p8[5:6],
        b_gate_x_w=r_gx.reshape(1, BH, HD, HD), b_gate_x_b=g_p8[6:7], b_lambda=g_p8[7:8], norm_f_w=r_nfw.reshape(D))
    small_names = [k for k in names if k not in upd]
    res = _adam_small([(as2d(weights[k]), as2d(grads[k]), as2d(mom1[k]), as2d(mom2[k])) for k in small_names])
    for k, r3 in zip(small_names, res):
        upd[k] = r3
    deltas = [upd[k][0].reshape(weights[k].shape) for k in names]
    new_m = [upd[k][1].reshape(weights[k].shape) for k in names]
    new_v = [upd[k][2].reshape(weights[k].shape) for k in names]
    return (loss, gx[None], *[grads[k] for k in names], *deltas, *new_m, *new_v)
```

```python
import jax
import jax.numpy as jnp
from jax import lax
from jax.experimental import pallas as pl
from jax.experimental.pallas import tpu as pltpu

F32 = jnp.float32
BF16 = jnp.bfloat16
MESH = pl.DeviceIdType.MESH

NDEV = 8
NCHIP_OTHER = 3
D = 1024
AW = 2048
G = 8
GD = AW // G
CH = 128
BW = 1536
BH = 12
HD = BW // BH
CA = 3 * AW // NDEV
CB = 2 * BW // NDEV
RMS_EPS = 1e-6
LN_EPS = 1e-5
RG_C = 8.0
LR, B1, B2, ADAM_EPS, WD, STEP = 0.001, 0.9, 0.999, 1e-08, 0.01, 10
V7X_VMEM_BYTES = 64 * 1024 * 1024
VMEM_LIMIT = V7X_VMEM_BYTES - 8 * 1024 * 1024
SUBLANES = 8
LANES = 128
BF16_ROWS = 16
TRANSPOSE_ROWS = 256
ADAM_ROWS = 512
GELU_C = 0.7978845608028654
GELU_K = 0.044715

_VMEM = pl.BlockSpec(memory_space=pltpu.VMEM)
_HBM = pl.BlockSpec(memory_space=pltpu.HBM)


def _sds(shape, dtype):
    return jax.ShapeDtypeStruct(tuple(shape), dtype)


def _params(**kw):
    return pltpu.CompilerParams(vmem_limit_bytes=VMEM_LIMIT, **kw)


def _gelu_t(z):
    t = jnp.tanh(GELU_C * (z + GELU_K * (z * z * z)))
    return 0.5 * z * (1.0 + t), t


def _dgelu(z, t):
    return 0.5 * (1.0 + t) + 0.5 * z * (1.0 - t * t) * (GELU_C * (1.0 + 3.0 * GELU_K * z * z))


def _sigmoid(v):
    return 0.5 * jnp.tanh(0.5 * v) + 0.5


def _softplus_neg(lam):
    return jnp.maximum(-lam, 0.0) + jnp.log1p(jnp.exp(-jnp.abs(lam)))


def _dot(a, b):
    return jnp.dot(a, b, preferred_element_type=F32)


def _dot_nt(a, b):
    return lax.dot_general(a, b, (((1,), (1,)), ((), ())), preferred_element_type=F32)


def _rowsum(v):
    return jnp.sum(v, axis=0, keepdims=True)


def _causal_mask():
    r = lax.broadcasted_iota(jnp.int32, (CH, CH), 0)
    c = lax.broadcasted_iota(jnp.int32, (CH, CH), 1)
    return r >= c


def _rms(x):
    return lax.rsqrt(jnp.mean(x * x, axis=-1, keepdims=True) + RMS_EPS)


def _rms_bwd(dh, x, r, nw):
    gy = dh * nw
    return r * gy - x * (r * r * r) * jnp.mean(gy * x, axis=-1, keepdims=True)


def _place():
    return lax.axis_index("x"), lax.axis_index("y"), lax.axis_index("c")


def _other_chips(x, y):
    return [(1 - x, y), (x, 1 - y), (1 - x, 1 - y)]


GATHER_SLOTS = 10


def _gather_ops(ins, outs, send_sems, recv_sems, local_sems):
    n = len(ins)
    x, y, c = _place()
    sibling = (x, y, 1 - c)
    xn, yn, dg = _other_chips(x, y)
    split = [ins[i].shape[0] % (2 * BF16_ROWS) == 0 for i in range(n)]

    def blk(chip, core):
        return 4 * chip[0] + 2 * chip[1] + core

    me = blk((x, y), c)

    def part(ref, i, half):
        if half is None:
            return ref
        h = ins[i].shape[0] // 2
        return ref.at[pl.ds(half * h, h)]

    def copy(i, k, block, to, half=None, src=None):
        dst = part(outs[i].at[block], i, half)
        return pltpu.make_async_remote_copy(
            src_ref=dst if src is None else part(src, i, half), dst_ref=dst,
            send_sem=send_sems.at[k, i], recv_sem=recv_sems.at[k, i], device_id=to, device_id_type=MESH)

    def first_copies():
        mine = [pltpu.make_async_copy(ins[i], outs[i].at[me], local_sems.at[i]) for i in range(n)]
        first = []
        for i in range(n):
            first.append(copy(i, 0, me, sibling, src=ins[i]))
            if split[i]:
                first.append(copy(i, 1, me, (*xn, c), 0, ins[i]))
                first.append(copy(i, 3, me, (*yn, c), 1, ins[i]))
                first.append(copy(i, 2, me, (*xn, c), 1, ins[i]))
                first.append(copy(i, 4, me, (*yn, c), 0, ins[i]))
            else:
                first.append(copy(i, 1, me, (*xn, c), None, ins[i]))
                first.append(copy(i, 3, me, (*yn, c), None, ins[i]))
                first.append(copy(i, 5, me, (*dg, c), None, ins[i]))
        return mine, first

    def onward():
        out = []
        for i in range(n):
            if split[i]:
                out.append(copy(i, 5, blk(xn, c), (*yn, c), 0))
                out.append(copy(i, 6, blk(yn, c), (*xn, c), 1))
        return out

    def start():
        mine, first = first_copies()
        for cp in mine + first:
            cp.start()

    def relay():
        sends = onward()
        for i in range(n):
            if split[i]:
                copy(i, 1, blk(xn, c), sibling, 0).wait_recv()
                sends.pop(0).start()
                copy(i, 3, blk(yn, c), sibling, 1).wait_recv()
                sends.pop(0).start()

    def finish():
        mine, first = first_copies()
        passed = []

        def pass_on(i, j, chip):
            fwd = copy(i, 7 + j, blk(chip, c), sibling)
            fwd.start()
            passed.append(fwd)

        for i in range(n):
            if split[i]:
                copy(i, 2, blk(xn, c), sibling, 1).wait_recv()
                pass_on(i, 0, xn)
                copy(i, 4, blk(yn, c), sibling, 0).wait_recv()
                pass_on(i, 1, yn)
                copy(i, 5, blk(dg, c), sibling, 0).wait_recv()
                copy(i, 6, blk(dg, c), sibling, 1).wait_recv()
                pass_on(i, 2, dg)
            else:
                copy(i, 1, blk(xn, c), sibling).wait_recv()
                pass_on(i, 0, xn)
                copy(i, 3, blk(yn, c), sibling).wait_recv()
                pass_on(i, 1, yn)
                copy(i, 5, blk(dg, c), sibling).wait_recv()
                pass_on(i, 2, dg)
        for i in range(n):
            copy(i, 0, blk((x, y), 1 - c), sibling).wait_recv()
            for j, chip in enumerate((xn, yn, dg)):
                copy(i, 7 + j, blk(chip, 1 - c), sibling).wait_recv()
        for cp in first + passed + onward():
            cp.wait_send()
        for cp in mine:
            cp.wait()

    return start, relay, finish


def _gather_sems(n):
    return [pltpu.SemaphoreType.DMA((GATHER_SLOTS, n)), pltpu.SemaphoreType.DMA((GATHER_SLOTS, n)),
            pltpu.SemaphoreType.DMA((n,))]


class _Gather:
    def __init__(self, shards, as_dtypes=None):
        n = len(shards)
        dts = [s.dtype for s in shards] if as_dtypes is None else list(as_dtypes)
        self.cast = [jnp.dtype(d) != s.dtype for d, s in zip(dts, shards)]
        self.ins = list(shards)
        self.in_specs = [_VMEM if c else _HBM for c in self.cast]
        self.out_shape = [_sds((NDEV,) + s.shape, d) for s, d in zip(shards, dts)]
        self.out_specs = [_HBM] * n
        self.scratch = [pltpu.VMEM(s.shape, d) for s, d, c in zip(shards, dts, self.cast) if c] + _gather_sems(n)

    def ops(self, ins, outs, scr):
        ncast = sum(self.cast)
        staged = iter(scr[:ncast])
        srcs = [next(staged) if c else ref for c, ref in zip(self.cast, ins)]
        start, relay, finish = _gather_ops(srcs, outs, *scr[ncast:])

        def cast_and_start():
            for c, ref, src in zip(self.cast, ins, srcs):
                if c:
                    src[...] = ref[...].astype(src.dtype)
            start()

        return cast_and_start, relay, finish


class _Exchange:
    def __init__(self, qs):
        n = len(qs)
        self.ins, self.in_specs = list(qs), [_HBM] * n
        self.out_shape = [_sds(q.shape, q.dtype) for q in qs]
        self.out_specs = [_HBM] * n
        self.scratch = [pltpu.SemaphoreType.DMA((NCHIP_OTHER, n)), pltpu.SemaphoreType.DMA((NCHIP_OTHER, n))]

    def ops(self, ins, outs, scr):
        send_sems, recv_sems = scr
        n = len(ins)
        x, y, c = _place()
        chips = _other_chips(x, y)

        def copies():
            return [pltpu.make_async_remote_copy(
                src_ref=ins[i].at[j], dst_ref=outs[i].at[j], send_sem=send_sems.at[j, i],
                recv_sem=recv_sems.at[j, i], device_id=(*chips[j], c), device_id_type=MESH)
                for i in range(n) for j in range(NCHIP_OTHER)]

        def start():
            for cp in copies():
                cp.start()

        def finish():
            cps = copies()
            for cp in cps:
                cp.wait_recv()
            for cp in cps:
                cp.wait_send()

        return start, lambda: None, finish


class _ExchangeVia:
    def __init__(self, q):
        _, r, cd = q.shape
        half = (2, r // 2, cd)
        self.ins, self.in_specs = [q], [_HBM]
        self.out_shape, self.out_specs = [_sds((2, r, cd), q.dtype)], [_HBM]
        self.scratch = [pltpu.VMEM(half, q.dtype), pltpu.VMEM(half, q.dtype), pltpu.VMEM(half, q.dtype),
                        pltpu.SemaphoreType.DMA((6,)), pltpu.SemaphoreType.DMA((6,)), pltpu.SemaphoreType.DMA((2,))]

    def ops(self, ins, outs, scr):
        (q,), (land,) = ins, outs
        relayed, own, comb, send_sems, recv_sems, local_sems = scr
        h = q.shape[1] // 2
        x, y, c = _place()
        xn, yn, _ = _other_chips(x, y)
        h0, h1 = pl.ds(0, h), pl.ds(h, h)

        def remote(k, src, dst, chip):
            return pltpu.make_async_remote_copy(src_ref=src, dst_ref=dst, send_sem=send_sems.at[k],
                                                recv_sem=recv_sems.at[k], device_id=(*chip, c), device_id_type=MESH)

        def via():
            return [remote(2, q.at[2, h0], relayed.at[0], xn), remote(3, q.at[2, h1], relayed.at[1], yn)]

        def direct():
            return [remote(0, q.at[0, h0], land.at[0, h0], xn), remote(1, q.at[1, h1], land.at[1, h1], yn)]

        def second():
            return [remote(4, comb.at[0], land.at[1, h0], yn), remote(5, comb.at[1], land.at[0, h1], xn)]

        def mine():
            return [pltpu.make_async_copy(q.at[1, h0], own.at[0], local_sems.at[0]),
                    pltpu.make_async_copy(q.at[0, h1], own.at[1], local_sems.at[1])]

        def start():
            for cp in via() + direct() + mine():
                cp.start()

        def relay():
            arrived, loaded, onward = via(), mine(), second()
            for k in range(2):
                arrived[k].wait_recv()
                loaded[k].wait()
                comb[k] = (own[k].astype(F32) + relayed[k].astype(F32)).astype(comb.dtype)
                onward[k].start()

        def finish():
            landing = direct() + second()
            for cp in landing:
                cp.wait_recv()
            for cp in via() + landing:
                cp.wait_send()

        return start, relay, finish


class _SumGather:
    def __init__(self, accs, lands):
        n = len(accs)
        self.n = n
        self.ins, self.in_specs = list(accs) + list(lands), [_VMEM] * (2 * n)
        self.out_shape = [_sds((NDEV,) + a.shape, a.dtype) for a in accs]
        self.out_specs = [_HBM] * n
        self.scratch = [pltpu.VMEM(a.shape, a.dtype) for a in accs] + _gather_sems(n)

    def ops(self, ins, outs, scr):
        n = self.n
        accs, lands, mine = ins[:n], ins[n:], scr[:n]
        g_start, relay, finish = _gather_ops(mine, outs, *scr[n:])

        def start():
            for i in range(n):
                mine[i][...] = accs[i][...] + lands[i][0] + lands[i][1] + lands[i][2]
            g_start()

        return start, relay, finish


def _call(main, jobs, *, name, grid, ins, in_specs, out_shape, out_specs, scratch, relay_step=0):
    nsteps = grid[0] if grid else 1
    n_in, n_out, n_scr = len(ins), len(out_shape), len(scratch)

    def body(*refs):
        pos = [0]

        def take(k):
            r = refs[pos[0]:pos[0] + k]
            pos[0] += k
            return r

        m_in = take(n_in)
        j_in = [take(len(j.ins)) for j in jobs]
        m_out = take(n_out)
        j_out = [take(len(j.out_shape)) for j in jobs]
        m_scr = take(n_scr)
        j_scr = [take(len(j.scratch)) for j in jobs]
        ops = [j.ops(a, b, s) for j, a, b, s in zip(jobs, j_in, j_out, j_scr)]
        i = pl.program_id(0) if grid else 0
        if not grid:
            for o in ops:
                o[0]()
            main(i, m_in, m_out, m_scr)
            for o in ops:
                o[1]()
            for o in ops:
                o[2]()
            return

        if ops:
            @pl.when(i == 0)
            def _():
                for o in ops:
                    o[0]()

        main(i, m_in, m_out, m_scr)

        if ops:
            @pl.when(i == min(relay_step, nsteps - 1))
            def _():
                for o in ops:
                    o[1]()

            @pl.when(i == nsteps - 1)
            def _():
                for o in ops:
                    o[2]()

    extra = dict(dimension_semantics=("arbitrary",)) if grid else {}
    res = pl.pallas_call(
        body, name=name, grid=grid,
        in_specs=list(in_specs) + [s for j in jobs for s in j.in_specs],
        out_specs=list(out_specs) + [s for j in jobs for s in j.out_specs],
        out_shape=list(out_shape) + [s for j in jobs for s in j.out_shape],
        scratch_shapes=list(scratch) + [s for j in jobs for s in j.scratch],
        compiler_params=_params(**extra),
    )(*ins, *[a for j in jobs for a in j.ins])
    main_out, rest, job_out = res[:n_out], res[n_out:], []
    for j in jobs:
        k = len(j.out_shape)
        job_out.append(rest[:k])
        rest = rest[k:]
    return main_out, job_out


def _comm_only(jobs, name):
    _, job_out = _call(lambda i, a, b, s: None, jobs, name=name, grid=(), ins=[], in_specs=[], out_shape=[],
                       out_specs=[], scratch=[])
    return job_out


class _InChip:
    def __init__(self, ps):
        n = len(ps)
        self.n = n
        blk = [p.shape[1:] for p in ps]
        self.ins, self.in_specs = list(ps), [_HBM] * n
        self.out_shape = [_sds((NCHIP_OTHER,) + b, p.dtype) for b, p in zip(blk, ps)] + [_sds(b, F32) for b in blk]
        self.out_specs = [_VMEM] * (2 * n)
        self.scratch = ([pltpu.VMEM((4,) + b, p.dtype) for b, p in zip(blk, ps)] * 2
                        + [pltpu.SemaphoreType.DMA((4, n))] * 3)

    def ops(self, ins, outs, scr):
        n = self.n
        q_refs, acc_refs = outs[:n], outs[n:]
        mines, lands = scr[:n], scr[n:2 * n]
        send_sems, recv_sems, local_sems = scr[2 * n:]
        x, y, c = _place()
        sibling = (x, y, 1 - c)

        def copies():
            out = []
            for i in range(n):
                for pi in range(4):
                    loc = pltpu.make_async_copy(ins[i].at[2 * pi + c], mines[i].at[pi], local_sems.at[pi, i])
                    cp = pltpu.make_async_remote_copy(
                        src_ref=ins[i].at[2 * pi + (1 - c)], dst_ref=lands[i].at[pi],
                        send_sem=send_sems.at[pi, i], recv_sem=recv_sems.at[pi, i],
                        device_id=sibling, device_id_type=MESH)
                    out.append((loc, cp))
            return out

        def start():
            for loc, cp in copies():
                loc.start()
                cp.start()

        def finish():
            pairs = copies()
            for loc, cp in pairs:
                loc.wait()
                cp.wait_recv()
            for i in range(n):
                _chip_sums(mines[i], lands[i], q_refs[i], acc_refs[i], x, y)
            for _, cp in pairs:
                cp.wait_send()

        return start, lambda: None, finish


def _chip_sums(mine, land, q_ref, acc_ref, x, y):
    for j, (qx, qy) in enumerate(_other_chips(x, y)):
        qi = 2 * qx + qy
        q_ref[j] = (mine[qi].astype(F32) + land[qi].astype(F32)).astype(q_ref.dtype)
    mi = 2 * x + y
    acc_ref[...] = mine[mi].astype(F32) + land[mi].astype(F32)


def _direct_sum(v, buf, send_sems, recv_sems):
    x, y, c = _place()
    me = 4 * x + 2 * y + c
    buf[me] = v
    cps = []
    for k in range(1, NDEV):
        fx, fy, fc = (k >> 2) & 1, (k >> 1) & 1, k & 1
        peer = ((1 - x) if fx else x, (1 - y) if fy else y, (1 - c) if fc else c)
        cps.append((peer, pltpu.make_async_remote_copy(
            src_ref=buf.at[me], dst_ref=buf.at[me], send_sem=send_sems.at[k - 1], recv_sem=recv_sems.at[k - 1],
            device_id=peer, device_id_type=MESH)))
    for _, cp in cps:
        cp.start()
    for k, (peer, _) in enumerate(cps):
        theirs = 4 * peer[0] + 2 * peer[1] + peer[2]
        pltpu.make_async_remote_copy(
            src_ref=buf.at[theirs], dst_ref=buf.at[theirs], send_sem=send_sems.at[k], recv_sem=recv_sems.at[k],
            device_id=peer, device_id_type=MESH).wait_recv()
    acc = buf[0]
    for j in range(1, NDEV):
        acc = acc + buf[j]
    for _, cp in cps:
        cp.wait_send()
    return acc


def _direct_sum_scratch(shape, dtype):
    return [pltpu.VMEM((NDEV,) + tuple(shape), dtype), pltpu.SemaphoreType.DMA((NDEV - 1,)),
            pltpu.SemaphoreType.DMA((NDEV - 1,))]


def _fwd_a(x, nw, win8, lnw, lnb, ws, bst, jobs, *, tm, relay_step):
    s_len = x.shape[0]
    nt = s_len // tm
    nch = tm // CH

    def main(i, ins, outs, scr):
        x_ref, nw_ref, win_ref, lnw_ref, lnb_ref, ws_ref, bst_ref = ins
        z_ref, h_ref, y_ref = outs
        wc_scr, gv_scr = scr

        @pl.when(i == 0)
        def _():
            m = _causal_mask()
            for g in range(G):
                wc_scr[g] = jnp.where(m, ws_ref[g], 0.0).astype(BF16)

        x = x_ref[...]
        h = (x * _rms(x) * nw_ref[...]).astype(BF16)
        h_ref[...] = h
        for k in range(NDEV):
            z_ref[:, k * CA:(k + 1) * CA] = _dot(h, win_ref[k])

        ssum = jnp.zeros((tm, 1), F32)
        for g in range(G):
            gv = _gelu_t(z_ref[:, AW + g * GD:AW + (g + 1) * GD])[0]
            gv_scr[:, g * GD:(g + 1) * GD] = gv
            ssum = ssum + jnp.sum(gv, axis=-1, keepdims=True)
        mu = ssum * (1.0 / AW)
        vsum = jnp.zeros((tm, 1), F32)
        for g in range(G):
            dlt = gv_scr[:, g * GD:(g + 1) * GD] - mu
            vsum = vsum + jnp.sum(dlt * dlt, axis=-1, keepdims=True)
        rstd = lax.rsqrt(vsum * (1.0 / AW) + LN_EPS)

        for g in range(G):
            cs = slice(g * GD, (g + 1) * GD)
            v = (gv_scr[:, cs] - mu) * rstd * lnw_ref[:, cs] + lnb_ref[:, cs]
            vb = v.astype(BF16)
            u = _gelu_t(z_ref[:, cs])[0]
            zg = z_ref[:, 2 * AW + g * GD:2 * AW + (g + 1) * GD]
            sg = zg * _sigmoid(zg)
            for n in range(nch):
                rs = slice(n * CH, (n + 1) * CH)
                s = _dot(wc_scr[g], vb[rs, :]) + bst_ref[:, g:g + 1]
                y_ref[rs, cs] = (u[rs, :] * s * sg[rs, :]).astype(BF16)

    tile = lambda w: pl.BlockSpec((tm, w), lambda i: (i, 0))
    return _call(
        main, jobs, name="fwd_a", grid=(nt,), relay_step=relay_step,
        ins=[x, nw, win8, lnw, lnb, ws, bst], in_specs=[tile(D), _VMEM, _VMEM, _VMEM, _VMEM, _VMEM, _VMEM],
        out_shape=[_sds((s_len, 3 * AW), F32), _sds((s_len, D), BF16), _sds((s_len, AW), BF16)],
        out_specs=[tile(3 * AW), tile(D), tile(AW)],
        scratch=[pltpu.VMEM((G, CH, CH), BF16), pltpu.VMEM((tm, AW), F32)])


def _bwd_a(dx1, z, lnw, lnb, ws, bst, wout, jobs, *, tm, relay_step):
    s_len = dx1.shape[0]
    nt = s_len // tm
    nch = tm // CH

    def main(i, ins, outs, scr):
        dx1_ref, z_ref, lnw_ref, lnb_ref, ws_ref, bst_ref, wout_ref = ins
        dz_ref, glnw_ref, glnb_ref, gws_ref, gbst_ref = outs
        wc_scr, wct_scr, vh_scr, dgv_scr, dy_scr, dv_scr, gbs_acc, gwc_acc = scr

        @pl.when(i == 0)
        def _():
            m = _causal_mask()
            for g in range(G):
                wm = jnp.where(m, ws_ref[g], 0.0)
                wc_scr[g] = wm.astype(BF16)
                wct_scr[g] = wm.T.astype(BF16)
            glnw_ref[...] = jnp.zeros_like(glnw_ref)
            glnb_ref[...] = jnp.zeros_like(glnb_ref)
            gbs_acc[...] = jnp.zeros_like(gbs_acc)
            gwc_acc[...] = jnp.zeros_like(gwc_acc)

        dy_scr[...] = _dot_nt(dx1_ref[...], wout_ref[...])

        ssum = jnp.zeros((tm, 1), F32)
        for g in range(G):
            cs = slice(g * GD, (g + 1) * GD)
            zv = z_ref[:, AW + g * GD:AW + (g + 1) * GD]
            gv, t = _gelu_t(zv)
            vh_scr[:, cs] = gv
            dgv_scr[:, cs] = _dgelu(zv, t)
            ssum = ssum + jnp.sum(gv, axis=-1, keepdims=True)
        mu = ssum * (1.0 / AW)
        vsum = jnp.zeros((tm, 1), F32)
        for g in range(G):
            dlt = vh_scr[:, g * GD:(g + 1) * GD] - mu
            vsum = vsum + jnp.sum(dlt * dlt, axis=-1, keepdims=True)
        rstd = lax.rsqrt(vsum * (1.0 / AW) + LN_EPS)

        m1 = jnp.zeros((tm, 1), F32)
        m2 = jnp.zeros((tm, 1), F32)
        for g in range(G):
            cs = slice(g * GD, (g + 1) * GD)
            gs = slice(2 * AW + g * GD, 2 * AW + (g + 1) * GD)
            vhat = (vh_scr[:, cs] - mu) * rstd
            vh_scr[:, cs] = vhat
            vb = (vhat * lnw_ref[:, cs] + lnb_ref[:, cs]).astype(BF16)
            zu = z_ref[:, cs]
            u, tu = _gelu_t(zu)
            zg = z_ref[:, gs]
            sig = _sigmoid(zg)
            sg = zg * sig
            dy = dy_scr[:, cs]
            dsf = dy * u * sg
            dsb = dsf.astype(BF16)
            dvs = []
            for n in range(nch):
                rs = slice(n * CH, (n + 1) * CH)
                s = _dot(wc_scr[g], vb[rs, :]) + bst_ref[:, g:g + 1]
                dys = dy[rs, :] * s
                dz_ref[rs, cs] = (dys * sg[rs, :] * _dgelu(zu[rs, :], tu[rs, :])).astype(BF16)
                dz_ref[rs, gs] = (dys * u[rs, :] * (sig[rs, :] * (1.0 + zg[rs, :] * (1.0 - sig[rs, :])))).astype(BF16)
                gbs_acc[g] += dsf[rs, :]
                gwc_acc[g] += _dot_nt(dsb[rs, :], vb[rs, :])
                dvs.append(_dot(wct_scr[g], dsb[rs, :]))
            dv = jnp.concatenate(dvs, axis=0) if nch > 1 else dvs[0]
            glnw_ref[:, cs] += _rowsum(dv * vhat)
            glnb_ref[:, cs] += _rowsum(dv)
            dvh = dv * lnw_ref[:, cs]
            dv_scr[:, cs] = dvh
            m1 = m1 + jnp.sum(dvh, axis=-1, keepdims=True)
            m2 = m2 + jnp.sum(dvh * vhat, axis=-1, keepdims=True)
        m1 = m1 * (1.0 / AW)
        m2 = m2 * (1.0 / AW)
        for g in range(G):
            cs = slice(g * GD, (g + 1) * GD)
            dgv = rstd * (dv_scr[:, cs] - m1 - vh_scr[:, cs] * m2)
            dz_ref[:, AW + g * GD:AW + (g + 1) * GD] = (dgv * dgv_scr[:, cs]).astype(BF16)

        @pl.when(i == nt - 1)
        def _():
            m = _causal_mask()
            for g in range(G):
                gws_ref[g] = jnp.where(m, gwc_acc[g], 0.0)
                gbst_ref[:, g:g + 1] = jnp.sum(gbs_acc[g], axis=-1, keepdims=True)

    tile = lambda w: pl.BlockSpec((tm, w), lambda i: (i, 0))
    whole = lambda *s: pl.BlockSpec(s, lambda i: (0,) * len(s))
    big = lambda dt: pltpu.VMEM((tm, AW), dt)
    return _call(
        main, jobs, name="bwd_a", grid=(nt,), relay_step=relay_step,
        ins=[dx1, z, lnw, lnb, ws, bst, wout], in_specs=[tile(D), tile(3 * AW), _VMEM, _VMEM, _VMEM, _VMEM, _VMEM],
        out_shape=[_sds((s_len, 3 * AW), BF16), _sds((1, AW), F32), _sds((1, AW), F32), _sds((G, CH, CH), F32),
                   _sds((CH, G), F32)],
        out_specs=[tile(3 * AW), whole(1, AW), whole(1, AW), whole(G, CH, CH), whole(CH, G)],
        scratch=[pltpu.VMEM((G, CH, CH), BF16), pltpu.VMEM((G, CH, CH), BF16), big(F32), big(F32), big(F32), big(F32),
                 pltpu.VMEM((G, CH, GD), F32), pltpu.VMEM((G, CH, CH), F32)])


def _bwd_a_in(dz, dx1, x, nw, win8, jobs, *, tm, relay_step):
    s_len = x.shape[0]
    nt = s_len // tm

    def main(i, ins, outs, scr):
        dz_ref, dx1_ref, x_ref, nw_ref, win_ref = ins
        gx_ref, gnw_ref = outs

        @pl.when(i == 0)
        def _():
            gnw_ref[...] = jnp.zeros_like(gnw_ref)

        dh = jnp.zeros((tm, D), F32)
        for k in range(NDEV):
            dh = dh + _dot_nt(dz_ref[:, k * CA:(k + 1) * CA], win_ref[k])
        x = x_ref[...]
        r = _rms(x)
        gx_ref[...] = dx1_ref[...] + _rms_bwd(dh, x, r, nw_ref[...])
        gnw_ref[...] += _rowsum(dh * x * r)

        @pl.when(i == nt - 1)
        def _():
            gnw_ref[...] = _direct_sum(gnw_ref[...], *scr)

    tile = lambda w: pl.BlockSpec((tm, w), lambda i: (i, 0))
    return _call(
        main, jobs, name="bwd_a_in", grid=(nt,), relay_step=relay_step,
        ins=[dz, dx1, x, nw, win8], in_specs=[tile(3 * AW), tile(D), tile(D), _VMEM, _VMEM],
        out_shape=[_sds((s_len, D), F32), _sds((1, D), F32)],
        out_specs=[tile(D), pl.BlockSpec((1, D), lambda i: (0, 0))], scratch=_direct_sum_scratch((1, D), F32))


def _conv(p8_ref, cs, xb, xm1, xm2, xm3):
    xc = p8_ref[4:5, cs] + p8_ref[3:4, cs] * xb
    xc = xc + p8_ref[0:1, cs] * xm3
    xc = xc + p8_ref[1:2, cs] * xm2
    return xc + p8_ref[2:3, cs] * xm1


def _gates(p8_ref, gcat_ref, hh, xc):
    cs = slice(hh * HD, (hh + 1) * HD)
    pre = _dot(xc.astype(BF16), gcat_ref[hh])
    r = _sigmoid(pre[:, :HD] + p8_ref[5:6, cs])
    ig = _sigmoid(pre[:, HD:] + p8_ref[6:7, cs])
    sp = _softplus_neg(p8_ref[7:8, cs])
    la = (-RG_C) * r * sp
    a = jnp.exp(la)
    half_log = 0.5 * jnp.log(jnp.tanh(-la) * (1.0 + a * a))
    return r, ig, sp, a, jnp.exp(half_log), jnp.exp(-half_log)


def _scan_rows(a_ref, b_ref, out_ref, carry, tm, reverse):
    row = lax.broadcasted_iota(jnp.int32, (SUBLANES, BW), 0)
    ngrp = tm // SUBLANES

    def step(j, cr):
        jj = (ngrp - 1 - j) if reverse else j
        off = pl.multiple_of(jj * SUBLANES, SUBLANES)
        a = a_ref[pl.ds(off, SUBLANES), :]
        b = b_ref[pl.ds(off, SUBLANES), :]
        for sh in (1, 2, 4):
            if reverse:
                a_s = pltpu.roll(a, SUBLANES - sh, 0)
                b_s = pltpu.roll(b, SUBLANES - sh, 0)
                m = row < SUBLANES - sh
            else:
                a_s = pltpu.roll(a, sh, 0)
                b_s = pltpu.roll(b, sh, 0)
                m = row >= sh
            b = jnp.where(m, a * b_s + b, b)
            a = jnp.where(m, a * a_s, a)
        o = b + a * cr
        out_ref[pl.ds(off, SUBLANES), :] = o
        return o[0:1, :] if reverse else o[SUBLANES - 1:SUBLANES, :]

    return lax.fori_loop(0, ngrp, step, carry)


def _fwd_b(x, ya, wout_a, nw, win8, p8, gcat, jobs, *, tm, relay_step):
    s_len = x.shape[0]
    nt = s_len // tm

    def main(i, ins, outs, scr):
        x_ref, ya_ref, wouta_ref, nw_ref, win_ref, p8_ref, gcat_ref = ins
        x1_ref, zb_ref, hs_ref, h1_ref, yb_ref, xc_ref, a_ref, cc_ref, r_ref, ig_ref, m_ref = outs
        xbe_scr, b_scr, k_scr, carry_scr = scr

        @pl.when(i == 0)
        def _():
            xbe_scr[0:SUBLANES, :] = jnp.zeros((SUBLANES, BW), F32)
            carry_scr[...] = jnp.zeros_like(carry_scr)

        x1 = x_ref[...] + _dot(ya_ref[...], wouta_ref[...])
        x1_ref[...] = x1
        h = (x1 * _rms(x1) * nw_ref[...]).astype(BF16)
        h1_ref[...] = h
        for k in range(NDEV):
            zb_ref[:, k * CB:(k + 1) * CB] = _dot(h, win_ref[k])
        xbe_scr[SUBLANES:SUBLANES + tm, :] = zb_ref[:, :BW]
        for hh in range(BH):
            cs = slice(hh * HD, (hh + 1) * HD)
            xc = _conv(p8_ref, cs, xbe_scr[SUBLANES:SUBLANES + tm, cs], xbe_scr[7:7 + tm, cs],
                       xbe_scr[6:6 + tm, cs], xbe_scr[5:5 + tm, cs])
            r, ig, _, a, mult, rm = _gates(p8_ref, gcat_ref, hh, xc)
            ixc = ig * xc
            xc_ref[:, cs] = xc
            a_ref[:, cs] = a
            r_ref[:, cs] = r.astype(BF16)
            ig_ref[:, cs] = ig.astype(BF16)
            m_ref[:, cs] = mult.astype(BF16)
            b_scr[:, cs] = mult * ixc
            k_scr[:, cs] = ixc * (a * a * rm)
        xbe_scr[0:SUBLANES, :] = xbe_scr[tm:tm + SUBLANES, :]
        carry_scr[...] = _scan_rows(a_ref, b_scr, hs_ref, carry_scr[...], tm, False)
        for hh in range(BH):
            cs = slice(hh * HD, (hh + 1) * HD)
            gt = zb_ref[:, BW + hh * HD:BW + (hh + 1) * HD]
            hsv = hs_ref[:, cs]
            yb_ref[:, cs] = (hsv * (gt * _sigmoid(gt))).astype(BF16)
            cc_ref[:, cs] = (hsv - b_scr[:, cs]) - k_scr[:, cs]

    tile = lambda w: pl.BlockSpec((tm, w), lambda i: (i, 0))
    wide = lambda dt: _sds((s_len, BW), dt)
    return _call(
        main, jobs, name="fwd_b", grid=(nt,), relay_step=relay_step,
        ins=[x, ya, wout_a, nw, win8, p8, gcat], in_specs=[tile(D), tile(AW), _VMEM, _VMEM, _VMEM, _VMEM, _VMEM],
        out_shape=[_sds((s_len, D), F32), _sds((s_len, 2 * BW), F32), wide(F32), _sds((s_len, D), BF16), wide(BF16),
                   wide(F32), wide(F32), wide(F32), wide(BF16), wide(BF16), wide(BF16)],
        out_specs=[tile(D), tile(2 * BW), tile(BW), tile(D)] + [tile(BW)] * 7,
        scratch=[pltpu.VMEM((tm + SUBLANES, BW), F32), pltpu.VMEM((tm, BW), F32), pltpu.VMEM((tm, BW), F32),
                 pltpu.VMEM((1, BW), F32)])


def _head(x1, yb, wout, nfw, tgt, *, tm):
    s_len = x1.shape[0]

    def main(i, ins, outs, scr):
        x1_ref, yb_ref, wout_ref, nfw_ref, t_ref = ins
        dx2_ref, dx2b_ref, loss_ref, gnfw_ref = outs

        @pl.when(i == 0)
        def _():
            loss_ref[...] = jnp.zeros_like(loss_ref)
            gnfw_ref[...] = jnp.zeros_like(gnfw_ref)

        x2 = x1_ref[...] + _dot(yb_ref[...], wout_ref[...])
        rf = _rms(x2)
        xn = x2 * rf
        e = xn * nfw_ref[...] - t_ref[...]
        loss_ref[...] += (0.5 / D) * jnp.sum(jnp.sum(e * e, axis=-1, keepdims=True), axis=0, keepdims=True)
        dyf = e * (1.0 / D)
        gnfw_ref[...] += _rowsum(dyf * xn)
        dx2 = _rms_bwd(dyf, x2, rf, nfw_ref[...])
        dx2_ref[...] = dx2
        dx2b_ref[...] = dx2.astype(BF16)

    tile = lambda w: pl.BlockSpec((tm, w), lambda i: (i, 0))
    whole = lambda *s: pl.BlockSpec(s, lambda i: (0,) * len(s))
    (dx2, dx2b, loss, gnfw), _ = _call(
        main, [], name="head", grid=(s_len // tm,),
        ins=[x1, yb, wout, nfw, tgt], in_specs=[tile(D), tile(BW), _VMEM, _VMEM, tile(D)],
        out_shape=[_sds((s_len, D), F32), _sds((s_len, D), BF16), _sds((1, 1), F32), _sds((1, D), F32)],
        out_specs=[tile(D), tile(D), whole(1, 1), whole(1, D)], scratch=[])
    return dx2, dx2b, loss, gnfw


def _bwd_b(dx2, zb, hs, x1, saved, nw, win8, p8, gcat, wout, *, tm):
    s_len = x1.shape[0]
    nt = s_len // tm

    def main(i, ins, outs, scr):
        (dx2_ref, zb_ref, hs_ref, x1_ref, xc_ref, a_ref, cc_ref, r_ref, ig_ref, m_ref,
         nw_ref, win_ref, p8_ref, gcat_ref, wout_ref) = ins
        dx1_ref, dx1b_ref, dzb_ref, gp8_ref, gga_ref, ggx_ref, gnw_ref = outs
        ae_scr, an_scr, dhd_scr, dh_scr, dy_scr, dxce_scr, carry_scr, afirst_scr = scr

        @pl.when(i == 0)
        def _():
            gp8_ref[...] = jnp.zeros_like(gp8_ref)
            gga_ref[...] = jnp.zeros_like(gga_ref)
            ggx_ref[...] = jnp.zeros_like(ggx_ref)
            gnw_ref[...] = jnp.zeros_like(gnw_ref)
            dxce_scr[tm:tm + SUBLANES, :] = jnp.zeros((SUBLANES, BW), F32)
            carry_scr[...] = jnp.zeros_like(carry_scr)
            afirst_scr[...] = jnp.zeros_like(afirst_scr)

        dx2 = dx2_ref[...]
        dy_scr[...] = _dot_nt(dx2.astype(BF16), wout_ref[...])
        for hh in range(BH):
            cs = slice(hh * HD, (hh + 1) * HD)
            gs = slice(BW + hh * HD, BW + (hh + 1) * HD)
            gt = zb_ref[:, gs]
            sig = _sigmoid(gt)
            dy = dy_scr[:, cs]
            dhd_scr[:, cs] = dy * (gt * sig)
            dzb_ref[:, gs] = (dy * hs_ref[:, cs] * (sig * (1.0 + gt * (1.0 - sig)))).astype(BF16)

        ae_scr[0:tm, :] = a_ref[...]
        ae_scr[tm:tm + SUBLANES, :] = jnp.broadcast_to(afirst_scr[...], (SUBLANES, BW))
        an_scr[...] = ae_scr[1:1 + tm, :]
        afirst_scr[...] = ae_scr[0:1, :]
        carry_scr[...] = _scan_rows(an_scr, dhd_scr, dh_scr, carry_scr[...], tm, True)

        for hh in range(BH):
            cs = slice(hh * HD, (hh + 1) * HD)
            dh = dh_scr[:, cs]
            mult = m_ref[:, cs].astype(F32)
            ig = ig_ref[:, cs].astype(F32)
            r = r_ref[:, cs].astype(F32)
            xc = xc_ref[:, cs]
            lam = p8_ref[7:8, cs]
            sp = _softplus_neg(lam)
            dla = dh * cc_ref[:, cs]
            gp8_ref[7:8, cs] += _rowsum(dla * ((-RG_C) * r)) * (-_sigmoid(-lam))
            dpr = dla * ((-RG_C) * sp) * (r * (1.0 - r))
            dpi = dh * mult * xc * (ig * (1.0 - ig))
            gp8_ref[5:6, cs] += _rowsum(dpr)
            gp8_ref[6:7, cs] += _rowsum(dpi)
            dcat = jnp.concatenate([dpr, dpi], axis=1).astype(BF16)
            dxc = dh * mult * ig + _dot_nt(dcat, gcat_ref[hh])
            gg = _dot(xc.T.astype(BF16), dcat)
            gga_ref[hh] += gg[:, :HD]
            ggx_ref[hh] += gg[:, HD:]
            dxce_scr[0:tm, cs] = dxc
            gp8_ref[4:5, cs] += _rowsum(dxc)
        for hh in range(BH):
            cs = slice(hh * HD, (hh + 1) * HD)
            xb = zb_ref[:, cs]
            d0, d1 = dxce_scr[0:tm, cs], dxce_scr[1:1 + tm, cs]
            d2, d3 = dxce_scr[2:2 + tm, cs], dxce_scr[3:3 + tm, cs]
            dzb_ref[:, cs] = (p8_ref[3:4, cs] * d0 + p8_ref[2:3, cs] * d1 + p8_ref[1:2, cs] * d2
                              + p8_ref[0:1, cs] * d3).astype(BF16)
            gp8_ref[3:4, cs] += _rowsum(d0 * xb)
            gp8_ref[2:3, cs] += _rowsum(d1 * xb)
            gp8_ref[1:2, cs] += _rowsum(d2 * xb)
            gp8_ref[0:1, cs] += _rowsum(d3 * xb)
        dxce_scr[tm:tm + SUBLANES, :] = dxce_scr[0:SUBLANES, :]

        dh1 = jnp.zeros((tm, D), F32)
        for k in range(NDEV):
            dh1 = dh1 + _dot_nt(dzb_ref[:, k * CB:(k + 1) * CB], win_ref[k])
        x1 = x1_ref[...]
        r1 = _rms(x1)
        dx1 = dx2 + _rms_bwd(dh1, x1, r1, nw_ref[...])
        dx1_ref[...] = dx1
        dx1b_ref[...] = dx1.astype(BF16)
        gnw_ref[...] += _rowsum(dh1 * x1 * r1)

    tile = lambda w: pl.BlockSpec((tm, w), lambda i: (nt - 1 - i, 0))
    whole = lambda *s: pl.BlockSpec(s, lambda i: (0,) * len(s))
    full = lambda: pltpu.VMEM((tm, BW), F32)
    ext = lambda: pltpu.VMEM((tm + SUBLANES, BW), F32)
    out, _ = _call(
        main, [], name="bwd_b", grid=(nt,),
        ins=[dx2, zb, hs, x1, *saved, nw, win8, p8, gcat, wout],
        in_specs=[tile(D), tile(2 * BW), tile(BW), tile(D)] + [tile(BW)] * 6 + [_VMEM] * 5,
        out_shape=[_sds((s_len, D), F32), _sds((s_len, D), BF16), _sds((s_len, 2 * BW), BF16), _sds((SUBLANES, BW), F32),
                   _sds((BH, HD, HD), F32), _sds((BH, HD, HD), F32), _sds((1, D), F32)],
        out_specs=[tile(D), tile(D), tile(2 * BW), whole(SUBLANES, BW), whole(BH, HD, HD), whole(BH, HD, HD),
                   whole(1, D)],
        scratch=[ext(), full(), full(), full(), full(), ext(), pltpu.VMEM((1, BW), F32), pltpu.VMEM((1, BW), F32)])
    return out


def _transpose_into(dst_ref, src_ref, rows):
    s_len = src_ref.shape[0]
    for r0 in range(0, s_len, rows):
        dst_ref[:, r0:r0 + rows] = src_ref[r0:r0 + rows, :].astype(F32).T.astype(BF16)


def _wgrad(a, b, jobs, *, by_rows, per, name, relay_step=0):
    s_len, m = a.shape
    n = b.shape[1]
    r, cd = (m // NDEV, n) if by_rows else (m, n // NDEV)
    nsteps = NDEV // per
    at_rows = per * r if by_rows else m

    def main(i, ins, outs, scr):
        a_ref, b_ref = ins
        q_ref, acc_ref = outs
        at_scr, stage, mine, land, send_sems, recv_sems = scr
        x, y, c = _place()

        def to_sibling(pi):
            return pltpu.make_async_remote_copy(
                src_ref=stage.at[pi & 1], dst_ref=land.at[pi], send_sem=send_sems.at[pi], recv_sem=recv_sems.at[pi],
                device_id=(x, y, 1 - c), device_id_type=MESH)

        if by_rows:
            _transpose_into(at_scr, a_ref, TRANSPOSE_ROWS)
        else:
            @pl.when(i == 0)
            def _():
                _transpose_into(at_scr, a_ref, TRANSPOSE_ROWS)

        res = _dot(at_scr[...], b_ref[...]).astype(BF16)
        for k in range(per):
            blk = per * i + k
            pi, pc = blk >> 1, blk & 1
            val = res[k * r:(k + 1) * r, :] if by_rows else res

            @pl.when(pc != c)
            def _():
                @pl.when(pi >= 2)
                def _():
                    to_sibling(pi - 2).wait_send()

                stage[pi & 1] = val
                to_sibling(pi).start()

            @pl.when(pc == c)
            def _():
                mine[pi] = val

        @pl.when(i == nsteps - 1)
        def _():
            for p in range(4):
                to_sibling(p).wait_recv()
            to_sibling(2).wait_send()
            to_sibling(3).wait_send()
            _chip_sums(mine, land, q_ref, acc_ref, x, y)

    if by_rows:
        in_specs = [pl.BlockSpec((s_len, at_rows), lambda j: (0, j)), _VMEM]
    else:
        in_specs = [_VMEM, pl.BlockSpec((s_len, cd), lambda j: (0, j))]
    blk_vmem = lambda k: pltpu.VMEM((k, r, cd), BF16)
    (q, acc), job_out = _call(
        main, jobs, name=name, grid=(nsteps,), relay_step=relay_step, ins=[a, b], in_specs=in_specs,
        out_shape=[_sds((NCHIP_OTHER, r, cd), BF16), _sds((r, cd), F32)],
        out_specs=[pl.BlockSpec((NCHIP_OTHER, r, cd), lambda j: (0, 0, 0)), pl.BlockSpec((r, cd), lambda j: (0, 0))],
        scratch=[pltpu.VMEM((at_rows, s_len), BF16), blk_vmem(2), blk_vmem(4), blk_vmem(4),
                 pltpu.SemaphoreType.DMA((4,)), pltpu.SemaphoreType.DMA((4,))])
    return q, acc, job_out


def _wgrad_cols_early(a, b, jobs, *, name, relay_step=0):
    s_len, m = a.shape
    r, cd = m, b.shape[1] // NDEV
    h = r // 2

    def chip_at(pos, base):
        return base ^ (3 - pos)

    def main(i, ins, outs, scr):
        a_ref, b_ref = ins
        q_ref, acc_ref, rel_ref = outs
        at_scr, stage, mine, land, q2_scr, send_sems, recv_sems, via_send, via_recv = scr
        x, y, c = _place()
        base = 2 * x + y
        xn, yn, _ = _other_chips(x, y)
        pos, pc = i >> 1, i & 1
        pi = chip_at(pos, base)

        def to_sibling(chip, slot):
            return pltpu.make_async_remote_copy(
                src_ref=stage.at[slot], dst_ref=land.at[chip], send_sem=send_sems.at[chip],
                recv_sem=recv_sems.at[chip], device_id=(x, y, 1 - c), device_id_type=MESH)

        def via(k):
            return pltpu.make_async_remote_copy(
                src_ref=q2_scr.at[pl.ds(k * h, h)], dst_ref=rel_ref.at[k], send_sem=via_send.at[k],
                recv_sem=via_recv.at[k], device_id=(*(xn, yn)[k], c), device_id_type=MESH)

        @pl.when(i == 0)
        def _():
            _transpose_into(at_scr, a_ref, TRANSPOSE_ROWS)

        res = _dot(at_scr[...], b_ref[...]).astype(BF16)

        @pl.when(pc != c)
        def _():
            @pl.when(pos >= 2)
            def _():
                to_sibling(chip_at(pos - 2, base), pos & 1).wait_send()

            stage[pos & 1] = res
            to_sibling(pi, pos & 1).start()

        @pl.when(pc == c)
        def _():
            mine[pi] = res

        @pl.when(i == 1)
        def _():
            dg = chip_at(0, base)
            to_sibling(dg, 0).wait_recv()
            q2 = (mine[dg].astype(F32) + land[dg].astype(F32)).astype(BF16)
            q2_scr[...] = q2
            q_ref[2] = q2
            via(0).start()
            via(1).start()

        @pl.when(i == NDEV - 1)
        def _():
            for pos_ in (1, 2, 3):
                to_sibling(chip_at(pos_, base), 0).wait_recv()
            to_sibling(chip_at(2, base), 0).wait_send()
            to_sibling(chip_at(3, base), 1).wait_send()
            for k in range(2):
                via(k).wait_recv()
            for k in range(2):
                via(k).wait_send()
            for j, chip in enumerate((base ^ 2, base ^ 1)):
                q_ref[j] = (mine[chip].astype(F32) + land[chip].astype(F32)).astype(BF16)
            acc_ref[...] = mine[base].astype(F32) + land[base].astype(F32)

    def b_block(j):
        base = 2 * lax.axis_index("x") + lax.axis_index("y")
        return (0, 2 * chip_at(j >> 1, base) + (j & 1))

    blk_vmem = lambda k: pltpu.VMEM((k, r, cd), BF16)
    (q, acc, rel), job_out = _call(
        main, jobs, name=name, grid=(NDEV,), relay_step=relay_step, ins=[a, b],
        in_specs=[_VMEM, pl.BlockSpec((s_len, cd), b_block)],
        out_shape=[_sds((NCHIP_OTHER, r, cd), BF16), _sds((r, cd), F32), _sds((2, h, cd), BF16)],
        out_specs=[pl.BlockSpec((NCHIP_OTHER, r, cd), lambda j: (0, 0, 0)), pl.BlockSpec((r, cd), lambda j: (0, 0)), _HBM],
        scratch=[pltpu.VMEM((m, s_len), BF16), blk_vmem(2), blk_vmem(4), blk_vmem(4), pltpu.VMEM((r, cd), BF16),
                 pltpu.SemaphoreType.DMA((4,)), pltpu.SemaphoreType.DMA((4,)), pltpu.SemaphoreType.DMA((2,)),
                 pltpu.SemaphoreType.DMA((2,))])
    return q, acc, rel, job_out


class _ExchangeRest:
    def __init__(self, q, relayed):
        _, r, cd = q.shape
        half = (2, r // 2, cd)
        self.ins, self.in_specs = [q, relayed], [_HBM, _HBM]
        self.out_shape, self.out_specs = [_sds((2, r, cd), q.dtype)], [_HBM]
        self.scratch = [pltpu.VMEM(half, q.dtype), pltpu.VMEM(half, q.dtype), pltpu.VMEM(half, q.dtype),
                        pltpu.SemaphoreType.DMA((4,)), pltpu.SemaphoreType.DMA((4,)), pltpu.SemaphoreType.DMA((4,))]

    def ops(self, ins, outs, scr):
        (q, rel_in), (land,) = ins, outs
        own, rel, comb, send_sems, recv_sems, local_sems = scr
        h = q.shape[1] // 2
        x, y, c = _place()
        xn, yn, _ = _other_chips(x, y)
        h0, h1 = pl.ds(0, h), pl.ds(h, h)

        def remote(k, src, dst, chip):
            return pltpu.make_async_remote_copy(src_ref=src, dst_ref=dst, send_sem=send_sems.at[k],
                                                recv_sem=recv_sems.at[k], device_id=(*chip, c), device_id_type=MESH)

        def sends():
            return [remote(0, q.at[0, h0], land.at[0, h0], xn), remote(1, q.at[1, h1], land.at[1, h1], yn),
                    remote(2, comb.at[0], land.at[1, h0], yn), remote(3, comb.at[1], land.at[0, h1], xn)]

        def loads():
            return [pltpu.make_async_copy(q.at[1, h0], own.at[0], local_sems.at[0]),
                    pltpu.make_async_copy(q.at[0, h1], own.at[1], local_sems.at[1]),
                    pltpu.make_async_copy(rel_in.at[0], rel.at[0], local_sems.at[2]),
                    pltpu.make_async_copy(rel_in.at[1], rel.at[1], local_sems.at[3])]

        def start():
            cps, lds = sends(), loads()
            for ld in lds:
                ld.start()
            cps[0].start()
            cps[1].start()
            for ld in lds:
                ld.wait()
            for k in range(2):
                comb[k] = (own[k].astype(F32) + rel[k].astype(F32)).astype(comb.dtype)
            cps[2].start()
            cps[3].start()

        def finish():
            cps = sends()
            for cp in cps:
                cp.wait_recv()
            for cp in cps:
                cp.wait_send()

        return start, lambda: None, finish


def _adam_math(w, g, m, v):
    m = B1 * m + (1.0 - B1) * g
    v = B2 * v + (1.0 - B2) * (g * g)
    m_hat = m / (1.0 - B1 ** STEP)
    v_hat = v / (1.0 - B2 ** STEP)
    delta = (-LR) * (m_hat / (jnp.sqrt(v_hat) + ADAM_EPS) + WD * w)
    return delta, m, v


def _adam_big(w, acc, land, m, v, name):
    r, cd = w.shape
    rb = ADAM_ROWS if r % ADAM_ROWS == 0 else r
    nland = land.shape[0]

    def body(w_ref, acc_ref, land_ref, m_ref, v_ref, g_ref, d_ref, mo_ref, vo_ref):
        g = acc_ref[...]
        for j in range(nland):
            g = g + land_ref[j].astype(F32)
        g_ref[...] = g
        d_ref[...], mo_ref[...], vo_ref[...] = _adam_math(w_ref[...], g, m_ref[...], v_ref[...])

    blk = pl.BlockSpec((rb, cd), lambda i: (i, 0))
    blk3 = pl.BlockSpec((nland, rb, cd), lambda i: (0, i, 0))
    return pl.pallas_call(
        body, name=name, grid=(r // rb,), in_specs=[blk, blk, blk3, blk, blk], out_specs=[blk] * 4,
        out_shape=[_sds((r, cd), F32)] * 4,
        compiler_params=_params(dimension_semantics=("arbitrary",)),
    )(w, acc, land, m, v)


def _adam_small(groups):
    n = len(groups)

    def body(*refs):
        ins, outs = refs[:4 * n], refs[4 * n:]
        for k in range(n):
            w_ref, g_ref, m_ref, v_ref = ins[4 * k:4 * k + 4]
            d, mo, vo = _adam_math(w_ref[...], g_ref[...], m_ref[...], v_ref[...])
            outs[3 * k][...] = d
            outs[3 * k + 1][...] = mo
            outs[3 * k + 2][...] = vo

    flat = [a for grp in groups for a in grp]
    shapes = [_sds(grp[0].shape, F32) for grp in groups for _ in range(3)]
    res = pl.pallas_call(
        body, name="adam_small", in_specs=[_VMEM] * (4 * n), out_specs=[_VMEM] * (3 * n), out_shape=shapes,
        compiler_params=_params(),
    )(*flat)
    return [tuple(res[3 * k:3 * k + 3]) for k in range(n)]


TM_FWD_A = 256
RELAY_STEP_FWD_A = 4
RELAY_STEP_FWD_B = 2
TM_BWD_A = 256
RELAY_STEP_BWD_A = 3
TM_BWD_A_IN = 512
RELAY_STEP_BWD_A_IN = 2
RELAY_STEP_WGRAD_A_IN = 2
TM_FWD_B = 256
TM_HEAD = 512
TM_BWD_B = 256


def _pack(parts, rows):
    flat = jnp.concatenate([p.reshape(-1) for p in parts])
    return jnp.pad(flat, (0, NDEV * rows * LANES - flat.shape[0])).reshape(NDEV, rows, LANES)


def _unpack(packed, shapes):
    flat, out, off = packed.reshape(-1), [], 0
    for s in shapes:
        size = 1
        for d in s:
            size *= d
        out.append(flat[off:off + size].reshape(s))
        off += size
    return out


def kernel(x, norm_w, a_w_in, a_ln_w, a_ln_b, a_w_s, a_b_s, a_w_out, b_w_in, b_conv_w, b_conv_b, b_gate_a_w, b_gate_a_b, b_gate_x_w, b_gate_x_b, b_lambda, b_w_out, norm_f_w, loss_target, m_norm_w, m_a_w_in, m_a_ln_w, m_a_ln_b, m_a_w_s, m_a_b_s, m_a_w_out, m_b_w_in, m_b_conv_w, m_b_conv_b, m_b_gate_a_w, m_b_gate_a_b, m_b_gate_x_w, m_b_gate_x_b, m_b_lambda, m_b_w_out, m_norm_f_w, v_norm_w, v_a_w_in, v_a_ln_w, v_a_ln_b, v_a_w_s, v_a_b_s, v_a_w_out, v_b_w_in, v_b_conv_w, v_b_conv_b, v_b_gate_a_w, v_b_gate_a_b, v_b_gate_x_w, v_b_gate_x_b, v_b_lambda, v_b_w_out, v_norm_f_w):
    me = 4 * lax.axis_index("x") + 2 * lax.axis_index("y") + lax.axis_index("c")
    xs, tgt = x[0], loss_target[0]
    nw0, nw1, nfw = norm_w[0:1], norm_w[1:2], norm_f_w.reshape(1, D)
    w_s, bst = a_w_s[0], a_b_s[0].T
    gcat = jnp.concatenate([b_gate_a_w[0], b_gate_x_w[0]], axis=-1).astype(BF16)

    p8_shard = jnp.concatenate([b_conv_w[0], b_conv_b, b_gate_a_b, b_gate_x_b, b_lambda], axis=0)
    ((win_a8, p8_all),) = _comm_only([_Gather([a_w_in[0], p8_shard], [BF16, F32])], "gather_first")
    p8 = jnp.transpose(p8_all, (1, 0, 2)).reshape(SUBLANES, BW)

    (z, h0, ya), ((wout_a8, win_b8),) = _fwd_a(
        xs, nw0, win_a8, a_ln_w, a_ln_b, w_s, bst, [_Gather([a_w_out[0], b_w_in[0]], [BF16, BF16])],
        tm=TM_FWD_A, relay_step=RELAY_STEP_FWD_A)
    wout_a = wout_a8.reshape(AW, D)
    (x1, zb, hs, h1, yb, *saved_b), ((wout_b8,),) = _fwd_b(
        xs, ya, wout_a, nw1, win_b8, p8, gcat, [_Gather([b_w_out[0]], [BF16])],
        tm=TM_FWD_B, relay_step=RELAY_STEP_FWD_B)
    wout_b = wout_b8.reshape(BW, D)
    dx2, dx2b, loss, g_nfw = _head(x1, yb, wout_b, nfw, tgt, tm=TM_HEAD)

    dx1, dx1b, dzb, g_p8, g_ga, g_gx, g_nw1 = _bwd_b(dx2, zb, hs, x1, saved_b, nw1, win_b8, p8, gcat, wout_b,
                                                     tm=TM_BWD_B)
    q_wout_b, acc_wout_b, _ = _wgrad(yb, dx2b, [], by_rows=True, per=2, name="wgrad_b_out")
    shapes_b = [(1, D), (1, D), (SUBLANES, BW), (1, 1)]
    pack_b = _pack([g_nfw, g_nw1, g_p8, loss], 16)
    small_b = _InChip([g_ga.reshape(NDEV, -1, HD), g_gx.reshape(NDEV, -1, HD), pack_b])
    q_win_b, acc_win_b, (sm_b, (l_wout_b,)) = _wgrad(h1, dzb, [small_b, _Exchange([q_wout_b])], by_rows=False, per=1,
                                                      name="wgrad_b_in")
    qs_b, accs_b = sm_b[:3], sm_b[3:]

    (dz, g_lnw, g_lnb, g_ws, g_bst), (lands_b, (l_win_b,)) = _bwd_a(
        dx1b, z, a_ln_w, a_ln_b, w_s, bst, wout_a, [_Exchange(qs_b), _ExchangeVia(q_win_b)],
        tm=TM_BWD_A, relay_step=RELAY_STEP_BWD_A)
    shapes_a = [(1, AW), (1, AW), (CH, G)]
    pack_a = _pack([g_lnw, g_lnb, g_bst], 8)
    q_wout_a, acc_wout_a, (red_b, sm_a) = _wgrad(
        ya, dx1b, [_SumGather(accs_b, lands_b), _InChip([g_ws, pack_a])], by_rows=True, per=2,
        name="wgrad_a_out", relay_step=1)
    qs_a, accs_a = sm_a[:2], sm_a[2:]
    q_win_a, acc_win_a, rel_a, (lands_a, (l_wout_a,)) = _wgrad_cols_early(
        h0, dz, [_Exchange(qs_a), _ExchangeVia(q_wout_a)], name="wgrad_a_in", relay_step=RELAY_STEP_WGRAD_A_IN)
    (gx, g_nw0), (red_a, (l_win_a,)) = _bwd_a_in(
        dz, dx1, xs, nw0, win_a8, [_SumGather(accs_a, lands_a), _ExchangeRest(q_win_a, rel_a)],
        tm=TM_BWD_A_IN, relay_step=RELAY_STEP_BWD_A_IN)

    r_ga, r_gx, r_pack_b = red_b
    r_nfw, r_nw1, r_p8, loss = _unpack(r_pack_b, shapes_b)
    r_ws, r_pack_a = red_a
    r_lnw, r_lnb, r_bst = _unpack(r_pack_a, shapes_a)
    g_p8 = lax.dynamic_slice_in_dim(r_p8, me * (BW // NDEV), BW // NDEV, axis=1)
    loss = loss[0, 0]

    weights = dict(norm_w=norm_w, a_w_in=a_w_in, a_ln_w=a_ln_w, a_ln_b=a_ln_b, a_w_s=a_w_s, a_b_s=a_b_s, a_w_out=a_w_out,
                   b_w_in=b_w_in, b_conv_w=b_conv_w, b_conv_b=b_conv_b, b_gate_a_w=b_gate_a_w, b_gate_a_b=b_gate_a_b,
                   b_gate_x_w=b_gate_x_w, b_gate_x_b=b_gate_x_b, b_lambda=b_lambda, b_w_out=b_w_out, norm_f_w=norm_f_w)
    mom1 = dict(norm_w=m_norm_w, a_w_in=m_a_w_in, a_ln_w=m_a_ln_w, a_ln_b=m_a_ln_b, a_w_s=m_a_w_s, a_b_s=m_a_b_s,
                a_w_out=m_a_w_out, b_w_in=m_b_w_in, b_conv_w=m_b_conv_w, b_conv_b=m_b_conv_b, b_gate_a_w=m_b_gate_a_w,
                b_gate_a_b=m_b_gate_a_b, b_gate_x_w=m_b_gate_x_w, b_gate_x_b=m_b_gate_x_b, b_lambda=m_b_lambda,
                b_w_out=m_b_w_out, norm_f_w=m_norm_f_w)
    mom2 = dict(norm_w=v_norm_w, a_w_in=v_a_w_in, a_ln_w=v_a_ln_w, a_ln_b=v_a_ln_b, a_w_s=v_a_w_s, a_b_s=v_a_b_s,
                a_w_out=v_a_w_out, b_w_in=v_b_w_in, b_conv_w=v_b_conv_w, b_conv_b=v_b_conv_b, b_gate_a_w=v_b_gate_a_w,
                b_gate_a_b=v_b_gate_a_b, b_gate_x_w=v_b_gate_x_w, b_gate_x_b=v_b_gate_x_b, b_lambda=v_b_lambda,
                b_w_out=v_b_w_out, norm_f_w=v_norm_f_w)
    names = list(weights)

    def as2d(a):
        return a.reshape(-1, a.shape[-1])

    upd, grads = {}, {}
    for k, acc, land in (("a_w_in", acc_win_a, l_win_a), ("a_w_out", acc_wout_a, l_wout_a),
                         ("b_w_in", acc_win_b, l_win_b), ("b_w_out", acc_wout_b, l_wout_b)):
        g, d, mo, vo = _adam_big(as2d(weights[k]), acc, land, as2d(mom1[k]), as2d(mom2[k]), "adam_" + k)
        grads[k] = g[None]
        upd[k] = (d, mo, vo)
    grads.update(
        norm_w=jnp.concatenate([g_nw0, r_nw1], axis=0), a_ln_w=r_lnw, a_ln_b=r_lnb,
        a_w_s=r_ws.reshape(1, G, CH, CH), a_b_s=r_bst.T[None],
        b_conv_w=g_p8[None, 0:4], b_conv_b=g_p8[4:5], b_gate_a_w=r_ga.reshape(1, BH, HD, HD), b_gate_a_b=g_p8[5:6],
        b_gate_x_w=r_gx.reshape(1, BH, HD, HD), b_gate_x_b=g_p8[6:7], b_lambda=g_p8[7:8], norm_f_w=r_nfw.reshape(D))
    small_names = [k for k in names if k not in upd]
    res = _adam_small([(as2d(weights[k]), as2d(grads[k]), as2d(mom1[k]), as2d(mom2[k])) for k in small_names])
    for k, r3 in zip(small_names, res):
        upd[k] = r3
    deltas = [upd[k][0].reshape(weights[k].shape) for k in names]
    new_m = [upd[k][1].reshape(weights[k].shape) for k in names]
    new_v = [upd[k][2].reshape(weights[k].shape) for k in names]
    return (loss, gx[None], *[grads[k] for k in names], *deltas, *new_m, *new_v)
```

```python
import jax
import jax.numpy as jnp
from jax import lax
from jax.experimental import pallas as pl
from jax.experimental.pallas import tpu as pltpu

F32 = jnp.float32
BF16 = jnp.bfloat16
MESH = pl.DeviceIdType.MESH

NDEV = 8
NCHIP_OTHER = 3
D = 1024
AW = 2048
G = 8
GD = AW // G
CH = 128
BW = 1536
BH = 12
HD = BW // BH
CA = 3 * AW // NDEV
CB = 2 * BW // NDEV
RMS_EPS = 1e-6
LN_EPS = 1e-5
RG_C = 8.0
LR, B1, B2, ADAM_EPS, WD, STEP = 0.001, 0.9, 0.999, 1e-08, 0.01, 10
V7X_VMEM_BYTES = 64 * 1024 * 1024
VMEM_LIMIT = V7X_VMEM_BYTES - 8 * 1024 * 1024
SUBLANES = 8
LANES = 128
BF16_ROWS = 16
TRANSPOSE_ROWS = 256
ADAM_ROWS = 512
GELU_C = 0.7978845608028654
GELU_K = 0.044715

_VMEM = pl.BlockSpec(memory_space=pltpu.VMEM)
_HBM = pl.BlockSpec(memory_space=pltpu.HBM)


def _sds(shape, dtype):
    return jax.ShapeDtypeStruct(tuple(shape), dtype)


def _params(**kw):
    return pltpu.CompilerParams(vmem_limit_bytes=VMEM_LIMIT, **kw)


def _gelu_t(z):
    p = 0.5 * jnp.tanh(z * (GELU_C + (GELU_C * GELU_K) * (z * z))) + 0.5
    return z * p, p


def _dgelu(z, p):
    return p * (1.0 + (z * (1.0 - p)) * (2.0 * GELU_C + (6.0 * GELU_C * GELU_K) * (z * z)))


def _sigmoid(v):
    return 0.5 * jnp.tanh(0.5 * v) + 0.5


def _softplus_neg(lam):
    return jnp.maximum(-lam, 0.0) + jnp.log1p(jnp.exp(-jnp.abs(lam)))


def _dot(a, b):
    return jnp.dot(a, b, preferred_element_type=F32)


def _dot_nt(a, b):
    return lax.dot_general(a, b, (((1,), (1,)), ((), ())), preferred_element_type=F32)


def _rowsum(v):
    return jnp.sum(v, axis=0, keepdims=True)


def _causal_mask():
    r = lax.broadcasted_iota(jnp.int32, (CH, CH), 0)
    c = lax.broadcasted_iota(jnp.int32, (CH, CH), 1)
    return r >= c


def _rms(x):
    return lax.rsqrt(jnp.mean(x * x, axis=-1, keepdims=True) + RMS_EPS)


def _rms_bwd(dh, x, r, nw):
    gy = dh * nw
    return r * gy - x * (r * r * r) * jnp.mean(gy * x, axis=-1, keepdims=True)


def _place():
    return lax.axis_index("x"), lax.axis_index("y"), lax.axis_index("c")


def _other_chips(x, y):
    return [(1 - x, y), (x, 1 - y), (1 - x, 1 - y)]


GATHER_SLOTS = 10


def _gather_ops(ins, outs, send_sems, recv_sems, local_sems):
    n = len(ins)
    x, y, c = _place()
    sibling = (x, y, 1 - c)
    xn, yn, dg = _other_chips(x, y)
    split = [ins[i].shape[0] % (2 * BF16_ROWS) == 0 for i in range(n)]

    def blk(chip, core):
        return 4 * chip[0] + 2 * chip[1] + core

    me = blk((x, y), c)

    def part(ref, i, half):
        if half is None:
            return ref
        h = ins[i].shape[0] // 2
        return ref.at[pl.ds(half * h, h)]

    def copy(i, k, block, to, half=None, src=None):
        dst = part(outs[i].at[block], i, half)
        return pltpu.make_async_remote_copy(
            src_ref=dst if src is None else part(src, i, half), dst_ref=dst,
            send_sem=send_sems.at[k, i], recv_sem=recv_sems.at[k, i], device_id=to, device_id_type=MESH)

    def first_copies():
        mine = [pltpu.make_async_copy(ins[i], outs[i].at[me], local_sems.at[i]) for i in range(n)]
        first = []
        for i in range(n):
            first.append(copy(i, 0, me, sibling, src=ins[i]))
            if split[i]:
                first.append(copy(i, 1, me, (*xn, c), 0, ins[i]))
                first.append(copy(i, 3, me, (*yn, c), 1, ins[i]))
                first.append(copy(i, 2, me, (*xn, c), 1, ins[i]))
                first.append(copy(i, 4, me, (*yn, c), 0, ins[i]))
            else:
                first.append(copy(i, 1, me, (*xn, c), None, ins[i]))
                first.append(copy(i, 3, me, (*yn, c), None, ins[i]))
                first.append(copy(i, 5, me, (*dg, c), None, ins[i]))
        return mine, first

    def onward():
        out = []
        for i in range(n):
            if split[i]:
                out.append(copy(i, 5, blk(xn, c), (*yn, c), 0))
                out.append(copy(i, 6, blk(yn, c), (*xn, c), 1))
        return out

    def start():
        mine, first = first_copies()
        for cp in mine + first:
            cp.start()

    def relay():
        sends = onward()
        for i in range(n):
            if split[i]:
                copy(i, 1, blk(xn, c), sibling, 0).wait_recv()
                sends.pop(0).start()
                copy(i, 3, blk(yn, c), sibling, 1).wait_recv()
                sends.pop(0).start()

    def finish():
        mine, first = first_copies()
        passed = []

        def pass_on(i, j, chip):
            fwd = copy(i, 7 + j, blk(chip, c), sibling)
            fwd.start()
            passed.append(fwd)

        for i in range(n):
            if split[i]:
                copy(i, 2, blk(xn, c), sibling, 1).wait_recv()
                pass_on(i, 0, xn)
                copy(i, 4, blk(yn, c), sibling, 0).wait_recv()
                pass_on(i, 1, yn)
                copy(i, 5, blk(dg, c), sibling, 0).wait_recv()
                copy(i, 6, blk(dg, c), sibling, 1).wait_recv()
                pass_on(i, 2, dg)
            else:
                copy(i, 1, blk(xn, c), sibling).wait_recv()
                pass_on(i, 0, xn)
                copy(i, 3, blk(yn, c), sibling).wait_recv()
                pass_on(i, 1, yn)
                copy(i, 5, blk(dg, c), sibling).wait_recv()
                pass_on(i, 2, dg)
        for i in range(n):
            copy(i, 0, blk((x, y), 1 - c), sibling).wait_recv()
            for j, chip in enumerate((xn, yn, dg)):
                copy(i, 7 + j, blk(chip, 1 - c), sibling).wait_recv()
        for cp in first + passed + onward():
            cp.wait_send()
        for cp in mine:
            cp.wait()

    return start, relay, finish


def _gather_sems(n):
    return [pltpu.SemaphoreType.DMA((GATHER_SLOTS, n)), pltpu.SemaphoreType.DMA((GATHER_SLOTS, n)),
            pltpu.SemaphoreType.DMA((n,))]


class _Gather:
    def __init__(self, shards, as_dtypes=None):
        n = len(shards)
        dts = [s.dtype for s in shards] if as_dtypes is None else list(as_dtypes)
        self.cast = [jnp.dtype(d) != s.dtype for d, s in zip(dts, shards)]
        self.ins = list(shards)
        self.in_specs = [_VMEM if c else _HBM for c in self.cast]
        self.out_shape = [_sds((NDEV,) + s.shape, d) for s, d in zip(shards, dts)]
        self.out_specs = [_HBM] * n
        self.scratch = [pltpu.VMEM(s.shape, d) for s, d, c in zip(shards, dts, self.cast) if c] + _gather_sems(n)

    def ops(self, ins, outs, scr):
        ncast = sum(self.cast)
        staged = iter(scr[:ncast])
        srcs = [next(staged) if c else ref for c, ref in zip(self.cast, ins)]
        start, relay, finish = _gather_ops(srcs, outs, *scr[ncast:])

        def cast_and_start():
            for c, ref, src in zip(self.cast, ins, srcs):
                if c:
                    src[...] = ref[...].astype(src.dtype)
            start()

        return cast_and_start, relay, finish


class _Exchange:
    def __init__(self, qs):
        n = len(qs)
        self.ins, self.in_specs = list(qs), [_HBM] * n
        self.out_shape = [_sds(q.shape, q.dtype) for q in qs]
        self.out_specs = [_HBM] * n
        self.scratch = [pltpu.SemaphoreType.DMA((NCHIP_OTHER, n)), pltpu.SemaphoreType.DMA((NCHIP_OTHER, n))]

    def ops(self, ins, outs, scr):
        send_sems, recv_sems = scr
        n = len(ins)
        x, y, c = _place()
        chips = _other_chips(x, y)

        def copies():
            return [pltpu.make_async_remote_copy(
                src_ref=ins[i].at[j], dst_ref=outs[i].at[j], send_sem=send_sems.at[j, i],
                recv_sem=recv_sems.at[j, i], device_id=(*chips[j], c), device_id_type=MESH)
                for i in range(n) for j in range(NCHIP_OTHER)]

        def start():
            for cp in copies():
                cp.start()

        def finish():
            cps = copies()
            for cp in cps:
                cp.wait_recv()
            for cp in cps:
                cp.wait_send()

        return start, lambda: None, finish


class _ExchangeVia:
    def __init__(self, q):
        _, r, cd = q.shape
        half = (2, r // 2, cd)
        self.ins, self.in_specs = [q], [_HBM]
        self.out_shape, self.out_specs = [_sds((2, r, cd), q.dtype)], [_HBM]
        self.scratch = [pltpu.VMEM(half, q.dtype), pltpu.VMEM(half, q.dtype), pltpu.VMEM(half, q.dtype),
                        pltpu.SemaphoreType.DMA((6,)), pltpu.SemaphoreType.DMA((6,)), pltpu.SemaphoreType.DMA((2,))]

    def ops(self, ins, outs, scr):
        (q,), (land,) = ins, outs
        relayed, own, comb, send_sems, recv_sems, local_sems = scr
        h = q.shape[1] // 2
        x, y, c = _place()
        xn, yn, _ = _other_chips(x, y)
        h0, h1 = pl.ds(0, h), pl.ds(h, h)

        def remote(k, src, dst, chip):
            return pltpu.make_async_remote_copy(src_ref=src, dst_ref=dst, send_sem=send_sems.at[k],
                                                recv_sem=recv_sems.at[k], device_id=(*chip, c), device_id_type=MESH)

        def via():
            return [remote(2, q.at[2, h0], relayed.at[0], xn), remote(3, q.at[2, h1], relayed.at[1], yn)]

        def direct():
            return [remote(0, q.at[0, h0], land.at[0, h0], xn), remote(1, q.at[1, h1], land.at[1, h1], yn)]

        def second():
            return [remote(4, comb.at[0], land.at[1, h0], yn), remote(5, comb.at[1], land.at[0, h1], xn)]

        def mine():
            return [pltpu.make_async_copy(q.at[1, h0], own.at[0], local_sems.at[0]),
                    pltpu.make_async_copy(q.at[0, h1], own.at[1], local_sems.at[1])]

        def start():
            for cp in via() + direct() + mine():
                cp.start()

        def relay():
            arrived, loaded, onward = via(), mine(), second()
            for k in range(2):
                arrived[k].wait_recv()
                loaded[k].wait()
                comb[k] = (own[k].astype(F32) + relayed[k].astype(F32)).astype(comb.dtype)
                onward[k].start()

        def finish():
            landing = direct() + second()
            for cp in landing:
                cp.wait_recv()
            for cp in via() + landing:
                cp.wait_send()

        return start, relay, finish


class _SumGather:
    def __init__(self, accs, lands):
        n = len(accs)
        self.n = n
        self.ins, self.in_specs = list(accs) + list(lands), [_VMEM] * (2 * n)
        self.out_shape = [_sds((NDEV,) + a.shape, a.dtype) for a in accs]
        self.out_specs = [_HBM] * n
        self.scratch = [pltpu.VMEM(a.shape, a.dtype) for a in accs] + _gather_sems(n)

    def ops(self, ins, outs, scr):
        n = self.n
        accs, lands, mine = ins[:n], ins[n:], scr[:n]
        g_start, relay, finish = _gather_ops(mine, outs, *scr[n:])

        def start():
            for i in range(n):
                mine[i][...] = accs[i][...] + lands[i][0] + lands[i][1] + lands[i][2]
            g_start()

        return start, relay, finish


def _call(main, jobs, *, name, grid, ins, in_specs, out_shape, out_specs, scratch, relay_step=0):
    nsteps = grid[0] if grid else 1
    n_in, n_out, n_scr = len(ins), len(out_shape), len(scratch)

    def body(*refs):
        pos = [0]

        def take(k):
            r = refs[pos[0]:pos[0] + k]
            pos[0] += k
            return r

        m_in = take(n_in)
        j_in = [take(len(j.ins)) for j in jobs]
        m_out = take(n_out)
        j_out = [take(len(j.out_shape)) for j in jobs]
        m_scr = take(n_scr)
        j_scr = [take(len(j.scratch)) for j in jobs]
        ops = [j.ops(a, b, s) for j, a, b, s in zip(jobs, j_in, j_out, j_scr)]
        i = pl.program_id(0) if grid else 0
        if not grid:
            for o in ops:
                o[0]()
            main(i, m_in, m_out, m_scr)
            for o in ops:
                o[1]()
            for o in ops:
                o[2]()
            return

        if ops:
            @pl.when(i == 0)
            def _():
                for o in ops:
                    o[0]()

        main(i, m_in, m_out, m_scr)

        if ops:
            @pl.when(i == min(relay_step, nsteps - 1))
            def _():
                for o in ops:
                    o[1]()

            @pl.when(i == nsteps - 1)
            def _():
                for o in ops:
                    o[2]()

    extra = dict(dimension_semantics=("arbitrary",)) if grid else {}
    res = pl.pallas_call(
        body, name=name, grid=grid,
        in_specs=list(in_specs) + [s for j in jobs for s in j.in_specs],
        out_specs=list(out_specs) + [s for j in jobs for s in j.out_specs],
        out_shape=list(out_shape) + [s for j in jobs for s in j.out_shape],
        scratch_shapes=list(scratch) + [s for j in jobs for s in j.scratch],
        compiler_params=_params(**extra),
    )(*ins, *[a for j in jobs for a in j.ins])
    main_out, rest, job_out = res[:n_out], res[n_out:], []
    for j in jobs:
        k = len(j.out_shape)
        job_out.append(rest[:k])
        rest = rest[k:]
    return main_out, job_out


def _comm_only(jobs, name):
    _, job_out = _call(lambda i, a, b, s: None, jobs, name=name, grid=(), ins=[], in_specs=[], out_shape=[],
                       out_specs=[], scratch=[])
    return job_out


class _InChip:
    def __init__(self, ps):
        n = len(ps)
        self.n = n
        blk = [p.shape[1:] for p in ps]
        self.ins, self.in_specs = list(ps), [_HBM] * n
        self.out_shape = [_sds((NCHIP_OTHER,) + b, p.dtype) for b, p in zip(blk, ps)] + [_sds(b, F32) for b in blk]
        self.out_specs = [_VMEM] * (2 * n)
        self.scratch = ([pltpu.VMEM((4,) + b, p.dtype) for b, p in zip(blk, ps)] * 2
                        + [pltpu.SemaphoreType.DMA((4, n))] * 3)

    def ops(self, ins, outs, scr):
        n = self.n
        q_refs, acc_refs = outs[:n], outs[n:]
        mines, lands = scr[:n], scr[n:2 * n]
        send_sems, recv_sems, local_sems = scr[2 * n:]
        x, y, c = _place()
        sibling = (x, y, 1 - c)

        def copies():
            out = []
            for i in range(n):
                for pi in range(4):
                    loc = pltpu.make_async_copy(ins[i].at[2 * pi + c], mines[i].at[pi], local_sems.at[pi, i])
                    cp = pltpu.make_async_remote_copy(
                        src_ref=ins[i].at[2 * pi + (1 - c)], dst_ref=lands[i].at[pi],
                        send_sem=send_sems.at[pi, i], recv_sem=recv_sems.at[pi, i],
                        device_id=sibling, device_id_type=MESH)
                    out.append((loc, cp))
            return out

        def start():
            for loc, cp in copies():
                loc.start()
                cp.start()

        def finish():
            pairs = copies()
            for loc, cp in pairs:
                loc.wait()
                cp.wait_recv()
            for i in range(n):
                _chip_sums(mines[i], lands[i], q_refs[i], acc_refs[i], x, y)
            for _, cp in pairs:
                cp.wait_send()

        return start, lambda: None, finish


def _chip_sums(mine, land, q_ref, acc_ref, x, y):
    for j, (qx, qy) in enumerate(_other_chips(x, y)):
        qi = 2 * qx + qy
        q_ref[j] = (mine[qi].astype(F32) + land[qi].astype(F32)).astype(q_ref.dtype)
    mi = 2 * x + y
    acc_ref[...] = mine[mi].astype(F32) + land[mi].astype(F32)


def _direct_sum(v, buf, send_sems, recv_sems):
    x, y, c = _place()
    me = 4 * x + 2 * y + c
    buf[me] = v
    cps = []
    for k in range(1, NDEV):
        fx, fy, fc = (k >> 2) & 1, (k >> 1) & 1, k & 1
        peer = ((1 - x) if fx else x, (1 - y) if fy else y, (1 - c) if fc else c)
        cps.append((peer, pltpu.make_async_remote_copy(
            src_ref=buf.at[me], dst_ref=buf.at[me], send_sem=send_sems.at[k - 1], recv_sem=recv_sems.at[k - 1],
            device_id=peer, device_id_type=MESH)))
    for _, cp in cps:
        cp.start()
    for k, (peer, _) in enumerate(cps):
        theirs = 4 * peer[0] + 2 * peer[1] + peer[2]
        pltpu.make_async_remote_copy(
            src_ref=buf.at[theirs], dst_ref=buf.at[theirs], send_sem=send_sems.at[k], recv_sem=recv_sems.at[k],
            device_id=peer, device_id_type=MESH).wait_recv()
    acc = buf[0]
    for j in range(1, NDEV):
        acc = acc + buf[j]
    for _, cp in cps:
        cp.wait_send()
    return acc


def _direct_sum_scratch(shape, dtype):
    return [pltpu.VMEM((NDEV,) + tuple(shape), dtype), pltpu.SemaphoreType.DMA((NDEV - 1,)),
            pltpu.SemaphoreType.DMA((NDEV - 1,))]


def _fwd_a(x, nw, win8, lnw, lnb, ws, bst, jobs, *, tm, relay_step):
    s_len = x.shape[0]
    nt = s_len // tm
    nch = tm // CH

    def main(i, ins, outs, scr):
        x_ref, nw_ref, win_ref, lnw_ref, lnb_ref, ws_ref, bst_ref = ins
        z_ref, h_ref, y_ref = outs
        wc_scr, gv_scr = scr

        @pl.when(i == 0)
        def _():
            m = _causal_mask()
            for g in range(G):
                wc_scr[g] = jnp.where(m, ws_ref[g], 0.0).astype(BF16)

        x = x_ref[...]
        h = (x * _rms(x) * nw_ref[...]).astype(BF16)
        h_ref[...] = h
        for k in range(NDEV):
            z_ref[:, k * CA:(k + 1) * CA] = _dot(h, win_ref[k])

        ssum = jnp.zeros((tm, 1), F32)
        for g in range(G):
            gv = _gelu_t(z_ref[:, AW + g * GD:AW + (g + 1) * GD])[0]
            gv_scr[:, g * GD:(g + 1) * GD] = gv
            ssum = ssum + jnp.sum(gv, axis=-1, keepdims=True)
        mu = ssum * (1.0 / AW)
        vsum = jnp.zeros((tm, 1), F32)
        for g in range(G):
            dlt = gv_scr[:, g * GD:(g + 1) * GD] - mu
            vsum = vsum + jnp.sum(dlt * dlt, axis=-1, keepdims=True)
        rstd = lax.rsqrt(vsum * (1.0 / AW) + LN_EPS)

        for g in range(G):
            cs = slice(g * GD, (g + 1) * GD)
            v = (gv_scr[:, cs] - mu) * rstd * lnw_ref[:, cs] + lnb_ref[:, cs]
            vb = v.astype(BF16)
            u = _gelu_t(z_ref[:, cs])[0]
            zg = z_ref[:, 2 * AW + g * GD:2 * AW + (g + 1) * GD]
            sg = zg * _sigmoid(zg)
            for n in range(nch):
                rs = slice(n * CH, (n + 1) * CH)
                s = _dot(wc_scr[g], vb[rs, :]) + bst_ref[:, g:g + 1]
                y_ref[rs, cs] = (u[rs, :] * s * sg[rs, :]).astype(BF16)

    tile = lambda w: pl.BlockSpec((tm, w), lambda i: (i, 0))
    return _call(
        main, jobs, name="fwd_a", grid=(nt,), relay_step=relay_step,
        ins=[x, nw, win8, lnw, lnb, ws, bst], in_specs=[tile(D), _VMEM, _VMEM, _VMEM, _VMEM, _VMEM, _VMEM],
        out_shape=[_sds((s_len, 3 * AW), F32), _sds((s_len, D), BF16), _sds((s_len, AW), BF16)],
        out_specs=[tile(3 * AW), tile(D), tile(AW)],
        scratch=[pltpu.VMEM((G, CH, CH), BF16), pltpu.VMEM((tm, AW), F32)])


def _bwd_a(dx1, z, lnw, lnb, ws, bst, wout, jobs, *, tm, relay_step):
    s_len = dx1.shape[0]
    nt = s_len // tm
    nch = tm // CH

    def main(i, ins, outs, scr):
        dx1_ref, z_ref, lnw_ref, lnb_ref, ws_ref, bst_ref, wout_ref = ins
        dz_ref, glnw_ref, glnb_ref, gws_ref, gbst_ref = outs
        wc_scr, wct_scr, vh_scr, dgv_scr, dy_scr, dv_scr, gbs_acc, gwc_acc = scr

        @pl.when(i == 0)
        def _():
            m = _causal_mask()
            for g in range(G):
                wm = jnp.where(m, ws_ref[g], 0.0)
                wc_scr[g] = wm.astype(BF16)
                wct_scr[g] = wm.T.astype(BF16)
            glnw_ref[...] = jnp.zeros_like(glnw_ref)
            glnb_ref[...] = jnp.zeros_like(glnb_ref)
            gbs_acc[...] = jnp.zeros_like(gbs_acc)
            gwc_acc[...] = jnp.zeros_like(gwc_acc)

        dy_scr[...] = _dot_nt(dx1_ref[...], wout_ref[...])

        ssum = jnp.zeros((tm, 1), F32)
        for g in range(G):
            cs = slice(g * GD, (g + 1) * GD)
            zv = z_ref[:, AW + g * GD:AW + (g + 1) * GD]
            gv, t = _gelu_t(zv)
            vh_scr[:, cs] = gv
            dgv_scr[:, cs] = _dgelu(zv, t)
            ssum = ssum + jnp.sum(gv, axis=-1, keepdims=True)
        mu = ssum * (1.0 / AW)
        vsum = jnp.zeros((tm, 1), F32)
        for g in range(G):
            dlt = vh_scr[:, g * GD:(g + 1) * GD] - mu
            vsum = vsum + jnp.sum(dlt * dlt, axis=-1, keepdims=True)
        rstd = lax.rsqrt(vsum * (1.0 / AW) + LN_EPS)

        m1 = jnp.zeros((tm, 1), F32)
        m2 = jnp.zeros((tm, 1), F32)
        for g in range(G):
            cs = slice(g * GD, (g + 1) * GD)
            gs = slice(2 * AW + g * GD, 2 * AW + (g + 1) * GD)
            vhat = (vh_scr[:, cs] - mu) * rstd
            vh_scr[:, cs] = vhat
            vb = (vhat * lnw_ref[:, cs] + lnb_ref[:, cs]).astype(BF16)
            zu = z_ref[:, cs]
            u, tu = _gelu_t(zu)
            zg = z_ref[:, gs]
            sig = _sigmoid(zg)
            sg = zg * sig
            dy = dy_scr[:, cs]
            dsf = dy * u * sg
            dsb = dsf.astype(BF16)
            dvs = []
            for n in range(nch):
                rs = slice(n * CH, (n + 1) * CH)
                s = _dot(wc_scr[g], vb[rs, :]) + bst_ref[:, g:g + 1]
                dys = dy[rs, :] * s
                dz_ref[rs, cs] = (dys * sg[rs, :] * _dgelu(zu[rs, :], tu[rs, :])).astype(BF16)
                dz_ref[rs, gs] = (dys * u[rs, :] * (sig[rs, :] * (1.0 + zg[rs, :] * (1.0 - sig[rs, :])))).astype(BF16)
                gbs_acc[g] += dsf[rs, :]
                gwc_acc[g] += _dot_nt(dsb[rs, :], vb[rs, :])
                dvs.append(_dot(wct_scr[g], dsb[rs, :]))
            dv = jnp.concatenate(dvs, axis=0) if nch > 1 else dvs[0]
            glnw_ref[:, cs] += _rowsum(dv * vhat)
            glnb_ref[:, cs] += _rowsum(dv)
            dvh = dv * lnw_ref[:, cs]
            dv_scr[:, cs] = dvh
            m1 = m1 + jnp.sum(dvh, axis=-1, keepdims=True)
            m2 = m2 + jnp.sum(dvh * vhat, axis=-1, keepdims=True)
        m1 = m1 * (1.0 / AW)
        m2 = m2 * (1.0 / AW)
        for g in range(G):
            cs = slice(g * GD, (g + 1) * GD)
            dgv = rstd * (dv_scr[:, cs] - m1 - vh_scr[:, cs] * m2)
            dz_ref[:, AW + g * GD:AW + (g + 1) * GD] = (dgv * dgv_scr[:, cs]).astype(BF16)

        @pl.when(i == nt - 1)
        def _():
            m = _causal_mask()
            for g in range(G):
                gws_ref[g] = jnp.where(m, gwc_acc[g], 0.0)
                gbst_ref[:, g:g + 1] = jnp.sum(gbs_acc[g], axis=-1, keepdims=True)

    tile = lambda w: pl.BlockSpec((tm, w), lambda i: (i, 0))
    whole = lambda *s: pl.BlockSpec(s, lambda i: (0,) * len(s))
    big = lambda dt: pltpu.VMEM((tm, AW), dt)
    return _call(
        main, jobs, name="bwd_a", grid=(nt,), relay_step=relay_step,
        ins=[dx1, z, lnw, lnb, ws, bst, wout], in_specs=[tile(D), tile(3 * AW), _VMEM, _VMEM, _VMEM, _VMEM, _VMEM],
        out_shape=[_sds((s_len, 3 * AW), BF16), _sds((1, AW), F32), _sds((1, AW), F32), _sds((G, CH, CH), F32),
                   _sds((CH, G), F32)],
        out_specs=[tile(3 * AW), whole(1, AW), whole(1, AW), whole(G, CH, CH), whole(CH, G)],
        scratch=[pltpu.VMEM((G, CH, CH), BF16), pltpu.VMEM((G, CH, CH), BF16), big(F32), big(F32), big(F32), big(F32),
                 pltpu.VMEM((G, CH, GD), F32), pltpu.VMEM((G, CH, CH), F32)])


def _bwd_a_in(dz, dx1, x, nw, win8, jobs, *, tm, relay_step):
    s_len = x.shape[0]
    nt = s_len // tm

    def main(i, ins, outs, scr):
        dz_ref, dx1_ref, x_ref, nw_ref, win_ref = ins
        gx_ref, gnw_ref = outs

        @pl.when(i == 0)
        def _():
            gnw_ref[...] = jnp.zeros_like(gnw_ref)

        dh = jnp.zeros((tm, D), F32)
        for k in range(NDEV):
            dh = dh + _dot_nt(dz_ref[:, k * CA:(k + 1) * CA], win_ref[k])
        x = x_ref[...]
        r = _rms(x)
        gx_ref[...] = dx1_ref[...] + _rms_bwd(dh, x, r, nw_ref[...])
        gnw_ref[...] += _rowsum(dh * x * r)

        @pl.when(i == nt - 1)
        def _():
            gnw_ref[...] = _direct_sum(gnw_ref[...], *scr)

    tile = lambda w: pl.BlockSpec((tm, w), lambda i: (i, 0))
    return _call(
        main, jobs, name="bwd_a_in", grid=(nt,), relay_step=relay_step,
        ins=[dz, dx1, x, nw, win8], in_specs=[tile(3 * AW), tile(D), tile(D), _VMEM, _VMEM],
        out_shape=[_sds((s_len, D), F32), _sds((1, D), F32)],
        out_specs=[tile(D), pl.BlockSpec((1, D), lambda i: (0, 0))], scratch=_direct_sum_scratch((1, D), F32))


def _conv(p8_ref, cs, xb, xm1, xm2, xm3):
    xc = p8_ref[4:5, cs] + p8_ref[3:4, cs] * xb
    xc = xc + p8_ref[0:1, cs] * xm3
    xc = xc + p8_ref[1:2, cs] * xm2
    return xc + p8_ref[2:3, cs] * xm1


def _gates(p8_ref, gcat_ref, hh, xc):
    cs = slice(hh * HD, (hh + 1) * HD)
    pre = _dot(xc.astype(BF16), gcat_ref[hh])
    r = _sigmoid(pre[:, :HD] + p8_ref[5:6, cs])
    ig = _sigmoid(pre[:, HD:] + p8_ref[6:7, cs])
    sp = _softplus_neg(p8_ref[7:8, cs])
    la = (-RG_C) * r * sp
    a = jnp.exp(la)
    half_log = 0.5 * jnp.log(jnp.tanh(-la) * (1.0 + a * a))
    return r, ig, sp, a, jnp.exp(half_log), jnp.exp(-half_log)


def _scan_rows(a_ref, b_ref, out_ref, carry, tm, reverse):
    row = lax.broadcasted_iota(jnp.int32, (SUBLANES, BW), 0)
    ngrp = tm // SUBLANES

    def step(j, cr):
        jj = (ngrp - 1 - j) if reverse else j
        off = pl.multiple_of(jj * SUBLANES, SUBLANES)
        a = a_ref[pl.ds(off, SUBLANES), :]
        b = b_ref[pl.ds(off, SUBLANES), :]
        for sh in (1, 2, 4):
            if reverse:
                a_s = pltpu.roll(a, SUBLANES - sh, 0)
                b_s = pltpu.roll(b, SUBLANES - sh, 0)
                m = row < SUBLANES - sh
            else:
                a_s = pltpu.roll(a, sh, 0)
                b_s = pltpu.roll(b, sh, 0)
                m = row >= sh
            b = jnp.where(m, a * b_s + b, b)
            a = jnp.where(m, a * a_s, a)
        o = b + a * cr
        out_ref[pl.ds(off, SUBLANES), :] = o
        return o[0:1, :] if reverse else o[SUBLANES - 1:SUBLANES, :]

    return lax.fori_loop(0, ngrp, step, carry)


def _fwd_b(x, ya, wout_a, nw, win8, p8, gcat, jobs, *, tm, relay_step):
    s_len = x.shape[0]
    nt = s_len // tm

    def main(i, ins, outs, scr):
        x_ref, ya_ref, wouta_ref, nw_ref, win_ref, p8_ref, gcat_ref = ins
        x1_ref, zb_ref, hs_ref, h1_ref, yb_ref, xc_ref, a_ref, cc_ref, r_ref, ig_ref, m_ref = outs
        xbe_scr, b_scr, k_scr, carry_scr = scr

        @pl.when(i == 0)
        def _():
            xbe_scr[0:SUBLANES, :] = jnp.zeros((SUBLANES, BW), F32)
            carry_scr[...] = jnp.zeros_like(carry_scr)

        x1 = x_ref[...] + _dot(ya_ref[...], wouta_ref[...])
        x1_ref[...] = x1
        h = (x1 * _rms(x1) * nw_ref[...]).astype(BF16)
        h1_ref[...] = h
        for k in range(NDEV):
            zb_ref[:, k * CB:(k + 1) * CB] = _dot(h, win_ref[k])
        xbe_scr[SUBLANES:SUBLANES + tm, :] = zb_ref[:, :BW]
        for hh in range(BH):
            cs = slice(hh * HD, (hh + 1) * HD)
            xc = _conv(p8_ref, cs, xbe_scr[SUBLANES:SUBLANES + tm, cs], xbe_scr[7:7 + tm, cs],
                       xbe_scr[6:6 + tm, cs], xbe_scr[5:5 + tm, cs])
            r, ig, _, a, mult, rm = _gates(p8_ref, gcat_ref, hh, xc)
            ixc = ig * xc
            xc_ref[:, cs] = xc
            a_ref[:, cs] = a
            r_ref[:, cs] = r.astype(BF16)
            ig_ref[:, cs] = ig.astype(BF16)
            m_ref[:, cs] = mult.astype(BF16)
            b_scr[:, cs] = mult * ixc
            k_scr[:, cs] = ixc * (a * a * rm)
        xbe_scr[0:SUBLANES, :] = xbe_scr[tm:tm + SUBLANES, :]
        carry_scr[...] = _scan_rows(a_ref, b_scr, hs_ref, carry_scr[...], tm, False)
        for hh in range(BH):
            cs = slice(hh * HD, (hh + 1) * HD)
            gt = zb_ref[:, BW + hh * HD:BW + (hh + 1) * HD]
            hsv = hs_ref[:, cs]
            yb_ref[:, cs] = (hsv * (gt * _sigmoid(gt))).astype(BF16)
            cc_ref[:, cs] = (hsv - b_scr[:, cs]) - k_scr[:, cs]

    tile = lambda w: pl.BlockSpec((tm, w), lambda i: (i, 0))
    wide = lambda dt: _sds((s_len, BW), dt)
    return _call(
        main, jobs, name="fwd_b", grid=(nt,), relay_step=relay_step,
        ins=[x, ya, wout_a, nw, win8, p8, gcat], in_specs=[tile(D), tile(AW), _VMEM, _VMEM, _VMEM, _VMEM, _VMEM],
        out_shape=[_sds((s_len, D), F32), _sds((s_len, 2 * BW), F32), wide(F32), _sds((s_len, D), BF16), wide(BF16),
                   wide(F32), wide(F32), wide(F32), wide(BF16), wide(BF16), wide(BF16)],
        out_specs=[tile(D), tile(2 * BW), tile(BW), tile(D)] + [tile(BW)] * 7,
        scratch=[pltpu.VMEM((tm + SUBLANES, BW), F32), pltpu.VMEM((tm, BW), F32), pltpu.VMEM((tm, BW), F32),
                 pltpu.VMEM((1, BW), F32)])


def _head(x1, yb, wout, nfw, tgt, *, tm):
    s_len = x1.shape[0]

    def main(i, ins, outs, scr):
        x1_ref, yb_ref, wout_ref, nfw_ref, t_ref = ins
        dx2_ref, dx2b_ref, loss_ref, gnfw_ref = outs

        @pl.when(i == 0)
        def _():
            loss_ref[...] = jnp.zeros_like(loss_ref)
            gnfw_ref[...] = jnp.zeros_like(gnfw_ref)

        x2 = x1_ref[...] + _dot(yb_ref[...], wout_ref[...])
        rf = _rms(x2)
        xn = x2 * rf
        e = xn * nfw_ref[...] - t_ref[...]
        loss_ref[...] += (0.5 / D) * jnp.sum(jnp.sum(e * e, axis=-1, keepdims=True), axis=0, keepdims=True)
        dyf = e * (1.0 / D)
        gnfw_ref[...] += _rowsum(dyf * xn)
        dx2 = _rms_bwd(dyf, x2, rf, nfw_ref[...])
        dx2_ref[...] = dx2
        dx2b_ref[...] = dx2.astype(BF16)

    tile = lambda w: pl.BlockSpec((tm, w), lambda i: (i, 0))
    whole = lambda *s: pl.BlockSpec(s, lambda i: (0,) * len(s))
    (dx2, dx2b, loss, gnfw), _ = _call(
        main, [], name="head", grid=(s_len // tm,),
        ins=[x1, yb, wout, nfw, tgt], in_specs=[tile(D), tile(BW), _VMEM, _VMEM, tile(D)],
        out_shape=[_sds((s_len, D), F32), _sds((s_len, D), BF16), _sds((1, 1), F32), _sds((1, D), F32)],
        out_specs=[tile(D), tile(D), whole(1, 1), whole(1, D)], scratch=[])
    return dx2, dx2b, loss, gnfw


def _bwd_b(dx2, zb, hs, x1, saved, nw, win8, p8, gcat, wout, *, tm):
    s_len = x1.shape[0]
    nt = s_len // tm

    def main(i, ins, outs, scr):
        (dx2_ref, zb_ref, hs_ref, x1_ref, xc_ref, a_ref, cc_ref, r_ref, ig_ref, m_ref,
         nw_ref, win_ref, p8_ref, gcat_ref, wout_ref) = ins
        dx1_ref, dx1b_ref, dzb_ref, gp8_ref, gga_ref, ggx_ref, gnw_ref = outs
        ae_scr, an_scr, dhd_scr, dh_scr, dy_scr, dxce_scr, carry_scr, afirst_scr = scr

        @pl.when(i == 0)
        def _():
            gp8_ref[...] = jnp.zeros_like(gp8_ref)
            gga_ref[...] = jnp.zeros_like(gga_ref)
            ggx_ref[...] = jnp.zeros_like(ggx_ref)
            gnw_ref[...] = jnp.zeros_like(gnw_ref)
            dxce_scr[tm:tm + SUBLANES, :] = jnp.zeros((SUBLANES, BW), F32)
            carry_scr[...] = jnp.zeros_like(carry_scr)
            afirst_scr[...] = jnp.zeros_like(afirst_scr)

        dx2 = dx2_ref[...]
        dy_scr[...] = _dot_nt(dx2.astype(BF16), wout_ref[...])
        for hh in range(BH):
            cs = slice(hh * HD, (hh + 1) * HD)
            gs = slice(BW + hh * HD, BW + (hh + 1) * HD)
            gt = zb_ref[:, gs]
            sig = _sigmoid(gt)
            dy = dy_scr[:, cs]
            dhd_scr[:, cs] = dy * (gt * sig)
            dzb_ref[:, gs] = (dy * hs_ref[:, cs] * (sig * (1.0 + gt * (1.0 - sig)))).astype(BF16)

        ae_scr[0:tm, :] = a_ref[...]
        ae_scr[tm:tm + SUBLANES, :] = jnp.broadcast_to(afirst_scr[...], (SUBLANES, BW))
        an_scr[...] = ae_scr[1:1 + tm, :]
        afirst_scr[...] = ae_scr[0:1, :]
        carry_scr[...] = _scan_rows(an_scr, dhd_scr, dh_scr, carry_scr[...], tm, True)

        for hh in range(BH):
            cs = slice(hh * HD, (hh + 1) * HD)
            dh = dh_scr[:, cs]
            mult = m_ref[:, cs].astype(F32)
            ig = ig_ref[:, cs].astype(F32)
            r = r_ref[:, cs].astype(F32)
            xc = xc_ref[:, cs]
            lam = p8_ref[7:8, cs]
            sp = _softplus_neg(lam)
            dla = dh * cc_ref[:, cs]
            gp8_ref[7:8, cs] += _rowsum(dla * ((-RG_C) * r)) * (-_sigmoid(-lam))
            dpr = dla * ((-RG_C) * sp) * (r * (1.0 - r))
            dpi = dh * mult * xc * (ig * (1.0 - ig))
            gp8_ref[5:6, cs] += _rowsum(dpr)
            gp8_ref[6:7, cs] += _rowsum(dpi)
            dcat = jnp.concatenate([dpr, dpi], axis=1).astype(BF16)
            dxc = dh * mult * ig + _dot_nt(dcat, gcat_ref[hh])
            gg = _dot(xc.T.astype(BF16), dcat)
            gga_ref[hh] += gg[:, :HD]
            ggx_ref[hh] += gg[:, HD:]
            dxce_scr[0:tm, cs] = dxc
            gp8_ref[4:5, cs] += _rowsum(dxc)
        for hh in range(BH):
            cs = slice(hh * HD, (hh + 1) * HD)
            xb = zb_ref[:, cs]
            d0, d1 = dxce_scr[0:tm, cs], dxce_scr[1:1 + tm, cs]
            d2, d3 = dxce_scr[2:2 + tm, cs], dxce_scr[3:3 + tm, cs]
            dzb_ref[:, cs] = (p8_ref[3:4, cs] * d0 + p8_ref[2:3, cs] * d1 + p8_ref[1:2, cs] * d2
                              + p8_ref[0:1, cs] * d3).astype(BF16)
            gp8_ref[3:4, cs] += _rowsum(d0 * xb)
            gp8_ref[2:3, cs] += _rowsum(d1 * xb)
            gp8_ref[1:2, cs] += _rowsum(d2 * xb)
            gp8_ref[0:1, cs] += _rowsum(d3 * xb)
        dxce_scr[tm:tm + SUBLANES, :] = dxce_scr[0:SUBLANES, :]

        dh1 = jnp.zeros((tm, D), F32)
        for k in range(NDEV):
            dh1 = dh1 + _dot_nt(dzb_ref[:, k * CB:(k + 1) * CB], win_ref[k])
        x1 = x1_ref[...]
        r1 = _rms(x1)
        dx1 = dx2 + _rms_bwd(dh1, x1, r1, nw_ref[...])
        dx1_ref[...] = dx1
        dx1b_ref[...] = dx1.astype(BF16)
        gnw_ref[...] += _rowsum(dh1 * x1 * r1)

    tile = lambda w: pl.BlockSpec((tm, w), lambda i: (nt - 1 - i, 0))
    whole = lambda *s: pl.BlockSpec(s, lambda i: (0,) * len(s))
    full = lambda: pltpu.VMEM((tm, BW), F32)
    ext = lambda: pltpu.VMEM((tm + SUBLANES, BW), F32)
    out, _ = _call(
        main, [], name="bwd_b", grid=(nt,),
        ins=[dx2, zb, hs, x1, *saved, nw, win8, p8, gcat, wout],
        in_specs=[tile(D), tile(2 * BW), tile(BW), tile(D)] + [tile(BW)] * 6 + [_VMEM] * 5,
        out_shape=[_sds((s_len, D), F32), _sds((s_len, D), BF16), _sds((s_len, 2 * BW), BF16), _sds((SUBLANES, BW), F32),
                   _sds((BH, HD, HD), F32), _sds((BH, HD, HD), F32), _sds((1, D), F32)],
        out_specs=[tile(D), tile(D), tile(2 * BW), whole(SUBLANES, BW), whole(BH, HD, HD), whole(BH, HD, HD),
                   whole(1, D)],
        scratch=[ext(), full(), full(), full(), full(), ext(), pltpu.VMEM((1, BW), F32), pltpu.VMEM((1, BW), F32)])
    return out


def _transpose_into(dst_ref, src_ref, rows):
    s_len = src_ref.shape[0]
    for r0 in range(0, s_len, rows):
        dst_ref[:, r0:r0 + rows] = src_ref[r0:r0 + rows, :].astype(F32).T.astype(BF16)


def _wgrad(a, b, jobs, *, by_rows, per, name, relay_step=0):
    s_len, m = a.shape
    n = b.shape[1]
    r, cd = (m // NDEV, n) if by_rows else (m, n // NDEV)
    nsteps = NDEV // per
    at_rows = per * r if by_rows else m

    def main(i, ins, outs, scr):
        a_ref, b_ref = ins
        q_ref, acc_ref = outs
        at_scr, stage, mine, land, send_sems, recv_sems = scr
        x, y, c = _place()

        def to_sibling(pi):
            return pltpu.make_async_remote_copy(
                src_ref=stage.at[pi & 1], dst_ref=land.at[pi], send_sem=send_sems.at[pi], recv_sem=recv_sems.at[pi],
                device_id=(x, y, 1 - c), device_id_type=MESH)

        if by_rows:
            _transpose_into(at_scr, a_ref, TRANSPOSE_ROWS)
        else:
            @pl.when(i == 0)
            def _():
                _transpose_into(at_scr, a_ref, TRANSPOSE_ROWS)

        res = _dot(at_scr[...], b_ref[...]).astype(BF16)
        for k in range(per):
            blk = per * i + k
            pi, pc = blk >> 1, blk & 1
            val = res[k * r:(k + 1) * r, :] if by_rows else res

            @pl.when(pc != c)
            def _():
                @pl.when(pi >= 2)
                def _():
                    to_sibling(pi - 2).wait_send()

                stage[pi & 1] = val
                to_sibling(pi).start()

            @pl.when(pc == c)
            def _():
                mine[pi] = val

        @pl.when(i == nsteps - 1)
        def _():
            for p in range(4):
                to_sibling(p).wait_recv()
            to_sibling(2).wait_send()
            to_sibling(3).wait_send()
            _chip_sums(mine, land, q_ref, acc_ref, x, y)

    if by_rows:
        in_specs = [pl.BlockSpec((s_len, at_rows), lambda j: (0, j)), _VMEM]
    else:
        in_specs = [_VMEM, pl.BlockSpec((s_len, cd), lambda j: (0, j))]
    blk_vmem = lambda k: pltpu.VMEM((k, r, cd), BF16)
    (q, acc), job_out = _call(
        main, jobs, name=name, grid=(nsteps,), relay_step=relay_step, ins=[a, b], in_specs=in_specs,
        out_shape=[_sds((NCHIP_OTHER, r, cd), BF16), _sds((r, cd), F32)],
        out_specs=[pl.BlockSpec((NCHIP_OTHER, r, cd), lambda j: (0, 0, 0)), pl.BlockSpec((r, cd), lambda j: (0, 0))],
        scratch=[pltpu.VMEM((at_rows, s_len), BF16), blk_vmem(2), blk_vmem(4), blk_vmem(4),
                 pltpu.SemaphoreType.DMA((4,)), pltpu.SemaphoreType.DMA((4,))])
    return q, acc, job_out


def _wgrad_cols_early(a, b, jobs, *, name, relay_step=0):
    s_len, m = a.shape
    r, cd = m, b.shape[1] // NDEV
    h = r // 2

    def chip_at(pos, base):
        return base ^ (3 - pos)

    def main(i, ins, outs, scr):
        a_ref, b_ref = ins
        q_ref, acc_ref, rel_ref = outs
        at_scr, stage, mine, land, q2_scr, send_sems, recv_sems, via_send, via_recv = scr
        x, y, c = _place()
        base = 2 * x + y
        xn, yn, _ = _other_chips(x, y)
        pos, pc = i >> 1, i & 1
        pi = chip_at(pos, base)

        def to_sibling(chip, slot):
            return pltpu.make_async_remote_copy(
                src_ref=stage.at[slot], dst_ref=land.at[chip], send_sem=send_sems.at[chip],
                recv_sem=recv_sems.at[chip], device_id=(x, y, 1 - c), device_id_type=MESH)

        def via(k):
            return pltpu.make_async_remote_copy(
                src_ref=q2_scr.at[pl.ds(k * h, h)], dst_ref=rel_ref.at[k], send_sem=via_send.at[k],
                recv_sem=via_recv.at[k], device_id=(*(xn, yn)[k], c), device_id_type=MESH)

        @pl.when(i == 0)
        def _():
            _transpose_into(at_scr, a_ref, TRANSPOSE_ROWS)

        res = _dot(at_scr[...], b_ref[...]).astype(BF16)

        @pl.when(pc != c)
        def _():
            @pl.when(pos >= 2)
            def _():
                to_sibling(chip_at(pos - 2, base), pos & 1).wait_send()

            stage[pos & 1] = res
            to_sibling(pi, pos & 1).start()

        @pl.when(pc == c)
        def _():
            mine[pi] = res

        @pl.when(i == 1)
        def _():
            dg = chip_at(0, base)
            to_sibling(dg, 0).wait_recv()
            q2 = (mine[dg].astype(F32) + land[dg].astype(F32)).astype(BF16)
            q2_scr[...] = q2
            q_ref[2] = q2
            via(0).start()
            via(1).start()

        @pl.when(i == NDEV - 1)
        def _():
            for pos_ in (1, 2, 3):
                to_sibling(chip_at(pos_, base), 0).wait_recv()
            to_sibling(chip_at(2, base), 0).wait_send()
            to_sibling(chip_at(3, base), 1).wait_send()
            for k in range(2):
                via(k).wait_recv()
            for k in range(2):
                via(k).wait_send()
            for j, chip in enumerate((base ^ 2, base ^ 1)):
                q_ref[j] = (mine[chip].astype(F32) + land[chip].astype(F32)).astype(BF16)
            acc_ref[...] = mine[base].astype(F32) + land[base].astype(F32)

    def b_block(j):
        base = 2 * lax.axis_index("x") + lax.axis_index("y")
        return (0, 2 * chip_at(j >> 1, base) + (j & 1))

    blk_vmem = lambda k: pltpu.VMEM((k, r, cd), BF16)
    (q, acc, rel), job_out = _call(
        main, jobs, name=name, grid=(NDEV,), relay_step=relay_step, ins=[a, b],
        in_specs=[_VMEM, pl.BlockSpec((s_len, cd), b_block)],
        out_shape=[_sds((NCHIP_OTHER, r, cd), BF16), _sds((r, cd), F32), _sds((2, h, cd), BF16)],
        out_specs=[pl.BlockSpec((NCHIP_OTHER, r, cd), lambda j: (0, 0, 0)), pl.BlockSpec((r, cd), lambda j: (0, 0)), _HBM],
        scratch=[pltpu.VMEM((m, s_len), BF16), blk_vmem(2), blk_vmem(4), blk_vmem(4), pltpu.VMEM((r, cd), BF16),
                 pltpu.SemaphoreType.DMA((4,)), pltpu.SemaphoreType.DMA((4,)), pltpu.SemaphoreType.DMA((2,)),
                 pltpu.SemaphoreType.DMA((2,))])
    return q, acc, rel, job_out


class _ExchangeRest:
    def __init__(self, q, relayed):
        _, r, cd = q.shape
        half = (2, r // 2, cd)
        self.ins, self.in_specs = [q, relayed], [_HBM, _HBM]
        self.out_shape, self.out_specs = [_sds((2, r, cd), q.dtype)], [_HBM]
        self.scratch = [pltpu.VMEM(half, q.dtype), pltpu.VMEM(half, q.dtype), pltpu.VMEM(half, q.dtype),
                        pltpu.SemaphoreType.DMA((4,)), pltpu.SemaphoreType.DMA((4,)), pltpu.SemaphoreType.DMA((4,))]

    def ops(self, ins, outs, scr):
        (q, rel_in), (land,) = ins, outs
        own, rel, comb, send_sems, recv_sems, local_sems = scr
        h = q.shape[1] // 2
        x, y, c = _place()
        xn, yn, _ = _other_chips(x, y)
        h0, h1 = pl.ds(0, h), pl.ds(h, h)

        def remote(k, src, dst, chip):
            return pltpu.make_async_remote_copy(src_ref=src, dst_ref=dst, send_sem=send_sems.at[k],
                                                recv_sem=recv_sems.at[k], device_id=(*chip, c), device_id_type=MESH)

        def sends():
            return [remote(0, q.at[0, h0], land.at[0, h0], xn), remote(1, q.at[1, h1], land.at[1, h1], yn),
                    remote(2, comb.at[0], land.at[1, h0], yn), remote(3, comb.at[1], land.at[0, h1], xn)]

        def loads():
            return [pltpu.make_async_copy(q.at[1, h0], own.at[0], local_sems.at[0]),
                    pltpu.make_async_copy(q.at[0, h1], own.at[1], local_sems.at[1]),
                    pltpu.make_async_copy(rel_in.at[0], rel.at[0], local_sems.at[2]),
                    pltpu.make_async_copy(rel_in.at[1], rel.at[1], local_sems.at[3])]

        def start():
            cps, lds = sends(), loads()
            for ld in lds:
                ld.start()
            cps[0].start()
            cps[1].start()
            for ld in lds:
                ld.wait()
            for k in range(2):
                comb[k] = (own[k].astype(F32) + rel[k].astype(F32)).astype(comb.dtype)
            cps[2].start()
            cps[3].start()

        def finish():
            cps = sends()
            for cp in cps:
                cp.wait_recv()
            for cp in cps:
                cp.wait_send()

        return start, lambda: None, finish


def _adam_math(w, g, m, v):
    m = B1 * m + (1.0 - B1) * g
    v = B2 * v + (1.0 - B2) * (g * g)
    m_hat = m / (1.0 - B1 ** STEP)
    v_hat = v / (1.0 - B2 ** STEP)
    delta = (-LR) * (m_hat / (jnp.sqrt(v_hat) + ADAM_EPS) + WD * w)
    return delta, m, v


def _adam_big(w, acc, land, m, v, name):
    r, cd = w.shape
    rb = ADAM_ROWS if r % ADAM_ROWS == 0 else r
    nland = land.shape[0]

    def body(w_ref, acc_ref, land_ref, m_ref, v_ref, g_ref, d_ref, mo_ref, vo_ref):
        g = acc_ref[...]
        for j in range(nland):
            g = g + land_ref[j].astype(F32)
        g_ref[...] = g
        d_ref[...], mo_ref[...], vo_ref[...] = _adam_math(w_ref[...], g, m_ref[...], v_ref[...])

    blk = pl.BlockSpec((rb, cd), lambda i: (i, 0))
    blk3 = pl.BlockSpec((nland, rb, cd), lambda i: (0, i, 0))
    return pl.pallas_call(
        body, name=name, grid=(r // rb,), in_specs=[blk, blk, blk3, blk, blk], out_specs=[blk] * 4,
        out_shape=[_sds((r, cd), F32)] * 4,
        compiler_params=_params(dimension_semantics=("arbitrary",)),
    )(w, acc, land, m, v)


def _adam_small(groups):
    n = len(groups)

    def body(*refs):
        ins, outs = refs[:4 * n], refs[4 * n:]
        for k in range(n):
            w_ref, g_ref, m_ref, v_ref = ins[4 * k:4 * k + 4]
            d, mo, vo = _adam_math(w_ref[...], g_ref[...], m_ref[...], v_ref[...])
            outs[3 * k][...] = d
            outs[3 * k + 1][...] = mo
            outs[3 * k + 2][...] = vo

    flat = [a for grp in groups for a in grp]
    shapes = [_sds(grp[0].shape, F32) for grp in groups for _ in range(3)]
    res = pl.pallas_call(
        body, name="adam_small", in_specs=[_VMEM] * (4 * n), out_specs=[_VMEM] * (3 * n), out_shape=shapes,
        compiler_params=_params(),
    )(*flat)
    return [tuple(res[3 * k:3 * k + 3]) for k in range(n)]


TM_FWD_A = 256
RELAY_STEP_FWD_A = 4
RELAY_STEP_FWD_B = 2
TM_BWD_A = 256
RELAY_STEP_BWD_A = 3
TM_BWD_A_IN = 256
RELAY_STEP_BWD_A_IN = 4
RELAY_STEP_WGRAD_A_IN = 2
TM_FWD_B = 256
TM_HEAD = 512
TM_BWD_B = 256


def _pack(parts, rows):
    flat = jnp.concatenate([p.reshape(-1) for p in parts])
    return jnp.pad(flat, (0, NDEV * rows * LANES - flat.shape[0])).reshape(NDEV, rows, LANES)


def _unpack(packed, shapes):
    flat, out, off = packed.reshape(-1), [], 0
    for s in shapes:
        size = 1
        for d in s:
            size *= d
        out.append(flat[off:off + size].reshape(s))
        off += size
    return out


def kernel(x, norm_w, a_w_in, a_ln_w, a_ln_b, a_w_s, a_b_s, a_w_out, b_w_in, b_conv_w, b_conv_b, b_gate_a_w, b_gate_a_b, b_gate_x_w, b_gate_x_b, b_lambda, b_w_out, norm_f_w, loss_target, m_norm_w, m_a_w_in, m_a_ln_w, m_a_ln_b, m_a_w_s, m_a_b_s, m_a_w_out, m_b_w_in, m_b_conv_w, m_b_conv_b, m_b_gate_a_w, m_b_gate_a_b, m_b_gate_x_w, m_b_gate_x_b, m_b_lambda, m_b_w_out, m_norm_f_w, v_norm_w, v_a_w_in, v_a_ln_w, v_a_ln_b, v_a_w_s, v_a_b_s, v_a_w_out, v_b_w_in, v_b_conv_w, v_b_conv_b, v_b_gate_a_w, v_b_gate_a_b, v_b_gate_x_w, v_b_gate_x_b, v_b_lambda, v_b_w_out, v_norm_f_w):
    me = 4 * lax.axis_index("x") + 2 * lax.axis_index("y") + lax.axis_index("c")
    xs, tgt = x[0], loss_target[0]
    nw0, nw1, nfw = norm_w[0:1], norm_w[1:2], norm_f_w.reshape(1, D)
    w_s, bst = a_w_s[0], a_b_s[0].T
    gcat = jnp.concatenate([b_gate_a_w[0], b_gate_x_w[0]], axis=-1).astype(BF16)

    p8_shard = jnp.concatenate([b_conv_w[0], b_conv_b, b_gate_a_b, b_gate_x_b, b_lambda], axis=0)
    ((win_a8, p8_all),) = _comm_only([_Gather([a_w_in[0], p8_shard], [BF16, F32])], "gather_first")
    p8 = jnp.transpose(p8_all, (1, 0, 2)).reshape(SUBLANES, BW)

    (z, h0, ya), ((wout_a8, win_b8),) = _fwd_a(
        xs, nw0, win_a8, a_ln_w, a_ln_b, w_s, bst, [_Gather([a_w_out[0], b_w_in[0]], [BF16, BF16])],
        tm=TM_FWD_A, relay_step=RELAY_STEP_FWD_A)
    wout_a = wout_a8.reshape(AW, D)
    (x1, zb, hs, h1, yb, *saved_b), ((wout_b8,),) = _fwd_b(
        xs, ya, wout_a, nw1, win_b8, p8, gcat, [_Gather([b_w_out[0]], [BF16])],
        tm=TM_FWD_B, relay_step=RELAY_STEP_FWD_B)
    wout_b = wout_b8.reshape(BW, D)
    dx2, dx2b, loss, g_nfw = _head(x1, yb, wout_b, nfw, tgt, tm=TM_HEAD)

    dx1, dx1b, dzb, g_p8, g_ga, g_gx, g_nw1 = _bwd_b(dx2, zb, hs, x1, saved_b, nw1, win_b8, p8, gcat, wout_b,
                                                     tm=TM_BWD_B)
    q_wout_b, acc_wout_b, _ = _wgrad(yb, dx2b, [], by_rows=True, per=2, name="wgrad_b_out")
    shapes_b = [(1, D), (1, D), (SUBLANES, BW), (1, 1)]
    pack_b = _pack([g_nfw, g_nw1, g_p8, loss], 16)
    small_b = _InChip([g_ga.reshape(NDEV, -1, HD), g_gx.reshape(NDEV, -1, HD), pack_b])
    q_win_b, acc_win_b, (sm_b, (l_wout_b,)) = _wgrad(h1, dzb, [small_b, _Exchange([q_wout_b])], by_rows=False, per=1,
                                                      name="wgrad_b_in")
    qs_b, accs_b = sm_b[:3], sm_b[3:]

    (dz, g_lnw, g_lnb, g_ws, g_bst), (lands_b, (l_win_b,)) = _bwd_a(
        dx1b, z, a_ln_w, a_ln_b, w_s, bst, wout_a, [_Exchange(qs_b), _ExchangeVia(q_win_b)],
        tm=TM_BWD_A, relay_step=RELAY_STEP_BWD_A)
    shapes_a = [(1, AW), (1, AW), (CH, G)]
    pack_a = _pack([g_lnw, g_lnb, g_bst], 8)
    q_wout_a, acc_wout_a, (red_b, sm_a) = _wgrad(
        ya, dx1b, [_SumGather(accs_b, lands_b), _InChip([g_ws, pack_a])], by_rows=True, per=2,
        name="wgrad_a_out", relay_step=1)
    qs_a, accs_a = sm_a[:2], sm_a[2:]
    q_win_a, acc_win_a, rel_a, (lands_a, (l_wout_a,)) = _wgrad_cols_early(
        h0, dz, [_Exchange(qs_a), _ExchangeVia(q_wout_a)], name="wgrad_a_in", relay_step=RELAY_STEP_WGRAD_A_IN)
    (gx, g_nw0), (red_a, (l_win_a,)) = _bwd_a_in(
        dz, dx1, xs, nw0, win_a8, [_SumGather(accs_a, lands_a), _ExchangeRest(q_win_a, rel_a)],
        tm=TM_BWD_A_IN, relay_step=RELAY_STEP_BWD_A_IN)

    r_ga, r_gx, r_pack_b = red_b
    r_nfw, r_nw1, r_p8, loss = _unpack(r_pack_b, shapes_b)
    r_ws, r_pack_a = red_a
    r_lnw, r_lnb, r_bst = _unpack(r_pack_a, shapes_a)
    g_p8 = lax.dynamic_slice_in_dim(r_p8, me * (BW // NDEV), BW // NDEV, axis=1)
    loss = loss[0, 0]

    weights = dict(norm_w=norm_w, a_w_in=a_w_in, a_ln_w=a_ln_w, a_ln_b=a_ln_b, a_w_s=a_w_s, a_b_s=a_b_s, a_w_out=a_w_out,
                   b_w_in=b_w_in, b_conv_w=b_conv_w, b_conv_b=b_conv_b, b_gate_a_w=b_gate_a_w, b_gate_a_b=b_gate_a_b,
                   b_gate_x_w=b_gate_x_w, b_gate_x_b=b_gate_x_b, b_lambda=b_lambda, b_w_out=b_w_out, norm_f_w=norm_f_w)
    mom1 = dict(norm_w=m_norm_w, a_w_in=m_a_w_in, a_ln_w=m_a_ln_w, a_ln_b=m_a_ln_b, a_w_s=m_a_w_s, a_b_s=m_a_b_s,
                a_w_out=m_a_w_out, b_w_in=m_b_w_in, b_conv_w=m_b_conv_w, b_conv_b=m_b_conv_b, b_gate_a_w=m_b_gate_a_w,
                b_gate_a_b=m_b_gate_a_b, b_gate_x_w=m_b_gate_x_w, b_gate_x_b=m_b_gate_x_b, b_lambda=m_b_lambda,
                b_w_out=m_b_w_out, norm_f_w=m_norm_f_w)
    mom2 = dict(norm_w=v_norm_w, a_w_in=v_a_w_in, a_ln_w=v_a_ln_w, a_ln_b=v_a_ln_b, a_w_s=v_a_w_s, a_b_s=v_a_b_s,
                a_w_out=v_a_w_out, b_w_in=v_b_w_in, b_conv_w=v_b_conv_w, b_conv_b=v_b_conv_b, b_gate_a_w=v_b_gate_a_w,
                b_gate_a_b=v_b_gate_a_b, b_gate_x_w=v_b_gate_x_w, b_gate_x_b=v_b_gate_x_b, b_lambda=v_b_lambda,
                b_w_out=v_b_w_out, norm_f_w=v_norm_f_w)
    names = list(weights)

    def as2d(a):
        return a.reshape(-1, a.shape[-1])

    upd, grads = {}, {}
    for k, acc, land in (("a_w_in", acc_win_a, l_win_a), ("a_w_out", acc_wout_a, l_wout_a),
                         ("b_w_in", acc_win_b, l_win_b), ("b_w_out", acc_wout_b, l_wout_b)):
        g, d, mo, vo = _adam_big(as2d(weights[k]), acc, land, as2d(mom1[k]), as2d(mom2[k]), "adam_" + k)
        grads[k] = g[None]
        upd[k] = (d, mo, vo)
    grads.update(
        norm_w=jnp.concatenate([g_nw0, r_nw1], axis=0), a_ln_w=r_lnw, a_ln_b=r_lnb,
        a_w_s=r_ws.reshape(1, G, CH, CH), a_b_s=r_bst.T[None],
        b_conv_w=g_p8[None, 0:4], b_conv_b=g_p8[4:5], b_gate_a_w=r_ga.reshape(1, BH, HD, HD), b_gate_a_b=g_p8[5:6],
        b_gate_x_w=r_gx.reshape(1, BH, HD, HD), b_gate_x_b=g_p8[6:7], b_lambda=g_p8[7:8], norm_f_w=r_nfw.reshape(D))
    small_names = [k for k in names if k not in upd]
    res = _adam_small([(as2d(weights[k]), as2d(grads[k]), as2d(mom1[k]), as2d(mom2[k])) for k in small_names])
    for k, r3 in zip(small_names, res):
        upd[k] = r3
    deltas = [upd[k][0].reshape(weights[k].shape) for k in names]
    new_m = [upd[k][1].reshape(weights[k].shape) for k in names]
    new_v = [upd[k][2].reshape(weights[k].shape) for k in names]
    return (loss, gx[None], *[grads[k] for k in names], *deltas, *new_m, *new_v)
```

```python
import jax
import jax.numpy as jnp
from jax import lax
from jax.experimental import pallas as pl
from jax.experimental.pallas import tpu as pltpu

F32 = jnp.float32
BF16 = jnp.bfloat16
MESH = pl.DeviceIdType.MESH

NDEV = 8
NCHIP_OTHER = 3
D = 1024
AW = 2048
G = 8
GD = AW // G
CH = 128
BW = 1536
BH = 12
HD = BW // BH
CA = 3 * AW // NDEV
CB = 2 * BW // NDEV
RMS_EPS = 1e-6
LN_EPS = 1e-5
RG_C = 8.0
LR, B1, B2, ADAM_EPS, WD, STEP = 0.001, 0.9, 0.999, 1e-08, 0.01, 10
V7X_VMEM_BYTES = 64 * 1024 * 1024
VMEM_LIMIT = V7X_VMEM_BYTES - 8 * 1024 * 1024
SUBLANES = 8
LANES = 128
BF16_ROWS = 16
TRANSPOSE_ROWS = 256
ADAM_ROWS = 512
GELU_C = 0.7978845608028654
GELU_K = 0.044715

_VMEM = pl.BlockSpec(memory_space=pltpu.VMEM)
_HBM = pl.BlockSpec(memory_space=pltpu.HBM)


def _sds(shape, dtype):
    return jax.ShapeDtypeStruct(tuple(shape), dtype)


def _params(**kw):
    return pltpu.CompilerParams(vmem_limit_bytes=VMEM_LIMIT, **kw)


def _gelu_t(z):
    p = 0.5 * jnp.tanh(z * (GELU_C + (GELU_C * GELU_K) * (z * z))) + 0.5
    return z * p, p


def _dgelu(z, p):
    return p * (1.0 + (z * (1.0 - p)) * (2.0 * GELU_C + (6.0 * GELU_C * GELU_K) * (z * z)))


def _sigmoid(v):
    return 0.5 * jnp.tanh(0.5 * v) + 0.5


def _softplus_neg(lam):
    return jnp.maximum(-lam, 0.0) + jnp.log1p(jnp.exp(-jnp.abs(lam)))


def _dot(a, b):
    return jnp.dot(a, b, preferred_element_type=F32)


def _dot_nt(a, b):
    return lax.dot_general(a, b, (((1,), (1,)), ((), ())), preferred_element_type=F32)


def _rowsum(v):
    return jnp.sum(v, axis=0, keepdims=True)


def _causal_mask():
    r = lax.broadcasted_iota(jnp.int32, (CH, CH), 0)
    c = lax.broadcasted_iota(jnp.int32, (CH, CH), 1)
    return r >= c


def _rms(x):
    return lax.rsqrt(jnp.mean(x * x, axis=-1, keepdims=True) + RMS_EPS)


def _rms_bwd(dh, x, r, nw):
    gy = dh * nw
    return r * gy - x * (r * r * r) * jnp.mean(gy * x, axis=-1, keepdims=True)


def _place():
    return lax.axis_index("x"), lax.axis_index("y"), lax.axis_index("c")


def _other_chips(x, y):
    return [(1 - x, y), (x, 1 - y), (1 - x, 1 - y)]


GATHER_SLOTS = 10


def _gather_ops(ins, outs, send_sems, recv_sems, local_sems):
    n = len(ins)
    x, y, c = _place()
    sibling = (x, y, 1 - c)
    xn, yn, dg = _other_chips(x, y)
    split = [ins[i].shape[0] % (2 * BF16_ROWS) == 0 for i in range(n)]

    def blk(chip, core):
        return 4 * chip[0] + 2 * chip[1] + core

    me = blk((x, y), c)

    def part(ref, i, half):
        if half is None:
            return ref
        h = ins[i].shape[0] // 2
        return ref.at[pl.ds(half * h, h)]

    def copy(i, k, block, to, half=None, src=None):
        dst = part(outs[i].at[block], i, half)
        return pltpu.make_async_remote_copy(
            src_ref=dst if src is None else part(src, i, half), dst_ref=dst,
            send_sem=send_sems.at[k, i], recv_sem=recv_sems.at[k, i], device_id=to, device_id_type=MESH)

    def first_copies():
        mine = [pltpu.make_async_copy(ins[i], outs[i].at[me], local_sems.at[i]) for i in range(n)]
        first = []
        for i in range(n):
            first.append(copy(i, 0, me, sibling, src=ins[i]))
            if split[i]:
                first.append(copy(i, 1, me, (*xn, c), 0, ins[i]))
                first.append(copy(i, 3, me, (*yn, c), 1, ins[i]))
                first.append(copy(i, 2, me, (*xn, c), 1, ins[i]))
                first.append(copy(i, 4, me, (*yn, c), 0, ins[i]))
            else:
                first.append(copy(i, 1, me, (*xn, c), None, ins[i]))
                first.append(copy(i, 3, me, (*yn, c), None, ins[i]))
                first.append(copy(i, 5, me, (*dg, c), None, ins[i]))
        return mine, first

    def onward():
        out = []
        for i in range(n):
            if split[i]:
                out.append(copy(i, 5, blk(xn, c), (*yn, c), 0))
                out.append(copy(i, 6, blk(yn, c), (*xn, c), 1))
        return out

    def start():
        mine, first = first_copies()
        for cp in mine + first:
            cp.start()

    def relay():
        sends = onward()
        for i in range(n):
            if split[i]:
                copy(i, 1, blk(xn, c), sibling, 0).wait_recv()
                sends.pop(0).start()
                copy(i, 3, blk(yn, c), sibling, 1).wait_recv()
                sends.pop(0).start()

    def finish():
        mine, first = first_copies()
        passed = []

        def pass_on(i, j, chip):
            fwd = copy(i, 7 + j, blk(chip, c), sibling)
            fwd.start()
            passed.append(fwd)

        for i in range(n):
            if split[i]:
                copy(i, 2, blk(xn, c), sibling, 1).wait_recv()
                pass_on(i, 0, xn)
                copy(i, 4, blk(yn, c), sibling, 0).wait_recv()
                pass_on(i, 1, yn)
                copy(i, 5, blk(dg, c), sibling, 0).wait_recv()
                copy(i, 6, blk(dg, c), sibling, 1).wait_recv()
                pass_on(i, 2, dg)
            else:
                copy(i, 1, blk(xn, c), sibling).wait_recv()
                pass_on(i, 0, xn)
                copy(i, 3, blk(yn, c), sibling).wait_recv()
                pass_on(i, 1, yn)
                copy(i, 5, blk(dg, c), sibling).wait_recv()
                pass_on(i, 2, dg)
        for i in range(n):
            copy(i, 0, blk((x, y), 1 - c), sibling).wait_recv()
            for j, chip in enumerate((xn, yn, dg)):
                copy(i, 7 + j, blk(chip, 1 - c), sibling).wait_recv()
        for cp in first + passed + onward():
            cp.wait_send()
        for cp in mine:
            cp.wait()

    return start, relay, finish


def _gather_sems(n):
    return [pltpu.SemaphoreType.DMA((GATHER_SLOTS, n)), pltpu.SemaphoreType.DMA((GATHER_SLOTS, n)),
            pltpu.SemaphoreType.DMA((n,))]


class _Gather:
    def __init__(self, shards, as_dtypes=None):
        n = len(shards)
        dts = [s.dtype for s in shards] if as_dtypes is None else list(as_dtypes)
        self.cast = [jnp.dtype(d) != s.dtype for d, s in zip(dts, shards)]
        self.ins = list(shards)
        self.in_specs = [_VMEM if c else _HBM for c in self.cast]
        self.out_shape = [_sds((NDEV,) + s.shape, d) for s, d in zip(shards, dts)]
        self.out_specs = [_HBM] * n
        self.scratch = [pltpu.VMEM(s.shape, d) for s, d, c in zip(shards, dts, self.cast) if c] + _gather_sems(n)

    def ops(self, ins, outs, scr):
        ncast = sum(self.cast)
        staged = iter(scr[:ncast])
        srcs = [next(staged) if c else ref for c, ref in zip(self.cast, ins)]
        start, relay, finish = _gather_ops(srcs, outs, *scr[ncast:])

        def cast_and_start():
            for c, ref, src in zip(self.cast, ins, srcs):
                if c:
                    src[...] = ref[...].astype(src.dtype)
            start()

        return cast_and_start, relay, finish


class _Exchange:
    def __init__(self, qs):
        n = len(qs)
        self.ins, self.in_specs = list(qs), [_HBM] * n
        self.out_shape = [_sds(q.shape, q.dtype) for q in qs]
        self.out_specs = [_HBM] * n
        self.scratch = [pltpu.SemaphoreType.DMA((NCHIP_OTHER, n)), pltpu.SemaphoreType.DMA((NCHIP_OTHER, n))]

    def ops(self, ins, outs, scr):
        send_sems, recv_sems = scr
        n = len(ins)
        x, y, c = _place()
        chips = _other_chips(x, y)

        def copies():
            return [pltpu.make_async_remote_copy(
                src_ref=ins[i].at[j], dst_ref=outs[i].at[j], send_sem=send_sems.at[j, i],
                recv_sem=recv_sems.at[j, i], device_id=(*chips[j], c), device_id_type=MESH)
                for i in range(n) for j in range(NCHIP_OTHER)]

        def start():
            for cp in copies():
                cp.start()

        def finish():
            cps = copies()
            for cp in cps:
                cp.wait_recv()
            for cp in cps:
                cp.wait_send()

        return start, lambda: None, finish


class _ExchangeVia:
    def __init__(self, q):
        _, r, cd = q.shape
        half = (2, r // 2, cd)
        self.ins, self.in_specs = [q], [_HBM]
        self.out_shape, self.out_specs = [_sds((2, r, cd), q.dtype)], [_HBM]
        self.scratch = [pltpu.VMEM(half, q.dtype), pltpu.VMEM(half, q.dtype), pltpu.VMEM(half, q.dtype),
                        pltpu.SemaphoreType.DMA((6,)), pltpu.SemaphoreType.DMA((6,)), pltpu.SemaphoreType.DMA((2,))]

    def ops(self, ins, outs, scr):
        (q,), (land,) = ins, outs
        relayed, own, comb, send_sems, recv_sems, local_sems = scr
        h = q.shape[1] // 2
        x, y, c = _place()
        xn, yn, _ = _other_chips(x, y)
        h0, h1 = pl.ds(0, h), pl.ds(h, h)

        def remote(k, src, dst, chip):
            return pltpu.make_async_remote_copy(src_ref=src, dst_ref=dst, send_sem=send_sems.at[k],
                                                recv_sem=recv_sems.at[k], device_id=(*chip, c), device_id_type=MESH)

        def via():
            return [remote(2, q.at[2, h0], relayed.at[0], xn), remote(3, q.at[2, h1], relayed.at[1], yn)]

        def direct():
            return [remote(0, q.at[0, h0], land.at[0, h0], xn), remote(1, q.at[1, h1], land.at[1, h1], yn)]

        def second():
            return [remote(4, comb.at[0], land.at[1, h0], yn), remote(5, comb.at[1], land.at[0, h1], xn)]

        def mine():
            return [pltpu.make_async_copy(q.at[1, h0], own.at[0], local_sems.at[0]),
                    pltpu.make_async_copy(q.at[0, h1], own.at[1], local_sems.at[1])]

        def start():
            for cp in via() + direct() + mine():
                cp.start()

        def relay():
            arrived, loaded, onward = via(), mine(), second()
            for k in range(2):
                arrived[k].wait_recv()
                loaded[k].wait()
                comb[k] = (own[k].astype(F32) + relayed[k].astype(F32)).astype(comb.dtype)
                onward[k].start()

        def finish():
            landing = direct() + second()
            for cp in landing:
                cp.wait_recv()
            for cp in via() + landing:
                cp.wait_send()

        return start, relay, finish


class _SumGather:
    def __init__(self, accs, lands):
        n = len(accs)
        self.n = n
        self.ins, self.in_specs = list(accs) + list(lands), [_VMEM] * (2 * n)
        self.out_shape = [_sds((NDEV,) + a.shape, a.dtype) for a in accs]
        self.out_specs = [_HBM] * n
        self.scratch = [pltpu.VMEM(a.shape, a.dtype) for a in accs] + _gather_sems(n)

    def ops(self, ins, outs, scr):
        n = self.n
        accs, lands, mine = ins[:n], ins[n:], scr[:n]
        g_start, relay, finish = _gather_ops(mine, outs, *scr[n:])

        def start():
            for i in range(n):
                mine[i][...] = accs[i][...] + lands[i][0] + lands[i][1] + lands[i][2]
            g_start()

        return start, relay, finish


def _call(main, jobs, *, name, grid, ins, in_specs, out_shape, out_specs, scratch, relay_step=0):
    nsteps = grid[0] if grid else 1
    n_in, n_out, n_scr = len(ins), len(out_shape), len(scratch)

    def body(*refs):
        pos = [0]

        def take(k):
            r = refs[pos[0]:pos[0] + k]
            pos[0] += k
            return r

        m_in = take(n_in)
        j_in = [take(len(j.ins)) for j in jobs]
        m_out = take(n_out)
        j_out = [take(len(j.out_shape)) for j in jobs]
        m_scr = take(n_scr)
        j_scr = [take(len(j.scratch)) for j in jobs]
        ops = [j.ops(a, b, s) for j, a, b, s in zip(jobs, j_in, j_out, j_scr)]
        i = pl.program_id(0) if grid else 0
        if not grid:
            for o in ops:
                o[0]()
            main(i, m_in, m_out, m_scr)
            for o in ops:
                o[1]()
            for o in ops:
                o[2]()
            return

        if ops:
            @pl.when(i == 0)
            def _():
                for o in ops:
                    o[0]()

        main(i, m_in, m_out, m_scr)

        if ops:
            @pl.when(i == min(relay_step, nsteps - 1))
            def _():
                for o in ops:
                    o[1]()

            @pl.when(i == nsteps - 1)
            def _():
                for o in ops:
                    o[2]()

    extra = dict(dimension_semantics=("arbitrary",)) if grid else {}
    res = pl.pallas_call(
        body, name=name, grid=grid,
        in_specs=list(in_specs) + [s for j in jobs for s in j.in_specs],
        out_specs=list(out_specs) + [s for j in jobs for s in j.out_specs],
        out_shape=list(out_shape) + [s for j in jobs for s in j.out_shape],
        scratch_shapes=list(scratch) + [s for j in jobs for s in j.scratch],
        compiler_params=_params(**extra),
    )(*ins, *[a for j in jobs for a in j.ins])
    main_out, rest, job_out = res[:n_out], res[n_out:], []
    for j in jobs:
        k = len(j.out_shape)
        job_out.append(rest[:k])
        rest = rest[k:]
    return main_out, job_out


def _comm_only(jobs, name):
    _, job_out = _call(lambda i, a, b, s: None, jobs, name=name, grid=(), ins=[], in_specs=[], out_shape=[],
                       out_specs=[], scratch=[])
    return job_out


class _InChip:
    def __init__(self, ps):
        n = len(ps)
        self.n = n
        blk = [p.shape[1:] for p in ps]
        self.ins, self.in_specs = list(ps), [_HBM] * n
        self.out_shape = [_sds((NCHIP_OTHER,) + b, p.dtype) for b, p in zip(blk, ps)] + [_sds(b, F32) for b in blk]
        self.out_specs = [_VMEM] * (2 * n)
        self.scratch = ([pltpu.VMEM((4,) + b, p.dtype) for b, p in zip(blk, ps)] * 2
                        + [pltpu.SemaphoreType.DMA((4, n))] * 3)

    def ops(self, ins, outs, scr):
        n = self.n
        q_refs, acc_refs = outs[:n], outs[n:]
        mines, lands = scr[:n], scr[n:2 * n]
        send_sems, recv_sems, local_sems = scr[2 * n:]
        x, y, c = _place()
        sibling = (x, y, 1 - c)

        def copies():
            out = []
            for i in range(n):
                for pi in range(4):
                    loc = pltpu.make_async_copy(ins[i].at[2 * pi + c], mines[i].at[pi], local_sems.at[pi, i])
                    cp = pltpu.make_async_remote_copy(
                        src_ref=ins[i].at[2 * pi + (1 - c)], dst_ref=lands[i].at[pi],
                        send_sem=send_sems.at[pi, i], recv_sem=recv_sems.at[pi, i],
                        device_id=sibling, device_id_type=MESH)
                    out.append((loc, cp))
            return out

        def start():
            for loc, cp in copies():
                loc.start()
                cp.start()

        def finish():
            pairs = copies()
            for loc, cp in pairs:
                loc.wait()
                cp.wait_recv()
            for i in range(n):
                _chip_sums(mines[i], lands[i], q_refs[i], acc_refs[i], x, y)
            for _, cp in pairs:
                cp.wait_send()

        return start, lambda: None, finish


def _chip_sums(mine, land, q_ref, acc_ref, x, y):
    for j, (qx, qy) in enumerate(_other_chips(x, y)):
        qi = 2 * qx + qy
        q_ref[j] = (mine[qi].astype(F32) + land[qi].astype(F32)).astype(q_ref.dtype)
    mi = 2 * x + y
    acc_ref[...] = mine[mi].astype(F32) + land[mi].astype(F32)


def _direct_sum(v, buf, send_sems, recv_sems):
    x, y, c = _place()
    me = 4 * x + 2 * y + c
    buf[me] = v
    cps = []
    for k in range(1, NDEV):
        fx, fy, fc = (k >> 2) & 1, (k >> 1) & 1, k & 1
        peer = ((1 - x) if fx else x, (1 - y) if fy else y, (1 - c) if fc else c)
        cps.append((peer, pltpu.make_async_remote_copy(
            src_ref=buf.at[me], dst_ref=buf.at[me], send_sem=send_sems.at[k - 1], recv_sem=recv_sems.at[k - 1],
            device_id=peer, device_id_type=MESH)))
    for _, cp in cps:
        cp.start()
    for k, (peer, _) in enumerate(cps):
        theirs = 4 * peer[0] + 2 * peer[1] + peer[2]
        pltpu.make_async_remote_copy(
            src_ref=buf.at[theirs], dst_ref=buf.at[theirs], send_sem=send_sems.at[k], recv_sem=recv_sems.at[k],
            device_id=peer, device_id_type=MESH).wait_recv()
    acc = buf[0]
    for j in range(1, NDEV):
        acc = acc + buf[j]
    for _, cp in cps:
        cp.wait_send()
    return acc


def _direct_sum_scratch(shape, dtype):
    return [pltpu.VMEM((NDEV,) + tuple(shape), dtype), pltpu.SemaphoreType.DMA((NDEV - 1,)),
            pltpu.SemaphoreType.DMA((NDEV - 1,))]


def _fwd_a(x, nw, win8, lnw, lnb, ws, bst, jobs, *, tm, relay_step):
    s_len = x.shape[0]
    nt = s_len // tm
    nch = tm // CH

    def main(i, ins, outs, scr):
        x_ref, nw_ref, win_ref, lnw_ref, lnb_ref, ws_ref, bst_ref = ins
        z_ref, h_ref, y_ref, pp_ref = outs
        wc_scr, gv_scr = scr

        @pl.when(i == 0)
        def _():
            m = _causal_mask()
            for g in range(G):
                wc_scr[g] = jnp.where(m, ws_ref[g], 0.0).astype(BF16)

        x = x_ref[...]
        h = (x * _rms(x) * nw_ref[...]).astype(BF16)
        h_ref[...] = h
        for k in range(NDEV):
            z_ref[:, k * CA:(k + 1) * CA] = _dot(h, win_ref[k])

        ssum = jnp.zeros((tm, 1), F32)
        for g in range(G):
            vs = slice(AW + g * GD, AW + (g + 1) * GD)
            gv, pv = _gelu_t(z_ref[:, vs])
            pp_ref[:, vs] = pv.astype(BF16)
            gv_scr[:, g * GD:(g + 1) * GD] = gv
            ssum = ssum + jnp.sum(gv, axis=-1, keepdims=True)
        mu = ssum * (1.0 / AW)
        vsum = jnp.zeros((tm, 1), F32)
        for g in range(G):
            dlt = gv_scr[:, g * GD:(g + 1) * GD] - mu
            vsum = vsum + jnp.sum(dlt * dlt, axis=-1, keepdims=True)
        rstd = lax.rsqrt(vsum * (1.0 / AW) + LN_EPS)

        for g in range(G):
            cs = slice(g * GD, (g + 1) * GD)
            gs = slice(2 * AW + g * GD, 2 * AW + (g + 1) * GD)
            v = (gv_scr[:, cs] - mu) * rstd * lnw_ref[:, cs] + lnb_ref[:, cs]
            vb = v.astype(BF16)
            u, pu = _gelu_t(z_ref[:, cs])
            pp_ref[:, cs] = pu.astype(BF16)
            zg = z_ref[:, gs]
            sig = _sigmoid(zg)
            pp_ref[:, gs] = sig.astype(BF16)
            sg = zg * sig
            for n in range(nch):
                rs = slice(n * CH, (n + 1) * CH)
                s = _dot(wc_scr[g], vb[rs, :]) + bst_ref[:, g:g + 1]
                y_ref[rs, cs] = (u[rs, :] * s * sg[rs, :]).astype(BF16)

    tile = lambda w: pl.BlockSpec((tm, w), lambda i: (i, 0))
    return _call(
        main, jobs, name="fwd_a", grid=(nt,), relay_step=relay_step,
        ins=[x, nw, win8, lnw, lnb, ws, bst], in_specs=[tile(D), _VMEM, _VMEM, _VMEM, _VMEM, _VMEM, _VMEM],
        out_shape=[_sds((s_len, 3 * AW), F32), _sds((s_len, D), BF16), _sds((s_len, AW), BF16),
                   _sds((s_len, 3 * AW), BF16)],
        out_specs=[tile(3 * AW), tile(D), tile(AW), tile(3 * AW)],
        scratch=[pltpu.VMEM((G, CH, CH), BF16), pltpu.VMEM((tm, AW), F32)])


def _bwd_a(dx1, z, pp, lnw, lnb, ws, bst, wout, jobs, *, tm, relay_step):
    s_len = dx1.shape[0]
    nt = s_len // tm
    nch = tm // CH

    def main(i, ins, outs, scr):
        dx1_ref, z_ref, pp_ref, lnw_ref, lnb_ref, ws_ref, bst_ref, wout_ref = ins
        dz_ref, glnw_ref, glnb_ref, gws_ref, gbst_ref = outs
        wc_scr, wct_scr, vh_scr, dgv_scr, dy_scr, dv_scr, gbs_acc, gwc_acc = scr

        @pl.when(i == 0)
        def _():
            m = _causal_mask()
            for g in range(G):
                wm = jnp.where(m, ws_ref[g], 0.0)
                wc_scr[g] = wm.astype(BF16)
                wct_scr[g] = wm.T.astype(BF16)
            glnw_ref[...] = jnp.zeros_like(glnw_ref)
            glnb_ref[...] = jnp.zeros_like(glnb_ref)
            gbs_acc[...] = jnp.zeros_like(gbs_acc)
            gwc_acc[...] = jnp.zeros_like(gwc_acc)

        dy_scr[...] = _dot_nt(dx1_ref[...], wout_ref[...])

        ssum = jnp.zeros((tm, 1), F32)
        for g in range(G):
            cs = slice(g * GD, (g + 1) * GD)
            vs = slice(AW + g * GD, AW + (g + 1) * GD)
            zv = z_ref[:, vs]
            pv = pp_ref[:, vs].astype(F32)
            gv = zv * pv
            vh_scr[:, cs] = gv
            dgv_scr[:, cs] = _dgelu(zv, pv)
            ssum = ssum + jnp.sum(gv, axis=-1, keepdims=True)
        mu = ssum * (1.0 / AW)
        vsum = jnp.zeros((tm, 1), F32)
        for g in range(G):
            dlt = vh_scr[:, g * GD:(g + 1) * GD] - mu
            vsum = vsum + jnp.sum(dlt * dlt, axis=-1, keepdims=True)
        rstd = lax.rsqrt(vsum * (1.0 / AW) + LN_EPS)

        m1 = jnp.zeros((tm, 1), F32)
        m2 = jnp.zeros((tm, 1), F32)
        for g in range(G):
            cs = slice(g * GD, (g + 1) * GD)
            gs = slice(2 * AW + g * GD, 2 * AW + (g + 1) * GD)
            vhat = (vh_scr[:, cs] - mu) * rstd
            vh_scr[:, cs] = vhat
            vb = (vhat * lnw_ref[:, cs] + lnb_ref[:, cs]).astype(BF16)
            zu = z_ref[:, cs]
            tu = pp_ref[:, cs].astype(F32)
            u = zu * tu
            zg = z_ref[:, gs]
            sig = pp_ref[:, gs].astype(F32)
            sg = zg * sig
            dy = dy_scr[:, cs]
            dsf = dy * u * sg
            dsb = dsf.astype(BF16)
            dvs = []
            for n in range(nch):
                rs = slice(n * CH, (n + 1) * CH)
                s = _dot(wc_scr[g], vb[rs, :]) + bst_ref[:, g:g + 1]
                dys = dy[rs, :] * s
                dz_ref[rs, cs] = (dys * sg[rs, :] * _dgelu(zu[rs, :], tu[rs, :])).astype(BF16)
                dz_ref[rs, gs] = (dys * u[rs, :] * (sig[rs, :] * (1.0 + zg[rs, :] * (1.0 - sig[rs, :])))).astype(BF16)
                gbs_acc[g] += dsf[rs, :]
                gwc_acc[g] += _dot_nt(dsb[rs, :], vb[rs, :])
                dvs.append(_dot(wct_scr[g], dsb[rs, :]))
            dv = jnp.concatenate(dvs, axis=0) if nch > 1 else dvs[0]
            glnw_ref[:, cs] += _rowsum(dv * vhat)
            glnb_ref[:, cs] += _rowsum(dv)
            dvh = dv * lnw_ref[:, cs]
            dv_scr[:, cs] = dvh
            m1 = m1 + jnp.sum(dvh, axis=-1, keepdims=True)
            m2 = m2 + jnp.sum(dvh * vhat, axis=-1, keepdims=True)
        m1 = m1 * (1.0 / AW)
        m2 = m2 * (1.0 / AW)
        for g in range(G):
            cs = slice(g * GD, (g + 1) * GD)
            dgv = rstd * (dv_scr[:, cs] - m1 - vh_scr[:, cs] * m2)
            dz_ref[:, AW + g * GD:AW + (g + 1) * GD] = (dgv * dgv_scr[:, cs]).astype(BF16)

        @pl.when(i == nt - 1)
        def _():
            m = _causal_mask()
            for g in range(G):
                gws_ref[g] = jnp.where(m, gwc_acc[g], 0.0)
                gbst_ref[:, g:g + 1] = jnp.sum(gbs_acc[g], axis=-1, keepdims=True)

    tile = lambda w: pl.BlockSpec((tm, w), lambda i: (i, 0))
    whole = lambda *s: pl.BlockSpec(s, lambda i: (0,) * len(s))
    big = lambda dt: pltpu.VMEM((tm, AW), dt)
    return _call(
        main, jobs, name="bwd_a", grid=(nt,), relay_step=relay_step,
        ins=[dx1, z, pp, lnw, lnb, ws, bst, wout],
        in_specs=[tile(D), tile(3 * AW), tile(3 * AW), _VMEM, _VMEM, _VMEM, _VMEM, _VMEM],
        out_shape=[_sds((s_len, 3 * AW), BF16), _sds((1, AW), F32), _sds((1, AW), F32), _sds((G, CH, CH), F32),
                   _sds((CH, G), F32)],
        out_specs=[tile(3 * AW), whole(1, AW), whole(1, AW), whole(G, CH, CH), whole(CH, G)],
        scratch=[pltpu.VMEM((G, CH, CH), BF16), pltpu.VMEM((G, CH, CH), BF16), big(F32), big(F32), big(F32), big(F32),
                 pltpu.VMEM((G, CH, GD), F32), pltpu.VMEM((G, CH, CH), F32)])


def _bwd_a_in(dz, dx1, x, nw, win8, jobs, *, tm, relay_step):
    s_len = x.shape[0]
    nt = s_len // tm

    def main(i, ins, outs, scr):
        dz_ref, dx1_ref, x_ref, nw_ref, win_ref = ins
        gx_ref, gnw_ref = outs

        @pl.when(i == 0)
        def _():
            gnw_ref[...] = jnp.zeros_like(gnw_ref)

        dh = jnp.zeros((tm, D), F32)
        for k in range(NDEV):
            dh = dh + _dot_nt(dz_ref[:, k * CA:(k + 1) * CA], win_ref[k])
        x = x_ref[...]
        r = _rms(x)
        gx_ref[...] = dx1_ref[...] + _rms_bwd(dh, x, r, nw_ref[...])
        gnw_ref[...] += _rowsum(dh * x * r)

        @pl.when(i == nt - 1)
        def _():
            gnw_ref[...] = _direct_sum(gnw_ref[...], *scr)

    tile = lambda w: pl.BlockSpec((tm, w), lambda i: (i, 0))
    return _call(
        main, jobs, name="bwd_a_in", grid=(nt,), relay_step=relay_step,
        ins=[dz, dx1, x, nw, win8], in_specs=[tile(3 * AW), tile(D), tile(D), _VMEM, _VMEM],
        out_shape=[_sds((s_len, D), F32), _sds((1, D), F32)],
        out_specs=[tile(D), pl.BlockSpec((1, D), lambda i: (0, 0))], scratch=_direct_sum_scratch((1, D), F32))


def _conv(p8_ref, cs, xb, xm1, xm2, xm3):
    xc = p8_ref[4:5, cs] + p8_ref[3:4, cs] * xb
    xc = xc + p8_ref[0:1, cs] * xm3
    xc = xc + p8_ref[1:2, cs] * xm2
    return xc + p8_ref[2:3, cs] * xm1


def _gates(p8_ref, gcat_ref, hh, xc):
    cs = slice(hh * HD, (hh + 1) * HD)
    pre = _dot(xc.astype(BF16), gcat_ref[hh])
    r = _sigmoid(pre[:, :HD] + p8_ref[5:6, cs])
    ig = _sigmoid(pre[:, HD:] + p8_ref[6:7, cs])
    sp = _softplus_neg(p8_ref[7:8, cs])
    la = (-RG_C) * r * sp
    a = jnp.exp(la)
    half_log = 0.5 * jnp.log(jnp.tanh(-la) * (1.0 + a * a))
    return r, ig, sp, a, jnp.exp(half_log), jnp.exp(-half_log)


def _scan_rows(a_ref, b_ref, out_ref, carry, tm, reverse):
    row = lax.broadcasted_iota(jnp.int32, (SUBLANES, BW), 0)
    ngrp = tm // SUBLANES

    def step(j, cr):
        jj = (ngrp - 1 - j) if reverse else j
        off = pl.multiple_of(jj * SUBLANES, SUBLANES)
        a = a_ref[pl.ds(off, SUBLANES), :]
        b = b_ref[pl.ds(off, SUBLANES), :]
        for sh in (1, 2, 4):
            if reverse:
                a_s = pltpu.roll(a, SUBLANES - sh, 0)
                b_s = pltpu.roll(b, SUBLANES - sh, 0)
                m = row < SUBLANES - sh
            else:
                a_s = pltpu.roll(a, sh, 0)
                b_s = pltpu.roll(b, sh, 0)
                m = row >= sh
            b = jnp.where(m, a * b_s + b, b)
            a = jnp.where(m, a * a_s, a)
        o = b + a * cr
        out_ref[pl.ds(off, SUBLANES), :] = o
        return o[0:1, :] if reverse else o[SUBLANES - 1:SUBLANES, :]

    return lax.fori_loop(0, ngrp, step, carry)


def _fwd_b(x, ya, wout_a, nw, win8, p8, gcat, jobs, *, tm, relay_step):
    s_len = x.shape[0]
    nt = s_len // tm

    def main(i, ins, outs, scr):
        x_ref, ya_ref, wouta_ref, nw_ref, win_ref, p8_ref, gcat_ref = ins
        x1_ref, zb_ref, hs_ref, h1_ref, yb_ref, xc_ref, a_ref, cc_ref, r_ref, ig_ref, m_ref = outs
        xbe_scr, b_scr, k_scr, carry_scr = scr

        @pl.when(i == 0)
        def _():
            xbe_scr[0:SUBLANES, :] = jnp.zeros((SUBLANES, BW), F32)
            carry_scr[...] = jnp.zeros_like(carry_scr)

        x1 = x_ref[...] + _dot(ya_ref[...], wouta_ref[...])
        x1_ref[...] = x1
        h = (x1 * _rms(x1) * nw_ref[...]).astype(BF16)
        h1_ref[...] = h
        for k in range(NDEV):
            zb_ref[:, k * CB:(k + 1) * CB] = _dot(h, win_ref[k])
        xbe_scr[SUBLANES:SUBLANES + tm, :] = zb_ref[:, :BW]
        for hh in range(BH):
            cs = slice(hh * HD, (hh + 1) * HD)
            xc = _conv(p8_ref, cs, xbe_scr[SUBLANES:SUBLANES + tm, cs], xbe_scr[7:7 + tm, cs],
                       xbe_scr[6:6 + tm, cs], xbe_scr[5:5 + tm, cs])
            r, ig, _, a, mult, rm = _gates(p8_ref, gcat_ref, hh, xc)
            ixc = ig * xc
            xc_ref[:, cs] = xc
            a_ref[:, cs] = a
            r_ref[:, cs] = r.astype(BF16)
            ig_ref[:, cs] = ig.astype(BF16)
            m_ref[:, cs] = mult.astype(BF16)
            b_scr[:, cs] = mult * ixc
            k_scr[:, cs] = ixc * (a * a * rm)
        xbe_scr[0:SUBLANES, :] = xbe_scr[tm:tm + SUBLANES, :]
        carry_scr[...] = _scan_rows(a_ref, b_scr, hs_ref, carry_scr[...], tm, False)
        for hh in range(BH):
            cs = slice(hh * HD, (hh + 1) * HD)
            gt = zb_ref[:, BW + hh * HD:BW + (hh + 1) * HD]
            hsv = hs_ref[:, cs]
            yb_ref[:, cs] = (hsv * (gt * _sigmoid(gt))).astype(BF16)
            cc_ref[:, cs] = (hsv - b_scr[:, cs]) - k_scr[:, cs]

    tile = lambda w: pl.BlockSpec((tm, w), lambda i: (i, 0))
    wide = lambda dt: _sds((s_len, BW), dt)
    return _call(
        main, jobs, name="fwd_b", grid=(nt,), relay_step=relay_step,
        ins=[x, ya, wout_a, nw, win8, p8, gcat], in_specs=[tile(D), tile(AW), _VMEM, _VMEM, _VMEM, _VMEM, _VMEM],
        out_shape=[_sds((s_len, D), F32), _sds((s_len, 2 * BW), F32), wide(F32), _sds((s_len, D), BF16), wide(BF16),
                   wide(F32), wide(F32), wide(F32), wide(BF16), wide(BF16), wide(BF16)],
        out_specs=[tile(D), tile(2 * BW), tile(BW), tile(D)] + [tile(BW)] * 7,
        scratch=[pltpu.VMEM((tm + SUBLANES, BW), F32), pltpu.VMEM((tm, BW), F32), pltpu.VMEM((tm, BW), F32),
                 pltpu.VMEM((1, BW), F32)])


def _head(x1, yb, wout, nfw, tgt, *, tm):
    s_len = x1.shape[0]

    def main(i, ins, outs, scr):
        x1_ref, yb_ref, wout_ref, nfw_ref, t_ref = ins
        dx2_ref, dx2b_ref, loss_ref, gnfw_ref = outs

        @pl.when(i == 0)
        def _():
            loss_ref[...] = jnp.zeros_like(loss_ref)
            gnfw_ref[...] = jnp.zeros_like(gnfw_ref)

        x2 = x1_ref[...] + _dot(yb_ref[...], wout_ref[...])
        rf = _rms(x2)
        xn = x2 * rf
        e = xn * nfw_ref[...] - t_ref[...]
        loss_ref[...] += (0.5 / D) * jnp.sum(jnp.sum(e * e, axis=-1, keepdims=True), axis=0, keepdims=True)
        dyf = e * (1.0 / D)
        gnfw_ref[...] += _rowsum(dyf * xn)
        dx2 = _rms_bwd(dyf, x2, rf, nfw_ref[...])
        dx2_ref[...] = dx2
        dx2b_ref[...] = dx2.astype(BF16)

    tile = lambda w: pl.BlockSpec((tm, w), lambda i: (i, 0))
    whole = lambda *s: pl.BlockSpec(s, lambda i: (0,) * len(s))
    (dx2, dx2b, loss, gnfw), _ = _call(
        main, [], name="head", grid=(s_len // tm,),
        ins=[x1, yb, wout, nfw, tgt], in_specs=[tile(D), tile(BW), _VMEM, _VMEM, tile(D)],
        out_shape=[_sds((s_len, D), F32), _sds((s_len, D), BF16), _sds((1, 1), F32), _sds((1, D), F32)],
        out_specs=[tile(D), tile(D), whole(1, 1), whole(1, D)], scratch=[])
    return dx2, dx2b, loss, gnfw


def _bwd_b(dx2, zb, hs, x1, saved, nw, win8, p8, gcat, wout, *, tm):
    s_len = x1.shape[0]
    nt = s_len // tm

    def main(i, ins, outs, scr):
        (dx2_ref, zb_ref, hs_ref, x1_ref, xc_ref, a_ref, cc_ref, r_ref, ig_ref, m_ref,
         nw_ref, win_ref, p8_ref, gcat_ref, wout_ref) = ins
        dx1_ref, dx1b_ref, dzb_ref, gp8_ref, gga_ref, ggx_ref, gnw_ref = outs
        ae_scr, an_scr, dhd_scr, dh_scr, dy_scr, dxce_scr, carry_scr, afirst_scr = scr

        @pl.when(i == 0)
        def _():
            gp8_ref[...] = jnp.zeros_like(gp8_ref)
            gga_ref[...] = jnp.zeros_like(gga_ref)
            ggx_ref[...] = jnp.zeros_like(ggx_ref)
            gnw_ref[...] = jnp.zeros_like(gnw_ref)
            dxce_scr[tm:tm + SUBLANES, :] = jnp.zeros((SUBLANES, BW), F32)
            carry_scr[...] = jnp.zeros_like(carry_scr)
            afirst_scr[...] = jnp.zeros_like(afirst_scr)

        dx2 = dx2_ref[...]
        dy_scr[...] = _dot_nt(dx2.astype(BF16), wout_ref[...])
        for hh in range(BH):
            cs = slice(hh * HD, (hh + 1) * HD)
            gs = slice(BW + hh * HD, BW + (hh + 1) * HD)
            gt = zb_ref[:, gs]
            sig = _sigmoid(gt)
            dy = dy_scr[:, cs]
            dhd_scr[:, cs] = dy * (gt * sig)
            dzb_ref[:, gs] = (dy * hs_ref[:, cs] * (sig * (1.0 + gt * (1.0 - sig)))).astype(BF16)

        ae_scr[0:tm, :] = a_ref[...]
        ae_scr[tm:tm + SUBLANES, :] = jnp.broadcast_to(afirst_scr[...], (SUBLANES, BW))
        an_scr[...] = ae_scr[1:1 + tm, :]
        afirst_scr[...] = ae_scr[0:1, :]
        carry_scr[...] = _scan_rows(an_scr, dhd_scr, dh_scr, carry_scr[...], tm, True)

        for hh in range(BH):
            cs = slice(hh * HD, (hh + 1) * HD)
            dh = dh_scr[:, cs]
            mult = m_ref[:, cs].astype(F32)
            ig = ig_ref[:, cs].astype(F32)
            r = r_ref[:, cs].astype(F32)
            xc = xc_ref[:, cs]
            lam = p8_ref[7:8, cs]
            sp = _softplus_neg(lam)
            dla = dh * cc_ref[:, cs]
            gp8_ref[7:8, cs] += _rowsum(dla * ((-RG_C) * r)) * (-_sigmoid(-lam))
            dpr = dla * ((-RG_C) * sp) * (r * (1.0 - r))
            dpi = dh * mult * xc * (ig * (1.0 - ig))
            gp8_ref[5:6, cs] += _rowsum(dpr)
            gp8_ref[6:7, cs] += _rowsum(dpi)
            dcat = jnp.concatenate([dpr, dpi], axis=1).astype(BF16)
            dxc = dh * mult * ig + _dot_nt(dcat, gcat_ref[hh])
            gg = _dot(xc.T.astype(BF16), dcat)
            gga_ref[hh] += gg[:, :HD]
            ggx_ref[hh] += gg[:, HD:]
            dxce_scr[0:tm, cs] = dxc
            gp8_ref[4:5, cs] += _rowsum(dxc)
        for hh in range(BH):
            cs = slice(hh * HD, (hh + 1) * HD)
            xb = zb_ref[:, cs]
            d0, d1 = dxce_scr[0:tm, cs], dxce_scr[1:1 + tm, cs]
            d2, d3 = dxce_scr[2:2 + tm, cs], dxce_scr[3:3 + tm, cs]
            dzb_ref[:, cs] = (p8_ref[3:4, cs] * d0 + p8_ref[2:3, cs] * d1 + p8_ref[1:2, cs] * d2
                              + p8_ref[0:1, cs] * d3).astype(BF16)
            gp8_ref[3:4, cs] += _rowsum(d0 * xb)
            gp8_ref[2:3, cs] += _rowsum(d1 * xb)
            gp8_ref[1:2, cs] += _rowsum(d2 * xb)
            gp8_ref[0:1, cs] += _rowsum(d3 * xb)
        dxce_scr[tm:tm + SUBLANES, :] = dxce_scr[0:SUBLANES, :]

        dh1 = jnp.zeros((tm, D), F32)
        for k in range(NDEV):
            dh1 = dh1 + _dot_nt(dzb_ref[:, k * CB:(k + 1) * CB], win_ref[k])
        x1 = x1_ref[...]
        r1 = _rms(x1)
        dx1 = dx2 + _rms_bwd(dh1, x1, r1, nw_ref[...])
        dx1_ref[...] = dx1
        dx1b_ref[...] = dx1.astype(BF16)
        gnw_ref[...] += _rowsum(dh1 * x1 * r1)

    tile = lambda w: pl.BlockSpec((tm, w), lambda i: (nt - 1 - i, 0))
    whole = lambda *s: pl.BlockSpec(s, lambda i: (0,) * len(s))
    full = lambda: pltpu.VMEM((tm, BW), F32)
    ext = lambda: pltpu.VMEM((tm + SUBLANES, BW), F32)
    out, _ = _call(
        main, [], name="bwd_b", grid=(nt,),
        ins=[dx2, zb, hs, x1, *saved, nw, win8, p8, gcat, wout],
        in_specs=[tile(D), tile(2 * BW), tile(BW), tile(D)] + [tile(BW)] * 6 + [_VMEM] * 5,
        out_shape=[_sds((s_len, D), F32), _sds((s_len, D), BF16), _sds((s_len, 2 * BW), BF16), _sds((SUBLANES, BW), F32),
                   _sds((BH, HD, HD), F32), _sds((BH, HD, HD), F32), _sds((1, D), F32)],
        out_specs=[tile(D), tile(D), tile(2 * BW), whole(SUBLANES, BW), whole(BH, HD, HD), whole(BH, HD, HD),
                   whole(1, D)],
        scratch=[ext(), full(), full(), full(), full(), ext(), pltpu.VMEM((1, BW), F32), pltpu.VMEM((1, BW), F32)])
    return out


def _transpose_into(dst_ref, src_ref, rows):
    s_len = src_ref.shape[0]
    for r0 in range(0, s_len, rows):
        dst_ref[:, r0:r0 + rows] = src_ref[r0:r0 + rows, :].astype(F32).T.astype(BF16)


def _wgrad(a, b, jobs, *, by_rows, per, name, relay_step=0):
    s_len, m = a.shape
    n = b.shape[1]
    r, cd = (m // NDEV, n) if by_rows else (m, n // NDEV)
    nsteps = NDEV // per
    at_rows = per * r if by_rows else m

    def main(i, ins, outs, scr):
        a_ref, b_ref = ins
        q_ref, acc_ref = outs
        at_scr, stage, mine, land, send_sems, recv_sems = scr
        x, y, c = _place()

        def to_sibling(pi):
            return pltpu.make_async_remote_copy(
                src_ref=stage.at[pi & 1], dst_ref=land.at[pi], send_sem=send_sems.at[pi], recv_sem=recv_sems.at[pi],
                device_id=(x, y, 1 - c), device_id_type=MESH)

        if by_rows:
            _transpose_into(at_scr, a_ref, TRANSPOSE_ROWS)
        else:
            @pl.when(i == 0)
            def _():
                _transpose_into(at_scr, a_ref, TRANSPOSE_ROWS)

        res = _dot(at_scr[...], b_ref[...]).astype(BF16)
        for k in range(per):
            blk = per * i + k
            pi, pc = blk >> 1, blk & 1
            val = res[k * r:(k + 1) * r, :] if by_rows else res

            @pl.when(pc != c)
            def _():
                @pl.when(pi >= 2)
                def _():
                    to_sibling(pi - 2).wait_send()

                stage[pi & 1] = val
                to_sibling(pi).start()

            @pl.when(pc == c)
            def _():
                mine[pi] = val

        @pl.when(i == nsteps - 1)
        def _():
            for p in range(4):
                to_sibling(p).wait_recv()
            to_sibling(2).wait_send()
            to_sibling(3).wait_send()
            _chip_sums(mine, land, q_ref, acc_ref, x, y)

    if by_rows:
        in_specs = [pl.BlockSpec((s_len, at_rows), lambda j: (0, j)), _VMEM]
    else:
        in_specs = [_VMEM, pl.BlockSpec((s_len, cd), lambda j: (0, j))]
    blk_vmem = lambda k: pltpu.VMEM((k, r, cd), BF16)
    (q, acc), job_out = _call(
        main, jobs, name=name, grid=(nsteps,), relay_step=relay_step, ins=[a, b], in_specs=in_specs,
        out_shape=[_sds((NCHIP_OTHER, r, cd), BF16), _sds((r, cd), F32)],
        out_specs=[pl.BlockSpec((NCHIP_OTHER, r, cd), lambda j: (0, 0, 0)), pl.BlockSpec((r, cd), lambda j: (0, 0))],
        scratch=[pltpu.VMEM((at_rows, s_len), BF16), blk_vmem(2), blk_vmem(4), blk_vmem(4),
                 pltpu.SemaphoreType.DMA((4,)), pltpu.SemaphoreType.DMA((4,))])
    return q, acc, job_out


def _wgrad_cols_early(a, b, jobs, *, name, relay_step=0):
    s_len, m = a.shape
    r, cd = m, b.shape[1] // NDEV
    h = r // 2

    def chip_at(pos, base):
        return base ^ (3 - pos)

    def main(i, ins, outs, scr):
        a_ref, b_ref = ins
        q_ref, acc_ref, rel_ref = outs
        at_scr, stage, mine, land, q2_scr, send_sems, recv_sems, via_send, via_recv = scr
        x, y, c = _place()
        base = 2 * x + y
        xn, yn, _ = _other_chips(x, y)
        pos, pc = i >> 1, i & 1
        pi = chip_at(pos, base)

        def to_sibling(chip, slot):
            return pltpu.make_async_remote_copy(
                src_ref=stage.at[slot], dst_ref=land.at[chip], send_sem=send_sems.at[chip],
                recv_sem=recv_sems.at[chip], device_id=(x, y, 1 - c), device_id_type=MESH)

        def via(k):
            return pltpu.make_async_remote_copy(
                src_ref=q2_scr.at[pl.ds(k * h, h)], dst_ref=rel_ref.at[k], send_sem=via_send.at[k],
                recv_sem=via_recv.at[k], device_id=(*(xn, yn)[k], c), device_id_type=MESH)

        @pl.when(i == 0)
        def _():
            _transpose_into(at_scr, a_ref, TRANSPOSE_ROWS)

        res = _dot(at_scr[...], b_ref[...]).astype(BF16)

        @pl.when(pc != c)
        def _():
            @pl.when(pos >= 2)
            def _():
                to_sibling(chip_at(pos - 2, base), pos & 1).wait_send()

            stage[pos & 1] = res
            to_sibling(pi, pos & 1).start()

        @pl.when(pc == c)
        def _():
            mine[pi] = res

        @pl.when(i == 1)
        def _():
            dg = chip_at(0, base)
            to_sibling(dg, 0).wait_recv()
            q2 = (mine[dg].astype(F32) + land[dg].astype(F32)).astype(BF16)
            q2_scr[...] = q2
            q_ref[2] = q2
            via(0).start()
            via(1).start()

        @pl.when(i == NDEV - 1)
        def _():
            for pos_ in (1, 2, 3):
                to_sibling(chip_at(pos_, base), 0).wait_recv()
            to_sibling(chip_at(2, base), 0).wait_send()
            to_sibling(chip_at(3, base), 1).wait_send()
            for k in range(2):
                via(k).wait_recv()
            for k in range(2):
                via(k).wait_send()
            for j, chip in enumerate((base ^ 2, base ^ 1)):
                q_ref[j] = (mine[chip].astype(F32) + land[chip].astype(F32)).astype(BF16)
            acc_ref[...] = mine[base].astype(F32) + land[base].astype(F32)

    def b_block(j):
        base = 2 * lax.axis_index("x") + lax.axis_index("y")
        return (0, 2 * chip_at(j >> 1, base) + (j & 1))

    blk_vmem = lambda k: pltpu.VMEM((k, r, cd), BF16)
    (q, acc, rel), job_out = _call(
        main, jobs, name=name, grid=(NDEV,), relay_step=relay_step, ins=[a, b],
        in_specs=[_VMEM, pl.BlockSpec((s_len, cd), b_block)],
        out_shape=[_sds((NCHIP_OTHER, r, cd), BF16), _sds((r, cd), F32), _sds((2, h, cd), BF16)],
        out_specs=[pl.BlockSpec((NCHIP_OTHER, r, cd), lambda j: (0, 0, 0)), pl.BlockSpec((r, cd), lambda j: (0, 0)), _HBM],
        scratch=[pltpu.VMEM((m, s_len), BF16), blk_vmem(2), blk_vmem(4), blk_vmem(4), pltpu.VMEM((r, cd), BF16),
                 pltpu.SemaphoreType.DMA((4,)), pltpu.SemaphoreType.DMA((4,)), pltpu.SemaphoreType.DMA((2,)),
                 pltpu.SemaphoreType.DMA((2,))])
    return q, acc, rel, job_out


class _ExchangeRest:
    def __init__(self, q, relayed):
        _, r, cd = q.shape
        half = (2, r // 2, cd)
        self.ins, self.in_specs = [q, relayed], [_HBM, _HBM]
        self.out_shape, self.out_specs = [_sds((2, r, cd), q.dtype)], [_HBM]
        self.scratch = [pltpu.VMEM(half, q.dtype), pltpu.VMEM(half, q.dtype), pltpu.VMEM(half, q.dtype),
                        pltpu.SemaphoreType.DMA((4,)), pltpu.SemaphoreType.DMA((4,)), pltpu.SemaphoreType.DMA((4,))]

    def ops(self, ins, outs, scr):
        (q, rel_in), (land,) = ins, outs
        own, rel, comb, send_sems, recv_sems, local_sems = scr
        h = q.shape[1] // 2
        x, y, c = _place()
        xn, yn, _ = _other_chips(x, y)
        h0, h1 = pl.ds(0, h), pl.ds(h, h)

        def remote(k, src, dst, chip):
            return pltpu.make_async_remote_copy(src_ref=src, dst_ref=dst, send_sem=send_sems.at[k],
                                                recv_sem=recv_sems.at[k], device_id=(*chip, c), device_id_type=MESH)

        def sends():
            return [remote(0, q.at[0, h0], land.at[0, h0], xn), remote(1, q.at[1, h1], land.at[1, h1], yn),
                    remote(2, comb.at[0], land.at[1, h0], yn), remote(3, comb.at[1], land.at[0, h1], xn)]

        def loads():
            return [pltpu.make_async_copy(q.at[1, h0], own.at[0], local_sems.at[0]),
                    pltpu.make_async_copy(q.at[0, h1], own.at[1], local_sems.at[1]),
                    pltpu.make_async_copy(rel_in.at[0], rel.at[0], local_sems.at[2]),
                    pltpu.make_async_copy(rel_in.at[1], rel.at[1], local_sems.at[3])]

        def start():
            cps, lds = sends(), loads()
            for ld in lds:
                ld.start()
            cps[0].start()
            cps[1].start()
            for ld in lds:
                ld.wait()
            for k in range(2):
                comb[k] = (own[k].astype(F32) + rel[k].astype(F32)).astype(comb.dtype)
            cps[2].start()
            cps[3].start()

        def finish():
            cps = sends()
            for cp in cps:
                cp.wait_recv()
            for cp in cps:
                cp.wait_send()

        return start, lambda: None, finish


def _adam_math(w, g, m, v):
    m = B1 * m + (1.0 - B1) * g
    v = B2 * v + (1.0 - B2) * (g * g)
    m_hat = m / (1.0 - B1 ** STEP)
    v_hat = v / (1.0 - B2 ** STEP)
    delta = (-LR) * (m_hat / (jnp.sqrt(v_hat) + ADAM_EPS) + WD * w)
    return delta, m, v


def _adam_big(w, acc, land, m, v, name):
    r, cd = w.shape
    rb = ADAM_ROWS if r % ADAM_ROWS == 0 else r
    nland = land.shape[0]

    def body(w_ref, acc_ref, land_ref, m_ref, v_ref, g_ref, d_ref, mo_ref, vo_ref):
        g = acc_ref[...]
        for j in range(nland):
            g = g + land_ref[j].astype(F32)
        g_ref[...] = g
        d_ref[...], mo_ref[...], vo_ref[...] = _adam_math(w_ref[...], g, m_ref[...], v_ref[...])

    blk = pl.BlockSpec((rb, cd), lambda i: (i, 0))
    blk3 = pl.BlockSpec((nland, rb, cd), lambda i: (0, i, 0))
    return pl.pallas_call(
        body, name=name, grid=(r // rb,), in_specs=[blk, blk, blk3, blk, blk], out_specs=[blk] * 4,
        out_shape=[_sds((r, cd), F32)] * 4,
        compiler_params=_params(dimension_semantics=("arbitrary",)),
    )(w, acc, land, m, v)


def _adam_small(groups):
    n = len(groups)

    def body(*refs):
        ins, outs = refs[:4 * n], refs[4 * n:]
        for k in range(n):
            w_ref, g_ref, m_ref, v_ref = ins[4 * k:4 * k + 4]
            d, mo, vo = _adam_math(w_ref[...], g_ref[...], m_ref[...], v_ref[...])
            outs[3 * k][...] = d
            outs[3 * k + 1][...] = mo
            outs[3 * k + 2][...] = vo

    flat = [a for grp in groups for a in grp]
    shapes = [_sds(grp[0].shape, F32) for grp in groups for _ in range(3)]
    res = pl.pallas_call(
        body, name="adam_small", in_specs=[_VMEM] * (4 * n), out_specs=[_VMEM] * (3 * n), out_shape=shapes,
        compiler_params=_params(),
    )(*flat)
    return [tuple(res[3 * k:3 * k + 3]) for k in range(n)]


TM_FWD_A = 256
RELAY_STEP_FWD_A = 4
RELAY_STEP_FWD_B = 2
TM_BWD_A = 256
RELAY_STEP_BWD_A = 3
TM_BWD_A_IN = 256
RELAY_STEP_BWD_A_IN = 4
RELAY_STEP_WGRAD_A_IN = 2
TM_FWD_B = 256
TM_HEAD = 512
TM_BWD_B = 256


def _pack(parts, rows):
    flat = jnp.concatenate([p.reshape(-1) for p in parts])
    return jnp.pad(flat, (0, NDEV * rows * LANES - flat.shape[0])).reshape(NDEV, rows, LANES)


def _unpack(packed, shapes):
    flat, out, off = packed.reshape(-1), [], 0
    for s in shapes:
        size = 1
        for d in s:
            size *= d
        out.append(flat[off:off + size].reshape(s))
        off += size
    return out


def kernel(x, norm_w, a_w_in, a_ln_w, a_ln_b, a_w_s, a_b_s, a_w_out, b_w_in, b_conv_w, b_conv_b, b_gate_a_w, b_gate_a_b, b_gate_x_w, b_gate_x_b, b_lambda, b_w_out, norm_f_w, loss_target, m_norm_w, m_a_w_in, m_a_ln_w, m_a_ln_b, m_a_w_s, m_a_b_s, m_a_w_out, m_b_w_in, m_b_conv_w, m_b_conv_b, m_b_gate_a_w, m_b_gate_a_b, m_b_gate_x_w, m_b_gate_x_b, m_b_lambda, m_b_w_out, m_norm_f_w, v_norm_w, v_a_w_in, v_a_ln_w, v_a_ln_b, v_a_w_s, v_a_b_s, v_a_w_out, v_b_w_in, v_b_conv_w, v_b_conv_b, v_b_gate_a_w, v_b_gate_a_b, v_b_gate_x_w, v_b_gate_x_b, v_b_lambda, v_b_w_out, v_norm_f_w):
    me = 4 * lax.axis_index("x") + 2 * lax.axis_index("y") + lax.axis_index("c")
    xs, tgt = x[0], loss_target[0]
    nw0, nw1, nfw = norm_w[0:1], norm_w[1:2], norm_f_w.reshape(1, D)
    w_s, bst = a_w_s[0], a_b_s[0].T
    gcat = jnp.concatenate([b_gate_a_w[0], b_gate_x_w[0]], axis=-1).astype(BF16)

    p8_shard = jnp.concatenate([b_conv_w[0], b_conv_b, b_gate_a_b, b_gate_x_b, b_lambda], axis=0)
    ((win_a8, p8_all),) = _comm_only([_Gather([a_w_in[0], p8_shard], [BF16, F32])], "gather_first")
    p8 = jnp.transpose(p8_all, (1, 0, 2)).reshape(SUBLANES, BW)

    (z, h0, ya, pp), ((wout_a8, win_b8),) = _fwd_a(
        xs, nw0, win_a8, a_ln_w, a_ln_b, w_s, bst, [_Gather([a_w_out[0], b_w_in[0]], [BF16, BF16])],
        tm=TM_FWD_A, relay_step=RELAY_STEP_FWD_A)
    wout_a = wout_a8.reshape(AW, D)
    (x1, zb, hs, h1, yb, *saved_b), ((wout_b8,),) = _fwd_b(
        xs, ya, wout_a, nw1, win_b8, p8, gcat, [_Gather([b_w_out[0]], [BF16])],
        tm=TM_FWD_B, relay_step=RELAY_STEP_FWD_B)
    wout_b = wout_b8.reshape(BW, D)
    dx2, dx2b, loss, g_nfw = _head(x1, yb, wout_b, nfw, tgt, tm=TM_HEAD)

    dx1, dx1b, dzb, g_p8, g_ga, g_gx, g_nw1 = _bwd_b(dx2, zb, hs, x1, saved_b, nw1, win_b8, p8, gcat, wout_b,
                                                     tm=TM_BWD_B)
    q_wout_b, acc_wout_b, _ = _wgrad(yb, dx2b, [], by_rows=True, per=2, name="wgrad_b_out")
    shapes_b = [(1, D), (1, D), (SUBLANES, BW), (1, 1)]
    pack_b = _pack([g_nfw, g_nw1, g_p8, loss], 16)
    small_b = _InChip([g_ga.reshape(NDEV, -1, HD), g_gx.reshape(NDEV, -1, HD), pack_b])
    q_win_b, acc_win_b, (sm_b, (l_wout_b,)) = _wgrad(h1, dzb, [small_b, _Exchange([q_wout_b])], by_rows=False, per=1,
                                                      name="wgrad_b_in")
    qs_b, accs_b = sm_b[:3], sm_b[3:]

    (dz, g_lnw, g_lnb, g_ws, g_bst), (lands_b, (l_win_b,)) = _bwd_a(
        dx1b, z, pp, a_ln_w, a_ln_b, w_s, bst, wout_a, [_Exchange(qs_b), _ExchangeVia(q_win_b)],
        tm=TM_BWD_A, relay_step=RELAY_STEP_BWD_A)
    shapes_a = [(1, AW), (1, AW), (CH, G)]
    pack_a = _pack([g_lnw, g_lnb, g_bst], 8)
    q_wout_a, acc_wout_a, (red_b, sm_a) = _wgrad(
        ya, dx1b, [_SumGather(accs_b, lands_b), _InChip([g_ws, pack_a])], by_rows=True, per=2,
        name="wgrad_a_out", relay_step=1)
    qs_a, accs_a = sm_a[:2], sm_a[2:]
    q_win_a, acc_win_a, rel_a, (lands_a, (l_wout_a,)) = _wgrad_cols_early(
        h0, dz, [_Exchange(qs_a), _ExchangeVia(q_wout_a)], name="wgrad_a_in", relay_step=RELAY_STEP_WGRAD_A_IN)
    (gx, g_nw0), (red_a, (l_win_a,)) = _bwd_a_in(
        dz, dx1, xs, nw0, win_a8, [_SumGather(accs_a, lands_a), _ExchangeRest(q_win_a, rel_a)],
        tm=TM_BWD_A_IN, relay_step=RELAY_STEP_BWD_A_IN)

    r_ga, r_gx, r_pack_b = red_b
    r_nfw, r_nw1, r_p8, loss = _unpack(r_pack_b, shapes_b)
    r_ws, r_pack_a = red_a
    r_lnw, r_lnb, r_bst = _unpack(r_pack_a, shapes_a)
    g_p8 = lax.dynamic_slice_in_dim(r_p8, me * (BW // NDEV), BW // NDEV, axis=1)
    loss = loss[0, 0]

    weights = dict(norm_w=norm_w, a_w_in=a_w_in, a_ln_w=a_ln_w, a_ln_b=a_ln_b, a_w_s=a_w_s, a_b_s=a_b_s, a_w_out=a_w_out,
                   b_w_in=b_w_in, b_conv_w=b_conv_w, b_conv_b=b_conv_b, b_gate_a_w=b_gate_a_w, b_gate_a_b=b_gate_a_b,
                   b_gate_x_w=b_gate_x_w, b_gate_x_b=b_gate_x_b, b_lambda=b_lambda, b_w_out=b_w_out, norm_f_w=norm_f_w)
    mom1 = dict(norm_w=m_norm_w, a_w_in=m_a_w_in, a_ln_w=m_a_ln_w, a_ln_b=m_a_ln_b, a_w_s=m_a_w_s, a_b_s=m_a_b_s,
                a_w_out=m_a_w_out, b_w_in=m_b_w_in, b_conv_w=m_b_conv_w, b_conv_b=m_b_conv_b, b_gate_a_w=m_b_gate_a_w,
                b_gate_a_b=m_b_gate_a_b, b_gate_x_w=m_b_gate_x_w, b_gate_x_b=m_b_gate_x_b, b_lambda=m_b_lambda,
                b_w_out=m_b_w_out, norm_f_w=m_norm_f_w)
    mom2 = dict(norm_w=v_norm_w, a_w_in=v_a_w_in, a_ln_w=v_a_ln_w, a_ln_b=v_a_ln_b, a_w_s=v_a_w_s, a_b_s=v_a_b_s,
                a_w_out=v_a_w_out, b_w_in=v_b_w_in, b_conv_w=v_b_conv_w, b_conv_b=v_b_conv_b, b_gate_a_w=v_b_gate_a_w,
                b_gate_a_b=v_b_gate_a_b, b_gate_x_w=v_b_gate_x_w, b_gate_x_b=v_b_gate_x_b, b_lambda=v_b_lambda,
                b_w_out=v_b_w_out, norm_f_w=v_norm_f_w)
    names = list(weights)

    def as2d(a):
        return a.reshape(-1, a.shape[-1])

    upd, grads = {}, {}
    for k, acc, land in (("a_w_in", acc_win_a, l_win_a), ("a_w_out", acc_wout_a, l_wout_a),
                         ("b_w_in", acc_win_b, l_win_b), ("b_w_out", acc_wout_b, l_wout_b)):
        g, d, mo, vo = _adam_big(as2d(weights[k]), acc, land, as2d(mom1[k]), as2d(mom2[k]), "adam_" + k)
        grads[k] = g[None]
        upd[k] = (d, mo, vo)
    grads.update(
        norm_w=jnp.concatenate([g_nw0, r_nw1], axis=0), a_ln_w=r_lnw, a_ln_b=r_lnb,
        a_w_s=r_ws.reshape(1, G, CH, CH), a_b_s=r_bst.T[None],
        b_conv_w=g_p8[None, 0:4], b_conv_b=g_p8[4:5], b_gate_a_w=r_ga.reshape(1, BH, HD, HD), b_gate_a_b=g_p8[5:6],
        b_gate_x_w=r_gx.reshape(1, BH, HD, HD), b_gate_x_b=g_p8[6:7], b_lambda=g_p8[7:8], norm_f_w=r_nfw.reshape(D))
    small_names = [k for k in names if k not in upd]
    res = _adam_small([(as2d(weights[k]), as2d(grads[k]), as2d(mom1[k]), as2d(mom2[k])) for k in small_names])
    for k, r3 in zip(small_names, res):
        upd[k] = r3
    deltas = [upd[k][0].reshape(weights[k].shape) for k in names]
    new_m = [upd[k][1].reshape(weights[k].shape) for k in names]
    new_v = [upd[k][2].reshape(weights[k].shape) for k in names]
    return (loss, gx[None], *[grads[k] for k in names], *deltas, *new_m, *new_v)
```

```python
import jax
import jax.numpy as jnp
from jax import lax
from jax.experimental import pallas as pl
from jax.experimental.pallas import tpu as pltpu

F32 = jnp.float32
BF16 = jnp.bfloat16
MESH = pl.DeviceIdType.MESH

NDEV = 8
NCHIP_OTHER = 3
D = 1024
AW = 2048
G = 8
GD = AW // G
CH = 128
BW = 1536
BH = 12
HD = BW // BH
CA = 3 * AW // NDEV
CB = 2 * BW // NDEV
RMS_EPS = 1e-6
LN_EPS = 1e-5
RG_C = 8.0
LR, B1, B2, ADAM_EPS, WD, STEP = 0.001, 0.9, 0.999, 1e-08, 0.01, 10
V7X_VMEM_BYTES = 64 * 1024 * 1024
VMEM_LIMIT = V7X_VMEM_BYTES - 8 * 1024 * 1024
SUBLANES = 8
LANES = 128
BF16_ROWS = 16
TRANSPOSE_ROWS = 256
ADAM_ROWS = 512
GELU_C = 0.7978845608028654
GELU_K = 0.044715

_VMEM = pl.BlockSpec(memory_space=pltpu.VMEM)
_HBM = pl.BlockSpec(memory_space=pltpu.HBM)


def _sds(shape, dtype):
    return jax.ShapeDtypeStruct(tuple(shape), dtype)


def _params(**kw):
    return pltpu.CompilerParams(vmem_limit_bytes=VMEM_LIMIT, **kw)


def _gelu_t(z):
    p = 0.5 * jnp.tanh(z * (GELU_C + (GELU_C * GELU_K) * (z * z))) + 0.5
    return z * p, p


def _dgelu(z, p):
    return p * (1.0 + (z * (1.0 - p)) * (2.0 * GELU_C + (6.0 * GELU_C * GELU_K) * (z * z)))


def _sigmoid(v):
    return 0.5 * jnp.tanh(0.5 * v) + 0.5


def _softplus_neg(lam):
    return jnp.maximum(-lam, 0.0) + jnp.log1p(jnp.exp(-jnp.abs(lam)))


def _dot(a, b):
    return jnp.dot(a, b, preferred_element_type=F32)


def _dot_nt(a, b):
    return lax.dot_general(a, b, (((1,), (1,)), ((), ())), preferred_element_type=F32)


def _rowsum(v):
    return jnp.sum(v, axis=0, keepdims=True)


def _causal_mask():
    r = lax.broadcasted_iota(jnp.int32, (CH, CH), 0)
    c = lax.broadcasted_iota(jnp.int32, (CH, CH), 1)
    return r >= c


def _rms(x):
    return lax.rsqrt(jnp.mean(x * x, axis=-1, keepdims=True) + RMS_EPS)


def _rms_bwd(dh, x, r, nw):
    gy = dh * nw
    return r * gy - x * (r * r * r) * jnp.mean(gy * x, axis=-1, keepdims=True)


def _place():
    return lax.axis_index("x"), lax.axis_index("y"), lax.axis_index("c")


def _other_chips(x, y):
    return [(1 - x, y), (x, 1 - y), (1 - x, 1 - y)]


GATHER_SLOTS = 10


def _gather_ops(ins, outs, send_sems, recv_sems, local_sems):
    n = len(ins)
    x, y, c = _place()
    sibling = (x, y, 1 - c)
    xn, yn, dg = _other_chips(x, y)
    split = [ins[i].shape[0] % (2 * BF16_ROWS) == 0 for i in range(n)]

    def blk(chip, core):
        return 4 * chip[0] + 2 * chip[1] + core

    me = blk((x, y), c)

    def part(ref, i, half):
        if half is None:
            return ref
        h = ins[i].shape[0] // 2
        return ref.at[pl.ds(half * h, h)]

    def copy(i, k, block, to, half=None, src=None):
        dst = part(outs[i].at[block], i, half)
        return pltpu.make_async_remote_copy(
            src_ref=dst if src is None else part(src, i, half), dst_ref=dst,
            send_sem=send_sems.at[k, i], recv_sem=recv_sems.at[k, i], device_id=to, device_id_type=MESH)

    def first_copies():
        mine = [pltpu.make_async_copy(ins[i], outs[i].at[me], local_sems.at[i]) for i in range(n)]
        first = []
        for i in range(n):
            first.append(copy(i, 0, me, sibling, src=ins[i]))
            if split[i]:
                first.append(copy(i, 1, me, (*xn, c), 0, ins[i]))
                first.append(copy(i, 3, me, (*yn, c), 1, ins[i]))
                first.append(copy(i, 2, me, (*xn, c), 1, ins[i]))
                first.append(copy(i, 4, me, (*yn, c), 0, ins[i]))
            else:
                first.append(copy(i, 1, me, (*xn, c), None, ins[i]))
                first.append(copy(i, 3, me, (*yn, c), None, ins[i]))
                first.append(copy(i, 5, me, (*dg, c), None, ins[i]))
        return mine, first

    def onward():
        out = []
        for i in range(n):
            if split[i]:
                out.append(copy(i, 5, blk(xn, c), (*yn, c), 0))
                out.append(copy(i, 6, blk(yn, c), (*xn, c), 1))
        return out

    def start():
        mine, first = first_copies()
        for cp in mine + first:
            cp.start()

    def relay():
        sends = onward()
        for i in range(n):
            if split[i]:
                copy(i, 1, blk(xn, c), sibling, 0).wait_recv()
                sends.pop(0).start()
                copy(i, 3, blk(yn, c), sibling, 1).wait_recv()
                sends.pop(0).start()

    def finish():
        mine, first = first_copies()
        passed = []

        def pass_on(i, j, chip):
            fwd = copy(i, 7 + j, blk(chip, c), sibling)
            fwd.start()
            passed.append(fwd)

        for i in range(n):
            if split[i]:
                copy(i, 2, blk(xn, c), sibling, 1).wait_recv()
                pass_on(i, 0, xn)
                copy(i, 4, blk(yn, c), sibling, 0).wait_recv()
                pass_on(i, 1, yn)
                copy(i, 5, blk(dg, c), sibling, 0).wait_recv()
                copy(i, 6, blk(dg, c), sibling, 1).wait_recv()
                pass_on(i, 2, dg)
            else:
                copy(i, 1, blk(xn, c), sibling).wait_recv()
                pass_on(i, 0, xn)
                copy(i, 3, blk(yn, c), sibling).wait_recv()
                pass_on(i, 1, yn)
                copy(i, 5, blk(dg, c), sibling).wait_recv()
                pass_on(i, 2, dg)
        for i in range(n):
            copy(i, 0, blk((x, y), 1 - c), sibling).wait_recv()
            for j, chip in enumerate((xn, yn, dg)):
                copy(i, 7 + j, blk(chip, 1 - c), sibling).wait_recv()
        for cp in first + passed + onward():
            cp.wait_send()
        for cp in mine:
            cp.wait()

    return start, relay, finish


def _gather_sems(n):
    return [pltpu.SemaphoreType.DMA((GATHER_SLOTS, n)), pltpu.SemaphoreType.DMA((GATHER_SLOTS, n)),
            pltpu.SemaphoreType.DMA((n,))]


class _Gather:
    def __init__(self, shards, as_dtypes=None):
        n = len(shards)
        dts = [s.dtype for s in shards] if as_dtypes is None else list(as_dtypes)
        self.cast = [jnp.dtype(d) != s.dtype for d, s in zip(dts, shards)]
        self.ins = list(shards)
        self.in_specs = [_VMEM if c else _HBM for c in self.cast]
        self.out_shape = [_sds((NDEV,) + s.shape, d) for s, d in zip(shards, dts)]
        self.out_specs = [_HBM] * n
        self.scratch = [pltpu.VMEM(s.shape, d) for s, d, c in zip(shards, dts, self.cast) if c] + _gather_sems(n)

    def ops(self, ins, outs, scr):
        ncast = sum(self.cast)
        staged = iter(scr[:ncast])
        srcs = [next(staged) if c else ref for c, ref in zip(self.cast, ins)]
        start, relay, finish = _gather_ops(srcs, outs, *scr[ncast:])

        def cast_and_start():
            for c, ref, src in zip(self.cast, ins, srcs):
                if c:
                    src[...] = ref[...].astype(src.dtype)
            start()

        return cast_and_start, relay, finish


class _Exchange:
    def __init__(self, qs):
        n = len(qs)
        self.ins, self.in_specs = list(qs), [_HBM] * n
        self.out_shape = [_sds(q.shape, q.dtype) for q in qs]
        self.out_specs = [_HBM] * n
        self.scratch = [pltpu.SemaphoreType.DMA((NCHIP_OTHER, n)), pltpu.SemaphoreType.DMA((NCHIP_OTHER, n))]

    def ops(self, ins, outs, scr):
        send_sems, recv_sems = scr
        n = len(ins)
        x, y, c = _place()
        chips = _other_chips(x, y)

        def copies():
            return [pltpu.make_async_remote_copy(
                src_ref=ins[i].at[j], dst_ref=outs[i].at[j], send_sem=send_sems.at[j, i],
                recv_sem=recv_sems.at[j, i], device_id=(*chips[j], c), device_id_type=MESH)
                for i in range(n) for j in range(NCHIP_OTHER)]

        def start():
            for cp in copies():
                cp.start()

        def finish():
            cps = copies()
            for cp in cps:
                cp.wait_recv()
            for cp in cps:
                cp.wait_send()

        return start, lambda: None, finish


class _ExchangeVia:
    def __init__(self, q):
        _, r, cd = q.shape
        half = (2, r // 2, cd)
        self.ins, self.in_specs = [q], [_HBM]
        self.out_shape, self.out_specs = [_sds((2, r, cd), q.dtype)], [_HBM]
        self.scratch = [pltpu.VMEM(half, q.dtype), pltpu.VMEM(half, q.dtype), pltpu.VMEM(half, q.dtype),
                        pltpu.SemaphoreType.DMA((6,)), pltpu.SemaphoreType.DMA((6,)), pltpu.SemaphoreType.DMA((2,))]

    def ops(self, ins, outs, scr):
        (q,), (land,) = ins, outs
        relayed, own, comb, send_sems, recv_sems, local_sems = scr
        h = q.shape[1] // 2
        x, y, c = _place()
        xn, yn, _ = _other_chips(x, y)
        h0, h1 = pl.ds(0, h), pl.ds(h, h)

        def remote(k, src, dst, chip):
            return pltpu.make_async_remote_copy(src_ref=src, dst_ref=dst, send_sem=send_sems.at[k],
                                                recv_sem=recv_sems.at[k], device_id=(*chip, c), device_id_type=MESH)

        def via():
            return [remote(2, q.at[2, h0], relayed.at[0], xn), remote(3, q.at[2, h1], relayed.at[1], yn)]

        def direct():
            return [remote(0, q.at[0, h0], land.at[0, h0], xn), remote(1, q.at[1, h1], land.at[1, h1], yn)]

        def second():
            return [remote(4, comb.at[0], land.at[1, h0], yn), remote(5, comb.at[1], land.at[0, h1], xn)]

        def mine():
            return [pltpu.make_async_copy(q.at[1, h0], own.at[0], local_sems.at[0]),
                    pltpu.make_async_copy(q.at[0, h1], own.at[1], local_sems.at[1])]

        def start():
            for cp in via() + direct() + mine():
                cp.start()

        def relay():
            arrived, loaded, onward = via(), mine(), second()
            for k in range(2):
                arrived[k].wait_recv()
                loaded[k].wait()
                comb[k] = (own[k].astype(F32) + relayed[k].astype(F32)).astype(comb.dtype)
                onward[k].start()

        def finish():
            landing = direct() + second()
            for cp in landing:
                cp.wait_recv()
            for cp in via() + landing:
                cp.wait_send()

        return start, relay, finish


class _SumGather:
    def __init__(self, accs, lands):
        n = len(accs)
        self.n = n
        self.ins, self.in_specs = list(accs) + list(lands), [_VMEM] * (2 * n)
        self.out_shape = [_sds((NDEV,) + a.shape, a.dtype) for a in accs]
        self.out_specs = [_HBM] * n
        self.scratch = [pltpu.VMEM(a.shape, a.dtype) for a in accs] + _gather_sems(n)

    def ops(self, ins, outs, scr):
        n = self.n
        accs, lands, mine = ins[:n], ins[n:], scr[:n]
        g_start, relay, finish = _gather_ops(mine, outs, *scr[n:])

        def start():
            for i in range(n):
                mine[i][...] = accs[i][...] + lands[i][0] + lands[i][1] + lands[i][2]
            g_start()

        return start, relay, finish


def _call(main, jobs, *, name, grid, ins, in_specs, out_shape, out_specs, scratch, relay_step=0):
    nsteps = grid[0] if grid else 1
    n_in, n_out, n_scr = len(ins), len(out_shape), len(scratch)

    def body(*refs):
        pos = [0]

        def take(k):
            r = refs[pos[0]:pos[0] + k]
            pos[0] += k
            return r

        m_in = take(n_in)
        j_in = [take(len(j.ins)) for j in jobs]
        m_out = take(n_out)
        j_out = [take(len(j.out_shape)) for j in jobs]
        m_scr = take(n_scr)
        j_scr = [take(len(j.scratch)) for j in jobs]
        ops = [j.ops(a, b, s) for j, a, b, s in zip(jobs, j_in, j_out, j_scr)]
        i = pl.program_id(0) if grid else 0
        if not grid:
            for o in ops:
                o[0]()
            main(i, m_in, m_out, m_scr)
            for o in ops:
                o[1]()
            for o in ops:
                o[2]()
            return

        if ops:
            @pl.when(i == 0)
            def _():
                for o in ops:
                    o[0]()

        main(i, m_in, m_out, m_scr)

        if ops:
            @pl.when(i == min(relay_step, nsteps - 1))
            def _():
                for o in ops:
                    o[1]()

            @pl.when(i == nsteps - 1)
            def _():
                for o in ops:
                    o[2]()

    extra = dict(dimension_semantics=("arbitrary",)) if grid else {}
    res = pl.pallas_call(
        body, name=name, grid=grid,
        in_specs=list(in_specs) + [s for j in jobs for s in j.in_specs],
        out_specs=list(out_specs) + [s for j in jobs for s in j.out_specs],
        out_shape=list(out_shape) + [s for j in jobs for s in j.out_shape],
        scratch_shapes=list(scratch) + [s for j in jobs for s in j.scratch],
        compiler_params=_params(**extra),
    )(*ins, *[a for j in jobs for a in j.ins])
    main_out, rest, job_out = res[:n_out], res[n_out:], []
    for j in jobs:
        k = len(j.out_shape)
        job_out.append(rest[:k])
        rest = rest[k:]
    return main_out, job_out


def _comm_only(jobs, name):
    _, job_out = _call(lambda i, a, b, s: None, jobs, name=name, grid=(), ins=[], in_specs=[], out_shape=[],
                       out_specs=[], scratch=[])
    return job_out


class _InChip:
    def __init__(self, ps):
        n = len(ps)
        self.n = n
        blk = [p.shape[1:] for p in ps]
        self.ins, self.in_specs = list(ps), [_HBM] * n
        self.out_shape = [_sds((NCHIP_OTHER,) + b, p.dtype) for b, p in zip(blk, ps)] + [_sds(b, F32) for b in blk]
        self.out_specs = [_VMEM] * (2 * n)
        self.scratch = ([pltpu.VMEM((4,) + b, p.dtype) for b, p in zip(blk, ps)] * 2
                        + [pltpu.SemaphoreType.DMA((4, n))] * 3)

    def ops(self, ins, outs, scr):
        n = self.n
        q_refs, acc_refs = outs[:n], outs[n:]
        mines, lands = scr[:n], scr[n:2 * n]
        send_sems, recv_sems, local_sems = scr[2 * n:]
        x, y, c = _place()
        sibling = (x, y, 1 - c)

        def copies():
            out = []
            for i in range(n):
                for pi in range(4):
                    loc = pltpu.make_async_copy(ins[i].at[2 * pi + c], mines[i].at[pi], local_sems.at[pi, i])
                    cp = pltpu.make_async_remote_copy(
                        src_ref=ins[i].at[2 * pi + (1 - c)], dst_ref=lands[i].at[pi],
                        send_sem=send_sems.at[pi, i], recv_sem=recv_sems.at[pi, i],
                        device_id=sibling, device_id_type=MESH)
                    out.append((loc, cp))
            return out

        def start():
            for loc, cp in copies():
                loc.start()
                cp.start()

        def finish():
            pairs = copies()
            for loc, cp in pairs:
                loc.wait()
                cp.wait_recv()
            for i in range(n):
                _chip_sums(mines[i], lands[i], q_refs[i], acc_refs[i], x, y)
            for _, cp in pairs:
                cp.wait_send()

        return start, lambda: None, finish


def _chip_sums(mine, land, q_ref, acc_ref, x, y):
    for j, (qx, qy) in enumerate(_other_chips(x, y)):
        qi = 2 * qx + qy
        q_ref[j] = (mine[qi].astype(F32) + land[qi].astype(F32)).astype(q_ref.dtype)
    mi = 2 * x + y
    acc_ref[...] = mine[mi].astype(F32) + land[mi].astype(F32)


def _direct_sum(v, buf, send_sems, recv_sems):
    x, y, c = _place()
    me = 4 * x + 2 * y + c
    buf[me] = v
    cps = []
    for k in range(1, NDEV):
        fx, fy, fc = (k >> 2) & 1, (k >> 1) & 1, k & 1
        peer = ((1 - x) if fx else x, (1 - y) if fy else y, (1 - c) if fc else c)
        cps.append((peer, pltpu.make_async_remote_copy(
            src_ref=buf.at[me], dst_ref=buf.at[me], send_sem=send_sems.at[k - 1], recv_sem=recv_sems.at[k - 1],
            device_id=peer, device_id_type=MESH)))
    for _, cp in cps:
        cp.start()
    for k, (peer, _) in enumerate(cps):
        theirs = 4 * peer[0] + 2 * peer[1] + peer[2]
        pltpu.make_async_remote_copy(
            src_ref=buf.at[theirs], dst_ref=buf.at[theirs], send_sem=send_sems.at[k], recv_sem=recv_sems.at[k],
            device_id=peer, device_id_type=MESH).wait_recv()
    acc = buf[0]
    for j in range(1, NDEV):
        acc = acc + buf[j]
    for _, cp in cps:
        cp.wait_send()
    return acc


def _direct_sum_scratch(shape, dtype):
    return [pltpu.VMEM((NDEV,) + tuple(shape), dtype), pltpu.SemaphoreType.DMA((NDEV - 1,)),
            pltpu.SemaphoreType.DMA((NDEV - 1,))]


def _fwd_a(x, nw, win8, lnw, lnb, ws, bst, jobs, *, tm, relay_step):
    s_len = x.shape[0]
    nt = s_len // tm
    nch = tm // CH

    def main(i, ins, outs, scr):
        x_ref, nw_ref, win_ref, lnw_ref, lnb_ref, ws_ref, bst_ref = ins
        z_ref, h_ref, y_ref, pp_ref = outs
        wc_scr, gv_scr = scr

        @pl.when(i == 0)
        def _():
            m = _causal_mask()
            for g in range(G):
                wc_scr[g] = jnp.where(m, ws_ref[g], 0.0).astype(BF16)

        x = x_ref[...]
        h = (x * _rms(x) * nw_ref[...]).astype(BF16)
        h_ref[...] = h
        for k in range(NDEV):
            z_ref[:, k * CA:(k + 1) * CA] = _dot(h, win_ref[k])

        ssum = jnp.zeros((tm, 1), F32)
        for g in range(G):
            vs = slice(AW + g * GD, AW + (g + 1) * GD)
            gv, pv = _gelu_t(z_ref[:, vs])
            pp_ref[:, vs] = pv.astype(BF16)
            gv_scr[:, g * GD:(g + 1) * GD] = gv
            ssum = ssum + jnp.sum(gv, axis=-1, keepdims=True)
        mu = ssum * (1.0 / AW)
        vsum = jnp.zeros((tm, 1), F32)
        for g in range(G):
            dlt = gv_scr[:, g * GD:(g + 1) * GD] - mu
            vsum = vsum + jnp.sum(dlt * dlt, axis=-1, keepdims=True)
        rstd = lax.rsqrt(vsum * (1.0 / AW) + LN_EPS)

        for g in range(G):
            cs = slice(g * GD, (g + 1) * GD)
            gs = slice(2 * AW + g * GD, 2 * AW + (g + 1) * GD)
            v = (gv_scr[:, cs] - mu) * rstd * lnw_ref[:, cs] + lnb_ref[:, cs]
            vb = v.astype(BF16)
            u, pu = _gelu_t(z_ref[:, cs])
            pp_ref[:, cs] = pu.astype(BF16)
            zg = z_ref[:, gs]
            sig = _sigmoid(zg)
            pp_ref[:, gs] = sig.astype(BF16)
            sg = zg * sig
            for n in range(nch):
                rs = slice(n * CH, (n + 1) * CH)
                s = _dot(wc_scr[g], vb[rs, :]) + bst_ref[:, g:g + 1]
                y_ref[rs, cs] = (u[rs, :] * s * sg[rs, :]).astype(BF16)

    tile = lambda w: pl.BlockSpec((tm, w), lambda i: (i, 0))
    return _call(
        main, jobs, name="fwd_a", grid=(nt,), relay_step=relay_step,
        ins=[x, nw, win8, lnw, lnb, ws, bst], in_specs=[tile(D), _VMEM, _VMEM, _VMEM, _VMEM, _VMEM, _VMEM],
        out_shape=[_sds((s_len, 3 * AW), F32), _sds((s_len, D), BF16), _sds((s_len, AW), BF16),
                   _sds((s_len, 3 * AW), BF16)],
        out_specs=[tile(3 * AW), tile(D), tile(AW), tile(3 * AW)],
        scratch=[pltpu.VMEM((G, CH, CH), BF16), pltpu.VMEM((tm, AW), F32)])


def _bwd_a(dx1, z, pp, lnw, lnb, ws, bst, wout, jobs, *, tm, relay_step):
    s_len = dx1.shape[0]
    nt = s_len // tm
    nch = tm // CH

    def main(i, ins, outs, scr):
        dx1_ref, z_ref, pp_ref, lnw_ref, lnb_ref, ws_ref, bst_ref, wout_ref = ins
        dz_ref, glnw_ref, glnb_ref, gws_ref, gbst_ref = outs
        wc_scr, wct_scr, vh_scr, dgv_scr, dy_scr, dv_scr, gbs_acc, gwc_acc = scr

        @pl.when(i == 0)
        def _():
            m = _causal_mask()
            for g in range(G):
                wm = jnp.where(m, ws_ref[g], 0.0)
                wc_scr[g] = wm.astype(BF16)
                wct_scr[g] = wm.T.astype(BF16)
            glnw_ref[...] = jnp.zeros_like(glnw_ref)
            glnb_ref[...] = jnp.zeros_like(glnb_ref)
            gbs_acc[...] = jnp.zeros_like(gbs_acc)
            gwc_acc[...] = jnp.zeros_like(gwc_acc)

        dy_scr[...] = _dot_nt(dx1_ref[...], wout_ref[...])

        ssum = jnp.zeros((tm, 1), F32)
        for g in range(G):
            cs = slice(g * GD, (g + 1) * GD)
            vs = slice(AW + g * GD, AW + (g + 1) * GD)
            zv = z_ref[:, vs]
            pv = pp_ref[:, vs].astype(F32)
            gv = zv * pv
            vh_scr[:, cs] = gv
            dgv_scr[:, cs] = _dgelu(zv, pv)
            ssum = ssum + jnp.sum(gv, axis=-1, keepdims=True)
        mu = ssum * (1.0 / AW)
        vsum = jnp.zeros((tm, 1), F32)
        for g in range(G):
            dlt = vh_scr[:, g * GD:(g + 1) * GD] - mu
            vsum = vsum + jnp.sum(dlt * dlt, axis=-1, keepdims=True)
        rstd = lax.rsqrt(vsum * (1.0 / AW) + LN_EPS)

        m1 = jnp.zeros((tm, 1), F32)
        m2 = jnp.zeros((tm, 1), F32)
        for g in range(G):
            cs = slice(g * GD, (g + 1) * GD)
            gs = slice(2 * AW + g * GD, 2 * AW + (g + 1) * GD)
            vhat = (vh_scr[:, cs] - mu) * rstd
            vh_scr[:, cs] = vhat
            vb = (vhat * lnw_ref[:, cs] + lnb_ref[:, cs]).astype(BF16)
            zu = z_ref[:, cs]
            tu = pp_ref[:, cs].astype(F32)
            u = zu * tu
            zg = z_ref[:, gs]
            sig = pp_ref[:, gs].astype(F32)
            sg = zg * sig
            dy = dy_scr[:, cs]
            dsf = dy * u * sg
            dsb = dsf.astype(BF16)
            dvs = []
            for n in range(nch):
                rs = slice(n * CH, (n + 1) * CH)
                s = _dot(wc_scr[g], vb[rs, :]) + bst_ref[:, g:g + 1]
                dys = dy[rs, :] * s
                dz_ref[rs, cs] = (dys * sg[rs, :] * _dgelu(zu[rs, :], tu[rs, :])).astype(BF16)
                dz_ref[rs, gs] = (dys * u[rs, :] * (sig[rs, :] * (1.0 + zg[rs, :] * (1.0 - sig[rs, :])))).astype(BF16)
                gbs_acc[g] += dsf[rs, :]
                gwc_acc[g] += _dot_nt(dsb[rs, :], vb[rs, :])
                dvs.append(_dot(wct_scr[g], dsb[rs, :]))
            dv = jnp.concatenate(dvs, axis=0) if nch > 1 else dvs[0]
            glnw_ref[:, cs] += _rowsum(dv * vhat)
            glnb_ref[:, cs] += _rowsum(dv)
            dvh = dv * lnw_ref[:, cs]
            dv_scr[:, cs] = dvh
            m1 = m1 + jnp.sum(dvh, axis=-1, keepdims=True)
            m2 = m2 + jnp.sum(dvh * vhat, axis=-1, keepdims=True)
        m1 = m1 * (1.0 / AW)
        m2 = m2 * (1.0 / AW)
        for g in range(G):
            cs = slice(g * GD, (g + 1) * GD)
            dgv = rstd * (dv_scr[:, cs] - m1 - vh_scr[:, cs] * m2)
            dz_ref[:, AW + g * GD:AW + (g + 1) * GD] = (dgv * dgv_scr[:, cs]).astype(BF16)

        @pl.when(i == nt - 1)
        def _():
            m = _causal_mask()
            for g in range(G):
                gws_ref[g] = jnp.where(m, gwc_acc[g], 0.0)
                gbst_ref[:, g:g + 1] = jnp.sum(gbs_acc[g], axis=-1, keepdims=True)

    tile = lambda w: pl.BlockSpec((tm, w), lambda i: (i, 0))
    whole = lambda *s: pl.BlockSpec(s, lambda i: (0,) * len(s))
    big = lambda dt: pltpu.VMEM((tm, AW), dt)
    return _call(
        main, jobs, name="bwd_a", grid=(nt,), relay_step=relay_step,
        ins=[dx1, z, pp, lnw, lnb, ws, bst, wout],
        in_specs=[tile(D), tile(3 * AW), tile(3 * AW), _VMEM, _VMEM, _VMEM, _VMEM, _VMEM],
        out_shape=[_sds((s_len, 3 * AW), BF16), _sds((1, AW), F32), _sds((1, AW), F32), _sds((G, CH, CH), F32),
                   _sds((CH, G), F32)],
        out_specs=[tile(3 * AW), whole(1, AW), whole(1, AW), whole(G, CH, CH), whole(CH, G)],
        scratch=[pltpu.VMEM((G, CH, CH), BF16), pltpu.VMEM((G, CH, CH), BF16), big(F32), big(F32), big(F32), big(F32),
                 pltpu.VMEM((G, CH, GD), F32), pltpu.VMEM((G, CH, CH), F32)])


def _bwd_a_in(dz, dx1, x, nw, win8, jobs, *, tm, relay_step):
    s_len = x.shape[0]
    nt = s_len // tm

    def main(i, ins, outs, scr):
        dz_ref, dx1_ref, x_ref, nw_ref, win_ref = ins
        gx_ref, gnw_ref = outs

        @pl.when(i == 0)
        def _():
            gnw_ref[...] = jnp.zeros_like(gnw_ref)

        dh = jnp.zeros((tm, D), F32)
        for k in range(NDEV):
            dh = dh + _dot_nt(dz_ref[:, k * CA:(k + 1) * CA], win_ref[k])
        x = x_ref[...]
        r = _rms(x)
        gx_ref[...] = dx1_ref[...] + _rms_bwd(dh, x, r, nw_ref[...])
        gnw_ref[...] += _rowsum(dh * x * r)

        @pl.when(i == nt - 1)
        def _():
            gnw_ref[...] = _direct_sum(gnw_ref[...], *scr)

    tile = lambda w: pl.BlockSpec((tm, w), lambda i: (i, 0))
    return _call(
        main, jobs, name="bwd_a_in", grid=(nt,), relay_step=relay_step,
        ins=[dz, dx1, x, nw, win8], in_specs=[tile(3 * AW), tile(D), tile(D), _VMEM, _VMEM],
        out_shape=[_sds((s_len, D), F32), _sds((1, D), F32)],
        out_specs=[tile(D), pl.BlockSpec((1, D), lambda i: (0, 0))], scratch=_direct_sum_scratch((1, D), F32))


def _conv(p8_ref, cs, xb, xm1, xm2, xm3):
    xc = p8_ref[4:5, cs] + p8_ref[3:4, cs] * xb
    xc = xc + p8_ref[0:1, cs] * xm3
    xc = xc + p8_ref[1:2, cs] * xm2
    return xc + p8_ref[2:3, cs] * xm1


def _gates(p8_ref, gcat_ref, hh, xc):
    cs = slice(hh * HD, (hh + 1) * HD)
    pre = _dot(xc.astype(BF16), gcat_ref[hh])
    r = _sigmoid(pre[:, :HD] + p8_ref[5:6, cs])
    ig = _sigmoid(pre[:, HD:] + p8_ref[6:7, cs])
    sp = _softplus_neg(p8_ref[7:8, cs])
    la = (-RG_C) * r * sp
    a = jnp.exp(la)
    half_log = 0.5 * jnp.log(jnp.tanh(-la) * (1.0 + a * a))
    return r, ig, sp, a, jnp.exp(half_log), jnp.exp(-half_log)


def _scan_rows(a_ref, b_ref, out_ref, carry, tm, reverse):
    row = lax.broadcasted_iota(jnp.int32, (SUBLANES, BW), 0)
    ngrp = tm // SUBLANES

    def step(j, cr):
        jj = (ngrp - 1 - j) if reverse else j
        off = pl.multiple_of(jj * SUBLANES, SUBLANES)
        a = a_ref[pl.ds(off, SUBLANES), :]
        b = b_ref[pl.ds(off, SUBLANES), :]
        for sh in (1, 2, 4):
            if reverse:
                a_s = pltpu.roll(a, SUBLANES - sh, 0)
                b_s = pltpu.roll(b, SUBLANES - sh, 0)
                m = row < SUBLANES - sh
            else:
                a_s = pltpu.roll(a, sh, 0)
                b_s = pltpu.roll(b, sh, 0)
                m = row >= sh
            b = jnp.where(m, a * b_s + b, b)
            a = jnp.where(m, a * a_s, a)
        o = b + a * cr
        out_ref[pl.ds(off, SUBLANES), :] = o
        return o[0:1, :] if reverse else o[SUBLANES - 1:SUBLANES, :]

    return lax.fori_loop(0, ngrp, step, carry)


def _fwd_b(x, ya, wout_a, nw, win8, p8, gcat, jobs, *, tm, relay_step):
    s_len = x.shape[0]
    nt = s_len // tm

    def main(i, ins, outs, scr):
        x_ref, ya_ref, wouta_ref, nw_ref, win_ref, p8_ref, gcat_ref = ins
        x1_ref, zb_ref, hs_ref, h1_ref, yb_ref, xc_ref, a_ref, cc_ref, r_ref, ig_ref, m_ref = outs
        xbe_scr, b_scr, k_scr, carry_scr = scr

        @pl.when(i == 0)
        def _():
            xbe_scr[0:SUBLANES, :] = jnp.zeros((SUBLANES, BW), F32)
            carry_scr[...] = jnp.zeros_like(carry_scr)

        x1 = x_ref[...] + _dot(ya_ref[...], wouta_ref[...])
        x1_ref[...] = x1
        h = (x1 * _rms(x1) * nw_ref[...]).astype(BF16)
        h1_ref[...] = h
        for k in range(NDEV):
            zb_ref[:, k * CB:(k + 1) * CB] = _dot(h, win_ref[k])
        xbe_scr[SUBLANES:SUBLANES + tm, :] = zb_ref[:, :BW]
        for hh in range(BH):
            cs = slice(hh * HD, (hh + 1) * HD)
            xc = _conv(p8_ref, cs, xbe_scr[SUBLANES:SUBLANES + tm, cs], xbe_scr[7:7 + tm, cs],
                       xbe_scr[6:6 + tm, cs], xbe_scr[5:5 + tm, cs])
            r, ig, _, a, mult, rm = _gates(p8_ref, gcat_ref, hh, xc)
            ixc = ig * xc
            xc_ref[:, cs] = xc
            a_ref[:, cs] = a
            r_ref[:, cs] = r.astype(BF16)
            ig_ref[:, cs] = ig.astype(BF16)
            m_ref[:, cs] = mult.astype(BF16)
            b_scr[:, cs] = mult * ixc
            k_scr[:, cs] = ixc * (a * a * rm)
        xbe_scr[0:SUBLANES, :] = xbe_scr[tm:tm + SUBLANES, :]
        carry_scr[...] = _scan_rows(a_ref, b_scr, hs_ref, carry_scr[...], tm, False)
        for hh in range(BH):
            cs = slice(hh * HD, (hh + 1) * HD)
            gt = zb_ref[:, BW + hh * HD:BW + (hh + 1) * HD]
            hsv = hs_ref[:, cs]
            yb_ref[:, cs] = (hsv * (gt * _sigmoid(gt))).astype(BF16)
            cc_ref[:, cs] = (hsv - b_scr[:, cs]) - k_scr[:, cs]

    tile = lambda w: pl.BlockSpec((tm, w), lambda i: (i, 0))
    wide = lambda dt: _sds((s_len, BW), dt)
    return _call(
        main, jobs, name="fwd_b", grid=(nt,), relay_step=relay_step,
        ins=[x, ya, wout_a, nw, win8, p8, gcat], in_specs=[tile(D), tile(AW), _VMEM, _VMEM, _VMEM, _VMEM, _VMEM],
        out_shape=[_sds((s_len, D), F32), _sds((s_len, 2 * BW), F32), wide(F32), _sds((s_len, D), BF16), wide(BF16),
                   wide(F32), wide(F32), wide(F32), wide(BF16), wide(BF16), wide(BF16)],
        out_specs=[tile(D), tile(2 * BW), tile(BW), tile(D)] + [tile(BW)] * 7,
        scratch=[pltpu.VMEM((tm + SUBLANES, BW), F32), pltpu.VMEM((tm, BW), F32), pltpu.VMEM((tm, BW), F32),
                 pltpu.VMEM((1, BW), F32)])


def _head(x1, yb, wout, nfw, tgt, *, tm):
    s_len = x1.shape[0]

    def main(i, ins, outs, scr):
        x1_ref, yb_ref, wout_ref, nfw_ref, t_ref = ins
        dx2_ref, dx2b_ref, loss_ref, gnfw_ref = outs

        @pl.when(i == 0)
        def _():
            loss_ref[...] = jnp.zeros_like(loss_ref)
            gnfw_ref[...] = jnp.zeros_like(gnfw_ref)

        x2 = x1_ref[...] + _dot(yb_ref[...], wout_ref[...])
        rf = _rms(x2)
        xn = x2 * rf
        e = xn * nfw_ref[...] - t_ref[...]
        loss_ref[...] += (0.5 / D) * jnp.sum(jnp.sum(e * e, axis=-1, keepdims=True), axis=0, keepdims=True)
        dyf = e * (1.0 / D)
        gnfw_ref[...] += _rowsum(dyf * xn)
        dx2 = _rms_bwd(dyf, x2, rf, nfw_ref[...])
        dx2_ref[...] = dx2
        dx2b_ref[...] = dx2.astype(BF16)

    tile = lambda w: pl.BlockSpec((tm, w), lambda i: (i, 0))
    whole = lambda *s: pl.BlockSpec(s, lambda i: (0,) * len(s))
    (dx2, dx2b, loss, gnfw), _ = _call(
        main, [], name="head", grid=(s_len // tm,),
        ins=[x1, yb, wout, nfw, tgt], in_specs=[tile(D), tile(BW), _VMEM, _VMEM, tile(D)],
        out_shape=[_sds((s_len, D), F32), _sds((s_len, D), BF16), _sds((1, 1), F32), _sds((1, D), F32)],
        out_specs=[tile(D), tile(D), whole(1, 1), whole(1, D)], scratch=[])
    return dx2, dx2b, loss, gnfw


def _bwd_b(dx2, zb, hs, x1, saved, nw, win8, p8, gcat, wout, *, tm):
    s_len = x1.shape[0]
    nt = s_len // tm

    def main(i, ins, outs, scr):
        (dx2_ref, zb_ref, hs_ref, x1_ref, xc_ref, a_ref, cc_ref, r_ref, ig_ref, m_ref,
         nw_ref, win_ref, p8_ref, gcat_ref, wout_ref) = ins
        dx1_ref, dx1b_ref, dzb_ref, gp8_ref, gga_ref, ggx_ref, gnw_ref = outs
        ae_scr, an_scr, dhd_scr, dh_scr, dy_scr, dxce_scr, carry_scr, afirst_scr = scr

        @pl.when(i == 0)
        def _():
            gp8_ref[...] = jnp.zeros_like(gp8_ref)
            gga_ref[...] = jnp.zeros_like(gga_ref)
            ggx_ref[...] = jnp.zeros_like(ggx_ref)
            gnw_ref[...] = jnp.zeros_like(gnw_ref)
            dxce_scr[tm:tm + SUBLANES, :] = jnp.zeros((SUBLANES, BW), F32)
            carry_scr[...] = jnp.zeros_like(carry_scr)
            afirst_scr[...] = jnp.zeros_like(afirst_scr)

        dx2 = dx2_ref[...]
        dy_scr[...] = _dot_nt(dx2.astype(BF16), wout_ref[...])
        for hh in range(BH):
            cs = slice(hh * HD, (hh + 1) * HD)
            gs = slice(BW + hh * HD, BW + (hh + 1) * HD)
            gt = zb_ref[:, gs]
            sig = _sigmoid(gt)
            dy = dy_scr[:, cs]
            dhd_scr[:, cs] = dy * (gt * sig)
            dzb_ref[:, gs] = (dy * hs_ref[:, cs] * (sig * (1.0 + gt * (1.0 - sig)))).astype(BF16)

        ae_scr[0:tm, :] = a_ref[...]
        ae_scr[tm:tm + SUBLANES, :] = jnp.broadcast_to(afirst_scr[...], (SUBLANES, BW))
        an_scr[...] = ae_scr[1:1 + tm, :]
        afirst_scr[...] = ae_scr[0:1, :]
        carry_scr[...] = _scan_rows(an_scr, dhd_scr, dh_scr, carry_scr[...], tm, True)

        for hh in range(BH):
            cs = slice(hh * HD, (hh + 1) * HD)
            dh = dh_scr[:, cs]
            mult = m_ref[:, cs].astype(F32)
            ig = ig_ref[:, cs].astype(F32)
            r = r_ref[:, cs].astype(F32)
            xc = xc_ref[:, cs]
            lam = p8_ref[7:8, cs]
            sp = _softplus_neg(lam)
            dla = dh * cc_ref[:, cs]
            gp8_ref[7:8, cs] += _rowsum(dla * ((-RG_C) * r)) * (-_sigmoid(-lam))
            dpr = dla * ((-RG_C) * sp) * (r * (1.0 - r))
            dpi = dh * mult * xc * (ig * (1.0 - ig))
            gp8_ref[5:6, cs] += _rowsum(dpr)
            gp8_ref[6:7, cs] += _rowsum(dpi)
            dcat = jnp.concatenate([dpr, dpi], axis=1).astype(BF16)
            dxc = dh * mult * ig + _dot_nt(dcat, gcat_ref[hh])
            gg = _dot(xc.T.astype(BF16), dcat)
            gga_ref[hh] += gg[:, :HD]
            ggx_ref[hh] += gg[:, HD:]
            dxce_scr[0:tm, cs] = dxc
            gp8_ref[4:5, cs] += _rowsum(dxc)
        for hh in range(BH):
            cs = slice(hh * HD, (hh + 1) * HD)
            xb = zb_ref[:, cs]
            d0, d1 = dxce_scr[0:tm, cs], dxce_scr[1:1 + tm, cs]
            d2, d3 = dxce_scr[2:2 + tm, cs], dxce_scr[3:3 + tm, cs]
            dzb_ref[:, cs] = (p8_ref[3:4, cs] * d0 + p8_ref[2:3, cs] * d1 + p8_ref[1:2, cs] * d2
                              + p8_ref[0:1, cs] * d3).astype(BF16)
            gp8_ref[3:4, cs] += _rowsum(d0 * xb)
            gp8_ref[2:3, cs] += _rowsum(d1 * xb)
            gp8_ref[1:2, cs] += _rowsum(d2 * xb)
            gp8_ref[0:1, cs] += _rowsum(d3 * xb)
        dxce_scr[tm:tm + SUBLANES, :] = dxce_scr[0:SUBLANES, :]

        dh1 = jnp.zeros((tm, D), F32)
        for k in range(NDEV):
            dh1 = dh1 + _dot_nt(dzb_ref[:, k * CB:(k + 1) * CB], win_ref[k])
        x1 = x1_ref[...]
        r1 = _rms(x1)
        dx1 = dx2 + _rms_bwd(dh1, x1, r1, nw_ref[...])
        dx1_ref[...] = dx1
        dx1b_ref[...] = dx1.astype(BF16)
        gnw_ref[...] += _rowsum(dh1 * x1 * r1)

    tile = lambda w: pl.BlockSpec((tm, w), lambda i: (nt - 1 - i, 0))
    whole = lambda *s: pl.BlockSpec(s, lambda i: (0,) * len(s))
    full = lambda: pltpu.VMEM((tm, BW), F32)
    ext = lambda: pltpu.VMEM((tm + SUBLANES, BW), F32)
    out, _ = _call(
        main, [], name="bwd_b", grid=(nt,),
        ins=[dx2, zb, hs, x1, *saved, nw, win8, p8, gcat, wout],
        in_specs=[tile(D), tile(2 * BW), tile(BW), tile(D)] + [tile(BW)] * 6 + [_VMEM] * 5,
        out_shape=[_sds((s_len, D), F32), _sds((s_len, D), BF16), _sds((s_len, 2 * BW), BF16), _sds((SUBLANES, BW), F32),
                   _sds((BH, HD, HD), F32), _sds((BH, HD, HD), F32), _sds((1, D), F32)],
        out_specs=[tile(D), tile(D), tile(2 * BW), whole(SUBLANES, BW), whole(BH, HD, HD), whole(BH, HD, HD),
                   whole(1, D)],
        scratch=[ext(), full(), full(), full(), full(), ext(), pltpu.VMEM((1, BW), F32), pltpu.VMEM((1, BW), F32)])
    return out


def _transpose_into(dst_ref, src_ref, rows):
    s_len = src_ref.shape[0]
    for r0 in range(0, s_len, rows):
        dst_ref[:, r0:r0 + rows] = src_ref[r0:r0 + rows, :].astype(F32).T.astype(BF16)


def _wgrad(a, b, jobs, *, by_rows, per, name, relay_step=0):
    s_len, m = a.shape
    n = b.shape[1]
    r, cd = (m // NDEV, n) if by_rows else (m, n // NDEV)
    nsteps = NDEV // per
    at_rows = per * r if by_rows else m

    def main(i, ins, outs, scr):
        a_ref, b_ref = ins
        q_ref, acc_ref = outs
        at_scr, stage, mine, land, send_sems, recv_sems = scr
        x, y, c = _place()

        def to_sibling(pi):
            return pltpu.make_async_remote_copy(
                src_ref=stage.at[pi & 1], dst_ref=land.at[pi], send_sem=send_sems.at[pi], recv_sem=recv_sems.at[pi],
                device_id=(x, y, 1 - c), device_id_type=MESH)

        if by_rows:
            _transpose_into(at_scr, a_ref, TRANSPOSE_ROWS)
        else:
            @pl.when(i == 0)
            def _():
                _transpose_into(at_scr, a_ref, TRANSPOSE_ROWS)

        res = _dot(at_scr[...], b_ref[...]).astype(BF16)
        for k in range(per):
            blk = per * i + k
            pi, pc = blk >> 1, blk & 1
            val = res[k * r:(k + 1) * r, :] if by_rows else res

            @pl.when(pc != c)
            def _():
                @pl.when(pi >= 2)
                def _():
                    to_sibling(pi - 2).wait_send()

                stage[pi & 1] = val
                to_sibling(pi).start()

            @pl.when(pc == c)
            def _():
                mine[pi] = val

        @pl.when(i == nsteps - 1)
        def _():
            for p in range(4):
                to_sibling(p).wait_recv()
            to_sibling(2).wait_send()
            to_sibling(3).wait_send()
            _chip_sums(mine, land, q_ref, acc_ref, x, y)

    if by_rows:
        in_specs = [pl.BlockSpec((s_len, at_rows), lambda j: (0, j)), _VMEM]
    else:
        in_specs = [_VMEM, pl.BlockSpec((s_len, cd), lambda j: (0, j))]
    blk_vmem = lambda k: pltpu.VMEM((k, r, cd), BF16)
    (q, acc), job_out = _call(
        main, jobs, name=name, grid=(nsteps,), relay_step=relay_step, ins=[a, b], in_specs=in_specs,
        out_shape=[_sds((NCHIP_OTHER, r, cd), BF16), _sds((r, cd), F32)],
        out_specs=[pl.BlockSpec((NCHIP_OTHER, r, cd), lambda j: (0, 0, 0)), pl.BlockSpec((r, cd), lambda j: (0, 0))],
        scratch=[pltpu.VMEM((at_rows, s_len), BF16), blk_vmem(2), blk_vmem(4), blk_vmem(4),
                 pltpu.SemaphoreType.DMA((4,)), pltpu.SemaphoreType.DMA((4,))])
    return q, acc, job_out


def _wgrad_cols_early(a, b, jobs, *, name, relay_step=0):
    s_len, m = a.shape
    r, cd = m, b.shape[1] // NDEV
    h = r // 2

    def chip_at(pos, base):
        return base ^ (3 - pos)

    def main(i, ins, outs, scr):
        a_ref, b_ref = ins
        q_ref, acc_ref, rel_ref = outs
        at_scr, stage, mine, land, q2_scr, send_sems, recv_sems, via_send, via_recv = scr
        x, y, c = _place()
        base = 2 * x + y
        xn, yn, _ = _other_chips(x, y)
        pos, pc = i >> 1, i & 1
        pi = chip_at(pos, base)

        def to_sibling(chip, slot):
            return pltpu.make_async_remote_copy(
                src_ref=stage.at[slot], dst_ref=land.at[chip], send_sem=send_sems.at[chip],
                recv_sem=recv_sems.at[chip], device_id=(x, y, 1 - c), device_id_type=MESH)

        def via(k):
            return pltpu.make_async_remote_copy(
                src_ref=q2_scr.at[pl.ds(k * h, h)], dst_ref=rel_ref.at[k], send_sem=via_send.at[k],
                recv_sem=via_recv.at[k], device_id=(*(xn, yn)[k], c), device_id_type=MESH)

        @pl.when(i == 0)
        def _():
            _transpose_into(at_scr, a_ref, TRANSPOSE_ROWS)

        res = _dot(at_scr[...], b_ref[...]).astype(BF16)

        @pl.when(pc != c)
        def _():
            @pl.when(pos >= 2)
            def _():
                to_sibling(chip_at(pos - 2, base), pos & 1).wait_send()

            stage[pos & 1] = res
            to_sibling(pi, pos & 1).start()

        @pl.when(pc == c)
        def _():
            mine[pi] = res

        @pl.when(i == 1)
        def _():
            dg = chip_at(0, base)
            to_sibling(dg, 0).wait_recv()
            q2 = (mine[dg].astype(F32) + land[dg].astype(F32)).astype(BF16)
            q2_scr[...] = q2
            q_ref[2] = q2
            via(0).start()
            via(1).start()

        @pl.when(i == NDEV - 1)
        def _():
            for pos_ in (1, 2, 3):
                to_sibling(chip_at(pos_, base), 0).wait_recv()
            to_sibling(chip_at(2, base), 0).wait_send()
            to_sibling(chip_at(3, base), 1).wait_send()
            for k in range(2):
                via(k).wait_recv()
            for k in range(2):
                via(k).wait_send()
            for j, chip in enumerate((base ^ 2, base ^ 1)):
                q_ref[j] = (mine[chip].astype(F32) + land[chip].astype(F32)).astype(BF16)
            acc_ref[...] = mine[base].astype(F32) + land[base].astype(F32)

    def b_block(j):
        base = 2 * lax.axis_index("x") + lax.axis_index("y")
        return (0, 2 * chip_at(j >> 1, base) + (j & 1))

    blk_vmem = lambda k: pltpu.VMEM((k, r, cd), BF16)
    (q, acc, rel), job_out = _call(
        main, jobs, name=name, grid=(NDEV,), relay_step=relay_step, ins=[a, b],
        in_specs=[_VMEM, pl.BlockSpec((s_len, cd), b_block)],
        out_shape=[_sds((NCHIP_OTHER, r, cd), BF16), _sds((r, cd), F32), _sds((2, h, cd), BF16)],
        out_specs=[pl.BlockSpec((NCHIP_OTHER, r, cd), lambda j: (0, 0, 0)), pl.BlockSpec((r, cd), lambda j: (0, 0)), _HBM],
        scratch=[pltpu.VMEM((m, s_len), BF16), blk_vmem(2), blk_vmem(4), blk_vmem(4), pltpu.VMEM((r, cd), BF16),
                 pltpu.SemaphoreType.DMA((4,)), pltpu.SemaphoreType.DMA((4,)), pltpu.SemaphoreType.DMA((2,)),
                 pltpu.SemaphoreType.DMA((2,))])
    return q, acc, rel, job_out


class _ExchangeRest:
    def __init__(self, q, relayed):
        _, r, cd = q.shape
        half = (2, r // 2, cd)
        self.ins, self.in_specs = [q, relayed], [_HBM, _HBM]
        self.out_shape, self.out_specs = [_sds((2, r, cd), q.dtype)], [_HBM]
        self.scratch = [pltpu.VMEM(half, q.dtype), pltpu.VMEM(half, q.dtype), pltpu.VMEM(half, q.dtype),
                        pltpu.SemaphoreType.DMA((4,)), pltpu.SemaphoreType.DMA((4,)), pltpu.SemaphoreType.DMA((4,))]

    def ops(self, ins, outs, scr):
        (q, rel_in), (land,) = ins, outs
        own, rel, comb, send_sems, recv_sems, local_sems = scr
        h = q.shape[1] // 2
        x, y, c = _place()
        xn, yn, _ = _other_chips(x, y)
        h0, h1 = pl.ds(0, h), pl.ds(h, h)

        def remote(k, src, dst, chip):
            return pltpu.make_async_remote_copy(src_ref=src, dst_ref=dst, send_sem=send_sems.at[k],
                                                recv_sem=recv_sems.at[k], device_id=(*chip, c), device_id_type=MESH)

        def sends():
            return [remote(0, q.at[0, h0], land.at[0, h0], xn), remote(1, q.at[1, h1], land.at[1, h1], yn),
                    remote(2, comb.at[0], land.at[1, h0], yn), remote(3, comb.at[1], land.at[0, h1], xn)]

        def loads():
            return [pltpu.make_async_copy(q.at[1, h0], own.at[0], local_sems.at[0]),
                    pltpu.make_async_copy(q.at[0, h1], own.at[1], local_sems.at[1]),
                    pltpu.make_async_copy(rel_in.at[0], rel.at[0], local_sems.at[2]),
                    pltpu.make_async_copy(rel_in.at[1], rel.at[1], local_sems.at[3])]

        def start():
            cps, lds = sends(), loads()
            for ld in lds:
                ld.start()
            cps[0].start()
            cps[1].start()
            for ld in lds:
                ld.wait()
            for k in range(2):
                comb[k] = (own[k].astype(F32) + rel[k].astype(F32)).astype(comb.dtype)
            cps[2].start()
            cps[3].start()

        def finish():
            cps = sends()
            for cp in cps:
                cp.wait_recv()
            for cp in cps:
                cp.wait_send()

        return start, lambda: None, finish


def _adam_math(w, g, m, v):
    m = B1 * m + (1.0 - B1) * g
    v = B2 * v + (1.0 - B2) * (g * g)
    m_hat = m / (1.0 - B1 ** STEP)
    v_hat = v / (1.0 - B2 ** STEP)
    delta = (-LR) * (m_hat / (jnp.sqrt(v_hat) + ADAM_EPS) + WD * w)
    return delta, m, v


def _adam_big(w, acc, land, m, v, name):
    r, cd = w.shape
    rb = ADAM_ROWS if r % ADAM_ROWS == 0 else r
    nland = land.shape[0]

    def body(w_ref, acc_ref, land_ref, m_ref, v_ref, g_ref, d_ref, mo_ref, vo_ref):
        g = acc_ref[...]
        for j in range(nland):
            g = g + land_ref[j].astype(F32)
        g_ref[...] = g
        d_ref[...], mo_ref[...], vo_ref[...] = _adam_math(w_ref[...], g, m_ref[...], v_ref[...])

    blk = pl.BlockSpec((rb, cd), lambda i: (i, 0))
    blk3 = pl.BlockSpec((nland, rb, cd), lambda i: (0, i, 0))
    return pl.pallas_call(
        body, name=name, grid=(r // rb,), in_specs=[blk, blk, blk3, blk, blk], out_specs=[blk] * 4,
        out_shape=[_sds((r, cd), F32)] * 4,
        compiler_params=_params(dimension_semantics=("arbitrary",)),
    )(w, acc, land, m, v)


def _adam_small(groups):
    n = len(groups)

    def body(*refs):
        ins, outs = refs[:4 * n], refs[4 * n:]
        for k in range(n):
            w_ref, g_ref, m_ref, v_ref = ins[4 * k:4 * k + 4]
            d, mo, vo = _adam_math(w_ref[...], g_ref[...], m_ref[...], v_ref[...])
            outs[3 * k][...] = d
            outs[3 * k + 1][...] = mo
            outs[3 * k + 2][...] = vo

    flat = [a for grp in groups for a in grp]
    shapes = [_sds(grp[0].shape, F32) for grp in groups for _ in range(3)]
    res = pl.pallas_call(
        body, name="adam_small", in_specs=[_VMEM] * (4 * n), out_specs=[_VMEM] * (3 * n), out_shape=shapes,
        compiler_params=_params(),
    )(*flat)
    return [tuple(res[3 * k:3 * k + 3]) for k in range(n)]


TM_FWD_A = 256
RELAY_STEP_FWD_A = 4
RELAY_STEP_FWD_B = 2
TM_BWD_A = 256
RELAY_STEP_BWD_A = 3
TM_BWD_A_IN = 256
RELAY_STEP_BWD_A_IN = 4
RELAY_STEP_WGRAD_A_IN = 2
TM_FWD_B = 256
TM_HEAD = 512
TM_BWD_B = 256


def _pack(parts, rows):
    flat = jnp.concatenate([p.reshape(-1) for p in parts])
    return jnp.pad(flat, (0, NDEV * rows * LANES - flat.shape[0])).reshape(NDEV, rows, LANES)


def _unpack(packed, shapes):
    flat, out, off = packed.reshape(-1), [], 0
    for s in shapes:
        size = 1
        for d in s:
            size *= d
        out.append(flat[off:off + size].reshape(s))
        off += size
    return out


def kernel(x, norm_w, a_w_in, a_ln_w, a_ln_b, a_w_s, a_b_s, a_w_out, b_w_in, b_conv_w, b_conv_b, b_gate_a_w, b_gate_a_b, b_gate_x_w, b_gate_x_b, b_lambda, b_w_out, norm_f_w, loss_target, m_norm_w, m_a_w_in, m_a_ln_w, m_a_ln_b, m_a_w_s, m_a_b_s, m_a_w_out, m_b_w_in, m_b_conv_w, m_b_conv_b, m_b_gate_a_w, m_b_gate_a_b, m_b_gate_x_w, m_b_gate_x_b, m_b_lambda, m_b_w_out, m_norm_f_w, v_norm_w, v_a_w_in, v_a_ln_w, v_a_ln_b, v_a_w_s, v_a_b_s, v_a_w_out, v_b_w_in, v_b_conv_w, v_b_conv_b, v_b_gate_a_w, v_b_gate_a_b, v_b_gate_x_w, v_b_gate_x_b, v_b_lambda, v_b_w_out, v_norm_f_w):
    me = 4 * lax.axis_index("x") + 2 * lax.axis_index("y") + lax.axis_index("c")
    xs, tgt = x[0], loss_target[0]
    nw0, nw1, nfw = norm_w[0:1], norm_w[1:2], norm_f_w.reshape(1, D)
    w_s, bst = a_w_s[0], a_b_s[0].T
    gcat = jnp.concatenate([b_gate_a_w[0], b_gate_x_w[0]], axis=-1).astype(BF16)

    p8_shard = jnp.concatenate([b_conv_w[0], b_conv_b, b_gate_a_b, b_gate_x_b, b_lambda], axis=0)
    ((win_a8, p8_all),) = _comm_only([_Gather([a_w_in[0], p8_shard], [BF16, F32])], "gather_first")
    p8 = jnp.transpose(p8_all, (1, 0, 2)).reshape(SUBLANES, BW)

    (z, h0, ya, pp), ((wout_a8, win_b8),) = _fwd_a(
        xs, nw0, win_a8, a_ln_w, a_ln_b, w_s, bst, [_Gather([a_w_out[0], b_w_in[0]], [BF16, BF16])],
        tm=TM_FWD_A, relay_step=RELAY_STEP_FWD_A)
    wout_a = wout_a8.reshape(AW, D)
    (x1, zb, hs, h1, yb, *saved_b), ((wout_b8,),) = _fwd_b(
        xs, ya, wout_a, nw1, win_b8, p8, gcat, [_Gather([b_w_out[0]], [BF16])],
        tm=TM_FWD_B, relay_step=RELAY_STEP_FWD_B)
    wout_b = wout_b8.reshape(BW, D)
    dx2, dx2b, loss, g_nfw = _head(x1, yb, wout_b, nfw, tgt, tm=TM_HEAD)

    dx1, dx1b, dzb, g_p8, g_ga, g_gx, g_nw1 = _bwd_b(dx2, zb, hs, x1, saved_b, nw1, win_b8, p8, gcat, wout_b,
                                                     tm=TM_BWD_B)
    q_wout_b, acc_wout_b, _ = _wgrad(yb, dx2b, [], by_rows=True, per=2, name="wgrad_b_out")
    shapes_b = [(1, D), (1, D), (SUBLANES, BW), (1, 1)]
    pack_b = _pack([g_nfw, g_nw1, g_p8, loss], 16)
    small_b = _InChip([g_ga.reshape(NDEV, -1, HD), g_gx.reshape(NDEV, -1, HD), pack_b])
    q_win_b, acc_win_b, rel_b, (sm_b, (l_wout_b,)) = _wgrad_cols_early(
        h1, dzb, [small_b, _Exchange([q_wout_b])], name="wgrad_b_in")
    qs_b, accs_b = sm_b[:3], sm_b[3:]

    (dz, g_lnw, g_lnb, g_ws, g_bst), (lands_b, (l_win_b,)) = _bwd_a(
        dx1b, z, pp, a_ln_w, a_ln_b, w_s, bst, wout_a, [_Exchange(qs_b), _ExchangeRest(q_win_b, rel_b)],
        tm=TM_BWD_A, relay_step=RELAY_STEP_BWD_A)
    shapes_a = [(1, AW), (1, AW), (CH, G)]
    pack_a = _pack([g_lnw, g_lnb, g_bst], 8)
    q_wout_a, acc_wout_a, (red_b, sm_a) = _wgrad(
        ya, dx1b, [_SumGather(accs_b, lands_b), _InChip([g_ws, pack_a])], by_rows=True, per=2,
        name="wgrad_a_out", relay_step=1)
    qs_a, accs_a = sm_a[:2], sm_a[2:]
    q_win_a, acc_win_a, rel_a, (lands_a, (l_wout_a,)) = _wgrad_cols_early(
        h0, dz, [_Exchange(qs_a), _ExchangeVia(q_wout_a)], name="wgrad_a_in", relay_step=RELAY_STEP_WGRAD_A_IN)
    (gx, g_nw0), (red_a, (l_win_a,)) = _bwd_a_in(
        dz, dx1, xs, nw0, win_a8, [_SumGather(accs_a, lands_a), _ExchangeRest(q_win_a, rel_a)],
        tm=TM_BWD_A_IN, relay_step=RELAY_STEP_BWD_A_IN)

    r_ga, r_gx, r_pack_b = red_b
    r_nfw, r_nw1, r_p8, loss = _unpack(r_pack_b, shapes_b)
    r_ws, r_pack_a = red_a
    r_lnw, r_lnb, r_bst = _unpack(r_pack_a, shapes_a)
    g_p8 = lax.dynamic_slice_in_dim(r_p8, me * (BW // NDEV), BW // NDEV, axis=1)
    loss = loss[0, 0]

    weights = dict(norm_w=norm_w, a_w_in=a_w_in, a_ln_w=a_ln_w, a_ln_b=a_ln_b, a_w_s=a_w_s, a_b_s=a_b_s, a_w_out=a_w_out,
                   b_w_in=b_w_in, b_conv_w=b_conv_w, b_conv_b=b_conv_b, b_gate_a_w=b_gate_a_w, b_gate_a_b=b_gate_a_b,
                   b_gate_x_w=b_gate_x_w, b_gate_x_b=b_gate_x_b, b_lambda=b_lambda, b_w_out=b_w_out, norm_f_w=norm_f_w)
    mom1 = dict(norm_w=m_norm_w, a_w_in=m_a_w_in, a_ln_w=m_a_ln_w, a_ln_b=m_a_ln_b, a_w_s=m_a_w_s, a_b_s=m_a_b_s,
                a_w_out=m_a_w_out, b_w_in=m_b_w_in, b_conv_w=m_b_conv_w, b_conv_b=m_b_conv_b, b_gate_a_w=m_b_gate_a_w,
                b_gate_a_b=m_b_gate_a_b, b_gate_x_w=m_b_gate_x_w, b_gate_x_b=m_b_gate_x_b, b_lambda=m_b_lambda,
                b_w_out=m_b_w_out, norm_f_w=m_norm_f_w)
    mom2 = dict(norm_w=v_norm_w, a_w_in=v_a_w_in, a_ln_w=v_a_ln_w, a_ln_b=v_a_ln_b, a_w_s=v_a_w_s, a_b_s=v_a_b_s,
                a_w_out=v_a_w_out, b_w_in=v_b_w_in, b_conv_w=v_b_conv_w, b_conv_b=v_b_conv_b, b_gate_a_w=v_b_gate_a_w,
                b_gate_a_b=v_b_gate_a_b, b_gate_x_w=v_b_gate_x_w, b_gate_x_b=v_b_gate_x_b, b_lambda=v_b_lambda,
                b_w_out=v_b_w_out, norm_f_w=v_norm_f_w)
    names = list(weights)

    def as2d(a):
        return a.reshape(-1, a.shape[-1])

    upd, grads = {}, {}
    for k, acc, land in (("a_w_in", acc_win_a, l_win_a), ("a_w_out", acc_wout_a, l_wout_a),
                         ("b_w_in", acc_win_b, l_win_b), ("b_w_out", acc_wout_b, l_wout_b)):
        g, d, mo, vo = _adam_big(as2d(weights[k]), acc, land, as2d(mom1[k]), as2d(mom2[k]), "adam_" + k)
        grads[k] = g[None]
        upd[k] = (d, mo, vo)
    grads.update(
        norm_w=jnp.concatenate([g_nw0, r_nw1], axis=0), a_ln_w=r_lnw, a_ln_b=r_lnb,
        a_w_s=r_ws.reshape(1, G, CH, CH), a_b_s=r_bst.T[None],
        b_conv_w=g_p8[None, 0:4], b_conv_b=g_p8[4:5], b_gate_a_w=r_ga.reshape(1, BH, HD, HD), b_gate_a_b=g_p8[5:6],
        b_gate_x_w=r_gx.reshape(1, BH, HD, HD), b_gate_x_b=g_p8[6:7], b_lambda=g_p8[7:8], norm_f_w=r_nfw.reshape(D))
    small_names = [k for k in names if k not in upd]
    res = _adam_small([(as2d(weights[k]), as2d(grads[k]), as2d(mom1[k]), as2d(mom2[k])) for k in small_names])
    for k, r3 in zip(small_names, res):
        upd[k] = r3
    deltas = [upd[k][0].reshape(weights[k].shape) for k in names]
    new_m = [upd[k][1].reshape(weights[k].shape) for k in names]
    new_v = [upd[k][2].reshape(weights[k].shape) for k in names]
    return (loss, gx[None], *[grads[k] for k in names], *deltas, *new_m, *new_v)
```

```python
import jax
import jax.numpy as jnp
from jax import lax
from jax.experimental import pallas as pl
from jax.experimental.pallas import tpu as pltpu

F32 = jnp.float32
BF16 = jnp.bfloat16
MESH = pl.DeviceIdType.MESH

NDEV = 8
NCHIP_OTHER = 3
D = 1024
AW = 2048
G = 8
GD = AW // G
CH = 128
BW = 1536
BH = 12
HD = BW // BH
CA = 3 * AW // NDEV
CB = 2 * BW // NDEV
RMS_EPS = 1e-6
LN_EPS = 1e-5
RG_C = 8.0
LR, B1, B2, ADAM_EPS, WD, STEP = 0.001, 0.9, 0.999, 1e-08, 0.01, 10
V7X_VMEM_BYTES = 64 * 1024 * 1024
VMEM_LIMIT = V7X_VMEM_BYTES - 8 * 1024 * 1024
SUBLANES = 8
LANES = 128
BF16_ROWS = 16
TRANSPOSE_ROWS = 256
ADAM_ROWS = 512
GELU_C = 0.7978845608028654
GELU_K = 0.044715

_VMEM = pl.BlockSpec(memory_space=pltpu.VMEM)
_HBM = pl.BlockSpec(memory_space=pltpu.HBM)


def _sds(shape, dtype):
    return jax.ShapeDtypeStruct(tuple(shape), dtype)


def _params(**kw):
    return pltpu.CompilerParams(vmem_limit_bytes=VMEM_LIMIT, **kw)


def _gelu_t(z):
    p = 0.5 * jnp.tanh(z * (GELU_C + (GELU_C * GELU_K) * (z * z))) + 0.5
    return z * p, p


def _dgelu(z, p):
    return p * (1.0 + (z * (1.0 - p)) * (2.0 * GELU_C + (6.0 * GELU_C * GELU_K) * (z * z)))


def _sigmoid(v):
    return 0.5 * jnp.tanh(0.5 * v) + 0.5


def _softplus_neg(lam):
    return jnp.maximum(-lam, 0.0) + jnp.log1p(jnp.exp(-jnp.abs(lam)))


def _dot(a, b):
    return jnp.dot(a, b, preferred_element_type=F32)


def _dot_nt(a, b):
    return lax.dot_general(a, b, (((1,), (1,)), ((), ())), preferred_element_type=F32)


def _rowsum(v):
    return jnp.sum(v, axis=0, keepdims=True)


def _causal_mask():
    r = lax.broadcasted_iota(jnp.int32, (CH, CH), 0)
    c = lax.broadcasted_iota(jnp.int32, (CH, CH), 1)
    return r >= c


def _rms(x):
    return lax.rsqrt(jnp.mean(x * x, axis=-1, keepdims=True) + RMS_EPS)


def _rms_bwd(dh, x, r, nw):
    gy = dh * nw
    return r * gy - x * (r * r * r) * jnp.mean(gy * x, axis=-1, keepdims=True)


def _place():
    return lax.axis_index("x"), lax.axis_index("y"), lax.axis_index("c")


def _other_chips(x, y):
    return [(1 - x, y), (x, 1 - y), (1 - x, 1 - y)]


GATHER_SLOTS = 10


def _gather_ops(ins, outs, send_sems, recv_sems, local_sems):
    n = len(ins)
    x, y, c = _place()
    sibling = (x, y, 1 - c)
    xn, yn, dg = _other_chips(x, y)
    split = [ins[i].shape[0] % (2 * BF16_ROWS) == 0 for i in range(n)]

    def blk(chip, core):
        return 4 * chip[0] + 2 * chip[1] + core

    me = blk((x, y), c)

    def part(ref, i, half):
        if half is None:
            return ref
        h = ins[i].shape[0] // 2
        return ref.at[pl.ds(half * h, h)]

    def copy(i, k, block, to, half=None, src=None):
        dst = part(outs[i].at[block], i, half)
        return pltpu.make_async_remote_copy(
            src_ref=dst if src is None else part(src, i, half), dst_ref=dst,
            send_sem=send_sems.at[k, i], recv_sem=recv_sems.at[k, i], device_id=to, device_id_type=MESH)

    def first_copies():
        mine = [pltpu.make_async_copy(ins[i], outs[i].at[me], local_sems.at[i]) for i in range(n)]
        first = []
        for i in range(n):
            first.append(copy(i, 0, me, sibling, src=ins[i]))
            if split[i]:
                first.append(copy(i, 1, me, (*xn, c), 0, ins[i]))
                first.append(copy(i, 3, me, (*yn, c), 1, ins[i]))
                first.append(copy(i, 2, me, (*xn, c), 1, ins[i]))
                first.append(copy(i, 4, me, (*yn, c), 0, ins[i]))
            else:
                first.append(copy(i, 1, me, (*xn, c), None, ins[i]))
                first.append(copy(i, 3, me, (*yn, c), None, ins[i]))
                first.append(copy(i, 5, me, (*dg, c), None, ins[i]))
        return mine, first

    def onward():
        out = []
        for i in range(n):
            if split[i]:
                out.append(copy(i, 5, blk(xn, c), (*yn, c), 0))
                out.append(copy(i, 6, blk(yn, c), (*xn, c), 1))
        return out

    def start():
        mine, first = first_copies()
        for cp in mine + first:
            cp.start()

    def relay():
        sends = onward()
        for i in range(n):
            if split[i]:
                copy(i, 1, blk(xn, c), sibling, 0).wait_recv()
                sends.pop(0).start()
                copy(i, 3, blk(yn, c), sibling, 1).wait_recv()
                sends.pop(0).start()

    def finish():
        mine, first = first_copies()
        passed = []

        def pass_on(i, j, chip):
            fwd = copy(i, 7 + j, blk(chip, c), sibling)
            fwd.start()
            passed.append(fwd)

        for i in range(n):
            if split[i]:
                copy(i, 2, blk(xn, c), sibling, 1).wait_recv()
                pass_on(i, 0, xn)
                copy(i, 4, blk(yn, c), sibling, 0).wait_recv()
                pass_on(i, 1, yn)
                copy(i, 5, blk(dg, c), sibling, 0).wait_recv()
                copy(i, 6, blk(dg, c), sibling, 1).wait_recv()
                pass_on(i, 2, dg)
            else:
                copy(i, 1, blk(xn, c), sibling).wait_recv()
                pass_on(i, 0, xn)
                copy(i, 3, blk(yn, c), sibling).wait_recv()
                pass_on(i, 1, yn)
                copy(i, 5, blk(dg, c), sibling).wait_recv()
                pass_on(i, 2, dg)
        for i in range(n):
            copy(i, 0, blk((x, y), 1 - c), sibling).wait_recv()
            for j, chip in enumerate((xn, yn, dg)):
                copy(i, 7 + j, blk(chip, 1 - c), sibling).wait_recv()
        for cp in first + passed + onward():
            cp.wait_send()
        for cp in mine:
            cp.wait()

    return start, relay, finish


def _gather_sems(n):
    return [pltpu.SemaphoreType.DMA((GATHER_SLOTS, n)), pltpu.SemaphoreType.DMA((GATHER_SLOTS, n)),
            pltpu.SemaphoreType.DMA((n,))]


class _Gather:
    def __init__(self, shards, as_dtypes=None):
        n = len(shards)
        dts = [s.dtype for s in shards] if as_dtypes is None else list(as_dtypes)
        self.cast = [jnp.dtype(d) != s.dtype for d, s in zip(dts, shards)]
        self.ins = list(shards)
        self.in_specs = [_VMEM if c else _HBM for c in self.cast]
        self.out_shape = [_sds((NDEV,) + s.shape, d) for s, d in zip(shards, dts)]
        self.out_specs = [_HBM] * n
        self.scratch = [pltpu.VMEM(s.shape, d) for s, d, c in zip(shards, dts, self.cast) if c] + _gather_sems(n)

    def ops(self, ins, outs, scr):
        ncast = sum(self.cast)
        staged = iter(scr[:ncast])
        srcs = [next(staged) if c else ref for c, ref in zip(self.cast, ins)]
        start, relay, finish = _gather_ops(srcs, outs, *scr[ncast:])

        def cast_and_start():
            for c, ref, src in zip(self.cast, ins, srcs):
                if c:
                    src[...] = ref[...].astype(src.dtype)
            start()

        return cast_and_start, relay, finish


class _Exchange:
    def __init__(self, qs):
        n = len(qs)
        self.ins, self.in_specs = list(qs), [_HBM] * n
        self.out_shape = [_sds(q.shape, q.dtype) for q in qs]
        self.out_specs = [_HBM] * n
        self.scratch = [pltpu.SemaphoreType.DMA((NCHIP_OTHER, n)), pltpu.SemaphoreType.DMA((NCHIP_OTHER, n))]

    def ops(self, ins, outs, scr):
        send_sems, recv_sems = scr
        n = len(ins)
        x, y, c = _place()
        chips = _other_chips(x, y)

        def copies():
            return [pltpu.make_async_remote_copy(
                src_ref=ins[i].at[j], dst_ref=outs[i].at[j], send_sem=send_sems.at[j, i],
                recv_sem=recv_sems.at[j, i], device_id=(*chips[j], c), device_id_type=MESH)
                for i in range(n) for j in range(NCHIP_OTHER)]

        def start():
            for cp in copies():
                cp.start()

        def finish():
            cps = copies()
            for cp in cps:
                cp.wait_recv()
            for cp in cps:
                cp.wait_send()

        return start, lambda: None, finish


class _ExchangeVia:
    def __init__(self, q):
        _, r, cd = q.shape
        half = (2, r // 2, cd)
        self.ins, self.in_specs = [q], [_HBM]
        self.out_shape, self.out_specs = [_sds((2, r, cd), q.dtype)], [_HBM]
        self.scratch = [pltpu.VMEM(half, q.dtype), pltpu.VMEM(half, q.dtype), pltpu.VMEM(half, q.dtype),
                        pltpu.SemaphoreType.DMA((6,)), pltpu.SemaphoreType.DMA((6,)), pltpu.SemaphoreType.DMA((2,))]

    def ops(self, ins, outs, scr):
        (q,), (land,) = ins, outs
        relayed, own, comb, send_sems, recv_sems, local_sems = scr
        h = q.shape[1] // 2
        x, y, c = _place()
        xn, yn, _ = _other_chips(x, y)
        h0, h1 = pl.ds(0, h), pl.ds(h, h)

        def remote(k, src, dst, chip):
            return pltpu.make_async_remote_copy(src_ref=src, dst_ref=dst, send_sem=send_sems.at[k],
                                                recv_sem=recv_sems.at[k], device_id=(*chip, c), device_id_type=MESH)

        def via():
            return [remote(2, q.at[2, h0], relayed.at[0], xn), remote(3, q.at[2, h1], relayed.at[1], yn)]

        def direct():
            return [remote(0, q.at[0, h0], land.at[0, h0], xn), remote(1, q.at[1, h1], land.at[1, h1], yn)]

        def second():
            return [remote(4, comb.at[0], land.at[1, h0], yn), remote(5, comb.at[1], land.at[0, h1], xn)]

        def mine():
            return [pltpu.make_async_copy(q.at[1, h0], own.at[0], local_sems.at[0]),
                    pltpu.make_async_copy(q.at[0, h1], own.at[1], local_sems.at[1])]

        def start():
            for cp in via() + direct() + mine():
                cp.start()

        def relay():
            arrived, loaded, onward = via(), mine(), second()
            for k in range(2):
                arrived[k].wait_recv()
                loaded[k].wait()
                comb[k] = (own[k].astype(F32) + relayed[k].astype(F32)).astype(comb.dtype)
                onward[k].start()

        def finish():
            landing = direct() + second()
            for cp in landing:
                cp.wait_recv()
            for cp in via() + landing:
                cp.wait_send()

        return start, relay, finish


class _SumGather:
    def __init__(self, accs, lands):
        n = len(accs)
        self.n = n
        self.ins, self.in_specs = list(accs) + list(lands), [_VMEM] * (2 * n)
        self.out_shape = [_sds((NDEV,) + a.shape, a.dtype) for a in accs]
        self.out_specs = [_HBM] * n
        self.scratch = [pltpu.VMEM(a.shape, a.dtype) for a in accs] + _gather_sems(n)

    def ops(self, ins, outs, scr):
        n = self.n
        accs, lands, mine = ins[:n], ins[n:], scr[:n]
        g_start, relay, finish = _gather_ops(mine, outs, *scr[n:])

        def start():
            for i in range(n):
                mine[i][...] = accs[i][...] + lands[i][0] + lands[i][1] + lands[i][2]
            g_start()

        return start, relay, finish


def _call(main, jobs, *, name, grid, ins, in_specs, out_shape, out_specs, scratch, relay_step=0, first=0,
          prologue=None):
    nsteps = grid[0] if grid else 1
    n_in, n_out, n_scr = len(ins), len(out_shape), len(scratch)

    def body(*refs):
        pos = [0]

        def take(k):
            r = refs[pos[0]:pos[0] + k]
            pos[0] += k
            return r

        m_in = take(n_in)
        j_in = [take(len(j.ins)) for j in jobs]
        m_out = take(n_out)
        j_out = [take(len(j.out_shape)) for j in jobs]
        m_scr = take(n_scr)
        j_scr = [take(len(j.scratch)) for j in jobs]
        ops = [j.ops(a, b, s) for j, a, b, s in zip(jobs, j_in, j_out, j_scr)]
        i = pl.program_id(0) if grid else 0
        if not grid:
            for o in ops:
                o[0]()
            main(i, m_in, m_out, m_scr)
            for o in ops:
                o[1]()
            for o in ops:
                o[2]()
            return

        if ops:
            @pl.when(i == 0)
            def _():
                for o in ops:
                    o[0]()
                for o in ops[:first]:
                    o[1]()
                for o in ops[:first]:
                    o[2]()
                if prologue is not None:
                    prologue(j_out[:first], m_scr)

        main(i, m_in, m_out, m_scr)

        if ops[first:]:
            @pl.when(i == min(relay_step, nsteps - 1))
            def _():
                for o in ops[first:]:
                    o[1]()

            @pl.when(i == nsteps - 1)
            def _():
                for o in ops[first:]:
                    o[2]()

    extra = dict(dimension_semantics=("arbitrary",)) if grid else {}
    res = pl.pallas_call(
        body, name=name, grid=grid,
        in_specs=list(in_specs) + [s for j in jobs for s in j.in_specs],
        out_specs=list(out_specs) + [s for j in jobs for s in j.out_specs],
        out_shape=list(out_shape) + [s for j in jobs for s in j.out_shape],
        scratch_shapes=list(scratch) + [s for j in jobs for s in j.scratch],
        compiler_params=_params(**extra),
    )(*ins, *[a for j in jobs for a in j.ins])
    main_out, rest, job_out = res[:n_out], res[n_out:], []
    for j in jobs:
        k = len(j.out_shape)
        job_out.append(rest[:k])
        rest = rest[k:]
    return main_out, job_out


def _comm_only(jobs, name):
    _, job_out = _call(lambda i, a, b, s: None, jobs, name=name, grid=(), ins=[], in_specs=[], out_shape=[],
                       out_specs=[], scratch=[])
    return job_out


class _InChip:
    def __init__(self, ps):
        n = len(ps)
        self.n = n
        blk = [p.shape[1:] for p in ps]
        self.ins, self.in_specs = list(ps), [_HBM] * n
        self.out_shape = [_sds((NCHIP_OTHER,) + b, p.dtype) for b, p in zip(blk, ps)] + [_sds(b, F32) for b in blk]
        self.out_specs = [_VMEM] * (2 * n)
        self.scratch = ([pltpu.VMEM((4,) + b, p.dtype) for b, p in zip(blk, ps)] * 2
                        + [pltpu.SemaphoreType.DMA((4, n))] * 3)

    def ops(self, ins, outs, scr):
        n = self.n
        q_refs, acc_refs = outs[:n], outs[n:]
        mines, lands = scr[:n], scr[n:2 * n]
        send_sems, recv_sems, local_sems = scr[2 * n:]
        x, y, c = _place()
        sibling = (x, y, 1 - c)

        def copies():
            out = []
            for i in range(n):
                for pi in range(4):
                    loc = pltpu.make_async_copy(ins[i].at[2 * pi + c], mines[i].at[pi], local_sems.at[pi, i])
                    cp = pltpu.make_async_remote_copy(
                        src_ref=ins[i].at[2 * pi + (1 - c)], dst_ref=lands[i].at[pi],
                        send_sem=send_sems.at[pi, i], recv_sem=recv_sems.at[pi, i],
                        device_id=sibling, device_id_type=MESH)
                    out.append((loc, cp))
            return out

        def start():
            for loc, cp in copies():
                loc.start()
                cp.start()

        def finish():
            pairs = copies()
            for loc, cp in pairs:
                loc.wait()
                cp.wait_recv()
            for i in range(n):
                _chip_sums(mines[i], lands[i], q_refs[i], acc_refs[i], x, y)
            for _, cp in pairs:
                cp.wait_send()

        return start, lambda: None, finish


def _chip_sums(mine, land, q_ref, acc_ref, x, y):
    for j, (qx, qy) in enumerate(_other_chips(x, y)):
        qi = 2 * qx + qy
        q_ref[j] = (mine[qi].astype(F32) + land[qi].astype(F32)).astype(q_ref.dtype)
    mi = 2 * x + y
    acc_ref[...] = mine[mi].astype(F32) + land[mi].astype(F32)


def _direct_sum(v, buf, send_sems, recv_sems):
    x, y, c = _place()
    me = 4 * x + 2 * y + c
    buf[me] = v
    cps = []
    for k in range(1, NDEV):
        fx, fy, fc = (k >> 2) & 1, (k >> 1) & 1, k & 1
        peer = ((1 - x) if fx else x, (1 - y) if fy else y, (1 - c) if fc else c)
        cps.append((peer, pltpu.make_async_remote_copy(
            src_ref=buf.at[me], dst_ref=buf.at[me], send_sem=send_sems.at[k - 1], recv_sem=recv_sems.at[k - 1],
            device_id=peer, device_id_type=MESH)))
    for _, cp in cps:
        cp.start()
    for k, (peer, _) in enumerate(cps):
        theirs = 4 * peer[0] + 2 * peer[1] + peer[2]
        pltpu.make_async_remote_copy(
            src_ref=buf.at[theirs], dst_ref=buf.at[theirs], send_sem=send_sems.at[k], recv_sem=recv_sems.at[k],
            device_id=peer, device_id_type=MESH).wait_recv()
    acc = buf[0]
    for j in range(1, NDEV):
        acc = acc + buf[j]
    for _, cp in cps:
        cp.wait_send()
    return acc


def _direct_sum_scratch(shape, dtype):
    return [pltpu.VMEM((NDEV,) + tuple(shape), dtype), pltpu.SemaphoreType.DMA((NDEV - 1,)),
            pltpu.SemaphoreType.DMA((NDEV - 1,))]


def _fwd_a(x, nw, lnw, lnb, ws, bst, jobs, *, tm, relay_step):
    s_len = x.shape[0]
    nt = s_len // tm
    nch = tm // CH

    def main(i, ins, outs, scr):
        x_ref, nw_ref, lnw_ref, lnb_ref, ws_ref, bst_ref = ins
        z_ref, h_ref, y_ref, pp_ref = outs
        wc_scr, gv_scr, win_ref = scr

        @pl.when(i == 0)
        def _():
            m = _causal_mask()
            for g in range(G):
                wc_scr[g] = jnp.where(m, ws_ref[g], 0.0).astype(BF16)

        x = x_ref[...]
        h = (x * _rms(x) * nw_ref[...]).astype(BF16)
        h_ref[...] = h
        for k in range(NDEV):
            z_ref[:, k * CA:(k + 1) * CA] = _dot(h, win_ref[k])

        ssum = jnp.zeros((tm, 1), F32)
        for g in range(G):
            vs = slice(AW + g * GD, AW + (g + 1) * GD)
            gv, pv = _gelu_t(z_ref[:, vs])
            pp_ref[:, vs] = pv.astype(BF16)
            gv_scr[:, g * GD:(g + 1) * GD] = gv
            ssum = ssum + jnp.sum(gv, axis=-1, keepdims=True)
        mu = ssum * (1.0 / AW)
        vsum = jnp.zeros((tm, 1), F32)
        for g in range(G):
            dlt = gv_scr[:, g * GD:(g + 1) * GD] - mu
            vsum = vsum + jnp.sum(dlt * dlt, axis=-1, keepdims=True)
        rstd = lax.rsqrt(vsum * (1.0 / AW) + LN_EPS)

        for g in range(G):
            cs = slice(g * GD, (g + 1) * GD)
            gs = slice(2 * AW + g * GD, 2 * AW + (g + 1) * GD)
            v = (gv_scr[:, cs] - mu) * rstd * lnw_ref[:, cs] + lnb_ref[:, cs]
            vb = v.astype(BF16)
            u, pu = _gelu_t(z_ref[:, cs])
            pp_ref[:, cs] = pu.astype(BF16)
            zg = z_ref[:, gs]
            sig = _sigmoid(zg)
            pp_ref[:, gs] = sig.astype(BF16)
            sg = zg * sig
            for n in range(nch):
                rs = slice(n * CH, (n + 1) * CH)
                s = _dot(wc_scr[g], vb[rs, :]) + bst_ref[:, g:g + 1]
                y_ref[rs, cs] = (u[rs, :] * s * sg[rs, :]).astype(BF16)

    tile = lambda w: pl.BlockSpec((tm, w), lambda i: (i, 0))
    return _call(
        main, jobs, name="fwd_a", grid=(nt,), relay_step=relay_step, first=1,
        prologue=lambda gathered, scr: pltpu.sync_copy(gathered[0][0], scr[2]),
        ins=[x, nw, lnw, lnb, ws, bst], in_specs=[tile(D), _VMEM, _VMEM, _VMEM, _VMEM, _VMEM],
        out_shape=[_sds((s_len, 3 * AW), F32), _sds((s_len, D), BF16), _sds((s_len, AW), BF16),
                   _sds((s_len, 3 * AW), BF16)],
        out_specs=[tile(3 * AW), tile(D), tile(AW), tile(3 * AW)],
        scratch=[pltpu.VMEM((G, CH, CH), BF16), pltpu.VMEM((tm, AW), F32), pltpu.VMEM((NDEV, D, CA), BF16)])


def _bwd_a(dx1, z, pp, lnw, lnb, ws, bst, wout, jobs, *, tm, relay_step):
    s_len = dx1.shape[0]
    nt = s_len // tm
    nch = tm // CH

    def main(i, ins, outs, scr):
        dx1_ref, z_ref, pp_ref, lnw_ref, lnb_ref, ws_ref, bst_ref, wout_ref = ins
        dz_ref, glnw_ref, glnb_ref, gws_ref, gbst_ref = outs
        wc_scr, wct_scr, vh_scr, dgv_scr, dy_scr, dv_scr, gbs_acc, gwc_acc = scr

        @pl.when(i == 0)
        def _():
            m = _causal_mask()
            for g in range(G):
                wm = jnp.where(m, ws_ref[g], 0.0)
                wc_scr[g] = wm.astype(BF16)
                wct_scr[g] = wm.T.astype(BF16)
            glnw_ref[...] = jnp.zeros_like(glnw_ref)
            glnb_ref[...] = jnp.zeros_like(glnb_ref)
            gbs_acc[...] = jnp.zeros_like(gbs_acc)
            gwc_acc[...] = jnp.zeros_like(gwc_acc)

        dy_scr[...] = _dot_nt(dx1_ref[...], wout_ref[...])

        ssum = jnp.zeros((tm, 1), F32)
        for g in range(G):
            cs = slice(g * GD, (g + 1) * GD)
            vs = slice(AW + g * GD, AW + (g + 1) * GD)
            zv = z_ref[:, vs]
            pv = pp_ref[:, vs].astype(F32)
            gv = zv * pv
            vh_scr[:, cs] = gv
            dgv_scr[:, cs] = _dgelu(zv, pv)
            ssum = ssum + jnp.sum(gv, axis=-1, keepdims=True)
        mu = ssum * (1.0 / AW)
        vsum = jnp.zeros((tm, 1), F32)
        for g in range(G):
            dlt = vh_scr[:, g * GD:(g + 1) * GD] - mu
            vsum = vsum + jnp.sum(dlt * dlt, axis=-1, keepdims=True)
        rstd = lax.rsqrt(vsum * (1.0 / AW) + LN_EPS)

        m1 = jnp.zeros((tm, 1), F32)
        m2 = jnp.zeros((tm, 1), F32)
        for g in range(G):
            cs = slice(g * GD, (g + 1) * GD)
            gs = slice(2 * AW + g * GD, 2 * AW + (g + 1) * GD)
            vhat = (vh_scr[:, cs] - mu) * rstd
            vh_scr[:, cs] = vhat
            vb = (vhat * lnw_ref[:, cs] + lnb_ref[:, cs]).astype(BF16)
            zu = z_ref[:, cs]
            tu = pp_ref[:, cs].astype(F32)
            u = zu * tu
            zg = z_ref[:, gs]
            sig = pp_ref[:, gs].astype(F32)
            sg = zg * sig
            dy = dy_scr[:, cs]
            dsf = dy * u * sg
            dsb = dsf.astype(BF16)
            dvs = []
            for n in range(nch):
                rs = slice(n * CH, (n + 1) * CH)
                s = _dot(wc_scr[g], vb[rs, :]) + bst_ref[:, g:g + 1]
                dys = dy[rs, :] * s
                dz_ref[rs, cs] = (dys * sg[rs, :] * _dgelu(zu[rs, :], tu[rs, :])).astype(BF16)
                dz_ref[rs, gs] = (dys * u[rs, :] * (sig[rs, :] * (1.0 + zg[rs, :] * (1.0 - sig[rs, :])))).astype(BF16)
                gbs_acc[g] += dsf[rs, :]
                gwc_acc[g] += _dot_nt(dsb[rs, :], vb[rs, :])
                dvs.append(_dot(wct_scr[g], dsb[rs, :]))
            dv = jnp.concatenate(dvs, axis=0) if nch > 1 else dvs[0]
            glnw_ref[:, cs] += _rowsum(dv * vhat)
            glnb_ref[:, cs] += _rowsum(dv)
            dvh = dv * lnw_ref[:, cs]
            dv_scr[:, cs] = dvh
            m1 = m1 + jnp.sum(dvh, axis=-1, keepdims=True)
            m2 = m2 + jnp.sum(dvh * vhat, axis=-1, keepdims=True)
        m1 = m1 * (1.0 / AW)
        m2 = m2 * (1.0 / AW)
        for g in range(G):
            cs = slice(g * GD, (g + 1) * GD)
            dgv = rstd * (dv_scr[:, cs] - m1 - vh_scr[:, cs] * m2)
            dz_ref[:, AW + g * GD:AW + (g + 1) * GD] = (dgv * dgv_scr[:, cs]).astype(BF16)

        @pl.when(i == nt - 1)
        def _():
            m = _causal_mask()
            for g in range(G):
                gws_ref[g] = jnp.where(m, gwc_acc[g], 0.0)
                gbst_ref[:, g:g + 1] = jnp.sum(gbs_acc[g], axis=-1, keepdims=True)

    tile = lambda w: pl.BlockSpec((tm, w), lambda i: (i, 0))
    whole = lambda *s: pl.BlockSpec(s, lambda i: (0,) * len(s))
    big = lambda dt: pltpu.VMEM((tm, AW), dt)
    return _call(
        main, jobs, name="bwd_a", grid=(nt,), relay_step=relay_step,
        ins=[dx1, z, pp, lnw, lnb, ws, bst, wout],
        in_specs=[tile(D), tile(3 * AW), tile(3 * AW), _VMEM, _VMEM, _VMEM, _VMEM, _VMEM],
        out_shape=[_sds((s_len, 3 * AW), BF16), _sds((1, AW), F32), _sds((1, AW), F32), _sds((G, CH, CH), F32),
                   _sds((CH, G), F32)],
        out_specs=[tile(3 * AW), whole(1, AW), whole(1, AW), whole(G, CH, CH), whole(CH, G)],
        scratch=[pltpu.VMEM((G, CH, CH), BF16), pltpu.VMEM((G, CH, CH), BF16), big(F32), big(F32), big(F32), big(F32),
                 pltpu.VMEM((G, CH, GD), F32), pltpu.VMEM((G, CH, CH), F32)])


def _bwd_a_in(dz, dx1, x, nw, win8, jobs, *, tm, relay_step):
    s_len = x.shape[0]
    nt = s_len // tm

    def main(i, ins, outs, scr):
        dz_ref, dx1_ref, x_ref, nw_ref, win_ref = ins
        gx_ref, gnw_ref = outs

        @pl.when(i == 0)
        def _():
            gnw_ref[...] = jnp.zeros_like(gnw_ref)

        dh = jnp.zeros((tm, D), F32)
        for k in range(NDEV):
            dh = dh + _dot_nt(dz_ref[:, k * CA:(k + 1) * CA], win_ref[k])
        x = x_ref[...]
        r = _rms(x)
        gx_ref[...] = dx1_ref[...] + _rms_bwd(dh, x, r, nw_ref[...])
        gnw_ref[...] += _rowsum(dh * x * r)

        @pl.when(i == nt - 1)
        def _():
            gnw_ref[...] = _direct_sum(gnw_ref[...], *scr)

    tile = lambda w: pl.BlockSpec((tm, w), lambda i: (i, 0))
    return _call(
        main, jobs, name="bwd_a_in", grid=(nt,), relay_step=relay_step,
        ins=[dz, dx1, x, nw, win8], in_specs=[tile(3 * AW), tile(D), tile(D), _VMEM, _VMEM],
        out_shape=[_sds((s_len, D), F32), _sds((1, D), F32)],
        out_specs=[tile(D), pl.BlockSpec((1, D), lambda i: (0, 0))], scratch=_direct_sum_scratch((1, D), F32))


def _conv(p8_ref, cs, xb, xm1, xm2, xm3):
    xc = p8_ref[4:5, cs] + p8_ref[3:4, cs] * xb
    xc = xc + p8_ref[0:1, cs] * xm3
    xc = xc + p8_ref[1:2, cs] * xm2
    return xc + p8_ref[2:3, cs] * xm1


def _gates(p8_ref, gcat_ref, hh, xc):
    cs = slice(hh * HD, (hh + 1) * HD)
    pre = _dot(xc.astype(BF16), gcat_ref[hh])
    r = _sigmoid(pre[:, :HD] + p8_ref[5:6, cs])
    ig = _sigmoid(pre[:, HD:] + p8_ref[6:7, cs])
    sp = _softplus_neg(p8_ref[7:8, cs])
    la = (-RG_C) * r * sp
    a = jnp.exp(la)
    half_log = 0.5 * jnp.log(jnp.tanh(-la) * (1.0 + a * a))
    return r, ig, sp, a, jnp.exp(half_log), jnp.exp(-half_log)


def _scan_rows(a_ref, b_ref, out_ref, carry, tm, reverse):
    row = lax.broadcasted_iota(jnp.int32, (SUBLANES, BW), 0)
    ngrp = tm // SUBLANES

    def step(j, cr):
        jj = (ngrp - 1 - j) if reverse else j
        off = pl.multiple_of(jj * SUBLANES, SUBLANES)
        a = a_ref[pl.ds(off, SUBLANES), :]
        b = b_ref[pl.ds(off, SUBLANES), :]
        for sh in (1, 2, 4):
            if reverse:
                a_s = pltpu.roll(a, SUBLANES - sh, 0)
                b_s = pltpu.roll(b, SUBLANES - sh, 0)
                m = row < SUBLANES - sh
            else:
                a_s = pltpu.roll(a, sh, 0)
                b_s = pltpu.roll(b, sh, 0)
                m = row >= sh
            b = jnp.where(m, a * b_s + b, b)
            a = jnp.where(m, a * a_s, a)
        o = b + a * cr
        out_ref[pl.ds(off, SUBLANES), :] = o
        return o[0:1, :] if reverse else o[SUBLANES - 1:SUBLANES, :]

    return lax.fori_loop(0, ngrp, step, carry)


def _fwd_b(x, ya, wout_a, nw, win8, p8, gcat, jobs, *, tm, relay_step):
    s_len = x.shape[0]
    nt = s_len // tm

    def main(i, ins, outs, scr):
        x_ref, ya_ref, wouta_ref, nw_ref, win_ref, p8_ref, gcat_ref = ins
        x1_ref, zb_ref, hs_ref, h1_ref, yb_ref, xc_ref, a_ref, cc_ref, r_ref, ig_ref, m_ref = outs
        xbe_scr, b_scr, k_scr, carry_scr = scr

        @pl.when(i == 0)
        def _():
            xbe_scr[0:SUBLANES, :] = jnp.zeros((SUBLANES, BW), F32)
            carry_scr[...] = jnp.zeros_like(carry_scr)

        x1 = x_ref[...] + _dot(ya_ref[...], wouta_ref[...])
        x1_ref[...] = x1
        h = (x1 * _rms(x1) * nw_ref[...]).astype(BF16)
        h1_ref[...] = h
        for k in range(NDEV):
            zb_ref[:, k * CB:(k + 1) * CB] = _dot(h, win_ref[k])
        xbe_scr[SUBLANES:SUBLANES + tm, :] = zb_ref[:, :BW]
        for hh in range(BH):
            cs = slice(hh * HD, (hh + 1) * HD)
            xc = _conv(p8_ref, cs, xbe_scr[SUBLANES:SUBLANES + tm, cs], xbe_scr[7:7 + tm, cs],
                       xbe_scr[6:6 + tm, cs], xbe_scr[5:5 + tm, cs])
            r, ig, _, a, mult, rm = _gates(p8_ref, gcat_ref, hh, xc)
            ixc = ig * xc
            xc_ref[:, cs] = xc
            a_ref[:, cs] = a
            r_ref[:, cs] = r.astype(BF16)
            ig_ref[:, cs] = ig.astype(BF16)
            m_ref[:, cs] = mult.astype(BF16)
            b_scr[:, cs] = mult * ixc
            k_scr[:, cs] = ixc * (a * a * rm)
        xbe_scr[0:SUBLANES, :] = xbe_scr[tm:tm + SUBLANES, :]
        carry_scr[...] = _scan_rows(a_ref, b_scr, hs_ref, carry_scr[...], tm, False)
        for hh in range(BH):
            cs = slice(hh * HD, (hh + 1) * HD)
            gt = zb_ref[:, BW + hh * HD:BW + (hh + 1) * HD]
            hsv = hs_ref[:, cs]
            yb_ref[:, cs] = (hsv * (gt * _sigmoid(gt))).astype(BF16)
            cc_ref[:, cs] = (hsv - b_scr[:, cs]) - k_scr[:, cs]

    tile = lambda w: pl.BlockSpec((tm, w), lambda i: (i, 0))
    wide = lambda dt: _sds((s_len, BW), dt)
    return _call(
        main, jobs, name="fwd_b", grid=(nt,), relay_step=relay_step,
        ins=[x, ya, wout_a, nw, win8, p8, gcat], in_specs=[tile(D), tile(AW), _VMEM, _VMEM, _VMEM, _VMEM, _VMEM],
        out_shape=[_sds((s_len, D), F32), _sds((s_len, 2 * BW), F32), wide(F32), _sds((s_len, D), BF16), wide(BF16),
                   wide(F32), wide(F32), wide(F32), wide(BF16), wide(BF16), wide(BF16)],
        out_specs=[tile(D), tile(2 * BW), tile(BW), tile(D)] + [tile(BW)] * 7,
        scratch=[pltpu.VMEM((tm + SUBLANES, BW), F32), pltpu.VMEM((tm, BW), F32), pltpu.VMEM((tm, BW), F32),
                 pltpu.VMEM((1, BW), F32)])


def _head(x1, yb, wout, nfw, tgt, *, tm):
    s_len = x1.shape[0]

    def main(i, ins, outs, scr):
        x1_ref, yb_ref, wout_ref, nfw_ref, t_ref = ins
        dx2_ref, dx2b_ref, loss_ref, gnfw_ref = outs

        @pl.when(i == 0)
        def _():
            loss_ref[...] = jnp.zeros_like(loss_ref)
            gnfw_ref[...] = jnp.zeros_like(gnfw_ref)

        x2 = x1_ref[...] + _dot(yb_ref[...], wout_ref[...])
        rf = _rms(x2)
        xn = x2 * rf
        e = xn * nfw_ref[...] - t_ref[...]
        loss_ref[...] += (0.5 / D) * jnp.sum(jnp.sum(e * e, axis=-1, keepdims=True), axis=0, keepdims=True)
        dyf = e * (1.0 / D)
        gnfw_ref[...] += _rowsum(dyf * xn)
        dx2 = _rms_bwd(dyf, x2, rf, nfw_ref[...])
        dx2_ref[...] = dx2
        dx2b_ref[...] = dx2.astype(BF16)

    tile = lambda w: pl.BlockSpec((tm, w), lambda i: (i, 0))
    whole = lambda *s: pl.BlockSpec(s, lambda i: (0,) * len(s))
    (dx2, dx2b, loss, gnfw), _ = _call(
        main, [], name="head", grid=(s_len // tm,),
        ins=[x1, yb, wout, nfw, tgt], in_specs=[tile(D), tile(BW), _VMEM, _VMEM, tile(D)],
        out_shape=[_sds((s_len, D), F32), _sds((s_len, D), BF16), _sds((1, 1), F32), _sds((1, D), F32)],
        out_specs=[tile(D), tile(D), whole(1, 1), whole(1, D)], scratch=[])
    return dx2, dx2b, loss, gnfw


def _bwd_b(dx2, zb, hs, x1, saved, nw, win8, p8, gcat, wout, *, tm):
    s_len = x1.shape[0]
    nt = s_len // tm

    def main(i, ins, outs, scr):
        (dx2_ref, zb_ref, hs_ref, x1_ref, xc_ref, a_ref, cc_ref, r_ref, ig_ref, m_ref,
         nw_ref, win_ref, p8_ref, gcat_ref, wout_ref) = ins
        dx1_ref, dx1b_ref, dzb_ref, gp8_ref, gga_ref, ggx_ref, gnw_ref = outs
        ae_scr, an_scr, dhd_scr, dh_scr, dy_scr, dxce_scr, carry_scr, afirst_scr = scr

        @pl.when(i == 0)
        def _():
            gp8_ref[...] = jnp.zeros_like(gp8_ref)
            gga_ref[...] = jnp.zeros_like(gga_ref)
            ggx_ref[...] = jnp.zeros_like(ggx_ref)
            gnw_ref[...] = jnp.zeros_like(gnw_ref)
            dxce_scr[tm:tm + SUBLANES, :] = jnp.zeros((SUBLANES, BW), F32)
            carry_scr[...] = jnp.zeros_like(carry_scr)
            afirst_scr[...] = jnp.zeros_like(afirst_scr)

        dx2 = dx2_ref[...]
        dy_scr[...] = _dot_nt(dx2.astype(BF16), wout_ref[...])
        for hh in range(BH):
            cs = slice(hh * HD, (hh + 1) * HD)
            gs = slice(BW + hh * HD, BW + (hh + 1) * HD)
            gt = zb_ref[:, gs]
            sig = _sigmoid(gt)
            dy = dy_scr[:, cs]
            dhd_scr[:, cs] = dy * (gt * sig)
            dzb_ref[:, gs] = (dy * hs_ref[:, cs] * (sig * (1.0 + gt * (1.0 - sig)))).astype(BF16)

        ae_scr[0:tm, :] = a_ref[...]
        ae_scr[tm:tm + SUBLANES, :] = jnp.broadcast_to(afirst_scr[...], (SUBLANES, BW))
        an_scr[...] = ae_scr[1:1 + tm, :]
        afirst_scr[...] = ae_scr[0:1, :]
        carry_scr[...] = _scan_rows(an_scr, dhd_scr, dh_scr, carry_scr[...], tm, True)

        for hh in range(BH):
            cs = slice(hh * HD, (hh + 1) * HD)
            dh = dh_scr[:, cs]
            mult = m_ref[:, cs].astype(F32)
            ig = ig_ref[:, cs].astype(F32)
            r = r_ref[:, cs].astype(F32)
            xc = xc_ref[:, cs]
            lam = p8_ref[7:8, cs]
            sp = _softplus_neg(lam)
            dla = dh * cc_ref[:, cs]
            gp8_ref[7:8, cs] += _rowsum(dla * ((-RG_C) * r)) * (-_sigmoid(-lam))
            dpr = dla * ((-RG_C) * sp) * (r * (1.0 - r))
            dpi = dh * mult * xc * (ig * (1.0 - ig))
            gp8_ref[5:6, cs] += _rowsum(dpr)
            gp8_ref[6:7, cs] += _rowsum(dpi)
            dcat = jnp.concatenate([dpr, dpi], axis=1).astype(BF16)
            dxc = dh * mult * ig + _dot_nt(dcat, gcat_ref[hh])
            gg = _dot(xc.T.astype(BF16), dcat)
            gga_ref[hh] += gg[:, :HD]
            ggx_ref[hh] += gg[:, HD:]
            dxce_scr[0:tm, cs] = dxc
            gp8_ref[4:5, cs] += _rowsum(dxc)
        for hh in range(BH):
            cs = slice(hh * HD, (hh + 1) * HD)
            xb = zb_ref[:, cs]
            d0, d1 = dxce_scr[0:tm, cs], dxce_scr[1:1 + tm, cs]
            d2, d3 = dxce_scr[2:2 + tm, cs], dxce_scr[3:3 + tm, cs]
            dzb_ref[:, cs] = (p8_ref[3:4, cs] * d0 + p8_ref[2:3, cs] * d1 + p8_ref[1:2, cs] * d2
                              + p8_ref[0:1, cs] * d3).astype(BF16)
            gp8_ref[3:4, cs] += _rowsum(d0 * xb)
            gp8_ref[2:3, cs] += _rowsum(d1 * xb)
            gp8_ref[1:2, cs] += _rowsum(d2 * xb)
            gp8_ref[0:1, cs] += _rowsum(d3 * xb)
        dxce_scr[tm:tm + SUBLANES, :] = dxce_scr[0:SUBLANES, :]

        dh1 = jnp.zeros((tm, D), F32)
        for k in range(NDEV):
            dh1 = dh1 + _dot_nt(dzb_ref[:, k * CB:(k + 1) * CB], win_ref[k])
        x1 = x1_ref[...]
        r1 = _rms(x1)
        dx1 = dx2 + _rms_bwd(dh1, x1, r1, nw_ref[...])
        dx1_ref[...] = dx1
        dx1b_ref[...] = dx1.astype(BF16)
        gnw_ref[...] += _rowsum(dh1 * x1 * r1)

    tile = lambda w: pl.BlockSpec((tm, w), lambda i: (nt - 1 - i, 0))
    whole = lambda *s: pl.BlockSpec(s, lambda i: (0,) * len(s))
    full = lambda: pltpu.VMEM((tm, BW), F32)
    ext = lambda: pltpu.VMEM((tm + SUBLANES, BW), F32)
    out, _ = _call(
        main, [], name="bwd_b", grid=(nt,),
        ins=[dx2, zb, hs, x1, *saved, nw, win8, p8, gcat, wout],
        in_specs=[tile(D), tile(2 * BW), tile(BW), tile(D)] + [tile(BW)] * 6 + [_VMEM] * 5,
        out_shape=[_sds((s_len, D), F32), _sds((s_len, D), BF16), _sds((s_len, 2 * BW), BF16), _sds((SUBLANES, BW), F32),
                   _sds((BH, HD, HD), F32), _sds((BH, HD, HD), F32), _sds((1, D), F32)],
        out_specs=[tile(D), tile(D), tile(2 * BW), whole(SUBLANES, BW), whole(BH, HD, HD), whole(BH, HD, HD),
                   whole(1, D)],
        scratch=[ext(), full(), full(), full(), full(), ext(), pltpu.VMEM((1, BW), F32), pltpu.VMEM((1, BW), F32)])
    return out


def _transpose_into(dst_ref, src_ref, rows):
    s_len = src_ref.shape[0]
    for r0 in range(0, s_len, rows):
        dst_ref[:, r0:r0 + rows] = src_ref[r0:r0 + rows, :].astype(F32).T.astype(BF16)


def _wgrad(a, b, jobs, *, by_rows, per, name, relay_step=0):
    s_len, m = a.shape
    n = b.shape[1]
    r, cd = (m // NDEV, n) if by_rows else (m, n // NDEV)
    nsteps = NDEV // per
    at_rows = per * r if by_rows else m

    def main(i, ins, outs, scr):
        a_ref, b_ref = ins
        q_ref, acc_ref = outs
        at_scr, stage, mine, land, send_sems, recv_sems = scr
        x, y, c = _place()

        def to_sibling(pi):
            return pltpu.make_async_remote_copy(
                src_ref=stage.at[pi & 1], dst_ref=land.at[pi], send_sem=send_sems.at[pi], recv_sem=recv_sems.at[pi],
                device_id=(x, y, 1 - c), device_id_type=MESH)

        if by_rows:
            _transpose_into(at_scr, a_ref, TRANSPOSE_ROWS)
        else:
            @pl.when(i == 0)
            def _():
                _transpose_into(at_scr, a_ref, TRANSPOSE_ROWS)

        res = _dot(at_scr[...], b_ref[...]).astype(BF16)
        for k in range(per):
            blk = per * i + k
            pi, pc = blk >> 1, blk & 1
            val = res[k * r:(k + 1) * r, :] if by_rows else res

            @pl.when(pc != c)
            def _():
                @pl.when(pi >= 2)
                def _():
                    to_sibling(pi - 2).wait_send()

                stage[pi & 1] = val
                to_sibling(pi).start()

            @pl.when(pc == c)
            def _():
                mine[pi] = val

        @pl.when(i == nsteps - 1)
        def _():
            for p in range(4):
                to_sibling(p).wait_recv()
            to_sibling(2).wait_send()
            to_sibling(3).wait_send()
            _chip_sums(mine, land, q_ref, acc_ref, x, y)

    if by_rows:
        in_specs = [pl.BlockSpec((s_len, at_rows), lambda j: (0, j)), _VMEM]
    else:
        in_specs = [_VMEM, pl.BlockSpec((s_len, cd), lambda j: (0, j))]
    blk_vmem = lambda k: pltpu.VMEM((k, r, cd), BF16)
    (q, acc), job_out = _call(
        main, jobs, name=name, grid=(nsteps,), relay_step=relay_step, ins=[a, b], in_specs=in_specs,
        out_shape=[_sds((NCHIP_OTHER, r, cd), BF16), _sds((r, cd), F32)],
        out_specs=[pl.BlockSpec((NCHIP_OTHER, r, cd), lambda j: (0, 0, 0)), pl.BlockSpec((r, cd), lambda j: (0, 0))],
        scratch=[pltpu.VMEM((at_rows, s_len), BF16), blk_vmem(2), blk_vmem(4), blk_vmem(4),
                 pltpu.SemaphoreType.DMA((4,)), pltpu.SemaphoreType.DMA((4,))])
    return q, acc, job_out


def _wgrad_cols_early(a, b, jobs, *, name, relay_step=0):
    s_len, m = a.shape
    r, cd = m, b.shape[1] // NDEV
    h = r // 2

    def chip_at(pos, base):
        return base ^ (3 - pos)

    def main(i, ins, outs, scr):
        a_ref, b_ref = ins
        q_ref, acc_ref, rel_ref = outs
        at_scr, stage, mine, land, q2_scr, send_sems, recv_sems, via_send, via_recv = scr
        x, y, c = _place()
        base = 2 * x + y
        xn, yn, _ = _other_chips(x, y)
        pos, pc = i >> 1, i & 1
        pi = chip_at(pos, base)

        def to_sibling(chip, slot):
            return pltpu.make_async_remote_copy(
                src_ref=stage.at[slot], dst_ref=land.at[chip], send_sem=send_sems.at[chip],
                recv_sem=recv_sems.at[chip], device_id=(x, y, 1 - c), device_id_type=MESH)

        def via(k):
            return pltpu.make_async_remote_copy(
                src_ref=q2_scr.at[pl.ds(k * h, h)], dst_ref=rel_ref.at[k], send_sem=via_send.at[k],
                recv_sem=via_recv.at[k], device_id=(*(xn, yn)[k], c), device_id_type=MESH)

        @pl.when(i == 0)
        def _():
            _transpose_into(at_scr, a_ref, TRANSPOSE_ROWS)

        res = _dot(at_scr[...], b_ref[...]).astype(BF16)

        @pl.when(pc != c)
        def _():
            @pl.when(pos >= 2)
            def _():
                to_sibling(chip_at(pos - 2, base), pos & 1).wait_send()

            stage[pos & 1] = res
            to_sibling(pi, pos & 1).start()

        @pl.when(pc == c)
        def _():
            mine[pi] = res

        @pl.when(i == 1)
        def _():
            dg = chip_at(0, base)
            to_sibling(dg, 0).wait_recv()
            q2 = (mine[dg].astype(F32) + land[dg].astype(F32)).astype(BF16)
            q2_scr[...] = q2
            q_ref[2] = q2
            via(0).start()
            via(1).start()

        @pl.when(i == NDEV - 1)
        def _():
            for pos_ in (1, 2, 3):
                to_sibling(chip_at(pos_, base), 0).wait_recv()
            to_sibling(chip_at(2, base), 0).wait_send()
            to_sibling(chip_at(3, base), 1).wait_send()
            for k in range(2):
                via(k).wait_recv()
            for k in range(2):
                via(k).wait_send()
            for j, chip in enumerate((base ^ 2, base ^ 1)):
                q_ref[j] = (mine[chip].astype(F32) + land[chip].astype(F32)).astype(BF16)
            acc_ref[...] = mine[base].astype(F32) + land[base].astype(F32)

    def b_block(j):
        base = 2 * lax.axis_index("x") + lax.axis_index("y")
        return (0, 2 * chip_at(j >> 1, base) + (j & 1))

    blk_vmem = lambda k: pltpu.VMEM((k, r, cd), BF16)
    (q, acc, rel), job_out = _call(
        main, jobs, name=name, grid=(NDEV,), relay_step=relay_step, ins=[a, b],
        in_specs=[_VMEM, pl.BlockSpec((s_len, cd), b_block)],
        out_shape=[_sds((NCHIP_OTHER, r, cd), BF16), _sds((r, cd), F32), _sds((2, h, cd), BF16)],
        out_specs=[pl.BlockSpec((NCHIP_OTHER, r, cd), lambda j: (0, 0, 0)), pl.BlockSpec((r, cd), lambda j: (0, 0)), _HBM],
        scratch=[pltpu.VMEM((m, s_len), BF16), blk_vmem(2), blk_vmem(4), blk_vmem(4), pltpu.VMEM((r, cd), BF16),
                 pltpu.SemaphoreType.DMA((4,)), pltpu.SemaphoreType.DMA((4,)), pltpu.SemaphoreType.DMA((2,)),
                 pltpu.SemaphoreType.DMA((2,))])
    return q, acc, rel, job_out


class _ExchangeRest:
    def __init__(self, q, relayed):
        _, r, cd = q.shape
        half = (2, r // 2, cd)
        self.ins, self.in_specs = [q, relayed], [_HBM, _HBM]
        self.out_shape, self.out_specs = [_sds((2, r, cd), q.dtype)], [_HBM]
        self.scratch = [pltpu.VMEM(half, q.dtype), pltpu.VMEM(half, q.dtype), pltpu.VMEM(half, q.dtype),
                        pltpu.SemaphoreType.DMA((4,)), pltpu.SemaphoreType.DMA((4,)), pltpu.SemaphoreType.DMA((4,))]

    def ops(self, ins, outs, scr):
        (q, rel_in), (land,) = ins, outs
        own, rel, comb, send_sems, recv_sems, local_sems = scr
        h = q.shape[1] // 2
        x, y, c = _place()
        xn, yn, _ = _other_chips(x, y)
        h0, h1 = pl.ds(0, h), pl.ds(h, h)

        def remote(k, src, dst, chip):
            return pltpu.make_async_remote_copy(src_ref=src, dst_ref=dst, send_sem=send_sems.at[k],
                                                recv_sem=recv_sems.at[k], device_id=(*chip, c), device_id_type=MESH)

        def sends():
            return [remote(0, q.at[0, h0], land.at[0, h0], xn), remote(1, q.at[1, h1], land.at[1, h1], yn),
                    remote(2, comb.at[0], land.at[1, h0], yn), remote(3, comb.at[1], land.at[0, h1], xn)]

        def loads():
            return [pltpu.make_async_copy(q.at[1, h0], own.at[0], local_sems.at[0]),
                    pltpu.make_async_copy(q.at[0, h1], own.at[1], local_sems.at[1]),
                    pltpu.make_async_copy(rel_in.at[0], rel.at[0], local_sems.at[2]),
                    pltpu.make_async_copy(rel_in.at[1], rel.at[1], local_sems.at[3])]

        def start():
            cps, lds = sends(), loads()
            for ld in lds:
                ld.start()
            cps[0].start()
            cps[1].start()
            for ld in lds:
                ld.wait()
            for k in range(2):
                comb[k] = (own[k].astype(F32) + rel[k].astype(F32)).astype(comb.dtype)
            cps[2].start()
            cps[3].start()

        def finish():
            cps = sends()
            for cp in cps:
                cp.wait_recv()
            for cp in cps:
                cp.wait_send()

        return start, lambda: None, finish


def _adam_math(w, g, m, v):
    m = B1 * m + (1.0 - B1) * g
    v = B2 * v + (1.0 - B2) * (g * g)
    m_hat = m / (1.0 - B1 ** STEP)
    v_hat = v / (1.0 - B2 ** STEP)
    delta = (-LR) * (m_hat / (jnp.sqrt(v_hat) + ADAM_EPS) + WD * w)
    return delta, m, v


def _adam_big(w, acc, land, m, v, name):
    r, cd = w.shape
    rb = ADAM_ROWS if r % ADAM_ROWS == 0 else r
    nland = land.shape[0]

    def body(w_ref, acc_ref, land_ref, m_ref, v_ref, g_ref, d_ref, mo_ref, vo_ref):
        g = acc_ref[...]
        for j in range(nland):
            g = g + land_ref[j].astype(F32)
        g_ref[...] = g
        d_ref[...], mo_ref[...], vo_ref[...] = _adam_math(w_ref[...], g, m_ref[...], v_ref[...])

    blk = pl.BlockSpec((rb, cd), lambda i: (i, 0))
    blk3 = pl.BlockSpec((nland, rb, cd), lambda i: (0, i, 0))
    return pl.pallas_call(
        body, name=name, grid=(r // rb,), in_specs=[blk, blk, blk3, blk, blk], out_specs=[blk] * 4,
        out_shape=[_sds((r, cd), F32)] * 4,
        compiler_params=_params(dimension_semantics=("arbitrary",)),
    )(w, acc, land, m, v)


def _adam_small(groups):
    n = len(groups)

    def body(*refs):
        ins, outs = refs[:4 * n], refs[4 * n:]
        for k in range(n):
            w_ref, g_ref, m_ref, v_ref = ins[4 * k:4 * k + 4]
            d, mo, vo = _adam_math(w_ref[...], g_ref[...], m_ref[...], v_ref[...])
            outs[3 * k][...] = d
            outs[3 * k + 1][...] = mo
            outs[3 * k + 2][...] = vo

    flat = [a for grp in groups for a in grp]
    shapes = [_sds(grp[0].shape, F32) for grp in groups for _ in range(3)]
    res = pl.pallas_call(
        body, name="adam_small", in_specs=[_VMEM] * (4 * n), out_specs=[_VMEM] * (3 * n), out_shape=shapes,
        compiler_params=_params(),
    )(*flat)
    return [tuple(res[3 * k:3 * k + 3]) for k in range(n)]


TM_FWD_A = 256
RELAY_STEP_FWD_A = 2
RELAY_STEP_FWD_B = 2
TM_BWD_A = 256
RELAY_STEP_BWD_A = 3
TM_BWD_A_IN = 256
RELAY_STEP_BWD_A_IN = 4
RELAY_STEP_WGRAD_A_IN = 2
TM_FWD_B = 256
TM_HEAD = 512
TM_BWD_B = 256


def _pack(parts, rows):
    flat = jnp.concatenate([p.reshape(-1) for p in parts])
    return jnp.pad(flat, (0, NDEV * rows * LANES - flat.shape[0])).reshape(NDEV, rows, LANES)


def _unpack(packed, shapes):
    flat, out, off = packed.reshape(-1), [], 0
    for s in shapes:
        size = 1
        for d in s:
            size *= d
        out.append(flat[off:off + size].reshape(s))
        off += size
    return out


def kernel(x, norm_w, a_w_in, a_ln_w, a_ln_b, a_w_s, a_b_s, a_w_out, b_w_in, b_conv_w, b_conv_b, b_gate_a_w, b_gate_a_b, b_gate_x_w, b_gate_x_b, b_lambda, b_w_out, norm_f_w, loss_target, m_norm_w, m_a_w_in, m_a_ln_w, m_a_ln_b, m_a_w_s, m_a_b_s, m_a_w_out, m_b_w_in, m_b_conv_w, m_b_conv_b, m_b_gate_a_w, m_b_gate_a_b, m_b_gate_x_w, m_b_gate_x_b, m_b_lambda, m_b_w_out, m_norm_f_w, v_norm_w, v_a_w_in, v_a_ln_w, v_a_ln_b, v_a_w_s, v_a_b_s, v_a_w_out, v_b_w_in, v_b_conv_w, v_b_conv_b, v_b_gate_a_w, v_b_gate_a_b, v_b_gate_x_w, v_b_gate_x_b, v_b_lambda, v_b_w_out, v_norm_f_w):
    me = 4 * lax.axis_index("x") + 2 * lax.axis_index("y") + lax.axis_index("c")
    xs, tgt = x[0], loss_target[0]
    nw0, nw1, nfw = norm_w[0:1], norm_w[1:2], norm_f_w.reshape(1, D)
    w_s, bst = a_w_s[0], a_b_s[0].T
    gcat = jnp.concatenate([b_gate_a_w[0], b_gate_x_w[0]], axis=-1).astype(BF16)

    p8_shard = jnp.concatenate([b_conv_w[0], b_conv_b, b_gate_a_b, b_gate_x_b, b_lambda], axis=0)
    (z, h0, ya, pp), ((win_a8, p8_all), (wout_a8, win_b8)) = _fwd_a(
        xs, nw0, a_ln_w, a_ln_b, w_s, bst,
        [_Gather([a_w_in[0], p8_shard], [BF16, F32]), _Gather([a_w_out[0], b_w_in[0]], [BF16, BF16])],
        tm=TM_FWD_A, relay_step=RELAY_STEP_FWD_A)
    p8 = jnp.transpose(p8_all, (1, 0, 2)).reshape(SUBLANES, BW)
    wout_a = wout_a8.reshape(AW, D)
    (x1, zb, hs, h1, yb, *saved_b), ((wout_b8,),) = _fwd_b(
        xs, ya, wout_a, nw1, win_b8, p8, gcat, [_Gather([b_w_out[0]], [BF16])],
        tm=TM_FWD_B, relay_step=RELAY_STEP_FWD_B)
    wout_b = wout_b8.reshape(BW, D)
    dx2, dx2b, loss, g_nfw = _head(x1, yb, wout_b, nfw, tgt, tm=TM_HEAD)

    dx1, dx1b, dzb, g_p8, g_ga, g_gx, g_nw1 = _bwd_b(dx2, zb, hs, x1, saved_b, nw1, win_b8, p8, gcat, wout_b,
                                                     tm=TM_BWD_B)
    q_wout_b, acc_wout_b, _ = _wgrad(yb, dx2b, [], by_rows=True, per=2, name="wgrad_b_out")
    shapes_b = [(1, D), (1, D), (SUBLANES, BW), (1, 1)]
    pack_b = _pack([g_nfw, g_nw1, g_p8, loss], 16)
    small_b = _InChip([g_ga.reshape(NDEV, -1, HD), g_gx.reshape(NDEV, -1, HD), pack_b])
    q_win_b, acc_win_b, (sm_b, (l_wout_b,)) = _wgrad(h1, dzb, [small_b, _Exchange([q_wout_b])], by_rows=False, per=1,
                                                      name="wgrad_b_in")
    qs_b, accs_b = sm_b[:3], sm_b[3:]

    (dz, g_lnw, g_lnb, g_ws, g_bst), (lands_b, (l_win_b,)) = _bwd_a(
        dx1b, z, pp, a_ln_w, a_ln_b, w_s, bst, wout_a, [_Exchange(qs_b), _ExchangeVia(q_win_b)],
        tm=TM_BWD_A, relay_step=RELAY_STEP_BWD_A)
    shapes_a = [(1, AW), (1, AW), (CH, G)]
    pack_a = _pack([g_lnw, g_lnb, g_bst], 8)
    q_wout_a, acc_wout_a, (red_b, sm_a) = _wgrad(
        ya, dx1b, [_SumGather(accs_b, lands_b), _InChip([g_ws, pack_a])], by_rows=True, per=2,
        name="wgrad_a_out", relay_step=1)
    qs_a, accs_a = sm_a[:2], sm_a[2:]
    q_win_a, acc_win_a, rel_a, (lands_a, (l_wout_a,)) = _wgrad_cols_early(
        h0, dz, [_Exchange(qs_a), _ExchangeVia(q_wout_a)], name="wgrad_a_in", relay_step=RELAY_STEP_WGRAD_A_IN)
    (gx, g_nw0), (red_a, (l_win_a,)) = _bwd_a_in(
        dz, dx1, xs, nw0, win_a8, [_SumGather(accs_a, lands_a), _ExchangeRest(q_win_a, rel_a)],
        tm=TM_BWD_A_IN, relay_step=RELAY_STEP_BWD_A_IN)

    r_ga, r_gx, r_pack_b = red_b
    r_nfw, r_nw1, r_p8, loss = _unpack(r_pack_b, shapes_b)
    r_ws, r_pack_a = red_a
    r_lnw, r_lnb, r_bst = _unpack(r_pack_a, shapes_a)
    g_p8 = lax.dynamic_slice_in_dim(r_p8, me * (BW // NDEV), BW // NDEV, axis=1)
    loss = loss[0, 0]

    weights = dict(norm_w=norm_w, a_w_in=a_w_in, a_ln_w=a_ln_w, a_ln_b=a_ln_b, a_w_s=a_w_s, a_b_s=a_b_s, a_w_out=a_w_out,
                   b_w_in=b_w_in, b_conv_w=b_conv_w, b_conv_b=b_conv_b, b_gate_a_w=b_gate_a_w, b_gate_a_b=b_gate_a_b,
                   b_gate_x_w=b_gate_x_w, b_gate_x_b=b_gate_x_b, b_lambda=b_lambda, b_w_out=b_w_out, norm_f_w=norm_f_w)
    mom1 = dict(norm_w=m_norm_w, a_w_in=m_a_w_in, a_ln_w=m_a_ln_w, a_ln_b=m_a_ln_b, a_w_s=m_a_w_s, a_b_s=m_a_b_s,
                a_w_out=m_a_w_out, b_w_in=m_b_w_in, b_conv_w=m_b_conv_w, b_conv_b=m_b_conv_b, b_gate_a_w=m_b_gate_a_w,
                b_gate_a_b=m_b_gate_a_b, b_gate_x_w=m_b_gate_x_w, b_gate_x_b=m_b_gate_x_b, b_lambda=m_b_lambda,
                b_w_out=m_b_w_out, norm_f_w=m_norm_f_w)
    mom2 = dict(norm_w=v_norm_w, a_w_in=v_a_w_in, a_ln_w=v_a_ln_w, a_ln_b=v_a_ln_b, a_w_s=v_a_w_s, a_b_s=v_a_b_s,
                a_w_out=v_a_w_out, b_w_in=v_b_w_in, b_conv_w=v_b_conv_w, b_conv_b=v_b_conv_b, b_gate_a_w=v_b_gate_a_w,
                b_gate_a_b=v_b_gate_a_b, b_gate_x_w=v_b_gate_x_w, b_gate_x_b=v_b_gate_x_b, b_lambda=v_b_lambda,
                b_w_out=v_b_w_out, norm_f_w=v_norm_f_w)
    names = list(weights)

    def as2d(a):
        return a.reshape(-1, a.shape[-1])

    upd, grads = {}, {}
    for k, acc, land in (("a_w_in", acc_win_a, l_win_a), ("a_w_out", acc_wout_a, l_wout_a),
                         ("b_w_in", acc_win_b, l_win_b), ("b_w_out", acc_wout_b, l_wout_b)):
        g, d, mo, vo = _adam_big(as2d(weights[k]), acc, land, as2d(mom1[k]), as2d(mom2[k]), "adam_" + k)
        grads[k] = g[None]
        upd[k] = (d, mo, vo)
    grads.update(
        norm_w=jnp.concatenate([g_nw0, r_nw1], axis=0), a_ln_w=r_lnw, a_ln_b=r_lnb,
        a_w_s=r_ws.reshape(1, G, CH, CH), a_b_s=r_bst.T[None],
        b_conv_w=g_p8[None, 0:4], b_conv_b=g_p8[4:5], b_gate_a_w=r_ga.reshape(1, BH, HD, HD), b_gate_a_b=g_p8[5:6],
        b_gate_x_w=r_gx.reshape(1, BH, HD, HD), b_gate_x_b=g_p8[6:7], b_lambda=g_p8[7:8], norm_f_w=r_nfw.reshape(D))
    small_names = [k for k in names if k not in upd]
    res = _adam_small([(as2d(weights[k]), as2d(grads[k]), as2d(mom1[k]), as2d(mom2[k])) for k in small_names])
    for k, r3 in zip(small_names, res):
        upd[k] = r3
    deltas = [upd[k][0].reshape(weights[k].shape) for k in names]
    new_m = [upd[k][1].reshape(weights[k].shape) for k in names]
    new_v = [upd[k][2].reshape(weights[k].shape) for k in names]
    return (loss, gx[None], *[grads[k] for k in names], *deltas, *new_m, *new_v)
```

```python
import jax
import jax.numpy as jnp
from jax import lax
from jax.experimental import pallas as pl
from jax.experimental.pallas import tpu as pltpu

F32 = jnp.float32
BF16 = jnp.bfloat16
MESH = pl.DeviceIdType.MESH

NDEV = 8
NCHIP_OTHER = 3
D = 1024
AW = 2048
G = 8
GD = AW // G
CH = 128
BW = 1536
BH = 12
HD = BW // BH
CA = 3 * AW // NDEV
CB = 2 * BW // NDEV
RMS_EPS = 1e-6
LN_EPS = 1e-5
RG_C = 8.0
LR, B1, B2, ADAM_EPS, WD, STEP = 0.001, 0.9, 0.999, 1e-08, 0.01, 10
V7X_VMEM_BYTES = 64 * 1024 * 1024
VMEM_LIMIT = V7X_VMEM_BYTES - 8 * 1024 * 1024
SUBLANES = 8
LANES = 128
BF16_ROWS = 16
TRANSPOSE_ROWS = 256
ADAM_ROWS = 512
GELU_C = 0.7978845608028654
GELU_K = 0.044715

_VMEM = pl.BlockSpec(memory_space=pltpu.VMEM)
_HBM = pl.BlockSpec(memory_space=pltpu.HBM)


def _sds(shape, dtype):
    return jax.ShapeDtypeStruct(tuple(shape), dtype)


def _params(**kw):
    return pltpu.CompilerParams(vmem_limit_bytes=VMEM_LIMIT, **kw)


def _gelu_t(z):
    p = 0.5 * jnp.tanh(z * (GELU_C + (GELU_C * GELU_K) * (z * z))) + 0.5
    return z * p, p


def _dgelu(z, p):
    return p * (1.0 + (z * (1.0 - p)) * (2.0 * GELU_C + (6.0 * GELU_C * GELU_K) * (z * z)))


def _sigmoid(v):
    return 0.5 * jnp.tanh(0.5 * v) + 0.5


def _softplus_neg(lam):
    return jnp.maximum(-lam, 0.0) + jnp.log1p(jnp.exp(-jnp.abs(lam)))


def _dot(a, b):
    return jnp.dot(a, b, preferred_element_type=F32)


def _dot_nt(a, b):
    return lax.dot_general(a, b, (((1,), (1,)), ((), ())), preferred_element_type=F32)


def _rowsum(v):
    return jnp.sum(v, axis=0, keepdims=True)


def _causal_mask():
    r = lax.broadcasted_iota(jnp.int32, (CH, CH), 0)
    c = lax.broadcasted_iota(jnp.int32, (CH, CH), 1)
    return r >= c


def _rms(x):
    return lax.rsqrt(jnp.mean(x * x, axis=-1, keepdims=True) + RMS_EPS)


def _rms_bwd(dh, x, r, nw):
    gy = dh * nw
    return r * gy - x * (r * r * r) * jnp.mean(gy * x, axis=-1, keepdims=True)


def _place():
    return lax.axis_index("x"), lax.axis_index("y"), lax.axis_index("c")


def _other_chips(x, y):
    return [(1 - x, y), (x, 1 - y), (1 - x, 1 - y)]


GATHER_SLOTS = 10


def _gather_ops(ins, outs, send_sems, recv_sems, local_sems):
    n = len(ins)
    x, y, c = _place()
    sibling = (x, y, 1 - c)
    xn, yn, dg = _other_chips(x, y)
    split = [ins[i].shape[0] % (2 * BF16_ROWS) == 0 for i in range(n)]

    def blk(chip, core):
        return 4 * chip[0] + 2 * chip[1] + core

    me = blk((x, y), c)

    def part(ref, i, half):
        if half is None:
            return ref
        h = ins[i].shape[0] // 2
        return ref.at[pl.ds(half * h, h)]

    def copy(i, k, block, to, half=None, src=None):
        dst = part(outs[i].at[block], i, half)
        return pltpu.make_async_remote_copy(
            src_ref=dst if src is None else part(src, i, half), dst_ref=dst,
            send_sem=send_sems.at[k, i], recv_sem=recv_sems.at[k, i], device_id=to, device_id_type=MESH)

    def first_copies():
        mine = [pltpu.make_async_copy(ins[i], outs[i].at[me], local_sems.at[i]) for i in range(n)]
        first = []
        for i in range(n):
            first.append(copy(i, 0, me, sibling, src=ins[i]))
            if split[i]:
                first.append(copy(i, 1, me, (*xn, c), 0, ins[i]))
                first.append(copy(i, 3, me, (*yn, c), 1, ins[i]))
                first.append(copy(i, 2, me, (*xn, c), 1, ins[i]))
                first.append(copy(i, 4, me, (*yn, c), 0, ins[i]))
            else:
                first.append(copy(i, 1, me, (*xn, c), None, ins[i]))
                first.append(copy(i, 3, me, (*yn, c), None, ins[i]))
                first.append(copy(i, 5, me, (*dg, c), None, ins[i]))
        return mine, first

    def onward():
        out = []
        for i in range(n):
            if split[i]:
                out.append(copy(i, 5, blk(xn, c), (*yn, c), 0))
                out.append(copy(i, 6, blk(yn, c), (*xn, c), 1))
        return out

    def start():
        mine, first = first_copies()
        for cp in mine + first:
            cp.start()

    def relay():
        sends = onward()
        for i in range(n):
            if split[i]:
                copy(i, 1, blk(xn, c), sibling, 0).wait_recv()
                sends.pop(0).start()
                copy(i, 3, blk(yn, c), sibling, 1).wait_recv()
                sends.pop(0).start()

    def finish():
        mine, first = first_copies()
        passed = []

        def pass_on(i, j, chip):
            fwd = copy(i, 7 + j, blk(chip, c), sibling)
            fwd.start()
            passed.append(fwd)

        for i in range(n):
            if split[i]:
                copy(i, 2, blk(xn, c), sibling, 1).wait_recv()
                pass_on(i, 0, xn)
                copy(i, 4, blk(yn, c), sibling, 0).wait_recv()
                pass_on(i, 1, yn)
                copy(i, 5, blk(dg, c), sibling, 0).wait_recv()
                copy(i, 6, blk(dg, c), sibling, 1).wait_recv()
                pass_on(i, 2, dg)
            else:
                copy(i, 1, blk(xn, c), sibling).wait_recv()
                pass_on(i, 0, xn)
                copy(i, 3, blk(yn, c), sibling).wait_recv()
                pass_on(i, 1, yn)
                copy(i, 5, blk(dg, c), sibling).wait_recv()
                pass_on(i, 2, dg)
        for i in range(n):
            copy(i, 0, blk((x, y), 1 - c), sibling).wait_recv()
            for j, chip in enumerate((xn, yn, dg)):
                copy(i, 7 + j, blk(chip, 1 - c), sibling).wait_recv()
        for cp in first + passed + onward():
            cp.wait_send()
        for cp in mine:
            cp.wait()

    return start, relay, finish


def _gather_sems(n):
    return [pltpu.SemaphoreType.DMA((GATHER_SLOTS, n)), pltpu.SemaphoreType.DMA((GATHER_SLOTS, n)),
            pltpu.SemaphoreType.DMA((n,))]


class _Gather:
    def __init__(self, shards, as_dtypes=None):
        n = len(shards)
        dts = [s.dtype for s in shards] if as_dtypes is None else list(as_dtypes)
        self.cast = [jnp.dtype(d) != s.dtype for d, s in zip(dts, shards)]
        self.ins = list(shards)
        self.in_specs = [_VMEM if c else _HBM for c in self.cast]
        self.out_shape = [_sds((NDEV,) + s.shape, d) for s, d in zip(shards, dts)]
        self.out_specs = [_HBM] * n
        self.scratch = [pltpu.VMEM(s.shape, d) for s, d, c in zip(shards, dts, self.cast) if c] + _gather_sems(n)

    def ops(self, ins, outs, scr):
        ncast = sum(self.cast)
        staged = iter(scr[:ncast])
        srcs = [next(staged) if c else ref for c, ref in zip(self.cast, ins)]
        start, relay, finish = _gather_ops(srcs, outs, *scr[ncast:])

        def cast_and_start():
            for c, ref, src in zip(self.cast, ins, srcs):
                if c:
                    src[...] = ref[...].astype(src.dtype)
            start()

        return cast_and_start, relay, finish


class _Exchange:
    def __init__(self, qs):
        n = len(qs)
        self.ins, self.in_specs = list(qs), [_HBM] * n
        self.out_shape = [_sds(q.shape, q.dtype) for q in qs]
        self.out_specs = [_HBM] * n
        self.scratch = [pltpu.SemaphoreType.DMA((NCHIP_OTHER, n)), pltpu.SemaphoreType.DMA((NCHIP_OTHER, n))]

    def ops(self, ins, outs, scr):
        send_sems, recv_sems = scr
        n = len(ins)
        x, y, c = _place()
        chips = _other_chips(x, y)

        def copies():
            return [pltpu.make_async_remote_copy(
                src_ref=ins[i].at[j], dst_ref=outs[i].at[j], send_sem=send_sems.at[j, i],
                recv_sem=recv_sems.at[j, i], device_id=(*chips[j], c), device_id_type=MESH)
                for i in range(n) for j in range(NCHIP_OTHER)]

        def start():
            for cp in copies():
                cp.start()

        def finish():
            cps = copies()
            for cp in cps:
                cp.wait_recv()
            for cp in cps:
                cp.wait_send()

        return start, lambda: None, finish


class _ExchangeVia:
    def __init__(self, q):
        _, r, cd = q.shape
        half = (2, r // 2, cd)
        self.ins, self.in_specs = [q], [_HBM]
        self.out_shape, self.out_specs = [_sds((2, r, cd), q.dtype)], [_HBM]
        self.scratch = [pltpu.VMEM(half, q.dtype), pltpu.VMEM(half, q.dtype), pltpu.VMEM(half, q.dtype),
                        pltpu.SemaphoreType.DMA((6,)), pltpu.SemaphoreType.DMA((6,)), pltpu.SemaphoreType.DMA((2,))]

    def ops(self, ins, outs, scr):
        (q,), (land,) = ins, outs
        relayed, own, comb, send_sems, recv_sems, local_sems = scr
        h = q.shape[1] // 2
        x, y, c = _place()
        xn, yn, _ = _other_chips(x, y)
        h0, h1 = pl.ds(0, h), pl.ds(h, h)

        def remote(k, src, dst, chip):
            return pltpu.make_async_remote_copy(src_ref=src, dst_ref=dst, send_sem=send_sems.at[k],
                                                recv_sem=recv_sems.at[k], device_id=(*chip, c), device_id_type=MESH)

        def via():
            return [remote(2, q.at[2, h0], relayed.at[0], xn), remote(3, q.at[2, h1], relayed.at[1], yn)]

        def direct():
            return [remote(0, q.at[0, h0], land.at[0, h0], xn), remote(1, q.at[1, h1], land.at[1, h1], yn)]

        def second():
            return [remote(4, comb.at[0], land.at[1, h0], yn), remote(5, comb.at[1], land.at[0, h1], xn)]

        def mine():
            return [pltpu.make_async_copy(q.at[1, h0], own.at[0], local_sems.at[0]),
                    pltpu.make_async_copy(q.at[0, h1], own.at[1], local_sems.at[1])]

        def start():
            for cp in via() + direct() + mine():
                cp.start()

        def relay():
            arrived, loaded, onward = via(), mine(), second()
            for k in range(2):
                arrived[k].wait_recv()
                loaded[k].wait()
                comb[k] = (own[k].astype(F32) + relayed[k].astype(F32)).astype(comb.dtype)
                onward[k].start()

        def finish():
            landing = direct() + second()
            for cp in landing:
                cp.wait_recv()
            for cp in via() + landing:
                cp.wait_send()

        return start, relay, finish


class _SumGather:
    def __init__(self, accs, lands):
        n = len(accs)
        self.n = n
        self.ins, self.in_specs = list(accs) + list(lands), [_VMEM] * (2 * n)
        self.out_shape = [_sds((NDEV,) + a.shape, a.dtype) for a in accs]
        self.out_specs = [_HBM] * n
        self.scratch = [pltpu.VMEM(a.shape, a.dtype) for a in accs] + _gather_sems(n)

    def ops(self, ins, outs, scr):
        n = self.n
        accs, lands, mine = ins[:n], ins[n:], scr[:n]
        g_start, relay, finish = _gather_ops(mine, outs, *scr[n:])

        def start():
            for i in range(n):
                mine[i][...] = accs[i][...] + lands[i][0] + lands[i][1] + lands[i][2]
            g_start()

        return start, relay, finish


def _call(main, jobs, *, name, grid, ins, in_specs, out_shape, out_specs, scratch, relay_step=0, first=0,
          prologue=None):
    nsteps = grid[0] if grid else 1
    n_in, n_out, n_scr = len(ins), len(out_shape), len(scratch)

    def body(*refs):
        pos = [0]

        def take(k):
            r = refs[pos[0]:pos[0] + k]
            pos[0] += k
            return r

        m_in = take(n_in)
        j_in = [take(len(j.ins)) for j in jobs]
        m_out = take(n_out)
        j_out = [take(len(j.out_shape)) for j in jobs]
        m_scr = take(n_scr)
        j_scr = [take(len(j.scratch)) for j in jobs]
        ops = [j.ops(a, b, s) for j, a, b, s in zip(jobs, j_in, j_out, j_scr)]
        i = pl.program_id(0) if grid else 0
        if not grid:
            for o in ops:
                o[0]()
            main(i, m_in, m_out, m_scr)
            for o in ops:
                o[1]()
            for o in ops:
                o[2]()
            return

        if ops:
            @pl.when(i == 0)
            def _():
                for o in ops[:first]:
                    o[0]()
                for o in ops[:first]:
                    o[1]()
                for o in ops[first:]:
                    o[0]()
                for o in ops[:first]:
                    o[2]()
                if prologue is not None:
                    prologue(j_out[:first], m_scr)

        main(i, m_in, m_out, m_scr)

        if ops[first:]:
            @pl.when(i == min(relay_step, nsteps - 1))
            def _():
                for o in ops[first:]:
                    o[1]()

            @pl.when(i == nsteps - 1)
            def _():
                for o in ops[first:]:
                    o[2]()

    extra = dict(dimension_semantics=("arbitrary",)) if grid else {}
    res = pl.pallas_call(
        body, name=name, grid=grid,
        in_specs=list(in_specs) + [s for j in jobs for s in j.in_specs],
        out_specs=list(out_specs) + [s for j in jobs for s in j.out_specs],
        out_shape=list(out_shape) + [s for j in jobs for s in j.out_shape],
        scratch_shapes=list(scratch) + [s for j in jobs for s in j.scratch],
        compiler_params=_params(**extra),
    )(*ins, *[a for j in jobs for a in j.ins])
    main_out, rest, job_out = res[:n_out], res[n_out:], []
    for j in jobs:
        k = len(j.out_shape)
        job_out.append(rest[:k])
        rest = rest[k:]
    return main_out, job_out


def _comm_only(jobs, name):
    _, job_out = _call(lambda i, a, b, s: None, jobs, name=name, grid=(), ins=[], in_specs=[], out_shape=[],
                       out_specs=[], scratch=[])
    return job_out


class _InChip:
    def __init__(self, ps):
        n = len(ps)
        self.n = n
        blk = [p.shape[1:] for p in ps]
        self.ins, self.in_specs = list(ps), [_HBM] * n
        self.out_shape = [_sds((NCHIP_OTHER,) + b, p.dtype) for b, p in zip(blk, ps)] + [_sds(b, F32) for b in blk]
        self.out_specs = [_VMEM] * (2 * n)
        self.scratch = ([pltpu.VMEM((4,) + b, p.dtype) for b, p in zip(blk, ps)] * 2
                        + [pltpu.SemaphoreType.DMA((4, n))] * 3)

    def ops(self, ins, outs, scr):
        n = self.n
        q_refs, acc_refs = outs[:n], outs[n:]
        mines, lands = scr[:n], scr[n:2 * n]
        send_sems, recv_sems, local_sems = scr[2 * n:]
        x, y, c = _place()
        sibling = (x, y, 1 - c)

        def copies():
            out = []
            for i in range(n):
                for pi in range(4):
                    loc = pltpu.make_async_copy(ins[i].at[2 * pi + c], mines[i].at[pi], local_sems.at[pi, i])
                    cp = pltpu.make_async_remote_copy(
                        src_ref=ins[i].at[2 * pi + (1 - c)], dst_ref=lands[i].at[pi],
                        send_sem=send_sems.at[pi, i], recv_sem=recv_sems.at[pi, i],
                        device_id=sibling, device_id_type=MESH)
                    out.append((loc, cp))
            return out

        def start():
            for loc, cp in copies():
                loc.start()
                cp.start()

        def finish():
            pairs = copies()
            for loc, cp in pairs:
                loc.wait()
                cp.wait_recv()
            for i in range(n):
                _chip_sums(mines[i], lands[i], q_refs[i], acc_refs[i], x, y)
            for _, cp in pairs:
                cp.wait_send()

        return start, lambda: None, finish


def _chip_sums(mine, land, q_ref, acc_ref, x, y):
    for j, (qx, qy) in enumerate(_other_chips(x, y)):
        qi = 2 * qx + qy
        q_ref[j] = (mine[qi].astype(F32) + land[qi].astype(F32)).astype(q_ref.dtype)
    mi = 2 * x + y
    acc_ref[...] = mine[mi].astype(F32) + land[mi].astype(F32)


def _direct_sum(v, buf, send_sems, recv_sems):
    x, y, c = _place()
    me = 4 * x + 2 * y + c
    buf[me] = v
    cps = []
    for k in range(1, NDEV):
        fx, fy, fc = (k >> 2) & 1, (k >> 1) & 1, k & 1
        peer = ((1 - x) if fx else x, (1 - y) if fy else y, (1 - c) if fc else c)
        cps.append((peer, pltpu.make_async_remote_copy(
            src_ref=buf.at[me], dst_ref=buf.at[me], send_sem=send_sems.at[k - 1], recv_sem=recv_sems.at[k - 1],
            device_id=peer, device_id_type=MESH)))
    for _, cp in cps:
        cp.start()
    for k, (peer, _) in enumerate(cps):
        theirs = 4 * peer[0] + 2 * peer[1] + peer[2]
        pltpu.make_async_remote_copy(
            src_ref=buf.at[theirs], dst_ref=buf.at[theirs], send_sem=send_sems.at[k], recv_sem=recv_sems.at[k],
            device_id=peer, device_id_type=MESH).wait_recv()
    acc = buf[0]
    for j in range(1, NDEV):
        acc = acc + buf[j]
    for _, cp in cps:
        cp.wait_send()
    return acc


def _direct_sum_scratch(shape, dtype):
    return [pltpu.VMEM((NDEV,) + tuple(shape), dtype), pltpu.SemaphoreType.DMA((NDEV - 1,)),
            pltpu.SemaphoreType.DMA((NDEV - 1,))]


def _fwd_a(x, nw, lnw, lnb, ws, bst, jobs, *, tm, relay_step):
    s_len = x.shape[0]
    nt = s_len // tm
    nch = tm // CH

    def main(i, ins, outs, scr):
        x_ref, nw_ref, lnw_ref, lnb_ref, ws_ref, bst_ref = ins
        z_ref, h_ref, y_ref, pp_ref = outs
        wc_scr, gv_scr, win_ref = scr

        @pl.when(i == 0)
        def _():
            m = _causal_mask()
            for g in range(G):
                wc_scr[g] = jnp.where(m, ws_ref[g], 0.0).astype(BF16)

        x = x_ref[...]
        h = (x * _rms(x) * nw_ref[...]).astype(BF16)
        h_ref[...] = h
        for k in range(NDEV):
            z_ref[:, k * CA:(k + 1) * CA] = _dot(h, win_ref[k])

        ssum = jnp.zeros((tm, 1), F32)
        for g in range(G):
            vs = slice(AW + g * GD, AW + (g + 1) * GD)
            gv, pv = _gelu_t(z_ref[:, vs])
            pp_ref[:, vs] = pv.astype(BF16)
            gv_scr[:, g * GD:(g + 1) * GD] = gv
            ssum = ssum + jnp.sum(gv, axis=-1, keepdims=True)
        mu = ssum * (1.0 / AW)
        vsum = jnp.zeros((tm, 1), F32)
        for g in range(G):
            dlt = gv_scr[:, g * GD:(g + 1) * GD] - mu
            vsum = vsum + jnp.sum(dlt * dlt, axis=-1, keepdims=True)
        rstd = lax.rsqrt(vsum * (1.0 / AW) + LN_EPS)

        for g in range(G):
            cs = slice(g * GD, (g + 1) * GD)
            gs = slice(2 * AW + g * GD, 2 * AW + (g + 1) * GD)
            v = (gv_scr[:, cs] - mu) * rstd * lnw_ref[:, cs] + lnb_ref[:, cs]
            vb = v.astype(BF16)
            u, pu = _gelu_t(z_ref[:, cs])
            pp_ref[:, cs] = pu.astype(BF16)
            zg = z_ref[:, gs]
            sig = _sigmoid(zg)
            pp_ref[:, gs] = sig.astype(BF16)
            sg = zg * sig
            for n in range(nch):
                rs = slice(n * CH, (n + 1) * CH)
                s = _dot(wc_scr[g], vb[rs, :]) + bst_ref[:, g:g + 1]
                y_ref[rs, cs] = (u[rs, :] * s * sg[rs, :]).astype(BF16)

    tile = lambda w: pl.BlockSpec((tm, w), lambda i: (i, 0))
    return _call(
        main, jobs, name="fwd_a", grid=(nt,), relay_step=relay_step, first=1,
        prologue=lambda gathered, scr: pltpu.sync_copy(gathered[0][0], scr[2]),
        ins=[x, nw, lnw, lnb, ws, bst], in_specs=[tile(D), _VMEM, _VMEM, _VMEM, _VMEM, _VMEM],
        out_shape=[_sds((s_len, 3 * AW), F32), _sds((s_len, D), BF16), _sds((s_len, AW), BF16),
                   _sds((s_len, 3 * AW), BF16)],
        out_specs=[tile(3 * AW), tile(D), tile(AW), tile(3 * AW)],
        scratch=[pltpu.VMEM((G, CH, CH), BF16), pltpu.VMEM((tm, AW), F32), pltpu.VMEM((NDEV, D, CA), BF16)])


def _bwd_a(dx1, z, pp, lnw, lnb, ws, bst, wout, jobs, *, tm, relay_step):
    s_len = dx1.shape[0]
    nt = s_len // tm
    nch = tm // CH

    def main(i, ins, outs, scr):
        dx1_ref, z_ref, pp_ref, lnw_ref, lnb_ref, ws_ref, bst_ref, wout_ref = ins
        dz_ref, glnw_ref, glnb_ref, gws_ref, gbst_ref = outs
        wc_scr, wct_scr, vh_scr, dgv_scr, dy_scr, dv_scr, gbs_acc, gwc_acc = scr

        @pl.when(i == 0)
        def _():
            m = _causal_mask()
            for g in range(G):
                wm = jnp.where(m, ws_ref[g], 0.0)
                wc_scr[g] = wm.astype(BF16)
                wct_scr[g] = wm.T.astype(BF16)
            glnw_ref[...] = jnp.zeros_like(glnw_ref)
            glnb_ref[...] = jnp.zeros_like(glnb_ref)
            gbs_acc[...] = jnp.zeros_like(gbs_acc)
            gwc_acc[...] = jnp.zeros_like(gwc_acc)

        dy_scr[...] = _dot_nt(dx1_ref[...], wout_ref[...])

        ssum = jnp.zeros((tm, 1), F32)
        for g in range(G):
            cs = slice(g * GD, (g + 1) * GD)
            vs = slice(AW + g * GD, AW + (g + 1) * GD)
            zv = z_ref[:, vs]
            pv = pp_ref[:, vs].astype(F32)
            gv = zv * pv
            vh_scr[:, cs] = gv
            dgv_scr[:, cs] = _dgelu(zv, pv)
            ssum = ssum + jnp.sum(gv, axis=-1, keepdims=True)
        mu = ssum * (1.0 / AW)
        vsum = jnp.zeros((tm, 1), F32)
        for g in range(G):
            dlt = vh_scr[:, g * GD:(g + 1) * GD] - mu
            vsum = vsum + jnp.sum(dlt * dlt, axis=-1, keepdims=True)
        rstd = lax.rsqrt(vsum * (1.0 / AW) + LN_EPS)

        m1 = jnp.zeros((tm, 1), F32)
        m2 = jnp.zeros((tm, 1), F32)
        for g in range(G):
            cs = slice(g * GD, (g + 1) * GD)
            gs = slice(2 * AW + g * GD, 2 * AW + (g + 1) * GD)
            vhat = (vh_scr[:, cs] - mu) * rstd
            vh_scr[:, cs] = vhat
            vb = (vhat * lnw_ref[:, cs] + lnb_ref[:, cs]).astype(BF16)
            zu = z_ref[:, cs]
            tu = pp_ref[:, cs].astype(F32)
            u = zu * tu
            zg = z_ref[:, gs]
            sig = pp_ref[:, gs].astype(F32)
            sg = zg * sig
            dy = dy_scr[:, cs]
            dsf = dy * u * sg
            dsb = dsf.astype(BF16)
            dvs = []
            for n in range(nch):
                rs = slice(n * CH, (n + 1) * CH)
                s = _dot(wc_scr[g], vb[rs, :]) + bst_ref[:, g:g + 1]
                dys = dy[rs, :] * s
                dz_ref[rs, cs] = (dys * sg[rs, :] * _dgelu(zu[rs, :], tu[rs, :])).astype(BF16)
                dz_ref[rs, gs] = (dys * u[rs, :] * (sig[rs, :] * (1.0 + zg[rs, :] * (1.0 - sig[rs, :])))).astype(BF16)
                gbs_acc[g] += dsf[rs, :]
                gwc_acc[g] += _dot_nt(dsb[rs, :], vb[rs, :])
                dvs.append(_dot(wct_scr[g], dsb[rs, :]))
            dv = jnp.concatenate(dvs, axis=0) if nch > 1 else dvs[0]
            glnw_ref[:, cs] += _rowsum(dv * vhat)
            glnb_ref[:, cs] += _rowsum(dv)
            dvh = dv * lnw_ref[:, cs]
            dv_scr[:, cs] = dvh
            m1 = m1 + jnp.sum(dvh, axis=-1, keepdims=True)
            m2 = m2 + jnp.sum(dvh * vhat, axis=-1, keepdims=True)
        m1 = m1 * (1.0 / AW)
        m2 = m2 * (1.0 / AW)
        for g in range(G):
            cs = slice(g * GD, (g + 1) * GD)
            dgv = rstd * (dv_scr[:, cs] - m1 - vh_scr[:, cs] * m2)
            dz_ref[:, AW + g * GD:AW + (g + 1) * GD] = (dgv * dgv_scr[:, cs]).astype(BF16)

        @pl.when(i == nt - 1)
        def _():
            m = _causal_mask()
            for g in range(G):
                gws_ref[g] = jnp.where(m, gwc_acc[g], 0.0)
                gbst_ref[:, g:g + 1] = jnp.sum(gbs_acc[g], axis=-1, keepdims=True)

    tile = lambda w: pl.BlockSpec((tm, w), lambda i: (i, 0))
    whole = lambda *s: pl.BlockSpec(s, lambda i: (0,) * len(s))
    big = lambda dt: pltpu.VMEM((tm, AW), dt)
    return _call(
        main, jobs, name="bwd_a", grid=(nt,), relay_step=relay_step,
        ins=[dx1, z, pp, lnw, lnb, ws, bst, wout],
        in_specs=[tile(D), tile(3 * AW), tile(3 * AW), _VMEM, _VMEM, _VMEM, _VMEM, _VMEM],
        out_shape=[_sds((s_len, 3 * AW), BF16), _sds((1, AW), F32), _sds((1, AW), F32), _sds((G, CH, CH), F32),
                   _sds((CH, G), F32)],
        out_specs=[tile(3 * AW), whole(1, AW), whole(1, AW), whole(G, CH, CH), whole(CH, G)],
        scratch=[pltpu.VMEM((G, CH, CH), BF16), pltpu.VMEM((G, CH, CH), BF16), big(F32), big(F32), big(F32), big(F32),
                 pltpu.VMEM((G, CH, GD), F32), pltpu.VMEM((G, CH, CH), F32)])


def _bwd_a_in(dz, dx1, x, nw, win8, jobs, *, tm, relay_step):
    s_len = x.shape[0]
    nt = s_len // tm

    def main(i, ins, outs, scr):
        dz_ref, dx1_ref, x_ref, nw_ref, win_ref = ins
        gx_ref, gnw_ref = outs

        @pl.when(i == 0)
        def _():
            gnw_ref[...] = jnp.zeros_like(gnw_ref)

        dh = jnp.zeros((tm, D), F32)
        for k in range(NDEV):
            dh = dh + _dot_nt(dz_ref[:, k * CA:(k + 1) * CA], win_ref[k])
        x = x_ref[...]
        r = _rms(x)
        gx_ref[...] = dx1_ref[...] + _rms_bwd(dh, x, r, nw_ref[...])
        gnw_ref[...] += _rowsum(dh * x * r)

        @pl.when(i == nt - 1)
        def _():
            gnw_ref[...] = _direct_sum(gnw_ref[...], *scr)

    tile = lambda w: pl.BlockSpec((tm, w), lambda i: (i, 0))
    return _call(
        main, jobs, name="bwd_a_in", grid=(nt,), relay_step=relay_step,
        ins=[dz, dx1, x, nw, win8], in_specs=[tile(3 * AW), tile(D), tile(D), _VMEM, _VMEM],
        out_shape=[_sds((s_len, D), F32), _sds((1, D), F32)],
        out_specs=[tile(D), pl.BlockSpec((1, D), lambda i: (0, 0))], scratch=_direct_sum_scratch((1, D), F32))


def _conv(p8_ref, cs, xb, xm1, xm2, xm3):
    xc = p8_ref[4:5, cs] + p8_ref[3:4, cs] * xb
    xc = xc + p8_ref[0:1, cs] * xm3
    xc = xc + p8_ref[1:2, cs] * xm2
    return xc + p8_ref[2:3, cs] * xm1


def _gates(p8_ref, gcat_ref, hh, xc):
    cs = slice(hh * HD, (hh + 1) * HD)
    pre = _dot(xc.astype(BF16), gcat_ref[hh])
    r = _sigmoid(pre[:, :HD] + p8_ref[5:6, cs])
    ig = _sigmoid(pre[:, HD:] + p8_ref[6:7, cs])
    sp = _softplus_neg(p8_ref[7:8, cs])
    la = (-RG_C) * r * sp
    a = jnp.exp(la)
    half_log = 0.5 * jnp.log(jnp.tanh(-la) * (1.0 + a * a))
    return r, ig, sp, a, jnp.exp(half_log), jnp.exp(-half_log)


def _scan_rows(a_ref, b_ref, out_ref, carry, tm, reverse):
    row = lax.broadcasted_iota(jnp.int32, (SUBLANES, BW), 0)
    ngrp = tm // SUBLANES

    def step(j, cr):
        jj = (ngrp - 1 - j) if reverse else j
        off = pl.multiple_of(jj * SUBLANES, SUBLANES)
        a = a_ref[pl.ds(off, SUBLANES), :]
        b = b_ref[pl.ds(off, SUBLANES), :]
        for sh in (1, 2, 4):
            if reverse:
                a_s = pltpu.roll(a, SUBLANES - sh, 0)
                b_s = pltpu.roll(b, SUBLANES - sh, 0)
                m = row < SUBLANES - sh
            else:
                a_s = pltpu.roll(a, sh, 0)
                b_s = pltpu.roll(b, sh, 0)
                m = row >= sh
            b = jnp.where(m, a * b_s + b, b)
            a = jnp.where(m, a * a_s, a)
        o = b + a * cr
        out_ref[pl.ds(off, SUBLANES), :] = o
        return o[0:1, :] if reverse else o[SUBLANES - 1:SUBLANES, :]

    return lax.fori_loop(0, ngrp, step, carry)


def _fwd_b(x, ya, wout_a, nw, win8, p8, gcat, jobs, *, tm, relay_step):
    s_len = x.shape[0]
    nt = s_len // tm

    def main(i, ins, outs, scr):
        x_ref, ya_ref, wouta_ref, nw_ref, win_ref, p8_ref, gcat_ref = ins
        x1_ref, zb_ref, hs_ref, h1_ref, yb_ref, xc_ref, a_ref, cc_ref, r_ref, ig_ref, m_ref = outs
        xbe_scr, b_scr, k_scr, carry_scr = scr

        @pl.when(i == 0)
        def _():
            xbe_scr[0:SUBLANES, :] = jnp.zeros((SUBLANES, BW), F32)
            carry_scr[...] = jnp.zeros_like(carry_scr)

        x1 = x_ref[...] + _dot(ya_ref[...], wouta_ref[...])
        x1_ref[...] = x1
        h = (x1 * _rms(x1) * nw_ref[...]).astype(BF16)
        h1_ref[...] = h
        for k in range(NDEV):
            zb_ref[:, k * CB:(k + 1) * CB] = _dot(h, win_ref[k])
        xbe_scr[SUBLANES:SUBLANES + tm, :] = zb_ref[:, :BW]
        for hh in range(BH):
            cs = slice(hh * HD, (hh + 1) * HD)
            xc = _conv(p8_ref, cs, xbe_scr[SUBLANES:SUBLANES + tm, cs], xbe_scr[7:7 + tm, cs],
                       xbe_scr[6:6 + tm, cs], xbe_scr[5:5 + tm, cs])
            r, ig, _, a, mult, rm = _gates(p8_ref, gcat_ref, hh, xc)
            ixc = ig * xc
            xc_ref[:, cs] = xc
            a_ref[:, cs] = a
            r_ref[:, cs] = r.astype(BF16)
            ig_ref[:, cs] = ig.astype(BF16)
            m_ref[:, cs] = mult.astype(BF16)
            b_scr[:, cs] = mult * ixc
            k_scr[:, cs] = ixc * (a * a * rm)
        xbe_scr[0:SUBLANES, :] = xbe_scr[tm:tm + SUBLANES, :]
        carry_scr[...] = _scan_rows(a_ref, b_scr, hs_ref, carry_scr[...], tm, False)
        for hh in range(BH):
            cs = slice(hh * HD, (hh + 1) * HD)
            gt = zb_ref[:, BW + hh * HD:BW + (hh + 1) * HD]
            hsv = hs_ref[:, cs]
            yb_ref[:, cs] = (hsv * (gt * _sigmoid(gt))).astype(BF16)
            cc_ref[:, cs] = (hsv - b_scr[:, cs]) - k_scr[:, cs]

    tile = lambda w: pl.BlockSpec((tm, w), lambda i: (i, 0))
    wide = lambda dt: _sds((s_len, BW), dt)
    return _call(
        main, jobs, name="fwd_b", grid=(nt,), relay_step=relay_step,
        ins=[x, ya, wout_a, nw, win8, p8, gcat], in_specs=[tile(D), tile(AW), _VMEM, _VMEM, _VMEM, _VMEM, _VMEM],
        out_shape=[_sds((s_len, D), F32), _sds((s_len, 2 * BW), F32), wide(F32), _sds((s_len, D), BF16), wide(BF16),
                   wide(F32), wide(F32), wide(F32), wide(BF16), wide(BF16), wide(BF16)],
        out_specs=[tile(D), tile(2 * BW), tile(BW), tile(D)] + [tile(BW)] * 7,
        scratch=[pltpu.VMEM((tm + SUBLANES, BW), F32), pltpu.VMEM((tm, BW), F32), pltpu.VMEM((tm, BW), F32),
                 pltpu.VMEM((1, BW), F32)])


def _head(x1, yb, wout, nfw, tgt, *, tm):
    s_len = x1.shape[0]

    def main(i, ins, outs, scr):
        x1_ref, yb_ref, wout_ref, nfw_ref, t_ref = ins
        dx2_ref, dx2b_ref, loss_ref, gnfw_ref = outs

        @pl.when(i == 0)
        def _():
            loss_ref[...] = jnp.zeros_like(loss_ref)
            gnfw_ref[...] = jnp.zeros_like(gnfw_ref)

        x2 = x1_ref[...] + _dot(yb_ref[...], wout_ref[...])
        rf = _rms(x2)
        xn = x2 * rf
        e = xn * nfw_ref[...] - t_ref[...]
        loss_ref[...] += (0.5 / D) * jnp.sum(jnp.sum(e * e, axis=-1, keepdims=True), axis=0, keepdims=True)
        dyf = e * (1.0 / D)
        gnfw_ref[...] += _rowsum(dyf * xn)
        dx2 = _rms_bwd(dyf, x2, rf, nfw_ref[...])
        dx2_ref[...] = dx2
        dx2b_ref[...] = dx2.astype(BF16)

    tile = lambda w: pl.BlockSpec((tm, w), lambda i: (i, 0))
    whole = lambda *s: pl.BlockSpec(s, lambda i: (0,) * len(s))
    (dx2, dx2b, loss, gnfw), _ = _call(
        main, [], name="head", grid=(s_len // tm,),
        ins=[x1, yb, wout, nfw, tgt], in_specs=[tile(D), tile(BW), _VMEM, _VMEM, tile(D)],
        out_shape=[_sds((s_len, D), F32), _sds((s_len, D), BF16), _sds((1, 1), F32), _sds((1, D), F32)],
        out_specs=[tile(D), tile(D), whole(1, 1), whole(1, D)], scratch=[])
    return dx2, dx2b, loss, gnfw


def _bwd_b(dx2, zb, hs, x1, saved, nw, win8, p8, gcat, wout, *, tm):
    s_len = x1.shape[0]
    nt = s_len // tm

    def main(i, ins, outs, scr):
        (dx2_ref, zb_ref, hs_ref, x1_ref, xc_ref, a_ref, cc_ref, r_ref, ig_ref, m_ref,
         nw_ref, win_ref, p8_ref, gcat_ref, wout_ref) = ins
        dx1_ref, dx1b_ref, dzb_ref, gp8_ref, gga_ref, ggx_ref, gnw_ref = outs
        ae_scr, an_scr, dhd_scr, dh_scr, dy_scr, dxce_scr, carry_scr, afirst_scr = scr

        @pl.when(i == 0)
        def _():
            gp8_ref[...] = jnp.zeros_like(gp8_ref)
            gga_ref[...] = jnp.zeros_like(gga_ref)
            ggx_ref[...] = jnp.zeros_like(ggx_ref)
            gnw_ref[...] = jnp.zeros_like(gnw_ref)
            dxce_scr[tm:tm + SUBLANES, :] = jnp.zeros((SUBLANES, BW), F32)
            carry_scr[...] = jnp.zeros_like(carry_scr)
            afirst_scr[...] = jnp.zeros_like(afirst_scr)

        dx2 = dx2_ref[...]
        dy_scr[...] = _dot_nt(dx2.astype(BF16), wout_ref[...])
        for hh in range(BH):
            cs = slice(hh * HD, (hh + 1) * HD)
            gs = slice(BW + hh * HD, BW + (hh + 1) * HD)
            gt = zb_ref[:, gs]
            sig = _sigmoid(gt)
            dy = dy_scr[:, cs]
            dhd_scr[:, cs] = dy * (gt * sig)
            dzb_ref[:, gs] = (dy * hs_ref[:, cs] * (sig * (1.0 + gt * (1.0 - sig)))).astype(BF16)

        ae_scr[0:tm, :] = a_ref[...]
        ae_scr[tm:tm + SUBLANES, :] = jnp.broadcast_to(afirst_scr[...], (SUBLANES, BW))
        an_scr[...] = ae_scr[1:1 + tm, :]
        afirst_scr[...] = ae_scr[0:1, :]
        carry_scr[...] = _scan_rows(an_scr, dhd_scr, dh_scr, carry_scr[...], tm, True)

        for hh in range(BH):
            cs = slice(hh * HD, (hh + 1) * HD)
            dh = dh_scr[:, cs]
            mult = m_ref[:, cs].astype(F32)
            ig = ig_ref[:, cs].astype(F32)
            r = r_ref[:, cs].astype(F32)
            xc = xc_ref[:, cs]
            lam = p8_ref[7:8, cs]
            sp = _softplus_neg(lam)
            dla = dh * cc_ref[:, cs]
            gp8_ref[7:8, cs] += _rowsum(dla * ((-RG_C) * r)) * (-_sigmoid(-lam))
            dpr = dla * ((-RG_C) * sp) * (r * (1.0 - r))
            dpi = dh * mult * xc * (ig * (1.0 - ig))
            gp8_ref[5:6, cs] += _rowsum(dpr)
            gp8_ref[6:7, cs] += _rowsum(dpi)
            dcat = jnp.concatenate([dpr, dpi], axis=1).astype(BF16)
            dxc = dh * mult * ig + _dot_nt(dcat, gcat_ref[hh])
            gg = _dot(xc.T.astype(BF16), dcat)
            gga_ref[hh] += gg[:, :HD]
            ggx_ref[hh] += gg[:, HD:]
            dxce_scr[0:tm, cs] = dxc
            gp8_ref[4:5, cs] += _rowsum(dxc)
        for hh in range(BH):
            cs = slice(hh * HD, (hh + 1) * HD)
            xb = zb_ref[:, cs]
            d0, d1 = dxce_scr[0:tm, cs], dxce_scr[1:1 + tm, cs]
            d2, d3 = dxce_scr[2:2 + tm, cs], dxce_scr[3:3 + tm, cs]
            dzb_ref[:, cs] = (p8_ref[3:4, cs] * d0 + p8_ref[2:3, cs] * d1 + p8_ref[1:2, cs] * d2
                              + p8_ref[0:1, cs] * d3).astype(BF16)
            gp8_ref[3:4, cs] += _rowsum(d0 * xb)
            gp8_ref[2:3, cs] += _rowsum(d1 * xb)
            gp8_ref[1:2, cs] += _rowsum(d2 * xb)
            gp8_ref[0:1, cs] += _rowsum(d3 * xb)
        dxce_scr[tm:tm + SUBLANES, :] = dxce_scr[0:SUBLANES, :]

        dh1 = jnp.zeros((tm, D), F32)
        for k in range(NDEV):
            dh1 = dh1 + _dot_nt(dzb_ref[:, k * CB:(k + 1) * CB], win_ref[k])
        x1 = x1_ref[...]
        r1 = _rms(x1)
        dx1 = dx2 + _rms_bwd(dh1, x1, r1, nw_ref[...])
        dx1_ref[...] = dx1
        dx1b_ref[...] = dx1.astype(BF16)
        gnw_ref[...] += _rowsum(dh1 * x1 * r1)

    tile = lambda w: pl.BlockSpec((tm, w), lambda i: (nt - 1 - i, 0))
    whole = lambda *s: pl.BlockSpec(s, lambda i: (0,) * len(s))
    full = lambda: pltpu.VMEM((tm, BW), F32)
    ext = lambda: pltpu.VMEM((tm + SUBLANES, BW), F32)
    out, _ = _call(
        main, [], name="bwd_b", grid=(nt,),
        ins=[dx2, zb, hs, x1, *saved, nw, win8, p8, gcat, wout],
        in_specs=[tile(D), tile(2 * BW), tile(BW), tile(D)] + [tile(BW)] * 6 + [_VMEM] * 5,
        out_shape=[_sds((s_len, D), F32), _sds((s_len, D), BF16), _sds((s_len, 2 * BW), BF16), _sds((SUBLANES, BW), F32),
                   _sds((BH, HD, HD), F32), _sds((BH, HD, HD), F32), _sds((1, D), F32)],
        out_specs=[tile(D), tile(D), tile(2 * BW), whole(SUBLANES, BW), whole(BH, HD, HD), whole(BH, HD, HD),
                   whole(1, D)],
        scratch=[ext(), full(), full(), full(), full(), ext(), pltpu.VMEM((1, BW), F32), pltpu.VMEM((1, BW), F32)])
    return out


def _transpose_into(dst_ref, src_ref, rows):
    s_len = src_ref.shape[0]
    for r0 in range(0, s_len, rows):
        dst_ref[:, r0:r0 + rows] = src_ref[r0:r0 + rows, :].astype(F32).T.astype(BF16)


def _wgrad(a, b, jobs, *, by_rows, per, name, relay_step=0):
    s_len, m = a.shape
    n = b.shape[1]
    r, cd = (m // NDEV, n) if by_rows else (m, n // NDEV)
    nsteps = NDEV // per
    at_rows = per * r if by_rows else m

    def main(i, ins, outs, scr):
        a_ref, b_ref = ins
        q_ref, acc_ref = outs
        at_scr, stage, mine, land, send_sems, recv_sems = scr
        x, y, c = _place()

        def to_sibling(pi):
            return pltpu.make_async_remote_copy(
                src_ref=stage.at[pi & 1], dst_ref=land.at[pi], send_sem=send_sems.at[pi], recv_sem=recv_sems.at[pi],
                device_id=(x, y, 1 - c), device_id_type=MESH)

        if by_rows:
            _transpose_into(at_scr, a_ref, TRANSPOSE_ROWS)
        else:
            @pl.when(i == 0)
            def _():
                _transpose_into(at_scr, a_ref, TRANSPOSE_ROWS)

        res = _dot(at_scr[...], b_ref[...]).astype(BF16)
        for k in range(per):
            blk = per * i + k
            pi, pc = blk >> 1, blk & 1
            val = res[k * r:(k + 1) * r, :] if by_rows else res

            @pl.when(pc != c)
            def _():
                @pl.when(pi >= 2)
                def _():
                    to_sibling(pi - 2).wait_send()

                stage[pi & 1] = val
                to_sibling(pi).start()

            @pl.when(pc == c)
            def _():
                mine[pi] = val

        @pl.when(i == nsteps - 1)
        def _():
            for p in range(4):
                to_sibling(p).wait_recv()
            to_sibling(2).wait_send()
            to_sibling(3).wait_send()
            _chip_sums(mine, land, q_ref, acc_ref, x, y)

    if by_rows:
        in_specs = [pl.BlockSpec((s_len, at_rows), lambda j: (0, j)), _VMEM]
    else:
        in_specs = [_VMEM, pl.BlockSpec((s_len, cd), lambda j: (0, j))]
    blk_vmem = lambda k: pltpu.VMEM((k, r, cd), BF16)
    (q, acc), job_out = _call(
        main, jobs, name=name, grid=(nsteps,), relay_step=relay_step, ins=[a, b], in_specs=in_specs,
        out_shape=[_sds((NCHIP_OTHER, r, cd), BF16), _sds((r, cd), F32)],
        out_specs=[pl.BlockSpec((NCHIP_OTHER, r, cd), lambda j: (0, 0, 0)), pl.BlockSpec((r, cd), lambda j: (0, 0))],
        scratch=[pltpu.VMEM((at_rows, s_len), BF16), blk_vmem(2), blk_vmem(4), blk_vmem(4),
                 pltpu.SemaphoreType.DMA((4,)), pltpu.SemaphoreType.DMA((4,))])
    return q, acc, job_out


def _wgrad_cols_early(a, b, jobs, *, name, relay_step=0):
    s_len, m = a.shape
    r, cd = m, b.shape[1] // NDEV
    h = r // 2

    def chip_at(pos, base):
        return base ^ (3 - pos)

    def main(i, ins, outs, scr):
        a_ref, b_ref = ins
        q_ref, acc_ref, rel_ref = outs
        at_scr, stage, mine, land, q2_scr, send_sems, recv_sems, via_send, via_recv = scr
        x, y, c = _place()
        base = 2 * x + y
        xn, yn, _ = _other_chips(x, y)
        pos, pc = i >> 1, i & 1
        pi = chip_at(pos, base)

        def to_sibling(chip, slot):
            return pltpu.make_async_remote_copy(
                src_ref=stage.at[slot], dst_ref=land.at[chip], send_sem=send_sems.at[chip],
                recv_sem=recv_sems.at[chip], device_id=(x, y, 1 - c), device_id_type=MESH)

        def via(k):
            return pltpu.make_async_remote_copy(
                src_ref=q2_scr.at[pl.ds(k * h, h)], dst_ref=rel_ref.at[k], send_sem=via_send.at[k],
                recv_sem=via_recv.at[k], device_id=(*(xn, yn)[k], c), device_id_type=MESH)

        @pl.when(i == 0)
        def _():
            _transpose_into(at_scr, a_ref, TRANSPOSE_ROWS)

        res = _dot(at_scr[...], b_ref[...]).astype(BF16)

        @pl.when(pc != c)
        def _():
            @pl.when(pos >= 2)
            def _():
                to_sibling(chip_at(pos - 2, base), pos & 1).wait_send()

            stage[pos & 1] = res
            to_sibling(pi, pos & 1).start()

        @pl.when(pc == c)
        def _():
            mine[pi] = res

        @pl.when(i == 1)
        def _():
            dg = chip_at(0, base)
            to_sibling(dg, 0).wait_recv()
            q2 = (mine[dg].astype(F32) + land[dg].astype(F32)).astype(BF16)
            q2_scr[...] = q2
            q_ref[2] = q2
            via(0).start()
            via(1).start()

        @pl.when(i == NDEV - 1)
        def _():
            for pos_ in (1, 2, 3):
                to_sibling(chip_at(pos_, base), 0).wait_recv()
            to_sibling(chip_at(2, base), 0).wait_send()
            to_sibling(chip_at(3, base), 1).wait_send()
            for k in range(2):
                via(k).wait_recv()
            for k in range(2):
                via(k).wait_send()
            for j, chip in enumerate((base ^ 2, base ^ 1)):
                q_ref[j] = (mine[chip].astype(F32) + land[chip].astype(F32)).astype(BF16)
            acc_ref[...] = mine[base].astype(F32) + land[base].astype(F32)

    def b_block(j):
        base = 2 * lax.axis_index("x") + lax.axis_index("y")
        return (0, 2 * chip_at(j >> 1, base) + (j & 1))

    blk_vmem = lambda k: pltpu.VMEM((k, r, cd), BF16)
    (q, acc, rel), job_out = _call(
        main, jobs, name=name, grid=(NDEV,), relay_step=relay_step, ins=[a, b],
        in_specs=[_VMEM, pl.BlockSpec((s_len, cd), b_block)],
        out_shape=[_sds((NCHIP_OTHER, r, cd), BF16), _sds((r, cd), F32), _sds((2, h, cd), BF16)],
        out_specs=[pl.BlockSpec((NCHIP_OTHER, r, cd), lambda j: (0, 0, 0)), pl.BlockSpec((r, cd), lambda j: (0, 0)), _HBM],
        scratch=[pltpu.VMEM((m, s_len), BF16), blk_vmem(2), blk_vmem(4), blk_vmem(4), pltpu.VMEM((r, cd), BF16),
                 pltpu.SemaphoreType.DMA((4,)), pltpu.SemaphoreType.DMA((4,)), pltpu.SemaphoreType.DMA((2,)),
                 pltpu.SemaphoreType.DMA((2,))])
    return q, acc, rel, job_out


class _ExchangeRest:
    def __init__(self, q, relayed):
        _, r, cd = q.shape
        half = (2, r // 2, cd)
        self.ins, self.in_specs = [q, relayed], [_HBM, _HBM]
        self.out_shape, self.out_specs = [_sds((2, r, cd), q.dtype)], [_HBM]
        self.scratch = [pltpu.VMEM(half, q.dtype), pltpu.VMEM(half, q.dtype), pltpu.VMEM(half, q.dtype),
                        pltpu.SemaphoreType.DMA((4,)), pltpu.SemaphoreType.DMA((4,)), pltpu.SemaphoreType.DMA((4,))]

    def ops(self, ins, outs, scr):
        (q, rel_in), (land,) = ins, outs
        own, rel, comb, send_sems, recv_sems, local_sems = scr
        h = q.shape[1] // 2
        x, y, c = _place()
        xn, yn, _ = _other_chips(x, y)
        h0, h1 = pl.ds(0, h), pl.ds(h, h)

        def remote(k, src, dst, chip):
            return pltpu.make_async_remote_copy(src_ref=src, dst_ref=dst, send_sem=send_sems.at[k],
                                                recv_sem=recv_sems.at[k], device_id=(*chip, c), device_id_type=MESH)

        def sends():
            return [remote(0, q.at[0, h0], land.at[0, h0], xn), remote(1, q.at[1, h1], land.at[1, h1], yn),
                    remote(2, comb.at[0], land.at[1, h0], yn), remote(3, comb.at[1], land.at[0, h1], xn)]

        def loads():
            return [pltpu.make_async_copy(q.at[1, h0], own.at[0], local_sems.at[0]),
                    pltpu.make_async_copy(q.at[0, h1], own.at[1], local_sems.at[1]),
                    pltpu.make_async_copy(rel_in.at[0], rel.at[0], local_sems.at[2]),
                    pltpu.make_async_copy(rel_in.at[1], rel.at[1], local_sems.at[3])]

        def start():
            cps, lds = sends(), loads()
            for ld in lds:
                ld.start()
            cps[0].start()
            cps[1].start()
            for ld in lds:
                ld.wait()
            for k in range(2):
                comb[k] = (own[k].astype(F32) + rel[k].astype(F32)).astype(comb.dtype)
            cps[2].start()
            cps[3].start()

        def finish():
            cps = sends()
            for cp in cps:
                cp.wait_recv()
            for cp in cps:
                cp.wait_send()

        return start, lambda: None, finish


def _adam_math(w, g, m, v):
    m = B1 * m + (1.0 - B1) * g
    v = B2 * v + (1.0 - B2) * (g * g)
    m_hat = m / (1.0 - B1 ** STEP)
    v_hat = v / (1.0 - B2 ** STEP)
    delta = (-LR) * (m_hat / (jnp.sqrt(v_hat) + ADAM_EPS) + WD * w)
    return delta, m, v


def _adam_big(w, acc, land, m, v, name):
    r, cd = w.shape
    rb = ADAM_ROWS if r % ADAM_ROWS == 0 else r
    nland = land.shape[0]

    def body(w_ref, acc_ref, land_ref, m_ref, v_ref, g_ref, d_ref, mo_ref, vo_ref):
        g = acc_ref[...]
        for j in range(nland):
            g = g + land_ref[j].astype(F32)
        g_ref[...] = g
        d_ref[...], mo_ref[...], vo_ref[...] = _adam_math(w_ref[...], g, m_ref[...], v_ref[...])

    blk = pl.BlockSpec((rb, cd), lambda i: (i, 0))
    blk3 = pl.BlockSpec((nland, rb, cd), lambda i: (0, i, 0))
    return pl.pallas_call(
        body, name=name, grid=(r // rb,), in_specs=[blk, blk, blk3, blk, blk], out_specs=[blk] * 4,
        out_shape=[_sds((r, cd), F32)] * 4,
        compiler_params=_params(dimension_semantics=("arbitrary",)),
    )(w, acc, land, m, v)


def _adam_small(groups):
    n = len(groups)

    def body(*refs):
        ins, outs = refs[:4 * n], refs[4 * n:]
        for k in range(n):
            w_ref, g_ref, m_ref, v_ref = ins[4 * k:4 * k + 4]
            d, mo, vo = _adam_math(w_ref[...], g_ref[...], m_ref[...], v_ref[...])
            outs[3 * k][...] = d
            outs[3 * k + 1][...] = mo
            outs[3 * k + 2][...] = vo

    flat = [a for grp in groups for a in grp]
    shapes = [_sds(grp[0].shape, F32) for grp in groups for _ in range(3)]
    res = pl.pallas_call(
        body, name="adam_small", in_specs=[_VMEM] * (4 * n), out_specs=[_VMEM] * (3 * n), out_shape=shapes,
        compiler_params=_params(),
    )(*flat)
    return [tuple(res[3 * k:3 * k + 3]) for k in range(n)]


TM_FWD_A = 256
RELAY_STEP_FWD_A = 2
RELAY_STEP_FWD_B = 2
TM_BWD_A = 256
RELAY_STEP_BWD_A = 3
TM_BWD_A_IN = 256
RELAY_STEP_BWD_A_IN = 4
RELAY_STEP_WGRAD_A_IN = 2
TM_FWD_B = 256
TM_HEAD = 512
TM_BWD_B = 256


def _pack(parts, rows):
    flat = jnp.concatenate([p.reshape(-1) for p in parts])
    return jnp.pad(flat, (0, NDEV * rows * LANES - flat.shape[0])).reshape(NDEV, rows, LANES)


def _unpack(packed, shapes):
    flat, out, off = packed.reshape(-1), [], 0
    for s in shapes:
        size = 1
        for d in s:
            size *= d
        out.append(flat[off:off + size].reshape(s))
        off += size
    return out


def kernel(x, norm_w, a_w_in, a_ln_w, a_ln_b, a_w_s, a_b_s, a_w_out, b_w_in, b_conv_w, b_conv_b, b_gate_a_w, b_gate_a_b, b_gate_x_w, b_gate_x_b, b_lambda, b_w_out, norm_f_w, loss_target, m_norm_w, m_a_w_in, m_a_ln_w, m_a_ln_b, m_a_w_s, m_a_b_s, m_a_w_out, m_b_w_in, m_b_conv_w, m_b_conv_b, m_b_gate_a_w, m_b_gate_a_b, m_b_gate_x_w, m_b_gate_x_b, m_b_lambda, m_b_w_out, m_norm_f_w, v_norm_w, v_a_w_in, v_a_ln_w, v_a_ln_b, v_a_w_s, v_a_b_s, v_a_w_out, v_b_w_in, v_b_conv_w, v_b_conv_b, v_b_gate_a_w, v_b_gate_a_b, v_b_gate_x_w, v_b_gate_x_b, v_b_lambda, v_b_w_out, v_norm_f_w):
    me = 4 * lax.axis_index("x") + 2 * lax.axis_index("y") + lax.axis_index("c")
    xs, tgt = x[0], loss_target[0]
    nw0, nw1, nfw = norm_w[0:1], norm_w[1:2], norm_f_w.reshape(1, D)
    w_s, bst = a_w_s[0], a_b_s[0].T
    gcat = jnp.concatenate([b_gate_a_w[0], b_gate_x_w[0]], axis=-1).astype(BF16)

    p8_shard = jnp.concatenate([b_conv_w[0], b_conv_b, b_gate_a_b, b_gate_x_b, b_lambda], axis=0)
    (z, h0, ya, pp), ((win_a8, p8_all), (wout_a8, win_b8)) = _fwd_a(
        xs, nw0, a_ln_w, a_ln_b, w_s, bst,
        [_Gather([a_w_in[0], p8_shard], [BF16, F32]), _Gather([a_w_out[0], b_w_in[0]], [BF16, BF16])],
        tm=TM_FWD_A, relay_step=RELAY_STEP_FWD_A)
    p8 = jnp.transpose(p8_all, (1, 0, 2)).reshape(SUBLANES, BW)
    wout_a = wout_a8.reshape(AW, D)
    (x1, zb, hs, h1, yb, *saved_b), ((wout_b8,),) = _fwd_b(
        xs, ya, wout_a, nw1, win_b8, p8, gcat, [_Gather([b_w_out[0]], [BF16])],
        tm=TM_FWD_B, relay_step=RELAY_STEP_FWD_B)
    wout_b = wout_b8.reshape(BW, D)
    dx2, dx2b, loss, g_nfw = _head(x1, yb, wout_b, nfw, tgt, tm=TM_HEAD)

    dx1, dx1b, dzb, g_p8, g_ga, g_gx, g_nw1 = _bwd_b(dx2, zb, hs, x1, saved_b, nw1, win_b8, p8, gcat, wout_b,
                                                     tm=TM_BWD_B)
    q_wout_b, acc_wout_b, _ = _wgrad(yb, dx2b, [], by_rows=True, per=2, name="wgrad_b_out")
    shapes_b = [(1, D), (1, D), (SUBLANES, BW), (1, 1)]
    pack_b = _pack([g_nfw, g_nw1, g_p8, loss], 16)
    small_b = _InChip([g_ga.reshape(NDEV, -1, HD), g_gx.reshape(NDEV, -1, HD), pack_b])
    q_win_b, acc_win_b, (sm_b, (l_wout_b,)) = _wgrad(h1, dzb, [small_b, _Exchange([q_wout_b])], by_rows=False, per=1,
                                                      name="wgrad_b_in")
    qs_b, accs_b = sm_b[:3], sm_b[3:]

    (dz, g_lnw, g_lnb, g_ws, g_bst), (lands_b, (l_win_b,)) = _bwd_a(
        dx1b, z, pp, a_ln_w, a_ln_b, w_s, bst, wout_a, [_Exchange(qs_b), _ExchangeVia(q_win_b)],
        tm=TM_BWD_A, relay_step=RELAY_STEP_BWD_A)
    shapes_a = [(1, AW), (1, AW), (CH, G)]
    pack_a = _pack([g_lnw, g_lnb, g_bst], 8)
    q_wout_a, acc_wout_a, (red_b, sm_a) = _wgrad(
        ya, dx1b, [_SumGather(accs_b, lands_b), _InChip([g_ws, pack_a])], by_rows=True, per=2,
        name="wgrad_a_out", relay_step=1)
    qs_a, accs_a = sm_a[:2], sm_a[2:]
    q_win_a, acc_win_a, rel_a, (lands_a, (l_wout_a,)) = _wgrad_cols_early(
        h0, dz, [_Exchange(qs_a), _ExchangeVia(q_wout_a)], name="wgrad_a_in", relay_step=RELAY_STEP_WGRAD_A_IN)
    (gx, g_nw0), (red_a, (l_win_a,)) = _bwd_a_in(
        dz, dx1, xs, nw0, win_a8, [_SumGather(accs_a, lands_a), _ExchangeRest(q_win_a, rel_a)],
        tm=TM_BWD_A_IN, relay_step=RELAY_STEP_BWD_A_IN)

    r_ga, r_gx, r_pack_b = red_b
    r_nfw, r_nw1, r_p8, loss = _unpack(r_pack_b, shapes_b)
    r_ws, r_pack_a = red_a
    r_lnw, r_lnb, r_bst = _unpack(r_pack_a, shapes_a)
    g_p8 = lax.dynamic_slice_in_dim(r_p8, me * (BW // NDEV), BW // NDEV, axis=1)
    loss = loss[0, 0]

    weights = dict(norm_w=norm_w, a_w_in=a_w_in, a_ln_w=a_ln_w, a_ln_b=a_ln_b, a_w_s=a_w_s, a_b_s=a_b_s, a_w_out=a_w_out,
                   b_w_in=b_w_in, b_conv_w=b_conv_w, b_conv_b=b_conv_b, b_gate_a_w=b_gate_a_w, b_gate_a_b=b_gate_a_b,
                   b_gate_x_w=b_gate_x_w, b_gate_x_b=b_gate_x_b, b_lambda=b_lambda, b_w_out=b_w_out, norm_f_w=norm_f_w)
    mom1 = dict(norm_w=m_norm_w, a_w_in=m_a_w_in, a_ln_w=m_a_ln_w, a_ln_b=m_a_ln_b, a_w_s=m_a_w_s, a_b_s=m_a_b_s,
                a_w_out=m_a_w_out, b_w_in=m_b_w_in, b_conv_w=m_b_conv_w, b_conv_b=m_b_conv_b, b_gate_a_w=m_b_gate_a_w,
                b_gate_a_b=m_b_gate_a_b, b_gate_x_w=m_b_gate_x_w, b_gate_x_b=m_b_gate_x_b, b_lambda=m_b_lambda,
                b_w_out=m_b_w_out, norm_f_w=m_norm_f_w)
    mom2 = dict(norm_w=v_norm_w, a_w_in=v_a_w_in, a_ln_w=v_a_ln_w, a_ln_b=v_a_ln_b, a_w_s=v_a_w_s, a_b_s=v_a_b_s,
                a_w_out=v_a_w_out, b_w_in=v_b_w_in, b_conv_w=v_b_conv_w, b_conv_b=v_b_conv_b, b_gate_a_w=v_b_gate_a_w,
                b_gate_a_b=v_b_gate_a_b, b_gate_x_w=v_b_gate_x_w, b_gate_x_b=v_b_gate_x_b, b_lambda=v_b_lambda,
                b_w_out=v_b_w_out, norm_f_w=v_norm_f_w)
    names = list(weights)

    def as2d(a):
        return a.reshape(-1, a.shape[-1])

    upd, grads = {}, {}
    for k, acc, land in (("a_w_in", acc_win_a, l_win_a), ("a_w_out", acc_wout_a, l_wout_a),
                         ("b_w_in", acc_win_b, l_win_b), ("b_w_out", acc_wout_b, l_wout_b)):
        g, d, mo, vo = _adam_big(as2d(weights[k]), acc, land, as2d(mom1[k]), as2d(mom2[k]), "adam_" + k)
        grads[k] = g[None]
        upd[k] = (d, mo, vo)
    grads.update(
        norm_w=jnp.concatenate([g_nw0, r_nw1], axis=0), a_ln_w=r_lnw, a_ln_b=r_lnb,
        a_w_s=r_ws.reshape(1, G, CH, CH), a_b_s=r_bst.T[None],
        b_conv_w=g_p8[None, 0:4], b_conv_b=g_p8[4:5], b_gate_a_w=r_ga.reshape(1, BH, HD, HD), b_gate_a_b=g_p8[5:6],
        b_gate_x_w=r_gx.reshape(1, BH, HD, HD), b_gate_x_b=g_p8[6:7], b_lambda=g_p8[7:8], norm_f_w=r_nfw.reshape(D))
    small_names = [k for k in names if k not in upd]
    res = _adam_small([(as2d(weights[k]), as2d(grads[k]), as2d(mom1[k]), as2d(mom2[k])) for k in small_names])
    for k, r3 in zip(small_names, res):
        upd[k] = r3
    deltas = [upd[k][0].reshape(weights[k].shape) for k in names]
    new_m = [upd[k][1].reshape(weights[k].shape) for k in names]
    new_v = [upd[k][2].reshape(weights[k].shape) for k in names]
    return (loss, gx[None], *[grads[k] for k in names], *deltas, *new_m, *new_v)
```

```python
import jax
import jax.numpy as jnp
from jax import lax
from jax.experimental import pallas as pl
from jax.experimental.pallas import tpu as pltpu

F32 = jnp.float32
BF16 = jnp.bfloat16
MESH = pl.DeviceIdType.MESH

NDEV = 8
NCHIP_OTHER = 3
D = 1024
AW = 2048
G = 8
GD = AW // G
CH = 128
BW = 1536
BH = 12
HD = BW // BH
CA = 3 * AW // NDEV
CB = 2 * BW // NDEV
RMS_EPS = 1e-6
LN_EPS = 1e-5
RG_C = 8.0
LR, B1, B2, ADAM_EPS, WD, STEP = 0.001, 0.9, 0.999, 1e-08, 0.01, 10
V7X_VMEM_BYTES = 64 * 1024 * 1024
VMEM_LIMIT = V7X_VMEM_BYTES - 8 * 1024 * 1024
SUBLANES = 8
LANES = 128
BF16_ROWS = 16
TRANSPOSE_ROWS = 256
ADAM_ROWS = 512
GELU_C = 0.7978845608028654
GELU_K = 0.044715

_VMEM = pl.BlockSpec(memory_space=pltpu.VMEM)
_HBM = pl.BlockSpec(memory_space=pltpu.HBM)


def _sds(shape, dtype):
    return jax.ShapeDtypeStruct(tuple(shape), dtype)


def _params(**kw):
    return pltpu.CompilerParams(vmem_limit_bytes=VMEM_LIMIT, **kw)


def _gelu_t(z):
    p = 0.5 * jnp.tanh(z * (GELU_C + (GELU_C * GELU_K) * (z * z))) + 0.5
    return z * p, p


def _dgelu(z, p):
    return p * (1.0 + (z * (1.0 - p)) * (2.0 * GELU_C + (6.0 * GELU_C * GELU_K) * (z * z)))


def _sigmoid(v):
    return 0.5 * jnp.tanh(0.5 * v) + 0.5


def _softplus_neg(lam):
    return jnp.maximum(-lam, 0.0) + jnp.log1p(jnp.exp(-jnp.abs(lam)))


def _dot(a, b):
    return jnp.dot(a, b, preferred_element_type=F32)


def _dot_nt(a, b):
    return lax.dot_general(a, b, (((1,), (1,)), ((), ())), preferred_element_type=F32)


def _rowsum(v):
    return jnp.sum(v, axis=0, keepdims=True)


def _causal_mask():
    r = lax.broadcasted_iota(jnp.int32, (CH, CH), 0)
    c = lax.broadcasted_iota(jnp.int32, (CH, CH), 1)
    return r >= c


def _rms(x):
    return lax.rsqrt(jnp.mean(x * x, axis=-1, keepdims=True) + RMS_EPS)


def _rms_bwd(dh, x, r, nw):
    gy = dh * nw
    return r * gy - x * (r * r * r) * jnp.mean(gy * x, axis=-1, keepdims=True)


def _place():
    return lax.axis_index("x"), lax.axis_index("y"), lax.axis_index("c")


def _other_chips(x, y):
    return [(1 - x, y), (x, 1 - y), (1 - x, 1 - y)]


GATHER_SLOTS = 10


def _gather_ops(ins, outs, send_sems, recv_sems, local_sems):
    n = len(ins)
    x, y, c = _place()
    sibling = (x, y, 1 - c)
    xn, yn, dg = _other_chips(x, y)
    split = [ins[i].shape[0] % (2 * BF16_ROWS) == 0 for i in range(n)]

    def blk(chip, core):
        return 4 * chip[0] + 2 * chip[1] + core

    me = blk((x, y), c)

    def part(ref, i, half):
        if half is None:
            return ref
        h = ins[i].shape[0] // 2
        return ref.at[pl.ds(half * h, h)]

    def copy(i, k, block, to, half=None, src=None):
        dst = part(outs[i].at[block], i, half)
        return pltpu.make_async_remote_copy(
            src_ref=dst if src is None else part(src, i, half), dst_ref=dst,
            send_sem=send_sems.at[k, i], recv_sem=recv_sems.at[k, i], device_id=to, device_id_type=MESH)

    def first_copies():
        mine = [pltpu.make_async_copy(ins[i], outs[i].at[me], local_sems.at[i]) for i in range(n)]
        first = []
        for i in range(n):
            first.append(copy(i, 0, me, sibling, src=ins[i]))
            if split[i]:
                first.append(copy(i, 1, me, (*xn, c), 0, ins[i]))
                first.append(copy(i, 3, me, (*yn, c), 1, ins[i]))
                first.append(copy(i, 2, me, (*xn, c), 1, ins[i]))
                first.append(copy(i, 4, me, (*yn, c), 0, ins[i]))
            else:
                first.append(copy(i, 1, me, (*xn, c), None, ins[i]))
                first.append(copy(i, 3, me, (*yn, c), None, ins[i]))
                first.append(copy(i, 5, me, (*dg, c), None, ins[i]))
        return mine, first

    def onward():
        out = []
        for i in range(n):
            if split[i]:
                out.append(copy(i, 5, blk(xn, c), (*yn, c), 0))
                out.append(copy(i, 6, blk(yn, c), (*xn, c), 1))
        return out

    def start():
        mine, first = first_copies()
        for cp in mine + first:
            cp.start()

    def relay():
        sends = onward()
        for i in range(n):
            if split[i]:
                copy(i, 1, blk(xn, c), sibling, 0).wait_recv()
                sends.pop(0).start()
                copy(i, 3, blk(yn, c), sibling, 1).wait_recv()
                sends.pop(0).start()

    def finish():
        mine, first = first_copies()
        passed = []

        def pass_on(i, j, chip):
            fwd = copy(i, 7 + j, blk(chip, c), sibling)
            fwd.start()
            passed.append(fwd)

        for i in range(n):
            if split[i]:
                copy(i, 2, blk(xn, c), sibling, 1).wait_recv()
                pass_on(i, 0, xn)
                copy(i, 4, blk(yn, c), sibling, 0).wait_recv()
                pass_on(i, 1, yn)
                copy(i, 5, blk(dg, c), sibling, 0).wait_recv()
                copy(i, 6, blk(dg, c), sibling, 1).wait_recv()
                pass_on(i, 2, dg)
            else:
                copy(i, 1, blk(xn, c), sibling).wait_recv()
                pass_on(i, 0, xn)
                copy(i, 3, blk(yn, c), sibling).wait_recv()
                pass_on(i, 1, yn)
                copy(i, 5, blk(dg, c), sibling).wait_recv()
                pass_on(i, 2, dg)
        for i in range(n):
            copy(i, 0, blk((x, y), 1 - c), sibling).wait_recv()
            for j, chip in enumerate((xn, yn, dg)):
                copy(i, 7 + j, blk(chip, 1 - c), sibling).wait_recv()
        for cp in first + passed + onward():
            cp.wait_send()
        for cp in mine:
            cp.wait()

    return start, relay, finish


def _gather_sems(n):
    return [pltpu.SemaphoreType.DMA((GATHER_SLOTS, n)), pltpu.SemaphoreType.DMA((GATHER_SLOTS, n)),
            pltpu.SemaphoreType.DMA((n,))]


class _Gather:
    def __init__(self, shards, as_dtypes=None):
        n = len(shards)
        dts = [s.dtype for s in shards] if as_dtypes is None else list(as_dtypes)
        self.cast = [jnp.dtype(d) != s.dtype for d, s in zip(dts, shards)]
        self.ins = list(shards)
        self.in_specs = [_VMEM if c else _HBM for c in self.cast]
        self.out_shape = [_sds((NDEV,) + s.shape, d) for s, d in zip(shards, dts)]
        self.out_specs = [_HBM] * n
        self.scratch = [pltpu.VMEM(s.shape, d) for s, d, c in zip(shards, dts, self.cast) if c] + _gather_sems(n)

    def ops(self, ins, outs, scr):
        ncast = sum(self.cast)
        staged = iter(scr[:ncast])
        srcs = [next(staged) if c else ref for c, ref in zip(self.cast, ins)]
        start, relay, finish = _gather_ops(srcs, outs, *scr[ncast:])

        def cast_and_start():
            for c, ref, src in zip(self.cast, ins, srcs):
                if c:
                    src[...] = ref[...].astype(src.dtype)
            start()

        return cast_and_start, relay, finish


class _Exchange:
    def __init__(self, qs):
        n = len(qs)
        self.ins, self.in_specs = list(qs), [_HBM] * n
        self.out_shape = [_sds(q.shape, q.dtype) for q in qs]
        self.out_specs = [_HBM] * n
        self.scratch = [pltpu.SemaphoreType.DMA((NCHIP_OTHER, n)), pltpu.SemaphoreType.DMA((NCHIP_OTHER, n))]

    def ops(self, ins, outs, scr):
        send_sems, recv_sems = scr
        n = len(ins)
        x, y, c = _place()
        chips = _other_chips(x, y)

        def copies():
            return [pltpu.make_async_remote_copy(
                src_ref=ins[i].at[j], dst_ref=outs[i].at[j], send_sem=send_sems.at[j, i],
                recv_sem=recv_sems.at[j, i], device_id=(*chips[j], c), device_id_type=MESH)
                for i in range(n) for j in range(NCHIP_OTHER)]

        def start():
            for cp in copies():
                cp.start()

        def finish():
            cps = copies()
            for cp in cps:
                cp.wait_recv()
            for cp in cps:
                cp.wait_send()

        return start, lambda: None, finish


class _ExchangeVia:
    def __init__(self, q):
        _, r, cd = q.shape
        half = (2, r // 2, cd)
        self.ins, self.in_specs = [q], [_HBM]
        self.out_shape, self.out_specs = [_sds((2, r, cd), q.dtype)], [_HBM]
        self.scratch = [pltpu.VMEM(half, q.dtype), pltpu.VMEM(half, q.dtype), pltpu.VMEM(half, q.dtype),
                        pltpu.SemaphoreType.DMA((6,)), pltpu.SemaphoreType.DMA((6,)), pltpu.SemaphoreType.DMA((2,))]

    def ops(self, ins, outs, scr):
        (q,), (land,) = ins, outs
        relayed, own, comb, send_sems, recv_sems, local_sems = scr
        h = q.shape[1] // 2
        x, y, c = _place()
        xn, yn, _ = _other_chips(x, y)
        h0, h1 = pl.ds(0, h), pl.ds(h, h)

        def remote(k, src, dst, chip):
            return pltpu.make_async_remote_copy(src_ref=src, dst_ref=dst, send_sem=send_sems.at[k],
                                                recv_sem=recv_sems.at[k], device_id=(*chip, c), device_id_type=MESH)

        def via():
            return [remote(2, q.at[2, h0], relayed.at[0], xn), remote(3, q.at[2, h1], relayed.at[1], yn)]

        def direct():
            return [remote(0, q.at[0, h0], land.at[0, h0], xn), remote(1, q.at[1, h1], land.at[1, h1], yn)]

        def second():
            return [remote(4, comb.at[0], land.at[1, h0], yn), remote(5, comb.at[1], land.at[0, h1], xn)]

        def mine():
            return [pltpu.make_async_copy(q.at[1, h0], own.at[0], local_sems.at[0]),
                    pltpu.make_async_copy(q.at[0, h1], own.at[1], local_sems.at[1])]

        def start():
            for cp in via() + direct() + mine():
                cp.start()

        def relay():
            arrived, loaded, onward = via(), mine(), second()
            for k in range(2):
                arrived[k].wait_recv()
                loaded[k].wait()
                comb[k] = (own[k].astype(F32) + relayed[k].astype(F32)).astype(comb.dtype)
                onward[k].start()

        def finish():
            landing = direct() + second()
            for cp in landing:
                cp.wait_recv()
            for cp in via() + landing:
                cp.wait_send()

        return start, relay, finish


class _SumGather:
    def __init__(self, accs, lands):
        n = len(accs)
        self.n = n
        self.ins, self.in_specs = list(accs) + list(lands), [_VMEM] * (2 * n)
        self.out_shape = [_sds((NDEV,) + a.shape, a.dtype) for a in accs]
        self.out_specs = [_HBM] * n
        self.scratch = [pltpu.VMEM(a.shape, a.dtype) for a in accs] + _gather_sems(n)

    def ops(self, ins, outs, scr):
        n = self.n
        accs, lands, mine = ins[:n], ins[n:], scr[:n]
        g_start, relay, finish = _gather_ops(mine, outs, *scr[n:])

        def start():
            for i in range(n):
                mine[i][...] = accs[i][...] + lands[i][0] + lands[i][1] + lands[i][2]
            g_start()

        return start, relay, finish


def _call(main, jobs, *, name, grid, ins, in_specs, out_shape, out_specs, scratch, relay_step=0, first=0,
          prologue=None):
    nsteps = grid[0] if grid else 1
    n_in, n_out, n_scr = len(ins), len(out_shape), len(scratch)

    def body(*refs):
        pos = [0]

        def take(k):
            r = refs[pos[0]:pos[0] + k]
            pos[0] += k
            return r

        m_in = take(n_in)
        j_in = [take(len(j.ins)) for j in jobs]
        m_out = take(n_out)
        j_out = [take(len(j.out_shape)) for j in jobs]
        m_scr = take(n_scr)
        j_scr = [take(len(j.scratch)) for j in jobs]
        ops = [j.ops(a, b, s) for j, a, b, s in zip(jobs, j_in, j_out, j_scr)]
        i = pl.program_id(0) if grid else 0
        if not grid:
            for o in ops:
                o[0]()
            main(i, m_in, m_out, m_scr)
            for o in ops:
                o[1]()
            for o in ops:
                o[2]()
            return

        if ops:
            @pl.when(i == 0)
            def _():
                for o in ops[:first]:
                    o[0]()
                for o in ops[:first]:
                    o[1]()
                for o in ops[first:]:
                    o[0]()
                for o in ops[:first]:
                    o[2]()
                if prologue is not None:
                    prologue(j_out[:first], m_scr)

        main(i, m_in, m_out, m_scr)

        if ops[first:]:
            @pl.when(i == min(relay_step, nsteps - 1))
            def _():
                for o in ops[first:]:
                    o[1]()

            @pl.when(i == nsteps - 1)
            def _():
                for o in ops[first:]:
                    o[2]()

    extra = dict(dimension_semantics=("arbitrary",)) if grid else {}
    res = pl.pallas_call(
        body, name=name, grid=grid,
        in_specs=list(in_specs) + [s for j in jobs for s in j.in_specs],
        out_specs=list(out_specs) + [s for j in jobs for s in j.out_specs],
        out_shape=list(out_shape) + [s for j in jobs for s in j.out_shape],
        scratch_shapes=list(scratch) + [s for j in jobs for s in j.scratch],
        compiler_params=_params(**extra),
    )(*ins, *[a for j in jobs for a in j.ins])
    main_out, rest, job_out = res[:n_out], res[n_out:], []
    for j in jobs:
        k = len(j.out_shape)
        job_out.append(rest[:k])
        rest = rest[k:]
    return main_out, job_out


def _comm_only(jobs, name):
    _, job_out = _call(lambda i, a, b, s: None, jobs, name=name, grid=(), ins=[], in_specs=[], out_shape=[],
                       out_specs=[], scratch=[])
    return job_out


class _InChip:
    def __init__(self, ps):
        n = len(ps)
        self.n = n
        blk = [p.shape[1:] for p in ps]
        self.ins, self.in_specs = list(ps), [_HBM] * n
        self.out_shape = [_sds((NCHIP_OTHER,) + b, p.dtype) for b, p in zip(blk, ps)] + [_sds(b, F32) for b in blk]
        self.out_specs = [_VMEM] * (2 * n)
        self.scratch = ([pltpu.VMEM((4,) + b, p.dtype) for b, p in zip(blk, ps)] * 2
                        + [pltpu.SemaphoreType.DMA((4, n))] * 3)

    def ops(self, ins, outs, scr):
        n = self.n
        q_refs, acc_refs = outs[:n], outs[n:]
        mines, lands = scr[:n], scr[n:2 * n]
        send_sems, recv_sems, local_sems = scr[2 * n:]
        x, y, c = _place()
        sibling = (x, y, 1 - c)

        def copies():
            out = []
            for i in range(n):
                for pi in range(4):
                    loc = pltpu.make_async_copy(ins[i].at[2 * pi + c], mines[i].at[pi], local_sems.at[pi, i])
                    cp = pltpu.make_async_remote_copy(
                        src_ref=ins[i].at[2 * pi + (1 - c)], dst_ref=lands[i].at[pi],
                        send_sem=send_sems.at[pi, i], recv_sem=recv_sems.at[pi, i],
                        device_id=sibling, device_id_type=MESH)
                    out.append((loc, cp))
            return out

        def start():
            for loc, cp in copies():
                loc.start()
                cp.start()

        def finish():
            pairs = copies()
            for loc, cp in pairs:
                loc.wait()
                cp.wait_recv()
            for i in range(n):
                _chip_sums(mines[i], lands[i], q_refs[i], acc_refs[i], x, y)
            for _, cp in pairs:
                cp.wait_send()

        return start, lambda: None, finish


def _chip_sums(mine, land, q_ref, acc_ref, x, y):
    for j, (qx, qy) in enumerate(_other_chips(x, y)):
        qi = 2 * qx + qy
        q_ref[j] = (mine[qi].astype(F32) + land[qi].astype(F32)).astype(q_ref.dtype)
    mi = 2 * x + y
    acc_ref[...] = mine[mi].astype(F32) + land[mi].astype(F32)


def _direct_sum(v, buf, send_sems, recv_sems):
    x, y, c = _place()
    me = 4 * x + 2 * y + c
    buf[me] = v
    cps = []
    for k in range(1, NDEV):
        fx, fy, fc = (k >> 2) & 1, (k >> 1) & 1, k & 1
        peer = ((1 - x) if fx else x, (1 - y) if fy else y, (1 - c) if fc else c)
        cps.append((peer, pltpu.make_async_remote_copy(
            src_ref=buf.at[me], dst_ref=buf.at[me], send_sem=send_sems.at[k - 1], recv_sem=recv_sems.at[k - 1],
            device_id=peer, device_id_type=MESH)))
    for _, cp in cps:
        cp.start()
    for k, (peer, _) in enumerate(cps):
        theirs = 4 * peer[0] + 2 * peer[1] + peer[2]
        pltpu.make_async_remote_copy(
            src_ref=buf.at[theirs], dst_ref=buf.at[theirs], send_sem=send_sems.at[k], recv_sem=recv_sems.at[k],
            device_id=peer, device_id_type=MESH).wait_recv()
    acc = buf[0]
    for j in range(1, NDEV):
        acc = acc + buf[j]
    for _, cp in cps:
        cp.wait_send()
    return acc


def _direct_sum_scratch(shape, dtype):
    return [pltpu.VMEM((NDEV,) + tuple(shape), dtype), pltpu.SemaphoreType.DMA((NDEV - 1,)),
            pltpu.SemaphoreType.DMA((NDEV - 1,))]


def _fwd_a(x, nw, lnw, lnb, ws, bst, jobs, *, tm, relay_step):
    s_len = x.shape[0]
    nt = s_len // tm
    nch = tm // CH

    def main(i, ins, outs, scr):
        x_ref, nw_ref, lnw_ref, lnb_ref, ws_ref, bst_ref = ins
        z_ref, h_ref, y_ref, pp_ref = outs
        wc_scr, gv_scr, win_ref = scr

        @pl.when(i == 0)
        def _():
            m = _causal_mask()
            for g in range(G):
                wc_scr[g] = jnp.where(m, ws_ref[g], 0.0).astype(BF16)

        x = x_ref[...]
        h = (x * _rms(x) * nw_ref[...]).astype(BF16)
        h_ref[...] = h
        for k in range(NDEV):
            z_ref[:, k * CA:(k + 1) * CA] = _dot(h, win_ref[k])

        ssum = jnp.zeros((tm, 1), F32)
        for g in range(G):
            vs = slice(AW + g * GD, AW + (g + 1) * GD)
            gv, pv = _gelu_t(z_ref[:, vs])
            pp_ref[:, vs] = pv.astype(BF16)
            gv_scr[:, g * GD:(g + 1) * GD] = gv
            ssum = ssum + jnp.sum(gv, axis=-1, keepdims=True)
        mu = ssum * (1.0 / AW)
        vsum = jnp.zeros((tm, 1), F32)
        for g in range(G):
            dlt = gv_scr[:, g * GD:(g + 1) * GD] - mu
            vsum = vsum + jnp.sum(dlt * dlt, axis=-1, keepdims=True)
        rstd = lax.rsqrt(vsum * (1.0 / AW) + LN_EPS)

        for g in range(G):
            cs = slice(g * GD, (g + 1) * GD)
            gs = slice(2 * AW + g * GD, 2 * AW + (g + 1) * GD)
            v = (gv_scr[:, cs] - mu) * rstd * lnw_ref[:, cs] + lnb_ref[:, cs]
            vb = v.astype(BF16)
            u, pu = _gelu_t(z_ref[:, cs])
            pp_ref[:, cs] = pu.astype(BF16)
            zg = z_ref[:, gs]
            sig = _sigmoid(zg)
            pp_ref[:, gs] = sig.astype(BF16)
            sg = zg * sig
            for n in range(nch):
                rs = slice(n * CH, (n + 1) * CH)
                s = _dot(wc_scr[g], vb[rs, :]) + bst_ref[:, g:g + 1]
                y_ref[rs, cs] = (u[rs, :] * s * sg[rs, :]).astype(BF16)

    tile = lambda w: pl.BlockSpec((tm, w), lambda i: (i, 0))
    return _call(
        main, jobs, name="fwd_a", grid=(nt,), relay_step=relay_step, first=1,
        prologue=lambda gathered, scr: pltpu.sync_copy(gathered[0][0], scr[2]),
        ins=[x, nw, lnw, lnb, ws, bst], in_specs=[tile(D), _VMEM, _VMEM, _VMEM, _VMEM, _VMEM],
        out_shape=[_sds((s_len, 3 * AW), F32), _sds((s_len, D), BF16), _sds((s_len, AW), BF16),
                   _sds((s_len, 3 * AW), BF16)],
        out_specs=[tile(3 * AW), tile(D), tile(AW), tile(3 * AW)],
        scratch=[pltpu.VMEM((G, CH, CH), BF16), pltpu.VMEM((tm, AW), F32), pltpu.VMEM((NDEV, D, CA), BF16)])


def _bwd_a(dx1, z, pp, lnw, lnb, ws, bst, wout, jobs, *, tm, relay_step):
    s_len = dx1.shape[0]
    nt = s_len // tm
    nch = tm // CH

    def main(i, ins, outs, scr):
        dx1_ref, z_ref, pp_ref, lnw_ref, lnb_ref, ws_ref, bst_ref, wout_ref = ins
        dz_ref, glnw_ref, glnb_ref, gws_ref, gbst_ref = outs
        wc_scr, wct_scr, vh_scr, dgv_scr, dy_scr, dv_scr, gbs_acc, gwc_acc = scr

        @pl.when(i == 0)
        def _():
            m = _causal_mask()
            for g in range(G):
                wm = jnp.where(m, ws_ref[g], 0.0)
                wc_scr[g] = wm.astype(BF16)
                wct_scr[g] = wm.T.astype(BF16)
            glnw_ref[...] = jnp.zeros_like(glnw_ref)
            glnb_ref[...] = jnp.zeros_like(glnb_ref)
            gbs_acc[...] = jnp.zeros_like(gbs_acc)
            gwc_acc[...] = jnp.zeros_like(gwc_acc)

        dy_scr[...] = _dot_nt(dx1_ref[...], wout_ref[...])

        ssum = jnp.zeros((tm, 1), F32)
        for g in range(G):
            cs = slice(g * GD, (g + 1) * GD)
            vs = slice(AW + g * GD, AW + (g + 1) * GD)
            zv = z_ref[:, vs]
            pv = pp_ref[:, vs].astype(F32)
            gv = zv * pv
            vh_scr[:, cs] = gv
            dgv_scr[:, cs] = _dgelu(zv, pv)
            ssum = ssum + jnp.sum(gv, axis=-1, keepdims=True)
        mu = ssum * (1.0 / AW)
        vsum = jnp.zeros((tm, 1), F32)
        for g in range(G):
            dlt = vh_scr[:, g * GD:(g + 1) * GD] - mu
            vsum = vsum + jnp.sum(dlt * dlt, axis=-1, keepdims=True)
        rstd = lax.rsqrt(vsum * (1.0 / AW) + LN_EPS)

        m1 = jnp.zeros((tm, 1), F32)
        m2 = jnp.zeros((tm, 1), F32)
        for g in range(G):
            cs = slice(g * GD, (g + 1) * GD)
            gs = slice(2 * AW + g * GD, 2 * AW + (g + 1) * GD)
            vhat = (vh_scr[:, cs] - mu) * rstd
            vh_scr[:, cs] = vhat
            vb = (vhat * lnw_ref[:, cs] + lnb_ref[:, cs]).astype(BF16)
            zu = z_ref[:, cs]
            tu = pp_ref[:, cs].astype(F32)
            u = zu * tu
            zg = z_ref[:, gs]
            sig = pp_ref[:, gs].astype(F32)
            sg = zg * sig
            dy = dy_scr[:, cs]
            dsf = dy * u * sg
            dsb = dsf.astype(BF16)
            dvs = []
            for n in range(nch):
                rs = slice(n * CH, (n + 1) * CH)
                s = _dot(wc_scr[g], vb[rs, :]) + bst_ref[:, g:g + 1]
                dys = dy[rs, :] * s
                dz_ref[rs, cs] = (dys * sg[rs, :] * _dgelu(zu[rs, :], tu[rs, :])).astype(BF16)
                dz_ref[rs, gs] = (dys * u[rs, :] * (sig[rs, :] * (1.0 + zg[rs, :] * (1.0 - sig[rs, :])))).astype(BF16)
                gbs_acc[g] += dsf[rs, :]
                gwc_acc[g] += _dot_nt(dsb[rs, :], vb[rs, :])
                dvs.append(_dot(wct_scr[g], dsb[rs, :]))
            dv = jnp.concatenate(dvs, axis=0) if nch > 1 else dvs[0]
            glnw_ref[:, cs] += _rowsum(dv * vhat)
            glnb_ref[:, cs] += _rowsum(dv)
            dvh = dv * lnw_ref[:, cs]
            dv_scr[:, cs] = dvh
            m1 = m1 + jnp.sum(dvh, axis=-1, keepdims=True)
            m2 = m2 + jnp.sum(dvh * vhat, axis=-1, keepdims=True)
        m1 = m1 * (1.0 / AW)
        m2 = m2 * (1.0 / AW)
        for g in range(G):
            cs = slice(g * GD, (g + 1) * GD)
            dgv = rstd * (dv_scr[:, cs] - m1 - vh_scr[:, cs] * m2)
            dz_ref[:, AW + g * GD:AW + (g + 1) * GD] = (dgv * dgv_scr[:, cs]).astype(BF16)

        @pl.when(i == nt - 1)
        def _():
            m = _causal_mask()
            for g in range(G):
                gws_ref[g] = jnp.where(m, gwc_acc[g], 0.0)
                gbst_ref[:, g:g + 1] = jnp.sum(gbs_acc[g], axis=-1, keepdims=True)

    tile = lambda w: pl.BlockSpec((tm, w), lambda i: (i, 0))
    whole = lambda *s: pl.BlockSpec(s, lambda i: (0,) * len(s))
    big = lambda dt: pltpu.VMEM((tm, AW), dt)
    return _call(
        main, jobs, name="bwd_a", grid=(nt,), relay_step=relay_step,
        ins=[dx1, z, pp, lnw, lnb, ws, bst, wout],
        in_specs=[tile(D), tile(3 * AW), tile(3 * AW), _VMEM, _VMEM, _VMEM, _VMEM, _VMEM],
        out_shape=[_sds((s_len, 3 * AW), BF16), _sds((1, AW), F32), _sds((1, AW), F32), _sds((G, CH, CH), F32),
                   _sds((CH, G), F32)],
        out_specs=[tile(3 * AW), whole(1, AW), whole(1, AW), whole(G, CH, CH), whole(CH, G)],
        scratch=[pltpu.VMEM((G, CH, CH), BF16), pltpu.VMEM((G, CH, CH), BF16), big(F32), big(F32), big(F32), big(F32),
                 pltpu.VMEM((G, CH, GD), F32), pltpu.VMEM((G, CH, CH), F32)])


def _bwd_a_in(dz, dx1, x, nw, win8, jobs, *, tm, relay_step):
    s_len = x.shape[0]
    nt = s_len // tm

    def main(i, ins, outs, scr):
        dz_ref, dx1_ref, x_ref, nw_ref, win_ref = ins
        gx_ref, gnw_ref = outs

        @pl.when(i == 0)
        def _():
            gnw_ref[...] = jnp.zeros_like(gnw_ref)

        dh = jnp.zeros((tm, D), F32)
        for k in range(NDEV):
            dh = dh + _dot_nt(dz_ref[:, k * CA:(k + 1) * CA], win_ref[k])
        x = x_ref[...]
        r = _rms(x)
        gx_ref[...] = dx1_ref[...] + _rms_bwd(dh, x, r, nw_ref[...])
        gnw_ref[...] += _rowsum(dh * x * r)

        @pl.when(i == nt - 1)
        def _():
            gnw_ref[...] = _direct_sum(gnw_ref[...], *scr)

    tile = lambda w: pl.BlockSpec((tm, w), lambda i: (i, 0))
    return _call(
        main, jobs, name="bwd_a_in", grid=(nt,), relay_step=relay_step,
        ins=[dz, dx1, x, nw, win8], in_specs=[tile(3 * AW), tile(D), tile(D), _VMEM, _VMEM],
        out_shape=[_sds((s_len, D), F32), _sds((1, D), F32)],
        out_specs=[tile(D), pl.BlockSpec((1, D), lambda i: (0, 0))], scratch=_direct_sum_scratch((1, D), F32))


def _conv(p8_ref, cs, xb, xm1, xm2, xm3):
    xc = p8_ref[4:5, cs] + p8_ref[3:4, cs] * xb
    xc = xc + p8_ref[0:1, cs] * xm3
    xc = xc + p8_ref[1:2, cs] * xm2
    return xc + p8_ref[2:3, cs] * xm1


def _gates(p8_ref, gcat_ref, hh, xc):
    cs = slice(hh * HD, (hh + 1) * HD)
    pre = _dot(xc.astype(BF16), gcat_ref[hh])
    r = _sigmoid(pre[:, :HD] + p8_ref[5:6, cs])
    ig = _sigmoid(pre[:, HD:] + p8_ref[6:7, cs])
    sp = _softplus_neg(p8_ref[7:8, cs])
    la = (-RG_C) * r * sp
    a = jnp.exp(la)
    half_log = 0.5 * jnp.log(jnp.tanh(-la) * (1.0 + a * a))
    return r, ig, sp, a, jnp.exp(half_log), jnp.exp(-half_log)


def _scan_rows(a_ref, b_ref, out_ref, carry, tm, reverse):
    row = lax.broadcasted_iota(jnp.int32, (SUBLANES, BW), 0)
    ngrp = tm // SUBLANES

    def step(j, cr):
        jj = (ngrp - 1 - j) if reverse else j
        off = pl.multiple_of(jj * SUBLANES, SUBLANES)
        a = a_ref[pl.ds(off, SUBLANES), :]
        b = b_ref[pl.ds(off, SUBLANES), :]
        for sh in (1, 2, 4):
            if reverse:
                a_s = pltpu.roll(a, SUBLANES - sh, 0)
                b_s = pltpu.roll(b, SUBLANES - sh, 0)
                m = row < SUBLANES - sh
            else:
                a_s = pltpu.roll(a, sh, 0)
                b_s = pltpu.roll(b, sh, 0)
                m = row >= sh
            b = jnp.where(m, a * b_s + b, b)
            a = jnp.where(m, a * a_s, a)
        o = b + a * cr
        out_ref[pl.ds(off, SUBLANES), :] = o
        return o[0:1, :] if reverse else o[SUBLANES - 1:SUBLANES, :]

    return lax.fori_loop(0, ngrp, step, carry)


def _fwd_b(x, ya, wout_a, nw, win8, p8, gcat, jobs, *, tm, relay_step):
    s_len = x.shape[0]
    nt = s_len // tm

    def main(i, ins, outs, scr):
        x_ref, ya_ref, wouta_ref, nw_ref, win_ref, p8_ref, gcat_ref = ins
        x1_ref, zb_ref, hs_ref, h1_ref, yb_ref, xc_ref, a_ref, cc_ref, r_ref, ig_ref, m_ref = outs
        xbe_scr, b_scr, k_scr, carry_scr = scr

        @pl.when(i == 0)
        def _():
            xbe_scr[0:SUBLANES, :] = jnp.zeros((SUBLANES, BW), F32)
            carry_scr[...] = jnp.zeros_like(carry_scr)

        x1 = x_ref[...] + _dot(ya_ref[...], wouta_ref[...])
        x1_ref[...] = x1
        h = (x1 * _rms(x1) * nw_ref[...]).astype(BF16)
        h1_ref[...] = h
        for k in range(NDEV):
            zb_ref[:, k * CB:(k + 1) * CB] = _dot(h, win_ref[k])
        xbe_scr[SUBLANES:SUBLANES + tm, :] = zb_ref[:, :BW]
        for hh in range(BH):
            cs = slice(hh * HD, (hh + 1) * HD)
            xc = _conv(p8_ref, cs, xbe_scr[SUBLANES:SUBLANES + tm, cs], xbe_scr[7:7 + tm, cs],
                       xbe_scr[6:6 + tm, cs], xbe_scr[5:5 + tm, cs])
            r, ig, _, a, mult, rm = _gates(p8_ref, gcat_ref, hh, xc)
            ixc = ig * xc
            xc_ref[:, cs] = xc
            a_ref[:, cs] = a
            r_ref[:, cs] = r.astype(BF16)
            ig_ref[:, cs] = ig.astype(BF16)
            m_ref[:, cs] = mult.astype(BF16)
            b_scr[:, cs] = mult * ixc
            k_scr[:, cs] = ixc * (a * a * rm)
        xbe_scr[0:SUBLANES, :] = xbe_scr[tm:tm + SUBLANES, :]
        carry_scr[...] = _scan_rows(a_ref, b_scr, hs_ref, carry_scr[...], tm, False)
        for hh in range(BH):
            cs = slice(hh * HD, (hh + 1) * HD)
            gt = zb_ref[:, BW + hh * HD:BW + (hh + 1) * HD]
            hsv = hs_ref[:, cs]
            yb_ref[:, cs] = (hsv * (gt * _sigmoid(gt))).astype(BF16)
            cc_ref[:, cs] = (hsv - b_scr[:, cs]) - k_scr[:, cs]

    tile = lambda w: pl.BlockSpec((tm, w), lambda i: (i, 0))
    wide = lambda dt: _sds((s_len, BW), dt)
    return _call(
        main, jobs, name="fwd_b", grid=(nt,), relay_step=relay_step,
        ins=[x, ya, wout_a, nw, win8, p8, gcat], in_specs=[tile(D), tile(AW), _VMEM, _VMEM, _VMEM, _VMEM, _VMEM],
        out_shape=[_sds((s_len, D), F32), _sds((s_len, 2 * BW), F32), wide(F32), _sds((s_len, D), BF16), wide(BF16),
                   wide(F32), wide(F32), wide(F32), wide(BF16), wide(BF16), wide(BF16)],
        out_specs=[tile(D), tile(2 * BW), tile(BW), tile(D)] + [tile(BW)] * 7,
        scratch=[pltpu.VMEM((tm + SUBLANES, BW), F32), pltpu.VMEM((tm, BW), F32), pltpu.VMEM((tm, BW), F32),
                 pltpu.VMEM((1, BW), F32)])


def _head(x1, yb, wout, nfw, tgt, *, tm):
    s_len = x1.shape[0]

    def main(i, ins, outs, scr):
        x1_ref, yb_ref, wout_ref, nfw_ref, t_ref = ins
        dx2_ref, dx2b_ref, loss_ref, gnfw_ref = outs

        @pl.when(i == 0)
        def _():
            loss_ref[...] = jnp.zeros_like(loss_ref)
            gnfw_ref[...] = jnp.zeros_like(gnfw_ref)

        x2 = x1_ref[...] + _dot(yb_ref[...], wout_ref[...])
        rf = _rms(x2)
        xn = x2 * rf
        e = xn * nfw_ref[...] - t_ref[...]
        loss_ref[...] += (0.5 / D) * jnp.sum(jnp.sum(e * e, axis=-1, keepdims=True), axis=0, keepdims=True)
        dyf = e * (1.0 / D)
        gnfw_ref[...] += _rowsum(dyf * xn)
        dx2 = _rms_bwd(dyf, x2, rf, nfw_ref[...])
        dx2_ref[...] = dx2
        dx2b_ref[...] = dx2.astype(BF16)

    tile = lambda w: pl.BlockSpec((tm, w), lambda i: (i, 0))
    whole = lambda *s: pl.BlockSpec(s, lambda i: (0,) * len(s))
    (dx2, dx2b, loss, gnfw), _ = _call(
        main, [], name="head", grid=(s_len // tm,),
        ins=[x1, yb, wout, nfw, tgt], in_specs=[tile(D), tile(BW), _VMEM, _VMEM, tile(D)],
        out_shape=[_sds((s_len, D), F32), _sds((s_len, D), BF16), _sds((1, 1), F32), _sds((1, D), F32)],
        out_specs=[tile(D), tile(D), whole(1, 1), whole(1, D)], scratch=[])
    return dx2, dx2b, loss, gnfw


def _bwd_b(dx2, zb, hs, x1, saved, nw, win8, p8, gcat, wout, *, tm):
    s_len = x1.shape[0]
    nt = s_len // tm

    def main(i, ins, outs, scr):
        (dx2_ref, zb_ref, hs_ref, x1_ref, xc_ref, a_ref, cc_ref, r_ref, ig_ref, m_ref,
         nw_ref, win_ref, p8_ref, gcat_ref, wout_ref) = ins
        dx1_ref, dx1b_ref, dzb_ref, gp8_ref, gga_ref, ggx_ref, gnw_ref = outs
        ae_scr, an_scr, dhd_scr, dh_scr, dy_scr, dxce_scr, carry_scr, afirst_scr = scr

        @pl.when(i == 0)
        def _():
            gp8_ref[...] = jnp.zeros_like(gp8_ref)
            gga_ref[...] = jnp.zeros_like(gga_ref)
            ggx_ref[...] = jnp.zeros_like(ggx_ref)
            gnw_ref[...] = jnp.zeros_like(gnw_ref)
            dxce_scr[tm:tm + SUBLANES, :] = jnp.zeros((SUBLANES, BW), F32)
            carry_scr[...] = jnp.zeros_like(carry_scr)
            afirst_scr[...] = jnp.zeros_like(afirst_scr)

        dx2 = dx2_ref[...]
        dy_scr[...] = _dot_nt(dx2.astype(BF16), wout_ref[...])
        for hh in range(BH):
            cs = slice(hh * HD, (hh + 1) * HD)
            gs = slice(BW + hh * HD, BW + (hh + 1) * HD)
            gt = zb_ref[:, gs]
            sig = _sigmoid(gt)
            dy = dy_scr[:, cs]
            dhd_scr[:, cs] = dy * (gt * sig)
            dzb_ref[:, gs] = (dy * hs_ref[:, cs] * (sig * (1.0 + gt * (1.0 - sig)))).astype(BF16)

        ae_scr[0:tm, :] = a_ref[...]
        ae_scr[tm:tm + SUBLANES, :] = jnp.broadcast_to(afirst_scr[...], (SUBLANES, BW))
        an_scr[...] = ae_scr[1:1 + tm, :]
        afirst_scr[...] = ae_scr[0:1, :]
        carry_scr[...] = _scan_rows(an_scr, dhd_scr, dh_scr, carry_scr[...], tm, True)

        for hh in range(BH):
            cs = slice(hh * HD, (hh + 1) * HD)
            dh = dh_scr[:, cs]
            mult = m_ref[:, cs].astype(F32)
            ig = ig_ref[:, cs].astype(F32)
            r = r_ref[:, cs].astype(F32)
            xc = xc_ref[:, cs]
            lam = p8_ref[7:8, cs]
            sp = _softplus_neg(lam)
            dla = dh * cc_ref[:, cs]
            gp8_ref[7:8, cs] += _rowsum(dla * ((-RG_C) * r)) * (-_sigmoid(-lam))
            dpr = dla * ((-RG_C) * sp) * (r * (1.0 - r))
            dpi = dh * mult * xc * (ig * (1.0 - ig))
            gp8_ref[5:6, cs] += _rowsum(dpr)
            gp8_ref[6:7, cs] += _rowsum(dpi)
            dcat = jnp.concatenate([dpr, dpi], axis=1).astype(BF16)
            dxc = dh * mult * ig + _dot_nt(dcat, gcat_ref[hh])
            gg = _dot(xc.T.astype(BF16), dcat)
            gga_ref[hh] += gg[:, :HD]
            ggx_ref[hh] += gg[:, HD:]
            dxce_scr[0:tm, cs] = dxc
            gp8_ref[4:5, cs] += _rowsum(dxc)
        for hh in range(BH):
            cs = slice(hh * HD, (hh + 1) * HD)
            xb = zb_ref[:, cs]
            d0, d1 = dxce_scr[0:tm, cs], dxce_scr[1:1 + tm, cs]
            d2, d3 = dxce_scr[2:2 + tm, cs], dxce_scr[3:3 + tm, cs]
            dzb_ref[:, cs] = (p8_ref[3:4, cs] * d0 + p8_ref[2:3, cs] * d1 + p8_ref[1:2, cs] * d2
                              + p8_ref[0:1, cs] * d3).astype(BF16)
            gp8_ref[3:4, cs] += _rowsum(d0 * xb)
            gp8_ref[2:3, cs] += _rowsum(d1 * xb)
            gp8_ref[1:2, cs] += _rowsum(d2 * xb)
            gp8_ref[0:1, cs] += _rowsum(d3 * xb)
        dxce_scr[tm:tm + SUBLANES, :] = dxce_scr[0:SUBLANES, :]

        dh1 = jnp.zeros((tm, D), F32)
        for k in range(NDEV):
            dh1 = dh1 + _dot_nt(dzb_ref[:, k * CB:(k + 1) * CB], win_ref[k])
        x1 = x1_ref[...]
        r1 = _rms(x1)
        dx1 = dx2 + _rms_bwd(dh1, x1, r1, nw_ref[...])
        dx1_ref[...] = dx1
        dx1b_ref[...] = dx1.astype(BF16)
        gnw_ref[...] += _rowsum(dh1 * x1 * r1)

    tile = lambda w: pl.BlockSpec((tm, w), lambda i: (nt - 1 - i, 0))
    whole = lambda *s: pl.BlockSpec(s, lambda i: (0,) * len(s))
    full = lambda: pltpu.VMEM((tm, BW), F32)
    ext = lambda: pltpu.VMEM((tm + SUBLANES, BW), F32)
    out, _ = _call(
        main, [], name="bwd_b", grid=(nt,),
        ins=[dx2, zb, hs, x1, *saved, nw, win8, p8, gcat, wout],
        in_specs=[tile(D), tile(2 * BW), tile(BW), tile(D)] + [tile(BW)] * 6 + [_VMEM] * 5,
        out_shape=[_sds((s_len, D), F32), _sds((s_len, D), BF16), _sds((s_len, 2 * BW), BF16), _sds((SUBLANES, BW), F32),
                   _sds((BH, HD, HD), F32), _sds((BH, HD, HD), F32), _sds((1, D), F32)],
        out_specs=[tile(D), tile(D), tile(2 * BW), whole(SUBLANES, BW), whole(BH, HD, HD), whole(BH, HD, HD),
                   whole(1, D)],
        scratch=[ext(), full(), full(), full(), full(), ext(), pltpu.VMEM((1, BW), F32), pltpu.VMEM((1, BW), F32)])
    return out


def _transpose_into(dst_ref, src_ref, rows):
    s_len = src_ref.shape[0]
    for r0 in range(0, s_len, rows):
        dst_ref[:, r0:r0 + rows] = src_ref[r0:r0 + rows, :].astype(F32).T.astype(BF16)


def _wgrad(a, b, jobs, *, by_rows, per, name, relay_step=0):
    s_len, m = a.shape
    n = b.shape[1]
    r, cd = (m // NDEV, n) if by_rows else (m, n // NDEV)
    nsteps = NDEV // per
    at_rows = per * r if by_rows else m

    def main(i, ins, outs, scr):
        a_ref, b_ref = ins
        q_ref, acc_ref = outs
        at_scr, stage, mine, land, send_sems, recv_sems = scr
        x, y, c = _place()

        def to_sibling(pi):
            return pltpu.make_async_remote_copy(
                src_ref=stage.at[pi & 1], dst_ref=land.at[pi], send_sem=send_sems.at[pi], recv_sem=recv_sems.at[pi],
                device_id=(x, y, 1 - c), device_id_type=MESH)

        if by_rows:
            _transpose_into(at_scr, a_ref, TRANSPOSE_ROWS)
        else:
            @pl.when(i == 0)
            def _():
                _transpose_into(at_scr, a_ref, TRANSPOSE_ROWS)

        res = _dot(at_scr[...], b_ref[...]).astype(BF16)
        for k in range(per):
            blk = per * i + k
            pi, pc = blk >> 1, blk & 1
            val = res[k * r:(k + 1) * r, :] if by_rows else res

            @pl.when(pc != c)
            def _():
                @pl.when(pi >= 2)
                def _():
                    to_sibling(pi - 2).wait_send()

                stage[pi & 1] = val
                to_sibling(pi).start()

            @pl.when(pc == c)
            def _():
                mine[pi] = val

        @pl.when(i == nsteps - 1)
        def _():
            for p in range(4):
                to_sibling(p).wait_recv()
            to_sibling(2).wait_send()
            to_sibling(3).wait_send()
            _chip_sums(mine, land, q_ref, acc_ref, x, y)

    if by_rows:
        in_specs = [pl.BlockSpec((s_len, at_rows), lambda j: (0, j)), _VMEM]
    else:
        in_specs = [_VMEM, pl.BlockSpec((s_len, cd), lambda j: (0, j))]
    blk_vmem = lambda k: pltpu.VMEM((k, r, cd), BF16)
    (q, acc), job_out = _call(
        main, jobs, name=name, grid=(nsteps,), relay_step=relay_step, ins=[a, b], in_specs=in_specs,
        out_shape=[_sds((NCHIP_OTHER, r, cd), BF16), _sds((r, cd), F32)],
        out_specs=[pl.BlockSpec((NCHIP_OTHER, r, cd), lambda j: (0, 0, 0)), pl.BlockSpec((r, cd), lambda j: (0, 0))],
        scratch=[pltpu.VMEM((at_rows, s_len), BF16), blk_vmem(2), blk_vmem(4), blk_vmem(4),
                 pltpu.SemaphoreType.DMA((4,)), pltpu.SemaphoreType.DMA((4,))])
    return q, acc, job_out


def _wgrad_cols_early(a, b, jobs, *, name, relay_step=0):
    s_len, m = a.shape
    r, cd = m, b.shape[1] // NDEV
    h = r // 2

    def chip_at(pos, base):
        return base ^ (3 - pos)

    def main(i, ins, outs, scr):
        a_ref, b_ref = ins
        q_ref, acc_ref, rel_ref = outs
        at_scr, stage, mine, land, q2_scr, send_sems, recv_sems, via_send, via_recv = scr
        x, y, c = _place()
        base = 2 * x + y
        xn, yn, _ = _other_chips(x, y)
        pos, pc = i >> 1, i & 1
        pi = chip_at(pos, base)

        def to_sibling(chip, slot):
            return pltpu.make_async_remote_copy(
                src_ref=stage.at[slot], dst_ref=land.at[chip], send_sem=send_sems.at[chip],
                recv_sem=recv_sems.at[chip], device_id=(x, y, 1 - c), device_id_type=MESH)

        def via(k):
            return pltpu.make_async_remote_copy(
                src_ref=q2_scr.at[pl.ds(k * h, h)], dst_ref=rel_ref.at[k], send_sem=via_send.at[k],
                recv_sem=via_recv.at[k], device_id=(*(xn, yn)[k], c), device_id_type=MESH)

        @pl.when(i == 0)
        def _():
            _transpose_into(at_scr, a_ref, TRANSPOSE_ROWS)

        res = _dot(at_scr[...], b_ref[...]).astype(BF16)

        @pl.when(pc != c)
        def _():
            @pl.when(pos >= 2)
            def _():
                to_sibling(chip_at(pos - 2, base), pos & 1).wait_send()

            stage[pos & 1] = res
            to_sibling(pi, pos & 1).start()

        @pl.when(pc == c)
        def _():
            mine[pi] = res

        @pl.when(i == 1)
        def _():
            dg = chip_at(0, base)
            to_sibling(dg, 0).wait_recv()
            q2 = (mine[dg].astype(F32) + land[dg].astype(F32)).astype(BF16)
            q2_scr[...] = q2
            q_ref[2] = q2
            via(0).start()
            via(1).start()

        @pl.when(i == NDEV - 1)
        def _():
            for pos_ in (1, 2, 3):
                to_sibling(chip_at(pos_, base), 0).wait_recv()
            to_sibling(chip_at(2, base), 0).wait_send()
            to_sibling(chip_at(3, base), 1).wait_send()
            for k in range(2):
                via(k).wait_recv()
            for k in range(2):
                via(k).wait_send()
            for j, chip in enumerate((base ^ 2, base ^ 1)):
                q_ref[j] = (mine[chip].astype(F32) + land[chip].astype(F32)).astype(BF16)
            acc_ref[...] = mine[base].astype(F32) + land[base].astype(F32)

    def b_block(j):
        base = 2 * lax.axis_index("x") + lax.axis_index("y")
        return (0, 2 * chip_at(j >> 1, base) + (j & 1))

    blk_vmem = lambda k: pltpu.VMEM((k, r, cd), BF16)
    (q, acc, rel), job_out = _call(
        main, jobs, name=name, grid=(NDEV,), relay_step=relay_step, ins=[a, b],
        in_specs=[_VMEM, pl.BlockSpec((s_len, cd), b_block)],
        out_shape=[_sds((NCHIP_OTHER, r, cd), BF16), _sds((r, cd), F32), _sds((2, h, cd), BF16)],
        out_specs=[pl.BlockSpec((NCHIP_OTHER, r, cd), lambda j: (0, 0, 0)), pl.BlockSpec((r, cd), lambda j: (0, 0)), _HBM],
        scratch=[pltpu.VMEM((m, s_len), BF16), blk_vmem(2), blk_vmem(4), blk_vmem(4), pltpu.VMEM((r, cd), BF16),
                 pltpu.SemaphoreType.DMA((4,)), pltpu.SemaphoreType.DMA((4,)), pltpu.SemaphoreType.DMA((2,)),
                 pltpu.SemaphoreType.DMA((2,))])
    return q, acc, rel, job_out


class _ExchangeRest:
    def __init__(self, q, relayed):
        _, r, cd = q.shape
        half = (2, r // 2, cd)
        self.ins, self.in_specs = [q, relayed], [_HBM, _HBM]
        self.out_shape, self.out_specs = [_sds((2, r, cd), q.dtype)], [_HBM]
        self.scratch = [pltpu.VMEM(half, q.dtype), pltpu.VMEM(half, q.dtype), pltpu.VMEM(half, q.dtype),
                        pltpu.SemaphoreType.DMA((4,)), pltpu.SemaphoreType.DMA((4,)), pltpu.SemaphoreType.DMA((4,))]

    def ops(self, ins, outs, scr):
        (q, rel_in), (land,) = ins, outs
        own, rel, comb, send_sems, recv_sems, local_sems = scr
        h = q.shape[1] // 2
        x, y, c = _place()
        xn, yn, _ = _other_chips(x, y)
        h0, h1 = pl.ds(0, h), pl.ds(h, h)

        def remote(k, src, dst, chip):
            return pltpu.make_async_remote_copy(src_ref=src, dst_ref=dst, send_sem=send_sems.at[k],
                                                recv_sem=recv_sems.at[k], device_id=(*chip, c), device_id_type=MESH)

        def sends():
            return [remote(0, q.at[0, h0], land.at[0, h0], xn), remote(1, q.at[1, h1], land.at[1, h1], yn),
                    remote(2, comb.at[0], land.at[1, h0], yn), remote(3, comb.at[1], land.at[0, h1], xn)]

        def loads():
            return [pltpu.make_async_copy(q.at[1, h0], own.at[0], local_sems.at[0]),
                    pltpu.make_async_copy(q.at[0, h1], own.at[1], local_sems.at[1]),
                    pltpu.make_async_copy(rel_in.at[0], rel.at[0], local_sems.at[2]),
                    pltpu.make_async_copy(rel_in.at[1], rel.at[1], local_sems.at[3])]

        def start():
            cps, lds = sends(), loads()
            for ld in lds:
                ld.start()
            cps[0].start()
            cps[1].start()
            for ld in lds:
                ld.wait()
            for k in range(2):
                comb[k] = (own[k].astype(F32) + rel[k].astype(F32)).astype(comb.dtype)
            cps[2].start()
            cps[3].start()

        def finish():
            cps = sends()
            for cp in cps:
                cp.wait_recv()
            for cp in cps:
                cp.wait_send()

        return start, lambda: None, finish


def _adam_math(w, g, m, v):
    m = B1 * m + (1.0 - B1) * g
    v = B2 * v + (1.0 - B2) * (g * g)
    m_hat = m / (1.0 - B1 ** STEP)
    v_hat = v / (1.0 - B2 ** STEP)
    delta = (-LR) * (m_hat / (jnp.sqrt(v_hat) + ADAM_EPS) + WD * w)
    return delta, m, v


def _adam_big(w, acc, land, m, v, name):
    r, cd = w.shape
    rb = ADAM_ROWS if r % ADAM_ROWS == 0 else r
    nland = land.shape[0]

    def body(w_ref, acc_ref, land_ref, m_ref, v_ref, g_ref, d_ref, mo_ref, vo_ref):
        g = acc_ref[...]
        for j in range(nland):
            g = g + land_ref[j].astype(F32)
        g_ref[...] = g
        d_ref[...], mo_ref[...], vo_ref[...] = _adam_math(w_ref[...], g, m_ref[...], v_ref[...])

    blk = pl.BlockSpec((rb, cd), lambda i: (i, 0))
    blk3 = pl.BlockSpec((nland, rb, cd), lambda i: (0, i, 0))
    return pl.pallas_call(
        body, name=name, grid=(r // rb,), in_specs=[blk, blk, blk3, blk, blk], out_specs=[blk] * 4,
        out_shape=[_sds((r, cd), F32)] * 4,
        compiler_params=_params(dimension_semantics=("arbitrary",)),
    )(w, acc, land, m, v)


def _adam_small(groups):
    n = len(groups)

    def body(*refs):
        ins, outs = refs[:4 * n], refs[4 * n:]
        for k in range(n):
            w_ref, g_ref, m_ref, v_ref = ins[4 * k:4 * k + 4]
            d, mo, vo = _adam_math(w_ref[...], g_ref[...], m_ref[...], v_ref[...])
            outs[3 * k][...] = d
            outs[3 * k + 1][...] = mo
            outs[3 * k + 2][...] = vo

    flat = [a for grp in groups for a in grp]
    shapes = [_sds(grp[0].shape, F32) for grp in groups for _ in range(3)]
    res = pl.pallas_call(
        body, name="adam_small", in_specs=[_VMEM] * (4 * n), out_specs=[_VMEM] * (3 * n), out_shape=shapes,
        compiler_params=_params(),
    )(*flat)
    return [tuple(res[3 * k:3 * k + 3]) for k in range(n)]


TM_FWD_A = 256
RELAY_STEP_FWD_A = 1
RELAY_STEP_FWD_B = 2
TM_BWD_A = 256
RELAY_STEP_BWD_A = 3
TM_BWD_A_IN = 256
RELAY_STEP_BWD_A_IN = 4
RELAY_STEP_WGRAD_A_IN = 2
TM_FWD_B = 256
TM_HEAD = 512
TM_BWD_B = 256


def _pack(parts, rows):
    flat = jnp.concatenate([p.reshape(-1) for p in parts])
    return jnp.pad(flat, (0, NDEV * rows * LANES - flat.shape[0])).reshape(NDEV, rows, LANES)


def _unpack(packed, shapes):
    flat, out, off = packed.reshape(-1), [], 0
    for s in shapes:
        size = 1
        for d in s:
            size *= d
        out.append(flat[off:off + size].reshape(s))
        off += size
    return out


def kernel(x, norm_w, a_w_in, a_ln_w, a_ln_b, a_w_s, a_b_s, a_w_out, b_w_in, b_conv_w, b_conv_b, b_gate_a_w, b_gate_a_b, b_gate_x_w, b_gate_x_b, b_lambda, b_w_out, norm_f_w, loss_target, m_norm_w, m_a_w_in, m_a_ln_w, m_a_ln_b, m_a_w_s, m_a_b_s, m_a_w_out, m_b_w_in, m_b_conv_w, m_b_conv_b, m_b_gate_a_w, m_b_gate_a_b, m_b_gate_x_w, m_b_gate_x_b, m_b_lambda, m_b_w_out, m_norm_f_w, v_norm_w, v_a_w_in, v_a_ln_w, v_a_ln_b, v_a_w_s, v_a_b_s, v_a_w_out, v_b_w_in, v_b_conv_w, v_b_conv_b, v_b_gate_a_w, v_b_gate_a_b, v_b_gate_x_w, v_b_gate_x_b, v_b_lambda, v_b_w_out, v_norm_f_w):
    me = 4 * lax.axis_index("x") + 2 * lax.axis_index("y") + lax.axis_index("c")
    xs, tgt = x[0], loss_target[0]
    nw0, nw1, nfw = norm_w[0:1], norm_w[1:2], norm_f_w.reshape(1, D)
    w_s, bst = a_w_s[0], a_b_s[0].T
    gcat = jnp.concatenate([b_gate_a_w[0], b_gate_x_w[0]], axis=-1).astype(BF16)

    p8_shard = jnp.concatenate([b_conv_w[0], b_conv_b, b_gate_a_b, b_gate_x_b, b_lambda], axis=0)
    (z, h0, ya, pp), ((win_a8, p8_all), (wout_a8, win_b8)) = _fwd_a(
        xs, nw0, a_ln_w, a_ln_b, w_s, bst,
        [_Gather([a_w_in[0], p8_shard], [BF16, F32]), _Gather([a_w_out[0], b_w_in[0]], [BF16, BF16])],
        tm=TM_FWD_A, relay_step=RELAY_STEP_FWD_A)
    p8 = jnp.transpose(p8_all, (1, 0, 2)).reshape(SUBLANES, BW)
    wout_a = wout_a8.reshape(AW, D)
    (x1, zb, hs, h1, yb, *saved_b), ((wout_b8,),) = _fwd_b(
        xs, ya, wout_a, nw1, win_b8, p8, gcat, [_Gather([b_w_out[0]], [BF16])],
        tm=TM_FWD_B, relay_step=RELAY_STEP_FWD_B)
    wout_b = wout_b8.reshape(BW, D)
    dx2, dx2b, loss, g_nfw = _head(x1, yb, wout_b, nfw, tgt, tm=TM_HEAD)

    dx1, dx1b, dzb, g_p8, g_ga, g_gx, g_nw1 = _bwd_b(dx2, zb, hs, x1, saved_b, nw1, win_b8, p8, gcat, wout_b,
                                                     tm=TM_BWD_B)
    q_wout_b, acc_wout_b, _ = _wgrad(yb, dx2b, [], by_rows=True, per=2, name="wgrad_b_out")
    shapes_b = [(1, D), (1, D), (SUBLANES, BW), (1, 1)]
    pack_b = _pack([g_nfw, g_nw1, g_p8, loss], 16)
    small_b = _InChip([g_ga.reshape(NDEV, -1, HD), g_gx.reshape(NDEV, -1, HD), pack_b])
    q_win_b, acc_win_b, (sm_b, (l_wout_b,)) = _wgrad(h1, dzb, [small_b, _Exchange([q_wout_b])], by_rows=False, per=1,
                                                      name="wgrad_b_in")
    qs_b, accs_b = sm_b[:3], sm_b[3:]

    (dz, g_lnw, g_lnb, g_ws, g_bst), (lands_b, (l_win_b,)) = _bwd_a(
        dx1b, z, pp, a_ln_w, a_ln_b, w_s, bst, wout_a, [_Exchange(qs_b), _ExchangeVia(q_win_b)],
        tm=TM_BWD_A, relay_step=RELAY_STEP_BWD_A)
    shapes_a = [(1, AW), (1, AW), (CH, G)]
    pack_a = _pack([g_lnw, g_lnb, g_bst], 8)
    q_wout_a, acc_wout_a, (red_b, sm_a) = _wgrad(
        ya, dx1b, [_SumGather(accs_b, lands_b), _InChip([g_ws, pack_a])], by_rows=True, per=2,
        name="wgrad_a_out", relay_step=1)
    qs_a, accs_a = sm_a[:2], sm_a[2:]
    q_win_a, acc_win_a, rel_a, (lands_a, (l_wout_a,)) = _wgrad_cols_early(
        h0, dz, [_Exchange(qs_a), _ExchangeVia(q_wout_a)], name="wgrad_a_in", relay_step=RELAY_STEP_WGRAD_A_IN)
    (gx, g_nw0), (red_a, (l_win_a,)) = _bwd_a_in(
        dz, dx1, xs, nw0, win_a8, [_SumGather(accs_a, lands_a), _ExchangeRest(q_win_a, rel_a)],
        tm=TM_BWD_A_IN, relay_step=RELAY_STEP_BWD_A_IN)

    r_ga, r_gx, r_pack_b = red_b
    r_nfw, r_nw1, r_p8, loss = _unpack(r_pack_b, shapes_b)
    r_ws, r_pack_a = red_a
    r_lnw, r_lnb, r_bst = _unpack(r_pack_a, shapes_a)
    g_p8 = lax.dynamic_slice_in_dim(r_p8, me * (BW // NDEV), BW // NDEV, axis=1)
    loss = loss[0, 0]

    weights = dict(norm_w=norm_w, a_w_in=a_w_in, a_ln_w=a_ln_w, a_ln_b=a_ln_b, a_w_s=a_w_s, a_b_s=a_b_s, a_w_out=a_w_out,
                   b_w_in=b_w_in, b_conv_w=b_conv_w, b_conv_b=b_conv_b, b_gate_a_w=b_gate_a_w, b_gate_a_b=b_gate_a_b,
                   b_gate_x_w=b_gate_x_w, b_gate_x_b=b_gate_x_b, b_lambda=b_lambda, b_w_out=b_w_out, norm_f_w=norm_f_w)
    mom1 = dict(norm_w=m_norm_w, a_w_in=m_a_w_in, a_ln_w=m_a_ln_w, a_ln_b=m_a_ln_b, a_w_s=m_a_w_s, a_b_s=m_a_b_s,
                a_w_out=m_a_w_out, b_w_in=m_b_w_in, b_conv_w=m_b_conv_w, b_conv_b=m_b_conv_b, b_gate_a_w=m_b_gate_a_w,
                b_gate_a_b=m_b_gate_a_b, b_gate_x_w=m_b_gate_x_w, b_gate_x_b=m_b_gate_x_b, b_lambda=m_b_lambda,
                b_w_out=m_b_w_out, norm_f_w=m_norm_f_w)
    mom2 = dict(norm_w=v_norm_w, a_w_in=v_a_w_in, a_ln_w=v_a_ln_w, a_ln_b=v_a_ln_b, a_w_s=v_a_w_s, a_b_s=v_a_b_s,
                a_w_out=v_a_w_out, b_w_in=v_b_w_in, b_conv_w=v_b_conv_w, b_conv_b=v_b_conv_b, b_gate_a_w=v_b_gate_a_w,
                b_gate_a_b=v_b_gate_a_b, b_gate_x_w=v_b_gate_x_w, b_gate_x_b=v_b_gate_x_b, b_lambda=v_b_lambda,
                b_w_out=v_b_w_out, norm_f_w=v_norm_f_w)
    names = list(weights)

    def as2d(a):
        return a.reshape(-1, a.shape[-1])

    upd, grads = {}, {}
    for k, acc, land in (("a_w_in", acc_win_a, l_win_a), ("a_w_out", acc_wout_a, l_wout_a),
                         ("b_w_in", acc_win_b, l_win_b), ("b_w_out", acc_wout_b, l_wout_b)):
        g, d, mo, vo = _adam_big(as2d(weights[k]), acc, land, as2d(mom1[k]), as2d(mom2[k]), "adam_" + k)
        grads[k] = g[None]
        upd[k] = (d, mo, vo)
    grads.update(
        norm_w=jnp.concatenate([g_nw0, r_nw1], axis=0), a_ln_w=r_lnw, a_ln_b=r_lnb,
        a_w_s=r_ws.reshape(1, G, CH, CH), a_b_s=r_bst.T[None],
        b_conv_w=g_p8[None, 0:4], b_conv_b=g_p8[4:5], b_gate_a_w=r_ga.reshape(1, BH, HD, HD), b_gate_a_b=g_p8[5:6],
        b_gate_x_w=r_gx.reshape(1, BH, HD, HD), b_gate_x_b=g_p8[6:7], b_lambda=g_p8[7:8], norm_f_w=r_nfw.reshape(D))
    small_names = [k for k in names if k not in upd]
    res = _adam_small([(as2d(weights[k]), as2d(grads[k]), as2d(mom1[k]), as2d(mom2[k])) for k in small_names])
    for k, r3 in zip(small_names, res):
        upd[k] = r3
    deltas = [upd[k][0].reshape(weights[k].shape) for k in names]
    new_m = [upd[k][1].reshape(weights[k].shape) for k in names]
    new_v = [upd[k][2].reshape(weights[k].shape) for k in names]
    return (loss, gx[None], *[grads[k] for k in names], *deltas, *new_m, *new_v)
```

```python
import jax
import jax.numpy as jnp
from jax import lax
from jax.experimental import pallas as pl
from jax.experimental.pallas import tpu as pltpu

F32 = jnp.float32
BF16 = jnp.bfloat16
MESH = pl.DeviceIdType.MESH

NDEV = 8
NCHIP_OTHER = 3
D = 1024
AW = 2048
G = 8
GD = AW // G
CH = 128
BW = 1536
BH = 12
HD = BW // BH
CA = 3 * AW // NDEV
CB = 2 * BW // NDEV
RMS_EPS = 1e-6
LN_EPS = 1e-5
RG_C = 8.0
LR, B1, B2, ADAM_EPS, WD, STEP = 0.001, 0.9, 0.999, 1e-08, 0.01, 10
V7X_VMEM_BYTES = 64 * 1024 * 1024
VMEM_LIMIT = V7X_VMEM_BYTES - 8 * 1024 * 1024
SUBLANES = 8
LANES = 128
BF16_ROWS = 16
TRANSPOSE_ROWS = 256
ADAM_ROWS = 512
GELU_C = 0.7978845608028654
GELU_K = 0.044715

_VMEM = pl.BlockSpec(memory_space=pltpu.VMEM)
_HBM = pl.BlockSpec(memory_space=pltpu.HBM)


def _sds(shape, dtype):
    return jax.ShapeDtypeStruct(tuple(shape), dtype)


def _params(**kw):
    return pltpu.CompilerParams(vmem_limit_bytes=VMEM_LIMIT, **kw)


def _gelu_t(z):
    p = 0.5 * jnp.tanh(z * (GELU_C + (GELU_C * GELU_K) * (z * z))) + 0.5
    return z * p, p


def _dgelu(z, p):
    return p * (1.0 + (z * (1.0 - p)) * (2.0 * GELU_C + (6.0 * GELU_C * GELU_K) * (z * z)))


def _sigmoid(v):
    return 0.5 * jnp.tanh(0.5 * v) + 0.5


def _softplus_neg(lam):
    return jnp.maximum(-lam, 0.0) + jnp.log1p(jnp.exp(-jnp.abs(lam)))


def _dot(a, b):
    return jnp.dot(a, b, preferred_element_type=F32)


def _dot_nt(a, b):
    return lax.dot_general(a, b, (((1,), (1,)), ((), ())), preferred_element_type=F32)


def _rowsum(v):
    return jnp.sum(v, axis=0, keepdims=True)


def _causal_mask():
    r = lax.broadcasted_iota(jnp.int32, (CH, CH), 0)
    c = lax.broadcasted_iota(jnp.int32, (CH, CH), 1)
    return r >= c


def _rms(x):
    return lax.rsqrt(jnp.mean(x * x, axis=-1, keepdims=True) + RMS_EPS)


def _rms_bwd(dh, x, r, nw):
    gy = dh * nw
    return r * gy - x * (r * r * r) * jnp.mean(gy * x, axis=-1, keepdims=True)


def _place():
    return lax.axis_index("x"), lax.axis_index("y"), lax.axis_index("c")


def _other_chips(x, y):
    return [(1 - x, y), (x, 1 - y), (1 - x, 1 - y)]


GATHER_SLOTS = 10


def _gather_ops(ins, outs, send_sems, recv_sems, local_sems):
    n = len(ins)
    x, y, c = _place()
    sibling = (x, y, 1 - c)
    xn, yn, dg = _other_chips(x, y)
    split = [ins[i].shape[0] % (2 * BF16_ROWS) == 0 for i in range(n)]

    def blk(chip, core):
        return 4 * chip[0] + 2 * chip[1] + core

    me = blk((x, y), c)

    def part(ref, i, half):
        if half is None:
            return ref
        h = ins[i].shape[0] // 2
        return ref.at[pl.ds(half * h, h)]

    def copy(i, k, block, to, half=None, src=None):
        dst = part(outs[i].at[block], i, half)
        return pltpu.make_async_remote_copy(
            src_ref=dst if src is None else part(src, i, half), dst_ref=dst,
            send_sem=send_sems.at[k, i], recv_sem=recv_sems.at[k, i], device_id=to, device_id_type=MESH)

    def first_copies():
        mine = [pltpu.make_async_copy(ins[i], outs[i].at[me], local_sems.at[i]) for i in range(n)]
        first = []
        for i in range(n):
            first.append(copy(i, 0, me, sibling, src=ins[i]))
            if split[i]:
                first.append(copy(i, 1, me, (*xn, c), 0, ins[i]))
                first.append(copy(i, 3, me, (*yn, c), 1, ins[i]))
                first.append(copy(i, 2, me, (*xn, c), 1, ins[i]))
                first.append(copy(i, 4, me, (*yn, c), 0, ins[i]))
            else:
                first.append(copy(i, 1, me, (*xn, c), None, ins[i]))
                first.append(copy(i, 3, me, (*yn, c), None, ins[i]))
                first.append(copy(i, 5, me, (*dg, c), None, ins[i]))
        return mine, first

    def onward():
        out = []
        for i in range(n):
            if split[i]:
                out.append(copy(i, 5, blk(xn, c), (*yn, c), 0))
                out.append(copy(i, 6, blk(yn, c), (*xn, c), 1))
        return out

    def start():
        mine, first = first_copies()
        for cp in mine + first:
            cp.start()

    def relay():
        sends = onward()
        for i in range(n):
            if split[i]:
                copy(i, 1, blk(xn, c), sibling, 0).wait_recv()
                sends.pop(0).start()
                copy(i, 3, blk(yn, c), sibling, 1).wait_recv()
                sends.pop(0).start()

    def finish():
        mine, first = first_copies()
        passed = []

        def pass_on(i, j, chip):
            fwd = copy(i, 7 + j, blk(chip, c), sibling)
            fwd.start()
            passed.append(fwd)

        for i in range(n):
            if split[i]:
                copy(i, 2, blk(xn, c), sibling, 1).wait_recv()
                pass_on(i, 0, xn)
                copy(i, 4, blk(yn, c), sibling, 0).wait_recv()
                pass_on(i, 1, yn)
                copy(i, 5, blk(dg, c), sibling, 0).wait_recv()
                copy(i, 6, blk(dg, c), sibling, 1).wait_recv()
                pass_on(i, 2, dg)
            else:
                copy(i, 1, blk(xn, c), sibling).wait_recv()
                pass_on(i, 0, xn)
                copy(i, 3, blk(yn, c), sibling).wait_recv()
                pass_on(i, 1, yn)
                copy(i, 5, blk(dg, c), sibling).wait_recv()
                pass_on(i, 2, dg)
        for i in range(n):
            copy(i, 0, blk((x, y), 1 - c), sibling).wait_recv()
            for j, chip in enumerate((xn, yn, dg)):
                copy(i, 7 + j, blk(chip, 1 - c), sibling).wait_recv()
        for cp in first + passed + onward():
            cp.wait_send()
        for cp in mine:
            cp.wait()

    return start, relay, finish


def _gather_sems(n):
    return [pltpu.SemaphoreType.DMA((GATHER_SLOTS, n)), pltpu.SemaphoreType.DMA((GATHER_SLOTS, n)),
            pltpu.SemaphoreType.DMA((n,))]


class _Gather:
    def __init__(self, shards, as_dtypes=None):
        n = len(shards)
        dts = [s.dtype for s in shards] if as_dtypes is None else list(as_dtypes)
        self.cast = [jnp.dtype(d) != s.dtype for d, s in zip(dts, shards)]
        self.ins = list(shards)
        self.in_specs = [_VMEM if c else _HBM for c in self.cast]
        self.out_shape = [_sds((NDEV,) + s.shape, d) for s, d in zip(shards, dts)]
        self.out_specs = [_HBM] * n
        self.scratch = [pltpu.VMEM(s.shape, d) for s, d, c in zip(shards, dts, self.cast) if c] + _gather_sems(n)

    def ops(self, ins, outs, scr):
        ncast = sum(self.cast)
        staged = iter(scr[:ncast])
        srcs = [next(staged) if c else ref for c, ref in zip(self.cast, ins)]
        start, relay, finish = _gather_ops(srcs, outs, *scr[ncast:])

        def cast_and_start():
            for c, ref, src in zip(self.cast, ins, srcs):
                if c:
                    src[...] = ref[...].astype(src.dtype)
            start()

        return cast_and_start, relay, finish


class _Exchange:
    def __init__(self, qs):
        n = len(qs)
        self.ins, self.in_specs = list(qs), [_HBM] * n
        self.out_shape = [_sds(q.shape, q.dtype) for q in qs]
        self.out_specs = [_HBM] * n
        self.scratch = [pltpu.SemaphoreType.DMA((NCHIP_OTHER, n)), pltpu.SemaphoreType.DMA((NCHIP_OTHER, n))]

    def ops(self, ins, outs, scr):
        send_sems, recv_sems = scr
        n = len(ins)
        x, y, c = _place()
        chips = _other_chips(x, y)

        def copies():
            return [pltpu.make_async_remote_copy(
                src_ref=ins[i].at[j], dst_ref=outs[i].at[j], send_sem=send_sems.at[j, i],
                recv_sem=recv_sems.at[j, i], device_id=(*chips[j], c), device_id_type=MESH)
                for i in range(n) for j in range(NCHIP_OTHER)]

        def start():
            for cp in copies():
                cp.start()

        def finish():
            cps = copies()
            for cp in cps:
                cp.wait_recv()
            for cp in cps:
                cp.wait_send()

        return start, lambda: None, finish


class _ExchangeVia:
    def __init__(self, q):
        _, r, cd = q.shape
        half = (2, r // 2, cd)
        self.ins, self.in_specs = [q], [_HBM]
        self.out_shape, self.out_specs = [_sds((2, r, cd), q.dtype)], [_HBM]
        self.scratch = [pltpu.VMEM(half, q.dtype), pltpu.VMEM(half, q.dtype), pltpu.VMEM(half, q.dtype),
                        pltpu.SemaphoreType.DMA((6,)), pltpu.SemaphoreType.DMA((6,)), pltpu.SemaphoreType.DMA((2,))]

    def ops(self, ins, outs, scr):
        (q,), (land,) = ins, outs
        relayed, own, comb, send_sems, recv_sems, local_sems = scr
        h = q.shape[1] // 2
        x, y, c = _place()
        xn, yn, _ = _other_chips(x, y)
        h0, h1 = pl.ds(0, h), pl.ds(h, h)

        def remote(k, src, dst, chip):
            return pltpu.make_async_remote_copy(src_ref=src, dst_ref=dst, send_sem=send_sems.at[k],
                                                recv_sem=recv_sems.at[k], device_id=(*chip, c), device_id_type=MESH)

        def via():
            return [remote(2, q.at[2, h0], relayed.at[0], xn), remote(3, q.at[2, h1], relayed.at[1], yn)]

        def direct():
            return [remote(0, q.at[0, h0], land.at[0, h0], xn), remote(1, q.at[1, h1], land.at[1, h1], yn)]

        def second():
            return [remote(4, comb.at[0], land.at[1, h0], yn), remote(5, comb.at[1], land.at[0, h1], xn)]

        def mine():
            return [pltpu.make_async_copy(q.at[1, h0], own.at[0], local_sems.at[0]),
                    pltpu.make_async_copy(q.at[0, h1], own.at[1], local_sems.at[1])]

        def start():
            for cp in via() + direct() + mine():
                cp.start()

        def relay():
            arrived, loaded, onward = via(), mine(), second()
            for k in range(2):
                arrived[k].wait_recv()
                loaded[k].wait()
                comb[k] = (own[k].astype(F32) + relayed[k].astype(F32)).astype(comb.dtype)
                onward[k].start()

        def finish():
            landing = direct() + second()
            for cp in landing:
                cp.wait_recv()
            for cp in via() + landing:
                cp.wait_send()

        return start, relay, finish


class _SumGather:
    def __init__(self, accs, lands):
        n = len(accs)
        self.n = n
        self.ins, self.in_specs = list(accs) + list(lands), [_VMEM] * (2 * n)
        self.out_shape = [_sds((NDEV,) + a.shape, a.dtype) for a in accs]
        self.out_specs = [_HBM] * n
        self.scratch = [pltpu.VMEM(a.shape, a.dtype) for a in accs] + _gather_sems(n)

    def ops(self, ins, outs, scr):
        n = self.n
        accs, lands, mine = ins[:n], ins[n:], scr[:n]
        g_start, relay, finish = _gather_ops(mine, outs, *scr[n:])

        def start():
            for i in range(n):
                mine[i][...] = accs[i][...] + lands[i][0] + lands[i][1] + lands[i][2]
            g_start()

        return start, relay, finish


def _call(main, jobs, *, name, grid, ins, in_specs, out_shape, out_specs, scratch, relay_step=0, first=0,
          prologue=None):
    nsteps = grid[0] if grid else 1
    n_in, n_out, n_scr = len(ins), len(out_shape), len(scratch)

    def body(*refs):
        pos = [0]

        def take(k):
            r = refs[pos[0]:pos[0] + k]
            pos[0] += k
            return r

        m_in = take(n_in)
        j_in = [take(len(j.ins)) for j in jobs]
        m_out = take(n_out)
        j_out = [take(len(j.out_shape)) for j in jobs]
        m_scr = take(n_scr)
        j_scr = [take(len(j.scratch)) for j in jobs]
        ops = [j.ops(a, b, s) for j, a, b, s in zip(jobs, j_in, j_out, j_scr)]
        i = pl.program_id(0) if grid else 0
        if not grid:
            for o in ops:
                o[0]()
            main(i, m_in, m_out, m_scr)
            for o in ops:
                o[1]()
            for o in ops:
                o[2]()
            return

        if ops:
            @pl.when(i == 0)
            def _():
                for o in ops[:first]:
                    o[0]()
                for o in ops[:first]:
                    o[1]()
                for o in ops[first:]:
                    o[0]()
                for o in ops[:first]:
                    o[2]()
                if prologue is not None:
                    prologue(j_out[:first], m_scr)

        main(i, m_in, m_out, m_scr)

        if ops[first:]:
            @pl.when(i == min(relay_step, nsteps - 1))
            def _():
                for o in ops[first:]:
                    o[1]()

            @pl.when(i == nsteps - 1)
            def _():
                for o in ops[first:]:
                    o[2]()

    extra = dict(dimension_semantics=("arbitrary",)) if grid else {}
    res = pl.pallas_call(
        body, name=name, grid=grid,
        in_specs=list(in_specs) + [s for j in jobs for s in j.in_specs],
        out_specs=list(out_specs) + [s for j in jobs for s in j.out_specs],
        out_shape=list(out_shape) + [s for j in jobs for s in j.out_shape],
        scratch_shapes=list(scratch) + [s for j in jobs for s in j.scratch],
        compiler_params=_params(**extra),
    )(*ins, *[a for j in jobs for a in j.ins])
    main_out, rest, job_out = res[:n_out], res[n_out:], []
    for j in jobs:
        k = len(j.out_shape)
        job_out.append(rest[:k])
        rest = rest[k:]
    return main_out, job_out


def _comm_only(jobs, name):
    _, job_out = _call(lambda i, a, b, s: None, jobs, name=name, grid=(), ins=[], in_specs=[], out_shape=[],
                       out_specs=[], scratch=[])
    return job_out


class _InChip:
    def __init__(self, ps):
        n = len(ps)
        self.n = n
        blk = [p.shape[1:] for p in ps]
        self.ins, self.in_specs = list(ps), [_HBM] * n
        self.out_shape = [_sds((NCHIP_OTHER,) + b, p.dtype) for b, p in zip(blk, ps)] + [_sds(b, F32) for b in blk]
        self.out_specs = [_VMEM] * (2 * n)
        self.scratch = ([pltpu.VMEM((4,) + b, p.dtype) for b, p in zip(blk, ps)] * 2
                        + [pltpu.SemaphoreType.DMA((4, n))] * 3)

    def ops(self, ins, outs, scr):
        n = self.n
        q_refs, acc_refs = outs[:n], outs[n:]
        mines, lands = scr[:n], scr[n:2 * n]
        send_sems, recv_sems, local_sems = scr[2 * n:]
        x, y, c = _place()
        sibling = (x, y, 1 - c)

        def copies():
            out = []
            for i in range(n):
                for pi in range(4):
                    loc = pltpu.make_async_copy(ins[i].at[2 * pi + c], mines[i].at[pi], local_sems.at[pi, i])
                    cp = pltpu.make_async_remote_copy(
                        src_ref=ins[i].at[2 * pi + (1 - c)], dst_ref=lands[i].at[pi],
                        send_sem=send_sems.at[pi, i], recv_sem=recv_sems.at[pi, i],
                        device_id=sibling, device_id_type=MESH)
                    out.append((loc, cp))
            return out

        def start():
            for loc, cp in copies():
                loc.start()
                cp.start()

        def finish():
            pairs = copies()
            for loc, cp in pairs:
                loc.wait()
                cp.wait_recv()
            for i in range(n):
                _chip_sums(mines[i], lands[i], q_refs[i], acc_refs[i], x, y)
            for _, cp in pairs:
                cp.wait_send()

        return start, lambda: None, finish


def _chip_sums(mine, land, q_ref, acc_ref, x, y):
    for j, (qx, qy) in enumerate(_other_chips(x, y)):
        qi = 2 * qx + qy
        q_ref[j] = (mine[qi].astype(F32) + land[qi].astype(F32)).astype(q_ref.dtype)
    mi = 2 * x + y
    acc_ref[...] = mine[mi].astype(F32) + land[mi].astype(F32)


def _direct_sum(v, buf, send_sems, recv_sems):
    x, y, c = _place()
    me = 4 * x + 2 * y + c
    buf[me] = v
    cps = []
    for k in range(1, NDEV):
        fx, fy, fc = (k >> 2) & 1, (k >> 1) & 1, k & 1
        peer = ((1 - x) if fx else x, (1 - y) if fy else y, (1 - c) if fc else c)
        cps.append((peer, pltpu.make_async_remote_copy(
            src_ref=buf.at[me], dst_ref=buf.at[me], send_sem=send_sems.at[k - 1], recv_sem=recv_sems.at[k - 1],
            device_id=peer, device_id_type=MESH)))
    for _, cp in cps:
        cp.start()
    for k, (peer, _) in enumerate(cps):
        theirs = 4 * peer[0] + 2 * peer[1] + peer[2]
        pltpu.make_async_remote_copy(
            src_ref=buf.at[theirs], dst_ref=buf.at[theirs], send_sem=send_sems.at[k], recv_sem=recv_sems.at[k],
            device_id=peer, device_id_type=MESH).wait_recv()
    acc = buf[0]
    for j in range(1, NDEV):
        acc = acc + buf[j]
    for _, cp in cps:
        cp.wait_send()
    return acc


def _direct_sum_scratch(shape, dtype):
    return [pltpu.VMEM((NDEV,) + tuple(shape), dtype), pltpu.SemaphoreType.DMA((NDEV - 1,)),
            pltpu.SemaphoreType.DMA((NDEV - 1,))]


def _fwd_a(x, nw, lnw, lnb, ws, bst, jobs, *, tm, relay_step):
    s_len = x.shape[0]
    nt = s_len // tm
    nch = tm // CH

    def main(i, ins, outs, scr):
        x_ref, nw_ref, lnw_ref, lnb_ref, ws_ref, bst_ref = ins
        z_ref, h_ref, y_ref, pp_ref = outs
        wc_scr, gv_scr, win_ref = scr

        @pl.when(i == 0)
        def _():
            m = _causal_mask()
            for g in range(G):
                wc_scr[g] = jnp.where(m, ws_ref[g], 0.0).astype(BF16)

        x = x_ref[...]
        h = (x * _rms(x) * nw_ref[...]).astype(BF16)
        h_ref[...] = h
        for k in range(NDEV):
            z_ref[:, k * CA:(k + 1) * CA] = _dot(h, win_ref[k])

        ssum = jnp.zeros((tm, 1), F32)
        for g in range(G):
            vs = slice(AW + g * GD, AW + (g + 1) * GD)
            gv, pv = _gelu_t(z_ref[:, vs])
            pp_ref[:, vs] = pv.astype(BF16)
            gv_scr[:, g * GD:(g + 1) * GD] = gv
            ssum = ssum + jnp.sum(gv, axis=-1, keepdims=True)
        mu = ssum * (1.0 / AW)
        vsum = jnp.zeros((tm, 1), F32)
        for g in range(G):
            dlt = gv_scr[:, g * GD:(g + 1) * GD] - mu
            vsum = vsum + jnp.sum(dlt * dlt, axis=-1, keepdims=True)
        rstd = lax.rsqrt(vsum * (1.0 / AW) + LN_EPS)

        for g in range(G):
            cs = slice(g * GD, (g + 1) * GD)
            gs = slice(2 * AW + g * GD, 2 * AW + (g + 1) * GD)
            v = (gv_scr[:, cs] - mu) * rstd * lnw_ref[:, cs] + lnb_ref[:, cs]
            vb = v.astype(BF16)
            u, pu = _gelu_t(z_ref[:, cs])
            pp_ref[:, cs] = pu.astype(BF16)
            zg = z_ref[:, gs]
            sig = _sigmoid(zg)
            pp_ref[:, gs] = sig.astype(BF16)
            sg = zg * sig
            for n in range(nch):
                rs = slice(n * CH, (n + 1) * CH)
                s = _dot(wc_scr[g], vb[rs, :]) + bst_ref[:, g:g + 1]
                y_ref[rs, cs] = (u[rs, :] * s * sg[rs, :]).astype(BF16)

    tile = lambda w: pl.BlockSpec((tm, w), lambda i: (i, 0))
    return _call(
        main, jobs, name="fwd_a", grid=(nt,), relay_step=relay_step, first=1,
        prologue=lambda gathered, scr: pltpu.sync_copy(gathered[0][0], scr[2]),
        ins=[x, nw, lnw, lnb, ws, bst], in_specs=[tile(D), _VMEM, _VMEM, _VMEM, _VMEM, _VMEM],
        out_shape=[_sds((s_len, 3 * AW), F32), _sds((s_len, D), BF16), _sds((s_len, AW), BF16),
                   _sds((s_len, 3 * AW), BF16)],
        out_specs=[tile(3 * AW), tile(D), tile(AW), tile(3 * AW)],
        scratch=[pltpu.VMEM((G, CH, CH), BF16), pltpu.VMEM((tm, AW), F32), pltpu.VMEM((NDEV, D, CA), BF16)])


def _bwd_a(dx1, z, pp, lnw, lnb, ws, bst, wout, jobs, *, tm, relay_step):
    s_len = dx1.shape[0]
    nt = s_len // tm
    nch = tm // CH

    def main(i, ins, outs, scr):
        dx1_ref, z_ref, pp_ref, lnw_ref, lnb_ref, ws_ref, bst_ref, wout_ref = ins
        dz_ref, glnw_ref, glnb_ref, gws_ref, gbst_ref = outs
        wc_scr, wct_scr, vh_scr, dgv_scr, dy_scr, dv_scr, gbs_acc, gwc_acc = scr

        @pl.when(i == 0)
        def _():
            m = _causal_mask()
            for g in range(G):
                wm = jnp.where(m, ws_ref[g], 0.0)
                wc_scr[g] = wm.astype(BF16)
                wct_scr[g] = wm.T.astype(BF16)
            glnw_ref[...] = jnp.zeros_like(glnw_ref)
            glnb_ref[...] = jnp.zeros_like(glnb_ref)
            gbs_acc[...] = jnp.zeros_like(gbs_acc)
            gwc_acc[...] = jnp.zeros_like(gwc_acc)

        dy_scr[...] = _dot_nt(dx1_ref[...], wout_ref[...])

        ssum = jnp.zeros((tm, 1), F32)
        for g in range(G):
            cs = slice(g * GD, (g + 1) * GD)
            vs = slice(AW + g * GD, AW + (g + 1) * GD)
            zv = z_ref[:, vs]
            pv = pp_ref[:, vs].astype(F32)
            gv = zv * pv
            vh_scr[:, cs] = gv
            dgv_scr[:, cs] = _dgelu(zv, pv)
            ssum = ssum + jnp.sum(gv, axis=-1, keepdims=True)
        mu = ssum * (1.0 / AW)
        vsum = jnp.zeros((tm, 1), F32)
        for g in range(G):
            dlt = vh_scr[:, g * GD:(g + 1) * GD] - mu
            vsum = vsum + jnp.sum(dlt * dlt, axis=-1, keepdims=True)
        rstd = lax.rsqrt(vsum * (1.0 / AW) + LN_EPS)

        m1 = jnp.zeros((tm, 1), F32)
        m2 = jnp.zeros((tm, 1), F32)
        for g in range(G):
            cs = slice(g * GD, (g + 1) * GD)
            gs = slice(2 * AW + g * GD, 2 * AW + (g + 1) * GD)
            vhat = (vh_scr[:, cs] - mu) * rstd
            vh_scr[:, cs] = vhat
            vb = (vhat * lnw_ref[:, cs] + lnb_ref[:, cs]).astype(BF16)
            zu = z_ref[:, cs]
            tu = pp_ref[:, cs].astype(F32)
            u = zu * tu
            zg = z_ref[:, gs]
            sig = pp_ref[:, gs].astype(F32)
            sg = zg * sig
            dy = dy_scr[:, cs]
            dsf = dy * u * sg
            dsb = dsf.astype(BF16)
            dvs = []
            for n in range(nch):
                rs = slice(n * CH, (n + 1) * CH)
                s = _dot(wc_scr[g], vb[rs, :]) + bst_ref[:, g:g + 1]
                dys = dy[rs, :] * s
                dz_ref[rs, cs] = (dys * sg[rs, :] * _dgelu(zu[rs, :], tu[rs, :])).astype(BF16)
                dz_ref[rs, gs] = (dys * u[rs, :] * (sig[rs, :] * (1.0 + zg[rs, :] * (1.0 - sig[rs, :])))).astype(BF16)
                gbs_acc[g] += dsf[rs, :]
                gwc_acc[g] += _dot_nt(dsb[rs, :], vb[rs, :])
                dvs.append(_dot(wct_scr[g], dsb[rs, :]))
            dv = jnp.concatenate(dvs, axis=0) if nch > 1 else dvs[0]
            glnw_ref[:, cs] += _rowsum(dv * vhat)
            glnb_ref[:, cs] += _rowsum(dv)
            dvh = dv * lnw_ref[:, cs]
            dv_scr[:, cs] = dvh
            m1 = m1 + jnp.sum(dvh, axis=-1, keepdims=True)
            m2 = m2 + jnp.sum(dvh * vhat, axis=-1, keepdims=True)
        m1 = m1 * (1.0 / AW)
        m2 = m2 * (1.0 / AW)
        for g in range(G):
            cs = slice(g * GD, (g + 1) * GD)
            dgv = rstd * (dv_scr[:, cs] - m1 - vh_scr[:, cs] * m2)
            dz_ref[:, AW + g * GD:AW + (g + 1) * GD] = (dgv * dgv_scr[:, cs]).astype(BF16)

        @pl.when(i == nt - 1)
        def _():
            m = _causal_mask()
            for g in range(G):
                gws_ref[g] = jnp.where(m, gwc_acc[g], 0.0)
                gbst_ref[:, g:g + 1] = jnp.sum(gbs_acc[g], axis=-1, keepdims=True)

    tile = lambda w: pl.BlockSpec((tm, w), lambda i: (i, 0))
    whole = lambda *s: pl.BlockSpec(s, lambda i: (0,) * len(s))
    big = lambda dt: pltpu.VMEM((tm, AW), dt)
    return _call(
        main, jobs, name="bwd_a", grid=(nt,), relay_step=relay_step,
        ins=[dx1, z, pp, lnw, lnb, ws, bst, wout],
        in_specs=[tile(D), tile(3 * AW), tile(3 * AW), _VMEM, _VMEM, _VMEM, _VMEM, _VMEM],
        out_shape=[_sds((s_len, 3 * AW), BF16), _sds((1, AW), F32), _sds((1, AW), F32), _sds((G, CH, CH), F32),
                   _sds((CH, G), F32)],
        out_specs=[tile(3 * AW), whole(1, AW), whole(1, AW), whole(G, CH, CH), whole(CH, G)],
        scratch=[pltpu.VMEM((G, CH, CH), BF16), pltpu.VMEM((G, CH, CH), BF16), big(F32), big(F32), big(F32), big(F32),
                 pltpu.VMEM((G, CH, GD), F32), pltpu.VMEM((G, CH, CH), F32)])


def _bwd_a_in(dz, dx1, x, nw, win8, jobs, *, tm, relay_step):
    s_len = x.shape[0]
    nt = s_len // tm

    def main(i, ins, outs, scr):
        dz_ref, dx1_ref, x_ref, nw_ref, win_ref = ins
        gx_ref, gnw_ref = outs

        @pl.when(i == 0)
        def _():
            gnw_ref[...] = jnp.zeros_like(gnw_ref)

        dh = jnp.zeros((tm, D), F32)
        for k in range(NDEV):
            dh = dh + _dot_nt(dz_ref[:, k * CA:(k + 1) * CA], win_ref[k])
        x = x_ref[...]
        r = _rms(x)
        gx_ref[...] = dx1_ref[...] + _rms_bwd(dh, x, r, nw_ref[...])
        gnw_ref[...] += _rowsum(dh * x * r)

        @pl.when(i == nt - 1)
        def _():
            gnw_ref[...] = _direct_sum(gnw_ref[...], *scr)

    tile = lambda w: pl.BlockSpec((tm, w), lambda i: (i, 0))
    return _call(
        main, jobs, name="bwd_a_in", grid=(nt,), relay_step=relay_step,
        ins=[dz, dx1, x, nw, win8], in_specs=[tile(3 * AW), tile(D), tile(D), _VMEM, _VMEM],
        out_shape=[_sds((s_len, D), F32), _sds((1, D), F32)],
        out_specs=[tile(D), pl.BlockSpec((1, D), lambda i: (0, 0))], scratch=_direct_sum_scratch((1, D), F32))


def _conv(p8_ref, cs, xb, xm1, xm2, xm3):
    xc = p8_ref[4:5, cs] + p8_ref[3:4, cs] * xb
    xc = xc + p8_ref[0:1, cs] * xm3
    xc = xc + p8_ref[1:2, cs] * xm2
    return xc + p8_ref[2:3, cs] * xm1


def _gates(p8_ref, gcat_ref, hh, xc):
    cs = slice(hh * HD, (hh + 1) * HD)
    pre = _dot(xc.astype(BF16), gcat_ref[hh])
    r = _sigmoid(pre[:, :HD] + p8_ref[5:6, cs])
    ig = _sigmoid(pre[:, HD:] + p8_ref[6:7, cs])
    sp = _softplus_neg(p8_ref[7:8, cs])
    la = (-RG_C) * r * sp
    a = jnp.exp(la)
    half_log = 0.5 * jnp.log(jnp.tanh(-la) * (1.0 + a * a))
    return r, ig, sp, a, jnp.exp(half_log), jnp.exp(-half_log)


def _scan_rows(a_ref, b_ref, out_ref, carry, tm, reverse):
    row = lax.broadcasted_iota(jnp.int32, (SUBLANES, BW), 0)
    ngrp = tm // SUBLANES

    def step(j, cr):
        jj = (ngrp - 1 - j) if reverse else j
        off = pl.multiple_of(jj * SUBLANES, SUBLANES)
        a = a_ref[pl.ds(off, SUBLANES), :]
        b = b_ref[pl.ds(off, SUBLANES), :]
        for sh in (1, 2, 4):
            if reverse:
                a_s = pltpu.roll(a, SUBLANES - sh, 0)
                b_s = pltpu.roll(b, SUBLANES - sh, 0)
                m = row < SUBLANES - sh
            else:
                a_s = pltpu.roll(a, sh, 0)
                b_s = pltpu.roll(b, sh, 0)
                m = row >= sh
            b = jnp.where(m, a * b_s + b, b)
            a = jnp.where(m, a * a_s, a)
        o = b + a * cr
        out_ref[pl.ds(off, SUBLANES), :] = o
        return o[0:1, :] if reverse else o[SUBLANES - 1:SUBLANES, :]

    return lax.fori_loop(0, ngrp, step, carry)


def _fwd_b(x, ya, wout_a, nw, win8, p8, gcat, jobs, *, tm, relay_step):
    s_len = x.shape[0]
    nt = s_len // tm

    def main(i, ins, outs, scr):
        x_ref, ya_ref, wouta_ref, nw_ref, win_ref, p8_ref, gcat_ref = ins
        x1_ref, zb_ref, hs_ref, h1_ref, yb_ref, xc_ref, a_ref, cc_ref, r_ref, ig_ref, m_ref = outs
        xbe_scr, b_scr, k_scr, carry_scr = scr

        @pl.when(i == 0)
        def _():
            xbe_scr[0:SUBLANES, :] = jnp.zeros((SUBLANES, BW), F32)
            carry_scr[...] = jnp.zeros_like(carry_scr)

        x1 = x_ref[...] + _dot(ya_ref[...], wouta_ref[...])
        x1_ref[...] = x1
        h = (x1 * _rms(x1) * nw_ref[...]).astype(BF16)
        h1_ref[...] = h
        for k in range(NDEV):
            zb_ref[:, k * CB:(k + 1) * CB] = _dot(h, win_ref[k])
        xbe_scr[SUBLANES:SUBLANES + tm, :] = zb_ref[:, :BW]
        for hh in range(BH):
            cs = slice(hh * HD, (hh + 1) * HD)
            xc = _conv(p8_ref, cs, xbe_scr[SUBLANES:SUBLANES + tm, cs], xbe_scr[7:7 + tm, cs],
                       xbe_scr[6:6 + tm, cs], xbe_scr[5:5 + tm, cs])
            r, ig, _, a, mult, rm = _gates(p8_ref, gcat_ref, hh, xc)
            ixc = ig * xc
            xc_ref[:, cs] = xc
            a_ref[:, cs] = a
            r_ref[:, cs] = r.astype(BF16)
            ig_ref[:, cs] = ig.astype(BF16)
            m_ref[:, cs] = mult.astype(BF16)
            b_scr[:, cs] = mult * ixc
            k_scr[:, cs] = ixc * (a * a * rm)
        xbe_scr[0:SUBLANES, :] = xbe_scr[tm:tm + SUBLANES, :]
        carry_scr[...] = _scan_rows(a_ref, b_scr, hs_ref, carry_scr[...], tm, False)
        for hh in range(BH):
            cs = slice(hh * HD, (hh + 1) * HD)
            gt = zb_ref[:, BW + hh * HD:BW + (hh + 1) * HD]
            hsv = hs_ref[:, cs]
            yb_ref[:, cs] = (hsv * (gt * _sigmoid(gt))).astype(BF16)
            cc_ref[:, cs] = (hsv - b_scr[:, cs]) - k_scr[:, cs]

    tile = lambda w: pl.BlockSpec((tm, w), lambda i: (i, 0))
    wide = lambda dt: _sds((s_len, BW), dt)
    return _call(
        main, jobs, name="fwd_b", grid=(nt,), relay_step=relay_step,
        ins=[x, ya, wout_a, nw, win8, p8, gcat], in_specs=[tile(D), tile(AW), _VMEM, _VMEM, _VMEM, _VMEM, _VMEM],
        out_shape=[_sds((s_len, D), F32), _sds((s_len, 2 * BW), F32), wide(F32), _sds((s_len, D), BF16), wide(BF16),
                   wide(F32), wide(F32), wide(F32), wide(BF16), wide(BF16), wide(BF16)],
        out_specs=[tile(D), tile(2 * BW), tile(BW), tile(D)] + [tile(BW)] * 7,
        scratch=[pltpu.VMEM((tm + SUBLANES, BW), F32), pltpu.VMEM((tm, BW), F32), pltpu.VMEM((tm, BW), F32),
                 pltpu.VMEM((1, BW), F32)])


def _head(x1, yb, wout, nfw, tgt, *, tm):
    s_len = x1.shape[0]

    def main(i, ins, outs, scr):
        x1_ref, yb_ref, wout_ref, nfw_ref, t_ref = ins
        dx2_ref, dx2b_ref, loss_ref, gnfw_ref = outs

        @pl.when(i == 0)
        def _():
            loss_ref[...] = jnp.zeros_like(loss_ref)
            gnfw_ref[...] = jnp.zeros_like(gnfw_ref)

        x2 = x1_ref[...] + _dot(yb_ref[...], wout_ref[...])
        rf = _rms(x2)
        xn = x2 * rf
        e = xn * nfw_ref[...] - t_ref[...]
        loss_ref[...] += (0.5 / D) * jnp.sum(jnp.sum(e * e, axis=-1, keepdims=True), axis=0, keepdims=True)
        dyf = e * (1.0 / D)
        gnfw_ref[...] += _rowsum(dyf * xn)
        dx2 = _rms_bwd(dyf, x2, rf, nfw_ref[...])
        dx2_ref[...] = dx2
        dx2b_ref[...] = dx2.astype(BF16)

    tile = lambda w: pl.BlockSpec((tm, w), lambda i: (i, 0))
    whole = lambda *s: pl.BlockSpec(s, lambda i: (0,) * len(s))
    (dx2, dx2b, loss, gnfw), _ = _call(
        main, [], name="head", grid=(s_len // tm,),
        ins=[x1, yb, wout, nfw, tgt], in_specs=[tile(D), tile(BW), _VMEM, _VMEM, tile(D)],
        out_shape=[_sds((s_len, D), F32), _sds((s_len, D), BF16), _sds((1, 1), F32), _sds((1, D), F32)],
        out_specs=[tile(D), tile(D), whole(1, 1), whole(1, D)], scratch=[])
    return dx2, dx2b, loss, gnfw


def _bwd_b(dx2, zb, hs, x1, saved, nw, win8, p8, gcat, wout, *, tm):
    s_len = x1.shape[0]
    nt = s_len // tm

    def main(i, ins, outs, scr):
        (dx2_ref, zb_ref, hs_ref, x1_ref, xc_ref, a_ref, cc_ref, r_ref, ig_ref, m_ref,
         nw_ref, win_ref, p8_ref, gcat_ref, wout_ref) = ins
        dx1_ref, dx1b_ref, dzb_ref, gp8_ref, gga_ref, ggx_ref, gnw_ref = outs
        ae_scr, an_scr, dhd_scr, dh_scr, dy_scr, dxce_scr, carry_scr, afirst_scr = scr

        @pl.when(i == 0)
        def _():
            gp8_ref[...] = jnp.zeros_like(gp8_ref)
            gga_ref[...] = jnp.zeros_like(gga_ref)
            ggx_ref[...] = jnp.zeros_like(ggx_ref)
            gnw_ref[...] = jnp.zeros_like(gnw_ref)
            dxce_scr[tm:tm + SUBLANES, :] = jnp.zeros((SUBLANES, BW), F32)
            carry_scr[...] = jnp.zeros_like(carry_scr)
            afirst_scr[...] = jnp.zeros_like(afirst_scr)

        dx2 = dx2_ref[...]
        dy_scr[...] = _dot_nt(dx2.astype(BF16), wout_ref[...])
        for hh in range(BH):
            cs = slice(hh * HD, (hh + 1) * HD)
            gs = slice(BW + hh * HD, BW + (hh + 1) * HD)
            gt = zb_ref[:, gs]
            sig = _sigmoid(gt)
            dy = dy_scr[:, cs]
            dhd_scr[:, cs] = dy * (gt * sig)
            dzb_ref[:, gs] = (dy * hs_ref[:, cs] * (sig * (1.0 + gt * (1.0 - sig)))).astype(BF16)

        ae_scr[0:tm, :] = a_ref[...]
        ae_scr[tm:tm + SUBLANES, :] = jnp.broadcast_to(afirst_scr[...], (SUBLANES, BW))
        an_scr[...] = ae_scr[1:1 + tm, :]
        afirst_scr[...] = ae_scr[0:1, :]
        carry_scr[...] = _scan_rows(an_scr, dhd_scr, dh_scr, carry_scr[...], tm, True)

        for hh in range(BH):
            cs = slice(hh * HD, (hh + 1) * HD)
            dh = dh_scr[:, cs]
            mult = m_ref[:, cs].astype(F32)
            ig = ig_ref[:, cs].astype(F32)
            r = r_ref[:, cs].astype(F32)
            xc = xc_ref[:, cs]
            lam = p8_ref[7:8, cs]
            sp = _softplus_neg(lam)
            dla = dh * cc_ref[:, cs]
            gp8_ref[7:8, cs] += _rowsum(dla * ((-RG_C) * r)) * (-_sigmoid(-lam))
            dpr = dla * ((-RG_C) * sp) * (r * (1.0 - r))
            dpi = dh * mult * xc * (ig * (1.0 - ig))
            gp8_ref[5:6, cs] += _rowsum(dpr)
            gp8_ref[6:7, cs] += _rowsum(dpi)
            dcat = jnp.concatenate([dpr, dpi], axis=1).astype(BF16)
            dxc = dh * mult * ig + _dot_nt(dcat, gcat_ref[hh])
            gg = _dot(xc.T.astype(BF16), dcat)
            gga_ref[hh] += gg[:, :HD]
            ggx_ref[hh] += gg[:, HD:]
            dxce_scr[0:tm, cs] = dxc
            gp8_ref[4:5, cs] += _rowsum(dxc)
        for hh in range(BH):
            cs = slice(hh * HD, (hh + 1) * HD)
            xb = zb_ref[:, cs]
            d0, d1 = dxce_scr[0:tm, cs], dxce_scr[1:1 + tm, cs]
            d2, d3 = dxce_scr[2:2 + tm, cs], dxce_scr[3:3 + tm, cs]
            dzb_ref[:, cs] = (p8_ref[3:4, cs] * d0 + p8_ref[2:3, cs] * d1 + p8_ref[1:2, cs] * d2
                              + p8_ref[0:1, cs] * d3).astype(BF16)
            gp8_ref[3:4, cs] += _rowsum(d0 * xb)
            gp8_ref[2:3, cs] += _rowsum(d1 * xb)
            gp8_ref[1:2, cs] += _rowsum(d2 * xb)
            gp8_ref[0:1, cs] += _rowsum(d3 * xb)
        dxce_scr[tm:tm + SUBLANES, :] = dxce_scr[0:SUBLANES, :]

        dh1 = jnp.zeros((tm, D), F32)
        for k in range(NDEV):
            dh1 = dh1 + _dot_nt(dzb_ref[:, k * CB:(k + 1) * CB], win_ref[k])
        x1 = x1_ref[...]
        r1 = _rms(x1)
        dx1 = dx2 + _rms_bwd(dh1, x1, r1, nw_ref[...])
        dx1_ref[...] = dx1
        dx1b_ref[...] = dx1.astype(BF16)
        gnw_ref[...] += _rowsum(dh1 * x1 * r1)

    tile = lambda w: pl.BlockSpec((tm, w), lambda i: (nt - 1 - i, 0))
    whole = lambda *s: pl.BlockSpec(s, lambda i: (0,) * len(s))
    full = lambda: pltpu.VMEM((tm, BW), F32)
    ext = lambda: pltpu.VMEM((tm + SUBLANES, BW), F32)
    out, _ = _call(
        main, [], name="bwd_b", grid=(nt,),
        ins=[dx2, zb, hs, x1, *saved, nw, win8, p8, gcat, wout],
        in_specs=[tile(D), tile(2 * BW), tile(BW), tile(D)] + [tile(BW)] * 6 + [_VMEM] * 5,
        out_shape=[_sds((s_len, D), F32), _sds((s_len, D), BF16), _sds((s_len, 2 * BW), BF16), _sds((SUBLANES, BW), F32),
                   _sds((BH, HD, HD), F32), _sds((BH, HD, HD), F32), _sds((1, D), F32)],
        out_specs=[tile(D), tile(D), tile(2 * BW), whole(SUBLANES, BW), whole(BH, HD, HD), whole(BH, HD, HD),
                   whole(1, D)],
        scratch=[ext(), full(), full(), full(), full(), ext(), pltpu.VMEM((1, BW), F32), pltpu.VMEM((1, BW), F32)])
    return out


def _transpose_into(dst_ref, src_ref, rows):
    s_len = src_ref.shape[0]
    for r0 in range(0, s_len, rows):
        dst_ref[:, r0:r0 + rows] = src_ref[r0:r0 + rows, :].astype(F32).T.astype(BF16)


def _wgrad(a, b, jobs, *, by_rows, per, name, relay_step=0):
    s_len, m = a.shape
    n = b.shape[1]
    r, cd = (m // NDEV, n) if by_rows else (m, n // NDEV)
    nsteps = NDEV // per
    at_rows = per * r if by_rows else m

    def main(i, ins, outs, scr):
        a_ref, b_ref = ins
        q_ref, acc_ref = outs
        at_scr, stage, mine, land, send_sems, recv_sems = scr
        x, y, c = _place()

        def to_sibling(pi):
            return pltpu.make_async_remote_copy(
                src_ref=stage.at[pi & 1], dst_ref=land.at[pi], send_sem=send_sems.at[pi], recv_sem=recv_sems.at[pi],
                device_id=(x, y, 1 - c), device_id_type=MESH)

        if by_rows:
            _transpose_into(at_scr, a_ref, TRANSPOSE_ROWS)
        else:
            @pl.when(i == 0)
            def _():
                _transpose_into(at_scr, a_ref, TRANSPOSE_ROWS)

        res = _dot(at_scr[...], b_ref[...]).astype(BF16)
        for k in range(per):
            blk = per * i + k
            pi, pc = blk >> 1, blk & 1
            val = res[k * r:(k + 1) * r, :] if by_rows else res

            @pl.when(pc != c)
            def _():
                @pl.when(pi >= 2)
                def _():
                    to_sibling(pi - 2).wait_send()

                stage[pi & 1] = val
                to_sibling(pi).start()

            @pl.when(pc == c)
            def _():
                mine[pi] = val

        @pl.when(i == nsteps - 1)
        def _():
            for p in range(4):
                to_sibling(p).wait_recv()
            to_sibling(2).wait_send()
            to_sibling(3).wait_send()
            _chip_sums(mine, land, q_ref, acc_ref, x, y)

    if by_rows:
        in_specs = [pl.BlockSpec((s_len, at_rows), lambda j: (0, j)), _VMEM]
    else:
        in_specs = [_VMEM, pl.BlockSpec((s_len, cd), lambda j: (0, j))]
    blk_vmem = lambda k: pltpu.VMEM((k, r, cd), BF16)
    (q, acc), job_out = _call(
        main, jobs, name=name, grid=(nsteps,), relay_step=relay_step, ins=[a, b], in_specs=in_specs,
        out_shape=[_sds((NCHIP_OTHER, r, cd), BF16), _sds((r, cd), F32)],
        out_specs=[pl.BlockSpec((NCHIP_OTHER, r, cd), lambda j: (0, 0, 0)), pl.BlockSpec((r, cd), lambda j: (0, 0))],
        scratch=[pltpu.VMEM((at_rows, s_len), BF16), blk_vmem(2), blk_vmem(4), blk_vmem(4),
                 pltpu.SemaphoreType.DMA((4,)), pltpu.SemaphoreType.DMA((4,))])
    return q, acc, job_out


def _wgrad_cols_early(a, b, jobs, *, name, relay_step=0):
    s_len, m = a.shape
    r, cd = m, b.shape[1] // NDEV
    h = r // 2

    def chip_at(pos, base):
        return base ^ (3 - pos)

    def main(i, ins, outs, scr):
        a_ref, b_ref = ins
        q_ref, acc_ref, rel_ref = outs
        at_scr, stage, mine, land, q2_scr, send_sems, recv_sems, via_send, via_recv = scr
        x, y, c = _place()
        base = 2 * x + y
        xn, yn, _ = _other_chips(x, y)
        pos, pc = i >> 1, i & 1
        pi = chip_at(pos, base)

        def to_sibling(chip, slot):
            return pltpu.make_async_remote_copy(
                src_ref=stage.at[slot], dst_ref=land.at[chip], send_sem=send_sems.at[chip],
                recv_sem=recv_sems.at[chip], device_id=(x, y, 1 - c), device_id_type=MESH)

        def via(k):
            return pltpu.make_async_remote_copy(
                src_ref=q2_scr.at[pl.ds(k * h, h)], dst_ref=rel_ref.at[k], send_sem=via_send.at[k],
                recv_sem=via_recv.at[k], device_id=(*(xn, yn)[k], c), device_id_type=MESH)

        @pl.when(i == 0)
        def _():
            _transpose_into(at_scr, a_ref, TRANSPOSE_ROWS)

        res = _dot(at_scr[...], b_ref[...]).astype(BF16)

        @pl.when(pc != c)
        def _():
            @pl.when(pos >= 2)
            def _():
                to_sibling(chip_at(pos - 2, base), pos & 1).wait_send()

            stage[pos & 1] = res
            to_sibling(pi, pos & 1).start()

        @pl.when(pc == c)
        def _():
            mine[pi] = res

        @pl.when(i == 1)
        def _():
            dg = chip_at(0, base)
            to_sibling(dg, 0).wait_recv()
            q2 = (mine[dg].astype(F32) + land[dg].astype(F32)).astype(BF16)
            q2_scr[...] = q2
            q_ref[2] = q2
            via(0).start()
            via(1).start()

        @pl.when(i == NDEV - 1)
        def _():
            for pos_ in (1, 2, 3):
                to_sibling(chip_at(pos_, base), 0).wait_recv()
            to_sibling(chip_at(2, base), 0).wait_send()
            to_sibling(chip_at(3, base), 1).wait_send()
            for k in range(2):
                via(k).wait_recv()
            for k in range(2):
                via(k).wait_send()
            for j, chip in enumerate((base ^ 2, base ^ 1)):
                q_ref[j] = (mine[chip].astype(F32) + land[chip].astype(F32)).astype(BF16)
            acc_ref[...] = mine[base].astype(F32) + land[base].astype(F32)

    def b_block(j):
        base = 2 * lax.axis_index("x") + lax.axis_index("y")
        return (0, 2 * chip_at(j >> 1, base) + (j & 1))

    blk_vmem = lambda k: pltpu.VMEM((k, r, cd), BF16)
    (q, acc, rel), job_out = _call(
        main, jobs, name=name, grid=(NDEV,), relay_step=relay_step, ins=[a, b],
        in_specs=[_VMEM, pl.BlockSpec((s_len, cd), b_block)],
        out_shape=[_sds((NCHIP_OTHER, r, cd), BF16), _sds((r, cd), F32), _sds((2, h, cd), BF16)],
        out_specs=[pl.BlockSpec((NCHIP_OTHER, r, cd), lambda j: (0, 0, 0)), pl.BlockSpec((r, cd), lambda j: (0, 0)), _HBM],
        scratch=[pltpu.VMEM((m, s_len), BF16), blk_vmem(2), blk_vmem(4), blk_vmem(4), pltpu.VMEM((r, cd), BF16),
                 pltpu.SemaphoreType.DMA((4,)), pltpu.SemaphoreType.DMA((4,)), pltpu.SemaphoreType.DMA((2,)),
                 pltpu.SemaphoreType.DMA((2,))])
    return q, acc, rel, job_out


class _ExchangeRest:
    def __init__(self, q, relayed):
        _, r, cd = q.shape
        half = (2, r // 2, cd)
        self.ins, self.in_specs = [q, relayed], [_HBM, _HBM]
        self.out_shape, self.out_specs = [_sds((2, r, cd), q.dtype)], [_HBM]
        self.scratch = [pltpu.VMEM(half, q.dtype), pltpu.VMEM(half, q.dtype), pltpu.VMEM(half, q.dtype),
                        pltpu.SemaphoreType.DMA((4,)), pltpu.SemaphoreType.DMA((4,)), pltpu.SemaphoreType.DMA((4,))]

    def ops(self, ins, outs, scr):
        (q, rel_in), (land,) = ins, outs
        own, rel, comb, send_sems, recv_sems, local_sems = scr
        h = q.shape[1] // 2
        x, y, c = _place()
        xn, yn, _ = _other_chips(x, y)
        h0, h1 = pl.ds(0, h), pl.ds(h, h)

        def remote(k, src, dst, chip):
            return pltpu.make_async_remote_copy(src_ref=src, dst_ref=dst, send_sem=send_sems.at[k],
                                                recv_sem=recv_sems.at[k], device_id=(*chip, c), device_id_type=MESH)

        def sends():
            return [remote(0, q.at[0, h0], land.at[0, h0], xn), remote(1, q.at[1, h1], land.at[1, h1], yn),
                    remote(2, comb.at[0], land.at[1, h0], yn), remote(3, comb.at[1], land.at[0, h1], xn)]

        def loads():
            return [pltpu.make_async_copy(q.at[1, h0], own.at[0], local_sems.at[0]),
                    pltpu.make_async_copy(q.at[0, h1], own.at[1], local_sems.at[1]),
                    pltpu.make_async_copy(rel_in.at[0], rel.at[0], local_sems.at[2]),
                    pltpu.make_async_copy(rel_in.at[1], rel.at[1], local_sems.at[3])]

        def start():
            cps, lds = sends(), loads()
            for ld in lds:
                ld.start()
            cps[0].start()
            cps[1].start()
            for ld in lds:
                ld.wait()
            for k in range(2):
                comb[k] = (own[k].astype(F32) + rel[k].astype(F32)).astype(comb.dtype)
            cps[2].start()
            cps[3].start()

        def finish():
            cps = sends()
            for cp in cps:
                cp.wait_recv()
            for cp in cps:
                cp.wait_send()

        return start, lambda: None, finish


def _adam_math(w, g, m, v):
    m = B1 * m + (1.0 - B1) * g
    v = B2 * v + (1.0 - B2) * (g * g)
    m_hat = m / (1.0 - B1 ** STEP)
    v_hat = v / (1.0 - B2 ** STEP)
    delta = (-LR) * (m_hat / (jnp.sqrt(v_hat) + ADAM_EPS) + WD * w)
    return delta, m, v


def _adam_big(w, acc, land, m, v, name):
    r, cd = w.shape
    rb = ADAM_ROWS if r % ADAM_ROWS == 0 else r
    nland = land.shape[0]

    def body(w_ref, acc_ref, land_ref, m_ref, v_ref, g_ref, d_ref, mo_ref, vo_ref):
        g = acc_ref[...]
        for j in range(nland):
            g = g + land_ref[j].astype(F32)
        g_ref[...] = g
        d_ref[...], mo_ref[...], vo_ref[...] = _adam_math(w_ref[...], g, m_ref[...], v_ref[...])

    blk = pl.BlockSpec((rb, cd), lambda i: (i, 0))
    blk3 = pl.BlockSpec((nland, rb, cd), lambda i: (0, i, 0))
    return pl.pallas_call(
        body, name=name, grid=(r // rb,), in_specs=[blk, blk, blk3, blk, blk], out_specs=[blk] * 4,
        out_shape=[_sds((r, cd), F32)] * 4,
        compiler_params=_params(dimension_semantics=("arbitrary",)),
    )(w, acc, land, m, v)


def _adam_small(groups):
    n = len(groups)

    def body(*refs):
        ins, outs = refs[:4 * n], refs[4 * n:]
        for k in range(n):
            w_ref, g_ref, m_ref, v_ref = ins[4 * k:4 * k + 4]
            d, mo, vo = _adam_math(w_ref[...], g_ref[...], m_ref[...], v_ref[...])
            outs[3 * k][...] = d
            outs[3 * k + 1][...] = mo
            outs[3 * k + 2][...] = vo

    flat = [a for grp in groups for a in grp]
    shapes = [_sds(grp[0].shape, F32) for grp in groups for _ in range(3)]
    res = pl.pallas_call(
        body, name="adam_small", in_specs=[_VMEM] * (4 * n), out_specs=[_VMEM] * (3 * n), out_shape=shapes,
        compiler_params=_params(),
    )(*flat)
    return [tuple(res[3 * k:3 * k + 3]) for k in range(n)]


TM_FWD_A = 256
RELAY_STEP_FWD_A = 2
RELAY_STEP_FWD_B = 2
TM_BWD_A = 256
RELAY_STEP_BWD_A = 3
TM_BWD_A_IN = 256
RELAY_STEP_BWD_A_IN = 4
RELAY_STEP_WGRAD_A_IN = 2
TM_FWD_B = 256
TM_HEAD = 512
TM_BWD_B = 256


def _pack(parts, rows):
    flat = jnp.concatenate([p.reshape(-1) for p in parts])
    return jnp.pad(flat, (0, NDEV * rows * LANES - flat.shape[0])).reshape(NDEV, rows, LANES)


def _unpack(packed, shapes):
    flat, out, off = packed.reshape(-1), [], 0
    for s in shapes:
        size = 1
        for d in s:
            size *= d
        out.append(flat[off:off + size].reshape(s))
        off += size
    return out


def kernel(x, norm_w, a_w_in, a_ln_w, a_ln_b, a_w_s, a_b_s, a_w_out, b_w_in, b_conv_w, b_conv_b, b_gate_a_w, b_gate_a_b, b_gate_x_w, b_gate_x_b, b_lambda, b_w_out, norm_f_w, loss_target, m_norm_w, m_a_w_in, m_a_ln_w, m_a_ln_b, m_a_w_s, m_a_b_s, m_a_w_out, m_b_w_in, m_b_conv_w, m_b_conv_b, m_b_gate_a_w, m_b_gate_a_b, m_b_gate_x_w, m_b_gate_x_b, m_b_lambda, m_b_w_out, m_norm_f_w, v_norm_w, v_a_w_in, v_a_ln_w, v_a_ln_b, v_a_w_s, v_a_b_s, v_a_w_out, v_b_w_in, v_b_conv_w, v_b_conv_b, v_b_gate_a_w, v_b_gate_a_b, v_b_gate_x_w, v_b_gate_x_b, v_b_lambda, v_b_w_out, v_norm_f_w):
    me = 4 * lax.axis_index("x") + 2 * lax.axis_index("y") + lax.axis_index("c")
    xs, tgt = x[0], loss_target[0]
    nw0, nw1, nfw = norm_w[0:1], norm_w[1:2], norm_f_w.reshape(1, D)
    w_s, bst = a_w_s[0], a_b_s[0].T
    gcat = jnp.concatenate([b_gate_a_w[0], b_gate_x_w[0]], axis=-1).astype(BF16)

    p8_shard = jnp.concatenate([b_conv_w[0], b_conv_b, b_gate_a_b, b_gate_x_b, b_lambda], axis=0)
    (z, h0, ya, pp), ((win_a8, p8_all), (wout_a8, win_b8, wout_b8)) = _fwd_a(
        xs, nw0, a_ln_w, a_ln_b, w_s, bst,
        [_Gather([a_w_in[0], p8_shard], [BF16, F32]),
         _Gather([a_w_out[0], b_w_in[0], b_w_out[0]], [BF16, BF16, BF16])],
        tm=TM_FWD_A, relay_step=RELAY_STEP_FWD_A)
    p8 = jnp.transpose(p8_all, (1, 0, 2)).reshape(SUBLANES, BW)
    wout_a = wout_a8.reshape(AW, D)
    (x1, zb, hs, h1, yb, *saved_b), _ = _fwd_b(xs, ya, wout_a, nw1, win_b8, p8, gcat, [], tm=TM_FWD_B,
                                               relay_step=RELAY_STEP_FWD_B)
    wout_b = wout_b8.reshape(BW, D)
    dx2, dx2b, loss, g_nfw = _head(x1, yb, wout_b, nfw, tgt, tm=TM_HEAD)

    dx1, dx1b, dzb, g_p8, g_ga, g_gx, g_nw1 = _bwd_b(dx2, zb, hs, x1, saved_b, nw1, win_b8, p8, gcat, wout_b,
                                                     tm=TM_BWD_B)
    q_wout_b, acc_wout_b, _ = _wgrad(yb, dx2b, [], by_rows=True, per=2, name="wgrad_b_out")
    shapes_b = [(1, D), (1, D), (SUBLANES, BW), (1, 1)]
    pack_b = _pack([g_nfw, g_nw1, g_p8, loss], 16)
    small_b = _InChip([g_ga.reshape(NDEV, -1, HD), g_gx.reshape(NDEV, -1, HD), pack_b])
    q_win_b, acc_win_b, (sm_b, (l_wout_b,)) = _wgrad(h1, dzb, [small_b, _Exchange([q_wout_b])], by_rows=False, per=1,
                                                      name="wgrad_b_in")
    qs_b, accs_b = sm_b[:3], sm_b[3:]

    (dz, g_lnw, g_lnb, g_ws, g_bst), (lands_b, (l_win_b,)) = _bwd_a(
        dx1b, z, pp, a_ln_w, a_ln_b, w_s, bst, wout_a, [_Exchange(qs_b), _ExchangeVia(q_win_b)],
        tm=TM_BWD_A, relay_step=RELAY_STEP_BWD_A)
    shapes_a = [(1, AW), (1, AW), (CH, G)]
    pack_a = _pack([g_lnw, g_lnb, g_bst], 8)
    q_wout_a, acc_wout_a, (red_b, sm_a) = _wgrad(
        ya, dx1b, [_SumGather(accs_b, lands_b), _InChip([g_ws, pack_a])], by_rows=True, per=2,
        name="wgrad_a_out", relay_step=1)
    qs_a, accs_a = sm_a[:2], sm_a[2:]
    q_win_a, acc_win_a, rel_a, (lands_a, (l_wout_a,)) = _wgrad_cols_early(
        h0, dz, [_Exchange(qs_a), _ExchangeVia(q_wout_a)], name="wgrad_a_in", relay_step=RELAY_STEP_WGRAD_A_IN)
    (gx, g_nw0), (red_a, (l_win_a,)) = _bwd_a_in(
        dz, dx1, xs, nw0, win_a8, [_SumGather(accs_a, lands_a), _ExchangeRest(q_win_a, rel_a)],
        tm=TM_BWD_A_IN, relay_step=RELAY_STEP_BWD_A_IN)

    r_ga, r_gx, r_pack_b = red_b
    r_nfw, r_nw1, r_p8, loss = _unpack(r_pack_b, shapes_b)
    r_ws, r_pack_a = red_a
    r_lnw, r_lnb, r_bst = _unpack(r_pack_a, shapes_a)
    g_p8 = lax.dynamic_slice_in_dim(r_p8, me * (BW // NDEV), BW // NDEV, axis=1)
    loss = loss[0, 0]

    weights = dict(norm_w=norm_w, a_w_in=a_w_in, a_ln_w=a_ln_w, a_ln_b=a_ln_b, a_w_s=a_w_s, a_b_s=a_b_s, a_w_out=a_w_out,
                   b_w_in=b_w_in, b_conv_w=b_conv_w, b_conv_b=b_conv_b, b_gate_a_w=b_gate_a_w, b_gate_a_b=b_gate_a_b,
                   b_gate_x_w=b_gate_x_w, b_gate_x_b=b_gate_x_b, b_lambda=b_lambda, b_w_out=b_w_out, norm_f_w=norm_f_w)
    mom1 = dict(norm_w=m_norm_w, a_w_in=m_a_w_in, a_ln_w=m_a_ln_w, a_ln_b=m_a_ln_b, a_w_s=m_a_w_s, a_b_s=m_a_b_s,
                a_w_out=m_a_w_out, b_w_in=m_b_w_in, b_conv_w=m_b_conv_w, b_conv_b=m_b_conv_b, b_gate_a_w=m_b_gate_a_w,
                b_gate_a_b=m_b_gate_a_b, b_gate_x_w=m_b_gate_x_w, b_gate_x_b=m_b_gate_x_b, b_lambda=m_b_lambda,
                b_w_out=m_b_w_out, norm_f_w=m_norm_f_w)
    mom2 = dict(norm_w=v_norm_w, a_w_in=v_a_w_in, a_ln_w=v_a_ln_w, a_ln_b=v_a_ln_b, a_w_s=v_a_w_s, a_b_s=v_a_b_s,
                a_w_out=v_a_w_out, b_w_in=v_b_w_in, b_conv_w=v_b_conv_w, b_conv_b=v_b_conv_b, b_gate_a_w=v_b_gate_a_w,
                b_gate_a_b=v_b_gate_a_b, b_gate_x_w=v_b_gate_x_w, b_gate_x_b=v_b_gate_x_b, b_lambda=v_b_lambda,
                b_w_out=v_b_w_out, norm_f_w=v_norm_f_w)
    names = list(weights)

    def as2d(a):
        return a.reshape(-1, a.shape[-1])

    upd, grads = {}, {}
    for k, acc, land in (("a_w_in", acc_win_a, l_win_a), ("a_w_out", acc_wout_a, l_wout_a),
                         ("b_w_in", acc_win_b, l_win_b), ("b_w_out", acc_wout_b, l_wout_b)):
        g, d, mo, vo = _adam_big(as2d(weights[k]), acc, land, as2d(mom1[k]), as2d(mom2[k]), "adam_" + k)
        grads[k] = g[None]
        upd[k] = (d, mo, vo)
    grads.update(
        norm_w=jnp.concatenate([g_nw0, r_nw1], axis=0), a_ln_w=r_lnw, a_ln_b=r_lnb,
        a_w_s=r_ws.reshape(1, G, CH, CH), a_b_s=r_bst.T[None],
        b_conv_w=g_p8[None, 0:4], b_conv_b=g_p8[4:5], b_gate_a_w=r_ga.reshape(1, BH, HD, HD), b_gate_a_b=g_p8[5:6],
        b_gate_x_w=r_gx.reshape(1, BH, HD, HD), b_gate_x_b=g_p8[6:7], b_lambda=g_p8[7:8], norm_f_w=r_nfw.reshape(D))
    small_names = [k for k in names if k not in upd]
    res = _adam_small([(as2d(weights[k]), as2d(grads[k]), as2d(mom1[k]), as2d(mom2[k])) for k in small_names])
    for k, r3 in zip(small_names, res):
        upd[k] = r3
    deltas = [upd[k][0].reshape(weights[k].shape) for k in names]
    new_m = [upd[k][1].reshape(weights[k].shape) for k in names]
    new_v = [upd[k][2].reshape(weights[k].shape) for k in names]
    return (loss, gx[None], *[grads[k] for k in names], *deltas, *new_m, *new_v)
```

```python
import jax
import jax.numpy as jnp
from jax import lax
from jax.experimental import pallas as pl
from jax.experimental.pallas import tpu as pltpu

F32 = jnp.float32
BF16 = jnp.bfloat16
MESH = pl.DeviceIdType.MESH

NDEV = 8
NCHIP_OTHER = 3
D = 1024
AW = 2048
G = 8
GD = AW // G
CH = 128
BW = 1536
BH = 12
HD = BW // BH
CA = 3 * AW // NDEV
CB = 2 * BW // NDEV
RMS_EPS = 1e-6
LN_EPS = 1e-5
RG_C = 8.0
LR, B1, B2, ADAM_EPS, WD, STEP = 0.001, 0.9, 0.999, 1e-08, 0.01, 10
V7X_VMEM_BYTES = 64 * 1024 * 1024
VMEM_LIMIT = V7X_VMEM_BYTES - 8 * 1024 * 1024
SUBLANES = 8
LANES = 128
BF16_ROWS = 16
TRANSPOSE_ROWS = 256
ADAM_ROWS = 512
GELU_C = 0.7978845608028654
GELU_K = 0.044715

_VMEM = pl.BlockSpec(memory_space=pltpu.VMEM)
_HBM = pl.BlockSpec(memory_space=pltpu.HBM)


def _sds(shape, dtype):
    return jax.ShapeDtypeStruct(tuple(shape), dtype)


def _params(**kw):
    return pltpu.CompilerParams(vmem_limit_bytes=VMEM_LIMIT, **kw)


def _gelu_t(z):
    p = 0.5 * jnp.tanh(z * (GELU_C + (GELU_C * GELU_K) * (z * z))) + 0.5
    return z * p, p


def _dgelu(z, p):
    return p * (1.0 + (z * (1.0 - p)) * (2.0 * GELU_C + (6.0 * GELU_C * GELU_K) * (z * z)))


def _sigmoid(v):
    return 0.5 * jnp.tanh(0.5 * v) + 0.5


def _softplus_neg(lam):
    return jnp.maximum(-lam, 0.0) + jnp.log1p(jnp.exp(-jnp.abs(lam)))


def _dot(a, b):
    return jnp.dot(a, b, preferred_element_type=F32)


def _dot_nt(a, b):
    return lax.dot_general(a, b, (((1,), (1,)), ((), ())), preferred_element_type=F32)


def _rowsum(v):
    return jnp.sum(v, axis=0, keepdims=True)


def _causal_mask():
    r = lax.broadcasted_iota(jnp.int32, (CH, CH), 0)
    c = lax.broadcasted_iota(jnp.int32, (CH, CH), 1)
    return r >= c


def _rms(x):
    return lax.rsqrt(jnp.mean(x * x, axis=-1, keepdims=True) + RMS_EPS)


def _rms_bwd(dh, x, r, nw):
    gy = dh * nw
    return r * gy - x * (r * r * r) * jnp.mean(gy * x, axis=-1, keepdims=True)


def _place():
    return lax.axis_index("x"), lax.axis_index("y"), lax.axis_index("c")


def _other_chips(x, y):
    return [(1 - x, y), (x, 1 - y), (1 - x, 1 - y)]


GATHER_SLOTS = 10


def _gather_ops(ins, outs, send_sems, recv_sems, local_sems):
    n = len(ins)
    x, y, c = _place()
    sibling = (x, y, 1 - c)
    xn, yn, dg = _other_chips(x, y)
    split = [ins[i].shape[0] % (2 * BF16_ROWS) == 0 for i in range(n)]

    def blk(chip, core):
        return 4 * chip[0] + 2 * chip[1] + core

    me = blk((x, y), c)

    def part(ref, i, half):
        if half is None:
            return ref
        h = ins[i].shape[0] // 2
        return ref.at[pl.ds(half * h, h)]

    def copy(i, k, block, to, half=None, src=None):
        dst = part(outs[i].at[block], i, half)
        return pltpu.make_async_remote_copy(
            src_ref=dst if src is None else part(src, i, half), dst_ref=dst,
            send_sem=send_sems.at[k, i], recv_sem=recv_sems.at[k, i], device_id=to, device_id_type=MESH)

    def first_copies():
        mine = [pltpu.make_async_copy(ins[i], outs[i].at[me], local_sems.at[i]) for i in range(n)]
        first = []
        for i in range(n):
            first.append(copy(i, 0, me, sibling, src=ins[i]))
            if split[i]:
                first.append(copy(i, 1, me, (*xn, c), 0, ins[i]))
                first.append(copy(i, 3, me, (*yn, c), 1, ins[i]))
                first.append(copy(i, 2, me, (*xn, c), 1, ins[i]))
                first.append(copy(i, 4, me, (*yn, c), 0, ins[i]))
            else:
                first.append(copy(i, 1, me, (*xn, c), None, ins[i]))
                first.append(copy(i, 3, me, (*yn, c), None, ins[i]))
                first.append(copy(i, 5, me, (*dg, c), None, ins[i]))
        return mine, first

    def onward():
        out = []
        for i in range(n):
            if split[i]:
                out.append(copy(i, 5, blk(xn, c), (*yn, c), 0))
                out.append(copy(i, 6, blk(yn, c), (*xn, c), 1))
        return out

    def start():
        mine, first = first_copies()
        for cp in mine + first:
            cp.start()

    def relay():
        sends = onward()
        for i in range(n):
            if split[i]:
                copy(i, 1, blk(xn, c), sibling, 0).wait_recv()
                sends.pop(0).start()
                copy(i, 3, blk(yn, c), sibling, 1).wait_recv()
                sends.pop(0).start()

    def passes():
        return [copy(i, 7 + j, blk(chip, c), sibling) for i in range(n) for j, chip in enumerate((xn, yn, dg))]

    def forward():
        fwd = passes()
        for i in range(n):
            if split[i]:
                copy(i, 2, blk(xn, c), sibling, 1).wait_recv()
                fwd[3 * i].start()
                copy(i, 4, blk(yn, c), sibling, 0).wait_recv()
                fwd[3 * i + 1].start()
                copy(i, 5, blk(dg, c), sibling, 0).wait_recv()
                copy(i, 6, blk(dg, c), sibling, 1).wait_recv()
                fwd[3 * i + 2].start()
            else:
                copy(i, 1, blk(xn, c), sibling).wait_recv()
                fwd[3 * i].start()
                copy(i, 3, blk(yn, c), sibling).wait_recv()
                fwd[3 * i + 1].start()
                copy(i, 5, blk(dg, c), sibling).wait_recv()
                fwd[3 * i + 2].start()

    def finish():
        mine, first = first_copies()
        for i in range(n):
            copy(i, 0, blk((x, y), 1 - c), sibling).wait_recv()
            for j, chip in enumerate((xn, yn, dg)):
                copy(i, 7 + j, blk(chip, 1 - c), sibling).wait_recv()
        for cp in first + passes() + onward():
            cp.wait_send()
        for cp in mine:
            cp.wait()

    return start, relay, forward, finish


def _gather_sems(n):
    return [pltpu.SemaphoreType.DMA((GATHER_SLOTS, n)), pltpu.SemaphoreType.DMA((GATHER_SLOTS, n)),
            pltpu.SemaphoreType.DMA((n,))]


class _Gather:
    def __init__(self, shards, as_dtypes=None):
        n = len(shards)
        dts = [s.dtype for s in shards] if as_dtypes is None else list(as_dtypes)
        self.cast = [jnp.dtype(d) != s.dtype for d, s in zip(dts, shards)]
        self.ins = list(shards)
        self.in_specs = [_VMEM if c else _HBM for c in self.cast]
        self.out_shape = [_sds((NDEV,) + s.shape, d) for s, d in zip(shards, dts)]
        self.out_specs = [_HBM] * n
        self.scratch = [pltpu.VMEM(s.shape, d) for s, d, c in zip(shards, dts, self.cast) if c] + _gather_sems(n)

    def ops(self, ins, outs, scr):
        ncast = sum(self.cast)
        staged = iter(scr[:ncast])
        srcs = [next(staged) if c else ref for c, ref in zip(self.cast, ins)]
        start, relay, forward, finish = _gather_ops(srcs, outs, *scr[ncast:])

        def cast_and_start():
            for c, ref, src in zip(self.cast, ins, srcs):
                if c:
                    src[...] = ref[...].astype(src.dtype)
            start()

        return cast_and_start, relay, forward, finish


class _Exchange:
    def __init__(self, qs):
        n = len(qs)
        self.ins, self.in_specs = list(qs), [_HBM] * n
        self.out_shape = [_sds(q.shape, q.dtype) for q in qs]
        self.out_specs = [_HBM] * n
        self.scratch = [pltpu.SemaphoreType.DMA((NCHIP_OTHER, n)), pltpu.SemaphoreType.DMA((NCHIP_OTHER, n))]

    def ops(self, ins, outs, scr):
        send_sems, recv_sems = scr
        n = len(ins)
        x, y, c = _place()
        chips = _other_chips(x, y)

        def copies():
            return [pltpu.make_async_remote_copy(
                src_ref=ins[i].at[j], dst_ref=outs[i].at[j], send_sem=send_sems.at[j, i],
                recv_sem=recv_sems.at[j, i], device_id=(*chips[j], c), device_id_type=MESH)
                for i in range(n) for j in range(NCHIP_OTHER)]

        def start():
            for cp in copies():
                cp.start()

        def finish():
            cps = copies()
            for cp in cps:
                cp.wait_recv()
            for cp in cps:
                cp.wait_send()

        return start, lambda: None, finish


class _ExchangeVia:
    def __init__(self, q):
        _, r, cd = q.shape
        half = (2, r // 2, cd)
        self.ins, self.in_specs = [q], [_HBM]
        self.out_shape, self.out_specs = [_sds((2, r, cd), q.dtype)], [_HBM]
        self.scratch = [pltpu.VMEM(half, q.dtype), pltpu.VMEM(half, q.dtype), pltpu.VMEM(half, q.dtype),
                        pltpu.SemaphoreType.DMA((6,)), pltpu.SemaphoreType.DMA((6,)), pltpu.SemaphoreType.DMA((2,))]

    def ops(self, ins, outs, scr):
        (q,), (land,) = ins, outs
        relayed, own, comb, send_sems, recv_sems, local_sems = scr
        h = q.shape[1] // 2
        x, y, c = _place()
        xn, yn, _ = _other_chips(x, y)
        h0, h1 = pl.ds(0, h), pl.ds(h, h)

        def remote(k, src, dst, chip):
            return pltpu.make_async_remote_copy(src_ref=src, dst_ref=dst, send_sem=send_sems.at[k],
                                                recv_sem=recv_sems.at[k], device_id=(*chip, c), device_id_type=MESH)

        def via():
            return [remote(2, q.at[2, h0], relayed.at[0], xn), remote(3, q.at[2, h1], relayed.at[1], yn)]

        def direct():
            return [remote(0, q.at[0, h0], land.at[0, h0], xn), remote(1, q.at[1, h1], land.at[1, h1], yn)]

        def second():
            return [remote(4, comb.at[0], land.at[1, h0], yn), remote(5, comb.at[1], land.at[0, h1], xn)]

        def mine():
            return [pltpu.make_async_copy(q.at[1, h0], own.at[0], local_sems.at[0]),
                    pltpu.make_async_copy(q.at[0, h1], own.at[1], local_sems.at[1])]

        def start():
            for cp in via() + direct() + mine():
                cp.start()

        def relay():
            arrived, loaded, onward = via(), mine(), second()
            for k in range(2):
                arrived[k].wait_recv()
                loaded[k].wait()
                comb[k] = (own[k].astype(F32) + relayed[k].astype(F32)).astype(comb.dtype)
                onward[k].start()

        def finish():
            landing = direct() + second()
            for cp in landing:
                cp.wait_recv()
            for cp in via() + landing:
                cp.wait_send()

        return start, relay, finish


class _SumGather:
    def __init__(self, accs, lands):
        n = len(accs)
        self.n = n
        self.ins, self.in_specs = list(accs) + list(lands), [_VMEM] * (2 * n)
        self.out_shape = [_sds((NDEV,) + a.shape, a.dtype) for a in accs]
        self.out_specs = [_HBM] * n
        self.scratch = [pltpu.VMEM(a.shape, a.dtype) for a in accs] + _gather_sems(n)

    def ops(self, ins, outs, scr):
        n = self.n
        accs, lands, mine = ins[:n], ins[n:], scr[:n]
        g_start, relay, forward, finish = _gather_ops(mine, outs, *scr[n:])

        def start():
            for i in range(n):
                mine[i][...] = accs[i][...] + lands[i][0] + lands[i][1] + lands[i][2]
            g_start()

        return start, relay, forward, finish


def _call(main, jobs, *, name, grid, ins, in_specs, out_shape, out_specs, scratch, relay_step=0, first=0,
          prologue=None):
    nsteps = grid[0] if grid else 1
    n_in, n_out, n_scr = len(ins), len(out_shape), len(scratch)

    def body(*refs):
        pos = [0]

        def take(k):
            r = refs[pos[0]:pos[0] + k]
            pos[0] += k
            return r

        m_in = take(n_in)
        j_in = [take(len(j.ins)) for j in jobs]
        m_out = take(n_out)
        j_out = [take(len(j.out_shape)) for j in jobs]
        m_scr = take(n_scr)
        j_scr = [take(len(j.scratch)) for j in jobs]
        ops = [_four(j.ops(a, b, s)) for j, a, b, s in zip(jobs, j_in, j_out, j_scr)]
        i = pl.program_id(0) if grid else 0
        if not grid:
            for stage in range(4):
                for o in ops:
                    o[stage]()
                if stage == 0:
                    main(i, m_in, m_out, m_scr)
            return

        if ops:
            @pl.when(i == 0)
            def _():
                for o in ops[:first]:
                    o[0]()
                for o in ops[:first]:
                    o[1]()
                for o in ops[first:]:
                    o[0]()
                for o in ops[:first]:
                    o[2]()
                for o in ops[:first]:
                    o[3]()
                if prologue is not None:
                    prologue(j_out[:first], m_scr)

        main(i, m_in, m_out, m_scr)

        for stage, at in ((1, min(relay_step, nsteps - 1)), (2, max(relay_step, nsteps - 2)), (3, nsteps - 1)):
            if ops[first:]:
                @pl.when(i == at)
                def _():
                    for o in ops[first:]:
                        o[stage]()

    extra = dict(dimension_semantics=("arbitrary",)) if grid else {}
    res = pl.pallas_call(
        body, name=name, grid=grid,
        in_specs=list(in_specs) + [s for j in jobs for s in j.in_specs],
        out_specs=list(out_specs) + [s for j in jobs for s in j.out_specs],
        out_shape=list(out_shape) + [s for j in jobs for s in j.out_shape],
        scratch_shapes=list(scratch) + [s for j in jobs for s in j.scratch],
        compiler_params=_params(**extra),
    )(*ins, *[a for j in jobs for a in j.ins])
    main_out, rest, job_out = res[:n_out], res[n_out:], []
    for j in jobs:
        k = len(j.out_shape)
        job_out.append(rest[:k])
        rest = rest[k:]
    return main_out, job_out


def _four(ops):
    return ops if len(ops) == 4 else (ops[0], ops[1], lambda: None, ops[2])


def _comm_only(jobs, name):
    _, job_out = _call(lambda i, a, b, s: None, jobs, name=name, grid=(), ins=[], in_specs=[], out_shape=[],
                       out_specs=[], scratch=[])
    return job_out


class _InChip:
    def __init__(self, ps):
        n = len(ps)
        self.n = n
        blk = [p.shape[1:] for p in ps]
        self.ins, self.in_specs = list(ps), [_HBM] * n
        self.out_shape = [_sds((NCHIP_OTHER,) + b, p.dtype) for b, p in zip(blk, ps)] + [_sds(b, F32) for b in blk]
        self.out_specs = [_VMEM] * (2 * n)
        self.scratch = ([pltpu.VMEM((4,) + b, p.dtype) for b, p in zip(blk, ps)] * 2
                        + [pltpu.SemaphoreType.DMA((4, n))] * 3)

    def ops(self, ins, outs, scr):
        n = self.n
        q_refs, acc_refs = outs[:n], outs[n:]
        mines, lands = scr[:n], scr[n:2 * n]
        send_sems, recv_sems, local_sems = scr[2 * n:]
        x, y, c = _place()
        sibling = (x, y, 1 - c)

        def copies():
            out = []
            for i in range(n):
                for pi in range(4):
                    loc = pltpu.make_async_copy(ins[i].at[2 * pi + c], mines[i].at[pi], local_sems.at[pi, i])
                    cp = pltpu.make_async_remote_copy(
                        src_ref=ins[i].at[2 * pi + (1 - c)], dst_ref=lands[i].at[pi],
                        send_sem=send_sems.at[pi, i], recv_sem=recv_sems.at[pi, i],
                        device_id=sibling, device_id_type=MESH)
                    out.append((loc, cp))
            return out

        def start():
            for loc, cp in copies():
                loc.start()
                cp.start()

        def finish():
            pairs = copies()
            for loc, cp in pairs:
                loc.wait()
                cp.wait_recv()
            for i in range(n):
                _chip_sums(mines[i], lands[i], q_refs[i], acc_refs[i], x, y)
            for _, cp in pairs:
                cp.wait_send()

        return start, lambda: None, finish


def _chip_sums(mine, land, q_ref, acc_ref, x, y):
    for j, (qx, qy) in enumerate(_other_chips(x, y)):
        qi = 2 * qx + qy
        q_ref[j] = (mine[qi].astype(F32) + land[qi].astype(F32)).astype(q_ref.dtype)
    mi = 2 * x + y
    acc_ref[...] = mine[mi].astype(F32) + land[mi].astype(F32)


def _direct_sum(v, buf, send_sems, recv_sems):
    x, y, c = _place()
    me = 4 * x + 2 * y + c
    buf[me] = v
    cps = []
    for k in range(1, NDEV):
        fx, fy, fc = (k >> 2) & 1, (k >> 1) & 1, k & 1
        peer = ((1 - x) if fx else x, (1 - y) if fy else y, (1 - c) if fc else c)
        cps.append((peer, pltpu.make_async_remote_copy(
            src_ref=buf.at[me], dst_ref=buf.at[me], send_sem=send_sems.at[k - 1], recv_sem=recv_sems.at[k - 1],
            device_id=peer, device_id_type=MESH)))
    for _, cp in cps:
        cp.start()
    for k, (peer, _) in enumerate(cps):
        theirs = 4 * peer[0] + 2 * peer[1] + peer[2]
        pltpu.make_async_remote_copy(
            src_ref=buf.at[theirs], dst_ref=buf.at[theirs], send_sem=send_sems.at[k], recv_sem=recv_sems.at[k],
            device_id=peer, device_id_type=MESH).wait_recv()
    acc = buf[0]
    for j in range(1, NDEV):
        acc = acc + buf[j]
    for _, cp in cps:
        cp.wait_send()
    return acc


def _direct_sum_scratch(shape, dtype):
    return [pltpu.VMEM((NDEV,) + tuple(shape), dtype), pltpu.SemaphoreType.DMA((NDEV - 1,)),
            pltpu.SemaphoreType.DMA((NDEV - 1,))]


def _fwd_a(x, nw, lnw, lnb, ws, bst, jobs, *, tm, relay_step):
    s_len = x.shape[0]
    nt = s_len // tm
    nch = tm // CH

    def main(i, ins, outs, scr):
        x_ref, nw_ref, lnw_ref, lnb_ref, ws_ref, bst_ref = ins
        z_ref, h_ref, y_ref, pp_ref = outs
        wc_scr, gv_scr, win_ref = scr

        @pl.when(i == 0)
        def _():
            m = _causal_mask()
            for g in range(G):
                wc_scr[g] = jnp.where(m, ws_ref[g], 0.0).astype(BF16)

        x = x_ref[...]
        h = (x * _rms(x) * nw_ref[...]).astype(BF16)
        h_ref[...] = h
        for k in range(NDEV):
            z_ref[:, k * CA:(k + 1) * CA] = _dot(h, win_ref[k])

        ssum = jnp.zeros((tm, 1), F32)
        for g in range(G):
            vs = slice(AW + g * GD, AW + (g + 1) * GD)
            gv, pv = _gelu_t(z_ref[:, vs])
            pp_ref[:, vs] = pv.astype(BF16)
            gv_scr[:, g * GD:(g + 1) * GD] = gv
            ssum = ssum + jnp.sum(gv, axis=-1, keepdims=True)
        mu = ssum * (1.0 / AW)
        vsum = jnp.zeros((tm, 1), F32)
        for g in range(G):
            dlt = gv_scr[:, g * GD:(g + 1) * GD] - mu
            vsum = vsum + jnp.sum(dlt * dlt, axis=-1, keepdims=True)
        rstd = lax.rsqrt(vsum * (1.0 / AW) + LN_EPS)

        for g in range(G):
            cs = slice(g * GD, (g + 1) * GD)
            gs = slice(2 * AW + g * GD, 2 * AW + (g + 1) * GD)
            v = (gv_scr[:, cs] - mu) * rstd * lnw_ref[:, cs] + lnb_ref[:, cs]
            vb = v.astype(BF16)
            u, pu = _gelu_t(z_ref[:, cs])
            pp_ref[:, cs] = pu.astype(BF16)
            zg = z_ref[:, gs]
            sig = _sigmoid(zg)
            pp_ref[:, gs] = sig.astype(BF16)
            sg = zg * sig
            for n in range(nch):
                rs = slice(n * CH, (n + 1) * CH)
                s = _dot(wc_scr[g], vb[rs, :]) + bst_ref[:, g:g + 1]
                y_ref[rs, cs] = (u[rs, :] * s * sg[rs, :]).astype(BF16)

    tile = lambda w: pl.BlockSpec((tm, w), lambda i: (i, 0))
    return _call(
        main, jobs, name="fwd_a", grid=(nt,), relay_step=relay_step, first=1,
        prologue=lambda gathered, scr: pltpu.sync_copy(gathered[0][0], scr[2]),
        ins=[x, nw, lnw, lnb, ws, bst], in_specs=[tile(D), _VMEM, _VMEM, _VMEM, _VMEM, _VMEM],
        out_shape=[_sds((s_len, 3 * AW), F32), _sds((s_len, D), BF16), _sds((s_len, AW), BF16),
                   _sds((s_len, 3 * AW), BF16)],
        out_specs=[tile(3 * AW), tile(D), tile(AW), tile(3 * AW)],
        scratch=[pltpu.VMEM((G, CH, CH), BF16), pltpu.VMEM((tm, AW), F32), pltpu.VMEM((NDEV, D, CA), BF16)])


def _bwd_a(dx1, z, pp, lnw, lnb, ws, bst, wout, jobs, *, tm, relay_step):
    s_len = dx1.shape[0]
    nt = s_len // tm
    nch = tm // CH

    def main(i, ins, outs, scr):
        dx1_ref, z_ref, pp_ref, lnw_ref, lnb_ref, ws_ref, bst_ref, wout_ref = ins
        dz_ref, glnw_ref, glnb_ref, gws_ref, gbst_ref = outs
        wc_scr, wct_scr, vh_scr, dgv_scr, dy_scr, dv_scr, gbs_acc, gwc_acc = scr

        @pl.when(i == 0)
        def _():
            m = _causal_mask()
            for g in range(G):
                wm = jnp.where(m, ws_ref[g], 0.0)
                wc_scr[g] = wm.astype(BF16)
                wct_scr[g] = wm.T.astype(BF16)
            glnw_ref[...] = jnp.zeros_like(glnw_ref)
            glnb_ref[...] = jnp.zeros_like(glnb_ref)
            gbs_acc[...] = jnp.zeros_like(gbs_acc)
            gwc_acc[...] = jnp.zeros_like(gwc_acc)

        dy_scr[...] = _dot_nt(dx1_ref[...], wout_ref[...])

        ssum = jnp.zeros((tm, 1), F32)
        for g in range(G):
            cs = slice(g * GD, (g + 1) * GD)
            vs = slice(AW + g * GD, AW + (g + 1) * GD)
            zv = z_ref[:, vs]
            pv = pp_ref[:, vs].astype(F32)
            gv = zv * pv
            vh_scr[:, cs] = gv
            dgv_scr[:, cs] = _dgelu(zv, pv)
            ssum = ssum + jnp.sum(gv, axis=-1, keepdims=True)
        mu = ssum * (1.0 / AW)
        vsum = jnp.zeros((tm, 1), F32)
        for g in range(G):
            dlt = vh_scr[:, g * GD:(g + 1) * GD] - mu
            vsum = vsum + jnp.sum(dlt * dlt, axis=-1, keepdims=True)
        rstd = lax.rsqrt(vsum * (1.0 / AW) + LN_EPS)

        m1 = jnp.zeros((tm, 1), F32)
        m2 = jnp.zeros((tm, 1), F32)
        for g in range(G):
            cs = slice(g * GD, (g + 1) * GD)
            gs = slice(2 * AW + g * GD, 2 * AW + (g + 1) * GD)
            vhat = (vh_scr[:, cs] - mu) * rstd
            vh_scr[:, cs] = vhat
            vb = (vhat * lnw_ref[:, cs] + lnb_ref[:, cs]).astype(BF16)
            zu = z_ref[:, cs]
            tu = pp_ref[:, cs].astype(F32)
            u = zu * tu
            zg = z_ref[:, gs]
            sig = pp_ref[:, gs].astype(F32)
            sg = zg * sig
            dy = dy_scr[:, cs]
            dsf = dy * u * sg
            dsb = dsf.astype(BF16)
            dvs = []
            for n in range(nch):
                rs = slice(n * CH, (n + 1) * CH)
                s = _dot(wc_scr[g], vb[rs, :]) + bst_ref[:, g:g + 1]
                dys = dy[rs, :] * s
                dz_ref[rs, cs] = (dys * sg[rs, :] * _dgelu(zu[rs, :], tu[rs, :])).astype(BF16)
                dz_ref[rs, gs] = (dys * u[rs, :] * (sig[rs, :] * (1.0 + zg[rs, :] * (1.0 - sig[rs, :])))).astype(BF16)
                gbs_acc[g] += dsf[rs, :]
                gwc_acc[g] += _dot_nt(dsb[rs, :], vb[rs, :])
                dvs.append(_dot(wct_scr[g], dsb[rs, :]))
            dv = jnp.concatenate(dvs, axis=0) if nch > 1 else dvs[0]
            glnw_ref[:, cs] += _rowsum(dv * vhat)
            glnb_ref[:, cs] += _rowsum(dv)
            dvh = dv * lnw_ref[:, cs]
            dv_scr[:, cs] = dvh
            m1 = m1 + jnp.sum(dvh, axis=-1, keepdims=True)
            m2 = m2 + jnp.sum(dvh * vhat, axis=-1, keepdims=True)
        m1 = m1 * (1.0 / AW)
        m2 = m2 * (1.0 / AW)
        for g in range(G):
            cs = slice(g * GD, (g + 1) * GD)
            dgv = rstd * (dv_scr[:, cs] - m1 - vh_scr[:, cs] * m2)
            dz_ref[:, AW + g * GD:AW + (g + 1) * GD] = (dgv * dgv_scr[:, cs]).astype(BF16)

        @pl.when(i == nt - 1)
        def _():
            m = _causal_mask()
            for g in range(G):
                gws_ref[g] = jnp.where(m, gwc_acc[g], 0.0)
                gbst_ref[:, g:g + 1] = jnp.sum(gbs_acc[g], axis=-1, keepdims=True)

    tile = lambda w: pl.BlockSpec((tm, w), lambda i: (i, 0))
    whole = lambda *s: pl.BlockSpec(s, lambda i: (0,) * len(s))
    big = lambda dt: pltpu.VMEM((tm, AW), dt)
    return _call(
        main, jobs, name="bwd_a", grid=(nt,), relay_step=relay_step,
        ins=[dx1, z, pp, lnw, lnb, ws, bst, wout],
        in_specs=[tile(D), tile(3 * AW), tile(3 * AW), _VMEM, _VMEM, _VMEM, _VMEM, _VMEM],
        out_shape=[_sds((s_len, 3 * AW), BF16), _sds((1, AW), F32), _sds((1, AW), F32), _sds((G, CH, CH), F32),
                   _sds((CH, G), F32)],
        out_specs=[tile(3 * AW), whole(1, AW), whole(1, AW), whole(G, CH, CH), whole(CH, G)],
        scratch=[pltpu.VMEM((G, CH, CH), BF16), pltpu.VMEM((G, CH, CH), BF16), big(F32), big(F32), big(F32), big(F32),
                 pltpu.VMEM((G, CH, GD), F32), pltpu.VMEM((G, CH, CH), F32)])


def _bwd_a_in(dz, dx1, x, nw, win8, jobs, *, tm, relay_step):
    s_len = x.shape[0]
    nt = s_len // tm

    def main(i, ins, outs, scr):
        dz_ref, dx1_ref, x_ref, nw_ref, win_ref = ins
        gx_ref, gnw_ref = outs

        @pl.when(i == 0)
        def _():
            gnw_ref[...] = jnp.zeros_like(gnw_ref)

        dh = jnp.zeros((tm, D), F32)
        for k in range(NDEV):
            dh = dh + _dot_nt(dz_ref[:, k * CA:(k + 1) * CA], win_ref[k])
        x = x_ref[...]
        r = _rms(x)
        gx_ref[...] = dx1_ref[...] + _rms_bwd(dh, x, r, nw_ref[...])
        gnw_ref[...] += _rowsum(dh * x * r)

        @pl.when(i == nt - 1)
        def _():
            gnw_ref[...] = _direct_sum(gnw_ref[...], *scr)

    tile = lambda w: pl.BlockSpec((tm, w), lambda i: (i, 0))
    return _call(
        main, jobs, name="bwd_a_in", grid=(nt,), relay_step=relay_step,
        ins=[dz, dx1, x, nw, win8], in_specs=[tile(3 * AW), tile(D), tile(D), _VMEM, _VMEM],
        out_shape=[_sds((s_len, D), F32), _sds((1, D), F32)],
        out_specs=[tile(D), pl.BlockSpec((1, D), lambda i: (0, 0))], scratch=_direct_sum_scratch((1, D), F32))


def _conv(p8_ref, cs, xb, xm1, xm2, xm3):
    xc = p8_ref[4:5, cs] + p8_ref[3:4, cs] * xb
    xc = xc + p8_ref[0:1, cs] * xm3
    xc = xc + p8_ref[1:2, cs] * xm2
    return xc + p8_ref[2:3, cs] * xm1


def _gates(p8_ref, gcat_ref, hh, xc):
    cs = slice(hh * HD, (hh + 1) * HD)
    pre = _dot(xc.astype(BF16), gcat_ref[hh])
    r = _sigmoid(pre[:, :HD] + p8_ref[5:6, cs])
    ig = _sigmoid(pre[:, HD:] + p8_ref[6:7, cs])
    sp = _softplus_neg(p8_ref[7:8, cs])
    la = (-RG_C) * r * sp
    a = jnp.exp(la)
    half_log = 0.5 * jnp.log(jnp.tanh(-la) * (1.0 + a * a))
    return r, ig, sp, a, jnp.exp(half_log), jnp.exp(-half_log)


def _scan_rows(a_ref, b_ref, out_ref, carry, tm, reverse):
    row = lax.broadcasted_iota(jnp.int32, (SUBLANES, BW), 0)
    ngrp = tm // SUBLANES

    def step(j, cr):
        jj = (ngrp - 1 - j) if reverse else j
        off = pl.multiple_of(jj * SUBLANES, SUBLANES)
        a = a_ref[pl.ds(off, SUBLANES), :]
        b = b_ref[pl.ds(off, SUBLANES), :]
        for sh in (1, 2, 4):
            if reverse:
                a_s = pltpu.roll(a, SUBLANES - sh, 0)
                b_s = pltpu.roll(b, SUBLANES - sh, 0)
                m = row < SUBLANES - sh
            else:
                a_s = pltpu.roll(a, sh, 0)
                b_s = pltpu.roll(b, sh, 0)
                m = row >= sh
            b = jnp.where(m, a * b_s + b, b)
            a = jnp.where(m, a * a_s, a)
        o = b + a * cr
        out_ref[pl.ds(off, SUBLANES), :] = o
        return o[0:1, :] if reverse else o[SUBLANES - 1:SUBLANES, :]

    return lax.fori_loop(0, ngrp, step, carry)


def _fwd_b(x, ya, wout_a, nw, win8, p8, gcat, jobs, *, tm, relay_step):
    s_len = x.shape[0]
    nt = s_len // tm

    def main(i, ins, outs, scr):
        x_ref, ya_ref, wouta_ref, nw_ref, win_ref, p8_ref, gcat_ref = ins
        x1_ref, zb_ref, hs_ref, h1_ref, yb_ref, xc_ref, a_ref, cc_ref, r_ref, ig_ref, m_ref = outs
        xbe_scr, b_scr, k_scr, carry_scr = scr

        @pl.when(i == 0)
        def _():
            xbe_scr[0:SUBLANES, :] = jnp.zeros((SUBLANES, BW), F32)
            carry_scr[...] = jnp.zeros_like(carry_scr)

        x1 = x_ref[...] + _dot(ya_ref[...], wouta_ref[...])
        x1_ref[...] = x1
        h = (x1 * _rms(x1) * nw_ref[...]).astype(BF16)
        h1_ref[...] = h
        for k in range(NDEV):
            zb_ref[:, k * CB:(k + 1) * CB] = _dot(h, win_ref[k])
        xbe_scr[SUBLANES:SUBLANES + tm, :] = zb_ref[:, :BW]
        for hh in range(BH):
            cs = slice(hh * HD, (hh + 1) * HD)
            xc = _conv(p8_ref, cs, xbe_scr[SUBLANES:SUBLANES + tm, cs], xbe_scr[7:7 + tm, cs],
                       xbe_scr[6:6 + tm, cs], xbe_scr[5:5 + tm, cs])
            r, ig, _, a, mult, rm = _gates(p8_ref, gcat_ref, hh, xc)
            ixc = ig * xc
            xc_ref[:, cs] = xc
            a_ref[:, cs] = a
            r_ref[:, cs] = r.astype(BF16)
            ig_ref[:, cs] = ig.astype(BF16)
            m_ref[:, cs] = mult.astype(BF16)
            b_scr[:, cs] = mult * ixc
            k_scr[:, cs] = ixc * (a * a * rm)
        xbe_scr[0:SUBLANES, :] = xbe_scr[tm:tm + SUBLANES, :]
        carry_scr[...] = _scan_rows(a_ref, b_scr, hs_ref, carry_scr[...], tm, False)
        for hh in range(BH):
            cs = slice(hh * HD, (hh + 1) * HD)
            gt = zb_ref[:, BW + hh * HD:BW + (hh + 1) * HD]
            hsv = hs_ref[:, cs]
            yb_ref[:, cs] = (hsv * (gt * _sigmoid(gt))).astype(BF16)
            cc_ref[:, cs] = (hsv - b_scr[:, cs]) - k_scr[:, cs]

    tile = lambda w: pl.BlockSpec((tm, w), lambda i: (i, 0))
    wide = lambda dt: _sds((s_len, BW), dt)
    return _call(
        main, jobs, name="fwd_b", grid=(nt,), relay_step=relay_step,
        ins=[x, ya, wout_a, nw, win8, p8, gcat], in_specs=[tile(D), tile(AW), _VMEM, _VMEM, _VMEM, _VMEM, _VMEM],
        out_shape=[_sds((s_len, D), F32), _sds((s_len, 2 * BW), F32), wide(F32), _sds((s_len, D), BF16), wide(BF16),
                   wide(F32), wide(F32), wide(F32), wide(BF16), wide(BF16), wide(BF16)],
        out_specs=[tile(D), tile(2 * BW), tile(BW), tile(D)] + [tile(BW)] * 7,
        scratch=[pltpu.VMEM((tm + SUBLANES, BW), F32), pltpu.VMEM((tm, BW), F32), pltpu.VMEM((tm, BW), F32),
                 pltpu.VMEM((1, BW), F32)])


def _head(x1, yb, wout, nfw, tgt, *, tm):
    s_len = x1.shape[0]

    def main(i, ins, outs, scr):
        x1_ref, yb_ref, wout_ref, nfw_ref, t_ref = ins
        dx2_ref, dx2b_ref, loss_ref, gnfw_ref = outs

        @pl.when(i == 0)
        def _():
            loss_ref[...] = jnp.zeros_like(loss_ref)
            gnfw_ref[...] = jnp.zeros_like(gnfw_ref)

        x2 = x1_ref[...] + _dot(yb_ref[...], wout_ref[...])
        rf = _rms(x2)
        xn = x2 * rf
        e = xn * nfw_ref[...] - t_ref[...]
        loss_ref[...] += (0.5 / D) * jnp.sum(jnp.sum(e * e, axis=-1, keepdims=True), axis=0, keepdims=True)
        dyf = e * (1.0 / D)
        gnfw_ref[...] += _rowsum(dyf * xn)
        dx2 = _rms_bwd(dyf, x2, rf, nfw_ref[...])
        dx2_ref[...] = dx2
        dx2b_ref[...] = dx2.astype(BF16)

    tile = lambda w: pl.BlockSpec((tm, w), lambda i: (i, 0))
    whole = lambda *s: pl.BlockSpec(s, lambda i: (0,) * len(s))
    (dx2, dx2b, loss, gnfw), _ = _call(
        main, [], name="head", grid=(s_len // tm,),
        ins=[x1, yb, wout, nfw, tgt], in_specs=[tile(D), tile(BW), _VMEM, _VMEM, tile(D)],
        out_shape=[_sds((s_len, D), F32), _sds((s_len, D), BF16), _sds((1, 1), F32), _sds((1, D), F32)],
        out_specs=[tile(D), tile(D), whole(1, 1), whole(1, D)], scratch=[])
    return dx2, dx2b, loss, gnfw


def _bwd_b(dx2, zb, hs, x1, saved, nw, win8, p8, gcat, wout, *, tm):
    s_len = x1.shape[0]
    nt = s_len // tm

    def main(i, ins, outs, scr):
        (dx2_ref, zb_ref, hs_ref, x1_ref, xc_ref, a_ref, cc_ref, r_ref, ig_ref, m_ref,
         nw_ref, win_ref, p8_ref, gcat_ref, wout_ref) = ins
        dx1_ref, dx1b_ref, dzb_ref, gp8_ref, gga_ref, ggx_ref, gnw_ref = outs
        ae_scr, an_scr, dhd_scr, dh_scr, dy_scr, dxce_scr, carry_scr, afirst_scr = scr

        @pl.when(i == 0)
        def _():
            gp8_ref[...] = jnp.zeros_like(gp8_ref)
            gga_ref[...] = jnp.zeros_like(gga_ref)
            ggx_ref[...] = jnp.zeros_like(ggx_ref)
            gnw_ref[...] = jnp.zeros_like(gnw_ref)
            dxce_scr[tm:tm + SUBLANES, :] = jnp.zeros((SUBLANES, BW), F32)
            carry_scr[...] = jnp.zeros_like(carry_scr)
            afirst_scr[...] = jnp.zeros_like(afirst_scr)

        dx2 = dx2_ref[...]
        dy_scr[...] = _dot_nt(dx2.astype(BF16), wout_ref[...])
        for hh in range(BH):
            cs = slice(hh * HD, (hh + 1) * HD)
            gs = slice(BW + hh * HD, BW + (hh + 1) * HD)
            gt = zb_ref[:, gs]
            sig = _sigmoid(gt)
            dy = dy_scr[:, cs]
            dhd_scr[:, cs] = dy * (gt * sig)
            dzb_ref[:, gs] = (dy * hs_ref[:, cs] * (sig * (1.0 + gt * (1.0 - sig)))).astype(BF16)

        ae_scr[0:tm, :] = a_ref[...]
        ae_scr[tm:tm + SUBLANES, :] = jnp.broadcast_to(afirst_scr[...], (SUBLANES, BW))
        an_scr[...] = ae_scr[1:1 + tm, :]
        afirst_scr[...] = ae_scr[0:1, :]
        carry_scr[...] = _scan_rows(an_scr, dhd_scr, dh_scr, carry_scr[...], tm, True)

        for hh in range(BH):
            cs = slice(hh * HD, (hh + 1) * HD)
            dh = dh_scr[:, cs]
            mult = m_ref[:, cs].astype(F32)
            ig = ig_ref[:, cs].astype(F32)
            r = r_ref[:, cs].astype(F32)
            xc = xc_ref[:, cs]
            lam = p8_ref[7:8, cs]
            sp = _softplus_neg(lam)
            dla = dh * cc_ref[:, cs]
            gp8_ref[7:8, cs] += _rowsum(dla * ((-RG_C) * r)) * (-_sigmoid(-lam))
            dpr = dla * ((-RG_C) * sp) * (r * (1.0 - r))
            dpi = dh * mult * xc * (ig * (1.0 - ig))
            gp8_ref[5:6, cs] += _rowsum(dpr)
            gp8_ref[6:7, cs] += _rowsum(dpi)
            dcat = jnp.concatenate([dpr, dpi], axis=1).astype(BF16)
            dxc = dh * mult * ig + _dot_nt(dcat, gcat_ref[hh])
            gg = _dot(xc.T.astype(BF16), dcat)
            gga_ref[hh] += gg[:, :HD]
            ggx_ref[hh] += gg[:, HD:]
            dxce_scr[0:tm, cs] = dxc
            gp8_ref[4:5, cs] += _rowsum(dxc)
        for hh in range(BH):
            cs = slice(hh * HD, (hh + 1) * HD)
            xb = zb_ref[:, cs]
            d0, d1 = dxce_scr[0:tm, cs], dxce_scr[1:1 + tm, cs]
            d2, d3 = dxce_scr[2:2 + tm, cs], dxce_scr[3:3 + tm, cs]
            dzb_ref[:, cs] = (p8_ref[3:4, cs] * d0 + p8_ref[2:3, cs] * d1 + p8_ref[1:2, cs] * d2
                              + p8_ref[0:1, cs] * d3).astype(BF16)
            gp8_ref[3:4, cs] += _rowsum(d0 * xb)
            gp8_ref[2:3, cs] += _rowsum(d1 * xb)
            gp8_ref[1:2, cs] += _rowsum(d2 * xb)
            gp8_ref[0:1, cs] += _rowsum(d3 * xb)
        dxce_scr[tm:tm + SUBLANES, :] = dxce_scr[0:SUBLANES, :]

        dh1 = jnp.zeros((tm, D), F32)
        for k in range(NDEV):
            dh1 = dh1 + _dot_nt(dzb_ref[:, k * CB:(k + 1) * CB], win_ref[k])
        x1 = x1_ref[...]
        r1 = _rms(x1)
        dx1 = dx2 + _rms_bwd(dh1, x1, r1, nw_ref[...])
        dx1_ref[...] = dx1
        dx1b_ref[...] = dx1.astype(BF16)
        gnw_ref[...] += _rowsum(dh1 * x1 * r1)

    tile = lambda w: pl.BlockSpec((tm, w), lambda i: (nt - 1 - i, 0))
    whole = lambda *s: pl.BlockSpec(s, lambda i: (0,) * len(s))
    full = lambda: pltpu.VMEM((tm, BW), F32)
    ext = lambda: pltpu.VMEM((tm + SUBLANES, BW), F32)
    out, _ = _call(
        main, [], name="bwd_b", grid=(nt,),
        ins=[dx2, zb, hs, x1, *saved, nw, win8, p8, gcat, wout],
        in_specs=[tile(D), tile(2 * BW), tile(BW), tile(D)] + [tile(BW)] * 6 + [_VMEM] * 5,
        out_shape=[_sds((s_len, D), F32), _sds((s_len, D), BF16), _sds((s_len, 2 * BW), BF16), _sds((SUBLANES, BW), F32),
                   _sds((BH, HD, HD), F32), _sds((BH, HD, HD), F32), _sds((1, D), F32)],
        out_specs=[tile(D), tile(D), tile(2 * BW), whole(SUBLANES, BW), whole(BH, HD, HD), whole(BH, HD, HD),
                   whole(1, D)],
        scratch=[ext(), full(), full(), full(), full(), ext(), pltpu.VMEM((1, BW), F32), pltpu.VMEM((1, BW), F32)])
    return out


def _transpose_into(dst_ref, src_ref, rows):
    s_len = src_ref.shape[0]
    for r0 in range(0, s_len, rows):
        dst_ref[:, r0:r0 + rows] = src_ref[r0:r0 + rows, :].astype(F32).T.astype(BF16)


def _wgrad(a, b, jobs, *, by_rows, per, name, relay_step=0):
    s_len, m = a.shape
    n = b.shape[1]
    r, cd = (m // NDEV, n) if by_rows else (m, n // NDEV)
    nsteps = NDEV // per
    at_rows = per * r if by_rows else m

    def main(i, ins, outs, scr):
        a_ref, b_ref = ins
        q_ref, acc_ref = outs
        at_scr, stage, mine, land, send_sems, recv_sems = scr
        x, y, c = _place()

        def to_sibling(pi):
            return pltpu.make_async_remote_copy(
                src_ref=stage.at[pi & 1], dst_ref=land.at[pi], send_sem=send_sems.at[pi], recv_sem=recv_sems.at[pi],
                device_id=(x, y, 1 - c), device_id_type=MESH)

        if by_rows:
            _transpose_into(at_scr, a_ref, TRANSPOSE_ROWS)
        else:
            @pl.when(i == 0)
            def _():
                _transpose_into(at_scr, a_ref, TRANSPOSE_ROWS)

        res = _dot(at_scr[...], b_ref[...]).astype(BF16)
        for k in range(per):
            blk = per * i + k
            pi, pc = blk >> 1, blk & 1
            val = res[k * r:(k + 1) * r, :] if by_rows else res

            @pl.when(pc != c)
            def _():
                @pl.when(pi >= 2)
                def _():
                    to_sibling(pi - 2).wait_send()

                stage[pi & 1] = val
                to_sibling(pi).start()

            @pl.when(pc == c)
            def _():
                mine[pi] = val

        @pl.when(i == nsteps - 1)
        def _():
            for p in range(4):
                to_sibling(p).wait_recv()
            to_sibling(2).wait_send()
            to_sibling(3).wait_send()
            _chip_sums(mine, land, q_ref, acc_ref, x, y)

    if by_rows:
        in_specs = [pl.BlockSpec((s_len, at_rows), lambda j: (0, j)), _VMEM]
    else:
        in_specs = [_VMEM, pl.BlockSpec((s_len, cd), lambda j: (0, j))]
    blk_vmem = lambda k: pltpu.VMEM((k, r, cd), BF16)
    (q, acc), job_out = _call(
        main, jobs, name=name, grid=(nsteps,), relay_step=relay_step, ins=[a, b], in_specs=in_specs,
        out_shape=[_sds((NCHIP_OTHER, r, cd), BF16), _sds((r, cd), F32)],
        out_specs=[pl.BlockSpec((NCHIP_OTHER, r, cd), lambda j: (0, 0, 0)), pl.BlockSpec((r, cd), lambda j: (0, 0))],
        scratch=[pltpu.VMEM((at_rows, s_len), BF16), blk_vmem(2), blk_vmem(4), blk_vmem(4),
                 pltpu.SemaphoreType.DMA((4,)), pltpu.SemaphoreType.DMA((4,))])
    return q, acc, job_out


def _wgrad_cols_early(a, b, jobs, *, name, relay_step=0):
    s_len, m = a.shape
    r, cd = m, b.shape[1] // NDEV
    h = r // 2

    def chip_at(pos, base):
        return base ^ (3 - pos)

    def main(i, ins, outs, scr):
        a_ref, b_ref = ins
        q_ref, acc_ref, rel_ref = outs
        at_scr, stage, mine, land, q2_scr, send_sems, recv_sems, via_send, via_recv = scr
        x, y, c = _place()
        base = 2 * x + y
        xn, yn, _ = _other_chips(x, y)
        pos, pc = i >> 1, i & 1
        pi = chip_at(pos, base)

        def to_sibling(chip, slot):
            return pltpu.make_async_remote_copy(
                src_ref=stage.at[slot], dst_ref=land.at[chip], send_sem=send_sems.at[chip],
                recv_sem=recv_sems.at[chip], device_id=(x, y, 1 - c), device_id_type=MESH)

        def via(k):
            return pltpu.make_async_remote_copy(
                src_ref=q2_scr.at[pl.ds(k * h, h)], dst_ref=rel_ref.at[k], send_sem=via_send.at[k],
                recv_sem=via_recv.at[k], device_id=(*(xn, yn)[k], c), device_id_type=MESH)

        @pl.when(i == 0)
        def _():
            _transpose_into(at_scr, a_ref, TRANSPOSE_ROWS)

        res = _dot(at_scr[...], b_ref[...]).astype(BF16)

        @pl.when(pc != c)
        def _():
            @pl.when(pos >= 2)
            def _():
                to_sibling(chip_at(pos - 2, base), pos & 1).wait_send()

            stage[pos & 1] = res
            to_sibling(pi, pos & 1).start()

        @pl.when(pc == c)
        def _():
            mine[pi] = res

        @pl.when(i == 1)
        def _():
            dg = chip_at(0, base)
            to_sibling(dg, 0).wait_recv()
            q2 = (mine[dg].astype(F32) + land[dg].astype(F32)).astype(BF16)
            q2_scr[...] = q2
            q_ref[2] = q2
            via(0).start()
            via(1).start()

        @pl.when(i == NDEV - 1)
        def _():
            for pos_ in (1, 2, 3):
                to_sibling(chip_at(pos_, base), 0).wait_recv()
            to_sibling(chip_at(2, base), 0).wait_send()
            to_sibling(chip_at(3, base), 1).wait_send()
            for k in range(2):
                via(k).wait_recv()
            for k in range(2):
                via(k).wait_send()
            for j, chip in enumerate((base ^ 2, base ^ 1)):
                q_ref[j] = (mine[chip].astype(F32) + land[chip].astype(F32)).astype(BF16)
            acc_ref[...] = mine[base].astype(F32) + land[base].astype(F32)

    def b_block(j):
        base = 2 * lax.axis_index("x") + lax.axis_index("y")
        return (0, 2 * chip_at(j >> 1, base) + (j & 1))

    blk_vmem = lambda k: pltpu.VMEM((k, r, cd), BF16)
    (q, acc, rel), job_out = _call(
        main, jobs, name=name, grid=(NDEV,), relay_step=relay_step, ins=[a, b],
        in_specs=[_VMEM, pl.BlockSpec((s_len, cd), b_block)],
        out_shape=[_sds((NCHIP_OTHER, r, cd), BF16), _sds((r, cd), F32), _sds((2, h, cd), BF16)],
        out_specs=[pl.BlockSpec((NCHIP_OTHER, r, cd), lambda j: (0, 0, 0)), pl.BlockSpec((r, cd), lambda j: (0, 0)), _HBM],
        scratch=[pltpu.VMEM((m, s_len), BF16), blk_vmem(2), blk_vmem(4), blk_vmem(4), pltpu.VMEM((r, cd), BF16),
                 pltpu.SemaphoreType.DMA((4,)), pltpu.SemaphoreType.DMA((4,)), pltpu.SemaphoreType.DMA((2,)),
                 pltpu.SemaphoreType.DMA((2,))])
    return q, acc, rel, job_out


class _ExchangeRest:
    def __init__(self, q, relayed):
        _, r, cd = q.shape
        half = (2, r // 2, cd)
        self.ins, self.in_specs = [q, relayed], [_HBM, _HBM]
        self.out_shape, self.out_specs = [_sds((2, r, cd), q.dtype)], [_HBM]
        self.scratch = [pltpu.VMEM(half, q.dtype), pltpu.VMEM(half, q.dtype), pltpu.VMEM(half, q.dtype),
                        pltpu.SemaphoreType.DMA((4,)), pltpu.SemaphoreType.DMA((4,)), pltpu.SemaphoreType.DMA((4,))]

    def ops(self, ins, outs, scr):
        (q, rel_in), (land,) = ins, outs
        own, rel, comb, send_sems, recv_sems, local_sems = scr
        h = q.shape[1] // 2
        x, y, c = _place()
        xn, yn, _ = _other_chips(x, y)
        h0, h1 = pl.ds(0, h), pl.ds(h, h)

        def remote(k, src, dst, chip):
            return pltpu.make_async_remote_copy(src_ref=src, dst_ref=dst, send_sem=send_sems.at[k],
                                                recv_sem=recv_sems.at[k], device_id=(*chip, c), device_id_type=MESH)

        def sends():
            return [remote(0, q.at[0, h0], land.at[0, h0], xn), remote(1, q.at[1, h1], land.at[1, h1], yn),
                    remote(2, comb.at[0], land.at[1, h0], yn), remote(3, comb.at[1], land.at[0, h1], xn)]

        def loads():
            return [pltpu.make_async_copy(q.at[1, h0], own.at[0], local_sems.at[0]),
                    pltpu.make_async_copy(q.at[0, h1], own.at[1], local_sems.at[1]),
                    pltpu.make_async_copy(rel_in.at[0], rel.at[0], local_sems.at[2]),
                    pltpu.make_async_copy(rel_in.at[1], rel.at[1], local_sems.at[3])]

        def start():
            cps, lds = sends(), loads()
            for ld in lds:
                ld.start()
            cps[0].start()
            cps[1].start()
            for ld in lds:
                ld.wait()
            for k in range(2):
                comb[k] = (own[k].astype(F32) + rel[k].astype(F32)).astype(comb.dtype)
            cps[2].start()
            cps[3].start()

        def finish():
            cps = sends()
            for cp in cps:
                cp.wait_recv()
            for cp in cps:
                cp.wait_send()

        return start, lambda: None, finish


def _adam_math(w, g, m, v):
    m = B1 * m + (1.0 - B1) * g
    v = B2 * v + (1.0 - B2) * (g * g)
    m_hat = m / (1.0 - B1 ** STEP)
    v_hat = v / (1.0 - B2 ** STEP)
    delta = (-LR) * (m_hat / (jnp.sqrt(v_hat) + ADAM_EPS) + WD * w)
    return delta, m, v


def _adam_big(w, acc, land, m, v, name):
    r, cd = w.shape
    rb = ADAM_ROWS if r % ADAM_ROWS == 0 else r
    nland = land.shape[0]

    def body(w_ref, acc_ref, land_ref, m_ref, v_ref, g_ref, d_ref, mo_ref, vo_ref):
        g = acc_ref[...]
        for j in range(nland):
            g = g + land_ref[j].astype(F32)
        g_ref[...] = g
        d_ref[...], mo_ref[...], vo_ref[...] = _adam_math(w_ref[...], g, m_ref[...], v_ref[...])

    blk = pl.BlockSpec((rb, cd), lambda i: (i, 0))
    blk3 = pl.BlockSpec((nland, rb, cd), lambda i: (0, i, 0))
    return pl.pallas_call(
        body, name=name, grid=(r // rb,), in_specs=[blk, blk, blk3, blk, blk], out_specs=[blk] * 4,
        out_shape=[_sds((r, cd), F32)] * 4,
        compiler_params=_params(dimension_semantics=("arbitrary",)),
    )(w, acc, land, m, v)


def _adam_small(groups):
    n = len(groups)

    def body(*refs):
        ins, outs = refs[:4 * n], refs[4 * n:]
        for k in range(n):
            w_ref, g_ref, m_ref, v_ref = ins[4 * k:4 * k + 4]
            d, mo, vo = _adam_math(w_ref[...], g_ref[...], m_ref[...], v_ref[...])
            outs[3 * k][...] = d
            outs[3 * k + 1][...] = mo
            outs[3 * k + 2][...] = vo

    flat = [a for grp in groups for a in grp]
    shapes = [_sds(grp[0].shape, F32) for grp in groups for _ in range(3)]
    res = pl.pallas_call(
        body, name="adam_small", in_specs=[_VMEM] * (4 * n), out_specs=[_VMEM] * (3 * n), out_shape=shapes,
        compiler_params=_params(),
    )(*flat)
    return [tuple(res[3 * k:3 * k + 3]) for k in range(n)]


TM_FWD_A = 256
RELAY_STEP_FWD_A = 2
RELAY_STEP_FWD_B = 2
TM_BWD_A = 256
RELAY_STEP_BWD_A = 3
TM_BWD_A_IN = 256
RELAY_STEP_BWD_A_IN = 4
RELAY_STEP_WGRAD_A_IN = 2
TM_FWD_B = 256
TM_HEAD = 512
TM_BWD_B = 256


def _pack(parts, rows):
    flat = jnp.concatenate([p.reshape(-1) for p in parts])
    return jnp.pad(flat, (0, NDEV * rows * LANES - flat.shape[0])).reshape(NDEV, rows, LANES)


def _unpack(packed, shapes):
    flat, out, off = packed.reshape(-1), [], 0
    for s in shapes:
        size = 1
        for d in s:
            size *= d
        out.append(flat[off:off + size].reshape(s))
        off += size
    return out


def kernel(x, norm_w, a_w_in, a_ln_w, a_ln_b, a_w_s, a_b_s, a_w_out, b_w_in, b_conv_w, b_conv_b, b_gate_a_w, b_gate_a_b, b_gate_x_w, b_gate_x_b, b_lambda, b_w_out, norm_f_w, loss_target, m_norm_w, m_a_w_in, m_a_ln_w, m_a_ln_b, m_a_w_s, m_a_b_s, m_a_w_out, m_b_w_in, m_b_conv_w, m_b_conv_b, m_b_gate_a_w, m_b_gate_a_b, m_b_gate_x_w, m_b_gate_x_b, m_b_lambda, m_b_w_out, m_norm_f_w, v_norm_w, v_a_w_in, v_a_ln_w, v_a_ln_b, v_a_w_s, v_a_b_s, v_a_w_out, v_b_w_in, v_b_conv_w, v_b_conv_b, v_b_gate_a_w, v_b_gate_a_b, v_b_gate_x_w, v_b_gate_x_b, v_b_lambda, v_b_w_out, v_norm_f_w):
    me = 4 * lax.axis_index("x") + 2 * lax.axis_index("y") + lax.axis_index("c")
    xs, tgt = x[0], loss_target[0]
    nw0, nw1, nfw = norm_w[0:1], norm_w[1:2], norm_f_w.reshape(1, D)
    w_s, bst = a_w_s[0], a_b_s[0].T
    gcat = jnp.concatenate([b_gate_a_w[0], b_gate_x_w[0]], axis=-1).astype(BF16)

    p8_shard = jnp.concatenate([b_conv_w[0], b_conv_b, b_gate_a_b, b_gate_x_b, b_lambda], axis=0)
    (z, h0, ya, pp), ((win_a8, p8_all), (wout_a8, win_b8)) = _fwd_a(
        xs, nw0, a_ln_w, a_ln_b, w_s, bst,
        [_Gather([a_w_in[0], p8_shard], [BF16, F32]), _Gather([a_w_out[0], b_w_in[0]], [BF16, BF16])],
        tm=TM_FWD_A, relay_step=RELAY_STEP_FWD_A)
    p8 = jnp.transpose(p8_all, (1, 0, 2)).reshape(SUBLANES, BW)
    wout_a = wout_a8.reshape(AW, D)
    (x1, zb, hs, h1, yb, *saved_b), ((wout_b8,),) = _fwd_b(
        xs, ya, wout_a, nw1, win_b8, p8, gcat, [_Gather([b_w_out[0]], [BF16])],
        tm=TM_FWD_B, relay_step=RELAY_STEP_FWD_B)
    wout_b = wout_b8.reshape(BW, D)
    dx2, dx2b, loss, g_nfw = _head(x1, yb, wout_b, nfw, tgt, tm=TM_HEAD)

    dx1, dx1b, dzb, g_p8, g_ga, g_gx, g_nw1 = _bwd_b(dx2, zb, hs, x1, saved_b, nw1, win_b8, p8, gcat, wout_b,
                                                     tm=TM_BWD_B)
    q_wout_b, acc_wout_b, _ = _wgrad(yb, dx2b, [], by_rows=True, per=2, name="wgrad_b_out")
    shapes_b = [(1, D), (1, D), (SUBLANES, BW), (1, 1)]
    pack_b = _pack([g_nfw, g_nw1, g_p8, loss], 16)
    small_b = _InChip([g_ga.reshape(NDEV, -1, HD), g_gx.reshape(NDEV, -1, HD), pack_b])
    q_win_b, acc_win_b, (sm_b, (l_wout_b,)) = _wgrad(h1, dzb, [small_b, _Exchange([q_wout_b])], by_rows=False, per=1,
                                                      name="wgrad_b_in")
    qs_b, accs_b = sm_b[:3], sm_b[3:]

    (dz, g_lnw, g_lnb, g_ws, g_bst), (lands_b, (l_win_b,)) = _bwd_a(
        dx1b, z, pp, a_ln_w, a_ln_b, w_s, bst, wout_a, [_Exchange(qs_b), _ExchangeVia(q_win_b)],
        tm=TM_BWD_A, relay_step=RELAY_STEP_BWD_A)
    shapes_a = [(1, AW), (1, AW), (CH, G)]
    pack_a = _pack([g_lnw, g_lnb, g_bst], 8)
    q_wout_a, acc_wout_a, (red_b, sm_a) = _wgrad(
        ya, dx1b, [_SumGather(accs_b, lands_b), _InChip([g_ws, pack_a])], by_rows=True, per=2,
        name="wgrad_a_out", relay_step=1)
    qs_a, accs_a = sm_a[:2], sm_a[2:]
    q_win_a, acc_win_a, rel_a, (lands_a, (l_wout_a,)) = _wgrad_cols_early(
        h0, dz, [_Exchange(qs_a), _ExchangeVia(q_wout_a)], name="wgrad_a_in", relay_step=RELAY_STEP_WGRAD_A_IN)
    (gx, g_nw0), (red_a, (l_win_a,)) = _bwd_a_in(
        dz, dx1, xs, nw0, win_a8, [_SumGather(accs_a, lands_a), _ExchangeRest(q_win_a, rel_a)],
        tm=TM_BWD_A_IN, relay_step=RELAY_STEP_BWD_A_IN)

    r_ga, r_gx, r_pack_b = red_b
    r_nfw, r_nw1, r_p8, loss = _unpack(r_pack_b, shapes_b)
    r_ws, r_pack_a = red_a
    r_lnw, r_lnb, r_bst = _unpack(r_pack_a, shapes_a)
    g_p8 = lax.dynamic_slice_in_dim(r_p8, me * (BW // NDEV), BW // NDEV, axis=1)
    loss = loss[0, 0]

    weights = dict(norm_w=norm_w, a_w_in=a_w_in, a_ln_w=a_ln_w, a_ln_b=a_ln_b, a_w_s=a_w_s, a_b_s=a_b_s, a_w_out=a_w_out,
                   b_w_in=b_w_in, b_conv_w=b_conv_w, b_conv_b=b_conv_b, b_gate_a_w=b_gate_a_w, b_gate_a_b=b_gate_a_b,
                   b_gate_x_w=b_gate_x_w, b_gate_x_b=b_gate_x_b, b_lambda=b_lambda, b_w_out=b_w_out, norm_f_w=norm_f_w)
    mom1 = dict(norm_w=m_norm_w, a_w_in=m_a_w_in, a_ln_w=m_a_ln_w, a_ln_b=m_a_ln_b, a_w_s=m_a_w_s, a_b_s=m_a_b_s,
                a_w_out=m_a_w_out, b_w_in=m_b_w_in, b_conv_w=m_b_conv_w, b_conv_b=m_b_conv_b, b_gate_a_w=m_b_gate_a_w,
                b_gate_a_b=m_b_gate_a_b, b_gate_x_w=m_b_gate_x_w, b_gate_x_b=m_b_gate_x_b, b_lambda=m_b_lambda,
                b_w_out=m_b_w_out, norm_f_w=m_norm_f_w)
    mom2 = dict(norm_w=v_norm_w, a_w_in=v_a_w_in, a_ln_w=v_a_ln_w, a_ln_b=v_a_ln_b, a_w_s=v_a_w_s, a_b_s=v_a_b_s,
                a_w_out=v_a_w_out, b_w_in=v_b_w_in, b_conv_w=v_b_conv_w, b_conv_b=v_b_conv_b, b_gate_a_w=v_b_gate_a_w,
                b_gate_a_b=v_b_gate_a_b, b_gate_x_w=v_b_gate_x_w, b_gate_x_b=v_b_gate_x_b, b_lambda=v_b_lambda,
                b_w_out=v_b_w_out, norm_f_w=v_norm_f_w)
    names = list(weights)

    def as2d(a):
        return a.reshape(-1, a.shape[-1])

    upd, grads = {}, {}
    for k, acc, land in (("a_w_in", acc_win_a, l_win_a), ("a_w_out", acc_wout_a, l_wout_a),
                         ("b_w_in", acc_win_b, l_win_b), ("b_w_out", acc_wout_b, l_wout_b)):
        g, d, mo, vo = _adam_big(as2d(weights[k]), acc, land, as2d(mom1[k]), as2d(mom2[k]), "adam_" + k)
        grads[k] = g[None]
        upd[k] = (d, mo, vo)
    grads.update(
        norm_w=jnp.concatenate([g_nw0, r_nw1], axis=0), a_ln_w=r_lnw, a_ln_b=r_lnb,
        a_w_s=r_ws.reshape(1, G, CH, CH), a_b_s=r_bst.T[None],
        b_conv_w=g_p8[None, 0:4], b_conv_b=g_p8[4:5], b_gate_a_w=r_ga.reshape(1, BH, HD, HD), b_gate_a_b=g_p8[5:6],
        b_gate_x_w=r_gx.reshape(1, BH, HD, HD), b_gate_x_b=g_p8[6:7], b_lambda=g_p8[7:8], norm_f_w=r_nfw.reshape(D))
    small_names = [k for k in names if k not in upd]
    res = _adam_small([(as2d(weights[k]), as2d(grads[k]), as2d(mom1[k]), as2d(mom2[k])) for k in small_names])
    for k, r3 in zip(small_names, res):
        upd[k] = r3
    deltas = [upd[k][0].reshape(weights[k].shape) for k in names]
    new_m = [upd[k][1].reshape(weights[k].shape) for k in names]
    new_v = [upd[k][2].reshape(weights[k].shape) for k in names]
    return (loss, gx[None], *[grads[k] for k in names], *deltas, *new_m, *new_v)
```

```python
import jax
import jax.numpy as jnp
from jax import lax
from jax.experimental import pallas as pl
from jax.experimental.pallas import tpu as pltpu

F32 = jnp.float32
BF16 = jnp.bfloat16
MESH = pl.DeviceIdType.MESH

NDEV = 8
NCHIP_OTHER = 3
D = 1024
AW = 2048
G = 8
GD = AW // G
CH = 128
BW = 1536
BH = 12
HD = BW // BH
CA = 3 * AW // NDEV
CB = 2 * BW // NDEV
RMS_EPS = 1e-6
LN_EPS = 1e-5
RG_C = 8.0
LR, B1, B2, ADAM_EPS, WD, STEP = 0.001, 0.9, 0.999, 1e-08, 0.01, 10
V7X_VMEM_BYTES = 64 * 1024 * 1024
VMEM_LIMIT = V7X_VMEM_BYTES - 8 * 1024 * 1024
SUBLANES = 8
LANES = 128
BF16_ROWS = 16
TRANSPOSE_ROWS = 256
ADAM_ROWS = 512
GELU_C = 0.7978845608028654
GELU_K = 0.044715

_VMEM = pl.BlockSpec(memory_space=pltpu.VMEM)
_HBM = pl.BlockSpec(memory_space=pltpu.HBM)


def _sds(shape, dtype):
    return jax.ShapeDtypeStruct(tuple(shape), dtype)


def _params(**kw):
    return pltpu.CompilerParams(vmem_limit_bytes=VMEM_LIMIT, **kw)


def _gelu_t(z):
    p = 0.5 * jnp.tanh(z * (GELU_C + (GELU_C * GELU_K) * (z * z))) + 0.5
    return z * p, p


def _dgelu(z, p):
    return p * (1.0 + (z * (1.0 - p)) * (2.0 * GELU_C + (6.0 * GELU_C * GELU_K) * (z * z)))


def _sigmoid(v):
    return 0.5 * jnp.tanh(0.5 * v) + 0.5


def _softplus_neg(lam):
    return jnp.maximum(-lam, 0.0) + jnp.log1p(jnp.exp(-jnp.abs(lam)))


def _dot(a, b):
    return jnp.dot(a, b, preferred_element_type=F32)


def _dot_nt(a, b):
    return lax.dot_general(a, b, (((1,), (1,)), ((), ())), preferred_element_type=F32)


def _rowsum(v):
    return jnp.sum(v, axis=0, keepdims=True)


def _causal_mask():
    r = lax.broadcasted_iota(jnp.int32, (CH, CH), 0)
    c = lax.broadcasted_iota(jnp.int32, (CH, CH), 1)
    return r >= c


def _rms(x):
    return lax.rsqrt(jnp.mean(x * x, axis=-1, keepdims=True) + RMS_EPS)


def _rms_bwd(dh, x, r, nw):
    gy = dh * nw
    return r * gy - x * (r * r * r) * jnp.mean(gy * x, axis=-1, keepdims=True)


def _place():
    return lax.axis_index("x"), lax.axis_index("y"), lax.axis_index("c")


def _other_chips(x, y):
    return [(1 - x, y), (x, 1 - y), (1 - x, 1 - y)]


GATHER_SLOTS = 10


def _gather_ops(ins, outs, send_sems, recv_sems, local_sems):
    n = len(ins)
    x, y, c = _place()
    sibling = (x, y, 1 - c)
    xn, yn, dg = _other_chips(x, y)
    split = [ins[i].shape[0] % (2 * BF16_ROWS) == 0 for i in range(n)]

    def blk(chip, core):
        return 4 * chip[0] + 2 * chip[1] + core

    me = blk((x, y), c)

    def part(ref, i, half):
        if half is None:
            return ref
        h = ins[i].shape[0] // 2
        return ref.at[pl.ds(half * h, h)]

    def copy(i, k, block, to, half=None, src=None):
        dst = part(outs[i].at[block], i, half)
        return pltpu.make_async_remote_copy(
            src_ref=dst if src is None else part(src, i, half), dst_ref=dst,
            send_sem=send_sems.at[k, i], recv_sem=recv_sems.at[k, i], device_id=to, device_id_type=MESH)

    def first_copies():
        mine = [pltpu.make_async_copy(ins[i], outs[i].at[me], local_sems.at[i]) for i in range(n)]
        first = []
        for i in range(n):
            first.append(copy(i, 0, me, sibling, src=ins[i]))
            if split[i]:
                first.append(copy(i, 1, me, (*xn, c), 0, ins[i]))
                first.append(copy(i, 3, me, (*yn, c), 1, ins[i]))
                first.append(copy(i, 2, me, (*xn, c), 1, ins[i]))
                first.append(copy(i, 4, me, (*yn, c), 0, ins[i]))
            else:
                first.append(copy(i, 1, me, (*xn, c), None, ins[i]))
                first.append(copy(i, 3, me, (*yn, c), None, ins[i]))
                first.append(copy(i, 5, me, (*dg, c), None, ins[i]))
        return mine, first

    def onward():
        out = []
        for i in range(n):
            if split[i]:
                out.append(copy(i, 5, blk(xn, c), (*yn, c), 0))
                out.append(copy(i, 6, blk(yn, c), (*xn, c), 1))
        return out

    def start():
        mine, first = first_copies()
        for cp in mine + first:
            cp.start()

    def relay():
        sends = onward()
        for i in range(n):
            if split[i]:
                copy(i, 1, blk(xn, c), sibling, 0).wait_recv()
                sends.pop(0).start()
                copy(i, 3, blk(yn, c), sibling, 1).wait_recv()
                sends.pop(0).start()

    def passes():
        return [copy(i, 7 + j, blk(chip, c), sibling) for i in range(n) for j, chip in enumerate((xn, yn, dg))]

    def forward():
        fwd = passes()
        for i in range(n):
            if split[i]:
                copy(i, 2, blk(xn, c), sibling, 1).wait_recv()
                fwd[3 * i].start()
                copy(i, 4, blk(yn, c), sibling, 0).wait_recv()
                fwd[3 * i + 1].start()
                copy(i, 5, blk(dg, c), sibling, 0).wait_recv()
                copy(i, 6, blk(dg, c), sibling, 1).wait_recv()
                fwd[3 * i + 2].start()
            else:
                copy(i, 1, blk(xn, c), sibling).wait_recv()
                fwd[3 * i].start()
                copy(i, 3, blk(yn, c), sibling).wait_recv()
                fwd[3 * i + 1].start()
                copy(i, 5, blk(dg, c), sibling).wait_recv()
                fwd[3 * i + 2].start()

    def finish():
        mine, first = first_copies()
        for i in range(n):
            copy(i, 0, blk((x, y), 1 - c), sibling).wait_recv()
            for j, chip in enumerate((xn, yn, dg)):
                copy(i, 7 + j, blk(chip, 1 - c), sibling).wait_recv()
        for cp in first + passes() + onward():
            cp.wait_send()
        for cp in mine:
            cp.wait()

    return start, relay, forward, finish


def _gather_sems(n):
    return [pltpu.SemaphoreType.DMA((GATHER_SLOTS, n)), pltpu.SemaphoreType.DMA((GATHER_SLOTS, n)),
            pltpu.SemaphoreType.DMA((n,))]


class _Gather:
    def __init__(self, shards, as_dtypes=None):
        n = len(shards)
        dts = [s.dtype for s in shards] if as_dtypes is None else list(as_dtypes)
        self.cast = [jnp.dtype(d) != s.dtype for d, s in zip(dts, shards)]
        self.ins = list(shards)
        self.in_specs = [_VMEM if c else _HBM for c in self.cast]
        self.out_shape = [_sds((NDEV,) + s.shape, d) for s, d in zip(shards, dts)]
        self.out_specs = [_HBM] * n
        self.scratch = [pltpu.VMEM(s.shape, d) for s, d, c in zip(shards, dts, self.cast) if c] + _gather_sems(n)

    def ops(self, ins, outs, scr):
        ncast = sum(self.cast)
        staged = iter(scr[:ncast])
        srcs = [next(staged) if c else ref for c, ref in zip(self.cast, ins)]
        start, relay, forward, finish = _gather_ops(srcs, outs, *scr[ncast:])

        def cast_and_start():
            for c, ref, src in zip(self.cast, ins, srcs):
                if c:
                    src[...] = ref[...].astype(src.dtype)
            start()

        return cast_and_start, relay, forward, finish


class _Exchange:
    def __init__(self, qs):
        n = len(qs)
        self.ins, self.in_specs = list(qs), [_HBM] * n
        self.out_shape = [_sds(q.shape, q.dtype) for q in qs]
        self.out_specs = [_HBM] * n
        self.scratch = [pltpu.SemaphoreType.DMA((NCHIP_OTHER, n)), pltpu.SemaphoreType.DMA((NCHIP_OTHER, n))]

    def ops(self, ins, outs, scr):
        send_sems, recv_sems = scr
        n = len(ins)
        x, y, c = _place()
        chips = _other_chips(x, y)

        def copies():
            return [pltpu.make_async_remote_copy(
                src_ref=ins[i].at[j], dst_ref=outs[i].at[j], send_sem=send_sems.at[j, i],
                recv_sem=recv_sems.at[j, i], device_id=(*chips[j], c), device_id_type=MESH)
                for i in range(n) for j in range(NCHIP_OTHER)]

        def start():
            for cp in copies():
                cp.start()

        def finish():
            cps = copies()
            for cp in cps:
                cp.wait_recv()
            for cp in cps:
                cp.wait_send()

        return start, lambda: None, finish


class _ExchangeVia:
    def __init__(self, q):
        _, r, cd = q.shape
        half = (2, r // 2, cd)
        self.ins, self.in_specs = [q], [_HBM]
        self.out_shape, self.out_specs = [_sds((2, r, cd), q.dtype)], [_HBM]
        self.scratch = [pltpu.VMEM(half, q.dtype), pltpu.VMEM(half, q.dtype), pltpu.VMEM(half, q.dtype),
                        pltpu.SemaphoreType.DMA((6,)), pltpu.SemaphoreType.DMA((6,)), pltpu.SemaphoreType.DMA((2,))]

    def ops(self, ins, outs, scr):
        (q,), (land,) = ins, outs
        relayed, own, comb, send_sems, recv_sems, local_sems = scr
        h = q.shape[1] // 2
        x, y, c = _place()
        xn, yn, _ = _other_chips(x, y)
        h0, h1 = pl.ds(0, h), pl.ds(h, h)

        def remote(k, src, dst, chip):
            return pltpu.make_async_remote_copy(src_ref=src, dst_ref=dst, send_sem=send_sems.at[k],
                                                recv_sem=recv_sems.at[k], device_id=(*chip, c), device_id_type=MESH)

        def via():
            return [remote(2, q.at[2, h0], relayed.at[0], xn), remote(3, q.at[2, h1], relayed.at[1], yn)]

        def direct():
            return [remote(0, q.at[0, h0], land.at[0, h0], xn), remote(1, q.at[1, h1], land.at[1, h1], yn)]

        def second():
            return [remote(4, comb.at[0], land.at[1, h0], yn), remote(5, comb.at[1], land.at[0, h1], xn)]

        def mine():
            return [pltpu.make_async_copy(q.at[1, h0], own.at[0], local_sems.at[0]),
                    pltpu.make_async_copy(q.at[0, h1], own.at[1], local_sems.at[1])]

        def start():
            for cp in via() + direct() + mine():
                cp.start()

        def relay():
            arrived, loaded, onward = via(), mine(), second()
            for k in range(2):
                arrived[k].wait_recv()
                loaded[k].wait()
                comb[k] = (own[k].astype(F32) + relayed[k].astype(F32)).astype(comb.dtype)
                onward[k].start()

        def finish():
            landing = direct() + second()
            for cp in landing:
                cp.wait_recv()
            for cp in via() + landing:
                cp.wait_send()

        return start, relay, finish


class _SumGather:
    def __init__(self, accs, lands):
        n = len(accs)
        self.n = n
        self.ins, self.in_specs = list(accs) + list(lands), [_VMEM] * (2 * n)
        self.out_shape = [_sds((NDEV,) + a.shape, a.dtype) for a in accs]
        self.out_specs = [_HBM] * n
        self.scratch = [pltpu.VMEM(a.shape, a.dtype) for a in accs] + _gather_sems(n)

    def ops(self, ins, outs, scr):
        n = self.n
        accs, lands, mine = ins[:n], ins[n:], scr[:n]
        g_start, relay, forward, finish = _gather_ops(mine, outs, *scr[n:])

        def start():
            for i in range(n):
                mine[i][...] = accs[i][...] + lands[i][0] + lands[i][1] + lands[i][2]
            g_start()

        return start, relay, forward, finish


def _call(main, jobs, *, name, grid, ins, in_specs, out_shape, out_specs, scratch, relay_step=0, first=0,
          prologue=None, forward_last=False):
    nsteps = grid[0] if grid else 1
    n_in, n_out, n_scr = len(ins), len(out_shape), len(scratch)

    def body(*refs):
        pos = [0]

        def take(k):
            r = refs[pos[0]:pos[0] + k]
            pos[0] += k
            return r

        m_in = take(n_in)
        j_in = [take(len(j.ins)) for j in jobs]
        m_out = take(n_out)
        j_out = [take(len(j.out_shape)) for j in jobs]
        m_scr = take(n_scr)
        j_scr = [take(len(j.scratch)) for j in jobs]
        ops = [_four(j.ops(a, b, s)) for j, a, b, s in zip(jobs, j_in, j_out, j_scr)]
        i = pl.program_id(0) if grid else 0
        if not grid:
            for stage in range(4):
                for o in ops:
                    o[stage]()
                if stage == 0:
                    main(i, m_in, m_out, m_scr)
            return

        if ops:
            @pl.when(i == 0)
            def _():
                for o in ops[:first]:
                    o[0]()
                for o in ops[:first]:
                    o[1]()
                for o in ops[first:]:
                    o[0]()
                for o in ops[:first]:
                    o[2]()
                for o in ops[:first]:
                    o[3]()
                if prologue is not None:
                    prologue(j_out[:first], m_scr)

        main(i, m_in, m_out, m_scr)

        forward_at = nsteps - 1 if forward_last else max(relay_step, nsteps - 2)
        for stage, at in ((1, min(relay_step, nsteps - 1)), (2, forward_at), (3, nsteps - 1)):
            if ops[first:]:
                @pl.when(i == at)
                def _():
                    for o in ops[first:]:
                        o[stage]()

    extra = dict(dimension_semantics=("arbitrary",)) if grid else {}
    res = pl.pallas_call(
        body, name=name, grid=grid,
        in_specs=list(in_specs) + [s for j in jobs for s in j.in_specs],
        out_specs=list(out_specs) + [s for j in jobs for s in j.out_specs],
        out_shape=list(out_shape) + [s for j in jobs for s in j.out_shape],
        scratch_shapes=list(scratch) + [s for j in jobs for s in j.scratch],
        compiler_params=_params(**extra),
    )(*ins, *[a for j in jobs for a in j.ins])
    main_out, rest, job_out = res[:n_out], res[n_out:], []
    for j in jobs:
        k = len(j.out_shape)
        job_out.append(rest[:k])
        rest = rest[k:]
    return main_out, job_out


def _four(ops):
    return ops if len(ops) == 4 else (ops[0], ops[1], lambda: None, ops[2])


def _comm_only(jobs, name):
    _, job_out = _call(lambda i, a, b, s: None, jobs, name=name, grid=(), ins=[], in_specs=[], out_shape=[],
                       out_specs=[], scratch=[])
    return job_out


class _InChip:
    def __init__(self, ps):
        n = len(ps)
        self.n = n
        blk = [p.shape[1:] for p in ps]
        self.ins, self.in_specs = list(ps), [_HBM] * n
        self.out_shape = [_sds((NCHIP_OTHER,) + b, p.dtype) for b, p in zip(blk, ps)] + [_sds(b, F32) for b in blk]
        self.out_specs = [_VMEM] * (2 * n)
        self.scratch = ([pltpu.VMEM((4,) + b, p.dtype) for b, p in zip(blk, ps)] * 2
                        + [pltpu.SemaphoreType.DMA((4, n))] * 3)

    def ops(self, ins, outs, scr):
        n = self.n
        q_refs, acc_refs = outs[:n], outs[n:]
        mines, lands = scr[:n], scr[n:2 * n]
        send_sems, recv_sems, local_sems = scr[2 * n:]
        x, y, c = _place()
        sibling = (x, y, 1 - c)

        def copies():
            out = []
            for i in range(n):
                for pi in range(4):
                    loc = pltpu.make_async_copy(ins[i].at[2 * pi + c], mines[i].at[pi], local_sems.at[pi, i])
                    cp = pltpu.make_async_remote_copy(
                        src_ref=ins[i].at[2 * pi + (1 - c)], dst_ref=lands[i].at[pi],
                        send_sem=send_sems.at[pi, i], recv_sem=recv_sems.at[pi, i],
                        device_id=sibling, device_id_type=MESH)
                    out.append((loc, cp))
            return out

        def start():
            for loc, cp in copies():
                loc.start()
                cp.start()

        def finish():
            pairs = copies()
            for loc, cp in pairs:
                loc.wait()
                cp.wait_recv()
            for i in range(n):
                _chip_sums(mines[i], lands[i], q_refs[i], acc_refs[i], x, y)
            for _, cp in pairs:
                cp.wait_send()

        return start, lambda: None, finish


def _chip_sums(mine, land, q_ref, acc_ref, x, y):
    for j, (qx, qy) in enumerate(_other_chips(x, y)):
        qi = 2 * qx + qy
        q_ref[j] = (mine[qi].astype(F32) + land[qi].astype(F32)).astype(q_ref.dtype)
    mi = 2 * x + y
    acc_ref[...] = mine[mi].astype(F32) + land[mi].astype(F32)


def _direct_sum(v, buf, send_sems, recv_sems):
    x, y, c = _place()
    me = 4 * x + 2 * y + c
    buf[me] = v
    cps = []
    for k in range(1, NDEV):
        fx, fy, fc = (k >> 2) & 1, (k >> 1) & 1, k & 1
        peer = ((1 - x) if fx else x, (1 - y) if fy else y, (1 - c) if fc else c)
        cps.append((peer, pltpu.make_async_remote_copy(
            src_ref=buf.at[me], dst_ref=buf.at[me], send_sem=send_sems.at[k - 1], recv_sem=recv_sems.at[k - 1],
            device_id=peer, device_id_type=MESH)))
    for _, cp in cps:
        cp.start()
    for k, (peer, _) in enumerate(cps):
        theirs = 4 * peer[0] + 2 * peer[1] + peer[2]
        pltpu.make_async_remote_copy(
            src_ref=buf.at[theirs], dst_ref=buf.at[theirs], send_sem=send_sems.at[k], recv_sem=recv_sems.at[k],
            device_id=peer, device_id_type=MESH).wait_recv()
    acc = buf[0]
    for j in range(1, NDEV):
        acc = acc + buf[j]
    for _, cp in cps:
        cp.wait_send()
    return acc


def _direct_sum_scratch(shape, dtype):
    return [pltpu.VMEM((NDEV,) + tuple(shape), dtype), pltpu.SemaphoreType.DMA((NDEV - 1,)),
            pltpu.SemaphoreType.DMA((NDEV - 1,))]


def _fwd_a(x, nw, lnw, lnb, ws, bst, jobs, *, tm, relay_step):
    s_len = x.shape[0]
    nt = s_len // tm
    nch = tm // CH

    def main(i, ins, outs, scr):
        x_ref, nw_ref, lnw_ref, lnb_ref, ws_ref, bst_ref = ins
        z_ref, h_ref, y_ref, pp_ref = outs
        wc_scr, gv_scr, win_ref = scr

        @pl.when(i == 0)
        def _():
            m = _causal_mask()
            for g in range(G):
                wc_scr[g] = jnp.where(m, ws_ref[g], 0.0).astype(BF16)

        x = x_ref[...]
        h = (x * _rms(x) * nw_ref[...]).astype(BF16)
        h_ref[...] = h
        for k in range(NDEV):
            z_ref[:, k * CA:(k + 1) * CA] = _dot(h, win_ref[k])

        ssum = jnp.zeros((tm, 1), F32)
        for g in range(G):
            vs = slice(AW + g * GD, AW + (g + 1) * GD)
            gv, pv = _gelu_t(z_ref[:, vs])
            pp_ref[:, vs] = pv.astype(BF16)
            gv_scr[:, g * GD:(g + 1) * GD] = gv
            ssum = ssum + jnp.sum(gv, axis=-1, keepdims=True)
        mu = ssum * (1.0 / AW)
        vsum = jnp.zeros((tm, 1), F32)
        for g in range(G):
            dlt = gv_scr[:, g * GD:(g + 1) * GD] - mu
            vsum = vsum + jnp.sum(dlt * dlt, axis=-1, keepdims=True)
        rstd = lax.rsqrt(vsum * (1.0 / AW) + LN_EPS)

        for g in range(G):
            cs = slice(g * GD, (g + 1) * GD)
            gs = slice(2 * AW + g * GD, 2 * AW + (g + 1) * GD)
            v = (gv_scr[:, cs] - mu) * rstd * lnw_ref[:, cs] + lnb_ref[:, cs]
            vb = v.astype(BF16)
            u, pu = _gelu_t(z_ref[:, cs])
            pp_ref[:, cs] = pu.astype(BF16)
            zg = z_ref[:, gs]
            sig = _sigmoid(zg)
            pp_ref[:, gs] = sig.astype(BF16)
            sg = zg * sig
            for n in range(nch):
                rs = slice(n * CH, (n + 1) * CH)
                s = _dot(wc_scr[g], vb[rs, :]) + bst_ref[:, g:g + 1]
                y_ref[rs, cs] = (u[rs, :] * s * sg[rs, :]).astype(BF16)

    tile = lambda w: pl.BlockSpec((tm, w), lambda i: (i, 0))
    return _call(
        main, jobs, name="fwd_a", grid=(nt,), relay_step=relay_step, first=1,
        prologue=lambda gathered, scr: pltpu.sync_copy(gathered[0][0], scr[2]),
        ins=[x, nw, lnw, lnb, ws, bst], in_specs=[tile(D), _VMEM, _VMEM, _VMEM, _VMEM, _VMEM],
        out_shape=[_sds((s_len, 3 * AW), F32), _sds((s_len, D), BF16), _sds((s_len, AW), BF16),
                   _sds((s_len, 3 * AW), BF16)],
        out_specs=[tile(3 * AW), tile(D), tile(AW), tile(3 * AW)],
        scratch=[pltpu.VMEM((G, CH, CH), BF16), pltpu.VMEM((tm, AW), F32), pltpu.VMEM((NDEV, D, CA), BF16)])


def _bwd_a(dx1, z, pp, lnw, lnb, ws, bst, wout, jobs, *, tm, relay_step):
    s_len = dx1.shape[0]
    nt = s_len // tm
    nch = tm // CH

    def main(i, ins, outs, scr):
        dx1_ref, z_ref, pp_ref, lnw_ref, lnb_ref, ws_ref, bst_ref, wout_ref = ins
        dz_ref, glnw_ref, glnb_ref, gws_ref, gbst_ref = outs
        wc_scr, wct_scr, vh_scr, dgv_scr, dy_scr, dv_scr, gbs_acc, gwc_acc = scr

        @pl.when(i == 0)
        def _():
            m = _causal_mask()
            for g in range(G):
                wm = jnp.where(m, ws_ref[g], 0.0)
                wc_scr[g] = wm.astype(BF16)
                wct_scr[g] = wm.T.astype(BF16)
            glnw_ref[...] = jnp.zeros_like(glnw_ref)
            glnb_ref[...] = jnp.zeros_like(glnb_ref)
            gbs_acc[...] = jnp.zeros_like(gbs_acc)
            gwc_acc[...] = jnp.zeros_like(gwc_acc)

        dy_scr[...] = _dot_nt(dx1_ref[...], wout_ref[...])

        ssum = jnp.zeros((tm, 1), F32)
        for g in range(G):
            cs = slice(g * GD, (g + 1) * GD)
            vs = slice(AW + g * GD, AW + (g + 1) * GD)
            zv = z_ref[:, vs]
            pv = pp_ref[:, vs].astype(F32)
            gv = zv * pv
            vh_scr[:, cs] = gv
            dgv_scr[:, cs] = _dgelu(zv, pv)
            ssum = ssum + jnp.sum(gv, axis=-1, keepdims=True)
        mu = ssum * (1.0 / AW)
        vsum = jnp.zeros((tm, 1), F32)
        for g in range(G):
            dlt = vh_scr[:, g * GD:(g + 1) * GD] - mu
            vsum = vsum + jnp.sum(dlt * dlt, axis=-1, keepdims=True)
        rstd = lax.rsqrt(vsum * (1.0 / AW) + LN_EPS)

        m1 = jnp.zeros((tm, 1), F32)
        m2 = jnp.zeros((tm, 1), F32)
        for g in range(G):
            cs = slice(g * GD, (g + 1) * GD)
            gs = slice(2 * AW + g * GD, 2 * AW + (g + 1) * GD)
            vhat = (vh_scr[:, cs] - mu) * rstd
            vh_scr[:, cs] = vhat
            vb = (vhat * lnw_ref[:, cs] + lnb_ref[:, cs]).astype(BF16)
            zu = z_ref[:, cs]
            tu = pp_ref[:, cs].astype(F32)
            u = zu * tu
            zg = z_ref[:, gs]
            sig = pp_ref[:, gs].astype(F32)
            sg = zg * sig
            dy = dy_scr[:, cs]
            dsf = dy * u * sg
            dsb = dsf.astype(BF16)
            dvs = []
            for n in range(nch):
                rs = slice(n * CH, (n + 1) * CH)
                s = _dot(wc_scr[g], vb[rs, :]) + bst_ref[:, g:g + 1]
                dys = dy[rs, :] * s
                dz_ref[rs, cs] = (dys * sg[rs, :] * _dgelu(zu[rs, :], tu[rs, :])).astype(BF16)
                dz_ref[rs, gs] = (dys * u[rs, :] * (sig[rs, :] * (1.0 + zg[rs, :] * (1.0 - sig[rs, :])))).astype(BF16)
                gbs_acc[g] += dsf[rs, :]
                gwc_acc[g] += _dot_nt(dsb[rs, :], vb[rs, :])
                dvs.append(_dot(wct_scr[g], dsb[rs, :]))
            dv = jnp.concatenate(dvs, axis=0) if nch > 1 else dvs[0]
            glnw_ref[:, cs] += _rowsum(dv * vhat)
            glnb_ref[:, cs] += _rowsum(dv)
            dvh = dv * lnw_ref[:, cs]
            dv_scr[:, cs] = dvh
            m1 = m1 + jnp.sum(dvh, axis=-1, keepdims=True)
            m2 = m2 + jnp.sum(dvh * vhat, axis=-1, keepdims=True)
        m1 = m1 * (1.0 / AW)
        m2 = m2 * (1.0 / AW)
        for g in range(G):
            cs = slice(g * GD, (g + 1) * GD)
            dgv = rstd * (dv_scr[:, cs] - m1 - vh_scr[:, cs] * m2)
            dz_ref[:, AW + g * GD:AW + (g + 1) * GD] = (dgv * dgv_scr[:, cs]).astype(BF16)

        @pl.when(i == nt - 1)
        def _():
            m = _causal_mask()
            for g in range(G):
                gws_ref[g] = jnp.where(m, gwc_acc[g], 0.0)
                gbst_ref[:, g:g + 1] = jnp.sum(gbs_acc[g], axis=-1, keepdims=True)

    tile = lambda w: pl.BlockSpec((tm, w), lambda i: (i, 0))
    whole = lambda *s: pl.BlockSpec(s, lambda i: (0,) * len(s))
    big = lambda dt: pltpu.VMEM((tm, AW), dt)
    return _call(
        main, jobs, name="bwd_a", grid=(nt,), relay_step=relay_step,
        ins=[dx1, z, pp, lnw, lnb, ws, bst, wout],
        in_specs=[tile(D), tile(3 * AW), tile(3 * AW), _VMEM, _VMEM, _VMEM, _VMEM, _VMEM],
        out_shape=[_sds((s_len, 3 * AW), BF16), _sds((1, AW), F32), _sds((1, AW), F32), _sds((G, CH, CH), F32),
                   _sds((CH, G), F32)],
        out_specs=[tile(3 * AW), whole(1, AW), whole(1, AW), whole(G, CH, CH), whole(CH, G)],
        scratch=[pltpu.VMEM((G, CH, CH), BF16), pltpu.VMEM((G, CH, CH), BF16), big(F32), big(F32), big(F32), big(F32),
                 pltpu.VMEM((G, CH, GD), F32), pltpu.VMEM((G, CH, CH), F32)])


def _bwd_a_in(dz, dx1, x, nw, win8, jobs, *, tm, relay_step):
    s_len = x.shape[0]
    nt = s_len // tm

    def main(i, ins, outs, scr):
        dz_ref, dx1_ref, x_ref, nw_ref, win_ref = ins
        gx_ref, gnw_ref = outs

        @pl.when(i == 0)
        def _():
            gnw_ref[...] = jnp.zeros_like(gnw_ref)

        dh = jnp.zeros((tm, D), F32)
        for k in range(NDEV):
            dh = dh + _dot_nt(dz_ref[:, k * CA:(k + 1) * CA], win_ref[k])
        x = x_ref[...]
        r = _rms(x)
        gx_ref[...] = dx1_ref[...] + _rms_bwd(dh, x, r, nw_ref[...])
        gnw_ref[...] += _rowsum(dh * x * r)

        @pl.when(i == nt - 1)
        def _():
            gnw_ref[...] = _direct_sum(gnw_ref[...], *scr)

    tile = lambda w: pl.BlockSpec((tm, w), lambda i: (i, 0))
    return _call(
        main, jobs, name="bwd_a_in", grid=(nt,), relay_step=relay_step, forward_last=True,
        ins=[dz, dx1, x, nw, win8], in_specs=[tile(3 * AW), tile(D), tile(D), _VMEM, _VMEM],
        out_shape=[_sds((s_len, D), F32), _sds((1, D), F32)],
        out_specs=[tile(D), pl.BlockSpec((1, D), lambda i: (0, 0))], scratch=_direct_sum_scratch((1, D), F32))


def _conv(p8_ref, cs, xb, xm1, xm2, xm3):
    xc = p8_ref[4:5, cs] + p8_ref[3:4, cs] * xb
    xc = xc + p8_ref[0:1, cs] * xm3
    xc = xc + p8_ref[1:2, cs] * xm2
    return xc + p8_ref[2:3, cs] * xm1


def _gates(p8_ref, gcat_ref, hh, xc):
    cs = slice(hh * HD, (hh + 1) * HD)
    pre = _dot(xc.astype(BF16), gcat_ref[hh])
    r = _sigmoid(pre[:, :HD] + p8_ref[5:6, cs])
    ig = _sigmoid(pre[:, HD:] + p8_ref[6:7, cs])
    sp = _softplus_neg(p8_ref[7:8, cs])
    la = (-RG_C) * r * sp
    a = jnp.exp(la)
    half_log = 0.5 * jnp.log(jnp.tanh(-la) * (1.0 + a * a))
    return r, ig, sp, a, jnp.exp(half_log), jnp.exp(-half_log)


def _scan_rows(a_ref, b_ref, out_ref, carry, tm, reverse):
    row = lax.broadcasted_iota(jnp.int32, (SUBLANES, BW), 0)
    ngrp = tm // SUBLANES

    def step(j, cr):
        jj = (ngrp - 1 - j) if reverse else j
        off = pl.multiple_of(jj * SUBLANES, SUBLANES)
        a = a_ref[pl.ds(off, SUBLANES), :]
        b = b_ref[pl.ds(off, SUBLANES), :]
        for sh in (1, 2, 4):
            if reverse:
                a_s = pltpu.roll(a, SUBLANES - sh, 0)
                b_s = pltpu.roll(b, SUBLANES - sh, 0)
                m = row < SUBLANES - sh
            else:
                a_s = pltpu.roll(a, sh, 0)
                b_s = pltpu.roll(b, sh, 0)
                m = row >= sh
            b = jnp.where(m, a * b_s + b, b)
            a = jnp.where(m, a * a_s, a)
        o = b + a * cr
        out_ref[pl.ds(off, SUBLANES), :] = o
        return o[0:1, :] if reverse else o[SUBLANES - 1:SUBLANES, :]

    return lax.fori_loop(0, ngrp, step, carry)


def _fwd_b(x, ya, wout_a, nw, win8, p8, gcat, jobs, *, tm, relay_step):
    s_len = x.shape[0]
    nt = s_len // tm

    def main(i, ins, outs, scr):
        x_ref, ya_ref, wouta_ref, nw_ref, win_ref, p8_ref, gcat_ref = ins
        x1_ref, zb_ref, hs_ref, h1_ref, yb_ref, xc_ref, a_ref, cc_ref, r_ref, ig_ref, m_ref = outs
        xbe_scr, b_scr, k_scr, carry_scr = scr

        @pl.when(i == 0)
        def _():
            xbe_scr[0:SUBLANES, :] = jnp.zeros((SUBLANES, BW), F32)
            carry_scr[...] = jnp.zeros_like(carry_scr)

        x1 = x_ref[...] + _dot(ya_ref[...], wouta_ref[...])
        x1_ref[...] = x1
        h = (x1 * _rms(x1) * nw_ref[...]).astype(BF16)
        h1_ref[...] = h
        for k in range(NDEV):
            zb_ref[:, k * CB:(k + 1) * CB] = _dot(h, win_ref[k])
        xbe_scr[SUBLANES:SUBLANES + tm, :] = zb_ref[:, :BW]
        for hh in range(BH):
            cs = slice(hh * HD, (hh + 1) * HD)
            xc = _conv(p8_ref, cs, xbe_scr[SUBLANES:SUBLANES + tm, cs], xbe_scr[7:7 + tm, cs],
                       xbe_scr[6:6 + tm, cs], xbe_scr[5:5 + tm, cs])
            r, ig, _, a, mult, rm = _gates(p8_ref, gcat_ref, hh, xc)
            ixc = ig * xc
            xc_ref[:, cs] = xc
            a_ref[:, cs] = a
            r_ref[:, cs] = r.astype(BF16)
            ig_ref[:, cs] = ig.astype(BF16)
            m_ref[:, cs] = mult.astype(BF16)
            b_scr[:, cs] = mult * ixc
            k_scr[:, cs] = ixc * (a * a * rm)
        xbe_scr[0:SUBLANES, :] = xbe_scr[tm:tm + SUBLANES, :]
        carry_scr[...] = _scan_rows(a_ref, b_scr, hs_ref, carry_scr[...], tm, False)
        for hh in range(BH):
            cs = slice(hh * HD, (hh + 1) * HD)
            gt = zb_ref[:, BW + hh * HD:BW + (hh + 1) * HD]
            hsv = hs_ref[:, cs]
            yb_ref[:, cs] = (hsv * (gt * _sigmoid(gt))).astype(BF16)
            cc_ref[:, cs] = (hsv - b_scr[:, cs]) - k_scr[:, cs]

    tile = lambda w: pl.BlockSpec((tm, w), lambda i: (i, 0))
    wide = lambda dt: _sds((s_len, BW), dt)
    return _call(
        main, jobs, name="fwd_b", grid=(nt,), relay_step=relay_step,
        ins=[x, ya, wout_a, nw, win8, p8, gcat], in_specs=[tile(D), tile(AW), _VMEM, _VMEM, _VMEM, _VMEM, _VMEM],
        out_shape=[_sds((s_len, D), F32), _sds((s_len, 2 * BW), F32), wide(F32), _sds((s_len, D), BF16), wide(BF16),
                   wide(F32), wide(F32), wide(F32), wide(BF16), wide(BF16), wide(BF16)],
        out_specs=[tile(D), tile(2 * BW), tile(BW), tile(D)] + [tile(BW)] * 7,
        scratch=[pltpu.VMEM((tm + SUBLANES, BW), F32), pltpu.VMEM((tm, BW), F32), pltpu.VMEM((tm, BW), F32),
                 pltpu.VMEM((1, BW), F32)])


def _head(x1, yb, wout, nfw, tgt, *, tm):
    s_len = x1.shape[0]

    def main(i, ins, outs, scr):
        x1_ref, yb_ref, wout_ref, nfw_ref, t_ref = ins
        dx2_ref, dx2b_ref, loss_ref, gnfw_ref = outs

        @pl.when(i == 0)
        def _():
            loss_ref[...] = jnp.zeros_like(loss_ref)
            gnfw_ref[...] = jnp.zeros_like(gnfw_ref)

        x2 = x1_ref[...] + _dot(yb_ref[...], wout_ref[...])
        rf = _rms(x2)
        xn = x2 * rf
        e = xn * nfw_ref[...] - t_ref[...]
        loss_ref[...] += (0.5 / D) * jnp.sum(jnp.sum(e * e, axis=-1, keepdims=True), axis=0, keepdims=True)
        dyf = e * (1.0 / D)
        gnfw_ref[...] += _rowsum(dyf * xn)
        dx2 = _rms_bwd(dyf, x2, rf, nfw_ref[...])
        dx2_ref[...] = dx2
        dx2b_ref[...] = dx2.astype(BF16)

    tile = lambda w: pl.BlockSpec((tm, w), lambda i: (i, 0))
    whole = lambda *s: pl.BlockSpec(s, lambda i: (0,) * len(s))
    (dx2, dx2b, loss, gnfw), _ = _call(
        main, [], name="head", grid=(s_len // tm,),
        ins=[x1, yb, wout, nfw, tgt], in_specs=[tile(D), tile(BW), _VMEM, _VMEM, tile(D)],
        out_shape=[_sds((s_len, D), F32), _sds((s_len, D), BF16), _sds((1, 1), F32), _sds((1, D), F32)],
        out_specs=[tile(D), tile(D), whole(1, 1), whole(1, D)], scratch=[])
    return dx2, dx2b, loss, gnfw


def _bwd_b(dx2, zb, hs, x1, saved, nw, win8, p8, gcat, wout, *, tm):
    s_len = x1.shape[0]
    nt = s_len // tm

    def main(i, ins, outs, scr):
        (dx2_ref, zb_ref, hs_ref, x1_ref, xc_ref, a_ref, cc_ref, r_ref, ig_ref, m_ref,
         nw_ref, win_ref, p8_ref, gcat_ref, wout_ref) = ins
        dx1_ref, dx1b_ref, dzb_ref, gp8_ref, gga_ref, ggx_ref, gnw_ref = outs
        ae_scr, an_scr, dhd_scr, dh_scr, dy_scr, dxce_scr, carry_scr, afirst_scr = scr

        @pl.when(i == 0)
        def _():
            gp8_ref[...] = jnp.zeros_like(gp8_ref)
            gga_ref[...] = jnp.zeros_like(gga_ref)
            ggx_ref[...] = jnp.zeros_like(ggx_ref)
            gnw_ref[...] = jnp.zeros_like(gnw_ref)
            dxce_scr[tm:tm + SUBLANES, :] = jnp.zeros((SUBLANES, BW), F32)
            carry_scr[...] = jnp.zeros_like(carry_scr)
            afirst_scr[...] = jnp.zeros_like(afirst_scr)

        dx2 = dx2_ref[...]
        dy_scr[...] = _dot_nt(dx2.astype(BF16), wout_ref[...])
        for hh in range(BH):
            cs = slice(hh * HD, (hh + 1) * HD)
            gs = slice(BW + hh * HD, BW + (hh + 1) * HD)
            gt = zb_ref[:, gs]
            sig = _sigmoid(gt)
            dy = dy_scr[:, cs]
            dhd_scr[:, cs] = dy * (gt * sig)
            dzb_ref[:, gs] = (dy * hs_ref[:, cs] * (sig * (1.0 + gt * (1.0 - sig)))).astype(BF16)

        ae_scr[0:tm, :] = a_ref[...]
        ae_scr[tm:tm + SUBLANES, :] = jnp.broadcast_to(afirst_scr[...], (SUBLANES, BW))
        an_scr[...] = ae_scr[1:1 + tm, :]
        afirst_scr[...] = ae_scr[0:1, :]
        carry_scr[...] = _scan_rows(an_scr, dhd_scr, dh_scr, carry_scr[...], tm, True)

        for hh in range(BH):
            cs = slice(hh * HD, (hh + 1) * HD)
            dh = dh_scr[:, cs]
            mult = m_ref[:, cs].astype(F32)
            ig = ig_ref[:, cs].astype(F32)
            r = r_ref[:, cs].astype(F32)
            xc = xc_ref[:, cs]
            lam = p8_ref[7:8, cs]
            sp = _softplus_neg(lam)
            dla = dh * cc_ref[:, cs]
            gp8_ref[7:8, cs] += _rowsum(dla * ((-RG_C) * r)) * (-_sigmoid(-lam))
            dpr = dla * ((-RG_C) * sp) * (r * (1.0 - r))
            dpi = dh * mult * xc * (ig * (1.0 - ig))
            gp8_ref[5:6, cs] += _rowsum(dpr)
            gp8_ref[6:7, cs] += _rowsum(dpi)
            dcat = jnp.concatenate([dpr, dpi], axis=1).astype(BF16)
            dxc = dh * mult * ig + _dot_nt(dcat, gcat_ref[hh])
            gg = _dot(xc.T.astype(BF16), dcat)
            gga_ref[hh] += gg[:, :HD]
            ggx_ref[hh] += gg[:, HD:]
            dxce_scr[0:tm, cs] = dxc
            gp8_ref[4:5, cs] += _rowsum(dxc)
        for hh in range(BH):
            cs = slice(hh * HD, (hh + 1) * HD)
            xb = zb_ref[:, cs]
            d0, d1 = dxce_scr[0:tm, cs], dxce_scr[1:1 + tm, cs]
            d2, d3 = dxce_scr[2:2 + tm, cs], dxce_scr[3:3 + tm, cs]
            dzb_ref[:, cs] = (p8_ref[3:4, cs] * d0 + p8_ref[2:3, cs] * d1 + p8_ref[1:2, cs] * d2
                              + p8_ref[0:1, cs] * d3).astype(BF16)
            gp8_ref[3:4, cs] += _rowsum(d0 * xb)
            gp8_ref[2:3, cs] += _rowsum(d1 * xb)
            gp8_ref[1:2, cs] += _rowsum(d2 * xb)
            gp8_ref[0:1, cs] += _rowsum(d3 * xb)
        dxce_scr[tm:tm + SUBLANES, :] = dxce_scr[0:SUBLANES, :]

        dh1 = jnp.zeros((tm, D), F32)
        for k in range(NDEV):
            dh1 = dh1 + _dot_nt(dzb_ref[:, k * CB:(k + 1) * CB], win_ref[k])
        x1 = x1_ref[...]
        r1 = _rms(x1)
        dx1 = dx2 + _rms_bwd(dh1, x1, r1, nw_ref[...])
        dx1_ref[...] = dx1
        dx1b_ref[...] = dx1.astype(BF16)
        gnw_ref[...] += _rowsum(dh1 * x1 * r1)

    tile = lambda w: pl.BlockSpec((tm, w), lambda i: (nt - 1 - i, 0))
    whole = lambda *s: pl.BlockSpec(s, lambda i: (0,) * len(s))
    full = lambda: pltpu.VMEM((tm, BW), F32)
    ext = lambda: pltpu.VMEM((tm + SUBLANES, BW), F32)
    out, _ = _call(
        main, [], name="bwd_b", grid=(nt,),
        ins=[dx2, zb, hs, x1, *saved, nw, win8, p8, gcat, wout],
        in_specs=[tile(D), tile(2 * BW), tile(BW), tile(D)] + [tile(BW)] * 6 + [_VMEM] * 5,
        out_shape=[_sds((s_len, D), F32), _sds((s_len, D), BF16), _sds((s_len, 2 * BW), BF16), _sds((SUBLANES, BW), F32),
                   _sds((BH, HD, HD), F32), _sds((BH, HD, HD), F32), _sds((1, D), F32)],
        out_specs=[tile(D), tile(D), tile(2 * BW), whole(SUBLANES, BW), whole(BH, HD, HD), whole(BH, HD, HD),
                   whole(1, D)],
        scratch=[ext(), full(), full(), full(), full(), ext(), pltpu.VMEM((1, BW), F32), pltpu.VMEM((1, BW), F32)])
    return out


def _transpose_into(dst_ref, src_ref, rows):
    s_len = src_ref.shape[0]
    for r0 in range(0, s_len, rows):
        dst_ref[:, r0:r0 + rows] = src_ref[r0:r0 + rows, :].astype(F32).T.astype(BF16)


def _wgrad(a, b, jobs, *, by_rows, per, name, relay_step=0):
    s_len, m = a.shape
    n = b.shape[1]
    r, cd = (m // NDEV, n) if by_rows else (m, n // NDEV)
    nsteps = NDEV // per
    at_rows = per * r if by_rows else m

    def main(i, ins, outs, scr):
        a_ref, b_ref = ins
        q_ref, acc_ref = outs
        at_scr, stage, mine, land, send_sems, recv_sems = scr
        x, y, c = _place()

        def to_sibling(pi):
            return pltpu.make_async_remote_copy(
                src_ref=stage.at[pi & 1], dst_ref=land.at[pi], send_sem=send_sems.at[pi], recv_sem=recv_sems.at[pi],
                device_id=(x, y, 1 - c), device_id_type=MESH)

        if by_rows:
            _transpose_into(at_scr, a_ref, TRANSPOSE_ROWS)
        else:
            @pl.when(i == 0)
            def _():
                _transpose_into(at_scr, a_ref, TRANSPOSE_ROWS)

        res = _dot(at_scr[...], b_ref[...]).astype(BF16)
        for k in range(per):
            blk = per * i + k
            pi, pc = blk >> 1, blk & 1
            val = res[k * r:(k + 1) * r, :] if by_rows else res

            @pl.when(pc != c)
            def _():
                @pl.when(pi >= 2)
                def _():
                    to_sibling(pi - 2).wait_send()

                stage[pi & 1] = val
                to_sibling(pi).start()

            @pl.when(pc == c)
            def _():
                mine[pi] = val

        @pl.when(i == nsteps - 1)
        def _():
            for p in range(4):
                to_sibling(p).wait_recv()
            to_sibling(2).wait_send()
            to_sibling(3).wait_send()
            _chip_sums(mine, land, q_ref, acc_ref, x, y)

    if by_rows:
        in_specs = [pl.BlockSpec((s_len, at_rows), lambda j: (0, j)), _VMEM]
    else:
        in_specs = [_VMEM, pl.BlockSpec((s_len, cd), lambda j: (0, j))]
    blk_vmem = lambda k: pltpu.VMEM((k, r, cd), BF16)
    (q, acc), job_out = _call(
        main, jobs, name=name, grid=(nsteps,), relay_step=relay_step, ins=[a, b], in_specs=in_specs,
        out_shape=[_sds((NCHIP_OTHER, r, cd), BF16), _sds((r, cd), F32)],
        out_specs=[pl.BlockSpec((NCHIP_OTHER, r, cd), lambda j: (0, 0, 0)), pl.BlockSpec((r, cd), lambda j: (0, 0))],
        scratch=[pltpu.VMEM((at_rows, s_len), BF16), blk_vmem(2), blk_vmem(4), blk_vmem(4),
                 pltpu.SemaphoreType.DMA((4,)), pltpu.SemaphoreType.DMA((4,))])
    return q, acc, job_out


def _wgrad_cols_early(a, b, jobs, *, name, relay_step=0):
    s_len, m = a.shape
    r, cd = m, b.shape[1] // NDEV
    h = r // 2

    def chip_at(pos, base):
        return base ^ (3 - pos)

    def main(i, ins, outs, scr):
        a_ref, b_ref = ins
        q_ref, acc_ref, rel_ref = outs
        at_scr, stage, mine, land, q2_scr, send_sems, recv_sems, via_send, via_recv = scr
        x, y, c = _place()
        base = 2 * x + y
        xn, yn, _ = _other_chips(x, y)
        pos, pc = i >> 1, i & 1
        pi = chip_at(pos, base)

        def to_sibling(chip, slot):
            return pltpu.make_async_remote_copy(
                src_ref=stage.at[slot], dst_ref=land.at[chip], send_sem=send_sems.at[chip],
                recv_sem=recv_sems.at[chip], device_id=(x, y, 1 - c), device_id_type=MESH)

        def via(k):
            return pltpu.make_async_remote_copy(
                src_ref=q2_scr.at[pl.ds(k * h, h)], dst_ref=rel_ref.at[k], send_sem=via_send.at[k],
                recv_sem=via_recv.at[k], device_id=(*(xn, yn)[k], c), device_id_type=MESH)

        @pl.when(i == 0)
        def _():
            _transpose_into(at_scr, a_ref, TRANSPOSE_ROWS)

        res = _dot(at_scr[...], b_ref[...]).astype(BF16)

        @pl.when(pc != c)
        def _():
            @pl.when(pos >= 2)
            def _():
                to_sibling(chip_at(pos - 2, base), pos & 1).wait_send()

            stage[pos & 1] = res
            to_sibling(pi, pos & 1).start()

        @pl.when(pc == c)
        def _():
            mine[pi] = res

        @pl.when(i == 1)
        def _():
            dg = chip_at(0, base)
            to_sibling(dg, 0).wait_recv()
            q2 = (mine[dg].astype(F32) + land[dg].astype(F32)).astype(BF16)
            q2_scr[...] = q2
            q_ref[2] = q2
            via(0).start()
            via(1).start()

        @pl.when(i == NDEV - 1)
        def _():
            for pos_ in (1, 2, 3):
                to_sibling(chip_at(pos_, base), 0).wait_recv()
            to_sibling(chip_at(2, base), 0).wait_send()
            to_sibling(chip_at(3, base), 1).wait_send()
            for k in range(2):
                via(k).wait_recv()
            for k in range(2):
                via(k).wait_send()
            for j, chip in enumerate((base ^ 2, base ^ 1)):
                q_ref[j] = (mine[chip].astype(F32) + land[chip].astype(F32)).astype(BF16)
            acc_ref[...] = mine[base].astype(F32) + land[base].astype(F32)

    def b_block(j):
        base = 2 * lax.axis_index("x") + lax.axis_index("y")
        return (0, 2 * chip_at(j >> 1, base) + (j & 1))

    blk_vmem = lambda k: pltpu.VMEM((k, r, cd), BF16)
    (q, acc, rel), job_out = _call(
        main, jobs, name=name, grid=(NDEV,), relay_step=relay_step, ins=[a, b],
        in_specs=[_VMEM, pl.BlockSpec((s_len, cd), b_block)],
        out_shape=[_sds((NCHIP_OTHER, r, cd), BF16), _sds((r, cd), F32), _sds((2, h, cd), BF16)],
        out_specs=[pl.BlockSpec((NCHIP_OTHER, r, cd), lambda j: (0, 0, 0)), pl.BlockSpec((r, cd), lambda j: (0, 0)), _HBM],
        scratch=[pltpu.VMEM((m, s_len), BF16), blk_vmem(2), blk_vmem(4), blk_vmem(4), pltpu.VMEM((r, cd), BF16),
                 pltpu.SemaphoreType.DMA((4,)), pltpu.SemaphoreType.DMA((4,)), pltpu.SemaphoreType.DMA((2,)),
                 pltpu.SemaphoreType.DMA((2,))])
    return q, acc, rel, job_out


class _ExchangeRest:
    def __init__(self, q, relayed):
        _, r, cd = q.shape
        half = (2, r // 2, cd)
        self.ins, self.in_specs = [q, relayed], [_HBM, _HBM]
        self.out_shape, self.out_specs = [_sds((2, r, cd), q.dtype)], [_HBM]
        self.scratch = [pltpu.VMEM(half, q.dtype), pltpu.VMEM(half, q.dtype), pltpu.VMEM(half, q.dtype),
                        pltpu.SemaphoreType.DMA((4,)), pltpu.SemaphoreType.DMA((4,)), pltpu.SemaphoreType.DMA((4,))]

    def ops(self, ins, outs, scr):
        (q, rel_in), (land,) = ins, outs
        own, rel, comb, send_sems, recv_sems, local_sems = scr
        h = q.shape[1] // 2
        x, y, c = _place()
        xn, yn, _ = _other_chips(x, y)
        h0, h1 = pl.ds(0, h), pl.ds(h, h)

        def remote(k, src, dst, chip):
            return pltpu.make_async_remote_copy(src_ref=src, dst_ref=dst, send_sem=send_sems.at[k],
                                                recv_sem=recv_sems.at[k], device_id=(*chip, c), device_id_type=MESH)

        def sends():
            return [remote(0, q.at[0, h0], land.at[0, h0], xn), remote(1, q.at[1, h1], land.at[1, h1], yn),
                    remote(2, comb.at[0], land.at[1, h0], yn), remote(3, comb.at[1], land.at[0, h1], xn)]

        def loads():
            return [pltpu.make_async_copy(q.at[1, h0], own.at[0], local_sems.at[0]),
                    pltpu.make_async_copy(q.at[0, h1], own.at[1], local_sems.at[1]),
                    pltpu.make_async_copy(rel_in.at[0], rel.at[0], local_sems.at[2]),
                    pltpu.make_async_copy(rel_in.at[1], rel.at[1], local_sems.at[3])]

        def start():
            cps, lds = sends(), loads()
            for ld in lds:
                ld.start()
            cps[0].start()
            cps[1].start()
            for ld in lds:
                ld.wait()
            for k in range(2):
                comb[k] = (own[k].astype(F32) + rel[k].astype(F32)).astype(comb.dtype)
            cps[2].start()
            cps[3].start()

        def finish():
            cps = sends()
            for cp in cps:
                cp.wait_recv()
            for cp in cps:
                cp.wait_send()

        return start, lambda: None, finish


def _adam_math(w, g, m, v):
    m = B1 * m + (1.0 - B1) * g
    v = B2 * v + (1.0 - B2) * (g * g)
    m_hat = m / (1.0 - B1 ** STEP)
    v_hat = v / (1.0 - B2 ** STEP)
    delta = (-LR) * (m_hat / (jnp.sqrt(v_hat) + ADAM_EPS) + WD * w)
    return delta, m, v


def _adam_big(w, acc, land, m, v, name):
    r, cd = w.shape
    rb = ADAM_ROWS if r % ADAM_ROWS == 0 else r
    nland = land.shape[0]

    def body(w_ref, acc_ref, land_ref, m_ref, v_ref, g_ref, d_ref, mo_ref, vo_ref):
        g = acc_ref[...]
        for j in range(nland):
            g = g + land_ref[j].astype(F32)
        g_ref[...] = g
        d_ref[...], mo_ref[...], vo_ref[...] = _adam_math(w_ref[...], g, m_ref[...], v_ref[...])

    blk = pl.BlockSpec((rb, cd), lambda i: (i, 0))
    blk3 = pl.BlockSpec((nland, rb, cd), lambda i: (0, i, 0))
    return pl.pallas_call(
        body, name=name, grid=(r // rb,), in_specs=[blk, blk, blk3, blk, blk], out_specs=[blk] * 4,
        out_shape=[_sds((r, cd), F32)] * 4,
        compiler_params=_params(dimension_semantics=("arbitrary",)),
    )(w, acc, land, m, v)


def _adam_small(groups):
    n = len(groups)

    def body(*refs):
        ins, outs = refs[:4 * n], refs[4 * n:]
        for k in range(n):
            w_ref, g_ref, m_ref, v_ref = ins[4 * k:4 * k + 4]
            d, mo, vo = _adam_math(w_ref[...], g_ref[...], m_ref[...], v_ref[...])
            outs[3 * k][...] = d
            outs[3 * k + 1][...] = mo
            outs[3 * k + 2][...] = vo

    flat = [a for grp in groups for a in grp]
    shapes = [_sds(grp[0].shape, F32) for grp in groups for _ in range(3)]
    res = pl.pallas_call(
        body, name="adam_small", in_specs=[_VMEM] * (4 * n), out_specs=[_VMEM] * (3 * n), out_shape=shapes,
        compiler_params=_params(),
    )(*flat)
    return [tuple(res[3 * k:3 * k + 3]) for k in range(n)]


TM_FWD_A = 256
RELAY_STEP_FWD_A = 2
RELAY_STEP_FWD_B = 2
TM_BWD_A = 256
RELAY_STEP_BWD_A = 3
TM_BWD_A_IN = 256
RELAY_STEP_BWD_A_IN = 4
RELAY_STEP_WGRAD_A_IN = 2
TM_FWD_B = 256
TM_HEAD = 512
TM_BWD_B = 256


def _pack(parts, rows):
    flat = jnp.concatenate([p.reshape(-1) for p in parts])
    return jnp.pad(flat, (0, NDEV * rows * LANES - flat.shape[0])).reshape(NDEV, rows, LANES)


def _unpack(packed, shapes):
    flat, out, off = packed.reshape(-1), [], 0
    for s in shapes:
        size = 1
        for d in s:
            size *= d
        out.append(flat[off:off + size].reshape(s))
        off += size
    return out


def kernel(x, norm_w, a_w_in, a_ln_w, a_ln_b, a_w_s, a_b_s, a_w_out, b_w_in, b_conv_w, b_conv_b, b_gate_a_w, b_gate_a_b, b_gate_x_w, b_gate_x_b, b_lambda, b_w_out, norm_f_w, loss_target, m_norm_w, m_a_w_in, m_a_ln_w, m_a_ln_b, m_a_w_s, m_a_b_s, m_a_w_out, m_b_w_in, m_b_conv_w, m_b_conv_b, m_b_gate_a_w, m_b_gate_a_b, m_b_gate_x_w, m_b_gate_x_b, m_b_lambda, m_b_w_out, m_norm_f_w, v_norm_w, v_a_w_in, v_a_ln_w, v_a_ln_b, v_a_w_s, v_a_b_s, v_a_w_out, v_b_w_in, v_b_conv_w, v_b_conv_b, v_b_gate_a_w, v_b_gate_a_b, v_b_gate_x_w, v_b_gate_x_b, v_b_lambda, v_b_w_out, v_norm_f_w):
    me = 4 * lax.axis_index("x") + 2 * lax.axis_index("y") + lax.axis_index("c")
    xs, tgt = x[0], loss_target[0]
    nw0, nw1, nfw = norm_w[0:1], norm_w[1:2], norm_f_w.reshape(1, D)
    w_s, bst = a_w_s[0], a_b_s[0].T
    gcat = jnp.concatenate([b_gate_a_w[0], b_gate_x_w[0]], axis=-1).astype(BF16)

    p8_shard = jnp.concatenate([b_conv_w[0], b_conv_b, b_gate_a_b, b_gate_x_b, b_lambda], axis=0)
    (z, h0, ya, pp), ((win_a8, p8_all), (wout_a8, win_b8)) = _fwd_a(
        xs, nw0, a_ln_w, a_ln_b, w_s, bst,
        [_Gather([a_w_in[0], p8_shard], [BF16, F32]), _Gather([a_w_out[0], b_w_in[0]], [BF16, BF16])],
        tm=TM_FWD_A, relay_step=RELAY_STEP_FWD_A)
    p8 = jnp.transpose(p8_all, (1, 0, 2)).reshape(SUBLANES, BW)
    wout_a = wout_a8.reshape(AW, D)
    (x1, zb, hs, h1, yb, *saved_b), ((wout_b8,),) = _fwd_b(
        xs, ya, wout_a, nw1, win_b8, p8, gcat, [_Gather([b_w_out[0]], [BF16])],
        tm=TM_FWD_B, relay_step=RELAY_STEP_FWD_B)
    wout_b = wout_b8.reshape(BW, D)
    dx2, dx2b, loss, g_nfw = _head(x1, yb, wout_b, nfw, tgt, tm=TM_HEAD)

    dx1, dx1b, dzb, g_p8, g_ga, g_gx, g_nw1 = _bwd_b(dx2, zb, hs, x1, saved_b, nw1, win_b8, p8, gcat, wout_b,
                                                     tm=TM_BWD_B)
    q_wout_b, acc_wout_b, _ = _wgrad(yb, dx2b, [], by_rows=True, per=2, name="wgrad_b_out")
    shapes_b = [(1, D), (1, D), (SUBLANES, BW), (1, 1)]
    pack_b = _pack([g_nfw, g_nw1, g_p8, loss], 16)
    small_b = _InChip([g_ga.reshape(NDEV, -1, HD), g_gx.reshape(NDEV, -1, HD), pack_b])
    q_win_b, acc_win_b, (sm_b, (l_wout_b,)) = _wgrad(h1, dzb, [small_b, _Exchange([q_wout_b])], by_rows=False, per=1,
                                                      name="wgrad_b_in")
    qs_b, accs_b = sm_b[:3], sm_b[3:]

    (dz, g_lnw, g_lnb, g_ws, g_bst), (lands_b, (l_win_b,)) = _bwd_a(
        dx1b, z, pp, a_ln_w, a_ln_b, w_s, bst, wout_a, [_Exchange(qs_b), _ExchangeVia(q_win_b)],
        tm=TM_BWD_A, relay_step=RELAY_STEP_BWD_A)
    shapes_a = [(1, AW), (1, AW), (CH, G)]
    pack_a = _pack([g_lnw, g_lnb, g_bst], 8)
    q_wout_a, acc_wout_a, (red_b, sm_a) = _wgrad(
        ya, dx1b, [_SumGather(accs_b, lands_b), _InChip([g_ws, pack_a])], by_rows=True, per=2,
        name="wgrad_a_out", relay_step=1)
    qs_a, accs_a = sm_a[:2], sm_a[2:]
    q_win_a, acc_win_a, rel_a, (lands_a, (l_wout_a,)) = _wgrad_cols_early(
        h0, dz, [_Exchange(qs_a), _ExchangeVia(q_wout_a)], name="wgrad_a_in", relay_step=RELAY_STEP_WGRAD_A_IN)
    (gx, g_nw0), (red_a, (l_win_a,)) = _bwd_a_in(
        dz, dx1, xs, nw0, win_a8, [_SumGather(accs_a, lands_a), _ExchangeRest(q_win_a, rel_a)],
        tm=TM_BWD_A_IN, relay_step=RELAY_STEP_BWD_A_IN)

    r_ga, r_gx, r_pack_b = red_b
    r_nfw, r_nw1, r_p8, loss = _unpack(r_pack_b, shapes_b)
    r_ws, r_pack_a = red_a
    r_lnw, r_lnb, r_bst = _unpack(r_pack_a, shapes_a)
    g_p8 = lax.dynamic_slice_in_dim(r_p8, me * (BW // NDEV), BW // NDEV, axis=1)
    loss = loss[0, 0]

    weights = dict(norm_w=norm_w, a_w_in=a_w_in, a_ln_w=a_ln_w, a_ln_b=a_ln_b, a_w_s=a_w_s, a_b_s=a_b_s, a_w_out=a_w_out,
                   b_w_in=b_w_in, b_conv_w=b_conv_w, b_conv_b=b_conv_b, b_gate_a_w=b_gate_a_w, b_gate_a_b=b_gate_a_b,
                   b_gate_x_w=b_gate_x_w, b_gate_x_b=b_gate_x_b, b_lambda=b_lambda, b_w_out=b_w_out, norm_f_w=norm_f_w)
    mom1 = dict(norm_w=m_norm_w, a_w_in=m_a_w_in, a_ln_w=m_a_ln_w, a_ln_b=m_a_ln_b, a_w_s=m_a_w_s, a_b_s=m_a_b_s,
                a_w_out=m_a_w_out, b_w_in=m_b_w_in, b_conv_w=m_b_conv_w, b_conv_b=m_b_conv_b, b_gate_a_w=m_b_gate_a_w,
                b_gate_a_b=m_b_gate_a_b, b_gate_x_w=m_b_gate_x_w, b_gate_x_b=m_b_gate_x_b, b_lambda=m_b_lambda,
                b_w_out=m_b_w_out, norm_f_w=m_norm_f_w)
    mom2 = dict(norm_w=v_norm_w, a_w_in=v_a_w_in, a_ln_w=v_a_ln_w, a_ln_b=v_a_ln_b, a_w_s=v_a_w_s, a_b_s=v_a_b_s,
                a_w_out=v_a_w_out, b_w_in=v_b_w_in, b_conv_w=v_b_conv_w, b_conv_b=v_b_conv_b, b_gate_a_w=v_b_gate_a_w,
                b_gate_a_b=v_b_gate_a_b, b_gate_x_w=v_b_gate_x_w, b_gate_x_b=v_b_gate_x_b, b_lambda=v_b_lambda,
                b_w_out=v_b_w_out, norm_f_w=v_norm_f_w)
    names = list(weights)

    def as2d(a):
        return a.reshape(-1, a.shape[-1])

    upd, grads = {}, {}
    for k, acc, land in (("a_w_in", acc_win_a, l_win_a), ("a_w_out", acc_wout_a, l_wout_a),
                         ("b_w_in", acc_win_b, l_win_b), ("b_w_out", acc_wout_b, l_wout_b)):
        g, d, mo, vo = _adam_big(as2d(weights[k]), acc, land, as2d(mom1[k]), as2d(mom2[k]), "adam_" + k)
        grads[k] = g[None]
        upd[k] = (d, mo, vo)
    grads.update(
        norm_w=jnp.concatenate([g_nw0, r_nw1], axis=0), a_ln_w=r_lnw, a_ln_b=r_lnb,
        a_w_s=r_ws.reshape(1, G, CH, CH), a_b_s=r_bst.T[None],
        b_conv_w=g_p8[None, 0:4], b_conv_b=g_p8[4:5], b_gate_a_w=r_ga.reshape(1, BH, HD, HD), b_gate_a_b=g_p8[5:6],
        b_gate_x_w=r_gx.reshape(1, BH, HD, HD), b_gate_x_b=g_p8[6:7], b_lambda=g_p8[7:8], norm_f_w=r_nfw.reshape(D))
    small_names = [k for k in names if k not in upd]
    res = _adam_small([(as2d(weights[k]), as2d(grads[k]), as2d(mom1[k]), as2d(mom2[k])) for k in small_names])
    for k, r3 in zip(small_names, res):
        upd[k] = r3
    deltas = [upd[k][0].reshape(weights[k].shape) for k in names]
    new_m = [upd[k][1].reshape(weights[k].shape) for k in names]
    new_v = [upd[k][2].reshape(weights[k].shape) for k in names]
    return (loss, gx[None], *[grads[k] for k in names], *deltas, *new_m, *new_v)
```

```python
import jax
import jax.numpy as jnp
from jax import lax
from jax.experimental import pallas as pl
from jax.experimental.pallas import tpu as pltpu

F32 = jnp.float32
BF16 = jnp.bfloat16
MESH = pl.DeviceIdType.MESH

NDEV = 8
NCHIP_OTHER = 3
D = 1024
AW = 2048
G = 8
GD = AW // G
CH = 128
BW = 1536
BH = 12
HD = BW // BH
CA = 3 * AW // NDEV
CB = 2 * BW // NDEV
RMS_EPS = 1e-6
LN_EPS = 1e-5
RG_C = 8.0
LR, B1, B2, ADAM_EPS, WD, STEP = 0.001, 0.9, 0.999, 1e-08, 0.01, 10
V7X_VMEM_BYTES = 64 * 1024 * 1024
VMEM_LIMIT = V7X_VMEM_BYTES - 8 * 1024 * 1024
SUBLANES = 8
LANES = 128
BF16_ROWS = 16
TRANSPOSE_ROWS = 256
ADAM_ROWS = 512
GELU_C = 0.7978845608028654
GELU_K = 0.044715

_VMEM = pl.BlockSpec(memory_space=pltpu.VMEM)
_HBM = pl.BlockSpec(memory_space=pltpu.HBM)


def _sds(shape, dtype):
    return jax.ShapeDtypeStruct(tuple(shape), dtype)


def _params(**kw):
    return pltpu.CompilerParams(vmem_limit_bytes=VMEM_LIMIT, **kw)


def _gelu_t(z):
    p = 0.5 * jnp.tanh(z * (GELU_C + (GELU_C * GELU_K) * (z * z))) + 0.5
    return z * p, p


def _dgelu(z, p):
    return p * (1.0 + (z * (1.0 - p)) * (2.0 * GELU_C + (6.0 * GELU_C * GELU_K) * (z * z)))


def _sigmoid(v):
    return 0.5 * jnp.tanh(0.5 * v) + 0.5


def _softplus_neg(lam):
    return jnp.maximum(-lam, 0.0) + jnp.log1p(jnp.exp(-jnp.abs(lam)))


def _dot(a, b):
    return jnp.dot(a, b, preferred_element_type=F32)


def _dot_nt(a, b):
    return lax.dot_general(a, b, (((1,), (1,)), ((), ())), preferred_element_type=F32)


def _rowsum(v):
    return jnp.sum(v, axis=0, keepdims=True)


def _causal_mask():
    r = lax.broadcasted_iota(jnp.int32, (CH, CH), 0)
    c = lax.broadcasted_iota(jnp.int32, (CH, CH), 1)
    return r >= c


def _rms(x):
    return lax.rsqrt(jnp.mean(x * x, axis=-1, keepdims=True) + RMS_EPS)


def _rms_bwd(dh, x, r, nw):
    gy = dh * nw
    return r * gy - x * (r * r * r) * jnp.mean(gy * x, axis=-1, keepdims=True)


def _place():
    return lax.axis_index("x"), lax.axis_index("y"), lax.axis_index("c")


def _other_chips(x, y):
    return [(1 - x, y), (x, 1 - y), (1 - x, 1 - y)]


GATHER_SLOTS = 10


def _gather_ops(ins, outs, send_sems, recv_sems, local_sems):
    n = len(ins)
    x, y, c = _place()
    sibling = (x, y, 1 - c)
    xn, yn, dg = _other_chips(x, y)
    split = [ins[i].shape[0] % (2 * BF16_ROWS) == 0 for i in range(n)]

    def blk(chip, core):
        return 4 * chip[0] + 2 * chip[1] + core

    me = blk((x, y), c)

    def part(ref, i, half):
        if half is None:
            return ref
        h = ins[i].shape[0] // 2
        return ref.at[pl.ds(half * h, h)]

    def copy(i, k, block, to, half=None, src=None):
        dst = part(outs[i].at[block], i, half)
        return pltpu.make_async_remote_copy(
            src_ref=dst if src is None else part(src, i, half), dst_ref=dst,
            send_sem=send_sems.at[k, i], recv_sem=recv_sems.at[k, i], device_id=to, device_id_type=MESH)

    def first_copies():
        mine = [pltpu.make_async_copy(ins[i], outs[i].at[me], local_sems.at[i]) for i in range(n)]
        first = []
        for i in range(n):
            first.append(copy(i, 0, me, sibling, src=ins[i]))
            if split[i]:
                first.append(copy(i, 1, me, (*xn, c), 0, ins[i]))
                first.append(copy(i, 3, me, (*yn, c), 1, ins[i]))
                first.append(copy(i, 2, me, (*xn, c), 1, ins[i]))
                first.append(copy(i, 4, me, (*yn, c), 0, ins[i]))
            else:
                first.append(copy(i, 1, me, (*xn, c), None, ins[i]))
                first.append(copy(i, 3, me, (*yn, c), None, ins[i]))
                first.append(copy(i, 5, me, (*dg, c), None, ins[i]))
        return mine, first

    def onward():
        out = []
        for i in range(n):
            if split[i]:
                out.append(copy(i, 5, blk(xn, c), (*yn, c), 0))
                out.append(copy(i, 6, blk(yn, c), (*xn, c), 1))
        return out

    def start():
        mine, first = first_copies()
        for cp in mine + first:
            cp.start()

    def relay():
        sends = onward()
        for i in range(n):
            if split[i]:
                copy(i, 1, blk(xn, c), sibling, 0).wait_recv()
                sends.pop(0).start()
                copy(i, 3, blk(yn, c), sibling, 1).wait_recv()
                sends.pop(0).start()

    def passes():
        return [copy(i, 7 + j, blk(chip, c), sibling) for i in range(n) for j, chip in enumerate((xn, yn, dg))]

    def forward():
        fwd = passes()
        for i in range(n):
            if split[i]:
                copy(i, 2, blk(xn, c), sibling, 1).wait_recv()
                fwd[3 * i].start()
                copy(i, 4, blk(yn, c), sibling, 0).wait_recv()
                fwd[3 * i + 1].start()
                copy(i, 5, blk(dg, c), sibling, 0).wait_recv()
                copy(i, 6, blk(dg, c), sibling, 1).wait_recv()
                fwd[3 * i + 2].start()
            else:
                copy(i, 1, blk(xn, c), sibling).wait_recv()
                fwd[3 * i].start()
                copy(i, 3, blk(yn, c), sibling).wait_recv()
                fwd[3 * i + 1].start()
                copy(i, 5, blk(dg, c), sibling).wait_recv()
                fwd[3 * i + 2].start()

    def finish():
        mine, first = first_copies()
        for i in range(n):
            copy(i, 0, blk((x, y), 1 - c), sibling).wait_recv()
            for j, chip in enumerate((xn, yn, dg)):
                copy(i, 7 + j, blk(chip, 1 - c), sibling).wait_recv()
        for cp in first + passes() + onward():
            cp.wait_send()
        for cp in mine:
            cp.wait()

    return start, relay, forward, finish


def _gather_sems(n):
    return [pltpu.SemaphoreType.DMA((GATHER_SLOTS, n)), pltpu.SemaphoreType.DMA((GATHER_SLOTS, n)),
            pltpu.SemaphoreType.DMA((n,))]


class _Gather:
    def __init__(self, shards, as_dtypes=None):
        n = len(shards)
        dts = [s.dtype for s in shards] if as_dtypes is None else list(as_dtypes)
        self.cast = [jnp.dtype(d) != s.dtype for d, s in zip(dts, shards)]
        self.ins = list(shards)
        self.in_specs = [_VMEM if c else _HBM for c in self.cast]
        self.out_shape = [_sds((NDEV,) + s.shape, d) for s, d in zip(shards, dts)]
        self.out_specs = [_HBM] * n
        self.scratch = [pltpu.VMEM(s.shape, d) for s, d, c in zip(shards, dts, self.cast) if c] + _gather_sems(n)

    def ops(self, ins, outs, scr):
        ncast = sum(self.cast)
        staged = iter(scr[:ncast])
        srcs = [next(staged) if c else ref for c, ref in zip(self.cast, ins)]
        start, relay, forward, finish = _gather_ops(srcs, outs, *scr[ncast:])

        def cast_and_start():
            for c, ref, src in zip(self.cast, ins, srcs):
                if c:
                    src[...] = ref[...].astype(src.dtype)
            start()

        return cast_and_start, relay, forward, finish


class _Exchange:
    def __init__(self, qs):
        n = len(qs)
        self.ins, self.in_specs = list(qs), [_HBM] * n
        self.out_shape = [_sds(q.shape, q.dtype) for q in qs]
        self.out_specs = [_HBM] * n
        self.scratch = [pltpu.SemaphoreType.DMA((NCHIP_OTHER, n)), pltpu.SemaphoreType.DMA((NCHIP_OTHER, n))]

    def ops(self, ins, outs, scr):
        send_sems, recv_sems = scr
        n = len(ins)
        x, y, c = _place()
        chips = _other_chips(x, y)

        def copies():
            return [pltpu.make_async_remote_copy(
                src_ref=ins[i].at[j], dst_ref=outs[i].at[j], send_sem=send_sems.at[j, i],
                recv_sem=recv_sems.at[j, i], device_id=(*chips[j], c), device_id_type=MESH)
                for i in range(n) for j in range(NCHIP_OTHER)]

        def start():
            for cp in copies():
                cp.start()

        def finish():
            cps = copies()
            for cp in cps:
                cp.wait_recv()
            for cp in cps:
                cp.wait_send()

        return start, lambda: None, finish


class _ExchangeVia:
    def __init__(self, q):
        _, r, cd = q.shape
        half = (2, r // 2, cd)
        self.ins, self.in_specs = [q], [_HBM]
        self.out_shape, self.out_specs = [_sds((2, r, cd), q.dtype)], [_HBM]
        self.scratch = [pltpu.VMEM(half, q.dtype), pltpu.VMEM(half, q.dtype), pltpu.VMEM(half, q.dtype),
                        pltpu.SemaphoreType.DMA((6,)), pltpu.SemaphoreType.DMA((6,)), pltpu.SemaphoreType.DMA((2,))]

    def ops(self, ins, outs, scr):
        (q,), (land,) = ins, outs
        relayed, own, comb, send_sems, recv_sems, local_sems = scr
        h = q.shape[1] // 2
        x, y, c = _place()
        xn, yn, _ = _other_chips(x, y)
        h0, h1 = pl.ds(0, h), pl.ds(h, h)

        def remote(k, src, dst, chip):
            return pltpu.make_async_remote_copy(src_ref=src, dst_ref=dst, send_sem=send_sems.at[k],
                                                recv_sem=recv_sems.at[k], device_id=(*chip, c), device_id_type=MESH)

        def via():
            return [remote(2, q.at[2, h0], relayed.at[0], xn), remote(3, q.at[2, h1], relayed.at[1], yn)]

        def direct():
            return [remote(0, q.at[0, h0], land.at[0, h0], xn), remote(1, q.at[1, h1], land.at[1, h1], yn)]

        def second():
            return [remote(4, comb.at[0], land.at[1, h0], yn), remote(5, comb.at[1], land.at[0, h1], xn)]

        def mine():
            return [pltpu.make_async_copy(q.at[1, h0], own.at[0], local_sems.at[0]),
                    pltpu.make_async_copy(q.at[0, h1], own.at[1], local_sems.at[1])]

        def start():
            for cp in via() + direct() + mine():
                cp.start()

        def relay():
            arrived, loaded, onward = via(), mine(), second()
            for k in range(2):
                arrived[k].wait_recv()
                loaded[k].wait()
                comb[k] = (own[k].astype(F32) + relayed[k].astype(F32)).astype(comb.dtype)
                onward[k].start()

        def finish():
            landing = direct() + second()
            for cp in landing:
                cp.wait_recv()
            for cp in via() + landing:
                cp.wait_send()

        return start, relay, finish


class _SumGather:
    def __init__(self, accs, lands):
        n = len(accs)
        self.n = n
        self.ins, self.in_specs = list(accs) + list(lands), [_VMEM] * (2 * n)
        self.out_shape = [_sds((NDEV,) + a.shape, a.dtype) for a in accs]
        self.out_specs = [_HBM] * n
        self.scratch = [pltpu.VMEM(a.shape, a.dtype) for a in accs] + _gather_sems(n)

    def ops(self, ins, outs, scr):
        n = self.n
        accs, lands, mine = ins[:n], ins[n:], scr[:n]
        g_start, relay, forward, finish = _gather_ops(mine, outs, *scr[n:])

        def start():
            for i in range(n):
                mine[i][...] = accs[i][...] + lands[i][0] + lands[i][1] + lands[i][2]
            g_start()

        return start, relay, forward, finish


def _call(main, jobs, *, name, grid, ins, in_specs, out_shape, out_specs, scratch, relay_step=0, first=0,
          prologue=None, forward_step=None):
    nsteps = grid[0] if grid else 1
    n_in, n_out, n_scr = len(ins), len(out_shape), len(scratch)

    def body(*refs):
        pos = [0]

        def take(k):
            r = refs[pos[0]:pos[0] + k]
            pos[0] += k
            return r

        m_in = take(n_in)
        j_in = [take(len(j.ins)) for j in jobs]
        m_out = take(n_out)
        j_out = [take(len(j.out_shape)) for j in jobs]
        m_scr = take(n_scr)
        j_scr = [take(len(j.scratch)) for j in jobs]
        ops = [_four(j.ops(a, b, s)) for j, a, b, s in zip(jobs, j_in, j_out, j_scr)]
        i = pl.program_id(0) if grid else 0
        if not grid:
            for stage in range(4):
                for o in ops:
                    o[stage]()
                if stage == 0:
                    main(i, m_in, m_out, m_scr)
            return

        if ops:
            @pl.when(i == 0)
            def _():
                for o in ops[:first]:
                    o[0]()
                for o in ops[:first]:
                    o[1]()
                for o in ops[first:]:
                    o[0]()
                for o in ops[:first]:
                    o[2]()
                for o in ops[:first]:
                    o[3]()
                if prologue is not None:
                    prologue(j_out[:first], m_scr)

        main(i, m_in, m_out, m_scr)

        forward_at = max(relay_step, nsteps - 2) if forward_step is None else min(forward_step, nsteps - 1)
        for stage, at in ((1, min(relay_step, nsteps - 1)), (2, forward_at), (3, nsteps - 1)):
            if ops[first:]:
                @pl.when(i == at)
                def _():
                    for o in ops[first:]:
                        o[stage]()

    extra = dict(dimension_semantics=("arbitrary",)) if grid else {}
    res = pl.pallas_call(
        body, name=name, grid=grid,
        in_specs=list(in_specs) + [s for j in jobs for s in j.in_specs],
        out_specs=list(out_specs) + [s for j in jobs for s in j.out_specs],
        out_shape=list(out_shape) + [s for j in jobs for s in j.out_shape],
        scratch_shapes=list(scratch) + [s for j in jobs for s in j.scratch],
        compiler_params=_params(**extra),
    )(*ins, *[a for j in jobs for a in j.ins])
    main_out, rest, job_out = res[:n_out], res[n_out:], []
    for j in jobs:
        k = len(j.out_shape)
        job_out.append(rest[:k])
        rest = rest[k:]
    return main_out, job_out


def _four(ops):
    return ops if len(ops) == 4 else (ops[0], ops[1], lambda: None, ops[2])


def _comm_only(jobs, name):
    _, job_out = _call(lambda i, a, b, s: None, jobs, name=name, grid=(), ins=[], in_specs=[], out_shape=[],
                       out_specs=[], scratch=[])
    return job_out


class _InChip:
    def __init__(self, ps):
        n = len(ps)
        self.n = n
        blk = [p.shape[1:] for p in ps]
        self.ins, self.in_specs = list(ps), [_HBM] * n
        self.out_shape = [_sds((NCHIP_OTHER,) + b, p.dtype) for b, p in zip(blk, ps)] + [_sds(b, F32) for b in blk]
        self.out_specs = [_VMEM] * (2 * n)
        self.scratch = ([pltpu.VMEM((4,) + b, p.dtype) for b, p in zip(blk, ps)] * 2
                        + [pltpu.SemaphoreType.DMA((4, n))] * 3)

    def ops(self, ins, outs, scr):
        n = self.n
        q_refs, acc_refs = outs[:n], outs[n:]
        mines, lands = scr[:n], scr[n:2 * n]
        send_sems, recv_sems, local_sems = scr[2 * n:]
        x, y, c = _place()
        sibling = (x, y, 1 - c)

        def copies():
            out = []
            for i in range(n):
                for pi in range(4):
                    loc = pltpu.make_async_copy(ins[i].at[2 * pi + c], mines[i].at[pi], local_sems.at[pi, i])
                    cp = pltpu.make_async_remote_copy(
                        src_ref=ins[i].at[2 * pi + (1 - c)], dst_ref=lands[i].at[pi],
                        send_sem=send_sems.at[pi, i], recv_sem=recv_sems.at[pi, i],
                        device_id=sibling, device_id_type=MESH)
                    out.append((loc, cp))
            return out

        def start():
            for loc, cp in copies():
                loc.start()
                cp.start()

        def finish():
            pairs = copies()
            for loc, cp in pairs:
                loc.wait()
                cp.wait_recv()
            for i in range(n):
                _chip_sums(mines[i], lands[i], q_refs[i], acc_refs[i], x, y)
            for _, cp in pairs:
                cp.wait_send()

        return start, lambda: None, finish


def _chip_sums(mine, land, q_ref, acc_ref, x, y):
    for j, (qx, qy) in enumerate(_other_chips(x, y)):
        qi = 2 * qx + qy
        q_ref[j] = (mine[qi].astype(F32) + land[qi].astype(F32)).astype(q_ref.dtype)
    mi = 2 * x + y
    acc_ref[...] = mine[mi].astype(F32) + land[mi].astype(F32)


def _direct_sum(v, buf, send_sems, recv_sems):
    x, y, c = _place()
    me = 4 * x + 2 * y + c
    buf[me] = v
    cps = []
    for k in range(1, NDEV):
        fx, fy, fc = (k >> 2) & 1, (k >> 1) & 1, k & 1
        peer = ((1 - x) if fx else x, (1 - y) if fy else y, (1 - c) if fc else c)
        cps.append((peer, pltpu.make_async_remote_copy(
            src_ref=buf.at[me], dst_ref=buf.at[me], send_sem=send_sems.at[k - 1], recv_sem=recv_sems.at[k - 1],
            device_id=peer, device_id_type=MESH)))
    for _, cp in cps:
        cp.start()
    for k, (peer, _) in enumerate(cps):
        theirs = 4 * peer[0] + 2 * peer[1] + peer[2]
        pltpu.make_async_remote_copy(
            src_ref=buf.at[theirs], dst_ref=buf.at[theirs], send_sem=send_sems.at[k], recv_sem=recv_sems.at[k],
            device_id=peer, device_id_type=MESH).wait_recv()
    acc = buf[0]
    for j in range(1, NDEV):
        acc = acc + buf[j]
    for _, cp in cps:
        cp.wait_send()
    return acc


def _direct_sum_scratch(shape, dtype):
    return [pltpu.VMEM((NDEV,) + tuple(shape), dtype), pltpu.SemaphoreType.DMA((NDEV - 1,)),
            pltpu.SemaphoreType.DMA((NDEV - 1,))]


def _fwd_a(x, nw, lnw, lnb, ws, bst, jobs, *, tm, relay_step):
    s_len = x.shape[0]
    nt = s_len // tm
    nch = tm // CH

    def main(i, ins, outs, scr):
        x_ref, nw_ref, lnw_ref, lnb_ref, ws_ref, bst_ref = ins
        z_ref, h_ref, y_ref, pp_ref = outs
        wc_scr, gv_scr, win_ref = scr

        @pl.when(i == 0)
        def _():
            m = _causal_mask()
            for g in range(G):
                wc_scr[g] = jnp.where(m, ws_ref[g], 0.0).astype(BF16)

        x = x_ref[...]
        h = (x * _rms(x) * nw_ref[...]).astype(BF16)
        h_ref[...] = h
        for k in range(NDEV):
            z_ref[:, k * CA:(k + 1) * CA] = _dot(h, win_ref[k])

        ssum = jnp.zeros((tm, 1), F32)
        for g in range(G):
            vs = slice(AW + g * GD, AW + (g + 1) * GD)
            gv, pv = _gelu_t(z_ref[:, vs])
            pp_ref[:, vs] = pv.astype(BF16)
            gv_scr[:, g * GD:(g + 1) * GD] = gv
            ssum = ssum + jnp.sum(gv, axis=-1, keepdims=True)
        mu = ssum * (1.0 / AW)
        vsum = jnp.zeros((tm, 1), F32)
        for g in range(G):
            dlt = gv_scr[:, g * GD:(g + 1) * GD] - mu
            vsum = vsum + jnp.sum(dlt * dlt, axis=-1, keepdims=True)
        rstd = lax.rsqrt(vsum * (1.0 / AW) + LN_EPS)

        for g in range(G):
            cs = slice(g * GD, (g + 1) * GD)
            gs = slice(2 * AW + g * GD, 2 * AW + (g + 1) * GD)
            v = (gv_scr[:, cs] - mu) * rstd * lnw_ref[:, cs] + lnb_ref[:, cs]
            vb = v.astype(BF16)
            u, pu = _gelu_t(z_ref[:, cs])
            pp_ref[:, cs] = pu.astype(BF16)
            zg = z_ref[:, gs]
            sig = _sigmoid(zg)
            pp_ref[:, gs] = sig.astype(BF16)
            sg = zg * sig
            for n in range(nch):
                rs = slice(n * CH, (n + 1) * CH)
                s = _dot(wc_scr[g], vb[rs, :]) + bst_ref[:, g:g + 1]
                y_ref[rs, cs] = (u[rs, :] * s * sg[rs, :]).astype(BF16)

    tile = lambda w: pl.BlockSpec((tm, w), lambda i: (i, 0))
    return _call(
        main, jobs, name="fwd_a", grid=(nt,), relay_step=relay_step, first=1, forward_step=FORWARD_STEP_FWD_A,
        prologue=lambda gathered, scr: pltpu.sync_copy(gathered[0][0], scr[2]),
        ins=[x, nw, lnw, lnb, ws, bst], in_specs=[tile(D), _VMEM, _VMEM, _VMEM, _VMEM, _VMEM],
        out_shape=[_sds((s_len, 3 * AW), F32), _sds((s_len, D), BF16), _sds((s_len, AW), BF16),
                   _sds((s_len, 3 * AW), BF16)],
        out_specs=[tile(3 * AW), tile(D), tile(AW), tile(3 * AW)],
        scratch=[pltpu.VMEM((G, CH, CH), BF16), pltpu.VMEM((tm, AW), F32), pltpu.VMEM((NDEV, D, CA), BF16)])


def _bwd_a(dx1, z, pp, lnw, lnb, ws, bst, wout, jobs, *, tm, relay_step):
    s_len = dx1.shape[0]
    nt = s_len // tm
    nch = tm // CH

    def main(i, ins, outs, scr):
        dx1_ref, z_ref, pp_ref, lnw_ref, lnb_ref, ws_ref, bst_ref, wout_ref = ins
        dz_ref, glnw_ref, glnb_ref, gws_ref, gbst_ref = outs
        wc_scr, wct_scr, vh_scr, dgv_scr, dy_scr, dv_scr, gbs_acc, gwc_acc = scr

        @pl.when(i == 0)
        def _():
            m = _causal_mask()
            for g in range(G):
                wm = jnp.where(m, ws_ref[g], 0.0)
                wc_scr[g] = wm.astype(BF16)
                wct_scr[g] = wm.T.astype(BF16)
            glnw_ref[...] = jnp.zeros_like(glnw_ref)
            glnb_ref[...] = jnp.zeros_like(glnb_ref)
            gbs_acc[...] = jnp.zeros_like(gbs_acc)
            gwc_acc[...] = jnp.zeros_like(gwc_acc)

        dy_scr[...] = _dot_nt(dx1_ref[...], wout_ref[...])

        ssum = jnp.zeros((tm, 1), F32)
        for g in range(G):
            cs = slice(g * GD, (g + 1) * GD)
            vs = slice(AW + g * GD, AW + (g + 1) * GD)
            zv = z_ref[:, vs]
            pv = pp_ref[:, vs].astype(F32)
            gv = zv * pv
            vh_scr[:, cs] = gv
            dgv_scr[:, cs] = _dgelu(zv, pv)
            ssum = ssum + jnp.sum(gv, axis=-1, keepdims=True)
        mu = ssum * (1.0 / AW)
        vsum = jnp.zeros((tm, 1), F32)
        for g in range(G):
            dlt = vh_scr[:, g * GD:(g + 1) * GD] - mu
            vsum = vsum + jnp.sum(dlt * dlt, axis=-1, keepdims=True)
        rstd = lax.rsqrt(vsum * (1.0 / AW) + LN_EPS)

        m1 = jnp.zeros((tm, 1), F32)
        m2 = jnp.zeros((tm, 1), F32)
        for g in range(G):
            cs = slice(g * GD, (g + 1) * GD)
            gs = slice(2 * AW + g * GD, 2 * AW + (g + 1) * GD)
            vhat = (vh_scr[:, cs] - mu) * rstd
            vh_scr[:, cs] = vhat
            vb = (vhat * lnw_ref[:, cs] + lnb_ref[:, cs]).astype(BF16)
            zu = z_ref[:, cs]
            tu = pp_ref[:, cs].astype(F32)
            u = zu * tu
            zg = z_ref[:, gs]
            sig = pp_ref[:, gs].astype(F32)
            sg = zg * sig
            dy = dy_scr[:, cs]
            dsf = dy * u * sg
            dsb = dsf.astype(BF16)
            dvs = []
            for n in range(nch):
                rs = slice(n * CH, (n + 1) * CH)
                s = _dot(wc_scr[g], vb[rs, :]) + bst_ref[:, g:g + 1]
                dys = dy[rs, :] * s
                dz_ref[rs, cs] = (dys * sg[rs, :] * _dgelu(zu[rs, :], tu[rs, :])).astype(BF16)
                dz_ref[rs, gs] = (dys * u[rs, :] * (sig[rs, :] * (1.0 + zg[rs, :] * (1.0 - sig[rs, :])))).astype(BF16)
                gbs_acc[g] += dsf[rs, :]
                gwc_acc[g] += _dot_nt(dsb[rs, :], vb[rs, :])
                dvs.append(_dot(wct_scr[g], dsb[rs, :]))
            dv = jnp.concatenate(dvs, axis=0) if nch > 1 else dvs[0]
            glnw_ref[:, cs] += _rowsum(dv * vhat)
            glnb_ref[:, cs] += _rowsum(dv)
            dvh = dv * lnw_ref[:, cs]
            dv_scr[:, cs] = dvh
            m1 = m1 + jnp.sum(dvh, axis=-1, keepdims=True)
            m2 = m2 + jnp.sum(dvh * vhat, axis=-1, keepdims=True)
        m1 = m1 * (1.0 / AW)
        m2 = m2 * (1.0 / AW)
        for g in range(G):
            cs = slice(g * GD, (g + 1) * GD)
            dgv = rstd * (dv_scr[:, cs] - m1 - vh_scr[:, cs] * m2)
            dz_ref[:, AW + g * GD:AW + (g + 1) * GD] = (dgv * dgv_scr[:, cs]).astype(BF16)

        @pl.when(i == nt - 1)
        def _():
            m = _causal_mask()
            for g in range(G):
                gws_ref[g] = jnp.where(m, gwc_acc[g], 0.0)
                gbst_ref[:, g:g + 1] = jnp.sum(gbs_acc[g], axis=-1, keepdims=True)

    tile = lambda w: pl.BlockSpec((tm, w), lambda i: (i, 0))
    whole = lambda *s: pl.BlockSpec(s, lambda i: (0,) * len(s))
    big = lambda dt: pltpu.VMEM((tm, AW), dt)
    return _call(
        main, jobs, name="bwd_a", grid=(nt,), relay_step=relay_step,
        ins=[dx1, z, pp, lnw, lnb, ws, bst, wout],
        in_specs=[tile(D), tile(3 * AW), tile(3 * AW), _VMEM, _VMEM, _VMEM, _VMEM, _VMEM],
        out_shape=[_sds((s_len, 3 * AW), BF16), _sds((1, AW), F32), _sds((1, AW), F32), _sds((G, CH, CH), F32),
                   _sds((CH, G), F32)],
        out_specs=[tile(3 * AW), whole(1, AW), whole(1, AW), whole(G, CH, CH), whole(CH, G)],
        scratch=[pltpu.VMEM((G, CH, CH), BF16), pltpu.VMEM((G, CH, CH), BF16), big(F32), big(F32), big(F32), big(F32),
                 pltpu.VMEM((G, CH, GD), F32), pltpu.VMEM((G, CH, CH), F32)])


def _bwd_a_in(dz, dx1, x, nw, win8, jobs, *, tm, relay_step):
    s_len = x.shape[0]
    nt = s_len // tm

    def main(i, ins, outs, scr):
        dz_ref, dx1_ref, x_ref, nw_ref, win_ref = ins
        gx_ref, gnw_ref = outs

        @pl.when(i == 0)
        def _():
            gnw_ref[...] = jnp.zeros_like(gnw_ref)

        dh = jnp.zeros((tm, D), F32)
        for k in range(NDEV):
            dh = dh + _dot_nt(dz_ref[:, k * CA:(k + 1) * CA], win_ref[k])
        x = x_ref[...]
        r = _rms(x)
        gx_ref[...] = dx1_ref[...] + _rms_bwd(dh, x, r, nw_ref[...])
        gnw_ref[...] += _rowsum(dh * x * r)

        @pl.when(i == nt - 1)
        def _():
            gnw_ref[...] = _direct_sum(gnw_ref[...], *scr)

    tile = lambda w: pl.BlockSpec((tm, w), lambda i: (i, 0))
    return _call(
        main, jobs, name="bwd_a_in", grid=(nt,), relay_step=relay_step, forward_step=nt - 1,
        ins=[dz, dx1, x, nw, win8], in_specs=[tile(3 * AW), tile(D), tile(D), _VMEM, _VMEM],
        out_shape=[_sds((s_len, D), F32), _sds((1, D), F32)],
        out_specs=[tile(D), pl.BlockSpec((1, D), lambda i: (0, 0))], scratch=_direct_sum_scratch((1, D), F32))


def _conv(p8_ref, cs, xb, xm1, xm2, xm3):
    xc = p8_ref[4:5, cs] + p8_ref[3:4, cs] * xb
    xc = xc + p8_ref[0:1, cs] * xm3
    xc = xc + p8_ref[1:2, cs] * xm2
    return xc + p8_ref[2:3, cs] * xm1


def _gates(p8_ref, gcat_ref, hh, xc):
    cs = slice(hh * HD, (hh + 1) * HD)
    pre = _dot(xc.astype(BF16), gcat_ref[hh])
    r = _sigmoid(pre[:, :HD] + p8_ref[5:6, cs])
    ig = _sigmoid(pre[:, HD:] + p8_ref[6:7, cs])
    sp = _softplus_neg(p8_ref[7:8, cs])
    la = (-RG_C) * r * sp
    a = jnp.exp(la)
    half_log = 0.5 * jnp.log(jnp.tanh(-la) * (1.0 + a * a))
    return r, ig, sp, a, jnp.exp(half_log), jnp.exp(-half_log)


def _scan_rows(a_ref, b_ref, out_ref, carry, tm, reverse):
    row = lax.broadcasted_iota(jnp.int32, (SUBLANES, BW), 0)
    ngrp = tm // SUBLANES

    def step(j, cr):
        jj = (ngrp - 1 - j) if reverse else j
        off = pl.multiple_of(jj * SUBLANES, SUBLANES)
        a = a_ref[pl.ds(off, SUBLANES), :]
        b = b_ref[pl.ds(off, SUBLANES), :]
        for sh in (1, 2, 4):
            if reverse:
                a_s = pltpu.roll(a, SUBLANES - sh, 0)
                b_s = pltpu.roll(b, SUBLANES - sh, 0)
                m = row < SUBLANES - sh
            else:
                a_s = pltpu.roll(a, sh, 0)
                b_s = pltpu.roll(b, sh, 0)
                m = row >= sh
            b = jnp.where(m, a * b_s + b, b)
            a = jnp.where(m, a * a_s, a)
        o = b + a * cr
        out_ref[pl.ds(off, SUBLANES), :] = o
        return o[0:1, :] if reverse else o[SUBLANES - 1:SUBLANES, :]

    return lax.fori_loop(0, ngrp, step, carry)


def _fwd_b(x, ya, wout_a, nw, win8, p8, gcat, jobs, *, tm, relay_step):
    s_len = x.shape[0]
    nt = s_len // tm

    def main(i, ins, outs, scr):
        x_ref, ya_ref, wouta_ref, nw_ref, win_ref, p8_ref, gcat_ref = ins
        x1_ref, zb_ref, hs_ref, h1_ref, yb_ref, xc_ref, a_ref, cc_ref, r_ref, ig_ref, m_ref = outs
        xbe_scr, b_scr, k_scr, carry_scr = scr

        @pl.when(i == 0)
        def _():
            xbe_scr[0:SUBLANES, :] = jnp.zeros((SUBLANES, BW), F32)
            carry_scr[...] = jnp.zeros_like(carry_scr)

        x1 = x_ref[...] + _dot(ya_ref[...], wouta_ref[...])
        x1_ref[...] = x1
        h = (x1 * _rms(x1) * nw_ref[...]).astype(BF16)
        h1_ref[...] = h
        for k in range(NDEV):
            zb_ref[:, k * CB:(k + 1) * CB] = _dot(h, win_ref[k])
        xbe_scr[SUBLANES:SUBLANES + tm, :] = zb_ref[:, :BW]
        for hh in range(BH):
            cs = slice(hh * HD, (hh + 1) * HD)
            xc = _conv(p8_ref, cs, xbe_scr[SUBLANES:SUBLANES + tm, cs], xbe_scr[7:7 + tm, cs],
                       xbe_scr[6:6 + tm, cs], xbe_scr[5:5 + tm, cs])
            r, ig, _, a, mult, rm = _gates(p8_ref, gcat_ref, hh, xc)
            ixc = ig * xc
            xc_ref[:, cs] = xc
            a_ref[:, cs] = a
            r_ref[:, cs] = r.astype(BF16)
            ig_ref[:, cs] = ig.astype(BF16)
            m_ref[:, cs] = mult.astype(BF16)
            b_scr[:, cs] = mult * ixc
            k_scr[:, cs] = ixc * (a * a * rm)
        xbe_scr[0:SUBLANES, :] = xbe_scr[tm:tm + SUBLANES, :]
        carry_scr[...] = _scan_rows(a_ref, b_scr, hs_ref, carry_scr[...], tm, False)
        for hh in range(BH):
            cs = slice(hh * HD, (hh + 1) * HD)
            gt = zb_ref[:, BW + hh * HD:BW + (hh + 1) * HD]
            hsv = hs_ref[:, cs]
            yb_ref[:, cs] = (hsv * (gt * _sigmoid(gt))).astype(BF16)
            cc_ref[:, cs] = (hsv - b_scr[:, cs]) - k_scr[:, cs]

    tile = lambda w: pl.BlockSpec((tm, w), lambda i: (i, 0))
    wide = lambda dt: _sds((s_len, BW), dt)
    return _call(
        main, jobs, name="fwd_b", grid=(nt,), relay_step=relay_step,
        ins=[x, ya, wout_a, nw, win8, p8, gcat], in_specs=[tile(D), tile(AW), _VMEM, _VMEM, _VMEM, _VMEM, _VMEM],
        out_shape=[_sds((s_len, D), F32), _sds((s_len, 2 * BW), F32), wide(F32), _sds((s_len, D), BF16), wide(BF16),
                   wide(F32), wide(F32), wide(F32), wide(BF16), wide(BF16), wide(BF16)],
        out_specs=[tile(D), tile(2 * BW), tile(BW), tile(D)] + [tile(BW)] * 7,
        scratch=[pltpu.VMEM((tm + SUBLANES, BW), F32), pltpu.VMEM((tm, BW), F32), pltpu.VMEM((tm, BW), F32),
                 pltpu.VMEM((1, BW), F32)])


def _head(x1, yb, wout, nfw, tgt, *, tm):
    s_len = x1.shape[0]

    def main(i, ins, outs, scr):
        x1_ref, yb_ref, wout_ref, nfw_ref, t_ref = ins
        dx2_ref, dx2b_ref, loss_ref, gnfw_ref = outs

        @pl.when(i == 0)
        def _():
            loss_ref[...] = jnp.zeros_like(loss_ref)
            gnfw_ref[...] = jnp.zeros_like(gnfw_ref)

        x2 = x1_ref[...] + _dot(yb_ref[...], wout_ref[...])
        rf = _rms(x2)
        xn = x2 * rf
        e = xn * nfw_ref[...] - t_ref[...]
        loss_ref[...] += (0.5 / D) * jnp.sum(jnp.sum(e * e, axis=-1, keepdims=True), axis=0, keepdims=True)
        dyf = e * (1.0 / D)
        gnfw_ref[...] += _rowsum(dyf * xn)
        dx2 = _rms_bwd(dyf, x2, rf, nfw_ref[...])
        dx2_ref[...] = dx2
        dx2b_ref[...] = dx2.astype(BF16)

    tile = lambda w: pl.BlockSpec((tm, w), lambda i: (i, 0))
    whole = lambda *s: pl.BlockSpec(s, lambda i: (0,) * len(s))
    (dx2, dx2b, loss, gnfw), _ = _call(
        main, [], name="head", grid=(s_len // tm,),
        ins=[x1, yb, wout, nfw, tgt], in_specs=[tile(D), tile(BW), _VMEM, _VMEM, tile(D)],
        out_shape=[_sds((s_len, D), F32), _sds((s_len, D), BF16), _sds((1, 1), F32), _sds((1, D), F32)],
        out_specs=[tile(D), tile(D), whole(1, 1), whole(1, D)], scratch=[])
    return dx2, dx2b, loss, gnfw


def _bwd_b(dx2, zb, hs, x1, saved, nw, win8, p8, gcat, wout, *, tm):
    s_len = x1.shape[0]
    nt = s_len // tm

    def main(i, ins, outs, scr):
        (dx2_ref, zb_ref, hs_ref, x1_ref, xc_ref, a_ref, cc_ref, r_ref, ig_ref, m_ref,
         nw_ref, win_ref, p8_ref, gcat_ref, wout_ref) = ins
        dx1_ref, dx1b_ref, dzb_ref, gp8_ref, gga_ref, ggx_ref, gnw_ref = outs
        ae_scr, an_scr, dhd_scr, dh_scr, dy_scr, dxce_scr, carry_scr, afirst_scr = scr

        @pl.when(i == 0)
        def _():
            gp8_ref[...] = jnp.zeros_like(gp8_ref)
            gga_ref[...] = jnp.zeros_like(gga_ref)
            ggx_ref[...] = jnp.zeros_like(ggx_ref)
            gnw_ref[...] = jnp.zeros_like(gnw_ref)
            dxce_scr[tm:tm + SUBLANES, :] = jnp.zeros((SUBLANES, BW), F32)
            carry_scr[...] = jnp.zeros_like(carry_scr)
            afirst_scr[...] = jnp.zeros_like(afirst_scr)

        dx2 = dx2_ref[...]
        dy_scr[...] = _dot_nt(dx2.astype(BF16), wout_ref[...])
        for hh in range(BH):
            cs = slice(hh * HD, (hh + 1) * HD)
            gs = slice(BW + hh * HD, BW + (hh + 1) * HD)
            gt = zb_ref[:, gs]
            sig = _sigmoid(gt)
            dy = dy_scr[:, cs]
            dhd_scr[:, cs] = dy * (gt * sig)
            dzb_ref[:, gs] = (dy * hs_ref[:, cs] * (sig * (1.0 + gt * (1.0 - sig)))).astype(BF16)

        ae_scr[0:tm, :] = a_ref[...]
        ae_scr[tm:tm + SUBLANES, :] = jnp.broadcast_to(afirst_scr[...], (SUBLANES, BW))
        an_scr[...] = ae_scr[1:1 + tm, :]
        afirst_scr[...] = ae_scr[0:1, :]
        carry_scr[...] = _scan_rows(an_scr, dhd_scr, dh_scr, carry_scr[...], tm, True)

        for hh in range(BH):
            cs = slice(hh * HD, (hh + 1) * HD)
            dh = dh_scr[:, cs]
            mult = m_ref[:, cs].astype(F32)
            ig = ig_ref[:, cs].astype(F32)
            r = r_ref[:, cs].astype(F32)
            xc = xc_ref[:, cs]
            lam = p8_ref[7:8, cs]
            sp = _softplus_neg(lam)
            dla = dh * cc_ref[:, cs]
            gp8_ref[7:8, cs] += _rowsum(dla * ((-RG_C) * r)) * (-_sigmoid(-lam))
            dpr = dla * ((-RG_C) * sp) * (r * (1.0 - r))
            dpi = dh * mult * xc * (ig * (1.0 - ig))
            gp8_ref[5:6, cs] += _rowsum(dpr)
            gp8_ref[6:7, cs] += _rowsum(dpi)
            dcat = jnp.concatenate([dpr, dpi], axis=1).astype(BF16)
            dxc = dh * mult * ig + _dot_nt(dcat, gcat_ref[hh])
            gg = _dot(xc.T.astype(BF16), dcat)
            gga_ref[hh] += gg[:, :HD]
            ggx_ref[hh] += gg[:, HD:]
            dxce_scr[0:tm, cs] = dxc
            gp8_ref[4:5, cs] += _rowsum(dxc)
        for hh in range(BH):
            cs = slice(hh * HD, (hh + 1) * HD)
            xb = zb_ref[:, cs]
            d0, d1 = dxce_scr[0:tm, cs], dxce_scr[1:1 + tm, cs]
            d2, d3 = dxce_scr[2:2 + tm, cs], dxce_scr[3:3 + tm, cs]
            dzb_ref[:, cs] = (p8_ref[3:4, cs] * d0 + p8_ref[2:3, cs] * d1 + p8_ref[1:2, cs] * d2
                              + p8_ref[0:1, cs] * d3).astype(BF16)
            gp8_ref[3:4, cs] += _rowsum(d0 * xb)
            gp8_ref[2:3, cs] += _rowsum(d1 * xb)
            gp8_ref[1:2, cs] += _rowsum(d2 * xb)
            gp8_ref[0:1, cs] += _rowsum(d3 * xb)
        dxce_scr[tm:tm + SUBLANES, :] = dxce_scr[0:SUBLANES, :]

        dh1 = jnp.zeros((tm, D), F32)
        for k in range(NDEV):
            dh1 = dh1 + _dot_nt(dzb_ref[:, k * CB:(k + 1) * CB], win_ref[k])
        x1 = x1_ref[...]
        r1 = _rms(x1)
        dx1 = dx2 + _rms_bwd(dh1, x1, r1, nw_ref[...])
        dx1_ref[...] = dx1
        dx1b_ref[...] = dx1.astype(BF16)
        gnw_ref[...] += _rowsum(dh1 * x1 * r1)

    tile = lambda w: pl.BlockSpec((tm, w), lambda i: (nt - 1 - i, 0))
    whole = lambda *s: pl.BlockSpec(s, lambda i: (0,) * len(s))
    full = lambda: pltpu.VMEM((tm, BW), F32)
    ext = lambda: pltpu.VMEM((tm + SUBLANES, BW), F32)
    out, _ = _call(
        main, [], name="bwd_b", grid=(nt,),
        ins=[dx2, zb, hs, x1, *saved, nw, win8, p8, gcat, wout],
        in_specs=[tile(D), tile(2 * BW), tile(BW), tile(D)] + [tile(BW)] * 6 + [_VMEM] * 5,
        out_shape=[_sds((s_len, D), F32), _sds((s_len, D), BF16), _sds((s_len, 2 * BW), BF16), _sds((SUBLANES, BW), F32),
                   _sds((BH, HD, HD), F32), _sds((BH, HD, HD), F32), _sds((1, D), F32)],
        out_specs=[tile(D), tile(D), tile(2 * BW), whole(SUBLANES, BW), whole(BH, HD, HD), whole(BH, HD, HD),
                   whole(1, D)],
        scratch=[ext(), full(), full(), full(), full(), ext(), pltpu.VMEM((1, BW), F32), pltpu.VMEM((1, BW), F32)])
    return out


def _transpose_into(dst_ref, src_ref, rows):
    s_len = src_ref.shape[0]
    for r0 in range(0, s_len, rows):
        dst_ref[:, r0:r0 + rows] = src_ref[r0:r0 + rows, :].astype(F32).T.astype(BF16)


def _wgrad(a, b, jobs, *, by_rows, per, name, relay_step=0):
    s_len, m = a.shape
    n = b.shape[1]
    r, cd = (m // NDEV, n) if by_rows else (m, n // NDEV)
    nsteps = NDEV // per
    at_rows = per * r if by_rows else m

    def main(i, ins, outs, scr):
        a_ref, b_ref = ins
        q_ref, acc_ref = outs
        at_scr, stage, mine, land, send_sems, recv_sems = scr
        x, y, c = _place()

        def to_sibling(pi):
            return pltpu.make_async_remote_copy(
                src_ref=stage.at[pi & 1], dst_ref=land.at[pi], send_sem=send_sems.at[pi], recv_sem=recv_sems.at[pi],
                device_id=(x, y, 1 - c), device_id_type=MESH)

        if by_rows:
            _transpose_into(at_scr, a_ref, TRANSPOSE_ROWS)
        else:
            @pl.when(i == 0)
            def _():
                _transpose_into(at_scr, a_ref, TRANSPOSE_ROWS)

        res = _dot(at_scr[...], b_ref[...]).astype(BF16)
        for k in range(per):
            blk = per * i + k
            pi, pc = blk >> 1, blk & 1
            val = res[k * r:(k + 1) * r, :] if by_rows else res

            @pl.when(pc != c)
            def _():
                @pl.when(pi >= 2)
                def _():
                    to_sibling(pi - 2).wait_send()

                stage[pi & 1] = val
                to_sibling(pi).start()

            @pl.when(pc == c)
            def _():
                mine[pi] = val

        @pl.when(i == nsteps - 1)
        def _():
            for p in range(4):
                to_sibling(p).wait_recv()
            to_sibling(2).wait_send()
            to_sibling(3).wait_send()
            _chip_sums(mine, land, q_ref, acc_ref, x, y)

    if by_rows:
        in_specs = [pl.BlockSpec((s_len, at_rows), lambda j: (0, j)), _VMEM]
    else:
        in_specs = [_VMEM, pl.BlockSpec((s_len, cd), lambda j: (0, j))]
    blk_vmem = lambda k: pltpu.VMEM((k, r, cd), BF16)
    (q, acc), job_out = _call(
        main, jobs, name=name, grid=(nsteps,), relay_step=relay_step, ins=[a, b], in_specs=in_specs,
        out_shape=[_sds((NCHIP_OTHER, r, cd), BF16), _sds((r, cd), F32)],
        out_specs=[pl.BlockSpec((NCHIP_OTHER, r, cd), lambda j: (0, 0, 0)), pl.BlockSpec((r, cd), lambda j: (0, 0))],
        scratch=[pltpu.VMEM((at_rows, s_len), BF16), blk_vmem(2), blk_vmem(4), blk_vmem(4),
                 pltpu.SemaphoreType.DMA((4,)), pltpu.SemaphoreType.DMA((4,))])
    return q, acc, job_out


def _wgrad_cols_early(a, b, jobs, *, name, relay_step=0):
    s_len, m = a.shape
    r, cd = m, b.shape[1] // NDEV
    h = r // 2

    def chip_at(pos, base):
        return base ^ (3 - pos)

    def main(i, ins, outs, scr):
        a_ref, b_ref = ins
        q_ref, acc_ref, rel_ref = outs
        at_scr, stage, mine, land, q2_scr, send_sems, recv_sems, via_send, via_recv = scr
        x, y, c = _place()
        base = 2 * x + y
        xn, yn, _ = _other_chips(x, y)
        pos, pc = i >> 1, i & 1
        pi = chip_at(pos, base)

        def to_sibling(chip, slot):
            return pltpu.make_async_remote_copy(
                src_ref=stage.at[slot], dst_ref=land.at[chip], send_sem=send_sems.at[chip],
                recv_sem=recv_sems.at[chip], device_id=(x, y, 1 - c), device_id_type=MESH)

        def via(k):
            return pltpu.make_async_remote_copy(
                src_ref=q2_scr.at[pl.ds(k * h, h)], dst_ref=rel_ref.at[k], send_sem=via_send.at[k],
                recv_sem=via_recv.at[k], device_id=(*(xn, yn)[k], c), device_id_type=MESH)

        @pl.when(i == 0)
        def _():
            _transpose_into(at_scr, a_ref, TRANSPOSE_ROWS)

        res = _dot(at_scr[...], b_ref[...]).astype(BF16)

        @pl.when(pc != c)
        def _():
            @pl.when(pos >= 2)
            def _():
                to_sibling(chip_at(pos - 2, base), pos & 1).wait_send()

            stage[pos & 1] = res
            to_sibling(pi, pos & 1).start()

        @pl.when(pc == c)
        def _():
            mine[pi] = res

        @pl.when(i == 1)
        def _():
            dg = chip_at(0, base)
            to_sibling(dg, 0).wait_recv()
            q2 = (mine[dg].astype(F32) + land[dg].astype(F32)).astype(BF16)
            q2_scr[...] = q2
            q_ref[2] = q2
            via(0).start()
            via(1).start()

        @pl.when(i == NDEV - 1)
        def _():
            for pos_ in (1, 2, 3):
                to_sibling(chip_at(pos_, base), 0).wait_recv()
            to_sibling(chip_at(2, base), 0).wait_send()
            to_sibling(chip_at(3, base), 1).wait_send()
            for k in range(2):
                via(k).wait_recv()
            for k in range(2):
                via(k).wait_send()
            for j, chip in enumerate((base ^ 2, base ^ 1)):
                q_ref[j] = (mine[chip].astype(F32) + land[chip].astype(F32)).astype(BF16)
            acc_ref[...] = mine[base].astype(F32) + land[base].astype(F32)

    def b_block(j):
        base = 2 * lax.axis_index("x") + lax.axis_index("y")
        return (0, 2 * chip_at(j >> 1, base) + (j & 1))

    blk_vmem = lambda k: pltpu.VMEM((k, r, cd), BF16)
    (q, acc, rel), job_out = _call(
        main, jobs, name=name, grid=(NDEV,), relay_step=relay_step, ins=[a, b],
        in_specs=[_VMEM, pl.BlockSpec((s_len, cd), b_block)],
        out_shape=[_sds((NCHIP_OTHER, r, cd), BF16), _sds((r, cd), F32), _sds((2, h, cd), BF16)],
        out_specs=[pl.BlockSpec((NCHIP_OTHER, r, cd), lambda j: (0, 0, 0)), pl.BlockSpec((r, cd), lambda j: (0, 0)), _HBM],
        scratch=[pltpu.VMEM((m, s_len), BF16), blk_vmem(2), blk_vmem(4), blk_vmem(4), pltpu.VMEM((r, cd), BF16),
                 pltpu.SemaphoreType.DMA((4,)), pltpu.SemaphoreType.DMA((4,)), pltpu.SemaphoreType.DMA((2,)),
                 pltpu.SemaphoreType.DMA((2,))])
    return q, acc, rel, job_out


class _ExchangeRest:
    def __init__(self, q, relayed):
        _, r, cd = q.shape
        half = (2, r // 2, cd)
        self.ins, self.in_specs = [q, relayed], [_HBM, _HBM]
        self.out_shape, self.out_specs = [_sds((2, r, cd), q.dtype)], [_HBM]
        self.scratch = [pltpu.VMEM(half, q.dtype), pltpu.VMEM(half, q.dtype), pltpu.VMEM(half, q.dtype),
                        pltpu.SemaphoreType.DMA((4,)), pltpu.SemaphoreType.DMA((4,)), pltpu.SemaphoreType.DMA((4,))]

    def ops(self, ins, outs, scr):
        (q, rel_in), (land,) = ins, outs
        own, rel, comb, send_sems, recv_sems, local_sems = scr
        h = q.shape[1] // 2
        x, y, c = _place()
        xn, yn, _ = _other_chips(x, y)
        h0, h1 = pl.ds(0, h), pl.ds(h, h)

        def remote(k, src, dst, chip):
            return pltpu.make_async_remote_copy(src_ref=src, dst_ref=dst, send_sem=send_sems.at[k],
                                                recv_sem=recv_sems.at[k], device_id=(*chip, c), device_id_type=MESH)

        def sends():
            return [remote(0, q.at[0, h0], land.at[0, h0], xn), remote(1, q.at[1, h1], land.at[1, h1], yn),
                    remote(2, comb.at[0], land.at[1, h0], yn), remote(3, comb.at[1], land.at[0, h1], xn)]

        def loads():
            return [pltpu.make_async_copy(q.at[1, h0], own.at[0], local_sems.at[0]),
                    pltpu.make_async_copy(q.at[0, h1], own.at[1], local_sems.at[1]),
                    pltpu.make_async_copy(rel_in.at[0], rel.at[0], local_sems.at[2]),
                    pltpu.make_async_copy(rel_in.at[1], rel.at[1], local_sems.at[3])]

        def start():
            cps, lds = sends(), loads()
            for ld in lds:
                ld.start()
            cps[0].start()
            cps[1].start()
            for ld in lds:
                ld.wait()
            for k in range(2):
                comb[k] = (own[k].astype(F32) + rel[k].astype(F32)).astype(comb.dtype)
            cps[2].start()
            cps[3].start()

        def finish():
            cps = sends()
            for cp in cps:
                cp.wait_recv()
            for cp in cps:
                cp.wait_send()

        return start, lambda: None, finish


def _adam_math(w, g, m, v):
    m = B1 * m + (1.0 - B1) * g
    v = B2 * v + (1.0 - B2) * (g * g)
    m_hat = m / (1.0 - B1 ** STEP)
    v_hat = v / (1.0 - B2 ** STEP)
    delta = (-LR) * (m_hat / (jnp.sqrt(v_hat) + ADAM_EPS) + WD * w)
    return delta, m, v


def _adam_big(w, acc, land, m, v, name):
    r, cd = w.shape
    rb = ADAM_ROWS if r % ADAM_ROWS == 0 else r
    nland = land.shape[0]

    def body(w_ref, acc_ref, land_ref, m_ref, v_ref, g_ref, d_ref, mo_ref, vo_ref):
        g = acc_ref[...]
        for j in range(nland):
            g = g + land_ref[j].astype(F32)
        g_ref[...] = g
        d_ref[...], mo_ref[...], vo_ref[...] = _adam_math(w_ref[...], g, m_ref[...], v_ref[...])

    blk = pl.BlockSpec((rb, cd), lambda i: (i, 0))
    blk3 = pl.BlockSpec((nland, rb, cd), lambda i: (0, i, 0))
    return pl.pallas_call(
        body, name=name, grid=(r // rb,), in_specs=[blk, blk, blk3, blk, blk], out_specs=[blk] * 4,
        out_shape=[_sds((r, cd), F32)] * 4,
        compiler_params=_params(dimension_semantics=("arbitrary",)),
    )(w, acc, land, m, v)


def _adam_small(groups):
    n = len(groups)

    def body(*refs):
        ins, outs = refs[:4 * n], refs[4 * n:]
        for k in range(n):
            w_ref, g_ref, m_ref, v_ref = ins[4 * k:4 * k + 4]
            d, mo, vo = _adam_math(w_ref[...], g_ref[...], m_ref[...], v_ref[...])
            outs[3 * k][...] = d
            outs[3 * k + 1][...] = mo
            outs[3 * k + 2][...] = vo

    flat = [a for grp in groups for a in grp]
    shapes = [_sds(grp[0].shape, F32) for grp in groups for _ in range(3)]
    res = pl.pallas_call(
        body, name="adam_small", in_specs=[_VMEM] * (4 * n), out_specs=[_VMEM] * (3 * n), out_shape=shapes,
        compiler_params=_params(),
    )(*flat)
    return [tuple(res[3 * k:3 * k + 3]) for k in range(n)]


TM_FWD_A = 256
RELAY_STEP_FWD_A = 2
FORWARD_STEP_FWD_A = 5
RELAY_STEP_FWD_B = 2
TM_BWD_A = 256
RELAY_STEP_BWD_A = 3
TM_BWD_A_IN = 256
RELAY_STEP_BWD_A_IN = 4
RELAY_STEP_WGRAD_A_IN = 2
TM_FWD_B = 256
TM_HEAD = 512
TM_BWD_B = 256


def _pack(parts, rows):
    flat = jnp.concatenate([p.reshape(-1) for p in parts])
    return jnp.pad(flat, (0, NDEV * rows * LANES - flat.shape[0])).reshape(NDEV, rows, LANES)


def _unpack(packed, shapes):
    flat, out, off = packed.reshape(-1), [], 0
    for s in shapes:
        size = 1
        for d in s:
            size *= d
        out.append(flat[off:off + size].reshape(s))
        off += size
    return out


def kernel(x, norm_w, a_w_in, a_ln_w, a_ln_b, a_w_s, a_b_s, a_w_out, b_w_in, b_conv_w, b_conv_b, b_gate_a_w, b_gate_a_b, b_gate_x_w, b_gate_x_b, b_lambda, b_w_out, norm_f_w, loss_target, m_norm_w, m_a_w_in, m_a_ln_w, m_a_ln_b, m_a_w_s, m_a_b_s, m_a_w_out, m_b_w_in, m_b_conv_w, m_b_conv_b, m_b_gate_a_w, m_b_gate_a_b, m_b_gate_x_w, m_b_gate_x_b, m_b_lambda, m_b_w_out, m_norm_f_w, v_norm_w, v_a_w_in, v_a_ln_w, v_a_ln_b, v_a_w_s, v_a_b_s, v_a_w_out, v_b_w_in, v_b_conv_w, v_b_conv_b, v_b_gate_a_w, v_b_gate_a_b, v_b_gate_x_w, v_b_gate_x_b, v_b_lambda, v_b_w_out, v_norm_f_w):
    me = 4 * lax.axis_index("x") + 2 * lax.axis_index("y") + lax.axis_index("c")
    xs, tgt = x[0], loss_target[0]
    nw0, nw1, nfw = norm_w[0:1], norm_w[1:2], norm_f_w.reshape(1, D)
    w_s, bst = a_w_s[0], a_b_s[0].T
    gcat = jnp.concatenate([b_gate_a_w[0], b_gate_x_w[0]], axis=-1).astype(BF16)

    p8_shard = jnp.concatenate([b_conv_w[0], b_conv_b, b_gate_a_b, b_gate_x_b, b_lambda], axis=0)
    (z, h0, ya, pp), ((win_a8, p8_all), (wout_a8, win_b8)) = _fwd_a(
        xs, nw0, a_ln_w, a_ln_b, w_s, bst,
        [_Gather([a_w_in[0], p8_shard], [BF16, F32]), _Gather([a_w_out[0], b_w_in[0]], [BF16, BF16])],
        tm=TM_FWD_A, relay_step=RELAY_STEP_FWD_A)
    p8 = jnp.transpose(p8_all, (1, 0, 2)).reshape(SUBLANES, BW)
    wout_a = wout_a8.reshape(AW, D)
    (x1, zb, hs, h1, yb, *saved_b), ((wout_b8,),) = _fwd_b(
        xs, ya, wout_a, nw1, win_b8, p8, gcat, [_Gather([b_w_out[0]], [BF16])],
        tm=TM_FWD_B, relay_step=RELAY_STEP_FWD_B)
    wout_b = wout_b8.reshape(BW, D)
    dx2, dx2b, loss, g_nfw = _head(x1, yb, wout_b, nfw, tgt, tm=TM_HEAD)

    dx1, dx1b, dzb, g_p8, g_ga, g_gx, g_nw1 = _bwd_b(dx2, zb, hs, x1, saved_b, nw1, win_b8, p8, gcat, wout_b,
                                                     tm=TM_BWD_B)
    q_wout_b, acc_wout_b, _ = _wgrad(yb, dx2b, [], by_rows=True, per=2, name="wgrad_b_out")
    shapes_b = [(1, D), (1, D), (SUBLANES, BW), (1, 1)]
    pack_b = _pack([g_nfw, g_nw1, g_p8, loss], 16)
    small_b = _InChip([g_ga.reshape(NDEV, -1, HD), g_gx.reshape(NDEV, -1, HD), pack_b])
    q_win_b, acc_win_b, (sm_b, (l_wout_b,)) = _wgrad(h1, dzb, [small_b, _Exchange([q_wout_b])], by_rows=False, per=1,
                                                      name="wgrad_b_in")
    qs_b, accs_b = sm_b[:3], sm_b[3:]

    (dz, g_lnw, g_lnb, g_ws, g_bst), (lands_b, (l_win_b,)) = _bwd_a(
        dx1b, z, pp, a_ln_w, a_ln_b, w_s, bst, wout_a, [_Exchange(qs_b), _ExchangeVia(q_win_b)],
        tm=TM_BWD_A, relay_step=RELAY_STEP_BWD_A)
    shapes_a = [(1, AW), (1, AW), (CH, G)]
    pack_a = _pack([g_lnw, g_lnb, g_bst], 8)
    q_wout_a, acc_wout_a, (red_b, sm_a) = _wgrad(
        ya, dx1b, [_SumGather(accs_b, lands_b), _InChip([g_ws, pack_a])], by_rows=True, per=2,
        name="wgrad_a_out", relay_step=1)
    qs_a, accs_a = sm_a[:2], sm_a[2:]
    q_win_a, acc_win_a, rel_a, (lands_a, (l_wout_a,)) = _wgrad_cols_early(
        h0, dz, [_Exchange(qs_a), _ExchangeVia(q_wout_a)], name="wgrad_a_in", relay_step=RELAY_STEP_WGRAD_A_IN)
    (gx, g_nw0), (red_a, (l_win_a,)) = _bwd_a_in(
        dz, dx1, xs, nw0, win_a8, [_SumGather(accs_a, lands_a), _ExchangeRest(q_win_a, rel_a)],
        tm=TM_BWD_A_IN, relay_step=RELAY_STEP_BWD_A_IN)

    r_ga, r_gx, r_pack_b = red_b
    r_nfw, r_nw1, r_p8, loss = _unpack(r_pack_b, shapes_b)
    r_ws, r_pack_a = red_a
    r_lnw, r_lnb, r_bst = _unpack(r_pack_a, shapes_a)
    g_p8 = lax.dynamic_slice_in_dim(r_p8, me * (BW // NDEV), BW // NDEV, axis=1)
    loss = loss[0, 0]

    weights = dict(norm_w=norm_w, a_w_in=a_w_in, a_ln_w=a_ln_w, a_ln_b=a_ln_b, a_w_s=a_w_s, a_b_s=a_b_s, a_w_out=a_w_out,
                   b_w_in=b_w_in, b_conv_w=b_conv_w, b_conv_b=b_conv_b, b_gate_a_w=b_gate_a_w, b_gate_a_b=b_gate_a_b,
                   b_gate_x_w=b_gate_x_w, b_gate_x_b=b_gate_x_b, b_lambda=b_lambda, b_w_out=b_w_out, norm_f_w=norm_f_w)
    mom1 = dict(norm_w=m_norm_w, a_w_in=m_a_w_in, a_ln_w=m_a_ln_w, a_ln_b=m_a_ln_b, a_w_s=m_a_w_s, a_b_s=m_a_b_s,
                a_w_out=m_a_w_out, b_w_in=m_b_w_in, b_conv_w=m_b_conv_w, b_conv_b=m_b_conv_b, b_gate_a_w=m_b_gate_a_w,
                b_gate_a_b=m_b_gate_a_b, b_gate_x_w=m_b_gate_x_w, b_gate_x_b=m_b_gate_x_b, b_lambda=m_b_lambda,
                b_w_out=m_b_w_out, norm_f_w=m_norm_f_w)
    mom2 = dict(norm_w=v_norm_w, a_w_in=v_a_w_in, a_ln_w=v_a_ln_w, a_ln_b=v_a_ln_b, a_w_s=v_a_w_s, a_b_s=v_a_b_s,
                a_w_out=v_a_w_out, b_w_in=v_b_w_in, b_conv_w=v_b_conv_w, b_conv_b=v_b_conv_b, b_gate_a_w=v_b_gate_a_w,
                b_gate_a_b=v_b_gate_a_b, b_gate_x_w=v_b_gate_x_w, b_gate_x_b=v_b_gate_x_b, b_lambda=v_b_lambda,
                b_w_out=v_b_w_out, norm_f_w=v_norm_f_w)
    names = list(weights)

    def as2d(a):
        return a.reshape(-1, a.shape[-1])

    upd, grads = {}, {}
    for k, acc, land in (("a_w_in", acc_win_a, l_win_a), ("a_w_out", acc_wout_a, l_wout_a),
                         ("b_w_in", acc_win_b, l_win_b), ("b_w_out", acc_wout_b, l_wout_b)):
        g, d, mo, vo = _adam_big(as2d(weights[k]), acc, land, as2d(mom1[k]), as2d(mom2[k]), "adam_" + k)
        grads[k] = g[None]
        upd[k] = (d, mo, vo)
    grads.update(
        norm_w=jnp.concatenate([g_nw0, r_nw1], axis=0), a_ln_w=r_lnw, a_ln_b=r_lnb,
        a_w_s=r_ws.reshape(1, G, CH, CH), a_b_s=r_bst.T[None],
        b_conv_w=g_p8[None, 0:4], b_conv_b=g_p8[4:5], b_gate_a_w=r_ga.reshape(1, BH, HD, HD), b_gate_a_b=g_p8[5:6],
        b_gate_x_w=r_gx.reshape(1, BH, HD, HD), b_gate_x_b=g_p8[6:7], b_lambda=g_p8[7:8], norm_f_w=r_nfw.reshape(D))
    small_names = [k for k in names if k not in upd]
    res = _adam_small([(as2d(weights[k]), as2d(grads[k]), as2d(mom1[k]), as2d(mom2[k])) for k in small_names])
    for k, r3 in zip(small_names, res):
        upd[k] = r3
    deltas = [upd[k][0].reshape(weights[k].shape) for k in names]
    new_m = [upd[k][1].reshape(weights[k].shape) for k in names]
    new_v = [upd[k][2].reshape(weights[k].shape) for k in names]
    return (loss, gx[None], *[grads[k] for k in names], *deltas, *new_m, *new_v)
```

```python
import jax
import jax.numpy as jnp
from jax import lax
from jax.experimental import pallas as pl
from jax.experimental.pallas import tpu as pltpu

F32 = jnp.float32
BF16 = jnp.bfloat16
MESH = pl.DeviceIdType.MESH

NDEV = 8
NCHIP_OTHER = 3
D = 1024
AW = 2048
G = 8
GD = AW // G
CH = 128
BW = 1536
BH = 12
HD = BW // BH
CA = 3 * AW // NDEV
CB = 2 * BW // NDEV
RMS_EPS = 1e-6
LN_EPS = 1e-5
RG_C = 8.0
LR, B1, B2, ADAM_EPS, WD, STEP = 0.001, 0.9, 0.999, 1e-08, 0.01, 10
V7X_VMEM_BYTES = 64 * 1024 * 1024
VMEM_LIMIT = V7X_VMEM_BYTES - 8 * 1024 * 1024
SUBLANES = 8
LANES = 128
BF16_ROWS = 16
TRANSPOSE_ROWS = 256
ADAM_ROWS = 512
GELU_C = 0.7978845608028654
GELU_K = 0.044715

_VMEM = pl.BlockSpec(memory_space=pltpu.VMEM)
_HBM = pl.BlockSpec(memory_space=pltpu.HBM)


def _sds(shape, dtype):
    return jax.ShapeDtypeStruct(tuple(shape), dtype)


def _params(**kw):
    return pltpu.CompilerParams(vmem_limit_bytes=VMEM_LIMIT, **kw)


def _gelu_t(z):
    p = 0.5 * jnp.tanh(z * (GELU_C + (GELU_C * GELU_K) * (z * z))) + 0.5
    return z * p, p


def _dgelu(z, p):
    return p * (1.0 + (z * (1.0 - p)) * (2.0 * GELU_C + (6.0 * GELU_C * GELU_K) * (z * z)))


def _sigmoid(v):
    return 0.5 * jnp.tanh(0.5 * v) + 0.5


def _softplus_neg(lam):
    return jnp.maximum(-lam, 0.0) + jnp.log1p(jnp.exp(-jnp.abs(lam)))


def _dot(a, b):
    return jnp.dot(a, b, preferred_element_type=F32)


def _dot_nt(a, b):
    return lax.dot_general(a, b, (((1,), (1,)), ((), ())), preferred_element_type=F32)


def _rowsum(v):
    return jnp.sum(v, axis=0, keepdims=True)


def _causal_mask():
    r = lax.broadcasted_iota(jnp.int32, (CH, CH), 0)
    c = lax.broadcasted_iota(jnp.int32, (CH, CH), 1)
    return r >= c


def _rms(x):
    return lax.rsqrt(jnp.mean(x * x, axis=-1, keepdims=True) + RMS_EPS)


def _rms_bwd(dh, x, r, nw):
    gy = dh * nw
    return r * gy - x * (r * r * r) * jnp.mean(gy * x, axis=-1, keepdims=True)


def _place():
    return lax.axis_index("x"), lax.axis_index("y"), lax.axis_index("c")


def _other_chips(x, y):
    return [(1 - x, y), (x, 1 - y), (1 - x, 1 - y)]


GATHER_SLOTS = 10


def _gather_ops(ins, outs, send_sems, recv_sems, local_sems):
    n = len(ins)
    x, y, c = _place()
    sibling = (x, y, 1 - c)
    xn, yn, dg = _other_chips(x, y)
    split = [ins[i].shape[0] % (2 * BF16_ROWS) == 0 for i in range(n)]

    def blk(chip, core):
        return 4 * chip[0] + 2 * chip[1] + core

    me = blk((x, y), c)

    def part(ref, i, half):
        if half is None:
            return ref
        h = ins[i].shape[0] // 2
        return ref.at[pl.ds(half * h, h)]

    def copy(i, k, block, to, half=None, src=None):
        dst = part(outs[i].at[block], i, half)
        return pltpu.make_async_remote_copy(
            src_ref=dst if src is None else part(src, i, half), dst_ref=dst,
            send_sem=send_sems.at[k, i], recv_sem=recv_sems.at[k, i], device_id=to, device_id_type=MESH)

    def first_copies():
        mine = [pltpu.make_async_copy(ins[i], outs[i].at[me], local_sems.at[i]) for i in range(n)]
        first = []
        for i in range(n):
            first.append(copy(i, 0, me, sibling, src=ins[i]))
            if split[i]:
                first.append(copy(i, 1, me, (*xn, c), 0, ins[i]))
                first.append(copy(i, 3, me, (*yn, c), 1, ins[i]))
                first.append(copy(i, 2, me, (*xn, c), 1, ins[i]))
                first.append(copy(i, 4, me, (*yn, c), 0, ins[i]))
            else:
                first.append(copy(i, 1, me, (*xn, c), None, ins[i]))
                first.append(copy(i, 3, me, (*yn, c), None, ins[i]))
                first.append(copy(i, 5, me, (*dg, c), None, ins[i]))
        return mine, first

    def onward():
        out = []
        for i in range(n):
            if split[i]:
                out.append(copy(i, 5, blk(xn, c), (*yn, c), 0))
                out.append(copy(i, 6, blk(yn, c), (*xn, c), 1))
        return out

    def start():
        mine, first = first_copies()
        for cp in mine + first:
            cp.start()

    def relay():
        sends = onward()
        for i in range(n):
            if split[i]:
                copy(i, 1, blk(xn, c), sibling, 0).wait_recv()
                sends.pop(0).start()
                copy(i, 3, blk(yn, c), sibling, 1).wait_recv()
                sends.pop(0).start()

    def passes():
        return [copy(i, 7 + j, blk(chip, c), sibling) for i in range(n) for j, chip in enumerate((xn, yn, dg))]

    def forward():
        fwd = passes()
        for i in range(n):
            if split[i]:
                copy(i, 2, blk(xn, c), sibling, 1).wait_recv()
                fwd[3 * i].start()
                copy(i, 4, blk(yn, c), sibling, 0).wait_recv()
                fwd[3 * i + 1].start()
                copy(i, 5, blk(dg, c), sibling, 0).wait_recv()
                copy(i, 6, blk(dg, c), sibling, 1).wait_recv()
                fwd[3 * i + 2].start()
            else:
                copy(i, 1, blk(xn, c), sibling).wait_recv()
                fwd[3 * i].start()
                copy(i, 3, blk(yn, c), sibling).wait_recv()
                fwd[3 * i + 1].start()
                copy(i, 5, blk(dg, c), sibling).wait_recv()
                fwd[3 * i + 2].start()

    def finish():
        mine, first = first_copies()
        for i in range(n):
            copy(i, 0, blk((x, y), 1 - c), sibling).wait_recv()
            for j, chip in enumerate((xn, yn, dg)):
                copy(i, 7 + j, blk(chip, 1 - c), sibling).wait_recv()
        for cp in first + passes() + onward():
            cp.wait_send()
        for cp in mine:
            cp.wait()

    return start, relay, forward, finish


def _gather_sems(n):
    return [pltpu.SemaphoreType.DMA((GATHER_SLOTS, n)), pltpu.SemaphoreType.DMA((GATHER_SLOTS, n)),
            pltpu.SemaphoreType.DMA((n,))]


class _Gather:
    def __init__(self, shards, as_dtypes=None):
        n = len(shards)
        dts = [s.dtype for s in shards] if as_dtypes is None else list(as_dtypes)
        self.cast = [jnp.dtype(d) != s.dtype for d, s in zip(dts, shards)]
        self.ins = list(shards)
        self.in_specs = [_VMEM if c else _HBM for c in self.cast]
        self.out_shape = [_sds((NDEV,) + s.shape, d) for s, d in zip(shards, dts)]
        self.out_specs = [_HBM] * n
        self.scratch = [pltpu.VMEM(s.shape, d) for s, d, c in zip(shards, dts, self.cast) if c] + _gather_sems(n)

    def ops(self, ins, outs, scr):
        ncast = sum(self.cast)
        staged = iter(scr[:ncast])
        srcs = [next(staged) if c else ref for c, ref in zip(self.cast, ins)]
        start, relay, forward, finish = _gather_ops(srcs, outs, *scr[ncast:])

        def cast_and_start():
            for c, ref, src in zip(self.cast, ins, srcs):
                if c:
                    src[...] = ref[...].astype(src.dtype)
            start()

        return cast_and_start, relay, forward, finish


class _Exchange:
    def __init__(self, qs):
        n = len(qs)
        self.ins, self.in_specs = list(qs), [_HBM] * n
        self.out_shape = [_sds(q.shape, q.dtype) for q in qs]
        self.out_specs = [_HBM] * n
        self.scratch = [pltpu.SemaphoreType.DMA((NCHIP_OTHER, n)), pltpu.SemaphoreType.DMA((NCHIP_OTHER, n))]

    def ops(self, ins, outs, scr):
        send_sems, recv_sems = scr
        n = len(ins)
        x, y, c = _place()
        chips = _other_chips(x, y)

        def copies():
            return [pltpu.make_async_remote_copy(
                src_ref=ins[i].at[j], dst_ref=outs[i].at[j], send_sem=send_sems.at[j, i],
                recv_sem=recv_sems.at[j, i], device_id=(*chips[j], c), device_id_type=MESH)
                for i in range(n) for j in range(NCHIP_OTHER)]

        def start():
            for cp in copies():
                cp.start()

        def finish():
            cps = copies()
            for cp in cps:
                cp.wait_recv()
            for cp in cps:
                cp.wait_send()

        return start, lambda: None, finish


class _ExchangeVia:
    def __init__(self, q):
        _, r, cd = q.shape
        half = (2, r // 2, cd)
        self.ins, self.in_specs = [q], [_HBM]
        self.out_shape, self.out_specs = [_sds((2, r, cd), q.dtype)], [_HBM]
        self.scratch = [pltpu.VMEM(half, q.dtype), pltpu.VMEM(half, q.dtype), pltpu.VMEM(half, q.dtype),
                        pltpu.SemaphoreType.DMA((6,)), pltpu.SemaphoreType.DMA((6,)), pltpu.SemaphoreType.DMA((2,))]

    def ops(self, ins, outs, scr):
        (q,), (land,) = ins, outs
        relayed, own, comb, send_sems, recv_sems, local_sems = scr
        h = q.shape[1] // 2
        x, y, c = _place()
        xn, yn, _ = _other_chips(x, y)
        h0, h1 = pl.ds(0, h), pl.ds(h, h)

        def remote(k, src, dst, chip):
            return pltpu.make_async_remote_copy(src_ref=src, dst_ref=dst, send_sem=send_sems.at[k],
                                                recv_sem=recv_sems.at[k], device_id=(*chip, c), device_id_type=MESH)

        def via():
            return [remote(2, q.at[2, h0], relayed.at[0], xn), remote(3, q.at[2, h1], relayed.at[1], yn)]

        def direct():
            return [remote(0, q.at[0, h0], land.at[0, h0], xn), remote(1, q.at[1, h1], land.at[1, h1], yn)]

        def second():
            return [remote(4, comb.at[0], land.at[1, h0], yn), remote(5, comb.at[1], land.at[0, h1], xn)]

        def mine():
            return [pltpu.make_async_copy(q.at[1, h0], own.at[0], local_sems.at[0]),
                    pltpu.make_async_copy(q.at[0, h1], own.at[1], local_sems.at[1])]

        def start():
            for cp in via() + direct() + mine():
                cp.start()

        def relay():
            arrived, loaded, onward = via(), mine(), second()
            for k in range(2):
                arrived[k].wait_recv()
                loaded[k].wait()
                comb[k] = (own[k].astype(F32) + relayed[k].astype(F32)).astype(comb.dtype)
                onward[k].start()

        def finish():
            landing = direct() + second()
            for cp in landing:
                cp.wait_recv()
            for cp in via() + landing:
                cp.wait_send()

        return start, relay, finish


class _SumGather:
    def __init__(self, accs, lands):
        n = len(accs)
        self.n = n
        self.ins, self.in_specs = list(accs) + list(lands), [_VMEM] * (2 * n)
        self.out_shape = [_sds((NDEV,) + a.shape, a.dtype) for a in accs]
        self.out_specs = [_HBM] * n
        self.scratch = [pltpu.VMEM(a.shape, a.dtype) for a in accs] + _gather_sems(n)

    def ops(self, ins, outs, scr):
        n = self.n
        accs, lands, mine = ins[:n], ins[n:], scr[:n]
        g_start, relay, forward, finish = _gather_ops(mine, outs, *scr[n:])

        def start():
            for i in range(n):
                mine[i][...] = accs[i][...] + lands[i][0] + lands[i][1] + lands[i][2]
            g_start()

        return start, relay, forward, finish


def _call(main, jobs, *, name, grid, ins, in_specs, out_shape, out_specs, scratch, relay_step=0, first=0,
          prologue=None, forward_step=None):
    (nsteps,) = grid
    n_in, n_out, n_scr = len(ins), len(out_shape), len(scratch)

    def body(*refs):
        pos = [0]

        def take(k):
            r = refs[pos[0]:pos[0] + k]
            pos[0] += k
            return r

        m_in = take(n_in)
        j_in = [take(len(j.ins)) for j in jobs]
        m_out = take(n_out)
        j_out = [take(len(j.out_shape)) for j in jobs]
        m_scr = take(n_scr)
        j_scr = [take(len(j.scratch)) for j in jobs]
        ops = [_four(j.ops(a, b, s)) for j, a, b, s in zip(jobs, j_in, j_out, j_scr)]
        i = pl.program_id(0)

        if ops:
            @pl.when(i == 0)
            def _():
                for o in ops[:first]:
                    o[0]()
                for o in ops[:first]:
                    o[1]()
                for o in ops[first:]:
                    o[0]()
                for o in ops[:first]:
                    o[2]()
                for o in ops[:first]:
                    o[3]()
                if prologue is not None:
                    prologue(j_out[:first], m_scr)

        main(i, m_in, m_out, m_scr)

        forward_at = max(relay_step, nsteps - 2) if forward_step is None else min(forward_step, nsteps - 1)
        for stage, at in ((1, min(relay_step, nsteps - 1)), (2, forward_at), (3, nsteps - 1)):
            if ops[first:]:
                @pl.when(i == at)
                def _():
                    for o in ops[first:]:
                        o[stage]()

    res = pl.pallas_call(
        body, name=name, grid=grid,
        in_specs=list(in_specs) + [s for j in jobs for s in j.in_specs],
        out_specs=list(out_specs) + [s for j in jobs for s in j.out_specs],
        out_shape=list(out_shape) + [s for j in jobs for s in j.out_shape],
        scratch_shapes=list(scratch) + [s for j in jobs for s in j.scratch],
        compiler_params=_params(dimension_semantics=("arbitrary",)),
    )(*ins, *[a for j in jobs for a in j.ins])
    main_out, rest, job_out = res[:n_out], res[n_out:], []
    for j in jobs:
        k = len(j.out_shape)
        job_out.append(rest[:k])
        rest = rest[k:]
    return main_out, job_out


def _four(ops):
    return ops if len(ops) == 4 else (ops[0], ops[1], lambda: None, ops[2])


class _InChip:
    def __init__(self, ps):
        n = len(ps)
        self.n = n
        blk = [p.shape[1:] for p in ps]
        self.ins, self.in_specs = list(ps), [_HBM] * n
        self.out_shape = [_sds((NCHIP_OTHER,) + b, p.dtype) for b, p in zip(blk, ps)] + [_sds(b, F32) for b in blk]
        self.out_specs = [_VMEM] * (2 * n)
        self.scratch = ([pltpu.VMEM((4,) + b, p.dtype) for b, p in zip(blk, ps)] * 2
                        + [pltpu.SemaphoreType.DMA((4, n))] * 3)

    def ops(self, ins, outs, scr):
        n = self.n
        q_refs, acc_refs = outs[:n], outs[n:]
        mines, lands = scr[:n], scr[n:2 * n]
        send_sems, recv_sems, local_sems = scr[2 * n:]
        x, y, c = _place()
        sibling = (x, y, 1 - c)

        def copies():
            out = []
            for i in range(n):
                for pi in range(4):
                    loc = pltpu.make_async_copy(ins[i].at[2 * pi + c], mines[i].at[pi], local_sems.at[pi, i])
                    cp = pltpu.make_async_remote_copy(
                        src_ref=ins[i].at[2 * pi + (1 - c)], dst_ref=lands[i].at[pi],
                        send_sem=send_sems.at[pi, i], recv_sem=recv_sems.at[pi, i],
                        device_id=sibling, device_id_type=MESH)
                    out.append((loc, cp))
            return out

        def start():
            for loc, cp in copies():
                loc.start()
                cp.start()

        def finish():
            pairs = copies()
            for loc, cp in pairs:
                loc.wait()
                cp.wait_recv()
            for i in range(n):
                _chip_sums(mines[i], lands[i], q_refs[i], acc_refs[i], x, y)
            for _, cp in pairs:
                cp.wait_send()

        return start, lambda: None, finish


def _chip_sums(mine, land, q_ref, acc_ref, x, y):
    for j, (qx, qy) in enumerate(_other_chips(x, y)):
        qi = 2 * qx + qy
        q_ref[j] = (mine[qi].astype(F32) + land[qi].astype(F32)).astype(q_ref.dtype)
    mi = 2 * x + y
    acc_ref[...] = mine[mi].astype(F32) + land[mi].astype(F32)


def _direct_sum(v, buf, send_sems, recv_sems):
    x, y, c = _place()
    me = 4 * x + 2 * y + c
    buf[me] = v
    cps = []
    for k in range(1, NDEV):
        fx, fy, fc = (k >> 2) & 1, (k >> 1) & 1, k & 1
        peer = ((1 - x) if fx else x, (1 - y) if fy else y, (1 - c) if fc else c)
        cps.append((peer, pltpu.make_async_remote_copy(
            src_ref=buf.at[me], dst_ref=buf.at[me], send_sem=send_sems.at[k - 1], recv_sem=recv_sems.at[k - 1],
            device_id=peer, device_id_type=MESH)))
    for _, cp in cps:
        cp.start()
    for k, (peer, _) in enumerate(cps):
        theirs = 4 * peer[0] + 2 * peer[1] + peer[2]
        pltpu.make_async_remote_copy(
            src_ref=buf.at[theirs], dst_ref=buf.at[theirs], send_sem=send_sems.at[k], recv_sem=recv_sems.at[k],
            device_id=peer, device_id_type=MESH).wait_recv()
    acc = buf[0]
    for j in range(1, NDEV):
        acc = acc + buf[j]
    for _, cp in cps:
        cp.wait_send()
    return acc


def _direct_sum_scratch(shape, dtype):
    return [pltpu.VMEM((NDEV,) + tuple(shape), dtype), pltpu.SemaphoreType.DMA((NDEV - 1,)),
            pltpu.SemaphoreType.DMA((NDEV - 1,))]


def _fwd_a(x, nw, lnw, lnb, ws, bst, jobs, *, tm, relay_step):
    s_len = x.shape[0]
    nt = s_len // tm
    nch = tm // CH

    def main(i, ins, outs, scr):
        x_ref, nw_ref, lnw_ref, lnb_ref, ws_ref, bst_ref = ins
        z_ref, h_ref, y_ref, pp_ref = outs
        wc_scr, gv_scr, win_ref = scr

        @pl.when(i == 0)
        def _():
            m = _causal_mask()
            for g in range(G):
                wc_scr[g] = jnp.where(m, ws_ref[g], 0.0).astype(BF16)

        x = x_ref[...]
        h = (x * _rms(x) * nw_ref[...]).astype(BF16)
        h_ref[...] = h
        for k in range(NDEV):
            z_ref[:, k * CA:(k + 1) * CA] = _dot(h, win_ref[k])

        ssum = jnp.zeros((tm, 1), F32)
        for g in range(G):
            vs = slice(AW + g * GD, AW + (g + 1) * GD)
            gv, pv = _gelu_t(z_ref[:, vs])
            pp_ref[:, vs] = pv.astype(BF16)
            gv_scr[:, g * GD:(g + 1) * GD] = gv
            ssum = ssum + jnp.sum(gv, axis=-1, keepdims=True)
        mu = ssum * (1.0 / AW)
        vsum = jnp.zeros((tm, 1), F32)
        for g in range(G):
            dlt = gv_scr[:, g * GD:(g + 1) * GD] - mu
            vsum = vsum + jnp.sum(dlt * dlt, axis=-1, keepdims=True)
        rstd = lax.rsqrt(vsum * (1.0 / AW) + LN_EPS)

        for g in range(G):
            cs = slice(g * GD, (g + 1) * GD)
            gs = slice(2 * AW + g * GD, 2 * AW + (g + 1) * GD)
            v = (gv_scr[:, cs] - mu) * rstd * lnw_ref[:, cs] + lnb_ref[:, cs]
            vb = v.astype(BF16)
            u, pu = _gelu_t(z_ref[:, cs])
            pp_ref[:, cs] = pu.astype(BF16)
            zg = z_ref[:, gs]
            sig = _sigmoid(zg)
            pp_ref[:, gs] = sig.astype(BF16)
            sg = zg * sig
            for n in range(nch):
                rs = slice(n * CH, (n + 1) * CH)
                s = _dot(wc_scr[g], vb[rs, :]) + bst_ref[:, g:g + 1]
                y_ref[rs, cs] = (u[rs, :] * s * sg[rs, :]).astype(BF16)

    tile = lambda w: pl.BlockSpec((tm, w), lambda i: (i, 0))
    return _call(
        main, jobs, name="fwd_a", grid=(nt,), relay_step=relay_step, first=1, forward_step=FORWARD_STEP_FWD_A,
        prologue=lambda gathered, scr: pltpu.sync_copy(gathered[0][0], scr[2]),
        ins=[x, nw, lnw, lnb, ws, bst], in_specs=[tile(D), _VMEM, _VMEM, _VMEM, _VMEM, _VMEM],
        out_shape=[_sds((s_len, 3 * AW), F32), _sds((s_len, D), BF16), _sds((s_len, AW), BF16),
                   _sds((s_len, 3 * AW), BF16)],
        out_specs=[tile(3 * AW), tile(D), tile(AW), tile(3 * AW)],
        scratch=[pltpu.VMEM((G, CH, CH), BF16), pltpu.VMEM((tm, AW), F32), pltpu.VMEM((NDEV, D, CA), BF16)])


def _bwd_a(dx1, z, pp, lnw, lnb, ws, bst, wout, jobs, *, tm, relay_step):
    s_len = dx1.shape[0]
    nt = s_len // tm
    nch = tm // CH

    def main(i, ins, outs, scr):
        dx1_ref, z_ref, pp_ref, lnw_ref, lnb_ref, ws_ref, bst_ref, wout_ref = ins
        dz_ref, glnw_ref, glnb_ref, gws_ref, gbst_ref = outs
        wc_scr, wct_scr, vh_scr, dgv_scr, dy_scr, dv_scr, gbs_acc, gwc_acc = scr

        @pl.when(i == 0)
        def _():
            m = _causal_mask()
            for g in range(G):
                wm = jnp.where(m, ws_ref[g], 0.0)
                wc_scr[g] = wm.astype(BF16)
                wct_scr[g] = wm.T.astype(BF16)
            glnw_ref[...] = jnp.zeros_like(glnw_ref)
            glnb_ref[...] = jnp.zeros_like(glnb_ref)
            gbs_acc[...] = jnp.zeros_like(gbs_acc)
            gwc_acc[...] = jnp.zeros_like(gwc_acc)

        dy_scr[...] = _dot_nt(dx1_ref[...], wout_ref[...])

        ssum = jnp.zeros((tm, 1), F32)
        for g in range(G):
            cs = slice(g * GD, (g + 1) * GD)
            vs = slice(AW + g * GD, AW + (g + 1) * GD)
            zv = z_ref[:, vs]
            pv = pp_ref[:, vs].astype(F32)
            gv = zv * pv
            vh_scr[:, cs] = gv
            dgv_scr[:, cs] = _dgelu(zv, pv)
            ssum = ssum + jnp.sum(gv, axis=-1, keepdims=True)
        mu = ssum * (1.0 / AW)
        vsum = jnp.zeros((tm, 1), F32)
        for g in range(G):
            dlt = vh_scr[:, g * GD:(g + 1) * GD] - mu
            vsum = vsum + jnp.sum(dlt * dlt, axis=-1, keepdims=True)
        rstd = lax.rsqrt(vsum * (1.0 / AW) + LN_EPS)

        m1 = jnp.zeros((tm, 1), F32)
        m2 = jnp.zeros((tm, 1), F32)
        for g in range(G):
            cs = slice(g * GD, (g + 1) * GD)
            gs = slice(2 * AW + g * GD, 2 * AW + (g + 1) * GD)
            vhat = (vh_scr[:, cs] - mu) * rstd
            vh_scr[:, cs] = vhat
            vb = (vhat * lnw_ref[:, cs] + lnb_ref[:, cs]).astype(BF16)
            zu = z_ref[:, cs]
            tu = pp_ref[:, cs].astype(F32)
            u = zu * tu
            zg = z_ref[:, gs]
            sig = pp_ref[:, gs].astype(F32)
            sg = zg * sig
            dy = dy_scr[:, cs]
            dsf = dy * u * sg
            dsb = dsf.astype(BF16)
            dvs = []
            for n in range(nch):
                rs = slice(n * CH, (n + 1) * CH)
                s = _dot(wc_scr[g], vb[rs, :]) + bst_ref[:, g:g + 1]
                dys = dy[rs, :] * s
                dz_ref[rs, cs] = (dys * sg[rs, :] * _dgelu(zu[rs, :], tu[rs, :])).astype(BF16)
                dz_ref[rs, gs] = (dys * u[rs, :] * (sig[rs, :] * (1.0 + zg[rs, :] * (1.0 - sig[rs, :])))).astype(BF16)
                gbs_acc[g] += dsf[rs, :]
                gwc_acc[g] += _dot_nt(dsb[rs, :], vb[rs, :])
                dvs.append(_dot(wct_scr[g], dsb[rs, :]))
            dv = jnp.concatenate(dvs, axis=0) if nch > 1 else dvs[0]
            glnw_ref[:, cs] += _rowsum(dv * vhat)
            glnb_ref[:, cs] += _rowsum(dv)
            dvh = dv * lnw_ref[:, cs]
            dv_scr[:, cs] = dvh
            m1 = m1 + jnp.sum(dvh, axis=-1, keepdims=True)
            m2 = m2 + jnp.sum(dvh * vhat, axis=-1, keepdims=True)
        m1 = m1 * (1.0 / AW)
        m2 = m2 * (1.0 / AW)
        for g in range(G):
            cs = slice(g * GD, (g + 1) * GD)
            dgv = rstd * (dv_scr[:, cs] - m1 - vh_scr[:, cs] * m2)
            dz_ref[:, AW + g * GD:AW + (g + 1) * GD] = (dgv * dgv_scr[:, cs]).astype(BF16)

        @pl.when(i == nt - 1)
        def _():
            m = _causal_mask()
            for g in range(G):
                gws_ref[g] = jnp.where(m, gwc_acc[g], 0.0)
                gbst_ref[:, g:g + 1] = jnp.sum(gbs_acc[g], axis=-1, keepdims=True)

    tile = lambda w: pl.BlockSpec((tm, w), lambda i: (i, 0))
    whole = lambda *s: pl.BlockSpec(s, lambda i: (0,) * len(s))
    big = lambda dt: pltpu.VMEM((tm, AW), dt)
    return _call(
        main, jobs, name="bwd_a", grid=(nt,), relay_step=relay_step,
        ins=[dx1, z, pp, lnw, lnb, ws, bst, wout],
        in_specs=[tile(D), tile(3 * AW), tile(3 * AW), _VMEM, _VMEM, _VMEM, _VMEM, _VMEM],
        out_shape=[_sds((s_len, 3 * AW), BF16), _sds((1, AW), F32), _sds((1, AW), F32), _sds((G, CH, CH), F32),
                   _sds((CH, G), F32)],
        out_specs=[tile(3 * AW), whole(1, AW), whole(1, AW), whole(G, CH, CH), whole(CH, G)],
        scratch=[pltpu.VMEM((G, CH, CH), BF16), pltpu.VMEM((G, CH, CH), BF16), big(F32), big(F32), big(F32), big(F32),
                 pltpu.VMEM((G, CH, GD), F32), pltpu.VMEM((G, CH, CH), F32)])


def _bwd_a_in(dz, dx1, x, nw, win8, jobs, *, tm, relay_step):
    s_len = x.shape[0]
    nt = s_len // tm

    def main(i, ins, outs, scr):
        dz_ref, dx1_ref, x_ref, nw_ref, win_ref = ins
        gx_ref, gnw_ref = outs

        @pl.when(i == 0)
        def _():
            gnw_ref[...] = jnp.zeros_like(gnw_ref)

        dh = jnp.zeros((tm, D), F32)
        for k in range(NDEV):
            dh = dh + _dot_nt(dz_ref[:, k * CA:(k + 1) * CA], win_ref[k])
        x = x_ref[...]
        r = _rms(x)
        gx_ref[...] = dx1_ref[...] + _rms_bwd(dh, x, r, nw_ref[...])
        gnw_ref[...] += _rowsum(dh * x * r)

        @pl.when(i == nt - 1)
        def _():
            gnw_ref[...] = _direct_sum(gnw_ref[...], *scr)

    tile = lambda w: pl.BlockSpec((tm, w), lambda i: (i, 0))
    return _call(
        main, jobs, name="bwd_a_in", grid=(nt,), relay_step=relay_step, forward_step=nt - 1,
        ins=[dz, dx1, x, nw, win8], in_specs=[tile(3 * AW), tile(D), tile(D), _VMEM, _VMEM],
        out_shape=[_sds((s_len, D), F32), _sds((1, D), F32)],
        out_specs=[tile(D), pl.BlockSpec((1, D), lambda i: (0, 0))], scratch=_direct_sum_scratch((1, D), F32))


def _conv(p8_ref, cs, xb, xm1, xm2, xm3):
    xc = p8_ref[4:5, cs] + p8_ref[3:4, cs] * xb
    xc = xc + p8_ref[0:1, cs] * xm3
    xc = xc + p8_ref[1:2, cs] * xm2
    return xc + p8_ref[2:3, cs] * xm1


def _gates(p8_ref, gcat_ref, hh, xc):
    cs = slice(hh * HD, (hh + 1) * HD)
    pre = _dot(xc.astype(BF16), gcat_ref[hh])
    r = _sigmoid(pre[:, :HD] + p8_ref[5:6, cs])
    ig = _sigmoid(pre[:, HD:] + p8_ref[6:7, cs])
    sp = _softplus_neg(p8_ref[7:8, cs])
    la = (-RG_C) * r * sp
    a = jnp.exp(la)
    half_log = 0.5 * jnp.log(jnp.tanh(-la) * (1.0 + a * a))
    return r, ig, sp, a, jnp.exp(half_log), jnp.exp(-half_log)


def _scan_rows(a_ref, b_ref, out_ref, carry, tm, reverse):
    row = lax.broadcasted_iota(jnp.int32, (SUBLANES, BW), 0)
    ngrp = tm // SUBLANES

    def step(j, cr):
        jj = (ngrp - 1 - j) if reverse else j
        off = pl.multiple_of(jj * SUBLANES, SUBLANES)
        a = a_ref[pl.ds(off, SUBLANES), :]
        b = b_ref[pl.ds(off, SUBLANES), :]
        for sh in (1, 2, 4):
            if reverse:
                a_s = pltpu.roll(a, SUBLANES - sh, 0)
                b_s = pltpu.roll(b, SUBLANES - sh, 0)
                m = row < SUBLANES - sh
            else:
                a_s = pltpu.roll(a, sh, 0)
                b_s = pltpu.roll(b, sh, 0)
                m = row >= sh
            b = jnp.where(m, a * b_s + b, b)
            a = jnp.where(m, a * a_s, a)
        o = b + a * cr
        out_ref[pl.ds(off, SUBLANES), :] = o
        return o[0:1, :] if reverse else o[SUBLANES - 1:SUBLANES, :]

    return lax.fori_loop(0, ngrp, step, carry)


def _fwd_b(x, ya, wout_a, nw, win8, p8, gcat, jobs, *, tm, relay_step):
    s_len = x.shape[0]
    nt = s_len // tm

    def main(i, ins, outs, scr):
        x_ref, ya_ref, wouta_ref, nw_ref, win_ref, p8_ref, gcat_ref = ins
        x1_ref, zb_ref, hs_ref, h1_ref, yb_ref, xc_ref, a_ref, cc_ref, r_ref, ig_ref, m_ref = outs
        xbe_scr, b_scr, k_scr, carry_scr = scr

        @pl.when(i == 0)
        def _():
            xbe_scr[0:SUBLANES, :] = jnp.zeros((SUBLANES, BW), F32)
            carry_scr[...] = jnp.zeros_like(carry_scr)

        x1 = x_ref[...] + _dot(ya_ref[...], wouta_ref[...])
        x1_ref[...] = x1
        h = (x1 * _rms(x1) * nw_ref[...]).astype(BF16)
        h1_ref[...] = h
        for k in range(NDEV):
            zb_ref[:, k * CB:(k + 1) * CB] = _dot(h, win_ref[k])
        xbe_scr[SUBLANES:SUBLANES + tm, :] = zb_ref[:, :BW]
        for hh in range(BH):
            cs = slice(hh * HD, (hh + 1) * HD)
            xc = _conv(p8_ref, cs, xbe_scr[SUBLANES:SUBLANES + tm, cs], xbe_scr[7:7 + tm, cs],
                       xbe_scr[6:6 + tm, cs], xbe_scr[5:5 + tm, cs])
            r, ig, _, a, mult, rm = _gates(p8_ref, gcat_ref, hh, xc)
            ixc = ig * xc
            xc_ref[:, cs] = xc
            a_ref[:, cs] = a
            r_ref[:, cs] = r.astype(BF16)
            ig_ref[:, cs] = ig.astype(BF16)
            m_ref[:, cs] = mult.astype(BF16)
            b_scr[:, cs] = mult * ixc
            k_scr[:, cs] = ixc * (a * a * rm)
        xbe_scr[0:SUBLANES, :] = xbe_scr[tm:tm + SUBLANES, :]
        carry_scr[...] = _scan_rows(a_ref, b_scr, hs_ref, carry_scr[...], tm, False)
        for hh in range(BH):
            cs = slice(hh * HD, (hh + 1) * HD)
            gt = zb_ref[:, BW + hh * HD:BW + (hh + 1) * HD]
            hsv = hs_ref[:, cs]
            yb_ref[:, cs] = (hsv * (gt * _sigmoid(gt))).astype(BF16)
            cc_ref[:, cs] = (hsv - b_scr[:, cs]) - k_scr[:, cs]

    tile = lambda w: pl.BlockSpec((tm, w), lambda i: (i, 0))
    wide = lambda dt: _sds((s_len, BW), dt)
    return _call(
        main, jobs, name="fwd_b", grid=(nt,), relay_step=relay_step,
        ins=[x, ya, wout_a, nw, win8, p8, gcat], in_specs=[tile(D), tile(AW), _VMEM, _VMEM, _VMEM, _VMEM, _VMEM],
        out_shape=[_sds((s_len, D), F32), _sds((s_len, 2 * BW), F32), wide(F32), _sds((s_len, D), BF16), wide(BF16),
                   wide(F32), wide(F32), wide(F32), wide(BF16), wide(BF16), wide(BF16)],
        out_specs=[tile(D), tile(2 * BW), tile(BW), tile(D)] + [tile(BW)] * 7,
        scratch=[pltpu.VMEM((tm + SUBLANES, BW), F32), pltpu.VMEM((tm, BW), F32), pltpu.VMEM((tm, BW), F32),
                 pltpu.VMEM((1, BW), F32)])


def _head(x1, yb, wout, nfw, tgt, *, tm):
    s_len = x1.shape[0]

    def main(i, ins, outs, scr):
        x1_ref, yb_ref, wout_ref, nfw_ref, t_ref = ins
        dx2_ref, dx2b_ref, loss_ref, gnfw_ref = outs

        @pl.when(i == 0)
        def _():
            loss_ref[...] = jnp.zeros_like(loss_ref)
            gnfw_ref[...] = jnp.zeros_like(gnfw_ref)

        x2 = x1_ref[...] + _dot(yb_ref[...], wout_ref[...])
        rf = _rms(x2)
        xn = x2 * rf
        e = xn * nfw_ref[...] - t_ref[...]
        loss_ref[...] += (0.5 / D) * jnp.sum(jnp.sum(e * e, axis=-1, keepdims=True), axis=0, keepdims=True)
        dyf = e * (1.0 / D)
        gnfw_ref[...] += _rowsum(dyf * xn)
        dx2 = _rms_bwd(dyf, x2, rf, nfw_ref[...])
        dx2_ref[...] = dx2
        dx2b_ref[...] = dx2.astype(BF16)

    tile = lambda w: pl.BlockSpec((tm, w), lambda i: (i, 0))
    whole = lambda *s: pl.BlockSpec(s, lambda i: (0,) * len(s))
    (dx2, dx2b, loss, gnfw), _ = _call(
        main, [], name="head", grid=(s_len // tm,),
        ins=[x1, yb, wout, nfw, tgt], in_specs=[tile(D), tile(BW), _VMEM, _VMEM, tile(D)],
        out_shape=[_sds((s_len, D), F32), _sds((s_len, D), BF16), _sds((1, 1), F32), _sds((1, D), F32)],
        out_specs=[tile(D), tile(D), whole(1, 1), whole(1, D)], scratch=[])
    return dx2, dx2b, loss, gnfw


def _bwd_b(dx2, zb, hs, x1, saved, nw, win8, p8, gcat, wout, *, tm):
    s_len = x1.shape[0]
    nt = s_len // tm

    def main(i, ins, outs, scr):
        (dx2_ref, zb_ref, hs_ref, x1_ref, xc_ref, a_ref, cc_ref, r_ref, ig_ref, m_ref,
         nw_ref, win_ref, p8_ref, gcat_ref, wout_ref) = ins
        dx1_ref, dx1b_ref, dzb_ref, gp8_ref, gga_ref, ggx_ref, gnw_ref = outs
        ae_scr, an_scr, dhd_scr, dh_scr, dy_scr, dxce_scr, carry_scr, afirst_scr = scr

        @pl.when(i == 0)
        def _():
            gp8_ref[...] = jnp.zeros_like(gp8_ref)
            gga_ref[...] = jnp.zeros_like(gga_ref)
            ggx_ref[...] = jnp.zeros_like(ggx_ref)
            gnw_ref[...] = jnp.zeros_like(gnw_ref)
            dxce_scr[tm:tm + SUBLANES, :] = jnp.zeros((SUBLANES, BW), F32)
            carry_scr[...] = jnp.zeros_like(carry_scr)
            afirst_scr[...] = jnp.zeros_like(afirst_scr)

        dx2 = dx2_ref[...]
        dy_scr[...] = _dot_nt(dx2.astype(BF16), wout_ref[...])
        for hh in range(BH):
            cs = slice(hh * HD, (hh + 1) * HD)
            gs = slice(BW + hh * HD, BW + (hh + 1) * HD)
            gt = zb_ref[:, gs]
            sig = _sigmoid(gt)
            dy = dy_scr[:, cs]
            dhd_scr[:, cs] = dy * (gt * sig)
            dzb_ref[:, gs] = (dy * hs_ref[:, cs] * (sig * (1.0 + gt * (1.0 - sig)))).astype(BF16)

        ae_scr[0:tm, :] = a_ref[...]
        ae_scr[tm:tm + SUBLANES, :] = jnp.broadcast_to(afirst_scr[...], (SUBLANES, BW))
        an_scr[...] = ae_scr[1:1 + tm, :]
        afirst_scr[...] = ae_scr[0:1, :]
        carry_scr[...] = _scan_rows(an_scr, dhd_scr, dh_scr, carry_scr[...], tm, True)

        for hh in range(BH):
            cs = slice(hh * HD, (hh + 1) * HD)
            dh = dh_scr[:, cs]
            mult = m_ref[:, cs].astype(F32)
            ig = ig_ref[:, cs].astype(F32)
            r = r_ref[:, cs].astype(F32)
            xc = xc_ref[:, cs]
            lam = p8_ref[7:8, cs]
            sp = _softplus_neg(lam)
            dla = dh * cc_ref[:, cs]
            gp8_ref[7:8, cs] += _rowsum(dla * ((-RG_C) * r)) * (-_sigmoid(-lam))
            dpr = dla * ((-RG_C) * sp) * (r * (1.0 - r))
            dpi = dh * mult * xc * (ig * (1.0 - ig))
            gp8_ref[5:6, cs] += _rowsum(dpr)
            gp8_ref[6:7, cs] += _rowsum(dpi)
            dcat = jnp.concatenate([dpr, dpi], axis=1).astype(BF16)
            dxc = dh * mult * ig + _dot_nt(dcat, gcat_ref[hh])
            gg = _dot(xc.T.astype(BF16), dcat)
            gga_ref[hh] += gg[:, :HD]
            ggx_ref[hh] += gg[:, HD:]
            dxce_scr[0:tm, cs] = dxc
            gp8_ref[4:5, cs] += _rowsum(dxc)
        for hh in range(BH):
            cs = slice(hh * HD, (hh + 1) * HD)
            xb = zb_ref[:, cs]
            d0, d1 = dxce_scr[0:tm, cs], dxce_scr[1:1 + tm, cs]
            d2, d3 = dxce_scr[2:2 + tm, cs], dxce_scr[3:3 + tm, cs]
            dzb_ref[:, cs] = (p8_ref[3:4, cs] * d0 + p8_ref[2:3, cs] * d1 + p8_ref[1:2, cs] * d2
                              + p8_ref[0:1, cs] * d3).astype(BF16)
            gp8_ref[3:4, cs] += _rowsum(d0 * xb)
            gp8_ref[2:3, cs] += _rowsum(d1 * xb)
            gp8_ref[1:2, cs] += _rowsum(d2 * xb)
            gp8_ref[0:1, cs] += _rowsum(d3 * xb)
        dxce_scr[tm:tm + SUBLANES, :] = dxce_scr[0:SUBLANES, :]

        dh1 = jnp.zeros((tm, D), F32)
        for k in range(NDEV):
            dh1 = dh1 + _dot_nt(dzb_ref[:, k * CB:(k + 1) * CB], win_ref[k])
        x1 = x1_ref[...]
        r1 = _rms(x1)
        dx1 = dx2 + _rms_bwd(dh1, x1, r1, nw_ref[...])
        dx1_ref[...] = dx1
        dx1b_ref[...] = dx1.astype(BF16)
        gnw_ref[...] += _rowsum(dh1 * x1 * r1)

    tile = lambda w: pl.BlockSpec((tm, w), lambda i: (nt - 1 - i, 0))
    whole = lambda *s: pl.BlockSpec(s, lambda i: (0,) * len(s))
    full = lambda: pltpu.VMEM((tm, BW), F32)
    ext = lambda: pltpu.VMEM((tm + SUBLANES, BW), F32)
    out, _ = _call(
        main, [], name="bwd_b", grid=(nt,),
        ins=[dx2, zb, hs, x1, *saved, nw, win8, p8, gcat, wout],
        in_specs=[tile(D), tile(2 * BW), tile(BW), tile(D)] + [tile(BW)] * 6 + [_VMEM] * 5,
        out_shape=[_sds((s_len, D), F32), _sds((s_len, D), BF16), _sds((s_len, 2 * BW), BF16), _sds((SUBLANES, BW), F32),
                   _sds((BH, HD, HD), F32), _sds((BH, HD, HD), F32), _sds((1, D), F32)],
        out_specs=[tile(D), tile(D), tile(2 * BW), whole(SUBLANES, BW), whole(BH, HD, HD), whole(BH, HD, HD),
                   whole(1, D)],
        scratch=[ext(), full(), full(), full(), full(), ext(), pltpu.VMEM((1, BW), F32), pltpu.VMEM((1, BW), F32)])
    return out


def _transpose_into(dst_ref, src_ref, rows):
    s_len = src_ref.shape[0]
    for r0 in range(0, s_len, rows):
        dst_ref[:, r0:r0 + rows] = src_ref[r0:r0 + rows, :].astype(F32).T.astype(BF16)


def _wgrad(a, b, jobs, *, by_rows, per, name, relay_step=0):
    s_len, m = a.shape
    n = b.shape[1]
    r, cd = (m // NDEV, n) if by_rows else (m, n // NDEV)
    nsteps = NDEV // per
    at_rows = per * r if by_rows else m

    def main(i, ins, outs, scr):
        a_ref, b_ref = ins
        q_ref, acc_ref = outs
        at_scr, stage, mine, land, send_sems, recv_sems = scr
        x, y, c = _place()

        def to_sibling(pi):
            return pltpu.make_async_remote_copy(
                src_ref=stage.at[pi & 1], dst_ref=land.at[pi], send_sem=send_sems.at[pi], recv_sem=recv_sems.at[pi],
                device_id=(x, y, 1 - c), device_id_type=MESH)

        if by_rows:
            _transpose_into(at_scr, a_ref, TRANSPOSE_ROWS)
        else:
            @pl.when(i == 0)
            def _():
                _transpose_into(at_scr, a_ref, TRANSPOSE_ROWS)

        res = _dot(at_scr[...], b_ref[...]).astype(BF16)
        for k in range(per):
            blk = per * i + k
            pi, pc = blk >> 1, blk & 1
            val = res[k * r:(k + 1) * r, :] if by_rows else res

            @pl.when(pc != c)
            def _():
                @pl.when(pi >= 2)
                def _():
                    to_sibling(pi - 2).wait_send()

                stage[pi & 1] = val
                to_sibling(pi).start()

            @pl.when(pc == c)
            def _():
                mine[pi] = val

        @pl.when(i == nsteps - 1)
        def _():
            for p in range(4):
                to_sibling(p).wait_recv()
            to_sibling(2).wait_send()
            to_sibling(3).wait_send()
            _chip_sums(mine, land, q_ref, acc_ref, x, y)

    if by_rows:
        in_specs = [pl.BlockSpec((s_len, at_rows), lambda j: (0, j)), _VMEM]
    else:
        in_specs = [_VMEM, pl.BlockSpec((s_len, cd), lambda j: (0, j))]
    blk_vmem = lambda k: pltpu.VMEM((k, r, cd), BF16)
    (q, acc), job_out = _call(
        main, jobs, name=name, grid=(nsteps,), relay_step=relay_step, ins=[a, b], in_specs=in_specs,
        out_shape=[_sds((NCHIP_OTHER, r, cd), BF16), _sds((r, cd), F32)],
        out_specs=[pl.BlockSpec((NCHIP_OTHER, r, cd), lambda j: (0, 0, 0)), pl.BlockSpec((r, cd), lambda j: (0, 0))],
        scratch=[pltpu.VMEM((at_rows, s_len), BF16), blk_vmem(2), blk_vmem(4), blk_vmem(4),
                 pltpu.SemaphoreType.DMA((4,)), pltpu.SemaphoreType.DMA((4,))])
    return q, acc, job_out


def _wgrad_cols_early(a, b, jobs, *, name, relay_step=0):
    s_len, m = a.shape
    r, cd = m, b.shape[1] // NDEV
    h = r // 2

    def chip_at(pos, base):
        return base ^ (3 - pos)

    def main(i, ins, outs, scr):
        a_ref, b_ref = ins
        q_ref, acc_ref, rel_ref = outs
        at_scr, stage, mine, land, q2_scr, send_sems, recv_sems, via_send, via_recv = scr
        x, y, c = _place()
        base = 2 * x + y
        xn, yn, _ = _other_chips(x, y)
        pos, pc = i >> 1, i & 1
        pi = chip_at(pos, base)

        def to_sibling(chip, slot):
            return pltpu.make_async_remote_copy(
                src_ref=stage.at[slot], dst_ref=land.at[chip], send_sem=send_sems.at[chip],
                recv_sem=recv_sems.at[chip], device_id=(x, y, 1 - c), device_id_type=MESH)

        def via(k):
            return pltpu.make_async_remote_copy(
                src_ref=q2_scr.at[pl.ds(k * h, h)], dst_ref=rel_ref.at[k], send_sem=via_send.at[k],
                recv_sem=via_recv.at[k], device_id=(*(xn, yn)[k], c), device_id_type=MESH)

        @pl.when(i == 0)
        def _():
            _transpose_into(at_scr, a_ref, TRANSPOSE_ROWS)

        res = _dot(at_scr[...], b_ref[...]).astype(BF16)

        @pl.when(pc != c)
        def _():
            @pl.when(pos >= 2)
            def _():
                to_sibling(chip_at(pos - 2, base), pos & 1).wait_send()

            stage[pos & 1] = res
            to_sibling(pi, pos & 1).start()

        @pl.when(pc == c)
        def _():
            mine[pi] = res

        @pl.when(i == 1)
        def _():
            dg = chip_at(0, base)
            to_sibling(dg, 0).wait_recv()
            q2 = (mine[dg].astype(F32) + land[dg].astype(F32)).astype(BF16)
            q2_scr[...] = q2
            q_ref[2] = q2
            via(0).start()
            via(1).start()

        @pl.when(i == NDEV - 1)
        def _():
            for pos_ in (1, 2, 3):
                to_sibling(chip_at(pos_, base), 0).wait_recv()
            to_sibling(chip_at(2, base), 0).wait_send()
            to_sibling(chip_at(3, base), 1).wait_send()
            for k in range(2):
                via(k).wait_recv()
            for k in range(2):
                via(k).wait_send()
            for j, chip in enumerate((base ^ 2, base ^ 1)):
                q_ref[j] = (mine[chip].astype(F32) + land[chip].astype(F32)).astype(BF16)
            acc_ref[...] = mine[base].astype(F32) + land[base].astype(F32)

    def b_block(j):
        base = 2 * lax.axis_index("x") + lax.axis_index("y")
        return (0, 2 * chip_at(j >> 1, base) + (j & 1))

    blk_vmem = lambda k: pltpu.VMEM((k, r, cd), BF16)
    (q, acc, rel), job_out = _call(
        main, jobs, name=name, grid=(NDEV,), relay_step=relay_step, ins=[a, b],
        in_specs=[_VMEM, pl.BlockSpec((s_len, cd), b_block)],
        out_shape=[_sds((NCHIP_OTHER, r, cd), BF16), _sds((r, cd), F32), _sds((2, h, cd), BF16)],
        out_specs=[pl.BlockSpec((NCHIP_OTHER, r, cd), lambda j: (0, 0, 0)), pl.BlockSpec((r, cd), lambda j: (0, 0)), _HBM],
        scratch=[pltpu.VMEM((m, s_len), BF16), blk_vmem(2), blk_vmem(4), blk_vmem(4), pltpu.VMEM((r, cd), BF16),
                 pltpu.SemaphoreType.DMA((4,)), pltpu.SemaphoreType.DMA((4,)), pltpu.SemaphoreType.DMA((2,)),
                 pltpu.SemaphoreType.DMA((2,))])
    return q, acc, rel, job_out


class _ExchangeRest:
    def __init__(self, q, relayed):
        _, r, cd = q.shape
        half = (2, r // 2, cd)
        self.ins, self.in_specs = [q, relayed], [_HBM, _HBM]
        self.out_shape, self.out_specs = [_sds((2, r, cd), q.dtype)], [_HBM]
        self.scratch = [pltpu.VMEM(half, q.dtype), pltpu.VMEM(half, q.dtype), pltpu.VMEM(half, q.dtype),
                        pltpu.SemaphoreType.DMA((4,)), pltpu.SemaphoreType.DMA((4,)), pltpu.SemaphoreType.DMA((4,))]

    def ops(self, ins, outs, scr):
        (q, rel_in), (land,) = ins, outs
        own, rel, comb, send_sems, recv_sems, local_sems = scr
        h = q.shape[1] // 2
        x, y, c = _place()
        xn, yn, _ = _other_chips(x, y)
        h0, h1 = pl.ds(0, h), pl.ds(h, h)

        def remote(k, src, dst, chip):
            return pltpu.make_async_remote_copy(src_ref=src, dst_ref=dst, send_sem=send_sems.at[k],
                                                recv_sem=recv_sems.at[k], device_id=(*chip, c), device_id_type=MESH)

        def sends():
            return [remote(0, q.at[0, h0], land.at[0, h0], xn), remote(1, q.at[1, h1], land.at[1, h1], yn),
                    remote(2, comb.at[0], land.at[1, h0], yn), remote(3, comb.at[1], land.at[0, h1], xn)]

        def loads():
            return [pltpu.make_async_copy(q.at[1, h0], own.at[0], local_sems.at[0]),
                    pltpu.make_async_copy(q.at[0, h1], own.at[1], local_sems.at[1]),
                    pltpu.make_async_copy(rel_in.at[0], rel.at[0], local_sems.at[2]),
                    pltpu.make_async_copy(rel_in.at[1], rel.at[1], local_sems.at[3])]

        def start():
            cps, lds = sends(), loads()
            for ld in lds:
                ld.start()
            cps[0].start()
            cps[1].start()
            for ld in lds:
                ld.wait()
            for k in range(2):
                comb[k] = (own[k].astype(F32) + rel[k].astype(F32)).astype(comb.dtype)
            cps[2].start()
            cps[3].start()

        def finish():
            cps = sends()
            for cp in cps:
                cp.wait_recv()
            for cp in cps:
                cp.wait_send()

        return start, lambda: None, finish


def _adam_math(w, g, m, v):
    m = B1 * m + (1.0 - B1) * g
    v = B2 * v + (1.0 - B2) * (g * g)
    m_hat = m / (1.0 - B1 ** STEP)
    v_hat = v / (1.0 - B2 ** STEP)
    delta = (-LR) * (m_hat / (jnp.sqrt(v_hat) + ADAM_EPS) + WD * w)
    return delta, m, v


def _adam_big(w, acc, land, m, v, name):
    r, cd = w.shape
    rb = ADAM_ROWS if r % ADAM_ROWS == 0 else r
    nland = land.shape[0]

    def body(w_ref, acc_ref, land_ref, m_ref, v_ref, g_ref, d_ref, mo_ref, vo_ref):
        g = acc_ref[...]
        for j in range(nland):
            g = g + land_ref[j].astype(F32)
        g_ref[...] = g
        d_ref[...], mo_ref[...], vo_ref[...] = _adam_math(w_ref[...], g, m_ref[...], v_ref[...])

    blk = pl.BlockSpec((rb, cd), lambda i: (i, 0))
    blk3 = pl.BlockSpec((nland, rb, cd), lambda i: (0, i, 0))
    return pl.pallas_call(
        body, name=name, grid=(r // rb,), in_specs=[blk, blk, blk3, blk, blk], out_specs=[blk] * 4,
        out_shape=[_sds((r, cd), F32)] * 4,
        compiler_params=_params(dimension_semantics=("arbitrary",)),
    )(w, acc, land, m, v)


def _adam_small(groups):
    n = len(groups)

    def body(*refs):
        ins, outs = refs[:4 * n], refs[4 * n:]
        for k in range(n):
            w_ref, g_ref, m_ref, v_ref = ins[4 * k:4 * k + 4]
            d, mo, vo = _adam_math(w_ref[...], g_ref[...], m_ref[...], v_ref[...])
            outs[3 * k][...] = d
            outs[3 * k + 1][...] = mo
            outs[3 * k + 2][...] = vo

    flat = [a for grp in groups for a in grp]
    shapes = [_sds(grp[0].shape, F32) for grp in groups for _ in range(3)]
    res = pl.pallas_call(
        body, name="adam_small", in_specs=[_VMEM] * (4 * n), out_specs=[_VMEM] * (3 * n), out_shape=shapes,
        compiler_params=_params(),
    )(*flat)
    return [tuple(res[3 * k:3 * k + 3]) for k in range(n)]


TM_FWD_A = 256
RELAY_STEP_FWD_A = 2
FORWARD_STEP_FWD_A = 6
RELAY_STEP_FWD_B = 2
TM_BWD_A = 256
RELAY_STEP_BWD_A = 3
TM_BWD_A_IN = 256
RELAY_STEP_BWD_A_IN = 4
RELAY_STEP_WGRAD_A_IN = 2
TM_FWD_B = 256
TM_HEAD = 256
TM_BWD_B = 256


def _pack(parts, rows):
    flat = jnp.concatenate([p.reshape(-1) for p in parts])
    return jnp.pad(flat, (0, NDEV * rows * LANES - flat.shape[0])).reshape(NDEV, rows, LANES)


def _unpack(packed, shapes):
    flat, out, off = packed.reshape(-1), [], 0
    for s in shapes:
        size = 1
        for d in s:
            size *= d
        out.append(flat[off:off + size].reshape(s))
        off += size
    return out


def kernel(x, norm_w, a_w_in, a_ln_w, a_ln_b, a_w_s, a_b_s, a_w_out, b_w_in, b_conv_w, b_conv_b, b_gate_a_w, b_gate_a_b, b_gate_x_w, b_gate_x_b, b_lambda, b_w_out, norm_f_w, loss_target, m_norm_w, m_a_w_in, m_a_ln_w, m_a_ln_b, m_a_w_s, m_a_b_s, m_a_w_out, m_b_w_in, m_b_conv_w, m_b_conv_b, m_b_gate_a_w, m_b_gate_a_b, m_b_gate_x_w, m_b_gate_x_b, m_b_lambda, m_b_w_out, m_norm_f_w, v_norm_w, v_a_w_in, v_a_ln_w, v_a_ln_b, v_a_w_s, v_a_b_s, v_a_w_out, v_b_w_in, v_b_conv_w, v_b_conv_b, v_b_gate_a_w, v_b_gate_a_b, v_b_gate_x_w, v_b_gate_x_b, v_b_lambda, v_b_w_out, v_norm_f_w):
    me = 4 * lax.axis_index("x") + 2 * lax.axis_index("y") + lax.axis_index("c")
    xs, tgt = x[0], loss_target[0]
    nw0, nw1, nfw = norm_w[0:1], norm_w[1:2], norm_f_w.reshape(1, D)
    w_s, bst = a_w_s[0], a_b_s[0].T
    gcat = jnp.concatenate([b_gate_a_w[0], b_gate_x_w[0]], axis=-1).astype(BF16)

    p8_shard = jnp.concatenate([b_conv_w[0], b_conv_b, b_gate_a_b, b_gate_x_b, b_lambda], axis=0)
    (z, h0, ya, pp), ((win_a8, p8_all), (wout_a8, win_b8)) = _fwd_a(
        xs, nw0, a_ln_w, a_ln_b, w_s, bst,
        [_Gather([a_w_in[0], p8_shard], [BF16, F32]), _Gather([a_w_out[0], b_w_in[0]], [BF16, BF16])],
        tm=TM_FWD_A, relay_step=RELAY_STEP_FWD_A)
    p8 = jnp.transpose(p8_all, (1, 0, 2)).reshape(SUBLANES, BW)
    wout_a = wout_a8.reshape(AW, D)
    (x1, zb, hs, h1, yb, *saved_b), ((wout_b8,),) = _fwd_b(
        xs, ya, wout_a, nw1, win_b8, p8, gcat, [_Gather([b_w_out[0]], [BF16])],
        tm=TM_FWD_B, relay_step=RELAY_STEP_FWD_B)
    wout_b = wout_b8.reshape(BW, D)
    dx2, dx2b, loss, g_nfw = _head(x1, yb, wout_b, nfw, tgt, tm=TM_HEAD)

    dx1, dx1b, dzb, g_p8, g_ga, g_gx, g_nw1 = _bwd_b(dx2, zb, hs, x1, saved_b, nw1, win_b8, p8, gcat, wout_b,
                                                     tm=TM_BWD_B)
    q_wout_b, acc_wout_b, _ = _wgrad(yb, dx2b, [], by_rows=True, per=2, name="wgrad_b_out")
    shapes_b = [(1, D), (1, D), (SUBLANES, BW), (1, 1)]
    pack_b = _pack([g_nfw, g_nw1, g_p8, loss], 16)
    small_b = _InChip([g_ga.reshape(NDEV, -1, HD), g_gx.reshape(NDEV, -1, HD), pack_b])
    q_win_b, acc_win_b, (sm_b, (l_wout_b,)) = _wgrad(h1, dzb, [small_b, _Exchange([q_wout_b])], by_rows=False, per=1,
                                                      name="wgrad_b_in")
    qs_b, accs_b = sm_b[:3], sm_b[3:]

    (dz, g_lnw, g_lnb, g_ws, g_bst), (lands_b, (l_win_b,)) = _bwd_a(
        dx1b, z, pp, a_ln_w, a_ln_b, w_s, bst, wout_a, [_Exchange(qs_b), _ExchangeVia(q_win_b)],
        tm=TM_BWD_A, relay_step=RELAY_STEP_BWD_A)
    shapes_a = [(1, AW), (1, AW), (CH, G)]
    pack_a = _pack([g_lnw, g_lnb, g_bst], 8)
    q_wout_a, acc_wout_a, (red_b, sm_a) = _wgrad(
        ya, dx1b, [_SumGather(accs_b, lands_b), _InChip([g_ws, pack_a])], by_rows=True, per=2,
        name="wgrad_a_out", relay_step=1)
    qs_a, accs_a = sm_a[:2], sm_a[2:]
    q_win_a, acc_win_a, rel_a, (lands_a, (l_wout_a,)) = _wgrad_cols_early(
        h0, dz, [_Exchange(qs_a), _ExchangeVia(q_wout_a)], name="wgrad_a_in", relay_step=RELAY_STEP_WGRAD_A_IN)
    (gx, g_nw0), (red_a, (l_win_a,)) = _bwd_a_in(
        dz, dx1, xs, nw0, win_a8, [_SumGather(accs_a, lands_a), _ExchangeRest(q_win_a, rel_a)],
        tm=TM_BWD_A_IN, relay_step=RELAY_STEP_BWD_A_IN)

    r_ga, r_gx, r_pack_b = red_b
    r_nfw, r_nw1, r_p8, loss = _unpack(r_pack_b, shapes_b)
    r_ws, r_pack_a = red_a
    r_lnw, r_lnb, r_bst = _unpack(r_pack_a, shapes_a)
    g_p8 = lax.dynamic_slice_in_dim(r_p8, me * (BW // NDEV), BW // NDEV, axis=1)
    loss = loss[0, 0]

    weights = dict(norm_w=norm_w, a_w_in=a_w_in, a_ln_w=a_ln_w, a_ln_b=a_ln_b, a_w_s=a_w_s, a_b_s=a_b_s, a_w_out=a_w_out,
                   b_w_in=b_w_in, b_conv_w=b_conv_w, b_conv_b=b_conv_b, b_gate_a_w=b_gate_a_w, b_gate_a_b=b_gate_a_b,
                   b_gate_x_w=b_gate_x_w, b_gate_x_b=b_gate_x_b, b_lambda=b_lambda, b_w_out=b_w_out, norm_f_w=norm_f_w)
    mom1 = dict(norm_w=m_norm_w, a_w_in=m_a_w_in, a_ln_w=m_a_ln_w, a_ln_b=m_a_ln_b, a_w_s=m_a_w_s, a_b_s=m_a_b_s,
                a_w_out=m_a_w_out, b_w_in=m_b_w_in, b_conv_w=m_b_conv_w, b_conv_b=m_b_conv_b, b_gate_a_w=m_b_gate_a_w,
                b_gate_a_b=m_b_gate_a_b, b_gate_x_w=m_b_gate_x_w, b_gate_x_b=m_b_gate_x_b, b_lambda=m_b_lambda,
                b_w_out=m_b_w_out, norm_f_w=m_norm_f_w)
    mom2 = dict(norm_w=v_norm_w, a_w_in=v_a_w_in, a_ln_w=v_a_ln_w, a_ln_b=v_a_ln_b, a_w_s=v_a_w_s, a_b_s=v_a_b_s,
                a_w_out=v_a_w_out, b_w_in=v_b_w_in, b_conv_w=v_b_conv_w, b_conv_b=v_b_conv_b, b_gate_a_w=v_b_gate_a_w,
                b_gate_a_b=v_b_gate_a_b, b_gate_x_w=v_b_gate_x_w, b_gate_x_b=v_b_gate_x_b, b_lambda=v_b_lambda,
                b_w_out=v_b_w_out, norm_f_w=v_norm_f_w)
    names = list(weights)

    def as2d(a):
        return a.reshape(-1, a.shape[-1])

    upd, grads = {}, {}
    for k, acc, land in (("a_w_in", acc_win_a, l_win_a), ("a_w_out", acc_wout_a, l_wout_a),
                         ("b_w_in", acc_win_b, l_win_b), ("b_w_out", acc_wout_b, l_wout_b)):
        g, d, mo, vo = _adam_big(as2d(weights[k]), acc, land, as2d(mom1[k]), as2d(mom2[k]), "adam_" + k)
        grads[k] = g[None]
        upd[k] = (d, mo, vo)
    grads.update(
        norm_w=jnp.concatenate([g_nw0, r_nw1], axis=0), a_ln_w=r_lnw, a_ln_b=r_lnb,
        a_w_s=r_ws.reshape(1, G, CH, CH), a_b_s=r_bst.T[None],
        b_conv_w=g_p8[None, 0:4], b_conv_b=g_p8[4:5], b_gate_a_w=r_ga.reshape(1, BH, HD, HD), b_gate_a_b=g_p8[5:6],
        b_gate_x_w=r_gx.reshape(1, BH, HD, HD), b_gate_x_b=g_p8[6:7], b_lambda=g_p8[7:8], norm_f_w=r_nfw.reshape(D))
    small_names = [k for k in names if k not in upd]
    res = _adam_small([(as2d(weights[k]), as2d(grads[k]), as2d(mom1[k]), as2d(mom2[k])) for k in small_names])
    for k, r3 in zip(small_names, res):
        upd[k] = r3
    deltas = [upd[k][0].reshape(weights[k].shape) for k in names]
    new_m = [upd[k][1].reshape(weights[k].shape) for k in names]
    new_v = [upd[k][2].reshape(weights[k].shape) for k in names]
    return (loss, gx[None], *[grads[k] for k in names], *deltas, *new_m, *new_v)
```

```python
import jax
import jax.numpy as jnp
from jax import lax
from jax.experimental import pallas as pl
from jax.experimental.pallas import tpu as pltpu

F32 = jnp.float32
BF16 = jnp.bfloat16
MESH = pl.DeviceIdType.MESH

NDEV = 8
NCHIP_OTHER = 3
D = 1024
AW = 2048
G = 8
GD = AW // G
CH = 128
BW = 1536
BH = 12
HD = BW // BH
CA = 3 * AW // NDEV
CB = 2 * BW // NDEV
RMS_EPS = 1e-6
LN_EPS = 1e-5
RG_C = 8.0
LR, B1, B2, ADAM_EPS, WD, STEP = 0.001, 0.9, 0.999, 1e-08, 0.01, 10
V7X_VMEM_BYTES = 64 * 1024 * 1024
VMEM_LIMIT = V7X_VMEM_BYTES - 8 * 1024 * 1024
SUBLANES = 8
LANES = 128
BF16_ROWS = 16
TRANSPOSE_ROWS = 256
ADAM_ROWS = 512
GELU_C = 0.7978845608028654
GELU_K = 0.044715

_VMEM = pl.BlockSpec(memory_space=pltpu.VMEM)
_HBM = pl.BlockSpec(memory_space=pltpu.HBM)


def _sds(shape, dtype):
    return jax.ShapeDtypeStruct(tuple(shape), dtype)


def _params(**kw):
    return pltpu.CompilerParams(vmem_limit_bytes=VMEM_LIMIT, **kw)


def _gelu_t(z):
    p = 0.5 * jnp.tanh(z * (GELU_C + (GELU_C * GELU_K) * (z * z))) + 0.5
    return z * p, p


def _dgelu(z, p):
    return p * (1.0 + (z * (1.0 - p)) * (2.0 * GELU_C + (6.0 * GELU_C * GELU_K) * (z * z)))


def _sigmoid(v):
    return 0.5 * jnp.tanh(0.5 * v) + 0.5


def _softplus_neg(lam):
    return jnp.maximum(-lam, 0.0) + jnp.log1p(jnp.exp(-jnp.abs(lam)))


def _dot(a, b):
    return jnp.dot(a, b, preferred_element_type=F32)


def _dot_nt(a, b):
    return lax.dot_general(a, b, (((1,), (1,)), ((), ())), preferred_element_type=F32)


def _rowsum(v):
    return jnp.sum(v, axis=0, keepdims=True)


def _causal_mask():
    r = lax.broadcasted_iota(jnp.int32, (CH, CH), 0)
    c = lax.broadcasted_iota(jnp.int32, (CH, CH), 1)
    return r >= c


def _rms(x):
    return lax.rsqrt(jnp.mean(x * x, axis=-1, keepdims=True) + RMS_EPS)


def _rms_bwd(dh, x, r, nw):
    gy = dh * nw
    return r * gy - x * (r * r * r) * jnp.mean(gy * x, axis=-1, keepdims=True)


def _place():
    return lax.axis_index("x"), lax.axis_index("y"), lax.axis_index("c")


def _other_chips(x, y):
    return [(1 - x, y), (x, 1 - y), (1 - x, 1 - y)]


GATHER_SLOTS = 10


def _gather_ops(ins, outs, send_sems, recv_sems, local_sems):
    n = len(ins)
    x, y, c = _place()
    sibling = (x, y, 1 - c)
    xn, yn, dg = _other_chips(x, y)
    split = [ins[i].shape[0] % (2 * BF16_ROWS) == 0 for i in range(n)]

    def blk(chip, core):
        return 4 * chip[0] + 2 * chip[1] + core

    me = blk((x, y), c)

    def part(ref, i, half):
        if half is None:
            return ref
        h = ins[i].shape[0] // 2
        return ref.at[pl.ds(half * h, h)]

    def copy(i, k, block, to, half=None, src=None):
        dst = part(outs[i].at[block], i, half)
        return pltpu.make_async_remote_copy(
            src_ref=dst if src is None else part(src, i, half), dst_ref=dst,
            send_sem=send_sems.at[k, i], recv_sem=recv_sems.at[k, i], device_id=to, device_id_type=MESH)

    def first_copies():
        mine = [pltpu.make_async_copy(ins[i], outs[i].at[me], local_sems.at[i]) for i in range(n)]
        first = []
        for i in range(n):
            first.append(copy(i, 0, me, sibling, src=ins[i]))
            if split[i]:
                first.append(copy(i, 1, me, (*xn, c), 0, ins[i]))
                first.append(copy(i, 3, me, (*yn, c), 1, ins[i]))
                first.append(copy(i, 2, me, (*xn, c), 1, ins[i]))
                first.append(copy(i, 4, me, (*yn, c), 0, ins[i]))
            else:
                first.append(copy(i, 1, me, (*xn, c), None, ins[i]))
                first.append(copy(i, 3, me, (*yn, c), None, ins[i]))
                first.append(copy(i, 5, me, (*dg, c), None, ins[i]))
        return mine, first

    def onward():
        out = []
        for i in range(n):
            if split[i]:
                out.append(copy(i, 5, blk(xn, c), (*yn, c), 0))
                out.append(copy(i, 6, blk(yn, c), (*xn, c), 1))
        return out

    def start():
        mine, first = first_copies()
        for cp in mine + first:
            cp.start()

    def relay():
        sends = onward()
        for i in range(n):
            if split[i]:
                copy(i, 1, blk(xn, c), sibling, 0).wait_recv()
                sends.pop(0).start()
                copy(i, 3, blk(yn, c), sibling, 1).wait_recv()
                sends.pop(0).start()

    def passes():
        return [copy(i, 7 + j, blk(chip, c), sibling) for i in range(n) for j, chip in enumerate((xn, yn, dg))]

    def forward():
        fwd = passes()
        for i in range(n):
            if split[i]:
                copy(i, 2, blk(xn, c), sibling, 1).wait_recv()
                fwd[3 * i].start()
                copy(i, 4, blk(yn, c), sibling, 0).wait_recv()
                fwd[3 * i + 1].start()
                copy(i, 5, blk(dg, c), sibling, 0).wait_recv()
                copy(i, 6, blk(dg, c), sibling, 1).wait_recv()
                fwd[3 * i + 2].start()
            else:
                copy(i, 1, blk(xn, c), sibling).wait_recv()
                fwd[3 * i].start()
                copy(i, 3, blk(yn, c), sibling).wait_recv()
                fwd[3 * i + 1].start()
                copy(i, 5, blk(dg, c), sibling).wait_recv()
                fwd[3 * i + 2].start()

    def finish():
        mine, first = first_copies()
        for i in range(n):
            copy(i, 0, blk((x, y), 1 - c), sibling).wait_recv()
            for j, chip in enumerate((xn, yn, dg)):
                copy(i, 7 + j, blk(chip, 1 - c), sibling).wait_recv()
        for cp in first + passes() + onward():
            cp.wait_send()
        for cp in mine:
            cp.wait()

    return start, relay, forward, finish


def _gather_sems(n):
    return [pltpu.SemaphoreType.DMA((GATHER_SLOTS, n)), pltpu.SemaphoreType.DMA((GATHER_SLOTS, n)),
            pltpu.SemaphoreType.DMA((n,))]


class _Gather:
    def __init__(self, shards, as_dtypes=None):
        n = len(shards)
        dts = [s.dtype for s in shards] if as_dtypes is None else list(as_dtypes)
        self.cast = [jnp.dtype(d) != s.dtype for d, s in zip(dts, shards)]
        self.ins = list(shards)
        self.in_specs = [_VMEM if c else _HBM for c in self.cast]
        self.out_shape = [_sds((NDEV,) + s.shape, d) for s, d in zip(shards, dts)]
        self.out_specs = [_HBM] * n
        self.scratch = [pltpu.VMEM(s.shape, d) for s, d, c in zip(shards, dts, self.cast) if c] + _gather_sems(n)

    def ops(self, ins, outs, scr):
        ncast = sum(self.cast)
        staged = iter(scr[:ncast])
        srcs = [next(staged) if c else ref for c, ref in zip(self.cast, ins)]
        start, relay, forward, finish = _gather_ops(srcs, outs, *scr[ncast:])

        def cast_and_start():
            for c, ref, src in zip(self.cast, ins, srcs):
                if c:
                    src[...] = ref[...].astype(src.dtype)
            start()

        return cast_and_start, relay, forward, finish


class _Exchange:
    def __init__(self, qs):
        n = len(qs)
        self.ins, self.in_specs = list(qs), [_HBM] * n
        self.out_shape = [_sds(q.shape, q.dtype) for q in qs]
        self.out_specs = [_HBM] * n
        self.scratch = [pltpu.SemaphoreType.DMA((NCHIP_OTHER, n)), pltpu.SemaphoreType.DMA((NCHIP_OTHER, n))]

    def ops(self, ins, outs, scr):
        send_sems, recv_sems = scr
        n = len(ins)
        x, y, c = _place()
        chips = _other_chips(x, y)

        def copies():
            return [pltpu.make_async_remote_copy(
                src_ref=ins[i].at[j], dst_ref=outs[i].at[j], send_sem=send_sems.at[j, i],
                recv_sem=recv_sems.at[j, i], device_id=(*chips[j], c), device_id_type=MESH)
                for i in range(n) for j in range(NCHIP_OTHER)]

        def start():
            for cp in copies():
                cp.start()

        def finish():
            cps = copies()
            for cp in cps:
                cp.wait_recv()
            for cp in cps:
                cp.wait_send()

        return start, lambda: None, finish


class _ExchangeVia:
    def __init__(self, q):
        _, r, cd = q.shape
        half = (2, r // 2, cd)
        self.ins, self.in_specs = [q], [_HBM]
        self.out_shape, self.out_specs = [_sds((2, r, cd), q.dtype)], [_HBM]
        self.scratch = [pltpu.VMEM(half, q.dtype), pltpu.VMEM(half, q.dtype), pltpu.VMEM(half, q.dtype),
                        pltpu.SemaphoreType.DMA((6,)), pltpu.SemaphoreType.DMA((6,)), pltpu.SemaphoreType.DMA((2,))]

    def ops(self, ins, outs, scr):
        (q,), (land,) = ins, outs
        relayed, own, comb, send_sems, recv_sems, local_sems = scr
        h = q.shape[1] // 2
        x, y, c = _place()
        xn, yn, _ = _other_chips(x, y)
        h0, h1 = pl.ds(0, h), pl.ds(h, h)

        def remote(k, src, dst, chip):
            return pltpu.make_async_remote_copy(src_ref=src, dst_ref=dst, send_sem=send_sems.at[k],
                                                recv_sem=recv_sems.at[k], device_id=(*chip, c), device_id_type=MESH)

        def via():
            return [remote(2, q.at[2, h0], relayed.at[0], xn), remote(3, q.at[2, h1], relayed.at[1], yn)]

        def direct():
            return [remote(0, q.at[0, h0], land.at[0, h0], xn), remote(1, q.at[1, h1], land.at[1, h1], yn)]

        def second():
            return [remote(4, comb.at[0], land.at[1, h0], yn), remote(5, comb.at[1], land.at[0, h1], xn)]

        def mine():
            return [pltpu.make_async_copy(q.at[1, h0], own.at[0], local_sems.at[0]),
                    pltpu.make_async_copy(q.at[0, h1], own.at[1], local_sems.at[1])]

        def start():
            for cp in via() + direct() + mine():
                cp.start()

        def relay():
            arrived, loaded, onward = via(), mine(), second()
            for k in range(2):
                arrived[k].wait_recv()
                loaded[k].wait()
                comb[k] = (own[k].astype(F32) + relayed[k].astype(F32)).astype(comb.dtype)
                onward[k].start()

        def finish():
            landing = direct() + second()
            for cp in landing:
                cp.wait_recv()
            for cp in via() + landing:
                cp.wait_send()

        return start, relay, finish


class _SumGather:
    def __init__(self, accs, lands):
        n = len(accs)
        self.n = n
        self.ins, self.in_specs = list(accs) + list(lands), [_VMEM] * (2 * n)
        self.out_shape = [_sds((NDEV,) + a.shape, a.dtype) for a in accs]
        self.out_specs = [_HBM] * n
        self.scratch = [pltpu.VMEM(a.shape, a.dtype) for a in accs] + _gather_sems(n)

    def ops(self, ins, outs, scr):
        n = self.n
        accs, lands, mine = ins[:n], ins[n:], scr[:n]
        g_start, relay, forward, finish = _gather_ops(mine, outs, *scr[n:])

        def start():
            for i in range(n):
                mine[i][...] = accs[i][...] + lands[i][0] + lands[i][1] + lands[i][2]
            g_start()

        return start, relay, forward, finish


def _call(main, jobs, *, name, grid, ins, in_specs, out_shape, out_specs, scratch, relay_step=0, first=0,
          prologue=None, forward_step=None):
    (nsteps,) = grid
    n_in, n_out, n_scr = len(ins), len(out_shape), len(scratch)

    def body(*refs):
        pos = [0]

        def take(k):
            r = refs[pos[0]:pos[0] + k]
            pos[0] += k
            return r

        m_in = take(n_in)
        j_in = [take(len(j.ins)) for j in jobs]
        m_out = take(n_out)
        j_out = [take(len(j.out_shape)) for j in jobs]
        m_scr = take(n_scr)
        j_scr = [take(len(j.scratch)) for j in jobs]
        ops = [_four(j.ops(a, b, s)) for j, a, b, s in zip(jobs, j_in, j_out, j_scr)]
        i = pl.program_id(0)

        if ops:
            @pl.when(i == 0)
            def _():
                for o in ops[:first]:
                    o[0]()
                for o in ops[:first]:
                    o[1]()
                for o in ops[first:]:
                    o[0]()
                for o in ops[:first]:
                    o[2]()
                for o in ops[:first]:
                    o[3]()
                if prologue is not None:
                    prologue(j_out[:first], m_scr)

        main(i, m_in, m_out, m_scr)

        forward_at = max(relay_step, nsteps - 2) if forward_step is None else min(forward_step, nsteps - 1)
        for stage, at in ((1, min(relay_step, nsteps - 1)), (2, forward_at), (3, nsteps - 1)):
            if ops[first:]:
                @pl.when(i == at)
                def _():
                    for o in ops[first:]:
                        o[stage]()

    res = pl.pallas_call(
        body, name=name, grid=grid,
        in_specs=list(in_specs) + [s for j in jobs for s in j.in_specs],
        out_specs=list(out_specs) + [s for j in jobs for s in j.out_specs],
        out_shape=list(out_shape) + [s for j in jobs for s in j.out_shape],
        scratch_shapes=list(scratch) + [s for j in jobs for s in j.scratch],
        compiler_params=_params(dimension_semantics=("arbitrary",)),
    )(*ins, *[a for j in jobs for a in j.ins])
    main_out, rest, job_out = res[:n_out], res[n_out:], []
    for j in jobs:
        k = len(j.out_shape)
        job_out.append(rest[:k])
        rest = rest[k:]
    return main_out, job_out


def _four(ops):
    return ops if len(ops) == 4 else (ops[0], ops[1], lambda: None, ops[2])


class _InChip:
    def __init__(self, ps):
        n = len(ps)
        self.n = n
        blk = [p.shape[1:] for p in ps]
        self.ins, self.in_specs = list(ps), [_HBM] * n
        self.out_shape = [_sds((NCHIP_OTHER,) + b, p.dtype) for b, p in zip(blk, ps)] + [_sds(b, F32) for b in blk]
        self.out_specs = [_VMEM] * (2 * n)
        self.scratch = ([pltpu.VMEM((4,) + b, p.dtype) for b, p in zip(blk, ps)] * 2
                        + [pltpu.SemaphoreType.DMA((4, n))] * 3)

    def ops(self, ins, outs, scr):
        n = self.n
        q_refs, acc_refs = outs[:n], outs[n:]
        mines, lands = scr[:n], scr[n:2 * n]
        send_sems, recv_sems, local_sems = scr[2 * n:]
        x, y, c = _place()
        sibling = (x, y, 1 - c)

        def copies():
            out = []
            for i in range(n):
                for pi in range(4):
                    loc = pltpu.make_async_copy(ins[i].at[2 * pi + c], mines[i].at[pi], local_sems.at[pi, i])
                    cp = pltpu.make_async_remote_copy(
                        src_ref=ins[i].at[2 * pi + (1 - c)], dst_ref=lands[i].at[pi],
                        send_sem=send_sems.at[pi, i], recv_sem=recv_sems.at[pi, i],
                        device_id=sibling, device_id_type=MESH)
                    out.append((loc, cp))
            return out

        def start():
            for loc, cp in copies():
                loc.start()
                cp.start()

        def finish():
            pairs = copies()
            for loc, cp in pairs:
                loc.wait()
                cp.wait_recv()
            for i in range(n):
                _chip_sums(mines[i], lands[i], q_refs[i], acc_refs[i], x, y)
            for _, cp in pairs:
                cp.wait_send()

        return start, lambda: None, finish


def _chip_sums(mine, land, q_ref, acc_ref, x, y):
    for j, (qx, qy) in enumerate(_other_chips(x, y)):
        qi = 2 * qx + qy
        q_ref[j] = (mine[qi].astype(F32) + land[qi].astype(F32)).astype(q_ref.dtype)
    mi = 2 * x + y
    acc_ref[...] = mine[mi].astype(F32) + land[mi].astype(F32)


def _direct_sum(v, buf, send_sems, recv_sems):
    x, y, c = _place()
    me = 4 * x + 2 * y + c
    buf[me] = v
    cps = []
    for k in range(1, NDEV):
        fx, fy, fc = (k >> 2) & 1, (k >> 1) & 1, k & 1
        peer = ((1 - x) if fx else x, (1 - y) if fy else y, (1 - c) if fc else c)
        cps.append((peer, pltpu.make_async_remote_copy(
            src_ref=buf.at[me], dst_ref=buf.at[me], send_sem=send_sems.at[k - 1], recv_sem=recv_sems.at[k - 1],
            device_id=peer, device_id_type=MESH)))
    for _, cp in cps:
        cp.start()
    for k, (peer, _) in enumerate(cps):
        theirs = 4 * peer[0] + 2 * peer[1] + peer[2]
        pltpu.make_async_remote_copy(
            src_ref=buf.at[theirs], dst_ref=buf.at[theirs], send_sem=send_sems.at[k], recv_sem=recv_sems.at[k],
            device_id=peer, device_id_type=MESH).wait_recv()
    acc = buf[0]
    for j in range(1, NDEV):
        acc = acc + buf[j]
    for _, cp in cps:
        cp.wait_send()
    return acc


def _direct_sum_scratch(shape, dtype):
    return [pltpu.VMEM((NDEV,) + tuple(shape), dtype), pltpu.SemaphoreType.DMA((NDEV - 1,)),
            pltpu.SemaphoreType.DMA((NDEV - 1,))]


def _fwd_a(x, nw, lnw, lnb, ws, bst, jobs, *, tm, relay_step):
    s_len = x.shape[0]
    nt = s_len // tm
    nch = tm // CH

    def main(i, ins, outs, scr):
        x_ref, nw_ref, lnw_ref, lnb_ref, ws_ref, bst_ref = ins
        z_ref, h_ref, y_ref, pp_ref = outs
        wc_scr, gv_scr, win_ref = scr

        @pl.when(i == 0)
        def _():
            m = _causal_mask()
            for g in range(G):
                wc_scr[g] = jnp.where(m, ws_ref[g], 0.0).astype(BF16)

        x = x_ref[...]
        h = (x * _rms(x) * nw_ref[...]).astype(BF16)
        h_ref[...] = h
        for k in range(NDEV):
            z_ref[:, k * CA:(k + 1) * CA] = _dot(h, win_ref[k])

        ssum = jnp.zeros((tm, 1), F32)
        for g in range(G):
            vs = slice(AW + g * GD, AW + (g + 1) * GD)
            gv, pv = _gelu_t(z_ref[:, vs])
            pp_ref[:, vs] = pv.astype(BF16)
            gv_scr[:, g * GD:(g + 1) * GD] = gv
            ssum = ssum + jnp.sum(gv, axis=-1, keepdims=True)
        mu = ssum * (1.0 / AW)
        vsum = jnp.zeros((tm, 1), F32)
        for g in range(G):
            dlt = gv_scr[:, g * GD:(g + 1) * GD] - mu
            vsum = vsum + jnp.sum(dlt * dlt, axis=-1, keepdims=True)
        rstd = lax.rsqrt(vsum * (1.0 / AW) + LN_EPS)

        for g in range(G):
            cs = slice(g * GD, (g + 1) * GD)
            gs = slice(2 * AW + g * GD, 2 * AW + (g + 1) * GD)
            v = (gv_scr[:, cs] - mu) * rstd * lnw_ref[:, cs] + lnb_ref[:, cs]
            vb = v.astype(BF16)
            u, pu = _gelu_t(z_ref[:, cs])
            pp_ref[:, cs] = pu.astype(BF16)
            zg = z_ref[:, gs]
            sig = _sigmoid(zg)
            pp_ref[:, gs] = sig.astype(BF16)
            sg = zg * sig
            for n in range(nch):
                rs = slice(n * CH, (n + 1) * CH)
                s = _dot(wc_scr[g], vb[rs, :]) + bst_ref[:, g:g + 1]
                y_ref[rs, cs] = (u[rs, :] * s * sg[rs, :]).astype(BF16)

    tile = lambda w: pl.BlockSpec((tm, w), lambda i: (i, 0))
    return _call(
        main, jobs, name="fwd_a", grid=(nt,), relay_step=relay_step, first=1, forward_step=FORWARD_STEP_FWD_A,
        prologue=lambda gathered, scr: pltpu.sync_copy(gathered[0][0], scr[2]),
        ins=[x, nw, lnw, lnb, ws, bst], in_specs=[tile(D), _VMEM, _VMEM, _VMEM, _VMEM, _VMEM],
        out_shape=[_sds((s_len, 3 * AW), F32), _sds((s_len, D), BF16), _sds((s_len, AW), BF16),
                   _sds((s_len, 3 * AW), BF16)],
        out_specs=[tile(3 * AW), tile(D), tile(AW), tile(3 * AW)],
        scratch=[pltpu.VMEM((G, CH, CH), BF16), pltpu.VMEM((tm, AW), F32), pltpu.VMEM((NDEV, D, CA), BF16)])


def _bwd_a(dx1, z, pp, lnw, lnb, ws, bst, wout, jobs, *, tm, relay_step):
    s_len = dx1.shape[0]
    nt = s_len // tm
    nch = tm // CH

    def main(i, ins, outs, scr):
        dx1_ref, z_ref, pp_ref, lnw_ref, lnb_ref, ws_ref, bst_ref, wout_ref = ins
        dz_ref, glnw_ref, glnb_ref, gws_ref, gbst_ref = outs
        wc_scr, wct_scr, vh_scr, dgv_scr, dy_scr, dv_scr, gbs_acc, gwc_acc = scr

        @pl.when(i == 0)
        def _():
            m = _causal_mask()
            for g in range(G):
                wm = jnp.where(m, ws_ref[g], 0.0)
                wc_scr[g] = wm.astype(BF16)
                wct_scr[g] = wm.T.astype(BF16)
            glnw_ref[...] = jnp.zeros_like(glnw_ref)
            glnb_ref[...] = jnp.zeros_like(glnb_ref)
            gbs_acc[...] = jnp.zeros_like(gbs_acc)
            gwc_acc[...] = jnp.zeros_like(gwc_acc)

        dy_scr[...] = _dot_nt(dx1_ref[...], wout_ref[...])

        ssum = jnp.zeros((tm, 1), F32)
        for g in range(G):
            cs = slice(g * GD, (g + 1) * GD)
            vs = slice(AW + g * GD, AW + (g + 1) * GD)
            zv = z_ref[:, vs]
            pv = pp_ref[:, vs].astype(F32)
            gv = zv * pv
            vh_scr[:, cs] = gv
            dgv_scr[:, cs] = _dgelu(zv, pv)
            ssum = ssum + jnp.sum(gv, axis=-1, keepdims=True)
        mu = ssum * (1.0 / AW)
        vsum = jnp.zeros((tm, 1), F32)
        for g in range(G):
            dlt = vh_scr[:, g * GD:(g + 1) * GD] - mu
            vsum = vsum + jnp.sum(dlt * dlt, axis=-1, keepdims=True)
        rstd = lax.rsqrt(vsum * (1.0 / AW) + LN_EPS)

        m1 = jnp.zeros((tm, 1), F32)
        m2 = jnp.zeros((tm, 1), F32)
        for g in range(G):
            cs = slice(g * GD, (g + 1) * GD)
            gs = slice(2 * AW + g * GD, 2 * AW + (g + 1) * GD)
            vhat = (vh_scr[:, cs] - mu) * rstd
            vh_scr[:, cs] = vhat
            vb = (vhat * lnw_ref[:, cs] + lnb_ref[:, cs]).astype(BF16)
            zu = z_ref[:, cs]
            tu = pp_ref[:, cs].astype(F32)
            u = zu * tu
            zg = z_ref[:, gs]
            sig = pp_ref[:, gs].astype(F32)
            sg = zg * sig
            dy = dy_scr[:, cs]
            dsf = dy * u * sg
            dsb = dsf.astype(BF16)
            dvs = []
            for n in range(nch):
                rs = slice(n * CH, (n + 1) * CH)
                s = _dot(wc_scr[g], vb[rs, :]) + bst_ref[:, g:g + 1]
                dys = dy[rs, :] * s
                dz_ref[rs, cs] = (dys * sg[rs, :] * _dgelu(zu[rs, :], tu[rs, :])).astype(BF16)
                dz_ref[rs, gs] = (dys * u[rs, :] * (sig[rs, :] * (1.0 + zg[rs, :] * (1.0 - sig[rs, :])))).astype(BF16)
                gbs_acc[g] += dsf[rs, :]
                gwc_acc[g] += _dot_nt(dsb[rs, :], vb[rs, :])
                dvs.append(_dot(wct_scr[g], dsb[rs, :]))
            dv = jnp.concatenate(dvs, axis=0) if nch > 1 else dvs[0]
            glnw_ref[:, cs] += _rowsum(dv * vhat)
            glnb_ref[:, cs] += _rowsum(dv)
            dvh = dv * lnw_ref[:, cs]
            dv_scr[:, cs] = dvh
            m1 = m1 + jnp.sum(dvh, axis=-1, keepdims=True)
            m2 = m2 + jnp.sum(dvh * vhat, axis=-1, keepdims=True)
        m1 = m1 * (1.0 / AW)
        m2 = m2 * (1.0 / AW)
        for g in range(G):
            cs = slice(g * GD, (g + 1) * GD)
            dgv = rstd * (dv_scr[:, cs] - m1 - vh_scr[:, cs] * m2)
            dz_ref[:, AW + g * GD:AW + (g + 1) * GD] = (dgv * dgv_scr[:, cs]).astype(BF16)

        @pl.when(i == nt - 1)
        def _():
            m = _causal_mask()
            for g in range(G):
                gws_ref[g] = jnp.where(m, gwc_acc[g], 0.0)
                gbst_ref[:, g:g + 1] = jnp.sum(gbs_acc[g], axis=-1, keepdims=True)

    tile = lambda w: pl.BlockSpec((tm, w), lambda i: (i, 0))
    whole = lambda *s: pl.BlockSpec(s, lambda i: (0,) * len(s))
    big = lambda dt: pltpu.VMEM((tm, AW), dt)
    return _call(
        main, jobs, name="bwd_a", grid=(nt,), relay_step=relay_step,
        ins=[dx1, z, pp, lnw, lnb, ws, bst, wout],
        in_specs=[tile(D), tile(3 * AW), tile(3 * AW), _VMEM, _VMEM, _VMEM, _VMEM, _VMEM],
        out_shape=[_sds((s_len, 3 * AW), BF16), _sds((1, AW), F32), _sds((1, AW), F32), _sds((G, CH, CH), F32),
                   _sds((CH, G), F32)],
        out_specs=[tile(3 * AW), whole(1, AW), whole(1, AW), whole(G, CH, CH), whole(CH, G)],
        scratch=[pltpu.VMEM((G, CH, CH), BF16), pltpu.VMEM((G, CH, CH), BF16), big(F32), big(F32), big(F32), big(F32),
                 pltpu.VMEM((G, CH, GD), F32), pltpu.VMEM((G, CH, CH), F32)])


def _bwd_a_in(dz, dx1, x, nw, win8, jobs, *, tm, relay_step):
    s_len = x.shape[0]
    nt = s_len // tm

    def main(i, ins, outs, scr):
        dz_ref, dx1_ref, x_ref, nw_ref, win_ref = ins
        gx_ref, gnw_ref = outs

        @pl.when(i == 0)
        def _():
            gnw_ref[...] = jnp.zeros_like(gnw_ref)

        dh = jnp.zeros((tm, D), F32)
        for k in range(NDEV):
            dh = dh + _dot_nt(dz_ref[:, k * CA:(k + 1) * CA], win_ref[k])
        x = x_ref[...]
        r = _rms(x)
        gx_ref[...] = dx1_ref[...] + _rms_bwd(dh, x, r, nw_ref[...])
        gnw_ref[...] += _rowsum(dh * x * r)

        @pl.when(i == nt - 1)
        def _():
            gnw_ref[...] = _direct_sum(gnw_ref[...], *scr)

    tile = lambda w: pl.BlockSpec((tm, w), lambda i: (i, 0))
    return _call(
        main, jobs, name="bwd_a_in", grid=(nt,), relay_step=relay_step, forward_step=nt - 1,
        ins=[dz, dx1, x, nw, win8], in_specs=[tile(3 * AW), tile(D), tile(D), _VMEM, _VMEM],
        out_shape=[_sds((s_len, D), F32), _sds((1, D), F32)],
        out_specs=[tile(D), pl.BlockSpec((1, D), lambda i: (0, 0))], scratch=_direct_sum_scratch((1, D), F32))


def _conv(p8_ref, cs, xb, xm1, xm2, xm3):
    xc = p8_ref[4:5, cs] + p8_ref[3:4, cs] * xb
    xc = xc + p8_ref[0:1, cs] * xm3
    xc = xc + p8_ref[1:2, cs] * xm2
    return xc + p8_ref[2:3, cs] * xm1


def _gates(p8_ref, gcat_ref, hh, xc):
    cs = slice(hh * HD, (hh + 1) * HD)
    pre = _dot(xc.astype(BF16), gcat_ref[hh])
    r = _sigmoid(pre[:, :HD] + p8_ref[5:6, cs])
    ig = _sigmoid(pre[:, HD:] + p8_ref[6:7, cs])
    sp = _softplus_neg(p8_ref[7:8, cs])
    la = (-RG_C) * r * sp
    a = jnp.exp(la)
    half_log = 0.5 * jnp.log(jnp.tanh(-la) * (1.0 + a * a))
    return r, ig, sp, a, jnp.exp(half_log), jnp.exp(-half_log)


def _scan_rows(a_ref, b_ref, out_ref, carry, tm, reverse):
    row = lax.broadcasted_iota(jnp.int32, (SUBLANES, BW), 0)
    ngrp = tm // SUBLANES

    def step(j, cr):
        jj = (ngrp - 1 - j) if reverse else j
        off = pl.multiple_of(jj * SUBLANES, SUBLANES)
        a = a_ref[pl.ds(off, SUBLANES), :]
        b = b_ref[pl.ds(off, SUBLANES), :]
        for sh in (1, 2, 4):
            if reverse:
                a_s = pltpu.roll(a, SUBLANES - sh, 0)
                b_s = pltpu.roll(b, SUBLANES - sh, 0)
                m = row < SUBLANES - sh
            else:
                a_s = pltpu.roll(a, sh, 0)
                b_s = pltpu.roll(b, sh, 0)
                m = row >= sh
            b = jnp.where(m, a * b_s + b, b)
            a = jnp.where(m, a * a_s, a)
        o = b + a * cr
        out_ref[pl.ds(off, SUBLANES), :] = o
        return o[0:1, :] if reverse else o[SUBLANES - 1:SUBLANES, :]

    return lax.fori_loop(0, ngrp, step, carry)


def _fwd_b(x, ya, wout_a, nw, win8, p8, gcat, jobs, *, tm, relay_step):
    s_len = x.shape[0]
    nt = s_len // tm

    def main(i, ins, outs, scr):
        x_ref, ya_ref, wouta_ref, nw_ref, win_ref, p8_ref, gcat_ref = ins
        x1_ref, zb_ref, hs_ref, h1_ref, yb_ref, xc_ref, a_ref, cc_ref, r_ref, ig_ref, m_ref = outs
        xbe_scr, b_scr, k_scr, carry_scr = scr

        @pl.when(i == 0)
        def _():
            xbe_scr[0:SUBLANES, :] = jnp.zeros((SUBLANES, BW), F32)
            carry_scr[...] = jnp.zeros_like(carry_scr)

        x1 = x_ref[...] + _dot(ya_ref[...], wouta_ref[...])
        x1_ref[...] = x1
        h = (x1 * _rms(x1) * nw_ref[...]).astype(BF16)
        h1_ref[...] = h
        for k in range(NDEV):
            zb_ref[:, k * CB:(k + 1) * CB] = _dot(h, win_ref[k])
        xbe_scr[SUBLANES:SUBLANES + tm, :] = zb_ref[:, :BW]
        for hh in range(BH):
            cs = slice(hh * HD, (hh + 1) * HD)
            xc = _conv(p8_ref, cs, xbe_scr[SUBLANES:SUBLANES + tm, cs], xbe_scr[7:7 + tm, cs],
                       xbe_scr[6:6 + tm, cs], xbe_scr[5:5 + tm, cs])
            r, ig, _, a, mult, rm = _gates(p8_ref, gcat_ref, hh, xc)
            ixc = ig * xc
            xc_ref[:, cs] = xc
            a_ref[:, cs] = a
            r_ref[:, cs] = r.astype(BF16)
            ig_ref[:, cs] = ig.astype(BF16)
            m_ref[:, cs] = mult.astype(BF16)
            b_scr[:, cs] = mult * ixc
            k_scr[:, cs] = ixc * (a * a * rm)
        xbe_scr[0:SUBLANES, :] = xbe_scr[tm:tm + SUBLANES, :]
        carry_scr[...] = _scan_rows(a_ref, b_scr, hs_ref, carry_scr[...], tm, False)
        for hh in range(BH):
            cs = slice(hh * HD, (hh + 1) * HD)
            gt = zb_ref[:, BW + hh * HD:BW + (hh + 1) * HD]
            hsv = hs_ref[:, cs]
            yb_ref[:, cs] = (hsv * (gt * _sigmoid(gt))).astype(BF16)
            cc_ref[:, cs] = (hsv - b_scr[:, cs]) - k_scr[:, cs]

    tile = lambda w: pl.BlockSpec((tm, w), lambda i: (i, 0))
    wide = lambda dt: _sds((s_len, BW), dt)
    return _call(
        main, jobs, name="fwd_b", grid=(nt,), relay_step=relay_step,
        ins=[x, ya, wout_a, nw, win8, p8, gcat], in_specs=[tile(D), tile(AW), _VMEM, _VMEM, _VMEM, _VMEM, _VMEM],
        out_shape=[_sds((s_len, D), F32), _sds((s_len, 2 * BW), F32), wide(F32), _sds((s_len, D), BF16), wide(BF16),
                   wide(F32), wide(F32), wide(F32), wide(BF16), wide(BF16), wide(BF16)],
        out_specs=[tile(D), tile(2 * BW), tile(BW), tile(D)] + [tile(BW)] * 7,
        scratch=[pltpu.VMEM((tm + SUBLANES, BW), F32), pltpu.VMEM((tm, BW), F32), pltpu.VMEM((tm, BW), F32),
                 pltpu.VMEM((1, BW), F32)])


def _head(x1, yb, wout, nfw, tgt, *, tm):
    s_len = x1.shape[0]

    def main(i, ins, outs, scr):
        x1_ref, yb_ref, wout_ref, nfw_ref, t_ref = ins
        dx2_ref, dx2b_ref, loss_ref, gnfw_ref = outs

        @pl.when(i == 0)
        def _():
            loss_ref[...] = jnp.zeros_like(loss_ref)
            gnfw_ref[...] = jnp.zeros_like(gnfw_ref)

        x2 = x1_ref[...] + _dot(yb_ref[...], wout_ref[...])
        rf = _rms(x2)
        xn = x2 * rf
        e = xn * nfw_ref[...] - t_ref[...]
        loss_ref[...] += (0.5 / D) * jnp.sum(jnp.sum(e * e, axis=-1, keepdims=True), axis=0, keepdims=True)
        dyf = e * (1.0 / D)
        gnfw_ref[...] += _rowsum(dyf * xn)
        dx2 = _rms_bwd(dyf, x2, rf, nfw_ref[...])
        dx2_ref[...] = dx2
        dx2b_ref[...] = dx2.astype(BF16)

    tile = lambda w: pl.BlockSpec((tm, w), lambda i: (i, 0))
    whole = lambda *s: pl.BlockSpec(s, lambda i: (0,) * len(s))
    (dx2, dx2b, loss, gnfw), _ = _call(
        main, [], name="head", grid=(s_len // tm,),
        ins=[x1, yb, wout, nfw, tgt], in_specs=[tile(D), tile(BW), _VMEM, _VMEM, tile(D)],
        out_shape=[_sds((s_len, D), F32), _sds((s_len, D), BF16), _sds((1, 1), F32), _sds((1, D), F32)],
        out_specs=[tile(D), tile(D), whole(1, 1), whole(1, D)], scratch=[])
    return dx2, dx2b, loss, gnfw


def _bwd_b(dx2, zb, hs, x1, saved, nw, win8, p8, gcat, wout, *, tm):
    s_len = x1.shape[0]
    nt = s_len // tm

    def main(i, ins, outs, scr):
        (dx2_ref, zb_ref, hs_ref, x1_ref, xc_ref, a_ref, cc_ref, r_ref, ig_ref, m_ref,
         nw_ref, win_ref, p8_ref, gcat_ref, wout_ref) = ins
        dx1_ref, dx1b_ref, dzb_ref, gp8_ref, gga_ref, ggx_ref, gnw_ref = outs
        ae_scr, an_scr, dhd_scr, dh_scr, dy_scr, dxce_scr, carry_scr, afirst_scr = scr

        @pl.when(i == 0)
        def _():
            gp8_ref[...] = jnp.zeros_like(gp8_ref)
            gga_ref[...] = jnp.zeros_like(gga_ref)
            ggx_ref[...] = jnp.zeros_like(ggx_ref)
            gnw_ref[...] = jnp.zeros_like(gnw_ref)
            dxce_scr[tm:tm + SUBLANES, :] = jnp.zeros((SUBLANES, BW), F32)
            carry_scr[...] = jnp.zeros_like(carry_scr)
            afirst_scr[...] = jnp.zeros_like(afirst_scr)

        dx2 = dx2_ref[...]
        dy_scr[...] = _dot_nt(dx2.astype(BF16), wout_ref[...])
        for hh in range(BH):
            cs = slice(hh * HD, (hh + 1) * HD)
            gs = slice(BW + hh * HD, BW + (hh + 1) * HD)
            gt = zb_ref[:, gs]
            sig = _sigmoid(gt)
            dy = dy_scr[:, cs]
            dhd_scr[:, cs] = dy * (gt * sig)
            dzb_ref[:, gs] = (dy * hs_ref[:, cs] * (sig * (1.0 + gt * (1.0 - sig)))).astype(BF16)

        ae_scr[0:tm, :] = a_ref[...]
        ae_scr[tm:tm + SUBLANES, :] = jnp.broadcast_to(afirst_scr[...], (SUBLANES, BW))
        an_scr[...] = ae_scr[1:1 + tm, :]
        afirst_scr[...] = ae_scr[0:1, :]
        carry_scr[...] = _scan_rows(an_scr, dhd_scr, dh_scr, carry_scr[...], tm, True)

        for hh in range(BH):
            cs = slice(hh * HD, (hh + 1) * HD)
            dh = dh_scr[:, cs]
            mult = m_ref[:, cs].astype(F32)
            ig = ig_ref[:, cs].astype(F32)
            r = r_ref[:, cs].astype(F32)
            xc = xc_ref[:, cs]
            lam = p8_ref[7:8, cs]
            sp = _softplus_neg(lam)
            dla = dh * cc_ref[:, cs]
            gp8_ref[7:8, cs] += _rowsum(dla * ((-RG_C) * r)) * (-_sigmoid(-lam))
            dpr = dla * ((-RG_C) * sp) * (r * (1.0 - r))
            dpi = dh * mult * xc * (ig * (1.0 - ig))
            gp8_ref[5:6, cs] += _rowsum(dpr)
            gp8_ref[6:7, cs] += _rowsum(dpi)
            dcat = jnp.concatenate([dpr, dpi], axis=1).astype(BF16)
            dxc = dh * mult * ig + _dot_nt(dcat, gcat_ref[hh])
            gg = _dot(xc.T.astype(BF16), dcat)
            gga_ref[hh] += gg[:, :HD]
            ggx_ref[hh] += gg[:, HD:]
            dxce_scr[0:tm, cs] = dxc
            gp8_ref[4:5, cs] += _rowsum(dxc)
        for hh in range(BH):
            cs = slice(hh * HD, (hh + 1) * HD)
            xb = zb_ref[:, cs]
            d0, d1 = dxce_scr[0:tm, cs], dxce_scr[1:1 + tm, cs]
            d2, d3 = dxce_scr[2:2 + tm, cs], dxce_scr[3:3 + tm, cs]
            dzb_ref[:, cs] = (p8_ref[3:4, cs] * d0 + p8_ref[2:3, cs] * d1 + p8_ref[1:2, cs] * d2
                              + p8_ref[0:1, cs] * d3).astype(BF16)
            gp8_ref[3:4, cs] += _rowsum(d0 * xb)
            gp8_ref[2:3, cs] += _rowsum(d1 * xb)
            gp8_ref[1:2, cs] += _rowsum(d2 * xb)
            gp8_ref[0:1, cs] += _rowsum(d3 * xb)
        dxce_scr[tm:tm + SUBLANES, :] = dxce_scr[0:SUBLANES, :]

        dh1 = jnp.zeros((tm, D), F32)
        for k in range(NDEV):
            dh1 = dh1 + _dot_nt(dzb_ref[:, k * CB:(k + 1) * CB], win_ref[k])
        x1 = x1_ref[...]
        r1 = _rms(x1)
        dx1 = dx2 + _rms_bwd(dh1, x1, r1, nw_ref[...])
        dx1_ref[...] = dx1
        dx1b_ref[...] = dx1.astype(BF16)
        gnw_ref[...] += _rowsum(dh1 * x1 * r1)

    tile = lambda w: pl.BlockSpec((tm, w), lambda i: (nt - 1 - i, 0))
    whole = lambda *s: pl.BlockSpec(s, lambda i: (0,) * len(s))
    full = lambda: pltpu.VMEM((tm, BW), F32)
    ext = lambda: pltpu.VMEM((tm + SUBLANES, BW), F32)
    out, _ = _call(
        main, [], name="bwd_b", grid=(nt,),
        ins=[dx2, zb, hs, x1, *saved, nw, win8, p8, gcat, wout],
        in_specs=[tile(D), tile(2 * BW), tile(BW), tile(D)] + [tile(BW)] * 6 + [_VMEM] * 5,
        out_shape=[_sds((s_len, D), F32), _sds((s_len, D), BF16), _sds((s_len, 2 * BW), BF16), _sds((SUBLANES, BW), F32),
                   _sds((BH, HD, HD), F32), _sds((BH, HD, HD), F32), _sds((1, D), F32)],
        out_specs=[tile(D), tile(D), tile(2 * BW), whole(SUBLANES, BW), whole(BH, HD, HD), whole(BH, HD, HD),
                   whole(1, D)],
        scratch=[ext(), full(), full(), full(), full(), ext(), pltpu.VMEM((1, BW), F32), pltpu.VMEM((1, BW), F32)])
    return out


def _transpose_into(dst_ref, src_ref, rows):
    s_len = src_ref.shape[0]
    for r0 in range(0, s_len, rows):
        dst_ref[:, r0:r0 + rows] = src_ref[r0:r0 + rows, :].astype(F32).T.astype(BF16)


def _wgrad(a, b, jobs, *, by_rows, per, name, relay_step=0):
    s_len, m = a.shape
    n = b.shape[1]
    r, cd = (m // NDEV, n) if by_rows else (m, n // NDEV)
    nsteps = NDEV // per
    at_rows = per * r if by_rows else m

    def main(i, ins, outs, scr):
        a_ref, b_ref = ins
        q_ref, acc_ref = outs
        at_scr, stage, mine, land, send_sems, recv_sems = scr
        x, y, c = _place()

        def to_sibling(pi):
            return pltpu.make_async_remote_copy(
                src_ref=stage.at[pi & 1], dst_ref=land.at[pi], send_sem=send_sems.at[pi], recv_sem=recv_sems.at[pi],
                device_id=(x, y, 1 - c), device_id_type=MESH)

        if by_rows:
            _transpose_into(at_scr, a_ref, TRANSPOSE_ROWS)
        else:
            @pl.when(i == 0)
            def _():
                _transpose_into(at_scr, a_ref, TRANSPOSE_ROWS)

        res = _dot(at_scr[...], b_ref[...]).astype(BF16)
        for k in range(per):
            blk = per * i + k
            pi, pc = blk >> 1, blk & 1
            val = res[k * r:(k + 1) * r, :] if by_rows else res

            @pl.when(pc != c)
            def _():
                @pl.when(pi >= 2)
                def _():
                    to_sibling(pi - 2).wait_send()

                stage[pi & 1] = val
                to_sibling(pi).start()

            @pl.when(pc == c)
            def _():
                mine[pi] = val

        @pl.when(i == nsteps - 1)
        def _():
            for p in range(4):
                to_sibling(p).wait_recv()
            to_sibling(2).wait_send()
            to_sibling(3).wait_send()
            _chip_sums(mine, land, q_ref, acc_ref, x, y)

    if by_rows:
        in_specs = [pl.BlockSpec((s_len, at_rows), lambda j: (0, j)), _VMEM]
    else:
        in_specs = [_VMEM, pl.BlockSpec((s_len, cd), lambda j: (0, j))]
    blk_vmem = lambda k: pltpu.VMEM((k, r, cd), BF16)
    (q, acc), job_out = _call(
        main, jobs, name=name, grid=(nsteps,), relay_step=relay_step, ins=[a, b], in_specs=in_specs,
        out_shape=[_sds((NCHIP_OTHER, r, cd), BF16), _sds((r, cd), F32)],
        out_specs=[pl.BlockSpec((NCHIP_OTHER, r, cd), lambda j: (0, 0, 0)), pl.BlockSpec((r, cd), lambda j: (0, 0))],
        scratch=[pltpu.VMEM((at_rows, s_len), BF16), blk_vmem(2), blk_vmem(4), blk_vmem(4),
                 pltpu.SemaphoreType.DMA((4,)), pltpu.SemaphoreType.DMA((4,))])
    return q, acc, job_out


def _wgrad_cols_early(a, b, jobs, *, name, relay_step=0):
    s_len, m = a.shape
    r, cd = m, b.shape[1] // NDEV
    h = r // 2

    def chip_at(pos, base):
        return base ^ (3 - pos)

    def main(i, ins, outs, scr):
        a_ref, b_ref = ins
        q_ref, acc_ref, rel_ref = outs
        at_scr, stage, mine, land, q2_scr, send_sems, recv_sems, via_send, via_recv = scr
        x, y, c = _place()
        base = 2 * x + y
        xn, yn, _ = _other_chips(x, y)
        pos, pc = i >> 1, i & 1
        pi = chip_at(pos, base)

        def to_sibling(chip, slot):
            return pltpu.make_async_remote_copy(
                src_ref=stage.at[slot], dst_ref=land.at[chip], send_sem=send_sems.at[chip],
                recv_sem=recv_sems.at[chip], device_id=(x, y, 1 - c), device_id_type=MESH)

        def via(k):
            return pltpu.make_async_remote_copy(
                src_ref=q2_scr.at[pl.ds(k * h, h)], dst_ref=rel_ref.at[k], send_sem=via_send.at[k],
                recv_sem=via_recv.at[k], device_id=(*(xn, yn)[k], c), device_id_type=MESH)

        @pl.when(i == 0)
        def _():
            _transpose_into(at_scr, a_ref, TRANSPOSE_ROWS)

        res = _dot(at_scr[...], b_ref[...]).astype(BF16)

        @pl.when(pc != c)
        def _():
            @pl.when(pos >= 2)
            def _():
                to_sibling(chip_at(pos - 2, base), pos & 1).wait_send()

            stage[pos & 1] = res
            to_sibling(pi, pos & 1).start()

        @pl.when(pc == c)
        def _():
            mine[pi] = res

        @pl.when(i == 1)
        def _():
            dg = chip_at(0, base)
            to_sibling(dg, 0).wait_recv()
            q2 = (mine[dg].astype(F32) + land[dg].astype(F32)).astype(BF16)
            q2_scr[...] = q2
            q_ref[2] = q2
            via(0).start()
            via(1).start()

        @pl.when(i == NDEV - 1)
        def _():
            for pos_ in (1, 2, 3):
                to_sibling(chip_at(pos_, base), 0).wait_recv()
            to_sibling(chip_at(2, base), 0).wait_send()
            to_sibling(chip_at(3, base), 1).wait_send()
            for k in range(2):
                via(k).wait_recv()
            for k in range(2):
                via(k).wait_send()
            for j, chip in enumerate((base ^ 2, base ^ 1)):
                q_ref[j] = (mine[chip].astype(F32) + land[chip].astype(F32)).astype(BF16)
            acc_ref[...] = mine[base].astype(F32) + land[base].astype(F32)

    def b_block(j):
        base = 2 * lax.axis_index("x") + lax.axis_index("y")
        return (0, 2 * chip_at(j >> 1, base) + (j & 1))

    blk_vmem = lambda k: pltpu.VMEM((k, r, cd), BF16)
    (q, acc, rel), job_out = _call(
        main, jobs, name=name, grid=(NDEV,), relay_step=relay_step, ins=[a, b],
        in_specs=[_VMEM, pl.BlockSpec((s_len, cd), b_block)],
        out_shape=[_sds((NCHIP_OTHER, r, cd), BF16), _sds((r, cd), F32), _sds((2, h, cd), BF16)],
        out_specs=[pl.BlockSpec((NCHIP_OTHER, r, cd), lambda j: (0, 0, 0)), pl.BlockSpec((r, cd), lambda j: (0, 0)), _HBM],
        scratch=[pltpu.VMEM((m, s_len), BF16), blk_vmem(2), blk_vmem(4), blk_vmem(4), pltpu.VMEM((r, cd), BF16),
                 pltpu.SemaphoreType.DMA((4,)), pltpu.SemaphoreType.DMA((4,)), pltpu.SemaphoreType.DMA((2,)),
                 pltpu.SemaphoreType.DMA((2,))])
    return q, acc, rel, job_out


class _ExchangeRest:
    def __init__(self, q, relayed):
        _, r, cd = q.shape
        half = (2, r // 2, cd)
        self.ins, self.in_specs = [q, relayed], [_HBM, _HBM]
        self.out_shape, self.out_specs = [_sds((2, r, cd), q.dtype)], [_HBM]
        self.scratch = [pltpu.VMEM(half, q.dtype), pltpu.VMEM(half, q.dtype), pltpu.VMEM(half, q.dtype),
                        pltpu.SemaphoreType.DMA((4,)), pltpu.SemaphoreType.DMA((4,)), pltpu.SemaphoreType.DMA((4,))]

    def ops(self, ins, outs, scr):
        (q, rel_in), (land,) = ins, outs
        own, rel, comb, send_sems, recv_sems, local_sems = scr
        h = q.shape[1] // 2
        x, y, c = _place()
        xn, yn, _ = _other_chips(x, y)
        h0, h1 = pl.ds(0, h), pl.ds(h, h)

        def remote(k, src, dst, chip):
            return pltpu.make_async_remote_copy(src_ref=src, dst_ref=dst, send_sem=send_sems.at[k],
                                                recv_sem=recv_sems.at[k], device_id=(*chip, c), device_id_type=MESH)

        def sends():
            return [remote(0, q.at[0, h0], land.at[0, h0], xn), remote(1, q.at[1, h1], land.at[1, h1], yn),
                    remote(2, comb.at[0], land.at[1, h0], yn), remote(3, comb.at[1], land.at[0, h1], xn)]

        def loads():
            return [pltpu.make_async_copy(q.at[1, h0], own.at[0], local_sems.at[0]),
                    pltpu.make_async_copy(q.at[0, h1], own.at[1], local_sems.at[1]),
                    pltpu.make_async_copy(rel_in.at[0], rel.at[0], local_sems.at[2]),
                    pltpu.make_async_copy(rel_in.at[1], rel.at[1], local_sems.at[3])]

        def start():
            cps, lds = sends(), loads()
            for ld in lds:
                ld.start()
            cps[0].start()
            cps[1].start()
            for ld in lds:
                ld.wait()
            for k in range(2):
                comb[k] = (own[k].astype(F32) + rel[k].astype(F32)).astype(comb.dtype)
            cps[2].start()
            cps[3].start()

        def finish():
            cps = sends()
            for cp in cps:
                cp.wait_recv()
            for cp in cps:
                cp.wait_send()

        return start, lambda: None, finish


def _adam_math(w, g, m, v):
    m = B1 * m + (1.0 - B1) * g
    v = B2 * v + (1.0 - B2) * (g * g)
    m_hat = m / (1.0 - B1 ** STEP)
    v_hat = v / (1.0 - B2 ** STEP)
    delta = (-LR) * (m_hat / (jnp.sqrt(v_hat) + ADAM_EPS) + WD * w)
    return delta, m, v


def _adam_big(w, acc, land, m, v, name):
    r, cd = w.shape
    rb = ADAM_ROWS if r % ADAM_ROWS == 0 else r
    nland = land.shape[0]

    def body(w_ref, acc_ref, land_ref, m_ref, v_ref, g_ref, d_ref, mo_ref, vo_ref):
        g = acc_ref[...]
        for j in range(nland):
            g = g + land_ref[j].astype(F32)
        g_ref[...] = g
        d_ref[...], mo_ref[...], vo_ref[...] = _adam_math(w_ref[...], g, m_ref[...], v_ref[...])

    blk = pl.BlockSpec((rb, cd), lambda i: (i, 0))
    blk3 = pl.BlockSpec((nland, rb, cd), lambda i: (0, i, 0))
    return pl.pallas_call(
        body, name=name, grid=(r // rb,), in_specs=[blk, blk, blk3, blk, blk], out_specs=[blk] * 4,
        out_shape=[_sds((r, cd), F32)] * 4,
        compiler_params=_params(dimension_semantics=("arbitrary",)),
    )(w, acc, land, m, v)


def _adam_small(groups):
    n = len(groups)

    def body(*refs):
        ins, outs = refs[:4 * n], refs[4 * n:]
        for k in range(n):
            w_ref, g_ref, m_ref, v_ref = ins[4 * k:4 * k + 4]
            d, mo, vo = _adam_math(w_ref[...], g_ref[...], m_ref[...], v_ref[...])
            outs[3 * k][...] = d
            outs[3 * k + 1][...] = mo
            outs[3 * k + 2][...] = vo

    flat = [a for grp in groups for a in grp]
    shapes = [_sds(grp[0].shape, F32) for grp in groups for _ in range(3)]
    res = pl.pallas_call(
        body, name="adam_small", in_specs=[_VMEM] * (4 * n), out_specs=[_VMEM] * (3 * n), out_shape=shapes,
        compiler_params=_params(),
    )(*flat)
    return [tuple(res[3 * k:3 * k + 3]) for k in range(n)]


TM_FWD_A = 256
RELAY_STEP_FWD_A = 2
FORWARD_STEP_FWD_A = 6
RELAY_STEP_FWD_B = 2
TM_BWD_A = 256
RELAY_STEP_BWD_A = 3
TM_BWD_A_IN = 256
RELAY_STEP_BWD_A_IN = 4
RELAY_STEP_WGRAD_A_IN = 1
TM_FWD_B = 256
TM_HEAD = 512
TM_BWD_B = 256


def _pack(parts, rows):
    flat = jnp.concatenate([p.reshape(-1) for p in parts])
    return jnp.pad(flat, (0, NDEV * rows * LANES - flat.shape[0])).reshape(NDEV, rows, LANES)


def _unpack(packed, shapes):
    flat, out, off = packed.reshape(-1), [], 0
    for s in shapes:
        size = 1
        for d in s:
            size *= d
        out.append(flat[off:off + size].reshape(s))
        off += size
    return out


def kernel(x, norm_w, a_w_in, a_ln_w, a_ln_b, a_w_s, a_b_s, a_w_out, b_w_in, b_conv_w, b_conv_b, b_gate_a_w, b_gate_a_b, b_gate_x_w, b_gate_x_b, b_lambda, b_w_out, norm_f_w, loss_target, m_norm_w, m_a_w_in, m_a_ln_w, m_a_ln_b, m_a_w_s, m_a_b_s, m_a_w_out, m_b_w_in, m_b_conv_w, m_b_conv_b, m_b_gate_a_w, m_b_gate_a_b, m_b_gate_x_w, m_b_gate_x_b, m_b_lambda, m_b_w_out, m_norm_f_w, v_norm_w, v_a_w_in, v_a_ln_w, v_a_ln_b, v_a_w_s, v_a_b_s, v_a_w_out, v_b_w_in, v_b_conv_w, v_b_conv_b, v_b_gate_a_w, v_b_gate_a_b, v_b_gate_x_w, v_b_gate_x_b, v_b_lambda, v_b_w_out, v_norm_f_w):
    me = 4 * lax.axis_index("x") + 2 * lax.axis_index("y") + lax.axis_index("c")
    xs, tgt = x[0], loss_target[0]
    nw0, nw1, nfw = norm_w[0:1], norm_w[1:2], norm_f_w.reshape(1, D)
    w_s, bst = a_w_s[0], a_b_s[0].T
    gcat = jnp.concatenate([b_gate_a_w[0], b_gate_x_w[0]], axis=-1).astype(BF16)

    p8_shard = jnp.concatenate([b_conv_w[0], b_conv_b, b_gate_a_b, b_gate_x_b, b_lambda], axis=0)
    (z, h0, ya, pp), ((win_a8, p8_all), (wout_a8, win_b8)) = _fwd_a(
        xs, nw0, a_ln_w, a_ln_b, w_s, bst,
        [_Gather([a_w_in[0], p8_shard], [BF16, F32]), _Gather([a_w_out[0], b_w_in[0]], [BF16, BF16])],
        tm=TM_FWD_A, relay_step=RELAY_STEP_FWD_A)
    p8 = jnp.transpose(p8_all, (1, 0, 2)).reshape(SUBLANES, BW)
    wout_a = wout_a8.reshape(AW, D)
    (x1, zb, hs, h1, yb, *saved_b), ((wout_b8,),) = _fwd_b(
        xs, ya, wout_a, nw1, win_b8, p8, gcat, [_Gather([b_w_out[0]], [BF16])],
        tm=TM_FWD_B, relay_step=RELAY_STEP_FWD_B)
    wout_b = wout_b8.reshape(BW, D)
    dx2, dx2b, loss, g_nfw = _head(x1, yb, wout_b, nfw, tgt, tm=TM_HEAD)

    dx1, dx1b, dzb, g_p8, g_ga, g_gx, g_nw1 = _bwd_b(dx2, zb, hs, x1, saved_b, nw1, win_b8, p8, gcat, wout_b,
                                                     tm=TM_BWD_B)
    q_wout_b, acc_wout_b, _ = _wgrad(yb, dx2b, [], by_rows=True, per=2, name="wgrad_b_out")
    shapes_b = [(1, D), (1, D), (SUBLANES, BW), (1, 1)]
    pack_b = _pack([g_nfw, g_nw1, g_p8, loss], 16)
    small_b = _InChip([g_ga.reshape(NDEV, -1, HD), g_gx.reshape(NDEV, -1, HD), pack_b])
    q_win_b, acc_win_b, (sm_b, (l_wout_b,)) = _wgrad(h1, dzb, [small_b, _Exchange([q_wout_b])], by_rows=False, per=1,
                                                      name="wgrad_b_in")
    qs_b, accs_b = sm_b[:3], sm_b[3:]

    (dz, g_lnw, g_lnb, g_ws, g_bst), (lands_b, (l_win_b,)) = _bwd_a(
        dx1b, z, pp, a_ln_w, a_ln_b, w_s, bst, wout_a, [_Exchange(qs_b), _ExchangeVia(q_win_b)],
        tm=TM_BWD_A, relay_step=RELAY_STEP_BWD_A)
    shapes_a = [(1, AW), (1, AW), (CH, G)]
    pack_a = _pack([g_lnw, g_lnb, g_bst], 8)
    q_wout_a, acc_wout_a, (red_b, sm_a) = _wgrad(
        ya, dx1b, [_SumGather(accs_b, lands_b), _InChip([g_ws, pack_a])], by_rows=True, per=2,
        name="wgrad_a_out", relay_step=1)
    qs_a, accs_a = sm_a[:2], sm_a[2:]
    q_win_a, acc_win_a, rel_a, (lands_a, (l_wout_a,)) = _wgrad_cols_early(
        h0, dz, [_Exchange(qs_a), _ExchangeVia(q_wout_a)], name="wgrad_a_in", relay_step=RELAY_STEP_WGRAD_A_IN)
    (gx, g_nw0), (red_a, (l_win_a,)) = _bwd_a_in(
        dz, dx1, xs, nw0, win_a8, [_SumGather(accs_a, lands_a), _ExchangeRest(q_win_a, rel_a)],
        tm=TM_BWD_A_IN, relay_step=RELAY_STEP_BWD_A_IN)

    r_ga, r_gx, r_pack_b = red_b
    r_nfw, r_nw1, r_p8, loss = _unpack(r_pack_b, shapes_b)
    r_ws, r_pack_a = red_a
    r_lnw, r_lnb, r_bst = _unpack(r_pack_a, shapes_a)
    g_p8 = lax.dynamic_slice_in_dim(r_p8, me * (BW // NDEV), BW // NDEV, axis=1)
    loss = loss[0, 0]

    weights = dict(norm_w=norm_w, a_w_in=a_w_in, a_ln_w=a_ln_w, a_ln_b=a_ln_b, a_w_s=a_w_s, a_b_s=a_b_s, a_w_out=a_w_out,
                   b_w_in=b_w_in, b_conv_w=b_conv_w, b_conv_b=b_conv_b, b_gate_a_w=b_gate_a_w, b_gate_a_b=b_gate_a_b,
                   b_gate_x_w=b_gate_x_w, b_gate_x_b=b_gate_x_b, b_lambda=b_lambda, b_w_out=b_w_out, norm_f_w=norm_f_w)
    mom1 = dict(norm_w=m_norm_w, a_w_in=m_a_w_in, a_ln_w=m_a_ln_w, a_ln_b=m_a_ln_b, a_w_s=m_a_w_s, a_b_s=m_a_b_s,
                a_w_out=m_a_w_out, b_w_in=m_b_w_in, b_conv_w=m_b_conv_w, b_conv_b=m_b_conv_b, b_gate_a_w=m_b_gate_a_w,
                b_gate_a_b=m_b_gate_a_b, b_gate_x_w=m_b_gate_x_w, b_gate_x_b=m_b_gate_x_b, b_lambda=m_b_lambda,
                b_w_out=m_b_w_out, norm_f_w=m_norm_f_w)
    mom2 = dict(norm_w=v_norm_w, a_w_in=v_a_w_in, a_ln_w=v_a_ln_w, a_ln_b=v_a_ln_b, a_w_s=v_a_w_s, a_b_s=v_a_b_s,
                a_w_out=v_a_w_out, b_w_in=v_b_w_in, b_conv_w=v_b_conv_w, b_conv_b=v_b_conv_b, b_gate_a_w=v_b_gate_a_w,
                b_gate_a_b=v_b_gate_a_b, b_gate_x_w=v_b_gate_x_w, b_gate_x_b=v_b_gate_x_b, b_lambda=v_b_lambda,
                b_w_out=v_b_w_out, norm_f_w=v_norm_f_w)
    names = list(weights)

    def as2d(a):
        return a.reshape(-1, a.shape[-1])

    upd, grads = {}, {}
    for k, acc, land in (("a_w_in", acc_win_a, l_win_a), ("a_w_out", acc_wout_a, l_wout_a),
                         ("b_w_in", acc_win_b, l_win_b), ("b_w_out", acc_wout_b, l_wout_b)):
        g, d, mo, vo = _adam_big(as2d(weights[k]), acc, land, as2d(mom1[k]), as2d(mom2[k]), "adam_" + k)
        grads[k] = g[None]
        upd[k] = (d, mo, vo)
    grads.update(
        norm_w=jnp.concatenate([g_nw0, r_nw1], axis=0), a_ln_w=r_lnw, a_ln_b=r_lnb,
        a_w_s=r_ws.reshape(1, G, CH, CH), a_b_s=r_bst.T[None],
        b_conv_w=g_p8[None, 0:4], b_conv_b=g_p8[4:5], b_gate_a_w=r_ga.reshape(1, BH, HD, HD), b_gate_a_b=g_p8[5:6],
        b_gate_x_w=r_gx.reshape(1, BH, HD, HD), b_gate_x_b=g_p8[6:7], b_lambda=g_p8[7:8], norm_f_w=r_nfw.reshape(D))
    small_names = [k for k in names if k not in upd]
    res = _adam_small([(as2d(weights[k]), as2d(grads[k]), as2d(mom1[k]), as2d(mom2[k])) for k in small_names])
    for k, r3 in zip(small_names, res):
        upd[k] = r3
    deltas = [upd[k][0].reshape(weights[k].shape) for k in names]
    new_m = [upd[k][1].reshape(weights[k].shape) for k in names]
    new_v = [upd[k][2].reshape(weights[k].shape) for k in names]
    return (loss, gx[None], *[grads[k] for k in names], *deltas, *new_m, *new_v)
```

```python
import jax
import jax.numpy as jnp
from jax import lax
from jax.experimental import pallas as pl
from jax.experimental.pallas import tpu as pltpu

F32 = jnp.float32
BF16 = jnp.bfloat16
MESH = pl.DeviceIdType.MESH

NDEV = 8
NCHIP_OTHER = 3
D = 1024
AW = 2048
G = 8
GD = AW // G
CH = 128
BW = 1536
BH = 12
HD = BW // BH
CA = 3 * AW // NDEV
CB = 2 * BW // NDEV
RMS_EPS = 1e-6
LN_EPS = 1e-5
RG_C = 8.0
LR, B1, B2, ADAM_EPS, WD, STEP = 0.001, 0.9, 0.999, 1e-08, 0.01, 10
V7X_VMEM_BYTES = 64 * 1024 * 1024
VMEM_LIMIT = V7X_VMEM_BYTES - 8 * 1024 * 1024
SUBLANES = 8
LANES = 128
BF16_ROWS = 16
TRANSPOSE_ROWS = 256
ADAM_ROWS = 512
GELU_C = 0.7978845608028654
GELU_K = 0.044715

_VMEM = pl.BlockSpec(memory_space=pltpu.VMEM)
_HBM = pl.BlockSpec(memory_space=pltpu.HBM)


def _sds(shape, dtype):
    return jax.ShapeDtypeStruct(tuple(shape), dtype)


def _params(**kw):
    return pltpu.CompilerParams(vmem_limit_bytes=VMEM_LIMIT, **kw)


def _gelu_t(z):
    p = 0.5 * jnp.tanh(z * (GELU_C + (GELU_C * GELU_K) * (z * z))) + 0.5
    return z * p, p


def _dgelu(z, p):
    return p * (1.0 + (z * (1.0 - p)) * (2.0 * GELU_C + (6.0 * GELU_C * GELU_K) * (z * z)))


def _sigmoid(v):
    return 0.5 * jnp.tanh(0.5 * v) + 0.5


def _softplus_neg(lam):
    return jnp.maximum(-lam, 0.0) + jnp.log1p(jnp.exp(-jnp.abs(lam)))


def _dot(a, b):
    return jnp.dot(a, b, preferred_element_type=F32)


def _dot_nt(a, b):
    return lax.dot_general(a, b, (((1,), (1,)), ((), ())), preferred_element_type=F32)


def _rowsum(v):
    return jnp.sum(v, axis=0, keepdims=True)


def _causal_mask():
    r = lax.broadcasted_iota(jnp.int32, (CH, CH), 0)
    c = lax.broadcasted_iota(jnp.int32, (CH, CH), 1)
    return r >= c


def _rms(x):
    return lax.rsqrt(jnp.mean(x * x, axis=-1, keepdims=True) + RMS_EPS)


def _rms_bwd(dh, x, r, nw):
    gy = dh * nw
    return r * gy - x * (r * r * r) * jnp.mean(gy * x, axis=-1, keepdims=True)


def _place():
    return lax.axis_index("x"), lax.axis_index("y"), lax.axis_index("c")


def _other_chips(x, y):
    return [(1 - x, y), (x, 1 - y), (1 - x, 1 - y)]


GATHER_SLOTS = 10


def _gather_ops(ins, outs, send_sems, recv_sems, local_sems):
    n = len(ins)
    x, y, c = _place()
    sibling = (x, y, 1 - c)
    xn, yn, dg = _other_chips(x, y)
    split = [ins[i].shape[0] % (2 * BF16_ROWS) == 0 for i in range(n)]

    def blk(chip, core):
        return 4 * chip[0] + 2 * chip[1] + core

    me = blk((x, y), c)

    def part(ref, i, half):
        if half is None:
            return ref
        h = ins[i].shape[0] // 2
        return ref.at[pl.ds(half * h, h)]

    def copy(i, k, block, to, half=None, src=None):
        dst = part(outs[i].at[block], i, half)
        return pltpu.make_async_remote_copy(
            src_ref=dst if src is None else part(src, i, half), dst_ref=dst,
            send_sem=send_sems.at[k, i], recv_sem=recv_sems.at[k, i], device_id=to, device_id_type=MESH)

    def first_copies():
        mine = [pltpu.make_async_copy(ins[i], outs[i].at[me], local_sems.at[i]) for i in range(n)]
        first = []
        for i in range(n):
            first.append(copy(i, 0, me, sibling, src=ins[i]))
            if split[i]:
                first.append(copy(i, 1, me, (*xn, c), 0, ins[i]))
                first.append(copy(i, 3, me, (*yn, c), 1, ins[i]))
                first.append(copy(i, 2, me, (*xn, c), 1, ins[i]))
                first.append(copy(i, 4, me, (*yn, c), 0, ins[i]))
            else:
                first.append(copy(i, 1, me, (*xn, c), None, ins[i]))
                first.append(copy(i, 3, me, (*yn, c), None, ins[i]))
                first.append(copy(i, 5, me, (*dg, c), None, ins[i]))
        return mine, first

    def onward():
        out = []
        for i in range(n):
            if split[i]:
                out.append(copy(i, 5, blk(xn, c), (*yn, c), 0))
                out.append(copy(i, 6, blk(yn, c), (*xn, c), 1))
        return out

    def start():
        mine, first = first_copies()
        for cp in mine + first:
            cp.start()

    def relay():
        sends = onward()
        for i in range(n):
            if split[i]:
                copy(i, 1, blk(xn, c), sibling, 0).wait_recv()
                sends.pop(0).start()
                copy(i, 3, blk(yn, c), sibling, 1).wait_recv()
                sends.pop(0).start()

    def passes():
        return [copy(i, 7 + j, blk(chip, c), sibling) for i in range(n) for j, chip in enumerate((xn, yn, dg))]

    def forward():
        fwd = passes()
        for i in range(n):
            if split[i]:
                copy(i, 2, blk(xn, c), sibling, 1).wait_recv()
                fwd[3 * i].start()
                copy(i, 4, blk(yn, c), sibling, 0).wait_recv()
                fwd[3 * i + 1].start()
                copy(i, 5, blk(dg, c), sibling, 0).wait_recv()
                copy(i, 6, blk(dg, c), sibling, 1).wait_recv()
                fwd[3 * i + 2].start()
            else:
                copy(i, 1, blk(xn, c), sibling).wait_recv()
                fwd[3 * i].start()
                copy(i, 3, blk(yn, c), sibling).wait_recv()
                fwd[3 * i + 1].start()
                copy(i, 5, blk(dg, c), sibling).wait_recv()
                fwd[3 * i + 2].start()

    def finish():
        mine, first = first_copies()
        for i in range(n):
            copy(i, 0, blk((x, y), 1 - c), sibling).wait_recv()
            for j, chip in enumerate((xn, yn, dg)):
                copy(i, 7 + j, blk(chip, 1 - c), sibling).wait_recv()
        for cp in first + passes() + onward():
            cp.wait_send()
        for cp in mine:
            cp.wait()

    return start, relay, forward, finish


def _gather_sems(n):
    return [pltpu.SemaphoreType.DMA((GATHER_SLOTS, n)), pltpu.SemaphoreType.DMA((GATHER_SLOTS, n)),
            pltpu.SemaphoreType.DMA((n,))]


class _Gather:
    def __init__(self, shards, as_dtypes=None):
        n = len(shards)
        dts = [s.dtype for s in shards] if as_dtypes is None else list(as_dtypes)
        self.cast = [jnp.dtype(d) != s.dtype for d, s in zip(dts, shards)]
        self.ins = list(shards)
        self.in_specs = [_VMEM if c else _HBM for c in self.cast]
        self.out_shape = [_sds((NDEV,) + s.shape, d) for s, d in zip(shards, dts)]
        self.out_specs = [_HBM] * n
        self.scratch = [pltpu.VMEM(s.shape, d) for s, d, c in zip(shards, dts, self.cast) if c] + _gather_sems(n)

    def ops(self, ins, outs, scr):
        ncast = sum(self.cast)
        staged = iter(scr[:ncast])
        srcs = [next(staged) if c else ref for c, ref in zip(self.cast, ins)]
        start, relay, forward, finish = _gather_ops(srcs, outs, *scr[ncast:])

        def cast_and_start():
            for c, ref, src in zip(self.cast, ins, srcs):
                if c:
                    src[...] = ref[...].astype(src.dtype)
            start()

        return cast_and_start, relay, forward, finish


class _Exchange:
    def __init__(self, qs):
        n = len(qs)
        self.ins, self.in_specs = list(qs), [_HBM] * n
        self.out_shape = [_sds(q.shape, q.dtype) for q in qs]
        self.out_specs = [_HBM] * n
        self.scratch = [pltpu.SemaphoreType.DMA((NCHIP_OTHER, n)), pltpu.SemaphoreType.DMA((NCHIP_OTHER, n))]

    def ops(self, ins, outs, scr):
        send_sems, recv_sems = scr
        n = len(ins)
        x, y, c = _place()
        chips = _other_chips(x, y)

        def copies():
            return [pltpu.make_async_remote_copy(
                src_ref=ins[i].at[j], dst_ref=outs[i].at[j], send_sem=send_sems.at[j, i],
                recv_sem=recv_sems.at[j, i], device_id=(*chips[j], c), device_id_type=MESH)
                for i in range(n) for j in range(NCHIP_OTHER)]

        def start():
            for cp in copies():
                cp.start()

        def finish():
            cps = copies()
            for cp in cps:
                cp.wait_recv()
            for cp in cps:
                cp.wait_send()

        return start, lambda: None, finish


class _ExchangeVia:
    def __init__(self, q):
        _, r, cd = q.shape
        half = (2, r // 2, cd)
        self.ins, self.in_specs = [q], [_HBM]
        self.out_shape, self.out_specs = [_sds((2, r, cd), q.dtype)], [_HBM]
        self.scratch = [pltpu.VMEM(half, q.dtype), pltpu.VMEM(half, q.dtype), pltpu.VMEM(half, q.dtype),
                        pltpu.SemaphoreType.DMA((6,)), pltpu.SemaphoreType.DMA((6,)), pltpu.SemaphoreType.DMA((2,))]

    def ops(self, ins, outs, scr):
        (q,), (land,) = ins, outs
        relayed, own, comb, send_sems, recv_sems, local_sems = scr
        h = q.shape[1] // 2
        x, y, c = _place()
        xn, yn, _ = _other_chips(x, y)
        h0, h1 = pl.ds(0, h), pl.ds(h, h)

        def remote(k, src, dst, chip):
            return pltpu.make_async_remote_copy(src_ref=src, dst_ref=dst, send_sem=send_sems.at[k],
                                                recv_sem=recv_sems.at[k], device_id=(*chip, c), device_id_type=MESH)

        def via():
            return [remote(2, q.at[2, h0], relayed.at[0], xn), remote(3, q.at[2, h1], relayed.at[1], yn)]

        def direct():
            return [remote(0, q.at[0, h0], land.at[0, h0], xn), remote(1, q.at[1, h1], land.at[1, h1], yn)]

        def second():
            return [remote(4, comb.at[0], land.at[1, h0], yn), remote(5, comb.at[1], land.at[0, h1], xn)]

        def mine():
            return [pltpu.make_async_copy(q.at[1, h0], own.at[0], local_sems.at[0]),
                    pltpu.make_async_copy(q.at[0, h1], own.at[1], local_sems.at[1])]

        def start():
            for cp in via() + direct() + mine():
                cp.start()

        def relay():
            arrived, loaded, onward = via(), mine(), second()
            for k in range(2):
                arrived[k].wait_recv()
                loaded[k].wait()
                comb[k] = (own[k].astype(F32) + relayed[k].astype(F32)).astype(comb.dtype)
                onward[k].start()

        def finish():
            landing = direct() + second()
            for cp in landing:
                cp.wait_recv()
            for cp in via() + landing:
                cp.wait_send()

        return start, relay, finish


class _SumGather:
    def __init__(self, accs, lands):
        n = len(accs)
        self.n = n
        self.ins, self.in_specs = list(accs) + list(lands), [_VMEM] * (2 * n)
        self.out_shape = [_sds((NDEV,) + a.shape, a.dtype) for a in accs]
        self.out_specs = [_HBM] * n
        self.scratch = [pltpu.VMEM(a.shape, a.dtype) for a in accs] + _gather_sems(n)

    def ops(self, ins, outs, scr):
        n = self.n
        accs, lands, mine = ins[:n], ins[n:], scr[:n]
        g_start, relay, forward, finish = _gather_ops(mine, outs, *scr[n:])

        def start():
            for i in range(n):
                mine[i][...] = accs[i][...] + lands[i][0] + lands[i][1] + lands[i][2]
            g_start()

        return start, relay, forward, finish


def _call(main, jobs, *, name, grid, ins, in_specs, out_shape, out_specs, scratch, relay_step=0, first=0,
          prologue=None, forward_step=None):
    (nsteps,) = grid
    n_in, n_out, n_scr = len(ins), len(out_shape), len(scratch)

    def body(*refs):
        pos = [0]

        def take(k):
            r = refs[pos[0]:pos[0] + k]
            pos[0] += k
            return r

        m_in = take(n_in)
        j_in = [take(len(j.ins)) for j in jobs]
        m_out = take(n_out)
        j_out = [take(len(j.out_shape)) for j in jobs]
        m_scr = take(n_scr)
        j_scr = [take(len(j.scratch)) for j in jobs]
        ops = [_four(j.ops(a, b, s)) for j, a, b, s in zip(jobs, j_in, j_out, j_scr)]
        i = pl.program_id(0)

        if ops:
            @pl.when(i == 0)
            def _():
                for o in ops[:first]:
                    o[0]()
                for o in ops[:first]:
                    o[1]()
                for o in ops[first:]:
                    o[0]()
                for o in ops[:first]:
                    o[2]()
                for o in ops[:first]:
                    o[3]()
                if prologue is not None:
                    prologue(j_out[:first], m_scr)

        main(i, m_in, m_out, m_scr)

        forward_at = max(relay_step, nsteps - 2) if forward_step is None else min(forward_step, nsteps - 1)
        for stage, at in ((1, min(relay_step, nsteps - 1)), (2, forward_at), (3, nsteps - 1)):
            if ops[first:]:
                @pl.when(i == at)
                def _():
                    for o in ops[first:]:
                        o[stage]()

    res = pl.pallas_call(
        body, name=name, grid=grid,
        in_specs=list(in_specs) + [s for j in jobs for s in j.in_specs],
        out_specs=list(out_specs) + [s for j in jobs for s in j.out_specs],
        out_shape=list(out_shape) + [s for j in jobs for s in j.out_shape],
        scratch_shapes=list(scratch) + [s for j in jobs for s in j.scratch],
        compiler_params=_params(dimension_semantics=("arbitrary",)),
    )(*ins, *[a for j in jobs for a in j.ins])
    main_out, rest, job_out = res[:n_out], res[n_out:], []
    for j in jobs:
        k = len(j.out_shape)
        job_out.append(rest[:k])
        rest = rest[k:]
    return main_out, job_out


def _four(ops):
    return ops if len(ops) == 4 else (ops[0], ops[1], lambda: None, ops[2])


class _InChip:
    def __init__(self, ps):
        n = len(ps)
        self.n = n
        blk = [p.shape[1:] for p in ps]
        self.ins, self.in_specs = list(ps), [_HBM] * n
        self.out_shape = [_sds((NCHIP_OTHER,) + b, p.dtype) for b, p in zip(blk, ps)] + [_sds(b, F32) for b in blk]
        self.out_specs = [_VMEM] * (2 * n)
        self.scratch = ([pltpu.VMEM((4,) + b, p.dtype) for b, p in zip(blk, ps)] * 2
                        + [pltpu.SemaphoreType.DMA((4, n))] * 3)

    def ops(self, ins, outs, scr):
        n = self.n
        q_refs, acc_refs = outs[:n], outs[n:]
        mines, lands = scr[:n], scr[n:2 * n]
        send_sems, recv_sems, local_sems = scr[2 * n:]
        x, y, c = _place()
        sibling = (x, y, 1 - c)

        def copies():
            out = []
            for i in range(n):
                for pi in range(4):
                    loc = pltpu.make_async_copy(ins[i].at[2 * pi + c], mines[i].at[pi], local_sems.at[pi, i])
                    cp = pltpu.make_async_remote_copy(
                        src_ref=ins[i].at[2 * pi + (1 - c)], dst_ref=lands[i].at[pi],
                        send_sem=send_sems.at[pi, i], recv_sem=recv_sems.at[pi, i],
                        device_id=sibling, device_id_type=MESH)
                    out.append((loc, cp))
            return out

        def start():
            for loc, cp in copies():
                loc.start()
                cp.start()

        def finish():
            pairs = copies()
            for loc, cp in pairs:
                loc.wait()
                cp.wait_recv()
            for i in range(n):
                _chip_sums(mines[i], lands[i], q_refs[i], acc_refs[i], x, y)
            for _, cp in pairs:
                cp.wait_send()

        return start, lambda: None, finish


def _chip_sums(mine, land, q_ref, acc_ref, x, y):
    for j, (qx, qy) in enumerate(_other_chips(x, y)):
        qi = 2 * qx + qy
        q_ref[j] = (mine[qi].astype(F32) + land[qi].astype(F32)).astype(q_ref.dtype)
    mi = 2 * x + y
    acc_ref[...] = mine[mi].astype(F32) + land[mi].astype(F32)


def _direct_sum(v, buf, send_sems, recv_sems):
    x, y, c = _place()
    me = 4 * x + 2 * y + c
    buf[me] = v
    cps = []
    for k in range(1, NDEV):
        fx, fy, fc = (k >> 2) & 1, (k >> 1) & 1, k & 1
        peer = ((1 - x) if fx else x, (1 - y) if fy else y, (1 - c) if fc else c)
        cps.append((peer, pltpu.make_async_remote_copy(
            src_ref=buf.at[me], dst_ref=buf.at[me], send_sem=send_sems.at[k - 1], recv_sem=recv_sems.at[k - 1],
            device_id=peer, device_id_type=MESH)))
    for _, cp in cps:
        cp.start()
    for k, (peer, _) in enumerate(cps):
        theirs = 4 * peer[0] + 2 * peer[1] + peer[2]
        pltpu.make_async_remote_copy(
            src_ref=buf.at[theirs], dst_ref=buf.at[theirs], send_sem=send_sems.at[k], recv_sem=recv_sems.at[k],
            device_id=peer, device_id_type=MESH).wait_recv()
    acc = buf[0]
    for j in range(1, NDEV):
        acc = acc + buf[j]
    for _, cp in cps:
        cp.wait_send()
    return acc


def _direct_sum_scratch(shape, dtype):
    return [pltpu.VMEM((NDEV,) + tuple(shape), dtype), pltpu.SemaphoreType.DMA((NDEV - 1,)),
            pltpu.SemaphoreType.DMA((NDEV - 1,))]


def _fwd_a(x, nw, lnw, lnb, ws, bst, jobs, *, tm, relay_step):
    s_len = x.shape[0]
    nt = s_len // tm
    nch = tm // CH

    def main(i, ins, outs, scr):
        x_ref, nw_ref, lnw_ref, lnb_ref, ws_ref, bst_ref = ins
        z_ref, h_ref, y_ref, pp_ref = outs
        wc_scr, gv_scr, win_ref = scr

        @pl.when(i == 0)
        def _():
            m = _causal_mask()
            for g in range(G):
                wc_scr[g] = jnp.where(m, ws_ref[g], 0.0).astype(BF16)

        x = x_ref[...]
        h = (x * _rms(x) * nw_ref[...]).astype(BF16)
        h_ref[...] = h
        for k in range(NDEV):
            z_ref[:, k * CA:(k + 1) * CA] = _dot(h, win_ref[k])

        ssum = jnp.zeros((tm, 1), F32)
        for g in range(G):
            vs = slice(AW + g * GD, AW + (g + 1) * GD)
            gv, pv = _gelu_t(z_ref[:, vs])
            pp_ref[:, vs] = pv.astype(BF16)
            gv_scr[:, g * GD:(g + 1) * GD] = gv
            ssum = ssum + jnp.sum(gv, axis=-1, keepdims=True)
        mu = ssum * (1.0 / AW)
        vsum = jnp.zeros((tm, 1), F32)
        for g in range(G):
            dlt = gv_scr[:, g * GD:(g + 1) * GD] - mu
            vsum = vsum + jnp.sum(dlt * dlt, axis=-1, keepdims=True)
        rstd = lax.rsqrt(vsum * (1.0 / AW) + LN_EPS)

        for g in range(G):
            cs = slice(g * GD, (g + 1) * GD)
            gs = slice(2 * AW + g * GD, 2 * AW + (g + 1) * GD)
            v = (gv_scr[:, cs] - mu) * rstd * lnw_ref[:, cs] + lnb_ref[:, cs]
            vb = v.astype(BF16)
            u, pu = _gelu_t(z_ref[:, cs])
            pp_ref[:, cs] = pu.astype(BF16)
            zg = z_ref[:, gs]
            sig = _sigmoid(zg)
            pp_ref[:, gs] = sig.astype(BF16)
            sg = zg * sig
            for n in range(nch):
                rs = slice(n * CH, (n + 1) * CH)
                s = _dot(wc_scr[g], vb[rs, :]) + bst_ref[:, g:g + 1]
                y_ref[rs, cs] = (u[rs, :] * s * sg[rs, :]).astype(BF16)

    tile = lambda w: pl.BlockSpec((tm, w), lambda i: (i, 0))
    return _call(
        main, jobs, name="fwd_a", grid=(nt,), relay_step=relay_step, first=1, forward_step=FORWARD_STEP_FWD_A,
        prologue=lambda gathered, scr: pltpu.sync_copy(gathered[0][0], scr[2]),
        ins=[x, nw, lnw, lnb, ws, bst], in_specs=[tile(D), _VMEM, _VMEM, _VMEM, _VMEM, _VMEM],
        out_shape=[_sds((s_len, 3 * AW), F32), _sds((s_len, D), BF16), _sds((s_len, AW), BF16),
                   _sds((s_len, 3 * AW), BF16)],
        out_specs=[tile(3 * AW), tile(D), tile(AW), tile(3 * AW)],
        scratch=[pltpu.VMEM((G, CH, CH), BF16), pltpu.VMEM((tm, AW), F32), pltpu.VMEM((NDEV, D, CA), BF16)])


def _bwd_a(dx1, z, pp, lnw, lnb, ws, bst, wout, jobs, *, tm, relay_step):
    s_len = dx1.shape[0]
    nt = s_len // tm
    nch = tm // CH

    def main(i, ins, outs, scr):
        dx1_ref, z_ref, pp_ref, lnw_ref, lnb_ref, ws_ref, bst_ref, wout_ref = ins
        dz_ref, glnw_ref, glnb_ref, gws_ref, gbst_ref = outs
        wc_scr, wct_scr, vh_scr, dgv_scr, dy_scr, dv_scr, gbs_acc, gwc_acc = scr

        @pl.when(i == 0)
        def _():
            m = _causal_mask()
            for g in range(G):
                wm = jnp.where(m, ws_ref[g], 0.0)
                wc_scr[g] = wm.astype(BF16)
                wct_scr[g] = wm.T.astype(BF16)
            glnw_ref[...] = jnp.zeros_like(glnw_ref)
            glnb_ref[...] = jnp.zeros_like(glnb_ref)
            gbs_acc[...] = jnp.zeros_like(gbs_acc)
            gwc_acc[...] = jnp.zeros_like(gwc_acc)

        dy_scr[...] = _dot_nt(dx1_ref[...], wout_ref[...])

        ssum = jnp.zeros((tm, 1), F32)
        for g in range(G):
            cs = slice(g * GD, (g + 1) * GD)
            vs = slice(AW + g * GD, AW + (g + 1) * GD)
            zv = z_ref[:, vs]
            pv = pp_ref[:, vs].astype(F32)
            gv = zv * pv
            vh_scr[:, cs] = gv
            dgv_scr[:, cs] = _dgelu(zv, pv)
            ssum = ssum + jnp.sum(gv, axis=-1, keepdims=True)
        mu = ssum * (1.0 / AW)
        vsum = jnp.zeros((tm, 1), F32)
        for g in range(G):
            dlt = vh_scr[:, g * GD:(g + 1) * GD] - mu
            vsum = vsum + jnp.sum(dlt * dlt, axis=-1, keepdims=True)
        rstd = lax.rsqrt(vsum * (1.0 / AW) + LN_EPS)

        m1 = jnp.zeros((tm, 1), F32)
        m2 = jnp.zeros((tm, 1), F32)
        for g in range(G):
            cs = slice(g * GD, (g + 1) * GD)
            gs = slice(2 * AW + g * GD, 2 * AW + (g + 1) * GD)
            vhat = (vh_scr[:, cs] - mu) * rstd
            vh_scr[:, cs] = vhat
            vb = (vhat * lnw_ref[:, cs] + lnb_ref[:, cs]).astype(BF16)
            zu = z_ref[:, cs]
            tu = pp_ref[:, cs].astype(F32)
            u = zu * tu
            zg = z_ref[:, gs]
            sig = pp_ref[:, gs].astype(F32)
            sg = zg * sig
            dy = dy_scr[:, cs]
            dsf = dy * u * sg
            dsb = dsf.astype(BF16)
            dvs = []
            for n in range(nch):
                rs = slice(n * CH, (n + 1) * CH)
                s = _dot(wc_scr[g], vb[rs, :]) + bst_ref[:, g:g + 1]
                dys = dy[rs, :] * s
                dz_ref[rs, cs] = (dys * sg[rs, :] * _dgelu(zu[rs, :], tu[rs, :])).astype(BF16)
                dz_ref[rs, gs] = (dys * u[rs, :] * (sig[rs, :] * (1.0 + zg[rs, :] * (1.0 - sig[rs, :])))).astype(BF16)
                gbs_acc[g] += dsf[rs, :]
                gwc_acc[g] += _dot_nt(dsb[rs, :], vb[rs, :])
                dvs.append(_dot(wct_scr[g], dsb[rs, :]))
            dv = jnp.concatenate(dvs, axis=0) if nch > 1 else dvs[0]
            glnw_ref[:, cs] += _rowsum(dv * vhat)
            glnb_ref[:, cs] += _rowsum(dv)
            dvh = dv * lnw_ref[:, cs]
            dv_scr[:, cs] = dvh
            m1 = m1 + jnp.sum(dvh, axis=-1, keepdims=True)
            m2 = m2 + jnp.sum(dvh * vhat, axis=-1, keepdims=True)
        m1 = m1 * (1.0 / AW)
        m2 = m2 * (1.0 / AW)
        for g in range(G):
            cs = slice(g * GD, (g + 1) * GD)
            dgv = rstd * (dv_scr[:, cs] - m1 - vh_scr[:, cs] * m2)
            dz_ref[:, AW + g * GD:AW + (g + 1) * GD] = (dgv * dgv_scr[:, cs]).astype(BF16)

        @pl.when(i == nt - 1)
        def _():
            m = _causal_mask()
            for g in range(G):
                gws_ref[g] = jnp.where(m, gwc_acc[g], 0.0)
                gbst_ref[:, g:g + 1] = jnp.sum(gbs_acc[g], axis=-1, keepdims=True)

    tile = lambda w: pl.BlockSpec((tm, w), lambda i: (i, 0))
    whole = lambda *s: pl.BlockSpec(s, lambda i: (0,) * len(s))
    big = lambda dt: pltpu.VMEM((tm, AW), dt)
    return _call(
        main, jobs, name="bwd_a", grid=(nt,), relay_step=relay_step,
        ins=[dx1, z, pp, lnw, lnb, ws, bst, wout],
        in_specs=[tile(D), tile(3 * AW), tile(3 * AW), _VMEM, _VMEM, _VMEM, _VMEM, _VMEM],
        out_shape=[_sds((s_len, 3 * AW), BF16), _sds((1, AW), F32), _sds((1, AW), F32), _sds((G, CH, CH), F32),
                   _sds((CH, G), F32)],
        out_specs=[tile(3 * AW), whole(1, AW), whole(1, AW), whole(G, CH, CH), whole(CH, G)],
        scratch=[pltpu.VMEM((G, CH, CH), BF16), pltpu.VMEM((G, CH, CH), BF16), big(F32), big(F32), big(F32), big(F32),
                 pltpu.VMEM((G, CH, GD), F32), pltpu.VMEM((G, CH, CH), F32)])


def _bwd_a_in(dz, dx1, x, nw, win8, jobs, *, tm, relay_step):
    s_len = x.shape[0]
    nt = s_len // tm

    def main(i, ins, outs, scr):
        dz_ref, dx1_ref, x_ref, nw_ref, win_ref = ins
        gx_ref, gnw_ref = outs

        @pl.when(i == 0)
        def _():
            gnw_ref[...] = jnp.zeros_like(gnw_ref)

        dh = jnp.zeros((tm, D), F32)
        for k in range(NDEV):
            dh = dh + _dot_nt(dz_ref[:, k * CA:(k + 1) * CA], win_ref[k])
        x = x_ref[...]
        r = _rms(x)
        gx_ref[...] = dx1_ref[...] + _rms_bwd(dh, x, r, nw_ref[...])
        gnw_ref[...] += _rowsum(dh * x * r)

        @pl.when(i == nt - 1)
        def _():
            gnw_ref[...] = _direct_sum(gnw_ref[...], *scr)

    tile = lambda w: pl.BlockSpec((tm, w), lambda i: (i, 0))
    return _call(
        main, jobs, name="bwd_a_in", grid=(nt,), relay_step=relay_step, forward_step=nt - 1,
        ins=[dz, dx1, x, nw, win8], in_specs=[tile(3 * AW), tile(D), tile(D), _VMEM, _VMEM],
        out_shape=[_sds((s_len, D), F32), _sds((1, D), F32)],
        out_specs=[tile(D), pl.BlockSpec((1, D), lambda i: (0, 0))], scratch=_direct_sum_scratch((1, D), F32))


def _conv(p8_ref, cs, xb, xm1, xm2, xm3):
    xc = p8_ref[4:5, cs] + p8_ref[3:4, cs] * xb
    xc = xc + p8_ref[0:1, cs] * xm3
    xc = xc + p8_ref[1:2, cs] * xm2
    return xc + p8_ref[2:3, cs] * xm1


def _gates(p8_ref, gcat_ref, hh, xc):
    cs = slice(hh * HD, (hh + 1) * HD)
    pre = _dot(xc.astype(BF16), gcat_ref[hh])
    r = _sigmoid(pre[:, :HD] + p8_ref[5:6, cs])
    ig = _sigmoid(pre[:, HD:] + p8_ref[6:7, cs])
    sp = _softplus_neg(p8_ref[7:8, cs])
    la = (-RG_C) * r * sp
    a = jnp.exp(la)
    half_log = 0.5 * jnp.log(jnp.tanh(-la) * (1.0 + a * a))
    return r, ig, sp, a, jnp.exp(half_log), jnp.exp(-half_log)


def _scan_rows(a_ref, b_ref, out_ref, carry, tm, reverse):
    row = lax.broadcasted_iota(jnp.int32, (SUBLANES, BW), 0)
    ngrp = tm // SUBLANES

    def step(j, cr):
        jj = (ngrp - 1 - j) if reverse else j
        off = pl.multiple_of(jj * SUBLANES, SUBLANES)
        a = a_ref[pl.ds(off, SUBLANES), :]
        b = b_ref[pl.ds(off, SUBLANES), :]
        for sh in (1, 2, 4):
            if reverse:
                a_s = pltpu.roll(a, SUBLANES - sh, 0)
                b_s = pltpu.roll(b, SUBLANES - sh, 0)
                m = row < SUBLANES - sh
            else:
                a_s = pltpu.roll(a, sh, 0)
                b_s = pltpu.roll(b, sh, 0)
                m = row >= sh
            b = jnp.where(m, a * b_s + b, b)
            a = jnp.where(m, a * a_s, a)
        o = b + a * cr
        out_ref[pl.ds(off, SUBLANES), :] = o
        return o[0:1, :] if reverse else o[SUBLANES - 1:SUBLANES, :]

    return lax.fori_loop(0, ngrp, step, carry)


def _fwd_b(x, ya, wout_a, nw, win8, p8, gcat, jobs, *, tm, relay_step):
    s_len = x.shape[0]
    nt = s_len // tm

    def main(i, ins, outs, scr):
        x_ref, ya_ref, wouta_ref, nw_ref, win_ref, p8_ref, gcat_ref = ins
        x1_ref, zb_ref, hs_ref, h1_ref, yb_ref, xc_ref, a_ref, cc_ref, r_ref, ig_ref, m_ref = outs
        xbe_scr, b_scr, k_scr, carry_scr = scr

        @pl.when(i == 0)
        def _():
            xbe_scr[0:SUBLANES, :] = jnp.zeros((SUBLANES, BW), F32)
            carry_scr[...] = jnp.zeros_like(carry_scr)

        x1 = x_ref[...] + _dot(ya_ref[...], wouta_ref[...])
        x1_ref[...] = x1
        h = (x1 * _rms(x1) * nw_ref[...]).astype(BF16)
        h1_ref[...] = h
        for k in range(NDEV):
            zb_ref[:, k * CB:(k + 1) * CB] = _dot(h, win_ref[k])
        xbe_scr[SUBLANES:SUBLANES + tm, :] = zb_ref[:, :BW]
        for hh in range(BH):
            cs = slice(hh * HD, (hh + 1) * HD)
            xc = _conv(p8_ref, cs, xbe_scr[SUBLANES:SUBLANES + tm, cs], xbe_scr[7:7 + tm, cs],
                       xbe_scr[6:6 + tm, cs], xbe_scr[5:5 + tm, cs])
            r, ig, _, a, mult, rm = _gates(p8_ref, gcat_ref, hh, xc)
            ixc = ig * xc
            xc_ref[:, cs] = xc
            a_ref[:, cs] = a
            r_ref[:, cs] = r.astype(BF16)
            ig_ref[:, cs] = ig.astype(BF16)
            m_ref[:, cs] = mult.astype(BF16)
            b_scr[:, cs] = mult * ixc
            k_scr[:, cs] = ixc * (a * a * rm)
        xbe_scr[0:SUBLANES, :] = xbe_scr[tm:tm + SUBLANES, :]
        carry_scr[...] = _scan_rows(a_ref, b_scr, hs_ref, carry_scr[...], tm, False)
        for hh in range(BH):
            cs = slice(hh * HD, (hh + 1) * HD)
            gt = zb_ref[:, BW + hh * HD:BW + (hh + 1) * HD]
            hsv = hs_ref[:, cs]
            yb_ref[:, cs] = (hsv * (gt * _sigmoid(gt))).astype(BF16)
            cc_ref[:, cs] = (hsv - b_scr[:, cs]) - k_scr[:, cs]

    tile = lambda w: pl.BlockSpec((tm, w), lambda i: (i, 0))
    wide = lambda dt: _sds((s_len, BW), dt)
    return _call(
        main, jobs, name="fwd_b", grid=(nt,), relay_step=relay_step,
        ins=[x, ya, wout_a, nw, win8, p8, gcat], in_specs=[tile(D), tile(AW), _VMEM, _VMEM, _VMEM, _VMEM, _VMEM],
        out_shape=[_sds((s_len, D), F32), _sds((s_len, 2 * BW), F32), wide(F32), _sds((s_len, D), BF16), wide(BF16),
                   wide(F32), wide(F32), wide(F32), wide(BF16), wide(BF16), wide(BF16)],
        out_specs=[tile(D), tile(2 * BW), tile(BW), tile(D)] + [tile(BW)] * 7,
        scratch=[pltpu.VMEM((tm + SUBLANES, BW), F32), pltpu.VMEM((tm, BW), F32), pltpu.VMEM((tm, BW), F32),
                 pltpu.VMEM((1, BW), F32)])


def _head(x1, yb, wout, nfw, tgt, *, tm):
    s_len = x1.shape[0]

    def main(i, ins, outs, scr):
        x1_ref, yb_ref, wout_ref, nfw_ref, t_ref = ins
        dx2_ref, dx2b_ref, loss_ref, gnfw_ref = outs

        @pl.when(i == 0)
        def _():
            loss_ref[...] = jnp.zeros_like(loss_ref)
            gnfw_ref[...] = jnp.zeros_like(gnfw_ref)

        x2 = x1_ref[...] + _dot(yb_ref[...], wout_ref[...])
        rf = _rms(x2)
        xn = x2 * rf
        e = xn * nfw_ref[...] - t_ref[...]
        loss_ref[...] += (0.5 / D) * jnp.sum(jnp.sum(e * e, axis=-1, keepdims=True), axis=0, keepdims=True)
        dyf = e * (1.0 / D)
        gnfw_ref[...] += _rowsum(dyf * xn)
        dx2 = _rms_bwd(dyf, x2, rf, nfw_ref[...])
        dx2_ref[...] = dx2
        dx2b_ref[...] = dx2.astype(BF16)

    tile = lambda w: pl.BlockSpec((tm, w), lambda i: (i, 0))
    whole = lambda *s: pl.BlockSpec(s, lambda i: (0,) * len(s))
    (dx2, dx2b, loss, gnfw), _ = _call(
        main, [], name="head", grid=(s_len // tm,),
        ins=[x1, yb, wout, nfw, tgt], in_specs=[tile(D), tile(BW), _VMEM, _VMEM, tile(D)],
        out_shape=[_sds((s_len, D), F32), _sds((s_len, D), BF16), _sds((1, 1), F32), _sds((1, D), F32)],
        out_specs=[tile(D), tile(D), whole(1, 1), whole(1, D)], scratch=[])
    return dx2, dx2b, loss, gnfw


def _bwd_b(dx2, zb, hs, x1, saved, nw, win8, p8, gcat, wout, *, tm):
    s_len = x1.shape[0]
    nt = s_len // tm

    def main(i, ins, outs, scr):
        (dx2_ref, zb_ref, hs_ref, x1_ref, xc_ref, a_ref, cc_ref, r_ref, ig_ref, m_ref,
         nw_ref, win_ref, p8_ref, gcat_ref, wout_ref) = ins
        dx1_ref, dx1b_ref, dzb_ref, gp8_ref, gga_ref, ggx_ref, gnw_ref = outs
        ae_scr, an_scr, dhd_scr, dh_scr, dy_scr, dxce_scr, carry_scr, afirst_scr = scr

        @pl.when(i == 0)
        def _():
            gp8_ref[...] = jnp.zeros_like(gp8_ref)
            gga_ref[...] = jnp.zeros_like(gga_ref)
            ggx_ref[...] = jnp.zeros_like(ggx_ref)
            gnw_ref[...] = jnp.zeros_like(gnw_ref)
            dxce_scr[tm:tm + SUBLANES, :] = jnp.zeros((SUBLANES, BW), F32)
            carry_scr[...] = jnp.zeros_like(carry_scr)
            afirst_scr[...] = jnp.zeros_like(afirst_scr)

        dx2 = dx2_ref[...]
        dy_scr[...] = _dot_nt(dx2.astype(BF16), wout_ref[...])
        for hh in range(BH):
            cs = slice(hh * HD, (hh + 1) * HD)
            gs = slice(BW + hh * HD, BW + (hh + 1) * HD)
            gt = zb_ref[:, gs]
            sig = _sigmoid(gt)
            dy = dy_scr[:, cs]
            dhd_scr[:, cs] = dy * (gt * sig)
            dzb_ref[:, gs] = (dy * hs_ref[:, cs] * (sig * (1.0 + gt * (1.0 - sig)))).astype(BF16)

        ae_scr[0:tm, :] = a_ref[...]
        ae_scr[tm:tm + SUBLANES, :] = jnp.broadcast_to(afirst_scr[...], (SUBLANES, BW))
        an_scr[...] = ae_scr[1:1 + tm, :]
        afirst_scr[...] = ae_scr[0:1, :]
        carry_scr[...] = _scan_rows(an_scr, dhd_scr, dh_scr, carry_scr[...], tm, True)

        for hh in range(BH):
            cs = slice(hh * HD, (hh + 1) * HD)
            dh = dh_scr[:, cs]
            mult = m_ref[:, cs].astype(F32)
            ig = ig_ref[:, cs].astype(F32)
            r = r_ref[:, cs].astype(F32)
            xc = xc_ref[:, cs]
            lam = p8_ref[7:8, cs]
            sp = _softplus_neg(lam)
            dla = dh * cc_ref[:, cs]
            gp8_ref[7:8, cs] += _rowsum(dla * ((-RG_C) * r)) * (-_sigmoid(-lam))
            dpr = dla * ((-RG_C) * sp) * (r * (1.0 - r))
            dpi = dh * mult * xc * (ig * (1.0 - ig))
            gp8_ref[5:6, cs] += _rowsum(dpr)
            gp8_ref[6:7, cs] += _rowsum(dpi)
            dcat = jnp.concatenate([dpr, dpi], axis=1).astype(BF16)
            dxc = dh * mult * ig + _dot_nt(dcat, gcat_ref[hh])
            gg = _dot(xc.T.astype(BF16), dcat)
            gga_ref[hh] += gg[:, :HD]
            ggx_ref[hh] += gg[:, HD:]
            dxce_scr[0:tm, cs] = dxc
            gp8_ref[4:5, cs] += _rowsum(dxc)
        for hh in range(BH):
            cs = slice(hh * HD, (hh + 1) * HD)
            xb = zb_ref[:, cs]
            d0, d1 = dxce_scr[0:tm, cs], dxce_scr[1:1 + tm, cs]
            d2, d3 = dxce_scr[2:2 + tm, cs], dxce_scr[3:3 + tm, cs]
            dzb_ref[:, cs] = (p8_ref[3:4, cs] * d0 + p8_ref[2:3, cs] * d1 + p8_ref[1:2, cs] * d2
                              + p8_ref[0:1, cs] * d3).astype(BF16)
            gp8_ref[3:4, cs] += _rowsum(d0 * xb)
            gp8_ref[2:3, cs] += _rowsum(d1 * xb)
            gp8_ref[1:2, cs] += _rowsum(d2 * xb)
            gp8_ref[0:1, cs] += _rowsum(d3 * xb)
        dxce_scr[tm:tm + SUBLANES, :] = dxce_scr[0:SUBLANES, :]

        dh1 = jnp.zeros((tm, D), F32)
        for k in range(NDEV):
            dh1 = dh1 + _dot_nt(dzb_ref[:, k * CB:(k + 1) * CB], win_ref[k])
        x1 = x1_ref[...]
        r1 = _rms(x1)
        dx1 = dx2 + _rms_bwd(dh1, x1, r1, nw_ref[...])
        dx1_ref[...] = dx1
        dx1b_ref[...] = dx1.astype(BF16)
        gnw_ref[...] += _rowsum(dh1 * x1 * r1)

    tile = lambda w: pl.BlockSpec((tm, w), lambda i: (nt - 1 - i, 0))
    whole = lambda *s: pl.BlockSpec(s, lambda i: (0,) * len(s))
    full = lambda: pltpu.VMEM((tm, BW), F32)
    ext = lambda: pltpu.VMEM((tm + SUBLANES, BW), F32)
    out, _ = _call(
        main, [], name="bwd_b", grid=(nt,),
        ins=[dx2, zb, hs, x1, *saved, nw, win8, p8, gcat, wout],
        in_specs=[tile(D), tile(2 * BW), tile(BW), tile(D)] + [tile(BW)] * 6 + [_VMEM] * 5,
        out_shape=[_sds((s_len, D), F32), _sds((s_len, D), BF16), _sds((s_len, 2 * BW), BF16), _sds((SUBLANES, BW), F32),
                   _sds((BH, HD, HD), F32), _sds((BH, HD, HD), F32), _sds((1, D), F32)],
        out_specs=[tile(D), tile(D), tile(2 * BW), whole(SUBLANES, BW), whole(BH, HD, HD), whole(BH, HD, HD),
                   whole(1, D)],
        scratch=[ext(), full(), full(), full(), full(), ext(), pltpu.VMEM((1, BW), F32), pltpu.VMEM((1, BW), F32)])
    return out


def _transpose_into(dst_ref, src_ref, rows):
    s_len = src_ref.shape[0]
    for r0 in range(0, s_len, rows):
        dst_ref[:, r0:r0 + rows] = src_ref[r0:r0 + rows, :].astype(F32).T.astype(BF16)


def _wgrad(a, b, jobs, *, by_rows, per, name, relay_step=0):
    s_len, m = a.shape
    n = b.shape[1]
    r, cd = (m // NDEV, n) if by_rows else (m, n // NDEV)
    nsteps = NDEV // per
    at_rows = per * r if by_rows else m

    def main(i, ins, outs, scr):
        a_ref, b_ref = ins
        q_ref, acc_ref = outs
        at_scr, stage, mine, land, send_sems, recv_sems = scr
        x, y, c = _place()

        def to_sibling(pi):
            return pltpu.make_async_remote_copy(
                src_ref=stage.at[pi & 1], dst_ref=land.at[pi], send_sem=send_sems.at[pi], recv_sem=recv_sems.at[pi],
                device_id=(x, y, 1 - c), device_id_type=MESH)

        if by_rows:
            _transpose_into(at_scr, a_ref, TRANSPOSE_ROWS)
        else:
            @pl.when(i == 0)
            def _():
                _transpose_into(at_scr, a_ref, TRANSPOSE_ROWS)

        res = _dot(at_scr[...], b_ref[...]).astype(BF16)
        for k in range(per):
            blk = per * i + k
            pi, pc = blk >> 1, blk & 1
            val = res[k * r:(k + 1) * r, :] if by_rows else res

            @pl.when(pc != c)
            def _():
                @pl.when(pi >= 2)
                def _():
                    to_sibling(pi - 2).wait_send()

                stage[pi & 1] = val
                to_sibling(pi).start()

            @pl.when(pc == c)
            def _():
                mine[pi] = val

        @pl.when(i == nsteps - 1)
        def _():
            for p in range(4):
                to_sibling(p).wait_recv()
            to_sibling(2).wait_send()
            to_sibling(3).wait_send()
            _chip_sums(mine, land, q_ref, acc_ref, x, y)

    if by_rows:
        in_specs = [pl.BlockSpec((s_len, at_rows), lambda j: (0, j)), _VMEM]
    else:
        in_specs = [_VMEM, pl.BlockSpec((s_len, cd), lambda j: (0, j))]
    blk_vmem = lambda k: pltpu.VMEM((k, r, cd), BF16)
    (q, acc), job_out = _call(
        main, jobs, name=name, grid=(nsteps,), relay_step=relay_step, ins=[a, b], in_specs=in_specs,
        out_shape=[_sds((NCHIP_OTHER, r, cd), BF16), _sds((r, cd), F32)],
        out_specs=[pl.BlockSpec((NCHIP_OTHER, r, cd), lambda j: (0, 0, 0)), pl.BlockSpec((r, cd), lambda j: (0, 0))],
        scratch=[pltpu.VMEM((at_rows, s_len), BF16), blk_vmem(2), blk_vmem(4), blk_vmem(4),
                 pltpu.SemaphoreType.DMA((4,)), pltpu.SemaphoreType.DMA((4,))])
    return q, acc, job_out


def _wgrad_cols_early(a, b, jobs, *, name, relay_step=0):
    s_len, m = a.shape
    r, cd = m, b.shape[1] // NDEV
    h = r // 2

    def chip_at(pos, base):
        return base ^ (3 - pos)

    def main(i, ins, outs, scr):
        a_ref, b_ref = ins
        q_ref, acc_ref, rel_ref = outs
        at_scr, stage, mine, land, q2_scr, send_sems, recv_sems, via_send, via_recv = scr
        x, y, c = _place()
        base = 2 * x + y
        xn, yn, _ = _other_chips(x, y)
        pos, pc = i >> 1, i & 1
        pi = chip_at(pos, base)

        def to_sibling(chip, slot):
            return pltpu.make_async_remote_copy(
                src_ref=stage.at[slot], dst_ref=land.at[chip], send_sem=send_sems.at[chip],
                recv_sem=recv_sems.at[chip], device_id=(x, y, 1 - c), device_id_type=MESH)

        def via(k):
            return pltpu.make_async_remote_copy(
                src_ref=q2_scr.at[pl.ds(k * h, h)], dst_ref=rel_ref.at[k], send_sem=via_send.at[k],
                recv_sem=via_recv.at[k], device_id=(*(xn, yn)[k], c), device_id_type=MESH)

        @pl.when(i == 0)
        def _():
            _transpose_into(at_scr, a_ref, TRANSPOSE_ROWS)

        res = _dot(at_scr[...], b_ref[...]).astype(BF16)

        @pl.when(pc != c)
        def _():
            @pl.when(pos >= 2)
            def _():
                to_sibling(chip_at(pos - 2, base), pos & 1).wait_send()

            stage[pos & 1] = res
            to_sibling(pi, pos & 1).start()

        @pl.when(pc == c)
        def _():
            mine[pi] = res

        @pl.when(i == 1)
        def _():
            dg = chip_at(0, base)
            to_sibling(dg, 0).wait_recv()
            q2 = (mine[dg].astype(F32) + land[dg].astype(F32)).astype(BF16)
            q2_scr[...] = q2
            q_ref[2] = q2
            via(0).start()
            via(1).start()

        @pl.when(i == NDEV - 1)
        def _():
            for pos_ in (1, 2, 3):
                to_sibling(chip_at(pos_, base), 0).wait_recv()
            to_sibling(chip_at(2, base), 0).wait_send()
            to_sibling(chip_at(3, base), 1).wait_send()
            for k in range(2):
                via(k).wait_recv()
            for k in range(2):
                via(k).wait_send()
            for j, chip in enumerate((base ^ 2, base ^ 1)):
                q_ref[j] = (mine[chip].astype(F32) + land[chip].astype(F32)).astype(BF16)
            acc_ref[...] = mine[base].astype(F32) + land[base].astype(F32)

    def b_block(j):
        base = 2 * lax.axis_index("x") + lax.axis_index("y")
        return (0, 2 * chip_at(j >> 1, base) + (j & 1))

    blk_vmem = lambda k: pltpu.VMEM((k, r, cd), BF16)
    (q, acc, rel), job_out = _call(
        main, jobs, name=name, grid=(NDEV,), relay_step=relay_step, ins=[a, b],
        in_specs=[_VMEM, pl.BlockSpec((s_len, cd), b_block)],
        out_shape=[_sds((NCHIP_OTHER, r, cd), BF16), _sds((r, cd), F32), _sds((2, h, cd), BF16)],
        out_specs=[pl.BlockSpec((NCHIP_OTHER, r, cd), lambda j: (0, 0, 0)), pl.BlockSpec((r, cd), lambda j: (0, 0)), _HBM],
        scratch=[pltpu.VMEM((m, s_len), BF16), blk_vmem(2), blk_vmem(4), blk_vmem(4), pltpu.VMEM((r, cd), BF16),
                 pltpu.SemaphoreType.DMA((4,)), pltpu.SemaphoreType.DMA((4,)), pltpu.SemaphoreType.DMA((2,)),
                 pltpu.SemaphoreType.DMA((2,))])
    return q, acc, rel, job_out


class _ExchangeRest:
    def __init__(self, q, relayed):
        _, r, cd = q.shape
        half = (2, r // 2, cd)
        self.ins, self.in_specs = [q, relayed], [_HBM, _HBM]
        self.out_shape, self.out_specs = [_sds((2, r, cd), q.dtype)], [_HBM]
        self.scratch = [pltpu.VMEM(half, q.dtype), pltpu.VMEM(half, q.dtype), pltpu.VMEM(half, q.dtype),
                        pltpu.SemaphoreType.DMA((4,)), pltpu.SemaphoreType.DMA((4,)), pltpu.SemaphoreType.DMA((4,))]

    def ops(self, ins, outs, scr):
        (q, rel_in), (land,) = ins, outs
        own, rel, comb, send_sems, recv_sems, local_sems = scr
        h = q.shape[1] // 2
        x, y, c = _place()
        xn, yn, _ = _other_chips(x, y)
        h0, h1 = pl.ds(0, h), pl.ds(h, h)

        def remote(k, src, dst, chip):
            return pltpu.make_async_remote_copy(src_ref=src, dst_ref=dst, send_sem=send_sems.at[k],
                                                recv_sem=recv_sems.at[k], device_id=(*chip, c), device_id_type=MESH)

        def sends():
            return [remote(0, q.at[0, h0], land.at[0, h0], xn), remote(1, q.at[1, h1], land.at[1, h1], yn),
                    remote(2, comb.at[0], land.at[1, h0], yn), remote(3, comb.at[1], land.at[0, h1], xn)]

        def loads():
            return [pltpu.make_async_copy(q.at[1, h0], own.at[0], local_sems.at[0]),
                    pltpu.make_async_copy(q.at[0, h1], own.at[1], local_sems.at[1]),
                    pltpu.make_async_copy(rel_in.at[0], rel.at[0], local_sems.at[2]),
                    pltpu.make_async_copy(rel_in.at[1], rel.at[1], local_sems.at[3])]

        def start():
            cps, lds = sends(), loads()
            for ld in lds:
                ld.start()
            cps[0].start()
            cps[1].start()
            for ld in lds:
                ld.wait()
            for k in range(2):
                comb[k] = (own[k].astype(F32) + rel[k].astype(F32)).astype(comb.dtype)
            cps[2].start()
            cps[3].start()

        def finish():
            cps = sends()
            for cp in cps:
                cp.wait_recv()
            for cp in cps:
                cp.wait_send()

        return start, lambda: None, finish


def _adam_math(w, g, m, v):
    m = B1 * m + (1.0 - B1) * g
    v = B2 * v + (1.0 - B2) * (g * g)
    m_hat = m / (1.0 - B1 ** STEP)
    v_hat = v / (1.0 - B2 ** STEP)
    delta = (-LR) * (m_hat / (jnp.sqrt(v_hat) + ADAM_EPS) + WD * w)
    return delta, m, v


def _adam_big(w, acc, land, m, v, name):
    r, cd = w.shape
    rb = ADAM_ROWS if r % ADAM_ROWS == 0 else r // 2
    nland = land.shape[0]

    def body(w_ref, acc_ref, land_ref, m_ref, v_ref, g_ref, d_ref, mo_ref, vo_ref):
        g = acc_ref[...]
        for j in range(nland):
            g = g + land_ref[j].astype(F32)
        g_ref[...] = g
        d_ref[...], mo_ref[...], vo_ref[...] = _adam_math(w_ref[...], g, m_ref[...], v_ref[...])

    blk = pl.BlockSpec((rb, cd), lambda i: (i, 0))
    blk3 = pl.BlockSpec((nland, rb, cd), lambda i: (0, i, 0))
    return pl.pallas_call(
        body, name=name, grid=(r // rb,), in_specs=[blk, blk, blk3, blk, blk], out_specs=[blk] * 4,
        out_shape=[_sds((r, cd), F32)] * 4,
        compiler_params=_params(dimension_semantics=("arbitrary",)),
    )(w, acc, land, m, v)


def _adam_small(groups):
    n = len(groups)

    def body(*refs):
        ins, outs = refs[:4 * n], refs[4 * n:]
        for k in range(n):
            w_ref, g_ref, m_ref, v_ref = ins[4 * k:4 * k + 4]
            d, mo, vo = _adam_math(w_ref[...], g_ref[...], m_ref[...], v_ref[...])
            outs[3 * k][...] = d
            outs[3 * k + 1][...] = mo
            outs[3 * k + 2][...] = vo

    flat = [a for grp in groups for a in grp]
    shapes = [_sds(grp[0].shape, F32) for grp in groups for _ in range(3)]
    res = pl.pallas_call(
        body, name="adam_small", in_specs=[_VMEM] * (4 * n), out_specs=[_VMEM] * (3 * n), out_shape=shapes,
        compiler_params=_params(),
    )(*flat)
    return [tuple(res[3 * k:3 * k + 3]) for k in range(n)]


TM_FWD_A = 256
RELAY_STEP_FWD_A = 2
FORWARD_STEP_FWD_A = 6
RELAY_STEP_FWD_B = 2
TM_BWD_A = 256
RELAY_STEP_BWD_A = 3
TM_BWD_A_IN = 256
RELAY_STEP_BWD_A_IN = 4
RELAY_STEP_WGRAD_A_IN = 2
TM_FWD_B = 256
TM_HEAD = 512
TM_BWD_B = 256


def _pack(parts, rows):
    flat = jnp.concatenate([p.reshape(-1) for p in parts])
    return jnp.pad(flat, (0, NDEV * rows * LANES - flat.shape[0])).reshape(NDEV, rows, LANES)


def _unpack(packed, shapes):
    flat, out, off = packed.reshape(-1), [], 0
    for s in shapes:
        size = 1
        for d in s:
            size *= d
        out.append(flat[off:off + size].reshape(s))
        off += size
    return out


def kernel(x, norm_w, a_w_in, a_ln_w, a_ln_b, a_w_s, a_b_s, a_w_out, b_w_in, b_conv_w, b_conv_b, b_gate_a_w, b_gate_a_b, b_gate_x_w, b_gate_x_b, b_lambda, b_w_out, norm_f_w, loss_target, m_norm_w, m_a_w_in, m_a_ln_w, m_a_ln_b, m_a_w_s, m_a_b_s, m_a_w_out, m_b_w_in, m_b_conv_w, m_b_conv_b, m_b_gate_a_w, m_b_gate_a_b, m_b_gate_x_w, m_b_gate_x_b, m_b_lambda, m_b_w_out, m_norm_f_w, v_norm_w, v_a_w_in, v_a_ln_w, v_a_ln_b, v_a_w_s, v_a_b_s, v_a_w_out, v_b_w_in, v_b_conv_w, v_b_conv_b, v_b_gate_a_w, v_b_gate_a_b, v_b_gate_x_w, v_b_gate_x_b, v_b_lambda, v_b_w_out, v_norm_f_w):
    me = 4 * lax.axis_index("x") + 2 * lax.axis_index("y") + lax.axis_index("c")
    xs, tgt = x[0], loss_target[0]
    nw0, nw1, nfw = norm_w[0:1], norm_w[1:2], norm_f_w.reshape(1, D)
    w_s, bst = a_w_s[0], a_b_s[0].T
    gcat = jnp.concatenate([b_gate_a_w[0], b_gate_x_w[0]], axis=-1).astype(BF16)

    p8_shard = jnp.concatenate([b_conv_w[0], b_conv_b, b_gate_a_b, b_gate_x_b, b_lambda], axis=0)
    (z, h0, ya, pp), ((win_a8, p8_all), (wout_a8, win_b8)) = _fwd_a(
        xs, nw0, a_ln_w, a_ln_b, w_s, bst,
        [_Gather([a_w_in[0], p8_shard], [BF16, F32]), _Gather([a_w_out[0], b_w_in[0]], [BF16, BF16])],
        tm=TM_FWD_A, relay_step=RELAY_STEP_FWD_A)
    p8 = jnp.transpose(p8_all, (1, 0, 2)).reshape(SUBLANES, BW)
    wout_a = wout_a8.reshape(AW, D)
    (x1, zb, hs, h1, yb, *saved_b), ((wout_b8,),) = _fwd_b(
        xs, ya, wout_a, nw1, win_b8, p8, gcat, [_Gather([b_w_out[0]], [BF16])],
        tm=TM_FWD_B, relay_step=RELAY_STEP_FWD_B)
    wout_b = wout_b8.reshape(BW, D)
    dx2, dx2b, loss, g_nfw = _head(x1, yb, wout_b, nfw, tgt, tm=TM_HEAD)

    dx1, dx1b, dzb, g_p8, g_ga, g_gx, g_nw1 = _bwd_b(dx2, zb, hs, x1, saved_b, nw1, win_b8, p8, gcat, wout_b,
                                                     tm=TM_BWD_B)
    q_wout_b, acc_wout_b, _ = _wgrad(yb, dx2b, [], by_rows=True, per=2, name="wgrad_b_out")
    shapes_b = [(1, D), (1, D), (SUBLANES, BW), (1, 1)]
    pack_b = _pack([g_nfw, g_nw1, g_p8, loss], 16)
    small_b = _InChip([g_ga.reshape(NDEV, -1, HD), g_gx.reshape(NDEV, -1, HD), pack_b])
    q_win_b, acc_win_b, (sm_b, (l_wout_b,)) = _wgrad(h1, dzb, [small_b, _Exchange([q_wout_b])], by_rows=False, per=1,
                                                      name="wgrad_b_in")
    qs_b, accs_b = sm_b[:3], sm_b[3:]

    (dz, g_lnw, g_lnb, g_ws, g_bst), (lands_b, (l_win_b,)) = _bwd_a(
        dx1b, z, pp, a_ln_w, a_ln_b, w_s, bst, wout_a, [_Exchange(qs_b), _ExchangeVia(q_win_b)],
        tm=TM_BWD_A, relay_step=RELAY_STEP_BWD_A)
    shapes_a = [(1, AW), (1, AW), (CH, G)]
    pack_a = _pack([g_lnw, g_lnb, g_bst], 8)
    q_wout_a, acc_wout_a, (red_b, sm_a) = _wgrad(
        ya, dx1b, [_SumGather(accs_b, lands_b), _InChip([g_ws, pack_a])], by_rows=True, per=2,
        name="wgrad_a_out", relay_step=1)
    qs_a, accs_a = sm_a[:2], sm_a[2:]
    q_win_a, acc_win_a, rel_a, (lands_a, (l_wout_a,)) = _wgrad_cols_early(
        h0, dz, [_Exchange(qs_a), _ExchangeVia(q_wout_a)], name="wgrad_a_in", relay_step=RELAY_STEP_WGRAD_A_IN)
    (gx, g_nw0), (red_a, (l_win_a,)) = _bwd_a_in(
        dz, dx1, xs, nw0, win_a8, [_SumGather(accs_a, lands_a), _ExchangeRest(q_win_a, rel_a)],
        tm=TM_BWD_A_IN, relay_step=RELAY_STEP_BWD_A_IN)

    r_ga, r_gx, r_pack_b = red_b
    r_nfw, r_nw1, r_p8, loss = _unpack(r_pack_b, shapes_b)
    r_ws, r_pack_a = red_a
    r_lnw, r_lnb, r_bst = _unpack(r_pack_a, shapes_a)
    g_p8 = lax.dynamic_slice_in_dim(r_p8, me * (BW // NDEV), BW // NDEV, axis=1)
    loss = loss[0, 0]

    weights = dict(norm_w=norm_w, a_w_in=a_w_in, a_ln_w=a_ln_w, a_ln_b=a_ln_b, a_w_s=a_w_s, a_b_s=a_b_s, a_w_out=a_w_out,
                   b_w_in=b_w_in, b_conv_w=b_conv_w, b_conv_b=b_conv_b, b_gate_a_w=b_gate_a_w, b_gate_a_b=b_gate_a_b,
                   b_gate_x_w=b_gate_x_w, b_gate_x_b=b_gate_x_b, b_lambda=b_lambda, b_w_out=b_w_out, norm_f_w=norm_f_w)
    mom1 = dict(norm_w=m_norm_w, a_w_in=m_a_w_in, a_ln_w=m_a_ln_w, a_ln_b=m_a_ln_b, a_w_s=m_a_w_s, a_b_s=m_a_b_s,
                a_w_out=m_a_w_out, b_w_in=m_b_w_in, b_conv_w=m_b_conv_w, b_conv_b=m_b_conv_b, b_gate_a_w=m_b_gate_a_w,
                b_gate_a_b=m_b_gate_a_b, b_gate_x_w=m_b_gate_x_w, b_gate_x_b=m_b_gate_x_b, b_lambda=m_b_lambda,
                b_w_out=m_b_w_out, norm_f_w=m_norm_f_w)
    mom2 = dict(norm_w=v_norm_w, a_w_in=v_a_w_in, a_ln_w=v_a_ln_w, a_ln_b=v_a_ln_b, a_w_s=v_a_w_s, a_b_s=v_a_b_s,
                a_w_out=v_a_w_out, b_w_in=v_b_w_in, b_conv_w=v_b_conv_w, b_conv_b=v_b_conv_b, b_gate_a_w=v_b_gate_a_w,
                b_gate_a_b=v_b_gate_a_b, b_gate_x_w=v_b_gate_x_w, b_gate_x_b=v_b_gate_x_b, b_lambda=v_b_lambda,
                b_w_out=v_b_w_out, norm_f_w=v_norm_f_w)
    names = list(weights)

    def as2d(a):
        return a.reshape(-1, a.shape[-1])

    upd, grads = {}, {}
    for k, acc, land in (("a_w_in", acc_win_a, l_win_a), ("a_w_out", acc_wout_a, l_wout_a),
                         ("b_w_in", acc_win_b, l_win_b), ("b_w_out", acc_wout_b, l_wout_b)):
        g, d, mo, vo = _adam_big(as2d(weights[k]), acc, land, as2d(mom1[k]), as2d(mom2[k]), "adam_" + k)
        grads[k] = g[None]
        upd[k] = (d, mo, vo)
    grads.update(
        norm_w=jnp.concatenate([g_nw0, r_nw1], axis=0), a_ln_w=r_lnw, a_ln_b=r_lnb,
        a_w_s=r_ws.reshape(1, G, CH, CH), a_b_s=r_bst.T[None],
        b_conv_w=g_p8[None, 0:4], b_conv_b=g_p8[4:5], b_gate_a_w=r_ga.reshape(1, BH, HD, HD), b_gate_a_b=g_p8[5:6],
        b_gate_x_w=r_gx.reshape(1, BH, HD, HD), b_gate_x_b=g_p8[6:7], b_lambda=g_p8[7:8], norm_f_w=r_nfw.reshape(D))
    small_names = [k for k in names if k not in upd]
    res = _adam_small([(as2d(weights[k]), as2d(grads[k]), as2d(mom1[k]), as2d(mom2[k])) for k in small_names])
    for k, r3 in zip(small_names, res):
        upd[k] = r3
    deltas = [upd[k][0].reshape(weights[k].shape) for k in names]
    new_m = [upd[k][1].reshape(weights[k].shape) for k in names]
    new_v = [upd[k][2].reshape(weights[k].shape) for k in names]
    return (loss, gx[None], *[grads[k] for k in names], *deltas, *new_m, *new_v)
```

```python
import jax
import jax.numpy as jnp
from jax import lax
from jax.experimental import pallas as pl
from jax.experimental.pallas import tpu as pltpu

F32 = jnp.float32
BF16 = jnp.bfloat16
MESH = pl.DeviceIdType.MESH

NDEV = 8
NCHIP_OTHER = 3
D = 1024
AW = 2048
G = 8
GD = AW // G
CH = 128
BW = 1536
BH = 12
HD = BW // BH
CA = 3 * AW // NDEV
CB = 2 * BW // NDEV
RMS_EPS = 1e-6
LN_EPS = 1e-5
RG_C = 8.0
LR, B1, B2, ADAM_EPS, WD, STEP = 0.001, 0.9, 0.999, 1e-08, 0.01, 10
V7X_VMEM_BYTES = 64 * 1024 * 1024
VMEM_LIMIT = V7X_VMEM_BYTES - 8 * 1024 * 1024
SUBLANES = 8
LANES = 128
BF16_ROWS = 16
TRANSPOSE_ROWS = 256
ADAM_ROWS = 512
GELU_C = 0.7978845608028654
GELU_K = 0.044715

_VMEM = pl.BlockSpec(memory_space=pltpu.VMEM)
_HBM = pl.BlockSpec(memory_space=pltpu.HBM)


def _sds(shape, dtype):
    return jax.ShapeDtypeStruct(tuple(shape), dtype)


def _params(**kw):
    return pltpu.CompilerParams(vmem_limit_bytes=VMEM_LIMIT, **kw)


def _gelu_t(z):
    p = 0.5 * jnp.tanh(z * (GELU_C + (GELU_C * GELU_K) * (z * z))) + 0.5
    return z * p, p


def _dgelu(z, p):
    return p * (1.0 + (z * (1.0 - p)) * (2.0 * GELU_C + (6.0 * GELU_C * GELU_K) * (z * z)))


def _sigmoid(v):
    return 0.5 * jnp.tanh(0.5 * v) + 0.5


def _softplus_neg(lam):
    return jnp.maximum(-lam, 0.0) + jnp.log1p(jnp.exp(-jnp.abs(lam)))


def _dot(a, b):
    return jnp.dot(a, b, preferred_element_type=F32)


def _dot_nt(a, b):
    return lax.dot_general(a, b, (((1,), (1,)), ((), ())), preferred_element_type=F32)


def _rowsum(v):
    return jnp.sum(v, axis=0, keepdims=True)


def _causal_mask():
    r = lax.broadcasted_iota(jnp.int32, (CH, CH), 0)
    c = lax.broadcasted_iota(jnp.int32, (CH, CH), 1)
    return r >= c


def _rms(x):
    return lax.rsqrt(jnp.mean(x * x, axis=-1, keepdims=True) + RMS_EPS)


def _rms_bwd(dh, x, r, nw):
    gy = dh * nw
    return r * gy - x * (r * r * r) * jnp.mean(gy * x, axis=-1, keepdims=True)


def _place():
    return lax.axis_index("x"), lax.axis_index("y"), lax.axis_index("c")


def _other_chips(x, y):
    return [(1 - x, y), (x, 1 - y), (1 - x, 1 - y)]


GATHER_SLOTS = 10


def _gather_ops(ins, outs, send_sems, recv_sems, local_sems):
    n = len(ins)
    x, y, c = _place()
    sibling = (x, y, 1 - c)
    xn, yn, dg = _other_chips(x, y)
    split = [ins[i].shape[0] % (2 * BF16_ROWS) == 0 for i in range(n)]

    def blk(chip, core):
        return 4 * chip[0] + 2 * chip[1] + core

    me = blk((x, y), c)

    def part(ref, i, half):
        if half is None:
            return ref
        h = ins[i].shape[0] // 2
        return ref.at[pl.ds(half * h, h)]

    def copy(i, k, block, to, half=None, src=None):
        dst = part(outs[i].at[block], i, half)
        return pltpu.make_async_remote_copy(
            src_ref=dst if src is None else part(src, i, half), dst_ref=dst,
            send_sem=send_sems.at[k, i], recv_sem=recv_sems.at[k, i], device_id=to, device_id_type=MESH)

    def first_copies():
        mine = [pltpu.make_async_copy(ins[i], outs[i].at[me], local_sems.at[i]) for i in range(n)]
        first = []
        for i in range(n):
            first.append(copy(i, 0, me, sibling, src=ins[i]))
            if split[i]:
                first.append(copy(i, 1, me, (*xn, c), 0, ins[i]))
                first.append(copy(i, 3, me, (*yn, c), 1, ins[i]))
                first.append(copy(i, 2, me, (*xn, c), 1, ins[i]))
                first.append(copy(i, 4, me, (*yn, c), 0, ins[i]))
            else:
                first.append(copy(i, 1, me, (*xn, c), None, ins[i]))
                first.append(copy(i, 3, me, (*yn, c), None, ins[i]))
                first.append(copy(i, 5, me, (*dg, c), None, ins[i]))
        return mine, first

    def onward():
        out = []
        for i in range(n):
            if split[i]:
                out.append(copy(i, 5, blk(xn, c), (*yn, c), 0))
                out.append(copy(i, 6, blk(yn, c), (*xn, c), 1))
        return out

    def start():
        mine, first = first_copies()
        for cp in mine + first:
            cp.start()

    def relay():
        sends = onward()
        for i in range(n):
            if split[i]:
                copy(i, 1, blk(xn, c), sibling, 0).wait_recv()
                sends.pop(0).start()
                copy(i, 3, blk(yn, c), sibling, 1).wait_recv()
                sends.pop(0).start()

    def passes():
        return [copy(i, 7 + j, blk(chip, c), sibling) for i in range(n) for j, chip in enumerate((xn, yn, dg))]

    def forward():
        fwd = passes()
        for i in range(n):
            if split[i]:
                copy(i, 2, blk(xn, c), sibling, 1).wait_recv()
                fwd[3 * i].start()
                copy(i, 4, blk(yn, c), sibling, 0).wait_recv()
                fwd[3 * i + 1].start()
                copy(i, 5, blk(dg, c), sibling, 0).wait_recv()
                copy(i, 6, blk(dg, c), sibling, 1).wait_recv()
                fwd[3 * i + 2].start()
            else:
                copy(i, 1, blk(xn, c), sibling).wait_recv()
                fwd[3 * i].start()
                copy(i, 3, blk(yn, c), sibling).wait_recv()
                fwd[3 * i + 1].start()
                copy(i, 5, blk(dg, c), sibling).wait_recv()
                fwd[3 * i + 2].start()

    def finish():
        mine, first = first_copies()
        for i in range(n):
            copy(i, 0, blk((x, y), 1 - c), sibling).wait_recv()
            for j, chip in enumerate((xn, yn, dg)):
                copy(i, 7 + j, blk(chip, 1 - c), sibling).wait_recv()
        for cp in first + passes() + onward():
            cp.wait_send()
        for cp in mine:
            cp.wait()

    return start, relay, forward, finish


def _gather_sems(n):
    return [pltpu.SemaphoreType.DMA((GATHER_SLOTS, n)), pltpu.SemaphoreType.DMA((GATHER_SLOTS, n)),
            pltpu.SemaphoreType.DMA((n,))]


class _Gather:
    def __init__(self, shards, as_dtypes=None):
        n = len(shards)
        dts = [s.dtype for s in shards] if as_dtypes is None else list(as_dtypes)
        self.cast = [jnp.dtype(d) != s.dtype for d, s in zip(dts, shards)]
        self.ins = list(shards)
        self.in_specs = [_VMEM if c else _HBM for c in self.cast]
        self.out_shape = [_sds((NDEV,) + s.shape, d) for s, d in zip(shards, dts)]
        self.out_specs = [_HBM] * n
        self.scratch = [pltpu.VMEM(s.shape, d) for s, d, c in zip(shards, dts, self.cast) if c] + _gather_sems(n)

    def ops(self, ins, outs, scr):
        ncast = sum(self.cast)
        staged = iter(scr[:ncast])
        srcs = [next(staged) if c else ref for c, ref in zip(self.cast, ins)]
        start, relay, forward, finish = _gather_ops(srcs, outs, *scr[ncast:])

        def cast_and_start():
            for c, ref, src in zip(self.cast, ins, srcs):
                if c:
                    src[...] = ref[...].astype(src.dtype)
            start()

        return cast_and_start, relay, forward, finish


class _Exchange:
    def __init__(self, qs):
        n = len(qs)
        self.ins, self.in_specs = list(qs), [_HBM] * n
        self.out_shape = [_sds(q.shape, q.dtype) for q in qs]
        self.out_specs = [_HBM] * n
        self.scratch = [pltpu.SemaphoreType.DMA((NCHIP_OTHER, n)), pltpu.SemaphoreType.DMA((NCHIP_OTHER, n))]

    def ops(self, ins, outs, scr):
        send_sems, recv_sems = scr
        n = len(ins)
        x, y, c = _place()
        chips = _other_chips(x, y)

        def copies():
            return [pltpu.make_async_remote_copy(
                src_ref=ins[i].at[j], dst_ref=outs[i].at[j], send_sem=send_sems.at[j, i],
                recv_sem=recv_sems.at[j, i], device_id=(*chips[j], c), device_id_type=MESH)
                for i in range(n) for j in range(NCHIP_OTHER)]

        def start():
            for cp in copies():
                cp.start()

        def finish():
            cps = copies()
            for cp in cps:
                cp.wait_recv()
            for cp in cps:
                cp.wait_send()

        return start, lambda: None, finish


class _ExchangeVia:
    def __init__(self, q):
        _, r, cd = q.shape
        half = (2, r // 2, cd)
        self.ins, self.in_specs = [q], [_HBM]
        self.out_shape, self.out_specs = [_sds((2, r, cd), q.dtype)], [_HBM]
        self.scratch = [pltpu.VMEM(half, q.dtype), pltpu.VMEM(half, q.dtype), pltpu.VMEM(half, q.dtype),
                        pltpu.SemaphoreType.DMA((6,)), pltpu.SemaphoreType.DMA((6,)), pltpu.SemaphoreType.DMA((2,))]

    def ops(self, ins, outs, scr):
        (q,), (land,) = ins, outs
        relayed, own, comb, send_sems, recv_sems, local_sems = scr
        h = q.shape[1] // 2
        x, y, c = _place()
        xn, yn, _ = _other_chips(x, y)
        h0, h1 = pl.ds(0, h), pl.ds(h, h)

        def remote(k, src, dst, chip):
            return pltpu.make_async_remote_copy(src_ref=src, dst_ref=dst, send_sem=send_sems.at[k],
                                                recv_sem=recv_sems.at[k], device_id=(*chip, c), device_id_type=MESH)

        def via():
            return [remote(2, q.at[2, h0], relayed.at[0], xn), remote(3, q.at[2, h1], relayed.at[1], yn)]

        def direct():
            return [remote(0, q.at[0, h0], land.at[0, h0], xn), remote(1, q.at[1, h1], land.at[1, h1], yn)]

        def second():
            return [remote(4, comb.at[0], land.at[1, h0], yn), remote(5, comb.at[1], land.at[0, h1], xn)]

        def mine():
            return [pltpu.make_async_copy(q.at[1, h0], own.at[0], local_sems.at[0]),
                    pltpu.make_async_copy(q.at[0, h1], own.at[1], local_sems.at[1])]

        def start():
            for cp in via() + direct() + mine():
                cp.start()

        def relay():
            arrived, loaded, onward = via(), mine(), second()
            for k in range(2):
                arrived[k].wait_recv()
                loaded[k].wait()
                comb[k] = (own[k].astype(F32) + relayed[k].astype(F32)).astype(comb.dtype)
                onward[k].start()

        def finish():
            landing = direct() + second()
            for cp in landing:
                cp.wait_recv()
            for cp in via() + landing:
                cp.wait_send()

        return start, relay, finish


class _SumGather:
    def __init__(self, accs, lands):
        n = len(accs)
        self.n = n
        self.ins, self.in_specs = list(accs) + list(lands), [_VMEM] * (2 * n)
        self.out_shape = [_sds((NDEV,) + a.shape, a.dtype) for a in accs]
        self.out_specs = [_HBM] * n
        self.scratch = [pltpu.VMEM(a.shape, a.dtype) for a in accs] + _gather_sems(n)

    def ops(self, ins, outs, scr):
        n = self.n
        accs, lands, mine = ins[:n], ins[n:], scr[:n]
        g_start, relay, forward, finish = _gather_ops(mine, outs, *scr[n:])

        def start():
            for i in range(n):
                mine[i][...] = accs[i][...] + lands[i][0] + lands[i][1] + lands[i][2]
            g_start()

        return start, relay, forward, finish


def _call(main, jobs, *, name, grid, ins, in_specs, out_shape, out_specs, scratch, relay_step=0, first=0,
          prologue=None, forward_step=None):
    (nsteps,) = grid
    n_in, n_out, n_scr = len(ins), len(out_shape), len(scratch)

    def body(*refs):
        pos = [0]

        def take(k):
            r = refs[pos[0]:pos[0] + k]
            pos[0] += k
            return r

        m_in = take(n_in)
        j_in = [take(len(j.ins)) for j in jobs]
        m_out = take(n_out)
        j_out = [take(len(j.out_shape)) for j in jobs]
        m_scr = take(n_scr)
        j_scr = [take(len(j.scratch)) for j in jobs]
        ops = [_four(j.ops(a, b, s)) for j, a, b, s in zip(jobs, j_in, j_out, j_scr)]
        i = pl.program_id(0)

        if ops:
            @pl.when(i == 0)
            def _():
                for o in ops[:first]:
                    o[0]()
                for o in ops[:first]:
                    o[1]()
                for o in ops[first:]:
                    o[0]()
                for o in ops[:first]:
                    o[2]()
                for o in ops[:first]:
                    o[3]()
                if prologue is not None:
                    prologue(j_out[:first], m_scr)

        main(i, m_in, m_out, m_scr)

        forward_at = max(relay_step, nsteps - 2) if forward_step is None else min(forward_step, nsteps - 1)
        for stage, at in ((1, min(relay_step, nsteps - 1)), (2, forward_at), (3, nsteps - 1)):
            if ops[first:]:
                @pl.when(i == at)
                def _():
                    for o in ops[first:]:
                        o[stage]()

    res = pl.pallas_call(
        body, name=name, grid=grid,
        in_specs=list(in_specs) + [s for j in jobs for s in j.in_specs],
        out_specs=list(out_specs) + [s for j in jobs for s in j.out_specs],
        out_shape=list(out_shape) + [s for j in jobs for s in j.out_shape],
        scratch_shapes=list(scratch) + [s for j in jobs for s in j.scratch],
        compiler_params=_params(dimension_semantics=("arbitrary",)),
    )(*ins, *[a for j in jobs for a in j.ins])
    main_out, rest, job_out = res[:n_out], res[n_out:], []
    for j in jobs:
        k = len(j.out_shape)
        job_out.append(rest[:k])
        rest = rest[k:]
    return main_out, job_out


def _four(ops):
    return ops if len(ops) == 4 else (ops[0], ops[1], lambda: None, ops[2])


class _InChip:
    def __init__(self, ps):
        n = len(ps)
        self.n = n
        blk = [p.shape[1:] for p in ps]
        self.ins, self.in_specs = list(ps), [_HBM] * n
        self.out_shape = [_sds((NCHIP_OTHER,) + b, p.dtype) for b, p in zip(blk, ps)] + [_sds(b, F32) for b in blk]
        self.out_specs = [_VMEM] * (2 * n)
        self.scratch = ([pltpu.VMEM((4,) + b, p.dtype) for b, p in zip(blk, ps)] * 2
                        + [pltpu.SemaphoreType.DMA((4, n))] * 3)

    def ops(self, ins, outs, scr):
        n = self.n
        q_refs, acc_refs = outs[:n], outs[n:]
        mines, lands = scr[:n], scr[n:2 * n]
        send_sems, recv_sems, local_sems = scr[2 * n:]
        x, y, c = _place()
        sibling = (x, y, 1 - c)

        def copies():
            out = []
            for i in range(n):
                for pi in range(4):
                    loc = pltpu.make_async_copy(ins[i].at[2 * pi + c], mines[i].at[pi], local_sems.at[pi, i])
                    cp = pltpu.make_async_remote_copy(
                        src_ref=ins[i].at[2 * pi + (1 - c)], dst_ref=lands[i].at[pi],
                        send_sem=send_sems.at[pi, i], recv_sem=recv_sems.at[pi, i],
                        device_id=sibling, device_id_type=MESH)
                    out.append((loc, cp))
            return out

        def start():
            for loc, cp in copies():
                loc.start()
                cp.start()

        def finish():
            pairs = copies()
            for loc, cp in pairs:
                loc.wait()
                cp.wait_recv()
            for i in range(n):
                _chip_sums(mines[i], lands[i], q_refs[i], acc_refs[i], x, y)
            for _, cp in pairs:
                cp.wait_send()

        return start, lambda: None, finish


def _chip_sums(mine, land, q_ref, acc_ref, x, y):
    for j, (qx, qy) in enumerate(_other_chips(x, y)):
        qi = 2 * qx + qy
        q_ref[j] = (mine[qi].astype(F32) + land[qi].astype(F32)).astype(q_ref.dtype)
    mi = 2 * x + y
    acc_ref[...] = mine[mi].astype(F32) + land[mi].astype(F32)


def _direct_sum(v, buf, send_sems, recv_sems):
    x, y, c = _place()
    me = 4 * x + 2 * y + c
    buf[me] = v
    cps = []
    for k in range(1, NDEV):
        fx, fy, fc = (k >> 2) & 1, (k >> 1) & 1, k & 1
        peer = ((1 - x) if fx else x, (1 - y) if fy else y, (1 - c) if fc else c)
        cps.append((peer, pltpu.make_async_remote_copy(
            src_ref=buf.at[me], dst_ref=buf.at[me], send_sem=send_sems.at[k - 1], recv_sem=recv_sems.at[k - 1],
            device_id=peer, device_id_type=MESH)))
    for _, cp in cps:
        cp.start()
    for k, (peer, _) in enumerate(cps):
        theirs = 4 * peer[0] + 2 * peer[1] + peer[2]
        pltpu.make_async_remote_copy(
            src_ref=buf.at[theirs], dst_ref=buf.at[theirs], send_sem=send_sems.at[k], recv_sem=recv_sems.at[k],
            device_id=peer, device_id_type=MESH).wait_recv()
    acc = buf[0]
    for j in range(1, NDEV):
        acc = acc + buf[j]
    for _, cp in cps:
        cp.wait_send()
    return acc


def _direct_sum_scratch(shape, dtype):
    return [pltpu.VMEM((NDEV,) + tuple(shape), dtype), pltpu.SemaphoreType.DMA((NDEV - 1,)),
            pltpu.SemaphoreType.DMA((NDEV - 1,))]


def _fwd_a(x, nw, lnw, lnb, ws, bst, jobs, *, tm, relay_step):
    s_len = x.shape[0]
    nt = s_len // tm
    nch = tm // CH

    def main(i, ins, outs, scr):
        x_ref, nw_ref, lnw_ref, lnb_ref, ws_ref, bst_ref = ins
        z_ref, h_ref, y_ref, pp_ref = outs
        wc_scr, gv_scr, win_ref = scr

        @pl.when(i == 0)
        def _():
            m = _causal_mask()
            for g in range(G):
                wc_scr[g] = jnp.where(m, ws_ref[g], 0.0).astype(BF16)

        x = x_ref[...]
        h = (x * _rms(x) * nw_ref[...]).astype(BF16)
        h_ref[...] = h
        for k in range(NDEV):
            z_ref[:, k * CA:(k + 1) * CA] = _dot(h, win_ref[k])

        ssum = jnp.zeros((tm, 1), F32)
        for g in range(G):
            vs = slice(AW + g * GD, AW + (g + 1) * GD)
            gv, pv = _gelu_t(z_ref[:, vs])
            pp_ref[:, vs] = pv.astype(BF16)
            gv_scr[:, g * GD:(g + 1) * GD] = gv
            ssum = ssum + jnp.sum(gv, axis=-1, keepdims=True)
        mu = ssum * (1.0 / AW)
        vsum = jnp.zeros((tm, 1), F32)
        for g in range(G):
            dlt = gv_scr[:, g * GD:(g + 1) * GD] - mu
            vsum = vsum + jnp.sum(dlt * dlt, axis=-1, keepdims=True)
        rstd = lax.rsqrt(vsum * (1.0 / AW) + LN_EPS)

        for g in range(G):
            cs = slice(g * GD, (g + 1) * GD)
            gs = slice(2 * AW + g * GD, 2 * AW + (g + 1) * GD)
            v = (gv_scr[:, cs] - mu) * rstd * lnw_ref[:, cs] + lnb_ref[:, cs]
            vb = v.astype(BF16)
            u, pu = _gelu_t(z_ref[:, cs])
            pp_ref[:, cs] = pu.astype(BF16)
            zg = z_ref[:, gs]
            sig = _sigmoid(zg)
            pp_ref[:, gs] = sig.astype(BF16)
            sg = zg * sig
            for n in range(nch):
                rs = slice(n * CH, (n + 1) * CH)
                s = _dot(wc_scr[g], vb[rs, :]) + bst_ref[:, g:g + 1]
                y_ref[rs, cs] = (u[rs, :] * s * sg[rs, :]).astype(BF16)

    tile = lambda w: pl.BlockSpec((tm, w), lambda i: (i, 0))
    return _call(
        main, jobs, name="fwd_a", grid=(nt,), relay_step=relay_step, first=1, forward_step=FORWARD_STEP_FWD_A,
        prologue=lambda gathered, scr: pltpu.sync_copy(gathered[0][0], scr[2]),
        ins=[x, nw, lnw, lnb, ws, bst], in_specs=[tile(D), _VMEM, _VMEM, _VMEM, _VMEM, _VMEM],
        out_shape=[_sds((s_len, 3 * AW), F32), _sds((s_len, D), BF16), _sds((s_len, AW), BF16),
                   _sds((s_len, 3 * AW), BF16)],
        out_specs=[tile(3 * AW), tile(D), tile(AW), tile(3 * AW)],
        scratch=[pltpu.VMEM((G, CH, CH), BF16), pltpu.VMEM((tm, AW), F32), pltpu.VMEM((NDEV, D, CA), BF16)])


def _bwd_a(dx1, z, pp, lnw, lnb, ws, bst, wout, jobs, *, tm, relay_step):
    s_len = dx1.shape[0]
    nt = s_len // tm
    nch = tm // CH

    def main(i, ins, outs, scr):
        dx1_ref, z_ref, pp_ref, lnw_ref, lnb_ref, ws_ref, bst_ref, wout_ref = ins
        dz_ref, glnw_ref, glnb_ref, gws_ref, gbst_ref = outs
        wc_scr, wct_scr, vh_scr, dgv_scr, dy_scr, dv_scr, gbs_acc, gwc_acc = scr

        @pl.when(i == 0)
        def _():
            m = _causal_mask()
            for g in range(G):
                wm = jnp.where(m, ws_ref[g], 0.0)
                wc_scr[g] = wm.astype(BF16)
                wct_scr[g] = wm.T.astype(BF16)
            glnw_ref[...] = jnp.zeros_like(glnw_ref)
            glnb_ref[...] = jnp.zeros_like(glnb_ref)
            gbs_acc[...] = jnp.zeros_like(gbs_acc)
            gwc_acc[...] = jnp.zeros_like(gwc_acc)

        dy_scr[...] = _dot_nt(dx1_ref[...], wout_ref[...])

        ssum = jnp.zeros((tm, 1), F32)
        for g in range(G):
            cs = slice(g * GD, (g + 1) * GD)
            vs = slice(AW + g * GD, AW + (g + 1) * GD)
            zv = z_ref[:, vs]
            pv = pp_ref[:, vs].astype(F32)
            gv = zv * pv
            vh_scr[:, cs] = gv
            dgv_scr[:, cs] = _dgelu(zv, pv)
            ssum = ssum + jnp.sum(gv, axis=-1, keepdims=True)
        mu = ssum * (1.0 / AW)
        vsum = jnp.zeros((tm, 1), F32)
        for g in range(G):
            dlt = vh_scr[:, g * GD:(g + 1) * GD] - mu
            vsum = vsum + jnp.sum(dlt * dlt, axis=-1, keepdims=True)
        rstd = lax.rsqrt(vsum * (1.0 / AW) + LN_EPS)

        m1 = jnp.zeros((tm, 1), F32)
        m2 = jnp.zeros((tm, 1), F32)
        for g in range(G):
            cs = slice(g * GD, (g + 1) * GD)
            gs = slice(2 * AW + g * GD, 2 * AW + (g + 1) * GD)
            vhat = (vh_scr[:, cs] - mu) * rstd
            vh_scr[:, cs] = vhat
            vb = (vhat * lnw_ref[:, cs] + lnb_ref[:, cs]).astype(BF16)
            zu = z_ref[:, cs]
            tu = pp_ref[:, cs].astype(F32)
            u = zu * tu
            zg = z_ref[:, gs]
            sig = pp_ref[:, gs].astype(F32)
            sg = zg * sig
            dy = dy_scr[:, cs]
            dsf = dy * u * sg
            dsb = dsf.astype(BF16)
            dvs = []
            for n in range(nch):
                rs = slice(n * CH, (n + 1) * CH)
                s = _dot(wc_scr[g], vb[rs, :]) + bst_ref[:, g:g + 1]
                dys = dy[rs, :] * s
                dz_ref[rs, cs] = (dys * sg[rs, :] * _dgelu(zu[rs, :], tu[rs, :])).astype(BF16)
                dz_ref[rs, gs] = (dys * u[rs, :] * (sig[rs, :] * (1.0 + zg[rs, :] * (1.0 - sig[rs, :])))).astype(BF16)
                gbs_acc[g] += dsf[rs, :]
                gwc_acc[g] += _dot_nt(dsb[rs, :], vb[rs, :])
                dvs.append(_dot(wct_scr[g], dsb[rs, :]))
            dv = jnp.concatenate(dvs, axis=0) if nch > 1 else dvs[0]
            glnw_ref[:, cs] += _rowsum(dv * vhat)
            glnb_ref[:, cs] += _rowsum(dv)
            dvh = dv * lnw_ref[:, cs]
            dv_scr[:, cs] = dvh
            m1 = m1 + jnp.sum(dvh, axis=-1, keepdims=True)
            m2 = m2 + jnp.sum(dvh * vhat, axis=-1, keepdims=True)
        m1 = m1 * (1.0 / AW)
        m2 = m2 * (1.0 / AW)
        for g in range(G):
            cs = slice(g * GD, (g + 1) * GD)
            dgv = rstd * (dv_scr[:, cs] - m1 - vh_scr[:, cs] * m2)
            dz_ref[:, AW + g * GD:AW + (g + 1) * GD] = (dgv * dgv_scr[:, cs]).astype(BF16)

        @pl.when(i == nt - 1)
        def _():
            m = _causal_mask()
            for g in range(G):
                gws_ref[g] = jnp.where(m, gwc_acc[g], 0.0)
                gbst_ref[:, g:g + 1] = jnp.sum(gbs_acc[g], axis=-1, keepdims=True)

    tile = lambda w: pl.BlockSpec((tm, w), lambda i: (i, 0))
    whole = lambda *s: pl.BlockSpec(s, lambda i: (0,) * len(s))
    big = lambda dt: pltpu.VMEM((tm, AW), dt)
    return _call(
        main, jobs, name="bwd_a", grid=(nt,), relay_step=relay_step,
        ins=[dx1, z, pp, lnw, lnb, ws, bst, wout],
        in_specs=[tile(D), tile(3 * AW), tile(3 * AW), _VMEM, _VMEM, _VMEM, _VMEM, _VMEM],
        out_shape=[_sds((s_len, 3 * AW), BF16), _sds((1, AW), F32), _sds((1, AW), F32), _sds((G, CH, CH), F32),
                   _sds((CH, G), F32)],
        out_specs=[tile(3 * AW), whole(1, AW), whole(1, AW), whole(G, CH, CH), whole(CH, G)],
        scratch=[pltpu.VMEM((G, CH, CH), BF16), pltpu.VMEM((G, CH, CH), BF16), big(F32), big(F32), big(F32), big(F32),
                 pltpu.VMEM((G, CH, GD), F32), pltpu.VMEM((G, CH, CH), F32)])


def _bwd_a_in(dz, dx1, x, nw, win8, jobs, *, tm, relay_step):
    s_len = x.shape[0]
    nt = s_len // tm

    def main(i, ins, outs, scr):
        dz_ref, dx1_ref, x_ref, nw_ref, win_ref = ins
        gx_ref, gnw_ref = outs

        @pl.when(i == 0)
        def _():
            gnw_ref[...] = jnp.zeros_like(gnw_ref)

        dh = jnp.zeros((tm, D), F32)
        for k in range(NDEV):
            dh = dh + _dot_nt(dz_ref[:, k * CA:(k + 1) * CA], win_ref[k])
        x = x_ref[...]
        r = _rms(x)
        gx_ref[...] = dx1_ref[...] + _rms_bwd(dh, x, r, nw_ref[...])
        gnw_ref[...] += _rowsum(dh * x * r)

        @pl.when(i == nt - 1)
        def _():
            gnw_ref[...] = _direct_sum(gnw_ref[...], *scr)

    tile = lambda w: pl.BlockSpec((tm, w), lambda i: (i, 0))
    return _call(
        main, jobs, name="bwd_a_in", grid=(nt,), relay_step=relay_step, forward_step=nt - 1,
        ins=[dz, dx1, x, nw, win8], in_specs=[tile(3 * AW), tile(D), tile(D), _VMEM, _VMEM],
        out_shape=[_sds((s_len, D), F32), _sds((1, D), F32)],
        out_specs=[tile(D), pl.BlockSpec((1, D), lambda i: (0, 0))], scratch=_direct_sum_scratch((1, D), F32))


def _conv(p8_ref, cs, xb, xm1, xm2, xm3):
    xc = p8_ref[4:5, cs] + p8_ref[3:4, cs] * xb
    xc = xc + p8_ref[0:1, cs] * xm3
    xc = xc + p8_ref[1:2, cs] * xm2
    return xc + p8_ref[2:3, cs] * xm1


def _gates(p8_ref, gcat_ref, hh, xc):
    cs = slice(hh * HD, (hh + 1) * HD)
    pre = _dot(xc.astype(BF16), gcat_ref[hh])
    r = _sigmoid(pre[:, :HD] + p8_ref[5:6, cs])
    ig = _sigmoid(pre[:, HD:] + p8_ref[6:7, cs])
    sp = _softplus_neg(p8_ref[7:8, cs])
    la = (-RG_C) * r * sp
    a = jnp.exp(la)
    half_log = 0.5 * jnp.log(jnp.tanh(-la) * (1.0 + a * a))
    return r, ig, sp, a, jnp.exp(half_log), jnp.exp(-half_log)


def _scan_rows(a_ref, b_ref, out_ref, carry, tm, reverse):
    row = lax.broadcasted_iota(jnp.int32, (SUBLANES, BW), 0)
    ngrp = tm // SUBLANES

    def step(j, cr):
        jj = (ngrp - 1 - j) if reverse else j
        off = pl.multiple_of(jj * SUBLANES, SUBLANES)
        a = a_ref[pl.ds(off, SUBLANES), :]
        b = b_ref[pl.ds(off, SUBLANES), :]
        for sh in (1, 2, 4):
            if reverse:
                a_s = pltpu.roll(a, SUBLANES - sh, 0)
                b_s = pltpu.roll(b, SUBLANES - sh, 0)
                m = row < SUBLANES - sh
            else:
                a_s = pltpu.roll(a, sh, 0)
                b_s = pltpu.roll(b, sh, 0)
                m = row >= sh
            b = jnp.where(m, a * b_s + b, b)
            a = jnp.where(m, a * a_s, a)
        o = b + a * cr
        out_ref[pl.ds(off, SUBLANES), :] = o
        return o[0:1, :] if reverse else o[SUBLANES - 1:SUBLANES, :]

    return lax.fori_loop(0, ngrp, step, carry)


def _fwd_b(x, ya, wout_a, nw, win8, p8, gcat, jobs, *, tm, relay_step):
    s_len = x.shape[0]
    nt = s_len // tm

    def main(i, ins, outs, scr):
        x_ref, ya_ref, wouta_ref, nw_ref, win_ref, p8_ref, gcat_ref = ins
        x1_ref, zb_ref, hs_ref, h1_ref, yb_ref, xc_ref, a_ref, cc_ref, r_ref, ig_ref, m_ref = outs
        xbe_scr, b_scr, k_scr, carry_scr = scr

        @pl.when(i == 0)
        def _():
            xbe_scr[0:SUBLANES, :] = jnp.zeros((SUBLANES, BW), F32)
            carry_scr[...] = jnp.zeros_like(carry_scr)

        x1 = x_ref[...] + _dot(ya_ref[...], wouta_ref[...])
        x1_ref[...] = x1
        h = (x1 * _rms(x1) * nw_ref[...]).astype(BF16)
        h1_ref[...] = h
        for k in range(NDEV):
            zb_ref[:, k * CB:(k + 1) * CB] = _dot(h, win_ref[k])
        xbe_scr[SUBLANES:SUBLANES + tm, :] = zb_ref[:, :BW]
        for hh in range(BH):
            cs = slice(hh * HD, (hh + 1) * HD)
            xc = _conv(p8_ref, cs, xbe_scr[SUBLANES:SUBLANES + tm, cs], xbe_scr[7:7 + tm, cs],
                       xbe_scr[6:6 + tm, cs], xbe_scr[5:5 + tm, cs])
            r, ig, _, a, mult, rm = _gates(p8_ref, gcat_ref, hh, xc)
            ixc = ig * xc
            xc_ref[:, cs] = xc
            a_ref[:, cs] = a
            r_ref[:, cs] = r.astype(BF16)
            ig_ref[:, cs] = ig.astype(BF16)
            m_ref[:, cs] = mult.astype(BF16)
            b_scr[:, cs] = mult * ixc
            k_scr[:, cs] = ixc * (a * a * rm)
        xbe_scr[0:SUBLANES, :] = xbe_scr[tm:tm + SUBLANES, :]
        carry_scr[...] = _scan_rows(a_ref, b_scr, hs_ref, carry_scr[...], tm, False)
        for hh in range(BH):
            cs = slice(hh * HD, (hh + 1) * HD)
            gt = zb_ref[:, BW + hh * HD:BW + (hh + 1) * HD]
            hsv = hs_ref[:, cs]
            yb_ref[:, cs] = (hsv * (gt * _sigmoid(gt))).astype(BF16)
            cc_ref[:, cs] = (hsv - b_scr[:, cs]) - k_scr[:, cs]

    tile = lambda w: pl.BlockSpec((tm, w), lambda i: (i, 0))
    wide = lambda dt: _sds((s_len, BW), dt)
    return _call(
        main, jobs, name="fwd_b", grid=(nt,), relay_step=relay_step,
        ins=[x, ya, wout_a, nw, win8, p8, gcat], in_specs=[tile(D), tile(AW), _VMEM, _VMEM, _VMEM, _VMEM, _VMEM],
        out_shape=[_sds((s_len, D), F32), _sds((s_len, 2 * BW), F32), wide(F32), _sds((s_len, D), BF16), wide(BF16),
                   wide(F32), wide(F32), wide(F32), wide(BF16), wide(BF16), wide(BF16)],
        out_specs=[tile(D), tile(2 * BW), tile(BW), tile(D)] + [tile(BW)] * 7,
        scratch=[pltpu.VMEM((tm + SUBLANES, BW), F32), pltpu.VMEM((tm, BW), F32), pltpu.VMEM((tm, BW), F32),
                 pltpu.VMEM((1, BW), F32)])


def _head(x1, yb, wout, nfw, tgt, *, tm):
    s_len = x1.shape[0]

    def main(i, ins, outs, scr):
        x1_ref, yb_ref, wout_ref, nfw_ref, t_ref = ins
        dx2_ref, dx2b_ref, loss_ref, gnfw_ref = outs

        @pl.when(i == 0)
        def _():
            loss_ref[...] = jnp.zeros_like(loss_ref)
            gnfw_ref[...] = jnp.zeros_like(gnfw_ref)

        x2 = x1_ref[...] + _dot(yb_ref[...], wout_ref[...])
        rf = _rms(x2)
        xn = x2 * rf
        e = xn * nfw_ref[...] - t_ref[...]
        loss_ref[...] += (0.5 / D) * jnp.sum(jnp.sum(e * e, axis=-1, keepdims=True), axis=0, keepdims=True)
        dyf = e * (1.0 / D)
        gnfw_ref[...] += _rowsum(dyf * xn)
        dx2 = _rms_bwd(dyf, x2, rf, nfw_ref[...])
        dx2_ref[...] = dx2
        dx2b_ref[...] = dx2.astype(BF16)

    tile = lambda w: pl.BlockSpec((tm, w), lambda i: (i, 0))
    whole = lambda *s: pl.BlockSpec(s, lambda i: (0,) * len(s))
    (dx2, dx2b, loss, gnfw), _ = _call(
        main, [], name="head", grid=(s_len // tm,),
        ins=[x1, yb, wout, nfw, tgt], in_specs=[tile(D), tile(BW), _VMEM, _VMEM, tile(D)],
        out_shape=[_sds((s_len, D), F32), _sds((s_len, D), BF16), _sds((1, 1), F32), _sds((1, D), F32)],
        out_specs=[tile(D), tile(D), whole(1, 1), whole(1, D)], scratch=[])
    return dx2, dx2b, loss, gnfw


def _bwd_b(dx2, zb, hs, x1, saved, nw, win8, p8, gcat, wout, *, tm):
    s_len = x1.shape[0]
    nt = s_len // tm

    def main(i, ins, outs, scr):
        (dx2_ref, zb_ref, hs_ref, x1_ref, xc_ref, a_ref, cc_ref, r_ref, ig_ref, m_ref,
         nw_ref, win_ref, p8_ref, gcat_ref, wout_ref) = ins
        dx1_ref, dx1b_ref, dzb_ref, gp8_ref, gga_ref, ggx_ref, gnw_ref = outs
        ae_scr, an_scr, dhd_scr, dh_scr, dy_scr, dxce_scr, carry_scr, afirst_scr = scr

        @pl.when(i == 0)
        def _():
            gp8_ref[...] = jnp.zeros_like(gp8_ref)
            gga_ref[...] = jnp.zeros_like(gga_ref)
            ggx_ref[...] = jnp.zeros_like(ggx_ref)
            gnw_ref[...] = jnp.zeros_like(gnw_ref)
            dxce_scr[tm:tm + SUBLANES, :] = jnp.zeros((SUBLANES, BW), F32)
            carry_scr[...] = jnp.zeros_like(carry_scr)
            afirst_scr[...] = jnp.zeros_like(afirst_scr)

        dx2 = dx2_ref[...]
        dy_scr[...] = _dot_nt(dx2.astype(BF16), wout_ref[...])
        for hh in range(BH):
            cs = slice(hh * HD, (hh + 1) * HD)
            gs = slice(BW + hh * HD, BW + (hh + 1) * HD)
            gt = zb_ref[:, gs]
            sig = _sigmoid(gt)
            dy = dy_scr[:, cs]
            dhd_scr[:, cs] = dy * (gt * sig)
            dzb_ref[:, gs] = (dy * hs_ref[:, cs] * (sig * (1.0 + gt * (1.0 - sig)))).astype(BF16)

        ae_scr[0:tm, :] = a_ref[...]
        ae_scr[tm:tm + SUBLANES, :] = jnp.broadcast_to(afirst_scr[...], (SUBLANES, BW))
        an_scr[...] = ae_scr[1:1 + tm, :]
        afirst_scr[...] = ae_scr[0:1, :]
        carry_scr[...] = _scan_rows(an_scr, dhd_scr, dh_scr, carry_scr[...], tm, True)

        for hh in range(BH):
            cs = slice(hh * HD, (hh + 1) * HD)
            dh = dh_scr[:, cs]
            mult = m_ref[:, cs].astype(F32)
            ig = ig_ref[:, cs].astype(F32)
            r = r_ref[:, cs].astype(F32)
            xc = xc_ref[:, cs]
            lam = p8_ref[7:8, cs]
            sp = _softplus_neg(lam)
            dla = dh * cc_ref[:, cs]
            gp8_ref[7:8, cs] += _rowsum(dla * ((-RG_C) * r)) * (-_sigmoid(-lam))
            dpr = dla * ((-RG_C) * sp) * (r * (1.0 - r))
            dpi = dh * mult * xc * (ig * (1.0 - ig))
            gp8_ref[5:6, cs] += _rowsum(dpr)
            gp8_ref[6:7, cs] += _rowsum(dpi)
            dcat = jnp.concatenate([dpr, dpi], axis=1).astype(BF16)
            dxc = dh * mult * ig + _dot_nt(dcat, gcat_ref[hh])
            gg = _dot(xc.T.astype(BF16), dcat)
            gga_ref[hh] += gg[:, :HD]
            ggx_ref[hh] += gg[:, HD:]
            dxce_scr[0:tm, cs] = dxc
            gp8_ref[4:5, cs] += _rowsum(dxc)
        for hh in range(BH):
            cs = slice(hh * HD, (hh + 1) * HD)
            xb = zb_ref[:, cs]
            d0, d1 = dxce_scr[0:tm, cs], dxce_scr[1:1 + tm, cs]
            d2, d3 = dxce_scr[2:2 + tm, cs], dxce_scr[3:3 + tm, cs]
            dzb_ref[:, cs] = (p8_ref[3:4, cs] * d0 + p8_ref[2:3, cs] * d1 + p8_ref[1:2, cs] * d2
                              + p8_ref[0:1, cs] * d3).astype(BF16)
            gp8_ref[3:4, cs] += _rowsum(d0 * xb)
            gp8_ref[2:3, cs] += _rowsum(d1 * xb)
            gp8_ref[1:2, cs] += _rowsum(d2 * xb)
            gp8_ref[0:1, cs] += _rowsum(d3 * xb)
        dxce_scr[tm:tm + SUBLANES, :] = dxce_scr[0:SUBLANES, :]

        dh1 = jnp.zeros((tm, D), F32)
        for k in range(NDEV):
            dh1 = dh1 + _dot_nt(dzb_ref[:, k * CB:(k + 1) * CB], win_ref[k])
        x1 = x1_ref[...]
        r1 = _rms(x1)
        dx1 = dx2 + _rms_bwd(dh1, x1, r1, nw_ref[...])
        dx1_ref[...] = dx1
        dx1b_ref[...] = dx1.astype(BF16)
        gnw_ref[...] += _rowsum(dh1 * x1 * r1)

    tile = lambda w: pl.BlockSpec((tm, w), lambda i: (nt - 1 - i, 0))
    whole = lambda *s: pl.BlockSpec(s, lambda i: (0,) * len(s))
    full = lambda: pltpu.VMEM((tm, BW), F32)
    ext = lambda: pltpu.VMEM((tm + SUBLANES, BW), F32)
    out, _ = _call(
        main, [], name="bwd_b", grid=(nt,),
        ins=[dx2, zb, hs, x1, *saved, nw, win8, p8, gcat, wout],
        in_specs=[tile(D), tile(2 * BW), tile(BW), tile(D)] + [tile(BW)] * 6 + [_VMEM] * 5,
        out_shape=[_sds((s_len, D), F32), _sds((s_len, D), BF16), _sds((s_len, 2 * BW), BF16), _sds((SUBLANES, BW), F32),
                   _sds((BH, HD, HD), F32), _sds((BH, HD, HD), F32), _sds((1, D), F32)],
        out_specs=[tile(D), tile(D), tile(2 * BW), whole(SUBLANES, BW), whole(BH, HD, HD), whole(BH, HD, HD),
                   whole(1, D)],
        scratch=[ext(), full(), full(), full(), full(), ext(), pltpu.VMEM((1, BW), F32), pltpu.VMEM((1, BW), F32)])
    return out


def _transpose_into(dst_ref, src_ref, rows):
    s_len = src_ref.shape[0]
    for r0 in range(0, s_len, rows):
        dst_ref[:, r0:r0 + rows] = src_ref[r0:r0 + rows, :].astype(F32).T.astype(BF16)


def _wgrad(a, b, jobs, *, by_rows, per, name, relay_step=0):
    s_len, m = a.shape
    n = b.shape[1]
    r, cd = (m // NDEV, n) if by_rows else (m, n // NDEV)
    nsteps = NDEV // per
    at_rows = per * r if by_rows else m

    def main(i, ins, outs, scr):
        a_ref, b_ref = ins
        q_ref, acc_ref = outs
        at_scr, stage, mine, land, send_sems, recv_sems = scr
        x, y, c = _place()

        def to_sibling(pi):
            return pltpu.make_async_remote_copy(
                src_ref=stage.at[pi & 1], dst_ref=land.at[pi], send_sem=send_sems.at[pi], recv_sem=recv_sems.at[pi],
                device_id=(x, y, 1 - c), device_id_type=MESH)

        if by_rows:
            _transpose_into(at_scr, a_ref, TRANSPOSE_ROWS)
        else:
            @pl.when(i == 0)
            def _():
                _transpose_into(at_scr, a_ref, TRANSPOSE_ROWS)

        res = _dot(at_scr[...], b_ref[...]).astype(BF16)
        for k in range(per):
            blk = per * i + k
            pi, pc = blk >> 1, blk & 1
            val = res[k * r:(k + 1) * r, :] if by_rows else res

            @pl.when(pc != c)
            def _():
                @pl.when(pi >= 2)
                def _():
                    to_sibling(pi - 2).wait_send()

                stage[pi & 1] = val
                to_sibling(pi).start()

            @pl.when(pc == c)
            def _():
                mine[pi] = val

        @pl.when(i == nsteps - 1)
        def _():
            for p in range(4):
                to_sibling(p).wait_recv()
            to_sibling(2).wait_send()
            to_sibling(3).wait_send()
            _chip_sums(mine, land, q_ref, acc_ref, x, y)

    if by_rows:
        in_specs = [pl.BlockSpec((s_len, at_rows), lambda j: (0, j)), _VMEM]
    else:
        in_specs = [_VMEM, pl.BlockSpec((s_len, cd), lambda j: (0, j))]
    blk_vmem = lambda k: pltpu.VMEM((k, r, cd), BF16)
    (q, acc), job_out = _call(
        main, jobs, name=name, grid=(nsteps,), relay_step=relay_step, ins=[a, b], in_specs=in_specs,
        out_shape=[_sds((NCHIP_OTHER, r, cd), BF16), _sds((r, cd), F32)],
        out_specs=[pl.BlockSpec((NCHIP_OTHER, r, cd), lambda j: (0, 0, 0)), pl.BlockSpec((r, cd), lambda j: (0, 0))],
        scratch=[pltpu.VMEM((at_rows, s_len), BF16), blk_vmem(2), blk_vmem(4), blk_vmem(4),
                 pltpu.SemaphoreType.DMA((4,)), pltpu.SemaphoreType.DMA((4,))])
    return q, acc, job_out


def _wgrad_cols_early(a, b, jobs, *, name, relay_step=0):
    s_len, m = a.shape
    r, cd = m, b.shape[1] // NDEV
    h = r // 2

    def chip_at(pos, base):
        return base ^ (3 - pos)

    def main(i, ins, outs, scr):
        a_ref, b_ref = ins
        q_ref, acc_ref, rel_ref = outs
        at_scr, stage, mine, land, q2_scr, send_sems, recv_sems, via_send, via_recv = scr
        x, y, c = _place()
        base = 2 * x + y
        xn, yn, _ = _other_chips(x, y)
        pos, pc = i >> 1, i & 1
        pi = chip_at(pos, base)

        def to_sibling(chip, slot):
            return pltpu.make_async_remote_copy(
                src_ref=stage.at[slot], dst_ref=land.at[chip], send_sem=send_sems.at[chip],
                recv_sem=recv_sems.at[chip], device_id=(x, y, 1 - c), device_id_type=MESH)

        def via(k):
            return pltpu.make_async_remote_copy(
                src_ref=q2_scr.at[pl.ds(k * h, h)], dst_ref=rel_ref.at[k], send_sem=via_send.at[k],
                recv_sem=via_recv.at[k], device_id=(*(xn, yn)[k], c), device_id_type=MESH)

        @pl.when(i == 0)
        def _():
            _transpose_into(at_scr, a_ref, TRANSPOSE_ROWS)

        res = _dot(at_scr[...], b_ref[...]).astype(BF16)

        @pl.when(pc != c)
        def _():
            @pl.when(pos >= 2)
            def _():
                to_sibling(chip_at(pos - 2, base), pos & 1).wait_send()

            stage[pos & 1] = res
            to_sibling(pi, pos & 1).start()

        @pl.when(pc == c)
        def _():
            mine[pi] = res

        @pl.when(i == 1)
        def _():
            dg = chip_at(0, base)
            to_sibling(dg, 0).wait_recv()
            q2 = (mine[dg].astype(F32) + land[dg].astype(F32)).astype(BF16)
            q2_scr[...] = q2
            q_ref[2] = q2
            via(0).start()
            via(1).start()

        @pl.when(i == NDEV - 1)
        def _():
            for pos_ in (1, 2, 3):
                to_sibling(chip_at(pos_, base), 0).wait_recv()
            to_sibling(chip_at(2, base), 0).wait_send()
            to_sibling(chip_at(3, base), 1).wait_send()
            for k in range(2):
                via(k).wait_recv()
            for k in range(2):
                via(k).wait_send()
            for j, chip in enumerate((base ^ 2, base ^ 1)):
                q_ref[j] = (mine[chip].astype(F32) + land[chip].astype(F32)).astype(BF16)
            acc_ref[...] = mine[base].astype(F32) + land[base].astype(F32)

    def b_block(j):
        base = 2 * lax.axis_index("x") + lax.axis_index("y")
        return (0, 2 * chip_at(j >> 1, base) + (j & 1))

    blk_vmem = lambda k: pltpu.VMEM((k, r, cd), BF16)
    (q, acc, rel), job_out = _call(
        main, jobs, name=name, grid=(NDEV,), relay_step=relay_step, ins=[a, b],
        in_specs=[_VMEM, pl.BlockSpec((s_len, cd), b_block)],
        out_shape=[_sds((NCHIP_OTHER, r, cd), BF16), _sds((r, cd), F32), _sds((2, h, cd), BF16)],
        out_specs=[pl.BlockSpec((NCHIP_OTHER, r, cd), lambda j: (0, 0, 0)), pl.BlockSpec((r, cd), lambda j: (0, 0)), _HBM],
        scratch=[pltpu.VMEM((m, s_len), BF16), blk_vmem(2), blk_vmem(4), blk_vmem(4), pltpu.VMEM((r, cd), BF16),
                 pltpu.SemaphoreType.DMA((4,)), pltpu.SemaphoreType.DMA((4,)), pltpu.SemaphoreType.DMA((2,)),
                 pltpu.SemaphoreType.DMA((2,))])
    return q, acc, rel, job_out


class _ExchangeRest:
    def __init__(self, q, relayed):
        _, r, cd = q.shape
        half = (2, r // 2, cd)
        self.ins, self.in_specs = [q, relayed], [_HBM, _HBM]
        self.out_shape, self.out_specs = [_sds((2, r, cd), q.dtype)], [_HBM]
        self.scratch = [pltpu.VMEM(half, q.dtype), pltpu.VMEM(half, q.dtype), pltpu.VMEM(half, q.dtype),
                        pltpu.SemaphoreType.DMA((4,)), pltpu.SemaphoreType.DMA((4,)), pltpu.SemaphoreType.DMA((4,))]

    def ops(self, ins, outs, scr):
        (q, rel_in), (land,) = ins, outs
        own, rel, comb, send_sems, recv_sems, local_sems = scr
        h = q.shape[1] // 2
        x, y, c = _place()
        xn, yn, _ = _other_chips(x, y)
        h0, h1 = pl.ds(0, h), pl.ds(h, h)

        def remote(k, src, dst, chip):
            return pltpu.make_async_remote_copy(src_ref=src, dst_ref=dst, send_sem=send_sems.at[k],
                                                recv_sem=recv_sems.at[k], device_id=(*chip, c), device_id_type=MESH)

        def sends():
            return [remote(0, q.at[0, h0], land.at[0, h0], xn), remote(1, q.at[1, h1], land.at[1, h1], yn),
                    remote(2, comb.at[0], land.at[1, h0], yn), remote(3, comb.at[1], land.at[0, h1], xn)]

        def loads():
            return [pltpu.make_async_copy(q.at[1, h0], own.at[0], local_sems.at[0]),
                    pltpu.make_async_copy(q.at[0, h1], own.at[1], local_sems.at[1]),
                    pltpu.make_async_copy(rel_in.at[0], rel.at[0], local_sems.at[2]),
                    pltpu.make_async_copy(rel_in.at[1], rel.at[1], local_sems.at[3])]

        def start():
            cps, lds = sends(), loads()
            for ld in lds:
                ld.start()
            cps[0].start()
            cps[1].start()
            for ld in lds:
                ld.wait()
            for k in range(2):
                comb[k] = (own[k].astype(F32) + rel[k].astype(F32)).astype(comb.dtype)
            cps[2].start()
            cps[3].start()

        def finish():
            cps = sends()
            for cp in cps:
                cp.wait_recv()
            for cp in cps:
                cp.wait_send()

        return start, lambda: None, finish


def _adam_math(w, g, m, v):
    m = B1 * m + (1.0 - B1) * g
    v = B2 * v + (1.0 - B2) * (g * g)
    m_hat = m / (1.0 - B1 ** STEP)
    v_hat = v / (1.0 - B2 ** STEP)
    delta = (-LR) * (m_hat / (jnp.sqrt(v_hat) + ADAM_EPS) + WD * w)
    return delta, m, v


def _adam_big(w, acc, land, m, v, name):
    r, cd = w.shape
    rb = ADAM_ROWS if r % ADAM_ROWS == 0 else r // 4
    nland = land.shape[0]

    def body(w_ref, acc_ref, land_ref, m_ref, v_ref, g_ref, d_ref, mo_ref, vo_ref):
        g = acc_ref[...]
        for j in range(nland):
            g = g + land_ref[j].astype(F32)
        g_ref[...] = g
        d_ref[...], mo_ref[...], vo_ref[...] = _adam_math(w_ref[...], g, m_ref[...], v_ref[...])

    blk = pl.BlockSpec((rb, cd), lambda i: (i, 0))
    blk3 = pl.BlockSpec((nland, rb, cd), lambda i: (0, i, 0))
    return pl.pallas_call(
        body, name=name, grid=(r // rb,), in_specs=[blk, blk, blk3, blk, blk], out_specs=[blk] * 4,
        out_shape=[_sds((r, cd), F32)] * 4,
        compiler_params=_params(dimension_semantics=("arbitrary",)),
    )(w, acc, land, m, v)


def _adam_small(groups):
    n = len(groups)

    def body(*refs):
        ins, outs = refs[:4 * n], refs[4 * n:]
        for k in range(n):
            w_ref, g_ref, m_ref, v_ref = ins[4 * k:4 * k + 4]
            d, mo, vo = _adam_math(w_ref[...], g_ref[...], m_ref[...], v_ref[...])
            outs[3 * k][...] = d
            outs[3 * k + 1][...] = mo
            outs[3 * k + 2][...] = vo

    flat = [a for grp in groups for a in grp]
    shapes = [_sds(grp[0].shape, F32) for grp in groups for _ in range(3)]
    res = pl.pallas_call(
        body, name="adam_small", in_specs=[_VMEM] * (4 * n), out_specs=[_VMEM] * (3 * n), out_shape=shapes,
        compiler_params=_params(),
    )(*flat)
    return [tuple(res[3 * k:3 * k + 3]) for k in range(n)]


TM_FWD_A = 256
RELAY_STEP_FWD_A = 2
FORWARD_STEP_FWD_A = 6
RELAY_STEP_FWD_B = 2
TM_BWD_A = 256
RELAY_STEP_BWD_A = 3
TM_BWD_A_IN = 256
RELAY_STEP_BWD_A_IN = 4
RELAY_STEP_WGRAD_A_IN = 2
TM_FWD_B = 256
TM_HEAD = 512
TM_BWD_B = 256


def _pack(parts, rows):
    flat = jnp.concatenate([p.reshape(-1) for p in parts])
    return jnp.pad(flat, (0, NDEV * rows * LANES - flat.shape[0])).reshape(NDEV, rows, LANES)


def _unpack(packed, shapes):
    flat, out, off = packed.reshape(-1), [], 0
    for s in shapes:
        size = 1
        for d in s:
            size *= d
        out.append(flat[off:off + size].reshape(s))
        off += size
    return out


def kernel(x, norm_w, a_w_in, a_ln_w, a_ln_b, a_w_s, a_b_s, a_w_out, b_w_in, b_conv_w, b_conv_b, b_gate_a_w, b_gate_a_b, b_gate_x_w, b_gate_x_b, b_lambda, b_w_out, norm_f_w, loss_target, m_norm_w, m_a_w_in, m_a_ln_w, m_a_ln_b, m_a_w_s, m_a_b_s, m_a_w_out, m_b_w_in, m_b_conv_w, m_b_conv_b, m_b_gate_a_w, m_b_gate_a_b, m_b_gate_x_w, m_b_gate_x_b, m_b_lambda, m_b_w_out, m_norm_f_w, v_norm_w, v_a_w_in, v_a_ln_w, v_a_ln_b, v_a_w_s, v_a_b_s, v_a_w_out, v_b_w_in, v_b_conv_w, v_b_conv_b, v_b_gate_a_w, v_b_gate_a_b, v_b_gate_x_w, v_b_gate_x_b, v_b_lambda, v_b_w_out, v_norm_f_w):
    me = 4 * lax.axis_index("x") + 2 * lax.axis_index("y") + lax.axis_index("c")
    xs, tgt = x[0], loss_target[0]
    nw0, nw1, nfw = norm_w[0:1], norm_w[1:2], norm_f_w.reshape(1, D)
    w_s, bst = a_w_s[0], a_b_s[0].T
    gcat = jnp.concatenate([b_gate_a_w[0], b_gate_x_w[0]], axis=-1).astype(BF16)

    p8_shard = jnp.concatenate([b_conv_w[0], b_conv_b, b_gate_a_b, b_gate_x_b, b_lambda], axis=0)
    (z, h0, ya, pp), ((win_a8, p8_all), (wout_a8, win_b8)) = _fwd_a(
        xs, nw0, a_ln_w, a_ln_b, w_s, bst,
        [_Gather([a_w_in[0], p8_shard], [BF16, F32]), _Gather([a_w_out[0], b_w_in[0]], [BF16, BF16])],
        tm=TM_FWD_A, relay_step=RELAY_STEP_FWD_A)
    p8 = jnp.transpose(p8_all, (1, 0, 2)).reshape(SUBLANES, BW)
    wout_a = wout_a8.reshape(AW, D)
    (x1, zb, hs, h1, yb, *saved_b), ((wout_b8,),) = _fwd_b(
        xs, ya, wout_a, nw1, win_b8, p8, gcat, [_Gather([b_w_out[0]], [BF16])],
        tm=TM_FWD_B, relay_step=RELAY_STEP_FWD_B)
    wout_b = wout_b8.reshape(BW, D)
    dx2, dx2b, loss, g_nfw = _head(x1, yb, wout_b, nfw, tgt, tm=TM_HEAD)

    dx1, dx1b, dzb, g_p8, g_ga, g_gx, g_nw1 = _bwd_b(dx2, zb, hs, x1, saved_b, nw1, win_b8, p8, gcat, wout_b,
                                                     tm=TM_BWD_B)
    q_wout_b, acc_wout_b, _ = _wgrad(yb, dx2b, [], by_rows=True, per=2, name="wgrad_b_out")
    shapes_b = [(1, D), (1, D), (SUBLANES, BW), (1, 1)]
    pack_b = _pack([g_nfw, g_nw1, g_p8, loss], 16)
    small_b = _InChip([g_ga.reshape(NDEV, -1, HD), g_gx.reshape(NDEV, -1, HD), pack_b])
    q_win_b, acc_win_b, (sm_b, (l_wout_b,)) = _wgrad(h1, dzb, [small_b, _Exchange([q_wout_b])], by_rows=False, per=1,
                                                      name="wgrad_b_in")
    qs_b, accs_b = sm_b[:3], sm_b[3:]

    (dz, g_lnw, g_lnb, g_ws, g_bst), (lands_b, (l_win_b,)) = _bwd_a(
        dx1b, z, pp, a_ln_w, a_ln_b, w_s, bst, wout_a, [_Exchange(qs_b), _ExchangeVia(q_win_b)],
        tm=TM_BWD_A, relay_step=RELAY_STEP_BWD_A)
    shapes_a = [(1, AW), (1, AW), (CH, G)]
    pack_a = _pack([g_lnw, g_lnb, g_bst], 8)
    q_wout_a, acc_wout_a, (red_b, sm_a) = _wgrad(
        ya, dx1b, [_SumGather(accs_b, lands_b), _InChip([g_ws, pack_a])], by_rows=True, per=2,
        name="wgrad_a_out", relay_step=1)
    qs_a, accs_a = sm_a[:2], sm_a[2:]
    q_win_a, acc_win_a, rel_a, (lands_a, (l_wout_a,)) = _wgrad_cols_early(
        h0, dz, [_Exchange(qs_a), _ExchangeVia(q_wout_a)], name="wgrad_a_in", relay_step=RELAY_STEP_WGRAD_A_IN)
    (gx, g_nw0), (red_a, (l_win_a,)) = _bwd_a_in(
        dz, dx1, xs, nw0, win_a8, [_SumGather(accs_a, lands_a), _ExchangeRest(q_win_a, rel_a)],
        tm=TM_BWD_A_IN, relay_step=RELAY_STEP_BWD_A_IN)

    r_ga, r_gx, r_pack_b = red_b
    r_nfw, r_nw1, r_p8, loss = _unpack(r_pack_b, shapes_b)
    r_ws, r_pack_a = red_a
    r_lnw, r_lnb, r_bst = _unpack(r_pack_a, shapes_a)
    g_p8 = lax.dynamic_slice_in_dim(r_p8, me * (BW // NDEV), BW // NDEV, axis=1)
    loss = loss[0, 0]

    weights = dict(norm_w=norm_w, a_w_in=a_w_in, a_ln_w=a_ln_w, a_ln_b=a_ln_b, a_w_s=a_w_s, a_b_s=a_b_s, a_w_out=a_w_out,
                   b_w_in=b_w_in, b_conv_w=b_conv_w, b_conv_b=b_conv_b, b_gate_a_w=b_gate_a_w, b_gate_a_b=b_gate_a_b,
                   b_gate_x_w=b_gate_x_w, b_gate_x_b=b_gate_x_b, b_lambda=b_lambda, b_w_out=b_w_out, norm_f_w=norm_f_w)
    mom1 = dict(norm_w=m_norm_w, a_w_in=m_a_w_in, a_ln_w=m_a_ln_w, a_ln_b=m_a_ln_b, a_w_s=m_a_w_s, a_b_s=m_a_b_s,
                a_w_out=m_a_w_out, b_w_in=m_b_w_in, b_conv_w=m_b_conv_w, b_conv_b=m_b_conv_b, b_gate_a_w=m_b_gate_a_w,
                b_gate_a_b=m_b_gate_a_b, b_gate_x_w=m_b_gate_x_w, b_gate_x_b=m_b_gate_x_b, b_lambda=m_b_lambda,
                b_w_out=m_b_w_out, norm_f_w=m_norm_f_w)
    mom2 = dict(norm_w=v_norm_w, a_w_in=v_a_w_in, a_ln_w=v_a_ln_w, a_ln_b=v_a_ln_b, a_w_s=v_a_w_s, a_b_s=v_a_b_s,
                a_w_out=v_a_w_out, b_w_in=v_b_w_in, b_conv_w=v_b_conv_w, b_conv_b=v_b_conv_b, b_gate_a_w=v_b_gate_a_w,
                b_gate_a_b=v_b_gate_a_b, b_gate_x_w=v_b_gate_x_w, b_gate_x_b=v_b_gate_x_b, b_lambda=v_b_lambda,
                b_w_out=v_b_w_out, norm_f_w=v_norm_f_w)
    names = list(weights)

    def as2d(a):
        return a.reshape(-1, a.shape[-1])

    upd, grads = {}, {}
    for k, acc, land in (("a_w_in", acc_win_a, l_win_a), ("a_w_out", acc_wout_a, l_wout_a),
                         ("b_w_in", acc_win_b, l_win_b), ("b_w_out", acc_wout_b, l_wout_b)):
        g, d, mo, vo = _adam_big(as2d(weights[k]), acc, land, as2d(mom1[k]), as2d(mom2[k]), "adam_" + k)
        grads[k] = g[None]
        upd[k] = (d, mo, vo)
    grads.update(
        norm_w=jnp.concatenate([g_nw0, r_nw1], axis=0), a_ln_w=r_lnw, a_ln_b=r_lnb,
        a_w_s=r_ws.reshape(1, G, CH, CH), a_b_s=r_bst.T[None],
        b_conv_w=g_p8[None, 0:4], b_conv_b=g_p8[4:5], b_gate_a_w=r_ga.reshape(1, BH, HD, HD), b_gate_a_b=g_p8[5:6],
        b_gate_x_w=r_gx.reshape(1, BH, HD, HD), b_gate_x_b=g_p8[6:7], b_lambda=g_p8[7:8], norm_f_w=r_nfw.reshape(D))
    small_names = [k for k in names if k not in upd]
    res = _adam_small([(as2d(weights[k]), as2d(grads[k]), as2d(mom1[k]), as2d(mom2[k])) for k in small_names])
    for k, r3 in zip(small_names, res):
        upd[k] = r3
    deltas = [upd[k][0].reshape(weights[k].shape) for k in names]
    new_m = [upd[k][1].reshape(weights[k].shape) for k in names]
    new_v = [upd[k][2].reshape(weights[k].shape) for k in names]
    return (loss, gx[None], *[grads[k] for k in names], *deltas, *new_m, *new_v)
```

```python
import jax
import jax.numpy as jnp
from jax import lax
from jax.experimental import pallas as pl
from jax.experimental.pallas import tpu as pltpu

F32 = jnp.float32
BF16 = jnp.bfloat16
MESH = pl.DeviceIdType.MESH

NDEV = 8
NCHIP_OTHER = 3
D = 1024
AW = 2048
G = 8
GD = AW // G
CH = 128
BW = 1536
BH = 12
HD = BW // BH
CA = 3 * AW // NDEV
CB = 2 * BW // NDEV
RMS_EPS = 1e-6
LN_EPS = 1e-5
RG_C = 8.0
LR, B1, B2, ADAM_EPS, WD, STEP = 0.001, 0.9, 0.999, 1e-08, 0.01, 10
V7X_VMEM_BYTES = 64 * 1024 * 1024
VMEM_LIMIT = V7X_VMEM_BYTES - 8 * 1024 * 1024
SUBLANES = 8
LANES = 128
BF16_ROWS = 16
TRANSPOSE_ROWS = 256
ADAM_ROWS = 512
GELU_C = 0.7978845608028654
GELU_K = 0.044715

_VMEM = pl.BlockSpec(memory_space=pltpu.VMEM)
_HBM = pl.BlockSpec(memory_space=pltpu.HBM)


def _sds(shape, dtype):
    return jax.ShapeDtypeStruct(tuple(shape), dtype)


def _params(**kw):
    return pltpu.CompilerParams(vmem_limit_bytes=VMEM_LIMIT, **kw)


def _gelu_t(z):
    p = 0.5 * jnp.tanh(z * (GELU_C + (GELU_C * GELU_K) * (z * z))) + 0.5
    return z * p, p


def _dgelu(z, p):
    return p * (1.0 + (z * (1.0 - p)) * (2.0 * GELU_C + (6.0 * GELU_C * GELU_K) * (z * z)))


def _sigmoid(v):
    return 0.5 * jnp.tanh(0.5 * v) + 0.5


def _softplus_neg(lam):
    return jnp.maximum(-lam, 0.0) + jnp.log1p(jnp.exp(-jnp.abs(lam)))


def _dot(a, b):
    return jnp.dot(a, b, preferred_element_type=F32)


def _dot_nt(a, b):
    return lax.dot_general(a, b, (((1,), (1,)), ((), ())), preferred_element_type=F32)


def _rowsum(v):
    return jnp.sum(v, axis=0, keepdims=True)


def _causal_mask():
    r = lax.broadcasted_iota(jnp.int32, (CH, CH), 0)
    c = lax.broadcasted_iota(jnp.int32, (CH, CH), 1)
    return r >= c


def _rms(x):
    return lax.rsqrt(jnp.mean(x * x, axis=-1, keepdims=True) + RMS_EPS)


def _rms_bwd(dh, x, r, nw):
    gy = dh * nw
    return r * gy - x * (r * r * r) * jnp.mean(gy * x, axis=-1, keepdims=True)


def _place():
    return lax.axis_index("x"), lax.axis_index("y"), lax.axis_index("c")


def _other_chips(x, y):
    return [(1 - x, y), (x, 1 - y), (1 - x, 1 - y)]


GATHER_SLOTS = 10


def _gather_ops(ins, outs, send_sems, recv_sems, local_sems):
    n = len(ins)
    x, y, c = _place()
    sibling = (x, y, 1 - c)
    xn, yn, dg = _other_chips(x, y)
    split = [ins[i].shape[0] % (2 * BF16_ROWS) == 0 for i in range(n)]

    def blk(chip, core):
        return 4 * chip[0] + 2 * chip[1] + core

    me = blk((x, y), c)

    def part(ref, i, half):
        if half is None:
            return ref
        h = ins[i].shape[0] // 2
        return ref.at[pl.ds(half * h, h)]

    def copy(i, k, block, to, half=None, src=None):
        dst = part(outs[i].at[block], i, half)
        return pltpu.make_async_remote_copy(
            src_ref=dst if src is None else part(src, i, half), dst_ref=dst,
            send_sem=send_sems.at[k, i], recv_sem=recv_sems.at[k, i], device_id=to, device_id_type=MESH)

    def first_copies():
        mine = [pltpu.make_async_copy(ins[i], outs[i].at[me], local_sems.at[i]) for i in range(n)]
        first = []
        for i in range(n):
            first.append(copy(i, 0, me, sibling, src=ins[i]))
            if split[i]:
                first.append(copy(i, 1, me, (*xn, c), 0, ins[i]))
                first.append(copy(i, 3, me, (*yn, c), 1, ins[i]))
                first.append(copy(i, 2, me, (*xn, c), 1, ins[i]))
                first.append(copy(i, 4, me, (*yn, c), 0, ins[i]))
            else:
                first.append(copy(i, 1, me, (*xn, c), None, ins[i]))
                first.append(copy(i, 3, me, (*yn, c), None, ins[i]))
                first.append(copy(i, 5, me, (*dg, c), None, ins[i]))
        return mine, first

    def onward():
        out = []
        for i in range(n):
            if split[i]:
                out.append(copy(i, 5, blk(xn, c), (*yn, c), 0))
                out.append(copy(i, 6, blk(yn, c), (*xn, c), 1))
        return out

    def start():
        mine, first = first_copies()
        for cp in mine + first:
            cp.start()

    def relay():
        sends = onward()
        for i in range(n):
            if split[i]:
                copy(i, 1, blk(xn, c), sibling, 0).wait_recv()
                sends.pop(0).start()
                copy(i, 3, blk(yn, c), sibling, 1).wait_recv()
                sends.pop(0).start()

    def passes():
        return [copy(i, 7 + j, blk(chip, c), sibling) for i in range(n) for j, chip in enumerate((xn, yn, dg))]

    def forward():
        fwd = passes()
        for i in range(n):
            if split[i]:
                copy(i, 2, blk(xn, c), sibling, 1).wait_recv()
                fwd[3 * i].start()
                copy(i, 4, blk(yn, c), sibling, 0).wait_recv()
                fwd[3 * i + 1].start()
                copy(i, 5, blk(dg, c), sibling, 0).wait_recv()
                copy(i, 6, blk(dg, c), sibling, 1).wait_recv()
                fwd[3 * i + 2].start()
            else:
                copy(i, 1, blk(xn, c), sibling).wait_recv()
                fwd[3 * i].start()
                copy(i, 3, blk(yn, c), sibling).wait_recv()
                fwd[3 * i + 1].start()
                copy(i, 5, blk(dg, c), sibling).wait_recv()
                fwd[3 * i + 2].start()

    def finish():
        mine, first = first_copies()
        for i in range(n):
            copy(i, 0, blk((x, y), 1 - c), sibling).wait_recv()
            for j, chip in enumerate((xn, yn, dg)):
                copy(i, 7 + j, blk(chip, 1 - c), sibling).wait_recv()
        for cp in first + passes() + onward():
            cp.wait_send()
        for cp in mine:
            cp.wait()

    return start, relay, forward, finish


def _gather_sems(n):
    return [pltpu.SemaphoreType.DMA((GATHER_SLOTS, n)), pltpu.SemaphoreType.DMA((GATHER_SLOTS, n)),
            pltpu.SemaphoreType.DMA((n,))]


class _Gather:
    def __init__(self, shards, as_dtypes=None):
        n = len(shards)
        dts = [s.dtype for s in shards] if as_dtypes is None else list(as_dtypes)
        self.cast = [jnp.dtype(d) != s.dtype for d, s in zip(dts, shards)]
        self.ins = list(shards)
        self.in_specs = [_VMEM if c else _HBM for c in self.cast]
        self.out_shape = [_sds((NDEV,) + s.shape, d) for s, d in zip(shards, dts)]
        self.out_specs = [_HBM] * n
        self.scratch = [pltpu.VMEM(s.shape, d) for s, d, c in zip(shards, dts, self.cast) if c] + _gather_sems(n)

    def ops(self, ins, outs, scr):
        ncast = sum(self.cast)
        staged = iter(scr[:ncast])
        srcs = [next(staged) if c else ref for c, ref in zip(self.cast, ins)]
        start, relay, forward, finish = _gather_ops(srcs, outs, *scr[ncast:])

        def cast_and_start():
            for c, ref, src in zip(self.cast, ins, srcs):
                if c:
                    src[...] = ref[...].astype(src.dtype)
            start()

        return cast_and_start, relay, forward, finish


class _Exchange:
    def __init__(self, qs):
        n = len(qs)
        self.ins, self.in_specs = list(qs), [_HBM] * n
        self.out_shape = [_sds(q.shape, q.dtype) for q in qs]
        self.out_specs = [_HBM] * n
        self.scratch = [pltpu.SemaphoreType.DMA((NCHIP_OTHER, n)), pltpu.SemaphoreType.DMA((NCHIP_OTHER, n))]

    def ops(self, ins, outs, scr):
        send_sems, recv_sems = scr
        n = len(ins)
        x, y, c = _place()
        chips = _other_chips(x, y)

        def copies():
            return [pltpu.make_async_remote_copy(
                src_ref=ins[i].at[j], dst_ref=outs[i].at[j], send_sem=send_sems.at[j, i],
                recv_sem=recv_sems.at[j, i], device_id=(*chips[j], c), device_id_type=MESH)
                for i in range(n) for j in range(NCHIP_OTHER)]

        def start():
            for cp in copies():
                cp.start()

        def finish():
            cps = copies()
            for cp in cps:
                cp.wait_recv()
            for cp in cps:
                cp.wait_send()

        return start, lambda: None, finish


class _ExchangeVia:
    def __init__(self, q):
        _, r, cd = q.shape
        half = (2, r // 2, cd)
        self.ins, self.in_specs = [q], [_HBM]
        self.out_shape, self.out_specs = [_sds((2, r, cd), q.dtype)], [_HBM]
        self.scratch = [pltpu.VMEM(half, q.dtype), pltpu.VMEM(half, q.dtype), pltpu.VMEM(half, q.dtype),
                        pltpu.SemaphoreType.DMA((6,)), pltpu.SemaphoreType.DMA((6,)), pltpu.SemaphoreType.DMA((2,))]

    def ops(self, ins, outs, scr):
        (q,), (land,) = ins, outs
        relayed, own, comb, send_sems, recv_sems, local_sems = scr
        h = q.shape[1] // 2
        x, y, c = _place()
        xn, yn, _ = _other_chips(x, y)
        h0, h1 = pl.ds(0, h), pl.ds(h, h)

        def remote(k, src, dst, chip):
            return pltpu.make_async_remote_copy(src_ref=src, dst_ref=dst, send_sem=send_sems.at[k],
                                                recv_sem=recv_sems.at[k], device_id=(*chip, c), device_id_type=MESH)

        def via():
            return [remote(2, q.at[2, h0], relayed.at[0], xn), remote(3, q.at[2, h1], relayed.at[1], yn)]

        def direct():
            return [remote(0, q.at[0, h0], land.at[0, h0], xn), remote(1, q.at[1, h1], land.at[1, h1], yn)]

        def second():
            return [remote(4, comb.at[0], land.at[1, h0], yn), remote(5, comb.at[1], land.at[0, h1], xn)]

        def mine():
            return [pltpu.make_async_copy(q.at[1, h0], own.at[0], local_sems.at[0]),
                    pltpu.make_async_copy(q.at[0, h1], own.at[1], local_sems.at[1])]

        def start():
            for cp in via() + direct() + mine():
                cp.start()

        def relay():
            arrived, loaded, onward = via(), mine(), second()
            for k in range(2):
                arrived[k].wait_recv()
                loaded[k].wait()
                comb[k] = (own[k].astype(F32) + relayed[k].astype(F32)).astype(comb.dtype)
                onward[k].start()

        def finish():
            landing = direct() + second()
            for cp in landing:
                cp.wait_recv()
            for cp in via() + landing:
                cp.wait_send()

        return start, relay, finish


class _SumGather:
    def __init__(self, accs, lands):
        n = len(accs)
        self.n = n
        self.ins, self.in_specs = list(accs) + list(lands), [_VMEM] * (2 * n)
        self.out_shape = [_sds((NDEV,) + a.shape, a.dtype) for a in accs]
        self.out_specs = [_HBM] * n
        self.scratch = [pltpu.VMEM(a.shape, a.dtype) for a in accs] + _gather_sems(n)

    def ops(self, ins, outs, scr):
        n = self.n
        accs, lands, mine = ins[:n], ins[n:], scr[:n]
        g_start, relay, forward, finish = _gather_ops(mine, outs, *scr[n:])

        def start():
            for i in range(n):
                mine[i][...] = accs[i][...] + lands[i][0] + lands[i][1] + lands[i][2]
            g_start()

        return start, relay, forward, finish


def _call(main, jobs, *, name, grid, ins, in_specs, out_shape, out_specs, scratch, relay_step=0, first=0,
          prologue=None, forward_step=None):
    (nsteps,) = grid
    n_in, n_out, n_scr = len(ins), len(out_shape), len(scratch)

    def body(*refs):
        pos = [0]

        def take(k):
            r = refs[pos[0]:pos[0] + k]
            pos[0] += k
            return r

        m_in = take(n_in)
        j_in = [take(len(j.ins)) for j in jobs]
        m_out = take(n_out)
        j_out = [take(len(j.out_shape)) for j in jobs]
        m_scr = take(n_scr)
        j_scr = [take(len(j.scratch)) for j in jobs]
        ops = [_four(j.ops(a, b, s)) for j, a, b, s in zip(jobs, j_in, j_out, j_scr)]
        i = pl.program_id(0)

        if ops:
            @pl.when(i == 0)
            def _():
                for o in ops[:first]:
                    o[0]()
                for o in ops[:first]:
                    o[1]()
                for o in ops[first:]:
                    o[0]()
                for o in ops[:first]:
                    o[2]()
                for o in ops[:first]:
                    o[3]()
                if prologue is not None:
                    prologue(j_out[:first], m_scr)

        main(i, m_in, m_out, m_scr)

        forward_at = max(relay_step, nsteps - 2) if forward_step is None else min(forward_step, nsteps - 1)
        for stage, at in ((1, min(relay_step, nsteps - 1)), (2, forward_at), (3, nsteps - 1)):
            if ops[first:]:
                @pl.when(i == at)
                def _():
                    for o in ops[first:]:
                        o[stage]()

    res = pl.pallas_call(
        body, name=name, grid=grid,
        in_specs=list(in_specs) + [s for j in jobs for s in j.in_specs],
        out_specs=list(out_specs) + [s for j in jobs for s in j.out_specs],
        out_shape=list(out_shape) + [s for j in jobs for s in j.out_shape],
        scratch_shapes=list(scratch) + [s for j in jobs for s in j.scratch],
        compiler_params=_params(dimension_semantics=("arbitrary",)),
    )(*ins, *[a for j in jobs for a in j.ins])
    main_out, rest, job_out = res[:n_out], res[n_out:], []
    for j in jobs:
        k = len(j.out_shape)
        job_out.append(rest[:k])
        rest = rest[k:]
    return main_out, job_out


def _four(ops):
    return ops if len(ops) == 4 else (ops[0], ops[1], lambda: None, ops[2])


class _InChip:
    def __init__(self, ps):
        n = len(ps)
        self.n = n
        blk = [p.shape[1:] for p in ps]
        self.ins, self.in_specs = list(ps), [_HBM] * n
        self.out_shape = [_sds((NCHIP_OTHER,) + b, p.dtype) for b, p in zip(blk, ps)] + [_sds(b, F32) for b in blk]
        self.out_specs = [_VMEM] * (2 * n)
        self.scratch = ([pltpu.VMEM((4,) + b, p.dtype) for b, p in zip(blk, ps)] * 2
                        + [pltpu.SemaphoreType.DMA((4, n))] * 3)

    def ops(self, ins, outs, scr):
        n = self.n
        q_refs, acc_refs = outs[:n], outs[n:]
        mines, lands = scr[:n], scr[n:2 * n]
        send_sems, recv_sems, local_sems = scr[2 * n:]
        x, y, c = _place()
        sibling = (x, y, 1 - c)

        def copies():
            out = []
            for i in range(n):
                for pi in range(4):
                    loc = pltpu.make_async_copy(ins[i].at[2 * pi + c], mines[i].at[pi], local_sems.at[pi, i])
                    cp = pltpu.make_async_remote_copy(
                        src_ref=ins[i].at[2 * pi + (1 - c)], dst_ref=lands[i].at[pi],
                        send_sem=send_sems.at[pi, i], recv_sem=recv_sems.at[pi, i],
                        device_id=sibling, device_id_type=MESH)
                    out.append((loc, cp))
            return out

        def start():
            for loc, cp in copies():
                loc.start()
                cp.start()

        def finish():
            pairs = copies()
            for loc, cp in pairs:
                loc.wait()
                cp.wait_recv()
            for i in range(n):
                _chip_sums(mines[i], lands[i], q_refs[i], acc_refs[i], x, y)
            for _, cp in pairs:
                cp.wait_send()

        return start, lambda: None, finish


def _chip_sums(mine, land, q_ref, acc_ref, x, y):
    for j, (qx, qy) in enumerate(_other_chips(x, y)):
        qi = 2 * qx + qy
        q_ref[j] = (mine[qi].astype(F32) + land[qi].astype(F32)).astype(q_ref.dtype)
    mi = 2 * x + y
    acc_ref[...] = mine[mi].astype(F32) + land[mi].astype(F32)


def _direct_sum(v, buf, send_sems, recv_sems):
    x, y, c = _place()
    me = 4 * x + 2 * y + c
    buf[me] = v
    cps = []
    for k in range(1, NDEV):
        fx, fy, fc = (k >> 2) & 1, (k >> 1) & 1, k & 1
        peer = ((1 - x) if fx else x, (1 - y) if fy else y, (1 - c) if fc else c)
        cps.append((peer, pltpu.make_async_remote_copy(
            src_ref=buf.at[me], dst_ref=buf.at[me], send_sem=send_sems.at[k - 1], recv_sem=recv_sems.at[k - 1],
            device_id=peer, device_id_type=MESH)))
    for _, cp in cps:
        cp.start()
    for k, (peer, _) in enumerate(cps):
        theirs = 4 * peer[0] + 2 * peer[1] + peer[2]
        pltpu.make_async_remote_copy(
            src_ref=buf.at[theirs], dst_ref=buf.at[theirs], send_sem=send_sems.at[k], recv_sem=recv_sems.at[k],
            device_id=peer, device_id_type=MESH).wait_recv()
    acc = buf[0]
    for j in range(1, NDEV):
        acc = acc + buf[j]
    for _, cp in cps:
        cp.wait_send()
    return acc


def _direct_sum_scratch(shape, dtype):
    return [pltpu.VMEM((NDEV,) + tuple(shape), dtype), pltpu.SemaphoreType.DMA((NDEV - 1,)),
            pltpu.SemaphoreType.DMA((NDEV - 1,))]


def _fwd_a(x, nw, lnw, lnb, ws, bst, jobs, *, tm, relay_step):
    s_len = x.shape[0]
    nt = s_len // tm
    nch = tm // CH

    def main(i, ins, outs, scr):
        x_ref, nw_ref, lnw_ref, lnb_ref, ws_ref, bst_ref = ins
        z_ref, h_ref, y_ref, pp_ref = outs
        wc_scr, gv_scr, win_ref = scr

        @pl.when(i == 0)
        def _():
            m = _causal_mask()
            for g in range(G):
                wc_scr[g] = jnp.where(m, ws_ref[g], 0.0).astype(BF16)

        x = x_ref[...]
        h = (x * _rms(x) * nw_ref[...]).astype(BF16)
        h_ref[...] = h
        for k in range(NDEV):
            z_ref[:, k * CA:(k + 1) * CA] = _dot(h, win_ref[k])

        ssum = jnp.zeros((tm, 1), F32)
        for g in range(G):
            vs = slice(AW + g * GD, AW + (g + 1) * GD)
            gv, pv = _gelu_t(z_ref[:, vs])
            pp_ref[:, vs] = pv.astype(BF16)
            gv_scr[:, g * GD:(g + 1) * GD] = gv
            ssum = ssum + jnp.sum(gv, axis=-1, keepdims=True)
        mu = ssum * (1.0 / AW)
        vsum = jnp.zeros((tm, 1), F32)
        for g in range(G):
            dlt = gv_scr[:, g * GD:(g + 1) * GD] - mu
            vsum = vsum + jnp.sum(dlt * dlt, axis=-1, keepdims=True)
        rstd = lax.rsqrt(vsum * (1.0 / AW) + LN_EPS)

        for g in range(G):
            cs = slice(g * GD, (g + 1) * GD)
            gs = slice(2 * AW + g * GD, 2 * AW + (g + 1) * GD)
            v = (gv_scr[:, cs] - mu) * rstd * lnw_ref[:, cs] + lnb_ref[:, cs]
            vb = v.astype(BF16)
            u, pu = _gelu_t(z_ref[:, cs])
            pp_ref[:, cs] = pu.astype(BF16)
            zg = z_ref[:, gs]
            sig = _sigmoid(zg)
            pp_ref[:, gs] = sig.astype(BF16)
            sg = zg * sig
            for n in range(nch):
                rs = slice(n * CH, (n + 1) * CH)
                s = _dot(wc_scr[g], vb[rs, :]) + bst_ref[:, g:g + 1]
                y_ref[rs, cs] = (u[rs, :] * s * sg[rs, :]).astype(BF16)

    tile = lambda w: pl.BlockSpec((tm, w), lambda i: (i, 0))
    return _call(
        main, jobs, name="fwd_a", grid=(nt,), relay_step=relay_step, first=1, forward_step=FORWARD_STEP_FWD_A,
        prologue=lambda gathered, scr: pltpu.sync_copy(gathered[0][0], scr[2]),
        ins=[x, nw, lnw, lnb, ws, bst], in_specs=[tile(D), _VMEM, _VMEM, _VMEM, _VMEM, _VMEM],
        out_shape=[_sds((s_len, 3 * AW), F32), _sds((s_len, D), BF16), _sds((s_len, AW), BF16),
                   _sds((s_len, 3 * AW), BF16)],
        out_specs=[tile(3 * AW), tile(D), tile(AW), tile(3 * AW)],
        scratch=[pltpu.VMEM((G, CH, CH), BF16), pltpu.VMEM((tm, AW), F32), pltpu.VMEM((NDEV, D, CA), BF16)])


def _bwd_a(dx1, z, pp, lnw, lnb, ws, bst, wout, jobs, *, tm, relay_step):
    s_len = dx1.shape[0]
    nt = s_len // tm
    nch = tm // CH

    def main(i, ins, outs, scr):
        dx1_ref, z_ref, pp_ref, lnw_ref, lnb_ref, ws_ref, bst_ref, wout_ref = ins
        dz_ref, glnw_ref, glnb_ref, gws_ref, gbst_ref = outs
        wc_scr, wct_scr, vh_scr, dgv_scr, dy_scr, dv_scr, gbs_acc, gwc_acc = scr

        @pl.when(i == 0)
        def _():
            m = _causal_mask()
            for g in range(G):
                wm = jnp.where(m, ws_ref[g], 0.0)
                wc_scr[g] = wm.astype(BF16)
                wct_scr[g] = wm.T.astype(BF16)
            glnw_ref[...] = jnp.zeros_like(glnw_ref)
            glnb_ref[...] = jnp.zeros_like(glnb_ref)
            gbs_acc[...] = jnp.zeros_like(gbs_acc)
            gwc_acc[...] = jnp.zeros_like(gwc_acc)

        dy_scr[...] = _dot_nt(dx1_ref[...], wout_ref[...])

        ssum = jnp.zeros((tm, 1), F32)
        for g in range(G):
            cs = slice(g * GD, (g + 1) * GD)
            vs = slice(AW + g * GD, AW + (g + 1) * GD)
            zv = z_ref[:, vs]
            pv = pp_ref[:, vs].astype(F32)
            gv = zv * pv
            vh_scr[:, cs] = gv
            dgv_scr[:, cs] = _dgelu(zv, pv)
            ssum = ssum + jnp.sum(gv, axis=-1, keepdims=True)
        mu = ssum * (1.0 / AW)
        vsum = jnp.zeros((tm, 1), F32)
        for g in range(G):
            dlt = vh_scr[:, g * GD:(g + 1) * GD] - mu
            vsum = vsum + jnp.sum(dlt * dlt, axis=-1, keepdims=True)
        rstd = lax.rsqrt(vsum * (1.0 / AW) + LN_EPS)

        m1 = jnp.zeros((tm, 1), F32)
        m2 = jnp.zeros((tm, 1), F32)
        for g in range(G):
            cs = slice(g * GD, (g + 1) * GD)
            gs = slice(2 * AW + g * GD, 2 * AW + (g + 1) * GD)
            vhat = (vh_scr[:, cs] - mu) * rstd
            vh_scr[:, cs] = vhat
            vb = (vhat * lnw_ref[:, cs] + lnb_ref[:, cs]).astype(BF16)
            zu = z_ref[:, cs]
            tu = pp_ref[:, cs].astype(F32)
            u = zu * tu
            zg = z_ref[:, gs]
            sig = pp_ref[:, gs].astype(F32)
            sg = zg * sig
            dy = dy_scr[:, cs]
            dsf = dy * u * sg
            dsb = dsf.astype(BF16)
            dvs = []
            for n in range(nch):
                rs = slice(n * CH, (n + 1) * CH)
                s = _dot(wc_scr[g], vb[rs, :]) + bst_ref[:, g:g + 1]
                dys = dy[rs, :] * s
                dz_ref[rs, cs] = (dys * sg[rs, :] * _dgelu(zu[rs, :], tu[rs, :])).astype(BF16)
                dz_ref[rs, gs] = (dys * u[rs, :] * (sig[rs, :] * (1.0 + zg[rs, :] * (1.0 - sig[rs, :])))).astype(BF16)
                gbs_acc[g] += dsf[rs, :]
                gwc_acc[g] += _dot_nt(dsb[rs, :], vb[rs, :])
                dvs.append(_dot(wct_scr[g], dsb[rs, :]))
            dv = jnp.concatenate(dvs, axis=0) if nch > 1 else dvs[0]
            glnw_ref[:, cs] += _rowsum(dv * vhat)
            glnb_ref[:, cs] += _rowsum(dv)
            dvh = dv * lnw_ref[:, cs]
            dv_scr[:, cs] = dvh
            m1 = m1 + jnp.sum(dvh, axis=-1, keepdims=True)
            m2 = m2 + jnp.sum(dvh * vhat, axis=-1, keepdims=True)
        m1 = m1 * (1.0 / AW)
        m2 = m2 * (1.0 / AW)
        for g in range(G):
            cs = slice(g * GD, (g + 1) * GD)
            dgv = rstd * (dv_scr[:, cs] - m1 - vh_scr[:, cs] * m2)
            dz_ref[:, AW + g * GD:AW + (g + 1) * GD] = (dgv * dgv_scr[:, cs]).astype(BF16)

        @pl.when(i == nt - 1)
        def _():
            m = _causal_mask()
            for g in range(G):
                gws_ref[g] = jnp.where(m, gwc_acc[g], 0.0)
                gbst_ref[:, g:g + 1] = jnp.sum(gbs_acc[g], axis=-1, keepdims=True)

    tile = lambda w: pl.BlockSpec((tm, w), lambda i: (i, 0))
    whole = lambda *s: pl.BlockSpec(s, lambda i: (0,) * len(s))
    big = lambda dt: pltpu.VMEM((tm, AW), dt)
    return _call(
        main, jobs, name="bwd_a", grid=(nt,), relay_step=relay_step,
        ins=[dx1, z, pp, lnw, lnb, ws, bst, wout],
        in_specs=[tile(D), tile(3 * AW), tile(3 * AW), _VMEM, _VMEM, _VMEM, _VMEM, _VMEM],
        out_shape=[_sds((s_len, 3 * AW), BF16), _sds((1, AW), F32), _sds((1, AW), F32), _sds((G, CH, CH), F32),
                   _sds((CH, G), F32)],
        out_specs=[tile(3 * AW), whole(1, AW), whole(1, AW), whole(G, CH, CH), whole(CH, G)],
        scratch=[pltpu.VMEM((G, CH, CH), BF16), pltpu.VMEM((G, CH, CH), BF16), big(F32), big(F32), big(F32), big(F32),
                 pltpu.VMEM((G, CH, GD), F32), pltpu.VMEM((G, CH, CH), F32)])


def _bwd_a_in(dz, dx1, x, nw, win8, jobs, *, tm, relay_step):
    s_len = x.shape[0]
    nt = s_len // tm

    def main(i, ins, outs, scr):
        dz_ref, dx1_ref, x_ref, nw_ref, win_ref = ins
        gx_ref, gnw_ref = outs

        @pl.when(i == 0)
        def _():
            gnw_ref[...] = jnp.zeros_like(gnw_ref)

        dh = jnp.zeros((tm, D), F32)
        for k in range(NDEV):
            dh = dh + _dot_nt(dz_ref[:, k * CA:(k + 1) * CA], win_ref[k])
        x = x_ref[...]
        r = _rms(x)
        gx_ref[...] = dx1_ref[...] + _rms_bwd(dh, x, r, nw_ref[...])
        gnw_ref[...] += _rowsum(dh * x * r)

        @pl.when(i == nt - 1)
        def _():
            gnw_ref[...] = _direct_sum(gnw_ref[...], *scr)

    tile = lambda w: pl.BlockSpec((tm, w), lambda i: (i, 0))
    return _call(
        main, jobs, name="bwd_a_in", grid=(nt,), relay_step=relay_step, forward_step=nt - 1,
        ins=[dz, dx1, x, nw, win8], in_specs=[tile(3 * AW), tile(D), tile(D), _VMEM, _VMEM],
        out_shape=[_sds((s_len, D), F32), _sds((1, D), F32)],
        out_specs=[tile(D), pl.BlockSpec((1, D), lambda i: (0, 0))], scratch=_direct_sum_scratch((1, D), F32))


def _conv(p8_ref, cs, xb, xm1, xm2, xm3):
    xc = p8_ref[4:5, cs] + p8_ref[3:4, cs] * xb
    xc = xc + p8_ref[0:1, cs] * xm3
    xc = xc + p8_ref[1:2, cs] * xm2
    return xc + p8_ref[2:3, cs] * xm1


def _gates(p8_ref, gcat_ref, hh, xc):
    cs = slice(hh * HD, (hh + 1) * HD)
    pre = _dot(xc.astype(BF16), gcat_ref[hh])
    r = _sigmoid(pre[:, :HD] + p8_ref[5:6, cs])
    ig = _sigmoid(pre[:, HD:] + p8_ref[6:7, cs])
    sp = _softplus_neg(p8_ref[7:8, cs])
    la = (-RG_C) * r * sp
    a = jnp.exp(la)
    half_log = 0.5 * jnp.log(jnp.tanh(-la) * (1.0 + a * a))
    return r, ig, sp, a, jnp.exp(half_log), jnp.exp(-half_log)


def _scan_rows(a_ref, b_ref, out_ref, carry, tm, reverse):
    row = lax.broadcasted_iota(jnp.int32, (SUBLANES, BW), 0)
    ngrp = tm // SUBLANES

    def step(j, cr):
        jj = (ngrp - 1 - j) if reverse else j
        off = pl.multiple_of(jj * SUBLANES, SUBLANES)
        a = a_ref[pl.ds(off, SUBLANES), :]
        b = b_ref[pl.ds(off, SUBLANES), :]
        for sh in (1, 2, 4):
            if reverse:
                a_s = pltpu.roll(a, SUBLANES - sh, 0)
                b_s = pltpu.roll(b, SUBLANES - sh, 0)
                m = row < SUBLANES - sh
            else:
                a_s = pltpu.roll(a, sh, 0)
                b_s = pltpu.roll(b, sh, 0)
                m = row >= sh
            b = jnp.where(m, a * b_s + b, b)
            a = jnp.where(m, a * a_s, a)
        o = b + a * cr
        out_ref[pl.ds(off, SUBLANES), :] = o
        return o[0:1, :] if reverse else o[SUBLANES - 1:SUBLANES, :]

    return lax.fori_loop(0, ngrp, step, carry)


def _fwd_b(x, ya, wout_a, nw, win8, p8, gcat, jobs, *, tm, relay_step):
    s_len = x.shape[0]
    nt = s_len // tm

    def main(i, ins, outs, scr):
        x_ref, ya_ref, wouta_ref, nw_ref, win_ref, p8_ref, gcat_ref = ins
        x1_ref, zb_ref, hs_ref, h1_ref, yb_ref, xc_ref, a_ref, cc_ref, r_ref, ig_ref, m_ref = outs
        xbe_scr, b_scr, k_scr, carry_scr = scr

        @pl.when(i == 0)
        def _():
            xbe_scr[0:SUBLANES, :] = jnp.zeros((SUBLANES, BW), F32)
            carry_scr[...] = jnp.zeros_like(carry_scr)

        x1 = x_ref[...] + _dot(ya_ref[...], wouta_ref[...])
        x1_ref[...] = x1
        h = (x1 * _rms(x1) * nw_ref[...]).astype(BF16)
        h1_ref[...] = h
        for k in range(NDEV):
            zb_ref[:, k * CB:(k + 1) * CB] = _dot(h, win_ref[k])
        xbe_scr[SUBLANES:SUBLANES + tm, :] = zb_ref[:, :BW]
        for hh in range(BH):
            cs = slice(hh * HD, (hh + 1) * HD)
            xc = _conv(p8_ref, cs, xbe_scr[SUBLANES:SUBLANES + tm, cs], xbe_scr[7:7 + tm, cs],
                       xbe_scr[6:6 + tm, cs], xbe_scr[5:5 + tm, cs])
            r, ig, _, a, mult, rm = _gates(p8_ref, gcat_ref, hh, xc)
            ixc = ig * xc
            xc_ref[:, cs] = xc
            a_ref[:, cs] = a
            r_ref[:, cs] = r.astype(BF16)
            ig_ref[:, cs] = ig.astype(BF16)
            m_ref[:, cs] = mult.astype(BF16)
            b_scr[:, cs] = mult * ixc
            k_scr[:, cs] = ixc * (a * a * rm)
        xbe_scr[0:SUBLANES, :] = xbe_scr[tm:tm + SUBLANES, :]
        carry_scr[...] = _scan_rows(a_ref, b_scr, hs_ref, carry_scr[...], tm, False)
        for hh in range(BH):
            cs = slice(hh * HD, (hh + 1) * HD)
            gt = zb_ref[:, BW + hh * HD:BW + (hh + 1) * HD]
            hsv = hs_ref[:, cs]
            yb_ref[:, cs] = (hsv * (gt * _sigmoid(gt))).astype(BF16)
            cc_ref[:, cs] = (hsv - b_scr[:, cs]) - k_scr[:, cs]

    tile = lambda w: pl.BlockSpec((tm, w), lambda i: (i, 0))
    wide = lambda dt: _sds((s_len, BW), dt)
    return _call(
        main, jobs, name="fwd_b", grid=(nt,), relay_step=relay_step, forward_step=nt // 2,
        ins=[x, ya, wout_a, nw, win8, p8, gcat], in_specs=[tile(D), tile(AW), _VMEM, _VMEM, _VMEM, _VMEM, _VMEM],
        out_shape=[_sds((s_len, D), F32), _sds((s_len, 2 * BW), F32), wide(F32), _sds((s_len, D), BF16), wide(BF16),
                   wide(F32), wide(F32), wide(F32), wide(BF16), wide(BF16), wide(BF16)],
        out_specs=[tile(D), tile(2 * BW), tile(BW), tile(D)] + [tile(BW)] * 7,
        scratch=[pltpu.VMEM((tm + SUBLANES, BW), F32), pltpu.VMEM((tm, BW), F32), pltpu.VMEM((tm, BW), F32),
                 pltpu.VMEM((1, BW), F32)])


def _head(x1, yb, wout, nfw, tgt, *, tm):
    s_len = x1.shape[0]

    def main(i, ins, outs, scr):
        x1_ref, yb_ref, wout_ref, nfw_ref, t_ref = ins
        dx2_ref, dx2b_ref, loss_ref, gnfw_ref = outs

        @pl.when(i == 0)
        def _():
            loss_ref[...] = jnp.zeros_like(loss_ref)
            gnfw_ref[...] = jnp.zeros_like(gnfw_ref)

        x2 = x1_ref[...] + _dot(yb_ref[...], wout_ref[...])
        rf = _rms(x2)
        xn = x2 * rf
        e = xn * nfw_ref[...] - t_ref[...]
        loss_ref[...] += (0.5 / D) * jnp.sum(jnp.sum(e * e, axis=-1, keepdims=True), axis=0, keepdims=True)
        dyf = e * (1.0 / D)
        gnfw_ref[...] += _rowsum(dyf * xn)
        dx2 = _rms_bwd(dyf, x2, rf, nfw_ref[...])
        dx2_ref[...] = dx2
        dx2b_ref[...] = dx2.astype(BF16)

    tile = lambda w: pl.BlockSpec((tm, w), lambda i: (i, 0))
    whole = lambda *s: pl.BlockSpec(s, lambda i: (0,) * len(s))
    (dx2, dx2b, loss, gnfw), _ = _call(
        main, [], name="head", grid=(s_len // tm,),
        ins=[x1, yb, wout, nfw, tgt], in_specs=[tile(D), tile(BW), _VMEM, _VMEM, tile(D)],
        out_shape=[_sds((s_len, D), F32), _sds((s_len, D), BF16), _sds((1, 1), F32), _sds((1, D), F32)],
        out_specs=[tile(D), tile(D), whole(1, 1), whole(1, D)], scratch=[])
    return dx2, dx2b, loss, gnfw


def _bwd_b(dx2, zb, hs, x1, saved, nw, win8, p8, gcat, wout, *, tm):
    s_len = x1.shape[0]
    nt = s_len // tm

    def main(i, ins, outs, scr):
        (dx2_ref, zb_ref, hs_ref, x1_ref, xc_ref, a_ref, cc_ref, r_ref, ig_ref, m_ref,
         nw_ref, win_ref, p8_ref, gcat_ref, wout_ref) = ins
        dx1_ref, dx1b_ref, dzb_ref, gp8_ref, gga_ref, ggx_ref, gnw_ref = outs
        ae_scr, an_scr, dhd_scr, dh_scr, dy_scr, dxce_scr, carry_scr, afirst_scr = scr

        @pl.when(i == 0)
        def _():
            gp8_ref[...] = jnp.zeros_like(gp8_ref)
            gga_ref[...] = jnp.zeros_like(gga_ref)
            ggx_ref[...] = jnp.zeros_like(ggx_ref)
            gnw_ref[...] = jnp.zeros_like(gnw_ref)
            dxce_scr[tm:tm + SUBLANES, :] = jnp.zeros((SUBLANES, BW), F32)
            carry_scr[...] = jnp.zeros_like(carry_scr)
            afirst_scr[...] = jnp.zeros_like(afirst_scr)

        dx2 = dx2_ref[...]
        dy_scr[...] = _dot_nt(dx2.astype(BF16), wout_ref[...])
        for hh in range(BH):
            cs = slice(hh * HD, (hh + 1) * HD)
            gs = slice(BW + hh * HD, BW + (hh + 1) * HD)
            gt = zb_ref[:, gs]
            sig = _sigmoid(gt)
            dy = dy_scr[:, cs]
            dhd_scr[:, cs] = dy * (gt * sig)
            dzb_ref[:, gs] = (dy * hs_ref[:, cs] * (sig * (1.0 + gt * (1.0 - sig)))).astype(BF16)

        ae_scr[0:tm, :] = a_ref[...]
        ae_scr[tm:tm + SUBLANES, :] = jnp.broadcast_to(afirst_scr[...], (SUBLANES, BW))
        an_scr[...] = ae_scr[1:1 + tm, :]
        afirst_scr[...] = ae_scr[0:1, :]
        carry_scr[...] = _scan_rows(an_scr, dhd_scr, dh_scr, carry_scr[...], tm, True)

        for hh in range(BH):
            cs = slice(hh * HD, (hh + 1) * HD)
            dh = dh_scr[:, cs]
            mult = m_ref[:, cs].astype(F32)
            ig = ig_ref[:, cs].astype(F32)
            r = r_ref[:, cs].astype(F32)
            xc = xc_ref[:, cs]
            lam = p8_ref[7:8, cs]
            sp = _softplus_neg(lam)
            dla = dh * cc_ref[:, cs]
            gp8_ref[7:8, cs] += _rowsum(dla * ((-RG_C) * r)) * (-_sigmoid(-lam))
            dpr = dla * ((-RG_C) * sp) * (r * (1.0 - r))
            dpi = dh * mult * xc * (ig * (1.0 - ig))
            gp8_ref[5:6, cs] += _rowsum(dpr)
            gp8_ref[6:7, cs] += _rowsum(dpi)
            dcat = jnp.concatenate([dpr, dpi], axis=1).astype(BF16)
            dxc = dh * mult * ig + _dot_nt(dcat, gcat_ref[hh])
            gg = _dot(xc.T.astype(BF16), dcat)
            gga_ref[hh] += gg[:, :HD]
            ggx_ref[hh] += gg[:, HD:]
            dxce_scr[0:tm, cs] = dxc
            gp8_ref[4:5, cs] += _rowsum(dxc)
        for hh in range(BH):
            cs = slice(hh * HD, (hh + 1) * HD)
            xb = zb_ref[:, cs]
            d0, d1 = dxce_scr[0:tm, cs], dxce_scr[1:1 + tm, cs]
            d2, d3 = dxce_scr[2:2 + tm, cs], dxce_scr[3:3 + tm, cs]
            dzb_ref[:, cs] = (p8_ref[3:4, cs] * d0 + p8_ref[2:3, cs] * d1 + p8_ref[1:2, cs] * d2
                              + p8_ref[0:1, cs] * d3).astype(BF16)
            gp8_ref[3:4, cs] += _rowsum(d0 * xb)
            gp8_ref[2:3, cs] += _rowsum(d1 * xb)
            gp8_ref[1:2, cs] += _rowsum(d2 * xb)
            gp8_ref[0:1, cs] += _rowsum(d3 * xb)
        dxce_scr[tm:tm + SUBLANES, :] = dxce_scr[0:SUBLANES, :]

        dh1 = jnp.zeros((tm, D), F32)
        for k in range(NDEV):
            dh1 = dh1 + _dot_nt(dzb_ref[:, k * CB:(k + 1) * CB], win_ref[k])
        x1 = x1_ref[...]
        r1 = _rms(x1)
        dx1 = dx2 + _rms_bwd(dh1, x1, r1, nw_ref[...])
        dx1_ref[...] = dx1
        dx1b_ref[...] = dx1.astype(BF16)
        gnw_ref[...] += _rowsum(dh1 * x1 * r1)

    tile = lambda w: pl.BlockSpec((tm, w), lambda i: (nt - 1 - i, 0))
    whole = lambda *s: pl.BlockSpec(s, lambda i: (0,) * len(s))
    full = lambda: pltpu.VMEM((tm, BW), F32)
    ext = lambda: pltpu.VMEM((tm + SUBLANES, BW), F32)
    out, _ = _call(
        main, [], name="bwd_b", grid=(nt,),
        ins=[dx2, zb, hs, x1, *saved, nw, win8, p8, gcat, wout],
        in_specs=[tile(D), tile(2 * BW), tile(BW), tile(D)] + [tile(BW)] * 6 + [_VMEM] * 5,
        out_shape=[_sds((s_len, D), F32), _sds((s_len, D), BF16), _sds((s_len, 2 * BW), BF16), _sds((SUBLANES, BW), F32),
                   _sds((BH, HD, HD), F32), _sds((BH, HD, HD), F32), _sds((1, D), F32)],
        out_specs=[tile(D), tile(D), tile(2 * BW), whole(SUBLANES, BW), whole(BH, HD, HD), whole(BH, HD, HD),
                   whole(1, D)],
        scratch=[ext(), full(), full(), full(), full(), ext(), pltpu.VMEM((1, BW), F32), pltpu.VMEM((1, BW), F32)])
    return out


def _transpose_into(dst_ref, src_ref, rows):
    s_len = src_ref.shape[0]
    for r0 in range(0, s_len, rows):
        dst_ref[:, r0:r0 + rows] = src_ref[r0:r0 + rows, :].astype(F32).T.astype(BF16)


def _wgrad(a, b, jobs, *, by_rows, per, name, relay_step=0):
    s_len, m = a.shape
    n = b.shape[1]
    r, cd = (m // NDEV, n) if by_rows else (m, n // NDEV)
    nsteps = NDEV // per
    at_rows = per * r if by_rows else m

    def main(i, ins, outs, scr):
        a_ref, b_ref = ins
        q_ref, acc_ref = outs
        at_scr, stage, mine, land, send_sems, recv_sems = scr
        x, y, c = _place()

        def to_sibling(pi):
            return pltpu.make_async_remote_copy(
                src_ref=stage.at[pi & 1], dst_ref=land.at[pi], send_sem=send_sems.at[pi], recv_sem=recv_sems.at[pi],
                device_id=(x, y, 1 - c), device_id_type=MESH)

        if by_rows:
            _transpose_into(at_scr, a_ref, TRANSPOSE_ROWS)
        else:
            @pl.when(i == 0)
            def _():
                _transpose_into(at_scr, a_ref, TRANSPOSE_ROWS)

        res = _dot(at_scr[...], b_ref[...]).astype(BF16)
        for k in range(per):
            blk = per * i + k
            pi, pc = blk >> 1, blk & 1
            val = res[k * r:(k + 1) * r, :] if by_rows else res

            @pl.when(pc != c)
            def _():
                @pl.when(pi >= 2)
                def _():
                    to_sibling(pi - 2).wait_send()

                stage[pi & 1] = val
                to_sibling(pi).start()

            @pl.when(pc == c)
            def _():
                mine[pi] = val

        @pl.when(i == nsteps - 1)
        def _():
            for p in range(4):
                to_sibling(p).wait_recv()
            to_sibling(2).wait_send()
            to_sibling(3).wait_send()
            _chip_sums(mine, land, q_ref, acc_ref, x, y)

    if by_rows:
        in_specs = [pl.BlockSpec((s_len, at_rows), lambda j: (0, j)), _VMEM]
    else:
        in_specs = [_VMEM, pl.BlockSpec((s_len, cd), lambda j: (0, j))]
    blk_vmem = lambda k: pltpu.VMEM((k, r, cd), BF16)
    (q, acc), job_out = _call(
        main, jobs, name=name, grid=(nsteps,), relay_step=relay_step, ins=[a, b], in_specs=in_specs,
        out_shape=[_sds((NCHIP_OTHER, r, cd), BF16), _sds((r, cd), F32)],
        out_specs=[pl.BlockSpec((NCHIP_OTHER, r, cd), lambda j: (0, 0, 0)), pl.BlockSpec((r, cd), lambda j: (0, 0))],
        scratch=[pltpu.VMEM((at_rows, s_len), BF16), blk_vmem(2), blk_vmem(4), blk_vmem(4),
                 pltpu.SemaphoreType.DMA((4,)), pltpu.SemaphoreType.DMA((4,))])
    return q, acc, job_out


def _wgrad_cols_early(a, b, jobs, *, name, relay_step=0):
    s_len, m = a.shape
    r, cd = m, b.shape[1] // NDEV
    h = r // 2

    def chip_at(pos, base):
        return base ^ (3 - pos)

    def main(i, ins, outs, scr):
        a_ref, b_ref = ins
        q_ref, acc_ref, rel_ref = outs
        at_scr, stage, mine, land, q2_scr, send_sems, recv_sems, via_send, via_recv = scr
        x, y, c = _place()
        base = 2 * x + y
        xn, yn, _ = _other_chips(x, y)
        pos, pc = i >> 1, i & 1
        pi = chip_at(pos, base)

        def to_sibling(chip, slot):
            return pltpu.make_async_remote_copy(
                src_ref=stage.at[slot], dst_ref=land.at[chip], send_sem=send_sems.at[chip],
                recv_sem=recv_sems.at[chip], device_id=(x, y, 1 - c), device_id_type=MESH)

        def via(k):
            return pltpu.make_async_remote_copy(
                src_ref=q2_scr.at[pl.ds(k * h, h)], dst_ref=rel_ref.at[k], send_sem=via_send.at[k],
                recv_sem=via_recv.at[k], device_id=(*(xn, yn)[k], c), device_id_type=MESH)

        @pl.when(i == 0)
        def _():
            _transpose_into(at_scr, a_ref, TRANSPOSE_ROWS)

        res = _dot(at_scr[...], b_ref[...]).astype(BF16)

        @pl.when(pc != c)
        def _():
            @pl.when(pos >= 2)
            def _():
                to_sibling(chip_at(pos - 2, base), pos & 1).wait_send()

            stage[pos & 1] = res
            to_sibling(pi, pos & 1).start()

        @pl.when(pc == c)
        def _():
            mine[pi] = res

        @pl.when(i == 1)
        def _():
            dg = chip_at(0, base)
            to_sibling(dg, 0).wait_recv()
            q2 = (mine[dg].astype(F32) + land[dg].astype(F32)).astype(BF16)
            q2_scr[...] = q2
            q_ref[2] = q2
            via(0).start()
            via(1).start()

        @pl.when(i == NDEV - 1)
        def _():
            for pos_ in (1, 2, 3):
                to_sibling(chip_at(pos_, base), 0).wait_recv()
            to_sibling(chip_at(2, base), 0).wait_send()
            to_sibling(chip_at(3, base), 1).wait_send()
            for k in range(2):
                via(k).wait_recv()
            for k in range(2):
                via(k).wait_send()
            for j, chip in enumerate((base ^ 2, base ^ 1)):
                q_ref[j] = (mine[chip].astype(F32) + land[chip].astype(F32)).astype(BF16)
            acc_ref[...] = mine[base].astype(F32) + land[base].astype(F32)

    def b_block(j):
        base = 2 * lax.axis_index("x") + lax.axis_index("y")
        return (0, 2 * chip_at(j >> 1, base) + (j & 1))

    blk_vmem = lambda k: pltpu.VMEM((k, r, cd), BF16)
    (q, acc, rel), job_out = _call(
        main, jobs, name=name, grid=(NDEV,), relay_step=relay_step, ins=[a, b],
        in_specs=[_VMEM, pl.BlockSpec((s_len, cd), b_block)],
        out_shape=[_sds((NCHIP_OTHER, r, cd), BF16), _sds((r, cd), F32), _sds((2, h, cd), BF16)],
        out_specs=[pl.BlockSpec((NCHIP_OTHER, r, cd), lambda j: (0, 0, 0)), pl.BlockSpec((r, cd), lambda j: (0, 0)), _HBM],
        scratch=[pltpu.VMEM((m, s_len), BF16), blk_vmem(2), blk_vmem(4), blk_vmem(4), pltpu.VMEM((r, cd), BF16),
                 pltpu.SemaphoreType.DMA((4,)), pltpu.SemaphoreType.DMA((4,)), pltpu.SemaphoreType.DMA((2,)),
                 pltpu.SemaphoreType.DMA((2,))])
    return q, acc, rel, job_out


class _ExchangeRest:
    def __init__(self, q, relayed):
        _, r, cd = q.shape
        half = (2, r // 2, cd)
        self.ins, self.in_specs = [q, relayed], [_HBM, _HBM]
        self.out_shape, self.out_specs = [_sds((2, r, cd), q.dtype)], [_HBM]
        self.scratch = [pltpu.VMEM(half, q.dtype), pltpu.VMEM(half, q.dtype), pltpu.VMEM(half, q.dtype),
                        pltpu.SemaphoreType.DMA((4,)), pltpu.SemaphoreType.DMA((4,)), pltpu.SemaphoreType.DMA((4,))]

    def ops(self, ins, outs, scr):
        (q, rel_in), (land,) = ins, outs
        own, rel, comb, send_sems, recv_sems, local_sems = scr
        h = q.shape[1] // 2
        x, y, c = _place()
        xn, yn, _ = _other_chips(x, y)
        h0, h1 = pl.ds(0, h), pl.ds(h, h)

        def remote(k, src, dst, chip):
            return pltpu.make_async_remote_copy(src_ref=src, dst_ref=dst, send_sem=send_sems.at[k],
                                                recv_sem=recv_sems.at[k], device_id=(*chip, c), device_id_type=MESH)

        def sends():
            return [remote(0, q.at[0, h0], land.at[0, h0], xn), remote(1, q.at[1, h1], land.at[1, h1], yn),
                    remote(2, comb.at[0], land.at[1, h0], yn), remote(3, comb.at[1], land.at[0, h1], xn)]

        def loads():
            return [pltpu.make_async_copy(q.at[1, h0], own.at[0], local_sems.at[0]),
                    pltpu.make_async_copy(q.at[0, h1], own.at[1], local_sems.at[1]),
                    pltpu.make_async_copy(rel_in.at[0], rel.at[0], local_sems.at[2]),
                    pltpu.make_async_copy(rel_in.at[1], rel.at[1], local_sems.at[3])]

        def start():
            cps, lds = sends(), loads()
            for ld in lds:
                ld.start()
            cps[0].start()
            cps[1].start()
            for ld in lds:
                ld.wait()
            for k in range(2):
                comb[k] = (own[k].astype(F32) + rel[k].astype(F32)).astype(comb.dtype)
            cps[2].start()
            cps[3].start()

        def finish():
            cps = sends()
            for cp in cps:
                cp.wait_recv()
            for cp in cps:
                cp.wait_send()

        return start, lambda: None, finish


def _adam_math(w, g, m, v):
    m = B1 * m + (1.0 - B1) * g
    v = B2 * v + (1.0 - B2) * (g * g)
    m_hat = m / (1.0 - B1 ** STEP)
    v_hat = v / (1.0 - B2 ** STEP)
    delta = (-LR) * (m_hat / (jnp.sqrt(v_hat) + ADAM_EPS) + WD * w)
    return delta, m, v


def _adam_big(w, acc, land, m, v, name):
    r, cd = w.shape
    rb = ADAM_ROWS if r % ADAM_ROWS == 0 else r // 2
    nland = land.shape[0]

    def body(w_ref, acc_ref, land_ref, m_ref, v_ref, g_ref, d_ref, mo_ref, vo_ref):
        g = acc_ref[...]
        for j in range(nland):
            g = g + land_ref[j].astype(F32)
        g_ref[...] = g
        d_ref[...], mo_ref[...], vo_ref[...] = _adam_math(w_ref[...], g, m_ref[...], v_ref[...])

    blk = pl.BlockSpec((rb, cd), lambda i: (i, 0))
    blk3 = pl.BlockSpec((nland, rb, cd), lambda i: (0, i, 0))
    return pl.pallas_call(
        body, name=name, grid=(r // rb,), in_specs=[blk, blk, blk3, blk, blk], out_specs=[blk] * 4,
        out_shape=[_sds((r, cd), F32)] * 4,
        compiler_params=_params(dimension_semantics=("arbitrary",)),
    )(w, acc, land, m, v)


def _adam_small(groups):
    n = len(groups)

    def body(*refs):
        ins, outs = refs[:4 * n], refs[4 * n:]
        for k in range(n):
            w_ref, g_ref, m_ref, v_ref = ins[4 * k:4 * k + 4]
            d, mo, vo = _adam_math(w_ref[...], g_ref[...], m_ref[...], v_ref[...])
            outs[3 * k][...] = d
            outs[3 * k + 1][...] = mo
            outs[3 * k + 2][...] = vo

    flat = [a for grp in groups for a in grp]
    shapes = [_sds(grp[0].shape, F32) for grp in groups for _ in range(3)]
    res = pl.pallas_call(
        body, name="adam_small", in_specs=[_VMEM] * (4 * n), out_specs=[_VMEM] * (3 * n), out_shape=shapes,
        compiler_params=_params(),
    )(*flat)
    return [tuple(res[3 * k:3 * k + 3]) for k in range(n)]


TM_FWD_A = 256
RELAY_STEP_FWD_A = 2
FORWARD_STEP_FWD_A = 6
RELAY_STEP_FWD_B = 2
TM_BWD_A = 256
RELAY_STEP_BWD_A = 3
TM_BWD_A_IN = 256
RELAY_STEP_BWD_A_IN = 4
RELAY_STEP_WGRAD_A_IN = 2
TM_FWD_B = 256
TM_HEAD = 512
TM_BWD_B = 256


def _pack(parts, rows):
    flat = jnp.concatenate([p.reshape(-1) for p in parts])
    return jnp.pad(flat, (0, NDEV * rows * LANES - flat.shape[0])).reshape(NDEV, rows, LANES)


def _unpack(packed, shapes):
    flat, out, off = packed.reshape(-1), [], 0
    for s in shapes:
        size = 1
        for d in s:
            size *= d
        out.append(flat[off:off + size].reshape(s))
        off += size
    return out


def kernel(x, norm_w, a_w_in, a_ln_w, a_ln_b, a_w_s, a_b_s, a_w_out, b_w_in, b_conv_w, b_conv_b, b_gate_a_w, b_gate_a_b, b_gate_x_w, b_gate_x_b, b_lambda, b_w_out, norm_f_w, loss_target, m_norm_w, m_a_w_in, m_a_ln_w, m_a_ln_b, m_a_w_s, m_a_b_s, m_a_w_out, m_b_w_in, m_b_conv_w, m_b_conv_b, m_b_gate_a_w, m_b_gate_a_b, m_b_gate_x_w, m_b_gate_x_b, m_b_lambda, m_b_w_out, m_norm_f_w, v_norm_w, v_a_w_in, v_a_ln_w, v_a_ln_b, v_a_w_s, v_a_b_s, v_a_w_out, v_b_w_in, v_b_conv_w, v_b_conv_b, v_b_gate_a_w, v_b_gate_a_b, v_b_gate_x_w, v_b_gate_x_b, v_b_lambda, v_b_w_out, v_norm_f_w):
    me = 4 * lax.axis_index("x") + 2 * lax.axis_index("y") + lax.axis_index("c")
    xs, tgt = x[0], loss_target[0]
    nw0, nw1, nfw = norm_w[0:1], norm_w[1:2], norm_f_w.reshape(1, D)
    w_s, bst = a_w_s[0], a_b_s[0].T
    gcat = jnp.concatenate([b_gate_a_w[0], b_gate_x_w[0]], axis=-1).astype(BF16)

    p8_shard = jnp.concatenate([b_conv_w[0], b_conv_b, b_gate_a_b, b_gate_x_b, b_lambda], axis=0)
    (z, h0, ya, pp), ((win_a8, p8_all), (wout_a8, win_b8)) = _fwd_a(
        xs, nw0, a_ln_w, a_ln_b, w_s, bst,
        [_Gather([a_w_in[0], p8_shard], [BF16, F32]), _Gather([a_w_out[0], b_w_in[0]], [BF16, BF16])],
        tm=TM_FWD_A, relay_step=RELAY_STEP_FWD_A)
    p8 = jnp.transpose(p8_all, (1, 0, 2)).reshape(SUBLANES, BW)
    wout_a = wout_a8.reshape(AW, D)
    (x1, zb, hs, h1, yb, *saved_b), ((wout_b8,),) = _fwd_b(
        xs, ya, wout_a, nw1, win_b8, p8, gcat, [_Gather([b_w_out[0]], [BF16])],
        tm=TM_FWD_B, relay_step=RELAY_STEP_FWD_B)
    wout_b = wout_b8.reshape(BW, D)
    dx2, dx2b, loss, g_nfw = _head(x1, yb, wout_b, nfw, tgt, tm=TM_HEAD)

    dx1, dx1b, dzb, g_p8, g_ga, g_gx, g_nw1 = _bwd_b(dx2, zb, hs, x1, saved_b, nw1, win_b8, p8, gcat, wout_b,
                                                     tm=TM_BWD_B)
    q_wout_b, acc_wout_b, _ = _wgrad(yb, dx2b, [], by_rows=True, per=2, name="wgrad_b_out")
    shapes_b = [(1, D), (1, D), (SUBLANES, BW), (1, 1)]
    pack_b = _pack([g_nfw, g_nw1, g_p8, loss], 16)
    small_b = _InChip([g_ga.reshape(NDEV, -1, HD), g_gx.reshape(NDEV, -1, HD), pack_b])
    q_win_b, acc_win_b, (sm_b, (l_wout_b,)) = _wgrad(h1, dzb, [small_b, _Exchange([q_wout_b])], by_rows=False, per=1,
                                                      name="wgrad_b_in")
    qs_b, accs_b = sm_b[:3], sm_b[3:]

    (dz, g_lnw, g_lnb, g_ws, g_bst), (lands_b, (l_win_b,)) = _bwd_a(
        dx1b, z, pp, a_ln_w, a_ln_b, w_s, bst, wout_a, [_Exchange(qs_b), _ExchangeVia(q_win_b)],
        tm=TM_BWD_A, relay_step=RELAY_STEP_BWD_A)
    shapes_a = [(1, AW), (1, AW), (CH, G)]
    pack_a = _pack([g_lnw, g_lnb, g_bst], 8)
    q_wout_a, acc_wout_a, (red_b, sm_a) = _wgrad(
        ya, dx1b, [_SumGather(accs_b, lands_b), _InChip([g_ws, pack_a])], by_rows=True, per=2,
        name="wgrad_a_out", relay_step=1)
    qs_a, accs_a = sm_a[:2], sm_a[2:]
    q_win_a, acc_win_a, rel_a, (lands_a, (l_wout_a,)) = _wgrad_cols_early(
        h0, dz, [_Exchange(qs_a), _ExchangeVia(q_wout_a)], name="wgrad_a_in", relay_step=RELAY_STEP_WGRAD_A_IN)
    (gx, g_nw0), (red_a, (l_win_a,)) = _bwd_a_in(
        dz, dx1, xs, nw0, win_a8, [_SumGather(accs_a, lands_a), _ExchangeRest(q_win_a, rel_a)],
        tm=TM_BWD_A_IN, relay_step=RELAY_STEP_BWD_A_IN)

    r_ga, r_gx, r_pack_b = red_b
    r_nfw, r_nw1, r_p8, loss = _unpack(r_pack_b, shapes_b)
    r_ws, r_pack_a = red_a
    r_lnw, r_lnb, r_bst = _unpack(r_pack_a, shapes_a)
    g_p8 = lax.dynamic_slice_in_dim(r_p8, me * (BW // NDEV), BW // NDEV, axis=1)
    loss = loss[0, 0]

    weights = dict(norm_w=norm_w, a_w_in=a_w_in, a_ln_w=a_ln_w, a_ln_b=a_ln_b, a_w_s=a_w_s, a_b_s=a_b_s, a_w_out=a_w_out,
                   b_w_in=b_w_in, b_conv_w=b_conv_w, b_conv_b=b_conv_b, b_gate_a_w=b_gate_a_w, b_gate_a_b=b_gate_a_b,
                   b_gate_x_w=b_gate_x_w, b_gate_x_b=b_gate_x_b, b_lambda=b_lambda, b_w_out=b_w_out, norm_f_w=norm_f_w)
    mom1 = dict(norm_w=m_norm_w, a_w_in=m_a_w_in, a_ln_w=m_a_ln_w, a_ln_b=m_a_ln_b, a_w_s=m_a_w_s, a_b_s=m_a_b_s,
                a_w_out=m_a_w_out, b_w_in=m_b_w_in, b_conv_w=m_b_conv_w, b_conv_b=m_b_conv_b, b_gate_a_w=m_b_gate_a_w,
                b_gate_a_b=m_b_gate_a_b, b_gate_x_w=m_b_gate_x_w, b_gate_x_b=m_b_gate_x_b, b_lambda=m_b_lambda,
                b_w_out=m_b_w_out, norm_f_w=m_norm_f_w)
    mom2 = dict(norm_w=v_norm_w, a_w_in=v_a_w_in, a_ln_w=v_a_ln_w, a_ln_b=v_a_ln_b, a_w_s=v_a_w_s, a_b_s=v_a_b_s,
                a_w_out=v_a_w_out, b_w_in=v_b_w_in, b_conv_w=v_b_conv_w, b_conv_b=v_b_conv_b, b_gate_a_w=v_b_gate_a_w,
                b_gate_a_b=v_b_gate_a_b, b_gate_x_w=v_b_gate_x_w, b_gate_x_b=v_b_gate_x_b, b_lambda=v_b_lambda,
                b_w_out=v_b_w_out, norm_f_w=v_norm_f_w)
    names = list(weights)

    def as2d(a):
        return a.reshape(-1, a.shape[-1])

    upd, grads = {}, {}
    for k, acc, land in (("a_w_in", acc_win_a, l_win_a), ("a_w_out", acc_wout_a, l_wout_a),
                         ("b_w_in", acc_win_b, l_win_b), ("b_w_out", acc_wout_b, l_wout_b)):
        g, d, mo, vo = _adam_big(as2d(weights[k]), acc, land, as2d(mom1[k]), as2d(mom2[k]), "adam_" + k)
        grads[k] = g[None]
        upd[k] = (d, mo, vo)
    grads.update(
        norm_w=jnp.concatenate([g_nw0, r_nw1], axis=0), a_ln_w=r_lnw, a_ln_b=r_lnb,
        a_w_s=r_ws.reshape(1, G, CH, CH), a_b_s=r_bst.T[None],
        b_conv_w=g_p8[None, 0:4], b_conv_b=g_p8[4:5], b_gate_a_w=r_ga.reshape(1, BH, HD, HD), b_gate_a_b=g_p8[5:6],
        b_gate_x_w=r_gx.reshape(1, BH, HD, HD), b_gate_x_b=g_p8[6:7], b_lambda=g_p8[7:8], norm_f_w=r_nfw.reshape(D))
    small_names = [k for k in names if k not in upd]
    res = _adam_small([(as2d(weights[k]), as2d(grads[k]), as2d(mom1[k]), as2d(mom2[k])) for k in small_names])
    for k, r3 in zip(small_names, res):
        upd[k] = r3
    deltas = [upd[k][0].reshape(weights[k].shape) for k in names]
    new_m = [upd[k][1].reshape(weights[k].shape) for k in names]
    new_v = [upd[k][2].reshape(weights[k].shape) for k in names]
    return (loss, gx[None], *[grads[k] for k in names], *deltas, *new_m, *new_v)
```
